```python
import math
import jax, jax.numpy as jnp
from jax import lax
import numpy as np

D_MODEL = 1024
BATCH = 8
SEQ = 2048
DEPTH = 1

D_A = D_MODEL
SGU_CHUNK = 128
SGU_GROUPS = 8
RWKV_HEAD = 64
D_B = D_MODEL
N_HEADS_B = D_B // RWKV_HEAD

def _lora_dim(factor, power):
    return max(32, int(round(factor * D_MODEL ** power / 32)) * 32)

LORA_W = _lora_dim(1.8, 0.5)
LORA_A = _lora_dim(1.8, 0.5)
LORA_G = _lora_dim(0.6, 0.8)
C_B = 3 * D_B + LORA_W + LORA_A + LORA_G
P_TOTAL = 2 * D_A + C_B + 2 * D_MODEL
D_FF = 4 * D_MODEL
NORM_EPS = 1e-6
LN_EPS = 1e-5
GN_EPS = 64e-5

kernel_name = "hybrid_gmlp_rwkv7_gated_block"


def _rms_norm(x, g):
    xf = x.astype(jnp.float32)
    y = xf * lax.rsqrt(jnp.mean(xf * xf, axis=-1, keepdims=True) + NORM_EPS)
    return (y * g.astype(jnp.float32)).astype(x.dtype)


def _layer_norm(x, g, b):
    xf = x.astype(jnp.float32)
    mu = jnp.mean(xf, axis=-1, keepdims=True)
    var = jnp.mean(jnp.square(xf - mu), axis=-1, keepdims=True)
    y = (xf - mu) * lax.rsqrt(var + LN_EPS)
    return (y * g.astype(jnp.float32) + b.astype(jnp.float32)).astype(x.dtype)


def _token_shift(p):
    return jnp.pad(p, ((0, 0), (1, 0), (0, 0)))[:, :-1, :]


def _sgu_branch(p_sgu, ln_w, ln_b, sgu_w, sgu_b, w_proj_a):
    B, T, _ = p_sgu.shape
    z = jax.nn.gelu(p_sgu, approximate=False)
    u, v = jnp.split(z, 2, axis=-1)
    v = _layer_norm(v, ln_w, ln_b)
    n_chunks = T // SGU_CHUNK
    dg = D_A // SGU_GROUPS
    v = v.reshape(B, n_chunks, SGU_CHUNK, SGU_GROUPS, dg)
    mask = jnp.tril(jnp.ones((SGU_CHUNK, SGU_CHUNK), dtype=sgu_w.dtype))
    ws = sgu_w * mask[None]
    sv = jnp.einsum('gij,bcjgd->bcigd', ws, v)
    sv = sv + jnp.swapaxes(sgu_b, 0, 1)[None, None, :, :, None]
    s = u * sv.reshape(B, T, D_A)
    return s @ w_proj_a


def _rwkv7_scan(r, w, k, v, kk, a):
    B, T, H, N = r.shape
    xs = tuple(jnp.moveaxis(t, 1, 0) for t in (r, w, k, v, kk, a))

    def step(S, inp):
        r_t, w_t, k_t, v_t, kk_t, a_t = inp
        sa = jnp.einsum('bhvk,bhk->bhv', S, -kk_t)
        S = (S * w_t[:, :, None, :]
             + sa[..., None] * (kk_t * a_t)[:, :, None, :]
             + v_t[..., None] * k_t[:, :, None, :])
        o = jnp.einsum('bhvk,bhk->bhv', S, r_t)
        return S, o

    S0 = jnp.zeros((B, H, N, N), dtype=jnp.float32)
    _, o = lax.scan(step, S0, xs)
    return jnp.moveaxis(o, 0, 1)


def _rwkv7_branch(p_rwkv, shift_b, w_lora_w, w0, a_lora_w, a0, g_lora_w,
                  k_k, k_a, r_k, ln_x_w, ln_x_b, w_proj_b):
    B, T, _ = p_rwkv.shape
    f32 = jnp.float32
    p = p_rwkv.astype(f32)
    sb = shift_b.astype(f32)
    q = p * sb[0] + _token_shift(p) * sb[1]
    cuts = np.cumsum([D_B, D_B, D_B, LORA_W, LORA_A]).tolist()
    r, k, v, xw, xa, xg = jnp.split(q, cuts, axis=-1)
    w = -jax.nn.softplus(-(w0.astype(f32) + jnp.tanh(xw) @ w_lora_w.astype(f32))) - 0.5
    decay = jnp.exp(-jnp.exp(w))
    aa = jax.nn.sigmoid(a0.astype(f32) + xa @ a_lora_w.astype(f32))
    g = jax.nn.sigmoid(xg) @ g_lora_w.astype(f32)
    kk = (k * k_k.astype(f32)).reshape(B, T, N_HEADS_B, RWKV_HEAD)
    kk = kk / jnp.maximum(jnp.linalg.norm(kk, axis=-1, keepdims=True), 1e-12)
    k = k * (1.0 + (aa - 1.0) * k_a.astype(f32))
    hs = lambda t: t.reshape(B, T, N_HEADS_B, RWKV_HEAD)
    rh, kh, vh = hs(r), hs(k), hs(v)
    o = _rwkv7_scan(rh, hs(decay), kh, vh, kk, hs(aa))
    mu = jnp.mean(o, axis=-1, keepdims=True)
    var = jnp.mean(jnp.square(o - mu), axis=-1, keepdims=True)
    o = ((o - mu) * lax.rsqrt(var + GN_EPS)).reshape(B, T, D_B)
    o = o * ln_x_w.astype(f32) + ln_x_b.astype(f32)
    r_k_h = r_k.astype(f32).reshape(N_HEADS_B, RWKV_HEAD)
    bonus = jnp.sum(rh * kh * r_k_h, axis=-1, keepdims=True) * vh
    o = (o + bonus.reshape(B, T, D_B)) * g
    return (o @ w_proj_b.astype(f32)).astype(p_rwkv.dtype)


def _fwd_setup_inputs(seed: int = 0) -> dict:
    key = jax.random.key(seed)
    ks = jax.random.split(key, 32)
    L = DEPTH
    f32 = jnp.float32

    def nrm(k, shape, scale):
        return jax.random.normal(k, shape, f32) * scale

    mu = jax.random.uniform(ks[8], (L, C_B), f32)
    return {
        "x": nrm(ks[0], (BATCH, SEQ, D_MODEL), 1.0),
        "g_mix": 1.0 + nrm(ks[1], (L, D_MODEL), 0.02),
        "w_in": nrm(ks[2], (L, D_MODEL, P_TOTAL), D_MODEL ** -0.5),
        "sgu_ln_w": 1.0 + nrm(ks[3], (L, D_A), 0.02),
        "sgu_ln_b": nrm(ks[4], (L, D_A), 0.02),
        "sgu_w": nrm(ks[5], (L, SGU_GROUPS, SGU_CHUNK, SGU_CHUNK), SGU_CHUNK ** -0.5),
        "sgu_b": 1.0 + nrm(ks[6], (L, SGU_GROUPS, SGU_CHUNK), 0.02),
        "w_proj_a": nrm(ks[7], (L, D_A, D_MODEL), D_A ** -0.5),
        "shift_b": jnp.stack([1.0 - mu, mu], axis=1),
        "w_lora_w": nrm(ks[9], (L, LORA_W, D_B), 0.1 * LORA_W ** -0.5),
        "w0": jax.random.uniform(ks[10], (L, D_B), f32, minval=-4.0, maxval=1.0),
        "a_lora_w": nrm(ks[11], (L, LORA_A, D_B), LORA_A ** -0.5),
        "a0": nrm(ks[12], (L, D_B), 0.1),
        "g_lora_w": nrm(ks[13], (L, LORA_G, D_B), LORA_G ** -0.5),
        "k_k": 0.85 + nrm(ks[14], (L, D_B), 0.02),
        "k_a": 1.0 + nrm(ks[15], (L, D_B), 0.02),
        "r_k": nrm(ks[16], (L, D_B), 0.1),
        "ln_x_w": 1.0 + nrm(ks[17], (L, D_B), 0.02),
        "ln_x_b": nrm(ks[18], (L, D_B), 0.02),
        "w_proj_b": nrm(ks[19], (L, D_B, D_MODEL), D_B ** -0.5),
        "w_out": nrm(ks[20], (L, D_MODEL, D_MODEL), D_MODEL ** -0.5),
        "g_ffn": 1.0 + nrm(ks[21], (L, D_MODEL), 0.02),
        "w_ffn1": nrm(ks[22], (L, D_MODEL, D_FF), D_MODEL ** -0.5),
        "w_ffn2": nrm(ks[23], (L, D_FF, D_MODEL), D_FF ** -0.5),
        "g_final": 1.0 + nrm(ks[24], (D_MODEL,), 0.02),
    }


def _fwd_reference(x, g_mix, w_in, sgu_ln_w, sgu_ln_b, sgu_w, sgu_b, w_proj_a, shift_b,
              w_lora_w, w0, a_lora_w, a0, g_lora_w, k_k, k_a, r_k, ln_x_w, ln_x_b,
              w_proj_b, w_out, g_ffn, w_ffn1, w_ffn2, g_final):
    h = x
    for l in range(DEPTH):
        a = _rms_norm(h, g_mix[l])
        p = a @ w_in[l]
        p_sgu, p_rwkv, p_gate = jnp.split(p, [2 * D_A, 2 * D_A + C_B], axis=-1)
        y_a = _sgu_branch(p_sgu, sgu_ln_w[l], sgu_ln_b[l], sgu_w[l], sgu_b[l], w_proj_a[l])
        y_b = _rwkv7_branch(p_rwkv, shift_b[l], w_lora_w[l], w0[l], a_lora_w[l], a0[l],
                            g_lora_w[l], k_k[l], k_a[l], r_k[l], ln_x_w[l], ln_x_b[l],
                            w_proj_b[l])
        gate_a, gate_b = jnp.split(p_gate, 2, axis=-1)
        mixed = jax.nn.sigmoid(gate_a) * y_a + jax.nn.sigmoid(gate_b) * y_b
        h = h + mixed @ w_out[l]
        f = _rms_norm(h, g_ffn[l])
        h = h + jnp.square(jax.nn.relu(f @ w_ffn1[l])) @ w_ffn2[l]
    return _rms_norm(h, g_final)


import jax as _jax
import jax.numpy as _jnp

TWIN_FORMAT = 'train_step'
FWD_PARAMS = ['x', 'g_mix', 'w_in', 'sgu_ln_w', 'sgu_ln_b', 'sgu_w', 'sgu_b', 'w_proj_a', 'shift_b', 'w_lora_w', 'w0', 'a_lora_w', 'a0', 'g_lora_w', 'k_k', 'k_a', 'r_k', 'ln_x_w', 'ln_x_b', 'w_proj_b', 'w_out', 'g_ffn', 'w_ffn1', 'w_ffn2', 'g_final']
TWIN_WEIGHTS = ['g_mix', 'w_in', 'sgu_ln_w', 'sgu_ln_b', 'sgu_w', 'sgu_b', 'w_proj_a', 'shift_b', 'w_lora_w', 'w0', 'a_lora_w', 'a0', 'g_lora_w', 'k_k', 'k_a', 'r_k', 'ln_x_w', 'ln_x_b', 'w_proj_b', 'w_out', 'g_ffn', 'w_ffn1', 'w_ffn2', 'g_final']
TWIN_DIFF_INPUT = 'x'
TWIN_INPUTS = ['x', 'g_mix', 'w_in', 'sgu_ln_w', 'sgu_ln_b', 'sgu_w', 'sgu_b', 'w_proj_a', 'shift_b', 'w_lora_w', 'w0', 'a_lora_w', 'a0', 'g_lora_w', 'k_k', 'k_a', 'r_k', 'ln_x_w', 'ln_x_b', 'w_proj_b', 'w_out', 'g_ffn', 'w_ffn1', 'w_ffn2', 'g_final', 'loss_target', 'm_g_mix', 'm_w_in', 'm_sgu_ln_w', 'm_sgu_ln_b', 'm_sgu_w', 'm_sgu_b', 'm_w_proj_a', 'm_shift_b', 'm_w_lora_w', 'm_w0', 'm_a_lora_w', 'm_a0', 'm_g_lora_w', 'm_k_k', 'm_k_a', 'm_r_k', 'm_ln_x_w', 'm_ln_x_b', 'm_w_proj_b', 'm_w_out', 'm_g_ffn', 'm_w_ffn1', 'm_w_ffn2', 'm_g_final', 'v_g_mix', 'v_w_in', 'v_sgu_ln_w', 'v_sgu_ln_b', 'v_sgu_w', 'v_sgu_b', 'v_w_proj_a', 'v_shift_b', 'v_w_lora_w', 'v_w0', 'v_a_lora_w', 'v_a0', 'v_g_lora_w', 'v_k_k', 'v_k_a', 'v_r_k', 'v_ln_x_w', 'v_ln_x_b', 'v_w_proj_b', 'v_w_out', 'v_g_ffn', 'v_w_ffn1', 'v_w_ffn2', 'v_g_final']
TWIN_OUTPUTS = ['loss', 'grad_x', 'grad_g_mix', 'grad_w_in', 'grad_sgu_ln_w', 'grad_sgu_ln_b', 'grad_sgu_w', 'grad_sgu_b', 'grad_w_proj_a', 'grad_shift_b', 'grad_w_lora_w', 'grad_w0', 'grad_a_lora_w', 'grad_a0', 'grad_g_lora_w', 'grad_k_k', 'grad_k_a', 'grad_r_k', 'grad_ln_x_w', 'grad_ln_x_b', 'grad_w_proj_b', 'grad_w_out', 'grad_g_ffn', 'grad_w_ffn1', 'grad_w_ffn2', 'grad_g_final', 'delta_g_mix', 'delta_w_in', 'delta_sgu_ln_w', 'delta_sgu_ln_b', 'delta_sgu_w', 'delta_sgu_b', 'delta_w_proj_a', 'delta_shift_b', 'delta_w_lora_w', 'delta_w0', 'delta_a_lora_w', 'delta_a0', 'delta_g_lora_w', 'delta_k_k', 'delta_k_a', 'delta_r_k', 'delta_ln_x_w', 'delta_ln_x_b', 'delta_w_proj_b', 'delta_w_out', 'delta_g_ffn', 'delta_w_ffn1', 'delta_w_ffn2', 'delta_g_final', 'new_m_g_mix', 'new_m_w_in', 'new_m_sgu_ln_w', 'new_m_sgu_ln_b', 'new_m_sgu_w', 'new_m_sgu_b', 'new_m_w_proj_a', 'new_m_shift_b', 'new_m_w_lora_w', 'new_m_w0', 'new_m_a_lora_w', 'new_m_a0', 'new_m_g_lora_w', 'new_m_k_k', 'new_m_k_a', 'new_m_r_k', 'new_m_ln_x_w', 'new_m_ln_x_b', 'new_m_w_proj_b', 'new_m_w_out', 'new_m_g_ffn', 'new_m_w_ffn1', 'new_m_w_ffn2', 'new_m_g_final', 'new_v_g_mix', 'new_v_w_in', 'new_v_sgu_ln_w', 'new_v_sgu_ln_b', 'new_v_sgu_w', 'new_v_sgu_b', 'new_v_w_proj_a', 'new_v_shift_b', 'new_v_w_lora_w', 'new_v_w0', 'new_v_a_lora_w', 'new_v_a0', 'new_v_g_lora_w', 'new_v_k_k', 'new_v_k_a', 'new_v_r_k', 'new_v_ln_x_w', 'new_v_ln_x_b', 'new_v_w_proj_b', 'new_v_w_out', 'new_v_g_ffn', 'new_v_w_ffn1', 'new_v_w_ffn2', 'new_v_g_final']
TWIN_LEAF_KINDS = {'loss': 'loss', 'grad_x': 'grad_x', 'grad_g_mix': 'grad_w', 'grad_w_in': 'grad_w', 'grad_sgu_ln_w': 'grad_w', 'grad_sgu_ln_b': 'grad_w', 'grad_sgu_w': 'grad_w', 'grad_sgu_b': 'grad_w', 'grad_w_proj_a': 'grad_w', 'grad_shift_b': 'grad_w', 'grad_w_lora_w': 'grad_w', 'grad_w0': 'grad_w', 'grad_a_lora_w': 'grad_w', 'grad_a0': 'grad_w', 'grad_g_lora_w': 'grad_w', 'grad_k_k': 'grad_w', 'grad_k_a': 'grad_w', 'grad_r_k': 'grad_w', 'grad_ln_x_w': 'grad_w', 'grad_ln_x_b': 'grad_w', 'grad_w_proj_b': 'grad_w', 'grad_w_out': 'grad_w', 'grad_g_ffn': 'grad_w', 'grad_w_ffn1': 'grad_w', 'grad_w_ffn2': 'grad_w', 'grad_g_final': 'grad_w', 'delta_g_mix': 'delta_w', 'delta_w_in': 'delta_w', 'delta_sgu_ln_w': 'delta_w', 'delta_sgu_ln_b': 'delta_w', 'delta_sgu_w': 'delta_w', 'delta_sgu_b': 'delta_w', 'delta_w_proj_a': 'delta_w', 'delta_shift_b': 'delta_w', 'delta_w_lora_w': 'delta_w', 'delta_w0': 'delta_w', 'delta_a_lora_w': 'delta_w', 'delta_a0': 'delta_w', 'delta_g_lora_w': 'delta_w', 'delta_k_k': 'delta_w', 'delta_k_a': 'delta_w', 'delta_r_k': 'delta_w', 'delta_ln_x_w': 'delta_w', 'delta_ln_x_b': 'delta_w', 'delta_w_proj_b': 'delta_w', 'delta_w_out': 'delta_w', 'delta_g_ffn': 'delta_w', 'delta_w_ffn1': 'delta_w', 'delta_w_ffn2': 'delta_w', 'delta_g_final': 'delta_w', 'new_m_g_mix': 'new_m', 'new_m_w_in': 'new_m', 'new_m_sgu_ln_w': 'new_m', 'new_m_sgu_ln_b': 'new_m', 'new_m_sgu_w': 'new_m', 'new_m_sgu_b': 'new_m', 'new_m_w_proj_a': 'new_m', 'new_m_shift_b': 'new_m', 'new_m_w_lora_w': 'new_m', 'new_m_w0': 'new_m', 'new_m_a_lora_w': 'new_m', 'new_m_a0': 'new_m', 'new_m_g_lora_w': 'new_m', 'new_m_k_k': 'new_m', 'new_m_k_a': 'new_m', 'new_m_r_k': 'new_m', 'new_m_ln_x_w': 'new_m', 'new_m_ln_x_b': 'new_m', 'new_m_w_proj_b': 'new_m', 'new_m_w_out': 'new_m', 'new_m_g_ffn': 'new_m', 'new_m_w_ffn1': 'new_m', 'new_m_w_ffn2': 'new_m', 'new_m_g_final': 'new_m', 'new_v_g_mix': 'new_v', 'new_v_w_in': 'new_v', 'new_v_sgu_ln_w': 'new_v', 'new_v_sgu_ln_b': 'new_v', 'new_v_sgu_w': 'new_v', 'new_v_sgu_b': 'new_v', 'new_v_w_proj_a': 'new_v', 'new_v_shift_b': 'new_v', 'new_v_w_lora_w': 'new_v', 'new_v_w0': 'new_v', 'new_v_a_lora_w': 'new_v', 'new_v_a0': 'new_v', 'new_v_g_lora_w': 'new_v', 'new_v_k_k': 'new_v', 'new_v_k_a': 'new_v', 'new_v_r_k': 'new_v', 'new_v_ln_x_w': 'new_v', 'new_v_ln_x_b': 'new_v', 'new_v_w_proj_b': 'new_v', 'new_v_w_out': 'new_v', 'new_v_g_ffn': 'new_v', 'new_v_w_ffn1': 'new_v', 'new_v_w_ffn2': 'new_v', 'new_v_g_final': 'new_v'}


def _forward(args):
    return _fwd_reference(*[args[k] for k in FWD_PARAMS])


def _output_shape():
    out = _jax.eval_shape(lambda: _forward(_fwd_setup_inputs(0)))
    return out.shape, out.dtype

N_MICROBATCH = 1
ADAM_LR = 0.001
ADAM_B1 = 0.9
ADAM_B2 = 0.999
ADAM_EPS = 1e-08
ADAM_WD = 0.01
ADAM_STEP = 10
PER_EXAMPLE_BATCH_AXIS = {'x': 0, 'loss_target': 0}
SHARED_INPUTS = []
_WEIGHT_DTYPES = {'g_mix': _jnp.float32, 'w_in': _jnp.float32, 'sgu_ln_w': _jnp.float32, 'sgu_ln_b': _jnp.float32, 'sgu_w': _jnp.float32, 'sgu_b': _jnp.float32, 'w_proj_a': _jnp.float32, 'shift_b': _jnp.float32, 'w_lora_w': _jnp.float32, 'w0': _jnp.float32, 'a_lora_w': _jnp.float32, 'a0': _jnp.float32, 'g_lora_w': _jnp.float32, 'k_k': _jnp.float32, 'k_a': _jnp.float32, 'r_k': _jnp.float32, 'ln_x_w': _jnp.float32, 'ln_x_b': _jnp.float32, 'w_proj_b': _jnp.float32, 'w_out': _jnp.float32, 'g_ffn': _jnp.float32, 'w_ffn1': _jnp.float32, 'w_ffn2': _jnp.float32, 'g_final': _jnp.float32}
MOMENT_SCALE = {'g_mix': 1.019946e-01, 'w_in': 3.587431e-02, 'sgu_ln_w': 2.960054e-02, 'sgu_ln_b': 2.888050e-02, 'sgu_w': 2.994704e-02, 'sgu_b': 4.269121e-02, 'w_proj_a': 5.378790e-02, 'shift_b': 4.397453e-02, 'w_lora_w': 2.912746e-03, 'w0': 1.838984e-02, 'a_lora_w': 1.227116e-02, 'a0': 1.461495e-02, 'g_lora_w': 3.583699e-02, 'k_k': 1.703932e-02, 'k_a': 3.805849e-02, 'r_k': 7.826472e-02, 'ln_x_w': 3.771556e-02, 'ln_x_b': 3.783613e-02, 'w_proj_b': 3.570660e-02, 'w_out': 6.462827e-02, 'g_ffn': 1.083084e-01, 'w_ffn1': 5.344452e-02, 'w_ffn2': 9.966124e-02, 'g_final': 1.614361e+01}


def _to_microbatches(a, axis):
    t = _jnp.moveaxis(a, axis, 0)
    t = t.reshape((N_MICROBATCH, t.shape[0] // N_MICROBATCH) + t.shape[1:])
    return _jnp.moveaxis(t, 1, axis + 1)


def setup_inputs(seed: int = 0) -> dict:
    inp = _fwd_setup_inputs(seed)
    key = _jax.random.fold_in(_jax.random.key(seed), 7919)
    shape, _ = _output_shape()
    out = dict(inp)
    out["loss_target"] = _jax.random.normal(_jax.random.fold_in(key, 0), shape, _jnp.float32)
    for i, name in enumerate(TWIN_WEIGHTS):
        w = inp[name].astype(_jnp.float32)
        if MOMENT_SCALE is None:
            s = _jnp.sqrt(_jnp.mean(_jnp.square(w)) + 1e-30)
        else:
            s = MOMENT_SCALE[name]
        km, kv = _jax.random.split(_jax.random.fold_in(key, i + 1))
        out[name] = w
        out["m_" + name] = s * _jax.random.normal(km, w.shape, _jnp.float32)
        out["v_" + name] = (s * s) * _jax.random.uniform(kv, w.shape, _jnp.float32, 0.5, 1.5)
    if N_MICROBATCH > 1:
        for name, axis in PER_EXAMPLE_BATCH_AXIS.items():
            out[name] = _to_microbatches(out[name], axis)
    return {'x': out['x'], 'g_mix': out['g_mix'], 'w_in': out['w_in'], 'sgu_ln_w': out['sgu_ln_w'], 'sgu_ln_b': out['sgu_ln_b'], 'sgu_w': out['sgu_w'], 'sgu_b': out['sgu_b'], 'w_proj_a': out['w_proj_a'], 'shift_b': out['shift_b'], 'w_lora_w': out['w_lora_w'], 'w0': out['w0'], 'a_lora_w': out['a_lora_w'], 'a0': out['a0'], 'g_lora_w': out['g_lora_w'], 'k_k': out['k_k'], 'k_a': out['k_a'], 'r_k': out['r_k'], 'ln_x_w': out['ln_x_w'], 'ln_x_b': out['ln_x_b'], 'w_proj_b': out['w_proj_b'], 'w_out': out['w_out'], 'g_ffn': out['g_ffn'], 'w_ffn1': out['w_ffn1'], 'w_ffn2': out['w_ffn2'], 'g_final': out['g_final'], 'loss_target': out['loss_target'], 'm_g_mix': out['m_g_mix'], 'm_w_in': out['m_w_in'], 'm_sgu_ln_w': out['m_sgu_ln_w'], 'm_sgu_ln_b': out['m_sgu_ln_b'], 'm_sgu_w': out['m_sgu_w'], 'm_sgu_b': out['m_sgu_b'], 'm_w_proj_a': out['m_w_proj_a'], 'm_shift_b': out['m_shift_b'], 'm_w_lora_w': out['m_w_lora_w'], 'm_w0': out['m_w0'], 'm_a_lora_w': out['m_a_lora_w'], 'm_a0': out['m_a0'], 'm_g_lora_w': out['m_g_lora_w'], 'm_k_k': out['m_k_k'], 'm_k_a': out['m_k_a'], 'm_r_k': out['m_r_k'], 'm_ln_x_w': out['m_ln_x_w'], 'm_ln_x_b': out['m_ln_x_b'], 'm_w_proj_b': out['m_w_proj_b'], 'm_w_out': out['m_w_out'], 'm_g_ffn': out['m_g_ffn'], 'm_w_ffn1': out['m_w_ffn1'], 'm_w_ffn2': out['m_w_ffn2'], 'm_g_final': out['m_g_final'], 'v_g_mix': out['v_g_mix'], 'v_w_in': out['v_w_in'], 'v_sgu_ln_w': out['v_sgu_ln_w'], 'v_sgu_ln_b': out['v_sgu_ln_b'], 'v_sgu_w': out['v_sgu_w'], 'v_sgu_b': out['v_sgu_b'], 'v_w_proj_a': out['v_w_proj_a'], 'v_shift_b': out['v_shift_b'], 'v_w_lora_w': out['v_w_lora_w'], 'v_w0': out['v_w0'], 'v_a_lora_w': out['v_a_lora_w'], 'v_a0': out['v_a0'], 'v_g_lora_w': out['v_g_lora_w'], 'v_k_k': out['v_k_k'], 'v_k_a': out['v_k_a'], 'v_r_k': out['v_r_k'], 'v_ln_x_w': out['v_ln_x_w'], 'v_ln_x_b': out['v_ln_x_b'], 'v_w_proj_b': out['v_w_proj_b'], 'v_w_out': out['v_w_out'], 'v_g_ffn': out['v_g_ffn'], 'v_w_ffn1': out['v_w_ffn1'], 'v_w_ffn2': out['v_w_ffn2'], 'v_g_final': out['v_g_final']}


def _loss(weights, diff, rest, loss_target):
    with _jax.named_scope("forward"):
        args = {**rest, TWIN_DIFF_INPUT: diff, **{k: w.astype(_WEIGHT_DTYPES[k]) for k, w in weights.items()}}
        y = _forward(args)
    with _jax.named_scope("loss_head"):
        err = _jnp.square(y.astype(_jnp.float32) - loss_target)
        return 0.5 * _jnp.sum(_jnp.mean(err, axis=-1)) if err.ndim else 0.5 * err


def _adamw(w, g, m, v):
    m = ADAM_B1 * m + (1.0 - ADAM_B1) * g
    v = ADAM_B2 * v + (1.0 - ADAM_B2) * _jnp.square(g)
    m_hat = m / (1.0 - ADAM_B1 ** ADAM_STEP)
    v_hat = v / (1.0 - ADAM_B2 ** ADAM_STEP)
    delta = -ADAM_LR * (m_hat / (_jnp.sqrt(v_hat) + ADAM_EPS) + ADAM_WD * w)
    return delta, m, v


def reference(x, g_mix, w_in, sgu_ln_w, sgu_ln_b, sgu_w, sgu_b, w_proj_a, shift_b, w_lora_w, w0, a_lora_w, a0, g_lora_w, k_k, k_a, r_k, ln_x_w, ln_x_b, w_proj_b, w_out, g_ffn, w_ffn1, w_ffn2, g_final, loss_target, m_g_mix, m_w_in, m_sgu_ln_w, m_sgu_ln_b, m_sgu_w, m_sgu_b, m_w_proj_a, m_shift_b, m_w_lora_w, m_w0, m_a_lora_w, m_a0, m_g_lora_w, m_k_k, m_k_a, m_r_k, m_ln_x_w, m_ln_x_b, m_w_proj_b, m_w_out, m_g_ffn, m_w_ffn1, m_w_ffn2, m_g_final, v_g_mix, v_w_in, v_sgu_ln_w, v_sgu_ln_b, v_sgu_w, v_sgu_b, v_w_proj_a, v_shift_b, v_w_lora_w, v_w0, v_a_lora_w, v_a0, v_g_lora_w, v_k_k, v_k_a, v_r_k, v_ln_x_w, v_ln_x_b, v_w_proj_b, v_w_out, v_g_ffn, v_w_ffn1, v_w_ffn2, v_g_final):
    given = dict(x=x, g_mix=g_mix, w_in=w_in, sgu_ln_w=sgu_ln_w, sgu_ln_b=sgu_ln_b, sgu_w=sgu_w, sgu_b=sgu_b, w_proj_a=w_proj_a, shift_b=shift_b, w_lora_w=w_lora_w, w0=w0, a_lora_w=a_lora_w, a0=a0, g_lora_w=g_lora_w, k_k=k_k, k_a=k_a, r_k=r_k, ln_x_w=ln_x_w, ln_x_b=ln_x_b, w_proj_b=w_proj_b, w_out=w_out, g_ffn=g_ffn, w_ffn1=w_ffn1, w_ffn2=w_ffn2, g_final=g_final, loss_target=loss_target, m_g_mix=m_g_mix, m_w_in=m_w_in, m_sgu_ln_w=m_sgu_ln_w, m_sgu_ln_b=m_sgu_ln_b, m_sgu_w=m_sgu_w, m_sgu_b=m_sgu_b, m_w_proj_a=m_w_proj_a, m_shift_b=m_shift_b, m_w_lora_w=m_w_lora_w, m_w0=m_w0, m_a_lora_w=m_a_lora_w, m_a0=m_a0, m_g_lora_w=m_g_lora_w, m_k_k=m_k_k, m_k_a=m_k_a, m_r_k=m_r_k, m_ln_x_w=m_ln_x_w, m_ln_x_b=m_ln_x_b, m_w_proj_b=m_w_proj_b, m_w_out=m_w_out, m_g_ffn=m_g_ffn, m_w_ffn1=m_w_ffn1, m_w_ffn2=m_w_ffn2, m_g_final=m_g_final, v_g_mix=v_g_mix, v_w_in=v_w_in, v_sgu_ln_w=v_sgu_ln_w, v_sgu_ln_b=v_sgu_ln_b, v_sgu_w=v_sgu_w, v_sgu_b=v_sgu_b, v_w_proj_a=v_w_proj_a, v_shift_b=v_shift_b, v_w_lora_w=v_w_lora_w, v_w0=v_w0, v_a_lora_w=v_a_lora_w, v_a0=v_a0, v_g_lora_w=v_g_lora_w, v_k_k=v_k_k, v_k_a=v_k_a, v_r_k=v_r_k, v_ln_x_w=v_ln_x_w, v_ln_x_b=v_ln_x_b, v_w_proj_b=v_w_proj_b, v_w_out=v_w_out, v_g_ffn=v_g_ffn, v_w_ffn1=v_w_ffn1, v_w_ffn2=v_w_ffn2, v_g_final=v_g_final)
    weights = {n: given[n] for n in TWIN_WEIGHTS}
    shared = {n: given[n] for n in SHARED_INPUTS}
    per_example = {n: given[n] for n in ['x']}
    grad_fn = _jax.value_and_grad(_loss, argnums=(0, 1))

    def one_microbatch(ex, loss_target):
        ex = dict(ex)
        diff = ex.pop(TWIN_DIFF_INPUT)
        return grad_fn(weights, diff, {**shared, **ex}, loss_target)

    if N_MICROBATCH == 1:
        loss, (grad_w, grad_x) = one_microbatch(per_example, given["loss_target"])
    else:
        def body(carry, xs):
            loss_sum, grad_sum = carry
            l_k, (gw_k, gx_k) = one_microbatch(xs[0], xs[1])
            with _jax.named_scope("update"):
                return (loss_sum + l_k, _jax.tree.map(_jnp.add, grad_sum, gw_k)), gx_k

        init = (_jnp.zeros((), _jnp.float32), _jax.tree.map(_jnp.zeros_like, weights))
        (loss, grad_w), grad_x = _jax.lax.scan(body, init, (per_example, given["loss_target"]))
    with _jax.named_scope("update"):
        delta_w, new_m, new_v = {}, {}, {}
        for n in TWIN_WEIGHTS:
            delta_w[n], new_m[n], new_v[n] = _adamw(weights[n], grad_w[n], given["m_" + n], given["v_" + n])
    return (loss, grad_x, *[grad_w[n] for n in TWIN_WEIGHTS], *[delta_w[n] for n in TWIN_WEIGHTS],
            *[new_m[n] for n in TWIN_WEIGHTS], *[new_v[n] for n in TWIN_WEIGHTS])
```

```python
import functools

import jax
import jax.numpy as jnp
from jax import lax
from jax.experimental import pallas as pl
from jax.experimental.pallas import tpu as pltpu

F32 = jnp.float32
BF16 = jnp.bfloat16

D_MODEL = 1024
N_HEADS = 16
HEAD = 64
SCAN_CHUNK = 64

VMEM_LIMIT = 56 * 1024 * 1024


_BDIMS = {
    "nn": (((2,), (1,)), ((0,), (0,))),
    "nt": (((2,), (2,)), ((0,), (0,))),
    "tn": (((1,), (1,)), ((0,), (0,))),
}


def _raw_bdot(x, y, mode):
    return lax.dot_general(x, y, _BDIMS[mode], precision=lax.Precision.HIGHEST, preferred_element_type=F32)


@functools.partial(jax.custom_vjp, nondiff_argnums=(2,))
def bdot(x, y, mode):
    return _raw_bdot(x, y, mode)


def _bdot_fwd(x, y, mode):
    return _raw_bdot(x, y, mode), (x, y)


def _bdot_bwd(mode, res, g):
    x, y = res
    if mode == "nn":
        return bdot(g, y, "nt"), bdot(x, g, "tn")
    if mode == "nt":
        return bdot(g, y, "nn"), bdot(g, x, "tn")
    return bdot(y, g, "nt"), bdot(x, g, "nn")


bdot.defvjp(_bdot_fwd, _bdot_bwd)


def _scan_chunk(S0, r, lw, k, v, a, b):
    nh, lc, _ = r.shape
    ti = lax.broadcasted_iota(jnp.int32, (lc, lc), 0)
    si = lax.broadcasted_iota(jnp.int32, (lc, lc), 1)
    incl = (si <= ti).astype(F32)
    strict = (si < ti).astype(F32)
    eye = (si == ti).astype(F32)
    cl = bdot(jnp.broadcast_to(incl, (nh, lc, lc)), lw, "nn")
    cl_last = cl[:, lc - 1:lc, :]
    g_last = jnp.exp(cl_last - cl)
    at = a * jnp.exp(cl - lw)
    bt = b * jnp.exp(-cl)
    kt = k * jnp.exp(-cl)
    rt = r * jnp.exp(cl)
    m_ab = bdot(at, bt, "nt") * strict
    m_ak = bdot(at, kt, "nt") * strict
    m_rb = bdot(rt, bt, "nt") * incl
    m_rk = bdot(rt, kt, "nt") * incl
    x = eye + m_ab
    p = m_ab
    n = 1
    while n * 2 < lc:
        p = bdot(p, p, "nn")
        x = x + bdot(x, p, "nn")
        n *= 2
    u = bdot(x, bdot(at, S0, "nt") + bdot(m_ak, v, "nn"), "nn")
    o = bdot(rt, S0, "nt") + bdot(m_rb, u, "nn") + bdot(m_rk, v, "nn")
    s_last = S0 * jnp.exp(cl_last) + bdot(u, b * g_last, "tn") + bdot(v, k * g_last, "tn")
    return o, s_last


def _scan_specs(t):
    nc = t // SCAN_CHUNK
    blk = lambda rev: pl.BlockSpec(
        (N_HEADS, SCAN_CHUNK, HEAD), (lambda c: (0, nc - 1 - c, 0)) if rev else (lambda c: (0, c, 0)))
    st = lambda rev: pl.BlockSpec(
        (1, N_HEADS, HEAD, HEAD), (lambda c: (nc - 1 - c, 0, 0, 0)) if rev else (lambda c: (c, 0, 0, 0)))
    return nc, blk, st


def scan_fwd(r, lw, k, v, a, b):
    t = r.shape[1]
    nc, blk, st = _scan_specs(t)

    def body(r_ref, lw_ref, k_ref, v_ref, a_ref, b_ref, o_ref, s0_ref, s_scr):
        @pl.when(pl.program_id(0) == 0)
        def _():
            s_scr[...] = jnp.zeros_like(s_scr)

        s0 = s_scr[...]
        s0_ref[0] = s0
        o, s_last = _scan_chunk(s0, r_ref[...], lw_ref[...], k_ref[...], v_ref[...], a_ref[...], b_ref[...])
        o_ref[...] = o
        s_scr[...] = s_last

    return pl.pallas_call(
        body,
        name="scan_fwd",
        grid=(nc,),
        in_specs=[blk(False)] * 6,
        out_specs=[blk(False), st(False)],
        out_shape=[jax.ShapeDtypeStruct(r.shape, F32), jax.ShapeDtypeStruct((nc, N_HEADS, HEAD, HEAD), F32)],
        scratch_shapes=[pltpu.VMEM((N_HEADS, HEAD, HEAD), F32)],
        compiler_params=pltpu.CompilerParams(dimension_semantics=("arbitrary",), vmem_limit_bytes=VMEM_LIMIT),
    )(r, lw, k, v, a, b)


def scan_bwd(r, lw, k, v, a, b, s0s, do):
    t = r.shape[1]
    nc, blk, st = _scan_specs(t)

    def body(r_ref, lw_ref, k_ref, v_ref, a_ref, b_ref, s0_ref, do_ref,
             dr_ref, dlw_ref, dk_ref, dv_ref, da_ref, db_ref, ds_scr):
        @pl.when(pl.program_id(0) == 0)
        def _():
            ds_scr[...] = jnp.zeros_like(ds_scr)

        _, vjp = jax.vjp(_scan_chunk, s0_ref[0], r_ref[...], lw_ref[...], k_ref[...], v_ref[...], a_ref[...],
                         b_ref[...])
        ds0, dr, dlw, dk, dv, da, db = vjp((do_ref[...], ds_scr[...]))
        dr_ref[...] = dr
        dlw_ref[...] = dlw
        dk_ref[...] = dk
        dv_ref[...] = dv
        da_ref[...] = da
        db_ref[...] = db
        ds_scr[...] = ds0

    return pl.pallas_call(
        body,
        name="scan_bwd",
        grid=(nc,),
        in_specs=[blk(True)] * 6 + [st(True), blk(True)],
        out_specs=[blk(True)] * 6,
        out_shape=[jax.ShapeDtypeStruct(r.shape, F32)] * 6,
        scratch_shapes=[pltpu.VMEM((N_HEADS, HEAD, HEAD), F32)],
        compiler_params=pltpu.CompilerParams(dimension_semantics=("arbitrary",), vmem_limit_bytes=VMEM_LIMIT),
    )(r, lw, k, v, a, b, s0s, do)


_MDIMS = {
    "nn": (((1,), (0,)), ((), ())),
    "nt": (((1,), (1,)), ((), ())),
    "tn": (((0,), (0,)), ((), ())),
}


def _raw_mdot(x, y, mode, exact):
    if exact:
        return lax.dot_general(x, y, _MDIMS[mode], precision=lax.Precision.HIGHEST, preferred_element_type=F32)
    return lax.dot_general(x.astype(BF16), y.astype(BF16), _MDIMS[mode], preferred_element_type=F32)


@functools.partial(jax.custom_vjp, nondiff_argnums=(2, 3))
def mdot(x, y, mode, exact):
    return _raw_mdot(x, y, mode, exact)


def _mdot_fwd(x, y, mode, exact):
    return _raw_mdot(x, y, mode, exact), (x, y)


def _mdot_bwd(mode, exact, res, g):
    x, y = res
    if mode == "nn":
        return mdot(g, y, "nt", exact), mdot(x, g, "tn", exact)
    if mode == "nt":
        return mdot(g, y, "nn", exact), mdot(g, x, "tn", exact)
    return mdot(y, g, "nt", exact), mdot(x, g, "nn", exact)


mdot.defvjp(_mdot_fwd, _mdot_bwd)


def _seg_ones():
    i = lax.broadcasted_iota(jnp.int32, (256, 256), 0) // HEAD
    j = lax.broadcasted_iota(jnp.int32, (256, 256), 1) // HEAD
    return (i == j).astype(BF16)


@jax.custom_vjp
def segsum(x):
    bd = _seg_ones()
    hi = x.astype(BF16)
    lo = (x - hi.astype(F32)).astype(BF16)
    cols = []
    for j in range(x.shape[1] // 256):
        sl = slice(256 * j, 256 * (j + 1))
        cols.append(jnp.dot(hi[:, sl], bd, preferred_element_type=F32)
                    + jnp.dot(lo[:, sl], bd, preferred_element_type=F32))
    return jnp.concatenate(cols, axis=1)


segsum.defvjp(lambda x: (segsum(x), None), lambda _, g: (segsum(g),))


NORM_EPS = 1e-6
LN_EPS = 1e-5
GN_EPS = 64e-5
SGU_CHUNK = 128
SGU_GROUPS = 8


def _rms(x, g):
    return x * lax.rsqrt(jnp.mean(x * x, axis=-1, keepdims=True) + NORM_EPS) * g


def f_norm_in(x, g):
    return _rms(x, g), x


def f_sgu(p, ln_w, ln_b, sw, sbt):
    tm = p.shape[0]
    z = 0.5 * p * (1.0 + lax.erf(p * 0.7071067811865476))
    u, v = z[:, :D_MODEL], z[:, D_MODEL:]
    mu = jnp.mean(v, axis=-1, keepdims=True)
    d = v - mu
    vn = d * lax.rsqrt(jnp.mean(d * d, axis=-1, keepdims=True) + LN_EPS) * ln_w + ln_b
    ii = lax.broadcasted_iota(jnp.int32, (SGU_CHUNK, SGU_CHUNK), 0)
    jj = lax.broadcasted_iota(jnp.int32, (SGU_CHUNK, SGU_CHUNK), 1)
    mask = (jj <= ii).astype(F32)
    gi = lax.broadcasted_iota(jnp.int32, (SGU_GROUPS, D_MODEL), 0)
    ci = lax.broadcasted_iota(jnp.int32, (SGU_GROUPS, D_MODEL), 1) // SGU_CHUNK
    bias = mdot(sbt, (gi == ci).astype(F32), "nn", True)
    rows = []
    for c in range(tm // SGU_CHUNK):
        cols = []
        for g in range(SGU_GROUPS):
            blk = vn[c * SGU_CHUNK:(c + 1) * SGU_CHUNK, g * SGU_CHUNK:(g + 1) * SGU_CHUNK]
            cols.append(mdot(sw[g] * mask, blk, "nn", False))
        rows.append(jnp.concatenate(cols, axis=1) + bias)
    return (u * jnp.concatenate(rows, axis=0),)


def _softplus(x):
    return jnp.maximum(x, 0.0) + jnp.log1p(jnp.exp(-jnp.abs(x)))


def f_pre(qr, qk, qv, ql, wl, w0, al, a0, gl, k_k, k_a):
    xw, xa, xg = ql[:, :128], ql[:, 128:256], ql[:, 256:512]
    wr = -_softplus(-(w0 + mdot(jnp.tanh(xw), wl, "nn", True))) - 0.5
    lw = -jnp.exp(wr)
    aa = jax.nn.sigmoid(a0 + mdot(xa, al, "nn", True))
    g = mdot(jax.nn.sigmoid(xg), gl, "nn", True)
    kkr = qk * k_k
    kk = kkr / jnp.maximum(jnp.sqrt(segsum(kkr * kkr)), 1e-12)
    kp = qk * (1.0 + (aa - 1.0) * k_a)
    return qr, lw, kp, qv, -kk, kk * aa, g, qr, kp, qv


def f_post(o, r, kp, v, g, lnw, lnb, rk):
    mu = segsum(o) * (1.0 / HEAD)
    d = o - mu
    gn = d * lax.rsqrt(segsum(d * d) * (1.0 / HEAD) + GN_EPS)
    return ((gn * lnw + lnb + segsum(r * kp * rk) * v) * g,)


def f_mix(ya, yb, ga, gb):
    return (jax.nn.sigmoid(ga) * ya + jax.nn.sigmoid(gb) * yb,)


def f_ffn_in(h1, g):
    return _rms(h1, g), h1


def f_final(h1, m3, tgt, g):
    y = _rms(h1 + m3, g)
    err = jnp.square(y - tgt)
    return 0.5 * jnp.sum(jnp.mean(err, axis=-1))


def _cparams(n_grid):
    return pltpu.CompilerParams(dimension_semantics=("arbitrary",) * n_grid, vmem_limit_bytes=VMEM_LIMIT)


def _tile_spec(tm, w, cb):
    return pl.BlockSpec((tm, w), lambda i: (i, cb))


def _const_spec(c):
    nd = c.ndim
    return pl.BlockSpec(c.shape, lambda i: (0,) * nd)


def ew_call(fn, tiled, consts, outs, *, tm, name):
    t = tiled[0][0].shape[0]
    n_t, n_c = len(tiled), len(consts)

    def body(*refs):
        tv = [r[...].astype(F32) for r in refs[:n_t]]
        cv = [r[...] for r in refs[n_t:n_t + n_c]]
        res = fn(*tv, *cv)
        for o_ref, val in zip(refs[n_t + n_c:], res):
            o_ref[...] = val.astype(o_ref.dtype)

    return pl.pallas_call(
        body,
        name=name,
        grid=(t // tm,),
        in_specs=[_tile_spec(tm, w, cb) for _, w, cb in tiled] + [_const_spec(c) for c in consts],
        out_specs=[_tile_spec(tm, w, 0) for w, _ in outs],
        out_shape=[jax.ShapeDtypeStruct((t, w), dt) for w, dt in outs],
        compiler_params=_cparams(1),
    )(*[a for a, _, _ in tiled], *consts)


def ew_vjp_call(fn, tiled, consts, cots, d_tiled, d_consts, *, tm, name):
    t = tiled[0][0].shape[0]
    n_t, n_c, n_g = len(tiled), len(consts), len(cots)
    dt_list = [(i, dt) for i, dts in enumerate(d_tiled) for dt in dts]
    dc_list = [i for i, want in enumerate(d_consts) if want]

    def body(*refs):
        tv = [r[...].astype(F32) for r in refs[:n_t]]
        cv = [r[...] for r in refs[n_t:n_t + n_c]]
        gv = tuple(r[...].astype(F32) for r in refs[n_t + n_c:n_t + n_c + n_g])
        out_refs = refs[n_t + n_c + n_g:]
        _, vjp = jax.vjp(fn, *tv, *cv)
        grads = vjp(gv)
        for o_ref, (i, _) in zip(out_refs, dt_list):
            o_ref[...] = grads[i].astype(o_ref.dtype)
        acc_refs = out_refs[len(dt_list):]

        @pl.when(pl.program_id(0) == 0)
        def _():
            for a_ref in acc_refs:
                a_ref[...] = jnp.zeros_like(a_ref)

        for a_ref, i in zip(acc_refs, dc_list):
            a_ref[...] += grads[n_t + i]

    res = pl.pallas_call(
        body,
        name=name,
        grid=(t // tm,),
        in_specs=[_tile_spec(tm, w, cb) for _, w, cb in tiled] + [_const_spec(c) for c in consts]
        + [_tile_spec(tm, w, cb) for _, w, cb in cots],
        out_specs=[_tile_spec(tm, tiled[i][1], 0) for i, _ in dt_list] + [_const_spec(consts[i]) for i in dc_list],
        out_shape=[jax.ShapeDtypeStruct((t, tiled[i][1]), dt) for i, dt in dt_list]
        + [jax.ShapeDtypeStruct(consts[i].shape, F32) for i in dc_list],
        compiler_params=_cparams(1),
    )(*[a for a, _, _ in tiled], *consts, *[a for a, _, _ in cots])
    return res[:len(dt_list)], res[len(dt_list):]


def mm(a, b, mode, *, tm, tn, name, out_dtypes=(F32,), epi=None, extras=()):
    m = a.shape[1] if mode == "tn" else a.shape[0]
    kd = a.shape[0] if mode == "tn" else a.shape[1]
    n = b.shape[0] if mode == "nt" else b.shape[1]
    tm, tn = min(tm, m), min(tn, n)
    if mode == "nn":
        a_spec = pl.BlockSpec((tm, kd), lambda i, j: (i, 0))
        b_spec = pl.BlockSpec((kd, tn), lambda i, j: (0, j))
    elif mode == "nt":
        a_spec = pl.BlockSpec((tm, kd), lambda i, j: (i, 0))
        b_spec = pl.BlockSpec((tn, kd), lambda i, j: (j, 0))
    else:
        a_spec = pl.BlockSpec((kd, tm), lambda i, j: (0, i))
        b_spec = pl.BlockSpec((kd, tn), lambda i, j: (0, j))
    n_e = len(extras)
    o_spec = pl.BlockSpec((tm, tn), lambda i, j: (i, j))

    def body(a_ref, b_ref, *refs):
        c = lax.dot_general(a_ref[...], b_ref[...], _MDIMS[mode], preferred_element_type=F32)
        res = epi(c, *[r[...] for r in refs[:n_e]]) if epi is not None else (c,)
        for o_ref, val in zip(refs[n_e:], res):
            o_ref[...] = val.astype(o_ref.dtype)

    res = pl.pallas_call(
        body,
        name=name,
        grid=(m // tm, n // tn),
        in_specs=[a_spec, b_spec] + [o_spec] * n_e,
        out_specs=[o_spec] * len(out_dtypes),
        out_shape=[jax.ShapeDtypeStruct((m, n), dt) for dt in out_dtypes],
        compiler_params=_cparams(2),
    )(a, b, *extras)
    return res if len(out_dtypes) > 1 else res[0]


P_WIDTH = 7680
RWKV_COL0 = 4096
RWKV_WIDTH = 3584
SHIFT_BLK = 512


def _shift_down(p, prev_row):
    rows = lax.broadcasted_iota(jnp.int32, p.shape, 0)
    return jnp.where(rows == 0, prev_row, pltpu.roll(p, 1, 0))


def shiftmix_fwd(p_all, sbp, *, tm):
    t = p_all.shape[0]
    c0 = RWKV_COL0 // SHIFT_BLK
    hb = tm // 8

    def body(p_ref, halo_ref, sb_ref, q_ref):
        p = p_ref[...]
        prev = jnp.where(pl.program_id(0) == 0, 0.0, halo_ref[7:8, :])
        q_ref[...] = p * sb_ref[0:1, :] + _shift_down(p, prev) * sb_ref[1:2, :]

    return pl.pallas_call(
        body,
        name="shiftmix_fwd",
        grid=(t // tm, RWKV_WIDTH // SHIFT_BLK),
        in_specs=[
            pl.BlockSpec((tm, SHIFT_BLK), lambda i, j: (i, c0 + j)),
            pl.BlockSpec((8, SHIFT_BLK), lambda i, j: (jnp.maximum(i * hb - 1, 0), c0 + j)),
            pl.BlockSpec((2, SHIFT_BLK), lambda i, j: (0, j)),
        ],
        out_specs=pl.BlockSpec((tm, SHIFT_BLK), lambda i, j: (i, j)),
        out_shape=jax.ShapeDtypeStruct((t, RWKV_WIDTH), F32),
        compiler_params=_cparams(2),
    )(p_all, p_all, sbp)


def shiftmix_bwd(dq, col0, p_all, sbp, *, tm, name):
    t, w = dq.shape
    n_i = t // tm
    hb = tm // 8
    cq = col0 // SHIFT_BLK
    cp = (RWKV_COL0 + col0) // SHIFT_BLK

    def body(dq_ref, dqn_ref, p_ref, ph_ref, sb_ref, dp_ref, dsb_ref):
        i = pl.program_id(1)
        dq_t = dq_ref[...]
        rows = lax.broadcasted_iota(jnp.int32, dq_t.shape, 0)
        nxt = jnp.where(i == n_i - 1, 0.0, dqn_ref[0:1, :])
        up = jnp.where(rows == tm - 1, nxt, pltpu.roll(dq_t, tm - 1, 0))
        dp_ref[...] = (dq_t * sb_ref[0:1, :] + up * sb_ref[1:2, :]).astype(dp_ref.dtype)
        p = p_ref[...]
        prev = jnp.where(i == 0, 0.0, ph_ref[7:8, :])
        s0 = jnp.sum(dq_t * p, axis=0, keepdims=True)
        s1 = jnp.sum(dq_t * _shift_down(p, prev), axis=0, keepdims=True)
        two = lax.broadcasted_iota(jnp.int32, (2, SHIFT_BLK), 0)

        @pl.when(i == 0)
        def _():
            dsb_ref[...] = jnp.zeros_like(dsb_ref)

        dsb_ref[...] += jnp.where(two == 0, s0, s1)

    return pl.pallas_call(
        body,
        name=name,
        grid=(w // SHIFT_BLK, n_i),
        in_specs=[
            pl.BlockSpec((tm, SHIFT_BLK), lambda j, i: (i, j)),
            pl.BlockSpec((8, SHIFT_BLK), lambda j, i: (jnp.minimum((i + 1) * hb, t // 8 - 1), j)),
            pl.BlockSpec((tm, SHIFT_BLK), lambda j, i: (i, cp + j)),
            pl.BlockSpec((8, SHIFT_BLK), lambda j, i: (jnp.maximum(i * hb - 1, 0), cp + j)),
            pl.BlockSpec((2, SHIFT_BLK), lambda j, i: (0, cq + j)),
        ],
        out_specs=[
            pl.BlockSpec((tm, SHIFT_BLK), lambda j, i: (i, j)),
            pl.BlockSpec((2, SHIFT_BLK), lambda j, i: (0, j)),
        ],
        out_shape=[jax.ShapeDtypeStruct((t, w), BF16), jax.ShapeDtypeStruct((2, w), F32)],
        compiler_params=_cparams(2),
    )(dq, dq, p_all, p_all, sbp)


def final_call(h1, m3, tgt, g_final, *, tm):
    t = h1.shape[0]

    def body(h1_ref, m3_ref, tgt_ref, g_ref, dh_ref, dhb_ref, dg_ref, loss_ref):
        loss, vjp = jax.vjp(f_final, h1_ref[...], m3_ref[...], tgt_ref[...], g_ref[...])
        dh, _, _, dg = vjp(jnp.ones((), F32))
        dh_ref[...] = dh
        dhb_ref[...] = dh.astype(BF16)

        @pl.when(pl.program_id(0) == 0)
        def _():
            dg_ref[...] = jnp.zeros_like(dg_ref)
            loss_ref[...] = jnp.zeros_like(loss_ref)

        dg_ref[...] += dg
        loss_ref[...] += jnp.full(loss_ref.shape, loss, F32)

    tile = _tile_spec(tm, D_MODEL, 0)
    return pl.pallas_call(
        body,
        name="final_loss",
        grid=(t // tm,),
        in_specs=[tile, tile, tile, _const_spec(g_final)],
        out_specs=[tile, tile, _const_spec(g_final), pl.BlockSpec((8, 128), lambda i: (0, 0))],
        out_shape=[jax.ShapeDtypeStruct((t, D_MODEL), F32), jax.ShapeDtypeStruct((t, D_MODEL), BF16),
                   jax.ShapeDtypeStruct(g_final.shape, F32), jax.ShapeDtypeStruct((8, 128), F32)],
        compiler_params=_cparams(1),
    )(h1, m3, tgt, g_final)


N_SGU = 2048
N_RWKV = 3360
LORA_W, LORA_A, LORA_G = 64, 64, 160


def _pad_rwkv_cols(z):
    zero = lambda n: jnp.zeros(z.shape[:-1] + (n,), z.dtype)
    return jnp.concatenate([z[..., :3072], z[..., 3072:3136], zero(64), z[..., 3136:3200], zero(64),
                            z[..., 3200:3360], zero(96)], axis=-1)


def _unpad_rwkv_cols(z):
    return jnp.concatenate([z[..., :3072], z[..., 3072:3136], z[..., 3200:3264], z[..., 3328:3488]], axis=-1)


def _pad_win(w):
    return jnp.concatenate([w[:, :N_SGU], w[:, N_SGU + N_RWKV:], _pad_rwkv_cols(w[:, N_SGU:N_SGU + N_RWKV])], axis=1)


def _unpad_win(w):
    return jnp.concatenate([w[:, :N_SGU], _unpad_rwkv_cols(w[:, RWKV_COL0:]), w[:, N_SGU:RWKV_COL0]], axis=1)


def _pad_rows(w, n):
    return jnp.concatenate([w, jnp.zeros((n - w.shape[0],) + w.shape[1:], w.dtype)], axis=0)


def _to_heads(z):
    return z.reshape(z.shape[0], N_HEADS, HEAD).transpose(1, 0, 2)


def _from_heads(z):
    return z.transpose(1, 0, 2).reshape(z.shape[1], D_MODEL)


def _relu2_epi(c):
    return c, jnp.square(jnp.maximum(c, 0.0))


def _relu2_bwd_epi(c, hid):
    return (c * (2.0 * jnp.maximum(hid, 0.0)),)


def _add_epi(c, x):
    return (c + x,)


def _pre_fwd(*args):
    res = f_pre(*args)
    return res[1], res[2], res[4], res[5], res[6]


def local_step(x, tgt, w):
    d = D_MODEL
    win_p = _pad_win(w["w_in"])
    sbp = _pad_rwkv_cols(w["shift_b"])
    wl = _pad_rows(w["w_lora_w"], 128)
    al = _pad_rows(w["a_lora_w"], 128)
    gl = _pad_rows(w["g_lora_w"], 256)
    sbt = w["sgu_b"].T

    (a_bf,) = ew_call(lambda x_, g_: (f_norm_in(x_, g_)[0],), [(x, d, 0)], [w["g_mix"]], [(d, BF16)], tm=256,
                      name="norm_in")
    p_all = mm(a_bf, win_p, "nn", tm=512, tn=1280, name="mm_in")
    sgu_t = [(p_all, 2 * d, 0)]
    sgu_c = [w["sgu_ln_w"], w["sgu_ln_b"], w["sgu_w"], sbt]
    (s_bf,) = ew_call(f_sgu, sgu_t, sgu_c, [(d, BF16)], tm=256, name="sgu_fwd")
    ya = mm(s_bf, w["w_proj_a"], "nn", tm=512, tn=1024, name="mm_proj_a")
    q = shiftmix_fwd(p_all, sbp, tm=256)
    pre_t = [(q, d, 0), (q, d, 1), (q, d, 2), (q, 512, 6)]
    pre_c = [wl, w["w0"], al, w["a0"], gl, w["k_k"], w["k_a"]]
    lw, kp, na, nb, g = ew_call(_pre_fwd, pre_t, pre_c, [(d, F32)] * 5, tm=256, name="rwkv_pre_fwd")
    r_hm, v_hm = _to_heads(q[:, :d]), _to_heads(q[:, 2 * d:3 * d])
    lw_hm, kp_hm, na_hm, nb_hm = _to_heads(lw), _to_heads(kp), _to_heads(na), _to_heads(nb)
    o_hm, s0s = scan_fwd(r_hm, lw_hm, kp_hm, v_hm, na_hm, nb_hm)
    o = _from_heads(o_hm)
    post_t = [(o, d, 0), (q, d, 0), (kp, d, 0), (q, d, 2), (g, d, 0)]
    post_c = [w["ln_x_w"], w["ln_x_b"], w["r_k"]]
    (ob_bf,) = ew_call(f_post, post_t, post_c, [(d, BF16)], tm=256, name="rwkv_post_fwd")
    yb = mm(ob_bf, w["w_proj_b"], "nn", tm=512, tn=1024, name="mm_proj_b")
    mix_t = [(ya, d, 0), (yb, d, 0), (p_all, d, 2), (p_all, d, 3)]
    (mixed_bf,) = ew_call(f_mix, mix_t, [], [(d, BF16)], tm=256, name="mix_fwd")
    h1 = mm(mixed_bf, w["w_out"], "nn", tm=512, tn=1024, name="mm_out", epi=_add_epi, extras=(x,))
    (f_bf,) = ew_call(lambda h_, g_: (f_ffn_in(h_, g_)[0],), [(h1, d, 0)], [w["g_ffn"]], [(d, BF16)], tm=256,
                      name="ffn_norm")
    hid, act_bf = mm(f_bf, w["w_ffn1"], "nn", tm=512, tn=1024, name="mm_ffn1", out_dtypes=(F32, BF16), epi=_relu2_epi)
    m3 = mm(act_bf, w["w_ffn2"], "nn", tm=512, tn=1024, name="mm_ffn2")
    dh2, dh2_bf, dg_final, loss = final_call(h1, m3, tgt, w["g_final"], tm=256)

    dhid_bf = mm(dh2_bf, w["w_ffn2"], "nt", tm=512, tn=1024, name="mm_dact", out_dtypes=(BF16,), epi=_relu2_bwd_epi,
                 extras=(hid,))
    d_ffn2 = mm(act_bf, dh2_bf, "tn", tm=512, tn=1024, name="mm_dw_ffn2")
    df = mm(dhid_bf, w["w_ffn1"], "nt", tm=512, tn=1024, name="mm_df")
    d_ffn1 = mm(f_bf, dhid_bf, "tn", tm=512, tn=1024, name="mm_dw_ffn1")
    (dh1, dh1_bf), (dg_ffn,) = ew_vjp_call(f_ffn_in, [(h1, d, 0)], [w["g_ffn"]], [(df, d, 0), (dh2, d, 0)],
                                           [(F32, BF16)], [True], tm=256, name="ffn_norm_bwd")
    dmixed = mm(dh1_bf, w["w_out"], "nt", tm=512, tn=1024, name="mm_dmixed")
    d_out = mm(mixed_bf, dh1_bf, "tn", tm=512, tn=1024, name="mm_dw_out")
    (dya_bf, dyb_bf, dga_bf, dgb_bf), _ = ew_vjp_call(f_mix, mix_t, [], [(dmixed, d, 0)], [(BF16,)] * 4, [], tm=256,
                                                      name="mix_bwd")
    dob = mm(dyb_bf, w["w_proj_b"], "nt", tm=512, tn=1024, name="mm_dob")
    d_proj_b = mm(ob_bf, dyb_bf, "tn", tm=512, tn=1024, name="mm_dw_proj_b")
    (do, dr_p, dkp_p, dv_p, dg), (dlnx_w, dlnx_b, dr_k) = ew_vjp_call(
        f_post, post_t, post_c, [(dob, d, 0)], [(F32,)] * 5, [True] * 3, tm=256, name="rwkv_post_bwd")
    scan_g = scan_bwd(r_hm, lw_hm, kp_hm, v_hm, na_hm, nb_hm, s0s, _to_heads(do))
    pre_g = [(_from_heads(z), d, 0) for z in scan_g] + [(dg, d, 0), (dr_p, d, 0), (dkp_p, d, 0), (dv_p, d, 0)]
    (dq_r, dq_k, dq_v, dq_l), (dwl, dw0, dal, da0, dgl, dk_k, dk_a) = ew_vjp_call(
        f_pre, pre_t, pre_c, pre_g, [(F32,)] * 4, [True] * 7, tm=128, name="rwkv_pre_bwd")
    dp_r, dsb_r = shiftmix_bwd(dq_r, 0, p_all, sbp, tm=256, name="shiftmix_bwd_r")
    dp_k, dsb_k = shiftmix_bwd(dq_k, d, p_all, sbp, tm=256, name="shiftmix_bwd_k")
    dp_v, dsb_v = shiftmix_bwd(dq_v, 2 * d, p_all, sbp, tm=256, name="shiftmix_bwd_v")
    dp_l, dsb_l = shiftmix_bwd(dq_l, 3 * d, p_all, sbp, tm=256, name="shiftmix_bwd_l")
    ds = mm(dya_bf, w["w_proj_a"], "nt", tm=512, tn=1024, name="mm_ds")
    d_proj_a = mm(s_bf, dya_bf, "tn", tm=512, tn=1024, name="mm_dw_proj_a")
    (dp_sgu,), (dln_w, dln_b, dsw, dsbt) = ew_vjp_call(f_sgu, sgu_t, sgu_c, [(ds, d, 0)], [(BF16,)], [True] * 4,
                                                       tm=256, name="sgu_bwd")
    dp_all = jnp.concatenate([dp_sgu, dga_bf, dgb_bf, dp_r, dp_k, dp_v, dp_l], axis=1)
    da = mm(dp_all, win_p, "nt", tm=512, tn=512, name="mm_da")
    d_in_p = mm(a_bf, dp_all, "tn", tm=512, tn=1280, name="mm_dw_in")
    (grad_x,), (dg_mix,) = ew_vjp_call(f_norm_in, [(x, d, 0)], [w["g_mix"]], [(da, d, 0), (dh1, d, 0)], [(F32,)],
                                       [True], tm=256, name="norm_in_bwd")

    grads = {
        "g_mix": dg_mix, "w_in": _unpad_win(d_in_p), "sgu_ln_w": dln_w, "sgu_ln_b": dln_b, "sgu_w": dsw,
        "sgu_b": dsbt.T, "w_proj_a": d_proj_a,
        "shift_b": _unpad_rwkv_cols(jnp.concatenate([dsb_r, dsb_k, dsb_v, dsb_l], axis=1)),
        "w_lora_w": dwl[:LORA_W], "w0": dw0, "a_lora_w": dal[:LORA_A], "a0": da0, "g_lora_w": dgl[:LORA_G],
        "k_k": dk_k, "k_a": dk_a, "r_k": dr_k, "ln_x_w": dlnx_w, "ln_x_b": dlnx_b, "w_proj_b": d_proj_b,
        "w_out": d_out, "g_ffn": dg_ffn, "w_ffn1": d_ffn1, "w_ffn2": d_ffn2, "g_final": dg_final,
    }
    return loss[0, 0], grad_x, grads


MESH = pl.DeviceIdType.MESH
N_CHIPS = 4
N_DEV = 8
PACK_ROWS = 4864
HALF_ROWS = PACK_ROWS // 2
SMALL_ROWS = 152
_ANY = pl.BlockSpec(memory_space=pl.ANY)


def _coords():
    return lax.axis_index("x"), lax.axis_index("y"), lax.axis_index("c")


def _other_chips(x, y):
    return [(1 - x, y), (x, 1 - y), (1 - x, 1 - y)]


def _remote(src, dst, send_sems, recv_sems, k, to):
    return pltpu.make_async_remote_copy(src_ref=src, dst_ref=dst, send_sem=send_sems.at[k], recv_sem=recv_sems.at[k],
                                        device_id=to, device_id_type=MESH)


def gather_shards(pack):
    def body(src_ref, out_ref, send_sems, recv_sems, local_sem):
        x, y, c = _coords()
        me = 2 * x + y
        sib = (x, y, 1 - c)
        chips = _other_chips(x, y)
        mine = pltpu.make_async_copy(src_ref, out_ref.at[me], local_sem)
        mine.start()
        first = [_remote(src_ref.at[c], out_ref.at[me, c], send_sems, recv_sems, k, (cx, cy, c))
                 for k, (cx, cy) in enumerate(chips)]
        for cp in first:
            cp.start()
        passed = []
        for k, (cx, cy) in enumerate(chips):
            j = 2 * cx + cy
            _remote(src_ref.at[c], out_ref.at[j, c], send_sems, recv_sems, k, (cx, cy, c)).wait_recv()
            fwd = _remote(out_ref.at[j, c], out_ref.at[j, c], send_sems, recv_sems, 3 + k, sib)
            fwd.start()
            passed.append(fwd)
        for k, (cx, cy) in enumerate(chips):
            j = 2 * cx + cy
            _remote(out_ref.at[j, 1 - c], out_ref.at[j, 1 - c], send_sems, recv_sems, 3 + k, sib).wait_recv()
        for cp in first + passed:
            cp.wait_send()
        mine.wait()

    return pl.pallas_call(
        body,
        name="gather_shards",
        in_specs=[_ANY],
        out_specs=_ANY,
        out_shape=jax.ShapeDtypeStruct((N_CHIPS,) + pack.shape, pack.dtype),
        scratch_shapes=[pltpu.SemaphoreType.DMA((6,)), pltpu.SemaphoreType.DMA((6,)), pltpu.SemaphoreType.DMA],
    )(pack)


def reduce_pair(g):
    def body(g_ref, mine_ref, got_ref, send_sems, recv_sems, local_sems):
        x, y, c = _coords()
        sib = (x, y, 1 - c)
        local = [pltpu.make_async_copy(g_ref.at[j, c], mine_ref.at[j], local_sems.at[j]) for j in range(N_CHIPS)]
        sends = [_remote(g_ref.at[j, 1 - c], got_ref.at[j], send_sems, recv_sems, j, sib) for j in range(N_CHIPS)]
        for cp in sends + local:
            cp.start()
        for cp in sends:
            cp.wait_recv()
        for cp in sends:
            cp.wait_send()
        for cp in local:
            cp.wait()

    half = jax.ShapeDtypeStruct((N_CHIPS,) + g.shape[2:], g.dtype)
    return pl.pallas_call(
        body,
        name="reduce_pair",
        in_specs=[_ANY],
        out_specs=[_ANY, _ANY],
        out_shape=[half, half],
        scratch_shapes=[pltpu.SemaphoreType.DMA((N_CHIPS,)), pltpu.SemaphoreType.DMA((N_CHIPS,)),
                        pltpu.SemaphoreType.DMA((N_CHIPS,))],
    )(g)


def reduce_chips(p):
    def body(p_ref, out_ref, send_sems, recv_sems, local_sem):
        x, y, c = _coords()
        me = 2 * x + y
        chips = _other_chips(x, y)
        mine = pltpu.make_async_copy(p_ref.at[me], out_ref.at[me], local_sem)
        mine.start()
        sends = [_remote(p_ref.at[2 * cx + cy], out_ref.at[me], send_sems, recv_sems, k, (cx, cy, c))
                 for k, (cx, cy) in enumerate(chips)]
        for cp in sends:
            cp.start()
        for k, (cx, cy) in enumerate(chips):
            _remote(p_ref.at[me], out_ref.at[2 * cx + cy], send_sems, recv_sems, k, (cx, cy, c)).wait_recv()
        for cp in sends:
            cp.wait_send()
        mine.wait()

    return pl.pallas_call(
        body,
        name="reduce_chips",
        in_specs=[_ANY],
        out_specs=_ANY,
        out_shape=jax.ShapeDtypeStruct(p.shape, p.dtype),
        scratch_shapes=[pltpu.SemaphoreType.DMA((3,)), pltpu.SemaphoreType.DMA((3,)), pltpu.SemaphoreType.DMA],
    )(p)


def exchange_halves(s):
    def body(s_ref, out_ref, send_sems, recv_sems, local_sem):
        x, y, c = _coords()
        mine = pltpu.make_async_copy(s_ref, out_ref.at[c], local_sem)
        mine.start()
        send = _remote(s_ref, out_ref.at[c], send_sems, recv_sems, 0, (x, y, 1 - c))
        send.start()
        _remote(s_ref, out_ref.at[1 - c], send_sems, recv_sems, 0, (x, y, 1 - c)).wait_recv()
        send.wait_send()
        mine.wait()

    return pl.pallas_call(
        body,
        name="exchange_halves",
        in_specs=[_ANY],
        out_specs=_ANY,
        out_shape=jax.ShapeDtypeStruct((2,) + s.shape, s.dtype),
        scratch_shapes=[pltpu.SemaphoreType.DMA((1,)), pltpu.SemaphoreType.DMA((1,)), pltpu.SemaphoreType.DMA],
    )(s)


def gather_all(s):
    def body(s_ref, out_ref, send_sems, recv_sems, local_sem):
        x, y, c = _coords()
        me = 4 * x + 2 * y + c
        mine = pltpu.make_async_copy(s_ref, out_ref.at[me], local_sem)
        mine.start()
        peers = []
        for mask in range(1, N_DEV):
            px = 1 - x if mask & 4 else x
            py = 1 - y if mask & 2 else y
            pc = 1 - c if mask & 1 else c
            peers.append((px, py, pc))
        sends = [_remote(s_ref, out_ref.at[me], send_sems, recv_sems, k, peer) for k, peer in enumerate(peers)]
        for cp in sends:
            cp.start()
        for k, (px, py, pc) in enumerate(peers):
            _remote(s_ref, out_ref.at[4 * px + 2 * py + pc], send_sems, recv_sems, k, (px, py, pc)).wait_recv()
        for cp in sends:
            cp.wait_send()
        mine.wait()

    return pl.pallas_call(
        body,
        name="gather_all",
        in_specs=[_ANY],
        out_specs=_ANY,
        out_shape=jax.ShapeDtypeStruct((N_DEV,) + s.shape, s.dtype),
        scratch_shapes=[pltpu.SemaphoreType.DMA((N_DEV - 1,)), pltpu.SemaphoreType.DMA((N_DEV - 1,)),
                        pltpu.SemaphoreType.DMA],
    )(s)


def sum_slots(slots, *, tm, name):
    n, rows, width = slots.shape

    def body(*refs):
        acc = refs[0][0]
        for r in refs[1:n]:
            acc = acc + r[0]
        refs[n][...] = acc

    return pl.pallas_call(
        body,
        name=name,
        grid=(rows // tm,),
        in_specs=[pl.BlockSpec((1, tm, width), lambda i, s=s: (s, i, 0)) for s in range(n)],
        out_specs=pl.BlockSpec((tm, width), lambda i: (i, 0)),
        out_shape=jax.ShapeDtypeStruct((rows, width), slots.dtype),
        compiler_params=_cparams(1),
    )(*([slots] * n))


ADAM_LR = 0.001
ADAM_B1 = 0.9
ADAM_B2 = 0.999
ADAM_EPS = 1e-08
ADAM_WD = 0.01
ADAM_STEP = 10


def f_adamw(g, w, m, v):
    m = ADAM_B1 * m + (1.0 - ADAM_B1) * g
    v = ADAM_B2 * v + (1.0 - ADAM_B2) * jnp.square(g)
    m_hat = m / (1.0 - ADAM_B1 ** ADAM_STEP)
    v_hat = v / (1.0 - ADAM_B2 ** ADAM_STEP)
    delta = -ADAM_LR * (m_hat / (jnp.sqrt(v_hat) + ADAM_EPS) + ADAM_WD * w)
    return delta, m, v


def adamw_call(g, w, m, v, *, tm, name):
    width = g.shape[1]
    return ew_call(f_adamw, [(g, width, 0), (w, width, 0), (m, width, 0), (v, width, 0)], [], [(width, F32)] * 3,
                   tm=tm, name=name)


SHARDED = ["w_in", "w_proj_a", "w_lora_w", "a_lora_w", "g_lora_w", "w_proj_b", "w_out", "w_ffn1", "w_ffn2"]
SHARD_AXIS = {"w_in": 1, "w_proj_a": 0, "w_lora_w": 1, "a_lora_w": 1, "g_lora_w": 1, "w_proj_b": 0, "w_out": 0,
              "w_ffn1": 1, "w_ffn2": 0}
SHARD_SHAPE = {"w_in": (1024, 1864), "w_proj_a": (256, 1024), "w_lora_w": (64, 256), "a_lora_w": (64, 256),
               "g_lora_w": (160, 256), "w_proj_b": (256, 1024), "w_out": (256, 1024), "w_ffn1": (1024, 1024),
               "w_ffn2": (1024, 1024)}
SHIFT_SHARD = (2, 840)
VECTORS = ["g_mix", "sgu_ln_w", "sgu_ln_b", "w0", "a0", "k_k", "k_a", "r_k", "ln_x_w", "ln_x_b", "g_ffn", "g_final"]
SMALL = VECTORS + ["sgu_w", "sgu_b"]
SMALL_SHAPE = {**{n: (1, 1024) for n in VECTORS}, "sgu_w": (8, 128, 128), "sgu_b": (8, 128)}
WEIGHTS = ["g_mix", "w_in", "sgu_ln_w", "sgu_ln_b", "sgu_w", "sgu_b", "w_proj_a", "shift_b", "w_lora_w", "w0",
           "a_lora_w", "a0", "g_lora_w", "k_k", "k_a", "r_k", "ln_x_w", "ln_x_b", "w_proj_b", "w_out", "g_ffn",
           "w_ffn1", "w_ffn2", "g_final"]


def _size(shape):
    n = 1
    for s in shape:
        n *= s
    return n


def _pack_rows(parts, rows, dtype):
    flat = jnp.concatenate([p.reshape(-1).astype(dtype) for p in parts])
    return jnp.concatenate([flat, jnp.zeros((rows * 1024 - flat.shape[0],), dtype)]).reshape(rows, 1024)


def _unpack_rows(packed, shapes):
    flat = packed.reshape(-1)
    out, off = [], 0
    for shp in shapes:
        out.append(flat[off:off + _size(shp)].reshape(shp))
        off += _size(shp)
    return out


def _shard_of(name, full, j):
    ax = SHARD_AXIS[name]
    n = SHARD_SHAPE[name][ax]
    return lax.slice_in_dim(full, j * n, (j + 1) * n, axis=ax)


def kernel(x, g_mix, w_in, sgu_ln_w, sgu_ln_b, sgu_w, sgu_b, w_proj_a, shift_b, w_lora_w, w0, a_lora_w, a0, g_lora_w, k_k, k_a, r_k, ln_x_w, ln_x_b, w_proj_b, w_out, g_ffn, w_ffn1, w_ffn2, g_final, loss_target, m_g_mix, m_w_in, m_sgu_ln_w, m_sgu_ln_b, m_sgu_w, m_sgu_b, m_w_proj_a, m_shift_b, m_w_lora_w, m_w0, m_a_lora_w, m_a0, m_g_lora_w, m_k_k, m_k_a, m_r_k, m_ln_x_w, m_ln_x_b, m_w_proj_b, m_w_out, m_g_ffn, m_w_ffn1, m_w_ffn2, m_g_final, v_g_mix, v_w_in, v_sgu_ln_w, v_sgu_ln_b, v_sgu_w, v_sgu_b, v_w_proj_a, v_shift_b, v_w_lora_w, v_w0, v_a_lora_w, v_a0, v_g_lora_w, v_k_k, v_k_a, v_r_k, v_ln_x_w, v_ln_x_b, v_w_proj_b, v_w_out, v_g_ffn, v_w_ffn1, v_w_ffn2, v_g_final):
    given = dict(zip(WEIGHTS, (g_mix, w_in, sgu_ln_w, sgu_ln_b, sgu_w, sgu_b, w_proj_a, shift_b, w_lora_w, w0, a_lora_w, a0, g_lora_w, k_k, k_a, r_k, ln_x_w, ln_x_b, w_proj_b, w_out, g_ffn, w_ffn1, w_ffn2, g_final)))
    mom_m = dict(zip(WEIGHTS, (m_g_mix, m_w_in, m_sgu_ln_w, m_sgu_ln_b, m_sgu_w, m_sgu_b, m_w_proj_a, m_shift_b, m_w_lora_w, m_w0, m_a_lora_w, m_a0, m_g_lora_w, m_k_k, m_k_a, m_r_k, m_ln_x_w, m_ln_x_b, m_w_proj_b, m_w_out, m_g_ffn, m_w_ffn1, m_w_ffn2, m_g_final)))
    mom_v = dict(zip(WEIGHTS, (v_g_mix, v_w_in, v_sgu_ln_w, v_sgu_ln_b, v_sgu_w, v_sgu_b, v_w_proj_a, v_shift_b, v_w_lora_w, v_w0, v_a_lora_w, v_a0, v_g_lora_w, v_k_k, v_k_a, v_r_k, v_ln_x_w, v_ln_x_b, v_w_proj_b, v_w_out, v_g_ffn, v_w_ffn1, v_w_ffn2, v_g_final)))
    chip = 2 * lax.axis_index("x") + lax.axis_index("y")

    def local_block(tree, n):
        return tree[n] if n == "g_final" else tree[n][0]

    sb = local_block(given, "shift_b")
    exact = ["w_lora_w", "a_lora_w", "g_lora_w"]
    lo_part = lambda z: (z - z.astype(BF16).astype(F32)).astype(BF16)
    pack_w = _pack_rows([local_block(given, n) for n in SHARDED] + [sb]
                        + [lo_part(local_block(given, n)) for n in exact] + [lo_part(sb)], PACK_ROWS, BF16)
    gathered = gather_shards(pack_w.reshape(2, HALF_ROWS, 1024)).reshape(N_CHIPS, PACK_ROWS, 1024)
    shapes = [SHARD_SHAPE[n] for n in SHARDED] + [SHIFT_SHARD] + [SHARD_SHAPE[n] for n in exact] + [SHIFT_SHARD]
    per_chip = [_unpack_rows(gathered[j], shapes) for j in range(N_CHIPS)]
    n_sh = len(SHARDED)
    w = {}
    for i, n in enumerate(SHARDED):
        w[n] = jnp.concatenate([per_chip[j][i] for j in range(N_CHIPS)], axis=SHARD_AXIS[n])
    for i, n in enumerate(exact):
        lo = jnp.concatenate([per_chip[j][n_sh + 1 + i] for j in range(N_CHIPS)], axis=1)
        w[n] = w[n].astype(F32) + lo.astype(F32)
    w["shift_b"] = jnp.concatenate(
        [per_chip[j][n_sh].astype(F32) + per_chip[j][-1].astype(F32) for j in range(N_CHIPS)], axis=1)
    for n in SMALL:
        w[n] = local_block(given, n).reshape(SMALL_SHAPE[n])

    loss, grad_x, grads = local_step(x[0], loss_target[0], w)
    loss = lax.psum(loss, ("x", "y", "c"))

    g_pack = jnp.stack([_pack_rows([_shard_of(n, grads[n], j) for n in SHARDED], PACK_ROWS, F32)
                        for j in range(N_CHIPS)])
    own, got = reduce_pair(g_pack.reshape(N_CHIPS, 2, HALF_ROWS, 1024))
    rows4 = N_CHIPS * HALF_ROWS
    (pair_sum,) = ew_call(lambda a_, b_: (a_ + b_,),
                          [(own.reshape(rows4, 1024), 1024, 0), (got.reshape(rows4, 1024), 1024, 0)], [],
                          [(1024, F32)], tm=256, name="pair_sum")
    slots = reduce_chips(pair_sum.reshape(N_CHIPS, HALF_ROWS, 1024))
    half_sum = sum_slots(slots, tm=128, name="chip_sum")
    g_big = exchange_halves(half_sum).reshape(PACK_ROWS, 1024)
    w_big = _pack_rows([local_block(given, n) for n in SHARDED], PACK_ROWS, F32)
    m_big = _pack_rows([local_block(mom_m, n) for n in SHARDED], PACK_ROWS, F32)
    v_big = _pack_rows([local_block(mom_v, n) for n in SHARDED], PACK_ROWS, F32)
    d_big, nm_big, nv_big = adamw_call(g_big, w_big, m_big, v_big, tm=256, name="adamw_sharded")
    big_shapes = [SHARD_SHAPE[n] for n in SHARDED]
    out_g = dict(zip(SHARDED, _unpack_rows(g_big, big_shapes)))
    out_d = dict(zip(SHARDED, _unpack_rows(d_big, big_shapes)))
    out_m = dict(zip(SHARDED, _unpack_rows(nm_big, big_shapes)))
    out_v = dict(zip(SHARDED, _unpack_rows(nv_big, big_shapes)))

    small_shapes = [SMALL_SHAPE[n] for n in SMALL]
    s_pack = _pack_rows([grads[n] for n in SMALL] + [grads["shift_b"]], SMALL_ROWS, F32)
    g_small = sum_slots(gather_all(s_pack), tm=SMALL_ROWS, name="small_sum")
    w_small = _pack_rows([local_block(given, n) for n in SMALL], SMALL_ROWS, F32)
    m_small = _pack_rows([local_block(mom_m, n) for n in SMALL], SMALL_ROWS, F32)
    v_small = _pack_rows([local_block(mom_v, n) for n in SMALL], SMALL_ROWS, F32)
    d_small, nm_small, nv_small = adamw_call(g_small, w_small, m_small, v_small, tm=SMALL_ROWS, name="adamw_small")
    g_parts = _unpack_rows(g_small, small_shapes + [(2, N_RWKV)])
    out_g.update(zip(SMALL, g_parts[:-1]))
    out_d.update(zip(SMALL, _unpack_rows(d_small, small_shapes)))
    out_m.update(zip(SMALL, _unpack_rows(nm_small, small_shapes)))
    out_v.update(zip(SMALL, _unpack_rows(nv_small, small_shapes)))
    g_sb = lax.dynamic_slice_in_dim(g_parts[-1], chip * SHIFT_SHARD[1], SHIFT_SHARD[1], axis=1)
    sb_args = [_pack_rows([z], 8, F32) for z in (g_sb, sb, local_block(mom_m, "shift_b"), local_block(mom_v, "shift_b"))]
    sb_res = adamw_call(*sb_args, tm=8, name="adamw_shift_b")
    out_g["shift_b"] = g_sb
    for tree, res in zip((out_d, out_m, out_v), sb_res):
        tree["shift_b"] = _unpack_rows(res, [SHIFT_SHARD])[0]

    def block_of(tree, n):
        return tree[n].reshape(given[n].shape)

    return (loss, grad_x[None], *[block_of(out_g, n) for n in WEIGHTS], *[block_of(out_d, n) for n in WEIGHTS],
            *[block_of(out_m, n) for n in WEIGHTS], *[block_of(out_v, n) for n in WEIGHTS])
```

```python
import functools

import jax
import jax.numpy as jnp
from jax import lax
from jax.experimental import pallas as pl
from jax.experimental.pallas import tpu as pltpu

F32 = jnp.float32
BF16 = jnp.bfloat16

D_MODEL = 1024
N_HEADS = 16
HEAD = 64
SCAN_CHUNK = 64

VMEM_LIMIT = 56 * 1024 * 1024


_BDIMS = {
    "nn": (((2,), (1,)), ((0,), (0,))),
    "nt": (((2,), (2,)), ((0,), (0,))),
    "tn": (((1,), (1,)), ((0,), (0,))),
}


def _raw_bdot(x, y, mode):
    return lax.dot_general(x, y, _BDIMS[mode], precision=lax.Precision.HIGH, preferred_element_type=F32)


@functools.partial(jax.custom_vjp, nondiff_argnums=(2,))
def bdot(x, y, mode):
    return _raw_bdot(x, y, mode)


def _bdot_fwd(x, y, mode):
    return _raw_bdot(x, y, mode), (x, y)


def _bdot_bwd(mode, res, g):
    x, y = res
    if mode == "nn":
        return bdot(g, y, "nt"), bdot(x, g, "tn")
    if mode == "nt":
        return bdot(g, y, "nn"), bdot(g, x, "tn")
    return bdot(y, g, "nt"), bdot(x, g, "nn")


bdot.defvjp(_bdot_fwd, _bdot_bwd)


def _scan_chunk(S0, r, lw, k, v, a, b):
    nh, lc, _ = r.shape
    ti = lax.broadcasted_iota(jnp.int32, (lc, lc), 0)
    si = lax.broadcasted_iota(jnp.int32, (lc, lc), 1)
    incl = (si <= ti).astype(F32)
    strict = (si < ti).astype(F32)
    eye = (si == ti).astype(F32)
    cl = bdot(jnp.broadcast_to(incl, (nh, lc, lc)), lw, "nn")
    cl_last = cl[:, lc - 1:lc, :]
    g_last = jnp.exp(cl_last - cl)
    at = a * jnp.exp(cl - lw)
    bt = b * jnp.exp(-cl)
    kt = k * jnp.exp(-cl)
    rt = r * jnp.exp(cl)
    m_ab = bdot(at, bt, "nt") * strict
    m_ak = bdot(at, kt, "nt") * strict
    m_rb = bdot(rt, bt, "nt") * incl
    m_rk = bdot(rt, kt, "nt") * incl
    x = eye + m_ab
    p = m_ab
    n = 1
    while n * 2 < lc:
        p = bdot(p, p, "nn")
        x = x + bdot(x, p, "nn")
        n *= 2
    u = bdot(x, bdot(at, S0, "nt") + bdot(m_ak, v, "nn"), "nn")
    o = bdot(rt, S0, "nt") + bdot(m_rb, u, "nn") + bdot(m_rk, v, "nn")
    s_last = S0 * jnp.exp(cl_last) + bdot(u, b * g_last, "tn") + bdot(v, k * g_last, "tn")
    return o, s_last


def _scan_specs(t):
    nc = t // SCAN_CHUNK
    blk = lambda rev: pl.BlockSpec(
        (N_HEADS, SCAN_CHUNK, HEAD), (lambda c: (0, nc - 1 - c, 0)) if rev else (lambda c: (0, c, 0)))
    st = lambda rev: pl.BlockSpec(
        (1, N_HEADS, HEAD, HEAD), (lambda c: (nc - 1 - c, 0, 0, 0)) if rev else (lambda c: (c, 0, 0, 0)))
    return nc, blk, st


def scan_fwd(r, lw, k, v, a, b):
    t = r.shape[1]
    nc, blk, st = _scan_specs(t)

    def body(r_ref, lw_ref, k_ref, v_ref, a_ref, b_ref, o_ref, s0_ref, s_scr):
        @pl.when(pl.program_id(0) == 0)
        def _():
            s_scr[...] = jnp.zeros_like(s_scr)

        s0 = s_scr[...]
        s0_ref[0] = s0
        o, s_last = _scan_chunk(s0, r_ref[...], lw_ref[...], k_ref[...], v_ref[...], a_ref[...], b_ref[...])
        o_ref[...] = o
        s_scr[...] = s_last

    return pl.pallas_call(
        body,
        name="scan_fwd",
        grid=(nc,),
        in_specs=[blk(False)] * 6,
        out_specs=[blk(False), st(False)],
        out_shape=[jax.ShapeDtypeStruct(r.shape, F32), jax.ShapeDtypeStruct((nc, N_HEADS, HEAD, HEAD), F32)],
        scratch_shapes=[pltpu.VMEM((N_HEADS, HEAD, HEAD), F32)],
        compiler_params=pltpu.CompilerParams(dimension_semantics=("arbitrary",), vmem_limit_bytes=VMEM_LIMIT),
    )(r, lw, k, v, a, b)


def scan_bwd(r, lw, k, v, a, b, s0s, do):
    t = r.shape[1]
    nc, blk, st = _scan_specs(t)

    def body(r_ref, lw_ref, k_ref, v_ref, a_ref, b_ref, s0_ref, do_ref,
             dr_ref, dlw_ref, dk_ref, dv_ref, da_ref, db_ref, ds_scr):
        @pl.when(pl.program_id(0) == 0)
        def _():
            ds_scr[...] = jnp.zeros_like(ds_scr)

        _, vjp = jax.vjp(_scan_chunk, s0_ref[0], r_ref[...], lw_ref[...], k_ref[...], v_ref[...], a_ref[...],
                         b_ref[...])
        ds0, dr, dlw, dk, dv, da, db = vjp((do_ref[...], ds_scr[...]))
        dr_ref[...] = dr
        dlw_ref[...] = dlw
        dk_ref[...] = dk
        dv_ref[...] = dv
        da_ref[...] = da
        db_ref[...] = db
        ds_scr[...] = ds0

    return pl.pallas_call(
        body,
        name="scan_bwd",
        grid=(nc,),
        in_specs=[blk(True)] * 6 + [st(True), blk(True)],
        out_specs=[blk(True)] * 6,
        out_shape=[jax.ShapeDtypeStruct(r.shape, F32)] * 6,
        scratch_shapes=[pltpu.VMEM((N_HEADS, HEAD, HEAD), F32)],
        compiler_params=pltpu.CompilerParams(dimension_semantics=("arbitrary",), vmem_limit_bytes=VMEM_LIMIT),
    )(r, lw, k, v, a, b, s0s, do)


_MDIMS = {
    "nn": (((1,), (0,)), ((), ())),
    "nt": (((1,), (1,)), ((), ())),
    "tn": (((0,), (0,)), ((), ())),
}


def _raw_mdot(x, y, mode, exact):
    if exact:
        return lax.dot_general(x, y, _MDIMS[mode], precision=lax.Precision.HIGHEST, preferred_element_type=F32)
    return lax.dot_general(x.astype(BF16), y.astype(BF16), _MDIMS[mode], preferred_element_type=F32)


@functools.partial(jax.custom_vjp, nondiff_argnums=(2, 3))
def mdot(x, y, mode, exact):
    return _raw_mdot(x, y, mode, exact)


def _mdot_fwd(x, y, mode, exact):
    return _raw_mdot(x, y, mode, exact), (x, y)


def _mdot_bwd(mode, exact, res, g):
    x, y = res
    if mode == "nn":
        return mdot(g, y, "nt", exact), mdot(x, g, "tn", exact)
    if mode == "nt":
        return mdot(g, y, "nn", exact), mdot(g, x, "tn", exact)
    return mdot(y, g, "nt", exact), mdot(x, g, "nn", exact)


mdot.defvjp(_mdot_fwd, _mdot_bwd)


def _seg_ones():
    i = lax.broadcasted_iota(jnp.int32, (256, 256), 0) // HEAD
    j = lax.broadcasted_iota(jnp.int32, (256, 256), 1) // HEAD
    return (i == j).astype(BF16)


@jax.custom_vjp
def segsum(x):
    bd = _seg_ones()
    hi = x.astype(BF16)
    lo = (x - hi.astype(F32)).astype(BF16)
    cols = []
    for j in range(x.shape[1] // 256):
        sl = slice(256 * j, 256 * (j + 1))
        cols.append(jnp.dot(hi[:, sl], bd, preferred_element_type=F32)
                    + jnp.dot(lo[:, sl], bd, preferred_element_type=F32))
    return jnp.concatenate(cols, axis=1)


segsum.defvjp(lambda x: (segsum(x), None), lambda _, g: (segsum(g),))


NORM_EPS = 1e-6
LN_EPS = 1e-5
GN_EPS = 64e-5
SGU_CHUNK = 128
SGU_GROUPS = 8


def _rms(x, g):
    return x * lax.rsqrt(jnp.mean(x * x, axis=-1, keepdims=True) + NORM_EPS) * g


def f_norm_in(x, g):
    return _rms(x, g), x


def f_sgu(p, ln_w, ln_b, sw, sbt):
    tm = p.shape[0]
    z = 0.5 * p * (1.0 + lax.erf(p * 0.7071067811865476))
    u, v = z[:, :D_MODEL], z[:, D_MODEL:]
    mu = jnp.mean(v, axis=-1, keepdims=True)
    d = v - mu
    vn = d * lax.rsqrt(jnp.mean(d * d, axis=-1, keepdims=True) + LN_EPS) * ln_w + ln_b
    ii = lax.broadcasted_iota(jnp.int32, (SGU_CHUNK, SGU_CHUNK), 0)
    jj = lax.broadcasted_iota(jnp.int32, (SGU_CHUNK, SGU_CHUNK), 1)
    mask = (jj <= ii).astype(F32)
    gi = lax.broadcasted_iota(jnp.int32, (SGU_GROUPS, D_MODEL), 0)
    ci = lax.broadcasted_iota(jnp.int32, (SGU_GROUPS, D_MODEL), 1) // SGU_CHUNK
    bias = mdot(sbt, (gi == ci).astype(F32), "nn", True)
    rows = []
    for c in range(tm // SGU_CHUNK):
        cols = []
        for g in range(SGU_GROUPS):
            blk = vn[c * SGU_CHUNK:(c + 1) * SGU_CHUNK, g * SGU_CHUNK:(g + 1) * SGU_CHUNK]
            cols.append(mdot(sw[g] * mask, blk, "nn", False))
        rows.append(jnp.concatenate(cols, axis=1) + bias)
    return (u * jnp.concatenate(rows, axis=0),)


def _softplus(x):
    return jnp.maximum(x, 0.0) + jnp.log1p(jnp.exp(-jnp.abs(x)))


def f_pre(qr, qk, qv, ql, wl, w0, al, a0, gl, k_k, k_a):
    xw, xa, xg = ql[:, :128], ql[:, 128:256], ql[:, 256:512]
    wr = -_softplus(-(w0 + mdot(jnp.tanh(xw), wl, "nn", True))) - 0.5
    lw = -jnp.exp(wr)
    aa = jax.nn.sigmoid(a0 + mdot(xa, al, "nn", True))
    g = mdot(jax.nn.sigmoid(xg), gl, "nn", True)
    kkr = qk * k_k
    kk = kkr / jnp.maximum(jnp.sqrt(segsum(kkr * kkr)), 1e-12)
    kp = qk * (1.0 + (aa - 1.0) * k_a)
    return qr, lw, kp, qv, -kk, kk * aa, g, qr, kp, qv


def f_post(o, r, kp, v, g, lnw, lnb, rk):
    mu = segsum(o) * (1.0 / HEAD)
    d = o - mu
    gn = d * lax.rsqrt(segsum(d * d) * (1.0 / HEAD) + GN_EPS)
    return ((gn * lnw + lnb + segsum(r * kp * rk) * v) * g,)


def f_mix(ya, yb, ga, gb):
    return (jax.nn.sigmoid(ga) * ya + jax.nn.sigmoid(gb) * yb,)


def f_ffn_in(h1, g):
    return _rms(h1, g), h1


def f_final(h1, m3, tgt, g):
    y = _rms(h1 + m3, g)
    err = jnp.square(y - tgt)
    return 0.5 * jnp.sum(jnp.mean(err, axis=-1))


def _cparams(n_grid):
    return pltpu.CompilerParams(dimension_semantics=("arbitrary",) * n_grid, vmem_limit_bytes=VMEM_LIMIT)


def _tile_spec(tm, w, cb):
    return pl.BlockSpec((tm, w), lambda i: (i, cb))


def _const_spec(c):
    nd = c.ndim
    return pl.BlockSpec(c.shape, lambda i: (0,) * nd)


def ew_call(fn, tiled, consts, outs, *, tm, name):
    t = tiled[0][0].shape[0]
    n_t, n_c = len(tiled), len(consts)

    def body(*refs):
        tv = [r[...].astype(F32) for r in refs[:n_t]]
        cv = [r[...] for r in refs[n_t:n_t + n_c]]
        res = fn(*tv, *cv)
        for o_ref, val in zip(refs[n_t + n_c:], res):
            o_ref[...] = val.astype(o_ref.dtype)

    return pl.pallas_call(
        body,
        name=name,
        grid=(t // tm,),
        in_specs=[_tile_spec(tm, w, cb) for _, w, cb in tiled] + [_const_spec(c) for c in consts],
        out_specs=[_tile_spec(tm, w, 0) for w, _ in outs],
        out_shape=[jax.ShapeDtypeStruct((t, w), dt) for w, dt in outs],
        compiler_params=_cparams(1),
    )(*[a for a, _, _ in tiled], *consts)


def ew_vjp_call(fn, tiled, consts, cots, d_tiled, d_consts, *, tm, name):
    t = tiled[0][0].shape[0]
    n_t, n_c, n_g = len(tiled), len(consts), len(cots)
    dt_list = [(i, dt) for i, dts in enumerate(d_tiled) for dt in dts]
    dc_list = [i for i, want in enumerate(d_consts) if want]

    def body(*refs):
        tv = [r[...].astype(F32) for r in refs[:n_t]]
        cv = [r[...] for r in refs[n_t:n_t + n_c]]
        gv = tuple(r[...].astype(F32) for r in refs[n_t + n_c:n_t + n_c + n_g])
        out_refs = refs[n_t + n_c + n_g:]
        _, vjp = jax.vjp(fn, *tv, *cv)
        grads = vjp(gv)
        for o_ref, (i, _) in zip(out_refs, dt_list):
            o_ref[...] = grads[i].astype(o_ref.dtype)
        acc_refs = out_refs[len(dt_list):]

        @pl.when(pl.program_id(0) == 0)
        def _():
            for a_ref in acc_refs:
                a_ref[...] = jnp.zeros_like(a_ref)

        for a_ref, i in zip(acc_refs, dc_list):
            a_ref[...] += grads[n_t + i]

    res = pl.pallas_call(
        body,
        name=name,
        grid=(t // tm,),
        in_specs=[_tile_spec(tm, w, cb) for _, w, cb in tiled] + [_const_spec(c) for c in consts]
        + [_tile_spec(tm, w, cb) for _, w, cb in cots],
        out_specs=[_tile_spec(tm, tiled[i][1], 0) for i, _ in dt_list] + [_const_spec(consts[i]) for i in dc_list],
        out_shape=[jax.ShapeDtypeStruct((t, tiled[i][1]), dt) for i, dt in dt_list]
        + [jax.ShapeDtypeStruct(consts[i].shape, F32) for i in dc_list],
        compiler_params=_cparams(1),
    )(*[a for a, _, _ in tiled], *consts, *[a for a, _, _ in cots])
    return res[:len(dt_list)], res[len(dt_list):]


def mm(a, b, mode, *, tm, tn, name, out_dtypes=(F32,), epi=None, extras=()):
    m = a.shape[1] if mode == "tn" else a.shape[0]
    kd = a.shape[0] if mode == "tn" else a.shape[1]
    n = b.shape[0] if mode == "nt" else b.shape[1]
    tm, tn = min(tm, m), min(tn, n)
    if mode == "nn":
        a_spec = pl.BlockSpec((tm, kd), lambda i, j: (i, 0))
        b_spec = pl.BlockSpec((kd, tn), lambda i, j: (0, j))
    elif mode == "nt":
        a_spec = pl.BlockSpec((tm, kd), lambda i, j: (i, 0))
        b_spec = pl.BlockSpec((tn, kd), lambda i, j: (j, 0))
    else:
        a_spec = pl.BlockSpec((kd, tm), lambda i, j: (0, i))
        b_spec = pl.BlockSpec((kd, tn), lambda i, j: (0, j))
    n_e = len(extras)
    o_spec = pl.BlockSpec((tm, tn), lambda i, j: (i, j))

    def body(a_ref, b_ref, *refs):
        c = lax.dot_general(a_ref[...], b_ref[...], _MDIMS[mode], preferred_element_type=F32)
        res = epi(c, *[r[...] for r in refs[:n_e]]) if epi is not None else (c,)
        for o_ref, val in zip(refs[n_e:], res):
            o_ref[...] = val.astype(o_ref.dtype)

    res = pl.pallas_call(
        body,
        name=name,
        grid=(m // tm, n // tn),
        in_specs=[a_spec, b_spec] + [o_spec] * n_e,
        out_specs=[o_spec] * len(out_dtypes),
        out_shape=[jax.ShapeDtypeStruct((m, n), dt) for dt in out_dtypes],
        compiler_params=_cparams(2),
    )(a, b, *extras)
    return res if len(out_dtypes) > 1 else res[0]


P_WIDTH = 7680
RWKV_COL0 = 4096
RWKV_WIDTH = 3584
SHIFT_BLK = 512


def _shift_down(p, prev_row):
    rows = lax.broadcasted_iota(jnp.int32, p.shape, 0)
    return jnp.where(rows == 0, prev_row, pltpu.roll(p, 1, 0))


def shiftmix_fwd(p_all, sbp, *, tm):
    t = p_all.shape[0]
    c0 = RWKV_COL0 // SHIFT_BLK
    hb = tm // 8

    def body(p_ref, halo_ref, sb_ref, q_ref):
        p = p_ref[...]
        prev = jnp.where(pl.program_id(0) == 0, 0.0, halo_ref[7:8, :])
        q_ref[...] = p * sb_ref[0:1, :] + _shift_down(p, prev) * sb_ref[1:2, :]

    return pl.pallas_call(
        body,
        name="shiftmix_fwd",
        grid=(t // tm, RWKV_WIDTH // SHIFT_BLK),
        in_specs=[
            pl.BlockSpec((tm, SHIFT_BLK), lambda i, j: (i, c0 + j)),
            pl.BlockSpec((8, SHIFT_BLK), lambda i, j: (jnp.maximum(i * hb - 1, 0), c0 + j)),
            pl.BlockSpec((2, SHIFT_BLK), lambda i, j: (0, j)),
        ],
        out_specs=pl.BlockSpec((tm, SHIFT_BLK), lambda i, j: (i, j)),
        out_shape=jax.ShapeDtypeStruct((t, RWKV_WIDTH), F32),
        compiler_params=_cparams(2),
    )(p_all, p_all, sbp)


def shiftmix_bwd(dq, col0, p_all, sbp, *, tm, name):
    t, w = dq.shape
    n_i = t // tm
    hb = tm // 8
    cq = col0 // SHIFT_BLK
    cp = (RWKV_COL0 + col0) // SHIFT_BLK

    def body(dq_ref, dqn_ref, p_ref, ph_ref, sb_ref, dp_ref, dsb_ref):
        i = pl.program_id(1)
        dq_t = dq_ref[...]
        rows = lax.broadcasted_iota(jnp.int32, dq_t.shape, 0)
        nxt = jnp.where(i == n_i - 1, 0.0, dqn_ref[0:1, :])
        up = jnp.where(rows == tm - 1, nxt, pltpu.roll(dq_t, tm - 1, 0))
        dp_ref[...] = (dq_t * sb_ref[0:1, :] + up * sb_ref[1:2, :]).astype(dp_ref.dtype)
        p = p_ref[...]
        prev = jnp.where(i == 0, 0.0, ph_ref[7:8, :])
        s0 = jnp.sum(dq_t * p, axis=0, keepdims=True)
        s1 = jnp.sum(dq_t * _shift_down(p, prev), axis=0, keepdims=True)
        two = lax.broadcasted_iota(jnp.int32, (2, SHIFT_BLK), 0)

        @pl.when(i == 0)
        def _():
            dsb_ref[...] = jnp.zeros_like(dsb_ref)

        dsb_ref[...] += jnp.where(two == 0, s0, s1)

    return pl.pallas_call(
        body,
        name=name,
        grid=(w // SHIFT_BLK, n_i),
        in_specs=[
            pl.BlockSpec((tm, SHIFT_BLK), lambda j, i: (i, j)),
            pl.BlockSpec((8, SHIFT_BLK), lambda j, i: (jnp.minimum((i + 1) * hb, t // 8 - 1), j)),
            pl.BlockSpec((tm, SHIFT_BLK), lambda j, i: (i, cp + j)),
            pl.BlockSpec((8, SHIFT_BLK), lambda j, i: (jnp.maximum(i * hb - 1, 0), cp + j)),
            pl.BlockSpec((2, SHIFT_BLK), lambda j, i: (0, cq + j)),
        ],
        out_specs=[
            pl.BlockSpec((tm, SHIFT_BLK), lambda j, i: (i, j)),
            pl.BlockSpec((2, SHIFT_BLK), lambda j, i: (0, j)),
        ],
        out_shape=[jax.ShapeDtypeStruct((t, w), BF16), jax.ShapeDtypeStruct((2, w), F32)],
        compiler_params=_cparams(2),
    )(dq, dq, p_all, p_all, sbp)


def final_call(h1, m3, tgt, g_final, *, tm):
    t = h1.shape[0]

    def body(h1_ref, m3_ref, tgt_ref, g_ref, dh_ref, dhb_ref, dg_ref, loss_ref):
        loss, vjp = jax.vjp(f_final, h1_ref[...], m3_ref[...], tgt_ref[...], g_ref[...])
        dh, _, _, dg = vjp(jnp.ones((), F32))
        dh_ref[...] = dh
        dhb_ref[...] = dh.astype(BF16)

        @pl.when(pl.program_id(0) == 0)
        def _():
            dg_ref[...] = jnp.zeros_like(dg_ref)
            loss_ref[...] = jnp.zeros_like(loss_ref)

        dg_ref[...] += dg
        loss_ref[...] += jnp.full(loss_ref.shape, loss, F32)

    tile = _tile_spec(tm, D_MODEL, 0)
    return pl.pallas_call(
        body,
        name="final_loss",
        grid=(t // tm,),
        in_specs=[tile, tile, tile, _const_spec(g_final)],
        out_specs=[tile, tile, _const_spec(g_final), pl.BlockSpec((8, 128), lambda i: (0, 0))],
        out_shape=[jax.ShapeDtypeStruct((t, D_MODEL), F32), jax.ShapeDtypeStruct((t, D_MODEL), BF16),
                   jax.ShapeDtypeStruct(g_final.shape, F32), jax.ShapeDtypeStruct((8, 128), F32)],
        compiler_params=_cparams(1),
    )(h1, m3, tgt, g_final)


N_SGU = 2048
N_RWKV = 3360
LORA_W, LORA_A, LORA_G = 64, 64, 160


def _pad_rwkv_cols(z):
    zero = lambda n: jnp.zeros(z.shape[:-1] + (n,), z.dtype)
    return jnp.concatenate([z[..., :3072], z[..., 3072:3136], zero(64), z[..., 3136:3200], zero(64),
                            z[..., 3200:3360], zero(96)], axis=-1)


def _unpad_rwkv_cols(z):
    return jnp.concatenate([z[..., :3072], z[..., 3072:3136], z[..., 3200:3264], z[..., 3328:3488]], axis=-1)


def _pad_win(w):
    return jnp.concatenate([w[:, :N_SGU], w[:, N_SGU + N_RWKV:], _pad_rwkv_cols(w[:, N_SGU:N_SGU + N_RWKV])], axis=1)


def _unpad_win(w):
    return jnp.concatenate([w[:, :N_SGU], _unpad_rwkv_cols(w[:, RWKV_COL0:]), w[:, N_SGU:RWKV_COL0]], axis=1)


def _pad_rows(w, n):
    return jnp.concatenate([w, jnp.zeros((n - w.shape[0],) + w.shape[1:], w.dtype)], axis=0)


def _to_heads(z):
    return z.reshape(z.shape[0], N_HEADS, HEAD).transpose(1, 0, 2)


def _from_heads(z):
    return z.transpose(1, 0, 2).reshape(z.shape[1], D_MODEL)


def _relu2_epi(c):
    return c, jnp.square(jnp.maximum(c, 0.0))


def _relu2_bwd_epi(c, hid):
    return (c * (2.0 * jnp.maximum(hid, 0.0)),)


def _add_epi(c, x):
    return (c + x,)


def _pre_fwd(*args):
    res = f_pre(*args)
    return res[1], res[2], res[4], res[5], res[6]


def local_step(x, tgt, w):
    d = D_MODEL
    win_p = _pad_win(w["w_in"])
    sbp = _pad_rwkv_cols(w["shift_b"])
    wl = _pad_rows(w["w_lora_w"], 128)
    al = _pad_rows(w["a_lora_w"], 128)
    gl = _pad_rows(w["g_lora_w"], 256)
    sbt = w["sgu_b"].T

    (a_bf,) = ew_call(lambda x_, g_: (f_norm_in(x_, g_)[0],), [(x, d, 0)], [w["g_mix"]], [(d, BF16)], tm=256,
                      name="norm_in")
    p_all = mm(a_bf, win_p, "nn", tm=512, tn=1280, name="mm_in")
    sgu_t = [(p_all, 2 * d, 0)]
    sgu_c = [w["sgu_ln_w"], w["sgu_ln_b"], w["sgu_w"], sbt]
    (s_bf,) = ew_call(f_sgu, sgu_t, sgu_c, [(d, BF16)], tm=256, name="sgu_fwd")
    ya = mm(s_bf, w["w_proj_a"], "nn", tm=512, tn=1024, name="mm_proj_a")
    q = shiftmix_fwd(p_all, sbp, tm=256)
    pre_t = [(q, d, 0), (q, d, 1), (q, d, 2), (q, 512, 6)]
    pre_c = [wl, w["w0"], al, w["a0"], gl, w["k_k"], w["k_a"]]
    lw, kp, na, nb, g = ew_call(_pre_fwd, pre_t, pre_c, [(d, F32)] * 5, tm=256, name="rwkv_pre_fwd")
    r_hm, v_hm = _to_heads(q[:, :d]), _to_heads(q[:, 2 * d:3 * d])
    lw_hm, kp_hm, na_hm, nb_hm = _to_heads(lw), _to_heads(kp), _to_heads(na), _to_heads(nb)
    o_hm, s0s = scan_fwd(r_hm, lw_hm, kp_hm, v_hm, na_hm, nb_hm)
    o = _from_heads(o_hm)
    post_t = [(o, d, 0), (q, d, 0), (kp, d, 0), (q, d, 2), (g, d, 0)]
    post_c = [w["ln_x_w"], w["ln_x_b"], w["r_k"]]
    (ob_bf,) = ew_call(f_post, post_t, post_c, [(d, BF16)], tm=256, name="rwkv_post_fwd")
    yb = mm(ob_bf, w["w_proj_b"], "nn", tm=512, tn=1024, name="mm_proj_b")
    mix_t = [(ya, d, 0), (yb, d, 0), (p_all, d, 2), (p_all, d, 3)]
    (mixed_bf,) = ew_call(f_mix, mix_t, [], [(d, BF16)], tm=256, name="mix_fwd")
    h1 = mm(mixed_bf, w["w_out"], "nn", tm=512, tn=1024, name="mm_out", epi=_add_epi, extras=(x,))
    (f_bf,) = ew_call(lambda h_, g_: (f_ffn_in(h_, g_)[0],), [(h1, d, 0)], [w["g_ffn"]], [(d, BF16)], tm=256,
                      name="ffn_norm")
    hid, act_bf = mm(f_bf, w["w_ffn1"], "nn", tm=512, tn=1024, name="mm_ffn1", out_dtypes=(F32, BF16), epi=_relu2_epi)
    m3 = mm(act_bf, w["w_ffn2"], "nn", tm=512, tn=1024, name="mm_ffn2")
    dh2, dh2_bf, dg_final, loss = final_call(h1, m3, tgt, w["g_final"], tm=256)

    dhid_bf = mm(dh2_bf, w["w_ffn2"], "nt", tm=512, tn=1024, name="mm_dact", out_dtypes=(BF16,), epi=_relu2_bwd_epi,
                 extras=(hid,))
    d_ffn2 = mm(act_bf, dh2_bf, "tn", tm=512, tn=1024, name="mm_dw_ffn2")
    df = mm(dhid_bf, w["w_ffn1"], "nt", tm=512, tn=1024, name="mm_df")
    d_ffn1 = mm(f_bf, dhid_bf, "tn", tm=512, tn=1024, name="mm_dw_ffn1")
    (dh1, dh1_bf), (dg_ffn,) = ew_vjp_call(f_ffn_in, [(h1, d, 0)], [w["g_ffn"]], [(df, d, 0), (dh2, d, 0)],
                                           [(F32, BF16)], [True], tm=256, name="ffn_norm_bwd")
    dmixed = mm(dh1_bf, w["w_out"], "nt", tm=512, tn=1024, name="mm_dmixed")
    d_out = mm(mixed_bf, dh1_bf, "tn", tm=512, tn=1024, name="mm_dw_out")
    (dya_bf, dyb_bf, dga_bf, dgb_bf), _ = ew_vjp_call(f_mix, mix_t, [], [(dmixed, d, 0)], [(BF16,)] * 4, [], tm=256,
                                                      name="mix_bwd")
    dob = mm(dyb_bf, w["w_proj_b"], "nt", tm=512, tn=1024, name="mm_dob")
    d_proj_b = mm(ob_bf, dyb_bf, "tn", tm=512, tn=1024, name="mm_dw_proj_b")
    (do, dr_p, dkp_p, dv_p, dg), (dlnx_w, dlnx_b, dr_k) = ew_vjp_call(
        f_post, post_t, post_c, [(dob, d, 0)], [(F32,)] * 5, [True] * 3, tm=256, name="rwkv_post_bwd")
    scan_g = scan_bwd(r_hm, lw_hm, kp_hm, v_hm, na_hm, nb_hm, s0s, _to_heads(do))
    pre_g = [(_from_heads(z), d, 0) for z in scan_g] + [(dg, d, 0), (dr_p, d, 0), (dkp_p, d, 0), (dv_p, d, 0)]
    (dq_r, dq_k, dq_v, dq_l), (dwl, dw0, dal, da0, dgl, dk_k, dk_a) = ew_vjp_call(
        f_pre, pre_t, pre_c, pre_g, [(F32,)] * 4, [True] * 7, tm=128, name="rwkv_pre_bwd")
    dp_r, dsb_r = shiftmix_bwd(dq_r, 0, p_all, sbp, tm=256, name="shiftmix_bwd_r")
    dp_k, dsb_k = shiftmix_bwd(dq_k, d, p_all, sbp, tm=256, name="shiftmix_bwd_k")
    dp_v, dsb_v = shiftmix_bwd(dq_v, 2 * d, p_all, sbp, tm=256, name="shiftmix_bwd_v")
    dp_l, dsb_l = shiftmix_bwd(dq_l, 3 * d, p_all, sbp, tm=256, name="shiftmix_bwd_l")
    ds = mm(dya_bf, w["w_proj_a"], "nt", tm=512, tn=1024, name="mm_ds")
    d_proj_a = mm(s_bf, dya_bf, "tn", tm=512, tn=1024, name="mm_dw_proj_a")
    (dp_sgu,), (dln_w, dln_b, dsw, dsbt) = ew_vjp_call(f_sgu, sgu_t, sgu_c, [(ds, d, 0)], [(BF16,)], [True] * 4,
                                                       tm=256, name="sgu_bwd")
    dp_all = jnp.concatenate([dp_sgu, dga_bf, dgb_bf, dp_r, dp_k, dp_v, dp_l], axis=1)
    da = mm(dp_all, win_p, "nt", tm=512, tn=512, name="mm_da")
    d_in_p = mm(a_bf, dp_all, "tn", tm=512, tn=1280, name="mm_dw_in")
    (grad_x,), (dg_mix,) = ew_vjp_call(f_norm_in, [(x, d, 0)], [w["g_mix"]], [(da, d, 0), (dh1, d, 0)], [(F32,)],
                                       [True], tm=256, name="norm_in_bwd")

    grads = {
        "g_mix": dg_mix, "w_in": _unpad_win(d_in_p), "sgu_ln_w": dln_w, "sgu_ln_b": dln_b, "sgu_w": dsw,
        "sgu_b": dsbt.T, "w_proj_a": d_proj_a,
        "shift_b": _unpad_rwkv_cols(jnp.concatenate([dsb_r, dsb_k, dsb_v, dsb_l], axis=1)),
        "w_lora_w": dwl[:LORA_W], "w0": dw0, "a_lora_w": dal[:LORA_A], "a0": da0, "g_lora_w": dgl[:LORA_G],
        "k_k": dk_k, "k_a": dk_a, "r_k": dr_k, "ln_x_w": dlnx_w, "ln_x_b": dlnx_b, "w_proj_b": d_proj_b,
        "w_out": d_out, "g_ffn": dg_ffn, "w_ffn1": d_ffn1, "w_ffn2": d_ffn2, "g_final": dg_final,
    }
    return loss[0, 0], grad_x, grads


MESH = pl.DeviceIdType.MESH
N_CHIPS = 4
N_DEV = 8
PACK_ROWS = 4864
HALF_ROWS = PACK_ROWS // 2
SMALL_ROWS = 152
_ANY = pl.BlockSpec(memory_space=pl.ANY)


def _coords():
    return lax.axis_index("x"), lax.axis_index("y"), lax.axis_index("c")


def _other_chips(x, y):
    return [(1 - x, y), (x, 1 - y), (1 - x, 1 - y)]


def _remote(src, dst, send_sems, recv_sems, k, to):
    return pltpu.make_async_remote_copy(src_ref=src, dst_ref=dst, send_sem=send_sems.at[k], recv_sem=recv_sems.at[k],
                                        device_id=to, device_id_type=MESH)


def gather_shards(pack):
    def body(src_ref, out_ref, send_sems, recv_sems, local_sem):
        x, y, c = _coords()
        me = 2 * x + y
        sib = (x, y, 1 - c)
        chips = _other_chips(x, y)
        mine = pltpu.make_async_copy(src_ref, out_ref.at[me], local_sem)
        mine.start()
        first = [_remote(src_ref.at[c], out_ref.at[me, c], send_sems, recv_sems, k, (cx, cy, c))
                 for k, (cx, cy) in enumerate(chips)]
        for cp in first:
            cp.start()
        passed = []
        for k, (cx, cy) in enumerate(chips):
            j = 2 * cx + cy
            _remote(src_ref.at[c], out_ref.at[j, c], send_sems, recv_sems, k, (cx, cy, c)).wait_recv()
            fwd = _remote(out_ref.at[j, c], out_ref.at[j, c], send_sems, recv_sems, 3 + k, sib)
            fwd.start()
            passed.append(fwd)
        for k, (cx, cy) in enumerate(chips):
            j = 2 * cx + cy
            _remote(out_ref.at[j, 1 - c], out_ref.at[j, 1 - c], send_sems, recv_sems, 3 + k, sib).wait_recv()
        for cp in first + passed:
            cp.wait_send()
        mine.wait()

    return pl.pallas_call(
        body,
        name="gather_shards",
        in_specs=[_ANY],
        out_specs=_ANY,
        out_shape=jax.ShapeDtypeStruct((N_CHIPS,) + pack.shape, pack.dtype),
        scratch_shapes=[pltpu.SemaphoreType.DMA((6,)), pltpu.SemaphoreType.DMA((6,)), pltpu.SemaphoreType.DMA],
    )(pack)


def reduce_pair(g):
    def body(g_ref, mine_ref, got_ref, send_sems, recv_sems, local_sems):
        x, y, c = _coords()
        sib = (x, y, 1 - c)
        local = [pltpu.make_async_copy(g_ref.at[j, c], mine_ref.at[j], local_sems.at[j]) for j in range(N_CHIPS)]
        sends = [_remote(g_ref.at[j, 1 - c], got_ref.at[j], send_sems, recv_sems, j, sib) for j in range(N_CHIPS)]
        for cp in sends + local:
            cp.start()
        for cp in sends:
            cp.wait_recv()
        for cp in sends:
            cp.wait_send()
        for cp in local:
            cp.wait()

    half = jax.ShapeDtypeStruct((N_CHIPS,) + g.shape[2:], g.dtype)
    return pl.pallas_call(
        body,
        name="reduce_pair",
        in_specs=[_ANY],
        out_specs=[_ANY, _ANY],
        out_shape=[half, half],
        scratch_shapes=[pltpu.SemaphoreType.DMA((N_CHIPS,)), pltpu.SemaphoreType.DMA((N_CHIPS,)),
                        pltpu.SemaphoreType.DMA((N_CHIPS,))],
    )(g)


def reduce_chips(p):
    def body(p_ref, out_ref, send_sems, recv_sems, local_sem):
        x, y, c = _coords()
        me = 2 * x + y
        chips = _other_chips(x, y)
        mine = pltpu.make_async_copy(p_ref.at[me], out_ref.at[me], local_sem)
        mine.start()
        sends = [_remote(p_ref.at[2 * cx + cy], out_ref.at[me], send_sems, recv_sems, k, (cx, cy, c))
                 for k, (cx, cy) in enumerate(chips)]
        for cp in sends:
            cp.start()
        for k, (cx, cy) in enumerate(chips):
            _remote(p_ref.at[me], out_ref.at[2 * cx + cy], send_sems, recv_sems, k, (cx, cy, c)).wait_recv()
        for cp in sends:
            cp.wait_send()
        mine.wait()

    return pl.pallas_call(
        body,
        name="reduce_chips",
        in_specs=[_ANY],
        out_specs=_ANY,
        out_shape=jax.ShapeDtypeStruct(p.shape, p.dtype),
        scratch_shapes=[pltpu.SemaphoreType.DMA((3,)), pltpu.SemaphoreType.DMA((3,)), pltpu.SemaphoreType.DMA],
    )(p)


def exchange_halves(s):
    nq = 8
    rq = s.shape[0] // nq

    def body(s_ref, out_ref, sbuf, rbuf, send_sems, recv_sems, in_sems, out_sems, local_sem):
        x, y, c = _coords()
        sib = (x, y, 1 - c)
        rows = lambda q: pl.ds(q * rq, rq)
        mine = pltpu.make_async_copy(s_ref, out_ref.at[c], local_sem)
        mine.start()
        loads = [pltpu.make_async_copy(s_ref.at[rows(q)], sbuf.at[rows(q)], in_sems.at[q]) for q in range(nq)]
        for cp in loads:
            cp.start()
        sends = []
        for q in range(nq):
            loads[q].wait()
            sends.append(_remote(sbuf.at[rows(q)], rbuf.at[rows(q)], send_sems, recv_sems, q, sib))
            sends[q].start()
        stores = []
        for q in range(nq):
            sends[q].wait_recv()
            stores.append(pltpu.make_async_copy(rbuf.at[rows(q)], out_ref.at[1 - c, rows(q)], out_sems.at[q]))
            stores[q].start()
        for cp in sends:
            cp.wait_send()
        for cp in stores:
            cp.wait()
        mine.wait()

    return pl.pallas_call(
        body,
        name="exchange_halves",
        in_specs=[_ANY],
        out_specs=_ANY,
        out_shape=jax.ShapeDtypeStruct((2,) + s.shape, s.dtype),
        scratch_shapes=[pltpu.VMEM(s.shape, s.dtype), pltpu.VMEM(s.shape, s.dtype)]
        + [pltpu.SemaphoreType.DMA((nq,))] * 4 + [pltpu.SemaphoreType.DMA],
        compiler_params=pltpu.CompilerParams(vmem_limit_bytes=VMEM_LIMIT),
    )(s)


def gather_all(s):
    def body(s_ref, out_ref, send_sems, recv_sems, local_sem):
        x, y, c = _coords()
        me = 4 * x + 2 * y + c
        mine = pltpu.make_async_copy(s_ref, out_ref.at[me], local_sem)
        mine.start()
        peers = []
        for mask in range(1, N_DEV):
            px = 1 - x if mask & 4 else x
            py = 1 - y if mask & 2 else y
            pc = 1 - c if mask & 1 else c
            peers.append((px, py, pc))
        sends = [_remote(s_ref, out_ref.at[me], send_sems, recv_sems, k, peer) for k, peer in enumerate(peers)]
        for cp in sends:
            cp.start()
        for k, (px, py, pc) in enumerate(peers):
            _remote(s_ref, out_ref.at[4 * px + 2 * py + pc], send_sems, recv_sems, k, (px, py, pc)).wait_recv()
        for cp in sends:
            cp.wait_send()
        mine.wait()

    return pl.pallas_call(
        body,
        name="gather_all",
        in_specs=[_ANY],
        out_specs=_ANY,
        out_shape=jax.ShapeDtypeStruct((N_DEV,) + s.shape, s.dtype),
        scratch_shapes=[pltpu.SemaphoreType.DMA((N_DEV - 1,)), pltpu.SemaphoreType.DMA((N_DEV - 1,)),
                        pltpu.SemaphoreType.DMA],
    )(s)


def sum_slots(slots, *, tm, name):
    n, rows, width = slots.shape

    def body(*refs):
        acc = refs[0][0]
        for r in refs[1:n]:
            acc = acc + r[0]
        refs[n][...] = acc

    return pl.pallas_call(
        body,
        name=name,
        grid=(rows // tm,),
        in_specs=[pl.BlockSpec((1, tm, width), lambda i, s=s: (s, i, 0)) for s in range(n)],
        out_specs=pl.BlockSpec((tm, width), lambda i: (i, 0)),
        out_shape=jax.ShapeDtypeStruct((rows, width), slots.dtype),
        compiler_params=_cparams(1),
    )(*([slots] * n))


ADAM_LR = 0.001
ADAM_B1 = 0.9
ADAM_B2 = 0.999
ADAM_EPS = 1e-08
ADAM_WD = 0.01
ADAM_STEP = 10


def f_adamw(g, w, m, v):
    m = ADAM_B1 * m + (1.0 - ADAM_B1) * g
    v = ADAM_B2 * v + (1.0 - ADAM_B2) * jnp.square(g)
    m_hat = m / (1.0 - ADAM_B1 ** ADAM_STEP)
    v_hat = v / (1.0 - ADAM_B2 ** ADAM_STEP)
    delta = -ADAM_LR * (m_hat / (jnp.sqrt(v_hat) + ADAM_EPS) + ADAM_WD * w)
    return delta, m, v


def adamw_call(g, w, m, v, *, tm, name):
    width = g.shape[1]
    return ew_call(f_adamw, [(g, width, 0), (w, width, 0), (m, width, 0), (v, width, 0)], [], [(width, F32)] * 3,
                   tm=tm, name=name)


SHARDED = ["w_in", "w_proj_a", "w_lora_w", "a_lora_w", "g_lora_w", "w_proj_b", "w_out", "w_ffn1", "w_ffn2"]
SHARD_AXIS = {"w_in": 1, "w_proj_a": 0, "w_lora_w": 1, "a_lora_w": 1, "g_lora_w": 1, "w_proj_b": 0, "w_out": 0,
              "w_ffn1": 1, "w_ffn2": 0}
SHARD_SHAPE = {"w_in": (1024, 1864), "w_proj_a": (256, 1024), "w_lora_w": (64, 256), "a_lora_w": (64, 256),
               "g_lora_w": (160, 256), "w_proj_b": (256, 1024), "w_out": (256, 1024), "w_ffn1": (1024, 1024),
               "w_ffn2": (1024, 1024)}
SHIFT_SHARD = (2, 840)
VECTORS = ["g_mix", "sgu_ln_w", "sgu_ln_b", "w0", "a0", "k_k", "k_a", "r_k", "ln_x_w", "ln_x_b", "g_ffn", "g_final"]
SMALL = VECTORS + ["sgu_w", "sgu_b"]
SMALL_SHAPE = {**{n: (1, 1024) for n in VECTORS}, "sgu_w": (8, 128, 128), "sgu_b": (8, 128)}
WEIGHTS = ["g_mix", "w_in", "sgu_ln_w", "sgu_ln_b", "sgu_w", "sgu_b", "w_proj_a", "shift_b", "w_lora_w", "w0",
           "a_lora_w", "a0", "g_lora_w", "k_k", "k_a", "r_k", "ln_x_w", "ln_x_b", "w_proj_b", "w_out", "g_ffn",
           "w_ffn1", "w_ffn2", "g_final"]


def _size(shape):
    n = 1
    for s in shape:
        n *= s
    return n


def _pack_rows(parts, rows, dtype):
    flat = jnp.concatenate([p.reshape(-1).astype(dtype) for p in parts])
    return jnp.concatenate([flat, jnp.zeros((rows * 1024 - flat.shape[0],), dtype)]).reshape(rows, 1024)


def _unpack_rows(packed, shapes):
    flat = packed.reshape(-1)
    out, off = [], 0
    for shp in shapes:
        out.append(flat[off:off + _size(shp)].reshape(shp))
        off += _size(shp)
    return out


def _shard_of(name, full, j):
    ax = SHARD_AXIS[name]
    n = SHARD_SHAPE[name][ax]
    return lax.slice_in_dim(full, j * n, (j + 1) * n, axis=ax)


def kernel(x, g_mix, w_in, sgu_ln_w, sgu_ln_b, sgu_w, sgu_b, w_proj_a, shift_b, w_lora_w, w0, a_lora_w, a0, g_lora_w, k_k, k_a, r_k, ln_x_w, ln_x_b, w_proj_b, w_out, g_ffn, w_ffn1, w_ffn2, g_final, loss_target, m_g_mix, m_w_in, m_sgu_ln_w, m_sgu_ln_b, m_sgu_w, m_sgu_b, m_w_proj_a, m_shift_b, m_w_lora_w, m_w0, m_a_lora_w, m_a0, m_g_lora_w, m_k_k, m_k_a, m_r_k, m_ln_x_w, m_ln_x_b, m_w_proj_b, m_w_out, m_g_ffn, m_w_ffn1, m_w_ffn2, m_g_final, v_g_mix, v_w_in, v_sgu_ln_w, v_sgu_ln_b, v_sgu_w, v_sgu_b, v_w_proj_a, v_shift_b, v_w_lora_w, v_w0, v_a_lora_w, v_a0, v_g_lora_w, v_k_k, v_k_a, v_r_k, v_ln_x_w, v_ln_x_b, v_w_proj_b, v_w_out, v_g_ffn, v_w_ffn1, v_w_ffn2, v_g_final):
    given = dict(zip(WEIGHTS, (g_mix, w_in, sgu_ln_w, sgu_ln_b, sgu_w, sgu_b, w_proj_a, shift_b, w_lora_w, w0, a_lora_w, a0, g_lora_w, k_k, k_a, r_k, ln_x_w, ln_x_b, w_proj_b, w_out, g_ffn, w_ffn1, w_ffn2, g_final)))
    mom_m = dict(zip(WEIGHTS, (m_g_mix, m_w_in, m_sgu_ln_w, m_sgu_ln_b, m_sgu_w, m_sgu_b, m_w_proj_a, m_shift_b, m_w_lora_w, m_w0, m_a_lora_w, m_a0, m_g_lora_w, m_k_k, m_k_a, m_r_k, m_ln_x_w, m_ln_x_b, m_w_proj_b, m_w_out, m_g_ffn, m_w_ffn1, m_w_ffn2, m_g_final)))
    mom_v = dict(zip(WEIGHTS, (v_g_mix, v_w_in, v_sgu_ln_w, v_sgu_ln_b, v_sgu_w, v_sgu_b, v_w_proj_a, v_shift_b, v_w_lora_w, v_w0, v_a_lora_w, v_a0, v_g_lora_w, v_k_k, v_k_a, v_r_k, v_ln_x_w, v_ln_x_b, v_w_proj_b, v_w_out, v_g_ffn, v_w_ffn1, v_w_ffn2, v_g_final)))
    chip = 2 * lax.axis_index("x") + lax.axis_index("y")

    def local_block(tree, n):
        return tree[n] if n == "g_final" else tree[n][0]

    sb = local_block(given, "shift_b")
    exact = ["w_lora_w", "a_lora_w", "g_lora_w"]
    lo_part = lambda z: (z - z.astype(BF16).astype(F32)).astype(BF16)
    pack_w = _pack_rows([local_block(given, n) for n in SHARDED] + [sb]
                        + [lo_part(local_block(given, n)) for n in exact] + [lo_part(sb)], PACK_ROWS, BF16)
    gathered = gather_shards(pack_w.reshape(2, HALF_ROWS, 1024)).reshape(N_CHIPS, PACK_ROWS, 1024)
    shapes = [SHARD_SHAPE[n] for n in SHARDED] + [SHIFT_SHARD] + [SHARD_SHAPE[n] for n in exact] + [SHIFT_SHARD]
    per_chip = [_unpack_rows(gathered[j], shapes) for j in range(N_CHIPS)]
    n_sh = len(SHARDED)
    w = {}
    for i, n in enumerate(SHARDED):
        w[n] = jnp.concatenate([per_chip[j][i] for j in range(N_CHIPS)], axis=SHARD_AXIS[n])
    for i, n in enumerate(exact):
        lo = jnp.concatenate([per_chip[j][n_sh + 1 + i] for j in range(N_CHIPS)], axis=1)
        w[n] = w[n].astype(F32) + lo.astype(F32)
    w["shift_b"] = jnp.concatenate(
        [per_chip[j][n_sh].astype(F32) + per_chip[j][-1].astype(F32) for j in range(N_CHIPS)], axis=1)
    for n in SMALL:
        w[n] = local_block(given, n).reshape(SMALL_SHAPE[n])

    loss, grad_x, grads = local_step(x[0], loss_target[0], w)
    loss = lax.psum(loss, ("x", "y", "c"))

    g_pack = jnp.stack([_pack_rows([_shard_of(n, grads[n], j) for n in SHARDED], PACK_ROWS, F32)
                        for j in range(N_CHIPS)])
    own, got = reduce_pair(g_pack.reshape(N_CHIPS, 2, HALF_ROWS, 1024))
    rows4 = N_CHIPS * HALF_ROWS
    (pair_sum,) = ew_call(lambda a_, b_: (a_ + b_,),
                          [(own.reshape(rows4, 1024), 1024, 0), (got.reshape(rows4, 1024), 1024, 0)], [],
                          [(1024, F32)], tm=256, name="pair_sum")
    slots = reduce_chips(pair_sum.reshape(N_CHIPS, HALF_ROWS, 1024))
    half_sum = sum_slots(slots, tm=128, name="chip_sum")
    g_big = exchange_halves(half_sum).reshape(PACK_ROWS, 1024)
    w_big = _pack_rows([local_block(given, n) for n in SHARDED], PACK_ROWS, F32)
    m_big = _pack_rows([local_block(mom_m, n) for n in SHARDED], PACK_ROWS, F32)
    v_big = _pack_rows([local_block(mom_v, n) for n in SHARDED], PACK_ROWS, F32)
    d_big, nm_big, nv_big = adamw_call(g_big, w_big, m_big, v_big, tm=256, name="adamw_sharded")
    big_shapes = [SHARD_SHAPE[n] for n in SHARDED]
    out_g = dict(zip(SHARDED, _unpack_rows(g_big, big_shapes)))
    out_d = dict(zip(SHARDED, _unpack_rows(d_big, big_shapes)))
    out_m = dict(zip(SHARDED, _unpack_rows(nm_big, big_shapes)))
    out_v = dict(zip(SHARDED, _unpack_rows(nv_big, big_shapes)))

    small_shapes = [SMALL_SHAPE[n] for n in SMALL]
    s_pack = _pack_rows([grads[n] for n in SMALL] + [grads["shift_b"]], SMALL_ROWS, F32)
    g_small = sum_slots(gather_all(s_pack), tm=SMALL_ROWS, name="small_sum")
    w_small = _pack_rows([local_block(given, n) for n in SMALL], SMALL_ROWS, F32)
    m_small = _pack_rows([local_block(mom_m, n) for n in SMALL], SMALL_ROWS, F32)
    v_small = _pack_rows([local_block(mom_v, n) for n in SMALL], SMALL_ROWS, F32)
    d_small, nm_small, nv_small = adamw_call(g_small, w_small, m_small, v_small, tm=SMALL_ROWS, name="adamw_small")
    g_parts = _unpack_rows(g_small, small_shapes + [(2, N_RWKV)])
    out_g.update(zip(SMALL, g_parts[:-1]))
    out_d.update(zip(SMALL, _unpack_rows(d_small, small_shapes)))
    out_m.update(zip(SMALL, _unpack_rows(nm_small, small_shapes)))
    out_v.update(zip(SMALL, _unpack_rows(nv_small, small_shapes)))
    g_sb = lax.dynamic_slice_in_dim(g_parts[-1], chip * SHIFT_SHARD[1], SHIFT_SHARD[1], axis=1)
    sb_args = [_pack_rows([z], 8, F32) for z in (g_sb, sb, local_block(mom_m, "shift_b"), local_block(mom_v, "shift_b"))]
    sb_res = adamw_call(*sb_args, tm=8, name="adamw_shift_b")
    out_g["shift_b"] = g_sb
    for tree, res in zip((out_d, out_m, out_v), sb_res):
        tree["shift_b"] = _unpack_rows(res, [SHIFT_SHARD])[0]

    def block_of(tree, n):
        return tree[n].reshape(given[n].shape)

    return (loss, grad_x[None], *[block_of(out_g, n) for n in WEIGHTS], *[block_of(out_d, n) for n in WEIGHTS],
            *[block_of(out_m, n) for n in WEIGHTS], *[block_of(out_v, n) for n in WEIGHTS])
```

```python
import functools

import jax
import jax.numpy as jnp
from jax import lax
from jax.experimental import pallas as pl
from jax.experimental.pallas import tpu as pltpu

F32 = jnp.float32
BF16 = jnp.bfloat16

D_MODEL = 1024
N_HEADS = 16
HEAD = 64
SCAN_CHUNK = 64

VMEM_LIMIT = 56 * 1024 * 1024


_BDIMS = {
    "nn": (((2,), (1,)), ((0,), (0,))),
    "nt": (((2,), (2,)), ((0,), (0,))),
    "tn": (((1,), (1,)), ((0,), (0,))),
}


def _raw_bdot(x, y, mode):
    return lax.dot_general(x, y, _BDIMS[mode], precision=lax.Precision.HIGH, preferred_element_type=F32)


@functools.partial(jax.custom_vjp, nondiff_argnums=(2,))
def bdot(x, y, mode):
    return _raw_bdot(x, y, mode)


def _bdot_fwd(x, y, mode):
    return _raw_bdot(x, y, mode), (x, y)


def _bdot_bwd(mode, res, g):
    x, y = res
    if mode == "nn":
        return bdot(g, y, "nt"), bdot(x, g, "tn")
    if mode == "nt":
        return bdot(g, y, "nn"), bdot(g, x, "tn")
    return bdot(y, g, "nt"), bdot(x, g, "nn")


bdot.defvjp(_bdot_fwd, _bdot_bwd)


def _scan_chunk(S0, r, lw, k, v, a, b):
    nh, lc, _ = r.shape
    ti = lax.broadcasted_iota(jnp.int32, (lc, lc), 0)
    si = lax.broadcasted_iota(jnp.int32, (lc, lc), 1)
    incl = (si <= ti).astype(F32)
    strict = (si < ti).astype(F32)
    eye = (si == ti).astype(F32)
    cl = bdot(jnp.broadcast_to(incl, (nh, lc, lc)), lw, "nn")
    cl_last = cl[:, lc - 1:lc, :]
    g_last = jnp.exp(cl_last - cl)
    at = a * jnp.exp(cl - lw)
    bt = b * jnp.exp(-cl)
    kt = k * jnp.exp(-cl)
    rt = r * jnp.exp(cl)
    m_ab = bdot(at, bt, "nt") * strict
    m_ak = bdot(at, kt, "nt") * strict
    m_rb = bdot(rt, bt, "nt") * incl
    m_rk = bdot(rt, kt, "nt") * incl
    x = eye + m_ab
    p = m_ab
    n = 1
    while n * 2 < lc:
        p = bdot(p, p, "nn")
        x = x + bdot(x, p, "nn")
        n *= 2
    u = bdot(x, bdot(at, S0, "nt") + bdot(m_ak, v, "nn"), "nn")
    o = bdot(rt, S0, "nt") + bdot(m_rb, u, "nn") + bdot(m_rk, v, "nn")
    s_last = S0 * jnp.exp(cl_last) + bdot(u, b * g_last, "tn") + bdot(v, k * g_last, "tn")
    return o, s_last


def _scan_specs(t):
    nc = t // SCAN_CHUNK
    blk = lambda rev: pl.BlockSpec(
        (N_HEADS, SCAN_CHUNK, HEAD), (lambda c: (0, nc - 1 - c, 0)) if rev else (lambda c: (0, c, 0)))
    st = lambda rev: pl.BlockSpec(
        (1, N_HEADS, HEAD, HEAD), (lambda c: (nc - 1 - c, 0, 0, 0)) if rev else (lambda c: (c, 0, 0, 0)))
    return nc, blk, st


def scan_fwd(r, lw, k, v, a, b):
    t = r.shape[1]
    nc, blk, st = _scan_specs(t)

    def body(r_ref, lw_ref, k_ref, v_ref, a_ref, b_ref, o_ref, s0_ref, s_scr):
        @pl.when(pl.program_id(0) == 0)
        def _():
            s_scr[...] = jnp.zeros_like(s_scr)

        s0 = s_scr[...]
        s0_ref[0] = s0
        o, s_last = _scan_chunk(s0, r_ref[...], lw_ref[...], k_ref[...], v_ref[...], a_ref[...], b_ref[...])
        o_ref[...] = o
        s_scr[...] = s_last

    return pl.pallas_call(
        body,
        name="scan_fwd",
        grid=(nc,),
        in_specs=[blk(False)] * 6,
        out_specs=[blk(False), st(False)],
        out_shape=[jax.ShapeDtypeStruct(r.shape, F32), jax.ShapeDtypeStruct((nc, N_HEADS, HEAD, HEAD), F32)],
        scratch_shapes=[pltpu.VMEM((N_HEADS, HEAD, HEAD), F32)],
        compiler_params=pltpu.CompilerParams(dimension_semantics=("arbitrary",), vmem_limit_bytes=VMEM_LIMIT),
    )(r, lw, k, v, a, b)


def scan_bwd(r, lw, k, v, a, b, s0s, do):
    t = r.shape[1]
    nc, blk, st = _scan_specs(t)

    def body(r_ref, lw_ref, k_ref, v_ref, a_ref, b_ref, s0_ref, do_ref,
             dr_ref, dlw_ref, dk_ref, dv_ref, da_ref, db_ref, ds_scr):
        @pl.when(pl.program_id(0) == 0)
        def _():
            ds_scr[...] = jnp.zeros_like(ds_scr)

        _, vjp = jax.vjp(_scan_chunk, s0_ref[0], r_ref[...], lw_ref[...], k_ref[...], v_ref[...], a_ref[...],
                         b_ref[...])
        ds0, dr, dlw, dk, dv, da, db = vjp((do_ref[...], ds_scr[...]))
        dr_ref[...] = dr
        dlw_ref[...] = dlw
        dk_ref[...] = dk
        dv_ref[...] = dv
        da_ref[...] = da
        db_ref[...] = db
        ds_scr[...] = ds0

    return pl.pallas_call(
        body,
        name="scan_bwd",
        grid=(nc,),
        in_specs=[blk(True)] * 6 + [st(True), blk(True)],
        out_specs=[blk(True)] * 6,
        out_shape=[jax.ShapeDtypeStruct(r.shape, F32)] * 6,
        scratch_shapes=[pltpu.VMEM((N_HEADS, HEAD, HEAD), F32)],
        compiler_params=pltpu.CompilerParams(dimension_semantics=("arbitrary",), vmem_limit_bytes=VMEM_LIMIT),
    )(r, lw, k, v, a, b, s0s, do)


_MDIMS = {
    "nn": (((1,), (0,)), ((), ())),
    "nt": (((1,), (1,)), ((), ())),
    "tn": (((0,), (0,)), ((), ())),
}


def _raw_mdot(x, y, mode, exact):
    if exact:
        return lax.dot_general(x, y, _MDIMS[mode], precision=lax.Precision.HIGHEST, preferred_element_type=F32)
    return lax.dot_general(x.astype(BF16), y.astype(BF16), _MDIMS[mode], preferred_element_type=F32)


@functools.partial(jax.custom_vjp, nondiff_argnums=(2, 3))
def mdot(x, y, mode, exact):
    return _raw_mdot(x, y, mode, exact)


def _mdot_fwd(x, y, mode, exact):
    return _raw_mdot(x, y, mode, exact), (x, y)


def _mdot_bwd(mode, exact, res, g):
    x, y = res
    if mode == "nn":
        return mdot(g, y, "nt", exact), mdot(x, g, "tn", exact)
    if mode == "nt":
        return mdot(g, y, "nn", exact), mdot(g, x, "tn", exact)
    return mdot(y, g, "nt", exact), mdot(x, g, "nn", exact)


mdot.defvjp(_mdot_fwd, _mdot_bwd)


def _seg_ones():
    i = lax.broadcasted_iota(jnp.int32, (256, 256), 0) // HEAD
    j = lax.broadcasted_iota(jnp.int32, (256, 256), 1) // HEAD
    return (i == j).astype(BF16)


@jax.custom_vjp
def segsum(x):
    bd = _seg_ones()
    hi = x.astype(BF16)
    lo = (x - hi.astype(F32)).astype(BF16)
    cols = []
    for j in range(x.shape[1] // 256):
        sl = slice(256 * j, 256 * (j + 1))
        cols.append(jnp.dot(hi[:, sl], bd, preferred_element_type=F32)
                    + jnp.dot(lo[:, sl], bd, preferred_element_type=F32))
    return jnp.concatenate(cols, axis=1)


segsum.defvjp(lambda x: (segsum(x), None), lambda _, g: (segsum(g),))


NORM_EPS = 1e-6
LN_EPS = 1e-5
GN_EPS = 64e-5
SGU_CHUNK = 128
SGU_GROUPS = 8


def _rms(x, g):
    return x * lax.rsqrt(jnp.mean(x * x, axis=-1, keepdims=True) + NORM_EPS) * g


def f_norm_in(x, g):
    return _rms(x, g), x


def f_sgu(p, ln_w, ln_b, sw, sbt):
    tm = p.shape[0]
    z = 0.5 * p * (1.0 + lax.erf(p * 0.7071067811865476))
    u, v = z[:, :D_MODEL], z[:, D_MODEL:]
    mu = jnp.mean(v, axis=-1, keepdims=True)
    d = v - mu
    vn = d * lax.rsqrt(jnp.mean(d * d, axis=-1, keepdims=True) + LN_EPS) * ln_w + ln_b
    ii = lax.broadcasted_iota(jnp.int32, (SGU_CHUNK, SGU_CHUNK), 0)
    jj = lax.broadcasted_iota(jnp.int32, (SGU_CHUNK, SGU_CHUNK), 1)
    mask = (jj <= ii).astype(F32)
    gi = lax.broadcasted_iota(jnp.int32, (SGU_GROUPS, D_MODEL), 0)
    ci = lax.broadcasted_iota(jnp.int32, (SGU_GROUPS, D_MODEL), 1) // SGU_CHUNK
    bias = mdot(sbt, (gi == ci).astype(F32), "nn", True)
    rows = []
    for c in range(tm // SGU_CHUNK):
        cols = []
        for g in range(SGU_GROUPS):
            blk = vn[c * SGU_CHUNK:(c + 1) * SGU_CHUNK, g * SGU_CHUNK:(g + 1) * SGU_CHUNK]
            cols.append(mdot(sw[g] * mask, blk, "nn", False))
        rows.append(jnp.concatenate(cols, axis=1) + bias)
    return (u * jnp.concatenate(rows, axis=0),)


def _softplus(x):
    return jnp.maximum(x, 0.0) + jnp.log1p(jnp.exp(-jnp.abs(x)))


def f_pre(qr, qk, qv, ql, wl, w0, al, a0, gl, k_k, k_a):
    xw, xa, xg = ql[:, :128], ql[:, 128:256], ql[:, 256:512]
    wr = -_softplus(-(w0 + mdot(jnp.tanh(xw), wl, "nn", True))) - 0.5
    lw = -jnp.exp(wr)
    aa = jax.nn.sigmoid(a0 + mdot(xa, al, "nn", True))
    g = mdot(jax.nn.sigmoid(xg), gl, "nn", True)
    kkr = qk * k_k
    kk = kkr / jnp.maximum(jnp.sqrt(segsum(kkr * kkr)), 1e-12)
    kp = qk * (1.0 + (aa - 1.0) * k_a)
    return qr, lw, kp, qv, -kk, kk * aa, g, qr, kp, qv


def f_post(o, r, kp, v, g, lnw, lnb, rk):
    mu = segsum(o) * (1.0 / HEAD)
    d = o - mu
    gn = d * lax.rsqrt(segsum(d * d) * (1.0 / HEAD) + GN_EPS)
    return ((gn * lnw + lnb + segsum(r * kp * rk) * v) * g,)


def f_mix(ya, yb, ga, gb):
    return (jax.nn.sigmoid(ga) * ya + jax.nn.sigmoid(gb) * yb,)


def f_ffn_in(h1, g):
    return _rms(h1, g), h1


def f_final(h1, m3, tgt, g):
    y = _rms(h1 + m3, g)
    err = jnp.square(y - tgt)
    return 0.5 * jnp.sum(jnp.mean(err, axis=-1))


def _cparams(n_grid):
    return pltpu.CompilerParams(dimension_semantics=("arbitrary",) * n_grid, vmem_limit_bytes=VMEM_LIMIT)


def _tile_spec(tm, w, cb):
    return pl.BlockSpec((tm, w), lambda i: (i, cb))


def _const_spec(c):
    nd = c.ndim
    return pl.BlockSpec(c.shape, lambda i: (0,) * nd)


def ew_call(fn, tiled, consts, outs, *, tm, name):
    t = tiled[0][0].shape[0]
    n_t, n_c = len(tiled), len(consts)

    def body(*refs):
        tv = [r[...].astype(F32) for r in refs[:n_t]]
        cv = [r[...] for r in refs[n_t:n_t + n_c]]
        res = fn(*tv, *cv)
        for o_ref, val in zip(refs[n_t + n_c:], res):
            o_ref[...] = val.astype(o_ref.dtype)

    return pl.pallas_call(
        body,
        name=name,
        grid=(t // tm,),
        in_specs=[_tile_spec(tm, w, cb) for _, w, cb in tiled] + [_const_spec(c) for c in consts],
        out_specs=[_tile_spec(tm, w, 0) for w, _ in outs],
        out_shape=[jax.ShapeDtypeStruct((t, w), dt) for w, dt in outs],
        compiler_params=_cparams(1),
    )(*[a for a, _, _ in tiled], *consts)


def ew_vjp_call(fn, tiled, consts, cots, d_tiled, d_consts, *, tm, name):
    t = tiled[0][0].shape[0]
    n_t, n_c, n_g = len(tiled), len(consts), len(cots)
    dt_list = [(i, dt) for i, dts in enumerate(d_tiled) for dt in dts]
    dc_list = [i for i, want in enumerate(d_consts) if want]

    def body(*refs):
        tv = [r[...].astype(F32) for r in refs[:n_t]]
        cv = [r[...] for r in refs[n_t:n_t + n_c]]
        gv = tuple(r[...].astype(F32) for r in refs[n_t + n_c:n_t + n_c + n_g])
        out_refs = refs[n_t + n_c + n_g:]
        _, vjp = jax.vjp(fn, *tv, *cv)
        grads = vjp(gv)
        for o_ref, (i, _) in zip(out_refs, dt_list):
            o_ref[...] = grads[i].astype(o_ref.dtype)
        acc_refs = out_refs[len(dt_list):]

        @pl.when(pl.program_id(0) == 0)
        def _():
            for a_ref in acc_refs:
                a_ref[...] = jnp.zeros_like(a_ref)

        for a_ref, i in zip(acc_refs, dc_list):
            a_ref[...] += grads[n_t + i]

    res = pl.pallas_call(
        body,
        name=name,
        grid=(t // tm,),
        in_specs=[_tile_spec(tm, w, cb) for _, w, cb in tiled] + [_const_spec(c) for c in consts]
        + [_tile_spec(tm, w, cb) for _, w, cb in cots],
        out_specs=[_tile_spec(tm, tiled[i][1], 0) for i, _ in dt_list] + [_const_spec(consts[i]) for i in dc_list],
        out_shape=[jax.ShapeDtypeStruct((t, tiled[i][1]), dt) for i, dt in dt_list]
        + [jax.ShapeDtypeStruct(consts[i].shape, F32) for i in dc_list],
        compiler_params=_cparams(1),
    )(*[a for a, _, _ in tiled], *consts, *[a for a, _, _ in cots])
    return res[:len(dt_list)], res[len(dt_list):]


def mm(a, b, mode, *, tm, tn, name, out_dtypes=(F32,), epi=None, extras=()):
    m = a.shape[1] if mode == "tn" else a.shape[0]
    kd = a.shape[0] if mode == "tn" else a.shape[1]
    n = b.shape[0] if mode == "nt" else b.shape[1]
    tm, tn = min(tm, m), min(tn, n)
    if mode == "nn":
        a_spec = pl.BlockSpec((tm, kd), lambda i, j: (i, 0))
        b_spec = pl.BlockSpec((kd, tn), lambda i, j: (0, j))
    elif mode == "nt":
        a_spec = pl.BlockSpec((tm, kd), lambda i, j: (i, 0))
        b_spec = pl.BlockSpec((tn, kd), lambda i, j: (j, 0))
    else:
        a_spec = pl.BlockSpec((kd, tm), lambda i, j: (0, i))
        b_spec = pl.BlockSpec((kd, tn), lambda i, j: (0, j))
    n_e = len(extras)
    o_spec = pl.BlockSpec((tm, tn), lambda i, j: (i, j))

    def body(a_ref, b_ref, *refs):
        c = lax.dot_general(a_ref[...], b_ref[...], _MDIMS[mode], preferred_element_type=F32)
        res = epi(c, *[r[...] for r in refs[:n_e]]) if epi is not None else (c,)
        for o_ref, val in zip(refs[n_e:], res):
            o_ref[...] = val.astype(o_ref.dtype)

    res = pl.pallas_call(
        body,
        name=name,
        grid=(m // tm, n // tn),
        in_specs=[a_spec, b_spec] + [o_spec] * n_e,
        out_specs=[o_spec] * len(out_dtypes),
        out_shape=[jax.ShapeDtypeStruct((m, n), dt) for dt in out_dtypes],
        compiler_params=_cparams(2),
    )(a, b, *extras)
    return res if len(out_dtypes) > 1 else res[0]


P_WIDTH = 7680
RWKV_COL0 = 4096
RWKV_WIDTH = 3584
SHIFT_BLK = 512


def _shift_down(p, prev_row):
    rows = lax.broadcasted_iota(jnp.int32, p.shape, 0)
    return jnp.where(rows == 0, prev_row, pltpu.roll(p, 1, 0))


def shiftmix_fwd(p_all, sbp, *, tm):
    t = p_all.shape[0]
    c0 = RWKV_COL0 // SHIFT_BLK
    hb = tm // 8

    def body(p_ref, halo_ref, sb_ref, q_ref):
        p = p_ref[...]
        prev = jnp.where(pl.program_id(0) == 0, 0.0, halo_ref[7:8, :])
        q_ref[...] = p * sb_ref[0:1, :] + _shift_down(p, prev) * sb_ref[1:2, :]

    return pl.pallas_call(
        body,
        name="shiftmix_fwd",
        grid=(t // tm, RWKV_WIDTH // SHIFT_BLK),
        in_specs=[
            pl.BlockSpec((tm, SHIFT_BLK), lambda i, j: (i, c0 + j)),
            pl.BlockSpec((8, SHIFT_BLK), lambda i, j: (jnp.maximum(i * hb - 1, 0), c0 + j)),
            pl.BlockSpec((2, SHIFT_BLK), lambda i, j: (0, j)),
        ],
        out_specs=pl.BlockSpec((tm, SHIFT_BLK), lambda i, j: (i, j)),
        out_shape=jax.ShapeDtypeStruct((t, RWKV_WIDTH), F32),
        compiler_params=_cparams(2),
    )(p_all, p_all, sbp)


def shiftmix_bwd(dq, col0, p_all, sbp, *, tm, name):
    t, w = dq.shape
    n_i = t // tm
    hb = tm // 8
    cq = col0 // SHIFT_BLK
    cp = (RWKV_COL0 + col0) // SHIFT_BLK

    def body(dq_ref, dqn_ref, p_ref, ph_ref, sb_ref, dp_ref, dsb_ref):
        i = pl.program_id(1)
        dq_t = dq_ref[...]
        rows = lax.broadcasted_iota(jnp.int32, dq_t.shape, 0)
        nxt = jnp.where(i == n_i - 1, 0.0, dqn_ref[0:1, :])
        up = jnp.where(rows == tm - 1, nxt, pltpu.roll(dq_t, tm - 1, 0))
        dp_ref[...] = (dq_t * sb_ref[0:1, :] + up * sb_ref[1:2, :]).astype(dp_ref.dtype)
        p = p_ref[...]
        prev = jnp.where(i == 0, 0.0, ph_ref[7:8, :])
        s0 = jnp.sum(dq_t * p, axis=0, keepdims=True)
        s1 = jnp.sum(dq_t * _shift_down(p, prev), axis=0, keepdims=True)
        two = lax.broadcasted_iota(jnp.int32, (2, SHIFT_BLK), 0)

        @pl.when(i == 0)
        def _():
            dsb_ref[...] = jnp.zeros_like(dsb_ref)

        dsb_ref[...] += jnp.where(two == 0, s0, s1)

    return pl.pallas_call(
        body,
        name=name,
        grid=(w // SHIFT_BLK, n_i),
        in_specs=[
            pl.BlockSpec((tm, SHIFT_BLK), lambda j, i: (i, j)),
            pl.BlockSpec((8, SHIFT_BLK), lambda j, i: (jnp.minimum((i + 1) * hb, t // 8 - 1), j)),
            pl.BlockSpec((tm, SHIFT_BLK), lambda j, i: (i, cp + j)),
            pl.BlockSpec((8, SHIFT_BLK), lambda j, i: (jnp.maximum(i * hb - 1, 0), cp + j)),
            pl.BlockSpec((2, SHIFT_BLK), lambda j, i: (0, cq + j)),
        ],
        out_specs=[
            pl.BlockSpec((tm, SHIFT_BLK), lambda j, i: (i, j)),
            pl.BlockSpec((2, SHIFT_BLK), lambda j, i: (0, j)),
        ],
        out_shape=[jax.ShapeDtypeStruct((t, w), BF16), jax.ShapeDtypeStruct((2, w), F32)],
        compiler_params=_cparams(2),
    )(dq, dq, p_all, p_all, sbp)


def final_call(h1, m3, tgt, g_final, *, tm):
    t = h1.shape[0]

    def body(h1_ref, m3_ref, tgt_ref, g_ref, dh_ref, dhb_ref, dg_ref, loss_ref):
        loss, vjp = jax.vjp(f_final, h1_ref[...], m3_ref[...], tgt_ref[...], g_ref[...])
        dh, _, _, dg = vjp(jnp.ones((), F32))
        dh_ref[...] = dh
        dhb_ref[...] = dh.astype(BF16)

        @pl.when(pl.program_id(0) == 0)
        def _():
            dg_ref[...] = jnp.zeros_like(dg_ref)
            loss_ref[...] = jnp.zeros_like(loss_ref)

        dg_ref[...] += dg
        loss_ref[...] += jnp.full(loss_ref.shape, loss, F32)

    tile = _tile_spec(tm, D_MODEL, 0)
    return pl.pallas_call(
        body,
        name="final_loss",
        grid=(t // tm,),
        in_specs=[tile, tile, tile, _const_spec(g_final)],
        out_specs=[tile, tile, _const_spec(g_final), pl.BlockSpec((8, 128), lambda i: (0, 0))],
        out_shape=[jax.ShapeDtypeStruct((t, D_MODEL), F32), jax.ShapeDtypeStruct((t, D_MODEL), BF16),
                   jax.ShapeDtypeStruct(g_final.shape, F32), jax.ShapeDtypeStruct((8, 128), F32)],
        compiler_params=_cparams(1),
    )(h1, m3, tgt, g_final)


N_SGU = 2048
N_RWKV = 3360
LORA_W, LORA_A, LORA_G = 64, 64, 160


def _pad_rwkv_cols(z):
    zero = lambda n: jnp.zeros(z.shape[:-1] + (n,), z.dtype)
    return jnp.concatenate([z[..., :3072], z[..., 3072:3136], zero(64), z[..., 3136:3200], zero(64),
                            z[..., 3200:3360], zero(96)], axis=-1)


def _unpad_rwkv_cols(z):
    return jnp.concatenate([z[..., :3072], z[..., 3072:3136], z[..., 3200:3264], z[..., 3328:3488]], axis=-1)


def _pad_win(w):
    return jnp.concatenate([w[:, :N_SGU], w[:, N_SGU + N_RWKV:], _pad_rwkv_cols(w[:, N_SGU:N_SGU + N_RWKV])], axis=1)


def _unpad_win(w):
    return jnp.concatenate([w[:, :N_SGU], _unpad_rwkv_cols(w[:, RWKV_COL0:]), w[:, N_SGU:RWKV_COL0]], axis=1)


def _pad_rows(w, n):
    return jnp.concatenate([w, jnp.zeros((n - w.shape[0],) + w.shape[1:], w.dtype)], axis=0)


def _to_heads(z):
    return z.reshape(z.shape[0], N_HEADS, HEAD).transpose(1, 0, 2)


def _from_heads(z):
    return z.transpose(1, 0, 2).reshape(z.shape[1], D_MODEL)


def _relu2_epi(c):
    return c, jnp.square(jnp.maximum(c, 0.0))


def _relu2_bwd_epi(c, hid):
    return (c * (2.0 * jnp.maximum(hid, 0.0)),)


def _add_epi(c, x):
    return (c + x,)


def _pre_fwd(*args):
    res = f_pre(*args)
    return res[1], res[2], res[4], res[5], res[6]


def local_step(x, tgt, w):
    d = D_MODEL
    win_p = _pad_win(w["w_in"])
    sbp = _pad_rwkv_cols(w["shift_b"])
    wl = _pad_rows(w["w_lora_w"], 128)
    al = _pad_rows(w["a_lora_w"], 128)
    gl = _pad_rows(w["g_lora_w"], 256)
    sbt = w["sgu_b"].T

    (a_bf,) = ew_call(lambda x_, g_: (f_norm_in(x_, g_)[0],), [(x, d, 0)], [w["g_mix"]], [(d, BF16)], tm=256,
                      name="norm_in")
    p_all = mm(a_bf, win_p, "nn", tm=512, tn=1280, name="mm_in")
    sgu_t = [(p_all, 2 * d, 0)]
    sgu_c = [w["sgu_ln_w"], w["sgu_ln_b"], w["sgu_w"], sbt]
    (s_bf,) = ew_call(f_sgu, sgu_t, sgu_c, [(d, BF16)], tm=256, name="sgu_fwd")
    ya = mm(s_bf, w["w_proj_a"], "nn", tm=512, tn=1024, name="mm_proj_a")
    q = shiftmix_fwd(p_all, sbp, tm=256)
    pre_t = [(q, d, 0), (q, d, 1), (q, d, 2), (q, 512, 6)]
    pre_c = [wl, w["w0"], al, w["a0"], gl, w["k_k"], w["k_a"]]
    lw, kp, na, nb, g = ew_call(_pre_fwd, pre_t, pre_c, [(d, F32)] * 5, tm=256, name="rwkv_pre_fwd")
    r_hm, v_hm = _to_heads(q[:, :d]), _to_heads(q[:, 2 * d:3 * d])
    lw_hm, kp_hm, na_hm, nb_hm = _to_heads(lw), _to_heads(kp), _to_heads(na), _to_heads(nb)
    o_hm, s0s = scan_fwd(r_hm, lw_hm, kp_hm, v_hm, na_hm, nb_hm)
    o = _from_heads(o_hm)
    post_t = [(o, d, 0), (q, d, 0), (kp, d, 0), (q, d, 2), (g, d, 0)]
    post_c = [w["ln_x_w"], w["ln_x_b"], w["r_k"]]
    (ob_bf,) = ew_call(f_post, post_t, post_c, [(d, BF16)], tm=256, name="rwkv_post_fwd")
    yb = mm(ob_bf, w["w_proj_b"], "nn", tm=512, tn=1024, name="mm_proj_b")
    mix_t = [(ya, d, 0), (yb, d, 0), (p_all, d, 2), (p_all, d, 3)]
    (mixed_bf,) = ew_call(f_mix, mix_t, [], [(d, BF16)], tm=256, name="mix_fwd")
    h1 = mm(mixed_bf, w["w_out"], "nn", tm=512, tn=1024, name="mm_out", epi=_add_epi, extras=(x,))
    (f_bf,) = ew_call(lambda h_, g_: (f_ffn_in(h_, g_)[0],), [(h1, d, 0)], [w["g_ffn"]], [(d, BF16)], tm=256,
                      name="ffn_norm")
    hid, act_bf = mm(f_bf, w["w_ffn1"], "nn", tm=512, tn=1024, name="mm_ffn1", out_dtypes=(F32, BF16), epi=_relu2_epi)
    m3 = mm(act_bf, w["w_ffn2"], "nn", tm=512, tn=1024, name="mm_ffn2")
    dh2, dh2_bf, dg_final, loss = final_call(h1, m3, tgt, w["g_final"], tm=256)

    dhid_bf = mm(dh2_bf, w["w_ffn2"], "nt", tm=512, tn=1024, name="mm_dact", out_dtypes=(BF16,), epi=_relu2_bwd_epi,
                 extras=(hid,))
    d_ffn2 = mm(act_bf, dh2_bf, "tn", tm=512, tn=1024, name="mm_dw_ffn2")
    df = mm(dhid_bf, w["w_ffn1"], "nt", tm=512, tn=1024, name="mm_df")
    d_ffn1 = mm(f_bf, dhid_bf, "tn", tm=512, tn=1024, name="mm_dw_ffn1")
    (dh1, dh1_bf), (dg_ffn,) = ew_vjp_call(f_ffn_in, [(h1, d, 0)], [w["g_ffn"]], [(df, d, 0), (dh2, d, 0)],
                                           [(F32, BF16)], [True], tm=256, name="ffn_norm_bwd")
    dmixed = mm(dh1_bf, w["w_out"], "nt", tm=512, tn=1024, name="mm_dmixed")
    d_out = mm(mixed_bf, dh1_bf, "tn", tm=512, tn=1024, name="mm_dw_out")
    (dya_bf, dyb_bf, dga_bf, dgb_bf), _ = ew_vjp_call(f_mix, mix_t, [], [(dmixed, d, 0)], [(BF16,)] * 4, [], tm=256,
                                                      name="mix_bwd")
    dob = mm(dyb_bf, w["w_proj_b"], "nt", tm=512, tn=1024, name="mm_dob")
    d_proj_b = mm(ob_bf, dyb_bf, "tn", tm=512, tn=1024, name="mm_dw_proj_b")
    (do, dr_p, dkp_p, dv_p, dg), (dlnx_w, dlnx_b, dr_k) = ew_vjp_call(
        f_post, post_t, post_c, [(dob, d, 0)], [(F32,)] * 5, [True] * 3, tm=256, name="rwkv_post_bwd")
    scan_g = scan_bwd(r_hm, lw_hm, kp_hm, v_hm, na_hm, nb_hm, s0s, _to_heads(do))
    pre_g = [(_from_heads(z), d, 0) for z in scan_g] + [(dg, d, 0), (dr_p, d, 0), (dkp_p, d, 0), (dv_p, d, 0)]
    (dq_r, dq_k, dq_v, dq_l), (dwl, dw0, dal, da0, dgl, dk_k, dk_a) = ew_vjp_call(
        f_pre, pre_t, pre_c, pre_g, [(F32,)] * 4, [True] * 7, tm=128, name="rwkv_pre_bwd")
    dp_r, dsb_r = shiftmix_bwd(dq_r, 0, p_all, sbp, tm=256, name="shiftmix_bwd_r")
    dp_k, dsb_k = shiftmix_bwd(dq_k, d, p_all, sbp, tm=256, name="shiftmix_bwd_k")
    dp_v, dsb_v = shiftmix_bwd(dq_v, 2 * d, p_all, sbp, tm=256, name="shiftmix_bwd_v")
    dp_l, dsb_l = shiftmix_bwd(dq_l, 3 * d, p_all, sbp, tm=256, name="shiftmix_bwd_l")
    ds = mm(dya_bf, w["w_proj_a"], "nt", tm=512, tn=1024, name="mm_ds")
    d_proj_a = mm(s_bf, dya_bf, "tn", tm=512, tn=1024, name="mm_dw_proj_a")
    (dp_sgu,), (dln_w, dln_b, dsw, dsbt) = ew_vjp_call(f_sgu, sgu_t, sgu_c, [(ds, d, 0)], [(BF16,)], [True] * 4,
                                                       tm=256, name="sgu_bwd")
    dp_all = jnp.concatenate([dp_sgu, dga_bf, dgb_bf, dp_r, dp_k, dp_v, dp_l], axis=1)
    da = mm(dp_all, win_p, "nt", tm=512, tn=512, name="mm_da")
    d_in_p = mm(a_bf, dp_all, "tn", tm=512, tn=1280, name="mm_dw_in")
    (grad_x,), (dg_mix,) = ew_vjp_call(f_norm_in, [(x, d, 0)], [w["g_mix"]], [(da, d, 0), (dh1, d, 0)], [(F32,)],
                                       [True], tm=256, name="norm_in_bwd")

    grads = {
        "g_mix": dg_mix, "w_in": _unpad_win(d_in_p), "sgu_ln_w": dln_w, "sgu_ln_b": dln_b, "sgu_w": dsw,
        "sgu_b": dsbt.T, "w_proj_a": d_proj_a,
        "shift_b": _unpad_rwkv_cols(jnp.concatenate([dsb_r, dsb_k, dsb_v, dsb_l], axis=1)),
        "w_lora_w": dwl[:LORA_W], "w0": dw0, "a_lora_w": dal[:LORA_A], "a0": da0, "g_lora_w": dgl[:LORA_G],
        "k_k": dk_k, "k_a": dk_a, "r_k": dr_k, "ln_x_w": dlnx_w, "ln_x_b": dlnx_b, "w_proj_b": d_proj_b,
        "w_out": d_out, "g_ffn": dg_ffn, "w_ffn1": d_ffn1, "w_ffn2": d_ffn2, "g_final": dg_final,
    }
    return loss[0, 0], grad_x, grads


MESH = pl.DeviceIdType.MESH
N_CHIPS = 4
N_DEV = 8
PACK_ROWS = 4864
HALF_ROWS = PACK_ROWS // 2
SMALL_ROWS = 152
_ANY = pl.BlockSpec(memory_space=pl.ANY)


def _coords():
    return lax.axis_index("x"), lax.axis_index("y"), lax.axis_index("c")


def _other_chips(x, y):
    return [(1 - x, y), (x, 1 - y), (1 - x, 1 - y)]


def _remote(src, dst, send_sems, recv_sems, k, to):
    return pltpu.make_async_remote_copy(src_ref=src, dst_ref=dst, send_sem=send_sems.at[k], recv_sem=recv_sems.at[k],
                                        device_id=to, device_id_type=MESH)


def gather_shards(pack):
    def body(src_ref, out_ref, send_sems, recv_sems):
        x, y, c = _coords()
        me = 2 * x + y
        sib = (x, y, 1 - c)
        chips = _other_chips(x, y)
        first = [_remote(src_ref.at[c], out_ref.at[me, c], send_sems, recv_sems, k, (cx, cy, c))
                 for k, (cx, cy) in enumerate(chips)]
        for cp in first:
            cp.start()
        passed = []
        for k, (cx, cy) in enumerate(chips):
            j = 2 * cx + cy
            _remote(src_ref.at[c], out_ref.at[j, c], send_sems, recv_sems, k, (cx, cy, c)).wait_recv()
            fwd = _remote(out_ref.at[j, c], out_ref.at[j, c], send_sems, recv_sems, 3 + k, sib)
            fwd.start()
            passed.append(fwd)
        for k, (cx, cy) in enumerate(chips):
            j = 2 * cx + cy
            _remote(out_ref.at[j, 1 - c], out_ref.at[j, 1 - c], send_sems, recv_sems, 3 + k, sib).wait_recv()
        for cp in first + passed:
            cp.wait_send()

    return pl.pallas_call(
        body,
        name="gather_shards",
        in_specs=[_ANY],
        out_specs=_ANY,
        out_shape=jax.ShapeDtypeStruct((N_CHIPS,) + pack.shape, pack.dtype),
        scratch_shapes=[pltpu.SemaphoreType.DMA((6,)), pltpu.SemaphoreType.DMA((6,))],
    )(pack)


def reduce_pair(g):
    def body(g_ref, got_ref, send_sems, recv_sems):
        x, y, c = _coords()
        sib = (x, y, 1 - c)
        sends = [_remote(g_ref.at[j, 1 - c], got_ref.at[j], send_sems, recv_sems, j, sib) for j in range(N_CHIPS)]
        for cp in sends:
            cp.start()
        for cp in sends:
            cp.wait_recv()
        for cp in sends:
            cp.wait_send()

    return pl.pallas_call(
        body,
        name="reduce_pair",
        in_specs=[_ANY],
        out_specs=_ANY,
        out_shape=jax.ShapeDtypeStruct((N_CHIPS,) + g.shape[2:], g.dtype),
        scratch_shapes=[pltpu.SemaphoreType.DMA((N_CHIPS,)), pltpu.SemaphoreType.DMA((N_CHIPS,))],
    )(g)


def pair_sum(g, got, *, tm):
    n, _, rows, width = g.shape

    def body(g0_ref, g1_ref, got_ref, out_ref):
        own = jnp.where(lax.axis_index("c") == 0, g0_ref[0, 0], g1_ref[0, 0])
        out_ref[0] = own + got_ref[0]

    blk = pl.BlockSpec((1, tm, width), lambda j, i: (j, i, 0))
    return pl.pallas_call(
        body,
        name="pair_sum",
        grid=(n, rows // tm),
        in_specs=[pl.BlockSpec((1, 1, tm, width), lambda j, i: (j, 0, i, 0)),
                  pl.BlockSpec((1, 1, tm, width), lambda j, i: (j, 1, i, 0)), blk],
        out_specs=blk,
        out_shape=jax.ShapeDtypeStruct(got.shape, got.dtype),
        compiler_params=_cparams(2),
    )(g, g, got)


def reduce_chips(p):
    def body(p_ref, out_ref, send_sems, recv_sems):
        x, y, c = _coords()
        me = 2 * x + y
        chips = _other_chips(x, y)
        sends = [_remote(p_ref.at[2 * cx + cy], out_ref.at[me], send_sems, recv_sems, k, (cx, cy, c))
                 for k, (cx, cy) in enumerate(chips)]
        for cp in sends:
            cp.start()
        for k, (cx, cy) in enumerate(chips):
            _remote(p_ref.at[me], out_ref.at[2 * cx + cy], send_sems, recv_sems, k, (cx, cy, c)).wait_recv()
        for cp in sends:
            cp.wait_send()

    return pl.pallas_call(
        body,
        name="reduce_chips",
        in_specs=[_ANY],
        out_specs=_ANY,
        out_shape=jax.ShapeDtypeStruct(p.shape, p.dtype),
        scratch_shapes=[pltpu.SemaphoreType.DMA((3,)), pltpu.SemaphoreType.DMA((3,))],
    )(p)


def sum_with_own(own, slots, index_fn, *, tm, name):
    n, rows, width = slots.shape
    own3 = own.ndim == 3

    def body(*refs):
        mine = index_fn()
        acc = None
        for s in range(n):
            o = refs[s][0] if own3 else refs[0][...]
            term = jnp.where(mine == s, o, refs[(n if own3 else 1) + s][0])
            acc = term if acc is None else acc + term
        refs[-1][...] = acc

    slot_specs = [pl.BlockSpec((1, tm, width), lambda i, s=s: (s, i, 0)) for s in range(n)]
    own_specs = slot_specs if own3 else [pl.BlockSpec((tm, width), lambda i: (i, 0))]
    return pl.pallas_call(
        body,
        name=name,
        grid=(rows // tm,),
        in_specs=own_specs + slot_specs,
        out_specs=pl.BlockSpec((tm, width), lambda i: (i, 0)),
        out_shape=jax.ShapeDtypeStruct((rows, width), slots.dtype),
        compiler_params=_cparams(1),
    )(*([own] * (n if own3 else 1)), *([slots] * n))


def exchange_halves(s):
    nq = 8
    rq = s.shape[0] // nq

    def body(s_ref, out_ref, sbuf, rbuf, send_sems, recv_sems, in_sems, out_sems):
        x, y, c = _coords()
        sib = (x, y, 1 - c)
        rows = lambda q: pl.ds(q * rq, rq)
        loads = [pltpu.make_async_copy(s_ref.at[rows(q)], sbuf.at[rows(q)], in_sems.at[q]) for q in range(nq)]
        for cp in loads:
            cp.start()
        sends = []
        for q in range(nq):
            loads[q].wait()
            sends.append(_remote(sbuf.at[rows(q)], rbuf.at[rows(q)], send_sems, recv_sems, q, sib))
            sends[q].start()
        stores = []
        for q in range(nq):
            sends[q].wait_recv()
            stores.append(pltpu.make_async_copy(rbuf.at[rows(q)], out_ref.at[rows(q)], out_sems.at[q]))
            stores[q].start()
        for cp in sends:
            cp.wait_send()
        for cp in stores:
            cp.wait()

    return pl.pallas_call(
        body,
        name="exchange_halves",
        in_specs=[_ANY],
        out_specs=_ANY,
        out_shape=jax.ShapeDtypeStruct(s.shape, s.dtype),
        scratch_shapes=[pltpu.VMEM(s.shape, s.dtype), pltpu.VMEM(s.shape, s.dtype)]
        + [pltpu.SemaphoreType.DMA((nq,))] * 4,
        compiler_params=pltpu.CompilerParams(vmem_limit_bytes=VMEM_LIMIT),
    )(s)


def gather_all(s):
    def body(s_ref, out_ref, send_sems, recv_sems):
        x, y, c = _coords()
        me = 4 * x + 2 * y + c
        peers = []
        for mask in range(1, N_DEV):
            px = 1 - x if mask & 4 else x
            py = 1 - y if mask & 2 else y
            pc = 1 - c if mask & 1 else c
            peers.append((px, py, pc))
        sends = [_remote(s_ref, out_ref.at[me], send_sems, recv_sems, k, peer) for k, peer in enumerate(peers)]
        for cp in sends:
            cp.start()
        for k, (px, py, pc) in enumerate(peers):
            _remote(s_ref, out_ref.at[4 * px + 2 * py + pc], send_sems, recv_sems, k, (px, py, pc)).wait_recv()
        for cp in sends:
            cp.wait_send()

    return pl.pallas_call(
        body,
        name="gather_all",
        in_specs=[_ANY],
        out_specs=_ANY,
        out_shape=jax.ShapeDtypeStruct((N_DEV,) + s.shape, s.dtype),
        scratch_shapes=[pltpu.SemaphoreType.DMA((N_DEV - 1,)), pltpu.SemaphoreType.DMA((N_DEV - 1,))],
    )(s)


ADAM_LR = 0.001
ADAM_B1 = 0.9
ADAM_B2 = 0.999
ADAM_EPS = 1e-08
ADAM_WD = 0.01
ADAM_STEP = 10


def f_adamw(g, w, m, v):
    m = ADAM_B1 * m + (1.0 - ADAM_B1) * g
    v = ADAM_B2 * v + (1.0 - ADAM_B2) * jnp.square(g)
    m_hat = m / (1.0 - ADAM_B1 ** ADAM_STEP)
    v_hat = v / (1.0 - ADAM_B2 ** ADAM_STEP)
    delta = -ADAM_LR * (m_hat / (jnp.sqrt(v_hat) + ADAM_EPS) + ADAM_WD * w)
    return delta, m, v


def adamw_call(g, w, m, v, *, tm, name):
    width = g.shape[1]
    return ew_call(f_adamw, [(g, width, 0), (w, width, 0), (m, width, 0), (v, width, 0)], [], [(width, F32)] * 3,
                   tm=tm, name=name)


def adamw_halves(g_own, g_other, w, m, v, *, tm):
    _, rows, width = w.shape

    def body(go_ref, gx_ref, w_ref, m_ref, v_ref, g_ref, d_ref, nm_ref, nv_ref):
        g = jnp.where(pl.program_id(0) == lax.axis_index("c"), go_ref[...], gx_ref[...])
        delta, nm, nv = f_adamw(g, w_ref[0], m_ref[0], v_ref[0])
        g_ref[0] = g
        d_ref[0] = delta
        nm_ref[0] = nm
        nv_ref[0] = nv

    half = pl.BlockSpec((tm, width), lambda h, i: (i, 0))
    full = pl.BlockSpec((1, tm, width), lambda h, i: (h, i, 0))
    return pl.pallas_call(
        body,
        name="adamw_sharded",
        grid=(2, rows // tm),
        in_specs=[half, half, full, full, full],
        out_specs=[full] * 4,
        out_shape=[jax.ShapeDtypeStruct(w.shape, F32)] * 4,
        compiler_params=_cparams(2),
    )(g_own, g_other, w, m, v)


SHARDED = ["w_in", "w_proj_a", "w_lora_w", "a_lora_w", "g_lora_w", "w_proj_b", "w_out", "w_ffn1", "w_ffn2"]
SHARD_AXIS = {"w_in": 1, "w_proj_a": 0, "w_lora_w": 1, "a_lora_w": 1, "g_lora_w": 1, "w_proj_b": 0, "w_out": 0,
              "w_ffn1": 1, "w_ffn2": 0}
SHARD_SHAPE = {"w_in": (1024, 1864), "w_proj_a": (256, 1024), "w_lora_w": (64, 256), "a_lora_w": (64, 256),
               "g_lora_w": (160, 256), "w_proj_b": (256, 1024), "w_out": (256, 1024), "w_ffn1": (1024, 1024),
               "w_ffn2": (1024, 1024)}
SHIFT_SHARD = (2, 840)
VECTORS = ["g_mix", "sgu_ln_w", "sgu_ln_b", "w0", "a0", "k_k", "k_a", "r_k", "ln_x_w", "ln_x_b", "g_ffn", "g_final"]
SMALL = VECTORS + ["sgu_w", "sgu_b"]
SMALL_SHAPE = {**{n: (1, 1024) for n in VECTORS}, "sgu_w": (8, 128, 128), "sgu_b": (8, 128)}
WEIGHTS = ["g_mix", "w_in", "sgu_ln_w", "sgu_ln_b", "sgu_w", "sgu_b", "w_proj_a", "shift_b", "w_lora_w", "w0",
           "a_lora_w", "a0", "g_lora_w", "k_k", "k_a", "r_k", "ln_x_w", "ln_x_b", "w_proj_b", "w_out", "g_ffn",
           "w_ffn1", "w_ffn2", "g_final"]


def _size(shape):
    n = 1
    for s in shape:
        n *= s
    return n


def _pack_rows(parts, rows, dtype):
    flat = jnp.concatenate([p.reshape(-1).astype(dtype) for p in parts])
    return jnp.concatenate([flat, jnp.zeros((rows * 1024 - flat.shape[0],), dtype)]).reshape(rows, 1024)


def _unpack_rows(packed, shapes):
    flat = packed.reshape(-1)
    out, off = [], 0
    for shp in shapes:
        out.append(flat[off:off + _size(shp)].reshape(shp))
        off += _size(shp)
    return out


def _shard_of(name, full, j):
    ax = SHARD_AXIS[name]
    n = SHARD_SHAPE[name][ax]
    return lax.slice_in_dim(full, j * n, (j + 1) * n, axis=ax)


def kernel(x, g_mix, w_in, sgu_ln_w, sgu_ln_b, sgu_w, sgu_b, w_proj_a, shift_b, w_lora_w, w0, a_lora_w, a0, g_lora_w, k_k, k_a, r_k, ln_x_w, ln_x_b, w_proj_b, w_out, g_ffn, w_ffn1, w_ffn2, g_final, loss_target, m_g_mix, m_w_in, m_sgu_ln_w, m_sgu_ln_b, m_sgu_w, m_sgu_b, m_w_proj_a, m_shift_b, m_w_lora_w, m_w0, m_a_lora_w, m_a0, m_g_lora_w, m_k_k, m_k_a, m_r_k, m_ln_x_w, m_ln_x_b, m_w_proj_b, m_w_out, m_g_ffn, m_w_ffn1, m_w_ffn2, m_g_final, v_g_mix, v_w_in, v_sgu_ln_w, v_sgu_ln_b, v_sgu_w, v_sgu_b, v_w_proj_a, v_shift_b, v_w_lora_w, v_w0, v_a_lora_w, v_a0, v_g_lora_w, v_k_k, v_k_a, v_r_k, v_ln_x_w, v_ln_x_b, v_w_proj_b, v_w_out, v_g_ffn, v_w_ffn1, v_w_ffn2, v_g_final):
    given = dict(zip(WEIGHTS, (g_mix, w_in, sgu_ln_w, sgu_ln_b, sgu_w, sgu_b, w_proj_a, shift_b, w_lora_w, w0, a_lora_w, a0, g_lora_w, k_k, k_a, r_k, ln_x_w, ln_x_b, w_proj_b, w_out, g_ffn, w_ffn1, w_ffn2, g_final)))
    mom_m = dict(zip(WEIGHTS, (m_g_mix, m_w_in, m_sgu_ln_w, m_sgu_ln_b, m_sgu_w, m_sgu_b, m_w_proj_a, m_shift_b, m_w_lora_w, m_w0, m_a_lora_w, m_a0, m_g_lora_w, m_k_k, m_k_a, m_r_k, m_ln_x_w, m_ln_x_b, m_w_proj_b, m_w_out, m_g_ffn, m_w_ffn1, m_w_ffn2, m_g_final)))
    mom_v = dict(zip(WEIGHTS, (v_g_mix, v_w_in, v_sgu_ln_w, v_sgu_ln_b, v_sgu_w, v_sgu_b, v_w_proj_a, v_shift_b, v_w_lora_w, v_w0, v_a_lora_w, v_a0, v_g_lora_w, v_k_k, v_k_a, v_r_k, v_ln_x_w, v_ln_x_b, v_w_proj_b, v_w_out, v_g_ffn, v_w_ffn1, v_w_ffn2, v_g_final)))
    chip = 2 * lax.axis_index("x") + lax.axis_index("y")

    def local_block(tree, n):
        return tree[n] if n == "g_final" else tree[n][0]

    sb = local_block(given, "shift_b")
    exact = ["w_lora_w", "a_lora_w", "g_lora_w"]
    lo_part = lambda z: (z - z.astype(BF16).astype(F32)).astype(BF16)
    pack_w = _pack_rows([local_block(given, n) for n in SHARDED] + [sb]
                        + [lo_part(local_block(given, n)) for n in exact] + [lo_part(sb)], PACK_ROWS, BF16)
    gathered = gather_shards(pack_w.reshape(2, HALF_ROWS, 1024)).reshape(N_CHIPS, PACK_ROWS, 1024)
    shapes = [SHARD_SHAPE[n] for n in SHARDED] + [SHIFT_SHARD] + [SHARD_SHAPE[n] for n in exact] + [SHIFT_SHARD]
    per_chip = [_unpack_rows(jnp.where(chip == j, pack_w, gathered[j]), shapes) for j in range(N_CHIPS)]
    n_sh = len(SHARDED)
    w = {}
    for i, n in enumerate(SHARDED):
        w[n] = jnp.concatenate([per_chip[j][i] for j in range(N_CHIPS)], axis=SHARD_AXIS[n])
    for i, n in enumerate(exact):
        lo = jnp.concatenate([per_chip[j][n_sh + 1 + i] for j in range(N_CHIPS)], axis=1)
        w[n] = w[n].astype(F32) + lo.astype(F32)
    w["shift_b"] = jnp.concatenate(
        [per_chip[j][n_sh].astype(F32) + per_chip[j][-1].astype(F32) for j in range(N_CHIPS)], axis=1)
    for n in SMALL:
        w[n] = local_block(given, n).reshape(SMALL_SHAPE[n])

    loss, grad_x, grads = local_step(x[0], loss_target[0], w)
    loss = lax.psum(loss, ("x", "y", "c"))

    g_pack = jnp.stack([_pack_rows([_shard_of(n, grads[n], j) for n in SHARDED], PACK_ROWS, F32)
                        for j in range(N_CHIPS)])
    g_pack = g_pack.reshape(N_CHIPS, 2, HALF_ROWS, 1024)
    chip_part = pair_sum(g_pack, reduce_pair(g_pack), tm=128)
    half_sum = sum_with_own(chip_part, reduce_chips(chip_part), lambda: 2 * lax.axis_index("x") + lax.axis_index("y"),
                            tm=128, name="chip_sum")
    other_half = exchange_halves(half_sum)
    halves = lambda tree: _pack_rows([local_block(tree, n) for n in SHARDED], PACK_ROWS, F32).reshape(
        2, HALF_ROWS, 1024)
    g_big, d_big, nm_big, nv_big = adamw_halves(half_sum, other_half, halves(given), halves(mom_m), halves(mom_v),
                                                tm=128)
    big_shapes = [SHARD_SHAPE[n] for n in SHARDED]
    out_g = dict(zip(SHARDED, _unpack_rows(g_big, big_shapes)))
    out_d = dict(zip(SHARDED, _unpack_rows(d_big, big_shapes)))
    out_m = dict(zip(SHARDED, _unpack_rows(nm_big, big_shapes)))
    out_v = dict(zip(SHARDED, _unpack_rows(nv_big, big_shapes)))

    small_shapes = [SMALL_SHAPE[n] for n in SMALL]
    s_pack = _pack_rows([grads[n] for n in SMALL] + [grads["shift_b"]], SMALL_ROWS, F32)
    g_small = sum_with_own(
        s_pack, gather_all(s_pack), lambda: 4 * lax.axis_index("x") + 2 * lax.axis_index("y") + lax.axis_index("c"),
        tm=SMALL_ROWS, name="small_sum")
    w_small = _pack_rows([local_block(given, n) for n in SMALL], SMALL_ROWS, F32)
    m_small = _pack_rows([local_block(mom_m, n) for n in SMALL], SMALL_ROWS, F32)
    v_small = _pack_rows([local_block(mom_v, n) for n in SMALL], SMALL_ROWS, F32)
    d_small, nm_small, nv_small = adamw_call(g_small, w_small, m_small, v_small, tm=SMALL_ROWS, name="adamw_small")
    g_parts = _unpack_rows(g_small, small_shapes + [(2, N_RWKV)])
    out_g.update(zip(SMALL, g_parts[:-1]))
    out_d.update(zip(SMALL, _unpack_rows(d_small, small_shapes)))
    out_m.update(zip(SMALL, _unpack_rows(nm_small, small_shapes)))
    out_v.update(zip(SMALL, _unpack_rows(nv_small, small_shapes)))
    g_sb = lax.dynamic_slice_in_dim(g_parts[-1], chip * SHIFT_SHARD[1], SHIFT_SHARD[1], axis=1)
    sb_args = [_pack_rows([z], 8, F32) for z in (g_sb, sb, local_block(mom_m, "shift_b"), local_block(mom_v, "shift_b"))]
    sb_res = adamw_call(*sb_args, tm=8, name="adamw_shift_b")
    out_g["shift_b"] = g_sb
    for tree, res in zip((out_d, out_m, out_v), sb_res):
        tree["shift_b"] = _unpack_rows(res, [SHIFT_SHARD])[0]

    def block_of(tree, n):
        return tree[n].reshape(given[n].shape)

    return (loss, grad_x[None], *[block_of(out_g, n) for n in WEIGHTS], *[block_of(out_d, n) for n in WEIGHTS],
            *[block_of(out_m, n) for n in WEIGHTS], *[block_of(out_v, n) for n in WEIGHTS])
```

```python
import functools

import jax
import jax.numpy as jnp
from jax import lax
from jax.experimental import pallas as pl
from jax.experimental.pallas import tpu as pltpu

F32 = jnp.float32
BF16 = jnp.bfloat16

D_MODEL = 1024
N_HEADS = 16
HEAD = 64
SCAN_CHUNK = 64

VMEM_LIMIT = 56 * 1024 * 1024


_BDIMS = {
    "nn": (((2,), (1,)), ((0,), (0,))),
    "nt": (((2,), (2,)), ((0,), (0,))),
    "tn": (((1,), (1,)), ((0,), (0,))),
}


def _raw_bdot(x, y, mode):
    return lax.dot_general(x, y, _BDIMS[mode], precision=lax.Precision.HIGH, preferred_element_type=F32)


@functools.partial(jax.custom_vjp, nondiff_argnums=(2,))
def bdot(x, y, mode):
    return _raw_bdot(x, y, mode)


def _bdot_fwd(x, y, mode):
    return _raw_bdot(x, y, mode), (x, y)


def _bdot_bwd(mode, res, g):
    x, y = res
    if mode == "nn":
        return bdot(g, y, "nt"), bdot(x, g, "tn")
    if mode == "nt":
        return bdot(g, y, "nn"), bdot(g, x, "tn")
    return bdot(y, g, "nt"), bdot(x, g, "nn")


bdot.defvjp(_bdot_fwd, _bdot_bwd)


def _scan_chunk(S0, r, lw, k, v, a, b):
    nh, lc, _ = r.shape
    ti = lax.broadcasted_iota(jnp.int32, (lc, lc), 0)
    si = lax.broadcasted_iota(jnp.int32, (lc, lc), 1)
    incl = (si <= ti).astype(F32)
    strict = (si < ti).astype(F32)
    eye = (si == ti).astype(F32)
    cl = bdot(jnp.broadcast_to(incl, (nh, lc, lc)), lw, "nn")
    cl_last = cl[:, lc - 1:lc, :]
    g_last = jnp.exp(cl_last - cl)
    at = a * jnp.exp(cl - lw)
    bt = b * jnp.exp(-cl)
    kt = k * jnp.exp(-cl)
    rt = r * jnp.exp(cl)
    m_ab = bdot(at, bt, "nt") * strict
    m_ak = bdot(at, kt, "nt") * strict
    m_rb = bdot(rt, bt, "nt") * incl
    m_rk = bdot(rt, kt, "nt") * incl
    x = eye + m_ab
    p = m_ab
    n = 1
    while n * 2 < lc:
        p = bdot(p, p, "nn")
        x = x + bdot(x, p, "nn")
        n *= 2
    u = bdot(x, bdot(at, S0, "nt") + bdot(m_ak, v, "nn"), "nn")
    o = bdot(rt, S0, "nt") + bdot(m_rb, u, "nn") + bdot(m_rk, v, "nn")
    s_last = S0 * jnp.exp(cl_last) + bdot(u, b * g_last, "tn") + bdot(v, k * g_last, "tn")
    return o, s_last


def _split_heads(z):
    return jnp.stack([z[:, HEAD * h:HEAD * (h + 1)] for h in range(N_HEADS)], axis=0)


def _merge_heads(z):
    return jnp.concatenate([z[h] for h in range(N_HEADS)], axis=1)


def _scan_specs(t, ops, rev):
    nc = t // SCAN_CHUNK
    row = (lambda c: nc - 1 - c) if rev else (lambda c: c)
    specs = [pl.BlockSpec((SCAN_CHUNK, D_MODEL), lambda c, cb=cb: (row(c), cb)) for _, cb in ops]
    state = pl.BlockSpec((1, N_HEADS, HEAD, HEAD), lambda c: (row(c), 0, 0, 0))
    return nc, specs, state


def scan_fwd(ops):
    t = ops[0][0].shape[0]
    nc, specs, state = _scan_specs(t, ops, False)

    def body(r_ref, lw_ref, k_ref, v_ref, a_ref, b_ref, o_ref, s0_ref, s_scr):
        @pl.when(pl.program_id(0) == 0)
        def _():
            s_scr[...] = jnp.zeros_like(s_scr)

        s0 = s_scr[...]
        s0_ref[0] = s0
        o, s_last = _scan_chunk(s0, *[_split_heads(z[...]) for z in (r_ref, lw_ref, k_ref, v_ref, a_ref, b_ref)])
        o_ref[...] = _merge_heads(o)
        s_scr[...] = s_last

    return pl.pallas_call(
        body,
        name="scan_fwd",
        grid=(nc,),
        in_specs=specs,
        out_specs=[pl.BlockSpec((SCAN_CHUNK, D_MODEL), lambda c: (c, 0)), state],
        out_shape=[jax.ShapeDtypeStruct((t, D_MODEL), F32), jax.ShapeDtypeStruct((nc, N_HEADS, HEAD, HEAD), F32)],
        scratch_shapes=[pltpu.VMEM((N_HEADS, HEAD, HEAD), F32)],
        compiler_params=_cparams(1),
    )(*[a for a, _ in ops])


def scan_bwd(ops, s0s, do):
    t = ops[0][0].shape[0]
    nc, specs, state = _scan_specs(t, ops + [(do, 0)], True)

    def body(r_ref, lw_ref, k_ref, v_ref, a_ref, b_ref, do_ref, s0_ref, *rest):
        out_refs, ds_scr = rest[:6], rest[6]

        @pl.when(pl.program_id(0) == 0)
        def _():
            ds_scr[...] = jnp.zeros_like(ds_scr)

        _, vjp = jax.vjp(_scan_chunk, s0_ref[0],
                         *[_split_heads(z[...]) for z in (r_ref, lw_ref, k_ref, v_ref, a_ref, b_ref)])
        grads = vjp((_split_heads(do_ref[...]), ds_scr[...]))
        for o_ref, g in zip(out_refs, grads[1:]):
            o_ref[...] = _merge_heads(g)
        ds_scr[...] = grads[0]

    return pl.pallas_call(
        body,
        name="scan_bwd",
        grid=(nc,),
        in_specs=specs + [state],
        out_specs=[pl.BlockSpec((SCAN_CHUNK, D_MODEL), lambda c: (nc - 1 - c, 0))] * 6,
        out_shape=[jax.ShapeDtypeStruct((t, D_MODEL), F32)] * 6,
        scratch_shapes=[pltpu.VMEM((N_HEADS, HEAD, HEAD), F32)],
        compiler_params=_cparams(1),
    )(*[a for a, _ in ops], do, s0s)


_MDIMS = {
    "nn": (((1,), (0,)), ((), ())),
    "nt": (((1,), (1,)), ((), ())),
    "tn": (((0,), (0,)), ((), ())),
}


def _raw_mdot(x, y, mode, exact):
    if exact:
        return lax.dot_general(x, y, _MDIMS[mode], precision=lax.Precision.HIGHEST, preferred_element_type=F32)
    return lax.dot_general(x.astype(BF16), y.astype(BF16), _MDIMS[mode], preferred_element_type=F32)


@functools.partial(jax.custom_vjp, nondiff_argnums=(2, 3))
def mdot(x, y, mode, exact):
    return _raw_mdot(x, y, mode, exact)


def _mdot_fwd(x, y, mode, exact):
    return _raw_mdot(x, y, mode, exact), (x, y)


def _mdot_bwd(mode, exact, res, g):
    x, y = res
    if mode == "nn":
        return mdot(g, y, "nt", exact), mdot(x, g, "tn", exact)
    if mode == "nt":
        return mdot(g, y, "nn", exact), mdot(g, x, "tn", exact)
    return mdot(y, g, "nt", exact), mdot(x, g, "nn", exact)


mdot.defvjp(_mdot_fwd, _mdot_bwd)


def _seg_ones():
    i = lax.broadcasted_iota(jnp.int32, (256, 256), 0) // HEAD
    j = lax.broadcasted_iota(jnp.int32, (256, 256), 1) // HEAD
    return (i == j).astype(BF16)


@jax.custom_vjp
def segsum(x):
    bd = _seg_ones()
    hi = x.astype(BF16)
    lo = (x - hi.astype(F32)).astype(BF16)
    cols = []
    for j in range(x.shape[1] // 256):
        sl = slice(256 * j, 256 * (j + 1))
        cols.append(jnp.dot(hi[:, sl], bd, preferred_element_type=F32)
                    + jnp.dot(lo[:, sl], bd, preferred_element_type=F32))
    return jnp.concatenate(cols, axis=1)


segsum.defvjp(lambda x: (segsum(x), None), lambda _, g: (segsum(g),))


NORM_EPS = 1e-6
LN_EPS = 1e-5
GN_EPS = 64e-5
SGU_CHUNK = 128
SGU_GROUPS = 8


def _rms(x, g):
    return x * lax.rsqrt(jnp.mean(x * x, axis=-1, keepdims=True) + NORM_EPS) * g


def f_norm_in(x, g):
    return _rms(x, g), x


def f_sgu(p, ln_w, ln_b, sw, sbt):
    tm = p.shape[0]
    z = 0.5 * p * (1.0 + lax.erf(p * 0.7071067811865476))
    u, v = z[:, :D_MODEL], z[:, D_MODEL:]
    mu = jnp.mean(v, axis=-1, keepdims=True)
    d = v - mu
    vn = d * lax.rsqrt(jnp.mean(d * d, axis=-1, keepdims=True) + LN_EPS) * ln_w + ln_b
    ii = lax.broadcasted_iota(jnp.int32, (SGU_CHUNK, SGU_CHUNK), 0)
    jj = lax.broadcasted_iota(jnp.int32, (SGU_CHUNK, SGU_CHUNK), 1)
    mask = (jj <= ii).astype(F32)
    gi = lax.broadcasted_iota(jnp.int32, (SGU_GROUPS, D_MODEL), 0)
    ci = lax.broadcasted_iota(jnp.int32, (SGU_GROUPS, D_MODEL), 1) // SGU_CHUNK
    bias = mdot(sbt, (gi == ci).astype(F32), "nn", True)
    rows = []
    for c in range(tm // SGU_CHUNK):
        cols = []
        for g in range(SGU_GROUPS):
            blk = vn[c * SGU_CHUNK:(c + 1) * SGU_CHUNK, g * SGU_CHUNK:(g + 1) * SGU_CHUNK]
            cols.append(mdot(sw[g] * mask, blk, "nn", False))
        rows.append(jnp.concatenate(cols, axis=1) + bias)
    return (u * jnp.concatenate(rows, axis=0),)


def _softplus(x):
    return jnp.maximum(x, 0.0) + jnp.log1p(jnp.exp(-jnp.abs(x)))


def f_pre(qr, qk, qv, ql, wl, w0, al, a0, gl, k_k, k_a):
    xw, xa, xg = ql[:, :128], ql[:, 128:256], ql[:, 256:512]
    wr = -_softplus(-(w0 + mdot(jnp.tanh(xw), wl, "nn", True))) - 0.5
    lw = -jnp.exp(wr)
    aa = jax.nn.sigmoid(a0 + mdot(xa, al, "nn", True))
    g = mdot(jax.nn.sigmoid(xg), gl, "nn", True)
    kkr = qk * k_k
    kk = kkr / jnp.maximum(jnp.sqrt(segsum(kkr * kkr)), 1e-12)
    kp = qk * (1.0 + (aa - 1.0) * k_a)
    return qr, lw, kp, qv, -kk, kk * aa, g, qr, kp, qv


def f_post(o, r, kp, v, g, lnw, lnb, rk):
    mu = segsum(o) * (1.0 / HEAD)
    d = o - mu
    gn = d * lax.rsqrt(segsum(d * d) * (1.0 / HEAD) + GN_EPS)
    return ((gn * lnw + lnb + segsum(r * kp * rk) * v) * g,)


def f_mix(ya, yb, ga, gb):
    return (jax.nn.sigmoid(ga) * ya + jax.nn.sigmoid(gb) * yb,)


def f_ffn_in(h1, g):
    return _rms(h1, g), h1


def f_final(h1, m3, tgt, g):
    y = _rms(h1 + m3, g)
    err = jnp.square(y - tgt)
    return 0.5 * jnp.sum(jnp.mean(err, axis=-1))


def _cparams(n_grid):
    return pltpu.CompilerParams(dimension_semantics=("arbitrary",) * n_grid, vmem_limit_bytes=VMEM_LIMIT)


def _tile_spec(tm, w, cb):
    return pl.BlockSpec((tm, w), lambda i: (i, cb))


def _const_spec(c):
    nd = c.ndim
    return pl.BlockSpec(c.shape, lambda i: (0,) * nd)


def ew_call(fn, tiled, consts, outs, *, tm, name):
    t = tiled[0][0].shape[0]
    n_t, n_c = len(tiled), len(consts)

    def body(*refs):
        tv = [r[...].astype(F32) for r in refs[:n_t]]
        cv = [r[...] for r in refs[n_t:n_t + n_c]]
        res = fn(*tv, *cv)
        for o_ref, val in zip(refs[n_t + n_c:], res):
            o_ref[...] = val.astype(o_ref.dtype)

    return pl.pallas_call(
        body,
        name=name,
        grid=(t // tm,),
        in_specs=[_tile_spec(tm, w, cb) for _, w, cb in tiled] + [_const_spec(c) for c in consts],
        out_specs=[_tile_spec(tm, w, 0) for w, _ in outs],
        out_shape=[jax.ShapeDtypeStruct((t, w), dt) for w, dt in outs],
        compiler_params=_cparams(1),
    )(*[a for a, _, _ in tiled], *consts)


def ew_vjp_call(fn, tiled, consts, cots, d_tiled, d_consts, *, tm, name):
    t = tiled[0][0].shape[0]
    n_t, n_c, n_g = len(tiled), len(consts), len(cots)
    dt_list = [(i, dt) for i, dts in enumerate(d_tiled) for dt in dts]
    dc_list = [i for i, want in enumerate(d_consts) if want]

    def body(*refs):
        tv = [r[...].astype(F32) for r in refs[:n_t]]
        cv = [r[...] for r in refs[n_t:n_t + n_c]]
        gv = tuple(r[...].astype(F32) for r in refs[n_t + n_c:n_t + n_c + n_g])
        out_refs = refs[n_t + n_c + n_g:]
        _, vjp = jax.vjp(fn, *tv, *cv)
        grads = vjp(gv)
        for o_ref, (i, _) in zip(out_refs, dt_list):
            o_ref[...] = grads[i].astype(o_ref.dtype)
        acc_refs = out_refs[len(dt_list):]

        @pl.when(pl.program_id(0) == 0)
        def _():
            for a_ref in acc_refs:
                a_ref[...] = jnp.zeros_like(a_ref)

        for a_ref, i in zip(acc_refs, dc_list):
            a_ref[...] += grads[n_t + i]

    res = pl.pallas_call(
        body,
        name=name,
        grid=(t // tm,),
        in_specs=[_tile_spec(tm, w, cb) for _, w, cb in tiled] + [_const_spec(c) for c in consts]
        + [_tile_spec(tm, w, cb) for _, w, cb in cots],
        out_specs=[_tile_spec(tm, tiled[i][1], 0) for i, _ in dt_list] + [_const_spec(consts[i]) for i in dc_list],
        out_shape=[jax.ShapeDtypeStruct((t, tiled[i][1]), dt) for i, dt in dt_list]
        + [jax.ShapeDtypeStruct(consts[i].shape, F32) for i in dc_list],
        compiler_params=_cparams(1),
    )(*[a for a, _, _ in tiled], *consts, *[a for a, _, _ in cots])
    return res[:len(dt_list)], res[len(dt_list):]


def mm(a, b, mode, *, tm, tn, name, out_dtypes=(F32,), epi=None, extras=()):
    m = a.shape[1] if mode == "tn" else a.shape[0]
    kd = a.shape[0] if mode == "tn" else a.shape[1]
    n = b.shape[0] if mode == "nt" else b.shape[1]
    tm, tn = min(tm, m), min(tn, n)
    if mode == "nn":
        a_spec = pl.BlockSpec((tm, kd), lambda i, j: (i, 0))
        b_spec = pl.BlockSpec((kd, tn), lambda i, j: (0, j))
    elif mode == "nt":
        a_spec = pl.BlockSpec((tm, kd), lambda i, j: (i, 0))
        b_spec = pl.BlockSpec((tn, kd), lambda i, j: (j, 0))
    else:
        a_spec = pl.BlockSpec((kd, tm), lambda i, j: (0, i))
        b_spec = pl.BlockSpec((kd, tn), lambda i, j: (0, j))
    n_e = len(extras)
    o_spec = pl.BlockSpec((tm, tn), lambda i, j: (i, j))

    def body(a_ref, b_ref, *refs):
        c = lax.dot_general(a_ref[...], b_ref[...], _MDIMS[mode], preferred_element_type=F32)
        res = epi(c, *[r[...] for r in refs[:n_e]]) if epi is not None else (c,)
        for o_ref, val in zip(refs[n_e:], res):
            o_ref[...] = val.astype(o_ref.dtype)

    res = pl.pallas_call(
        body,
        name=name,
        grid=(m // tm, n // tn),
        in_specs=[a_spec, b_spec] + [o_spec] * n_e,
        out_specs=[o_spec] * len(out_dtypes),
        out_shape=[jax.ShapeDtypeStruct((m, n), dt) for dt in out_dtypes],
        compiler_params=_cparams(2),
    )(a, b, *extras)
    return res if len(out_dtypes) > 1 else res[0]


P_WIDTH = 7680
RWKV_COL0 = 4096
RWKV_WIDTH = 3584
SHIFT_BLK = 512


def _shift_down(p, prev_row):
    rows = lax.broadcasted_iota(jnp.int32, p.shape, 0)
    return jnp.where(rows == 0, prev_row, pltpu.roll(p, 1, 0))


def shiftmix_fwd(p_all, sbp, *, tm):
    t = p_all.shape[0]
    c0 = RWKV_COL0 // SHIFT_BLK
    hb = tm // 8

    def body(p_ref, halo_ref, sb_ref, q_ref):
        p = p_ref[...]
        prev = jnp.where(pl.program_id(0) == 0, 0.0, halo_ref[7:8, :])
        q_ref[...] = p * sb_ref[0:1, :] + _shift_down(p, prev) * sb_ref[1:2, :]

    return pl.pallas_call(
        body,
        name="shiftmix_fwd",
        grid=(t // tm, RWKV_WIDTH // SHIFT_BLK),
        in_specs=[
            pl.BlockSpec((tm, SHIFT_BLK), lambda i, j: (i, c0 + j)),
            pl.BlockSpec((8, SHIFT_BLK), lambda i, j: (jnp.maximum(i * hb - 1, 0), c0 + j)),
            pl.BlockSpec((2, SHIFT_BLK), lambda i, j: (0, j)),
        ],
        out_specs=pl.BlockSpec((tm, SHIFT_BLK), lambda i, j: (i, j)),
        out_shape=jax.ShapeDtypeStruct((t, RWKV_WIDTH), F32),
        compiler_params=_cparams(2),
    )(p_all, p_all, sbp)


def shiftmix_bwd(dq, col0, p_all, sbp, *, tm, name):
    t, w = dq.shape
    n_i = t // tm
    hb = tm // 8
    cq = col0 // SHIFT_BLK
    cp = (RWKV_COL0 + col0) // SHIFT_BLK

    def body(dq_ref, dqn_ref, p_ref, ph_ref, sb_ref, dp_ref, dsb_ref):
        i = pl.program_id(1)
        dq_t = dq_ref[...]
        rows = lax.broadcasted_iota(jnp.int32, dq_t.shape, 0)
        nxt = jnp.where(i == n_i - 1, 0.0, dqn_ref[0:1, :])
        up = jnp.where(rows == tm - 1, nxt, pltpu.roll(dq_t, tm - 1, 0))
        dp_ref[...] = (dq_t * sb_ref[0:1, :] + up * sb_ref[1:2, :]).astype(dp_ref.dtype)
        p = p_ref[...]
        prev = jnp.where(i == 0, 0.0, ph_ref[7:8, :])
        s0 = jnp.sum(dq_t * p, axis=0, keepdims=True)
        s1 = jnp.sum(dq_t * _shift_down(p, prev), axis=0, keepdims=True)
        two = lax.broadcasted_iota(jnp.int32, (2, SHIFT_BLK), 0)

        @pl.when(i == 0)
        def _():
            dsb_ref[...] = jnp.zeros_like(dsb_ref)

        dsb_ref[...] += jnp.where(two == 0, s0, s1)

    return pl.pallas_call(
        body,
        name=name,
        grid=(w // SHIFT_BLK, n_i),
        in_specs=[
            pl.BlockSpec((tm, SHIFT_BLK), lambda j, i: (i, j)),
            pl.BlockSpec((8, SHIFT_BLK), lambda j, i: (jnp.minimum((i + 1) * hb, t // 8 - 1), j)),
            pl.BlockSpec((tm, SHIFT_BLK), lambda j, i: (i, cp + j)),
            pl.BlockSpec((8, SHIFT_BLK), lambda j, i: (jnp.maximum(i * hb - 1, 0), cp + j)),
            pl.BlockSpec((2, SHIFT_BLK), lambda j, i: (0, cq + j)),
        ],
        out_specs=[
            pl.BlockSpec((tm, SHIFT_BLK), lambda j, i: (i, j)),
            pl.BlockSpec((2, SHIFT_BLK), lambda j, i: (0, j)),
        ],
        out_shape=[jax.ShapeDtypeStruct((t, w), BF16), jax.ShapeDtypeStruct((2, w), F32)],
        compiler_params=_cparams(2),
    )(dq, dq, p_all, p_all, sbp)


def final_call(h1, m3, tgt, g_final, *, tm):
    t = h1.shape[0]

    def body(h1_ref, m3_ref, tgt_ref, g_ref, dh_ref, dhb_ref, dg_ref, loss_ref):
        loss, vjp = jax.vjp(f_final, h1_ref[...], m3_ref[...], tgt_ref[...], g_ref[...])
        dh, _, _, dg = vjp(jnp.ones((), F32))
        dh_ref[...] = dh
        dhb_ref[...] = dh.astype(BF16)

        @pl.when(pl.program_id(0) == 0)
        def _():
            dg_ref[...] = jnp.zeros_like(dg_ref)
            loss_ref[...] = jnp.zeros_like(loss_ref)

        dg_ref[...] += dg
        loss_ref[...] += jnp.full(loss_ref.shape, loss, F32)

    tile = _tile_spec(tm, D_MODEL, 0)
    return pl.pallas_call(
        body,
        name="final_loss",
        grid=(t // tm,),
        in_specs=[tile, tile, tile, _const_spec(g_final)],
        out_specs=[tile, tile, _const_spec(g_final), pl.BlockSpec((8, 128), lambda i: (0, 0))],
        out_shape=[jax.ShapeDtypeStruct((t, D_MODEL), F32), jax.ShapeDtypeStruct((t, D_MODEL), BF16),
                   jax.ShapeDtypeStruct(g_final.shape, F32), jax.ShapeDtypeStruct((8, 128), F32)],
        compiler_params=_cparams(1),
    )(h1, m3, tgt, g_final)


N_SGU = 2048
N_RWKV = 3360
LORA_W, LORA_A, LORA_G = 64, 64, 160


def _pad_rwkv_cols(z):
    zero = lambda n: jnp.zeros(z.shape[:-1] + (n,), z.dtype)
    return jnp.concatenate([z[..., :3072], z[..., 3072:3136], zero(64), z[..., 3136:3200], zero(64),
                            z[..., 3200:3360], zero(96)], axis=-1)


def _unpad_rwkv_cols(z):
    return jnp.concatenate([z[..., :3072], z[..., 3072:3136], z[..., 3200:3264], z[..., 3328:3488]], axis=-1)


def _pad_win(w):
    return jnp.concatenate([w[:, :N_SGU], w[:, N_SGU + N_RWKV:], _pad_rwkv_cols(w[:, N_SGU:N_SGU + N_RWKV])], axis=1)


def _unpad_win(w):
    return jnp.concatenate([w[:, :N_SGU], _unpad_rwkv_cols(w[:, RWKV_COL0:]), w[:, N_SGU:RWKV_COL0]], axis=1)


def _pad_rows(w, n):
    return jnp.concatenate([w, jnp.zeros((n - w.shape[0],) + w.shape[1:], w.dtype)], axis=0)


def _relu2_epi(c):
    return c, jnp.square(jnp.maximum(c, 0.0))


def _relu2_bwd_epi(c, hid):
    return (c * (2.0 * jnp.maximum(hid, 0.0)),)


def _add_epi(c, x):
    return (c + x,)


def _pre_fwd(*args):
    res = f_pre(*args)
    return res[1], res[2], res[4], res[5], res[6]


def local_step(x, tgt, w):
    d = D_MODEL
    win_p = _pad_win(w["w_in"])
    sbp = _pad_rwkv_cols(w["shift_b"])
    wl = _pad_rows(w["w_lora_w"], 128)
    al = _pad_rows(w["a_lora_w"], 128)
    gl = _pad_rows(w["g_lora_w"], 256)
    sbt = w["sgu_b"].T

    (a_bf,) = ew_call(lambda x_, g_: (f_norm_in(x_, g_)[0],), [(x, d, 0)], [w["g_mix"]], [(d, BF16)], tm=256,
                      name="norm_in")
    p_all = mm(a_bf, win_p, "nn", tm=512, tn=1280, name="mm_in")
    sgu_t = [(p_all, 2 * d, 0)]
    sgu_c = [w["sgu_ln_w"], w["sgu_ln_b"], w["sgu_w"], sbt]
    (s_bf,) = ew_call(f_sgu, sgu_t, sgu_c, [(d, BF16)], tm=256, name="sgu_fwd")
    ya = mm(s_bf, w["w_proj_a"], "nn", tm=512, tn=1024, name="mm_proj_a")
    q = shiftmix_fwd(p_all, sbp, tm=256)
    pre_t = [(q, d, 0), (q, d, 1), (q, d, 2), (q, 512, 6)]
    pre_c = [wl, w["w0"], al, w["a0"], gl, w["k_k"], w["k_a"]]
    lw, kp, na, nb, g = ew_call(_pre_fwd, pre_t, pre_c, [(d, F32)] * 5, tm=256, name="rwkv_pre_fwd")
    scan_ops = [(q, 0), (lw, 0), (kp, 0), (q, 2), (na, 0), (nb, 0)]
    o, s0s = scan_fwd(scan_ops)
    post_t = [(o, d, 0), (q, d, 0), (kp, d, 0), (q, d, 2), (g, d, 0)]
    post_c = [w["ln_x_w"], w["ln_x_b"], w["r_k"]]
    (ob_bf,) = ew_call(f_post, post_t, post_c, [(d, BF16)], tm=256, name="rwkv_post_fwd")
    yb = mm(ob_bf, w["w_proj_b"], "nn", tm=512, tn=1024, name="mm_proj_b")
    mix_t = [(ya, d, 0), (yb, d, 0), (p_all, d, 2), (p_all, d, 3)]
    (mixed_bf,) = ew_call(f_mix, mix_t, [], [(d, BF16)], tm=256, name="mix_fwd")
    h1 = mm(mixed_bf, w["w_out"], "nn", tm=512, tn=1024, name="mm_out", epi=_add_epi, extras=(x,))
    (f_bf,) = ew_call(lambda h_, g_: (f_ffn_in(h_, g_)[0],), [(h1, d, 0)], [w["g_ffn"]], [(d, BF16)], tm=256,
                      name="ffn_norm")
    hid, act_bf = mm(f_bf, w["w_ffn1"], "nn", tm=512, tn=1024, name="mm_ffn1", out_dtypes=(F32, BF16), epi=_relu2_epi)
    m3 = mm(act_bf, w["w_ffn2"], "nn", tm=512, tn=1024, name="mm_ffn2")
    dh2, dh2_bf, dg_final, loss = final_call(h1, m3, tgt, w["g_final"], tm=256)

    dhid_bf = mm(dh2_bf, w["w_ffn2"], "nt", tm=512, tn=1024, name="mm_dact", out_dtypes=(BF16,), epi=_relu2_bwd_epi,
                 extras=(hid,))
    d_ffn2 = mm(act_bf, dh2_bf, "tn", tm=512, tn=1024, name="mm_dw_ffn2")
    df = mm(dhid_bf, w["w_ffn1"], "nt", tm=512, tn=1024, name="mm_df")
    d_ffn1 = mm(f_bf, dhid_bf, "tn", tm=512, tn=1024, name="mm_dw_ffn1")
    (dh1, dh1_bf), (dg_ffn,) = ew_vjp_call(f_ffn_in, [(h1, d, 0)], [w["g_ffn"]], [(df, d, 0), (dh2, d, 0)],
                                           [(F32, BF16)], [True], tm=256, name="ffn_norm_bwd")
    dmixed = mm(dh1_bf, w["w_out"], "nt", tm=512, tn=1024, name="mm_dmixed")
    d_out = mm(mixed_bf, dh1_bf, "tn", tm=512, tn=1024, name="mm_dw_out")
    (dya_bf, dyb_bf, dga_bf, dgb_bf), _ = ew_vjp_call(f_mix, mix_t, [], [(dmixed, d, 0)], [(BF16,)] * 4, [], tm=256,
                                                      name="mix_bwd")
    dob = mm(dyb_bf, w["w_proj_b"], "nt", tm=512, tn=1024, name="mm_dob")
    d_proj_b = mm(ob_bf, dyb_bf, "tn", tm=512, tn=1024, name="mm_dw_proj_b")
    (do, dr_p, dkp_p, dv_p, dg), (dlnx_w, dlnx_b, dr_k) = ew_vjp_call(
        f_post, post_t, post_c, [(dob, d, 0)], [(F32,)] * 5, [True] * 3, tm=256, name="rwkv_post_bwd")
    scan_g = scan_bwd(scan_ops, s0s, do)
    pre_g = [(z, d, 0) for z in scan_g] + [(dg, d, 0), (dr_p, d, 0), (dkp_p, d, 0), (dv_p, d, 0)]
    (dq_r, dq_k, dq_v, dq_l), (dwl, dw0, dal, da0, dgl, dk_k, dk_a) = ew_vjp_call(
        f_pre, pre_t, pre_c, pre_g, [(F32,)] * 4, [True] * 7, tm=128, name="rwkv_pre_bwd")
    dp_r, dsb_r = shiftmix_bwd(dq_r, 0, p_all, sbp, tm=256, name="shiftmix_bwd_r")
    dp_k, dsb_k = shiftmix_bwd(dq_k, d, p_all, sbp, tm=256, name="shiftmix_bwd_k")
    dp_v, dsb_v = shiftmix_bwd(dq_v, 2 * d, p_all, sbp, tm=256, name="shiftmix_bwd_v")
    dp_l, dsb_l = shiftmix_bwd(dq_l, 3 * d, p_all, sbp, tm=256, name="shiftmix_bwd_l")
    ds = mm(dya_bf, w["w_proj_a"], "nt", tm=512, tn=1024, name="mm_ds")
    d_proj_a = mm(s_bf, dya_bf, "tn", tm=512, tn=1024, name="mm_dw_proj_a")
    (dp_sgu,), (dln_w, dln_b, dsw, dsbt) = ew_vjp_call(f_sgu, sgu_t, sgu_c, [(ds, d, 0)], [(BF16,)], [True] * 4,
                                                       tm=256, name="sgu_bwd")
    dp_all = jnp.concatenate([dp_sgu, dga_bf, dgb_bf, dp_r, dp_k, dp_v, dp_l], axis=1)
    da = mm(dp_all, win_p, "nt", tm=512, tn=512, name="mm_da")
    d_in_p = mm(a_bf, dp_all, "tn", tm=512, tn=1280, name="mm_dw_in")
    (grad_x,), (dg_mix,) = ew_vjp_call(f_norm_in, [(x, d, 0)], [w["g_mix"]], [(da, d, 0), (dh1, d, 0)], [(F32,)],
                                       [True], tm=256, name="norm_in_bwd")

    grads = {
        "g_mix": dg_mix, "w_in": _unpad_win(d_in_p), "sgu_ln_w": dln_w, "sgu_ln_b": dln_b, "sgu_w": dsw,
        "sgu_b": dsbt.T, "w_proj_a": d_proj_a,
        "shift_b": _unpad_rwkv_cols(jnp.concatenate([dsb_r, dsb_k, dsb_v, dsb_l], axis=1)),
        "w_lora_w": dwl[:LORA_W], "w0": dw0, "a_lora_w": dal[:LORA_A], "a0": da0, "g_lora_w": dgl[:LORA_G],
        "k_k": dk_k, "k_a": dk_a, "r_k": dr_k, "ln_x_w": dlnx_w, "ln_x_b": dlnx_b, "w_proj_b": d_proj_b,
        "w_out": d_out, "g_ffn": dg_ffn, "w_ffn1": d_ffn1, "w_ffn2": d_ffn2, "g_final": dg_final,
    }
    return loss[0, 0], grad_x, grads


MESH = pl.DeviceIdType.MESH
N_CHIPS = 4
N_DEV = 8
PACK_ROWS = 5152
HALF_ROWS = PACK_ROWS // 2
GATHER_ROWS = 5472
PACK_TILE = 368
SMALL_ROWS = 152
_ANY = pl.BlockSpec(memory_space=pl.ANY)


def _coords():
    return lax.axis_index("x"), lax.axis_index("y"), lax.axis_index("c")


def _other_chips(x, y):
    return [(1 - x, y), (x, 1 - y), (1 - x, 1 - y)]


def _remote(src, dst, send_sems, recv_sems, k, to):
    return pltpu.make_async_remote_copy(src_ref=src, dst_ref=dst, send_sem=send_sems.at[k], recv_sem=recv_sems.at[k],
                                        device_id=to, device_id_type=MESH)


def gather_shards(pack):
    def body(src_ref, out_ref, send_sems, recv_sems):
        x, y, c = _coords()
        me = 2 * x + y
        sib = (x, y, 1 - c)
        chips = _other_chips(x, y)
        first = [_remote(src_ref.at[c], out_ref.at[me, c], send_sems, recv_sems, k, (cx, cy, c))
                 for k, (cx, cy) in enumerate(chips)]
        for cp in first:
            cp.start()
        passed = []
        for k, (cx, cy) in enumerate(chips):
            j = 2 * cx + cy
            _remote(src_ref.at[c], out_ref.at[j, c], send_sems, recv_sems, k, (cx, cy, c)).wait_recv()
            fwd = _remote(out_ref.at[j, c], out_ref.at[j, c], send_sems, recv_sems, 3 + k, sib)
            fwd.start()
            passed.append(fwd)
        for k, (cx, cy) in enumerate(chips):
            j = 2 * cx + cy
            _remote(out_ref.at[j, 1 - c], out_ref.at[j, 1 - c], send_sems, recv_sems, 3 + k, sib).wait_recv()
        for cp in first + passed:
            cp.wait_send()

    return pl.pallas_call(
        body,
        name="gather_shards",
        in_specs=[_ANY],
        out_specs=_ANY,
        out_shape=jax.ShapeDtypeStruct((N_CHIPS,) + pack.shape, pack.dtype),
        scratch_shapes=[pltpu.SemaphoreType.DMA((6,)), pltpu.SemaphoreType.DMA((6,))],
    )(pack)


def reduce_pair(g):
    def body(g_ref, got_ref, send_sems, recv_sems):
        x, y, c = _coords()
        sib = (x, y, 1 - c)
        sends = [_remote(g_ref.at[j, 1 - c], got_ref.at[j], send_sems, recv_sems, j, sib) for j in range(N_CHIPS)]
        for cp in sends:
            cp.start()
        for cp in sends:
            cp.wait_recv()
        for cp in sends:
            cp.wait_send()

    return pl.pallas_call(
        body,
        name="reduce_pair",
        in_specs=[_ANY],
        out_specs=_ANY,
        out_shape=jax.ShapeDtypeStruct((N_CHIPS,) + g.shape[2:], g.dtype),
        scratch_shapes=[pltpu.SemaphoreType.DMA((N_CHIPS,)), pltpu.SemaphoreType.DMA((N_CHIPS,))],
    )(g)


def pair_sum(g, got, *, tm):
    n, _, rows, width = g.shape

    def body(g0_ref, g1_ref, got_ref, out_ref, out16_ref):
        own = jnp.where(lax.axis_index("c") == 0, g0_ref[0, 0], g1_ref[0, 0])
        total = own + got_ref[0]
        out_ref[0] = total
        out16_ref[0] = total.astype(BF16)

    blk = pl.BlockSpec((1, tm, width), lambda j, i: (j, i, 0))
    return pl.pallas_call(
        body,
        name="pair_sum",
        grid=(n, rows // tm),
        in_specs=[pl.BlockSpec((1, 1, tm, width), lambda j, i: (j, 0, i, 0)),
                  pl.BlockSpec((1, 1, tm, width), lambda j, i: (j, 1, i, 0)), blk],
        out_specs=[blk, blk],
        out_shape=[jax.ShapeDtypeStruct(got.shape, F32), jax.ShapeDtypeStruct(got.shape, BF16)],
        compiler_params=_cparams(2),
    )(g, g, got)


def reduce_chips(p):
    def body(p_ref, out_ref, send_sems, recv_sems):
        x, y, c = _coords()
        me = 2 * x + y
        chips = _other_chips(x, y)
        sends = [_remote(p_ref.at[2 * cx + cy], out_ref.at[me], send_sems, recv_sems, k, (cx, cy, c))
                 for k, (cx, cy) in enumerate(chips)]
        for cp in sends:
            cp.start()
        for k, (cx, cy) in enumerate(chips):
            _remote(p_ref.at[me], out_ref.at[2 * cx + cy], send_sems, recv_sems, k, (cx, cy, c)).wait_recv()
        for cp in sends:
            cp.wait_send()

    return pl.pallas_call(
        body,
        name="reduce_chips",
        in_specs=[_ANY],
        out_specs=_ANY,
        out_shape=jax.ShapeDtypeStruct(p.shape, p.dtype),
        scratch_shapes=[pltpu.SemaphoreType.DMA((3,)), pltpu.SemaphoreType.DMA((3,))],
    )(p)


def sum_with_own(own, slots, index_fn, *, tm, name):
    n, rows, width = slots.shape
    own3 = own.ndim == 3

    def body(*refs):
        mine = index_fn()
        acc = None
        for s in range(n):
            o = refs[s][0] if own3 else refs[0][...]
            term = jnp.where(mine == s, o, refs[(n if own3 else 1) + s][0].astype(F32))
            acc = term if acc is None else acc + term
        refs[-1][...] = acc

    slot_specs = [pl.BlockSpec((1, tm, width), lambda i, s=s: (s, i, 0)) for s in range(n)]
    own_specs = slot_specs if own3 else [pl.BlockSpec((tm, width), lambda i: (i, 0))]
    return pl.pallas_call(
        body,
        name=name,
        grid=(rows // tm,),
        in_specs=own_specs + slot_specs,
        out_specs=pl.BlockSpec((tm, width), lambda i: (i, 0)),
        out_shape=jax.ShapeDtypeStruct((rows, width), F32),
        compiler_params=_cparams(1),
    )(*([own] * (n if own3 else 1)), *([slots] * n))


def exchange_halves(s):
    rq = PACK_TILE
    nq = s.shape[0] // rq

    def body(s_ref, out_ref, sbuf, rbuf, send_sems, recv_sems, in_sems, out_sems):
        x, y, c = _coords()
        sib = (x, y, 1 - c)
        rows = lambda q: pl.ds(q * rq, rq)
        loads = [pltpu.make_async_copy(s_ref.at[rows(q)], sbuf.at[rows(q)], in_sems.at[q]) for q in range(nq)]
        for cp in loads:
            cp.start()
        sends = []
        for q in range(nq):
            loads[q].wait()
            sends.append(_remote(sbuf.at[rows(q)], rbuf.at[rows(q)], send_sems, recv_sems, q, sib))
            sends[q].start()
        stores = []
        for q in range(nq):
            sends[q].wait_recv()
            stores.append(pltpu.make_async_copy(rbuf.at[rows(q)], out_ref.at[rows(q)], out_sems.at[q]))
            stores[q].start()
        for cp in sends:
            cp.wait_send()
        for cp in stores:
            cp.wait()

    return pl.pallas_call(
        body,
        name="exchange_halves",
        in_specs=[_ANY],
        out_specs=_ANY,
        out_shape=jax.ShapeDtypeStruct(s.shape, s.dtype),
        scratch_shapes=[pltpu.VMEM(s.shape, s.dtype), pltpu.VMEM(s.shape, s.dtype)]
        + [pltpu.SemaphoreType.DMA((nq,))] * 4,
        compiler_params=pltpu.CompilerParams(vmem_limit_bytes=VMEM_LIMIT),
    )(s)


def gather_all(s):
    def body(s_ref, out_ref, send_sems, recv_sems):
        x, y, c = _coords()
        me = 4 * x + 2 * y + c
        peers = []
        for mask in range(1, N_DEV):
            px = 1 - x if mask & 4 else x
            py = 1 - y if mask & 2 else y
            pc = 1 - c if mask & 1 else c
            peers.append((px, py, pc))
        sends = [_remote(s_ref, out_ref.at[me], send_sems, recv_sems, k, peer) for k, peer in enumerate(peers)]
        for cp in sends:
            cp.start()
        for k, (px, py, pc) in enumerate(peers):
            _remote(s_ref, out_ref.at[4 * px + 2 * py + pc], send_sems, recv_sems, k, (px, py, pc)).wait_recv()
        for cp in sends:
            cp.wait_send()

    return pl.pallas_call(
        body,
        name="gather_all",
        in_specs=[_ANY],
        out_specs=_ANY,
        out_shape=jax.ShapeDtypeStruct((N_DEV,) + s.shape, s.dtype),
        scratch_shapes=[pltpu.SemaphoreType.DMA((N_DEV - 1,)), pltpu.SemaphoreType.DMA((N_DEV - 1,))],
    )(s)


ADAM_LR = 0.001
ADAM_B1 = 0.9
ADAM_B2 = 0.999
ADAM_EPS = 1e-08
ADAM_WD = 0.01
ADAM_STEP = 10


def f_adamw(g, w, m, v):
    m = ADAM_B1 * m + (1.0 - ADAM_B1) * g
    v = ADAM_B2 * v + (1.0 - ADAM_B2) * jnp.square(g)
    m_hat = m / (1.0 - ADAM_B1 ** ADAM_STEP)
    v_hat = v / (1.0 - ADAM_B2 ** ADAM_STEP)
    delta = -ADAM_LR * (m_hat / (jnp.sqrt(v_hat) + ADAM_EPS) + ADAM_WD * w)
    return delta, m, v


def adamw_call(g, w, m, v, *, tm, name):
    width = g.shape[1]
    return ew_call(f_adamw, [(g, width, 0), (w, width, 0), (m, width, 0), (v, width, 0)], [], [(width, F32)] * 3,
                   tm=tm, name=name)


def adamw_halves(g_own, g_other, w, m, v, *, tm):
    _, rows, width = w.shape

    def body(go_ref, gx_ref, w_ref, m_ref, v_ref, g_ref, d_ref, nm_ref, nv_ref):
        g = jnp.where(pl.program_id(0) == lax.axis_index("c"), go_ref[...], gx_ref[...])
        delta, nm, nv = f_adamw(g, w_ref[0], m_ref[0], v_ref[0])
        g_ref[0] = g
        d_ref[0] = delta
        nm_ref[0] = nm
        nv_ref[0] = nv

    half = pl.BlockSpec((tm, width), lambda h, i: (i, 0))
    full = pl.BlockSpec((1, tm, width), lambda h, i: (h, i, 0))
    return pl.pallas_call(
        body,
        name="adamw_sharded",
        grid=(2, rows // tm),
        in_specs=[half, half, full, full, full],
        out_specs=[full] * 4,
        out_shape=[jax.ShapeDtypeStruct(w.shape, F32)] * 4,
        compiler_params=_cparams(2),
    )(g_own, g_other, w, m, v)


SHARDED = ["w_in", "w_proj_a", "w_proj_b", "w_out", "w_ffn1", "w_ffn2", "w_lora_w", "a_lora_w", "g_lora_w"]
LORAS = ["w_lora_w", "a_lora_w", "g_lora_w"]
SHARD_ROWS = {"w_in": 2048, "w_proj_a": 256, "w_proj_b": 256, "w_out": 256, "w_ffn1": 1024, "w_ffn2": 1024,
              "w_lora_w": 64, "a_lora_w": 64, "g_lora_w": 160}
SHARD_AXIS = {"w_in": 1, "w_proj_a": 0, "w_lora_w": 1, "a_lora_w": 1, "g_lora_w": 1, "w_proj_b": 0, "w_out": 0,
              "w_ffn1": 1, "w_ffn2": 0}
SHARD_SHAPE = {"w_in": (1024, 1864), "w_proj_a": (256, 1024), "w_lora_w": (64, 256), "a_lora_w": (64, 256),
               "g_lora_w": (160, 256), "w_proj_b": (256, 1024), "w_out": (256, 1024), "w_ffn1": (1024, 1024),
               "w_ffn2": (1024, 1024)}
SHIFT_SHARD = (2, 840)
VECTORS = ["g_mix", "sgu_ln_w", "sgu_ln_b", "w0", "a0", "k_k", "k_a", "r_k", "ln_x_w", "ln_x_b", "g_ffn", "g_final"]
SMALL = VECTORS + ["sgu_w", "sgu_b"]
SMALL_SHAPE = {**{n: (1, 1024) for n in VECTORS}, "sgu_w": (8, 128, 128), "sgu_b": (8, 128)}
WEIGHTS = ["g_mix", "w_in", "sgu_ln_w", "sgu_ln_b", "sgu_w", "sgu_b", "w_proj_a", "shift_b", "w_lora_w", "w0",
           "a_lora_w", "a0", "g_lora_w", "k_k", "k_a", "r_k", "ln_x_w", "ln_x_b", "w_proj_b", "w_out", "g_ffn",
           "w_ffn1", "w_ffn2", "g_final"]


def _size(shape):
    n = 1
    for s in shape:
        n *= s
    return n


def _pack_rows(parts, rows, dtype):
    flat = jnp.concatenate([p.reshape(-1).astype(dtype) for p in parts])
    return jnp.concatenate([flat, jnp.zeros((rows * 1024 - flat.shape[0],), dtype)]).reshape(rows, 1024)


def _unpack_rows(packed, shapes):
    flat = packed.reshape(-1)
    out, off = [], 0
    for shp in shapes:
        out.append(flat[off:off + _size(shp)].reshape(shp))
        off += _size(shp)
    return out


def _shard_of(name, full, j):
    ax = SHARD_AXIS[name]
    n = SHARD_SHAPE[name][ax]
    return lax.slice_in_dim(full, j * n, (j + 1) * n, axis=ax)


def _pad_cols(z, n=1024):
    return jnp.concatenate([z, jnp.zeros((z.shape[0], n - z.shape[1]), z.dtype)], axis=1)


def _shard_rows(name, s):
    if name == "w_in":
        return jnp.concatenate([s[:, :1024], _pad_cols(s[:, 1024:])], axis=0)
    return s if s.shape[1] == 1024 else _pad_cols(s)


def _rows_shard(name, r):
    if name == "w_in":
        return jnp.concatenate([r[:1024], r[1024:, :SHARD_SHAPE[name][1] - 1024]], axis=1)
    return r[:, :SHARD_SHAPE[name][1]]


def _pack_shard(shard_fn, dtype):
    return jnp.concatenate([_shard_rows(n, shard_fn(n)).astype(dtype) for n in SHARDED], axis=0)


def _unpack_shard(packed):
    out, off = {}, 0
    for n in SHARDED:
        out[n] = _rows_shard(n, packed[off:off + SHARD_ROWS[n]])
        off += SHARD_ROWS[n]
    return out


def kernel(x, g_mix, w_in, sgu_ln_w, sgu_ln_b, sgu_w, sgu_b, w_proj_a, shift_b, w_lora_w, w0, a_lora_w, a0, g_lora_w, k_k, k_a, r_k, ln_x_w, ln_x_b, w_proj_b, w_out, g_ffn, w_ffn1, w_ffn2, g_final, loss_target, m_g_mix, m_w_in, m_sgu_ln_w, m_sgu_ln_b, m_sgu_w, m_sgu_b, m_w_proj_a, m_shift_b, m_w_lora_w, m_w0, m_a_lora_w, m_a0, m_g_lora_w, m_k_k, m_k_a, m_r_k, m_ln_x_w, m_ln_x_b, m_w_proj_b, m_w_out, m_g_ffn, m_w_ffn1, m_w_ffn2, m_g_final, v_g_mix, v_w_in, v_sgu_ln_w, v_sgu_ln_b, v_sgu_w, v_sgu_b, v_w_proj_a, v_shift_b, v_w_lora_w, v_w0, v_a_lora_w, v_a0, v_g_lora_w, v_k_k, v_k_a, v_r_k, v_ln_x_w, v_ln_x_b, v_w_proj_b, v_w_out, v_g_ffn, v_w_ffn1, v_w_ffn2, v_g_final):
    given = dict(zip(WEIGHTS, (g_mix, w_in, sgu_ln_w, sgu_ln_b, sgu_w, sgu_b, w_proj_a, shift_b, w_lora_w, w0, a_lora_w, a0, g_lora_w, k_k, k_a, r_k, ln_x_w, ln_x_b, w_proj_b, w_out, g_ffn, w_ffn1, w_ffn2, g_final)))
    mom_m = dict(zip(WEIGHTS, (m_g_mix, m_w_in, m_sgu_ln_w, m_sgu_ln_b, m_sgu_w, m_sgu_b, m_w_proj_a, m_shift_b, m_w_lora_w, m_w0, m_a_lora_w, m_a0, m_g_lora_w, m_k_k, m_k_a, m_r_k, m_ln_x_w, m_ln_x_b, m_w_proj_b, m_w_out, m_g_ffn, m_w_ffn1, m_w_ffn2, m_g_final)))
    mom_v = dict(zip(WEIGHTS, (v_g_mix, v_w_in, v_sgu_ln_w, v_sgu_ln_b, v_sgu_w, v_sgu_b, v_w_proj_a, v_shift_b, v_w_lora_w, v_w0, v_a_lora_w, v_a0, v_g_lora_w, v_k_k, v_k_a, v_r_k, v_ln_x_w, v_ln_x_b, v_w_proj_b, v_w_out, v_g_ffn, v_w_ffn1, v_w_ffn2, v_g_final)))
    chip = 2 * lax.axis_index("x") + lax.axis_index("y")

    def local_block(tree, n):
        return tree[n] if n == "g_final" else tree[n][0]

    sb = local_block(given, "shift_b")
    lo_part = lambda z: (z - z.astype(BF16).astype(F32)).astype(BF16)
    extra = [_pad_cols(lo_part(local_block(given, n))) for n in LORAS]
    tile16 = lambda z: jnp.pad(z, ((0, 16 - z.shape[0]), (0, 1024 - z.shape[1])))
    extra += [tile16(sb.astype(BF16)), tile16(lo_part(sb))]
    pack_w = jnp.concatenate([_pack_shard(lambda n: local_block(given, n), BF16)] + extra, axis=0)
    gathered = gather_shards(pack_w.reshape(2, GATHER_ROWS // 2, 1024)).reshape(N_CHIPS, GATHER_ROWS, 1024)
    per_chip = [jnp.where(chip == j, pack_w, gathered[j]) for j in range(N_CHIPS)]
    shards = [_unpack_shard(p) for p in per_chip]
    w = {n: jnp.concatenate([shards[j][n] for j in range(N_CHIPS)], axis=SHARD_AXIS[n]) for n in SHARDED}
    off = PACK_ROWS
    for n in LORAS:
        rows, cols = SHARD_SHAPE[n]
        lo = jnp.concatenate([p[off:off + rows, :cols] for p in per_chip], axis=1)
        w[n] = w[n].astype(F32) + lo.astype(F32)
        off += rows
    w["shift_b"] = jnp.concatenate(
        [p[off:off + 2, :SHIFT_SHARD[1]].astype(F32) + p[off + 16:off + 18, :SHIFT_SHARD[1]].astype(F32)
         for p in per_chip], axis=1)
    for n in SMALL:
        w[n] = local_block(given, n).reshape(SMALL_SHAPE[n])

    loss, grad_x, grads = local_step(x[0], loss_target[0], w)
    loss = lax.psum(loss, ("x", "y", "c"))

    g_pack = jnp.stack([_pack_shard(lambda n: _shard_of(n, grads[n], j), F32) for j in range(N_CHIPS)])
    g_pack = g_pack.reshape(N_CHIPS, 2, HALF_ROWS, 1024)
    chip_part, chip_part16 = pair_sum(g_pack, reduce_pair(g_pack), tm=PACK_TILE)
    half_sum = sum_with_own(chip_part, reduce_chips(chip_part16),
                            lambda: 2 * lax.axis_index("x") + lax.axis_index("y"), tm=PACK_TILE, name="chip_sum")
    other_half = exchange_halves(half_sum)
    halves = lambda tree: _pack_shard(lambda n: local_block(tree, n), F32).reshape(2, HALF_ROWS, 1024)
    big = adamw_halves(half_sum, other_half, halves(given), halves(mom_m), halves(mom_v), tm=PACK_TILE)
    out_g, out_d, out_m, out_v = [_unpack_shard(z.reshape(PACK_ROWS, 1024)) for z in big]

    small_shapes = [SMALL_SHAPE[n] for n in SMALL]
    s_pack = _pack_rows([grads[n] for n in SMALL] + [grads["shift_b"]], SMALL_ROWS, F32)
    g_small = sum_with_own(
        s_pack, gather_all(s_pack), lambda: 4 * lax.axis_index("x") + 2 * lax.axis_index("y") + lax.axis_index("c"),
        tm=SMALL_ROWS, name="small_sum")
    w_small = _pack_rows([local_block(given, n) for n in SMALL], SMALL_ROWS, F32)
    m_small = _pack_rows([local_block(mom_m, n) for n in SMALL], SMALL_ROWS, F32)
    v_small = _pack_rows([local_block(mom_v, n) for n in SMALL], SMALL_ROWS, F32)
    d_small, nm_small, nv_small = adamw_call(g_small, w_small, m_small, v_small, tm=SMALL_ROWS, name="adamw_small")
    g_parts = _unpack_rows(g_small, small_shapes + [(2, N_RWKV)])
    out_g.update(zip(SMALL, g_parts[:-1]))
    out_d.update(zip(SMALL, _unpack_rows(d_small, small_shapes)))
    out_m.update(zip(SMALL, _unpack_rows(nm_small, small_shapes)))
    out_v.update(zip(SMALL, _unpack_rows(nv_small, small_shapes)))
    g_sb = lax.dynamic_slice_in_dim(g_parts[-1], chip * SHIFT_SHARD[1], SHIFT_SHARD[1], axis=1)
    sb_args = [_pack_rows([z], 8, F32) for z in (g_sb, sb, local_block(mom_m, "shift_b"), local_block(mom_v, "shift_b"))]
    sb_res = adamw_call(*sb_args, tm=8, name="adamw_shift_b")
    out_g["shift_b"] = g_sb
    for tree, res in zip((out_d, out_m, out_v), sb_res):
        tree["shift_b"] = _unpack_rows(res, [SHIFT_SHARD])[0]

    def block_of(tree, n):
        return tree[n].reshape(given[n].shape)

    return (loss, grad_x[None], *[block_of(out_g, n) for n in WEIGHTS], *[block_of(out_d, n) for n in WEIGHTS],
            *[block_of(out_m, n) for n in WEIGHTS], *[block_of(out_v, n) for n in WEIGHTS])
```

```python
import functools

import jax
import jax.numpy as jnp
from jax import lax
from jax.experimental import pallas as pl
from jax.experimental.pallas import tpu as pltpu

F32 = jnp.float32
BF16 = jnp.bfloat16

D_MODEL = 1024
N_HEADS = 16
HEAD = 64
SCAN_CHUNK = 64

VMEM_LIMIT = 56 * 1024 * 1024


_BDIMS = {
    "nn": (((2,), (1,)), ((0,), (0,))),
    "nt": (((2,), (2,)), ((0,), (0,))),
    "tn": (((1,), (1,)), ((0,), (0,))),
}


def _raw_bdot(x, y, mode):
    return lax.dot_general(x, y, _BDIMS[mode], precision=lax.Precision.HIGH, preferred_element_type=F32)


@functools.partial(jax.custom_vjp, nondiff_argnums=(2,))
def bdot(x, y, mode):
    return _raw_bdot(x, y, mode)


def _bdot_fwd(x, y, mode):
    return _raw_bdot(x, y, mode), (x, y)


def _bdot_bwd(mode, res, g):
    x, y = res
    if mode == "nn":
        return bdot(g, y, "nt"), bdot(x, g, "tn")
    if mode == "nt":
        return bdot(g, y, "nn"), bdot(g, x, "tn")
    return bdot(y, g, "nt"), bdot(x, g, "nn")


bdot.defvjp(_bdot_fwd, _bdot_bwd)


def _scan_chunk(S0, r, lw, k, v, a, b):
    nh, lc, _ = r.shape
    ti = lax.broadcasted_iota(jnp.int32, (lc, lc), 0)
    si = lax.broadcasted_iota(jnp.int32, (lc, lc), 1)
    incl = (si <= ti).astype(F32)
    strict = (si < ti).astype(F32)
    eye = (si == ti).astype(F32)
    cl = bdot(jnp.broadcast_to(incl, (nh, lc, lc)), lw, "nn")
    cl_last = cl[:, lc - 1:lc, :]
    g_last = jnp.exp(cl_last - cl)
    at = a * jnp.exp(cl - lw)
    bt = b * jnp.exp(-cl)
    kt = k * jnp.exp(-cl)
    rt = r * jnp.exp(cl)
    ar = jnp.concatenate([at, rt], axis=1)
    ar_b = bdot(ar, bt, "nt")
    ar_k = bdot(ar, kt, "nt")
    m_ab, m_rb = ar_b[:, :lc] * strict, ar_b[:, lc:] * incl
    m_ak, m_rk = ar_k[:, :lc] * strict, ar_k[:, lc:] * incl
    x = eye + m_ab
    p = bdot(m_ab, m_ab, "nn")
    n = 2
    while n * 2 < lc:
        px = bdot(jnp.concatenate([p, x], axis=1), p, "nn")
        p = px[:, :lc]
        x = x + px[:, lc:]
        n *= 2
    x = x + bdot(x, p, "nn")
    ar_s = bdot(ar, S0, "nt")
    akrk_v = bdot(jnp.concatenate([m_ak, m_rk], axis=1), v, "nn")
    u = bdot(x, ar_s[:, :lc] + akrk_v[:, :lc], "nn")
    o = ar_s[:, lc:] + bdot(m_rb, u, "nn") + akrk_v[:, lc:]
    s_last = S0 * jnp.exp(cl_last) + bdot(jnp.concatenate([u, v], axis=1),
                                          jnp.concatenate([b * g_last, k * g_last], axis=1), "tn")
    return o, s_last


def _split_heads(z):
    return jnp.stack([z[:, HEAD * h:HEAD * (h + 1)] for h in range(N_HEADS)], axis=0)


def _merge_heads(z):
    return jnp.concatenate([z[h] for h in range(N_HEADS)], axis=1)


def _scan_specs(t, ops, rev):
    nc = t // SCAN_CHUNK
    row = (lambda c: nc - 1 - c) if rev else (lambda c: c)
    specs = [pl.BlockSpec((SCAN_CHUNK, D_MODEL), lambda c, cb=cb: (row(c), cb)) for _, cb in ops]
    state = pl.BlockSpec((1, N_HEADS, HEAD, HEAD), lambda c: (row(c), 0, 0, 0))
    return nc, specs, state


def scan_fwd(ops):
    t = ops[0][0].shape[0]
    nc, specs, state = _scan_specs(t, ops, False)

    def body(r_ref, lw_ref, k_ref, v_ref, a_ref, b_ref, o_ref, s0_ref, s_scr):
        @pl.when(pl.program_id(0) == 0)
        def _():
            s_scr[...] = jnp.zeros_like(s_scr)

        s0 = s_scr[...]
        s0_ref[0] = s0
        o, s_last = _scan_chunk(s0, *[_split_heads(z[...]) for z in (r_ref, lw_ref, k_ref, v_ref, a_ref, b_ref)])
        o_ref[...] = _merge_heads(o)
        s_scr[...] = s_last

    return pl.pallas_call(
        body,
        name="scan_fwd",
        grid=(nc,),
        in_specs=specs,
        out_specs=[pl.BlockSpec((SCAN_CHUNK, D_MODEL), lambda c: (c, 0)), state],
        out_shape=[jax.ShapeDtypeStruct((t, D_MODEL), F32), jax.ShapeDtypeStruct((nc, N_HEADS, HEAD, HEAD), F32)],
        scratch_shapes=[pltpu.VMEM((N_HEADS, HEAD, HEAD), F32)],
        compiler_params=_cparams(1),
    )(*[a for a, _ in ops])


def scan_bwd(ops, s0s, do):
    t = ops[0][0].shape[0]
    nc, specs, state = _scan_specs(t, ops + [(do, 0)], True)

    def body(r_ref, lw_ref, k_ref, v_ref, a_ref, b_ref, do_ref, s0_ref, *rest):
        out_refs, ds_scr = rest[:6], rest[6]

        @pl.when(pl.program_id(0) == 0)
        def _():
            ds_scr[...] = jnp.zeros_like(ds_scr)

        _, vjp = jax.vjp(_scan_chunk, s0_ref[0],
                         *[_split_heads(z[...]) for z in (r_ref, lw_ref, k_ref, v_ref, a_ref, b_ref)])
        grads = vjp((_split_heads(do_ref[...]), ds_scr[...]))
        for o_ref, g in zip(out_refs, grads[1:]):
            o_ref[...] = _merge_heads(g)
        ds_scr[...] = grads[0]

    return pl.pallas_call(
        body,
        name="scan_bwd",
        grid=(nc,),
        in_specs=specs + [state],
        out_specs=[pl.BlockSpec((SCAN_CHUNK, D_MODEL), lambda c: (nc - 1 - c, 0))] * 6,
        out_shape=[jax.ShapeDtypeStruct((t, D_MODEL), F32)] * 6,
        scratch_shapes=[pltpu.VMEM((N_HEADS, HEAD, HEAD), F32)],
        compiler_params=_cparams(1),
    )(*[a for a, _ in ops], do, s0s)


_MDIMS = {
    "nn": (((1,), (0,)), ((), ())),
    "nt": (((1,), (1,)), ((), ())),
    "tn": (((0,), (0,)), ((), ())),
}


def _raw_mdot(x, y, mode, exact):
    if exact:
        return lax.dot_general(x, y, _MDIMS[mode], precision=lax.Precision.HIGH, preferred_element_type=F32)
    return lax.dot_general(x.astype(BF16), y.astype(BF16), _MDIMS[mode], preferred_element_type=F32)


@functools.partial(jax.custom_vjp, nondiff_argnums=(2, 3))
def mdot(x, y, mode, exact):
    return _raw_mdot(x, y, mode, exact)


def _mdot_fwd(x, y, mode, exact):
    return _raw_mdot(x, y, mode, exact), (x, y)


def _mdot_bwd(mode, exact, res, g):
    x, y = res
    if mode == "nn":
        return mdot(g, y, "nt", exact), mdot(x, g, "tn", exact)
    if mode == "nt":
        return mdot(g, y, "nn", exact), mdot(g, x, "tn", exact)
    return mdot(y, g, "nt", exact), mdot(x, g, "nn", exact)


mdot.defvjp(_mdot_fwd, _mdot_bwd)


def _seg_ones():
    i = lax.broadcasted_iota(jnp.int32, (256, 256), 0) // HEAD
    j = lax.broadcasted_iota(jnp.int32, (256, 256), 1) // HEAD
    return (i == j).astype(BF16)


@jax.custom_vjp
def segsum(x):
    bd = _seg_ones()
    hi = x.astype(BF16)
    lo = (x - hi.astype(F32)).astype(BF16)
    cols = []
    for j in range(x.shape[1] // 256):
        sl = slice(256 * j, 256 * (j + 1))
        cols.append(jnp.dot(hi[:, sl], bd, preferred_element_type=F32)
                    + jnp.dot(lo[:, sl], bd, preferred_element_type=F32))
    return jnp.concatenate(cols, axis=1)


segsum.defvjp(lambda x: (segsum(x), None), lambda _, g: (segsum(g),))


NORM_EPS = 1e-6
LN_EPS = 1e-5
GN_EPS = 64e-5
SGU_CHUNK = 128
SGU_GROUPS = 8


def _rms(x, g):
    return x * lax.rsqrt(jnp.mean(x * x, axis=-1, keepdims=True) + NORM_EPS) * g


def f_norm_in(x, g):
    return _rms(x, g), x


def f_sgu(p, ln_w, ln_b, sw, sbt):
    tm = p.shape[0]
    z = 0.5 * p * (1.0 + lax.erf(p * 0.7071067811865476))
    u, v = z[:, :D_MODEL], z[:, D_MODEL:]
    mu = jnp.mean(v, axis=-1, keepdims=True)
    d = v - mu
    vn = d * lax.rsqrt(jnp.mean(d * d, axis=-1, keepdims=True) + LN_EPS) * ln_w + ln_b
    ii = lax.broadcasted_iota(jnp.int32, (SGU_CHUNK, SGU_CHUNK), 0)
    jj = lax.broadcasted_iota(jnp.int32, (SGU_CHUNK, SGU_CHUNK), 1)
    mask = (jj <= ii).astype(F32)
    gi = lax.broadcasted_iota(jnp.int32, (SGU_GROUPS, D_MODEL), 0)
    ci = lax.broadcasted_iota(jnp.int32, (SGU_GROUPS, D_MODEL), 1) // SGU_CHUNK
    bias = mdot(sbt, (gi == ci).astype(F32), "nn", True)
    rows = []
    for c in range(tm // SGU_CHUNK):
        cols = []
        for g in range(SGU_GROUPS):
            blk = vn[c * SGU_CHUNK:(c + 1) * SGU_CHUNK, g * SGU_CHUNK:(g + 1) * SGU_CHUNK]
            cols.append(mdot(sw[g] * mask, blk, "nn", False))
        rows.append(jnp.concatenate(cols, axis=1) + bias)
    return (u * jnp.concatenate(rows, axis=0),)


def _softplus(x):
    return jnp.maximum(x, 0.0) + jnp.log1p(jnp.exp(-jnp.abs(x)))


def f_pre(qr, qk, qv, ql, wl, w0, al, a0, gl, k_k, k_a):
    xw, xa, xg = ql[:, :128], ql[:, 128:256], ql[:, 256:512]
    wr = -_softplus(-(w0 + mdot(jnp.tanh(xw), wl, "nn", True))) - 0.5
    lw = -jnp.exp(wr)
    aa = jax.nn.sigmoid(a0 + mdot(xa, al, "nn", True))
    g = mdot(jax.nn.sigmoid(xg), gl, "nn", True)
    kkr = qk * k_k
    kk = kkr / jnp.maximum(jnp.sqrt(segsum(kkr * kkr)), 1e-12)
    kp = qk * (1.0 + (aa - 1.0) * k_a)
    return qr, lw, kp, qv, -kk, kk * aa, g, qr, kp, qv


def f_post(o, r, kp, v, g, lnw, lnb, rk):
    mu = segsum(o) * (1.0 / HEAD)
    d = o - mu
    gn = d * lax.rsqrt(segsum(d * d) * (1.0 / HEAD) + GN_EPS)
    return ((gn * lnw + lnb + segsum(r * kp * rk) * v) * g,)


def f_mix(ya, yb, ga, gb):
    return (jax.nn.sigmoid(ga) * ya + jax.nn.sigmoid(gb) * yb,)


def f_ffn_in(h1, g):
    return _rms(h1, g), h1


def f_final(h1, m3, tgt, g):
    y = _rms(h1 + m3, g)
    err = jnp.square(y - tgt)
    return 0.5 * jnp.sum(jnp.mean(err, axis=-1))


def _cparams(n_grid):
    return pltpu.CompilerParams(dimension_semantics=("arbitrary",) * n_grid, vmem_limit_bytes=VMEM_LIMIT)


def _tile_spec(tm, w, cb):
    return pl.BlockSpec((tm, w), lambda i: (i, cb))


def _const_spec(c):
    nd = c.ndim
    return pl.BlockSpec(c.shape, lambda i: (0,) * nd)


def ew_call(fn, tiled, consts, outs, *, tm, name):
    t = tiled[0][0].shape[0]
    n_t, n_c = len(tiled), len(consts)

    def body(*refs):
        tv = [r[...].astype(F32) for r in refs[:n_t]]
        cv = [r[...] for r in refs[n_t:n_t + n_c]]
        res = fn(*tv, *cv)
        for o_ref, val in zip(refs[n_t + n_c:], res):
            o_ref[...] = val.astype(o_ref.dtype)

    return pl.pallas_call(
        body,
        name=name,
        grid=(t // tm,),
        in_specs=[_tile_spec(tm, w, cb) for _, w, cb in tiled] + [_const_spec(c) for c in consts],
        out_specs=[_tile_spec(tm, w, 0) for w, _ in outs],
        out_shape=[jax.ShapeDtypeStruct((t, w), dt) for w, dt in outs],
        compiler_params=_cparams(1),
    )(*[a for a, _, _ in tiled], *consts)


def ew_vjp_call(fn, tiled, consts, cots, d_tiled, d_consts, *, tm, name):
    t = tiled[0][0].shape[0]
    n_t, n_c, n_g = len(tiled), len(consts), len(cots)
    dt_list = [(i, dt) for i, dts in enumerate(d_tiled) for dt in dts]
    dc_list = [i for i, want in enumerate(d_consts) if want]

    def body(*refs):
        tv = [r[...].astype(F32) for r in refs[:n_t]]
        cv = [r[...] for r in refs[n_t:n_t + n_c]]
        gv = tuple(r[...].astype(F32) for r in refs[n_t + n_c:n_t + n_c + n_g])
        out_refs = refs[n_t + n_c + n_g:]
        _, vjp = jax.vjp(fn, *tv, *cv)
        grads = vjp(gv)
        for o_ref, (i, _) in zip(out_refs, dt_list):
            o_ref[...] = grads[i].astype(o_ref.dtype)
        acc_refs = out_refs[len(dt_list):]

        @pl.when(pl.program_id(0) == 0)
        def _():
            for a_ref in acc_refs:
                a_ref[...] = jnp.zeros_like(a_ref)

        for a_ref, i in zip(acc_refs, dc_list):
            a_ref[...] += grads[n_t + i]

    res = pl.pallas_call(
        body,
        name=name,
        grid=(t // tm,),
        in_specs=[_tile_spec(tm, w, cb) for _, w, cb in tiled] + [_const_spec(c) for c in consts]
        + [_tile_spec(tm, w, cb) for _, w, cb in cots],
        out_specs=[_tile_spec(tm, tiled[i][1], 0) for i, _ in dt_list] + [_const_spec(consts[i]) for i in dc_list],
        out_shape=[jax.ShapeDtypeStruct((t, tiled[i][1]), dt) for i, dt in dt_list]
        + [jax.ShapeDtypeStruct(consts[i].shape, F32) for i in dc_list],
        compiler_params=_cparams(1),
    )(*[a for a, _, _ in tiled], *consts, *[a for a, _, _ in cots])
    return res[:len(dt_list)], res[len(dt_list):]


def mm(a, b, mode, *, tm, tn, name, out_dtypes=(F32,), epi=None, extras=()):
    m = a.shape[1] if mode == "tn" else a.shape[0]
    kd = a.shape[0] if mode == "tn" else a.shape[1]
    n = b.shape[0] if mode == "nt" else b.shape[1]
    tm, tn = min(tm, m), min(tn, n)
    if mode == "nn":
        a_spec = pl.BlockSpec((tm, kd), lambda i, j: (i, 0))
        b_spec = pl.BlockSpec((kd, tn), lambda i, j: (0, j))
    elif mode == "nt":
        a_spec = pl.BlockSpec((tm, kd), lambda i, j: (i, 0))
        b_spec = pl.BlockSpec((tn, kd), lambda i, j: (j, 0))
    else:
        a_spec = pl.BlockSpec((kd, tm), lambda i, j: (0, i))
        b_spec = pl.BlockSpec((kd, tn), lambda i, j: (0, j))
    n_e = len(extras)
    o_spec = pl.BlockSpec((tm, tn), lambda i, j: (i, j))

    def body(a_ref, b_ref, *refs):
        c = lax.dot_general(a_ref[...], b_ref[...], _MDIMS[mode], preferred_element_type=F32)
        res = epi(c, *[r[...] for r in refs[:n_e]]) if epi is not None else (c,)
        for o_ref, val in zip(refs[n_e:], res):
            o_ref[...] = val.astype(o_ref.dtype)

    res = pl.pallas_call(
        body,
        name=name,
        grid=(m // tm, n // tn),
        in_specs=[a_spec, b_spec] + [o_spec] * n_e,
        out_specs=[o_spec] * len(out_dtypes),
        out_shape=[jax.ShapeDtypeStruct((m, n), dt) for dt in out_dtypes],
        compiler_params=_cparams(2),
    )(a, b, *extras)
    return res if len(out_dtypes) > 1 else res[0]


P_WIDTH = 7680
RWKV_COL0 = 4096
RWKV_WIDTH = 3584
SHIFT_BLK = 512


def _shift_down(p, prev_row):
    rows = lax.broadcasted_iota(jnp.int32, p.shape, 0)
    return jnp.where(rows == 0, prev_row, pltpu.roll(p, 1, 0))


def shiftmix_fwd(p_all, sbp, *, tm):
    t = p_all.shape[0]
    c0 = RWKV_COL0 // SHIFT_BLK
    hb = tm // 8

    def body(p_ref, halo_ref, sb_ref, q_ref):
        p = p_ref[...]
        prev = jnp.where(pl.program_id(0) == 0, 0.0, halo_ref[7:8, :])
        q_ref[...] = p * sb_ref[0:1, :] + _shift_down(p, prev) * sb_ref[1:2, :]

    return pl.pallas_call(
        body,
        name="shiftmix_fwd",
        grid=(t // tm, RWKV_WIDTH // SHIFT_BLK),
        in_specs=[
            pl.BlockSpec((tm, SHIFT_BLK), lambda i, j: (i, c0 + j)),
            pl.BlockSpec((8, SHIFT_BLK), lambda i, j: (jnp.maximum(i * hb - 1, 0), c0 + j)),
            pl.BlockSpec((2, SHIFT_BLK), lambda i, j: (0, j)),
        ],
        out_specs=pl.BlockSpec((tm, SHIFT_BLK), lambda i, j: (i, j)),
        out_shape=jax.ShapeDtypeStruct((t, RWKV_WIDTH), F32),
        compiler_params=_cparams(2),
    )(p_all, p_all, sbp)


def shiftmix_bwd(dq, col0, p_all, sbp, *, tm, name):
    t, w = dq.shape
    n_i = t // tm
    hb = tm // 8
    cq = col0 // SHIFT_BLK
    cp = (RWKV_COL0 + col0) // SHIFT_BLK

    def body(dq_ref, dqn_ref, p_ref, ph_ref, sb_ref, dp_ref, dsb_ref):
        i = pl.program_id(1)
        dq_t = dq_ref[...]
        rows = lax.broadcasted_iota(jnp.int32, dq_t.shape, 0)
        nxt = jnp.where(i == n_i - 1, 0.0, dqn_ref[0:1, :])
        up = jnp.where(rows == tm - 1, nxt, pltpu.roll(dq_t, tm - 1, 0))
        dp_ref[...] = (dq_t * sb_ref[0:1, :] + up * sb_ref[1:2, :]).astype(dp_ref.dtype)
        p = p_ref[...]
        prev = jnp.where(i == 0, 0.0, ph_ref[7:8, :])
        s0 = jnp.sum(dq_t * p, axis=0, keepdims=True)
        s1 = jnp.sum(dq_t * _shift_down(p, prev), axis=0, keepdims=True)
        two = lax.broadcasted_iota(jnp.int32, (2, SHIFT_BLK), 0)

        @pl.when(i == 0)
        def _():
            dsb_ref[...] = jnp.zeros_like(dsb_ref)

        dsb_ref[...] += jnp.where(two == 0, s0, s1)

    return pl.pallas_call(
        body,
        name=name,
        grid=(w // SHIFT_BLK, n_i),
        in_specs=[
            pl.BlockSpec((tm, SHIFT_BLK), lambda j, i: (i, j)),
            pl.BlockSpec((8, SHIFT_BLK), lambda j, i: (jnp.minimum((i + 1) * hb, t // 8 - 1), j)),
            pl.BlockSpec((tm, SHIFT_BLK), lambda j, i: (i, cp + j)),
            pl.BlockSpec((8, SHIFT_BLK), lambda j, i: (jnp.maximum(i * hb - 1, 0), cp + j)),
            pl.BlockSpec((2, SHIFT_BLK), lambda j, i: (0, cq + j)),
        ],
        out_specs=[
            pl.BlockSpec((tm, SHIFT_BLK), lambda j, i: (i, j)),
            pl.BlockSpec((2, SHIFT_BLK), lambda j, i: (0, j)),
        ],
        out_shape=[jax.ShapeDtypeStruct((t, w), BF16), jax.ShapeDtypeStruct((2, w), F32)],
        compiler_params=_cparams(2),
    )(dq, dq, p_all, p_all, sbp)


def final_call(h1, m3, tgt, g_final, *, tm):
    t = h1.shape[0]

    def body(h1_ref, m3_ref, tgt_ref, g_ref, dh_ref, dhb_ref, dg_ref, loss_ref):
        loss, vjp = jax.vjp(f_final, h1_ref[...], m3_ref[...], tgt_ref[...], g_ref[...])
        dh, _, _, dg = vjp(jnp.ones((), F32))
        dh_ref[...] = dh
        dhb_ref[...] = dh.astype(BF16)

        @pl.when(pl.program_id(0) == 0)
        def _():
            dg_ref[...] = jnp.zeros_like(dg_ref)
            loss_ref[...] = jnp.zeros_like(loss_ref)

        dg_ref[...] += dg
        loss_ref[...] += jnp.full(loss_ref.shape, loss, F32)

    tile = _tile_spec(tm, D_MODEL, 0)
    return pl.pallas_call(
        body,
        name="final_loss",
        grid=(t // tm,),
        in_specs=[tile, tile, tile, _const_spec(g_final)],
        out_specs=[tile, tile, _const_spec(g_final), pl.BlockSpec((8, 128), lambda i: (0, 0))],
        out_shape=[jax.ShapeDtypeStruct((t, D_MODEL), F32), jax.ShapeDtypeStruct((t, D_MODEL), BF16),
                   jax.ShapeDtypeStruct(g_final.shape, F32), jax.ShapeDtypeStruct((8, 128), F32)],
        compiler_params=_cparams(1),
    )(h1, m3, tgt, g_final)


N_SGU = 2048
N_RWKV = 3360
LORA_W, LORA_A, LORA_G = 64, 64, 160


def _pad_rwkv_cols(z):
    zero = lambda n: jnp.zeros(z.shape[:-1] + (n,), z.dtype)
    return jnp.concatenate([z[..., :3072], z[..., 3072:3136], zero(64), z[..., 3136:3200], zero(64),
                            z[..., 3200:3360], zero(96)], axis=-1)


def _unpad_rwkv_cols(z):
    return jnp.concatenate([z[..., :3072], z[..., 3072:3136], z[..., 3200:3264], z[..., 3328:3488]], axis=-1)


def _pad_win(w):
    return jnp.concatenate([w[:, :N_SGU], w[:, N_SGU + N_RWKV:], _pad_rwkv_cols(w[:, N_SGU:N_SGU + N_RWKV])], axis=1)


def _unpad_win(w):
    return jnp.concatenate([w[:, :N_SGU], _unpad_rwkv_cols(w[:, RWKV_COL0:]), w[:, N_SGU:RWKV_COL0]], axis=1)


def _pad_rows(w, n):
    return jnp.concatenate([w, jnp.zeros((n - w.shape[0],) + w.shape[1:], w.dtype)], axis=0)


def _relu2_epi(c):
    return c, jnp.square(jnp.maximum(c, 0.0))


def _relu2_bwd_epi(c, hid):
    return (c * (2.0 * jnp.maximum(hid, 0.0)),)


def _add_epi(c, x):
    return (c + x,)


def _pre_fwd(*args):
    res = f_pre(*args)
    return res[1], res[2], res[4], res[5], res[6]


def local_step(x, tgt, w):
    d = D_MODEL
    win_p = _pad_win(w["w_in"])
    sbp = _pad_rwkv_cols(w["shift_b"])
    wl = _pad_rows(w["w_lora_w"], 128)
    al = _pad_rows(w["a_lora_w"], 128)
    gl = _pad_rows(w["g_lora_w"], 256)
    sbt = w["sgu_b"].T

    (a_bf,) = ew_call(lambda x_, g_: (f_norm_in(x_, g_)[0],), [(x, d, 0)], [w["g_mix"]], [(d, BF16)], tm=256,
                      name="norm_in")
    p_all = mm(a_bf, win_p, "nn", tm=512, tn=1280, name="mm_in")
    sgu_t = [(p_all, 2 * d, 0)]
    sgu_c = [w["sgu_ln_w"], w["sgu_ln_b"], w["sgu_w"], sbt]
    (s_bf,) = ew_call(f_sgu, sgu_t, sgu_c, [(d, BF16)], tm=256, name="sgu_fwd")
    ya = mm(s_bf, w["w_proj_a"], "nn", tm=512, tn=1024, name="mm_proj_a")
    q = shiftmix_fwd(p_all, sbp, tm=256)
    pre_t = [(q, d, 0), (q, d, 1), (q, d, 2), (q, 512, 6)]
    pre_c = [wl, w["w0"], al, w["a0"], gl, w["k_k"], w["k_a"]]
    lw, kp, na, nb, g = ew_call(_pre_fwd, pre_t, pre_c, [(d, F32)] * 5, tm=256, name="rwkv_pre_fwd")
    scan_ops = [(q, 0), (lw, 0), (kp, 0), (q, 2), (na, 0), (nb, 0)]
    o, s0s = scan_fwd(scan_ops)
    post_t = [(o, d, 0), (q, d, 0), (kp, d, 0), (q, d, 2), (g, d, 0)]
    post_c = [w["ln_x_w"], w["ln_x_b"], w["r_k"]]
    (ob_bf,) = ew_call(f_post, post_t, post_c, [(d, BF16)], tm=256, name="rwkv_post_fwd")
    yb = mm(ob_bf, w["w_proj_b"], "nn", tm=512, tn=1024, name="mm_proj_b")
    mix_t = [(ya, d, 0), (yb, d, 0), (p_all, d, 2), (p_all, d, 3)]
    (mixed_bf,) = ew_call(f_mix, mix_t, [], [(d, BF16)], tm=256, name="mix_fwd")
    h1 = mm(mixed_bf, w["w_out"], "nn", tm=512, tn=1024, name="mm_out", epi=_add_epi, extras=(x,))
    (f_bf,) = ew_call(lambda h_, g_: (f_ffn_in(h_, g_)[0],), [(h1, d, 0)], [w["g_ffn"]], [(d, BF16)], tm=256,
                      name="ffn_norm")
    hid, act_bf = mm(f_bf, w["w_ffn1"], "nn", tm=512, tn=1024, name="mm_ffn1", out_dtypes=(F32, BF16), epi=_relu2_epi)
    m3 = mm(act_bf, w["w_ffn2"], "nn", tm=512, tn=1024, name="mm_ffn2")
    dh2, dh2_bf, dg_final, loss = final_call(h1, m3, tgt, w["g_final"], tm=256)

    dhid_bf = mm(dh2_bf, w["w_ffn2"], "nt", tm=512, tn=1024, name="mm_dact", out_dtypes=(BF16,), epi=_relu2_bwd_epi,
                 extras=(hid,))
    d_ffn2 = mm(act_bf, dh2_bf, "tn", tm=512, tn=1024, name="mm_dw_ffn2")
    df = mm(dhid_bf, w["w_ffn1"], "nt", tm=512, tn=1024, name="mm_df")
    d_ffn1 = mm(f_bf, dhid_bf, "tn", tm=512, tn=1024, name="mm_dw_ffn1")
    (dh1, dh1_bf), (dg_ffn,) = ew_vjp_call(f_ffn_in, [(h1, d, 0)], [w["g_ffn"]], [(df, d, 0), (dh2, d, 0)],
                                           [(F32, BF16)], [True], tm=256, name="ffn_norm_bwd")
    dmixed = mm(dh1_bf, w["w_out"], "nt", tm=512, tn=1024, name="mm_dmixed")
    d_out = mm(mixed_bf, dh1_bf, "tn", tm=512, tn=1024, name="mm_dw_out")
    (dya_bf, dyb_bf, dga_bf, dgb_bf), _ = ew_vjp_call(f_mix, mix_t, [], [(dmixed, d, 0)], [(BF16,)] * 4, [], tm=256,
                                                      name="mix_bwd")
    dob = mm(dyb_bf, w["w_proj_b"], "nt", tm=512, tn=1024, name="mm_dob")
    d_proj_b = mm(ob_bf, dyb_bf, "tn", tm=512, tn=1024, name="mm_dw_proj_b")
    (do, dr_p, dkp_p, dv_p, dg), (dlnx_w, dlnx_b, dr_k) = ew_vjp_call(
        f_post, post_t, post_c, [(dob, d, 0)], [(F32,)] * 5, [True] * 3, tm=256, name="rwkv_post_bwd")
    scan_g = scan_bwd(scan_ops, s0s, do)
    pre_g = [(z, d, 0) for z in scan_g] + [(dg, d, 0), (dr_p, d, 0), (dkp_p, d, 0), (dv_p, d, 0)]
    (dq_r, dq_k, dq_v, dq_l), (dwl, dw0, dal, da0, dgl, dk_k, dk_a) = ew_vjp_call(
        f_pre, pre_t, pre_c, pre_g, [(F32,)] * 4, [True] * 7, tm=128, name="rwkv_pre_bwd")
    dp_r, dsb_r = shiftmix_bwd(dq_r, 0, p_all, sbp, tm=256, name="shiftmix_bwd_r")
    dp_k, dsb_k = shiftmix_bwd(dq_k, d, p_all, sbp, tm=256, name="shiftmix_bwd_k")
    dp_v, dsb_v = shiftmix_bwd(dq_v, 2 * d, p_all, sbp, tm=256, name="shiftmix_bwd_v")
    dp_l, dsb_l = shiftmix_bwd(dq_l, 3 * d, p_all, sbp, tm=256, name="shiftmix_bwd_l")
    ds = mm(dya_bf, w["w_proj_a"], "nt", tm=512, tn=1024, name="mm_ds")
    d_proj_a = mm(s_bf, dya_bf, "tn", tm=512, tn=1024, name="mm_dw_proj_a")
    (dp_sgu,), (dln_w, dln_b, dsw, dsbt) = ew_vjp_call(f_sgu, sgu_t, sgu_c, [(ds, d, 0)], [(BF16,)], [True] * 4,
                                                       tm=256, name="sgu_bwd")
    dp_all = jnp.concatenate([dp_sgu, dga_bf, dgb_bf, dp_r, dp_k, dp_v, dp_l], axis=1)
    da = mm(dp_all, win_p, "nt", tm=512, tn=512, name="mm_da")
    d_in_p = mm(a_bf, dp_all, "tn", tm=512, tn=1280, name="mm_dw_in")
    (grad_x,), (dg_mix,) = ew_vjp_call(f_norm_in, [(x, d, 0)], [w["g_mix"]], [(da, d, 0), (dh1, d, 0)], [(F32,)],
                                       [True], tm=256, name="norm_in_bwd")

    grads = {
        "g_mix": dg_mix, "w_in": _unpad_win(d_in_p), "sgu_ln_w": dln_w, "sgu_ln_b": dln_b, "sgu_w": dsw,
        "sgu_b": dsbt.T, "w_proj_a": d_proj_a,
        "shift_b": _unpad_rwkv_cols(jnp.concatenate([dsb_r, dsb_k, dsb_v, dsb_l], axis=1)),
        "w_lora_w": dwl[:LORA_W], "w0": dw0, "a_lora_w": dal[:LORA_A], "a0": da0, "g_lora_w": dgl[:LORA_G],
        "k_k": dk_k, "k_a": dk_a, "r_k": dr_k, "ln_x_w": dlnx_w, "ln_x_b": dlnx_b, "w_proj_b": d_proj_b,
        "w_out": d_out, "g_ffn": dg_ffn, "w_ffn1": d_ffn1, "w_ffn2": d_ffn2, "g_final": dg_final,
    }
    return loss[0, 0], grad_x, grads


MESH = pl.DeviceIdType.MESH
N_CHIPS = 4
N_DEV = 8
PACK_ROWS = 5152
HALF_ROWS = PACK_ROWS // 2
GATHER_ROWS = 5472
PACK_TILE = 368
SMALL_ROWS = 152
_ANY = pl.BlockSpec(memory_space=pl.ANY)


def _coords():
    return lax.axis_index("x"), lax.axis_index("y"), lax.axis_index("c")


def _other_chips(x, y):
    return [(1 - x, y), (x, 1 - y), (1 - x, 1 - y)]


def _remote(src, dst, send_sems, recv_sems, k, to):
    return pltpu.make_async_remote_copy(src_ref=src, dst_ref=dst, send_sem=send_sems.at[k], recv_sem=recv_sems.at[k],
                                        device_id=to, device_id_type=MESH)


def gather_shards(pack):
    def body(src_ref, out_ref, send_sems, recv_sems):
        x, y, c = _coords()
        me = 2 * x + y
        sib = (x, y, 1 - c)
        chips = _other_chips(x, y)
        first = [_remote(src_ref.at[c], out_ref.at[me, c], send_sems, recv_sems, k, (cx, cy, c))
                 for k, (cx, cy) in enumerate(chips)]
        for cp in first:
            cp.start()
        passed = []
        for k, (cx, cy) in enumerate(chips):
            j = 2 * cx + cy
            _remote(src_ref.at[c], out_ref.at[j, c], send_sems, recv_sems, k, (cx, cy, c)).wait_recv()
            fwd = _remote(out_ref.at[j, c], out_ref.at[j, c], send_sems, recv_sems, 3 + k, sib)
            fwd.start()
            passed.append(fwd)
        for k, (cx, cy) in enumerate(chips):
            j = 2 * cx + cy
            _remote(out_ref.at[j, 1 - c], out_ref.at[j, 1 - c], send_sems, recv_sems, 3 + k, sib).wait_recv()
        for cp in first + passed:
            cp.wait_send()

    return pl.pallas_call(
        body,
        name="gather_shards",
        in_specs=[_ANY],
        out_specs=_ANY,
        out_shape=jax.ShapeDtypeStruct((N_CHIPS,) + pack.shape, pack.dtype),
        scratch_shapes=[pltpu.SemaphoreType.DMA((6,)), pltpu.SemaphoreType.DMA((6,))],
    )(pack)


def reduce_pair(g):
    def body(g_ref, got_ref, send_sems, recv_sems):
        x, y, c = _coords()
        sib = (x, y, 1 - c)
        sends = [_remote(g_ref.at[j, 1 - c], got_ref.at[j], send_sems, recv_sems, j, sib) for j in range(N_CHIPS)]
        for cp in sends:
            cp.start()
        for cp in sends:
            cp.wait_recv()
        for cp in sends:
            cp.wait_send()

    return pl.pallas_call(
        body,
        name="reduce_pair",
        in_specs=[_ANY],
        out_specs=_ANY,
        out_shape=jax.ShapeDtypeStruct((N_CHIPS,) + g.shape[2:], g.dtype),
        scratch_shapes=[pltpu.SemaphoreType.DMA((N_CHIPS,)), pltpu.SemaphoreType.DMA((N_CHIPS,))],
    )(g)


def pair_sum(g, got, *, tm):
    n, _, rows, width = g.shape

    def body(g0_ref, g1_ref, got_ref, out_ref, out16_ref):
        own = jnp.where(lax.axis_index("c") == 0, g0_ref[0, 0], g1_ref[0, 0])
        total = own + got_ref[0]
        out_ref[0] = total
        out16_ref[0] = total.astype(BF16)

    blk = pl.BlockSpec((1, tm, width), lambda j, i: (j, i, 0))
    return pl.pallas_call(
        body,
        name="pair_sum",
        grid=(n, rows // tm),
        in_specs=[pl.BlockSpec((1, 1, tm, width), lambda j, i: (j, 0, i, 0)),
                  pl.BlockSpec((1, 1, tm, width), lambda j, i: (j, 1, i, 0)), blk],
        out_specs=[blk, blk],
        out_shape=[jax.ShapeDtypeStruct(got.shape, F32), jax.ShapeDtypeStruct(got.shape, BF16)],
        compiler_params=_cparams(2),
    )(g, g, got)


def reduce_chips(p):
    def body(p_ref, out_ref, send_sems, recv_sems):
        x, y, c = _coords()
        me = 2 * x + y
        chips = _other_chips(x, y)
        sends = [_remote(p_ref.at[2 * cx + cy], out_ref.at[me], send_sems, recv_sems, k, (cx, cy, c))
                 for k, (cx, cy) in enumerate(chips)]
        for cp in sends:
            cp.start()
        for k, (cx, cy) in enumerate(chips):
            _remote(p_ref.at[me], out_ref.at[2 * cx + cy], send_sems, recv_sems, k, (cx, cy, c)).wait_recv()
        for cp in sends:
            cp.wait_send()

    return pl.pallas_call(
        body,
        name="reduce_chips",
        in_specs=[_ANY],
        out_specs=_ANY,
        out_shape=jax.ShapeDtypeStruct(p.shape, p.dtype),
        scratch_shapes=[pltpu.SemaphoreType.DMA((3,)), pltpu.SemaphoreType.DMA((3,))],
    )(p)


def sum_with_own(own, slots, index_fn, *, tm, name):
    n, rows, width = slots.shape
    own3 = own.ndim == 3

    def body(*refs):
        mine = index_fn()
        acc = None
        for s in range(n):
            o = refs[s][0] if own3 else refs[0][...]
            term = jnp.where(mine == s, o, refs[(n if own3 else 1) + s][0].astype(F32))
            acc = term if acc is None else acc + term
        refs[-1][...] = acc

    slot_specs = [pl.BlockSpec((1, tm, width), lambda i, s=s: (s, i, 0)) for s in range(n)]
    own_specs = slot_specs if own3 else [pl.BlockSpec((tm, width), lambda i: (i, 0))]
    return pl.pallas_call(
        body,
        name=name,
        grid=(rows // tm,),
        in_specs=own_specs + slot_specs,
        out_specs=pl.BlockSpec((tm, width), lambda i: (i, 0)),
        out_shape=jax.ShapeDtypeStruct((rows, width), F32),
        compiler_params=_cparams(1),
    )(*([own] * (n if own3 else 1)), *([slots] * n))


def exchange_halves(s):
    rq = PACK_TILE
    nq = s.shape[0] // rq

    def body(s_ref, out_ref, sbuf, rbuf, send_sems, recv_sems, in_sems, out_sems):
        x, y, c = _coords()
        sib = (x, y, 1 - c)
        rows = lambda q: pl.ds(q * rq, rq)
        loads = [pltpu.make_async_copy(s_ref.at[rows(q)], sbuf.at[rows(q)], in_sems.at[q]) for q in range(nq)]
        for cp in loads:
            cp.start()
        sends = []
        for q in range(nq):
            loads[q].wait()
            sends.append(_remote(sbuf.at[rows(q)], rbuf.at[rows(q)], send_sems, recv_sems, q, sib))
            sends[q].start()
        stores = []
        for q in range(nq):
            sends[q].wait_recv()
            stores.append(pltpu.make_async_copy(rbuf.at[rows(q)], out_ref.at[rows(q)], out_sems.at[q]))
            stores[q].start()
        for cp in sends:
            cp.wait_send()
        for cp in stores:
            cp.wait()

    return pl.pallas_call(
        body,
        name="exchange_halves",
        in_specs=[_ANY],
        out_specs=_ANY,
        out_shape=jax.ShapeDtypeStruct(s.shape, s.dtype),
        scratch_shapes=[pltpu.VMEM(s.shape, s.dtype), pltpu.VMEM(s.shape, s.dtype)]
        + [pltpu.SemaphoreType.DMA((nq,))] * 4,
        compiler_params=pltpu.CompilerParams(vmem_limit_bytes=VMEM_LIMIT),
    )(s)


def gather_all(s):
    def body(s_ref, out_ref, send_sems, recv_sems):
        x, y, c = _coords()
        me = 4 * x + 2 * y + c
        peers = []
        for mask in range(1, N_DEV):
            px = 1 - x if mask & 4 else x
            py = 1 - y if mask & 2 else y
            pc = 1 - c if mask & 1 else c
            peers.append((px, py, pc))
        sends = [_remote(s_ref, out_ref.at[me], send_sems, recv_sems, k, peer) for k, peer in enumerate(peers)]
        for cp in sends:
            cp.start()
        for k, (px, py, pc) in enumerate(peers):
            _remote(s_ref, out_ref.at[4 * px + 2 * py + pc], send_sems, recv_sems, k, (px, py, pc)).wait_recv()
        for cp in sends:
            cp.wait_send()

    return pl.pallas_call(
        body,
        name="gather_all",
        in_specs=[_ANY],
        out_specs=_ANY,
        out_shape=jax.ShapeDtypeStruct((N_DEV,) + s.shape, s.dtype),
        scratch_shapes=[pltpu.SemaphoreType.DMA((N_DEV - 1,)), pltpu.SemaphoreType.DMA((N_DEV - 1,))],
    )(s)


ADAM_LR = 0.001
ADAM_B1 = 0.9
ADAM_B2 = 0.999
ADAM_EPS = 1e-08
ADAM_WD = 0.01
ADAM_STEP = 10


def f_adamw(g, w, m, v):
    m = ADAM_B1 * m + (1.0 - ADAM_B1) * g
    v = ADAM_B2 * v + (1.0 - ADAM_B2) * jnp.square(g)
    m_hat = m / (1.0 - ADAM_B1 ** ADAM_STEP)
    v_hat = v / (1.0 - ADAM_B2 ** ADAM_STEP)
    delta = -ADAM_LR * (m_hat / (jnp.sqrt(v_hat) + ADAM_EPS) + ADAM_WD * w)
    return delta, m, v


def adamw_call(g, w, m, v, *, tm, name):
    width = g.shape[1]
    return ew_call(f_adamw, [(g, width, 0), (w, width, 0), (m, width, 0), (v, width, 0)], [], [(width, F32)] * 3,
                   tm=tm, name=name)


def adamw_halves(g_own, g_other, w, m, v, *, tm):
    _, rows, width = w.shape

    def body(go_ref, gx_ref, w_ref, m_ref, v_ref, g_ref, d_ref, nm_ref, nv_ref):
        g = jnp.where(pl.program_id(0) == lax.axis_index("c"), go_ref[...], gx_ref[...])
        delta, nm, nv = f_adamw(g, w_ref[0], m_ref[0], v_ref[0])
        g_ref[0] = g
        d_ref[0] = delta
        nm_ref[0] = nm
        nv_ref[0] = nv

    half = pl.BlockSpec((tm, width), lambda h, i: (i, 0))
    full = pl.BlockSpec((1, tm, width), lambda h, i: (h, i, 0))
    return pl.pallas_call(
        body,
        name="adamw_sharded",
        grid=(2, rows // tm),
        in_specs=[half, half, full, full, full],
        out_specs=[full] * 4,
        out_shape=[jax.ShapeDtypeStruct(w.shape, F32)] * 4,
        compiler_params=_cparams(2),
    )(g_own, g_other, w, m, v)


SHARDED = ["w_in", "w_proj_a", "w_proj_b", "w_out", "w_ffn1", "w_ffn2", "w_lora_w", "a_lora_w", "g_lora_w"]
LORAS = ["w_lora_w", "a_lora_w", "g_lora_w"]
SHARD_ROWS = {"w_in": 2048, "w_proj_a": 256, "w_proj_b": 256, "w_out": 256, "w_ffn1": 1024, "w_ffn2": 1024,
              "w_lora_w": 64, "a_lora_w": 64, "g_lora_w": 160}
SHARD_AXIS = {"w_in": 1, "w_proj_a": 0, "w_lora_w": 1, "a_lora_w": 1, "g_lora_w": 1, "w_proj_b": 0, "w_out": 0,
              "w_ffn1": 1, "w_ffn2": 0}
SHARD_SHAPE = {"w_in": (1024, 1864), "w_proj_a": (256, 1024), "w_lora_w": (64, 256), "a_lora_w": (64, 256),
               "g_lora_w": (160, 256), "w_proj_b": (256, 1024), "w_out": (256, 1024), "w_ffn1": (1024, 1024),
               "w_ffn2": (1024, 1024)}
SHIFT_SHARD = (2, 840)
VECTORS = ["g_mix", "sgu_ln_w", "sgu_ln_b", "w0", "a0", "k_k", "k_a", "r_k", "ln_x_w", "ln_x_b", "g_ffn", "g_final"]
SMALL = VECTORS + ["sgu_w", "sgu_b"]
SMALL_SHAPE = {**{n: (1, 1024) for n in VECTORS}, "sgu_w": (8, 128, 128), "sgu_b": (8, 128)}
WEIGHTS = ["g_mix", "w_in", "sgu_ln_w", "sgu_ln_b", "sgu_w", "sgu_b", "w_proj_a", "shift_b", "w_lora_w", "w0",
           "a_lora_w", "a0", "g_lora_w", "k_k", "k_a", "r_k", "ln_x_w", "ln_x_b", "w_proj_b", "w_out", "g_ffn",
           "w_ffn1", "w_ffn2", "g_final"]


def _size(shape):
    n = 1
    for s in shape:
        n *= s
    return n


def _pack_rows(parts, rows, dtype):
    flat = jnp.concatenate([p.reshape(-1).astype(dtype) for p in parts])
    return jnp.concatenate([flat, jnp.zeros((rows * 1024 - flat.shape[0],), dtype)]).reshape(rows, 1024)


def _unpack_rows(packed, shapes):
    flat = packed.reshape(-1)
    out, off = [], 0
    for shp in shapes:
        out.append(flat[off:off + _size(shp)].reshape(shp))
        off += _size(shp)
    return out


def _shard_of(name, full, j):
    ax = SHARD_AXIS[name]
    n = SHARD_SHAPE[name][ax]
    return lax.slice_in_dim(full, j * n, (j + 1) * n, axis=ax)


def _pad_cols(z, n=1024):
    return jnp.concatenate([z, jnp.zeros((z.shape[0], n - z.shape[1]), z.dtype)], axis=1)


def _shard_rows(name, s):
    if name == "w_in":
        return jnp.concatenate([s[:, :1024], _pad_cols(s[:, 1024:])], axis=0)
    return s if s.shape[1] == 1024 else _pad_cols(s)


def _rows_shard(name, r):
    if name == "w_in":
        return jnp.concatenate([r[:1024], r[1024:, :SHARD_SHAPE[name][1] - 1024]], axis=1)
    return r[:, :SHARD_SHAPE[name][1]]


def _pack_shard(shard_fn, dtype):
    return jnp.concatenate([_shard_rows(n, shard_fn(n)).astype(dtype) for n in SHARDED], axis=0)


def _unpack_shard(packed):
    out, off = {}, 0
    for n in SHARDED:
        out[n] = _rows_shard(n, packed[off:off + SHARD_ROWS[n]])
        off += SHARD_ROWS[n]
    return out


def kernel(x, g_mix, w_in, sgu_ln_w, sgu_ln_b, sgu_w, sgu_b, w_proj_a, shift_b, w_lora_w, w0, a_lora_w, a0, g_lora_w, k_k, k_a, r_k, ln_x_w, ln_x_b, w_proj_b, w_out, g_ffn, w_ffn1, w_ffn2, g_final, loss_target, m_g_mix, m_w_in, m_sgu_ln_w, m_sgu_ln_b, m_sgu_w, m_sgu_b, m_w_proj_a, m_shift_b, m_w_lora_w, m_w0, m_a_lora_w, m_a0, m_g_lora_w, m_k_k, m_k_a, m_r_k, m_ln_x_w, m_ln_x_b, m_w_proj_b, m_w_out, m_g_ffn, m_w_ffn1, m_w_ffn2, m_g_final, v_g_mix, v_w_in, v_sgu_ln_w, v_sgu_ln_b, v_sgu_w, v_sgu_b, v_w_proj_a, v_shift_b, v_w_lora_w, v_w0, v_a_lora_w, v_a0, v_g_lora_w, v_k_k, v_k_a, v_r_k, v_ln_x_w, v_ln_x_b, v_w_proj_b, v_w_out, v_g_ffn, v_w_ffn1, v_w_ffn2, v_g_final):
    given = dict(zip(WEIGHTS, (g_mix, w_in, sgu_ln_w, sgu_ln_b, sgu_w, sgu_b, w_proj_a, shift_b, w_lora_w, w0, a_lora_w, a0, g_lora_w, k_k, k_a, r_k, ln_x_w, ln_x_b, w_proj_b, w_out, g_ffn, w_ffn1, w_ffn2, g_final)))
    mom_m = dict(zip(WEIGHTS, (m_g_mix, m_w_in, m_sgu_ln_w, m_sgu_ln_b, m_sgu_w, m_sgu_b, m_w_proj_a, m_shift_b, m_w_lora_w, m_w0, m_a_lora_w, m_a0, m_g_lora_w, m_k_k, m_k_a, m_r_k, m_ln_x_w, m_ln_x_b, m_w_proj_b, m_w_out, m_g_ffn, m_w_ffn1, m_w_ffn2, m_g_final)))
    mom_v = dict(zip(WEIGHTS, (v_g_mix, v_w_in, v_sgu_ln_w, v_sgu_ln_b, v_sgu_w, v_sgu_b, v_w_proj_a, v_shift_b, v_w_lora_w, v_w0, v_a_lora_w, v_a0, v_g_lora_w, v_k_k, v_k_a, v_r_k, v_ln_x_w, v_ln_x_b, v_w_proj_b, v_w_out, v_g_ffn, v_w_ffn1, v_w_ffn2, v_g_final)))
    chip = 2 * lax.axis_index("x") + lax.axis_index("y")

    def local_block(tree, n):
        return tree[n] if n == "g_final" else tree[n][0]

    sb = local_block(given, "shift_b")
    lo_part = lambda z: (z - z.astype(BF16).astype(F32)).astype(BF16)
    extra = [_pad_cols(lo_part(local_block(given, n))) for n in LORAS]
    tile16 = lambda z: jnp.pad(z, ((0, 16 - z.shape[0]), (0, 1024 - z.shape[1])))
    extra += [tile16(sb.astype(BF16)), tile16(lo_part(sb))]
    pack_w = jnp.concatenate([_pack_shard(lambda n: local_block(given, n), BF16)] + extra, axis=0)
    gathered = gather_shards(pack_w.reshape(2, GATHER_ROWS // 2, 1024)).reshape(N_CHIPS, GATHER_ROWS, 1024)
    gathered = lax.dynamic_update_index_in_dim(gathered, pack_w, chip, 0)
    per_chip = [gathered[j] for j in range(N_CHIPS)]
    shards = [_unpack_shard(p) for p in per_chip]
    w = {n: jnp.concatenate([shards[j][n] for j in range(N_CHIPS)], axis=SHARD_AXIS[n]) for n in SHARDED}
    off = PACK_ROWS
    for n in LORAS:
        rows, cols = SHARD_SHAPE[n]
        lo = jnp.concatenate([p[off:off + rows, :cols] for p in per_chip], axis=1)
        w[n] = w[n].astype(F32) + lo.astype(F32)
        off += rows
    w["shift_b"] = jnp.concatenate(
        [p[off:off + 2, :SHIFT_SHARD[1]].astype(F32) + p[off + 16:off + 18, :SHIFT_SHARD[1]].astype(F32)
         for p in per_chip], axis=1)
    for n in SMALL:
        w[n] = local_block(given, n).reshape(SMALL_SHAPE[n])

    loss, grad_x, grads = local_step(x[0], loss_target[0], w)
    loss = lax.psum(loss, ("x", "y", "c"))

    g_pack = jnp.stack([_pack_shard(lambda n: _shard_of(n, grads[n], j), F32) for j in range(N_CHIPS)])
    g_pack = g_pack.reshape(N_CHIPS, 2, HALF_ROWS, 1024)
    chip_part, chip_part16 = pair_sum(g_pack, reduce_pair(g_pack), tm=PACK_TILE)
    half_sum = sum_with_own(chip_part, reduce_chips(chip_part16),
                            lambda: 2 * lax.axis_index("x") + lax.axis_index("y"), tm=PACK_TILE, name="chip_sum")
    other_half = exchange_halves(half_sum)
    halves = lambda tree: _pack_shard(lambda n: local_block(tree, n), F32).reshape(2, HALF_ROWS, 1024)
    big = adamw_halves(half_sum, other_half, halves(given), halves(mom_m), halves(mom_v), tm=PACK_TILE)
    out_g, out_d, out_m, out_v = [_unpack_shard(z.reshape(PACK_ROWS, 1024)) for z in big]

    small_shapes = [SMALL_SHAPE[n] for n in SMALL]
    s_pack = _pack_rows([grads[n] for n in SMALL] + [grads["shift_b"]], SMALL_ROWS, F32)
    g_small = sum_with_own(
        s_pack, gather_all(s_pack), lambda: 4 * lax.axis_index("x") + 2 * lax.axis_index("y") + lax.axis_index("c"),
        tm=SMALL_ROWS, name="small_sum")
    w_small = _pack_rows([local_block(given, n) for n in SMALL], SMALL_ROWS, F32)
    m_small = _pack_rows([local_block(mom_m, n) for n in SMALL], SMALL_ROWS, F32)
    v_small = _pack_rows([local_block(mom_v, n) for n in SMALL], SMALL_ROWS, F32)
    d_small, nm_small, nv_small = adamw_call(g_small, w_small, m_small, v_small, tm=SMALL_ROWS, name="adamw_small")
    g_parts = _unpack_rows(g_small, small_shapes + [(2, N_RWKV)])
    out_g.update(zip(SMALL, g_parts[:-1]))
    out_d.update(zip(SMALL, _unpack_rows(d_small, small_shapes)))
    out_m.update(zip(SMALL, _unpack_rows(nm_small, small_shapes)))
    out_v.update(zip(SMALL, _unpack_rows(nv_small, small_shapes)))
    g_sb = lax.dynamic_slice_in_dim(g_parts[-1], chip * SHIFT_SHARD[1], SHIFT_SHARD[1], axis=1)
    sb_args = [_pack_rows([z], 8, F32) for z in (g_sb, sb, local_block(mom_m, "shift_b"), local_block(mom_v, "shift_b"))]
    sb_res = adamw_call(*sb_args, tm=8, name="adamw_shift_b")
    out_g["shift_b"] = g_sb
    for tree, res in zip((out_d, out_m, out_v), sb_res):
        tree["shift_b"] = _unpack_rows(res, [SHIFT_SHARD])[0]

    def block_of(tree, n):
        return tree[n].reshape(given[n].shape)

    return (loss, grad_x[None], *[block_of(out_g, n) for n in WEIGHTS], *[block_of(out_d, n) for n in WEIGHTS],
            *[block_of(out_m, n) for n in WEIGHTS], *[block_of(out_v, n) for n in WEIGHTS])
```

```python
import functools

import jax
import jax.numpy as jnp
from jax import lax
from jax.experimental import pallas as pl
from jax.experimental.pallas import tpu as pltpu

F32 = jnp.float32
BF16 = jnp.bfloat16

D_MODEL = 1024
N_HEADS = 16
HEAD = 64
SCAN_CHUNK = 64

VMEM_LIMIT = 56 * 1024 * 1024


_BDIMS = {
    "nn": (((2,), (1,)), ((0,), (0,))),
    "nt": (((2,), (2,)), ((0,), (0,))),
    "tn": (((1,), (1,)), ((0,), (0,))),
}


def _raw_bdot(x, y, mode, fine):
    if fine:
        return lax.dot_general(x, y, _BDIMS[mode], precision=lax.Precision.HIGH, preferred_element_type=F32)
    return lax.dot_general(x.astype(BF16), y.astype(BF16), _BDIMS[mode], preferred_element_type=F32)


@functools.partial(jax.custom_vjp, nondiff_argnums=(2, 3))
def bdot(x, y, mode, fine=True):
    return _raw_bdot(x, y, mode, fine)


def _bdot_fwd(x, y, mode, fine):
    return _raw_bdot(x, y, mode, fine), (x, y)


def _bdot_bwd(mode, fine, res, g):
    x, y = res
    if mode == "nn":
        return bdot(g, y, "nt", fine), bdot(x, g, "tn", fine)
    if mode == "nt":
        return bdot(g, y, "nn", fine), bdot(g, x, "tn", fine)
    return bdot(y, g, "nt", fine), bdot(x, g, "nn", fine)


bdot.defvjp(_bdot_fwd, _bdot_bwd)


def _scan_chunk(S0, r, lw, k, v, a, b):
    nh, lc, _ = r.shape
    ti = lax.broadcasted_iota(jnp.int32, (lc, lc), 0)
    si = lax.broadcasted_iota(jnp.int32, (lc, lc), 1)
    incl = (si <= ti).astype(F32)
    strict = (si < ti).astype(F32)
    eye = (si == ti).astype(F32)
    cl = bdot(jnp.broadcast_to(incl, (nh, lc, lc)), lw, "nn")
    cl_last = cl[:, lc - 1:lc, :]
    g_last = jnp.exp(cl_last - cl)
    at = a * jnp.exp(cl - lw)
    bt = b * jnp.exp(-cl)
    kt = k * jnp.exp(-cl)
    rt = r * jnp.exp(cl)
    ar = jnp.concatenate([at, rt], axis=1)
    ar_b = bdot(ar, bt, "nt")
    ar_k = bdot(ar, kt, "nt")
    m_ab, m_rb = ar_b[:, :lc] * strict, ar_b[:, lc:] * incl
    m_ak, m_rk = ar_k[:, :lc] * strict, ar_k[:, lc:] * incl
    x = eye + m_ab
    p = bdot(m_ab, m_ab, "nn", False)
    n = 2
    while n * 2 < lc:
        px = bdot(jnp.concatenate([p, x], axis=1), p, "nn", False)
        p = px[:, :lc]
        x = x + px[:, lc:]
        n *= 2
    x = x + bdot(x, p, "nn", False)
    ar_s = bdot(ar, S0, "nt", False)
    akrk_v = bdot(jnp.concatenate([m_ak, m_rk], axis=1), v, "nn")
    u = bdot(x, ar_s[:, :lc] + akrk_v[:, :lc], "nn", False)
    o = ar_s[:, lc:] + bdot(m_rb, u, "nn", False) + akrk_v[:, lc:]
    s_last = S0 * jnp.exp(cl_last) + bdot(jnp.concatenate([u, v], axis=1),
                                          jnp.concatenate([b * g_last, k * g_last], axis=1), "tn", False)
    return o, s_last


def _split_heads(z):
    return jnp.stack([z[:, HEAD * h:HEAD * (h + 1)] for h in range(N_HEADS)], axis=0)


def _merge_heads(z):
    return jnp.concatenate([z[h] for h in range(N_HEADS)], axis=1)


def _scan_specs(t, ops, rev):
    nc = t // SCAN_CHUNK
    row = (lambda c: nc - 1 - c) if rev else (lambda c: c)
    specs = [pl.BlockSpec((SCAN_CHUNK, D_MODEL), lambda c, cb=cb: (row(c), cb)) for _, cb in ops]
    state = pl.BlockSpec((1, N_HEADS, HEAD, HEAD), lambda c: (row(c), 0, 0, 0))
    return nc, specs, state


def scan_fwd(ops):
    t = ops[0][0].shape[0]
    nc, specs, state = _scan_specs(t, ops, False)

    def body(r_ref, lw_ref, k_ref, v_ref, a_ref, b_ref, o_ref, s0_ref, s_scr):
        @pl.when(pl.program_id(0) == 0)
        def _():
            s_scr[...] = jnp.zeros_like(s_scr)

        s0 = s_scr[...]
        s0_ref[0] = s0
        o, s_last = _scan_chunk(s0, *[_split_heads(z[...]) for z in (r_ref, lw_ref, k_ref, v_ref, a_ref, b_ref)])
        o_ref[...] = _merge_heads(o)
        s_scr[...] = s_last

    return pl.pallas_call(
        body,
        name="scan_fwd",
        grid=(nc,),
        in_specs=specs,
        out_specs=[pl.BlockSpec((SCAN_CHUNK, D_MODEL), lambda c: (c, 0)), state],
        out_shape=[jax.ShapeDtypeStruct((t, D_MODEL), F32), jax.ShapeDtypeStruct((nc, N_HEADS, HEAD, HEAD), F32)],
        scratch_shapes=[pltpu.VMEM((N_HEADS, HEAD, HEAD), F32)],
        compiler_params=_cparams(1),
    )(*[a for a, _ in ops])


def scan_bwd(ops, s0s, do):
    t = ops[0][0].shape[0]
    nc, specs, state = _scan_specs(t, ops + [(do, 0)], True)

    def body(r_ref, lw_ref, k_ref, v_ref, a_ref, b_ref, do_ref, s0_ref, *rest):
        out_refs, ds_scr = rest[:6], rest[6]

        @pl.when(pl.program_id(0) == 0)
        def _():
            ds_scr[...] = jnp.zeros_like(ds_scr)

        _, vjp = jax.vjp(_scan_chunk, s0_ref[0],
                         *[_split_heads(z[...]) for z in (r_ref, lw_ref, k_ref, v_ref, a_ref, b_ref)])
        grads = vjp((_split_heads(do_ref[...]), ds_scr[...]))
        for o_ref, g in zip(out_refs, grads[1:]):
            o_ref[...] = _merge_heads(g)
        ds_scr[...] = grads[0]

    return pl.pallas_call(
        body,
        name="scan_bwd",
        grid=(nc,),
        in_specs=specs + [state],
        out_specs=[pl.BlockSpec((SCAN_CHUNK, D_MODEL), lambda c: (nc - 1 - c, 0))] * 6,
        out_shape=[jax.ShapeDtypeStruct((t, D_MODEL), F32)] * 6,
        scratch_shapes=[pltpu.VMEM((N_HEADS, HEAD, HEAD), F32)],
        compiler_params=_cparams(1),
    )(*[a for a, _ in ops], do, s0s)


_MDIMS = {
    "nn": (((1,), (0,)), ((), ())),
    "nt": (((1,), (1,)), ((), ())),
    "tn": (((0,), (0,)), ((), ())),
}


def _raw_mdot(x, y, mode, exact):
    if exact:
        return lax.dot_general(x, y, _MDIMS[mode], precision=lax.Precision.HIGH, preferred_element_type=F32)
    return lax.dot_general(x.astype(BF16), y.astype(BF16), _MDIMS[mode], preferred_element_type=F32)


@functools.partial(jax.custom_vjp, nondiff_argnums=(2, 3))
def mdot(x, y, mode, exact):
    return _raw_mdot(x, y, mode, exact)


def _mdot_fwd(x, y, mode, exact):
    return _raw_mdot(x, y, mode, exact), (x, y)


def _mdot_bwd(mode, exact, res, g):
    x, y = res
    if mode == "nn":
        return mdot(g, y, "nt", exact), mdot(x, g, "tn", exact)
    if mode == "nt":
        return mdot(g, y, "nn", exact), mdot(g, x, "tn", exact)
    return mdot(y, g, "nt", exact), mdot(x, g, "nn", exact)


mdot.defvjp(_mdot_fwd, _mdot_bwd)


def _seg_ones():
    i = lax.broadcasted_iota(jnp.int32, (256, 256), 0) // HEAD
    j = lax.broadcasted_iota(jnp.int32, (256, 256), 1) // HEAD
    return (i == j).astype(BF16)


@jax.custom_vjp
def segsum(x):
    bd = _seg_ones()
    hi = x.astype(BF16)
    lo = (x - hi.astype(F32)).astype(BF16)
    cols = []
    for j in range(x.shape[1] // 256):
        sl = slice(256 * j, 256 * (j + 1))
        cols.append(jnp.dot(hi[:, sl], bd, preferred_element_type=F32)
                    + jnp.dot(lo[:, sl], bd, preferred_element_type=F32))
    return jnp.concatenate(cols, axis=1)


segsum.defvjp(lambda x: (segsum(x), None), lambda _, g: (segsum(g),))


NORM_EPS = 1e-6
LN_EPS = 1e-5
GN_EPS = 64e-5
SGU_CHUNK = 128
SGU_GROUPS = 8


def _rms(x, g):
    return x * lax.rsqrt(jnp.mean(x * x, axis=-1, keepdims=True) + NORM_EPS) * g


def f_norm_in(x, g):
    return _rms(x, g), x


def f_sgu(p, ln_w, ln_b, sw, sbt):
    tm = p.shape[0]
    z = 0.5 * p * (1.0 + lax.erf(p * 0.7071067811865476))
    u, v = z[:, :D_MODEL], z[:, D_MODEL:]
    mu = jnp.mean(v, axis=-1, keepdims=True)
    d = v - mu
    vn = d * lax.rsqrt(jnp.mean(d * d, axis=-1, keepdims=True) + LN_EPS) * ln_w + ln_b
    ii = lax.broadcasted_iota(jnp.int32, (SGU_CHUNK, SGU_CHUNK), 0)
    jj = lax.broadcasted_iota(jnp.int32, (SGU_CHUNK, SGU_CHUNK), 1)
    mask = (jj <= ii).astype(F32)
    gi = lax.broadcasted_iota(jnp.int32, (SGU_GROUPS, D_MODEL), 0)
    ci = lax.broadcasted_iota(jnp.int32, (SGU_GROUPS, D_MODEL), 1) // SGU_CHUNK
    bias = mdot(sbt, (gi == ci).astype(F32), "nn", True)
    rows = []
    for c in range(tm // SGU_CHUNK):
        cols = []
        for g in range(SGU_GROUPS):
            blk = vn[c * SGU_CHUNK:(c + 1) * SGU_CHUNK, g * SGU_CHUNK:(g + 1) * SGU_CHUNK]
            cols.append(mdot(sw[g] * mask, blk, "nn", False))
        rows.append(jnp.concatenate(cols, axis=1) + bias)
    return (u * jnp.concatenate(rows, axis=0),)


def _softplus(x):
    return jnp.maximum(x, 0.0) + jnp.log1p(jnp.exp(-jnp.abs(x)))


def f_pre(qr, qk, qv, ql, wl, w0, al, a0, gl, k_k, k_a):
    xw, xa, xg = ql[:, :128], ql[:, 128:256], ql[:, 256:512]
    wr = -_softplus(-(w0 + mdot(jnp.tanh(xw), wl, "nn", True))) - 0.5
    lw = -jnp.exp(wr)
    aa = jax.nn.sigmoid(a0 + mdot(xa, al, "nn", True))
    g = mdot(jax.nn.sigmoid(xg), gl, "nn", True)
    kkr = qk * k_k
    kk = kkr / jnp.maximum(jnp.sqrt(segsum(kkr * kkr)), 1e-12)
    kp = qk * (1.0 + (aa - 1.0) * k_a)
    return qr, lw, kp, qv, -kk, kk * aa, g, qr, kp, qv


def f_post(o, r, kp, v, g, lnw, lnb, rk):
    mu = segsum(o) * (1.0 / HEAD)
    d = o - mu
    gn = d * lax.rsqrt(segsum(d * d) * (1.0 / HEAD) + GN_EPS)
    return ((gn * lnw + lnb + segsum(r * kp * rk) * v) * g,)


def f_mix(ya, yb, ga, gb):
    return (jax.nn.sigmoid(ga) * ya + jax.nn.sigmoid(gb) * yb,)


def f_ffn_in(h1, g):
    return _rms(h1, g), h1


def f_final(h1, m3, tgt, g):
    y = _rms(h1 + m3, g)
    err = jnp.square(y - tgt)
    return 0.5 * jnp.sum(jnp.mean(err, axis=-1))


def _cparams(n_grid):
    return pltpu.CompilerParams(dimension_semantics=("arbitrary",) * n_grid, vmem_limit_bytes=VMEM_LIMIT)


def _tile_spec(tm, w, cb):
    return pl.BlockSpec((tm, w), lambda i: (i, cb))


def _const_spec(c):
    nd = c.ndim
    return pl.BlockSpec(c.shape, lambda i: (0,) * nd)


def ew_call(fn, tiled, consts, outs, *, tm, name):
    t = tiled[0][0].shape[0]
    n_t, n_c = len(tiled), len(consts)

    def body(*refs):
        tv = [r[...].astype(F32) for r in refs[:n_t]]
        cv = [r[...] for r in refs[n_t:n_t + n_c]]
        res = fn(*tv, *cv)
        for o_ref, val in zip(refs[n_t + n_c:], res):
            o_ref[...] = val.astype(o_ref.dtype)

    return pl.pallas_call(
        body,
        name=name,
        grid=(t // tm,),
        in_specs=[_tile_spec(tm, w, cb) for _, w, cb in tiled] + [_const_spec(c) for c in consts],
        out_specs=[_tile_spec(tm, w, 0) for w, _ in outs],
        out_shape=[jax.ShapeDtypeStruct((t, w), dt) for w, dt in outs],
        compiler_params=_cparams(1),
    )(*[a for a, _, _ in tiled], *consts)


def ew_vjp_call(fn, tiled, consts, cots, d_tiled, d_consts, *, tm, name):
    t = tiled[0][0].shape[0]
    n_t, n_c, n_g = len(tiled), len(consts), len(cots)
    dt_list = [(i, dt) for i, dts in enumerate(d_tiled) for dt in dts]
    dc_list = [i for i, want in enumerate(d_consts) if want]

    def body(*refs):
        tv = [r[...].astype(F32) for r in refs[:n_t]]
        cv = [r[...] for r in refs[n_t:n_t + n_c]]
        gv = tuple(r[...].astype(F32) for r in refs[n_t + n_c:n_t + n_c + n_g])
        out_refs = refs[n_t + n_c + n_g:]
        _, vjp = jax.vjp(fn, *tv, *cv)
        grads = vjp(gv)
        for o_ref, (i, _) in zip(out_refs, dt_list):
            o_ref[...] = grads[i].astype(o_ref.dtype)
        acc_refs = out_refs[len(dt_list):]

        @pl.when(pl.program_id(0) == 0)
        def _():
            for a_ref in acc_refs:
                a_ref[...] = jnp.zeros_like(a_ref)

        for a_ref, i in zip(acc_refs, dc_list):
            a_ref[...] += grads[n_t + i]

    res = pl.pallas_call(
        body,
        name=name,
        grid=(t // tm,),
        in_specs=[_tile_spec(tm, w, cb) for _, w, cb in tiled] + [_const_spec(c) for c in consts]
        + [_tile_spec(tm, w, cb) for _, w, cb in cots],
        out_specs=[_tile_spec(tm, tiled[i][1], 0) for i, _ in dt_list] + [_const_spec(consts[i]) for i in dc_list],
        out_shape=[jax.ShapeDtypeStruct((t, tiled[i][1]), dt) for i, dt in dt_list]
        + [jax.ShapeDtypeStruct(consts[i].shape, F32) for i in dc_list],
        compiler_params=_cparams(1),
    )(*[a for a, _, _ in tiled], *consts, *[a for a, _, _ in cots])
    return res[:len(dt_list)], res[len(dt_list):]


def mm(a, b, mode, *, tm, tn, name, out_dtypes=(F32,), epi=None, extras=()):
    m = a.shape[1] if mode == "tn" else a.shape[0]
    kd = a.shape[0] if mode == "tn" else a.shape[1]
    n = b.shape[0] if mode == "nt" else b.shape[1]
    tm, tn = min(tm, m), min(tn, n)
    if mode == "nn":
        a_spec = pl.BlockSpec((tm, kd), lambda i, j: (i, 0))
        b_spec = pl.BlockSpec((kd, tn), lambda i, j: (0, j))
    elif mode == "nt":
        a_spec = pl.BlockSpec((tm, kd), lambda i, j: (i, 0))
        b_spec = pl.BlockSpec((tn, kd), lambda i, j: (j, 0))
    else:
        a_spec = pl.BlockSpec((kd, tm), lambda i, j: (0, i))
        b_spec = pl.BlockSpec((kd, tn), lambda i, j: (0, j))
    n_e = len(extras)
    o_spec = pl.BlockSpec((tm, tn), lambda i, j: (i, j))

    def body(a_ref, b_ref, *refs):
        c = lax.dot_general(a_ref[...], b_ref[...], _MDIMS[mode], preferred_element_type=F32)
        res = epi(c, *[r[...] for r in refs[:n_e]]) if epi is not None else (c,)
        for o_ref, val in zip(refs[n_e:], res):
            o_ref[...] = val.astype(o_ref.dtype)

    res = pl.pallas_call(
        body,
        name=name,
        grid=(m // tm, n // tn),
        in_specs=[a_spec, b_spec] + [o_spec] * n_e,
        out_specs=[o_spec] * len(out_dtypes),
        out_shape=[jax.ShapeDtypeStruct((m, n), dt) for dt in out_dtypes],
        compiler_params=_cparams(2),
    )(a, b, *extras)
    return res if len(out_dtypes) > 1 else res[0]


P_WIDTH = 7680
RWKV_COL0 = 4096
RWKV_WIDTH = 3584
SHIFT_BLK = 512


def _shift_down(p, prev_row):
    rows = lax.broadcasted_iota(jnp.int32, p.shape, 0)
    return jnp.where(rows == 0, prev_row, pltpu.roll(p, 1, 0))


def shiftmix_fwd(p_all, sbp, *, tm):
    t = p_all.shape[0]
    c0 = RWKV_COL0 // SHIFT_BLK
    hb = tm // 8

    def body(p_ref, halo_ref, sb_ref, q_ref):
        p = p_ref[...]
        prev = jnp.where(pl.program_id(0) == 0, 0.0, halo_ref[7:8, :])
        q_ref[...] = p * sb_ref[0:1, :] + _shift_down(p, prev) * sb_ref[1:2, :]

    return pl.pallas_call(
        body,
        name="shiftmix_fwd",
        grid=(t // tm, RWKV_WIDTH // SHIFT_BLK),
        in_specs=[
            pl.BlockSpec((tm, SHIFT_BLK), lambda i, j: (i, c0 + j)),
            pl.BlockSpec((8, SHIFT_BLK), lambda i, j: (jnp.maximum(i * hb - 1, 0), c0 + j)),
            pl.BlockSpec((2, SHIFT_BLK), lambda i, j: (0, j)),
        ],
        out_specs=pl.BlockSpec((tm, SHIFT_BLK), lambda i, j: (i, j)),
        out_shape=jax.ShapeDtypeStruct((t, RWKV_WIDTH), F32),
        compiler_params=_cparams(2),
    )(p_all, p_all, sbp)


def shiftmix_bwd(dq, col0, p_all, sbp, *, tm, name):
    t, w = dq.shape
    n_i = t // tm
    hb = tm // 8
    cq = col0 // SHIFT_BLK
    cp = (RWKV_COL0 + col0) // SHIFT_BLK

    def body(dq_ref, dqn_ref, p_ref, ph_ref, sb_ref, dp_ref, dsb_ref):
        i = pl.program_id(1)
        dq_t = dq_ref[...]
        rows = lax.broadcasted_iota(jnp.int32, dq_t.shape, 0)
        nxt = jnp.where(i == n_i - 1, 0.0, dqn_ref[0:1, :])
        up = jnp.where(rows == tm - 1, nxt, pltpu.roll(dq_t, tm - 1, 0))
        dp_ref[...] = (dq_t * sb_ref[0:1, :] + up * sb_ref[1:2, :]).astype(dp_ref.dtype)
        p = p_ref[...]
        prev = jnp.where(i == 0, 0.0, ph_ref[7:8, :])
        s0 = jnp.sum(dq_t * p, axis=0, keepdims=True)
        s1 = jnp.sum(dq_t * _shift_down(p, prev), axis=0, keepdims=True)
        two = lax.broadcasted_iota(jnp.int32, (2, SHIFT_BLK), 0)

        @pl.when(i == 0)
        def _():
            dsb_ref[...] = jnp.zeros_like(dsb_ref)

        dsb_ref[...] += jnp.where(two == 0, s0, s1)

    return pl.pallas_call(
        body,
        name=name,
        grid=(w // SHIFT_BLK, n_i),
        in_specs=[
            pl.BlockSpec((tm, SHIFT_BLK), lambda j, i: (i, j)),
            pl.BlockSpec((8, SHIFT_BLK), lambda j, i: (jnp.minimum((i + 1) * hb, t // 8 - 1), j)),
            pl.BlockSpec((tm, SHIFT_BLK), lambda j, i: (i, cp + j)),
            pl.BlockSpec((8, SHIFT_BLK), lambda j, i: (jnp.maximum(i * hb - 1, 0), cp + j)),
            pl.BlockSpec((2, SHIFT_BLK), lambda j, i: (0, cq + j)),
        ],
        out_specs=[
            pl.BlockSpec((tm, SHIFT_BLK), lambda j, i: (i, j)),
            pl.BlockSpec((2, SHIFT_BLK), lambda j, i: (0, j)),
        ],
        out_shape=[jax.ShapeDtypeStruct((t, w), BF16), jax.ShapeDtypeStruct((2, w), F32)],
        compiler_params=_cparams(2),
    )(dq, dq, p_all, p_all, sbp)


def final_call(h1, m3, tgt, g_final, *, tm):
    t = h1.shape[0]

    def body(h1_ref, m3_ref, tgt_ref, g_ref, dh_ref, dhb_ref, dg_ref, loss_ref):
        loss, vjp = jax.vjp(f_final, h1_ref[...], m3_ref[...], tgt_ref[...], g_ref[...])
        dh, _, _, dg = vjp(jnp.ones((), F32))
        dh_ref[...] = dh
        dhb_ref[...] = dh.astype(BF16)

        @pl.when(pl.program_id(0) == 0)
        def _():
            dg_ref[...] = jnp.zeros_like(dg_ref)
            loss_ref[...] = jnp.zeros_like(loss_ref)

        dg_ref[...] += dg
        loss_ref[...] += jnp.full(loss_ref.shape, loss, F32)

    tile = _tile_spec(tm, D_MODEL, 0)
    return pl.pallas_call(
        body,
        name="final_loss",
        grid=(t // tm,),
        in_specs=[tile, tile, tile, _const_spec(g_final)],
        out_specs=[tile, tile, _const_spec(g_final), pl.BlockSpec((8, 128), lambda i: (0, 0))],
        out_shape=[jax.ShapeDtypeStruct((t, D_MODEL), F32), jax.ShapeDtypeStruct((t, D_MODEL), BF16),
                   jax.ShapeDtypeStruct(g_final.shape, F32), jax.ShapeDtypeStruct((8, 128), F32)],
        compiler_params=_cparams(1),
    )(h1, m3, tgt, g_final)


N_SGU = 2048
N_RWKV = 3360
LORA_W, LORA_A, LORA_G = 64, 64, 160


def _pad_rwkv_cols(z):
    zero = lambda n: jnp.zeros(z.shape[:-1] + (n,), z.dtype)
    return jnp.concatenate([z[..., :3072], z[..., 3072:3136], zero(64), z[..., 3136:3200], zero(64),
                            z[..., 3200:3360], zero(96)], axis=-1)


def _unpad_rwkv_cols(z):
    return jnp.concatenate([z[..., :3072], z[..., 3072:3136], z[..., 3200:3264], z[..., 3328:3488]], axis=-1)


def _pad_win(w):
    return jnp.concatenate([w[:, :N_SGU], w[:, N_SGU + N_RWKV:], _pad_rwkv_cols(w[:, N_SGU:N_SGU + N_RWKV])], axis=1)


def _unpad_win(w):
    return jnp.concatenate([w[:, :N_SGU], _unpad_rwkv_cols(w[:, RWKV_COL0:]), w[:, N_SGU:RWKV_COL0]], axis=1)


def _pad_rows(w, n):
    return jnp.concatenate([w, jnp.zeros((n - w.shape[0],) + w.shape[1:], w.dtype)], axis=0)


def _relu2_epi(c):
    return c, jnp.square(jnp.maximum(c, 0.0))


def _relu2_bwd_epi(c, hid):
    return (c * (2.0 * jnp.maximum(hid, 0.0)),)


def _add_epi(c, x):
    return (c + x,)


def _pre_fwd(*args):
    res = f_pre(*args)
    return res[1], res[2], res[4], res[5], res[6]


def local_step(x, tgt, w):
    d = D_MODEL
    win_p = _pad_win(w["w_in"])
    sbp = _pad_rwkv_cols(w["shift_b"])
    wl = _pad_rows(w["w_lora_w"], 128)
    al = _pad_rows(w["a_lora_w"], 128)
    gl = _pad_rows(w["g_lora_w"], 256)
    sbt = w["sgu_b"].T

    (a_bf,) = ew_call(lambda x_, g_: (f_norm_in(x_, g_)[0],), [(x, d, 0)], [w["g_mix"]], [(d, BF16)], tm=256,
                      name="norm_in")
    p_all = mm(a_bf, win_p, "nn", tm=512, tn=1280, name="mm_in")
    sgu_t = [(p_all, 2 * d, 0)]
    sgu_c = [w["sgu_ln_w"], w["sgu_ln_b"], w["sgu_w"], sbt]
    (s_bf,) = ew_call(f_sgu, sgu_t, sgu_c, [(d, BF16)], tm=256, name="sgu_fwd")
    ya = mm(s_bf, w["w_proj_a"], "nn", tm=512, tn=1024, name="mm_proj_a")
    q = shiftmix_fwd(p_all, sbp, tm=256)
    pre_t = [(q, d, 0), (q, d, 1), (q, d, 2), (q, 512, 6)]
    pre_c = [wl, w["w0"], al, w["a0"], gl, w["k_k"], w["k_a"]]
    lw, kp, na, nb, g = ew_call(_pre_fwd, pre_t, pre_c, [(d, F32)] * 5, tm=256, name="rwkv_pre_fwd")
    scan_ops = [(q, 0), (lw, 0), (kp, 0), (q, 2), (na, 0), (nb, 0)]
    o, s0s = scan_fwd(scan_ops)
    post_t = [(o, d, 0), (q, d, 0), (kp, d, 0), (q, d, 2), (g, d, 0)]
    post_c = [w["ln_x_w"], w["ln_x_b"], w["r_k"]]
    (ob_bf,) = ew_call(f_post, post_t, post_c, [(d, BF16)], tm=256, name="rwkv_post_fwd")
    yb = mm(ob_bf, w["w_proj_b"], "nn", tm=512, tn=1024, name="mm_proj_b")
    mix_t = [(ya, d, 0), (yb, d, 0), (p_all, d, 2), (p_all, d, 3)]
    (mixed_bf,) = ew_call(f_mix, mix_t, [], [(d, BF16)], tm=256, name="mix_fwd")
    h1 = mm(mixed_bf, w["w_out"], "nn", tm=512, tn=1024, name="mm_out", epi=_add_epi, extras=(x,))
    (f_bf,) = ew_call(lambda h_, g_: (f_ffn_in(h_, g_)[0],), [(h1, d, 0)], [w["g_ffn"]], [(d, BF16)], tm=256,
                      name="ffn_norm")
    hid, act_bf = mm(f_bf, w["w_ffn1"], "nn", tm=512, tn=1024, name="mm_ffn1", out_dtypes=(F32, BF16), epi=_relu2_epi)
    m3 = mm(act_bf, w["w_ffn2"], "nn", tm=512, tn=1024, name="mm_ffn2")
    dh2, dh2_bf, dg_final, loss = final_call(h1, m3, tgt, w["g_final"], tm=256)

    dhid_bf = mm(dh2_bf, w["w_ffn2"], "nt", tm=512, tn=1024, name="mm_dact", out_dtypes=(BF16,), epi=_relu2_bwd_epi,
                 extras=(hid,))
    d_ffn2 = mm(act_bf, dh2_bf, "tn", tm=512, tn=1024, name="mm_dw_ffn2")
    df = mm(dhid_bf, w["w_ffn1"], "nt", tm=512, tn=1024, name="mm_df")
    d_ffn1 = mm(f_bf, dhid_bf, "tn", tm=512, tn=1024, name="mm_dw_ffn1")
    (dh1, dh1_bf), (dg_ffn,) = ew_vjp_call(f_ffn_in, [(h1, d, 0)], [w["g_ffn"]], [(df, d, 0), (dh2, d, 0)],
                                           [(F32, BF16)], [True], tm=256, name="ffn_norm_bwd")
    dmixed = mm(dh1_bf, w["w_out"], "nt", tm=512, tn=1024, name="mm_dmixed")
    d_out = mm(mixed_bf, dh1_bf, "tn", tm=512, tn=1024, name="mm_dw_out")
    (dya_bf, dyb_bf, dga_bf, dgb_bf), _ = ew_vjp_call(f_mix, mix_t, [], [(dmixed, d, 0)], [(BF16,)] * 4, [], tm=256,
                                                      name="mix_bwd")
    dob = mm(dyb_bf, w["w_proj_b"], "nt", tm=512, tn=1024, name="mm_dob")
    d_proj_b = mm(ob_bf, dyb_bf, "tn", tm=512, tn=1024, name="mm_dw_proj_b")
    (do, dr_p, dkp_p, dv_p, dg), (dlnx_w, dlnx_b, dr_k) = ew_vjp_call(
        f_post, post_t, post_c, [(dob, d, 0)], [(F32,)] * 5, [True] * 3, tm=256, name="rwkv_post_bwd")
    scan_g = scan_bwd(scan_ops, s0s, do)
    pre_g = [(z, d, 0) for z in scan_g] + [(dg, d, 0), (dr_p, d, 0), (dkp_p, d, 0), (dv_p, d, 0)]
    (dq_r, dq_k, dq_v, dq_l), (dwl, dw0, dal, da0, dgl, dk_k, dk_a) = ew_vjp_call(
        f_pre, pre_t, pre_c, pre_g, [(F32,)] * 4, [True] * 7, tm=128, name="rwkv_pre_bwd")
    dp_r, dsb_r = shiftmix_bwd(dq_r, 0, p_all, sbp, tm=256, name="shiftmix_bwd_r")
    dp_k, dsb_k = shiftmix_bwd(dq_k, d, p_all, sbp, tm=256, name="shiftmix_bwd_k")
    dp_v, dsb_v = shiftmix_bwd(dq_v, 2 * d, p_all, sbp, tm=256, name="shiftmix_bwd_v")
    dp_l, dsb_l = shiftmix_bwd(dq_l, 3 * d, p_all, sbp, tm=256, name="shiftmix_bwd_l")
    ds = mm(dya_bf, w["w_proj_a"], "nt", tm=512, tn=1024, name="mm_ds")
    d_proj_a = mm(s_bf, dya_bf, "tn", tm=512, tn=1024, name="mm_dw_proj_a")
    (dp_sgu,), (dln_w, dln_b, dsw, dsbt) = ew_vjp_call(f_sgu, sgu_t, sgu_c, [(ds, d, 0)], [(BF16,)], [True] * 4,
                                                       tm=256, name="sgu_bwd")
    dp_all = jnp.concatenate([dp_sgu, dga_bf, dgb_bf, dp_r, dp_k, dp_v, dp_l], axis=1)
    da = mm(dp_all, win_p, "nt", tm=512, tn=512, name="mm_da")
    d_in_p = mm(a_bf, dp_all, "tn", tm=512, tn=1280, name="mm_dw_in")
    (grad_x,), (dg_mix,) = ew_vjp_call(f_norm_in, [(x, d, 0)], [w["g_mix"]], [(da, d, 0), (dh1, d, 0)], [(F32,)],
                                       [True], tm=256, name="norm_in_bwd")

    grads = {
        "g_mix": dg_mix, "w_in": _unpad_win(d_in_p), "sgu_ln_w": dln_w, "sgu_ln_b": dln_b, "sgu_w": dsw,
        "sgu_b": dsbt.T, "w_proj_a": d_proj_a,
        "shift_b": _unpad_rwkv_cols(jnp.concatenate([dsb_r, dsb_k, dsb_v, dsb_l], axis=1)),
        "w_lora_w": dwl[:LORA_W], "w0": dw0, "a_lora_w": dal[:LORA_A], "a0": da0, "g_lora_w": dgl[:LORA_G],
        "k_k": dk_k, "k_a": dk_a, "r_k": dr_k, "ln_x_w": dlnx_w, "ln_x_b": dlnx_b, "w_proj_b": d_proj_b,
        "w_out": d_out, "g_ffn": dg_ffn, "w_ffn1": d_ffn1, "w_ffn2": d_ffn2, "g_final": dg_final,
    }
    return loss[0, 0], grad_x, grads


MESH = pl.DeviceIdType.MESH
N_CHIPS = 4
N_DEV = 8
PACK_ROWS = 5152
HALF_ROWS = PACK_ROWS // 2
GATHER_ROWS = 5472
PACK_TILE = 368
SMALL_ROWS = 152
_ANY = pl.BlockSpec(memory_space=pl.ANY)


def _coords():
    return lax.axis_index("x"), lax.axis_index("y"), lax.axis_index("c")


def _other_chips(x, y):
    return [(1 - x, y), (x, 1 - y), (1 - x, 1 - y)]


def _remote(src, dst, send_sems, recv_sems, k, to):
    return pltpu.make_async_remote_copy(src_ref=src, dst_ref=dst, send_sem=send_sems.at[k], recv_sem=recv_sems.at[k],
                                        device_id=to, device_id_type=MESH)


def gather_shards(pack):
    def body(src_ref, out_ref, send_sems, recv_sems):
        x, y, c = _coords()
        me = 2 * x + y
        sib = (x, y, 1 - c)
        chips = _other_chips(x, y)
        first = [_remote(src_ref.at[c], out_ref.at[me, c], send_sems, recv_sems, k, (cx, cy, c))
                 for k, (cx, cy) in enumerate(chips)]
        for cp in first:
            cp.start()
        passed = []
        for k, (cx, cy) in enumerate(chips):
            j = 2 * cx + cy
            _remote(src_ref.at[c], out_ref.at[j, c], send_sems, recv_sems, k, (cx, cy, c)).wait_recv()
            fwd = _remote(out_ref.at[j, c], out_ref.at[j, c], send_sems, recv_sems, 3 + k, sib)
            fwd.start()
            passed.append(fwd)
        for k, (cx, cy) in enumerate(chips):
            j = 2 * cx + cy
            _remote(out_ref.at[j, 1 - c], out_ref.at[j, 1 - c], send_sems, recv_sems, 3 + k, sib).wait_recv()
        for cp in first + passed:
            cp.wait_send()

    return pl.pallas_call(
        body,
        name="gather_shards",
        in_specs=[_ANY],
        out_specs=_ANY,
        out_shape=jax.ShapeDtypeStruct((N_CHIPS,) + pack.shape, pack.dtype),
        scratch_shapes=[pltpu.SemaphoreType.DMA((6,)), pltpu.SemaphoreType.DMA((6,))],
    )(pack)


def reduce_pair(g):
    def body(g_ref, got_ref, send_sems, recv_sems):
        x, y, c = _coords()
        sib = (x, y, 1 - c)
        sends = [_remote(g_ref.at[j, 1 - c], got_ref.at[j], send_sems, recv_sems, j, sib) for j in range(N_CHIPS)]
        for cp in sends:
            cp.start()
        for cp in sends:
            cp.wait_recv()
        for cp in sends:
            cp.wait_send()

    return pl.pallas_call(
        body,
        name="reduce_pair",
        in_specs=[_ANY],
        out_specs=_ANY,
        out_shape=jax.ShapeDtypeStruct((N_CHIPS,) + g.shape[2:], g.dtype),
        scratch_shapes=[pltpu.SemaphoreType.DMA((N_CHIPS,)), pltpu.SemaphoreType.DMA((N_CHIPS,))],
    )(g)


def pair_sum(g, got, *, tm):
    n, _, rows, width = g.shape

    def body(g0_ref, g1_ref, got_ref, out_ref, out16_ref):
        own = jnp.where(lax.axis_index("c") == 0, g0_ref[0, 0], g1_ref[0, 0])
        total = own + got_ref[0]
        out_ref[0] = total
        out16_ref[0] = total.astype(BF16)

    blk = pl.BlockSpec((1, tm, width), lambda j, i: (j, i, 0))
    return pl.pallas_call(
        body,
        name="pair_sum",
        grid=(n, rows // tm),
        in_specs=[pl.BlockSpec((1, 1, tm, width), lambda j, i: (j, 0, i, 0)),
                  pl.BlockSpec((1, 1, tm, width), lambda j, i: (j, 1, i, 0)), blk],
        out_specs=[blk, blk],
        out_shape=[jax.ShapeDtypeStruct(got.shape, F32), jax.ShapeDtypeStruct(got.shape, BF16)],
        compiler_params=_cparams(2),
    )(g, g, got)


def reduce_chips(p):
    def body(p_ref, out_ref, send_sems, recv_sems):
        x, y, c = _coords()
        me = 2 * x + y
        chips = _other_chips(x, y)
        sends = [_remote(p_ref.at[2 * cx + cy], out_ref.at[me], send_sems, recv_sems, k, (cx, cy, c))
                 for k, (cx, cy) in enumerate(chips)]
        for cp in sends:
            cp.start()
        for k, (cx, cy) in enumerate(chips):
            _remote(p_ref.at[me], out_ref.at[2 * cx + cy], send_sems, recv_sems, k, (cx, cy, c)).wait_recv()
        for cp in sends:
            cp.wait_send()

    return pl.pallas_call(
        body,
        name="reduce_chips",
        in_specs=[_ANY],
        out_specs=_ANY,
        out_shape=jax.ShapeDtypeStruct(p.shape, p.dtype),
        scratch_shapes=[pltpu.SemaphoreType.DMA((3,)), pltpu.SemaphoreType.DMA((3,))],
    )(p)


def sum_with_own(own, slots, index_fn, *, tm, name):
    n, rows, width = slots.shape
    own3 = own.ndim == 3

    def body(*refs):
        mine = index_fn()
        acc = None
        for s in range(n):
            o = refs[s][0] if own3 else refs[0][...]
            term = jnp.where(mine == s, o, refs[(n if own3 else 1) + s][0].astype(F32))
            acc = term if acc is None else acc + term
        refs[-1][...] = acc

    slot_specs = [pl.BlockSpec((1, tm, width), lambda i, s=s: (s, i, 0)) for s in range(n)]
    own_specs = slot_specs if own3 else [pl.BlockSpec((tm, width), lambda i: (i, 0))]
    return pl.pallas_call(
        body,
        name=name,
        grid=(rows // tm,),
        in_specs=own_specs + slot_specs,
        out_specs=pl.BlockSpec((tm, width), lambda i: (i, 0)),
        out_shape=jax.ShapeDtypeStruct((rows, width), F32),
        compiler_params=_cparams(1),
    )(*([own] * (n if own3 else 1)), *([slots] * n))


def exchange_halves(s):
    rq = PACK_TILE
    nq = s.shape[0] // rq

    def body(s_ref, out_ref, sbuf, rbuf, send_sems, recv_sems, in_sems, out_sems):
        x, y, c = _coords()
        sib = (x, y, 1 - c)
        rows = lambda q: pl.ds(q * rq, rq)
        loads = [pltpu.make_async_copy(s_ref.at[rows(q)], sbuf.at[rows(q)], in_sems.at[q]) for q in range(nq)]
        for cp in loads:
            cp.start()
        sends = []
        for q in range(nq):
            loads[q].wait()
            sends.append(_remote(sbuf.at[rows(q)], rbuf.at[rows(q)], send_sems, recv_sems, q, sib))
            sends[q].start()
        stores = []
        for q in range(nq):
            sends[q].wait_recv()
            stores.append(pltpu.make_async_copy(rbuf.at[rows(q)], out_ref.at[rows(q)], out_sems.at[q]))
            stores[q].start()
        for cp in sends:
            cp.wait_send()
        for cp in stores:
            cp.wait()

    return pl.pallas_call(
        body,
        name="exchange_halves",
        in_specs=[_ANY],
        out_specs=_ANY,
        out_shape=jax.ShapeDtypeStruct(s.shape, s.dtype),
        scratch_shapes=[pltpu.VMEM(s.shape, s.dtype), pltpu.VMEM(s.shape, s.dtype)]
        + [pltpu.SemaphoreType.DMA((nq,))] * 4,
        compiler_params=pltpu.CompilerParams(vmem_limit_bytes=VMEM_LIMIT),
    )(s)


def gather_all(s):
    def body(s_ref, out_ref, send_sems, recv_sems):
        x, y, c = _coords()
        me = 4 * x + 2 * y + c
        peers = []
        for mask in range(1, N_DEV):
            px = 1 - x if mask & 4 else x
            py = 1 - y if mask & 2 else y
            pc = 1 - c if mask & 1 else c
            peers.append((px, py, pc))
        sends = [_remote(s_ref, out_ref.at[me], send_sems, recv_sems, k, peer) for k, peer in enumerate(peers)]
        for cp in sends:
            cp.start()
        for k, (px, py, pc) in enumerate(peers):
            _remote(s_ref, out_ref.at[4 * px + 2 * py + pc], send_sems, recv_sems, k, (px, py, pc)).wait_recv()
        for cp in sends:
            cp.wait_send()

    return pl.pallas_call(
        body,
        name="gather_all",
        in_specs=[_ANY],
        out_specs=_ANY,
        out_shape=jax.ShapeDtypeStruct((N_DEV,) + s.shape, s.dtype),
        scratch_shapes=[pltpu.SemaphoreType.DMA((N_DEV - 1,)), pltpu.SemaphoreType.DMA((N_DEV - 1,))],
    )(s)


ADAM_LR = 0.001
ADAM_B1 = 0.9
ADAM_B2 = 0.999
ADAM_EPS = 1e-08
ADAM_WD = 0.01
ADAM_STEP = 10


def f_adamw(g, w, m, v):
    m = ADAM_B1 * m + (1.0 - ADAM_B1) * g
    v = ADAM_B2 * v + (1.0 - ADAM_B2) * jnp.square(g)
    m_hat = m / (1.0 - ADAM_B1 ** ADAM_STEP)
    v_hat = v / (1.0 - ADAM_B2 ** ADAM_STEP)
    delta = -ADAM_LR * (m_hat / (jnp.sqrt(v_hat) + ADAM_EPS) + ADAM_WD * w)
    return delta, m, v


def adamw_call(g, w, m, v, *, tm, name):
    width = g.shape[1]
    return ew_call(f_adamw, [(g, width, 0), (w, width, 0), (m, width, 0), (v, width, 0)], [], [(width, F32)] * 3,
                   tm=tm, name=name)


def adamw_halves(g_own, g_other, w, m, v, *, tm):
    _, rows, width = w.shape

    def body(go_ref, gx_ref, w_ref, m_ref, v_ref, g_ref, d_ref, nm_ref, nv_ref):
        g = jnp.where(pl.program_id(0) == lax.axis_index("c"), go_ref[...], gx_ref[...])
        delta, nm, nv = f_adamw(g, w_ref[0], m_ref[0], v_ref[0])
        g_ref[0] = g
        d_ref[0] = delta
        nm_ref[0] = nm
        nv_ref[0] = nv

    half = pl.BlockSpec((tm, width), lambda h, i: (i, 0))
    full = pl.BlockSpec((1, tm, width), lambda h, i: (h, i, 0))
    return pl.pallas_call(
        body,
        name="adamw_sharded",
        grid=(2, rows // tm),
        in_specs=[half, half, full, full, full],
        out_specs=[full] * 4,
        out_shape=[jax.ShapeDtypeStruct(w.shape, F32)] * 4,
        compiler_params=_cparams(2),
    )(g_own, g_other, w, m, v)


SHARDED = ["w_in", "w_proj_a", "w_proj_b", "w_out", "w_ffn1", "w_ffn2", "w_lora_w", "a_lora_w", "g_lora_w"]
LORAS = ["w_lora_w", "a_lora_w", "g_lora_w"]
SHARD_ROWS = {"w_in": 2048, "w_proj_a": 256, "w_proj_b": 256, "w_out": 256, "w_ffn1": 1024, "w_ffn2": 1024,
              "w_lora_w": 64, "a_lora_w": 64, "g_lora_w": 160}
SHARD_AXIS = {"w_in": 1, "w_proj_a": 0, "w_lora_w": 1, "a_lora_w": 1, "g_lora_w": 1, "w_proj_b": 0, "w_out": 0,
              "w_ffn1": 1, "w_ffn2": 0}
SHARD_SHAPE = {"w_in": (1024, 1864), "w_proj_a": (256, 1024), "w_lora_w": (64, 256), "a_lora_w": (64, 256),
               "g_lora_w": (160, 256), "w_proj_b": (256, 1024), "w_out": (256, 1024), "w_ffn1": (1024, 1024),
               "w_ffn2": (1024, 1024)}
SHIFT_SHARD = (2, 840)
VECTORS = ["g_mix", "sgu_ln_w", "sgu_ln_b", "w0", "a0", "k_k", "k_a", "r_k", "ln_x_w", "ln_x_b", "g_ffn", "g_final"]
SMALL = VECTORS + ["sgu_w", "sgu_b"]
SMALL_SHAPE = {**{n: (1, 1024) for n in VECTORS}, "sgu_w": (8, 128, 128), "sgu_b": (8, 128)}
WEIGHTS = ["g_mix", "w_in", "sgu_ln_w", "sgu_ln_b", "sgu_w", "sgu_b", "w_proj_a", "shift_b", "w_lora_w", "w0",
           "a_lora_w", "a0", "g_lora_w", "k_k", "k_a", "r_k", "ln_x_w", "ln_x_b", "w_proj_b", "w_out", "g_ffn",
           "w_ffn1", "w_ffn2", "g_final"]


def _size(shape):
    n = 1
    for s in shape:
        n *= s
    return n


def _pack_rows(parts, rows, dtype):
    flat = jnp.concatenate([p.reshape(-1).astype(dtype) for p in parts])
    return jnp.concatenate([flat, jnp.zeros((rows * 1024 - flat.shape[0],), dtype)]).reshape(rows, 1024)


def _unpack_rows(packed, shapes):
    flat = packed.reshape(-1)
    out, off = [], 0
    for shp in shapes:
        out.append(flat[off:off + _size(shp)].reshape(shp))
        off += _size(shp)
    return out


def _shard_of(name, full, j):
    ax = SHARD_AXIS[name]
    n = SHARD_SHAPE[name][ax]
    return lax.slice_in_dim(full, j * n, (j + 1) * n, axis=ax)


def _pad_cols(z, n=1024):
    return jnp.concatenate([z, jnp.zeros((z.shape[0], n - z.shape[1]), z.dtype)], axis=1)


def _shard_rows(name, s):
    if name == "w_in":
        return jnp.concatenate([s[:, :1024], _pad_cols(s[:, 1024:])], axis=0)
    return s if s.shape[1] == 1024 else _pad_cols(s)


def _rows_shard(name, r):
    if name == "w_in":
        return jnp.concatenate([r[:1024], r[1024:, :SHARD_SHAPE[name][1] - 1024]], axis=1)
    return r[:, :SHARD_SHAPE[name][1]]


def _pack_shard(shard_fn, dtype):
    return jnp.concatenate([_shard_rows(n, shard_fn(n)).astype(dtype) for n in SHARDED], axis=0)


def _unpack_shard(packed):
    out, off = {}, 0
    for n in SHARDED:
        out[n] = _rows_shard(n, packed[off:off + SHARD_ROWS[n]])
        off += SHARD_ROWS[n]
    return out


def kernel(x, g_mix, w_in, sgu_ln_w, sgu_ln_b, sgu_w, sgu_b, w_proj_a, shift_b, w_lora_w, w0, a_lora_w, a0, g_lora_w, k_k, k_a, r_k, ln_x_w, ln_x_b, w_proj_b, w_out, g_ffn, w_ffn1, w_ffn2, g_final, loss_target, m_g_mix, m_w_in, m_sgu_ln_w, m_sgu_ln_b, m_sgu_w, m_sgu_b, m_w_proj_a, m_shift_b, m_w_lora_w, m_w0, m_a_lora_w, m_a0, m_g_lora_w, m_k_k, m_k_a, m_r_k, m_ln_x_w, m_ln_x_b, m_w_proj_b, m_w_out, m_g_ffn, m_w_ffn1, m_w_ffn2, m_g_final, v_g_mix, v_w_in, v_sgu_ln_w, v_sgu_ln_b, v_sgu_w, v_sgu_b, v_w_proj_a, v_shift_b, v_w_lora_w, v_w0, v_a_lora_w, v_a0, v_g_lora_w, v_k_k, v_k_a, v_r_k, v_ln_x_w, v_ln_x_b, v_w_proj_b, v_w_out, v_g_ffn, v_w_ffn1, v_w_ffn2, v_g_final):
    given = dict(zip(WEIGHTS, (g_mix, w_in, sgu_ln_w, sgu_ln_b, sgu_w, sgu_b, w_proj_a, shift_b, w_lora_w, w0, a_lora_w, a0, g_lora_w, k_k, k_a, r_k, ln_x_w, ln_x_b, w_proj_b, w_out, g_ffn, w_ffn1, w_ffn2, g_final)))
    mom_m = dict(zip(WEIGHTS, (m_g_mix, m_w_in, m_sgu_ln_w, m_sgu_ln_b, m_sgu_w, m_sgu_b, m_w_proj_a, m_shift_b, m_w_lora_w, m_w0, m_a_lora_w, m_a0, m_g_lora_w, m_k_k, m_k_a, m_r_k, m_ln_x_w, m_ln_x_b, m_w_proj_b, m_w_out, m_g_ffn, m_w_ffn1, m_w_ffn2, m_g_final)))
    mom_v = dict(zip(WEIGHTS, (v_g_mix, v_w_in, v_sgu_ln_w, v_sgu_ln_b, v_sgu_w, v_sgu_b, v_w_proj_a, v_shift_b, v_w_lora_w, v_w0, v_a_lora_w, v_a0, v_g_lora_w, v_k_k, v_k_a, v_r_k, v_ln_x_w, v_ln_x_b, v_w_proj_b, v_w_out, v_g_ffn, v_w_ffn1, v_w_ffn2, v_g_final)))
    chip = 2 * lax.axis_index("x") + lax.axis_index("y")

    def local_block(tree, n):
        return tree[n] if n == "g_final" else tree[n][0]

    sb = local_block(given, "shift_b")
    lo_part = lambda z: (z - z.astype(BF16).astype(F32)).astype(BF16)
    extra = [_pad_cols(lo_part(local_block(given, n))) for n in LORAS]
    tile16 = lambda z: jnp.pad(z, ((0, 16 - z.shape[0]), (0, 1024 - z.shape[1])))
    extra += [tile16(sb.astype(BF16)), tile16(lo_part(sb))]
    pack_w = jnp.concatenate([_pack_shard(lambda n: local_block(given, n), BF16)] + extra, axis=0)
    gathered = gather_shards(pack_w.reshape(2, GATHER_ROWS // 2, 1024)).reshape(N_CHIPS, GATHER_ROWS, 1024)
    gathered = lax.dynamic_update_index_in_dim(gathered, pack_w, chip, 0)
    per_chip = [gathered[j] for j in range(N_CHIPS)]
    shards = [_unpack_shard(p) for p in per_chip]
    w = {n: jnp.concatenate([shards[j][n] for j in range(N_CHIPS)], axis=SHARD_AXIS[n]) for n in SHARDED}
    off = PACK_ROWS
    for n in LORAS:
        rows, cols = SHARD_SHAPE[n]
        lo = jnp.concatenate([p[off:off + rows, :cols] for p in per_chip], axis=1)
        w[n] = w[n].astype(F32) + lo.astype(F32)
        off += rows
    w["shift_b"] = jnp.concatenate(
        [p[off:off + 2, :SHIFT_SHARD[1]].astype(F32) + p[off + 16:off + 18, :SHIFT_SHARD[1]].astype(F32)
         for p in per_chip], axis=1)
    for n in SMALL:
        w[n] = local_block(given, n).reshape(SMALL_SHAPE[n])

    loss, grad_x, grads = local_step(x[0], loss_target[0], w)
    loss = lax.psum(loss, ("x", "y", "c"))

    g_pack = jnp.stack([_pack_shard(lambda n: _shard_of(n, grads[n], j), F32) for j in range(N_CHIPS)])
    g_pack = g_pack.reshape(N_CHIPS, 2, HALF_ROWS, 1024)
    chip_part, chip_part16 = pair_sum(g_pack, reduce_pair(g_pack), tm=PACK_TILE)
    half_sum = sum_with_own(chip_part, reduce_chips(chip_part16),
                            lambda: 2 * lax.axis_index("x") + lax.axis_index("y"), tm=PACK_TILE, name="chip_sum")
    other_half = exchange_halves(half_sum)
    halves = lambda tree: _pack_shard(lambda n: local_block(tree, n), F32).reshape(2, HALF_ROWS, 1024)
    big = adamw_halves(half_sum, other_half, halves(given), halves(mom_m), halves(mom_v), tm=PACK_TILE)
    out_g, out_d, out_m, out_v = [_unpack_shard(z.reshape(PACK_ROWS, 1024)) for z in big]

    small_shapes = [SMALL_SHAPE[n] for n in SMALL]
    s_pack = _pack_rows([grads[n] for n in SMALL] + [grads["shift_b"]], SMALL_ROWS, F32)
    g_small = sum_with_own(
        s_pack, gather_all(s_pack), lambda: 4 * lax.axis_index("x") + 2 * lax.axis_index("y") + lax.axis_index("c"),
        tm=SMALL_ROWS, name="small_sum")
    w_small = _pack_rows([local_block(given, n) for n in SMALL], SMALL_ROWS, F32)
    m_small = _pack_rows([local_block(mom_m, n) for n in SMALL], SMALL_ROWS, F32)
    v_small = _pack_rows([local_block(mom_v, n) for n in SMALL], SMALL_ROWS, F32)
    d_small, nm_small, nv_small = adamw_call(g_small, w_small, m_small, v_small, tm=SMALL_ROWS, name="adamw_small")
    g_parts = _unpack_rows(g_small, small_shapes + [(2, N_RWKV)])
    out_g.update(zip(SMALL, g_parts[:-1]))
    out_d.update(zip(SMALL, _unpack_rows(d_small, small_shapes)))
    out_m.update(zip(SMALL, _unpack_rows(nm_small, small_shapes)))
    out_v.update(zip(SMALL, _unpack_rows(nv_small, small_shapes)))
    g_sb = lax.dynamic_slice_in_dim(g_parts[-1], chip * SHIFT_SHARD[1], SHIFT_SHARD[1], axis=1)
    sb_args = [_pack_rows([z], 8, F32) for z in (g_sb, sb, local_block(mom_m, "shift_b"), local_block(mom_v, "shift_b"))]
    sb_res = adamw_call(*sb_args, tm=8, name="adamw_shift_b")
    out_g["shift_b"] = g_sb
    for tree, res in zip((out_d, out_m, out_v), sb_res):
        tree["shift_b"] = _unpack_rows(res, [SHIFT_SHARD])[0]

    def block_of(tree, n):
        return tree[n].reshape(given[n].shape)

    return (loss, grad_x[None], *[block_of(out_g, n) for n in WEIGHTS], *[block_of(out_d, n) for n in WEIGHTS],
            *[block_of(out_m, n) for n in WEIGHTS], *[block_of(out_v, n) for n in WEIGHTS])
```

```python
import functools

import jax
import jax.numpy as jnp
from jax import lax
from jax.experimental import pallas as pl
from jax.experimental.pallas import tpu as pltpu

F32 = jnp.float32
BF16 = jnp.bfloat16

D_MODEL = 1024
N_HEADS = 16
HEAD = 64
SCAN_CHUNK = 64

VMEM_LIMIT = 56 * 1024 * 1024


_BDIMS = {
    "nn": (((2,), (1,)), ((0,), (0,))),
    "nt": (((2,), (2,)), ((0,), (0,))),
    "tn": (((1,), (1,)), ((0,), (0,))),
}


def _raw_bdot(x, y, mode, fine):
    if fine:
        return lax.dot_general(x, y, _BDIMS[mode], precision=lax.Precision.HIGH, preferred_element_type=F32)
    return lax.dot_general(x.astype(BF16), y.astype(BF16), _BDIMS[mode], preferred_element_type=F32)


@functools.partial(jax.custom_vjp, nondiff_argnums=(2, 3))
def bdot(x, y, mode, fine=True):
    return _raw_bdot(x, y, mode, fine)


def _bdot_fwd(x, y, mode, fine):
    return _raw_bdot(x, y, mode, fine), (x, y)


def _bdot_bwd(mode, fine, res, g):
    x, y = res
    if mode == "nn":
        return bdot(g, y, "nt", fine), bdot(x, g, "tn", fine)
    if mode == "nt":
        return bdot(g, y, "nn", fine), bdot(g, x, "tn", fine)
    return bdot(y, g, "nt", fine), bdot(x, g, "nn", fine)


bdot.defvjp(_bdot_fwd, _bdot_bwd)


def _scan_chunk(S0, r, lw, k, v, a, b):
    nh, lc, _ = r.shape
    ti = lax.broadcasted_iota(jnp.int32, (lc, lc), 0)
    si = lax.broadcasted_iota(jnp.int32, (lc, lc), 1)
    incl = (si <= ti).astype(F32)
    strict = (si < ti).astype(F32)
    eye = (si == ti).astype(F32)
    cl = bdot(jnp.broadcast_to(incl, (nh, lc, lc)), lw, "nn")
    cl_last = cl[:, lc - 1:lc, :]
    g_last = jnp.exp(cl_last - cl)
    at = a * jnp.exp(cl - lw)
    bt = b * jnp.exp(-cl)
    kt = k * jnp.exp(-cl)
    rt = r * jnp.exp(cl)
    ar = jnp.concatenate([at, rt], axis=1)
    ar_b = bdot(ar, bt, "nt")
    ar_k = bdot(ar, kt, "nt")
    m_ab, m_rb = ar_b[:, :lc] * strict, ar_b[:, lc:] * incl
    m_ak, m_rk = ar_k[:, :lc] * strict, ar_k[:, lc:] * incl
    x = eye + m_ab
    p = bdot(m_ab, m_ab, "nn", False)
    n = 2
    while n * 2 < lc:
        px = bdot(jnp.concatenate([p, x], axis=1), p, "nn", False)
        p = px[:, :lc]
        x = x + px[:, lc:]
        n *= 2
    x = x + bdot(x, p, "nn", False)
    ar_s = bdot(ar, S0, "nt", False)
    akrk_v = bdot(jnp.concatenate([m_ak, m_rk], axis=1), v, "nn")
    u = bdot(x, ar_s[:, :lc] + akrk_v[:, :lc], "nn", False)
    o = ar_s[:, lc:] + bdot(m_rb, u, "nn", False) + akrk_v[:, lc:]
    s_last = S0 * jnp.exp(cl_last) + bdot(jnp.concatenate([u, v], axis=1),
                                          jnp.concatenate([b * g_last, k * g_last], axis=1), "tn", False)
    return o, s_last


def _split_heads(z):
    return jnp.stack([z[:, HEAD * h:HEAD * (h + 1)] for h in range(N_HEADS)], axis=0)


def _merge_heads(z):
    return jnp.concatenate([z[h] for h in range(N_HEADS)], axis=1)


def _scan_specs(t, ops, rev):
    nc = t // SCAN_CHUNK
    row = (lambda c: nc - 1 - c) if rev else (lambda c: c)
    specs = [pl.BlockSpec((SCAN_CHUNK, D_MODEL), lambda c, cb=cb: (row(c), cb)) for _, cb in ops]
    state = pl.BlockSpec((1, N_HEADS, HEAD, HEAD), lambda c: (row(c), 0, 0, 0))
    return nc, specs, state


def scan_fwd(ops):
    t = ops[0][0].shape[0]
    nc, specs, state = _scan_specs(t, ops, False)

    def body(r_ref, lw_ref, k_ref, v_ref, a_ref, b_ref, o_ref, s0_ref, s_scr):
        @pl.when(pl.program_id(0) == 0)
        def _():
            s_scr[...] = jnp.zeros_like(s_scr)

        s0 = s_scr[...]
        s0_ref[0] = s0
        o, s_last = _scan_chunk(s0, *[_split_heads(z[...]) for z in (r_ref, lw_ref, k_ref, v_ref, a_ref, b_ref)])
        o_ref[...] = _merge_heads(o)
        s_scr[...] = s_last

    return pl.pallas_call(
        body,
        name="scan_fwd",
        grid=(nc,),
        in_specs=specs,
        out_specs=[pl.BlockSpec((SCAN_CHUNK, D_MODEL), lambda c: (c, 0)), state],
        out_shape=[jax.ShapeDtypeStruct((t, D_MODEL), F32), jax.ShapeDtypeStruct((nc, N_HEADS, HEAD, HEAD), F32)],
        scratch_shapes=[pltpu.VMEM((N_HEADS, HEAD, HEAD), F32)],
        compiler_params=_cparams(1),
    )(*[a for a, _ in ops])


def scan_bwd(ops, s0s, do):
    t = ops[0][0].shape[0]
    nc, specs, state = _scan_specs(t, ops + [(do, 0)], True)

    def body(r_ref, lw_ref, k_ref, v_ref, a_ref, b_ref, do_ref, s0_ref, *rest):
        out_refs, ds_scr = rest[:6], rest[6]

        @pl.when(pl.program_id(0) == 0)
        def _():
            ds_scr[...] = jnp.zeros_like(ds_scr)

        _, vjp = jax.vjp(_scan_chunk, s0_ref[0],
                         *[_split_heads(z[...]) for z in (r_ref, lw_ref, k_ref, v_ref, a_ref, b_ref)])
        grads = vjp((_split_heads(do_ref[...]), ds_scr[...]))
        for o_ref, g in zip(out_refs, grads[1:]):
            o_ref[...] = _merge_heads(g)
        ds_scr[...] = grads[0]

    return pl.pallas_call(
        body,
        name="scan_bwd",
        grid=(nc,),
        in_specs=specs + [state],
        out_specs=[pl.BlockSpec((SCAN_CHUNK, D_MODEL), lambda c: (nc - 1 - c, 0))] * 6,
        out_shape=[jax.ShapeDtypeStruct((t, D_MODEL), F32)] * 6,
        scratch_shapes=[pltpu.VMEM((N_HEADS, HEAD, HEAD), F32)],
        compiler_params=_cparams(1),
    )(*[a for a, _ in ops], do, s0s)


_MDIMS = {
    "nn": (((1,), (0,)), ((), ())),
    "nt": (((1,), (1,)), ((), ())),
    "tn": (((0,), (0,)), ((), ())),
}


def _raw_mdot(x, y, mode, exact):
    if exact:
        return lax.dot_general(x, y, _MDIMS[mode], precision=lax.Precision.HIGH, preferred_element_type=F32)
    return lax.dot_general(x.astype(BF16), y.astype(BF16), _MDIMS[mode], preferred_element_type=F32)


@functools.partial(jax.custom_vjp, nondiff_argnums=(2, 3))
def mdot(x, y, mode, exact):
    return _raw_mdot(x, y, mode, exact)


def _mdot_fwd(x, y, mode, exact):
    return _raw_mdot(x, y, mode, exact), (x, y)


def _mdot_bwd(mode, exact, res, g):
    x, y = res
    if mode == "nn":
        return mdot(g, y, "nt", exact), mdot(x, g, "tn", exact)
    if mode == "nt":
        return mdot(g, y, "nn", exact), mdot(g, x, "tn", exact)
    return mdot(y, g, "nt", exact), mdot(x, g, "nn", exact)


mdot.defvjp(_mdot_fwd, _mdot_bwd)


def _seg_ones():
    i = lax.broadcasted_iota(jnp.int32, (256, 256), 0) // HEAD
    j = lax.broadcasted_iota(jnp.int32, (256, 256), 1) // HEAD
    return (i == j).astype(BF16)


@jax.custom_vjp
def segsum(x):
    bd = _seg_ones()
    hi = x.astype(BF16)
    lo = (x - hi.astype(F32)).astype(BF16)
    cols = []
    for j in range(x.shape[1] // 256):
        sl = slice(256 * j, 256 * (j + 1))
        cols.append(jnp.dot(hi[:, sl], bd, preferred_element_type=F32)
                    + jnp.dot(lo[:, sl], bd, preferred_element_type=F32))
    return jnp.concatenate(cols, axis=1)


segsum.defvjp(lambda x: (segsum(x), None), lambda _, g: (segsum(g),))


NORM_EPS = 1e-6
LN_EPS = 1e-5
GN_EPS = 64e-5
SGU_CHUNK = 128
SGU_GROUPS = 8


def _rms(x, g):
    return x * lax.rsqrt(jnp.mean(x * x, axis=-1, keepdims=True) + NORM_EPS) * g


def f_norm_in(x, g):
    return _rms(x, g), x


def f_sgu(p, ln_w, ln_b, sw, sbt):
    tm = p.shape[0]
    z = 0.5 * p * (1.0 + lax.erf(p * 0.7071067811865476))
    u, v = z[:, :D_MODEL], z[:, D_MODEL:]
    mu = jnp.mean(v, axis=-1, keepdims=True)
    d = v - mu
    vn = d * lax.rsqrt(jnp.mean(d * d, axis=-1, keepdims=True) + LN_EPS) * ln_w + ln_b
    ii = lax.broadcasted_iota(jnp.int32, (SGU_CHUNK, SGU_CHUNK), 0)
    jj = lax.broadcasted_iota(jnp.int32, (SGU_CHUNK, SGU_CHUNK), 1)
    mask = (jj <= ii).astype(F32)
    gi = lax.broadcasted_iota(jnp.int32, (SGU_GROUPS, D_MODEL), 0)
    ci = lax.broadcasted_iota(jnp.int32, (SGU_GROUPS, D_MODEL), 1) // SGU_CHUNK
    bias = mdot(sbt, (gi == ci).astype(F32), "nn", True)
    rows = []
    for c in range(tm // SGU_CHUNK):
        cols = []
        for g in range(SGU_GROUPS):
            blk = vn[c * SGU_CHUNK:(c + 1) * SGU_CHUNK, g * SGU_CHUNK:(g + 1) * SGU_CHUNK]
            cols.append(mdot(sw[g] * mask, blk, "nn", False))
        rows.append(jnp.concatenate(cols, axis=1) + bias)
    return (u * jnp.concatenate(rows, axis=0),)


def _softplus(x):
    return jnp.maximum(x, 0.0) + jnp.log1p(jnp.exp(-jnp.abs(x)))


def f_pre(qr, qk, qv, ql, wl, w0, al, a0, gl, k_k, k_a):
    xw, xa, xg = ql[:, :128], ql[:, 128:256], ql[:, 256:512]
    wr = -_softplus(-(w0 + mdot(jnp.tanh(xw), wl, "nn", True))) - 0.5
    lw = -jnp.exp(wr)
    aa = jax.nn.sigmoid(a0 + mdot(xa, al, "nn", True))
    g = mdot(jax.nn.sigmoid(xg), gl, "nn", True)
    kkr = qk * k_k
    kk = kkr / jnp.maximum(jnp.sqrt(segsum(kkr * kkr)), 1e-12)
    kp = qk * (1.0 + (aa - 1.0) * k_a)
    return qr, lw, kp, qv, -kk, kk * aa, g, qr, kp, qv


def f_post(o, r, kp, v, g, lnw, lnb, rk):
    mu = segsum(o) * (1.0 / HEAD)
    d = o - mu
    gn = d * lax.rsqrt(segsum(d * d) * (1.0 / HEAD) + GN_EPS)
    return ((gn * lnw + lnb + segsum(r * kp * rk) * v) * g,)


def f_mix(ya, yb, ga, gb):
    return (jax.nn.sigmoid(ga) * ya + jax.nn.sigmoid(gb) * yb,)


def f_ffn_in(h1, g):
    return _rms(h1, g), h1


def f_final(h1, m3, tgt, g):
    y = _rms(h1 + m3, g)
    err = jnp.square(y - tgt)
    return 0.5 * jnp.sum(jnp.mean(err, axis=-1))


def _cparams(n_grid):
    return pltpu.CompilerParams(dimension_semantics=("arbitrary",) * n_grid, vmem_limit_bytes=VMEM_LIMIT)


def _tile_spec(tm, w, cb):
    return pl.BlockSpec((tm, w), lambda i: (i, cb))


def _const_spec(c):
    nd = c.ndim
    return pl.BlockSpec(c.shape, lambda i: (0,) * nd)


def ew_call(fn, tiled, consts, outs, *, tm, name):
    t = tiled[0][0].shape[0]
    n_t, n_c = len(tiled), len(consts)

    def body(*refs):
        tv = [r[...].astype(F32) for r in refs[:n_t]]
        cv = [r[...] for r in refs[n_t:n_t + n_c]]
        res = fn(*tv, *cv)
        for o_ref, val in zip(refs[n_t + n_c:], res):
            o_ref[...] = val.astype(o_ref.dtype)

    return pl.pallas_call(
        body,
        name=name,
        grid=(t // tm,),
        in_specs=[_tile_spec(tm, w, cb) for _, w, cb in tiled] + [_const_spec(c) for c in consts],
        out_specs=[_tile_spec(tm, w, 0) for w, _ in outs],
        out_shape=[jax.ShapeDtypeStruct((t, w), dt) for w, dt in outs],
        compiler_params=_cparams(1),
    )(*[a for a, _, _ in tiled], *consts)


def ew_vjp_call(fn, tiled, consts, cots, d_tiled, d_consts, *, tm, name):
    t = tiled[0][0].shape[0]
    n_t, n_c, n_g = len(tiled), len(consts), len(cots)
    dt_list = [(i, dt) for i, dts in enumerate(d_tiled) for dt in dts]
    dc_list = [i for i, want in enumerate(d_consts) if want]

    def body(*refs):
        tv = [r[...].astype(F32) for r in refs[:n_t]]
        cv = [r[...] for r in refs[n_t:n_t + n_c]]
        gv = tuple(r[...].astype(F32) for r in refs[n_t + n_c:n_t + n_c + n_g])
        out_refs = refs[n_t + n_c + n_g:]
        _, vjp = jax.vjp(fn, *tv, *cv)
        grads = vjp(gv)
        for o_ref, (i, _) in zip(out_refs, dt_list):
            o_ref[...] = grads[i].astype(o_ref.dtype)
        acc_refs = out_refs[len(dt_list):]

        @pl.when(pl.program_id(0) == 0)
        def _():
            for a_ref in acc_refs:
                a_ref[...] = jnp.zeros_like(a_ref)

        for a_ref, i in zip(acc_refs, dc_list):
            a_ref[...] += grads[n_t + i]

    res = pl.pallas_call(
        body,
        name=name,
        grid=(t // tm,),
        in_specs=[_tile_spec(tm, w, cb) for _, w, cb in tiled] + [_const_spec(c) for c in consts]
        + [_tile_spec(tm, w, cb) for _, w, cb in cots],
        out_specs=[_tile_spec(tm, tiled[i][1], 0) for i, _ in dt_list] + [_const_spec(consts[i]) for i in dc_list],
        out_shape=[jax.ShapeDtypeStruct((t, tiled[i][1]), dt) for i, dt in dt_list]
        + [jax.ShapeDtypeStruct(consts[i].shape, F32) for i in dc_list],
        compiler_params=_cparams(1),
    )(*[a for a, _, _ in tiled], *consts, *[a for a, _, _ in cots])
    return res[:len(dt_list)], res[len(dt_list):]


def mm(a, b, mode, *, tm, tn, name, out_dtypes=(F32,), epi=None, extras=()):
    m = a.shape[1] if mode == "tn" else a.shape[0]
    kd = a.shape[0] if mode == "tn" else a.shape[1]
    n = b.shape[0] if mode == "nt" else b.shape[1]
    tm, tn = min(tm, m), min(tn, n)
    if mode == "nn":
        a_spec = pl.BlockSpec((tm, kd), lambda i, j: (i, 0))
        b_spec = pl.BlockSpec((kd, tn), lambda i, j: (0, j))
    elif mode == "nt":
        a_spec = pl.BlockSpec((tm, kd), lambda i, j: (i, 0))
        b_spec = pl.BlockSpec((tn, kd), lambda i, j: (j, 0))
    else:
        a_spec = pl.BlockSpec((kd, tm), lambda i, j: (0, i))
        b_spec = pl.BlockSpec((kd, tn), lambda i, j: (0, j))
    n_e = len(extras)
    o_spec = pl.BlockSpec((tm, tn), lambda i, j: (i, j))

    def body(a_ref, b_ref, *refs):
        c = lax.dot_general(a_ref[...].astype(BF16), b_ref[...].astype(BF16), _MDIMS[mode],
                            preferred_element_type=F32)
        res = epi(c, *[r[...] for r in refs[:n_e]]) if epi is not None else (c,)
        for o_ref, val in zip(refs[n_e:], res):
            o_ref[...] = val.astype(o_ref.dtype)

    res = pl.pallas_call(
        body,
        name=name,
        grid=(m // tm, n // tn),
        in_specs=[a_spec, b_spec] + [o_spec] * n_e,
        out_specs=[o_spec] * len(out_dtypes),
        out_shape=[jax.ShapeDtypeStruct((m, n), dt) for dt in out_dtypes],
        compiler_params=_cparams(2),
    )(a, b, *extras)
    return res if len(out_dtypes) > 1 else res[0]


P_WIDTH = 7680
RWKV_COL0 = 4096
RWKV_WIDTH = 3584
SHIFT_BLK = 512


def _shift_down(p, prev_row):
    rows = lax.broadcasted_iota(jnp.int32, p.shape, 0)
    return jnp.where(rows == 0, prev_row, pltpu.roll(p, 1, 0))


def shiftmix_fwd(p_all, sbp, *, tm):
    t = p_all.shape[0]
    c0 = RWKV_COL0 // SHIFT_BLK
    hb = tm // 8

    def body(p_ref, halo_ref, sb_ref, q_ref):
        p = p_ref[...]
        prev = jnp.where(pl.program_id(0) == 0, 0.0, halo_ref[7:8, :])
        q_ref[...] = p * sb_ref[0:1, :] + _shift_down(p, prev) * sb_ref[1:2, :]

    return pl.pallas_call(
        body,
        name="shiftmix_fwd",
        grid=(t // tm, RWKV_WIDTH // SHIFT_BLK),
        in_specs=[
            pl.BlockSpec((tm, SHIFT_BLK), lambda i, j: (i, c0 + j)),
            pl.BlockSpec((8, SHIFT_BLK), lambda i, j: (jnp.maximum(i * hb - 1, 0), c0 + j)),
            pl.BlockSpec((2, SHIFT_BLK), lambda i, j: (0, j)),
        ],
        out_specs=pl.BlockSpec((tm, SHIFT_BLK), lambda i, j: (i, j)),
        out_shape=jax.ShapeDtypeStruct((t, RWKV_WIDTH), F32),
        compiler_params=_cparams(2),
    )(p_all, p_all, sbp)


def shiftmix_bwd(dq, col0, p_all, sbp, *, tm, name):
    t, w = dq.shape
    n_i = t // tm
    hb = tm // 8
    cq = col0 // SHIFT_BLK
    cp = (RWKV_COL0 + col0) // SHIFT_BLK

    def body(dq_ref, dqn_ref, p_ref, ph_ref, sb_ref, dp_ref, dsb_ref):
        i = pl.program_id(1)
        dq_t = dq_ref[...]
        rows = lax.broadcasted_iota(jnp.int32, dq_t.shape, 0)
        nxt = jnp.where(i == n_i - 1, 0.0, dqn_ref[0:1, :])
        up = jnp.where(rows == tm - 1, nxt, pltpu.roll(dq_t, tm - 1, 0))
        dp_ref[...] = (dq_t * sb_ref[0:1, :] + up * sb_ref[1:2, :]).astype(dp_ref.dtype)
        p = p_ref[...]
        prev = jnp.where(i == 0, 0.0, ph_ref[7:8, :])
        s0 = jnp.sum(dq_t * p, axis=0, keepdims=True)
        s1 = jnp.sum(dq_t * _shift_down(p, prev), axis=0, keepdims=True)
        two = lax.broadcasted_iota(jnp.int32, (2, SHIFT_BLK), 0)

        @pl.when(i == 0)
        def _():
            dsb_ref[...] = jnp.zeros_like(dsb_ref)

        dsb_ref[...] += jnp.where(two == 0, s0, s1)

    return pl.pallas_call(
        body,
        name=name,
        grid=(w // SHIFT_BLK, n_i),
        in_specs=[
            pl.BlockSpec((tm, SHIFT_BLK), lambda j, i: (i, j)),
            pl.BlockSpec((8, SHIFT_BLK), lambda j, i: (jnp.minimum((i + 1) * hb, t // 8 - 1), j)),
            pl.BlockSpec((tm, SHIFT_BLK), lambda j, i: (i, cp + j)),
            pl.BlockSpec((8, SHIFT_BLK), lambda j, i: (jnp.maximum(i * hb - 1, 0), cp + j)),
            pl.BlockSpec((2, SHIFT_BLK), lambda j, i: (0, cq + j)),
        ],
        out_specs=[
            pl.BlockSpec((tm, SHIFT_BLK), lambda j, i: (i, j)),
            pl.BlockSpec((2, SHIFT_BLK), lambda j, i: (0, j)),
        ],
        out_shape=[jax.ShapeDtypeStruct((t, w), BF16), jax.ShapeDtypeStruct((2, w), F32)],
        compiler_params=_cparams(2),
    )(dq, dq, p_all, p_all, sbp)


def final_call(h1, m3, tgt, g_final, *, tm):
    t = h1.shape[0]

    def body(h1_ref, m3_ref, tgt_ref, g_ref, dh_ref, dhb_ref, dg_ref, loss_ref):
        loss, vjp = jax.vjp(f_final, h1_ref[...], m3_ref[...], tgt_ref[...], g_ref[...])
        dh, _, _, dg = vjp(jnp.ones((), F32))
        dh_ref[...] = dh
        dhb_ref[...] = dh.astype(BF16)

        @pl.when(pl.program_id(0) == 0)
        def _():
            dg_ref[...] = jnp.zeros_like(dg_ref)
            loss_ref[...] = jnp.zeros_like(loss_ref)

        dg_ref[...] += dg
        loss_ref[...] += jnp.full(loss_ref.shape, loss, F32)

    tile = _tile_spec(tm, D_MODEL, 0)
    return pl.pallas_call(
        body,
        name="final_loss",
        grid=(t // tm,),
        in_specs=[tile, tile, tile, _const_spec(g_final)],
        out_specs=[tile, tile, _const_spec(g_final), pl.BlockSpec((8, 128), lambda i: (0, 0))],
        out_shape=[jax.ShapeDtypeStruct((t, D_MODEL), F32), jax.ShapeDtypeStruct((t, D_MODEL), BF16),
                   jax.ShapeDtypeStruct(g_final.shape, F32), jax.ShapeDtypeStruct((8, 128), F32)],
        compiler_params=_cparams(1),
    )(h1, m3, tgt, g_final)


N_SGU = 2048
N_RWKV = 3360
LORA_W, LORA_A, LORA_G = 64, 64, 160


def _pad_rwkv_cols(z):
    zero = lambda n: jnp.zeros(z.shape[:-1] + (n,), z.dtype)
    return jnp.concatenate([z[..., :3072], z[..., 3072:3136], zero(64), z[..., 3136:3200], zero(64),
                            z[..., 3200:3360], zero(96)], axis=-1)


def _unpad_rwkv_cols(z):
    return jnp.concatenate([z[..., :3072], z[..., 3072:3136], z[..., 3200:3264], z[..., 3328:3488]], axis=-1)


def _pad_win_rows(wt):
    z = wt[N_SGU:N_SGU + N_RWKV]
    zero = lambda n: jnp.zeros((n, wt.shape[1]), wt.dtype)
    return jnp.concatenate([wt[:N_SGU], wt[N_SGU + N_RWKV:], z[:3072], z[3072:3136], zero(64), z[3136:3200], zero(64),
                            z[3200:3360], zero(96)], axis=0)


def _unpad_win_rows(wt):
    z = wt[RWKV_COL0:]
    return jnp.concatenate([wt[:N_SGU], z[:3072], z[3072:3136], z[3200:3264], z[3328:3488], wt[N_SGU:RWKV_COL0]],
                           axis=0)


def _pad_rows(w, n):
    return jnp.concatenate([w, jnp.zeros((n - w.shape[0],) + w.shape[1:], w.dtype)], axis=0)


def _relu2_epi(c):
    return c, jnp.square(jnp.maximum(c, 0.0))


def _relu2_bwd_epi(c, hid):
    return (c * (2.0 * jnp.maximum(hid, 0.0)),)


def _add_epi(c, x):
    return (c + x,)


def _pre_fwd(*args):
    res = f_pre(*args)
    return res[1], res[2], res[4], res[5], res[6]


def local_step(x, tgt, w):
    d = D_MODEL
    win_pt = _pad_win_rows(w["w_in"])
    sbp = _pad_rwkv_cols(w["shift_b"])
    wl = _pad_rows(w["w_lora_w"], 128)
    al = _pad_rows(w["a_lora_w"], 128)
    gl = _pad_rows(w["g_lora_w"], 256)
    sbt = w["sgu_b"].T

    (a_bf,) = ew_call(lambda x_, g_: (f_norm_in(x_, g_)[0],), [(x, d, 0)], [w["g_mix"]], [(d, BF16)], tm=256,
                      name="norm_in")
    p_all = mm(a_bf, win_pt, "nt", tm=512, tn=1280, name="mm_in")
    sgu_t = [(p_all, 2 * d, 0)]
    sgu_c = [w["sgu_ln_w"], w["sgu_ln_b"], w["sgu_w"], sbt]
    (s_bf,) = ew_call(f_sgu, sgu_t, sgu_c, [(d, BF16)], tm=256, name="sgu_fwd")
    ya = mm(s_bf, w["w_proj_a"], "nn", tm=512, tn=1024, name="mm_proj_a")
    q = shiftmix_fwd(p_all, sbp, tm=256)
    pre_t = [(q, d, 0), (q, d, 1), (q, d, 2), (q, 512, 6)]
    pre_c = [wl, w["w0"], al, w["a0"], gl, w["k_k"], w["k_a"]]
    lw, kp, na, nb, g = ew_call(_pre_fwd, pre_t, pre_c, [(d, F32)] * 5, tm=256, name="rwkv_pre_fwd")
    scan_ops = [(q, 0), (lw, 0), (kp, 0), (q, 2), (na, 0), (nb, 0)]
    o, s0s = scan_fwd(scan_ops)
    post_t = [(o, d, 0), (q, d, 0), (kp, d, 0), (q, d, 2), (g, d, 0)]
    post_c = [w["ln_x_w"], w["ln_x_b"], w["r_k"]]
    (ob_bf,) = ew_call(f_post, post_t, post_c, [(d, BF16)], tm=256, name="rwkv_post_fwd")
    yb = mm(ob_bf, w["w_proj_b"], "nn", tm=512, tn=1024, name="mm_proj_b")
    mix_t = [(ya, d, 0), (yb, d, 0), (p_all, d, 2), (p_all, d, 3)]
    (mixed_bf,) = ew_call(f_mix, mix_t, [], [(d, BF16)], tm=256, name="mix_fwd")
    h1 = mm(mixed_bf, w["w_out"], "nn", tm=512, tn=1024, name="mm_out", epi=_add_epi, extras=(x,))
    (f_bf,) = ew_call(lambda h_, g_: (f_ffn_in(h_, g_)[0],), [(h1, d, 0)], [w["g_ffn"]], [(d, BF16)], tm=256,
                      name="ffn_norm")
    hid, act_bf = mm(f_bf, w["w_ffn1"], "nn", tm=512, tn=1024, name="mm_ffn1", out_dtypes=(F32, BF16), epi=_relu2_epi)
    m3 = mm(act_bf, w["w_ffn2"], "nn", tm=512, tn=1024, name="mm_ffn2")
    dh2, dh2_bf, dg_final, loss = final_call(h1, m3, tgt, w["g_final"], tm=256)

    dhid_bf = mm(dh2_bf, w["w_ffn2"], "nt", tm=512, tn=1024, name="mm_dact", out_dtypes=(BF16,), epi=_relu2_bwd_epi,
                 extras=(hid,))
    d_ffn2 = mm(act_bf, dh2_bf, "tn", tm=512, tn=1024, name="mm_dw_ffn2")
    df = mm(dhid_bf, w["w_ffn1"], "nt", tm=512, tn=1024, name="mm_df")
    d_ffn1 = mm(f_bf, dhid_bf, "tn", tm=512, tn=1024, name="mm_dw_ffn1")
    (dh1, dh1_bf), (dg_ffn,) = ew_vjp_call(f_ffn_in, [(h1, d, 0)], [w["g_ffn"]], [(df, d, 0), (dh2, d, 0)],
                                           [(F32, BF16)], [True], tm=256, name="ffn_norm_bwd")
    dmixed = mm(dh1_bf, w["w_out"], "nt", tm=512, tn=1024, name="mm_dmixed")
    d_out = mm(mixed_bf, dh1_bf, "tn", tm=512, tn=1024, name="mm_dw_out")
    (dya_bf, dyb_bf, dga_bf, dgb_bf), _ = ew_vjp_call(f_mix, mix_t, [], [(dmixed, d, 0)], [(BF16,)] * 4, [], tm=256,
                                                      name="mix_bwd")
    dob = mm(dyb_bf, w["w_proj_b"], "nt", tm=512, tn=1024, name="mm_dob")
    d_proj_b = mm(ob_bf, dyb_bf, "tn", tm=512, tn=1024, name="mm_dw_proj_b")
    (do, dr_p, dkp_p, dv_p, dg), (dlnx_w, dlnx_b, dr_k) = ew_vjp_call(
        f_post, post_t, post_c, [(dob, d, 0)], [(F32,)] * 5, [True] * 3, tm=256, name="rwkv_post_bwd")
    scan_g = scan_bwd(scan_ops, s0s, do)
    pre_g = [(z, d, 0) for z in scan_g] + [(dg, d, 0), (dr_p, d, 0), (dkp_p, d, 0), (dv_p, d, 0)]
    (dq_r, dq_k, dq_v, dq_l), (dwl, dw0, dal, da0, dgl, dk_k, dk_a) = ew_vjp_call(
        f_pre, pre_t, pre_c, pre_g, [(F32,)] * 4, [True] * 7, tm=128, name="rwkv_pre_bwd")
    dp_r, dsb_r = shiftmix_bwd(dq_r, 0, p_all, sbp, tm=256, name="shiftmix_bwd_r")
    dp_k, dsb_k = shiftmix_bwd(dq_k, d, p_all, sbp, tm=256, name="shiftmix_bwd_k")
    dp_v, dsb_v = shiftmix_bwd(dq_v, 2 * d, p_all, sbp, tm=256, name="shiftmix_bwd_v")
    dp_l, dsb_l = shiftmix_bwd(dq_l, 3 * d, p_all, sbp, tm=256, name="shiftmix_bwd_l")
    ds = mm(dya_bf, w["w_proj_a"], "nt", tm=512, tn=1024, name="mm_ds")
    d_proj_a = mm(s_bf, dya_bf, "tn", tm=512, tn=1024, name="mm_dw_proj_a")
    (dp_sgu,), (dln_w, dln_b, dsw, dsbt) = ew_vjp_call(f_sgu, sgu_t, sgu_c, [(ds, d, 0)], [(BF16,)], [True] * 4,
                                                       tm=256, name="sgu_bwd")
    dp_all = jnp.concatenate([dp_sgu, dga_bf, dgb_bf, dp_r, dp_k, dp_v, dp_l], axis=1)
    da = mm(dp_all, win_pt, "nn", tm=512, tn=256, name="mm_da")
    d_in_pt = mm(dp_all, a_bf, "tn", tm=1280, tn=1024, name="mm_dw_in")
    (grad_x,), (dg_mix,) = ew_vjp_call(f_norm_in, [(x, d, 0)], [w["g_mix"]], [(da, d, 0), (dh1, d, 0)], [(F32,)],
                                       [True], tm=256, name="norm_in_bwd")

    grads = {
        "g_mix": dg_mix, "w_in": _unpad_win_rows(d_in_pt), "sgu_ln_w": dln_w, "sgu_ln_b": dln_b, "sgu_w": dsw,
        "sgu_b": dsbt.T, "w_proj_a": d_proj_a,
        "shift_b": _unpad_rwkv_cols(jnp.concatenate([dsb_r, dsb_k, dsb_v, dsb_l], axis=1)),
        "w_lora_w": dwl[:LORA_W], "w0": dw0, "a_lora_w": dal[:LORA_A], "a0": da0, "g_lora_w": dgl[:LORA_G],
        "k_k": dk_k, "k_a": dk_a, "r_k": dr_k, "ln_x_w": dlnx_w, "ln_x_b": dlnx_b, "w_proj_b": d_proj_b,
        "w_out": d_out, "g_ffn": dg_ffn, "w_ffn1": d_ffn1, "w_ffn2": d_ffn2, "g_final": dg_final,
    }
    return loss[0, 0], grad_x, grads


MESH = pl.DeviceIdType.MESH
N_CHIPS = 4
N_DEV = 8
PACK_ROWS = 4992
GATHER_ROWS = 5120
PACK_TILE = 384
SMALL_ROWS = 152
_ANY = pl.BlockSpec(memory_space=pl.ANY)


def _coords():
    return lax.axis_index("x"), lax.axis_index("y"), lax.axis_index("c")


def _other_chips(x, y):
    return [(1 - x, y), (x, 1 - y), (1 - x, 1 - y)]


def _remote(src, dst, send_sems, recv_sems, k, to):
    return pltpu.make_async_remote_copy(src_ref=src, dst_ref=dst, send_sem=send_sems.at[k], recv_sem=recv_sems.at[k],
                                        device_id=to, device_id_type=MESH)


def gather_shards(pack):
    def body(src_ref, out_ref, send_sems, recv_sems):
        x, y, c = _coords()
        me = 2 * x + y
        sib = (x, y, 1 - c)
        chips = _other_chips(x, y)
        first = [_remote(src_ref.at[c], out_ref.at[me, c], send_sems, recv_sems, k, (cx, cy, c))
                 for k, (cx, cy) in enumerate(chips)]
        for cp in first:
            cp.start()
        passed = []
        for k, (cx, cy) in enumerate(chips):
            j = 2 * cx + cy
            _remote(src_ref.at[c], out_ref.at[j, c], send_sems, recv_sems, k, (cx, cy, c)).wait_recv()
            fwd = _remote(out_ref.at[j, c], out_ref.at[j, c], send_sems, recv_sems, 3 + k, sib)
            fwd.start()
            passed.append(fwd)
        for k, (cx, cy) in enumerate(chips):
            j = 2 * cx + cy
            _remote(out_ref.at[j, 1 - c], out_ref.at[j, 1 - c], send_sems, recv_sems, 3 + k, sib).wait_recv()
        for cp in first + passed:
            cp.wait_send()

    return pl.pallas_call(
        body,
        name="gather_shards",
        in_specs=[_ANY],
        out_specs=_ANY,
        out_shape=jax.ShapeDtypeStruct((N_CHIPS,) + pack.shape, pack.dtype),
        scratch_shapes=[pltpu.SemaphoreType.DMA((6,)), pltpu.SemaphoreType.DMA((6,))],
    )(pack)


def reduce_pair(g):
    def body(g_ref, got_ref, send_sems, recv_sems):
        x, y, c = _coords()
        sib = (x, y, 1 - c)
        sends = [_remote(g_ref.at[j, 1 - c], got_ref.at[j], send_sems, recv_sems, j, sib) for j in range(N_CHIPS)]
        for cp in sends:
            cp.start()
        for cp in sends:
            cp.wait_recv()
        for cp in sends:
            cp.wait_send()

    return pl.pallas_call(
        body,
        name="reduce_pair",
        in_specs=[_ANY],
        out_specs=_ANY,
        out_shape=jax.ShapeDtypeStruct((N_CHIPS,) + g.shape[2:], g.dtype),
        scratch_shapes=[pltpu.SemaphoreType.DMA((N_CHIPS,)), pltpu.SemaphoreType.DMA((N_CHIPS,))],
    )(g)


def pair_sum(g, got, *, tm):
    n, _, rows, width = g.shape

    def body(g0_ref, g1_ref, got_ref, out_ref, out16_ref):
        own = jnp.where(lax.axis_index("c") == 0, g0_ref[0, 0], g1_ref[0, 0])
        total = own + got_ref[0]
        out_ref[0] = total
        out16_ref[0] = total.astype(BF16)

    blk = pl.BlockSpec((1, tm, width), lambda j, i: (j, i, 0))
    return pl.pallas_call(
        body,
        name="pair_sum",
        grid=(n, rows // tm),
        in_specs=[pl.BlockSpec((1, 1, tm, width), lambda j, i: (j, 0, i, 0)),
                  pl.BlockSpec((1, 1, tm, width), lambda j, i: (j, 1, i, 0)), blk],
        out_specs=[blk, blk],
        out_shape=[jax.ShapeDtypeStruct(got.shape, F32), jax.ShapeDtypeStruct(got.shape, BF16)],
        compiler_params=_cparams(2),
    )(g, g, got)


def reduce_chips(p):
    def body(p_ref, out_ref, send_sems, recv_sems):
        x, y, c = _coords()
        me = 2 * x + y
        chips = _other_chips(x, y)
        sends = [_remote(p_ref.at[2 * cx + cy], out_ref.at[me], send_sems, recv_sems, k, (cx, cy, c))
                 for k, (cx, cy) in enumerate(chips)]
        for cp in sends:
            cp.start()
        for k, (cx, cy) in enumerate(chips):
            _remote(p_ref.at[me], out_ref.at[2 * cx + cy], send_sems, recv_sems, k, (cx, cy, c)).wait_recv()
        for cp in sends:
            cp.wait_send()

    return pl.pallas_call(
        body,
        name="reduce_chips",
        in_specs=[_ANY],
        out_specs=_ANY,
        out_shape=jax.ShapeDtypeStruct(p.shape, p.dtype),
        scratch_shapes=[pltpu.SemaphoreType.DMA((3,)), pltpu.SemaphoreType.DMA((3,))],
    )(p)


def sum_with_own(own, slots, index_fn, *, tm, name):
    n, rows, width = slots.shape
    own3 = own.ndim == 3

    def body(*refs):
        mine = index_fn()
        acc = None
        for s in range(n):
            o = refs[s][0] if own3 else refs[0][...]
            term = jnp.where(mine == s, o, refs[(n if own3 else 1) + s][0].astype(F32))
            acc = term if acc is None else acc + term
        refs[-1][...] = acc

    slot_specs = [pl.BlockSpec((1, tm, width), lambda i, s=s: (s, i, 0)) for s in range(n)]
    own_specs = slot_specs if own3 else [pl.BlockSpec((tm, width), lambda i: (i, 0))]
    return pl.pallas_call(
        body,
        name=name,
        grid=(rows // tm,),
        in_specs=own_specs + slot_specs,
        out_specs=pl.BlockSpec((tm, width), lambda i: (i, 0)),
        out_shape=jax.ShapeDtypeStruct((rows, width), F32),
        compiler_params=_cparams(1),
    )(*([own] * (n if own3 else 1)), *([slots] * n))


def exchange_halves(s):
    rq = PACK_TILE
    nq = s.shape[0] // rq

    def body(s_ref, out_ref, sbuf, rbuf, send_sems, recv_sems, in_sems, out_sems):
        x, y, c = _coords()
        sib = (x, y, 1 - c)
        rows = lambda q: pl.ds(q * rq, rq)
        loads = [pltpu.make_async_copy(s_ref.at[rows(q)], sbuf.at[rows(q)], in_sems.at[q]) for q in range(nq)]
        for cp in loads:
            cp.start()
        sends = []
        for q in range(nq):
            loads[q].wait()
            sends.append(_remote(sbuf.at[rows(q)], rbuf.at[rows(q)], send_sems, recv_sems, q, sib))
            sends[q].start()
        stores = []
        for q in range(nq):
            sends[q].wait_recv()
            stores.append(pltpu.make_async_copy(rbuf.at[rows(q)], out_ref.at[rows(q)], out_sems.at[q]))
            stores[q].start()
        for cp in sends:
            cp.wait_send()
        for cp in stores:
            cp.wait()

    return pl.pallas_call(
        body,
        name="exchange_halves",
        in_specs=[_ANY],
        out_specs=_ANY,
        out_shape=jax.ShapeDtypeStruct(s.shape, s.dtype),
        scratch_shapes=[pltpu.VMEM(s.shape, s.dtype), pltpu.VMEM(s.shape, s.dtype)]
        + [pltpu.SemaphoreType.DMA((nq,))] * 4,
        compiler_params=pltpu.CompilerParams(vmem_limit_bytes=VMEM_LIMIT),
    )(s)


def gather_all(s):
    def body(s_ref, out_ref, send_sems, recv_sems):
        x, y, c = _coords()
        me = 4 * x + 2 * y + c
        peers = []
        for mask in range(1, N_DEV):
            px = 1 - x if mask & 4 else x
            py = 1 - y if mask & 2 else y
            pc = 1 - c if mask & 1 else c
            peers.append((px, py, pc))
        sends = [_remote(s_ref, out_ref.at[me], send_sems, recv_sems, k, peer) for k, peer in enumerate(peers)]
        for cp in sends:
            cp.start()
        for k, (px, py, pc) in enumerate(peers):
            _remote(s_ref, out_ref.at[4 * px + 2 * py + pc], send_sems, recv_sems, k, (px, py, pc)).wait_recv()
        for cp in sends:
            cp.wait_send()

    return pl.pallas_call(
        body,
        name="gather_all",
        in_specs=[_ANY],
        out_specs=_ANY,
        out_shape=jax.ShapeDtypeStruct((N_DEV,) + s.shape, s.dtype),
        scratch_shapes=[pltpu.SemaphoreType.DMA((N_DEV - 1,)), pltpu.SemaphoreType.DMA((N_DEV - 1,))],
    )(s)


ADAM_LR = 0.001
ADAM_B1 = 0.9
ADAM_B2 = 0.999
ADAM_EPS = 1e-08
ADAM_WD = 0.01
ADAM_STEP = 10


def f_adamw(g, w, m, v):
    m = ADAM_B1 * m + (1.0 - ADAM_B1) * g
    v = ADAM_B2 * v + (1.0 - ADAM_B2) * jnp.square(g)
    m_hat = m / (1.0 - ADAM_B1 ** ADAM_STEP)
    v_hat = v / (1.0 - ADAM_B2 ** ADAM_STEP)
    delta = -ADAM_LR * (m_hat / (jnp.sqrt(v_hat) + ADAM_EPS) + ADAM_WD * w)
    return delta, m, v


def adamw_call(g, w, m, v, *, tm, name):
    width = g.shape[1]
    return ew_call(f_adamw, [(g, width, 0), (w, width, 0), (m, width, 0), (v, width, 0)], [], [(width, F32)] * 3,
                   tm=tm, name=name)


def adamw_halves(g_own, g_other, w, m, v, *, tm):
    _, rows, width = w.shape

    def body(go_ref, gx_ref, w_ref, m_ref, v_ref, g_ref, d_ref, nm_ref, nv_ref):
        g = jnp.where(pl.program_id(0) == lax.axis_index("c"), go_ref[...], gx_ref[...])
        delta, nm, nv = f_adamw(g, w_ref[0], m_ref[0], v_ref[0])
        g_ref[0] = g
        d_ref[0] = delta
        nm_ref[0] = nm
        nv_ref[0] = nv

    half = pl.BlockSpec((tm, width), lambda h, i: (i, 0))
    full = pl.BlockSpec((1, tm, width), lambda h, i: (h, i, 0))
    return pl.pallas_call(
        body,
        name="adamw_sharded",
        grid=(2, rows // tm),
        in_specs=[half, half, full, full, full],
        out_specs=[full] * 4,
        out_shape=[jax.ShapeDtypeStruct(w.shape, F32)] * 4,
        compiler_params=_cparams(2),
    )(g_own, g_other, w, m, v)


SHARDED = ["w_in", "w_ffn1", "w_ffn2", "w_proj_a", "w_proj_b", "w_out", "w_lora_w", "a_lora_w", "g_lora_w"]
LORAS = ["w_lora_w", "a_lora_w", "g_lora_w"]
HALF_W = 512
PIECE_ROWS = {"w_in": 1864, "w_ffn1": 1024, "w_ffn2": 1024, "w_proj_a": 256, "w_proj_b": 256, "w_out": 256,
              "w_lora_w": 32, "a_lora_w": 32, "g_lora_w": 80}
PIECE_OFF = {"w_in": 0, "w_ffn1": 1920, "w_ffn2": 2944, "w_proj_a": 3968, "w_proj_b": 4224, "w_out": 4480,
             "w_lora_w": 4736, "a_lora_w": 4768, "g_lora_w": 4800}
LO_OFF = 4880
SHARD_AXIS = {"w_in": 1, "w_proj_a": 0, "w_lora_w": 1, "a_lora_w": 1, "g_lora_w": 1, "w_proj_b": 0, "w_out": 0,
              "w_ffn1": 1, "w_ffn2": 0}
SHARD_SHAPE = {"w_in": (1024, 1864), "w_proj_a": (256, 1024), "w_lora_w": (64, 256), "a_lora_w": (64, 256),
               "g_lora_w": (160, 256), "w_proj_b": (256, 1024), "w_out": (256, 1024), "w_ffn1": (1024, 1024),
               "w_ffn2": (1024, 1024)}
SHIFT_SHARD = (2, 840)
VECTORS = ["g_mix", "sgu_ln_w", "sgu_ln_b", "w0", "a0", "k_k", "k_a", "r_k", "ln_x_w", "ln_x_b", "g_ffn", "g_final"]
SMALL = VECTORS + ["sgu_w", "sgu_b"]
SMALL_SHAPE = {**{n: (1, 1024) for n in VECTORS}, "sgu_w": (8, 128, 128), "sgu_b": (8, 128)}
WEIGHTS = ["g_mix", "w_in", "sgu_ln_w", "sgu_ln_b", "sgu_w", "sgu_b", "w_proj_a", "shift_b", "w_lora_w", "w0",
           "a_lora_w", "a0", "g_lora_w", "k_k", "k_a", "r_k", "ln_x_w", "ln_x_b", "w_proj_b", "w_out", "g_ffn",
           "w_ffn1", "w_ffn2", "g_final"]


def _size(shape):
    n = 1
    for s in shape:
        n *= s
    return n


def _pack_rows(parts, rows, dtype):
    flat = jnp.concatenate([p.reshape(-1).astype(dtype) for p in parts])
    return jnp.concatenate([flat, jnp.zeros((rows * 1024 - flat.shape[0],), dtype)]).reshape(rows, 1024)


def _unpack_rows(packed, shapes):
    flat = packed.reshape(-1)
    out, off = [], 0
    for shp in shapes:
        out.append(flat[off:off + _size(shp)].reshape(shp))
        off += _size(shp)
    return out


def _shard_of(name, full, j):
    ax = SHARD_AXIS[name]
    n = SHARD_SHAPE[name][ax]
    return lax.slice_in_dim(full, j * n, (j + 1) * n, axis=ax)


def _pad_cols(z, n):
    return jnp.concatenate([z, jnp.zeros((z.shape[0], n - z.shape[1]), z.dtype)], axis=1)


def _row_form(name, s):
    return s.T if name == "w_in" else s


def _half_piece(name, rf, h):
    if name in LORAS:
        r = PIECE_ROWS[name]
        return _pad_cols(rf[h * r:(h + 1) * r], HALF_W)
    return rf[:, HALF_W * h:HALF_W * (h + 1)]


def _pack_half(rf_fn, h, dtype, rows, tail=()):
    parts, pos = [], 0
    for n in SHARDED:
        if PIECE_OFF[n] > pos:
            parts.append(jnp.zeros((PIECE_OFF[n] - pos, HALF_W), dtype))
        parts.append(_half_piece(n, rf_fn(n), h).astype(dtype))
        pos = PIECE_OFF[n] + PIECE_ROWS[n]
    for t in tail:
        parts.append(t)
        pos += t.shape[0]
    parts.append(jnp.zeros((rows - pos, HALF_W), dtype))
    return jnp.concatenate(parts, axis=0)


def _piece(pack, name):
    return pack[PIECE_OFF[name]:PIECE_OFF[name] + PIECE_ROWS[name]]


def _join_halves(name, p0, p1):
    if name in LORAS:
        return jnp.concatenate([p0[:, :SHARD_SHAPE[name][1]], p1[:, :SHARD_SHAPE[name][1]]], axis=0)
    return jnp.concatenate([p0, p1], axis=1)


def _grad_row_form(name, full, j):
    if name == "w_in":
        return full[SHARD_SHAPE[name][1] * j:SHARD_SHAPE[name][1] * (j + 1)]
    return _shard_of(name, full, j)


def adamw_weight(name, g_own, g_other, w, m, v):
    rows, width = w.shape
    if name in LORAS:
        tm = PIECE_ROWS[name]
        grid = (2, 1)
        native = pl.BlockSpec((tm, width), lambda h, i: (h, 0))
    elif name == "w_in":
        tm, lanes = rows, 128
        grid = (2, HALF_W // lanes)
        native = pl.BlockSpec((tm, lanes), lambda h, i: (0, h * (HALF_W // lanes) + i))
    else:
        tm = 128
        grid = (2, rows // tm)
        native = pl.BlockSpec((tm, HALF_W), lambda h, i: (i, h))
    off = PIECE_OFF[name] // tm
    if name == "w_in":
        packed = pl.BlockSpec((tm, 128), lambda h, i: (0, i))
    else:
        packed = pl.BlockSpec((tm, HALF_W), lambda h, i: (off + i, 0))

    def body(go_ref, gx_ref, w_ref, m_ref, v_ref, g_ref, d_ref, nm_ref, nv_ref):
        g = jnp.where(pl.program_id(0) == lax.axis_index("c"), go_ref[...], gx_ref[...])[:, :w_ref.shape[1]]
        delta, nm, nv = f_adamw(g, w_ref[...], m_ref[...], v_ref[...])
        g_ref[...] = g
        d_ref[...] = delta
        nm_ref[...] = nm
        nv_ref[...] = nv

    return pl.pallas_call(
        body,
        name="adamw_" + name,
        grid=grid,
        in_specs=[packed, packed, native, native, native],
        out_specs=[native] * 4,
        out_shape=[jax.ShapeDtypeStruct(w.shape, F32)] * 4,
        compiler_params=_cparams(2),
    )(g_own, g_other, w, m, v)


def kernel(x, g_mix, w_in, sgu_ln_w, sgu_ln_b, sgu_w, sgu_b, w_proj_a, shift_b, w_lora_w, w0, a_lora_w, a0, g_lora_w, k_k, k_a, r_k, ln_x_w, ln_x_b, w_proj_b, w_out, g_ffn, w_ffn1, w_ffn2, g_final, loss_target, m_g_mix, m_w_in, m_sgu_ln_w, m_sgu_ln_b, m_sgu_w, m_sgu_b, m_w_proj_a, m_shift_b, m_w_lora_w, m_w0, m_a_lora_w, m_a0, m_g_lora_w, m_k_k, m_k_a, m_r_k, m_ln_x_w, m_ln_x_b, m_w_proj_b, m_w_out, m_g_ffn, m_w_ffn1, m_w_ffn2, m_g_final, v_g_mix, v_w_in, v_sgu_ln_w, v_sgu_ln_b, v_sgu_w, v_sgu_b, v_w_proj_a, v_shift_b, v_w_lora_w, v_w0, v_a_lora_w, v_a0, v_g_lora_w, v_k_k, v_k_a, v_r_k, v_ln_x_w, v_ln_x_b, v_w_proj_b, v_w_out, v_g_ffn, v_w_ffn1, v_w_ffn2, v_g_final):
    given = dict(zip(WEIGHTS, (g_mix, w_in, sgu_ln_w, sgu_ln_b, sgu_w, sgu_b, w_proj_a, shift_b, w_lora_w, w0, a_lora_w, a0, g_lora_w, k_k, k_a, r_k, ln_x_w, ln_x_b, w_proj_b, w_out, g_ffn, w_ffn1, w_ffn2, g_final)))
    mom_m = dict(zip(WEIGHTS, (m_g_mix, m_w_in, m_sgu_ln_w, m_sgu_ln_b, m_sgu_w, m_sgu_b, m_w_proj_a, m_shift_b, m_w_lora_w, m_w0, m_a_lora_w, m_a0, m_g_lora_w, m_k_k, m_k_a, m_r_k, m_ln_x_w, m_ln_x_b, m_w_proj_b, m_w_out, m_g_ffn, m_w_ffn1, m_w_ffn2, m_g_final)))
    mom_v = dict(zip(WEIGHTS, (v_g_mix, v_w_in, v_sgu_ln_w, v_sgu_ln_b, v_sgu_w, v_sgu_b, v_w_proj_a, v_shift_b, v_w_lora_w, v_w0, v_a_lora_w, v_a0, v_g_lora_w, v_k_k, v_k_a, v_r_k, v_ln_x_w, v_ln_x_b, v_w_proj_b, v_w_out, v_g_ffn, v_w_ffn1, v_w_ffn2, v_g_final)))
    chip = 2 * lax.axis_index("x") + lax.axis_index("y")

    def local_block(tree, n):
        return tree[n] if n == "g_final" else tree[n][0]

    sb = local_block(given, "shift_b")
    lo_part = lambda z: (z - z.astype(BF16).astype(F32)).astype(BF16)
    row_form = lambda tree: (lambda n: _row_form(n, local_block(tree, n)))
    tile16 = lambda z: jnp.pad(z, ((0, 16 - z.shape[0]), (0, HALF_W - z.shape[1])))
    sb_tiles = [tile16(f(sb[:, lanes])) for f in (lambda z: z.astype(BF16), lo_part)
                for lanes in (slice(0, HALF_W), slice(HALF_W, None))]
    tails = [[_half_piece(n, lo_part(local_block(given, n)), h) for n in LORAS] + sb_tiles for h in range(2)]
    pack_w = jnp.stack([_pack_half(row_form(given), h, BF16, GATHER_ROWS, tails[h]) for h in range(2)])
    gathered = gather_shards(pack_w)
    gathered = lax.dynamic_update_index_in_dim(gathered, pack_w, chip, 0)
    shard = lambda n, j: _join_halves(n, _piece(gathered[j, 0], n), _piece(gathered[j, 1], n))
    w = {}
    for n in SHARDED:
        axis = 0 if n == "w_in" else SHARD_AXIS[n]
        w[n] = jnp.concatenate([shard(n, j).astype(F32 if n == "w_in" else BF16) for j in range(N_CHIPS)], axis=axis)
    off = LO_OFF
    for n in LORAS:
        r, cols = PIECE_ROWS[n], SHARD_SHAPE[n][1]
        lo = jnp.concatenate([jnp.concatenate([gathered[j, 0, off:off + r, :cols], gathered[j, 1, off:off + r, :cols]],
                                              axis=0) for j in range(N_CHIPS)], axis=1)
        w[n] = w[n].astype(F32) + lo.astype(F32)
        off += r
    sb_tile = lambda j, t, lanes: gathered[j, 0, off + 16 * t:off + 16 * t + 2, :lanes].astype(F32)
    rest = SHIFT_SHARD[1] - HALF_W
    w["shift_b"] = jnp.concatenate(
        [jnp.concatenate([sb_tile(j, 0, HALF_W) + sb_tile(j, 2, HALF_W), sb_tile(j, 1, rest) + sb_tile(j, 3, rest)],
                         axis=1) for j in range(N_CHIPS)], axis=1)
    for n in SMALL:
        w[n] = local_block(given, n).reshape(SMALL_SHAPE[n])

    loss, grad_x, grads = local_step(x[0], loss_target[0], w)
    loss = lax.psum(loss, ("x", "y", "c"))

    g_pack = jnp.stack([jnp.stack([_pack_half(lambda n: _grad_row_form(n, grads[n], j), h, F32, PACK_ROWS)
                                   for h in range(2)]) for j in range(N_CHIPS)])
    chip_part, chip_part16 = pair_sum(g_pack, reduce_pair(g_pack), tm=PACK_TILE)
    half_sum = sum_with_own(chip_part, reduce_chips(chip_part16),
                            lambda: 2 * lax.axis_index("x") + lax.axis_index("y"), tm=PACK_TILE, name="chip_sum")
    other_half = exchange_halves(half_sum)
    out_g, out_d, out_m, out_v = {}, {}, {}, {}
    for n in SHARDED:
        res = adamw_weight(n, half_sum, other_half, *[_row_form(n, local_block(t, n)) for t in (given, mom_m, mom_v)])
        for tree, z in zip((out_g, out_d, out_m, out_v), res):
            tree[n] = _row_form(n, z)

    small_shapes = [SMALL_SHAPE[n] for n in SMALL]
    s_pack = _pack_rows([grads[n] for n in SMALL] + [grads["shift_b"]], SMALL_ROWS, F32)
    g_small = sum_with_own(
        s_pack, gather_all(s_pack), lambda: 4 * lax.axis_index("x") + 2 * lax.axis_index("y") + lax.axis_index("c"),
        tm=SMALL_ROWS, name="small_sum")
    w_small = _pack_rows([local_block(given, n) for n in SMALL], SMALL_ROWS, F32)
    m_small = _pack_rows([local_block(mom_m, n) for n in SMALL], SMALL_ROWS, F32)
    v_small = _pack_rows([local_block(mom_v, n) for n in SMALL], SMALL_ROWS, F32)
    d_small, nm_small, nv_small = adamw_call(g_small, w_small, m_small, v_small, tm=SMALL_ROWS, name="adamw_small")
    g_parts = _unpack_rows(g_small, small_shapes + [(2, N_RWKV)])
    out_g.update(zip(SMALL, g_parts[:-1]))
    out_d.update(zip(SMALL, _unpack_rows(d_small, small_shapes)))
    out_m.update(zip(SMALL, _unpack_rows(nm_small, small_shapes)))
    out_v.update(zip(SMALL, _unpack_rows(nv_small, small_shapes)))
    g_sb = lax.dynamic_slice_in_dim(g_parts[-1], chip * SHIFT_SHARD[1], SHIFT_SHARD[1], axis=1)
    sb_args = [_pack_rows([z], 8, F32) for z in (g_sb, sb, local_block(mom_m, "shift_b"), local_block(mom_v, "shift_b"))]
    sb_res = adamw_call(*sb_args, tm=8, name="adamw_shift_b")
    out_g["shift_b"] = g_sb
    for tree, res in zip((out_d, out_m, out_v), sb_res):
        tree["shift_b"] = _unpack_rows(res, [SHIFT_SHARD])[0]

    def block_of(tree, n):
        return tree[n].reshape(given[n].shape)

    return (loss, grad_x[None], *[block_of(out_g, n) for n in WEIGHTS], *[block_of(out_d, n) for n in WEIGHTS],
            *[block_of(out_m, n) for n in WEIGHTS], *[block_of(out_v, n) for n in WEIGHTS])
```

```python
import functools

import jax
import jax.numpy as jnp
from jax import lax
from jax.experimental import pallas as pl
from jax.experimental.pallas import tpu as pltpu

F32 = jnp.float32
BF16 = jnp.bfloat16

D_MODEL = 1024
N_HEADS = 16
HEAD = 64
SCAN_CHUNK = 64

VMEM_LIMIT = 56 * 1024 * 1024


_BDIMS = {
    "nn": (((2,), (1,)), ((0,), (0,))),
    "nt": (((2,), (2,)), ((0,), (0,))),
    "tn": (((1,), (1,)), ((0,), (0,))),
}


def _raw_bdot(x, y, mode, fine):
    if fine:
        return lax.dot_general(x, y, _BDIMS[mode], precision=lax.Precision.HIGH, preferred_element_type=F32)
    return lax.dot_general(x.astype(BF16), y.astype(BF16), _BDIMS[mode], preferred_element_type=F32)


@functools.partial(jax.custom_vjp, nondiff_argnums=(2, 3))
def bdot(x, y, mode, fine=True):
    return _raw_bdot(x, y, mode, fine)


def _bdot_fwd(x, y, mode, fine):
    return _raw_bdot(x, y, mode, fine), (x, y)


def _bdot_bwd(mode, fine, res, g):
    x, y = res
    if mode == "nn":
        return bdot(g, y, "nt", fine), bdot(x, g, "tn", fine)
    if mode == "nt":
        return bdot(g, y, "nn", fine), bdot(g, x, "tn", fine)
    return bdot(y, g, "nt", fine), bdot(x, g, "nn", fine)


bdot.defvjp(_bdot_fwd, _bdot_bwd)


def _scan_chunk(S0, r, lw, k, v, a, b):
    nh, lc, _ = r.shape
    ti = lax.broadcasted_iota(jnp.int32, (lc, lc), 0)
    si = lax.broadcasted_iota(jnp.int32, (lc, lc), 1)
    incl = (si <= ti).astype(F32)
    strict = (si < ti).astype(F32)
    eye = (si == ti).astype(F32)
    cl = bdot(jnp.broadcast_to(incl, (nh, lc, lc)), lw, "nn")
    cl_last = cl[:, lc - 1:lc, :]
    g_last = jnp.exp(cl_last - cl)
    at = a * jnp.exp(cl - lw)
    bt = b * jnp.exp(-cl)
    kt = k * jnp.exp(-cl)
    rt = r * jnp.exp(cl)
    ar = jnp.concatenate([at, rt], axis=1)
    ar_b = bdot(ar, bt, "nt")
    ar_k = bdot(ar, kt, "nt")
    m_ab, m_rb = ar_b[:, :lc] * strict, ar_b[:, lc:] * incl
    m_ak, m_rk = ar_k[:, :lc] * strict, ar_k[:, lc:] * incl
    x = eye + m_ab
    p = bdot(m_ab, m_ab, "nn", False)
    n = 2
    while n * 2 < lc:
        px = bdot(jnp.concatenate([p, x], axis=1), p, "nn", False)
        p = px[:, :lc]
        x = x + px[:, lc:]
        n *= 2
    x = x + bdot(x, p, "nn", False)
    ar_s = bdot(ar, S0, "nt", False)
    akrk_v = bdot(jnp.concatenate([m_ak, m_rk], axis=1), v, "nn")
    u = bdot(x, ar_s[:, :lc] + akrk_v[:, :lc], "nn", False)
    o = ar_s[:, lc:] + bdot(m_rb, u, "nn", False) + akrk_v[:, lc:]
    s_last = S0 * jnp.exp(cl_last) + bdot(jnp.concatenate([u, v], axis=1),
                                          jnp.concatenate([b * g_last, k * g_last], axis=1), "tn", False)
    return o, s_last


def _split_heads(z):
    return jnp.stack([z[:, HEAD * h:HEAD * (h + 1)] for h in range(N_HEADS)], axis=0)


def _merge_heads(z):
    return jnp.concatenate([z[h] for h in range(N_HEADS)], axis=1)


def _scan_specs(t, ops, rev):
    nc = t // SCAN_CHUNK
    row = (lambda c: nc - 1 - c) if rev else (lambda c: c)
    specs = [pl.BlockSpec((SCAN_CHUNK, D_MODEL), lambda c, cb=cb: (row(c), cb)) for _, cb in ops]
    state = pl.BlockSpec((1, N_HEADS, HEAD, HEAD), lambda c: (row(c), 0, 0, 0))
    return nc, specs, state


def scan_fwd(ops, pack):
    t = ops[0][0].shape[0]
    nc, specs, state = _scan_specs(t, ops, False)

    def body(r_ref, lw_ref, k_ref, v_ref, a_ref, b_ref, pack_ref, o_ref, s0_ref, all_ref, s_scr, send_sems, recv_sems):
        step = pl.program_id(0)
        x, y, c = _coords()
        me = 2 * x + y
        sib = (x, y, 1 - c)
        chips = _other_chips(x, y)
        first = [_remote(pack_ref.at[c], all_ref.at[me, c], send_sems, recv_sems, k, (cx, cy, c))
                 for k, (cx, cy) in enumerate(chips)]
        passed = [_remote(all_ref.at[2 * cx + cy, c], all_ref.at[2 * cx + cy, c], send_sems, recv_sems, 3 + k, sib)
                  for k, (cx, cy) in enumerate(chips)]

        @pl.when(step == 0)
        def _():
            s_scr[...] = jnp.zeros_like(s_scr)
            for cp in first:
                cp.start()

        s0 = s_scr[...]
        s0_ref[0] = s0
        o, s_last = _scan_chunk(s0, *[_split_heads(z[...]) for z in (r_ref, lw_ref, k_ref, v_ref, a_ref, b_ref)])
        o_ref[...] = _merge_heads(o)
        s_scr[...] = s_last

        @pl.when(step == nc - 1)
        def _():
            for k, (cx, cy) in enumerate(chips):
                j = 2 * cx + cy
                _remote(pack_ref.at[c], all_ref.at[j, c], send_sems, recv_sems, k, (cx, cy, c)).wait_recv()
                passed[k].start()
            for k, (cx, cy) in enumerate(chips):
                j = 2 * cx + cy
                _remote(all_ref.at[j, 1 - c], all_ref.at[j, 1 - c], send_sems, recv_sems, 3 + k, sib).wait_recv()
            for cp in first + passed:
                cp.wait_send()

    return pl.pallas_call(
        body,
        name="scan_fwd",
        grid=(nc,),
        in_specs=specs + [_ANY],
        out_specs=[pl.BlockSpec((SCAN_CHUNK, D_MODEL), lambda c: (c, 0)), state, _ANY],
        out_shape=[jax.ShapeDtypeStruct((t, D_MODEL), F32), jax.ShapeDtypeStruct((nc, N_HEADS, HEAD, HEAD), F32),
                   jax.ShapeDtypeStruct((N_CHIPS,) + pack.shape, pack.dtype)],
        scratch_shapes=[pltpu.VMEM((N_HEADS, HEAD, HEAD), F32), pltpu.SemaphoreType.DMA((6,)),
                        pltpu.SemaphoreType.DMA((6,))],
        compiler_params=_cparams(1),
    )(*[a for a, _ in ops], pack)


def scan_bwd(ops, s0s, do, part):
    t = ops[0][0].shape[0]
    nc, specs, state = _scan_specs(t, ops + [(do, 0)], True)

    def body(r_ref, lw_ref, k_ref, v_ref, a_ref, b_ref, do_ref, s0_ref, part_ref, *rest):
        out_refs, slots_ref, ds_scr, send_sems, recv_sems = rest[:6], rest[6], rest[7], rest[8], rest[9]
        step = pl.program_id(0)
        x, y, c = _coords()
        me = 2 * x + y
        chips = _other_chips(x, y)
        sends = [_remote(part_ref.at[2 * cx + cy], slots_ref.at[me], send_sems, recv_sems, k, (cx, cy, c))
                 for k, (cx, cy) in enumerate(chips)]

        @pl.when(step == 0)
        def _():
            ds_scr[...] = jnp.zeros_like(ds_scr)
            for cp in sends:
                cp.start()

        _, vjp = jax.vjp(_scan_chunk, s0_ref[0],
                         *[_split_heads(z[...]) for z in (r_ref, lw_ref, k_ref, v_ref, a_ref, b_ref)])
        grads = vjp((_split_heads(do_ref[...]), ds_scr[...]))
        for o_ref, g in zip(out_refs, grads[1:]):
            o_ref[...] = _merge_heads(g)
        ds_scr[...] = grads[0]

        @pl.when(step == nc - 1)
        def _():
            for k, (cx, cy) in enumerate(chips):
                _remote(part_ref.at[me], slots_ref.at[2 * cx + cy], send_sems, recv_sems, k, (cx, cy, c)).wait_recv()
            for cp in sends:
                cp.wait_send()

    return pl.pallas_call(
        body,
        name="scan_bwd",
        grid=(nc,),
        in_specs=specs + [state, _ANY],
        out_specs=[pl.BlockSpec((SCAN_CHUNK, D_MODEL), lambda c: (nc - 1 - c, 0))] * 6 + [_ANY],
        out_shape=[jax.ShapeDtypeStruct((t, D_MODEL), F32)] * 6 + [jax.ShapeDtypeStruct(part.shape, part.dtype)],
        scratch_shapes=[pltpu.VMEM((N_HEADS, HEAD, HEAD), F32), pltpu.SemaphoreType.DMA((3,)),
                        pltpu.SemaphoreType.DMA((3,))],
        compiler_params=_cparams(1),
    )(*[a for a, _ in ops], do, s0s, part)


_MDIMS = {
    "nn": (((1,), (0,)), ((), ())),
    "nt": (((1,), (1,)), ((), ())),
    "tn": (((0,), (0,)), ((), ())),
}


def _raw_mdot(x, y, mode, exact):
    if exact:
        return lax.dot_general(x, y, _MDIMS[mode], precision=lax.Precision.HIGH, preferred_element_type=F32)
    return lax.dot_general(x.astype(BF16), y.astype(BF16), _MDIMS[mode], preferred_element_type=F32)


@functools.partial(jax.custom_vjp, nondiff_argnums=(2, 3))
def mdot(x, y, mode, exact):
    return _raw_mdot(x, y, mode, exact)


def _mdot_fwd(x, y, mode, exact):
    return _raw_mdot(x, y, mode, exact), (x, y)


def _mdot_bwd(mode, exact, res, g):
    x, y = res
    if mode == "nn":
        return mdot(g, y, "nt", exact), mdot(x, g, "tn", exact)
    if mode == "nt":
        return mdot(g, y, "nn", exact), mdot(g, x, "tn", exact)
    return mdot(y, g, "nt", exact), mdot(x, g, "nn", exact)


mdot.defvjp(_mdot_fwd, _mdot_bwd)


def _seg_ones():
    i = lax.broadcasted_iota(jnp.int32, (256, 256), 0) // HEAD
    j = lax.broadcasted_iota(jnp.int32, (256, 256), 1) // HEAD
    return (i == j).astype(BF16)


@jax.custom_vjp
def segsum(x):
    bd = _seg_ones()
    hi = x.astype(BF16)
    lo = (x - hi.astype(F32)).astype(BF16)
    cols = []
    for j in range(x.shape[1] // 256):
        sl = slice(256 * j, 256 * (j + 1))
        cols.append(jnp.dot(hi[:, sl], bd, preferred_element_type=F32)
                    + jnp.dot(lo[:, sl], bd, preferred_element_type=F32))
    return jnp.concatenate(cols, axis=1)


segsum.defvjp(lambda x: (segsum(x), None), lambda _, g: (segsum(g),))


NORM_EPS = 1e-6
LN_EPS = 1e-5
GN_EPS = 64e-5
SGU_CHUNK = 128
SGU_GROUPS = 8


def _rms(x, g):
    return x * lax.rsqrt(jnp.mean(x * x, axis=-1, keepdims=True) + NORM_EPS) * g


def f_norm_in(x, g):
    return _rms(x, g), x


def f_sgu(p, ln_w, ln_b, sw, sbt):
    tm = p.shape[0]
    z = 0.5 * p * (1.0 + lax.erf(p * 0.7071067811865476))
    u, v = z[:, :D_MODEL], z[:, D_MODEL:]
    mu = jnp.mean(v, axis=-1, keepdims=True)
    d = v - mu
    vn = d * lax.rsqrt(jnp.mean(d * d, axis=-1, keepdims=True) + LN_EPS) * ln_w + ln_b
    ii = lax.broadcasted_iota(jnp.int32, (SGU_CHUNK, SGU_CHUNK), 0)
    jj = lax.broadcasted_iota(jnp.int32, (SGU_CHUNK, SGU_CHUNK), 1)
    mask = (jj <= ii).astype(F32)
    gi = lax.broadcasted_iota(jnp.int32, (SGU_GROUPS, D_MODEL), 0)
    ci = lax.broadcasted_iota(jnp.int32, (SGU_GROUPS, D_MODEL), 1) // SGU_CHUNK
    bias = mdot(sbt, (gi == ci).astype(F32), "nn", True)
    rows = []
    for c in range(tm // SGU_CHUNK):
        cols = []
        for g in range(SGU_GROUPS):
            blk = vn[c * SGU_CHUNK:(c + 1) * SGU_CHUNK, g * SGU_CHUNK:(g + 1) * SGU_CHUNK]
            cols.append(mdot(sw[g] * mask, blk, "nn", False))
        rows.append(jnp.concatenate(cols, axis=1) + bias)
    return (u * jnp.concatenate(rows, axis=0),)


def _softplus(x):
    return jnp.maximum(x, 0.0) + jnp.log1p(jnp.exp(-jnp.abs(x)))


def f_pre(qr, qk, qv, ql, wl, w0, al, a0, gl, k_k, k_a):
    xw, xa, xg = ql[:, :128], ql[:, 128:256], ql[:, 256:512]
    wr = -_softplus(-(w0 + mdot(jnp.tanh(xw), wl, "nn", True))) - 0.5
    lw = -jnp.exp(wr)
    aa = jax.nn.sigmoid(a0 + mdot(xa, al, "nn", True))
    g = mdot(jax.nn.sigmoid(xg), gl, "nn", True)
    kkr = qk * k_k
    kk = kkr / jnp.maximum(jnp.sqrt(segsum(kkr * kkr)), 1e-12)
    kp = qk * (1.0 + (aa - 1.0) * k_a)
    return qr, lw, kp, qv, -kk, kk * aa, g, qr, kp, qv


def f_post(o, r, kp, v, g, lnw, lnb, rk):
    mu = segsum(o) * (1.0 / HEAD)
    d = o - mu
    gn = d * lax.rsqrt(segsum(d * d) * (1.0 / HEAD) + GN_EPS)
    return ((gn * lnw + lnb + segsum(r * kp * rk) * v) * g,)


def f_mix(ya, yb, ga, gb):
    return (jax.nn.sigmoid(ga) * ya + jax.nn.sigmoid(gb) * yb,)


def f_ffn_in(h1, g):
    return _rms(h1, g), h1


def f_final(h1, m3, tgt, g):
    y = _rms(h1 + m3, g)
    err = jnp.square(y - tgt)
    return 0.5 * jnp.sum(jnp.mean(err, axis=-1))


def _cparams(n_grid):
    return pltpu.CompilerParams(dimension_semantics=("arbitrary",) * n_grid, vmem_limit_bytes=VMEM_LIMIT)


def _tile_spec(tm, w, cb):
    return pl.BlockSpec((tm, w), lambda i: (i, cb))


def _const_spec(c):
    nd = c.ndim
    return pl.BlockSpec(c.shape, lambda i: (0,) * nd)


def ew_call(fn, tiled, consts, outs, *, tm, name):
    t = tiled[0][0].shape[0]
    n_t, n_c = len(tiled), len(consts)

    def body(*refs):
        tv = [r[...].astype(F32) for r in refs[:n_t]]
        cv = [r[...] for r in refs[n_t:n_t + n_c]]
        res = fn(*tv, *cv)
        for o_ref, val in zip(refs[n_t + n_c:], res):
            o_ref[...] = val.astype(o_ref.dtype)

    return pl.pallas_call(
        body,
        name=name,
        grid=(t // tm,),
        in_specs=[_tile_spec(tm, w, cb) for _, w, cb in tiled] + [_const_spec(c) for c in consts],
        out_specs=[_tile_spec(tm, w, 0) for w, _ in outs],
        out_shape=[jax.ShapeDtypeStruct((t, w), dt) for w, dt in outs],
        compiler_params=_cparams(1),
    )(*[a for a, _, _ in tiled], *consts)


def ew_vjp_call(fn, tiled, consts, cots, d_tiled, d_consts, *, tm, name):
    t = tiled[0][0].shape[0]
    n_t, n_c, n_g = len(tiled), len(consts), len(cots)
    dt_list = [(i, dt) for i, dts in enumerate(d_tiled) for dt in dts]
    dc_list = [i for i, want in enumerate(d_consts) if want]

    def body(*refs):
        tv = [r[...].astype(F32) for r in refs[:n_t]]
        cv = [r[...] for r in refs[n_t:n_t + n_c]]
        gv = tuple(r[...].astype(F32) for r in refs[n_t + n_c:n_t + n_c + n_g])
        out_refs = refs[n_t + n_c + n_g:]
        _, vjp = jax.vjp(fn, *tv, *cv)
        grads = vjp(gv)
        for o_ref, (i, _) in zip(out_refs, dt_list):
            o_ref[...] = grads[i].astype(o_ref.dtype)
        acc_refs = out_refs[len(dt_list):]

        @pl.when(pl.program_id(0) == 0)
        def _():
            for a_ref in acc_refs:
                a_ref[...] = jnp.zeros_like(a_ref)

        for a_ref, i in zip(acc_refs, dc_list):
            a_ref[...] += grads[n_t + i]

    res = pl.pallas_call(
        body,
        name=name,
        grid=(t // tm,),
        in_specs=[_tile_spec(tm, w, cb) for _, w, cb in tiled] + [_const_spec(c) for c in consts]
        + [_tile_spec(tm, w, cb) for _, w, cb in cots],
        out_specs=[_tile_spec(tm, tiled[i][1], 0) for i, _ in dt_list] + [_const_spec(consts[i]) for i in dc_list],
        out_shape=[jax.ShapeDtypeStruct((t, tiled[i][1]), dt) for i, dt in dt_list]
        + [jax.ShapeDtypeStruct(consts[i].shape, F32) for i in dc_list],
        compiler_params=_cparams(1),
    )(*[a for a, _, _ in tiled], *consts, *[a for a, _, _ in cots])
    return res[:len(dt_list)], res[len(dt_list):]


def mm(a, b, mode, *, tm, tn, name, out_dtypes=(F32,), epi=None, extras=()):
    m = a.shape[1] if mode == "tn" else a.shape[0]
    kd = a.shape[0] if mode == "tn" else a.shape[1]
    n = b.shape[0] if mode == "nt" else b.shape[1]
    tm, tn = min(tm, m), min(tn, n)
    if mode == "nn":
        a_spec = pl.BlockSpec((tm, kd), lambda i, j: (i, 0))
        b_spec = pl.BlockSpec((kd, tn), lambda i, j: (0, j))
    elif mode == "nt":
        a_spec = pl.BlockSpec((tm, kd), lambda i, j: (i, 0))
        b_spec = pl.BlockSpec((tn, kd), lambda i, j: (j, 0))
    else:
        a_spec = pl.BlockSpec((kd, tm), lambda i, j: (0, i))
        b_spec = pl.BlockSpec((kd, tn), lambda i, j: (0, j))
    n_e = len(extras)
    o_spec = pl.BlockSpec((tm, tn), lambda i, j: (i, j))

    def body(a_ref, b_ref, *refs):
        c = lax.dot_general(a_ref[...].astype(BF16), b_ref[...].astype(BF16), _MDIMS[mode],
                            preferred_element_type=F32)
        res = epi(c, *[r[...] for r in refs[:n_e]]) if epi is not None else (c,)
        for o_ref, val in zip(refs[n_e:], res):
            o_ref[...] = val.astype(o_ref.dtype)

    res = pl.pallas_call(
        body,
        name=name,
        grid=(m // tm, n // tn),
        in_specs=[a_spec, b_spec] + [o_spec] * n_e,
        out_specs=[o_spec] * len(out_dtypes),
        out_shape=[jax.ShapeDtypeStruct((m, n), dt) for dt in out_dtypes],
        compiler_params=_cparams(2),
    )(a, b, *extras)
    return res if len(out_dtypes) > 1 else res[0]


P_WIDTH = 7680
RWKV_COL0 = 4096
RWKV_WIDTH = 3584
SHIFT_BLK = 512


def _shift_down(p, prev_row):
    rows = lax.broadcasted_iota(jnp.int32, p.shape, 0)
    return jnp.where(rows == 0, prev_row, pltpu.roll(p, 1, 0))


def shiftmix_fwd(p_all, sbp, *, tm):
    t = p_all.shape[0]
    c0 = RWKV_COL0 // SHIFT_BLK
    hb = tm // 8

    def body(p_ref, halo_ref, sb_ref, q_ref):
        p = p_ref[...]
        prev = jnp.where(pl.program_id(0) == 0, 0.0, halo_ref[7:8, :])
        q_ref[...] = p * sb_ref[0:1, :] + _shift_down(p, prev) * sb_ref[1:2, :]

    return pl.pallas_call(
        body,
        name="shiftmix_fwd",
        grid=(t // tm, RWKV_WIDTH // SHIFT_BLK),
        in_specs=[
            pl.BlockSpec((tm, SHIFT_BLK), lambda i, j: (i, c0 + j)),
            pl.BlockSpec((8, SHIFT_BLK), lambda i, j: (jnp.maximum(i * hb - 1, 0), c0 + j)),
            pl.BlockSpec((2, SHIFT_BLK), lambda i, j: (0, j)),
        ],
        out_specs=pl.BlockSpec((tm, SHIFT_BLK), lambda i, j: (i, j)),
        out_shape=jax.ShapeDtypeStruct((t, RWKV_WIDTH), F32),
        compiler_params=_cparams(2),
    )(p_all, p_all, sbp)


def shiftmix_bwd(dq, col0, p_all, sbp, *, tm, name):
    t, w = dq.shape
    n_i = t // tm
    hb = tm // 8
    cq = col0 // SHIFT_BLK
    cp = (RWKV_COL0 + col0) // SHIFT_BLK

    def body(dq_ref, dqn_ref, p_ref, ph_ref, sb_ref, dp_ref, dsb_ref):
        i = pl.program_id(1)
        dq_t = dq_ref[...]
        rows = lax.broadcasted_iota(jnp.int32, dq_t.shape, 0)
        nxt = jnp.where(i == n_i - 1, 0.0, dqn_ref[0:1, :])
        up = jnp.where(rows == tm - 1, nxt, pltpu.roll(dq_t, tm - 1, 0))
        dp_ref[...] = (dq_t * sb_ref[0:1, :] + up * sb_ref[1:2, :]).astype(dp_ref.dtype)
        p = p_ref[...]
        prev = jnp.where(i == 0, 0.0, ph_ref[7:8, :])
        s0 = jnp.sum(dq_t * p, axis=0, keepdims=True)
        s1 = jnp.sum(dq_t * _shift_down(p, prev), axis=0, keepdims=True)
        two = lax.broadcasted_iota(jnp.int32, (2, SHIFT_BLK), 0)

        @pl.when(i == 0)
        def _():
            dsb_ref[...] = jnp.zeros_like(dsb_ref)

        dsb_ref[...] += jnp.where(two == 0, s0, s1)

    return pl.pallas_call(
        body,
        name=name,
        grid=(w // SHIFT_BLK, n_i),
        in_specs=[
            pl.BlockSpec((tm, SHIFT_BLK), lambda j, i: (i, j)),
            pl.BlockSpec((8, SHIFT_BLK), lambda j, i: (jnp.minimum((i + 1) * hb, t // 8 - 1), j)),
            pl.BlockSpec((tm, SHIFT_BLK), lambda j, i: (i, cp + j)),
            pl.BlockSpec((8, SHIFT_BLK), lambda j, i: (jnp.maximum(i * hb - 1, 0), cp + j)),
            pl.BlockSpec((2, SHIFT_BLK), lambda j, i: (0, cq + j)),
        ],
        out_specs=[
            pl.BlockSpec((tm, SHIFT_BLK), lambda j, i: (i, j)),
            pl.BlockSpec((2, SHIFT_BLK), lambda j, i: (0, j)),
        ],
        out_shape=[jax.ShapeDtypeStruct((t, w), BF16), jax.ShapeDtypeStruct((2, w), F32)],
        compiler_params=_cparams(2),
    )(dq, dq, p_all, p_all, sbp)


def final_call(h1, m3, tgt, g_final, *, tm):
    t = h1.shape[0]

    def body(h1_ref, m3_ref, tgt_ref, g_ref, dh_ref, dhb_ref, dg_ref, loss_ref):
        loss, vjp = jax.vjp(f_final, h1_ref[...], m3_ref[...], tgt_ref[...], g_ref[...])
        dh, _, _, dg = vjp(jnp.ones((), F32))
        dh_ref[...] = dh
        dhb_ref[...] = dh.astype(BF16)

        @pl.when(pl.program_id(0) == 0)
        def _():
            dg_ref[...] = jnp.zeros_like(dg_ref)
            loss_ref[...] = jnp.zeros_like(loss_ref)

        dg_ref[...] += dg
        loss_ref[...] += jnp.full(loss_ref.shape, loss, F32)

    tile = _tile_spec(tm, D_MODEL, 0)
    return pl.pallas_call(
        body,
        name="final_loss",
        grid=(t // tm,),
        in_specs=[tile, tile, tile, _const_spec(g_final)],
        out_specs=[tile, tile, _const_spec(g_final), pl.BlockSpec((8, 128), lambda i: (0, 0))],
        out_shape=[jax.ShapeDtypeStruct((t, D_MODEL), F32), jax.ShapeDtypeStruct((t, D_MODEL), BF16),
                   jax.ShapeDtypeStruct(g_final.shape, F32), jax.ShapeDtypeStruct((8, 128), F32)],
        compiler_params=_cparams(1),
    )(h1, m3, tgt, g_final)


N_SGU = 2048
N_RWKV = 3360
LORA_W, LORA_A, LORA_G = 64, 64, 160


def _pad_rwkv_cols(z):
    zero = lambda n: jnp.zeros(z.shape[:-1] + (n,), z.dtype)
    return jnp.concatenate([z[..., :3072], z[..., 3072:3136], zero(64), z[..., 3136:3200], zero(64),
                            z[..., 3200:3360], zero(96)], axis=-1)


def _unpad_rwkv_cols(z):
    return jnp.concatenate([z[..., :3072], z[..., 3072:3136], z[..., 3200:3264], z[..., 3328:3488]], axis=-1)


def _pad_win_rows(wt):
    z = wt[N_SGU:N_SGU + N_RWKV]
    zero = lambda n: jnp.zeros((n, wt.shape[1]), wt.dtype)
    return jnp.concatenate([wt[:N_SGU], wt[N_SGU + N_RWKV:], z[:3072], z[3072:3136], zero(64), z[3136:3200], zero(64),
                            z[3200:3360], zero(96)], axis=0)


def _unpad_win_rows(wt):
    z = wt[RWKV_COL0:]
    return jnp.concatenate([wt[:N_SGU], z[:3072], z[3072:3136], z[3200:3264], z[3328:3488], wt[N_SGU:RWKV_COL0]],
                           axis=0)


def _pad_rows(w, n):
    return jnp.concatenate([w, jnp.zeros((n - w.shape[0],) + w.shape[1:], w.dtype)], axis=0)


def _relu2_epi(c):
    return c, jnp.square(jnp.maximum(c, 0.0))


def _relu2_bwd_epi(c, hid):
    return (c * (2.0 * jnp.maximum(hid, 0.0)),)


def _add_epi(c, x):
    return (c + x,)


def _pre_fwd(*args):
    res = f_pre(*args)
    return res[1], res[2], res[4], res[5], res[6]


def local_step(x, tgt, w, late_pack, late_weights, late_partials):
    d = D_MODEL
    win_pt = _pad_win_rows(w["w_in"])
    sbp = _pad_rwkv_cols(w["shift_b"])
    wl = _pad_rows(w["w_lora_w"], 128)
    al = _pad_rows(w["a_lora_w"], 128)
    gl = _pad_rows(w["g_lora_w"], 256)
    sbt = w["sgu_b"].T

    (a_bf,) = ew_call(lambda x_, g_: (f_norm_in(x_, g_)[0],), [(x, d, 0)], [w["g_mix"]], [(d, BF16)], tm=256,
                      name="norm_in")
    p_all = mm(a_bf, win_pt, "nt", tm=512, tn=1280, name="mm_in")
    sgu_t = [(p_all, 2 * d, 0)]
    sgu_c = [w["sgu_ln_w"], w["sgu_ln_b"], w["sgu_w"], sbt]
    (s_bf,) = ew_call(f_sgu, sgu_t, sgu_c, [(d, BF16)], tm=256, name="sgu_fwd")
    ya = mm(s_bf, w["w_proj_a"], "nn", tm=512, tn=1024, name="mm_proj_a")
    q = shiftmix_fwd(p_all, sbp, tm=256)
    pre_t = [(q, d, 0), (q, d, 1), (q, d, 2), (q, 512, 6)]
    pre_c = [wl, w["w0"], al, w["a0"], gl, w["k_k"], w["k_a"]]
    lw, kp, na, nb, g = ew_call(_pre_fwd, pre_t, pre_c, [(d, F32)] * 5, tm=256, name="rwkv_pre_fwd")
    scan_ops = [(q, 0), (lw, 0), (kp, 0), (q, 2), (na, 0), (nb, 0)]
    o, s0s, late_all = scan_fwd(scan_ops, late_pack)
    w = {**w, **late_weights(late_all)}
    post_t = [(o, d, 0), (q, d, 0), (kp, d, 0), (q, d, 2), (g, d, 0)]
    post_c = [w["ln_x_w"], w["ln_x_b"], w["r_k"]]
    (ob_bf,) = ew_call(f_post, post_t, post_c, [(d, BF16)], tm=256, name="rwkv_post_fwd")
    yb = mm(ob_bf, w["w_proj_b"], "nn", tm=512, tn=1024, name="mm_proj_b")
    mix_t = [(ya, d, 0), (yb, d, 0), (p_all, d, 2), (p_all, d, 3)]
    (mixed_bf,) = ew_call(f_mix, mix_t, [], [(d, BF16)], tm=256, name="mix_fwd")
    h1 = mm(mixed_bf, w["w_out"], "nn", tm=512, tn=1024, name="mm_out", epi=_add_epi, extras=(x,))
    (f_bf,) = ew_call(lambda h_, g_: (f_ffn_in(h_, g_)[0],), [(h1, d, 0)], [w["g_ffn"]], [(d, BF16)], tm=256,
                      name="ffn_norm")
    hid, act_bf = mm(f_bf, w["w_ffn1"], "nn", tm=512, tn=1024, name="mm_ffn1", out_dtypes=(F32, BF16), epi=_relu2_epi)
    m3 = mm(act_bf, w["w_ffn2"], "nn", tm=512, tn=1024, name="mm_ffn2")
    dh2, dh2_bf, dg_final, loss = final_call(h1, m3, tgt, w["g_final"], tm=256)

    dhid_bf = mm(dh2_bf, w["w_ffn2"], "nt", tm=512, tn=1024, name="mm_dact", out_dtypes=(BF16,), epi=_relu2_bwd_epi,
                 extras=(hid,))
    d_ffn2 = mm(act_bf, dh2_bf, "tn", tm=512, tn=1024, name="mm_dw_ffn2")
    df = mm(dhid_bf, w["w_ffn1"], "nt", tm=512, tn=1024, name="mm_df")
    d_ffn1 = mm(f_bf, dhid_bf, "tn", tm=512, tn=1024, name="mm_dw_ffn1")
    (dh1, dh1_bf), (dg_ffn,) = ew_vjp_call(f_ffn_in, [(h1, d, 0)], [w["g_ffn"]], [(df, d, 0), (dh2, d, 0)],
                                           [(F32, BF16)], [True], tm=256, name="ffn_norm_bwd")
    dmixed = mm(dh1_bf, w["w_out"], "nt", tm=512, tn=1024, name="mm_dmixed")
    d_out = mm(mixed_bf, dh1_bf, "tn", tm=512, tn=1024, name="mm_dw_out")
    (dya_bf, dyb_bf, dga_bf, dgb_bf), _ = ew_vjp_call(f_mix, mix_t, [], [(dmixed, d, 0)], [(BF16,)] * 4, [], tm=256,
                                                      name="mix_bwd")
    dob = mm(dyb_bf, w["w_proj_b"], "nt", tm=512, tn=1024, name="mm_dob")
    d_proj_b = mm(ob_bf, dyb_bf, "tn", tm=512, tn=1024, name="mm_dw_proj_b")
    (do, dr_p, dkp_p, dv_p, dg), (dlnx_w, dlnx_b, dr_k) = ew_vjp_call(
        f_post, post_t, post_c, [(dob, d, 0)], [(F32,)] * 5, [True] * 3, tm=256, name="rwkv_post_bwd")
    late_part, late_part16 = late_partials({"w_ffn1": d_ffn1, "w_ffn2": d_ffn2, "w_proj_b": d_proj_b, "w_out": d_out})
    *scan_g, late_slots = scan_bwd(scan_ops, s0s, do, late_part16)
    pre_g = [(z, d, 0) for z in scan_g] + [(dg, d, 0), (dr_p, d, 0), (dkp_p, d, 0), (dv_p, d, 0)]
    (dq_r, dq_k, dq_v, dq_l), (dwl, dw0, dal, da0, dgl, dk_k, dk_a) = ew_vjp_call(
        f_pre, pre_t, pre_c, pre_g, [(F32,)] * 4, [True] * 7, tm=128, name="rwkv_pre_bwd")
    dp_r, dsb_r = shiftmix_bwd(dq_r, 0, p_all, sbp, tm=256, name="shiftmix_bwd_r")
    dp_k, dsb_k = shiftmix_bwd(dq_k, d, p_all, sbp, tm=256, name="shiftmix_bwd_k")
    dp_v, dsb_v = shiftmix_bwd(dq_v, 2 * d, p_all, sbp, tm=256, name="shiftmix_bwd_v")
    dp_l, dsb_l = shiftmix_bwd(dq_l, 3 * d, p_all, sbp, tm=256, name="shiftmix_bwd_l")
    ds = mm(dya_bf, w["w_proj_a"], "nt", tm=512, tn=1024, name="mm_ds")
    d_proj_a = mm(s_bf, dya_bf, "tn", tm=512, tn=1024, name="mm_dw_proj_a")
    (dp_sgu,), (dln_w, dln_b, dsw, dsbt) = ew_vjp_call(f_sgu, sgu_t, sgu_c, [(ds, d, 0)], [(BF16,)], [True] * 4,
                                                       tm=256, name="sgu_bwd")
    dp_all = jnp.concatenate([dp_sgu, dga_bf, dgb_bf, dp_r, dp_k, dp_v, dp_l], axis=1)
    da = mm(dp_all, win_pt, "nn", tm=512, tn=256, name="mm_da")
    d_in_pt = mm(dp_all, a_bf, "tn", tm=1280, tn=1024, name="mm_dw_in")
    (grad_x,), (dg_mix,) = ew_vjp_call(f_norm_in, [(x, d, 0)], [w["g_mix"]], [(da, d, 0), (dh1, d, 0)], [(F32,)],
                                       [True], tm=256, name="norm_in_bwd")

    grads = {
        "g_mix": dg_mix, "w_in": _unpad_win_rows(d_in_pt), "sgu_ln_w": dln_w, "sgu_ln_b": dln_b, "sgu_w": dsw,
        "sgu_b": dsbt.T, "w_proj_a": d_proj_a,
        "shift_b": _unpad_rwkv_cols(jnp.concatenate([dsb_r, dsb_k, dsb_v, dsb_l], axis=1)),
        "w_lora_w": dwl[:LORA_W], "w0": dw0, "a_lora_w": dal[:LORA_A], "a0": da0, "g_lora_w": dgl[:LORA_G],
        "k_k": dk_k, "k_a": dk_a, "r_k": dr_k, "ln_x_w": dlnx_w, "ln_x_b": dlnx_b, "g_ffn": dg_ffn,
        "g_final": dg_final,
    }
    return loss[0, 0], grad_x, grads, (late_part, late_slots)


MESH = pl.DeviceIdType.MESH
N_CHIPS = 4
N_DEV = 8
PACK_ROWS = 2560
PACK_TILE = 512
SMALL_ROWS = 152
_ANY = pl.BlockSpec(memory_space=pl.ANY)


def _coords():
    return lax.axis_index("x"), lax.axis_index("y"), lax.axis_index("c")


def _other_chips(x, y):
    return [(1 - x, y), (x, 1 - y), (1 - x, 1 - y)]


def _remote(src, dst, send_sems, recv_sems, k, to):
    return pltpu.make_async_remote_copy(src_ref=src, dst_ref=dst, send_sem=send_sems.at[k], recv_sem=recv_sems.at[k],
                                        device_id=to, device_id_type=MESH)


def gather_shards(pack):
    def body(src_ref, out_ref, send_sems, recv_sems):
        x, y, c = _coords()
        me = 2 * x + y
        sib = (x, y, 1 - c)
        chips = _other_chips(x, y)
        first = [_remote(src_ref.at[c], out_ref.at[me, c], send_sems, recv_sems, k, (cx, cy, c))
                 for k, (cx, cy) in enumerate(chips)]
        for cp in first:
            cp.start()
        passed = []
        for k, (cx, cy) in enumerate(chips):
            j = 2 * cx + cy
            _remote(src_ref.at[c], out_ref.at[j, c], send_sems, recv_sems, k, (cx, cy, c)).wait_recv()
            fwd = _remote(out_ref.at[j, c], out_ref.at[j, c], send_sems, recv_sems, 3 + k, sib)
            fwd.start()
            passed.append(fwd)
        for k, (cx, cy) in enumerate(chips):
            j = 2 * cx + cy
            _remote(out_ref.at[j, 1 - c], out_ref.at[j, 1 - c], send_sems, recv_sems, 3 + k, sib).wait_recv()
        for cp in first + passed:
            cp.wait_send()

    return pl.pallas_call(
        body,
        name="gather_shards",
        in_specs=[_ANY],
        out_specs=_ANY,
        out_shape=jax.ShapeDtypeStruct((N_CHIPS,) + pack.shape, pack.dtype),
        scratch_shapes=[pltpu.SemaphoreType.DMA((6,)), pltpu.SemaphoreType.DMA((6,))],
    )(pack)


def reduce_pair(g, tag):
    def body(g_ref, got_ref, send_sems, recv_sems):
        x, y, c = _coords()
        sib = (x, y, 1 - c)
        sends = [_remote(g_ref.at[j, 1 - c], got_ref.at[j], send_sems, recv_sems, j, sib) for j in range(N_CHIPS)]
        for cp in sends:
            cp.start()
        for cp in sends:
            cp.wait_recv()
        for cp in sends:
            cp.wait_send()

    return pl.pallas_call(
        body,
        name="reduce_pair_" + tag,
        in_specs=[_ANY],
        out_specs=_ANY,
        out_shape=jax.ShapeDtypeStruct((N_CHIPS,) + g.shape[2:], g.dtype),
        scratch_shapes=[pltpu.SemaphoreType.DMA((N_CHIPS,)), pltpu.SemaphoreType.DMA((N_CHIPS,))],
    )(g)


def pair_sum(g, got, tag, *, tm):
    n, _, rows, width = g.shape

    def body(g0_ref, g1_ref, got_ref, out_ref, out16_ref):
        own = jnp.where(lax.axis_index("c") == 0, g0_ref[0, 0], g1_ref[0, 0])
        total = own + got_ref[0]
        out_ref[0] = total
        out16_ref[0] = total.astype(BF16)

    blk = pl.BlockSpec((1, tm, width), lambda j, i: (j, i, 0))
    return pl.pallas_call(
        body,
        name="pair_sum_" + tag,
        grid=(n, rows // tm),
        in_specs=[pl.BlockSpec((1, 1, tm, width), lambda j, i: (j, 0, i, 0)),
                  pl.BlockSpec((1, 1, tm, width), lambda j, i: (j, 1, i, 0)), blk],
        out_specs=[blk, blk],
        out_shape=[jax.ShapeDtypeStruct(got.shape, F32), jax.ShapeDtypeStruct(got.shape, BF16)],
        compiler_params=_cparams(2),
    )(g, g, got)


def reduce_chips(p):
    def body(p_ref, out_ref, send_sems, recv_sems):
        x, y, c = _coords()
        me = 2 * x + y
        chips = _other_chips(x, y)
        sends = [_remote(p_ref.at[2 * cx + cy], out_ref.at[me], send_sems, recv_sems, k, (cx, cy, c))
                 for k, (cx, cy) in enumerate(chips)]
        for cp in sends:
            cp.start()
        for k, (cx, cy) in enumerate(chips):
            _remote(p_ref.at[me], out_ref.at[2 * cx + cy], send_sems, recv_sems, k, (cx, cy, c)).wait_recv()
        for cp in sends:
            cp.wait_send()

    return pl.pallas_call(
        body,
        name="reduce_chips",
        in_specs=[_ANY],
        out_specs=_ANY,
        out_shape=jax.ShapeDtypeStruct(p.shape, p.dtype),
        scratch_shapes=[pltpu.SemaphoreType.DMA((3,)), pltpu.SemaphoreType.DMA((3,))],
    )(p)


def sum_with_own(own, slots, index_fn, *, tm, name):
    n, rows, width = slots.shape
    own3 = own.ndim == 3

    def body(*refs):
        mine = index_fn()
        acc = None
        for s in range(n):
            o = refs[s][0] if own3 else refs[0][...]
            term = jnp.where(mine == s, o, refs[(n if own3 else 1) + s][0].astype(F32))
            acc = term if acc is None else acc + term
        refs[-1][...] = acc

    slot_specs = [pl.BlockSpec((1, tm, width), lambda i, s=s: (s, i, 0)) for s in range(n)]
    own_specs = slot_specs if own3 else [pl.BlockSpec((tm, width), lambda i: (i, 0))]
    return pl.pallas_call(
        body,
        name=name,
        grid=(rows // tm,),
        in_specs=own_specs + slot_specs,
        out_specs=pl.BlockSpec((tm, width), lambda i: (i, 0)),
        out_shape=jax.ShapeDtypeStruct((rows, width), F32),
        compiler_params=_cparams(1),
    )(*([own] * (n if own3 else 1)), *([slots] * n))


def exchange_halves(s, tag):
    rq = PACK_TILE
    nq = s.shape[0] // rq

    def body(s_ref, out_ref, sbuf, rbuf, send_sems, recv_sems, in_sems, out_sems):
        x, y, c = _coords()
        sib = (x, y, 1 - c)
        rows = lambda q: pl.ds(q * rq, rq)
        loads = [pltpu.make_async_copy(s_ref.at[rows(q)], sbuf.at[rows(q)], in_sems.at[q]) for q in range(nq)]
        for cp in loads:
            cp.start()
        sends = []
        for q in range(nq):
            loads[q].wait()
            sends.append(_remote(sbuf.at[rows(q)], rbuf.at[rows(q)], send_sems, recv_sems, q, sib))
            sends[q].start()
        stores = []
        for q in range(nq):
            sends[q].wait_recv()
            stores.append(pltpu.make_async_copy(rbuf.at[rows(q)], out_ref.at[rows(q)], out_sems.at[q]))
            stores[q].start()
        for cp in sends:
            cp.wait_send()
        for cp in stores:
            cp.wait()

    return pl.pallas_call(
        body,
        name="exchange_halves_" + tag,
        in_specs=[_ANY],
        out_specs=_ANY,
        out_shape=jax.ShapeDtypeStruct(s.shape, s.dtype),
        scratch_shapes=[pltpu.VMEM(s.shape, s.dtype), pltpu.VMEM(s.shape, s.dtype)]
        + [pltpu.SemaphoreType.DMA((nq,))] * 4,
        compiler_params=pltpu.CompilerParams(vmem_limit_bytes=VMEM_LIMIT),
    )(s)


def gather_all(s):
    def body(s_ref, out_ref, send_sems, recv_sems):
        x, y, c = _coords()
        me = 4 * x + 2 * y + c
        peers = []
        for mask in range(1, N_DEV):
            px = 1 - x if mask & 4 else x
            py = 1 - y if mask & 2 else y
            pc = 1 - c if mask & 1 else c
            peers.append((px, py, pc))
        sends = [_remote(s_ref, out_ref.at[me], send_sems, recv_sems, k, peer) for k, peer in enumerate(peers)]
        for cp in sends:
            cp.start()
        for k, (px, py, pc) in enumerate(peers):
            _remote(s_ref, out_ref.at[4 * px + 2 * py + pc], send_sems, recv_sems, k, (px, py, pc)).wait_recv()
        for cp in sends:
            cp.wait_send()

    return pl.pallas_call(
        body,
        name="gather_all",
        in_specs=[_ANY],
        out_specs=_ANY,
        out_shape=jax.ShapeDtypeStruct((N_DEV,) + s.shape, s.dtype),
        scratch_shapes=[pltpu.SemaphoreType.DMA((N_DEV - 1,)), pltpu.SemaphoreType.DMA((N_DEV - 1,))],
    )(s)


ADAM_LR = 0.001
ADAM_B1 = 0.9
ADAM_B2 = 0.999
ADAM_EPS = 1e-08
ADAM_WD = 0.01
ADAM_STEP = 10


def f_adamw(g, w, m, v):
    m = ADAM_B1 * m + (1.0 - ADAM_B1) * g
    v = ADAM_B2 * v + (1.0 - ADAM_B2) * jnp.square(g)
    m_hat = m / (1.0 - ADAM_B1 ** ADAM_STEP)
    v_hat = v / (1.0 - ADAM_B2 ** ADAM_STEP)
    delta = -ADAM_LR * (m_hat / (jnp.sqrt(v_hat) + ADAM_EPS) + ADAM_WD * w)
    return delta, m, v


def adamw_call(g, w, m, v, *, tm, name):
    width = g.shape[1]
    return ew_call(f_adamw, [(g, width, 0), (w, width, 0), (m, width, 0), (v, width, 0)], [], [(width, F32)] * 3,
                   tm=tm, name=name)


def adamw_halves(g_own, g_other, w, m, v, *, tm):
    _, rows, width = w.shape

    def body(go_ref, gx_ref, w_ref, m_ref, v_ref, g_ref, d_ref, nm_ref, nv_ref):
        g = jnp.where(pl.program_id(0) == lax.axis_index("c"), go_ref[...], gx_ref[...])
        delta, nm, nv = f_adamw(g, w_ref[0], m_ref[0], v_ref[0])
        g_ref[0] = g
        d_ref[0] = delta
        nm_ref[0] = nm
        nv_ref[0] = nv

    half = pl.BlockSpec((tm, width), lambda h, i: (i, 0))
    full = pl.BlockSpec((1, tm, width), lambda h, i: (h, i, 0))
    return pl.pallas_call(
        body,
        name="adamw_sharded",
        grid=(2, rows // tm),
        in_specs=[half, half, full, full, full],
        out_specs=[full] * 4,
        out_shape=[jax.ShapeDtypeStruct(w.shape, F32)] * 4,
        compiler_params=_cparams(2),
    )(g_own, g_other, w, m, v)


EARLY = ["w_in", "w_proj_a", "w_lora_w", "a_lora_w", "g_lora_w"]
LATE = ["w_ffn1", "w_ffn2", "w_proj_b", "w_out"]
SHARDED = EARLY + LATE
LORAS = ["w_lora_w", "a_lora_w", "g_lora_w"]
HALF_W = 512
PIECE_ROWS = {"w_in": 1864, "w_ffn1": 1024, "w_ffn2": 1024, "w_proj_a": 256, "w_proj_b": 256, "w_out": 256,
              "w_lora_w": 32, "a_lora_w": 32, "g_lora_w": 80}
PIECE_OFF = {"w_in": 0, "w_proj_a": 1920, "w_lora_w": 2176, "a_lora_w": 2208, "g_lora_w": 2240,
             "w_ffn1": 0, "w_ffn2": 1024, "w_proj_b": 2048, "w_out": 2304}
LO_OFF = 2320
SHARD_AXIS = {"w_in": 1, "w_proj_a": 0, "w_lora_w": 1, "a_lora_w": 1, "g_lora_w": 1, "w_proj_b": 0, "w_out": 0,
              "w_ffn1": 1, "w_ffn2": 0}
SHARD_SHAPE = {"w_in": (1024, 1864), "w_proj_a": (256, 1024), "w_lora_w": (64, 256), "a_lora_w": (64, 256),
               "g_lora_w": (160, 256), "w_proj_b": (256, 1024), "w_out": (256, 1024), "w_ffn1": (1024, 1024),
               "w_ffn2": (1024, 1024)}
SHIFT_SHARD = (2, 840)
VECTORS = ["g_mix", "sgu_ln_w", "sgu_ln_b", "w0", "a0", "k_k", "k_a", "r_k", "ln_x_w", "ln_x_b", "g_ffn", "g_final"]
SMALL = VECTORS + ["sgu_w", "sgu_b"]
SMALL_SHAPE = {**{n: (1, 1024) for n in VECTORS}, "sgu_w": (8, 128, 128), "sgu_b": (8, 128)}
WEIGHTS = ["g_mix", "w_in", "sgu_ln_w", "sgu_ln_b", "sgu_w", "sgu_b", "w_proj_a", "shift_b", "w_lora_w", "w0",
           "a_lora_w", "a0", "g_lora_w", "k_k", "k_a", "r_k", "ln_x_w", "ln_x_b", "w_proj_b", "w_out", "g_ffn",
           "w_ffn1", "w_ffn2", "g_final"]


def _size(shape):
    n = 1
    for s in shape:
        n *= s
    return n


def _pack_rows(parts, rows, dtype):
    flat = jnp.concatenate([p.reshape(-1).astype(dtype) for p in parts])
    return jnp.concatenate([flat, jnp.zeros((rows * 1024 - flat.shape[0],), dtype)]).reshape(rows, 1024)


def _unpack_rows(packed, shapes):
    flat = packed.reshape(-1)
    out, off = [], 0
    for shp in shapes:
        out.append(flat[off:off + _size(shp)].reshape(shp))
        off += _size(shp)
    return out


def _shard_of(name, full, j):
    ax = SHARD_AXIS[name]
    n = SHARD_SHAPE[name][ax]
    return lax.slice_in_dim(full, j * n, (j + 1) * n, axis=ax)


def _pad_cols(z, n):
    return jnp.concatenate([z, jnp.zeros((z.shape[0], n - z.shape[1]), z.dtype)], axis=1)


def _row_form(name, s):
    return s.T if name == "w_in" else s


def _half_piece(name, rf, h):
    if name in LORAS:
        r = PIECE_ROWS[name]
        return _pad_cols(rf[h * r:(h + 1) * r], HALF_W)
    return rf[:, HALF_W * h:HALF_W * (h + 1)]


def _pack_half(group, rf_fn, h, dtype, tail=()):
    parts, pos, rows = [], 0, PACK_ROWS
    for n in group:
        if PIECE_OFF[n] > pos:
            parts.append(jnp.zeros((PIECE_OFF[n] - pos, HALF_W), dtype))
        parts.append(_half_piece(n, rf_fn(n), h).astype(dtype))
        pos = PIECE_OFF[n] + PIECE_ROWS[n]
    for t in tail:
        parts.append(t)
        pos += t.shape[0]
    parts.append(jnp.zeros((rows - pos, HALF_W), dtype))
    return jnp.concatenate(parts, axis=0)


def _piece(pack, name):
    return pack[PIECE_OFF[name]:PIECE_OFF[name] + PIECE_ROWS[name]]


def _join_halves(name, p0, p1):
    if name in LORAS:
        return jnp.concatenate([p0[:, :SHARD_SHAPE[name][1]], p1[:, :SHARD_SHAPE[name][1]]], axis=0)
    return jnp.concatenate([p0, p1], axis=1)


def _grad_row_form(name, full, j):
    if name == "w_in":
        return full[SHARD_SHAPE[name][1] * j:SHARD_SHAPE[name][1] * (j + 1)]
    return _shard_of(name, full, j)


def adamw_weight(name, g_own, g_other, w, m, v):
    rows, width = w.shape
    if name in LORAS:
        tm = PIECE_ROWS[name]
        grid = (2, 1)
        native = pl.BlockSpec((tm, width), lambda h, i: (h, 0))
    elif name == "w_in":
        tm, lanes = rows, 128
        grid = (2, HALF_W // lanes)
        native = pl.BlockSpec((tm, lanes), lambda h, i: (0, h * (HALF_W // lanes) + i))
    else:
        tm = 128
        grid = (2, rows // tm)
        native = pl.BlockSpec((tm, HALF_W), lambda h, i: (i, h))
    off = PIECE_OFF[name] // tm
    if name == "w_in":
        packed = pl.BlockSpec((tm, 128), lambda h, i: (0, i))
    else:
        packed = pl.BlockSpec((tm, HALF_W), lambda h, i: (off + i, 0))

    def body(go_ref, gx_ref, w_ref, m_ref, v_ref, g_ref, d_ref, nm_ref, nv_ref):
        g = jnp.where(pl.program_id(0) == lax.axis_index("c"), go_ref[...], gx_ref[...])[:, :w_ref.shape[1]]
        delta, nm, nv = f_adamw(g, w_ref[...], m_ref[...], v_ref[...])
        g_ref[...] = g
        d_ref[...] = delta
        nm_ref[...] = nm
        nv_ref[...] = nv

    return pl.pallas_call(
        body,
        name="adamw_" + name,
        grid=grid,
        in_specs=[packed, packed, native, native, native],
        out_specs=[native] * 4,
        out_shape=[jax.ShapeDtypeStruct(w.shape, F32)] * 4,
        compiler_params=_cparams(2),
    )(g_own, g_other, w, m, v)


def kernel(x, g_mix, w_in, sgu_ln_w, sgu_ln_b, sgu_w, sgu_b, w_proj_a, shift_b, w_lora_w, w0, a_lora_w, a0, g_lora_w, k_k, k_a, r_k, ln_x_w, ln_x_b, w_proj_b, w_out, g_ffn, w_ffn1, w_ffn2, g_final, loss_target, m_g_mix, m_w_in, m_sgu_ln_w, m_sgu_ln_b, m_sgu_w, m_sgu_b, m_w_proj_a, m_shift_b, m_w_lora_w, m_w0, m_a_lora_w, m_a0, m_g_lora_w, m_k_k, m_k_a, m_r_k, m_ln_x_w, m_ln_x_b, m_w_proj_b, m_w_out, m_g_ffn, m_w_ffn1, m_w_ffn2, m_g_final, v_g_mix, v_w_in, v_sgu_ln_w, v_sgu_ln_b, v_sgu_w, v_sgu_b, v_w_proj_a, v_shift_b, v_w_lora_w, v_w0, v_a_lora_w, v_a0, v_g_lora_w, v_k_k, v_k_a, v_r_k, v_ln_x_w, v_ln_x_b, v_w_proj_b, v_w_out, v_g_ffn, v_w_ffn1, v_w_ffn2, v_g_final):
    given = dict(zip(WEIGHTS, (g_mix, w_in, sgu_ln_w, sgu_ln_b, sgu_w, sgu_b, w_proj_a, shift_b, w_lora_w, w0, a_lora_w, a0, g_lora_w, k_k, k_a, r_k, ln_x_w, ln_x_b, w_proj_b, w_out, g_ffn, w_ffn1, w_ffn2, g_final)))
    mom_m = dict(zip(WEIGHTS, (m_g_mix, m_w_in, m_sgu_ln_w, m_sgu_ln_b, m_sgu_w, m_sgu_b, m_w_proj_a, m_shift_b, m_w_lora_w, m_w0, m_a_lora_w, m_a0, m_g_lora_w, m_k_k, m_k_a, m_r_k, m_ln_x_w, m_ln_x_b, m_w_proj_b, m_w_out, m_g_ffn, m_w_ffn1, m_w_ffn2, m_g_final)))
    mom_v = dict(zip(WEIGHTS, (v_g_mix, v_w_in, v_sgu_ln_w, v_sgu_ln_b, v_sgu_w, v_sgu_b, v_w_proj_a, v_shift_b, v_w_lora_w, v_w0, v_a_lora_w, v_a0, v_g_lora_w, v_k_k, v_k_a, v_r_k, v_ln_x_w, v_ln_x_b, v_w_proj_b, v_w_out, v_g_ffn, v_w_ffn1, v_w_ffn2, v_g_final)))
    chip = 2 * lax.axis_index("x") + lax.axis_index("y")

    def local_block(tree, n):
        return tree[n] if n == "g_final" else tree[n][0]

    sb = local_block(given, "shift_b")
    lo_part = lambda z: (z - z.astype(BF16).astype(F32)).astype(BF16)
    row_form = lambda tree: (lambda n: _row_form(n, local_block(tree, n)))
    tile16 = lambda z: jnp.pad(z, ((0, 16 - z.shape[0]), (0, HALF_W - z.shape[1])))
    sb_tiles = [tile16(f(sb[:, lanes])) for f in (lambda z: z.astype(BF16), lo_part)
                for lanes in (slice(0, HALF_W), slice(HALF_W, None))]
    tails = [[_half_piece(n, lo_part(local_block(given, n)), h) for n in LORAS] + sb_tiles for h in range(2)]
    pack_w = jnp.stack([_pack_half(EARLY, row_form(given), h, BF16, tails[h]) for h in range(2)])
    gathered = gather_shards(pack_w)
    gathered = lax.dynamic_update_index_in_dim(gathered, pack_w, chip, 0)
    pack_late = jnp.stack([_pack_half(LATE, row_form(given), h, BF16) for h in range(2)])

    def whole(group, got):
        shard = lambda n, j: _join_halves(n, _piece(got[j, 0], n), _piece(got[j, 1], n))
        return {n: jnp.concatenate([shard(n, j).astype(F32 if n == "w_in" else BF16) for j in range(N_CHIPS)],
                                   axis=0 if n == "w_in" else SHARD_AXIS[n]) for n in group}

    w = whole(EARLY, gathered)
    late_weights = lambda got: whole(LATE, lax.dynamic_update_index_in_dim(got, pack_late, chip, 0))
    off = LO_OFF
    for n in LORAS:
        r, cols = PIECE_ROWS[n], SHARD_SHAPE[n][1]
        lo = jnp.concatenate([jnp.concatenate([gathered[j, 0, off:off + r, :cols], gathered[j, 1, off:off + r, :cols]],
                                              axis=0) for j in range(N_CHIPS)], axis=1)
        w[n] = w[n].astype(F32) + lo.astype(F32)
        off += r
    sb_tile = lambda j, t, lanes: gathered[j, 0, off + 16 * t:off + 16 * t + 2, :lanes].astype(F32)
    rest = SHIFT_SHARD[1] - HALF_W
    w["shift_b"] = jnp.concatenate(
        [jnp.concatenate([sb_tile(j, 0, HALF_W) + sb_tile(j, 2, HALF_W), sb_tile(j, 1, rest) + sb_tile(j, 3, rest)],
                         axis=1) for j in range(N_CHIPS)], axis=1)
    for n in SMALL:
        w[n] = local_block(given, n).reshape(SMALL_SHAPE[n])

    def partials(group, g, tag):
        g_pack = jnp.stack([jnp.stack([_pack_half(group, lambda n: _grad_row_form(n, g[n], j), h, F32)
                                       for h in range(2)]) for j in range(N_CHIPS)])
        return pair_sum(g_pack, reduce_pair(g_pack, tag), tag, tm=PACK_TILE)

    loss, grad_x, grads, (late_part, late_slots) = local_step(
        x[0], loss_target[0], w, pack_late, late_weights, lambda g: partials(LATE, g, "late"))
    loss = lax.psum(loss, ("x", "y", "c"))

    early_part, early_part16 = partials(EARLY, grads, "early")
    my_chip = lambda: 2 * lax.axis_index("x") + lax.axis_index("y")
    out_g, out_d, out_m, out_v = {}, {}, {}, {}
    for group, tag, part, slots in ((LATE, "late", late_part, late_slots),
                                    (EARLY, "early", early_part, reduce_chips(early_part16))):
        half_sum = sum_with_own(part, slots, my_chip, tm=PACK_TILE, name="chip_sum_" + tag)
        other_half = exchange_halves(half_sum, tag)
        for n in group:
            res = adamw_weight(n, half_sum, other_half,
                               *[_row_form(n, local_block(t, n)) for t in (given, mom_m, mom_v)])
            for tree, z in zip((out_g, out_d, out_m, out_v), res):
                tree[n] = _row_form(n, z)

    small_shapes = [SMALL_SHAPE[n] for n in SMALL]
    s_pack = _pack_rows([grads[n] for n in SMALL] + [grads["shift_b"]], SMALL_ROWS, F32)
    g_small = sum_with_own(
        s_pack, gather_all(s_pack), lambda: 4 * lax.axis_index("x") + 2 * lax.axis_index("y") + lax.axis_index("c"),
        tm=SMALL_ROWS, name="small_sum")
    w_small = _pack_rows([local_block(given, n) for n in SMALL], SMALL_ROWS, F32)
    m_small = _pack_rows([local_block(mom_m, n) for n in SMALL], SMALL_ROWS, F32)
    v_small = _pack_rows([local_block(mom_v, n) for n in SMALL], SMALL_ROWS, F32)
    d_small, nm_small, nv_small = adamw_call(g_small, w_small, m_small, v_small, tm=SMALL_ROWS, name="adamw_small")
    g_parts = _unpack_rows(g_small, small_shapes + [(2, N_RWKV)])
    out_g.update(zip(SMALL, g_parts[:-1]))
    out_d.update(zip(SMALL, _unpack_rows(d_small, small_shapes)))
    out_m.update(zip(SMALL, _unpack_rows(nm_small, small_shapes)))
    out_v.update(zip(SMALL, _unpack_rows(nv_small, small_shapes)))
    g_sb = lax.dynamic_slice_in_dim(g_parts[-1], chip * SHIFT_SHARD[1], SHIFT_SHARD[1], axis=1)
    sb_args = [_pack_rows([z], 8, F32) for z in (g_sb, sb, local_block(mom_m, "shift_b"), local_block(mom_v, "shift_b"))]
    sb_res = adamw_call(*sb_args, tm=8, name="adamw_shift_b")
    out_g["shift_b"] = g_sb
    for tree, res in zip((out_d, out_m, out_v), sb_res):
        tree["shift_b"] = _unpack_rows(res, [SHIFT_SHARD])[0]

    def block_of(tree, n):
        return tree[n].reshape(given[n].shape)

    return (loss, grad_x[None], *[block_of(out_g, n) for n in WEIGHTS], *[block_of(out_d, n) for n in WEIGHTS],
            *[block_of(out_m, n) for n in WEIGHTS], *[block_of(out_v, n) for n in WEIGHTS])
```

```python
import functools

import jax
import jax.numpy as jnp
from jax import lax
from jax.experimental import pallas as pl
from jax.experimental.pallas import tpu as pltpu

F32 = jnp.float32
BF16 = jnp.bfloat16

D_MODEL = 1024
N_HEADS = 16
HEAD = 64
SCAN_CHUNK = 64

VMEM_LIMIT = 56 * 1024 * 1024


_BDIMS = {
    "nn": (((2,), (1,)), ((0,), (0,))),
    "nt": (((2,), (2,)), ((0,), (0,))),
    "tn": (((1,), (1,)), ((0,), (0,))),
}


def _raw_bdot(x, y, mode, fine):
    if fine:
        return lax.dot_general(x, y, _BDIMS[mode], precision=lax.Precision.HIGH, preferred_element_type=F32)
    return lax.dot_general(x.astype(BF16), y.astype(BF16), _BDIMS[mode], preferred_element_type=F32)


@functools.partial(jax.custom_vjp, nondiff_argnums=(2, 3))
def bdot(x, y, mode, fine=True):
    return _raw_bdot(x, y, mode, fine)


def _bdot_fwd(x, y, mode, fine):
    return _raw_bdot(x, y, mode, fine), (x, y)


def _bdot_bwd(mode, fine, res, g):
    x, y = res
    if mode == "nn":
        return bdot(g, y, "nt", fine), bdot(x, g, "tn", fine)
    if mode == "nt":
        return bdot(g, y, "nn", fine), bdot(g, x, "tn", fine)
    return bdot(y, g, "nt", fine), bdot(x, g, "nn", fine)


bdot.defvjp(_bdot_fwd, _bdot_bwd)


def _scan_chunk(S0, r, lw, k, v, a, b):
    nh, lc, _ = r.shape
    ti = lax.broadcasted_iota(jnp.int32, (lc, lc), 0)
    si = lax.broadcasted_iota(jnp.int32, (lc, lc), 1)
    incl = (si <= ti).astype(F32)
    strict = (si < ti).astype(F32)
    eye = (si == ti).astype(F32)
    cl = bdot(jnp.broadcast_to(incl, (nh, lc, lc)), lw, "nn")
    cl_last = cl[:, lc - 1:lc, :]
    g_last = jnp.exp(cl_last - cl)
    at = a * jnp.exp(cl - lw)
    bt = b * jnp.exp(-cl)
    kt = k * jnp.exp(-cl)
    rt = r * jnp.exp(cl)
    ar = jnp.concatenate([at, rt], axis=1)
    ar_b = bdot(ar, bt, "nt")
    ar_k = bdot(ar, kt, "nt")
    m_ab, m_rb = ar_b[:, :lc] * strict, ar_b[:, lc:] * incl
    m_ak, m_rk = ar_k[:, :lc] * strict, ar_k[:, lc:] * incl
    x = eye + m_ab
    p = bdot(m_ab, m_ab, "nn", False)
    n = 2
    while n * 2 < lc:
        px = bdot(jnp.concatenate([p, x], axis=1), p, "nn", False)
        p = px[:, :lc]
        x = x + px[:, lc:]
        n *= 2
    x = x + bdot(x, p, "nn", False)
    ar_s = bdot(ar, S0, "nt", False)
    akrk_v = bdot(jnp.concatenate([m_ak, m_rk], axis=1), v, "nn")
    u = bdot(x, ar_s[:, :lc] + akrk_v[:, :lc], "nn", False)
    o = ar_s[:, lc:] + bdot(m_rb, u, "nn", False) + akrk_v[:, lc:]
    s_last = S0 * jnp.exp(cl_last) + bdot(jnp.concatenate([u, v], axis=1),
                                          jnp.concatenate([b * g_last, k * g_last], axis=1), "tn", False)
    return o, s_last


def _split_heads(z):
    return jnp.stack([z[:, HEAD * h:HEAD * (h + 1)] for h in range(N_HEADS)], axis=0)


def _merge_heads(z):
    return jnp.concatenate([z[h] for h in range(N_HEADS)], axis=1)


def _scan_specs(t, ops, rev):
    nc = t // SCAN_CHUNK
    row = (lambda c: nc - 1 - c) if rev else (lambda c: c)
    specs = [pl.BlockSpec((SCAN_CHUNK, D_MODEL), lambda c, cb=cb: (row(c), cb)) for _, cb in ops]
    state = pl.BlockSpec((1, N_HEADS, HEAD, HEAD), lambda c: (row(c), 0, 0, 0))
    return nc, specs, state


def scan_fwd(ops, pack):
    t = ops[0][0].shape[0]
    nc, specs, state = _scan_specs(t, ops, False)

    def body(r_ref, lw_ref, k_ref, v_ref, a_ref, b_ref, pack_ref, o_ref, s0_ref, all_ref, s_scr, send_sems, recv_sems):
        step = pl.program_id(0)
        x, y, c = _coords()
        me = 2 * x + y
        sib = (x, y, 1 - c)
        chips = _other_chips(x, y)
        first = [_remote(pack_ref.at[c], all_ref.at[me, c], send_sems, recv_sems, k, (cx, cy, c))
                 for k, (cx, cy) in enumerate(chips)]
        passed = [_remote(all_ref.at[2 * cx + cy, c], all_ref.at[2 * cx + cy, c], send_sems, recv_sems, 3 + k, sib)
                  for k, (cx, cy) in enumerate(chips)]

        @pl.when(step == 0)
        def _():
            s_scr[...] = jnp.zeros_like(s_scr)
            for cp in first:
                cp.start()

        s0 = s_scr[...]
        s0_ref[0] = s0
        o, s_last = _scan_chunk(s0, *[_split_heads(z[...]) for z in (r_ref, lw_ref, k_ref, v_ref, a_ref, b_ref)])
        o_ref[...] = _merge_heads(o)
        s_scr[...] = s_last

        @pl.when(step == nc - 1)
        def _():
            for k, (cx, cy) in enumerate(chips):
                j = 2 * cx + cy
                _remote(pack_ref.at[c], all_ref.at[j, c], send_sems, recv_sems, k, (cx, cy, c)).wait_recv()
                passed[k].start()
            for k, (cx, cy) in enumerate(chips):
                j = 2 * cx + cy
                _remote(all_ref.at[j, 1 - c], all_ref.at[j, 1 - c], send_sems, recv_sems, 3 + k, sib).wait_recv()
            for cp in first + passed:
                cp.wait_send()

    return pl.pallas_call(
        body,
        name="scan_fwd",
        grid=(nc,),
        in_specs=specs + [_ANY],
        out_specs=[pl.BlockSpec((SCAN_CHUNK, D_MODEL), lambda c: (c, 0)), state, _ANY],
        out_shape=[jax.ShapeDtypeStruct((t, D_MODEL), F32), jax.ShapeDtypeStruct((nc, N_HEADS, HEAD, HEAD), F32),
                   jax.ShapeDtypeStruct((N_CHIPS,) + pack.shape, pack.dtype)],
        scratch_shapes=[pltpu.VMEM((N_HEADS, HEAD, HEAD), F32), pltpu.SemaphoreType.DMA((6,)),
                        pltpu.SemaphoreType.DMA((6,))],
        compiler_params=_cparams(1),
    )(*[a for a, _ in ops], pack)


def scan_bwd(ops, s0s, do, part):
    t = ops[0][0].shape[0]
    nc, specs, state = _scan_specs(t, ops + [(do, 0)], True)

    def body(r_ref, lw_ref, k_ref, v_ref, a_ref, b_ref, do_ref, s0_ref, part_ref, *rest):
        out_refs, slots_ref, ds_scr, send_sems, recv_sems = rest[:6], rest[6], rest[7], rest[8], rest[9]
        step = pl.program_id(0)
        x, y, c = _coords()
        me = 2 * x + y
        chips = _other_chips(x, y)
        sends = [_remote(part_ref.at[2 * cx + cy], slots_ref.at[me], send_sems, recv_sems, k, (cx, cy, c))
                 for k, (cx, cy) in enumerate(chips)]

        @pl.when(step == 0)
        def _():
            ds_scr[...] = jnp.zeros_like(ds_scr)
            for cp in sends:
                cp.start()

        _, vjp = jax.vjp(_scan_chunk, s0_ref[0],
                         *[_split_heads(z[...]) for z in (r_ref, lw_ref, k_ref, v_ref, a_ref, b_ref)])
        grads = vjp((_split_heads(do_ref[...]), ds_scr[...]))
        for o_ref, g in zip(out_refs, grads[1:]):
            o_ref[...] = _merge_heads(g)
        ds_scr[...] = grads[0]

        @pl.when(step == nc - 1)
        def _():
            for k, (cx, cy) in enumerate(chips):
                _remote(part_ref.at[me], slots_ref.at[2 * cx + cy], send_sems, recv_sems, k, (cx, cy, c)).wait_recv()
            for cp in sends:
                cp.wait_send()

    return pl.pallas_call(
        body,
        name="scan_bwd",
        grid=(nc,),
        in_specs=specs + [state, _ANY],
        out_specs=[pl.BlockSpec((SCAN_CHUNK, D_MODEL), lambda c: (nc - 1 - c, 0))] * 6 + [_ANY],
        out_shape=[jax.ShapeDtypeStruct((t, D_MODEL), F32)] * 6 + [jax.ShapeDtypeStruct(part.shape, part.dtype)],
        scratch_shapes=[pltpu.VMEM((N_HEADS, HEAD, HEAD), F32), pltpu.SemaphoreType.DMA((3,)),
                        pltpu.SemaphoreType.DMA((3,))],
        compiler_params=_cparams(1),
    )(*[a for a, _ in ops], do, s0s, part)


_MDIMS = {
    "nn": (((1,), (0,)), ((), ())),
    "nt": (((1,), (1,)), ((), ())),
    "tn": (((0,), (0,)), ((), ())),
}


def _raw_mdot(x, y, mode, exact):
    if exact:
        return lax.dot_general(x, y, _MDIMS[mode], precision=lax.Precision.HIGH, preferred_element_type=F32)
    return lax.dot_general(x.astype(BF16), y.astype(BF16), _MDIMS[mode], preferred_element_type=F32)


@functools.partial(jax.custom_vjp, nondiff_argnums=(2, 3))
def mdot(x, y, mode, exact):
    return _raw_mdot(x, y, mode, exact)


def _mdot_fwd(x, y, mode, exact):
    return _raw_mdot(x, y, mode, exact), (x, y)


def _mdot_bwd(mode, exact, res, g):
    x, y = res
    if mode == "nn":
        return mdot(g, y, "nt", exact), mdot(x, g, "tn", exact)
    if mode == "nt":
        return mdot(g, y, "nn", exact), mdot(g, x, "tn", exact)
    return mdot(y, g, "nt", exact), mdot(x, g, "nn", exact)


mdot.defvjp(_mdot_fwd, _mdot_bwd)


def _seg_ones():
    i = lax.broadcasted_iota(jnp.int32, (256, 256), 0) // HEAD
    j = lax.broadcasted_iota(jnp.int32, (256, 256), 1) // HEAD
    return (i == j).astype(BF16)


@jax.custom_vjp
def segsum(x):
    bd = _seg_ones()
    hi = x.astype(BF16)
    lo = (x - hi.astype(F32)).astype(BF16)
    cols = []
    for j in range(x.shape[1] // 256):
        sl = slice(256 * j, 256 * (j + 1))
        cols.append(jnp.dot(hi[:, sl], bd, preferred_element_type=F32)
                    + jnp.dot(lo[:, sl], bd, preferred_element_type=F32))
    return jnp.concatenate(cols, axis=1)


segsum.defvjp(lambda x: (segsum(x), None), lambda _, g: (segsum(g),))


NORM_EPS = 1e-6
LN_EPS = 1e-5
GN_EPS = 64e-5
SGU_CHUNK = 128
SGU_GROUPS = 8


def _rms(x, g):
    return x * lax.rsqrt(jnp.mean(x * x, axis=-1, keepdims=True) + NORM_EPS) * g


def f_norm_in(x, g):
    return _rms(x, g), x


def f_sgu(p, ln_w, ln_b, sw, sbt):
    tm = p.shape[0]
    z = 0.5 * p * (1.0 + lax.erf(p * 0.7071067811865476))
    u, v = z[:, :D_MODEL], z[:, D_MODEL:]
    mu = jnp.mean(v, axis=-1, keepdims=True)
    d = v - mu
    vn = d * lax.rsqrt(jnp.mean(d * d, axis=-1, keepdims=True) + LN_EPS) * ln_w + ln_b
    ii = lax.broadcasted_iota(jnp.int32, (SGU_CHUNK, SGU_CHUNK), 0)
    jj = lax.broadcasted_iota(jnp.int32, (SGU_CHUNK, SGU_CHUNK), 1)
    mask = (jj <= ii).astype(F32)
    gi = lax.broadcasted_iota(jnp.int32, (SGU_GROUPS, D_MODEL), 0)
    ci = lax.broadcasted_iota(jnp.int32, (SGU_GROUPS, D_MODEL), 1) // SGU_CHUNK
    bias = mdot(sbt, (gi == ci).astype(F32), "nn", True)
    rows = []
    for c in range(tm // SGU_CHUNK):
        cols = []
        for g in range(SGU_GROUPS):
            blk = vn[c * SGU_CHUNK:(c + 1) * SGU_CHUNK, g * SGU_CHUNK:(g + 1) * SGU_CHUNK]
            cols.append(mdot(sw[g] * mask, blk, "nn", False))
        rows.append(jnp.concatenate(cols, axis=1) + bias)
    return (u * jnp.concatenate(rows, axis=0),)


def _softplus(x):
    return jnp.maximum(x, 0.0) + jnp.log1p(jnp.exp(-jnp.abs(x)))


def f_pre(qr, qk, qv, ql, wl, w0, al, a0, gl, k_k, k_a):
    xw, xa, xg = ql[:, :128], ql[:, 128:256], ql[:, 256:512]
    wr = -_softplus(-(w0 + mdot(jnp.tanh(xw), wl, "nn", True))) - 0.5
    lw = -jnp.exp(wr)
    aa = jax.nn.sigmoid(a0 + mdot(xa, al, "nn", True))
    g = mdot(jax.nn.sigmoid(xg), gl, "nn", True)
    kkr = qk * k_k
    kk = kkr / jnp.maximum(jnp.sqrt(segsum(kkr * kkr)), 1e-12)
    kp = qk * (1.0 + (aa - 1.0) * k_a)
    return qr, lw, kp, qv, -kk, kk * aa, g, qr, kp, qv


def f_post(o, r, kp, v, g, lnw, lnb, rk):
    mu = segsum(o) * (1.0 / HEAD)
    d = o - mu
    gn = d * lax.rsqrt(segsum(d * d) * (1.0 / HEAD) + GN_EPS)
    return ((gn * lnw + lnb + segsum(r * kp * rk) * v) * g,)


def f_mix(ya, yb, ga, gb):
    return (jax.nn.sigmoid(ga) * ya + jax.nn.sigmoid(gb) * yb,)


def f_ffn_in(h1, g):
    return _rms(h1, g), h1


def f_final(h1, m3, tgt, g):
    y = _rms(h1 + m3, g)
    err = jnp.square(y - tgt)
    return 0.5 * jnp.sum(jnp.mean(err, axis=-1))


def _cparams(n_grid):
    return pltpu.CompilerParams(dimension_semantics=("arbitrary",) * n_grid, vmem_limit_bytes=VMEM_LIMIT)


def _tile_spec(tm, w, cb):
    return pl.BlockSpec((tm, w), lambda i: (i, cb))


def _const_spec(c):
    nd = c.ndim
    return pl.BlockSpec(c.shape, lambda i: (0,) * nd)


def ew_call(fn, tiled, consts, outs, *, tm, name):
    t = tiled[0][0].shape[0]
    n_t, n_c = len(tiled), len(consts)

    def body(*refs):
        tv = [r[...].astype(F32) for r in refs[:n_t]]
        cv = [r[...] for r in refs[n_t:n_t + n_c]]
        res = fn(*tv, *cv)
        for o_ref, val in zip(refs[n_t + n_c:], res):
            o_ref[...] = val.astype(o_ref.dtype)

    return pl.pallas_call(
        body,
        name=name,
        grid=(t // tm,),
        in_specs=[_tile_spec(tm, w, cb) for _, w, cb in tiled] + [_const_spec(c) for c in consts],
        out_specs=[_tile_spec(tm, w, 0) for w, _ in outs],
        out_shape=[jax.ShapeDtypeStruct((t, w), dt) for w, dt in outs],
        compiler_params=_cparams(1),
    )(*[a for a, _, _ in tiled], *consts)


def ew_vjp_call(fn, tiled, consts, cots, d_tiled, d_consts, *, tm, name):
    t = tiled[0][0].shape[0]
    n_t, n_c, n_g = len(tiled), len(consts), len(cots)
    dt_list = [(i, dt) for i, dts in enumerate(d_tiled) for dt in dts]
    dc_list = [i for i, want in enumerate(d_consts) if want]

    def body(*refs):
        tv = [r[...].astype(F32) for r in refs[:n_t]]
        cv = [r[...] for r in refs[n_t:n_t + n_c]]
        gv = tuple(r[...].astype(F32) for r in refs[n_t + n_c:n_t + n_c + n_g])
        out_refs = refs[n_t + n_c + n_g:]
        _, vjp = jax.vjp(fn, *tv, *cv)
        grads = vjp(gv)
        for o_ref, (i, _) in zip(out_refs, dt_list):
            o_ref[...] = grads[i].astype(o_ref.dtype)
        acc_refs = out_refs[len(dt_list):]

        @pl.when(pl.program_id(0) == 0)
        def _():
            for a_ref in acc_refs:
                a_ref[...] = jnp.zeros_like(a_ref)

        for a_ref, i in zip(acc_refs, dc_list):
            a_ref[...] += grads[n_t + i]

    res = pl.pallas_call(
        body,
        name=name,
        grid=(t // tm,),
        in_specs=[_tile_spec(tm, w, cb) for _, w, cb in tiled] + [_const_spec(c) for c in consts]
        + [_tile_spec(tm, w, cb) for _, w, cb in cots],
        out_specs=[_tile_spec(tm, tiled[i][1], 0) for i, _ in dt_list] + [_const_spec(consts[i]) for i in dc_list],
        out_shape=[jax.ShapeDtypeStruct((t, tiled[i][1]), dt) for i, dt in dt_list]
        + [jax.ShapeDtypeStruct(consts[i].shape, F32) for i in dc_list],
        compiler_params=_cparams(1),
    )(*[a for a, _, _ in tiled], *consts, *[a for a, _, _ in cots])
    return res[:len(dt_list)], res[len(dt_list):]


def mm(a, b, mode, *, tm, tn, name, out_dtypes=(F32,), epi=None, extras=()):
    m = a.shape[1] if mode == "tn" else a.shape[0]
    kd = a.shape[0] if mode == "tn" else a.shape[1]
    n = b.shape[0] if mode == "nt" else b.shape[1]
    tm, tn = min(tm, m), min(tn, n)
    if mode == "nn":
        a_spec = pl.BlockSpec((tm, kd), lambda i, j: (i, 0))
        b_spec = pl.BlockSpec((kd, tn), lambda i, j: (0, j))
    elif mode == "nt":
        a_spec = pl.BlockSpec((tm, kd), lambda i, j: (i, 0))
        b_spec = pl.BlockSpec((tn, kd), lambda i, j: (j, 0))
    else:
        a_spec = pl.BlockSpec((kd, tm), lambda i, j: (0, i))
        b_spec = pl.BlockSpec((kd, tn), lambda i, j: (0, j))
    n_e = len(extras)
    o_spec = pl.BlockSpec((tm, tn), lambda i, j: (i, j))

    def body(a_ref, b_ref, *refs):
        c = lax.dot_general(a_ref[...].astype(BF16), b_ref[...].astype(BF16), _MDIMS[mode],
                            preferred_element_type=F32)
        res = epi(c, *[r[...] for r in refs[:n_e]]) if epi is not None else (c,)
        for o_ref, val in zip(refs[n_e:], res):
            o_ref[...] = val.astype(o_ref.dtype)

    res = pl.pallas_call(
        body,
        name=name,
        grid=(m // tm, n // tn),
        in_specs=[a_spec, b_spec] + [o_spec] * n_e,
        out_specs=[o_spec] * len(out_dtypes),
        out_shape=[jax.ShapeDtypeStruct((m, n), dt) for dt in out_dtypes],
        compiler_params=_cparams(2),
    )(a, b, *extras)
    return res if len(out_dtypes) > 1 else res[0]


P_WIDTH = 7680
RWKV_COL0 = 4096
RWKV_WIDTH = 3584
SHIFT_BLK = 512


def _shift_down(p, prev_row):
    rows = lax.broadcasted_iota(jnp.int32, p.shape, 0)
    return jnp.where(rows == 0, prev_row, pltpu.roll(p, 1, 0))


def shiftmix_fwd(p_all, sbp, *, tm):
    t = p_all.shape[0]
    tm = min(tm, t)
    c0 = RWKV_COL0 // SHIFT_BLK
    hb = tm // 8

    def body(p_ref, halo_ref, sb_ref, q_ref):
        p = p_ref[...]
        prev = jnp.where(pl.program_id(0) == 0, 0.0, halo_ref[7:8, :])
        q_ref[...] = p * sb_ref[0:1, :] + _shift_down(p, prev) * sb_ref[1:2, :]

    return pl.pallas_call(
        body,
        name="shiftmix_fwd",
        grid=(t // tm, RWKV_WIDTH // SHIFT_BLK),
        in_specs=[
            pl.BlockSpec((tm, SHIFT_BLK), lambda i, j: (i, c0 + j)),
            pl.BlockSpec((8, SHIFT_BLK), lambda i, j: (jnp.maximum(i * hb - 1, 0), c0 + j)),
            pl.BlockSpec((2, SHIFT_BLK), lambda i, j: (0, j)),
        ],
        out_specs=pl.BlockSpec((tm, SHIFT_BLK), lambda i, j: (i, j)),
        out_shape=jax.ShapeDtypeStruct((t, RWKV_WIDTH), F32),
        compiler_params=_cparams(2),
    )(p_all, p_all, sbp)


def shiftmix_bwd(dq, col0, p_all, sbp, *, tm, name):
    t, w = dq.shape
    n_i = t // tm
    hb = tm // 8
    cq = col0 // SHIFT_BLK
    cp = (RWKV_COL0 + col0) // SHIFT_BLK

    def body(dq_ref, dqn_ref, p_ref, ph_ref, sb_ref, dp_ref, dsb_ref):
        i = pl.program_id(1)
        dq_t = dq_ref[...]
        rows = lax.broadcasted_iota(jnp.int32, dq_t.shape, 0)
        nxt = jnp.where(i == n_i - 1, 0.0, dqn_ref[0:1, :])
        up = jnp.where(rows == tm - 1, nxt, pltpu.roll(dq_t, tm - 1, 0))
        dp_ref[...] = (dq_t * sb_ref[0:1, :] + up * sb_ref[1:2, :]).astype(dp_ref.dtype)
        p = p_ref[...]
        prev = jnp.where(i == 0, 0.0, ph_ref[7:8, :])
        s0 = jnp.sum(dq_t * p, axis=0, keepdims=True)
        s1 = jnp.sum(dq_t * _shift_down(p, prev), axis=0, keepdims=True)
        two = lax.broadcasted_iota(jnp.int32, (2, SHIFT_BLK), 0)

        @pl.when(i == 0)
        def _():
            dsb_ref[...] = jnp.zeros_like(dsb_ref)

        dsb_ref[...] += jnp.where(two == 0, s0, s1)

    return pl.pallas_call(
        body,
        name=name,
        grid=(w // SHIFT_BLK, n_i),
        in_specs=[
            pl.BlockSpec((tm, SHIFT_BLK), lambda j, i: (i, j)),
            pl.BlockSpec((8, SHIFT_BLK), lambda j, i: (jnp.minimum((i + 1) * hb, t // 8 - 1), j)),
            pl.BlockSpec((tm, SHIFT_BLK), lambda j, i: (i, cp + j)),
            pl.BlockSpec((8, SHIFT_BLK), lambda j, i: (jnp.maximum(i * hb - 1, 0), cp + j)),
            pl.BlockSpec((2, SHIFT_BLK), lambda j, i: (0, cq + j)),
        ],
        out_specs=[
            pl.BlockSpec((tm, SHIFT_BLK), lambda j, i: (i, j)),
            pl.BlockSpec((2, SHIFT_BLK), lambda j, i: (0, j)),
        ],
        out_shape=[jax.ShapeDtypeStruct((t, w), BF16), jax.ShapeDtypeStruct((2, w), F32)],
        compiler_params=_cparams(2),
    )(dq, dq, p_all, p_all, sbp)


def final_call(h1, m3, tgt, g_final, *, tm):
    t = h1.shape[0]

    def body(h1_ref, m3_ref, tgt_ref, g_ref, dh_ref, dhb_ref, dg_ref, loss_ref):
        loss, vjp = jax.vjp(f_final, h1_ref[...], m3_ref[...], tgt_ref[...], g_ref[...])
        dh, _, _, dg = vjp(jnp.ones((), F32))
        dh_ref[...] = dh
        dhb_ref[...] = dh.astype(BF16)

        @pl.when(pl.program_id(0) == 0)
        def _():
            dg_ref[...] = jnp.zeros_like(dg_ref)
            loss_ref[...] = jnp.zeros_like(loss_ref)

        dg_ref[...] += dg
        loss_ref[...] += jnp.full(loss_ref.shape, loss, F32)

    tile = _tile_spec(tm, D_MODEL, 0)
    return pl.pallas_call(
        body,
        name="final_loss",
        grid=(t // tm,),
        in_specs=[tile, tile, tile, _const_spec(g_final)],
        out_specs=[tile, tile, _const_spec(g_final), pl.BlockSpec((8, 128), lambda i: (0, 0))],
        out_shape=[jax.ShapeDtypeStruct((t, D_MODEL), F32), jax.ShapeDtypeStruct((t, D_MODEL), BF16),
                   jax.ShapeDtypeStruct(g_final.shape, F32), jax.ShapeDtypeStruct((8, 128), F32)],
        compiler_params=_cparams(1),
    )(h1, m3, tgt, g_final)


N_SGU = 2048
N_RWKV = 3360
LORA_W, LORA_A, LORA_G = 64, 64, 160


def _pad_rwkv_cols(z):
    zero = lambda n: jnp.zeros(z.shape[:-1] + (n,), z.dtype)
    return jnp.concatenate([z[..., :3072], z[..., 3072:3136], zero(64), z[..., 3136:3200], zero(64),
                            z[..., 3200:3360], zero(96)], axis=-1)


def _unpad_rwkv_cols(z):
    return jnp.concatenate([z[..., :3072], z[..., 3072:3136], z[..., 3200:3264], z[..., 3328:3488]], axis=-1)


def _pad_win_rows(wt):
    z = wt[N_SGU:N_SGU + N_RWKV]
    zero = lambda n: jnp.zeros((n, wt.shape[1]), wt.dtype)
    return jnp.concatenate([wt[:N_SGU], wt[N_SGU + N_RWKV:], z[:3072], z[3072:3136], zero(64), z[3136:3200], zero(64),
                            z[3200:3360], zero(96)], axis=0)


def _unpad_win_rows(wt):
    z = wt[RWKV_COL0:]
    return jnp.concatenate([wt[:N_SGU], z[:3072], z[3072:3136], z[3200:3264], z[3328:3488], wt[N_SGU:RWKV_COL0]],
                           axis=0)


def _pad_rows(w, n):
    return jnp.concatenate([w, jnp.zeros((n - w.shape[0],) + w.shape[1:], w.dtype)], axis=0)


def _relu2_epi(c):
    return c, jnp.square(jnp.maximum(c, 0.0))


def _relu2_bwd_epi(c, hid):
    return (c * (2.0 * jnp.maximum(hid, 0.0)),)


def _add_epi(c, x):
    return (c + x,)


def _pre_fwd(*args):
    res = f_pre(*args)
    return res[1], res[2], res[4], res[5], res[6]


def local_step(x, tgt, w, late_pack, late_weights, late_partials):
    d = D_MODEL
    win_pt = _pad_win_rows(w["w_in"])
    sbp = _pad_rwkv_cols(w["shift_b"])
    wl = _pad_rows(w["w_lora_w"], 128)
    al = _pad_rows(w["a_lora_w"], 128)
    gl = _pad_rows(w["g_lora_w"], 256)
    sbt = w["sgu_b"].T

    (a_bf,) = ew_call(lambda x_, g_: (f_norm_in(x_, g_)[0],), [(x, d, 0)], [w["g_mix"]], [(d, BF16)], tm=256,
                      name="norm_in")
    p_all = mm(a_bf, win_pt, "nt", tm=2048, tn=640, name="mm_in")
    sgu_t = [(p_all, 2 * d, 0)]
    sgu_c = [w["sgu_ln_w"], w["sgu_ln_b"], w["sgu_w"], sbt]
    (s_bf,) = ew_call(f_sgu, sgu_t, sgu_c, [(d, BF16)], tm=256, name="sgu_fwd")
    ya = mm(s_bf, w["w_proj_a"], "nn", tm=512, tn=1024, name="mm_proj_a")
    q = shiftmix_fwd(p_all, sbp, tm=1024)
    pre_t = [(q, d, 0), (q, d, 1), (q, d, 2), (q, 512, 6)]
    pre_c = [wl, w["w0"], al, w["a0"], gl, w["k_k"], w["k_a"]]
    lw, kp, na, nb, g = ew_call(_pre_fwd, pre_t, pre_c, [(d, F32)] * 5, tm=256, name="rwkv_pre_fwd")
    scan_ops = [(q, 0), (lw, 0), (kp, 0), (q, 2), (na, 0), (nb, 0)]
    o, s0s, late_all = scan_fwd(scan_ops, late_pack)
    w = {**w, **late_weights(late_all)}
    post_t = [(o, d, 0), (q, d, 0), (kp, d, 0), (q, d, 2), (g, d, 0)]
    post_c = [w["ln_x_w"], w["ln_x_b"], w["r_k"]]
    (ob_bf,) = ew_call(f_post, post_t, post_c, [(d, BF16)], tm=256, name="rwkv_post_fwd")
    yb = mm(ob_bf, w["w_proj_b"], "nn", tm=512, tn=1024, name="mm_proj_b")
    mix_t = [(ya, d, 0), (yb, d, 0), (p_all, d, 2), (p_all, d, 3)]
    (mixed_bf,) = ew_call(f_mix, mix_t, [], [(d, BF16)], tm=256, name="mix_fwd")
    h1 = mm(mixed_bf, w["w_out"], "nn", tm=512, tn=1024, name="mm_out", epi=_add_epi, extras=(x,))
    (f_bf,) = ew_call(lambda h_, g_: (f_ffn_in(h_, g_)[0],), [(h1, d, 0)], [w["g_ffn"]], [(d, BF16)], tm=256,
                      name="ffn_norm")
    hid, act_bf = mm(f_bf, w["w_ffn1"], "nn", tm=2048, tn=1024, name="mm_ffn1", out_dtypes=(F32, BF16), epi=_relu2_epi)
    m3 = mm(act_bf, w["w_ffn2"], "nn", tm=1024, tn=512, name="mm_ffn2")
    dh2, dh2_bf, dg_final, loss = final_call(h1, m3, tgt, w["g_final"], tm=256)

    dhid_bf = mm(dh2_bf, w["w_ffn2"], "nt", tm=2048, tn=1024, name="mm_dact", out_dtypes=(BF16,), epi=_relu2_bwd_epi,
                 extras=(hid,))
    d_ffn2 = mm(act_bf, dh2_bf, "tn", tm=512, tn=1024, name="mm_dw_ffn2")
    df = mm(dhid_bf, w["w_ffn1"], "nt", tm=1024, tn=512, name="mm_df")
    d_ffn1 = mm(f_bf, dhid_bf, "tn", tm=512, tn=1024, name="mm_dw_ffn1")
    (dh1, dh1_bf), (dg_ffn,) = ew_vjp_call(f_ffn_in, [(h1, d, 0)], [w["g_ffn"]], [(df, d, 0), (dh2, d, 0)],
                                           [(F32, BF16)], [True], tm=256, name="ffn_norm_bwd")
    dmixed = mm(dh1_bf, w["w_out"], "nt", tm=512, tn=1024, name="mm_dmixed")
    d_out = mm(mixed_bf, dh1_bf, "tn", tm=512, tn=1024, name="mm_dw_out")
    (dya_bf, dyb_bf, dga_bf, dgb_bf), _ = ew_vjp_call(f_mix, mix_t, [], [(dmixed, d, 0)], [(BF16,)] * 4, [], tm=256,
                                                      name="mix_bwd")
    dob = mm(dyb_bf, w["w_proj_b"], "nt", tm=512, tn=1024, name="mm_dob")
    d_proj_b = mm(ob_bf, dyb_bf, "tn", tm=512, tn=1024, name="mm_dw_proj_b")
    (do, dr_p, dkp_p, dv_p, dg), (dlnx_w, dlnx_b, dr_k) = ew_vjp_call(
        f_post, post_t, post_c, [(dob, d, 0)], [(F32,)] * 5, [True] * 3, tm=256, name="rwkv_post_bwd")
    late_part, late_part16 = late_partials({"w_ffn1": d_ffn1, "w_ffn2": d_ffn2, "w_proj_b": d_proj_b, "w_out": d_out})
    *scan_g, late_slots = scan_bwd(scan_ops, s0s, do, late_part16)
    pre_g = [(z, d, 0) for z in scan_g] + [(dg, d, 0), (dr_p, d, 0), (dkp_p, d, 0), (dv_p, d, 0)]
    (dq_r, dq_k, dq_v, dq_l), (dwl, dw0, dal, da0, dgl, dk_k, dk_a) = ew_vjp_call(
        f_pre, pre_t, pre_c, pre_g, [(F32,)] * 4, [True] * 7, tm=128, name="rwkv_pre_bwd")
    dp_r, dsb_r = shiftmix_bwd(dq_r, 0, p_all, sbp, tm=256, name="shiftmix_bwd_r")
    dp_k, dsb_k = shiftmix_bwd(dq_k, d, p_all, sbp, tm=256, name="shiftmix_bwd_k")
    dp_v, dsb_v = shiftmix_bwd(dq_v, 2 * d, p_all, sbp, tm=256, name="shiftmix_bwd_v")
    dp_l, dsb_l = shiftmix_bwd(dq_l, 3 * d, p_all, sbp, tm=256, name="shiftmix_bwd_l")
    ds = mm(dya_bf, w["w_proj_a"], "nt", tm=512, tn=1024, name="mm_ds")
    d_proj_a = mm(s_bf, dya_bf, "tn", tm=512, tn=1024, name="mm_dw_proj_a")
    (dp_sgu,), (dln_w, dln_b, dsw, dsbt) = ew_vjp_call(f_sgu, sgu_t, sgu_c, [(ds, d, 0)], [(BF16,)], [True] * 4,
                                                       tm=256, name="sgu_bwd")
    dp_all = jnp.concatenate([dp_sgu, dga_bf, dgb_bf, dp_r, dp_k, dp_v, dp_l], axis=1)
    da = mm(dp_all, win_pt, "nn", tm=1024, tn=256, name="mm_da")
    d_in_pt = mm(dp_all, a_bf, "tn", tm=1280, tn=1024, name="mm_dw_in")
    (grad_x,), (dg_mix,) = ew_vjp_call(f_norm_in, [(x, d, 0)], [w["g_mix"]], [(da, d, 0), (dh1, d, 0)], [(F32,)],
                                       [True], tm=256, name="norm_in_bwd")

    grads = {
        "g_mix": dg_mix, "w_in": _unpad_win_rows(d_in_pt), "sgu_ln_w": dln_w, "sgu_ln_b": dln_b, "sgu_w": dsw,
        "sgu_b": dsbt.T, "w_proj_a": d_proj_a,
        "shift_b": _unpad_rwkv_cols(jnp.concatenate([dsb_r, dsb_k, dsb_v, dsb_l], axis=1)),
        "w_lora_w": dwl[:LORA_W], "w0": dw0, "a_lora_w": dal[:LORA_A], "a0": da0, "g_lora_w": dgl[:LORA_G],
        "k_k": dk_k, "k_a": dk_a, "r_k": dr_k, "ln_x_w": dlnx_w, "ln_x_b": dlnx_b, "g_ffn": dg_ffn,
        "g_final": dg_final,
    }
    return loss[0, 0], grad_x, grads, (late_part, late_slots)


MESH = pl.DeviceIdType.MESH
N_CHIPS = 4
N_DEV = 8
PACK_ROWS = 2560
PACK_TILE = 512
SMALL_ROWS = 152
_ANY = pl.BlockSpec(memory_space=pl.ANY)


def _coords():
    return lax.axis_index("x"), lax.axis_index("y"), lax.axis_index("c")


def _other_chips(x, y):
    return [(1 - x, y), (x, 1 - y), (1 - x, 1 - y)]


def _remote(src, dst, send_sems, recv_sems, k, to):
    return pltpu.make_async_remote_copy(src_ref=src, dst_ref=dst, send_sem=send_sems.at[k], recv_sem=recv_sems.at[k],
                                        device_id=to, device_id_type=MESH)


def gather_shards(pack):
    def body(src_ref, out_ref, send_sems, recv_sems):
        x, y, c = _coords()
        me = 2 * x + y
        sib = (x, y, 1 - c)
        chips = _other_chips(x, y)
        first = [_remote(src_ref.at[c], out_ref.at[me, c], send_sems, recv_sems, k, (cx, cy, c))
                 for k, (cx, cy) in enumerate(chips)]
        for cp in first:
            cp.start()
        passed = []
        for k, (cx, cy) in enumerate(chips):
            j = 2 * cx + cy
            _remote(src_ref.at[c], out_ref.at[j, c], send_sems, recv_sems, k, (cx, cy, c)).wait_recv()
            fwd = _remote(out_ref.at[j, c], out_ref.at[j, c], send_sems, recv_sems, 3 + k, sib)
            fwd.start()
            passed.append(fwd)
        for k, (cx, cy) in enumerate(chips):
            j = 2 * cx + cy
            _remote(out_ref.at[j, 1 - c], out_ref.at[j, 1 - c], send_sems, recv_sems, 3 + k, sib).wait_recv()
        for cp in first + passed:
            cp.wait_send()

    return pl.pallas_call(
        body,
        name="gather_shards",
        in_specs=[_ANY],
        out_specs=_ANY,
        out_shape=jax.ShapeDtypeStruct((N_CHIPS,) + pack.shape, pack.dtype),
        scratch_shapes=[pltpu.SemaphoreType.DMA((6,)), pltpu.SemaphoreType.DMA((6,))],
    )(pack)


def reduce_pair(g, tag):
    def body(g_ref, got_ref, send_sems, recv_sems):
        x, y, c = _coords()
        sib = (x, y, 1 - c)
        sends = [_remote(g_ref.at[j, 1 - c], got_ref.at[j], send_sems, recv_sems, j, sib) for j in range(N_CHIPS)]
        for cp in sends:
            cp.start()
        for cp in sends:
            cp.wait_recv()
        for cp in sends:
            cp.wait_send()

    return pl.pallas_call(
        body,
        name="reduce_pair_" + tag,
        in_specs=[_ANY],
        out_specs=_ANY,
        out_shape=jax.ShapeDtypeStruct((N_CHIPS,) + g.shape[2:], g.dtype),
        scratch_shapes=[pltpu.SemaphoreType.DMA((N_CHIPS,)), pltpu.SemaphoreType.DMA((N_CHIPS,))],
    )(g)


def pair_sum(g, got, tag, *, tm):
    n, _, rows, width = g.shape

    def body(g0_ref, g1_ref, got_ref, out_ref, out16_ref):
        own = jnp.where(lax.axis_index("c") == 0, g0_ref[0, 0], g1_ref[0, 0])
        total = own + got_ref[0]
        out_ref[0] = total
        out16_ref[0] = total.astype(BF16)

    blk = pl.BlockSpec((1, tm, width), lambda j, i: (j, i, 0))
    return pl.pallas_call(
        body,
        name="pair_sum_" + tag,
        grid=(n, rows // tm),
        in_specs=[pl.BlockSpec((1, 1, tm, width), lambda j, i: (j, 0, i, 0)),
                  pl.BlockSpec((1, 1, tm, width), lambda j, i: (j, 1, i, 0)), blk],
        out_specs=[blk, blk],
        out_shape=[jax.ShapeDtypeStruct(got.shape, F32), jax.ShapeDtypeStruct(got.shape, BF16)],
        compiler_params=_cparams(2),
    )(g, g, got)


def reduce_chips(p):
    def body(p_ref, out_ref, send_sems, recv_sems):
        x, y, c = _coords()
        me = 2 * x + y
        chips = _other_chips(x, y)
        sends = [_remote(p_ref.at[2 * cx + cy], out_ref.at[me], send_sems, recv_sems, k, (cx, cy, c))
                 for k, (cx, cy) in enumerate(chips)]
        for cp in sends:
            cp.start()
        for k, (cx, cy) in enumerate(chips):
            _remote(p_ref.at[me], out_ref.at[2 * cx + cy], send_sems, recv_sems, k, (cx, cy, c)).wait_recv()
        for cp in sends:
            cp.wait_send()

    return pl.pallas_call(
        body,
        name="reduce_chips",
        in_specs=[_ANY],
        out_specs=_ANY,
        out_shape=jax.ShapeDtypeStruct(p.shape, p.dtype),
        scratch_shapes=[pltpu.SemaphoreType.DMA((3,)), pltpu.SemaphoreType.DMA((3,))],
    )(p)


def sum_with_own(own, slots, index_fn, *, tm, name):
    n, rows, width = slots.shape
    own3 = own.ndim == 3

    def body(*refs):
        mine = index_fn()
        acc = None
        for s in range(n):
            o = refs[s][0] if own3 else refs[0][...]
            term = jnp.where(mine == s, o, refs[(n if own3 else 1) + s][0].astype(F32))
            acc = term if acc is None else acc + term
        refs[-1][...] = acc

    slot_specs = [pl.BlockSpec((1, tm, width), lambda i, s=s: (s, i, 0)) for s in range(n)]
    own_specs = slot_specs if own3 else [pl.BlockSpec((tm, width), lambda i: (i, 0))]
    return pl.pallas_call(
        body,
        name=name,
        grid=(rows // tm,),
        in_specs=own_specs + slot_specs,
        out_specs=pl.BlockSpec((tm, width), lambda i: (i, 0)),
        out_shape=jax.ShapeDtypeStruct((rows, width), F32),
        compiler_params=_cparams(1),
    )(*([own] * (n if own3 else 1)), *([slots] * n))


def exchange_halves(s, tag):
    rq = PACK_TILE
    nq = s.shape[0] // rq

    def body(s_ref, out_ref, sbuf, rbuf, send_sems, recv_sems, in_sems, out_sems):
        x, y, c = _coords()
        sib = (x, y, 1 - c)
        rows = lambda q: pl.ds(q * rq, rq)
        loads = [pltpu.make_async_copy(s_ref.at[rows(q)], sbuf.at[rows(q)], in_sems.at[q]) for q in range(nq)]
        for cp in loads:
            cp.start()
        sends = []
        for q in range(nq):
            loads[q].wait()
            sends.append(_remote(sbuf.at[rows(q)], rbuf.at[rows(q)], send_sems, recv_sems, q, sib))
            sends[q].start()
        stores = []
        for q in range(nq):
            sends[q].wait_recv()
            stores.append(pltpu.make_async_copy(rbuf.at[rows(q)], out_ref.at[rows(q)], out_sems.at[q]))
            stores[q].start()
        for cp in sends:
            cp.wait_send()
        for cp in stores:
            cp.wait()

    return pl.pallas_call(
        body,
        name="exchange_halves_" + tag,
        in_specs=[_ANY],
        out_specs=_ANY,
        out_shape=jax.ShapeDtypeStruct(s.shape, s.dtype),
        scratch_shapes=[pltpu.VMEM(s.shape, s.dtype), pltpu.VMEM(s.shape, s.dtype)]
        + [pltpu.SemaphoreType.DMA((nq,))] * 4,
        compiler_params=pltpu.CompilerParams(vmem_limit_bytes=VMEM_LIMIT),
    )(s)


def gather_all(s):
    def body(s_ref, out_ref, send_sems, recv_sems):
        x, y, c = _coords()
        me = 4 * x + 2 * y + c
        peers = []
        for mask in range(1, N_DEV):
            px = 1 - x if mask & 4 else x
            py = 1 - y if mask & 2 else y
            pc = 1 - c if mask & 1 else c
            peers.append((px, py, pc))
        sends = [_remote(s_ref, out_ref.at[me], send_sems, recv_sems, k, peer) for k, peer in enumerate(peers)]
        for cp in sends:
            cp.start()
        for k, (px, py, pc) in enumerate(peers):
            _remote(s_ref, out_ref.at[4 * px + 2 * py + pc], send_sems, recv_sems, k, (px, py, pc)).wait_recv()
        for cp in sends:
            cp.wait_send()

    return pl.pallas_call(
        body,
        name="gather_all",
        in_specs=[_ANY],
        out_specs=_ANY,
        out_shape=jax.ShapeDtypeStruct((N_DEV,) + s.shape, s.dtype),
        scratch_shapes=[pltpu.SemaphoreType.DMA((N_DEV - 1,)), pltpu.SemaphoreType.DMA((N_DEV - 1,))],
    )(s)


ADAM_LR = 0.001
ADAM_B1 = 0.9
ADAM_B2 = 0.999
ADAM_EPS = 1e-08
ADAM_WD = 0.01
ADAM_STEP = 10


def f_adamw(g, w, m, v):
    m = ADAM_B1 * m + (1.0 - ADAM_B1) * g
    v = ADAM_B2 * v + (1.0 - ADAM_B2) * jnp.square(g)
    m_hat = m / (1.0 - ADAM_B1 ** ADAM_STEP)
    v_hat = v / (1.0 - ADAM_B2 ** ADAM_STEP)
    delta = -ADAM_LR * (m_hat / (jnp.sqrt(v_hat) + ADAM_EPS) + ADAM_WD * w)
    return delta, m, v


def adamw_call(g, w, m, v, *, tm, name):
    width = g.shape[1]
    return ew_call(f_adamw, [(g, width, 0), (w, width, 0), (m, width, 0), (v, width, 0)], [], [(width, F32)] * 3,
                   tm=tm, name=name)


def adamw_halves(g_own, g_other, w, m, v, *, tm):
    _, rows, width = w.shape

    def body(go_ref, gx_ref, w_ref, m_ref, v_ref, g_ref, d_ref, nm_ref, nv_ref):
        g = jnp.where(pl.program_id(0) == lax.axis_index("c"), go_ref[...], gx_ref[...])
        delta, nm, nv = f_adamw(g, w_ref[0], m_ref[0], v_ref[0])
        g_ref[0] = g
        d_ref[0] = delta
        nm_ref[0] = nm
        nv_ref[0] = nv

    half = pl.BlockSpec((tm, width), lambda h, i: (i, 0))
    full = pl.BlockSpec((1, tm, width), lambda h, i: (h, i, 0))
    return pl.pallas_call(
        body,
        name="adamw_sharded",
        grid=(2, rows // tm),
        in_specs=[half, half, full, full, full],
        out_specs=[full] * 4,
        out_shape=[jax.ShapeDtypeStruct(w.shape, F32)] * 4,
        compiler_params=_cparams(2),
    )(g_own, g_other, w, m, v)


EARLY = ["w_in", "w_proj_a", "w_lora_w", "a_lora_w", "g_lora_w"]
LATE = ["w_ffn1", "w_ffn2", "w_proj_b", "w_out"]
SHARDED = EARLY + LATE
LORAS = ["w_lora_w", "a_lora_w", "g_lora_w"]
HALF_W = 512
PIECE_ROWS = {"w_in": 1864, "w_ffn1": 1024, "w_ffn2": 1024, "w_proj_a": 256, "w_proj_b": 256, "w_out": 256,
              "w_lora_w": 32, "a_lora_w": 32, "g_lora_w": 80}
PIECE_OFF = {"w_in": 0, "w_proj_a": 1920, "w_lora_w": 2176, "a_lora_w": 2208, "g_lora_w": 2240,
             "w_ffn1": 0, "w_ffn2": 1024, "w_proj_b": 2048, "w_out": 2304}
LO_OFF = 2320
SHARD_AXIS = {"w_in": 1, "w_proj_a": 0, "w_lora_w": 1, "a_lora_w": 1, "g_lora_w": 1, "w_proj_b": 0, "w_out": 0,
              "w_ffn1": 1, "w_ffn2": 0}
SHARD_SHAPE = {"w_in": (1024, 1864), "w_proj_a": (256, 1024), "w_lora_w": (64, 256), "a_lora_w": (64, 256),
               "g_lora_w": (160, 256), "w_proj_b": (256, 1024), "w_out": (256, 1024), "w_ffn1": (1024, 1024),
               "w_ffn2": (1024, 1024)}
SHIFT_SHARD = (2, 840)
VECTORS = ["g_mix", "sgu_ln_w", "sgu_ln_b", "w0", "a0", "k_k", "k_a", "r_k", "ln_x_w", "ln_x_b", "g_ffn", "g_final"]
SMALL = VECTORS + ["sgu_w", "sgu_b"]
SMALL_SHAPE = {**{n: (1, 1024) for n in VECTORS}, "sgu_w": (8, 128, 128), "sgu_b": (8, 128)}
WEIGHTS = ["g_mix", "w_in", "sgu_ln_w", "sgu_ln_b", "sgu_w", "sgu_b", "w_proj_a", "shift_b", "w_lora_w", "w0",
           "a_lora_w", "a0", "g_lora_w", "k_k", "k_a", "r_k", "ln_x_w", "ln_x_b", "w_proj_b", "w_out", "g_ffn",
           "w_ffn1", "w_ffn2", "g_final"]


def _size(shape):
    n = 1
    for s in shape:
        n *= s
    return n


def _pack_rows(parts, rows, dtype):
    flat = jnp.concatenate([p.reshape(-1).astype(dtype) for p in parts])
    return jnp.concatenate([flat, jnp.zeros((rows * 1024 - flat.shape[0],), dtype)]).reshape(rows, 1024)


def _unpack_rows(packed, shapes):
    flat = packed.reshape(-1)
    out, off = [], 0
    for shp in shapes:
        out.append(flat[off:off + _size(shp)].reshape(shp))
        off += _size(shp)
    return out


def _shard_of(name, full, j):
    ax = SHARD_AXIS[name]
    n = SHARD_SHAPE[name][ax]
    return lax.slice_in_dim(full, j * n, (j + 1) * n, axis=ax)


def _pad_cols(z, n):
    return jnp.concatenate([z, jnp.zeros((z.shape[0], n - z.shape[1]), z.dtype)], axis=1)


def _row_form(name, s):
    return s.T if name == "w_in" else s


def _half_piece(name, rf, h):
    if name in LORAS:
        r = PIECE_ROWS[name]
        return _pad_cols(rf[h * r:(h + 1) * r], HALF_W)
    return rf[:, HALF_W * h:HALF_W * (h + 1)]


def _pack_half(group, rf_fn, h, dtype, tail=()):
    parts, pos, rows = [], 0, PACK_ROWS
    for n in group:
        if PIECE_OFF[n] > pos:
            parts.append(jnp.zeros((PIECE_OFF[n] - pos, HALF_W), dtype))
        parts.append(_half_piece(n, rf_fn(n), h).astype(dtype))
        pos = PIECE_OFF[n] + PIECE_ROWS[n]
    for t in tail:
        parts.append(t)
        pos += t.shape[0]
    parts.append(jnp.zeros((rows - pos, HALF_W), dtype))
    return jnp.concatenate(parts, axis=0)


def _piece(pack, name):
    return pack[PIECE_OFF[name]:PIECE_OFF[name] + PIECE_ROWS[name]]


def _join_halves(name, p0, p1):
    if name in LORAS:
        return jnp.concatenate([p0[:, :SHARD_SHAPE[name][1]], p1[:, :SHARD_SHAPE[name][1]]], axis=0)
    return jnp.concatenate([p0, p1], axis=1)


def _grad_row_form(name, full, j):
    if name == "w_in":
        return full[SHARD_SHAPE[name][1] * j:SHARD_SHAPE[name][1] * (j + 1)]
    return _shard_of(name, full, j)


def adamw_weight(name, g_own, g_other, w, m, v):
    rows, width = w.shape
    if name in LORAS:
        tm = PIECE_ROWS[name]
        grid = (2, 1)
        native = pl.BlockSpec((tm, width), lambda h, i: (h, 0))
    elif name == "w_in":
        tm, lanes = rows, 128
        grid = (2, HALF_W // lanes)
        native = pl.BlockSpec((tm, lanes), lambda h, i: (0, h * (HALF_W // lanes) + i))
    else:
        tm = 128
        grid = (2, rows // tm)
        native = pl.BlockSpec((tm, HALF_W), lambda h, i: (i, h))
    off = PIECE_OFF[name] // tm
    if name == "w_in":
        packed = pl.BlockSpec((tm, 128), lambda h, i: (0, i))
    else:
        packed = pl.BlockSpec((tm, HALF_W), lambda h, i: (off + i, 0))

    def body(go_ref, gx_ref, w_ref, m_ref, v_ref, g_ref, d_ref, nm_ref, nv_ref):
        g = jnp.where(pl.program_id(0) == lax.axis_index("c"), go_ref[...], gx_ref[...])[:, :w_ref.shape[1]]
        delta, nm, nv = f_adamw(g, w_ref[...], m_ref[...], v_ref[...])
        g_ref[...] = g
        d_ref[...] = delta
        nm_ref[...] = nm
        nv_ref[...] = nv

    return pl.pallas_call(
        body,
        name="adamw_" + name,
        grid=grid,
        in_specs=[packed, packed, native, native, native],
        out_specs=[native] * 4,
        out_shape=[jax.ShapeDtypeStruct(w.shape, F32)] * 4,
        compiler_params=_cparams(2),
    )(g_own, g_other, w, m, v)


def kernel(x, g_mix, w_in, sgu_ln_w, sgu_ln_b, sgu_w, sgu_b, w_proj_a, shift_b, w_lora_w, w0, a_lora_w, a0, g_lora_w, k_k, k_a, r_k, ln_x_w, ln_x_b, w_proj_b, w_out, g_ffn, w_ffn1, w_ffn2, g_final, loss_target, m_g_mix, m_w_in, m_sgu_ln_w, m_sgu_ln_b, m_sgu_w, m_sgu_b, m_w_proj_a, m_shift_b, m_w_lora_w, m_w0, m_a_lora_w, m_a0, m_g_lora_w, m_k_k, m_k_a, m_r_k, m_ln_x_w, m_ln_x_b, m_w_proj_b, m_w_out, m_g_ffn, m_w_ffn1, m_w_ffn2, m_g_final, v_g_mix, v_w_in, v_sgu_ln_w, v_sgu_ln_b, v_sgu_w, v_sgu_b, v_w_proj_a, v_shift_b, v_w_lora_w, v_w0, v_a_lora_w, v_a0, v_g_lora_w, v_k_k, v_k_a, v_r_k, v_ln_x_w, v_ln_x_b, v_w_proj_b, v_w_out, v_g_ffn, v_w_ffn1, v_w_ffn2, v_g_final):
    given = dict(zip(WEIGHTS, (g_mix, w_in, sgu_ln_w, sgu_ln_b, sgu_w, sgu_b, w_proj_a, shift_b, w_lora_w, w0, a_lora_w, a0, g_lora_w, k_k, k_a, r_k, ln_x_w, ln_x_b, w_proj_b, w_out, g_ffn, w_ffn1, w_ffn2, g_final)))
    mom_m = dict(zip(WEIGHTS, (m_g_mix, m_w_in, m_sgu_ln_w, m_sgu_ln_b, m_sgu_w, m_sgu_b, m_w_proj_a, m_shift_b, m_w_lora_w, m_w0, m_a_lora_w, m_a0, m_g_lora_w, m_k_k, m_k_a, m_r_k, m_ln_x_w, m_ln_x_b, m_w_proj_b, m_w_out, m_g_ffn, m_w_ffn1, m_w_ffn2, m_g_final)))
    mom_v = dict(zip(WEIGHTS, (v_g_mix, v_w_in, v_sgu_ln_w, v_sgu_ln_b, v_sgu_w, v_sgu_b, v_w_proj_a, v_shift_b, v_w_lora_w, v_w0, v_a_lora_w, v_a0, v_g_lora_w, v_k_k, v_k_a, v_r_k, v_ln_x_w, v_ln_x_b, v_w_proj_b, v_w_out, v_g_ffn, v_w_ffn1, v_w_ffn2, v_g_final)))
    chip = 2 * lax.axis_index("x") + lax.axis_index("y")

    def local_block(tree, n):
        return tree[n] if n == "g_final" else tree[n][0]

    sb = local_block(given, "shift_b")
    lo_part = lambda z: (z - z.astype(BF16).astype(F32)).astype(BF16)
    row_form = lambda tree: (lambda n: _row_form(n, local_block(tree, n)))
    tile16 = lambda z: jnp.pad(z, ((0, 16 - z.shape[0]), (0, HALF_W - z.shape[1])))
    sb_tiles = [tile16(f(sb[:, lanes])) for f in (lambda z: z.astype(BF16), lo_part)
                for lanes in (slice(0, HALF_W), slice(HALF_W, None))]
    tails = [[_half_piece(n, lo_part(local_block(given, n)), h) for n in LORAS] + sb_tiles for h in range(2)]
    pack_w = jnp.stack([_pack_half(EARLY, row_form(given), h, BF16, tails[h]) for h in range(2)])
    gathered = gather_shards(pack_w)
    gathered = lax.dynamic_update_index_in_dim(gathered, pack_w, chip, 0)
    pack_late = jnp.stack([_pack_half(LATE, row_form(given), h, BF16) for h in range(2)])

    def whole(group, got):
        shard = lambda n, j: _join_halves(n, _piece(got[j, 0], n), _piece(got[j, 1], n))
        return {n: jnp.concatenate([shard(n, j).astype(F32 if n == "w_in" else BF16) for j in range(N_CHIPS)],
                                   axis=0 if n == "w_in" else SHARD_AXIS[n]) for n in group}

    w = whole(EARLY, gathered)
    late_weights = lambda got: whole(LATE, lax.dynamic_update_index_in_dim(got, pack_late, chip, 0))
    off = LO_OFF
    for n in LORAS:
        r, cols = PIECE_ROWS[n], SHARD_SHAPE[n][1]
        lo = jnp.concatenate([jnp.concatenate([gathered[j, 0, off:off + r, :cols], gathered[j, 1, off:off + r, :cols]],
                                              axis=0) for j in range(N_CHIPS)], axis=1)
        w[n] = w[n].astype(F32) + lo.astype(F32)
        off += r
    sb_tile = lambda j, t, lanes: gathered[j, 0, off + 16 * t:off + 16 * t + 2, :lanes].astype(F32)
    rest = SHIFT_SHARD[1] - HALF_W
    w["shift_b"] = jnp.concatenate(
        [jnp.concatenate([sb_tile(j, 0, HALF_W) + sb_tile(j, 2, HALF_W), sb_tile(j, 1, rest) + sb_tile(j, 3, rest)],
                         axis=1) for j in range(N_CHIPS)], axis=1)
    for n in SMALL:
        w[n] = local_block(given, n).reshape(SMALL_SHAPE[n])

    def partials(group, g, tag):
        g_pack = jnp.stack([jnp.stack([_pack_half(group, lambda n: _grad_row_form(n, g[n], j), h, F32)
                                       for h in range(2)]) for j in range(N_CHIPS)])
        return pair_sum(g_pack, reduce_pair(g_pack, tag), tag, tm=PACK_TILE)

    loss, grad_x, grads, (late_part, late_slots) = local_step(
        x[0], loss_target[0], w, pack_late, late_weights, lambda g: partials(LATE, g, "late"))
    loss = lax.psum(loss, ("x", "y", "c"))

    early_part, early_part16 = partials(EARLY, grads, "early")
    my_chip = lambda: 2 * lax.axis_index("x") + lax.axis_index("y")
    out_g, out_d, out_m, out_v = {}, {}, {}, {}
    for group, tag, part, slots in ((LATE, "late", late_part, late_slots),
                                    (EARLY, "early", early_part, reduce_chips(early_part16))):
        half_sum = sum_with_own(part, slots, my_chip, tm=PACK_TILE, name="chip_sum_" + tag)
        other_half = exchange_halves(half_sum, tag)
        for n in group:
            res = adamw_weight(n, half_sum, other_half,
                               *[_row_form(n, local_block(t, n)) for t in (given, mom_m, mom_v)])
            for tree, z in zip((out_g, out_d, out_m, out_v), res):
                tree[n] = _row_form(n, z)

    small_shapes = [SMALL_SHAPE[n] for n in SMALL]
    s_pack = _pack_rows([grads[n] for n in SMALL] + [grads["shift_b"]], SMALL_ROWS, F32)
    g_small = sum_with_own(
        s_pack, gather_all(s_pack), lambda: 4 * lax.axis_index("x") + 2 * lax.axis_index("y") + lax.axis_index("c"),
        tm=SMALL_ROWS, name="small_sum")
    w_small = _pack_rows([local_block(given, n) for n in SMALL], SMALL_ROWS, F32)
    m_small = _pack_rows([local_block(mom_m, n) for n in SMALL], SMALL_ROWS, F32)
    v_small = _pack_rows([local_block(mom_v, n) for n in SMALL], SMALL_ROWS, F32)
    d_small, nm_small, nv_small = adamw_call(g_small, w_small, m_small, v_small, tm=SMALL_ROWS, name="adamw_small")
    g_parts = _unpack_rows(g_small, small_shapes + [(2, N_RWKV)])
    out_g.update(zip(SMALL, g_parts[:-1]))
    out_d.update(zip(SMALL, _unpack_rows(d_small, small_shapes)))
    out_m.update(zip(SMALL, _unpack_rows(nm_small, small_shapes)))
    out_v.update(zip(SMALL, _unpack_rows(nv_small, small_shapes)))
    g_sb = lax.dynamic_slice_in_dim(g_parts[-1], chip * SHIFT_SHARD[1], SHIFT_SHARD[1], axis=1)
    sb_args = [_pack_rows([z], 8, F32) for z in (g_sb, sb, local_block(mom_m, "shift_b"), local_block(mom_v, "shift_b"))]
    sb_res = adamw_call(*sb_args, tm=8, name="adamw_shift_b")
    out_g["shift_b"] = g_sb
    for tree, res in zip((out_d, out_m, out_v), sb_res):
        tree["shift_b"] = _unpack_rows(res, [SHIFT_SHARD])[0]

    def block_of(tree, n):
        return tree[n].reshape(given[n].shape)

    return (loss, grad_x[None], *[block_of(out_g, n) for n in WEIGHTS], *[block_of(out_d, n) for n in WEIGHTS],
            *[block_of(out_m, n) for n in WEIGHTS], *[block_of(out_v, n) for n in WEIGHTS])
```

```python
import functools

import jax
import jax.numpy as jnp
from jax import lax
from jax.experimental import pallas as pl
from jax.experimental.pallas import tpu as pltpu

F32 = jnp.float32
BF16 = jnp.bfloat16

D_MODEL = 1024
N_HEADS = 16
HEAD = 64
SCAN_CHUNK = 64

VMEM_LIMIT = 56 * 1024 * 1024


_BDIMS = {
    "nn": (((2,), (1,)), ((0,), (0,))),
    "nt": (((2,), (2,)), ((0,), (0,))),
    "tn": (((1,), (1,)), ((0,), (0,))),
}


def _raw_bdot(x, y, mode, fine):
    if fine:
        return lax.dot_general(x, y, _BDIMS[mode], precision=lax.Precision.HIGH, preferred_element_type=F32)
    return lax.dot_general(x.astype(BF16), y.astype(BF16), _BDIMS[mode], preferred_element_type=F32)


@functools.partial(jax.custom_vjp, nondiff_argnums=(2, 3))
def bdot(x, y, mode, fine=True):
    return _raw_bdot(x, y, mode, fine)


def _bdot_fwd(x, y, mode, fine):
    return _raw_bdot(x, y, mode, fine), (x, y)


def _bdot_bwd(mode, fine, res, g):
    x, y = res
    if mode == "nn":
        return bdot(g, y, "nt", fine), bdot(x, g, "tn", fine)
    if mode == "nt":
        return bdot(g, y, "nn", fine), bdot(g, x, "tn", fine)
    return bdot(y, g, "nt", fine), bdot(x, g, "nn", fine)


bdot.defvjp(_bdot_fwd, _bdot_bwd)


def _scan_chunk(S0, r, lw, k, v, a, b):
    nh, lc, _ = r.shape
    ti = lax.broadcasted_iota(jnp.int32, (lc, lc), 0)
    si = lax.broadcasted_iota(jnp.int32, (lc, lc), 1)
    incl = (si <= ti).astype(F32)
    strict = (si < ti).astype(F32)
    eye = (si == ti).astype(F32)
    cl = bdot(jnp.broadcast_to(incl, (nh, lc, lc)), lw, "nn")
    cl_last = cl[:, lc - 1:lc, :]
    g_last = jnp.exp(cl_last - cl)
    at = a * jnp.exp(cl - lw)
    bt = b * jnp.exp(-cl)
    kt = k * jnp.exp(-cl)
    rt = r * jnp.exp(cl)
    ar = jnp.concatenate([at, rt], axis=1)
    ar_b = bdot(ar, bt, "nt")
    ar_k = bdot(ar, kt, "nt")
    m_ab, m_rb = ar_b[:, :lc] * strict, ar_b[:, lc:] * incl
    m_ak, m_rk = ar_k[:, :lc] * strict, ar_k[:, lc:] * incl
    x = eye + m_ab
    p = bdot(m_ab, m_ab, "nn", False)
    n = 2
    while n * 2 < lc:
        px = bdot(jnp.concatenate([p, x], axis=1), p, "nn", False)
        p = px[:, :lc]
        x = x + px[:, lc:]
        n *= 2
    x = x + bdot(x, p, "nn", False)
    ar_s = bdot(ar, S0, "nt", False)
    akrk_v = bdot(jnp.concatenate([m_ak, m_rk], axis=1), v, "nn")
    u = bdot(x, ar_s[:, :lc] + akrk_v[:, :lc], "nn", False)
    o = ar_s[:, lc:] + bdot(m_rb, u, "nn", False) + akrk_v[:, lc:]
    s_last = S0 * jnp.exp(cl_last) + bdot(jnp.concatenate([u, v], axis=1),
                                          jnp.concatenate([b * g_last, k * g_last], axis=1), "tn", False)
    return o, s_last


def _split_heads(z):
    return jnp.stack([z[:, HEAD * h:HEAD * (h + 1)] for h in range(N_HEADS)], axis=0)


def _merge_heads(z):
    return jnp.concatenate([z[h] for h in range(N_HEADS)], axis=1)


def _scan_specs(t, ops, rev):
    nc = t // SCAN_CHUNK
    row = (lambda c: nc - 1 - c) if rev else (lambda c: c)
    specs = [pl.BlockSpec((SCAN_CHUNK, D_MODEL), lambda c, cb=cb: (row(c), cb)) for _, cb in ops]
    state = pl.BlockSpec((1, N_HEADS, HEAD, HEAD), lambda c: (row(c), 0, 0, 0))
    return nc, specs, state


def scan_fwd(ops, pack):
    t = ops[0][0].shape[0]
    nc, specs, state = _scan_specs(t, ops, False)

    def body(r_ref, lw_ref, k_ref, v_ref, a_ref, b_ref, pack_ref, o_ref, s0_ref, all_ref, s_scr, send_sems, recv_sems):
        step = pl.program_id(0)
        x, y, c = _coords()
        me = 2 * x + y
        sib = (x, y, 1 - c)
        chips = _other_chips(x, y)
        first = [_remote(pack_ref.at[c], all_ref.at[me, c], send_sems, recv_sems, k, (cx, cy, c))
                 for k, (cx, cy) in enumerate(chips)]
        passed = [_remote(all_ref.at[2 * cx + cy, c], all_ref.at[2 * cx + cy, c], send_sems, recv_sems, 3 + k, sib)
                  for k, (cx, cy) in enumerate(chips)]

        @pl.when(step == 0)
        def _():
            s_scr[...] = jnp.zeros_like(s_scr)
            for cp in first:
                cp.start()

        s0 = s_scr[...]
        s0_ref[0] = s0
        o, s_last = _scan_chunk(s0, *[_split_heads(z[...]) for z in (r_ref, lw_ref, k_ref, v_ref, a_ref, b_ref)])
        o_ref[...] = _merge_heads(o)
        s_scr[...] = s_last

        @pl.when(step == nc - 1)
        def _():
            for k, (cx, cy) in enumerate(chips):
                j = 2 * cx + cy
                _remote(pack_ref.at[c], all_ref.at[j, c], send_sems, recv_sems, k, (cx, cy, c)).wait_recv()
                passed[k].start()
            for k, (cx, cy) in enumerate(chips):
                j = 2 * cx + cy
                _remote(all_ref.at[j, 1 - c], all_ref.at[j, 1 - c], send_sems, recv_sems, 3 + k, sib).wait_recv()
            for cp in first + passed:
                cp.wait_send()

    return pl.pallas_call(
        body,
        name="scan_fwd",
        grid=(nc,),
        in_specs=specs + [_ANY],
        out_specs=[pl.BlockSpec((SCAN_CHUNK, D_MODEL), lambda c: (c, 0)), state, _ANY],
        out_shape=[jax.ShapeDtypeStruct((t, D_MODEL), F32), jax.ShapeDtypeStruct((nc, N_HEADS, HEAD, HEAD), F32),
                   jax.ShapeDtypeStruct((N_CHIPS,) + pack.shape, pack.dtype)],
        scratch_shapes=[pltpu.VMEM((N_HEADS, HEAD, HEAD), F32), pltpu.SemaphoreType.DMA((6,)),
                        pltpu.SemaphoreType.DMA((6,))],
        compiler_params=_cparams(1),
    )(*[a for a, _ in ops], pack)


def scan_bwd(ops, s0s, do, part):
    t = ops[0][0].shape[0]
    nc, specs, state = _scan_specs(t, ops + [(do, 0)], True)

    def body(r_ref, lw_ref, k_ref, v_ref, a_ref, b_ref, do_ref, s0_ref, part_ref, *rest):
        out_refs, slots_ref, ds_scr, send_sems, recv_sems = rest[:6], rest[6], rest[7], rest[8], rest[9]
        step = pl.program_id(0)
        x, y, c = _coords()
        me = 2 * x + y
        chips = _other_chips(x, y)
        sends = [_remote(part_ref.at[2 * cx + cy], slots_ref.at[me], send_sems, recv_sems, k, (cx, cy, c))
                 for k, (cx, cy) in enumerate(chips)]

        @pl.when(step == 0)
        def _():
            ds_scr[...] = jnp.zeros_like(ds_scr)
            for cp in sends:
                cp.start()

        _, vjp = jax.vjp(_scan_chunk, s0_ref[0],
                         *[_split_heads(z[...]) for z in (r_ref, lw_ref, k_ref, v_ref, a_ref, b_ref)])
        grads = vjp((_split_heads(do_ref[...]), ds_scr[...]))
        for o_ref, g in zip(out_refs, grads[1:]):
            o_ref[...] = _merge_heads(g)
        ds_scr[...] = grads[0]

        @pl.when(step == nc - 1)
        def _():
            for k, (cx, cy) in enumerate(chips):
                _remote(part_ref.at[me], slots_ref.at[2 * cx + cy], send_sems, recv_sems, k, (cx, cy, c)).wait_recv()
            for cp in sends:
                cp.wait_send()

    return pl.pallas_call(
        body,
        name="scan_bwd",
        grid=(nc,),
        in_specs=specs + [state, _ANY],
        out_specs=[pl.BlockSpec((SCAN_CHUNK, D_MODEL), lambda c: (nc - 1 - c, 0))] * 6 + [_ANY],
        out_shape=[jax.ShapeDtypeStruct((t, D_MODEL), F32)] * 6 + [jax.ShapeDtypeStruct(part.shape, part.dtype)],
        scratch_shapes=[pltpu.VMEM((N_HEADS, HEAD, HEAD), F32), pltpu.SemaphoreType.DMA((3,)),
                        pltpu.SemaphoreType.DMA((3,))],
        compiler_params=_cparams(1),
    )(*[a for a, _ in ops], do, s0s, part)


_MDIMS = {
    "nn": (((1,), (0,)), ((), ())),
    "nt": (((1,), (1,)), ((), ())),
    "tn": (((0,), (0,)), ((), ())),
}


def _raw_mdot(x, y, mode, exact):
    if exact:
        return lax.dot_general(x, y, _MDIMS[mode], precision=lax.Precision.HIGH, preferred_element_type=F32)
    return lax.dot_general(x.astype(BF16), y.astype(BF16), _MDIMS[mode], preferred_element_type=F32)


@functools.partial(jax.custom_vjp, nondiff_argnums=(2, 3))
def mdot(x, y, mode, exact):
    return _raw_mdot(x, y, mode, exact)


def _mdot_fwd(x, y, mode, exact):
    return _raw_mdot(x, y, mode, exact), (x, y)


def _mdot_bwd(mode, exact, res, g):
    x, y = res
    if mode == "nn":
        return mdot(g, y, "nt", exact), mdot(x, g, "tn", exact)
    if mode == "nt":
        return mdot(g, y, "nn", exact), mdot(g, x, "tn", exact)
    return mdot(y, g, "nt", exact), mdot(x, g, "nn", exact)


mdot.defvjp(_mdot_fwd, _mdot_bwd)


def _seg_ones():
    i = lax.broadcasted_iota(jnp.int32, (256, 256), 0) // HEAD
    j = lax.broadcasted_iota(jnp.int32, (256, 256), 1) // HEAD
    return (i == j).astype(BF16)


@jax.custom_vjp
def segsum(x):
    bd = _seg_ones()
    hi = x.astype(BF16)
    lo = (x - hi.astype(F32)).astype(BF16)
    cols = []
    for j in range(x.shape[1] // 256):
        sl = slice(256 * j, 256 * (j + 1))
        cols.append(jnp.dot(hi[:, sl], bd, preferred_element_type=F32)
                    + jnp.dot(lo[:, sl], bd, preferred_element_type=F32))
    return jnp.concatenate(cols, axis=1)


segsum.defvjp(lambda x: (segsum(x), None), lambda _, g: (segsum(g),))


NORM_EPS = 1e-6
LN_EPS = 1e-5
GN_EPS = 64e-5
SGU_CHUNK = 128
SGU_GROUPS = 8


def _rms(x, g):
    return x * lax.rsqrt(jnp.mean(x * x, axis=-1, keepdims=True) + NORM_EPS) * g


def f_norm_in(x, g):
    return _rms(x, g), x


def f_sgu(p, ln_w, ln_b, sw, sbt):
    tm = p.shape[0]
    z = 0.5 * p * (1.0 + lax.erf(p * 0.7071067811865476))
    u, v = z[:, :D_MODEL], z[:, D_MODEL:]
    mu = jnp.mean(v, axis=-1, keepdims=True)
    d = v - mu
    vn = d * lax.rsqrt(jnp.mean(d * d, axis=-1, keepdims=True) + LN_EPS) * ln_w + ln_b
    ii = lax.broadcasted_iota(jnp.int32, (SGU_CHUNK, SGU_CHUNK), 0)
    jj = lax.broadcasted_iota(jnp.int32, (SGU_CHUNK, SGU_CHUNK), 1)
    mask = (jj <= ii).astype(F32)
    gi = lax.broadcasted_iota(jnp.int32, (SGU_GROUPS, D_MODEL), 0)
    ci = lax.broadcasted_iota(jnp.int32, (SGU_GROUPS, D_MODEL), 1) // SGU_CHUNK
    bias = mdot(sbt, (gi == ci).astype(F32), "nn", True)
    rows = []
    for c in range(tm // SGU_CHUNK):
        cols = []
        for g in range(SGU_GROUPS):
            blk = vn[c * SGU_CHUNK:(c + 1) * SGU_CHUNK, g * SGU_CHUNK:(g + 1) * SGU_CHUNK]
            cols.append(mdot(sw[g] * mask, blk, "nn", False))
        rows.append(jnp.concatenate(cols, axis=1) + bias)
    return (u * jnp.concatenate(rows, axis=0),)


def _softplus(x):
    return jnp.maximum(x, 0.0) + jnp.log1p(jnp.exp(-jnp.abs(x)))


def f_pre(qr, qk, qv, ql, wl, w0, al, a0, gl, k_k, k_a):
    xw, xa, xg = ql[:, :128], ql[:, 128:256], ql[:, 256:512]
    wr = -_softplus(-(w0 + mdot(jnp.tanh(xw), wl, "nn", True))) - 0.5
    lw = -jnp.exp(wr)
    aa = jax.nn.sigmoid(a0 + mdot(xa, al, "nn", True))
    g = mdot(jax.nn.sigmoid(xg), gl, "nn", True)
    kkr = qk * k_k
    kk = kkr / jnp.maximum(jnp.sqrt(segsum(kkr * kkr)), 1e-12)
    kp = qk * (1.0 + (aa - 1.0) * k_a)
    return qr, lw, kp, qv, -kk, kk * aa, g, qr, kp, qv


def f_post(o, r, kp, v, g, lnw, lnb, rk):
    mu = segsum(o) * (1.0 / HEAD)
    d = o - mu
    gn = d * lax.rsqrt(segsum(d * d) * (1.0 / HEAD) + GN_EPS)
    return ((gn * lnw + lnb + segsum(r * kp * rk) * v) * g,)


def f_mix(ya, yb, ga, gb):
    return (jax.nn.sigmoid(ga) * ya + jax.nn.sigmoid(gb) * yb,)


def f_ffn_in(h1, g):
    return _rms(h1, g), h1


def f_final(h1, m3, tgt, g):
    y = _rms(h1 + m3, g)
    err = jnp.square(y - tgt)
    return 0.5 * jnp.sum(jnp.mean(err, axis=-1))


def _cparams(n_grid):
    return pltpu.CompilerParams(dimension_semantics=("arbitrary",) * n_grid, vmem_limit_bytes=VMEM_LIMIT)


def _tile_spec(tm, w, cb):
    return pl.BlockSpec((tm, w), lambda i: (i, cb))


def _const_spec(c):
    nd = c.ndim
    return pl.BlockSpec(c.shape, lambda i: (0,) * nd)


def ew_call(fn, tiled, consts, outs, *, tm, name):
    t = tiled[0][0].shape[0]
    n_t, n_c = len(tiled), len(consts)

    def body(*refs):
        tv = [r[...].astype(F32) for r in refs[:n_t]]
        cv = [r[...] for r in refs[n_t:n_t + n_c]]
        res = fn(*tv, *cv)
        for o_ref, val in zip(refs[n_t + n_c:], res):
            o_ref[...] = val.astype(o_ref.dtype)

    return pl.pallas_call(
        body,
        name=name,
        grid=(t // tm,),
        in_specs=[_tile_spec(tm, w, cb) for _, w, cb in tiled] + [_const_spec(c) for c in consts],
        out_specs=[_tile_spec(tm, w, 0) for w, _ in outs],
        out_shape=[jax.ShapeDtypeStruct((t, w), dt) for w, dt in outs],
        compiler_params=_cparams(1),
    )(*[a for a, _, _ in tiled], *consts)


def ew_vjp_call(fn, tiled, consts, cots, d_tiled, d_consts, *, tm, name):
    t = tiled[0][0].shape[0]
    n_t, n_c, n_g = len(tiled), len(consts), len(cots)
    dt_list = [(i, dt) for i, dts in enumerate(d_tiled) for dt in dts]
    dc_list = [i for i, want in enumerate(d_consts) if want]

    def body(*refs):
        tv = [r[...].astype(F32) for r in refs[:n_t]]
        cv = [r[...] for r in refs[n_t:n_t + n_c]]
        gv = tuple(r[...].astype(F32) for r in refs[n_t + n_c:n_t + n_c + n_g])
        out_refs = refs[n_t + n_c + n_g:]
        _, vjp = jax.vjp(fn, *tv, *cv)
        grads = vjp(gv)
        for o_ref, (i, _) in zip(out_refs, dt_list):
            o_ref[...] = grads[i].astype(o_ref.dtype)
        acc_refs = out_refs[len(dt_list):]

        @pl.when(pl.program_id(0) == 0)
        def _():
            for a_ref in acc_refs:
                a_ref[...] = jnp.zeros_like(a_ref)

        for a_ref, i in zip(acc_refs, dc_list):
            a_ref[...] += grads[n_t + i]

    res = pl.pallas_call(
        body,
        name=name,
        grid=(t // tm,),
        in_specs=[_tile_spec(tm, w, cb) for _, w, cb in tiled] + [_const_spec(c) for c in consts]
        + [_tile_spec(tm, w, cb) for _, w, cb in cots],
        out_specs=[_tile_spec(tm, tiled[i][1], 0) for i, _ in dt_list] + [_const_spec(consts[i]) for i in dc_list],
        out_shape=[jax.ShapeDtypeStruct((t, tiled[i][1]), dt) for i, dt in dt_list]
        + [jax.ShapeDtypeStruct(consts[i].shape, F32) for i in dc_list],
        compiler_params=_cparams(1),
    )(*[a for a, _, _ in tiled], *consts, *[a for a, _, _ in cots])
    return res[:len(dt_list)], res[len(dt_list):]


def mm(a, b, mode, *, tm, tn, name, out_dtypes=(F32,), epi=None, extras=(), into=None):
    m = a.shape[1] if mode == "tn" else a.shape[0]
    kd = a.shape[0] if mode == "tn" else a.shape[1]
    n = b.shape[0] if mode == "nt" else b.shape[1]
    tm, tn = min(tm, m), min(tn, n)
    if mode == "nn":
        a_spec = pl.BlockSpec((tm, kd), lambda i, j: (i, 0))
        b_spec = pl.BlockSpec((kd, tn), lambda i, j: (0, j))
    elif mode == "nt":
        a_spec = pl.BlockSpec((tm, kd), lambda i, j: (i, 0))
        b_spec = pl.BlockSpec((tn, kd), lambda i, j: (j, 0))
    else:
        a_spec = pl.BlockSpec((kd, tm), lambda i, j: (0, i))
        b_spec = pl.BlockSpec((kd, tn), lambda i, j: (0, j))
    n_e = len(extras)
    o_spec = pl.BlockSpec((tm, tn), lambda i, j: (i, j))

    if into is not None:
        buf, place = into

        def body_into(a_ref, b_ref, buf_ref, o_ref):
            o_ref[0, 0] = lax.dot_general(a_ref[...].astype(BF16), b_ref[...].astype(BF16), _MDIMS[mode],
                                          preferred_element_type=F32)

        return pl.pallas_call(
            body_into,
            name=name,
            grid=(m // tm, n // tn),
            in_specs=[a_spec, b_spec, pl.BlockSpec(memory_space=pl.ANY)],
            out_specs=pl.BlockSpec((1, 1, tm, tn), lambda i, j: (*place(i, j), 0)),
            out_shape=jax.ShapeDtypeStruct(buf.shape, F32),
            input_output_aliases={2: 0},
            compiler_params=_cparams(2),
        )(a, b, buf)

    def body(a_ref, b_ref, *refs):
        c = lax.dot_general(a_ref[...].astype(BF16), b_ref[...].astype(BF16), _MDIMS[mode],
                            preferred_element_type=F32)
        res = epi(c, *[r[...] for r in refs[:n_e]]) if epi is not None else (c,)
        for o_ref, val in zip(refs[n_e:], res):
            o_ref[...] = val.astype(o_ref.dtype)

    res = pl.pallas_call(
        body,
        name=name,
        grid=(m // tm, n // tn),
        in_specs=[a_spec, b_spec] + [o_spec] * n_e,
        out_specs=[o_spec] * len(out_dtypes),
        out_shape=[jax.ShapeDtypeStruct((m, n), dt) for dt in out_dtypes],
        compiler_params=_cparams(2),
    )(a, b, *extras)
    return res if len(out_dtypes) > 1 else res[0]


P_WIDTH = 7680
RWKV_COL0 = 4096
RWKV_WIDTH = 3584
SHIFT_BLK = 512


def _shift_down(p, prev_row):
    rows = lax.broadcasted_iota(jnp.int32, p.shape, 0)
    return jnp.where(rows == 0, prev_row, pltpu.roll(p, 1, 0))


def shiftmix_fwd(p_all, sbp, *, tm):
    t = p_all.shape[0]
    tm = min(tm, t)
    c0 = RWKV_COL0 // SHIFT_BLK
    hb = tm // 8

    def body(p_ref, halo_ref, sb_ref, q_ref):
        p = p_ref[...]
        prev = jnp.where(pl.program_id(0) == 0, 0.0, halo_ref[7:8, :])
        q_ref[...] = p * sb_ref[0:1, :] + _shift_down(p, prev) * sb_ref[1:2, :]

    return pl.pallas_call(
        body,
        name="shiftmix_fwd",
        grid=(t // tm, RWKV_WIDTH // SHIFT_BLK),
        in_specs=[
            pl.BlockSpec((tm, SHIFT_BLK), lambda i, j: (i, c0 + j)),
            pl.BlockSpec((8, SHIFT_BLK), lambda i, j: (jnp.maximum(i * hb - 1, 0), c0 + j)),
            pl.BlockSpec((2, SHIFT_BLK), lambda i, j: (0, j)),
        ],
        out_specs=pl.BlockSpec((tm, SHIFT_BLK), lambda i, j: (i, j)),
        out_shape=jax.ShapeDtypeStruct((t, RWKV_WIDTH), F32),
        compiler_params=_cparams(2),
    )(p_all, p_all, sbp)


def shiftmix_bwd(dq, col0, p_all, sbp, *, tm, name):
    t, w = dq.shape
    n_i = t // tm
    hb = tm // 8
    cq = col0 // SHIFT_BLK
    cp = (RWKV_COL0 + col0) // SHIFT_BLK

    def body(dq_ref, dqn_ref, p_ref, ph_ref, sb_ref, dp_ref, dsb_ref):
        i = pl.program_id(1)
        dq_t = dq_ref[...]
        rows = lax.broadcasted_iota(jnp.int32, dq_t.shape, 0)
        nxt = jnp.where(i == n_i - 1, 0.0, dqn_ref[0:1, :])
        up = jnp.where(rows == tm - 1, nxt, pltpu.roll(dq_t, tm - 1, 0))
        dp_ref[...] = (dq_t * sb_ref[0:1, :] + up * sb_ref[1:2, :]).astype(dp_ref.dtype)
        p = p_ref[...]
        prev = jnp.where(i == 0, 0.0, ph_ref[7:8, :])
        s0 = jnp.sum(dq_t * p, axis=0, keepdims=True)
        s1 = jnp.sum(dq_t * _shift_down(p, prev), axis=0, keepdims=True)
        two = lax.broadcasted_iota(jnp.int32, (2, SHIFT_BLK), 0)

        @pl.when(i == 0)
        def _():
            dsb_ref[...] = jnp.zeros_like(dsb_ref)

        dsb_ref[...] += jnp.where(two == 0, s0, s1)

    return pl.pallas_call(
        body,
        name=name,
        grid=(w // SHIFT_BLK, n_i),
        in_specs=[
            pl.BlockSpec((tm, SHIFT_BLK), lambda j, i: (i, j)),
            pl.BlockSpec((8, SHIFT_BLK), lambda j, i: (jnp.minimum((i + 1) * hb, t // 8 - 1), j)),
            pl.BlockSpec((tm, SHIFT_BLK), lambda j, i: (i, cp + j)),
            pl.BlockSpec((8, SHIFT_BLK), lambda j, i: (jnp.maximum(i * hb - 1, 0), cp + j)),
            pl.BlockSpec((2, SHIFT_BLK), lambda j, i: (0, cq + j)),
        ],
        out_specs=[
            pl.BlockSpec((tm, SHIFT_BLK), lambda j, i: (i, j)),
            pl.BlockSpec((2, SHIFT_BLK), lambda j, i: (0, j)),
        ],
        out_shape=[jax.ShapeDtypeStruct((t, w), BF16), jax.ShapeDtypeStruct((2, w), F32)],
        compiler_params=_cparams(2),
    )(dq, dq, p_all, p_all, sbp)


def final_call(h1, m3, tgt, g_final, *, tm):
    t = h1.shape[0]

    def body(h1_ref, m3_ref, tgt_ref, g_ref, dh_ref, dhb_ref, dg_ref, loss_ref):
        loss, vjp = jax.vjp(f_final, h1_ref[...], m3_ref[...], tgt_ref[...], g_ref[...])
        dh, _, _, dg = vjp(jnp.ones((), F32))
        dh_ref[...] = dh
        dhb_ref[...] = dh.astype(BF16)

        @pl.when(pl.program_id(0) == 0)
        def _():
            dg_ref[...] = jnp.zeros_like(dg_ref)
            loss_ref[...] = jnp.zeros_like(loss_ref)

        dg_ref[...] += dg
        loss_ref[...] += jnp.full(loss_ref.shape, loss, F32)

    tile = _tile_spec(tm, D_MODEL, 0)
    return pl.pallas_call(
        body,
        name="final_loss",
        grid=(t // tm,),
        in_specs=[tile, tile, tile, _const_spec(g_final)],
        out_specs=[tile, tile, _const_spec(g_final), pl.BlockSpec((8, 128), lambda i: (0, 0))],
        out_shape=[jax.ShapeDtypeStruct((t, D_MODEL), F32), jax.ShapeDtypeStruct((t, D_MODEL), BF16),
                   jax.ShapeDtypeStruct(g_final.shape, F32), jax.ShapeDtypeStruct((8, 128), F32)],
        compiler_params=_cparams(1),
    )(h1, m3, tgt, g_final)


N_SGU = 2048
N_RWKV = 3360
LORA_W, LORA_A, LORA_G = 64, 64, 160


def _pad_rwkv_cols(z):
    zero = lambda n: jnp.zeros(z.shape[:-1] + (n,), z.dtype)
    return jnp.concatenate([z[..., :3072], z[..., 3072:3136], zero(64), z[..., 3136:3200], zero(64),
                            z[..., 3200:3360], zero(96)], axis=-1)


def _unpad_rwkv_cols(z):
    return jnp.concatenate([z[..., :3072], z[..., 3072:3136], z[..., 3200:3264], z[..., 3328:3488]], axis=-1)


def _pad_win_rows(wt):
    z = wt[N_SGU:N_SGU + N_RWKV]
    zero = lambda n: jnp.zeros((n, wt.shape[1]), wt.dtype)
    return jnp.concatenate([wt[:N_SGU], wt[N_SGU + N_RWKV:], z[:3072], z[3072:3136], zero(64), z[3136:3200], zero(64),
                            z[3200:3360], zero(96)], axis=0)


def _unpad_win_rows(wt):
    z = wt[RWKV_COL0:]
    return jnp.concatenate([wt[:N_SGU], z[:3072], z[3072:3136], z[3200:3264], z[3328:3488], wt[N_SGU:RWKV_COL0]],
                           axis=0)


def _pad_rows(w, n):
    return jnp.concatenate([w, jnp.zeros((n - w.shape[0],) + w.shape[1:], w.dtype)], axis=0)


def _relu2_epi(c):
    return c, jnp.square(jnp.maximum(c, 0.0))


def _relu2_bwd_epi(c, hid):
    return (c * (2.0 * jnp.maximum(hid, 0.0)),)


def _add_epi(c, x):
    return (c + x,)


def _pre_fwd(*args):
    res = f_pre(*args)
    return res[1], res[2], res[4], res[5], res[6]


def local_step(x, tgt, w, late_pack, late_weights, late_partials):
    d = D_MODEL
    win_pt = _pad_win_rows(w["w_in"])
    sbp = _pad_rwkv_cols(w["shift_b"])
    wl = _pad_rows(w["w_lora_w"], 128)
    al = _pad_rows(w["a_lora_w"], 128)
    gl = _pad_rows(w["g_lora_w"], 256)
    sbt = w["sgu_b"].T

    (a_bf,) = ew_call(lambda x_, g_: (f_norm_in(x_, g_)[0],), [(x, d, 0)], [w["g_mix"]], [(d, BF16)], tm=256,
                      name="norm_in")
    p_all = mm(a_bf, win_pt, "nt", tm=2048, tn=640, name="mm_in")
    sgu_t = [(p_all, 2 * d, 0)]
    sgu_c = [w["sgu_ln_w"], w["sgu_ln_b"], w["sgu_w"], sbt]
    (s_bf,) = ew_call(f_sgu, sgu_t, sgu_c, [(d, BF16)], tm=256, name="sgu_fwd")
    ya = mm(s_bf, w["w_proj_a"], "nn", tm=512, tn=1024, name="mm_proj_a")
    q = shiftmix_fwd(p_all, sbp, tm=1024)
    pre_t = [(q, d, 0), (q, d, 1), (q, d, 2), (q, 512, 6)]
    pre_c = [wl, w["w0"], al, w["a0"], gl, w["k_k"], w["k_a"]]
    lw, kp, na, nb, g = ew_call(_pre_fwd, pre_t, pre_c, [(d, F32)] * 5, tm=256, name="rwkv_pre_fwd")
    scan_ops = [(q, 0), (lw, 0), (kp, 0), (q, 2), (na, 0), (nb, 0)]
    o, s0s, late_all = scan_fwd(scan_ops, late_pack)
    w = {**w, **late_weights(late_all)}
    post_t = [(o, d, 0), (q, d, 0), (kp, d, 0), (q, d, 2), (g, d, 0)]
    post_c = [w["ln_x_w"], w["ln_x_b"], w["r_k"]]
    (ob_bf,) = ew_call(f_post, post_t, post_c, [(d, BF16)], tm=256, name="rwkv_post_fwd")
    yb = mm(ob_bf, w["w_proj_b"], "nn", tm=512, tn=1024, name="mm_proj_b")
    mix_t = [(ya, d, 0), (yb, d, 0), (p_all, d, 2), (p_all, d, 3)]
    (mixed_bf,) = ew_call(f_mix, mix_t, [], [(d, BF16)], tm=256, name="mix_fwd")
    h1 = mm(mixed_bf, w["w_out"], "nn", tm=512, tn=1024, name="mm_out", epi=_add_epi, extras=(x,))
    (f_bf,) = ew_call(lambda h_, g_: (f_ffn_in(h_, g_)[0],), [(h1, d, 0)], [w["g_ffn"]], [(d, BF16)], tm=256,
                      name="ffn_norm")
    hid, act_bf = mm(f_bf, w["w_ffn1"], "nn", tm=2048, tn=1024, name="mm_ffn1", out_dtypes=(F32, BF16), epi=_relu2_epi)
    m3 = mm(act_bf, w["w_ffn2"], "nn", tm=1024, tn=512, name="mm_ffn2")
    dh2, dh2_bf, dg_final, loss = final_call(h1, m3, tgt, w["g_final"], tm=256)

    dhid_bf = mm(dh2_bf, w["w_ffn2"], "nt", tm=2048, tn=1024, name="mm_dact", out_dtypes=(BF16,), epi=_relu2_bwd_epi,
                 extras=(hid,))
    late_g = lax.empty((N_CHIPS, 2, PACK_ROWS, HALF_W), F32)
    late_g = mm(act_bf, dh2_bf, "tn", tm=512, tn=HALF_W, name="mm_dw_ffn2",
                into=(late_g, lambda i, j: (i // 2, j, PIECE_OFF["w_ffn2"] // 512 + i % 2)))
    df = mm(dhid_bf, w["w_ffn1"], "nt", tm=1024, tn=512, name="mm_df")
    late_g = mm(f_bf, dhid_bf, "tn", tm=512, tn=HALF_W, name="mm_dw_ffn1",
                into=(late_g, lambda i, j: (j // 2, j % 2, PIECE_OFF["w_ffn1"] // 512 + i)))
    (dh1, dh1_bf), (dg_ffn,) = ew_vjp_call(f_ffn_in, [(h1, d, 0)], [w["g_ffn"]], [(df, d, 0), (dh2, d, 0)],
                                           [(F32, BF16)], [True], tm=256, name="ffn_norm_bwd")
    dmixed = mm(dh1_bf, w["w_out"], "nt", tm=512, tn=1024, name="mm_dmixed")
    late_g = mm(mixed_bf, dh1_bf, "tn", tm=256, tn=HALF_W, name="mm_dw_out",
                into=(late_g, lambda i, j: (i, j, PIECE_OFF["w_out"] // 256)))
    (dya_bf, dyb_bf, dga_bf, dgb_bf), _ = ew_vjp_call(f_mix, mix_t, [], [(dmixed, d, 0)], [(BF16,)] * 4, [], tm=256,
                                                      name="mix_bwd")
    dob = mm(dyb_bf, w["w_proj_b"], "nt", tm=512, tn=1024, name="mm_dob")
    late_g = mm(ob_bf, dyb_bf, "tn", tm=256, tn=HALF_W, name="mm_dw_proj_b",
                into=(late_g, lambda i, j: (i, j, PIECE_OFF["w_proj_b"] // 256)))
    (do, dr_p, dkp_p, dv_p, dg), (dlnx_w, dlnx_b, dr_k) = ew_vjp_call(
        f_post, post_t, post_c, [(dob, d, 0)], [(F32,)] * 5, [True] * 3, tm=256, name="rwkv_post_bwd")
    late_part, late_part16 = late_partials(late_g)
    *scan_g, late_slots = scan_bwd(scan_ops, s0s, do, late_part16)
    pre_g = [(z, d, 0) for z in scan_g] + [(dg, d, 0), (dr_p, d, 0), (dkp_p, d, 0), (dv_p, d, 0)]
    (dq_r, dq_k, dq_v, dq_l), (dwl, dw0, dal, da0, dgl, dk_k, dk_a) = ew_vjp_call(
        f_pre, pre_t, pre_c, pre_g, [(F32,)] * 4, [True] * 7, tm=128, name="rwkv_pre_bwd")
    dp_r, dsb_r = shiftmix_bwd(dq_r, 0, p_all, sbp, tm=256, name="shiftmix_bwd_r")
    dp_k, dsb_k = shiftmix_bwd(dq_k, d, p_all, sbp, tm=256, name="shiftmix_bwd_k")
    dp_v, dsb_v = shiftmix_bwd(dq_v, 2 * d, p_all, sbp, tm=256, name="shiftmix_bwd_v")
    dp_l, dsb_l = shiftmix_bwd(dq_l, 3 * d, p_all, sbp, tm=256, name="shiftmix_bwd_l")
    ds = mm(dya_bf, w["w_proj_a"], "nt", tm=512, tn=1024, name="mm_ds")
    d_proj_a = mm(s_bf, dya_bf, "tn", tm=512, tn=1024, name="mm_dw_proj_a")
    (dp_sgu,), (dln_w, dln_b, dsw, dsbt) = ew_vjp_call(f_sgu, sgu_t, sgu_c, [(ds, d, 0)], [(BF16,)], [True] * 4,
                                                       tm=256, name="sgu_bwd")
    dp_all = jnp.concatenate([dp_sgu, dga_bf, dgb_bf, dp_r, dp_k, dp_v, dp_l], axis=1)
    da = mm(dp_all, win_pt, "nn", tm=1024, tn=256, name="mm_da")
    d_in_pt = mm(dp_all, a_bf, "tn", tm=1280, tn=1024, name="mm_dw_in")
    (grad_x,), (dg_mix,) = ew_vjp_call(f_norm_in, [(x, d, 0)], [w["g_mix"]], [(da, d, 0), (dh1, d, 0)], [(F32,)],
                                       [True], tm=256, name="norm_in_bwd")

    grads = {
        "g_mix": dg_mix, "w_in": _unpad_win_rows(d_in_pt), "sgu_ln_w": dln_w, "sgu_ln_b": dln_b, "sgu_w": dsw,
        "sgu_b": dsbt.T, "w_proj_a": d_proj_a,
        "shift_b": _unpad_rwkv_cols(jnp.concatenate([dsb_r, dsb_k, dsb_v, dsb_l], axis=1)),
        "w_lora_w": dwl[:LORA_W], "w0": dw0, "a_lora_w": dal[:LORA_A], "a0": da0, "g_lora_w": dgl[:LORA_G],
        "k_k": dk_k, "k_a": dk_a, "r_k": dr_k, "ln_x_w": dlnx_w, "ln_x_b": dlnx_b, "g_ffn": dg_ffn,
        "g_final": dg_final,
    }
    return loss[0, 0], grad_x, grads, (late_part, late_slots)


MESH = pl.DeviceIdType.MESH
N_CHIPS = 4
N_DEV = 8
PACK_ROWS = 2560
PACK_TILE = 512
SMALL_ROWS = 152
_ANY = pl.BlockSpec(memory_space=pl.ANY)


def _coords():
    return lax.axis_index("x"), lax.axis_index("y"), lax.axis_index("c")


def _other_chips(x, y):
    return [(1 - x, y), (x, 1 - y), (1 - x, 1 - y)]


def _remote(src, dst, send_sems, recv_sems, k, to):
    return pltpu.make_async_remote_copy(src_ref=src, dst_ref=dst, send_sem=send_sems.at[k], recv_sem=recv_sems.at[k],
                                        device_id=to, device_id_type=MESH)


def gather_shards(pack):
    def body(src_ref, out_ref, send_sems, recv_sems):
        x, y, c = _coords()
        me = 2 * x + y
        sib = (x, y, 1 - c)
        chips = _other_chips(x, y)
        first = [_remote(src_ref.at[c], out_ref.at[me, c], send_sems, recv_sems, k, (cx, cy, c))
                 for k, (cx, cy) in enumerate(chips)]
        for cp in first:
            cp.start()
        passed = []
        for k, (cx, cy) in enumerate(chips):
            j = 2 * cx + cy
            _remote(src_ref.at[c], out_ref.at[j, c], send_sems, recv_sems, k, (cx, cy, c)).wait_recv()
            fwd = _remote(out_ref.at[j, c], out_ref.at[j, c], send_sems, recv_sems, 3 + k, sib)
            fwd.start()
            passed.append(fwd)
        for k, (cx, cy) in enumerate(chips):
            j = 2 * cx + cy
            _remote(out_ref.at[j, 1 - c], out_ref.at[j, 1 - c], send_sems, recv_sems, 3 + k, sib).wait_recv()
        for cp in first + passed:
            cp.wait_send()

    return pl.pallas_call(
        body,
        name="gather_shards",
        in_specs=[_ANY],
        out_specs=_ANY,
        out_shape=jax.ShapeDtypeStruct((N_CHIPS,) + pack.shape, pack.dtype),
        scratch_shapes=[pltpu.SemaphoreType.DMA((6,)), pltpu.SemaphoreType.DMA((6,))],
    )(pack)


def reduce_pair(g, tag):
    def body(g_ref, got_ref, send_sems, recv_sems):
        x, y, c = _coords()
        sib = (x, y, 1 - c)
        sends = [_remote(g_ref.at[j, 1 - c], got_ref.at[j], send_sems, recv_sems, j, sib) for j in range(N_CHIPS)]
        for cp in sends:
            cp.start()
        for cp in sends:
            cp.wait_recv()
        for cp in sends:
            cp.wait_send()

    return pl.pallas_call(
        body,
        name="reduce_pair_" + tag,
        in_specs=[_ANY],
        out_specs=_ANY,
        out_shape=jax.ShapeDtypeStruct((N_CHIPS,) + g.shape[2:], g.dtype),
        scratch_shapes=[pltpu.SemaphoreType.DMA((N_CHIPS,)), pltpu.SemaphoreType.DMA((N_CHIPS,))],
    )(g)


def pair_sum(g, got, tag, *, tm):
    n, _, rows, width = g.shape

    def body(g0_ref, g1_ref, got_ref, out_ref, out16_ref):
        own = jnp.where(lax.axis_index("c") == 0, g0_ref[0, 0], g1_ref[0, 0])
        total = own + got_ref[0]
        out_ref[0] = total
        out16_ref[0] = total.astype(BF16)

    blk = pl.BlockSpec((1, tm, width), lambda j, i: (j, i, 0))
    return pl.pallas_call(
        body,
        name="pair_sum_" + tag,
        grid=(n, rows // tm),
        in_specs=[pl.BlockSpec((1, 1, tm, width), lambda j, i: (j, 0, i, 0)),
                  pl.BlockSpec((1, 1, tm, width), lambda j, i: (j, 1, i, 0)), blk],
        out_specs=[blk, blk],
        out_shape=[jax.ShapeDtypeStruct(got.shape, F32), jax.ShapeDtypeStruct(got.shape, BF16)],
        compiler_params=_cparams(2),
    )(g, g, got)


def reduce_chips(p):
    def body(p_ref, out_ref, send_sems, recv_sems):
        x, y, c = _coords()
        me = 2 * x + y
        chips = _other_chips(x, y)
        sends = [_remote(p_ref.at[2 * cx + cy], out_ref.at[me], send_sems, recv_sems, k, (cx, cy, c))
                 for k, (cx, cy) in enumerate(chips)]
        for cp in sends:
            cp.start()
        for k, (cx, cy) in enumerate(chips):
            _remote(p_ref.at[me], out_ref.at[2 * cx + cy], send_sems, recv_sems, k, (cx, cy, c)).wait_recv()
        for cp in sends:
            cp.wait_send()

    return pl.pallas_call(
        body,
        name="reduce_chips",
        in_specs=[_ANY],
        out_specs=_ANY,
        out_shape=jax.ShapeDtypeStruct(p.shape, p.dtype),
        scratch_shapes=[pltpu.SemaphoreType.DMA((3,)), pltpu.SemaphoreType.DMA((3,))],
    )(p)


def sum_with_own(own, slots, index_fn, *, tm, name):
    n, rows, width = slots.shape
    own3 = own.ndim == 3

    def body(*refs):
        mine = index_fn()
        acc = None
        for s in range(n):
            o = refs[s][0] if own3 else refs[0][...]
            term = jnp.where(mine == s, o, refs[(n if own3 else 1) + s][0].astype(F32))
            acc = term if acc is None else acc + term
        refs[-1][...] = acc

    slot_specs = [pl.BlockSpec((1, tm, width), lambda i, s=s: (s, i, 0)) for s in range(n)]
    own_specs = slot_specs if own3 else [pl.BlockSpec((tm, width), lambda i: (i, 0))]
    return pl.pallas_call(
        body,
        name=name,
        grid=(rows // tm,),
        in_specs=own_specs + slot_specs,
        out_specs=pl.BlockSpec((tm, width), lambda i: (i, 0)),
        out_shape=jax.ShapeDtypeStruct((rows, width), F32),
        compiler_params=_cparams(1),
    )(*([own] * (n if own3 else 1)), *([slots] * n))


def exchange_halves(s, tag):
    rq = PACK_TILE
    nq = s.shape[0] // rq

    def body(s_ref, out_ref, sbuf, rbuf, send_sems, recv_sems, in_sems, out_sems):
        x, y, c = _coords()
        sib = (x, y, 1 - c)
        rows = lambda q: pl.ds(q * rq, rq)
        loads = [pltpu.make_async_copy(s_ref.at[rows(q)], sbuf.at[rows(q)], in_sems.at[q]) for q in range(nq)]
        for cp in loads:
            cp.start()
        sends = []
        for q in range(nq):
            loads[q].wait()
            sends.append(_remote(sbuf.at[rows(q)], rbuf.at[rows(q)], send_sems, recv_sems, q, sib))
            sends[q].start()
        stores = []
        for q in range(nq):
            sends[q].wait_recv()
            stores.append(pltpu.make_async_copy(rbuf.at[rows(q)], out_ref.at[rows(q)], out_sems.at[q]))
            stores[q].start()
        for cp in sends:
            cp.wait_send()
        for cp in stores:
            cp.wait()

    return pl.pallas_call(
        body,
        name="exchange_halves_" + tag,
        in_specs=[_ANY],
        out_specs=_ANY,
        out_shape=jax.ShapeDtypeStruct(s.shape, s.dtype),
        scratch_shapes=[pltpu.VMEM(s.shape, s.dtype), pltpu.VMEM(s.shape, s.dtype)]
        + [pltpu.SemaphoreType.DMA((nq,))] * 4,
        compiler_params=pltpu.CompilerParams(vmem_limit_bytes=VMEM_LIMIT),
    )(s)


def sum_all(s):
    def body(s_ref, out_ref, slots, mine, theirs, send_sems, recv_sems):
        x, y, c = _coords()
        me = 2 * x + y
        chips = _other_chips(x, y)
        sends = [_remote(s_ref, slots.at[me], send_sems, recv_sems, k, (cx, cy, c)) for k, (cx, cy) in enumerate(chips)]
        for cp in sends:
            cp.start()
        for k, (cx, cy) in enumerate(chips):
            _remote(s_ref, slots.at[2 * cx + cy], send_sems, recv_sems, k, (cx, cy, c)).wait_recv()
        slots[me] = s_ref[...]
        acc = ((slots[0] + slots[1]) + slots[2]) + slots[3]
        mine[...] = acc
        swap = _remote(mine, theirs, send_sems, recv_sems, 3, (x, y, 1 - c))
        swap.start()
        swap.wait_recv()
        out_ref[...] = acc + theirs[...]
        swap.wait_send()
        for cp in sends:
            cp.wait_send()

    vmem = pl.BlockSpec(memory_space=pltpu.VMEM)
    return pl.pallas_call(
        body,
        name="sum_all",
        in_specs=[vmem],
        out_specs=vmem,
        out_shape=jax.ShapeDtypeStruct(s.shape, s.dtype),
        scratch_shapes=[pltpu.VMEM((N_CHIPS,) + s.shape, s.dtype), pltpu.VMEM(s.shape, s.dtype),
                        pltpu.VMEM(s.shape, s.dtype), pltpu.SemaphoreType.DMA((4,)), pltpu.SemaphoreType.DMA((4,))],
        compiler_params=pltpu.CompilerParams(vmem_limit_bytes=VMEM_LIMIT),
    )(s)


ADAM_LR = 0.001
ADAM_B1 = 0.9
ADAM_B2 = 0.999
ADAM_EPS = 1e-08
ADAM_WD = 0.01
ADAM_STEP = 10


def f_adamw(g, w, m, v):
    m = ADAM_B1 * m + (1.0 - ADAM_B1) * g
    v = ADAM_B2 * v + (1.0 - ADAM_B2) * jnp.square(g)
    m_hat = m / (1.0 - ADAM_B1 ** ADAM_STEP)
    v_hat = v / (1.0 - ADAM_B2 ** ADAM_STEP)
    delta = -ADAM_LR * (m_hat / (jnp.sqrt(v_hat) + ADAM_EPS) + ADAM_WD * w)
    return delta, m, v


def adamw_call(g, w, m, v, *, tm, name):
    width = g.shape[1]
    return ew_call(f_adamw, [(g, width, 0), (w, width, 0), (m, width, 0), (v, width, 0)], [], [(width, F32)] * 3,
                   tm=tm, name=name)


def adamw_halves(g_own, g_other, w, m, v, *, tm):
    _, rows, width = w.shape

    def body(go_ref, gx_ref, w_ref, m_ref, v_ref, g_ref, d_ref, nm_ref, nv_ref):
        g = jnp.where(pl.program_id(0) == lax.axis_index("c"), go_ref[...], gx_ref[...])
        delta, nm, nv = f_adamw(g, w_ref[0], m_ref[0], v_ref[0])
        g_ref[0] = g
        d_ref[0] = delta
        nm_ref[0] = nm
        nv_ref[0] = nv

    half = pl.BlockSpec((tm, width), lambda h, i: (i, 0))
    full = pl.BlockSpec((1, tm, width), lambda h, i: (h, i, 0))
    return pl.pallas_call(
        body,
        name="adamw_sharded",
        grid=(2, rows // tm),
        in_specs=[half, half, full, full, full],
        out_specs=[full] * 4,
        out_shape=[jax.ShapeDtypeStruct(w.shape, F32)] * 4,
        compiler_params=_cparams(2),
    )(g_own, g_other, w, m, v)


EARLY = ["w_in", "w_proj_a", "w_lora_w", "a_lora_w", "g_lora_w"]
LATE = ["w_ffn1", "w_ffn2", "w_proj_b", "w_out"]
SHARDED = EARLY + LATE
LORAS = ["w_lora_w", "a_lora_w", "g_lora_w"]
HALF_W = 512
PIECE_ROWS = {"w_in": 1864, "w_ffn1": 1024, "w_ffn2": 1024, "w_proj_a": 256, "w_proj_b": 256, "w_out": 256,
              "w_lora_w": 32, "a_lora_w": 32, "g_lora_w": 80}
PIECE_OFF = {"w_in": 0, "w_proj_a": 1920, "w_lora_w": 2176, "a_lora_w": 2208, "g_lora_w": 2240,
             "w_ffn1": 0, "w_ffn2": 1024, "w_proj_b": 2048, "w_out": 2304}
LO_OFF = 2320
SHARD_AXIS = {"w_in": 1, "w_proj_a": 0, "w_lora_w": 1, "a_lora_w": 1, "g_lora_w": 1, "w_proj_b": 0, "w_out": 0,
              "w_ffn1": 1, "w_ffn2": 0}
SHARD_SHAPE = {"w_in": (1024, 1864), "w_proj_a": (256, 1024), "w_lora_w": (64, 256), "a_lora_w": (64, 256),
               "g_lora_w": (160, 256), "w_proj_b": (256, 1024), "w_out": (256, 1024), "w_ffn1": (1024, 1024),
               "w_ffn2": (1024, 1024)}
SHIFT_SHARD = (2, 840)
VECTORS = ["g_mix", "sgu_ln_w", "sgu_ln_b", "w0", "a0", "k_k", "k_a", "r_k", "ln_x_w", "ln_x_b", "g_ffn", "g_final"]
SMALL = VECTORS + ["sgu_w", "sgu_b"]
SMALL_SHAPE = {**{n: (1, 1024) for n in VECTORS}, "sgu_w": (8, 128, 128), "sgu_b": (8, 128)}
WEIGHTS = ["g_mix", "w_in", "sgu_ln_w", "sgu_ln_b", "sgu_w", "sgu_b", "w_proj_a", "shift_b", "w_lora_w", "w0",
           "a_lora_w", "a0", "g_lora_w", "k_k", "k_a", "r_k", "ln_x_w", "ln_x_b", "w_proj_b", "w_out", "g_ffn",
           "w_ffn1", "w_ffn2", "g_final"]


def _size(shape):
    n = 1
    for s in shape:
        n *= s
    return n


def _pack_rows(parts, rows, dtype):
    flat = jnp.concatenate([p.reshape(-1).astype(dtype) for p in parts])
    return jnp.concatenate([flat, jnp.zeros((rows * 1024 - flat.shape[0],), dtype)]).reshape(rows, 1024)


def _unpack_rows(packed, shapes):
    flat = packed.reshape(-1)
    out, off = [], 0
    for shp in shapes:
        out.append(flat[off:off + _size(shp)].reshape(shp))
        off += _size(shp)
    return out


def _shard_of(name, full, j):
    ax = SHARD_AXIS[name]
    n = SHARD_SHAPE[name][ax]
    return lax.slice_in_dim(full, j * n, (j + 1) * n, axis=ax)


def _pad_cols(z, n):
    return jnp.concatenate([z, jnp.zeros((z.shape[0], n - z.shape[1]), z.dtype)], axis=1)


def _row_form(name, s):
    return s.T if name == "w_in" else s


def _half_piece(name, rf, h):
    if name in LORAS:
        r = PIECE_ROWS[name]
        return _pad_cols(rf[h * r:(h + 1) * r], HALF_W)
    return rf[:, HALF_W * h:HALF_W * (h + 1)]


def _pack_half(group, rf_fn, h, dtype, tail=()):
    parts, pos, rows = [], 0, PACK_ROWS
    for n in group:
        if PIECE_OFF[n] > pos:
            parts.append(jnp.zeros((PIECE_OFF[n] - pos, HALF_W), dtype))
        parts.append(_half_piece(n, rf_fn(n), h).astype(dtype))
        pos = PIECE_OFF[n] + PIECE_ROWS[n]
    for t in tail:
        parts.append(t)
        pos += t.shape[0]
    parts.append(jnp.zeros((rows - pos, HALF_W), dtype))
    return jnp.concatenate(parts, axis=0)


def _piece(pack, name):
    return pack[PIECE_OFF[name]:PIECE_OFF[name] + PIECE_ROWS[name]]


def _join_halves(name, p0, p1):
    if name in LORAS:
        return jnp.concatenate([p0[:, :SHARD_SHAPE[name][1]], p1[:, :SHARD_SHAPE[name][1]]], axis=0)
    return jnp.concatenate([p0, p1], axis=1)


def _grad_row_form(name, full, j):
    if name == "w_in":
        return full[SHARD_SHAPE[name][1] * j:SHARD_SHAPE[name][1] * (j + 1)]
    return _shard_of(name, full, j)


def adamw_weight(name, g_own, g_other, w, m, v):
    rows, width = w.shape
    if name in LORAS:
        tm = PIECE_ROWS[name]
        grid = (2, 1)
        native = pl.BlockSpec((tm, width), lambda h, i: (h, 0))
    elif name == "w_in":
        tm, lanes = rows, 128
        grid = (2, HALF_W // lanes)
        native = pl.BlockSpec((tm, lanes), lambda h, i: (0, h * (HALF_W // lanes) + i))
    else:
        tm = 128
        grid = (2, rows // tm)
        native = pl.BlockSpec((tm, HALF_W), lambda h, i: (i, h))
    off = PIECE_OFF[name] // tm
    if name == "w_in":
        packed = pl.BlockSpec((tm, 128), lambda h, i: (0, i))
    else:
        packed = pl.BlockSpec((tm, HALF_W), lambda h, i: (off + i, 0))

    def body(go_ref, gx_ref, w_ref, m_ref, v_ref, g_ref, d_ref, nm_ref, nv_ref):
        g = jnp.where(pl.program_id(0) == lax.axis_index("c"), go_ref[...], gx_ref[...])[:, :w_ref.shape[1]]
        delta, nm, nv = f_adamw(g, w_ref[...], m_ref[...], v_ref[...])
        g_ref[...] = g
        d_ref[...] = delta
        nm_ref[...] = nm
        nv_ref[...] = nv

    return pl.pallas_call(
        body,
        name="adamw_" + name,
        grid=grid,
        in_specs=[packed, packed, native, native, native],
        out_specs=[native] * 4,
        out_shape=[jax.ShapeDtypeStruct(w.shape, F32)] * 4,
        compiler_params=_cparams(2),
    )(g_own, g_other, w, m, v)


def kernel(x, g_mix, w_in, sgu_ln_w, sgu_ln_b, sgu_w, sgu_b, w_proj_a, shift_b, w_lora_w, w0, a_lora_w, a0, g_lora_w, k_k, k_a, r_k, ln_x_w, ln_x_b, w_proj_b, w_out, g_ffn, w_ffn1, w_ffn2, g_final, loss_target, m_g_mix, m_w_in, m_sgu_ln_w, m_sgu_ln_b, m_sgu_w, m_sgu_b, m_w_proj_a, m_shift_b, m_w_lora_w, m_w0, m_a_lora_w, m_a0, m_g_lora_w, m_k_k, m_k_a, m_r_k, m_ln_x_w, m_ln_x_b, m_w_proj_b, m_w_out, m_g_ffn, m_w_ffn1, m_w_ffn2, m_g_final, v_g_mix, v_w_in, v_sgu_ln_w, v_sgu_ln_b, v_sgu_w, v_sgu_b, v_w_proj_a, v_shift_b, v_w_lora_w, v_w0, v_a_lora_w, v_a0, v_g_lora_w, v_k_k, v_k_a, v_r_k, v_ln_x_w, v_ln_x_b, v_w_proj_b, v_w_out, v_g_ffn, v_w_ffn1, v_w_ffn2, v_g_final):
    given = dict(zip(WEIGHTS, (g_mix, w_in, sgu_ln_w, sgu_ln_b, sgu_w, sgu_b, w_proj_a, shift_b, w_lora_w, w0, a_lora_w, a0, g_lora_w, k_k, k_a, r_k, ln_x_w, ln_x_b, w_proj_b, w_out, g_ffn, w_ffn1, w_ffn2, g_final)))
    mom_m = dict(zip(WEIGHTS, (m_g_mix, m_w_in, m_sgu_ln_w, m_sgu_ln_b, m_sgu_w, m_sgu_b, m_w_proj_a, m_shift_b, m_w_lora_w, m_w0, m_a_lora_w, m_a0, m_g_lora_w, m_k_k, m_k_a, m_r_k, m_ln_x_w, m_ln_x_b, m_w_proj_b, m_w_out, m_g_ffn, m_w_ffn1, m_w_ffn2, m_g_final)))
    mom_v = dict(zip(WEIGHTS, (v_g_mix, v_w_in, v_sgu_ln_w, v_sgu_ln_b, v_sgu_w, v_sgu_b, v_w_proj_a, v_shift_b, v_w_lora_w, v_w0, v_a_lora_w, v_a0, v_g_lora_w, v_k_k, v_k_a, v_r_k, v_ln_x_w, v_ln_x_b, v_w_proj_b, v_w_out, v_g_ffn, v_w_ffn1, v_w_ffn2, v_g_final)))
    chip = 2 * lax.axis_index("x") + lax.axis_index("y")

    def local_block(tree, n):
        return tree[n] if n == "g_final" else tree[n][0]

    sb = local_block(given, "shift_b")
    lo_part = lambda z: (z - z.astype(BF16).astype(F32)).astype(BF16)
    row_form = lambda tree: (lambda n: _row_form(n, local_block(tree, n)))
    tile16 = lambda z: jnp.pad(z, ((0, 16 - z.shape[0]), (0, HALF_W - z.shape[1])))
    sb_tiles = [tile16(f(sb[:, lanes])) for f in (lambda z: z.astype(BF16), lo_part)
                for lanes in (slice(0, HALF_W), slice(HALF_W, None))]
    tails = [[_half_piece(n, lo_part(local_block(given, n)), h) for n in LORAS] + sb_tiles for h in range(2)]
    pack_w = jnp.stack([_pack_half(EARLY, row_form(given), h, BF16, tails[h]) for h in range(2)])
    gathered = gather_shards(pack_w)
    gathered = lax.dynamic_update_index_in_dim(gathered, pack_w, chip, 0)
    pack_late = jnp.stack([_pack_half(LATE, row_form(given), h, BF16) for h in range(2)])

    def whole(group, got):
        shard = lambda n, j: _join_halves(n, _piece(got[j, 0], n), _piece(got[j, 1], n))
        return {n: jnp.concatenate([shard(n, j).astype(F32 if n == "w_in" else BF16) for j in range(N_CHIPS)],
                                   axis=0 if n == "w_in" else SHARD_AXIS[n]) for n in group}

    w = whole(EARLY, gathered)
    late_weights = lambda got: whole(LATE, lax.dynamic_update_index_in_dim(got, pack_late, chip, 0))
    off = LO_OFF
    for n in LORAS:
        r, cols = PIECE_ROWS[n], SHARD_SHAPE[n][1]
        lo = jnp.concatenate([jnp.concatenate([gathered[j, 0, off:off + r, :cols], gathered[j, 1, off:off + r, :cols]],
                                              axis=0) for j in range(N_CHIPS)], axis=1)
        w[n] = w[n].astype(F32) + lo.astype(F32)
        off += r
    sb_tile = lambda j, t, lanes: gathered[j, 0, off + 16 * t:off + 16 * t + 2, :lanes].astype(F32)
    rest = SHIFT_SHARD[1] - HALF_W
    w["shift_b"] = jnp.concatenate(
        [jnp.concatenate([sb_tile(j, 0, HALF_W) + sb_tile(j, 2, HALF_W), sb_tile(j, 1, rest) + sb_tile(j, 3, rest)],
                         axis=1) for j in range(N_CHIPS)], axis=1)
    for n in SMALL:
        w[n] = local_block(given, n).reshape(SMALL_SHAPE[n])

    def partials(g_pack, tag):
        return pair_sum(g_pack, reduce_pair(g_pack, tag), tag, tm=PACK_TILE)

    loss, grad_x, grads, (late_part, late_slots) = local_step(
        x[0], loss_target[0], w, pack_late, late_weights, lambda g_pack: partials(g_pack, "late"))
    loss = lax.psum(loss, ("x", "y", "c"))

    early_g = jnp.stack([jnp.stack([_pack_half(EARLY, lambda n: _grad_row_form(n, grads[n], j), h, F32)
                                    for h in range(2)]) for j in range(N_CHIPS)])
    early_part, early_part16 = partials(early_g, "early")
    my_chip = lambda: 2 * lax.axis_index("x") + lax.axis_index("y")
    out_g, out_d, out_m, out_v = {}, {}, {}, {}
    for group, tag, part, slots in ((LATE, "late", late_part, late_slots),
                                    (EARLY, "early", early_part, reduce_chips(early_part16))):
        half_sum = sum_with_own(part, slots, my_chip, tm=PACK_TILE, name="chip_sum_" + tag)
        other_half = exchange_halves(half_sum, tag)
        for n in group:
            res = adamw_weight(n, half_sum, other_half,
                               *[_row_form(n, local_block(t, n)) for t in (given, mom_m, mom_v)])
            for tree, z in zip((out_g, out_d, out_m, out_v), res):
                tree[n] = _row_form(n, z)

    small_shapes = [SMALL_SHAPE[n] for n in SMALL]
    s_pack = _pack_rows([grads[n] for n in SMALL] + [grads["shift_b"]], SMALL_ROWS, F32)
    g_small = sum_all(s_pack)
    w_small = _pack_rows([local_block(given, n) for n in SMALL], SMALL_ROWS, F32)
    m_small = _pack_rows([local_block(mom_m, n) for n in SMALL], SMALL_ROWS, F32)
    v_small = _pack_rows([local_block(mom_v, n) for n in SMALL], SMALL_ROWS, F32)
    d_small, nm_small, nv_small = adamw_call(g_small, w_small, m_small, v_small, tm=SMALL_ROWS, name="adamw_small")
    g_parts = _unpack_rows(g_small, small_shapes + [(2, N_RWKV)])
    out_g.update(zip(SMALL, g_parts[:-1]))
    out_d.update(zip(SMALL, _unpack_rows(d_small, small_shapes)))
    out_m.update(zip(SMALL, _unpack_rows(nm_small, small_shapes)))
    out_v.update(zip(SMALL, _unpack_rows(nv_small, small_shapes)))
    g_sb = lax.dynamic_slice_in_dim(g_parts[-1], chip * SHIFT_SHARD[1], SHIFT_SHARD[1], axis=1)
    sb_args = [_pack_rows([z], 8, F32) for z in (g_sb, sb, local_block(mom_m, "shift_b"), local_block(mom_v, "shift_b"))]
    sb_res = adamw_call(*sb_args, tm=8, name="adamw_shift_b")
    out_g["shift_b"] = g_sb
    for tree, res in zip((out_d, out_m, out_v), sb_res):
        tree["shift_b"] = _unpack_rows(res, [SHIFT_SHARD])[0]

    def block_of(tree, n):
        return tree[n].reshape(given[n].shape)

    return (loss, grad_x[None], *[block_of(out_g, n) for n in WEIGHTS], *[block_of(out_d, n) for n in WEIGHTS],
            *[block_of(out_m, n) for n in WEIGHTS], *[block_of(out_v, n) for n in WEIGHTS])
```

```python
import functools

import jax
import jax.numpy as jnp
from jax import lax
from jax.experimental import pallas as pl
from jax.experimental.pallas import tpu as pltpu

F32 = jnp.float32
BF16 = jnp.bfloat16

D_MODEL = 1024
N_HEADS = 16
HEAD = 64
SCAN_CHUNK = 64

VMEM_LIMIT = 56 * 1024 * 1024


_BDIMS = {
    "nn": (((2,), (1,)), ((0,), (0,))),
    "nt": (((2,), (2,)), ((0,), (0,))),
    "tn": (((1,), (1,)), ((0,), (0,))),
}


def _raw_bdot(x, y, mode, fine):
    if fine:
        return lax.dot_general(x, y, _BDIMS[mode], precision=lax.Precision.HIGH, preferred_element_type=F32)
    return lax.dot_general(x.astype(BF16), y.astype(BF16), _BDIMS[mode], preferred_element_type=F32)


@functools.partial(jax.custom_vjp, nondiff_argnums=(2, 3))
def bdot(x, y, mode, fine=True):
    return _raw_bdot(x, y, mode, fine)


def _bdot_fwd(x, y, mode, fine):
    return _raw_bdot(x, y, mode, fine), (x, y)


def _bdot_bwd(mode, fine, res, g):
    x, y = res
    if mode == "nn":
        return bdot(g, y, "nt", fine), bdot(x, g, "tn", fine)
    if mode == "nt":
        return bdot(g, y, "nn", fine), bdot(g, x, "tn", fine)
    return bdot(y, g, "nt", fine), bdot(x, g, "nn", fine)


bdot.defvjp(_bdot_fwd, _bdot_bwd)


def _scan_chunk(S0, r, lw, k, v, a, b):
    nh, lc, _ = r.shape
    ti = lax.broadcasted_iota(jnp.int32, (lc, lc), 0)
    si = lax.broadcasted_iota(jnp.int32, (lc, lc), 1)
    incl = (si <= ti).astype(F32)
    strict = (si < ti).astype(F32)
    eye = (si == ti).astype(F32)
    cl = bdot(jnp.broadcast_to(incl, (nh, lc, lc)), lw, "nn")
    cl_last = cl[:, lc - 1:lc, :]
    g_last = jnp.exp(cl_last - cl)
    at = a * jnp.exp(cl - lw)
    bt = b * jnp.exp(-cl)
    kt = k * jnp.exp(-cl)
    rt = r * jnp.exp(cl)
    ar = jnp.concatenate([at, rt], axis=1)
    ar_b = bdot(ar, bt, "nt", False)
    ar_k = bdot(ar, kt, "nt", False)
    m_ab, m_rb = ar_b[:, :lc] * strict, ar_b[:, lc:] * incl
    m_ak, m_rk = ar_k[:, :lc] * strict, ar_k[:, lc:] * incl
    x = eye + m_ab
    p = bdot(m_ab, m_ab, "nn", False)
    n = 2
    while n * 2 < lc:
        px = bdot(jnp.concatenate([p, x], axis=1), p, "nn", False)
        p = px[:, :lc]
        x = x + px[:, lc:]
        n *= 2
    x = x + bdot(x, p, "nn", False)
    ar_s = bdot(ar, S0, "nt", False)
    akrk_v = bdot(jnp.concatenate([m_ak, m_rk], axis=1), v, "nn", False)
    u = bdot(x, ar_s[:, :lc] + akrk_v[:, :lc], "nn", False)
    o = ar_s[:, lc:] + bdot(m_rb, u, "nn", False) + akrk_v[:, lc:]
    s_last = S0 * jnp.exp(cl_last) + bdot(jnp.concatenate([u, v], axis=1),
                                          jnp.concatenate([b * g_last, k * g_last], axis=1), "tn", False)
    return o, s_last


def _split_heads(z):
    return jnp.stack([z[:, HEAD * h:HEAD * (h + 1)] for h in range(N_HEADS)], axis=0)


def _merge_heads(z):
    return jnp.concatenate([z[h] for h in range(N_HEADS)], axis=1)


def _scan_specs(t, ops, rev):
    nc = t // SCAN_CHUNK
    row = (lambda c: nc - 1 - c) if rev else (lambda c: c)
    specs = [pl.BlockSpec((SCAN_CHUNK, D_MODEL), lambda c, cb=cb: (row(c), cb)) for _, cb in ops]
    state = pl.BlockSpec((1, N_HEADS, HEAD, HEAD), lambda c: (row(c), 0, 0, 0))
    return nc, specs, state


def scan_fwd(ops, pack):
    t = ops[0][0].shape[0]
    nc, specs, state = _scan_specs(t, ops, False)

    def body(r_ref, lw_ref, k_ref, v_ref, a_ref, b_ref, pack_ref, o_ref, s0_ref, all_ref, s_scr, send_sems, recv_sems):
        step = pl.program_id(0)
        x, y, c = _coords()
        me = 2 * x + y
        sib = (x, y, 1 - c)
        chips = _other_chips(x, y)
        first = [_remote(pack_ref.at[c], all_ref.at[me, c], send_sems, recv_sems, k, (cx, cy, c))
                 for k, (cx, cy) in enumerate(chips)]
        passed = [_remote(all_ref.at[2 * cx + cy, c], all_ref.at[2 * cx + cy, c], send_sems, recv_sems, 3 + k, sib)
                  for k, (cx, cy) in enumerate(chips)]

        @pl.when(step == 0)
        def _():
            s_scr[...] = jnp.zeros_like(s_scr)
            for cp in first:
                cp.start()

        s0 = s_scr[...]
        s0_ref[0] = s0
        o, s_last = _scan_chunk(s0, *[_split_heads(z[...]) for z in (r_ref, lw_ref, k_ref, v_ref, a_ref, b_ref)])
        o_ref[...] = _merge_heads(o)
        s_scr[...] = s_last

        @pl.when(step == nc - 1)
        def _():
            for k, (cx, cy) in enumerate(chips):
                j = 2 * cx + cy
                _remote(pack_ref.at[c], all_ref.at[j, c], send_sems, recv_sems, k, (cx, cy, c)).wait_recv()
                passed[k].start()
            for k, (cx, cy) in enumerate(chips):
                j = 2 * cx + cy
                _remote(all_ref.at[j, 1 - c], all_ref.at[j, 1 - c], send_sems, recv_sems, 3 + k, sib).wait_recv()
            for cp in first + passed:
                cp.wait_send()

    return pl.pallas_call(
        body,
        name="scan_fwd",
        grid=(nc,),
        in_specs=specs + [_ANY],
        out_specs=[pl.BlockSpec((SCAN_CHUNK, D_MODEL), lambda c: (c, 0)), state, _ANY],
        out_shape=[jax.ShapeDtypeStruct((t, D_MODEL), F32), jax.ShapeDtypeStruct((nc, N_HEADS, HEAD, HEAD), F32),
                   jax.ShapeDtypeStruct((N_CHIPS,) + pack.shape, pack.dtype)],
        scratch_shapes=[pltpu.VMEM((N_HEADS, HEAD, HEAD), F32), pltpu.SemaphoreType.DMA((6,)),
                        pltpu.SemaphoreType.DMA((6,))],
        compiler_params=_cparams(1),
    )(*[a for a, _ in ops], pack)


def scan_bwd(ops, s0s, do, part):
    t = ops[0][0].shape[0]
    nc, specs, state = _scan_specs(t, ops + [(do, 0)], True)

    def body(r_ref, lw_ref, k_ref, v_ref, a_ref, b_ref, do_ref, s0_ref, part_ref, *rest):
        out_refs, slots_ref, ds_scr, send_sems, recv_sems = rest[:6], rest[6], rest[7], rest[8], rest[9]
        step = pl.program_id(0)
        x, y, c = _coords()
        me = 2 * x + y
        chips = _other_chips(x, y)
        sends = [_remote(part_ref.at[2 * cx + cy], slots_ref.at[me], send_sems, recv_sems, k, (cx, cy, c))
                 for k, (cx, cy) in enumerate(chips)]

        @pl.when(step == 0)
        def _():
            ds_scr[...] = jnp.zeros_like(ds_scr)
            for cp in sends:
                cp.start()

        _, vjp = jax.vjp(_scan_chunk, s0_ref[0],
                         *[_split_heads(z[...]) for z in (r_ref, lw_ref, k_ref, v_ref, a_ref, b_ref)])
        grads = vjp((_split_heads(do_ref[...]), ds_scr[...]))
        for o_ref, g in zip(out_refs, grads[1:]):
            o_ref[...] = _merge_heads(g)
        ds_scr[...] = grads[0]

        @pl.when(step == nc - 1)
        def _():
            for k, (cx, cy) in enumerate(chips):
                _remote(part_ref.at[me], slots_ref.at[2 * cx + cy], send_sems, recv_sems, k, (cx, cy, c)).wait_recv()
            for cp in sends:
                cp.wait_send()

    return pl.pallas_call(
        body,
        name="scan_bwd",
        grid=(nc,),
        in_specs=specs + [state, _ANY],
        out_specs=[pl.BlockSpec((SCAN_CHUNK, D_MODEL), lambda c: (nc - 1 - c, 0))] * 6 + [_ANY],
        out_shape=[jax.ShapeDtypeStruct((t, D_MODEL), F32)] * 6 + [jax.ShapeDtypeStruct(part.shape, part.dtype)],
        scratch_shapes=[pltpu.VMEM((N_HEADS, HEAD, HEAD), F32), pltpu.SemaphoreType.DMA((3,)),
                        pltpu.SemaphoreType.DMA((3,))],
        compiler_params=_cparams(1),
    )(*[a for a, _ in ops], do, s0s, part)


_MDIMS = {
    "nn": (((1,), (0,)), ((), ())),
    "nt": (((1,), (1,)), ((), ())),
    "tn": (((0,), (0,)), ((), ())),
}


def _raw_mdot(x, y, mode, exact):
    if exact:
        return lax.dot_general(x, y, _MDIMS[mode], precision=lax.Precision.HIGH, preferred_element_type=F32)
    return lax.dot_general(x.astype(BF16), y.astype(BF16), _MDIMS[mode], preferred_element_type=F32)


@functools.partial(jax.custom_vjp, nondiff_argnums=(2, 3))
def mdot(x, y, mode, exact):
    return _raw_mdot(x, y, mode, exact)


def _mdot_fwd(x, y, mode, exact):
    return _raw_mdot(x, y, mode, exact), (x, y)


def _mdot_bwd(mode, exact, res, g):
    x, y = res
    if mode == "nn":
        return mdot(g, y, "nt", exact), mdot(x, g, "tn", exact)
    if mode == "nt":
        return mdot(g, y, "nn", exact), mdot(g, x, "tn", exact)
    return mdot(y, g, "nt", exact), mdot(x, g, "nn", exact)


mdot.defvjp(_mdot_fwd, _mdot_bwd)


def _seg_ones():
    i = lax.broadcasted_iota(jnp.int32, (256, 256), 0) // HEAD
    j = lax.broadcasted_iota(jnp.int32, (256, 256), 1) // HEAD
    return (i == j).astype(BF16)


@jax.custom_vjp
def segsum(x):
    bd = _seg_ones()
    hi = x.astype(BF16)
    lo = (x - hi.astype(F32)).astype(BF16)
    cols = []
    for j in range(x.shape[1] // 256):
        sl = slice(256 * j, 256 * (j + 1))
        cols.append(jnp.dot(hi[:, sl], bd, preferred_element_type=F32)
                    + jnp.dot(lo[:, sl], bd, preferred_element_type=F32))
    return jnp.concatenate(cols, axis=1)


segsum.defvjp(lambda x: (segsum(x), None), lambda _, g: (segsum(g),))


NORM_EPS = 1e-6
LN_EPS = 1e-5
GN_EPS = 64e-5
SGU_CHUNK = 128
SGU_GROUPS = 8


def _rms(x, g):
    return x * lax.rsqrt(jnp.mean(x * x, axis=-1, keepdims=True) + NORM_EPS) * g


def f_norm_in(x, g):
    return _rms(x, g), x


def f_sgu(p, ln_w, ln_b, sw, sbt):
    tm = p.shape[0]
    z = 0.5 * p * (1.0 + lax.erf(p * 0.7071067811865476))
    u, v = z[:, :D_MODEL], z[:, D_MODEL:]
    mu = jnp.mean(v, axis=-1, keepdims=True)
    d = v - mu
    vn = d * lax.rsqrt(jnp.mean(d * d, axis=-1, keepdims=True) + LN_EPS) * ln_w + ln_b
    ii = lax.broadcasted_iota(jnp.int32, (SGU_CHUNK, SGU_CHUNK), 0)
    jj = lax.broadcasted_iota(jnp.int32, (SGU_CHUNK, SGU_CHUNK), 1)
    mask = (jj <= ii).astype(F32)
    gi = lax.broadcasted_iota(jnp.int32, (SGU_GROUPS, D_MODEL), 0)
    ci = lax.broadcasted_iota(jnp.int32, (SGU_GROUPS, D_MODEL), 1) // SGU_CHUNK
    bias = mdot(sbt, (gi == ci).astype(F32), "nn", True)
    rows = []
    for c in range(tm // SGU_CHUNK):
        cols = []
        for g in range(SGU_GROUPS):
            blk = vn[c * SGU_CHUNK:(c + 1) * SGU_CHUNK, g * SGU_CHUNK:(g + 1) * SGU_CHUNK]
            cols.append(mdot(sw[g] * mask, blk, "nn", False))
        rows.append(jnp.concatenate(cols, axis=1) + bias)
    return (u * jnp.concatenate(rows, axis=0),)


def _softplus(x):
    return jnp.maximum(x, 0.0) + jnp.log1p(jnp.exp(-jnp.abs(x)))


def f_pre(qr, qk, qv, ql, wl, w0, al, a0, gl, k_k, k_a):
    xw, xa, xg = ql[:, :128], ql[:, 128:256], ql[:, 256:512]
    wr = -_softplus(-(w0 + mdot(jnp.tanh(xw), wl, "nn", True))) - 0.5
    lw = -jnp.exp(wr)
    aa = jax.nn.sigmoid(a0 + mdot(xa, al, "nn", True))
    g = mdot(jax.nn.sigmoid(xg), gl, "nn", True)
    kkr = qk * k_k
    kk = kkr / jnp.maximum(jnp.sqrt(segsum(kkr * kkr)), 1e-12)
    kp = qk * (1.0 + (aa - 1.0) * k_a)
    return qr, lw, kp, qv, -kk, kk * aa, g, qr, kp, qv


def f_post(o, r, kp, v, g, lnw, lnb, rk):
    mu = segsum(o) * (1.0 / HEAD)
    d = o - mu
    gn = d * lax.rsqrt(segsum(d * d) * (1.0 / HEAD) + GN_EPS)
    return ((gn * lnw + lnb + segsum(r * kp * rk) * v) * g,)


def f_mix(ya, yb, ga, gb):
    return (jax.nn.sigmoid(ga) * ya + jax.nn.sigmoid(gb) * yb,)


def f_ffn_in(h1, g):
    return _rms(h1, g), h1


def f_final(h1, m3, tgt, g):
    y = _rms(h1 + m3, g)
    err = jnp.square(y - tgt)
    return 0.5 * jnp.sum(jnp.mean(err, axis=-1))


def _cparams(n_grid):
    return pltpu.CompilerParams(dimension_semantics=("arbitrary",) * n_grid, vmem_limit_bytes=VMEM_LIMIT)


def _tile_spec(tm, w, cb):
    return pl.BlockSpec((tm, w), lambda i: (i, cb))


def _const_spec(c):
    nd = c.ndim
    return pl.BlockSpec(c.shape, lambda i: (0,) * nd)


def ew_call(fn, tiled, consts, outs, *, tm, name):
    t = tiled[0][0].shape[0]
    n_t, n_c = len(tiled), len(consts)

    def body(*refs):
        tv = [r[...].astype(F32) for r in refs[:n_t]]
        cv = [r[...] for r in refs[n_t:n_t + n_c]]
        res = fn(*tv, *cv)
        for o_ref, val in zip(refs[n_t + n_c:], res):
            o_ref[...] = val.astype(o_ref.dtype)

    return pl.pallas_call(
        body,
        name=name,
        grid=(t // tm,),
        in_specs=[_tile_spec(tm, w, cb) for _, w, cb in tiled] + [_const_spec(c) for c in consts],
        out_specs=[_tile_spec(tm, w, 0) for w, _ in outs],
        out_shape=[jax.ShapeDtypeStruct((t, w), dt) for w, dt in outs],
        compiler_params=_cparams(1),
    )(*[a for a, _, _ in tiled], *consts)


def ew_vjp_call(fn, tiled, consts, cots, d_tiled, d_consts, *, tm, name):
    t = tiled[0][0].shape[0]
    n_t, n_c, n_g = len(tiled), len(consts), len(cots)
    dt_list = [(i, dt) for i, dts in enumerate(d_tiled) for dt in dts]
    dc_list = [i for i, want in enumerate(d_consts) if want]

    def body(*refs):
        tv = [r[...].astype(F32) for r in refs[:n_t]]
        cv = [r[...] for r in refs[n_t:n_t + n_c]]
        gv = tuple(r[...].astype(F32) for r in refs[n_t + n_c:n_t + n_c + n_g])
        out_refs = refs[n_t + n_c + n_g:]
        _, vjp = jax.vjp(fn, *tv, *cv)
        grads = vjp(gv)
        for o_ref, (i, _) in zip(out_refs, dt_list):
            o_ref[...] = grads[i].astype(o_ref.dtype)
        acc_refs = out_refs[len(dt_list):]

        @pl.when(pl.program_id(0) == 0)
        def _():
            for a_ref in acc_refs:
                a_ref[...] = jnp.zeros_like(a_ref)

        for a_ref, i in zip(acc_refs, dc_list):
            a_ref[...] += grads[n_t + i]

    res = pl.pallas_call(
        body,
        name=name,
        grid=(t // tm,),
        in_specs=[_tile_spec(tm, w, cb) for _, w, cb in tiled] + [_const_spec(c) for c in consts]
        + [_tile_spec(tm, w, cb) for _, w, cb in cots],
        out_specs=[_tile_spec(tm, tiled[i][1], 0) for i, _ in dt_list] + [_const_spec(consts[i]) for i in dc_list],
        out_shape=[jax.ShapeDtypeStruct((t, tiled[i][1]), dt) for i, dt in dt_list]
        + [jax.ShapeDtypeStruct(consts[i].shape, F32) for i in dc_list],
        compiler_params=_cparams(1),
    )(*[a for a, _, _ in tiled], *consts, *[a for a, _, _ in cots])
    return res[:len(dt_list)], res[len(dt_list):]


def mm(a, b, mode, *, tm, tn, name, out_dtypes=(F32,), epi=None, extras=(), into=None):
    m = a.shape[1] if mode == "tn" else a.shape[0]
    kd = a.shape[0] if mode == "tn" else a.shape[1]
    n = b.shape[0] if mode == "nt" else b.shape[1]
    tm, tn = min(tm, m), min(tn, n)
    if mode == "nn":
        a_spec = pl.BlockSpec((tm, kd), lambda i, j: (i, 0))
        b_spec = pl.BlockSpec((kd, tn), lambda i, j: (0, j))
    elif mode == "nt":
        a_spec = pl.BlockSpec((tm, kd), lambda i, j: (i, 0))
        b_spec = pl.BlockSpec((tn, kd), lambda i, j: (j, 0))
    else:
        a_spec = pl.BlockSpec((kd, tm), lambda i, j: (0, i))
        b_spec = pl.BlockSpec((kd, tn), lambda i, j: (0, j))
    n_e = len(extras)
    o_spec = pl.BlockSpec((tm, tn), lambda i, j: (i, j))

    if into is not None:
        buf, place = into

        def body_into(a_ref, b_ref, buf_ref, o_ref):
            o_ref[0, 0] = lax.dot_general(a_ref[...].astype(BF16), b_ref[...].astype(BF16), _MDIMS[mode],
                                          preferred_element_type=F32)

        return pl.pallas_call(
            body_into,
            name=name,
            grid=(m // tm, n // tn),
            in_specs=[a_spec, b_spec, pl.BlockSpec(memory_space=pl.ANY)],
            out_specs=pl.BlockSpec((1, 1, tm, tn), lambda i, j: (*place(i, j), 0)),
            out_shape=jax.ShapeDtypeStruct(buf.shape, F32),
            input_output_aliases={2: 0},
            compiler_params=_cparams(2),
        )(a, b, buf)

    def body(a_ref, b_ref, *refs):
        c = lax.dot_general(a_ref[...].astype(BF16), b_ref[...].astype(BF16), _MDIMS[mode],
                            preferred_element_type=F32)
        res = epi(c, *[r[...] for r in refs[:n_e]]) if epi is not None else (c,)
        for o_ref, val in zip(refs[n_e:], res):
            o_ref[...] = val.astype(o_ref.dtype)

    res = pl.pallas_call(
        body,
        name=name,
        grid=(m // tm, n // tn),
        in_specs=[a_spec, b_spec] + [o_spec] * n_e,
        out_specs=[o_spec] * len(out_dtypes),
        out_shape=[jax.ShapeDtypeStruct((m, n), dt) for dt in out_dtypes],
        compiler_params=_cparams(2),
    )(a, b, *extras)
    return res if len(out_dtypes) > 1 else res[0]


P_WIDTH = 7680
RWKV_COL0 = 4096
RWKV_WIDTH = 3584
SHIFT_BLK = 512


def _shift_down(p, prev_row):
    rows = lax.broadcasted_iota(jnp.int32, p.shape, 0)
    return jnp.where(rows == 0, prev_row, pltpu.roll(p, 1, 0))


def shiftmix_fwd(p_all, sbp, *, tm):
    t = p_all.shape[0]
    tm = min(tm, t)
    c0 = RWKV_COL0 // SHIFT_BLK
    hb = tm // 8

    def body(p_ref, halo_ref, sb_ref, q_ref):
        p = p_ref[...]
        prev = jnp.where(pl.program_id(0) == 0, 0.0, halo_ref[7:8, :])
        q_ref[...] = p * sb_ref[0:1, :] + _shift_down(p, prev) * sb_ref[1:2, :]

    return pl.pallas_call(
        body,
        name="shiftmix_fwd",
        grid=(t // tm, RWKV_WIDTH // SHIFT_BLK),
        in_specs=[
            pl.BlockSpec((tm, SHIFT_BLK), lambda i, j: (i, c0 + j)),
            pl.BlockSpec((8, SHIFT_BLK), lambda i, j: (jnp.maximum(i * hb - 1, 0), c0 + j)),
            pl.BlockSpec((2, SHIFT_BLK), lambda i, j: (0, j)),
        ],
        out_specs=pl.BlockSpec((tm, SHIFT_BLK), lambda i, j: (i, j)),
        out_shape=jax.ShapeDtypeStruct((t, RWKV_WIDTH), F32),
        compiler_params=_cparams(2),
    )(p_all, p_all, sbp)


def shiftmix_bwd(dq, col0, p_all, sbp, *, tm, name):
    t, w = dq.shape
    n_i = t // tm
    hb = tm // 8
    cq = col0 // SHIFT_BLK
    cp = (RWKV_COL0 + col0) // SHIFT_BLK

    def body(dq_ref, dqn_ref, p_ref, ph_ref, sb_ref, dp_ref, dsb_ref):
        i = pl.program_id(1)
        dq_t = dq_ref[...]
        rows = lax.broadcasted_iota(jnp.int32, dq_t.shape, 0)
        nxt = jnp.where(i == n_i - 1, 0.0, dqn_ref[0:1, :])
        up = jnp.where(rows == tm - 1, nxt, pltpu.roll(dq_t, tm - 1, 0))
        dp_ref[...] = (dq_t * sb_ref[0:1, :] + up * sb_ref[1:2, :]).astype(dp_ref.dtype)
        p = p_ref[...]
        prev = jnp.where(i == 0, 0.0, ph_ref[7:8, :])
        s0 = jnp.sum(dq_t * p, axis=0, keepdims=True)
        s1 = jnp.sum(dq_t * _shift_down(p, prev), axis=0, keepdims=True)
        two = lax.broadcasted_iota(jnp.int32, (2, SHIFT_BLK), 0)

        @pl.when(i == 0)
        def _():
            dsb_ref[...] = jnp.zeros_like(dsb_ref)

        dsb_ref[...] += jnp.where(two == 0, s0, s1)

    return pl.pallas_call(
        body,
        name=name,
        grid=(w // SHIFT_BLK, n_i),
        in_specs=[
            pl.BlockSpec((tm, SHIFT_BLK), lambda j, i: (i, j)),
            pl.BlockSpec((8, SHIFT_BLK), lambda j, i: (jnp.minimum((i + 1) * hb, t // 8 - 1), j)),
            pl.BlockSpec((tm, SHIFT_BLK), lambda j, i: (i, cp + j)),
            pl.BlockSpec((8, SHIFT_BLK), lambda j, i: (jnp.maximum(i * hb - 1, 0), cp + j)),
            pl.BlockSpec((2, SHIFT_BLK), lambda j, i: (0, cq + j)),
        ],
        out_specs=[
            pl.BlockSpec((tm, SHIFT_BLK), lambda j, i: (i, j)),
            pl.BlockSpec((2, SHIFT_BLK), lambda j, i: (0, j)),
        ],
        out_shape=[jax.ShapeDtypeStruct((t, w), BF16), jax.ShapeDtypeStruct((2, w), F32)],
        compiler_params=_cparams(2),
    )(dq, dq, p_all, p_all, sbp)


def final_call(h1, m3, tgt, g_final, *, tm):
    t = h1.shape[0]

    def body(h1_ref, m3_ref, tgt_ref, g_ref, dh_ref, dhb_ref, dg_ref, loss_ref):
        loss, vjp = jax.vjp(f_final, h1_ref[...], m3_ref[...], tgt_ref[...], g_ref[...])
        dh, _, _, dg = vjp(jnp.ones((), F32))
        dh_ref[...] = dh
        dhb_ref[...] = dh.astype(BF16)

        @pl.when(pl.program_id(0) == 0)
        def _():
            dg_ref[...] = jnp.zeros_like(dg_ref)
            loss_ref[...] = jnp.zeros_like(loss_ref)

        dg_ref[...] += dg
        loss_ref[...] += jnp.full(loss_ref.shape, loss, F32)

    tile = _tile_spec(tm, D_MODEL, 0)
    return pl.pallas_call(
        body,
        name="final_loss",
        grid=(t // tm,),
        in_specs=[tile, tile, tile, _const_spec(g_final)],
        out_specs=[tile, tile, _const_spec(g_final), pl.BlockSpec((8, 128), lambda i: (0, 0))],
        out_shape=[jax.ShapeDtypeStruct((t, D_MODEL), F32), jax.ShapeDtypeStruct((t, D_MODEL), BF16),
                   jax.ShapeDtypeStruct(g_final.shape, F32), jax.ShapeDtypeStruct((8, 128), F32)],
        compiler_params=_cparams(1),
    )(h1, m3, tgt, g_final)


N_SGU = 2048
N_RWKV = 3360
LORA_W, LORA_A, LORA_G = 64, 64, 160


def _pad_rwkv_cols(z):
    zero = lambda n: jnp.zeros(z.shape[:-1] + (n,), z.dtype)
    return jnp.concatenate([z[..., :3072], z[..., 3072:3136], zero(64), z[..., 3136:3200], zero(64),
                            z[..., 3200:3360], zero(96)], axis=-1)


def _unpad_rwkv_cols(z):
    return jnp.concatenate([z[..., :3072], z[..., 3072:3136], z[..., 3200:3264], z[..., 3328:3488]], axis=-1)


def _pad_win_rows(wt):
    z = wt[N_SGU:N_SGU + N_RWKV]
    zero = lambda n: jnp.zeros((n, wt.shape[1]), wt.dtype)
    return jnp.concatenate([wt[:N_SGU], wt[N_SGU + N_RWKV:], z[:3072], z[3072:3136], zero(64), z[3136:3200], zero(64),
                            z[3200:3360], zero(96)], axis=0)


def _unpad_win_rows(wt):
    z = wt[RWKV_COL0:]
    return jnp.concatenate([wt[:N_SGU], z[:3072], z[3072:3136], z[3200:3264], z[3328:3488], wt[N_SGU:RWKV_COL0]],
                           axis=0)


def _pad_rows(w, n):
    return jnp.concatenate([w, jnp.zeros((n - w.shape[0],) + w.shape[1:], w.dtype)], axis=0)


def _relu2_epi(c):
    return c, jnp.square(jnp.maximum(c, 0.0))


def _relu2_bwd_epi(c, hid):
    return (c * (2.0 * jnp.maximum(hid.astype(F32), 0.0)),)


def _add_epi(c, x):
    return (c + x,)


def _pre_fwd(*args):
    res = f_pre(*args)
    return res[1], res[2], res[4], res[5], res[6]


def local_step(x, tgt, w, late_pack, late_weights, late_partials):
    d = D_MODEL
    win_pt = _pad_win_rows(w["w_in"])
    sbp = _pad_rwkv_cols(w["shift_b"])
    wl = _pad_rows(w["w_lora_w"], 128)
    al = _pad_rows(w["a_lora_w"], 128)
    gl = _pad_rows(w["g_lora_w"], 256)
    sbt = w["sgu_b"].T

    (a_bf,) = ew_call(lambda x_, g_: (f_norm_in(x_, g_)[0],), [(x, d, 0)], [w["g_mix"]], [(d, BF16)], tm=256,
                      name="norm_in")
    p_all = mm(a_bf, win_pt, "nt", tm=2048, tn=640, name="mm_in")
    sgu_t = [(p_all, 2 * d, 0)]
    sgu_c = [w["sgu_ln_w"], w["sgu_ln_b"], w["sgu_w"], sbt]
    (s_bf,) = ew_call(f_sgu, sgu_t, sgu_c, [(d, BF16)], tm=256, name="sgu_fwd")
    ya = mm(s_bf, w["w_proj_a"], "nn", tm=512, tn=1024, name="mm_proj_a")
    q = shiftmix_fwd(p_all, sbp, tm=1024)
    pre_t = [(q, d, 0), (q, d, 1), (q, d, 2), (q, 512, 6)]
    pre_c = [wl, w["w0"], al, w["a0"], gl, w["k_k"], w["k_a"]]
    lw, kp, na, nb, g = ew_call(_pre_fwd, pre_t, pre_c, [(d, F32)] * 5, tm=256, name="rwkv_pre_fwd")
    scan_ops = [(q, 0), (lw, 0), (kp, 0), (q, 2), (na, 0), (nb, 0)]
    o, s0s, late_all = scan_fwd(scan_ops, late_pack)
    w = {**w, **late_weights(late_all)}
    post_t = [(o, d, 0), (q, d, 0), (kp, d, 0), (q, d, 2), (g, d, 0)]
    post_c = [w["ln_x_w"], w["ln_x_b"], w["r_k"]]
    (ob_bf,) = ew_call(f_post, post_t, post_c, [(d, BF16)], tm=256, name="rwkv_post_fwd")
    yb = mm(ob_bf, w["w_proj_b"], "nn", tm=512, tn=1024, name="mm_proj_b")
    mix_t = [(ya, d, 0), (yb, d, 0), (p_all, d, 2), (p_all, d, 3)]
    (mixed_bf,) = ew_call(f_mix, mix_t, [], [(d, BF16)], tm=256, name="mix_fwd")
    h1 = mm(mixed_bf, w["w_out"], "nn", tm=512, tn=1024, name="mm_out", epi=_add_epi, extras=(x,))
    (f_bf,) = ew_call(lambda h_, g_: (f_ffn_in(h_, g_)[0],), [(h1, d, 0)], [w["g_ffn"]], [(d, BF16)], tm=256,
                      name="ffn_norm")
    hid, act_bf = mm(f_bf, w["w_ffn1"], "nn", tm=2048, tn=1024, name="mm_ffn1", out_dtypes=(BF16, BF16), epi=_relu2_epi)
    m3 = mm(act_bf, w["w_ffn2"], "nn", tm=1024, tn=512, name="mm_ffn2")
    dh2, dh2_bf, dg_final, loss = final_call(h1, m3, tgt, w["g_final"], tm=256)

    dhid_bf = mm(dh2_bf, w["w_ffn2"], "nt", tm=2048, tn=1024, name="mm_dact", out_dtypes=(BF16,), epi=_relu2_bwd_epi,
                 extras=(hid,))
    late_g = lax.empty((N_CHIPS, 2, PACK_ROWS, HALF_W), F32)
    late_g = mm(act_bf, dh2_bf, "tn", tm=512, tn=HALF_W, name="mm_dw_ffn2",
                into=(late_g, lambda i, j: (i // 2, j, PIECE_OFF["w_ffn2"] // 512 + i % 2)))
    df = mm(dhid_bf, w["w_ffn1"], "nt", tm=1024, tn=512, name="mm_df")
    late_g = mm(f_bf, dhid_bf, "tn", tm=512, tn=HALF_W, name="mm_dw_ffn1",
                into=(late_g, lambda i, j: (j // 2, j % 2, PIECE_OFF["w_ffn1"] // 512 + i)))
    (dh1, dh1_bf), (dg_ffn,) = ew_vjp_call(f_ffn_in, [(h1, d, 0)], [w["g_ffn"]], [(df, d, 0), (dh2, d, 0)],
                                           [(F32, BF16)], [True], tm=256, name="ffn_norm_bwd")
    dmixed = mm(dh1_bf, w["w_out"], "nt", tm=512, tn=1024, name="mm_dmixed")
    late_g = mm(mixed_bf, dh1_bf, "tn", tm=256, tn=HALF_W, name="mm_dw_out",
                into=(late_g, lambda i, j: (i, j, PIECE_OFF["w_out"] // 256)))
    (dya_bf, dyb_bf, dga_bf, dgb_bf), _ = ew_vjp_call(f_mix, mix_t, [], [(dmixed, d, 0)], [(BF16,)] * 4, [], tm=256,
                                                      name="mix_bwd")
    dob = mm(dyb_bf, w["w_proj_b"], "nt", tm=512, tn=1024, name="mm_dob")
    late_g = mm(ob_bf, dyb_bf, "tn", tm=256, tn=HALF_W, name="mm_dw_proj_b",
                into=(late_g, lambda i, j: (i, j, PIECE_OFF["w_proj_b"] // 256)))
    (do, dr_p, dkp_p, dv_p, dg), (dlnx_w, dlnx_b, dr_k) = ew_vjp_call(
        f_post, post_t, post_c, [(dob, d, 0)], [(F32,)] * 5, [True] * 3, tm=256, name="rwkv_post_bwd")
    late_part, late_part16 = late_partials(late_g)
    *scan_g, late_slots = scan_bwd(scan_ops, s0s, do, late_part16)
    pre_g = [(z, d, 0) for z in scan_g] + [(dg, d, 0), (dr_p, d, 0), (dkp_p, d, 0), (dv_p, d, 0)]
    (dq_r, dq_k, dq_v, dq_l), (dwl, dw0, dal, da0, dgl, dk_k, dk_a) = ew_vjp_call(
        f_pre, pre_t, pre_c, pre_g, [(F32,)] * 4, [True] * 7, tm=128, name="rwkv_pre_bwd")
    dp_r, dsb_r = shiftmix_bwd(dq_r, 0, p_all, sbp, tm=256, name="shiftmix_bwd_r")
    dp_k, dsb_k = shiftmix_bwd(dq_k, d, p_all, sbp, tm=256, name="shiftmix_bwd_k")
    dp_v, dsb_v = shiftmix_bwd(dq_v, 2 * d, p_all, sbp, tm=256, name="shiftmix_bwd_v")
    dp_l, dsb_l = shiftmix_bwd(dq_l, 3 * d, p_all, sbp, tm=256, name="shiftmix_bwd_l")
    ds = mm(dya_bf, w["w_proj_a"], "nt", tm=512, tn=1024, name="mm_ds")
    d_proj_a = mm(s_bf, dya_bf, "tn", tm=512, tn=1024, name="mm_dw_proj_a")
    (dp_sgu,), (dln_w, dln_b, dsw, dsbt) = ew_vjp_call(f_sgu, sgu_t, sgu_c, [(ds, d, 0)], [(BF16,)], [True] * 4,
                                                       tm=256, name="sgu_bwd")
    dp_all = jnp.concatenate([dp_sgu, dga_bf, dgb_bf, dp_r, dp_k, dp_v, dp_l], axis=1)
    da = mm(dp_all, win_pt, "nn", tm=1024, tn=256, name="mm_da")
    d_in_pt = mm(dp_all, a_bf, "tn", tm=1280, tn=1024, name="mm_dw_in")
    (grad_x,), (dg_mix,) = ew_vjp_call(f_norm_in, [(x, d, 0)], [w["g_mix"]], [(da, d, 0), (dh1, d, 0)], [(F32,)],
                                       [True], tm=256, name="norm_in_bwd")

    grads = {
        "g_mix": dg_mix, "w_in": _unpad_win_rows(d_in_pt), "sgu_ln_w": dln_w, "sgu_ln_b": dln_b, "sgu_w": dsw,
        "sgu_b": dsbt.T, "w_proj_a": d_proj_a,
        "shift_b": _unpad_rwkv_cols(jnp.concatenate([dsb_r, dsb_k, dsb_v, dsb_l], axis=1)),
        "w_lora_w": dwl[:LORA_W], "w0": dw0, "a_lora_w": dal[:LORA_A], "a0": da0, "g_lora_w": dgl[:LORA_G],
        "k_k": dk_k, "k_a": dk_a, "r_k": dr_k, "ln_x_w": dlnx_w, "ln_x_b": dlnx_b, "g_ffn": dg_ffn,
        "g_final": dg_final,
    }
    return loss[0, 0], grad_x, grads, (late_part, late_slots)


MESH = pl.DeviceIdType.MESH
N_CHIPS = 4
N_DEV = 8
PACK_ROWS = 2560
PACK_TILE = 512
SMALL_ROWS = 152
_ANY = pl.BlockSpec(memory_space=pl.ANY)


def _coords():
    return lax.axis_index("x"), lax.axis_index("y"), lax.axis_index("c")


def _other_chips(x, y):
    return [(1 - x, y), (x, 1 - y), (1 - x, 1 - y)]


def _remote(src, dst, send_sems, recv_sems, k, to):
    return pltpu.make_async_remote_copy(src_ref=src, dst_ref=dst, send_sem=send_sems.at[k], recv_sem=recv_sems.at[k],
                                        device_id=to, device_id_type=MESH)


def gather_shards(pack):
    def body(src_ref, out_ref, send_sems, recv_sems):
        x, y, c = _coords()
        me = 2 * x + y
        sib = (x, y, 1 - c)
        chips = _other_chips(x, y)
        first = [_remote(src_ref.at[c], out_ref.at[me, c], send_sems, recv_sems, k, (cx, cy, c))
                 for k, (cx, cy) in enumerate(chips)]
        for cp in first:
            cp.start()
        passed = []
        for k, (cx, cy) in enumerate(chips):
            j = 2 * cx + cy
            _remote(src_ref.at[c], out_ref.at[j, c], send_sems, recv_sems, k, (cx, cy, c)).wait_recv()
            fwd = _remote(out_ref.at[j, c], out_ref.at[j, c], send_sems, recv_sems, 3 + k, sib)
            fwd.start()
            passed.append(fwd)
        for k, (cx, cy) in enumerate(chips):
            j = 2 * cx + cy
            _remote(out_ref.at[j, 1 - c], out_ref.at[j, 1 - c], send_sems, recv_sems, 3 + k, sib).wait_recv()
        for cp in first + passed:
            cp.wait_send()

    return pl.pallas_call(
        body,
        name="gather_shards",
        in_specs=[_ANY],
        out_specs=_ANY,
        out_shape=jax.ShapeDtypeStruct((N_CHIPS,) + pack.shape, pack.dtype),
        scratch_shapes=[pltpu.SemaphoreType.DMA((6,)), pltpu.SemaphoreType.DMA((6,))],
    )(pack)


def reduce_pair(g, tag):
    def body(g_ref, got_ref, send_sems, recv_sems):
        x, y, c = _coords()
        sib = (x, y, 1 - c)
        sends = [_remote(g_ref.at[j, 1 - c], got_ref.at[j], send_sems, recv_sems, j, sib) for j in range(N_CHIPS)]
        for cp in sends:
            cp.start()
        for cp in sends:
            cp.wait_recv()
        for cp in sends:
            cp.wait_send()

    return pl.pallas_call(
        body,
        name="reduce_pair_" + tag,
        in_specs=[_ANY],
        out_specs=_ANY,
        out_shape=jax.ShapeDtypeStruct((N_CHIPS,) + g.shape[2:], g.dtype),
        scratch_shapes=[pltpu.SemaphoreType.DMA((N_CHIPS,)), pltpu.SemaphoreType.DMA((N_CHIPS,))],
    )(g)


def pair_sum(g, got, tag, *, tm):
    n, _, rows, width = g.shape

    def body(g0_ref, g1_ref, got_ref, out_ref, out16_ref):
        own = jnp.where(lax.axis_index("c") == 0, g0_ref[0, 0], g1_ref[0, 0])
        total = own + got_ref[0]
        out_ref[0] = total
        out16_ref[0] = total.astype(BF16)

    blk = pl.BlockSpec((1, tm, width), lambda j, i: (j, i, 0))
    return pl.pallas_call(
        body,
        name="pair_sum_" + tag,
        grid=(n, rows // tm),
        in_specs=[pl.BlockSpec((1, 1, tm, width), lambda j, i: (j, 0, i, 0)),
                  pl.BlockSpec((1, 1, tm, width), lambda j, i: (j, 1, i, 0)), blk],
        out_specs=[blk, blk],
        out_shape=[jax.ShapeDtypeStruct(got.shape, F32), jax.ShapeDtypeStruct(got.shape, BF16)],
        compiler_params=_cparams(2),
    )(g, g, got)


def reduce_chips(p):
    def body(p_ref, out_ref, send_sems, recv_sems):
        x, y, c = _coords()
        me = 2 * x + y
        chips = _other_chips(x, y)
        sends = [_remote(p_ref.at[2 * cx + cy], out_ref.at[me], send_sems, recv_sems, k, (cx, cy, c))
                 for k, (cx, cy) in enumerate(chips)]
        for cp in sends:
            cp.start()
        for k, (cx, cy) in enumerate(chips):
            _remote(p_ref.at[me], out_ref.at[2 * cx + cy], send_sems, recv_sems, k, (cx, cy, c)).wait_recv()
        for cp in sends:
            cp.wait_send()

    return pl.pallas_call(
        body,
        name="reduce_chips",
        in_specs=[_ANY],
        out_specs=_ANY,
        out_shape=jax.ShapeDtypeStruct(p.shape, p.dtype),
        scratch_shapes=[pltpu.SemaphoreType.DMA((3,)), pltpu.SemaphoreType.DMA((3,))],
    )(p)


def sum_with_own(own, slots, index_fn, *, tm, name):
    n, rows, width = slots.shape
    own3 = own.ndim == 3

    def body(*refs):
        mine = index_fn()
        acc = None
        for s in range(n):
            o = refs[s][0] if own3 else refs[0][...]
            term = jnp.where(mine == s, o, refs[(n if own3 else 1) + s][0].astype(F32))
            acc = term if acc is None else acc + term
        refs[-1][...] = acc

    slot_specs = [pl.BlockSpec((1, tm, width), lambda i, s=s: (s, i, 0)) for s in range(n)]
    own_specs = slot_specs if own3 else [pl.BlockSpec((tm, width), lambda i: (i, 0))]
    return pl.pallas_call(
        body,
        name=name,
        grid=(rows // tm,),
        in_specs=own_specs + slot_specs,
        out_specs=pl.BlockSpec((tm, width), lambda i: (i, 0)),
        out_shape=jax.ShapeDtypeStruct((rows, width), F32),
        compiler_params=_cparams(1),
    )(*([own] * (n if own3 else 1)), *([slots] * n))


def exchange_halves(s, tag):
    rq = PACK_TILE
    nq = s.shape[0] // rq

    def body(s_ref, out_ref, sbuf, rbuf, send_sems, recv_sems, in_sems, out_sems):
        x, y, c = _coords()
        sib = (x, y, 1 - c)
        rows = lambda q: pl.ds(q * rq, rq)
        loads = [pltpu.make_async_copy(s_ref.at[rows(q)], sbuf.at[rows(q)], in_sems.at[q]) for q in range(nq)]
        for cp in loads:
            cp.start()
        sends = []
        for q in range(nq):
            loads[q].wait()
            sends.append(_remote(sbuf.at[rows(q)], rbuf.at[rows(q)], send_sems, recv_sems, q, sib))
            sends[q].start()
        stores = []
        for q in range(nq):
            sends[q].wait_recv()
            stores.append(pltpu.make_async_copy(rbuf.at[rows(q)], out_ref.at[rows(q)], out_sems.at[q]))
            stores[q].start()
        for cp in sends:
            cp.wait_send()
        for cp in stores:
            cp.wait()

    return pl.pallas_call(
        body,
        name="exchange_halves_" + tag,
        in_specs=[_ANY],
        out_specs=_ANY,
        out_shape=jax.ShapeDtypeStruct(s.shape, s.dtype),
        scratch_shapes=[pltpu.VMEM(s.shape, s.dtype), pltpu.VMEM(s.shape, s.dtype)]
        + [pltpu.SemaphoreType.DMA((nq,))] * 4,
        compiler_params=pltpu.CompilerParams(vmem_limit_bytes=VMEM_LIMIT),
    )(s)


def sum_all(s):
    def body(s_ref, out_ref, slots, mine, theirs, send_sems, recv_sems):
        x, y, c = _coords()
        me = 2 * x + y
        chips = _other_chips(x, y)
        sends = [_remote(s_ref, slots.at[me], send_sems, recv_sems, k, (cx, cy, c)) for k, (cx, cy) in enumerate(chips)]
        for cp in sends:
            cp.start()
        for k, (cx, cy) in enumerate(chips):
            _remote(s_ref, slots.at[2 * cx + cy], send_sems, recv_sems, k, (cx, cy, c)).wait_recv()
        slots[me] = s_ref[...]
        acc = ((slots[0] + slots[1]) + slots[2]) + slots[3]
        mine[...] = acc
        swap = _remote(mine, theirs, send_sems, recv_sems, 3, (x, y, 1 - c))
        swap.start()
        swap.wait_recv()
        out_ref[...] = acc + theirs[...]
        swap.wait_send()
        for cp in sends:
            cp.wait_send()

    vmem = pl.BlockSpec(memory_space=pltpu.VMEM)
    return pl.pallas_call(
        body,
        name="sum_all",
        in_specs=[vmem],
        out_specs=vmem,
        out_shape=jax.ShapeDtypeStruct(s.shape, s.dtype),
        scratch_shapes=[pltpu.VMEM((N_CHIPS,) + s.shape, s.dtype), pltpu.VMEM(s.shape, s.dtype),
                        pltpu.VMEM(s.shape, s.dtype), pltpu.SemaphoreType.DMA((4,)), pltpu.SemaphoreType.DMA((4,))],
        compiler_params=pltpu.CompilerParams(vmem_limit_bytes=VMEM_LIMIT),
    )(s)


ADAM_LR = 0.001
ADAM_B1 = 0.9
ADAM_B2 = 0.999
ADAM_EPS = 1e-08
ADAM_WD = 0.01
ADAM_STEP = 10


def f_adamw(g, w, m, v):
    m = ADAM_B1 * m + (1.0 - ADAM_B1) * g
    v = ADAM_B2 * v + (1.0 - ADAM_B2) * jnp.square(g)
    m_hat = m / (1.0 - ADAM_B1 ** ADAM_STEP)
    v_hat = v / (1.0 - ADAM_B2 ** ADAM_STEP)
    delta = -ADAM_LR * (m_hat / (jnp.sqrt(v_hat) + ADAM_EPS) + ADAM_WD * w)
    return delta, m, v


def adamw_call(g, w, m, v, *, tm, name):
    width = g.shape[1]
    return ew_call(f_adamw, [(g, width, 0), (w, width, 0), (m, width, 0), (v, width, 0)], [], [(width, F32)] * 3,
                   tm=tm, name=name)


def adamw_halves(g_own, g_other, w, m, v, *, tm):
    _, rows, width = w.shape

    def body(go_ref, gx_ref, w_ref, m_ref, v_ref, g_ref, d_ref, nm_ref, nv_ref):
        g = jnp.where(pl.program_id(0) == lax.axis_index("c"), go_ref[...], gx_ref[...])
        delta, nm, nv = f_adamw(g, w_ref[0], m_ref[0], v_ref[0])
        g_ref[0] = g
        d_ref[0] = delta
        nm_ref[0] = nm
        nv_ref[0] = nv

    half = pl.BlockSpec((tm, width), lambda h, i: (i, 0))
    full = pl.BlockSpec((1, tm, width), lambda h, i: (h, i, 0))
    return pl.pallas_call(
        body,
        name="adamw_sharded",
        grid=(2, rows // tm),
        in_specs=[half, half, full, full, full],
        out_specs=[full] * 4,
        out_shape=[jax.ShapeDtypeStruct(w.shape, F32)] * 4,
        compiler_params=_cparams(2),
    )(g_own, g_other, w, m, v)


EARLY = ["w_in", "w_proj_a", "w_lora_w", "a_lora_w", "g_lora_w"]
LATE = ["w_ffn1", "w_ffn2", "w_proj_b", "w_out"]
SHARDED = EARLY + LATE
LORAS = ["w_lora_w", "a_lora_w", "g_lora_w"]
HALF_W = 512
PIECE_ROWS = {"w_in": 1864, "w_ffn1": 1024, "w_ffn2": 1024, "w_proj_a": 256, "w_proj_b": 256, "w_out": 256,
              "w_lora_w": 32, "a_lora_w": 32, "g_lora_w": 80}
PIECE_OFF = {"w_in": 0, "w_proj_a": 1920, "w_lora_w": 2176, "a_lora_w": 2208, "g_lora_w": 2240,
             "w_ffn1": 0, "w_ffn2": 1024, "w_proj_b": 2048, "w_out": 2304}
LO_OFF = 2320
SHARD_AXIS = {"w_in": 1, "w_proj_a": 0, "w_lora_w": 1, "a_lora_w": 1, "g_lora_w": 1, "w_proj_b": 0, "w_out": 0,
              "w_ffn1": 1, "w_ffn2": 0}
SHARD_SHAPE = {"w_in": (1024, 1864), "w_proj_a": (256, 1024), "w_lora_w": (64, 256), "a_lora_w": (64, 256),
               "g_lora_w": (160, 256), "w_proj_b": (256, 1024), "w_out": (256, 1024), "w_ffn1": (1024, 1024),
               "w_ffn2": (1024, 1024)}
SHIFT_SHARD = (2, 840)
VECTORS = ["g_mix", "sgu_ln_w", "sgu_ln_b", "w0", "a0", "k_k", "k_a", "r_k", "ln_x_w", "ln_x_b", "g_ffn", "g_final"]
SMALL = VECTORS + ["sgu_w", "sgu_b"]
SMALL_SHAPE = {**{n: (1, 1024) for n in VECTORS}, "sgu_w": (8, 128, 128), "sgu_b": (8, 128)}
WEIGHTS = ["g_mix", "w_in", "sgu_ln_w", "sgu_ln_b", "sgu_w", "sgu_b", "w_proj_a", "shift_b", "w_lora_w", "w0",
           "a_lora_w", "a0", "g_lora_w", "k_k", "k_a", "r_k", "ln_x_w", "ln_x_b", "w_proj_b", "w_out", "g_ffn",
           "w_ffn1", "w_ffn2", "g_final"]


def _size(shape):
    n = 1
    for s in shape:
        n *= s
    return n


def _pack_rows(parts, rows, dtype):
    flat = jnp.concatenate([p.reshape(-1).astype(dtype) for p in parts])
    return jnp.concatenate([flat, jnp.zeros((rows * 1024 - flat.shape[0],), dtype)]).reshape(rows, 1024)


def _unpack_rows(packed, shapes):
    flat = packed.reshape(-1)
    out, off = [], 0
    for shp in shapes:
        out.append(flat[off:off + _size(shp)].reshape(shp))
        off += _size(shp)
    return out


def _shard_of(name, full, j):
    ax = SHARD_AXIS[name]
    n = SHARD_SHAPE[name][ax]
    return lax.slice_in_dim(full, j * n, (j + 1) * n, axis=ax)


def _pad_cols(z, n):
    return jnp.concatenate([z, jnp.zeros((z.shape[0], n - z.shape[1]), z.dtype)], axis=1)


def _row_form(name, s):
    return s.T if name == "w_in" else s


def _half_piece(name, rf, h):
    if name in LORAS:
        r = PIECE_ROWS[name]
        return _pad_cols(rf[h * r:(h + 1) * r], HALF_W)
    return rf[:, HALF_W * h:HALF_W * (h + 1)]


def _pack_half(group, rf_fn, h, dtype, tail=()):
    parts, pos, rows = [], 0, PACK_ROWS
    for n in group:
        if PIECE_OFF[n] > pos:
            parts.append(jnp.zeros((PIECE_OFF[n] - pos, HALF_W), dtype))
        parts.append(_half_piece(n, rf_fn(n), h).astype(dtype))
        pos = PIECE_OFF[n] + PIECE_ROWS[n]
    for t in tail:
        parts.append(t)
        pos += t.shape[0]
    parts.append(jnp.zeros((rows - pos, HALF_W), dtype))
    return jnp.concatenate(parts, axis=0)


def _piece(pack, name):
    return pack[PIECE_OFF[name]:PIECE_OFF[name] + PIECE_ROWS[name]]


def _join_halves(name, p0, p1):
    if name in LORAS:
        return jnp.concatenate([p0[:, :SHARD_SHAPE[name][1]], p1[:, :SHARD_SHAPE[name][1]]], axis=0)
    return jnp.concatenate([p0, p1], axis=1)


def _grad_row_form(name, full, j):
    if name == "w_in":
        return full[SHARD_SHAPE[name][1] * j:SHARD_SHAPE[name][1] * (j + 1)]
    return _shard_of(name, full, j)


def adamw_weight(name, g_own, g_other, w, m, v):
    rows, width = w.shape
    if name in LORAS:
        tm = PIECE_ROWS[name]
        grid = (2, 1)
        native = pl.BlockSpec((tm, width), lambda h, i: (h, 0))
    elif name == "w_in":
        tm, lanes = rows, 128
        grid = (2, HALF_W // lanes)
        native = pl.BlockSpec((tm, lanes), lambda h, i: (0, h * (HALF_W // lanes) + i))
    else:
        tm = 128
        grid = (2, rows // tm)
        native = pl.BlockSpec((tm, HALF_W), lambda h, i: (i, h))
    off = PIECE_OFF[name] // tm
    if name == "w_in":
        packed = pl.BlockSpec((tm, 128), lambda h, i: (0, i))
    else:
        packed = pl.BlockSpec((tm, HALF_W), lambda h, i: (off + i, 0))

    def body(go_ref, gx_ref, w_ref, m_ref, v_ref, g_ref, d_ref, nm_ref, nv_ref):
        g = jnp.where(pl.program_id(0) == lax.axis_index("c"), go_ref[...], gx_ref[...])[:, :w_ref.shape[1]]
        delta, nm, nv = f_adamw(g, w_ref[...], m_ref[...], v_ref[...])
        g_ref[...] = g
        d_ref[...] = delta
        nm_ref[...] = nm
        nv_ref[...] = nv

    return pl.pallas_call(
        body,
        name="adamw_" + name,
        grid=grid,
        in_specs=[packed, packed, native, native, native],
        out_specs=[native] * 4,
        out_shape=[jax.ShapeDtypeStruct(w.shape, F32)] * 4,
        compiler_params=_cparams(2),
    )(g_own, g_other, w, m, v)


def kernel(x, g_mix, w_in, sgu_ln_w, sgu_ln_b, sgu_w, sgu_b, w_proj_a, shift_b, w_lora_w, w0, a_lora_w, a0, g_lora_w, k_k, k_a, r_k, ln_x_w, ln_x_b, w_proj_b, w_out, g_ffn, w_ffn1, w_ffn2, g_final, loss_target, m_g_mix, m_w_in, m_sgu_ln_w, m_sgu_ln_b, m_sgu_w, m_sgu_b, m_w_proj_a, m_shift_b, m_w_lora_w, m_w0, m_a_lora_w, m_a0, m_g_lora_w, m_k_k, m_k_a, m_r_k, m_ln_x_w, m_ln_x_b, m_w_proj_b, m_w_out, m_g_ffn, m_w_ffn1, m_w_ffn2, m_g_final, v_g_mix, v_w_in, v_sgu_ln_w, v_sgu_ln_b, v_sgu_w, v_sgu_b, v_w_proj_a, v_shift_b, v_w_lora_w, v_w0, v_a_lora_w, v_a0, v_g_lora_w, v_k_k, v_k_a, v_r_k, v_ln_x_w, v_ln_x_b, v_w_proj_b, v_w_out, v_g_ffn, v_w_ffn1, v_w_ffn2, v_g_final):
    given = dict(zip(WEIGHTS, (g_mix, w_in, sgu_ln_w, sgu_ln_b, sgu_w, sgu_b, w_proj_a, shift_b, w_lora_w, w0, a_lora_w, a0, g_lora_w, k_k, k_a, r_k, ln_x_w, ln_x_b, w_proj_b, w_out, g_ffn, w_ffn1, w_ffn2, g_final)))
    mom_m = dict(zip(WEIGHTS, (m_g_mix, m_w_in, m_sgu_ln_w, m_sgu_ln_b, m_sgu_w, m_sgu_b, m_w_proj_a, m_shift_b, m_w_lora_w, m_w0, m_a_lora_w, m_a0, m_g_lora_w, m_k_k, m_k_a, m_r_k, m_ln_x_w, m_ln_x_b, m_w_proj_b, m_w_out, m_g_ffn, m_w_ffn1, m_w_ffn2, m_g_final)))
    mom_v = dict(zip(WEIGHTS, (v_g_mix, v_w_in, v_sgu_ln_w, v_sgu_ln_b, v_sgu_w, v_sgu_b, v_w_proj_a, v_shift_b, v_w_lora_w, v_w0, v_a_lora_w, v_a0, v_g_lora_w, v_k_k, v_k_a, v_r_k, v_ln_x_w, v_ln_x_b, v_w_proj_b, v_w_out, v_g_ffn, v_w_ffn1, v_w_ffn2, v_g_final)))
    chip = 2 * lax.axis_index("x") + lax.axis_index("y")

    def local_block(tree, n):
        return tree[n] if n == "g_final" else tree[n][0]

    sb = local_block(given, "shift_b")
    lo_part = lambda z: (z - z.astype(BF16).astype(F32)).astype(BF16)
    row_form = lambda tree: (lambda n: _row_form(n, local_block(tree, n)))
    tile16 = lambda z: jnp.pad(z, ((0, 16 - z.shape[0]), (0, HALF_W - z.shape[1])))
    sb_tiles = [tile16(f(sb[:, lanes])) for f in (lambda z: z.astype(BF16), lo_part)
                for lanes in (slice(0, HALF_W), slice(HALF_W, None))]
    tails = [[_half_piece(n, lo_part(local_block(given, n)), h) for n in LORAS] + sb_tiles for h in range(2)]
    pack_w = jnp.stack([_pack_half(EARLY, row_form(given), h, BF16, tails[h]) for h in range(2)])
    gathered = gather_shards(pack_w)
    gathered = lax.dynamic_update_index_in_dim(gathered, pack_w, chip, 0)
    pack_late = jnp.stack([_pack_half(LATE, row_form(given), h, BF16) for h in range(2)])

    def whole(group, got):
        shard = lambda n, j: _join_halves(n, _piece(got[j, 0], n), _piece(got[j, 1], n))
        return {n: jnp.concatenate([shard(n, j).astype(F32 if n == "w_in" else BF16) for j in range(N_CHIPS)],
                                   axis=0 if n == "w_in" else SHARD_AXIS[n]) for n in group}

    w = whole(EARLY, gathered)
    late_weights = lambda got: whole(LATE, lax.dynamic_update_index_in_dim(got, pack_late, chip, 0))
    off = LO_OFF
    for n in LORAS:
        r, cols = PIECE_ROWS[n], SHARD_SHAPE[n][1]
        lo = jnp.concatenate([jnp.concatenate([gathered[j, 0, off:off + r, :cols], gathered[j, 1, off:off + r, :cols]],
                                              axis=0) for j in range(N_CHIPS)], axis=1)
        w[n] = w[n].astype(F32) + lo.astype(F32)
        off += r
    sb_tile = lambda j, t, lanes: gathered[j, 0, off + 16 * t:off + 16 * t + 2, :lanes].astype(F32)
    rest = SHIFT_SHARD[1] - HALF_W
    w["shift_b"] = jnp.concatenate(
        [jnp.concatenate([sb_tile(j, 0, HALF_W) + sb_tile(j, 2, HALF_W), sb_tile(j, 1, rest) + sb_tile(j, 3, rest)],
                         axis=1) for j in range(N_CHIPS)], axis=1)
    for n in SMALL:
        w[n] = local_block(given, n).reshape(SMALL_SHAPE[n])

    def partials(g_pack, tag):
        return pair_sum(g_pack, reduce_pair(g_pack, tag), tag, tm=PACK_TILE)

    loss, grad_x, grads, (late_part, late_slots) = local_step(
        x[0], loss_target[0], w, pack_late, late_weights, lambda g_pack: partials(g_pack, "late"))
    loss = lax.psum(loss, ("x", "y", "c"))

    early_g = jnp.stack([jnp.stack([_pack_half(EARLY, lambda n: _grad_row_form(n, grads[n], j), h, F32)
                                    for h in range(2)]) for j in range(N_CHIPS)])
    early_part, early_part16 = partials(early_g, "early")
    my_chip = lambda: 2 * lax.axis_index("x") + lax.axis_index("y")
    out_g, out_d, out_m, out_v = {}, {}, {}, {}
    for group, tag, part, slots in ((LATE, "late", late_part, late_slots),
                                    (EARLY, "early", early_part, reduce_chips(early_part16))):
        half_sum = sum_with_own(part, slots, my_chip, tm=PACK_TILE, name="chip_sum_" + tag)
        other_half = exchange_halves(half_sum, tag)
        for n in group:
            res = adamw_weight(n, half_sum, other_half,
                               *[_row_form(n, local_block(t, n)) for t in (given, mom_m, mom_v)])
            for tree, z in zip((out_g, out_d, out_m, out_v), res):
                tree[n] = _row_form(n, z)

    small_shapes = [SMALL_SHAPE[n] for n in SMALL]
    s_pack = _pack_rows([grads[n] for n in SMALL] + [grads["shift_b"]], SMALL_ROWS, F32)
    g_small = sum_all(s_pack)
    w_small = _pack_rows([local_block(given, n) for n in SMALL], SMALL_ROWS, F32)
    m_small = _pack_rows([local_block(mom_m, n) for n in SMALL], SMALL_ROWS, F32)
    v_small = _pack_rows([local_block(mom_v, n) for n in SMALL], SMALL_ROWS, F32)
    d_small, nm_small, nv_small = adamw_call(g_small, w_small, m_small, v_small, tm=SMALL_ROWS, name="adamw_small")
    g_parts = _unpack_rows(g_small, small_shapes + [(2, N_RWKV)])
    out_g.update(zip(SMALL, g_parts[:-1]))
    out_d.update(zip(SMALL, _unpack_rows(d_small, small_shapes)))
    out_m.update(zip(SMALL, _unpack_rows(nm_small, small_shapes)))
    out_v.update(zip(SMALL, _unpack_rows(nv_small, small_shapes)))
    g_sb = lax.dynamic_slice_in_dim(g_parts[-1], chip * SHIFT_SHARD[1], SHIFT_SHARD[1], axis=1)
    sb_args = [_pack_rows([z], 8, F32) for z in (g_sb, sb, local_block(mom_m, "shift_b"), local_block(mom_v, "shift_b"))]
    sb_res = adamw_call(*sb_args, tm=8, name="adamw_shift_b")
    out_g["shift_b"] = g_sb
    for tree, res in zip((out_d, out_m, out_v), sb_res):
        tree["shift_b"] = _unpack_rows(res, [SHIFT_SHARD])[0]

    def block_of(tree, n):
        return tree[n].reshape(given[n].shape)

    return (loss, grad_x[None], *[block_of(out_g, n) for n in WEIGHTS], *[block_of(out_d, n) for n in WEIGHTS],
            *[block_of(out_m, n) for n in WEIGHTS], *[block_of(out_v, n) for n in WEIGHTS])
```

```python
import functools

import jax
import jax.numpy as jnp
from jax import lax
from jax.experimental import pallas as pl
from jax.experimental.pallas import tpu as pltpu

F32 = jnp.float32
BF16 = jnp.bfloat16

D_MODEL = 1024
N_HEADS = 16
HEAD = 64
SCAN_CHUNK = 64

VMEM_LIMIT = 56 * 1024 * 1024


_BDIMS = {
    "nn": (((2,), (1,)), ((0,), (0,))),
    "nt": (((2,), (2,)), ((0,), (0,))),
    "tn": (((1,), (1,)), ((0,), (0,))),
}


def _raw_bdot(x, y, mode, fine):
    if fine:
        return lax.dot_general(x, y, _BDIMS[mode], precision=lax.Precision.HIGH, preferred_element_type=F32)
    return lax.dot_general(x.astype(BF16), y.astype(BF16), _BDIMS[mode], preferred_element_type=F32)


@functools.partial(jax.custom_vjp, nondiff_argnums=(2, 3))
def bdot(x, y, mode, fine=True):
    return _raw_bdot(x, y, mode, fine)


def _bdot_fwd(x, y, mode, fine):
    return _raw_bdot(x, y, mode, fine), (x, y)


def _bdot_bwd(mode, fine, res, g):
    x, y = res
    if mode == "nn":
        return bdot(g, y, "nt", fine), bdot(x, g, "tn", fine)
    if mode == "nt":
        return bdot(g, y, "nn", fine), bdot(g, x, "tn", fine)
    return bdot(y, g, "nt", fine), bdot(x, g, "nn", fine)


bdot.defvjp(_bdot_fwd, _bdot_bwd)


def _scan_chunk(S0, r, lw, k, v, a, b):
    nh, lc, _ = r.shape
    ti = lax.broadcasted_iota(jnp.int32, (lc, lc), 0)
    si = lax.broadcasted_iota(jnp.int32, (lc, lc), 1)
    incl = (si <= ti).astype(F32)
    strict = (si < ti).astype(F32)
    eye = (si == ti).astype(F32)
    cl = bdot(jnp.broadcast_to(incl, (nh, lc, lc)), lw, "nn")
    cl_last = cl[:, lc - 1:lc, :]
    g_last = jnp.exp(cl_last - cl)
    at = a * jnp.exp(cl - lw)
    bt = b * jnp.exp(-cl)
    kt = k * jnp.exp(-cl)
    rt = r * jnp.exp(cl)
    ar = jnp.concatenate([at, rt], axis=1)
    ar_b = bdot(ar, bt, "nt", False)
    ar_k = bdot(ar, kt, "nt", False)
    m_ab, m_rb = ar_b[:, :lc] * strict, ar_b[:, lc:] * incl
    m_ak, m_rk = ar_k[:, :lc] * strict, ar_k[:, lc:] * incl
    x = eye + m_ab
    p = bdot(m_ab, m_ab, "nn", False)
    n = 2
    while n * 2 < lc:
        px = bdot(jnp.concatenate([p, x], axis=1), p, "nn", False)
        p = px[:, :lc]
        x = x + px[:, lc:]
        n *= 2
    x = x + bdot(x, p, "nn", False)
    ar_s = bdot(ar, S0, "nt", False)
    akrk_v = bdot(jnp.concatenate([m_ak, m_rk], axis=1), v, "nn", False)
    u = bdot(x, ar_s[:, :lc] + akrk_v[:, :lc], "nn", False)
    o = ar_s[:, lc:] + bdot(m_rb, u, "nn", False) + akrk_v[:, lc:]
    s_last = S0 * jnp.exp(cl_last) + bdot(jnp.concatenate([u, v], axis=1),
                                          jnp.concatenate([b * g_last, k * g_last], axis=1), "tn", False)
    return o, s_last


def _split_heads(z):
    return jnp.stack([z[:, HEAD * h:HEAD * (h + 1)] for h in range(N_HEADS)], axis=0)


def _merge_heads(z):
    return jnp.concatenate([z[h] for h in range(N_HEADS)], axis=1)


def _scan_specs(t, ops, rev):
    nc = t // SCAN_CHUNK
    row = (lambda c: nc - 1 - c) if rev else (lambda c: c)
    specs = [pl.BlockSpec((SCAN_CHUNK, D_MODEL), lambda c, cb=cb: (row(c), cb)) for _, cb in ops]
    state = pl.BlockSpec((1, N_HEADS, HEAD, HEAD), lambda c: (row(c), 0, 0, 0))
    return nc, specs, state


def scan_fwd(ops, pack):
    t = ops[0][0].shape[0]
    nc, specs, state = _scan_specs(t, ops, False)

    def body(r_ref, lw_ref, k_ref, v_ref, a_ref, b_ref, pack_ref, o_ref, s0_ref, all_ref, s_scr, send_sems, recv_sems):
        step = pl.program_id(0)
        x, y, c = _coords()
        me = 2 * x + y
        sib = (x, y, 1 - c)
        chips = _other_chips(x, y)
        first = [_remote(pack_ref.at[c], all_ref.at[me, c], send_sems, recv_sems, k, (cx, cy, c))
                 for k, (cx, cy) in enumerate(chips)]
        passed = [_remote(all_ref.at[2 * cx + cy, c], all_ref.at[2 * cx + cy, c], send_sems, recv_sems, 3 + k, sib)
                  for k, (cx, cy) in enumerate(chips)]

        @pl.when(step == 0)
        def _():
            s_scr[...] = jnp.zeros_like(s_scr)
            for cp in first:
                cp.start()

        s0 = s_scr[...]
        s0_ref[0] = s0
        o, s_last = _scan_chunk(s0, *[_split_heads(z[...]) for z in (r_ref, lw_ref, k_ref, v_ref, a_ref, b_ref)])
        o_ref[...] = _merge_heads(o)
        s_scr[...] = s_last

        @pl.when(step == nc - 1)
        def _():
            for k, (cx, cy) in enumerate(chips):
                j = 2 * cx + cy
                _remote(pack_ref.at[c], all_ref.at[j, c], send_sems, recv_sems, k, (cx, cy, c)).wait_recv()
                passed[k].start()
            for k, (cx, cy) in enumerate(chips):
                j = 2 * cx + cy
                _remote(all_ref.at[j, 1 - c], all_ref.at[j, 1 - c], send_sems, recv_sems, 3 + k, sib).wait_recv()
            for cp in first + passed:
                cp.wait_send()

    return pl.pallas_call(
        body,
        name="scan_fwd",
        grid=(nc,),
        in_specs=specs + [_ANY],
        out_specs=[pl.BlockSpec((SCAN_CHUNK, D_MODEL), lambda c: (c, 0)), state, _ANY],
        out_shape=[jax.ShapeDtypeStruct((t, D_MODEL), F32), jax.ShapeDtypeStruct((nc, N_HEADS, HEAD, HEAD), F32),
                   jax.ShapeDtypeStruct((N_CHIPS,) + pack.shape, pack.dtype)],
        scratch_shapes=[pltpu.VMEM((N_HEADS, HEAD, HEAD), F32), pltpu.SemaphoreType.DMA((6,)),
                        pltpu.SemaphoreType.DMA((6,))],
        compiler_params=_cparams(1),
    )(*[a for a, _ in ops], pack)


def scan_bwd(ops, s0s, do, part):
    t = ops[0][0].shape[0]
    nc, specs, state = _scan_specs(t, ops + [(do, 0)], True)

    def body(r_ref, lw_ref, k_ref, v_ref, a_ref, b_ref, do_ref, s0_ref, part_ref, *rest):
        out_refs, slots_ref, ds_scr, send_sems, recv_sems = rest[:6], rest[6], rest[7], rest[8], rest[9]
        step = pl.program_id(0)
        x, y, c = _coords()
        me = 2 * x + y
        chips = _other_chips(x, y)
        sends = [_remote(part_ref.at[2 * cx + cy], slots_ref.at[me], send_sems, recv_sems, k, (cx, cy, c))
                 for k, (cx, cy) in enumerate(chips)]

        @pl.when(step == 0)
        def _():
            ds_scr[...] = jnp.zeros_like(ds_scr)
            for cp in sends:
                cp.start()

        _, vjp = jax.vjp(_scan_chunk, s0_ref[0],
                         *[_split_heads(z[...]) for z in (r_ref, lw_ref, k_ref, v_ref, a_ref, b_ref)])
        grads = vjp((_split_heads(do_ref[...]), ds_scr[...]))
        for o_ref, g in zip(out_refs, grads[1:]):
            o_ref[...] = _merge_heads(g)
        ds_scr[...] = grads[0]

        @pl.when(step == nc - 1)
        def _():
            for k, (cx, cy) in enumerate(chips):
                _remote(part_ref.at[me], slots_ref.at[2 * cx + cy], send_sems, recv_sems, k, (cx, cy, c)).wait_recv()
            for cp in sends:
                cp.wait_send()

    return pl.pallas_call(
        body,
        name="scan_bwd",
        grid=(nc,),
        in_specs=specs + [state, _ANY],
        out_specs=[pl.BlockSpec((SCAN_CHUNK, D_MODEL), lambda c: (nc - 1 - c, 0))] * 6 + [_ANY],
        out_shape=[jax.ShapeDtypeStruct((t, D_MODEL), F32)] * 6 + [jax.ShapeDtypeStruct(part.shape, part.dtype)],
        scratch_shapes=[pltpu.VMEM((N_HEADS, HEAD, HEAD), F32), pltpu.SemaphoreType.DMA((3,)),
                        pltpu.SemaphoreType.DMA((3,))],
        compiler_params=_cparams(1),
    )(*[a for a, _ in ops], do, s0s, part)


_MDIMS = {
    "nn": (((1,), (0,)), ((), ())),
    "nt": (((1,), (1,)), ((), ())),
    "tn": (((0,), (0,)), ((), ())),
}


def _raw_mdot(x, y, mode, exact):
    if exact:
        return lax.dot_general(x, y, _MDIMS[mode], precision=lax.Precision.HIGH, preferred_element_type=F32)
    return lax.dot_general(x.astype(BF16), y.astype(BF16), _MDIMS[mode], preferred_element_type=F32)


@functools.partial(jax.custom_vjp, nondiff_argnums=(2, 3))
def mdot(x, y, mode, exact):
    return _raw_mdot(x, y, mode, exact)


def _mdot_fwd(x, y, mode, exact):
    return _raw_mdot(x, y, mode, exact), (x, y)


def _mdot_bwd(mode, exact, res, g):
    x, y = res
    if mode == "nn":
        return mdot(g, y, "nt", exact), mdot(x, g, "tn", exact)
    if mode == "nt":
        return mdot(g, y, "nn", exact), mdot(g, x, "tn", exact)
    return mdot(y, g, "nt", exact), mdot(x, g, "nn", exact)


mdot.defvjp(_mdot_fwd, _mdot_bwd)


def _seg_ones():
    i = lax.broadcasted_iota(jnp.int32, (256, 256), 0) // HEAD
    j = lax.broadcasted_iota(jnp.int32, (256, 256), 1) // HEAD
    return (i == j).astype(BF16)


@jax.custom_vjp
def segsum(x):
    bd = _seg_ones()
    hi = x.astype(BF16)
    lo = (x - hi.astype(F32)).astype(BF16)
    cols = []
    for j in range(x.shape[1] // 256):
        sl = slice(256 * j, 256 * (j + 1))
        cols.append(jnp.dot(hi[:, sl], bd, preferred_element_type=F32)
                    + jnp.dot(lo[:, sl], bd, preferred_element_type=F32))
    return jnp.concatenate(cols, axis=1)


segsum.defvjp(lambda x: (segsum(x), None), lambda _, g: (segsum(g),))


NORM_EPS = 1e-6
LN_EPS = 1e-5
GN_EPS = 64e-5
SGU_CHUNK = 128
SGU_GROUPS = 8


def _rms(x, g):
    return x * lax.rsqrt(jnp.mean(x * x, axis=-1, keepdims=True) + NORM_EPS) * g


def f_norm_in(x, g):
    return _rms(x, g), x


def f_sgu(p, ln_w, ln_b, sw, sbt):
    tm = p.shape[0]
    z = 0.5 * p * (1.0 + lax.erf(p * 0.7071067811865476))
    u, v = z[:, :D_MODEL], z[:, D_MODEL:]
    mu = jnp.mean(v, axis=-1, keepdims=True)
    d = v - mu
    vn = d * lax.rsqrt(jnp.mean(d * d, axis=-1, keepdims=True) + LN_EPS) * ln_w + ln_b
    ii = lax.broadcasted_iota(jnp.int32, (SGU_CHUNK, SGU_CHUNK), 0)
    jj = lax.broadcasted_iota(jnp.int32, (SGU_CHUNK, SGU_CHUNK), 1)
    mask = (jj <= ii).astype(F32)
    gi = lax.broadcasted_iota(jnp.int32, (SGU_GROUPS, D_MODEL), 0)
    ci = lax.broadcasted_iota(jnp.int32, (SGU_GROUPS, D_MODEL), 1) // SGU_CHUNK
    bias = mdot(sbt, (gi == ci).astype(F32), "nn", True)
    rows = []
    for c in range(tm // SGU_CHUNK):
        cols = []
        for g in range(SGU_GROUPS):
            blk = vn[c * SGU_CHUNK:(c + 1) * SGU_CHUNK, g * SGU_CHUNK:(g + 1) * SGU_CHUNK]
            cols.append(mdot(sw[g] * mask, blk, "nn", False))
        rows.append(jnp.concatenate(cols, axis=1) + bias)
    return (u * jnp.concatenate(rows, axis=0),)


def _softplus(x):
    return jnp.maximum(x, 0.0) + jnp.log1p(jnp.exp(-jnp.abs(x)))


def f_pre(qr, qk, qv, ql, wl, w0, al, a0, gl, k_k, k_a):
    xw, xa, xg = ql[:, :128], ql[:, 128:256], ql[:, 256:512]
    wr = -_softplus(-(w0 + mdot(jnp.tanh(xw), wl, "nn", True))) - 0.5
    lw = -jnp.exp(wr)
    aa = jax.nn.sigmoid(a0 + mdot(xa, al, "nn", True))
    g = mdot(jax.nn.sigmoid(xg), gl, "nn", True)
    kkr = qk * k_k
    kk = kkr / jnp.maximum(jnp.sqrt(segsum(kkr * kkr)), 1e-12)
    kp = qk * (1.0 + (aa - 1.0) * k_a)
    return qr, lw, kp, qv, -kk, kk * aa, g, qr, kp, qv


def f_post(o, r, kp, v, g, lnw, lnb, rk):
    mu = segsum(o) * (1.0 / HEAD)
    d = o - mu
    gn = d * lax.rsqrt(segsum(d * d) * (1.0 / HEAD) + GN_EPS)
    return ((gn * lnw + lnb + segsum(r * kp * rk) * v) * g,)


def f_mix(ya, yb, ga, gb):
    return (jax.nn.sigmoid(ga) * ya + jax.nn.sigmoid(gb) * yb,)


def f_ffn_in(h1, g):
    return _rms(h1, g), h1


def f_final(h1, m3, tgt, g):
    y = _rms(h1 + m3, g)
    err = jnp.square(y - tgt)
    return 0.5 * jnp.sum(jnp.mean(err, axis=-1))


def _cparams(n_grid):
    return pltpu.CompilerParams(dimension_semantics=("arbitrary",) * n_grid, vmem_limit_bytes=VMEM_LIMIT)


def _tile_spec(tm, w, cb):
    return pl.BlockSpec((tm, w), lambda i: (i, cb))


def _const_spec(c):
    nd = c.ndim
    return pl.BlockSpec(c.shape, lambda i: (0,) * nd)


def ew_call(fn, tiled, consts, outs, *, tm, name):
    t = tiled[0][0].shape[0]
    n_t, n_c = len(tiled), len(consts)

    def body(*refs):
        tv = [r[...].astype(F32) for r in refs[:n_t]]
        cv = [r[...] for r in refs[n_t:n_t + n_c]]
        res = fn(*tv, *cv)
        for o_ref, val in zip(refs[n_t + n_c:], res):
            o_ref[...] = val.astype(o_ref.dtype)

    return pl.pallas_call(
        body,
        name=name,
        grid=(t // tm,),
        in_specs=[_tile_spec(tm, w, cb) for _, w, cb in tiled] + [_const_spec(c) for c in consts],
        out_specs=[_tile_spec(tm, w, 0) for w, _ in outs],
        out_shape=[jax.ShapeDtypeStruct((t, w), dt) for w, dt in outs],
        compiler_params=_cparams(1),
    )(*[a for a, _, _ in tiled], *consts)


def ew_vjp_call(fn, tiled, consts, cots, d_tiled, d_consts, *, tm, name):
    t = tiled[0][0].shape[0]
    n_t, n_c, n_g = len(tiled), len(consts), len(cots)
    dt_list = [(i, dt) for i, dts in enumerate(d_tiled) for dt in dts]
    dc_list = [i for i, want in enumerate(d_consts) if want]

    def body(*refs):
        tv = [r[...].astype(F32) for r in refs[:n_t]]
        cv = [r[...] for r in refs[n_t:n_t + n_c]]
        gv = tuple(r[...].astype(F32) for r in refs[n_t + n_c:n_t + n_c + n_g])
        out_refs = refs[n_t + n_c + n_g:]
        _, vjp = jax.vjp(fn, *tv, *cv)
        grads = vjp(gv)
        for o_ref, (i, _) in zip(out_refs, dt_list):
            o_ref[...] = grads[i].astype(o_ref.dtype)
        acc_refs = out_refs[len(dt_list):]

        @pl.when(pl.program_id(0) == 0)
        def _():
            for a_ref in acc_refs:
                a_ref[...] = jnp.zeros_like(a_ref)

        for a_ref, i in zip(acc_refs, dc_list):
            a_ref[...] += grads[n_t + i]

    res = pl.pallas_call(
        body,
        name=name,
        grid=(t // tm,),
        in_specs=[_tile_spec(tm, w, cb) for _, w, cb in tiled] + [_const_spec(c) for c in consts]
        + [_tile_spec(tm, w, cb) for _, w, cb in cots],
        out_specs=[_tile_spec(tm, tiled[i][1], 0) for i, _ in dt_list] + [_const_spec(consts[i]) for i in dc_list],
        out_shape=[jax.ShapeDtypeStruct((t, tiled[i][1]), dt) for i, dt in dt_list]
        + [jax.ShapeDtypeStruct(consts[i].shape, F32) for i in dc_list],
        compiler_params=_cparams(1),
    )(*[a for a, _, _ in tiled], *consts, *[a for a, _, _ in cots])
    return res[:len(dt_list)], res[len(dt_list):]


def mm(a, b, mode, *, tm, tn, name, out_dtypes=(F32,), epi=None, extras=(), into=None):
    m = a.shape[1] if mode == "tn" else a.shape[0]
    kd = a.shape[0] if mode == "tn" else a.shape[1]
    n = b.shape[0] if mode == "nt" else b.shape[1]
    tm, tn = min(tm, m), min(tn, n)
    if mode == "nn":
        a_spec = pl.BlockSpec((tm, kd), lambda i, j: (i, 0))
        b_spec = pl.BlockSpec((kd, tn), lambda i, j: (0, j))
    elif mode == "nt":
        a_spec = pl.BlockSpec((tm, kd), lambda i, j: (i, 0))
        b_spec = pl.BlockSpec((tn, kd), lambda i, j: (j, 0))
    else:
        a_spec = pl.BlockSpec((kd, tm), lambda i, j: (0, i))
        b_spec = pl.BlockSpec((kd, tn), lambda i, j: (0, j))
    n_e = len(extras)
    o_spec = pl.BlockSpec((tm, tn), lambda i, j: (i, j))

    if into is not None:
        buf, place = into

        def body_into(a_ref, b_ref, buf_ref, o_ref):
            o_ref[0, 0] = lax.dot_general(a_ref[...].astype(BF16), b_ref[...].astype(BF16), _MDIMS[mode],
                                          preferred_element_type=F32)

        return pl.pallas_call(
            body_into,
            name=name,
            grid=(m // tm, n // tn),
            in_specs=[a_spec, b_spec, pl.BlockSpec(memory_space=pl.ANY)],
            out_specs=pl.BlockSpec((1, 1, tm, tn), lambda i, j: (*place(i, j), 0)),
            out_shape=jax.ShapeDtypeStruct(buf.shape, F32),
            input_output_aliases={2: 0},
            compiler_params=_cparams(2),
        )(a, b, buf)

    def body(a_ref, b_ref, *refs):
        c = lax.dot_general(a_ref[...].astype(BF16), b_ref[...].astype(BF16), _MDIMS[mode],
                            preferred_element_type=F32)
        res = epi(c, *[r[...] for r in refs[:n_e]]) if epi is not None else (c,)
        for o_ref, val in zip(refs[n_e:], res):
            o_ref[...] = val.astype(o_ref.dtype)

    res = pl.pallas_call(
        body,
        name=name,
        grid=(m // tm, n // tn),
        in_specs=[a_spec, b_spec] + [o_spec] * n_e,
        out_specs=[o_spec] * len(out_dtypes),
        out_shape=[jax.ShapeDtypeStruct((m, n), dt) for dt in out_dtypes],
        compiler_params=_cparams(2),
    )(a, b, *extras)
    return res if len(out_dtypes) > 1 else res[0]


P_WIDTH = 7680
RWKV_COL0 = 4096
RWKV_WIDTH = 3584
SHIFT_BLK = 512


def _shift_down(p, prev_row):
    rows = lax.broadcasted_iota(jnp.int32, p.shape, 0)
    return jnp.where(rows == 0, prev_row, pltpu.roll(p, 1, 0))


def shiftmix_fwd(p_all, sbp, *, tm):
    t = p_all.shape[0]
    tm = min(tm, t)
    c0 = RWKV_COL0 // SHIFT_BLK
    hb = tm // 8

    def body(p_ref, halo_ref, sb_ref, q_ref):
        p = p_ref[...]
        prev = jnp.where(pl.program_id(0) == 0, 0.0, halo_ref[7:8, :])
        q_ref[...] = p * sb_ref[0:1, :] + _shift_down(p, prev) * sb_ref[1:2, :]

    return pl.pallas_call(
        body,
        name="shiftmix_fwd",
        grid=(t // tm, RWKV_WIDTH // SHIFT_BLK),
        in_specs=[
            pl.BlockSpec((tm, SHIFT_BLK), lambda i, j: (i, c0 + j)),
            pl.BlockSpec((8, SHIFT_BLK), lambda i, j: (jnp.maximum(i * hb - 1, 0), c0 + j)),
            pl.BlockSpec((2, SHIFT_BLK), lambda i, j: (0, j)),
        ],
        out_specs=pl.BlockSpec((tm, SHIFT_BLK), lambda i, j: (i, j)),
        out_shape=jax.ShapeDtypeStruct((t, RWKV_WIDTH), F32),
        compiler_params=_cparams(2),
    )(p_all, p_all, sbp)


def shiftmix_bwd(dq, col0, p_all, sbp, *, tm, name):
    t, w = dq.shape
    n_i = t // tm
    hb = tm // 8
    cq = col0 // SHIFT_BLK
    cp = (RWKV_COL0 + col0) // SHIFT_BLK

    def body(dq_ref, dqn_ref, p_ref, ph_ref, sb_ref, dp_ref, dsb_ref):
        i = pl.program_id(1)
        dq_t = dq_ref[...]
        rows = lax.broadcasted_iota(jnp.int32, dq_t.shape, 0)
        nxt = jnp.where(i == n_i - 1, 0.0, dqn_ref[0:1, :])
        up = jnp.where(rows == tm - 1, nxt, pltpu.roll(dq_t, tm - 1, 0))
        dp_ref[...] = (dq_t * sb_ref[0:1, :] + up * sb_ref[1:2, :]).astype(dp_ref.dtype)
        p = p_ref[...]
        prev = jnp.where(i == 0, 0.0, ph_ref[7:8, :])
        s0 = jnp.sum(dq_t * p, axis=0, keepdims=True)
        s1 = jnp.sum(dq_t * _shift_down(p, prev), axis=0, keepdims=True)
        two = lax.broadcasted_iota(jnp.int32, (2, SHIFT_BLK), 0)

        @pl.when(i == 0)
        def _():
            dsb_ref[...] = jnp.zeros_like(dsb_ref)

        dsb_ref[...] += jnp.where(two == 0, s0, s1)

    return pl.pallas_call(
        body,
        name=name,
        grid=(w // SHIFT_BLK, n_i),
        in_specs=[
            pl.BlockSpec((tm, SHIFT_BLK), lambda j, i: (i, j)),
            pl.BlockSpec((8, SHIFT_BLK), lambda j, i: (jnp.minimum((i + 1) * hb, t // 8 - 1), j)),
            pl.BlockSpec((tm, SHIFT_BLK), lambda j, i: (i, cp + j)),
            pl.BlockSpec((8, SHIFT_BLK), lambda j, i: (jnp.maximum(i * hb - 1, 0), cp + j)),
            pl.BlockSpec((2, SHIFT_BLK), lambda j, i: (0, cq + j)),
        ],
        out_specs=[
            pl.BlockSpec((tm, SHIFT_BLK), lambda j, i: (i, j)),
            pl.BlockSpec((2, SHIFT_BLK), lambda j, i: (0, j)),
        ],
        out_shape=[jax.ShapeDtypeStruct((t, w), BF16), jax.ShapeDtypeStruct((2, w), F32)],
        compiler_params=_cparams(2),
    )(dq, dq, p_all, p_all, sbp)


def final_call(h1, m3, tgt, g_final, *, tm):
    t = h1.shape[0]

    def body(h1_ref, m3_ref, tgt_ref, g_ref, dh_ref, dhb_ref, dg_ref, loss_ref):
        loss, vjp = jax.vjp(f_final, h1_ref[...], m3_ref[...], tgt_ref[...], g_ref[...])
        dh, _, _, dg = vjp(jnp.ones((), F32))
        dh_ref[...] = dh
        dhb_ref[...] = dh.astype(BF16)

        @pl.when(pl.program_id(0) == 0)
        def _():
            dg_ref[...] = jnp.zeros_like(dg_ref)
            loss_ref[...] = jnp.zeros_like(loss_ref)

        dg_ref[...] += dg
        loss_ref[...] += jnp.full(loss_ref.shape, loss, F32)

    tile = _tile_spec(tm, D_MODEL, 0)
    return pl.pallas_call(
        body,
        name="final_loss",
        grid=(t // tm,),
        in_specs=[tile, tile, tile, _const_spec(g_final)],
        out_specs=[tile, tile, _const_spec(g_final), pl.BlockSpec((8, 128), lambda i: (0, 0))],
        out_shape=[jax.ShapeDtypeStruct((t, D_MODEL), F32), jax.ShapeDtypeStruct((t, D_MODEL), BF16),
                   jax.ShapeDtypeStruct(g_final.shape, F32), jax.ShapeDtypeStruct((8, 128), F32)],
        compiler_params=_cparams(1),
    )(h1, m3, tgt, g_final)


N_SGU = 2048
N_RWKV = 3360
LORA_W, LORA_A, LORA_G = 64, 64, 160


def _pad_rwkv_cols(z):
    zero = lambda n: jnp.zeros(z.shape[:-1] + (n,), z.dtype)
    return jnp.concatenate([z[..., :3072], z[..., 3072:3136], zero(64), z[..., 3136:3200], zero(64),
                            z[..., 3200:3360], zero(96)], axis=-1)


def _unpad_rwkv_cols(z):
    return jnp.concatenate([z[..., :3072], z[..., 3072:3136], z[..., 3200:3264], z[..., 3328:3488]], axis=-1)


def _pad_win_rows(wt):
    z = wt[N_SGU:N_SGU + N_RWKV]
    zero = lambda n: jnp.zeros((n, wt.shape[1]), wt.dtype)
    return jnp.concatenate([wt[:N_SGU], wt[N_SGU + N_RWKV:], z[:3072], z[3072:3136], zero(64), z[3136:3200], zero(64),
                            z[3200:3360], zero(96)], axis=0)


def _unpad_win_rows(wt):
    z = wt[RWKV_COL0:]
    return jnp.concatenate([wt[:N_SGU], z[:3072], z[3072:3136], z[3200:3264], z[3328:3488], wt[N_SGU:RWKV_COL0]],
                           axis=0)


def _pad_rows(w, n):
    return jnp.concatenate([w, jnp.zeros((n - w.shape[0],) + w.shape[1:], w.dtype)], axis=0)


def _relu2_epi(c):
    return c, jnp.square(jnp.maximum(c, 0.0))


def _relu2_bwd_epi(c, hid):
    return (c * (2.0 * jnp.maximum(hid.astype(F32), 0.0)),)


def _add_epi(c, x):
    return (c + x,)


def _pre_fwd(*args):
    res = f_pre(*args)
    return res[1], res[2], res[4], res[5], res[6]


def local_step(x, tgt, w, late_pack, late_weights, late_partials):
    d = D_MODEL
    win_pt = _pad_win_rows(w["w_in"])
    sbp = _pad_rwkv_cols(w["shift_b"])
    wl = _pad_rows(w["w_lora_w"], 128)
    al = _pad_rows(w["a_lora_w"], 128)
    gl = _pad_rows(w["g_lora_w"], 256)
    sbt = w["sgu_b"].T

    (a_bf,) = ew_call(lambda x_, g_: (f_norm_in(x_, g_)[0],), [(x, d, 0)], [w["g_mix"]], [(d, BF16)], tm=256,
                      name="norm_in")
    p_all = mm(a_bf, win_pt, "nt", tm=2048, tn=640, name="mm_in")
    sgu_t = [(p_all, 2 * d, 0)]
    sgu_c = [w["sgu_ln_w"], w["sgu_ln_b"], w["sgu_w"], sbt]
    (s_bf,) = ew_call(f_sgu, sgu_t, sgu_c, [(d, BF16)], tm=256, name="sgu_fwd")
    ya = mm(s_bf, w["w_proj_a"], "nn", tm=512, tn=1024, name="mm_proj_a")
    q = shiftmix_fwd(p_all, sbp, tm=1024)
    pre_t = [(q, d, 0), (q, d, 1), (q, d, 2), (q, 512, 6)]
    pre_c = [wl, w["w0"], al, w["a0"], gl, w["k_k"], w["k_a"]]
    lw, kp, na, nb, g = ew_call(_pre_fwd, pre_t, pre_c, [(d, F32)] * 5, tm=256, name="rwkv_pre_fwd")
    scan_ops = [(q, 0), (lw, 0), (kp, 0), (q, 2), (na, 0), (nb, 0)]
    o, s0s, late_all = scan_fwd(scan_ops, late_pack)
    w = {**w, **late_weights(late_all)}
    post_t = [(o, d, 0), (q, d, 0), (kp, d, 0), (q, d, 2), (g, d, 0)]
    post_c = [w["ln_x_w"], w["ln_x_b"], w["r_k"]]
    (ob_bf,) = ew_call(f_post, post_t, post_c, [(d, BF16)], tm=256, name="rwkv_post_fwd")
    yb = mm(ob_bf, w["w_proj_b"], "nn", tm=512, tn=1024, name="mm_proj_b")
    mix_t = [(ya, d, 0), (yb, d, 0), (p_all, d, 2), (p_all, d, 3)]
    (mixed_bf,) = ew_call(f_mix, mix_t, [], [(d, BF16)], tm=256, name="mix_fwd")
    h1 = mm(mixed_bf, w["w_out"], "nn", tm=512, tn=1024, name="mm_out", epi=_add_epi, extras=(x,))
    (f_bf,) = ew_call(lambda h_, g_: (f_ffn_in(h_, g_)[0],), [(h1, d, 0)], [w["g_ffn"]], [(d, BF16)], tm=256,
                      name="ffn_norm")
    hid, act_bf = mm(f_bf, w["w_ffn1"], "nn", tm=2048, tn=1024, name="mm_ffn1", out_dtypes=(BF16, BF16), epi=_relu2_epi)
    m3 = mm(act_bf, w["w_ffn2"], "nn", tm=1024, tn=512, name="mm_ffn2")
    dh2, dh2_bf, dg_final, loss = final_call(h1, m3, tgt, w["g_final"], tm=256)

    dhid_bf = mm(dh2_bf, w["w_ffn2"], "nt", tm=2048, tn=1024, name="mm_dact", out_dtypes=(BF16,), epi=_relu2_bwd_epi,
                 extras=(hid,))
    late_g = lax.empty((N_CHIPS, 2, PACK_ROWS, HALF_W), F32)
    late_g = mm(act_bf, dh2_bf, "tn", tm=512, tn=HALF_W, name="mm_dw_ffn2",
                into=(late_g, lambda i, j: (i // 2, j, PIECE_OFF["w_ffn2"] // 512 + i % 2)))
    df = mm(dhid_bf, w["w_ffn1"], "nt", tm=1024, tn=512, name="mm_df")
    late_g = mm(f_bf, dhid_bf, "tn", tm=512, tn=HALF_W, name="mm_dw_ffn1",
                into=(late_g, lambda i, j: (j // 2, j % 2, PIECE_OFF["w_ffn1"] // 512 + i)))
    (dh1, dh1_bf), (dg_ffn,) = ew_vjp_call(f_ffn_in, [(h1, d, 0)], [w["g_ffn"]], [(df, d, 0), (dh2, d, 0)],
                                           [(F32, BF16)], [True], tm=256, name="ffn_norm_bwd")
    dmixed = mm(dh1_bf, w["w_out"], "nt", tm=512, tn=1024, name="mm_dmixed")
    late_g = mm(mixed_bf, dh1_bf, "tn", tm=256, tn=HALF_W, name="mm_dw_out",
                into=(late_g, lambda i, j: (i, j, PIECE_OFF["w_out"] // 256)))
    (dya_bf, dyb_bf, dga_bf, dgb_bf), _ = ew_vjp_call(f_mix, mix_t, [], [(dmixed, d, 0)], [(BF16,)] * 4, [], tm=256,
                                                      name="mix_bwd")
    dob = mm(dyb_bf, w["w_proj_b"], "nt", tm=512, tn=1024, name="mm_dob")
    late_g = mm(ob_bf, dyb_bf, "tn", tm=256, tn=HALF_W, name="mm_dw_proj_b",
                into=(late_g, lambda i, j: (i, j, PIECE_OFF["w_proj_b"] // 256)))
    (do, dr_p, dkp_p, dv_p, dg), (dlnx_w, dlnx_b, dr_k) = ew_vjp_call(
        f_post, post_t, post_c, [(dob, d, 0)], [(F32,)] * 5, [True] * 3, tm=256, name="rwkv_post_bwd")
    late_part, late_part16 = late_partials(late_g)
    *scan_g, late_slots = scan_bwd(scan_ops, s0s, do, late_part16)
    pre_g = [(z, d, 0) for z in scan_g] + [(dg, d, 0), (dr_p, d, 0), (dkp_p, d, 0), (dv_p, d, 0)]
    (dq_r, dq_k, dq_v, dq_l), (dwl, dw0, dal, da0, dgl, dk_k, dk_a) = ew_vjp_call(
        f_pre, pre_t, pre_c, pre_g, [(F32,)] * 4, [True] * 7, tm=128, name="rwkv_pre_bwd")
    dp_r, dsb_r = shiftmix_bwd(dq_r, 0, p_all, sbp, tm=256, name="shiftmix_bwd_r")
    dp_k, dsb_k = shiftmix_bwd(dq_k, d, p_all, sbp, tm=256, name="shiftmix_bwd_k")
    dp_v, dsb_v = shiftmix_bwd(dq_v, 2 * d, p_all, sbp, tm=256, name="shiftmix_bwd_v")
    dp_l, dsb_l = shiftmix_bwd(dq_l, 3 * d, p_all, sbp, tm=256, name="shiftmix_bwd_l")
    ds = mm(dya_bf, w["w_proj_a"], "nt", tm=512, tn=1024, name="mm_ds")
    d_proj_a = mm(s_bf, dya_bf, "tn", tm=512, tn=1024, name="mm_dw_proj_a")
    (dp_sgu,), (dln_w, dln_b, dsw, dsbt) = ew_vjp_call(f_sgu, sgu_t, sgu_c, [(ds, d, 0)], [(BF16,)], [True] * 4,
                                                       tm=256, name="sgu_bwd")
    dp_all = jnp.concatenate([dp_sgu, dga_bf, dgb_bf, dp_r, dp_k, dp_v, dp_l], axis=1)
    da = mm(dp_all, win_pt, "nn", tm=1024, tn=256, name="mm_da")
    d_in_pt = mm(dp_all, a_bf, "tn", tm=1280, tn=1024, name="mm_dw_in")
    (grad_x,), (dg_mix,) = ew_vjp_call(f_norm_in, [(x, d, 0)], [w["g_mix"]], [(da, d, 0), (dh1, d, 0)], [(F32,)],
                                       [True], tm=256, name="norm_in_bwd")

    grads = {
        "g_mix": dg_mix, "w_in": _unpad_win_rows(d_in_pt), "sgu_ln_w": dln_w, "sgu_ln_b": dln_b, "sgu_w": dsw,
        "sgu_b": dsbt.T, "w_proj_a": d_proj_a,
        "shift_b": _unpad_rwkv_cols(jnp.concatenate([dsb_r, dsb_k, dsb_v, dsb_l], axis=1)),
        "w_lora_w": dwl[:LORA_W], "w0": dw0, "a_lora_w": dal[:LORA_A], "a0": da0, "g_lora_w": dgl[:LORA_G],
        "k_k": dk_k, "k_a": dk_a, "r_k": dr_k, "ln_x_w": dlnx_w, "ln_x_b": dlnx_b, "g_ffn": dg_ffn,
        "g_final": dg_final,
    }
    return loss[0, 0], grad_x, grads, (late_part, late_slots)


MESH = pl.DeviceIdType.MESH
N_CHIPS = 4
N_DEV = 8
PACK_ROWS = 2560
PACK_TILE = 512
SMALL_ROWS = 152
_ANY = pl.BlockSpec(memory_space=pl.ANY)


def _coords():
    return lax.axis_index("x"), lax.axis_index("y"), lax.axis_index("c")


def _other_chips(x, y):
    return [(1 - x, y), (x, 1 - y), (1 - x, 1 - y)]


def _remote(src, dst, send_sems, recv_sems, k, to):
    return pltpu.make_async_remote_copy(src_ref=src, dst_ref=dst, send_sem=send_sems.at[k], recv_sem=recv_sems.at[k],
                                        device_id=to, device_id_type=MESH)


def gather_shards(pack):
    def body(src_ref, out_ref, send_sems, recv_sems):
        x, y, c = _coords()
        me = 2 * x + y
        sib = (x, y, 1 - c)
        chips = _other_chips(x, y)
        first = [_remote(src_ref.at[c], out_ref.at[me, c], send_sems, recv_sems, k, (cx, cy, c))
                 for k, (cx, cy) in enumerate(chips)]
        for cp in first:
            cp.start()
        passed = []
        for k, (cx, cy) in enumerate(chips):
            j = 2 * cx + cy
            _remote(src_ref.at[c], out_ref.at[j, c], send_sems, recv_sems, k, (cx, cy, c)).wait_recv()
            fwd = _remote(out_ref.at[j, c], out_ref.at[j, c], send_sems, recv_sems, 3 + k, sib)
            fwd.start()
            passed.append(fwd)
        for k, (cx, cy) in enumerate(chips):
            j = 2 * cx + cy
            _remote(out_ref.at[j, 1 - c], out_ref.at[j, 1 - c], send_sems, recv_sems, 3 + k, sib).wait_recv()
        for cp in first + passed:
            cp.wait_send()

    return pl.pallas_call(
        body,
        name="gather_shards",
        in_specs=[_ANY],
        out_specs=_ANY,
        out_shape=jax.ShapeDtypeStruct((N_CHIPS,) + pack.shape, pack.dtype),
        scratch_shapes=[pltpu.SemaphoreType.DMA((6,)), pltpu.SemaphoreType.DMA((6,))],
    )(pack)


def reduce_pair(g, tag):
    def body(g_ref, got_ref, send_sems, recv_sems):
        x, y, c = _coords()
        sib = (x, y, 1 - c)
        sends = [_remote(g_ref.at[j, 1 - c], got_ref.at[j], send_sems, recv_sems, j, sib) for j in range(N_CHIPS)]
        for cp in sends:
            cp.start()
        for cp in sends:
            cp.wait_recv()
        for cp in sends:
            cp.wait_send()

    return pl.pallas_call(
        body,
        name="reduce_pair_" + tag,
        in_specs=[_ANY],
        out_specs=_ANY,
        out_shape=jax.ShapeDtypeStruct((N_CHIPS,) + g.shape[2:], g.dtype),
        scratch_shapes=[pltpu.SemaphoreType.DMA((N_CHIPS,)), pltpu.SemaphoreType.DMA((N_CHIPS,))],
    )(g)


def pair_sum(g, got, tag, *, tm):
    n, _, rows, width = g.shape

    def body(g0_ref, g1_ref, got_ref, out_ref, out16_ref):
        own = jnp.where(lax.axis_index("c") == 0, g0_ref[0, 0], g1_ref[0, 0])
        total = own + got_ref[0]
        out_ref[0] = total
        out16_ref[0] = total.astype(BF16)

    blk = pl.BlockSpec((1, tm, width), lambda j, i: (j, i, 0))
    return pl.pallas_call(
        body,
        name="pair_sum_" + tag,
        grid=(n, rows // tm),
        in_specs=[pl.BlockSpec((1, 1, tm, width), lambda j, i: (j, 0, i, 0)),
                  pl.BlockSpec((1, 1, tm, width), lambda j, i: (j, 1, i, 0)), blk],
        out_specs=[blk, blk],
        out_shape=[jax.ShapeDtypeStruct(got.shape, F32), jax.ShapeDtypeStruct(got.shape, BF16)],
        compiler_params=_cparams(2),
    )(g, g, got)


def reduce_chips(p):
    def body(p_ref, out_ref, send_sems, recv_sems):
        x, y, c = _coords()
        me = 2 * x + y
        chips = _other_chips(x, y)
        sends = [_remote(p_ref.at[2 * cx + cy], out_ref.at[me], send_sems, recv_sems, k, (cx, cy, c))
                 for k, (cx, cy) in enumerate(chips)]
        for cp in sends:
            cp.start()
        for k, (cx, cy) in enumerate(chips):
            _remote(p_ref.at[me], out_ref.at[2 * cx + cy], send_sems, recv_sems, k, (cx, cy, c)).wait_recv()
        for cp in sends:
            cp.wait_send()

    return pl.pallas_call(
        body,
        name="reduce_chips",
        in_specs=[_ANY],
        out_specs=_ANY,
        out_shape=jax.ShapeDtypeStruct(p.shape, p.dtype),
        scratch_shapes=[pltpu.SemaphoreType.DMA((3,)), pltpu.SemaphoreType.DMA((3,))],
    )(p)


def _chip_copies(p_ref, slots_ref, send_sems, recv_sems):
    x, y, c = _coords()
    me = 2 * x + y
    return [(_remote(p_ref.at[2 * cx + cy], slots_ref.at[me], send_sems, recv_sems, k, (cx, cy, c)),
             _remote(p_ref.at[me], slots_ref.at[2 * cx + cy], send_sems, recv_sems, k, (cx, cy, c)))
            for k, (cx, cy) in enumerate(_other_chips(x, y))]


def reduce_chips_start(p):
    hbm = pl.BlockSpec(memory_space=pltpu.HBM)
    sem = pl.BlockSpec(memory_space=pltpu.SEMAPHORE)

    def body(p_ref, slots_ref, send_sems, recv_sems, p_thru, slots_thru, token):
        for send, _ in _chip_copies(p_ref, slots_ref, send_sems, recv_sems):
            send.start()
        token[...] = jnp.zeros_like(token)

    return pl.pallas_call(
        body,
        name="reduce_chips_start",
        out_shape=(pltpu.SemaphoreType.DMA((3,)), pltpu.SemaphoreType.DMA((3,)), pltpu.HBM(p.shape, p.dtype),
                   pltpu.HBM(p.shape, p.dtype), jax.ShapeDtypeStruct((8, 128), F32)),
        in_specs=(hbm, hbm),
        out_specs=(sem, sem, hbm, hbm, pl.BlockSpec(memory_space=pltpu.VMEM)),
        input_output_aliases={0: 2, 1: 3},
        compiler_params=pltpu.CompilerParams(has_side_effects=pltpu.SideEffectType.DATAFLOW_SIDE_EFFECTING),
    )(pltpu.with_memory_space_constraint(p, pltpu.HBM),
      pltpu.with_memory_space_constraint(lax.empty(p.shape, p.dtype), pltpu.HBM))


def reduce_chips_wait(send_sems, recv_sems, p_thru, slots_thru, after):
    hbm = pl.BlockSpec(memory_space=pltpu.HBM)
    sem = pl.BlockSpec(memory_space=pltpu.SEMAPHORE)

    def body(p_ref, slots_ref, send_sems, recv_sems, after_ref, p_dead, slots_out):
        for send, arrival in _chip_copies(p_ref, slots_ref, send_sems, recv_sems):
            send.wait_send()
            arrival.wait_recv()

    return pl.pallas_call(
        body,
        name="reduce_chips_wait",
        out_shape=(pltpu.HBM(p_thru.shape, p_thru.dtype), pltpu.HBM(slots_thru.shape, slots_thru.dtype)),
        in_specs=(hbm, hbm, sem, sem, pl.BlockSpec(memory_space=pl.ANY)),
        out_specs=(hbm, hbm),
        input_output_aliases={0: 0, 1: 1},
        compiler_params=pltpu.CompilerParams(has_side_effects=pltpu.SideEffectType.DATAFLOW_SIDE_EFFECTING),
    )(p_thru, slots_thru, send_sems, recv_sems, after)[1]


def sum_with_own(own, slots, index_fn, *, tm, name):
    n, rows, width = slots.shape
    own3 = own.ndim == 3

    def body(*refs):
        mine = index_fn()
        acc = None
        for s in range(n):
            o = refs[s][0] if own3 else refs[0][...]
            term = jnp.where(mine == s, o, refs[(n if own3 else 1) + s][0].astype(F32))
            acc = term if acc is None else acc + term
        refs[-1][...] = acc

    slot_specs = [pl.BlockSpec((1, tm, width), lambda i, s=s: (s, i, 0)) for s in range(n)]
    own_specs = slot_specs if own3 else [pl.BlockSpec((tm, width), lambda i: (i, 0))]
    return pl.pallas_call(
        body,
        name=name,
        grid=(rows // tm,),
        in_specs=own_specs + slot_specs,
        out_specs=pl.BlockSpec((tm, width), lambda i: (i, 0)),
        out_shape=jax.ShapeDtypeStruct((rows, width), F32),
        compiler_params=_cparams(1),
    )(*([own] * (n if own3 else 1)), *([slots] * n))


def exchange_halves(s, tag):
    rq = PACK_TILE
    nq = s.shape[0] // rq

    def body(s_ref, out_ref, sbuf, rbuf, send_sems, recv_sems, in_sems, out_sems):
        x, y, c = _coords()
        sib = (x, y, 1 - c)
        rows = lambda q: pl.ds(q * rq, rq)
        loads = [pltpu.make_async_copy(s_ref.at[rows(q)], sbuf.at[rows(q)], in_sems.at[q]) for q in range(nq)]
        for cp in loads:
            cp.start()
        sends = []
        for q in range(nq):
            loads[q].wait()
            sends.append(_remote(sbuf.at[rows(q)], rbuf.at[rows(q)], send_sems, recv_sems, q, sib))
            sends[q].start()
        stores = []
        for q in range(nq):
            sends[q].wait_recv()
            stores.append(pltpu.make_async_copy(rbuf.at[rows(q)], out_ref.at[rows(q)], out_sems.at[q]))
            stores[q].start()
        for cp in sends:
            cp.wait_send()
        for cp in stores:
            cp.wait()

    return pl.pallas_call(
        body,
        name="exchange_halves_" + tag,
        in_specs=[_ANY],
        out_specs=_ANY,
        out_shape=jax.ShapeDtypeStruct(s.shape, s.dtype),
        scratch_shapes=[pltpu.VMEM(s.shape, s.dtype), pltpu.VMEM(s.shape, s.dtype)]
        + [pltpu.SemaphoreType.DMA((nq,))] * 4,
        compiler_params=pltpu.CompilerParams(vmem_limit_bytes=VMEM_LIMIT),
    )(s)


def sum_all(s):
    def body(s_ref, out_ref, slots, mine, theirs, send_sems, recv_sems):
        x, y, c = _coords()
        me = 2 * x + y
        chips = _other_chips(x, y)
        sends = [_remote(s_ref, slots.at[me], send_sems, recv_sems, k, (cx, cy, c)) for k, (cx, cy) in enumerate(chips)]
        for cp in sends:
            cp.start()
        for k, (cx, cy) in enumerate(chips):
            _remote(s_ref, slots.at[2 * cx + cy], send_sems, recv_sems, k, (cx, cy, c)).wait_recv()
        slots[me] = s_ref[...]
        acc = ((slots[0] + slots[1]) + slots[2]) + slots[3]
        mine[...] = acc
        swap = _remote(mine, theirs, send_sems, recv_sems, 3, (x, y, 1 - c))
        swap.start()
        swap.wait_recv()
        out_ref[...] = acc + theirs[...]
        swap.wait_send()
        for cp in sends:
            cp.wait_send()

    vmem = pl.BlockSpec(memory_space=pltpu.VMEM)
    return pl.pallas_call(
        body,
        name="sum_all",
        in_specs=[vmem],
        out_specs=vmem,
        out_shape=jax.ShapeDtypeStruct(s.shape, s.dtype),
        scratch_shapes=[pltpu.VMEM((N_CHIPS,) + s.shape, s.dtype), pltpu.VMEM(s.shape, s.dtype),
                        pltpu.VMEM(s.shape, s.dtype), pltpu.SemaphoreType.DMA((4,)), pltpu.SemaphoreType.DMA((4,))],
        compiler_params=pltpu.CompilerParams(vmem_limit_bytes=VMEM_LIMIT),
    )(s)


ADAM_LR = 0.001
ADAM_B1 = 0.9
ADAM_B2 = 0.999
ADAM_EPS = 1e-08
ADAM_WD = 0.01
ADAM_STEP = 10


def f_adamw(g, w, m, v):
    m = ADAM_B1 * m + (1.0 - ADAM_B1) * g
    v = ADAM_B2 * v + (1.0 - ADAM_B2) * jnp.square(g)
    m_hat = m / (1.0 - ADAM_B1 ** ADAM_STEP)
    v_hat = v / (1.0 - ADAM_B2 ** ADAM_STEP)
    delta = -ADAM_LR * (m_hat / (jnp.sqrt(v_hat) + ADAM_EPS) + ADAM_WD * w)
    return delta, m, v


def adamw_call(g, w, m, v, *, tm, name):
    width = g.shape[1]
    return ew_call(f_adamw, [(g, width, 0), (w, width, 0), (m, width, 0), (v, width, 0)], [], [(width, F32)] * 3,
                   tm=tm, name=name)


def adamw_halves(g_own, g_other, w, m, v, *, tm):
    _, rows, width = w.shape

    def body(go_ref, gx_ref, w_ref, m_ref, v_ref, g_ref, d_ref, nm_ref, nv_ref):
        g = jnp.where(pl.program_id(0) == lax.axis_index("c"), go_ref[...], gx_ref[...])
        delta, nm, nv = f_adamw(g, w_ref[0], m_ref[0], v_ref[0])
        g_ref[0] = g
        d_ref[0] = delta
        nm_ref[0] = nm
        nv_ref[0] = nv

    half = pl.BlockSpec((tm, width), lambda h, i: (i, 0))
    full = pl.BlockSpec((1, tm, width), lambda h, i: (h, i, 0))
    return pl.pallas_call(
        body,
        name="adamw_sharded",
        grid=(2, rows // tm),
        in_specs=[half, half, full, full, full],
        out_specs=[full] * 4,
        out_shape=[jax.ShapeDtypeStruct(w.shape, F32)] * 4,
        compiler_params=_cparams(2),
    )(g_own, g_other, w, m, v)


EARLY = ["w_in", "w_proj_a", "w_lora_w", "a_lora_w", "g_lora_w"]
LATE = ["w_ffn1", "w_ffn2", "w_proj_b", "w_out"]
SHARDED = EARLY + LATE
LORAS = ["w_lora_w", "a_lora_w", "g_lora_w"]
HALF_W = 512
PIECE_ROWS = {"w_in": 1864, "w_ffn1": 1024, "w_ffn2": 1024, "w_proj_a": 256, "w_proj_b": 256, "w_out": 256,
              "w_lora_w": 32, "a_lora_w": 32, "g_lora_w": 80}
PIECE_OFF = {"w_in": 0, "w_proj_a": 1920, "w_lora_w": 2176, "a_lora_w": 2208, "g_lora_w": 2240,
             "w_ffn1": 0, "w_ffn2": 1024, "w_proj_b": 2048, "w_out": 2304}
LO_OFF = 2320
SHARD_AXIS = {"w_in": 1, "w_proj_a": 0, "w_lora_w": 1, "a_lora_w": 1, "g_lora_w": 1, "w_proj_b": 0, "w_out": 0,
              "w_ffn1": 1, "w_ffn2": 0}
SHARD_SHAPE = {"w_in": (1024, 1864), "w_proj_a": (256, 1024), "w_lora_w": (64, 256), "a_lora_w": (64, 256),
               "g_lora_w": (160, 256), "w_proj_b": (256, 1024), "w_out": (256, 1024), "w_ffn1": (1024, 1024),
               "w_ffn2": (1024, 1024)}
SHIFT_SHARD = (2, 840)
VECTORS = ["g_mix", "sgu_ln_w", "sgu_ln_b", "w0", "a0", "k_k", "k_a", "r_k", "ln_x_w", "ln_x_b", "g_ffn", "g_final"]
SMALL = VECTORS + ["sgu_w", "sgu_b"]
SMALL_SHAPE = {**{n: (1, 1024) for n in VECTORS}, "sgu_w": (8, 128, 128), "sgu_b": (8, 128)}
WEIGHTS = ["g_mix", "w_in", "sgu_ln_w", "sgu_ln_b", "sgu_w", "sgu_b", "w_proj_a", "shift_b", "w_lora_w", "w0",
           "a_lora_w", "a0", "g_lora_w", "k_k", "k_a", "r_k", "ln_x_w", "ln_x_b", "w_proj_b", "w_out", "g_ffn",
           "w_ffn1", "w_ffn2", "g_final"]


def _size(shape):
    n = 1
    for s in shape:
        n *= s
    return n


def _pack_rows(parts, rows, dtype):
    flat = jnp.concatenate([p.reshape(-1).astype(dtype) for p in parts])
    return jnp.concatenate([flat, jnp.zeros((rows * 1024 - flat.shape[0],), dtype)]).reshape(rows, 1024)


def _unpack_rows(packed, shapes):
    flat = packed.reshape(-1)
    out, off = [], 0
    for shp in shapes:
        out.append(flat[off:off + _size(shp)].reshape(shp))
        off += _size(shp)
    return out


def _shard_of(name, full, j):
    ax = SHARD_AXIS[name]
    n = SHARD_SHAPE[name][ax]
    return lax.slice_in_dim(full, j * n, (j + 1) * n, axis=ax)


def _pad_cols(z, n):
    return jnp.concatenate([z, jnp.zeros((z.shape[0], n - z.shape[1]), z.dtype)], axis=1)


def _row_form(name, s):
    return s.T if name == "w_in" else s


def _half_piece(name, rf, h):
    if name in LORAS:
        r = PIECE_ROWS[name]
        return _pad_cols(rf[h * r:(h + 1) * r], HALF_W)
    return rf[:, HALF_W * h:HALF_W * (h + 1)]


def _pack_half(group, rf_fn, h, dtype, tail=()):
    parts, pos, rows = [], 0, PACK_ROWS
    for n in group:
        if PIECE_OFF[n] > pos:
            parts.append(jnp.zeros((PIECE_OFF[n] - pos, HALF_W), dtype))
        parts.append(_half_piece(n, rf_fn(n), h).astype(dtype))
        pos = PIECE_OFF[n] + PIECE_ROWS[n]
    for t in tail:
        parts.append(t)
        pos += t.shape[0]
    parts.append(jnp.zeros((rows - pos, HALF_W), dtype))
    return jnp.concatenate(parts, axis=0)


def _piece(pack, name):
    return pack[PIECE_OFF[name]:PIECE_OFF[name] + PIECE_ROWS[name]]


def _join_halves(name, p0, p1):
    if name in LORAS:
        return jnp.concatenate([p0[:, :SHARD_SHAPE[name][1]], p1[:, :SHARD_SHAPE[name][1]]], axis=0)
    return jnp.concatenate([p0, p1], axis=1)


def _grad_row_form(name, full, j):
    if name == "w_in":
        return full[SHARD_SHAPE[name][1] * j:SHARD_SHAPE[name][1] * (j + 1)]
    return _shard_of(name, full, j)


def adamw_weight(name, g_own, g_other, w, m, v):
    rows, width = w.shape
    if name in LORAS:
        tm = PIECE_ROWS[name]
        grid = (2, 1)
        native = pl.BlockSpec((tm, width), lambda h, i: (h, 0))
    elif name == "w_in":
        tm, lanes = rows, 128
        grid = (2, HALF_W // lanes)
        native = pl.BlockSpec((tm, lanes), lambda h, i: (0, h * (HALF_W // lanes) + i))
    else:
        tm = 128
        grid = (2, rows // tm)
        native = pl.BlockSpec((tm, HALF_W), lambda h, i: (i, h))
    off = PIECE_OFF[name] // tm
    if name == "w_in":
        packed = pl.BlockSpec((tm, 128), lambda h, i: (0, i))
    else:
        packed = pl.BlockSpec((tm, HALF_W), lambda h, i: (off + i, 0))

    def body(go_ref, gx_ref, w_ref, m_ref, v_ref, g_ref, d_ref, nm_ref, nv_ref):
        g = jnp.where(pl.program_id(0) == lax.axis_index("c"), go_ref[...], gx_ref[...])[:, :w_ref.shape[1]]
        delta, nm, nv = f_adamw(g, w_ref[...], m_ref[...], v_ref[...])
        g_ref[...] = g
        d_ref[...] = delta
        nm_ref[...] = nm
        nv_ref[...] = nv

    return pl.pallas_call(
        body,
        name="adamw_" + name,
        grid=grid,
        in_specs=[packed, packed, native, native, native],
        out_specs=[native] * 4,
        out_shape=[jax.ShapeDtypeStruct(w.shape, F32)] * 4,
        compiler_params=_cparams(2),
    )(g_own, g_other, w, m, v)


def kernel(x, g_mix, w_in, sgu_ln_w, sgu_ln_b, sgu_w, sgu_b, w_proj_a, shift_b, w_lora_w, w0, a_lora_w, a0, g_lora_w, k_k, k_a, r_k, ln_x_w, ln_x_b, w_proj_b, w_out, g_ffn, w_ffn1, w_ffn2, g_final, loss_target, m_g_mix, m_w_in, m_sgu_ln_w, m_sgu_ln_b, m_sgu_w, m_sgu_b, m_w_proj_a, m_shift_b, m_w_lora_w, m_w0, m_a_lora_w, m_a0, m_g_lora_w, m_k_k, m_k_a, m_r_k, m_ln_x_w, m_ln_x_b, m_w_proj_b, m_w_out, m_g_ffn, m_w_ffn1, m_w_ffn2, m_g_final, v_g_mix, v_w_in, v_sgu_ln_w, v_sgu_ln_b, v_sgu_w, v_sgu_b, v_w_proj_a, v_shift_b, v_w_lora_w, v_w0, v_a_lora_w, v_a0, v_g_lora_w, v_k_k, v_k_a, v_r_k, v_ln_x_w, v_ln_x_b, v_w_proj_b, v_w_out, v_g_ffn, v_w_ffn1, v_w_ffn2, v_g_final):
    given = dict(zip(WEIGHTS, (g_mix, w_in, sgu_ln_w, sgu_ln_b, sgu_w, sgu_b, w_proj_a, shift_b, w_lora_w, w0, a_lora_w, a0, g_lora_w, k_k, k_a, r_k, ln_x_w, ln_x_b, w_proj_b, w_out, g_ffn, w_ffn1, w_ffn2, g_final)))
    mom_m = dict(zip(WEIGHTS, (m_g_mix, m_w_in, m_sgu_ln_w, m_sgu_ln_b, m_sgu_w, m_sgu_b, m_w_proj_a, m_shift_b, m_w_lora_w, m_w0, m_a_lora_w, m_a0, m_g_lora_w, m_k_k, m_k_a, m_r_k, m_ln_x_w, m_ln_x_b, m_w_proj_b, m_w_out, m_g_ffn, m_w_ffn1, m_w_ffn2, m_g_final)))
    mom_v = dict(zip(WEIGHTS, (v_g_mix, v_w_in, v_sgu_ln_w, v_sgu_ln_b, v_sgu_w, v_sgu_b, v_w_proj_a, v_shift_b, v_w_lora_w, v_w0, v_a_lora_w, v_a0, v_g_lora_w, v_k_k, v_k_a, v_r_k, v_ln_x_w, v_ln_x_b, v_w_proj_b, v_w_out, v_g_ffn, v_w_ffn1, v_w_ffn2, v_g_final)))
    chip = 2 * lax.axis_index("x") + lax.axis_index("y")

    def local_block(tree, n):
        return tree[n] if n == "g_final" else tree[n][0]

    sb = local_block(given, "shift_b")
    lo_part = lambda z: (z - z.astype(BF16).astype(F32)).astype(BF16)
    row_form = lambda tree: (lambda n: _row_form(n, local_block(tree, n)))
    tile16 = lambda z: jnp.pad(z, ((0, 16 - z.shape[0]), (0, HALF_W - z.shape[1])))
    sb_tiles = [tile16(f(sb[:, lanes])) for f in (lambda z: z.astype(BF16), lo_part)
                for lanes in (slice(0, HALF_W), slice(HALF_W, None))]
    tails = [[_half_piece(n, lo_part(local_block(given, n)), h) for n in LORAS] + sb_tiles for h in range(2)]
    pack_w = jnp.stack([_pack_half(EARLY, row_form(given), h, BF16, tails[h]) for h in range(2)])
    gathered = gather_shards(pack_w)
    gathered = lax.dynamic_update_index_in_dim(gathered, pack_w, chip, 0)
    pack_late = jnp.stack([_pack_half(LATE, row_form(given), h, BF16) for h in range(2)])

    def whole(group, got):
        shard = lambda n, j: _join_halves(n, _piece(got[j, 0], n), _piece(got[j, 1], n))
        return {n: jnp.concatenate([shard(n, j).astype(F32 if n == "w_in" else BF16) for j in range(N_CHIPS)],
                                   axis=0 if n == "w_in" else SHARD_AXIS[n]) for n in group}

    w = whole(EARLY, gathered)
    late_weights = lambda got: whole(LATE, lax.dynamic_update_index_in_dim(got, pack_late, chip, 0))
    off = LO_OFF
    for n in LORAS:
        r, cols = PIECE_ROWS[n], SHARD_SHAPE[n][1]
        lo = jnp.concatenate([jnp.concatenate([gathered[j, 0, off:off + r, :cols], gathered[j, 1, off:off + r, :cols]],
                                              axis=0) for j in range(N_CHIPS)], axis=1)
        w[n] = w[n].astype(F32) + lo.astype(F32)
        off += r
    sb_tile = lambda j, t, lanes: gathered[j, 0, off + 16 * t:off + 16 * t + 2, :lanes].astype(F32)
    rest = SHIFT_SHARD[1] - HALF_W
    w["shift_b"] = jnp.concatenate(
        [jnp.concatenate([sb_tile(j, 0, HALF_W) + sb_tile(j, 2, HALF_W), sb_tile(j, 1, rest) + sb_tile(j, 3, rest)],
                         axis=1) for j in range(N_CHIPS)], axis=1)
    for n in SMALL:
        w[n] = local_block(given, n).reshape(SMALL_SHAPE[n])

    def partials(g_pack, tag):
        return pair_sum(g_pack, reduce_pair(g_pack, tag), tag, tm=PACK_TILE)

    loss, grad_x, grads, (late_part, late_slots) = local_step(
        x[0], loss_target[0], w, pack_late, late_weights, lambda g_pack: partials(g_pack, "late"))
    loss = lax.psum(loss, ("x", "y", "c"))

    early_g = jnp.stack([jnp.stack([_pack_half(EARLY, lambda n: _grad_row_form(n, grads[n], j), h, F32)
                                    for h in range(2)]) for j in range(N_CHIPS)])
    early_part, early_part16 = partials(early_g, "early")
    s_pack = _pack_rows([grads[n] for n in SMALL] + [grads["shift_b"]], SMALL_ROWS, F32)
    sends, recvs, part_thru, slots_thru, token = reduce_chips_start(early_part16)
    token, late_part, late_slots, s_pack = lax.optimization_barrier((token, late_part, late_slots, s_pack))
    my_chip = lambda: 2 * lax.axis_index("x") + lax.axis_index("y")
    out_g, out_d, out_m, out_v = {}, {}, {}, {}

    def finish(group, tag, part, slots):
        half_sum = sum_with_own(part, slots, my_chip, tm=PACK_TILE, name="chip_sum_" + tag)
        other_half = exchange_halves(half_sum, tag)
        for n in group:
            res = adamw_weight(n, half_sum, other_half,
                               *[_row_form(n, local_block(t, n)) for t in (given, mom_m, mom_v)])
            for tree, z in zip((out_g, out_d, out_m, out_v), res):
                tree[n] = _row_form(n, z)

    finish(LATE, "late", late_part, late_slots)

    small_shapes = [SMALL_SHAPE[n] for n in SMALL]
    g_small = sum_all(s_pack)
    w_small = _pack_rows([local_block(given, n) for n in SMALL], SMALL_ROWS, F32)
    m_small = _pack_rows([local_block(mom_m, n) for n in SMALL], SMALL_ROWS, F32)
    v_small = _pack_rows([local_block(mom_v, n) for n in SMALL], SMALL_ROWS, F32)
    d_small, nm_small, nv_small = adamw_call(g_small, w_small, m_small, v_small, tm=SMALL_ROWS, name="adamw_small")
    g_parts = _unpack_rows(g_small, small_shapes + [(2, N_RWKV)])
    out_g.update(zip(SMALL, g_parts[:-1]))
    out_d.update(zip(SMALL, _unpack_rows(d_small, small_shapes)))
    out_m.update(zip(SMALL, _unpack_rows(nm_small, small_shapes)))
    out_v.update(zip(SMALL, _unpack_rows(nv_small, small_shapes)))
    g_sb = lax.dynamic_slice_in_dim(g_parts[-1], chip * SHIFT_SHARD[1], SHIFT_SHARD[1], axis=1)
    sb_args = [_pack_rows([z], 8, F32) for z in (g_sb, sb, local_block(mom_m, "shift_b"), local_block(mom_v, "shift_b"))]
    sb_res = adamw_call(*sb_args, tm=8, name="adamw_shift_b")
    out_g["shift_b"] = g_sb
    for tree, res in zip((out_d, out_m, out_v), sb_res):
        tree["shift_b"] = _unpack_rows(res, [SHIFT_SHARD])[0]

    after = (out_v["w_out"], nv_small, sb_res[2])
    early_slots = reduce_chips_wait(sends, recvs, part_thru, slots_thru, jnp.concatenate([z.reshape(-1)[:8] for z in after]))
    finish(EARLY, "early", early_part, early_slots)

    def block_of(tree, n):
        return tree[n].reshape(given[n].shape)

    return (loss, grad_x[None], *[block_of(out_g, n) for n in WEIGHTS], *[block_of(out_d, n) for n in WEIGHTS],
            *[block_of(out_m, n) for n in WEIGHTS], *[block_of(out_v, n) for n in WEIGHTS])
```

```python
import functools

import jax
import jax.numpy as jnp
from jax import lax
from jax.experimental import pallas as pl
from jax.experimental.pallas import tpu as pltpu

F32 = jnp.float32
BF16 = jnp.bfloat16

D_MODEL = 1024
N_HEADS = 16
HEAD = 64
SCAN_CHUNK = 64

VMEM_LIMIT = 56 * 1024 * 1024


_BDIMS = {
    "nn": (((2,), (1,)), ((0,), (0,))),
    "nt": (((2,), (2,)), ((0,), (0,))),
    "tn": (((1,), (1,)), ((0,), (0,))),
}


def _raw_bdot(x, y, mode, fine):
    if fine:
        return lax.dot_general(x, y, _BDIMS[mode], precision=lax.Precision.HIGH, preferred_element_type=F32)
    return lax.dot_general(x.astype(BF16), y.astype(BF16), _BDIMS[mode], preferred_element_type=F32)


@functools.partial(jax.custom_vjp, nondiff_argnums=(2, 3))
def bdot(x, y, mode, fine=True):
    return _raw_bdot(x, y, mode, fine)


def _bdot_fwd(x, y, mode, fine):
    return _raw_bdot(x, y, mode, fine), (x, y)


def _bdot_bwd(mode, fine, res, g):
    x, y = res
    if mode == "nn":
        return bdot(g, y, "nt", fine), bdot(x, g, "tn", fine)
    if mode == "nt":
        return bdot(g, y, "nn", fine), bdot(g, x, "tn", fine)
    return bdot(y, g, "nt", fine), bdot(x, g, "nn", fine)


bdot.defvjp(_bdot_fwd, _bdot_bwd)


def _scan_chunk(S0, r, lw, k, v, a, b):
    nh, lc, _ = r.shape
    ti = lax.broadcasted_iota(jnp.int32, (lc, lc), 0)
    si = lax.broadcasted_iota(jnp.int32, (lc, lc), 1)
    incl = (si <= ti).astype(F32)
    strict = (si < ti).astype(F32)
    eye = (si == ti).astype(F32)
    cl = bdot(jnp.broadcast_to(incl, (nh, lc, lc)), lw, "nn")
    cl_last = cl[:, lc - 1:lc, :]
    g_last = jnp.exp(cl_last - cl)
    at = a * jnp.exp(cl - lw)
    bt = b * jnp.exp(-cl)
    kt = k * jnp.exp(-cl)
    rt = r * jnp.exp(cl)
    ar = jnp.concatenate([at, rt], axis=1)
    ar_b = bdot(ar, bt, "nt", False)
    ar_k = bdot(ar, kt, "nt", False)
    m_ab, m_rb = ar_b[:, :lc] * strict, ar_b[:, lc:] * incl
    m_ak, m_rk = ar_k[:, :lc] * strict, ar_k[:, lc:] * incl
    x = eye + m_ab
    p = bdot(m_ab, m_ab, "nn", False)
    n = 2
    while n * 2 < lc:
        px = bdot(jnp.concatenate([p, x], axis=1), p, "nn", False)
        p = px[:, :lc]
        x = x + px[:, lc:]
        n *= 2
    x = x + bdot(x, p, "nn", False)
    ar_s = bdot(ar, S0, "nt", False)
    akrk_v = bdot(jnp.concatenate([m_ak, m_rk], axis=1), v, "nn", False)
    u = bdot(x, ar_s[:, :lc] + akrk_v[:, :lc], "nn", False)
    o = ar_s[:, lc:] + bdot(m_rb, u, "nn", False) + akrk_v[:, lc:]
    s_last = S0 * jnp.exp(cl_last) + bdot(jnp.concatenate([u, v], axis=1),
                                          jnp.concatenate([b * g_last, k * g_last], axis=1), "tn", False)
    return o, s_last


def _split_heads(z):
    return jnp.stack([z[:, HEAD * h:HEAD * (h + 1)] for h in range(N_HEADS)], axis=0)


def _merge_heads(z):
    return jnp.concatenate([z[h] for h in range(N_HEADS)], axis=1)


def _scan_specs(t, ops, rev):
    nc = t // SCAN_CHUNK
    row = (lambda c: nc - 1 - c) if rev else (lambda c: c)
    specs = [pl.BlockSpec((SCAN_CHUNK, D_MODEL), lambda c, cb=cb: (row(c), cb)) for _, cb in ops]
    state = pl.BlockSpec((1, N_HEADS, HEAD, HEAD), lambda c: (row(c), 0, 0, 0))
    return nc, specs, state


def scan_fwd(ops, pack):
    t = ops[0][0].shape[0]
    nc, specs, state = _scan_specs(t, ops, False)

    def body(r_ref, lw_ref, k_ref, v_ref, a_ref, b_ref, pack_ref, o_ref, s0_ref, all_ref, s_scr, send_sems, recv_sems):
        step = pl.program_id(0)
        x, y, c = _coords()
        me = 2 * x + y
        sib = (x, y, 1 - c)
        chips = _other_chips(x, y)
        first = [_remote(pack_ref.at[c], all_ref.at[me, c], send_sems, recv_sems, k, (cx, cy, c))
                 for k, (cx, cy) in enumerate(chips)]
        passed = [_remote(all_ref.at[2 * cx + cy, c], all_ref.at[2 * cx + cy, c], send_sems, recv_sems, 3 + k, sib)
                  for k, (cx, cy) in enumerate(chips)]

        @pl.when(step == 0)
        def _():
            s_scr[...] = jnp.zeros_like(s_scr)
            for cp in first:
                cp.start()

        s0 = s_scr[...]
        s0_ref[0] = s0
        o, s_last = _scan_chunk(s0, *[_split_heads(z[...]) for z in (r_ref, lw_ref, k_ref, v_ref, a_ref, b_ref)])
        o_ref[...] = _merge_heads(o)
        s_scr[...] = s_last

        @pl.when(step == nc - 1)
        def _():
            for k, (cx, cy) in enumerate(chips):
                j = 2 * cx + cy
                _remote(pack_ref.at[c], all_ref.at[j, c], send_sems, recv_sems, k, (cx, cy, c)).wait_recv()
                passed[k].start()
            for k, (cx, cy) in enumerate(chips):
                j = 2 * cx + cy
                _remote(all_ref.at[j, 1 - c], all_ref.at[j, 1 - c], send_sems, recv_sems, 3 + k, sib).wait_recv()
            for cp in first + passed:
                cp.wait_send()

    return pl.pallas_call(
        body,
        name="scan_fwd",
        grid=(nc,),
        in_specs=specs + [_ANY],
        out_specs=[pl.BlockSpec((SCAN_CHUNK, D_MODEL), lambda c: (c, 0)), state, _ANY],
        out_shape=[jax.ShapeDtypeStruct((t, D_MODEL), F32), jax.ShapeDtypeStruct((nc, N_HEADS, HEAD, HEAD), F32),
                   jax.ShapeDtypeStruct((N_CHIPS,) + pack.shape, pack.dtype)],
        scratch_shapes=[pltpu.VMEM((N_HEADS, HEAD, HEAD), F32), pltpu.SemaphoreType.DMA((6,)),
                        pltpu.SemaphoreType.DMA((6,))],
        compiler_params=_cparams(1),
    )(*[a for a, _ in ops], pack)


def scan_bwd(ops, s0s, do, part):
    t = ops[0][0].shape[0]
    nc, specs, state = _scan_specs(t, ops + [(do, 0)], True)

    def body(r_ref, lw_ref, k_ref, v_ref, a_ref, b_ref, do_ref, s0_ref, part_ref, *rest):
        out_refs, slots_ref, ds_scr, send_sems, recv_sems = rest[:6], rest[6], rest[7], rest[8], rest[9]
        step = pl.program_id(0)
        x, y, c = _coords()
        me = 2 * x + y
        chips = _other_chips(x, y)
        sends = [_remote(part_ref.at[2 * cx + cy], slots_ref.at[me], send_sems, recv_sems, k, (cx, cy, c))
                 for k, (cx, cy) in enumerate(chips)]

        @pl.when(step == 0)
        def _():
            ds_scr[...] = jnp.zeros_like(ds_scr)
            for cp in sends:
                cp.start()

        _, vjp = jax.vjp(_scan_chunk, s0_ref[0],
                         *[_split_heads(z[...]) for z in (r_ref, lw_ref, k_ref, v_ref, a_ref, b_ref)])
        grads = vjp((_split_heads(do_ref[...]), ds_scr[...]))
        for o_ref, g in zip(out_refs, grads[1:]):
            o_ref[...] = _merge_heads(g)
        ds_scr[...] = grads[0]

        @pl.when(step == nc - 1)
        def _():
            for k, (cx, cy) in enumerate(chips):
                _remote(part_ref.at[me], slots_ref.at[2 * cx + cy], send_sems, recv_sems, k, (cx, cy, c)).wait_recv()
            for cp in sends:
                cp.wait_send()

    return pl.pallas_call(
        body,
        name="scan_bwd",
        grid=(nc,),
        in_specs=specs + [state, _ANY],
        out_specs=[pl.BlockSpec((SCAN_CHUNK, D_MODEL), lambda c: (nc - 1 - c, 0))] * 6 + [_ANY],
        out_shape=[jax.ShapeDtypeStruct((t, D_MODEL), F32)] * 6 + [jax.ShapeDtypeStruct(part.shape, part.dtype)],
        scratch_shapes=[pltpu.VMEM((N_HEADS, HEAD, HEAD), F32), pltpu.SemaphoreType.DMA((3,)),
                        pltpu.SemaphoreType.DMA((3,))],
        compiler_params=_cparams(1),
    )(*[a for a, _ in ops], do, s0s, part)


_MDIMS = {
    "nn": (((1,), (0,)), ((), ())),
    "nt": (((1,), (1,)), ((), ())),
    "tn": (((0,), (0,)), ((), ())),
}


def _raw_mdot(x, y, mode, exact):
    if exact:
        return lax.dot_general(x, y, _MDIMS[mode], precision=lax.Precision.HIGH, preferred_element_type=F32)
    return lax.dot_general(x.astype(BF16), y.astype(BF16), _MDIMS[mode], preferred_element_type=F32)


@functools.partial(jax.custom_vjp, nondiff_argnums=(2, 3))
def mdot(x, y, mode, exact):
    return _raw_mdot(x, y, mode, exact)


def _mdot_fwd(x, y, mode, exact):
    return _raw_mdot(x, y, mode, exact), (x, y)


def _mdot_bwd(mode, exact, res, g):
    x, y = res
    if mode == "nn":
        return mdot(g, y, "nt", exact), mdot(x, g, "tn", exact)
    if mode == "nt":
        return mdot(g, y, "nn", exact), mdot(g, x, "tn", exact)
    return mdot(y, g, "nt", exact), mdot(x, g, "nn", exact)


mdot.defvjp(_mdot_fwd, _mdot_bwd)


def _seg_ones():
    i = lax.broadcasted_iota(jnp.int32, (256, 256), 0) // HEAD
    j = lax.broadcasted_iota(jnp.int32, (256, 256), 1) // HEAD
    return (i == j).astype(BF16)


@jax.custom_vjp
def segsum(x):
    bd = _seg_ones()
    hi = x.astype(BF16)
    lo = (x - hi.astype(F32)).astype(BF16)
    cols = []
    for j in range(x.shape[1] // 256):
        sl = slice(256 * j, 256 * (j + 1))
        cols.append(jnp.dot(hi[:, sl], bd, preferred_element_type=F32)
                    + jnp.dot(lo[:, sl], bd, preferred_element_type=F32))
    return jnp.concatenate(cols, axis=1)


segsum.defvjp(lambda x: (segsum(x), None), lambda _, g: (segsum(g),))


NORM_EPS = 1e-6
LN_EPS = 1e-5
GN_EPS = 64e-5
SGU_CHUNK = 128
SGU_GROUPS = 8


def _rms(x, g):
    return x * lax.rsqrt(jnp.mean(x * x, axis=-1, keepdims=True) + NORM_EPS) * g


def f_norm_in(x, g):
    return _rms(x, g), x


def f_sgu(p, ln_w, ln_b, sw, sbt):
    tm = p.shape[0]
    z = 0.5 * p * (1.0 + lax.erf(p * 0.7071067811865476))
    u, v = z[:, :D_MODEL], z[:, D_MODEL:]
    mu = jnp.mean(v, axis=-1, keepdims=True)
    d = v - mu
    vn = d * lax.rsqrt(jnp.mean(d * d, axis=-1, keepdims=True) + LN_EPS) * ln_w + ln_b
    ii = lax.broadcasted_iota(jnp.int32, (SGU_CHUNK, SGU_CHUNK), 0)
    jj = lax.broadcasted_iota(jnp.int32, (SGU_CHUNK, SGU_CHUNK), 1)
    mask = (jj <= ii).astype(F32)
    gi = lax.broadcasted_iota(jnp.int32, (SGU_GROUPS, D_MODEL), 0)
    ci = lax.broadcasted_iota(jnp.int32, (SGU_GROUPS, D_MODEL), 1) // SGU_CHUNK
    bias = mdot(sbt, (gi == ci).astype(F32), "nn", True)
    rows = []
    for c in range(tm // SGU_CHUNK):
        cols = []
        for g in range(SGU_GROUPS):
            blk = vn[c * SGU_CHUNK:(c + 1) * SGU_CHUNK, g * SGU_CHUNK:(g + 1) * SGU_CHUNK]
            cols.append(mdot(sw[g] * mask, blk, "nn", False))
        rows.append(jnp.concatenate(cols, axis=1) + bias)
    return (u * jnp.concatenate(rows, axis=0),)


def _softplus(x):
    return jnp.maximum(x, 0.0) + jnp.log1p(jnp.exp(-jnp.abs(x)))


def f_pre(qr, qk, qv, ql, wl, w0, al, a0, gl, k_k, k_a):
    xw, xa, xg = ql[:, :128], ql[:, 128:256], ql[:, 256:512]
    wr = -_softplus(-(w0 + mdot(jnp.tanh(xw), wl, "nn", True))) - 0.5
    lw = -jnp.exp(wr)
    aa = jax.nn.sigmoid(a0 + mdot(xa, al, "nn", True))
    g = mdot(jax.nn.sigmoid(xg), gl, "nn", True)
    kkr = qk * k_k
    kk = kkr / jnp.maximum(jnp.sqrt(segsum(kkr * kkr)), 1e-12)
    kp = qk * (1.0 + (aa - 1.0) * k_a)
    return qr, lw, kp, qv, -kk, kk * aa, g, qr, kp, qv


def f_post(o, r, kp, v, g, lnw, lnb, rk):
    mu = segsum(o) * (1.0 / HEAD)
    d = o - mu
    gn = d * lax.rsqrt(segsum(d * d) * (1.0 / HEAD) + GN_EPS)
    return ((gn * lnw + lnb + segsum(r * kp * rk) * v) * g,)


def f_mix(ya, yb, ga, gb):
    return (jax.nn.sigmoid(ga) * ya + jax.nn.sigmoid(gb) * yb,)


def f_ffn_in(h1, g):
    return _rms(h1, g), h1


def f_final(h1, m3, tgt, g):
    y = _rms(h1 + m3, g)
    err = jnp.square(y - tgt)
    return 0.5 * jnp.sum(jnp.mean(err, axis=-1))


def _cparams(n_grid):
    return pltpu.CompilerParams(dimension_semantics=("arbitrary",) * n_grid, vmem_limit_bytes=VMEM_LIMIT)


def _tile_spec(tm, w, cb):
    return pl.BlockSpec((tm, w), lambda i: (i, cb))


def _const_spec(c):
    nd = c.ndim
    return pl.BlockSpec(c.shape, lambda i: (0,) * nd)


def ew_call(fn, tiled, consts, outs, *, tm, name):
    t = tiled[0][0].shape[0]
    n_t, n_c = len(tiled), len(consts)

    def body(*refs):
        tv = [r[...].astype(F32) for r in refs[:n_t]]
        cv = [r[...] for r in refs[n_t:n_t + n_c]]
        res = fn(*tv, *cv)
        for o_ref, val in zip(refs[n_t + n_c:], res):
            o_ref[...] = val.astype(o_ref.dtype)

    return pl.pallas_call(
        body,
        name=name,
        grid=(t // tm,),
        in_specs=[_tile_spec(tm, w, cb) for _, w, cb in tiled] + [_const_spec(c) for c in consts],
        out_specs=[_tile_spec(tm, w, 0) for w, _ in outs],
        out_shape=[jax.ShapeDtypeStruct((t, w), dt) for w, dt in outs],
        compiler_params=_cparams(1),
    )(*[a for a, _, _ in tiled], *consts)


def ew_vjp_call(fn, tiled, consts, cots, d_tiled, d_consts, *, tm, name):
    t = tiled[0][0].shape[0]
    n_t, n_c, n_g = len(tiled), len(consts), len(cots)
    dt_list = [(i, dt) for i, dts in enumerate(d_tiled) for dt in dts]
    dc_list = [i for i, want in enumerate(d_consts) if want]

    def body(*refs):
        tv = [r[...].astype(F32) for r in refs[:n_t]]
        cv = [r[...] for r in refs[n_t:n_t + n_c]]
        gv = tuple(r[...].astype(F32) for r in refs[n_t + n_c:n_t + n_c + n_g])
        out_refs = refs[n_t + n_c + n_g:]
        _, vjp = jax.vjp(fn, *tv, *cv)
        grads = vjp(gv)
        for o_ref, (i, _) in zip(out_refs, dt_list):
            o_ref[...] = grads[i].astype(o_ref.dtype)
        acc_refs = out_refs[len(dt_list):]

        @pl.when(pl.program_id(0) == 0)
        def _():
            for a_ref in acc_refs:
                a_ref[...] = jnp.zeros_like(a_ref)

        for a_ref, i in zip(acc_refs, dc_list):
            a_ref[...] += grads[n_t + i]

    res = pl.pallas_call(
        body,
        name=name,
        grid=(t // tm,),
        in_specs=[_tile_spec(tm, w, cb) for _, w, cb in tiled] + [_const_spec(c) for c in consts]
        + [_tile_spec(tm, w, cb) for _, w, cb in cots],
        out_specs=[_tile_spec(tm, tiled[i][1], 0) for i, _ in dt_list] + [_const_spec(consts[i]) for i in dc_list],
        out_shape=[jax.ShapeDtypeStruct((t, tiled[i][1]), dt) for i, dt in dt_list]
        + [jax.ShapeDtypeStruct(consts[i].shape, F32) for i in dc_list],
        compiler_params=_cparams(1),
    )(*[a for a, _, _ in tiled], *consts, *[a for a, _, _ in cots])
    return res[:len(dt_list)], res[len(dt_list):]


def mm(a, b, mode, *, tm, tn, name, out_dtypes=(F32,), epi=None, extras=(), into=None):
    m = a.shape[1] if mode == "tn" else a.shape[0]
    kd = a.shape[0] if mode == "tn" else a.shape[1]
    n = b.shape[0] if mode == "nt" else b.shape[1]
    tm, tn = min(tm, m), min(tn, n)
    if mode == "nn":
        a_spec = pl.BlockSpec((tm, kd), lambda i, j: (i, 0))
        b_spec = pl.BlockSpec((kd, tn), lambda i, j: (0, j))
    elif mode == "nt":
        a_spec = pl.BlockSpec((tm, kd), lambda i, j: (i, 0))
        b_spec = pl.BlockSpec((tn, kd), lambda i, j: (j, 0))
    else:
        a_spec = pl.BlockSpec((kd, tm), lambda i, j: (0, i))
        b_spec = pl.BlockSpec((kd, tn), lambda i, j: (0, j))
    n_e = len(extras)
    o_spec = pl.BlockSpec((tm, tn), lambda i, j: (i, j))

    if into is not None:
        buf, place = into

        def body_into(a_ref, b_ref, buf_ref, o_ref):
            o_ref[0, 0] = lax.dot_general(a_ref[...].astype(BF16), b_ref[...].astype(BF16), _MDIMS[mode],
                                          preferred_element_type=F32)

        return pl.pallas_call(
            body_into,
            name=name,
            grid=(m // tm, n // tn),
            in_specs=[a_spec, b_spec, pl.BlockSpec(memory_space=pl.ANY)],
            out_specs=pl.BlockSpec((1, 1, tm, tn), lambda i, j: (*place(i, j), 0)),
            out_shape=jax.ShapeDtypeStruct(buf.shape, F32),
            input_output_aliases={2: 0},
            compiler_params=_cparams(2),
        )(a, b, buf)

    def body(a_ref, b_ref, *refs):
        c = lax.dot_general(a_ref[...].astype(BF16), b_ref[...].astype(BF16), _MDIMS[mode],
                            preferred_element_type=F32)
        res = epi(c, *[r[...] for r in refs[:n_e]]) if epi is not None else (c,)
        for o_ref, val in zip(refs[n_e:], res):
            o_ref[...] = val.astype(o_ref.dtype)

    res = pl.pallas_call(
        body,
        name=name,
        grid=(m // tm, n // tn),
        in_specs=[a_spec, b_spec] + [o_spec] * n_e,
        out_specs=[o_spec] * len(out_dtypes),
        out_shape=[jax.ShapeDtypeStruct((m, n), dt) for dt in out_dtypes],
        compiler_params=_cparams(2),
    )(a, b, *extras)
    return res if len(out_dtypes) > 1 else res[0]


P_WIDTH = 7680
RWKV_COL0 = 4096
RWKV_WIDTH = 3584
SHIFT_BLK = 512


def _shift_down(p, prev_row):
    rows = lax.broadcasted_iota(jnp.int32, p.shape, 0)
    return jnp.where(rows == 0, prev_row, pltpu.roll(p, 1, 0))


def shiftmix_fwd(p_all, sbp, *, tm):
    t = p_all.shape[0]
    tm = min(tm, t)
    c0 = RWKV_COL0 // SHIFT_BLK
    hb = tm // 8

    def body(p_ref, halo_ref, sb_ref, q_ref):
        p = p_ref[...]
        prev = jnp.where(pl.program_id(0) == 0, 0.0, halo_ref[7:8, :])
        q_ref[...] = p * sb_ref[0:1, :] + _shift_down(p, prev) * sb_ref[1:2, :]

    return pl.pallas_call(
        body,
        name="shiftmix_fwd",
        grid=(t // tm, RWKV_WIDTH // SHIFT_BLK),
        in_specs=[
            pl.BlockSpec((tm, SHIFT_BLK), lambda i, j: (i, c0 + j)),
            pl.BlockSpec((8, SHIFT_BLK), lambda i, j: (jnp.maximum(i * hb - 1, 0), c0 + j)),
            pl.BlockSpec((2, SHIFT_BLK), lambda i, j: (0, j)),
        ],
        out_specs=pl.BlockSpec((tm, SHIFT_BLK), lambda i, j: (i, j)),
        out_shape=jax.ShapeDtypeStruct((t, RWKV_WIDTH), F32),
        compiler_params=_cparams(2),
    )(p_all, p_all, sbp)


def shiftmix_bwd(dq, col0, p_all, sbp, *, tm, name):
    t, w = dq.shape
    n_i = t // tm
    hb = tm // 8
    cq = col0 // SHIFT_BLK
    cp = (RWKV_COL0 + col0) // SHIFT_BLK

    def body(dq_ref, dqn_ref, p_ref, ph_ref, sb_ref, dp_ref, dsb_ref):
        i = pl.program_id(1)
        dq_t = dq_ref[...]
        rows = lax.broadcasted_iota(jnp.int32, dq_t.shape, 0)
        nxt = jnp.where(i == n_i - 1, 0.0, dqn_ref[0:1, :])
        up = jnp.where(rows == tm - 1, nxt, pltpu.roll(dq_t, tm - 1, 0))
        dp_ref[...] = (dq_t * sb_ref[0:1, :] + up * sb_ref[1:2, :]).astype(dp_ref.dtype)
        p = p_ref[...]
        prev = jnp.where(i == 0, 0.0, ph_ref[7:8, :])
        s0 = jnp.sum(dq_t * p, axis=0, keepdims=True)
        s1 = jnp.sum(dq_t * _shift_down(p, prev), axis=0, keepdims=True)
        two = lax.broadcasted_iota(jnp.int32, (2, SHIFT_BLK), 0)

        @pl.when(i == 0)
        def _():
            dsb_ref[...] = jnp.zeros_like(dsb_ref)

        dsb_ref[...] += jnp.where(two == 0, s0, s1)

    return pl.pallas_call(
        body,
        name=name,
        grid=(w // SHIFT_BLK, n_i),
        in_specs=[
            pl.BlockSpec((tm, SHIFT_BLK), lambda j, i: (i, j)),
            pl.BlockSpec((8, SHIFT_BLK), lambda j, i: (jnp.minimum((i + 1) * hb, t // 8 - 1), j)),
            pl.BlockSpec((tm, SHIFT_BLK), lambda j, i: (i, cp + j)),
            pl.BlockSpec((8, SHIFT_BLK), lambda j, i: (jnp.maximum(i * hb - 1, 0), cp + j)),
            pl.BlockSpec((2, SHIFT_BLK), lambda j, i: (0, cq + j)),
        ],
        out_specs=[
            pl.BlockSpec((tm, SHIFT_BLK), lambda j, i: (i, j)),
            pl.BlockSpec((2, SHIFT_BLK), lambda j, i: (0, j)),
        ],
        out_shape=[jax.ShapeDtypeStruct((t, w), BF16), jax.ShapeDtypeStruct((2, w), F32)],
        compiler_params=_cparams(2),
    )(dq, dq, p_all, p_all, sbp)


def final_call(h1, m3, tgt, g_final, *, tm):
    t = h1.shape[0]

    def body(h1_ref, m3_ref, tgt_ref, g_ref, dh_ref, dhb_ref, dg_ref, loss_ref):
        loss, vjp = jax.vjp(f_final, h1_ref[...], m3_ref[...], tgt_ref[...], g_ref[...])
        dh, _, _, dg = vjp(jnp.ones((), F32))
        dh_ref[...] = dh
        dhb_ref[...] = dh.astype(BF16)

        @pl.when(pl.program_id(0) == 0)
        def _():
            dg_ref[...] = jnp.zeros_like(dg_ref)
            loss_ref[...] = jnp.zeros_like(loss_ref)

        dg_ref[...] += dg
        loss_ref[...] += jnp.full(loss_ref.shape, loss, F32)

    tile = _tile_spec(tm, D_MODEL, 0)
    return pl.pallas_call(
        body,
        name="final_loss",
        grid=(t // tm,),
        in_specs=[tile, tile, tile, _const_spec(g_final)],
        out_specs=[tile, tile, _const_spec(g_final), pl.BlockSpec((8, 128), lambda i: (0, 0))],
        out_shape=[jax.ShapeDtypeStruct((t, D_MODEL), F32), jax.ShapeDtypeStruct((t, D_MODEL), BF16),
                   jax.ShapeDtypeStruct(g_final.shape, F32), jax.ShapeDtypeStruct((8, 128), F32)],
        compiler_params=_cparams(1),
    )(h1, m3, tgt, g_final)


N_SGU = 2048
N_RWKV = 3360
LORA_W, LORA_A, LORA_G = 64, 64, 160


def _pad_rwkv_cols(z):
    zero = lambda n: jnp.zeros(z.shape[:-1] + (n,), z.dtype)
    return jnp.concatenate([z[..., :3072], z[..., 3072:3136], zero(64), z[..., 3136:3200], zero(64),
                            z[..., 3200:3360], zero(96)], axis=-1)


def _unpad_rwkv_cols(z):
    return jnp.concatenate([z[..., :3072], z[..., 3072:3136], z[..., 3200:3264], z[..., 3328:3488]], axis=-1)


def _pad_win_rows(wt):
    z = wt[N_SGU:N_SGU + N_RWKV]
    zero = lambda n: jnp.zeros((n, wt.shape[1]), wt.dtype)
    return jnp.concatenate([wt[:N_SGU], wt[N_SGU + N_RWKV:], z[:3072], z[3072:3136], zero(64), z[3136:3200], zero(64),
                            z[3200:3360], zero(96)], axis=0)


def _unpad_win_rows(wt):
    z = wt[RWKV_COL0:]
    return jnp.concatenate([wt[:N_SGU], z[:3072], z[3072:3136], z[3200:3264], z[3328:3488], wt[N_SGU:RWKV_COL0]],
                           axis=0)


def _pad_rows(w, n):
    return jnp.concatenate([w, jnp.zeros((n - w.shape[0],) + w.shape[1:], w.dtype)], axis=0)


def _relu2_epi(c):
    return c, jnp.square(jnp.maximum(c, 0.0))


def _relu2_bwd_epi(c, hid):
    return (c * (2.0 * jnp.maximum(hid.astype(F32), 0.0)),)


def _add_epi(c, x):
    return (c + x,)


def _pre_fwd(*args):
    res = f_pre(*args)
    return res[1], res[2], res[4], res[5], res[6]


def local_step(x, tgt, w, late_pack, late_weights, late_partials):
    d = D_MODEL
    win_pt = _pad_win_rows(w["w_in"])
    sbp = _pad_rwkv_cols(w["shift_b"])
    wl = _pad_rows(w["w_lora_w"], 128)
    al = _pad_rows(w["a_lora_w"], 128)
    gl = _pad_rows(w["g_lora_w"], 256)
    sbt = w["sgu_b"].T

    (a_bf,) = ew_call(lambda x_, g_: (f_norm_in(x_, g_)[0],), [(x, d, 0)], [w["g_mix"]], [(d, BF16)], tm=256,
                      name="norm_in")
    p_all = mm(a_bf, win_pt, "nt", tm=2048, tn=640, name="mm_in")
    sgu_t = [(p_all, 2 * d, 0)]
    sgu_c = [w["sgu_ln_w"], w["sgu_ln_b"], w["sgu_w"], sbt]
    (s_bf,) = ew_call(f_sgu, sgu_t, sgu_c, [(d, BF16)], tm=256, name="sgu_fwd")
    ya = mm(s_bf, w["w_proj_a"], "nn", tm=512, tn=1024, name="mm_proj_a")
    q = shiftmix_fwd(p_all, sbp, tm=1024)
    pre_t = [(q, d, 0), (q, d, 1), (q, d, 2), (q, 512, 6)]
    pre_c = [wl, w["w0"], al, w["a0"], gl, w["k_k"], w["k_a"]]
    lw, kp, na, nb, g = ew_call(_pre_fwd, pre_t, pre_c, [(d, F32)] * 5, tm=256, name="rwkv_pre_fwd")
    scan_ops = [(q, 0), (lw, 0), (kp, 0), (q, 2), (na, 0), (nb, 0)]
    o, s0s, late_all = scan_fwd(scan_ops, late_pack)
    w = {**w, **late_weights(late_all)}
    post_t = [(o, d, 0), (q, d, 0), (kp, d, 0), (q, d, 2), (g, d, 0)]
    post_c = [w["ln_x_w"], w["ln_x_b"], w["r_k"]]
    (ob_bf,) = ew_call(f_post, post_t, post_c, [(d, BF16)], tm=256, name="rwkv_post_fwd")
    yb = mm(ob_bf, w["w_proj_b"], "nn", tm=512, tn=1024, name="mm_proj_b")
    mix_t = [(ya, d, 0), (yb, d, 0), (p_all, d, 2), (p_all, d, 3)]
    (mixed_bf,) = ew_call(f_mix, mix_t, [], [(d, BF16)], tm=256, name="mix_fwd")
    h1 = mm(mixed_bf, w["w_out"], "nn", tm=512, tn=1024, name="mm_out", epi=_add_epi, extras=(x,))
    (f_bf,) = ew_call(lambda h_, g_: (f_ffn_in(h_, g_)[0],), [(h1, d, 0)], [w["g_ffn"]], [(d, BF16)], tm=256,
                      name="ffn_norm")
    hid, act_bf = mm(f_bf, w["w_ffn1"], "nn", tm=2048, tn=1024, name="mm_ffn1", out_dtypes=(BF16, BF16), epi=_relu2_epi)
    m3 = mm(act_bf, w["w_ffn2"], "nn", tm=1024, tn=512, name="mm_ffn2")
    dh2, dh2_bf, dg_final, loss = final_call(h1, m3, tgt, w["g_final"], tm=256)

    dhid_bf = mm(dh2_bf, w["w_ffn2"], "nt", tm=2048, tn=1024, name="mm_dact", out_dtypes=(BF16,), epi=_relu2_bwd_epi,
                 extras=(hid,))
    late_g = lax.empty((N_CHIPS, 2, PACK_ROWS, HALF_W), F32)
    late_g = mm(act_bf, dh2_bf, "tn", tm=512, tn=HALF_W, name="mm_dw_ffn2",
                into=(late_g, lambda i, j: (i // 2, j, PIECE_OFF["w_ffn2"] // 512 + i % 2)))
    df = mm(dhid_bf, w["w_ffn1"], "nt", tm=1024, tn=512, name="mm_df")
    late_g = mm(f_bf, dhid_bf, "tn", tm=512, tn=HALF_W, name="mm_dw_ffn1",
                into=(late_g, lambda i, j: (j // 2, j % 2, PIECE_OFF["w_ffn1"] // 512 + i)))
    (dh1, dh1_bf), (dg_ffn,) = ew_vjp_call(f_ffn_in, [(h1, d, 0)], [w["g_ffn"]], [(df, d, 0), (dh2, d, 0)],
                                           [(F32, BF16)], [True], tm=256, name="ffn_norm_bwd")
    dmixed = mm(dh1_bf, w["w_out"], "nt", tm=512, tn=1024, name="mm_dmixed")
    late_g = mm(mixed_bf, dh1_bf, "tn", tm=256, tn=HALF_W, name="mm_dw_out",
                into=(late_g, lambda i, j: (i, j, PIECE_OFF["w_out"] // 256)))
    (dya_bf, dyb_bf, dga_bf, dgb_bf), _ = ew_vjp_call(f_mix, mix_t, [], [(dmixed, d, 0)], [(BF16,)] * 4, [], tm=256,
                                                      name="mix_bwd")
    dob = mm(dyb_bf, w["w_proj_b"], "nt", tm=512, tn=1024, name="mm_dob")
    late_g = mm(ob_bf, dyb_bf, "tn", tm=256, tn=HALF_W, name="mm_dw_proj_b",
                into=(late_g, lambda i, j: (i, j, PIECE_OFF["w_proj_b"] // 256)))
    (do, dr_p, dkp_p, dv_p, dg), (dlnx_w, dlnx_b, dr_k) = ew_vjp_call(
        f_post, post_t, post_c, [(dob, d, 0)], [(F32,)] * 5, [True] * 3, tm=256, name="rwkv_post_bwd")
    late_part, late_part16 = late_partials(late_g)
    *scan_g, late_slots = scan_bwd(scan_ops, s0s, do, late_part16)
    pre_g = [(z, d, 0) for z in scan_g] + [(dg, d, 0), (dr_p, d, 0), (dkp_p, d, 0), (dv_p, d, 0)]
    (dq_r, dq_k, dq_v, dq_l), (dwl, dw0, dal, da0, dgl, dk_k, dk_a) = ew_vjp_call(
        f_pre, pre_t, pre_c, pre_g, [(F32,)] * 4, [True] * 7, tm=128, name="rwkv_pre_bwd")
    dp_r, dsb_r = shiftmix_bwd(dq_r, 0, p_all, sbp, tm=256, name="shiftmix_bwd_r")
    dp_k, dsb_k = shiftmix_bwd(dq_k, d, p_all, sbp, tm=256, name="shiftmix_bwd_k")
    dp_v, dsb_v = shiftmix_bwd(dq_v, 2 * d, p_all, sbp, tm=256, name="shiftmix_bwd_v")
    dp_l, dsb_l = shiftmix_bwd(dq_l, 3 * d, p_all, sbp, tm=256, name="shiftmix_bwd_l")
    ds = mm(dya_bf, w["w_proj_a"], "nt", tm=512, tn=1024, name="mm_ds")
    d_proj_a = mm(s_bf, dya_bf, "tn", tm=512, tn=1024, name="mm_dw_proj_a")
    (dp_sgu,), (dln_w, dln_b, dsw, dsbt) = ew_vjp_call(f_sgu, sgu_t, sgu_c, [(ds, d, 0)], [(BF16,)], [True] * 4,
                                                       tm=256, name="sgu_bwd")
    dp_all = jnp.concatenate([dp_sgu, dga_bf, dgb_bf, dp_r, dp_k, dp_v, dp_l], axis=1)
    da = mm(dp_all, win_pt, "nn", tm=1024, tn=256, name="mm_da")
    d_in_pt = mm(dp_all, a_bf, "tn", tm=1280, tn=1024, name="mm_dw_in")
    (grad_x,), (dg_mix,) = ew_vjp_call(f_norm_in, [(x, d, 0)], [w["g_mix"]], [(da, d, 0), (dh1, d, 0)], [(F32,)],
                                       [True], tm=256, name="norm_in_bwd")

    grads = {
        "g_mix": dg_mix, "w_in": _unpad_win_rows(d_in_pt), "sgu_ln_w": dln_w, "sgu_ln_b": dln_b, "sgu_w": dsw,
        "sgu_b": dsbt.T, "w_proj_a": d_proj_a,
        "shift_b": _unpad_rwkv_cols(jnp.concatenate([dsb_r, dsb_k, dsb_v, dsb_l], axis=1)),
        "w_lora_w": dwl[:LORA_W], "w0": dw0, "a_lora_w": dal[:LORA_A], "a0": da0, "g_lora_w": dgl[:LORA_G],
        "k_k": dk_k, "k_a": dk_a, "r_k": dr_k, "ln_x_w": dlnx_w, "ln_x_b": dlnx_b, "g_ffn": dg_ffn,
        "g_final": dg_final,
    }
    return loss[0, 0], grad_x, grads, (late_part, late_slots)


MESH = pl.DeviceIdType.MESH
N_CHIPS = 4
N_DEV = 8
PACK_ROWS = 2560
PACK_TILE = 512
SMALL_ROWS = 152
_ANY = pl.BlockSpec(memory_space=pl.ANY)


def _coords():
    return lax.axis_index("x"), lax.axis_index("y"), lax.axis_index("c")


def _other_chips(x, y):
    return [(1 - x, y), (x, 1 - y), (1 - x, 1 - y)]


def _remote(src, dst, send_sems, recv_sems, k, to):
    return pltpu.make_async_remote_copy(src_ref=src, dst_ref=dst, send_sem=send_sems.at[k], recv_sem=recv_sems.at[k],
                                        device_id=to, device_id_type=MESH)


def gather_shards(pack):
    def body(src_ref, out_ref, send_sems, recv_sems):
        x, y, c = _coords()
        me = 2 * x + y
        sib = (x, y, 1 - c)
        chips = _other_chips(x, y)
        first = [_remote(src_ref.at[c], out_ref.at[me, c], send_sems, recv_sems, k, (cx, cy, c))
                 for k, (cx, cy) in enumerate(chips)]
        for cp in first:
            cp.start()
        passed = []
        for k, (cx, cy) in enumerate(chips):
            j = 2 * cx + cy
            _remote(src_ref.at[c], out_ref.at[j, c], send_sems, recv_sems, k, (cx, cy, c)).wait_recv()
            fwd = _remote(out_ref.at[j, c], out_ref.at[j, c], send_sems, recv_sems, 3 + k, sib)
            fwd.start()
            passed.append(fwd)
        for k, (cx, cy) in enumerate(chips):
            j = 2 * cx + cy
            _remote(out_ref.at[j, 1 - c], out_ref.at[j, 1 - c], send_sems, recv_sems, 3 + k, sib).wait_recv()
        for cp in first + passed:
            cp.wait_send()

    return pl.pallas_call(
        body,
        name="gather_shards",
        in_specs=[_ANY],
        out_specs=_ANY,
        out_shape=jax.ShapeDtypeStruct((N_CHIPS,) + pack.shape, pack.dtype),
        scratch_shapes=[pltpu.SemaphoreType.DMA((6,)), pltpu.SemaphoreType.DMA((6,))],
    )(pack)


def reduce_pair(g, tag):
    def body(g_ref, got_ref, send_sems, recv_sems):
        x, y, c = _coords()
        sib = (x, y, 1 - c)
        sends = [_remote(g_ref.at[j, 1 - c], got_ref.at[j], send_sems, recv_sems, j, sib) for j in range(N_CHIPS)]
        for cp in sends:
            cp.start()
        for cp in sends:
            cp.wait_recv()
        for cp in sends:
            cp.wait_send()

    return pl.pallas_call(
        body,
        name="reduce_pair_" + tag,
        in_specs=[_ANY],
        out_specs=_ANY,
        out_shape=jax.ShapeDtypeStruct((N_CHIPS,) + g.shape[2:], g.dtype),
        scratch_shapes=[pltpu.SemaphoreType.DMA((N_CHIPS,)), pltpu.SemaphoreType.DMA((N_CHIPS,))],
    )(g)


def pair_sum(g, got, tag, *, tm):
    n, _, rows, width = g.shape

    def body(g0_ref, g1_ref, got_ref, out_ref, out16_ref):
        own = jnp.where(lax.axis_index("c") == 0, g0_ref[0, 0], g1_ref[0, 0])
        total = own + got_ref[0]
        out_ref[0] = total
        out16_ref[0] = total.astype(BF16)

    blk = pl.BlockSpec((1, tm, width), lambda j, i: (j, i, 0))
    return pl.pallas_call(
        body,
        name="pair_sum_" + tag,
        grid=(n, rows // tm),
        in_specs=[pl.BlockSpec((1, 1, tm, width), lambda j, i: (j, 0, i, 0)),
                  pl.BlockSpec((1, 1, tm, width), lambda j, i: (j, 1, i, 0)), blk],
        out_specs=[blk, blk],
        out_shape=[jax.ShapeDtypeStruct(got.shape, F32), jax.ShapeDtypeStruct(got.shape, BF16)],
        compiler_params=_cparams(2),
    )(g, g, got)


def reduce_chips(p):
    def body(p_ref, out_ref, send_sems, recv_sems):
        x, y, c = _coords()
        me = 2 * x + y
        chips = _other_chips(x, y)
        sends = [_remote(p_ref.at[2 * cx + cy], out_ref.at[me], send_sems, recv_sems, k, (cx, cy, c))
                 for k, (cx, cy) in enumerate(chips)]
        for cp in sends:
            cp.start()
        for k, (cx, cy) in enumerate(chips):
            _remote(p_ref.at[me], out_ref.at[2 * cx + cy], send_sems, recv_sems, k, (cx, cy, c)).wait_recv()
        for cp in sends:
            cp.wait_send()

    return pl.pallas_call(
        body,
        name="reduce_chips",
        in_specs=[_ANY],
        out_specs=_ANY,
        out_shape=jax.ShapeDtypeStruct(p.shape, p.dtype),
        scratch_shapes=[pltpu.SemaphoreType.DMA((3,)), pltpu.SemaphoreType.DMA((3,))],
    )(p)


def _chip_copies(p_ref, slots_ref, send_sems, recv_sems):
    x, y, c = _coords()
    me = 2 * x + y
    return [(_remote(p_ref.at[2 * cx + cy], slots_ref.at[me], send_sems, recv_sems, k, (cx, cy, c)),
             _remote(p_ref.at[me], slots_ref.at[2 * cx + cy], send_sems, recv_sems, k, (cx, cy, c)))
            for k, (cx, cy) in enumerate(_other_chips(x, y))]


def reduce_chips_start(p):
    hbm = pl.BlockSpec(memory_space=pltpu.HBM)
    sem = pl.BlockSpec(memory_space=pltpu.SEMAPHORE)

    def body(p_ref, slots_ref, send_sems, recv_sems, p_thru, slots_thru, token):
        for send, _ in _chip_copies(p_ref, slots_ref, send_sems, recv_sems):
            send.start()
        token[...] = jnp.zeros_like(token)

    return pl.pallas_call(
        body,
        name="reduce_chips_start",
        out_shape=(pltpu.SemaphoreType.DMA((3,)), pltpu.SemaphoreType.DMA((3,)), pltpu.HBM(p.shape, p.dtype),
                   pltpu.HBM(p.shape, p.dtype), jax.ShapeDtypeStruct((8, 128), F32)),
        in_specs=(hbm, hbm),
        out_specs=(sem, sem, hbm, hbm, pl.BlockSpec(memory_space=pltpu.VMEM)),
        input_output_aliases={0: 2, 1: 3},
        compiler_params=pltpu.CompilerParams(has_side_effects=pltpu.SideEffectType.DATAFLOW_SIDE_EFFECTING),
    )(pltpu.with_memory_space_constraint(p, pltpu.HBM),
      pltpu.with_memory_space_constraint(lax.empty(p.shape, p.dtype), pltpu.HBM))


def reduce_chips_wait(send_sems, recv_sems, p_thru, slots_thru, after):
    hbm = pl.BlockSpec(memory_space=pltpu.HBM)
    sem = pl.BlockSpec(memory_space=pltpu.SEMAPHORE)

    def body(p_ref, slots_ref, send_sems, recv_sems, after_ref, p_dead, slots_out):
        for send, arrival in _chip_copies(p_ref, slots_ref, send_sems, recv_sems):
            send.wait_send()
            arrival.wait_recv()

    return pl.pallas_call(
        body,
        name="reduce_chips_wait",
        out_shape=(pltpu.HBM(p_thru.shape, p_thru.dtype), pltpu.HBM(slots_thru.shape, slots_thru.dtype)),
        in_specs=(hbm, hbm, sem, sem, pl.BlockSpec(memory_space=pl.ANY)),
        out_specs=(hbm, hbm),
        input_output_aliases={0: 0, 1: 1},
        compiler_params=pltpu.CompilerParams(has_side_effects=pltpu.SideEffectType.DATAFLOW_SIDE_EFFECTING),
    )(p_thru, slots_thru, send_sems, recv_sems, after)[1]


def sum_with_own(own, slots, index_fn, after, *, tm, name):
    n, rows, width = slots.shape

    def body(*refs):
        mine = index_fn()
        acc = None
        for s in range(n):
            term = jnp.where(mine == s, refs[s][0], refs[n + s][0].astype(F32))
            acc = term if acc is None else acc + term
        refs[-1][...] = acc

    slot_specs = [pl.BlockSpec((1, tm, width), lambda i, s=s: (s, i, 0)) for s in range(n)]
    return pl.pallas_call(
        body,
        name=name,
        grid=(rows // tm,),
        in_specs=slot_specs + slot_specs + [pl.BlockSpec(after.shape, lambda i: (0,) * after.ndim)],
        out_specs=pl.BlockSpec((tm, width), lambda i: (i, 0)),
        out_shape=jax.ShapeDtypeStruct((rows, width), F32),
        compiler_params=_cparams(1),
    )(*([own] * n), *([slots] * n), after)


def exchange_halves(s, tag):
    rq = PACK_TILE
    nq = s.shape[0] // rq

    def body(s_ref, out_ref, sbuf, rbuf, send_sems, recv_sems, in_sems, out_sems):
        x, y, c = _coords()
        sib = (x, y, 1 - c)
        rows = lambda q: pl.ds(q * rq, rq)
        loads = [pltpu.make_async_copy(s_ref.at[rows(q)], sbuf.at[rows(q)], in_sems.at[q]) for q in range(nq)]
        for cp in loads:
            cp.start()
        sends = []
        for q in range(nq):
            loads[q].wait()
            sends.append(_remote(sbuf.at[rows(q)], rbuf.at[rows(q)], send_sems, recv_sems, q, sib))
            sends[q].start()
        stores = []
        for q in range(nq):
            sends[q].wait_recv()
            stores.append(pltpu.make_async_copy(rbuf.at[rows(q)], out_ref.at[rows(q)], out_sems.at[q]))
            stores[q].start()
        for cp in sends:
            cp.wait_send()
        for cp in stores:
            cp.wait()

    return pl.pallas_call(
        body,
        name="exchange_halves_" + tag,
        in_specs=[_ANY],
        out_specs=_ANY,
        out_shape=jax.ShapeDtypeStruct(s.shape, s.dtype),
        scratch_shapes=[pltpu.VMEM(s.shape, s.dtype), pltpu.VMEM(s.shape, s.dtype)]
        + [pltpu.SemaphoreType.DMA((nq,))] * 4,
        compiler_params=pltpu.CompilerParams(vmem_limit_bytes=VMEM_LIMIT),
    )(s)


def sum_all(s, after):
    def body(s_ref, after_ref, out_ref, slots, mine, theirs, send_sems, recv_sems):
        x, y, c = _coords()
        me = 2 * x + y
        chips = _other_chips(x, y)
        sends = [_remote(s_ref, slots.at[me], send_sems, recv_sems, k, (cx, cy, c)) for k, (cx, cy) in enumerate(chips)]
        for cp in sends:
            cp.start()
        for k, (cx, cy) in enumerate(chips):
            _remote(s_ref, slots.at[2 * cx + cy], send_sems, recv_sems, k, (cx, cy, c)).wait_recv()
        slots[me] = s_ref[...]
        acc = ((slots[0] + slots[1]) + slots[2]) + slots[3]
        mine[...] = acc
        swap = _remote(mine, theirs, send_sems, recv_sems, 3, (x, y, 1 - c))
        swap.start()
        swap.wait_recv()
        out_ref[...] = acc + theirs[...]
        swap.wait_send()
        for cp in sends:
            cp.wait_send()

    vmem = pl.BlockSpec(memory_space=pltpu.VMEM)
    return pl.pallas_call(
        body,
        name="sum_all",
        in_specs=[vmem, vmem],
        out_specs=vmem,
        out_shape=jax.ShapeDtypeStruct(s.shape, s.dtype),
        scratch_shapes=[pltpu.VMEM((N_CHIPS,) + s.shape, s.dtype), pltpu.VMEM(s.shape, s.dtype),
                        pltpu.VMEM(s.shape, s.dtype), pltpu.SemaphoreType.DMA((4,)), pltpu.SemaphoreType.DMA((4,))],
        compiler_params=pltpu.CompilerParams(vmem_limit_bytes=VMEM_LIMIT),
    )(s, after)


ADAM_LR = 0.001
ADAM_B1 = 0.9
ADAM_B2 = 0.999
ADAM_EPS = 1e-08
ADAM_WD = 0.01
ADAM_STEP = 10


def f_adamw(g, w, m, v):
    m = ADAM_B1 * m + (1.0 - ADAM_B1) * g
    v = ADAM_B2 * v + (1.0 - ADAM_B2) * jnp.square(g)
    m_hat = m / (1.0 - ADAM_B1 ** ADAM_STEP)
    v_hat = v / (1.0 - ADAM_B2 ** ADAM_STEP)
    delta = -ADAM_LR * (m_hat / (jnp.sqrt(v_hat) + ADAM_EPS) + ADAM_WD * w)
    return delta, m, v


def adamw_call(g, w, m, v, *, tm, name):
    width = g.shape[1]
    return ew_call(f_adamw, [(g, width, 0), (w, width, 0), (m, width, 0), (v, width, 0)], [], [(width, F32)] * 3,
                   tm=tm, name=name)


def adamw_halves(g_own, g_other, w, m, v, *, tm):
    _, rows, width = w.shape

    def body(go_ref, gx_ref, w_ref, m_ref, v_ref, g_ref, d_ref, nm_ref, nv_ref):
        g = jnp.where(pl.program_id(0) == lax.axis_index("c"), go_ref[...], gx_ref[...])
        delta, nm, nv = f_adamw(g, w_ref[0], m_ref[0], v_ref[0])
        g_ref[0] = g
        d_ref[0] = delta
        nm_ref[0] = nm
        nv_ref[0] = nv

    half = pl.BlockSpec((tm, width), lambda h, i: (i, 0))
    full = pl.BlockSpec((1, tm, width), lambda h, i: (h, i, 0))
    return pl.pallas_call(
        body,
        name="adamw_sharded",
        grid=(2, rows // tm),
        in_specs=[half, half, full, full, full],
        out_specs=[full] * 4,
        out_shape=[jax.ShapeDtypeStruct(w.shape, F32)] * 4,
        compiler_params=_cparams(2),
    )(g_own, g_other, w, m, v)


EARLY = ["w_in", "w_proj_a", "w_lora_w", "a_lora_w", "g_lora_w"]
LATE = ["w_ffn1", "w_ffn2", "w_proj_b", "w_out"]
SHARDED = EARLY + LATE
LORAS = ["w_lora_w", "a_lora_w", "g_lora_w"]
HALF_W = 512
PIECE_ROWS = {"w_in": 1864, "w_ffn1": 1024, "w_ffn2": 1024, "w_proj_a": 256, "w_proj_b": 256, "w_out": 256,
              "w_lora_w": 32, "a_lora_w": 32, "g_lora_w": 80}
PIECE_OFF = {"w_in": 0, "w_proj_a": 1920, "w_lora_w": 2176, "a_lora_w": 2208, "g_lora_w": 2240,
             "w_ffn1": 0, "w_ffn2": 1024, "w_proj_b": 2048, "w_out": 2304}
LO_OFF = 2320
SHARD_AXIS = {"w_in": 1, "w_proj_a": 0, "w_lora_w": 1, "a_lora_w": 1, "g_lora_w": 1, "w_proj_b": 0, "w_out": 0,
              "w_ffn1": 1, "w_ffn2": 0}
SHARD_SHAPE = {"w_in": (1024, 1864), "w_proj_a": (256, 1024), "w_lora_w": (64, 256), "a_lora_w": (64, 256),
               "g_lora_w": (160, 256), "w_proj_b": (256, 1024), "w_out": (256, 1024), "w_ffn1": (1024, 1024),
               "w_ffn2": (1024, 1024)}
SHIFT_SHARD = (2, 840)
VECTORS = ["g_mix", "sgu_ln_w", "sgu_ln_b", "w0", "a0", "k_k", "k_a", "r_k", "ln_x_w", "ln_x_b", "g_ffn", "g_final"]
SMALL = VECTORS + ["sgu_w", "sgu_b"]
SMALL_SHAPE = {**{n: (1, 1024) for n in VECTORS}, "sgu_w": (8, 128, 128), "sgu_b": (8, 128)}
WEIGHTS = ["g_mix", "w_in", "sgu_ln_w", "sgu_ln_b", "sgu_w", "sgu_b", "w_proj_a", "shift_b", "w_lora_w", "w0",
           "a_lora_w", "a0", "g_lora_w", "k_k", "k_a", "r_k", "ln_x_w", "ln_x_b", "w_proj_b", "w_out", "g_ffn",
           "w_ffn1", "w_ffn2", "g_final"]


def _size(shape):
    n = 1
    for s in shape:
        n *= s
    return n


def _pack_rows(parts, rows, dtype):
    flat = jnp.concatenate([p.reshape(-1).astype(dtype) for p in parts])
    return jnp.concatenate([flat, jnp.zeros((rows * 1024 - flat.shape[0],), dtype)]).reshape(rows, 1024)


def _unpack_rows(packed, shapes):
    flat = packed.reshape(-1)
    out, off = [], 0
    for shp in shapes:
        out.append(flat[off:off + _size(shp)].reshape(shp))
        off += _size(shp)
    return out


def _shard_of(name, full, j):
    ax = SHARD_AXIS[name]
    n = SHARD_SHAPE[name][ax]
    return lax.slice_in_dim(full, j * n, (j + 1) * n, axis=ax)


def _pad_cols(z, n):
    return jnp.concatenate([z, jnp.zeros((z.shape[0], n - z.shape[1]), z.dtype)], axis=1)


def _row_form(name, s):
    return s.T if name == "w_in" else s


def _half_piece(name, rf, h):
    if name in LORAS:
        r = PIECE_ROWS[name]
        return _pad_cols(rf[h * r:(h + 1) * r], HALF_W)
    return rf[:, HALF_W * h:HALF_W * (h + 1)]


def _pack_half(group, rf_fn, h, dtype, tail=()):
    parts, pos, rows = [], 0, PACK_ROWS
    for n in group:
        if PIECE_OFF[n] > pos:
            parts.append(jnp.zeros((PIECE_OFF[n] - pos, HALF_W), dtype))
        parts.append(_half_piece(n, rf_fn(n), h).astype(dtype))
        pos = PIECE_OFF[n] + PIECE_ROWS[n]
    for t in tail:
        parts.append(t)
        pos += t.shape[0]
    parts.append(jnp.zeros((rows - pos, HALF_W), dtype))
    return jnp.concatenate(parts, axis=0)


def _piece(pack, name):
    return pack[PIECE_OFF[name]:PIECE_OFF[name] + PIECE_ROWS[name]]


def _join_halves(name, p0, p1):
    if name in LORAS:
        return jnp.concatenate([p0[:, :SHARD_SHAPE[name][1]], p1[:, :SHARD_SHAPE[name][1]]], axis=0)
    return jnp.concatenate([p0, p1], axis=1)


def _grad_row_form(name, full, j):
    if name == "w_in":
        return full[SHARD_SHAPE[name][1] * j:SHARD_SHAPE[name][1] * (j + 1)]
    return _shard_of(name, full, j)


def adamw_weight(name, g_own, g_other, w, m, v):
    rows, width = w.shape
    if name in LORAS:
        tm = PIECE_ROWS[name]
        grid = (2, 1)
        native = pl.BlockSpec((tm, width), lambda h, i: (h, 0))
    elif name == "w_in":
        tm, lanes = rows, 128
        grid = (2, HALF_W // lanes)
        native = pl.BlockSpec((tm, lanes), lambda h, i: (0, h * (HALF_W // lanes) + i))
    else:
        tm = 128
        grid = (2, rows // tm)
        native = pl.BlockSpec((tm, HALF_W), lambda h, i: (i, h))
    off = PIECE_OFF[name] // tm
    if name == "w_in":
        packed = pl.BlockSpec((tm, 128), lambda h, i: (0, i))
    else:
        packed = pl.BlockSpec((tm, HALF_W), lambda h, i: (off + i, 0))

    def body(go_ref, gx_ref, w_ref, m_ref, v_ref, g_ref, d_ref, nm_ref, nv_ref):
        g = jnp.where(pl.program_id(0) == lax.axis_index("c"), go_ref[...], gx_ref[...])[:, :w_ref.shape[1]]
        delta, nm, nv = f_adamw(g, w_ref[...], m_ref[...], v_ref[...])
        g_ref[...] = g
        d_ref[...] = delta
        nm_ref[...] = nm
        nv_ref[...] = nv

    return pl.pallas_call(
        body,
        name="adamw_" + name,
        grid=grid,
        in_specs=[packed, packed, native, native, native],
        out_specs=[native] * 4,
        out_shape=[jax.ShapeDtypeStruct(w.shape, F32)] * 4,
        compiler_params=_cparams(2),
    )(g_own, g_other, w, m, v)


def kernel(x, g_mix, w_in, sgu_ln_w, sgu_ln_b, sgu_w, sgu_b, w_proj_a, shift_b, w_lora_w, w0, a_lora_w, a0, g_lora_w, k_k, k_a, r_k, ln_x_w, ln_x_b, w_proj_b, w_out, g_ffn, w_ffn1, w_ffn2, g_final, loss_target, m_g_mix, m_w_in, m_sgu_ln_w, m_sgu_ln_b, m_sgu_w, m_sgu_b, m_w_proj_a, m_shift_b, m_w_lora_w, m_w0, m_a_lora_w, m_a0, m_g_lora_w, m_k_k, m_k_a, m_r_k, m_ln_x_w, m_ln_x_b, m_w_proj_b, m_w_out, m_g_ffn, m_w_ffn1, m_w_ffn2, m_g_final, v_g_mix, v_w_in, v_sgu_ln_w, v_sgu_ln_b, v_sgu_w, v_sgu_b, v_w_proj_a, v_shift_b, v_w_lora_w, v_w0, v_a_lora_w, v_a0, v_g_lora_w, v_k_k, v_k_a, v_r_k, v_ln_x_w, v_ln_x_b, v_w_proj_b, v_w_out, v_g_ffn, v_w_ffn1, v_w_ffn2, v_g_final):
    given = dict(zip(WEIGHTS, (g_mix, w_in, sgu_ln_w, sgu_ln_b, sgu_w, sgu_b, w_proj_a, shift_b, w_lora_w, w0, a_lora_w, a0, g_lora_w, k_k, k_a, r_k, ln_x_w, ln_x_b, w_proj_b, w_out, g_ffn, w_ffn1, w_ffn2, g_final)))
    mom_m = dict(zip(WEIGHTS, (m_g_mix, m_w_in, m_sgu_ln_w, m_sgu_ln_b, m_sgu_w, m_sgu_b, m_w_proj_a, m_shift_b, m_w_lora_w, m_w0, m_a_lora_w, m_a0, m_g_lora_w, m_k_k, m_k_a, m_r_k, m_ln_x_w, m_ln_x_b, m_w_proj_b, m_w_out, m_g_ffn, m_w_ffn1, m_w_ffn2, m_g_final)))
    mom_v = dict(zip(WEIGHTS, (v_g_mix, v_w_in, v_sgu_ln_w, v_sgu_ln_b, v_sgu_w, v_sgu_b, v_w_proj_a, v_shift_b, v_w_lora_w, v_w0, v_a_lora_w, v_a0, v_g_lora_w, v_k_k, v_k_a, v_r_k, v_ln_x_w, v_ln_x_b, v_w_proj_b, v_w_out, v_g_ffn, v_w_ffn1, v_w_ffn2, v_g_final)))
    chip = 2 * lax.axis_index("x") + lax.axis_index("y")

    def local_block(tree, n):
        return tree[n] if n == "g_final" else tree[n][0]

    sb = local_block(given, "shift_b")
    lo_part = lambda z: (z - z.astype(BF16).astype(F32)).astype(BF16)
    row_form = lambda tree: (lambda n: _row_form(n, local_block(tree, n)))
    tile16 = lambda z: jnp.pad(z, ((0, 16 - z.shape[0]), (0, HALF_W - z.shape[1])))
    sb_tiles = [tile16(f(sb[:, lanes])) for f in (lambda z: z.astype(BF16), lo_part)
                for lanes in (slice(0, HALF_W), slice(HALF_W, None))]
    tails = [[_half_piece(n, lo_part(local_block(given, n)), h) for n in LORAS] + sb_tiles for h in range(2)]
    pack_w = jnp.stack([_pack_half(EARLY, row_form(given), h, BF16, tails[h]) for h in range(2)])
    gathered = gather_shards(pack_w)
    gathered = lax.dynamic_update_index_in_dim(gathered, pack_w, chip, 0)
    pack_late = jnp.stack([_pack_half(LATE, row_form(given), h, BF16) for h in range(2)])

    def whole(group, got, own):
        half = lambda n, j, h: jnp.where(chip == j, _piece(own[h], n), _piece(got[j, h], n))
        shard = lambda n, j: _join_halves(n, half(n, j, 0), half(n, j, 1))
        return {n: jnp.concatenate([shard(n, j).astype(F32 if n == "w_in" else BF16) for j in range(N_CHIPS)],
                                   axis=0 if n == "w_in" else SHARD_AXIS[n]) for n in group}

    w = whole(EARLY, gathered, pack_w)
    late_weights = lambda got: whole(LATE, got, pack_late)
    off = LO_OFF
    for n in LORAS:
        r, cols = PIECE_ROWS[n], SHARD_SHAPE[n][1]
        lo = jnp.concatenate([jnp.concatenate([gathered[j, 0, off:off + r, :cols], gathered[j, 1, off:off + r, :cols]],
                                              axis=0) for j in range(N_CHIPS)], axis=1)
        w[n] = w[n].astype(F32) + lo.astype(F32)
        off += r
    sb_tile = lambda j, t, lanes: gathered[j, 0, off + 16 * t:off + 16 * t + 2, :lanes].astype(F32)
    rest = SHIFT_SHARD[1] - HALF_W
    w["shift_b"] = jnp.concatenate(
        [jnp.concatenate([sb_tile(j, 0, HALF_W) + sb_tile(j, 2, HALF_W), sb_tile(j, 1, rest) + sb_tile(j, 3, rest)],
                         axis=1) for j in range(N_CHIPS)], axis=1)
    for n in SMALL:
        w[n] = local_block(given, n).reshape(SMALL_SHAPE[n])

    def partials(g_pack, tag):
        return pair_sum(g_pack, reduce_pair(g_pack, tag), tag, tm=PACK_TILE)

    loss, grad_x, grads, (late_part, late_slots) = local_step(
        x[0], loss_target[0], w, pack_late, late_weights, lambda g_pack: partials(g_pack, "late"))
    loss = lax.psum(loss, ("x", "y", "c"))

    early_g = jnp.stack([jnp.stack([_pack_half(EARLY, lambda n: _grad_row_form(n, grads[n], j), h, F32)
                                    for h in range(2)]) for j in range(N_CHIPS)])
    early_part, early_part16 = partials(early_g, "early")
    s_pack = _pack_rows([grads[n] for n in SMALL] + [grads["shift_b"]], SMALL_ROWS, F32)
    sends, recvs, part_thru, slots_thru, token = reduce_chips_start(early_part16)
    my_chip = lambda: 2 * lax.axis_index("x") + lax.axis_index("y")
    out_g, out_d, out_m, out_v = {}, {}, {}, {}

    def finish(group, tag, part, slots):
        half_sum = sum_with_own(part, slots, my_chip, token, tm=PACK_TILE, name="chip_sum_" + tag)
        other_half = exchange_halves(half_sum, tag)
        for n in group:
            res = adamw_weight(n, half_sum, other_half,
                               *[_row_form(n, local_block(t, n)) for t in (given, mom_m, mom_v)])
            for tree, z in zip((out_g, out_d, out_m, out_v), res):
                tree[n] = _row_form(n, z)

    finish(LATE, "late", late_part, late_slots)

    small_shapes = [SMALL_SHAPE[n] for n in SMALL]
    g_small = sum_all(s_pack, token)
    w_small = _pack_rows([local_block(given, n) for n in SMALL], SMALL_ROWS, F32)
    m_small = _pack_rows([local_block(mom_m, n) for n in SMALL], SMALL_ROWS, F32)
    v_small = _pack_rows([local_block(mom_v, n) for n in SMALL], SMALL_ROWS, F32)
    d_small, nm_small, nv_small = adamw_call(g_small, w_small, m_small, v_small, tm=SMALL_ROWS, name="adamw_small")
    g_parts = _unpack_rows(g_small, small_shapes + [(2, N_RWKV)])
    out_g.update(zip(SMALL, g_parts[:-1]))
    out_d.update(zip(SMALL, _unpack_rows(d_small, small_shapes)))
    out_m.update(zip(SMALL, _unpack_rows(nm_small, small_shapes)))
    out_v.update(zip(SMALL, _unpack_rows(nv_small, small_shapes)))
    g_sb = lax.dynamic_slice_in_dim(g_parts[-1], chip * SHIFT_SHARD[1], SHIFT_SHARD[1], axis=1)
    sb_args = [_pack_rows([z], 8, F32) for z in (g_sb, sb, local_block(mom_m, "shift_b"), local_block(mom_v, "shift_b"))]
    sb_res = adamw_call(*sb_args, tm=8, name="adamw_shift_b")
    out_g["shift_b"] = g_sb
    for tree, res in zip((out_d, out_m, out_v), sb_res):
        tree["shift_b"] = _unpack_rows(res, [SHIFT_SHARD])[0]

    after = (out_v["w_out"], nv_small, sb_res[2])
    early_slots = reduce_chips_wait(sends, recvs, part_thru, slots_thru, jnp.concatenate([z.reshape(-1)[:8] for z in after]))
    finish(EARLY, "early", early_part, early_slots)

    def block_of(tree, n):
        return tree[n].reshape(given[n].shape)

    return (loss, grad_x[None], *[block_of(out_g, n) for n in WEIGHTS], *[block_of(out_d, n) for n in WEIGHTS],
            *[block_of(out_m, n) for n in WEIGHTS], *[block_of(out_v, n) for n in WEIGHTS])
```

```python
import functools

import jax
import jax.numpy as jnp
from jax import lax
from jax.experimental import pallas as pl
from jax.experimental.pallas import tpu as pltpu

F32 = jnp.float32
BF16 = jnp.bfloat16

D_MODEL = 1024
N_HEADS = 16
HEAD = 64
SCAN_CHUNK = 64

VMEM_LIMIT = 56 * 1024 * 1024


_BDIMS = {
    "nn": (((2,), (1,)), ((0,), (0,))),
    "nt": (((2,), (2,)), ((0,), (0,))),
    "tn": (((1,), (1,)), ((0,), (0,))),
}


def _raw_bdot(x, y, mode, fine):
    if fine:
        return lax.dot_general(x, y, _BDIMS[mode], precision=lax.Precision.HIGH, preferred_element_type=F32)
    return lax.dot_general(x.astype(BF16), y.astype(BF16), _BDIMS[mode], preferred_element_type=F32)


@functools.partial(jax.custom_vjp, nondiff_argnums=(2, 3))
def bdot(x, y, mode, fine=True):
    return _raw_bdot(x, y, mode, fine)


def _bdot_fwd(x, y, mode, fine):
    return _raw_bdot(x, y, mode, fine), (x, y)


def _bdot_bwd(mode, fine, res, g):
    x, y = res
    if mode == "nn":
        return bdot(g, y, "nt", fine), bdot(x, g, "tn", fine)
    if mode == "nt":
        return bdot(g, y, "nn", fine), bdot(g, x, "tn", fine)
    return bdot(y, g, "nt", fine), bdot(x, g, "nn", fine)


bdot.defvjp(_bdot_fwd, _bdot_bwd)


def _scan_chunk(S0, r, lw, k, v, a, b):
    nh, lc, _ = r.shape
    ti = lax.broadcasted_iota(jnp.int32, (lc, lc), 0)
    si = lax.broadcasted_iota(jnp.int32, (lc, lc), 1)
    incl = (si <= ti).astype(F32)
    strict = (si < ti).astype(F32)
    eye = (si == ti).astype(F32)
    cl = bdot(jnp.broadcast_to(incl, (nh, lc, lc)), lw, "nn")
    cl_last = cl[:, lc - 1:lc, :]
    g_last = jnp.exp(cl_last - cl)
    at = a * jnp.exp(cl - lw)
    bt = b * jnp.exp(-cl)
    kt = k * jnp.exp(-cl)
    rt = r * jnp.exp(cl)
    ar = jnp.concatenate([at, rt], axis=1)
    ar_b = bdot(ar, bt, "nt", False)
    ar_k = bdot(ar, kt, "nt", False)
    m_ab, m_rb = ar_b[:, :lc] * strict, ar_b[:, lc:] * incl
    m_ak, m_rk = ar_k[:, :lc] * strict, ar_k[:, lc:] * incl
    x = eye + m_ab
    p = bdot(m_ab, m_ab, "nn", False)
    n = 2
    while n * 2 < lc:
        px = bdot(jnp.concatenate([p, x], axis=1), p, "nn", False)
        p = px[:, :lc]
        x = x + px[:, lc:]
        n *= 2
    x = x + bdot(x, p, "nn", False)
    ar_s = bdot(ar, S0, "nt", False)
    akrk_v = bdot(jnp.concatenate([m_ak, m_rk], axis=1), v, "nn", False)
    u = bdot(x, ar_s[:, :lc] + akrk_v[:, :lc], "nn", False)
    o = ar_s[:, lc:] + bdot(m_rb, u, "nn", False) + akrk_v[:, lc:]
    s_last = S0 * jnp.exp(cl_last) + bdot(jnp.concatenate([u, v], axis=1),
                                          jnp.concatenate([b * g_last, k * g_last], axis=1), "tn", False)
    return o, s_last


def _split_heads(z):
    return jnp.stack([z[:, HEAD * h:HEAD * (h + 1)] for h in range(N_HEADS)], axis=0)


def _merge_heads(z):
    return jnp.concatenate([z[h] for h in range(N_HEADS)], axis=1)


def _scan_specs(t, ops, rev):
    nc = t // SCAN_CHUNK
    row = (lambda c: nc - 1 - c) if rev else (lambda c: c)
    specs = [pl.BlockSpec((SCAN_CHUNK, D_MODEL), lambda c, cb=cb: (row(c), cb)) for _, cb in ops]
    state = pl.BlockSpec((1, N_HEADS, HEAD, HEAD), lambda c: (row(c), 0, 0, 0))
    return nc, specs, state


def scan_fwd(ops, pack):
    t = ops[0][0].shape[0]
    nc, specs, state = _scan_specs(t, ops, False)

    def body(r_ref, lw_ref, k_ref, v_ref, a_ref, b_ref, pack_ref, o_ref, s0_ref, all_ref, s_scr, send_sems, recv_sems):
        step = pl.program_id(0)
        x, y, c = _coords()
        me = 2 * x + y
        sib = (x, y, 1 - c)
        chips = _other_chips(x, y)
        first = [_remote(pack_ref.at[c], all_ref.at[me, c], send_sems, recv_sems, k, (cx, cy, c))
                 for k, (cx, cy) in enumerate(chips)]
        passed = [_remote(all_ref.at[2 * cx + cy, c], all_ref.at[2 * cx + cy, c], send_sems, recv_sems, 3 + k, sib)
                  for k, (cx, cy) in enumerate(chips)]

        @pl.when(step == 0)
        def _():
            s_scr[...] = jnp.zeros_like(s_scr)
            for cp in first:
                cp.start()

        s0 = s_scr[...]
        s0_ref[0] = s0
        o, s_last = _scan_chunk(s0, *[_split_heads(z[...]) for z in (r_ref, lw_ref, k_ref, v_ref, a_ref, b_ref)])
        o_ref[...] = _merge_heads(o)
        s_scr[...] = s_last

        @pl.when(step == nc - 1)
        def _():
            for k, (cx, cy) in enumerate(chips):
                j = 2 * cx + cy
                _remote(pack_ref.at[c], all_ref.at[j, c], send_sems, recv_sems, k, (cx, cy, c)).wait_recv()
                passed[k].start()
            for k, (cx, cy) in enumerate(chips):
                j = 2 * cx + cy
                _remote(all_ref.at[j, 1 - c], all_ref.at[j, 1 - c], send_sems, recv_sems, 3 + k, sib).wait_recv()
            for cp in first + passed:
                cp.wait_send()

    return pl.pallas_call(
        body,
        name="scan_fwd",
        grid=(nc,),
        in_specs=specs + [_ANY],
        out_specs=[pl.BlockSpec((SCAN_CHUNK, D_MODEL), lambda c: (c, 0)), state, _ANY],
        out_shape=[jax.ShapeDtypeStruct((t, D_MODEL), F32), jax.ShapeDtypeStruct((nc, N_HEADS, HEAD, HEAD), F32),
                   jax.ShapeDtypeStruct((N_CHIPS,) + pack.shape, pack.dtype)],
        scratch_shapes=[pltpu.VMEM((N_HEADS, HEAD, HEAD), F32), pltpu.SemaphoreType.DMA((6,)),
                        pltpu.SemaphoreType.DMA((6,))],
        compiler_params=_cparams(1),
    )(*[a for a, _ in ops], pack)


def scan_bwd(ops, s0s, do, part):
    t = ops[0][0].shape[0]
    nc, specs, state = _scan_specs(t, ops + [(do, 0)], True)

    def body(r_ref, lw_ref, k_ref, v_ref, a_ref, b_ref, do_ref, s0_ref, part_ref, *rest):
        out_refs, slots_ref, ds_scr, send_sems, recv_sems = rest[:6], rest[6], rest[7], rest[8], rest[9]
        step = pl.program_id(0)
        x, y, c = _coords()
        me = 2 * x + y
        chips = _other_chips(x, y)
        sends = [_remote(part_ref.at[2 * cx + cy], slots_ref.at[me], send_sems, recv_sems, k, (cx, cy, c))
                 for k, (cx, cy) in enumerate(chips)]

        @pl.when(step == 0)
        def _():
            ds_scr[...] = jnp.zeros_like(ds_scr)
            for cp in sends:
                cp.start()

        _, vjp = jax.vjp(_scan_chunk, s0_ref[0],
                         *[_split_heads(z[...]) for z in (r_ref, lw_ref, k_ref, v_ref, a_ref, b_ref)])
        grads = vjp((_split_heads(do_ref[...]), ds_scr[...]))
        for o_ref, g in zip(out_refs, grads[1:]):
            o_ref[...] = _merge_heads(g)
        ds_scr[...] = grads[0]

        @pl.when(step == nc - 1)
        def _():
            for k, (cx, cy) in enumerate(chips):
                _remote(part_ref.at[me], slots_ref.at[2 * cx + cy], send_sems, recv_sems, k, (cx, cy, c)).wait_recv()
            for cp in sends:
                cp.wait_send()

    return pl.pallas_call(
        body,
        name="scan_bwd",
        grid=(nc,),
        in_specs=specs + [state, _ANY],
        out_specs=[pl.BlockSpec((SCAN_CHUNK, D_MODEL), lambda c: (nc - 1 - c, 0))] * 6 + [_ANY],
        out_shape=[jax.ShapeDtypeStruct((t, D_MODEL), F32)] * 6 + [jax.ShapeDtypeStruct(part.shape, part.dtype)],
        scratch_shapes=[pltpu.VMEM((N_HEADS, HEAD, HEAD), F32), pltpu.SemaphoreType.DMA((3,)),
                        pltpu.SemaphoreType.DMA((3,))],
        compiler_params=_cparams(1),
    )(*[a for a, _ in ops], do, s0s, part)


_MDIMS = {
    "nn": (((1,), (0,)), ((), ())),
    "nt": (((1,), (1,)), ((), ())),
    "tn": (((0,), (0,)), ((), ())),
}


def _raw_mdot(x, y, mode, exact):
    if exact:
        return lax.dot_general(x, y, _MDIMS[mode], precision=lax.Precision.HIGH, preferred_element_type=F32)
    return lax.dot_general(x.astype(BF16), y.astype(BF16), _MDIMS[mode], preferred_element_type=F32)


@functools.partial(jax.custom_vjp, nondiff_argnums=(2, 3))
def mdot(x, y, mode, exact):
    return _raw_mdot(x, y, mode, exact)


def _mdot_fwd(x, y, mode, exact):
    return _raw_mdot(x, y, mode, exact), (x, y)


def _mdot_bwd(mode, exact, res, g):
    x, y = res
    if mode == "nn":
        return mdot(g, y, "nt", exact), mdot(x, g, "tn", exact)
    if mode == "nt":
        return mdot(g, y, "nn", exact), mdot(g, x, "tn", exact)
    return mdot(y, g, "nt", exact), mdot(x, g, "nn", exact)


mdot.defvjp(_mdot_fwd, _mdot_bwd)


def _seg_ones():
    i = lax.broadcasted_iota(jnp.int32, (256, 256), 0) // HEAD
    j = lax.broadcasted_iota(jnp.int32, (256, 256), 1) // HEAD
    return (i == j).astype(BF16)


@jax.custom_vjp
def segsum(x):
    bd = _seg_ones()
    hi = x.astype(BF16)
    lo = (x - hi.astype(F32)).astype(BF16)
    cols = []
    for j in range(x.shape[1] // 256):
        sl = slice(256 * j, 256 * (j + 1))
        cols.append(jnp.dot(hi[:, sl], bd, preferred_element_type=F32)
                    + jnp.dot(lo[:, sl], bd, preferred_element_type=F32))
    return jnp.concatenate(cols, axis=1)


segsum.defvjp(lambda x: (segsum(x), None), lambda _, g: (segsum(g),))


NORM_EPS = 1e-6
LN_EPS = 1e-5
GN_EPS = 64e-5
SGU_CHUNK = 128
SGU_GROUPS = 8


def _rms(x, g):
    return x * lax.rsqrt(jnp.mean(x * x, axis=-1, keepdims=True) + NORM_EPS) * g


def f_norm_in(x, g):
    return _rms(x, g), x


def f_sgu(p, ln_w, ln_b, sw, sbt):
    tm = p.shape[0]
    z = 0.5 * p * (1.0 + lax.erf(p * 0.7071067811865476))
    u, v = z[:, :D_MODEL], z[:, D_MODEL:]
    mu = jnp.mean(v, axis=-1, keepdims=True)
    d = v - mu
    vn = d * lax.rsqrt(jnp.mean(d * d, axis=-1, keepdims=True) + LN_EPS) * ln_w + ln_b
    ii = lax.broadcasted_iota(jnp.int32, (SGU_CHUNK, SGU_CHUNK), 0)
    jj = lax.broadcasted_iota(jnp.int32, (SGU_CHUNK, SGU_CHUNK), 1)
    mask = (jj <= ii).astype(F32)
    gi = lax.broadcasted_iota(jnp.int32, (SGU_GROUPS, D_MODEL), 0)
    ci = lax.broadcasted_iota(jnp.int32, (SGU_GROUPS, D_MODEL), 1) // SGU_CHUNK
    bias = mdot(sbt, (gi == ci).astype(F32), "nn", True)
    rows = []
    for c in range(tm // SGU_CHUNK):
        cols = []
        for g in range(SGU_GROUPS):
            blk = vn[c * SGU_CHUNK:(c + 1) * SGU_CHUNK, g * SGU_CHUNK:(g + 1) * SGU_CHUNK]
            cols.append(mdot(sw[g] * mask, blk, "nn", False))
        rows.append(jnp.concatenate(cols, axis=1) + bias)
    return (u * jnp.concatenate(rows, axis=0),)


def _softplus(x):
    return jnp.maximum(x, 0.0) + jnp.log1p(jnp.exp(-jnp.abs(x)))


def f_pre(qr, qk, qv, ql, wl, w0, al, a0, gl, k_k, k_a):
    xw, xa, xg = ql[:, :128], ql[:, 128:256], ql[:, 256:512]
    wr = -_softplus(-(w0 + mdot(jnp.tanh(xw), wl, "nn", True))) - 0.5
    lw = -jnp.exp(wr)
    aa = jax.nn.sigmoid(a0 + mdot(xa, al, "nn", True))
    g = mdot(jax.nn.sigmoid(xg), gl, "nn", True)
    kkr = qk * k_k
    kk = kkr / jnp.maximum(jnp.sqrt(segsum(kkr * kkr)), 1e-12)
    kp = qk * (1.0 + (aa - 1.0) * k_a)
    return qr, lw, kp, qv, -kk, kk * aa, g, qr, kp, qv


def f_post(o, r, kp, v, g, lnw, lnb, rk):
    mu = segsum(o) * (1.0 / HEAD)
    d = o - mu
    gn = d * lax.rsqrt(segsum(d * d) * (1.0 / HEAD) + GN_EPS)
    return ((gn * lnw + lnb + segsum(r * kp * rk) * v) * g,)


def f_mix(ya, yb, ga, gb):
    return (jax.nn.sigmoid(ga) * ya + jax.nn.sigmoid(gb) * yb,)


def f_ffn_in(h1, g):
    return _rms(h1, g), h1


def f_final(h1, m3, tgt, g):
    y = _rms(h1 + m3, g)
    err = jnp.square(y - tgt)
    return 0.5 * jnp.sum(jnp.mean(err, axis=-1))


def _cparams(n_grid):
    return pltpu.CompilerParams(dimension_semantics=("arbitrary",) * n_grid, vmem_limit_bytes=VMEM_LIMIT)


def _tile_spec(tm, w, cb):
    return pl.BlockSpec((tm, w), lambda i: (i, cb))


def _const_spec(c):
    nd = c.ndim
    return pl.BlockSpec(c.shape, lambda i: (0,) * nd)


def ew_call(fn, tiled, consts, outs, *, tm, name):
    t = tiled[0][0].shape[0]
    n_t, n_c = len(tiled), len(consts)

    def body(*refs):
        tv = [r[...].astype(F32) for r in refs[:n_t]]
        cv = [r[...] for r in refs[n_t:n_t + n_c]]
        res = fn(*tv, *cv)
        for o_ref, val in zip(refs[n_t + n_c:], res):
            o_ref[...] = val.astype(o_ref.dtype)

    return pl.pallas_call(
        body,
        name=name,
        grid=(t // tm,),
        in_specs=[_tile_spec(tm, w, cb) for _, w, cb in tiled] + [_const_spec(c) for c in consts],
        out_specs=[_tile_spec(tm, w, 0) for w, _ in outs],
        out_shape=[jax.ShapeDtypeStruct((t, w), dt) for w, dt in outs],
        compiler_params=_cparams(1),
    )(*[a for a, _, _ in tiled], *consts)


def ew_vjp_call(fn, tiled, consts, cots, d_tiled, d_consts, *, tm, name):
    t = tiled[0][0].shape[0]
    n_t, n_c, n_g = len(tiled), len(consts), len(cots)
    dt_list = [(i, dt) for i, dts in enumerate(d_tiled) for dt in dts]
    dc_list = [i for i, want in enumerate(d_consts) if want]

    def body(*refs):
        tv = [r[...].astype(F32) for r in refs[:n_t]]
        cv = [r[...] for r in refs[n_t:n_t + n_c]]
        gv = tuple(r[...].astype(F32) for r in refs[n_t + n_c:n_t + n_c + n_g])
        out_refs = refs[n_t + n_c + n_g:]
        _, vjp = jax.vjp(fn, *tv, *cv)
        grads = vjp(gv)
        for o_ref, (i, _) in zip(out_refs, dt_list):
            o_ref[...] = grads[i].astype(o_ref.dtype)
        acc_refs = out_refs[len(dt_list):]

        @pl.when(pl.program_id(0) == 0)
        def _():
            for a_ref in acc_refs:
                a_ref[...] = jnp.zeros_like(a_ref)

        for a_ref, i in zip(acc_refs, dc_list):
            a_ref[...] += grads[n_t + i]

    res = pl.pallas_call(
        body,
        name=name,
        grid=(t // tm,),
        in_specs=[_tile_spec(tm, w, cb) for _, w, cb in tiled] + [_const_spec(c) for c in consts]
        + [_tile_spec(tm, w, cb) for _, w, cb in cots],
        out_specs=[_tile_spec(tm, tiled[i][1], 0) for i, _ in dt_list] + [_const_spec(consts[i]) for i in dc_list],
        out_shape=[jax.ShapeDtypeStruct((t, tiled[i][1]), dt) for i, dt in dt_list]
        + [jax.ShapeDtypeStruct(consts[i].shape, F32) for i in dc_list],
        compiler_params=_cparams(1),
    )(*[a for a, _, _ in tiled], *consts, *[a for a, _, _ in cots])
    return res[:len(dt_list)], res[len(dt_list):]


def mm(a, b, mode, *, tm, tn, name, out_dtypes=(F32,), epi=None, extras=(), into=None):
    m = a.shape[1] if mode == "tn" else a.shape[0]
    kd = a.shape[0] if mode == "tn" else a.shape[1]
    n = b.shape[0] if mode == "nt" else b.shape[1]
    tm, tn = min(tm, m), min(tn, n)
    if mode == "nn":
        a_spec = pl.BlockSpec((tm, kd), lambda i, j: (i, 0))
        b_spec = pl.BlockSpec((kd, tn), lambda i, j: (0, j))
    elif mode == "nt":
        a_spec = pl.BlockSpec((tm, kd), lambda i, j: (i, 0))
        b_spec = pl.BlockSpec((tn, kd), lambda i, j: (j, 0))
    else:
        a_spec = pl.BlockSpec((kd, tm), lambda i, j: (0, i))
        b_spec = pl.BlockSpec((kd, tn), lambda i, j: (0, j))
    n_e = len(extras)
    o_spec = pl.BlockSpec((tm, tn), lambda i, j: (i, j))

    if into is not None:
        buf, place = into

        def body_into(a_ref, b_ref, buf_ref, o_ref):
            o_ref[0, 0] = lax.dot_general(a_ref[...].astype(BF16), b_ref[...].astype(BF16), _MDIMS[mode],
                                          preferred_element_type=F32)

        return pl.pallas_call(
            body_into,
            name=name,
            grid=(m // tm, n // tn),
            in_specs=[a_spec, b_spec, pl.BlockSpec(memory_space=pl.ANY)],
            out_specs=pl.BlockSpec((1, 1, tm, tn), lambda i, j: (*place(i, j), 0)),
            out_shape=jax.ShapeDtypeStruct(buf.shape, F32),
            input_output_aliases={2: 0},
            compiler_params=_cparams(2),
        )(a, b, buf)

    def body(a_ref, b_ref, *refs):
        c = lax.dot_general(a_ref[...].astype(BF16), b_ref[...].astype(BF16), _MDIMS[mode],
                            preferred_element_type=F32)
        res = epi(c, *[r[...] for r in refs[:n_e]]) if epi is not None else (c,)
        for o_ref, val in zip(refs[n_e:], res):
            o_ref[...] = val.astype(o_ref.dtype)

    res = pl.pallas_call(
        body,
        name=name,
        grid=(m // tm, n // tn),
        in_specs=[a_spec, b_spec] + [o_spec] * n_e,
        out_specs=[o_spec] * len(out_dtypes),
        out_shape=[jax.ShapeDtypeStruct((m, n), dt) for dt in out_dtypes],
        compiler_params=_cparams(2),
    )(a, b, *extras)
    return res if len(out_dtypes) > 1 else res[0]


P_WIDTH = 7680
RWKV_COL0 = 4096
RWKV_WIDTH = 3584
SHIFT_BLK = 512


def _shift_down(p, prev_row):
    rows = lax.broadcasted_iota(jnp.int32, p.shape, 0)
    return jnp.where(rows == 0, prev_row, pltpu.roll(p, 1, 0))


def shiftmix_fwd(p_all, sbp, *, tm):
    t = p_all.shape[0]
    tm = min(tm, t)
    c0 = RWKV_COL0 // SHIFT_BLK
    hb = tm // 8

    def body(p_ref, halo_ref, sb_ref, q_ref):
        p = p_ref[...]
        prev = jnp.where(pl.program_id(0) == 0, 0.0, halo_ref[7:8, :])
        q_ref[...] = p * sb_ref[0:1, :] + _shift_down(p, prev) * sb_ref[1:2, :]

    return pl.pallas_call(
        body,
        name="shiftmix_fwd",
        grid=(t // tm, RWKV_WIDTH // SHIFT_BLK),
        in_specs=[
            pl.BlockSpec((tm, SHIFT_BLK), lambda i, j: (i, c0 + j)),
            pl.BlockSpec((8, SHIFT_BLK), lambda i, j: (jnp.maximum(i * hb - 1, 0), c0 + j)),
            pl.BlockSpec((2, SHIFT_BLK), lambda i, j: (0, j)),
        ],
        out_specs=pl.BlockSpec((tm, SHIFT_BLK), lambda i, j: (i, j)),
        out_shape=jax.ShapeDtypeStruct((t, RWKV_WIDTH), F32),
        compiler_params=_cparams(2),
    )(p_all, p_all, sbp)


def shiftmix_bwd(dq, col0, p_all, sbp, *, tm, name):
    t, w = dq.shape
    n_i = t // tm
    hb = tm // 8
    cq = col0 // SHIFT_BLK
    cp = (RWKV_COL0 + col0) // SHIFT_BLK

    def body(dq_ref, dqn_ref, p_ref, ph_ref, sb_ref, dp_ref, dsb_ref):
        i = pl.program_id(1)
        dq_t = dq_ref[...]
        rows = lax.broadcasted_iota(jnp.int32, dq_t.shape, 0)
        nxt = jnp.where(i == n_i - 1, 0.0, dqn_ref[0:1, :])
        up = jnp.where(rows == tm - 1, nxt, pltpu.roll(dq_t, tm - 1, 0))
        dp_ref[...] = (dq_t * sb_ref[0:1, :] + up * sb_ref[1:2, :]).astype(dp_ref.dtype)
        p = p_ref[...]
        prev = jnp.where(i == 0, 0.0, ph_ref[7:8, :])
        s0 = jnp.sum(dq_t * p, axis=0, keepdims=True)
        s1 = jnp.sum(dq_t * _shift_down(p, prev), axis=0, keepdims=True)
        two = lax.broadcasted_iota(jnp.int32, (2, SHIFT_BLK), 0)

        @pl.when(i == 0)
        def _():
            dsb_ref[...] = jnp.zeros_like(dsb_ref)

        dsb_ref[...] += jnp.where(two == 0, s0, s1)

    return pl.pallas_call(
        body,
        name=name,
        grid=(w // SHIFT_BLK, n_i),
        in_specs=[
            pl.BlockSpec((tm, SHIFT_BLK), lambda j, i: (i, j)),
            pl.BlockSpec((8, SHIFT_BLK), lambda j, i: (jnp.minimum((i + 1) * hb, t // 8 - 1), j)),
            pl.BlockSpec((tm, SHIFT_BLK), lambda j, i: (i, cp + j)),
            pl.BlockSpec((8, SHIFT_BLK), lambda j, i: (jnp.maximum(i * hb - 1, 0), cp + j)),
            pl.BlockSpec((2, SHIFT_BLK), lambda j, i: (0, cq + j)),
        ],
        out_specs=[
            pl.BlockSpec((tm, SHIFT_BLK), lambda j, i: (i, j)),
            pl.BlockSpec((2, SHIFT_BLK), lambda j, i: (0, j)),
        ],
        out_shape=[jax.ShapeDtypeStruct((t, w), BF16), jax.ShapeDtypeStruct((2, w), F32)],
        compiler_params=_cparams(2),
    )(dq, dq, p_all, p_all, sbp)


def final_call(h1, m3, tgt, g_final, *, tm):
    t = h1.shape[0]

    def body(h1_ref, m3_ref, tgt_ref, g_ref, dh_ref, dhb_ref, dg_ref, loss_ref):
        loss, vjp = jax.vjp(f_final, h1_ref[...], m3_ref[...], tgt_ref[...], g_ref[...])
        dh, _, _, dg = vjp(jnp.ones((), F32))
        dh_ref[...] = dh
        dhb_ref[...] = dh.astype(BF16)

        @pl.when(pl.program_id(0) == 0)
        def _():
            dg_ref[...] = jnp.zeros_like(dg_ref)
            loss_ref[...] = jnp.zeros_like(loss_ref)

        dg_ref[...] += dg
        loss_ref[...] += jnp.full(loss_ref.shape, loss, F32)

    tile = _tile_spec(tm, D_MODEL, 0)
    return pl.pallas_call(
        body,
        name="final_loss",
        grid=(t // tm,),
        in_specs=[tile, tile, tile, _const_spec(g_final)],
        out_specs=[tile, tile, _const_spec(g_final), pl.BlockSpec((8, 128), lambda i: (0, 0))],
        out_shape=[jax.ShapeDtypeStruct((t, D_MODEL), F32), jax.ShapeDtypeStruct((t, D_MODEL), BF16),
                   jax.ShapeDtypeStruct(g_final.shape, F32), jax.ShapeDtypeStruct((8, 128), F32)],
        compiler_params=_cparams(1),
    )(h1, m3, tgt, g_final)


N_SGU = 2048
N_RWKV = 3360
LORA_W, LORA_A, LORA_G = 64, 64, 160


def _pad_rwkv_cols(z):
    zero = lambda n: jnp.zeros(z.shape[:-1] + (n,), z.dtype)
    return jnp.concatenate([z[..., :3072], z[..., 3072:3136], zero(64), z[..., 3136:3200], zero(64),
                            z[..., 3200:3360], zero(96)], axis=-1)


def _unpad_rwkv_cols(z):
    return jnp.concatenate([z[..., :3072], z[..., 3072:3136], z[..., 3200:3264], z[..., 3328:3488]], axis=-1)


def _pad_win_rows(wt):
    z = wt[N_SGU:N_SGU + N_RWKV]
    zero = lambda n: jnp.zeros((n, wt.shape[1]), wt.dtype)
    return jnp.concatenate([wt[:N_SGU], wt[N_SGU + N_RWKV:], z[:3072], z[3072:3136], zero(64), z[3136:3200], zero(64),
                            z[3200:3360], zero(96)], axis=0)


def _unpad_win_rows(wt):
    z = wt[RWKV_COL0:]
    return jnp.concatenate([wt[:N_SGU], z[:3072], z[3072:3136], z[3200:3264], z[3328:3488], wt[N_SGU:RWKV_COL0]],
                           axis=0)


def _pad_rows(w, n):
    return jnp.concatenate([w, jnp.zeros((n - w.shape[0],) + w.shape[1:], w.dtype)], axis=0)


def _relu2_epi(c):
    return c, jnp.square(jnp.maximum(c, 0.0))


def _relu2_bwd_epi(c, hid):
    return (c * (2.0 * jnp.maximum(hid.astype(F32), 0.0)),)


def _add_epi(c, x):
    return (c + x,)


def _pre_fwd(*args):
    res = f_pre(*args)
    return res[1], res[2], res[4], res[5], res[6]


def local_step(x, tgt, w, late_pack, late_weights, pair_start, pair_finish, pack_early):
    d = D_MODEL
    win_pt = _pad_win_rows(w["w_in"])
    sbp = _pad_rwkv_cols(w["shift_b"])
    wl = _pad_rows(w["w_lora_w"], 128)
    al = _pad_rows(w["a_lora_w"], 128)
    gl = _pad_rows(w["g_lora_w"], 256)
    sbt = w["sgu_b"].T

    (a_bf,) = ew_call(lambda x_, g_: (f_norm_in(x_, g_)[0],), [(x, d, 0)], [w["g_mix"]], [(d, BF16)], tm=256,
                      name="norm_in")
    p_all = mm(a_bf, win_pt, "nt", tm=2048, tn=640, name="mm_in")
    sgu_t = [(p_all, 2 * d, 0)]
    sgu_c = [w["sgu_ln_w"], w["sgu_ln_b"], w["sgu_w"], sbt]
    (s_bf,) = ew_call(f_sgu, sgu_t, sgu_c, [(d, BF16)], tm=256, name="sgu_fwd")
    ya = mm(s_bf, w["w_proj_a"], "nn", tm=512, tn=1024, name="mm_proj_a")
    q = shiftmix_fwd(p_all, sbp, tm=1024)
    pre_t = [(q, d, 0), (q, d, 1), (q, d, 2), (q, 512, 6)]
    pre_c = [wl, w["w0"], al, w["a0"], gl, w["k_k"], w["k_a"]]
    lw, kp, na, nb, g = ew_call(_pre_fwd, pre_t, pre_c, [(d, F32)] * 5, tm=256, name="rwkv_pre_fwd")
    scan_ops = [(q, 0), (lw, 0), (kp, 0), (q, 2), (na, 0), (nb, 0)]
    o, s0s, late_all = scan_fwd(scan_ops, late_pack)
    w = {**w, **late_weights(late_all)}
    post_t = [(o, d, 0), (q, d, 0), (kp, d, 0), (q, d, 2), (g, d, 0)]
    post_c = [w["ln_x_w"], w["ln_x_b"], w["r_k"]]
    (ob_bf,) = ew_call(f_post, post_t, post_c, [(d, BF16)], tm=256, name="rwkv_post_fwd")
    yb = mm(ob_bf, w["w_proj_b"], "nn", tm=512, tn=1024, name="mm_proj_b")
    mix_t = [(ya, d, 0), (yb, d, 0), (p_all, d, 2), (p_all, d, 3)]
    (mixed_bf,) = ew_call(f_mix, mix_t, [], [(d, BF16)], tm=256, name="mix_fwd")
    h1 = mm(mixed_bf, w["w_out"], "nn", tm=512, tn=1024, name="mm_out", epi=_add_epi, extras=(x,))
    (f_bf,) = ew_call(lambda h_, g_: (f_ffn_in(h_, g_)[0],), [(h1, d, 0)], [w["g_ffn"]], [(d, BF16)], tm=256,
                      name="ffn_norm")
    hid, act_bf = mm(f_bf, w["w_ffn1"], "nn", tm=2048, tn=1024, name="mm_ffn1", out_dtypes=(BF16, BF16), epi=_relu2_epi)
    m3 = mm(act_bf, w["w_ffn2"], "nn", tm=1024, tn=512, name="mm_ffn2")
    dh2, dh2_bf, dg_final, loss = final_call(h1, m3, tgt, w["g_final"], tm=256)

    dhid_bf = mm(dh2_bf, w["w_ffn2"], "nt", tm=2048, tn=1024, name="mm_dact", out_dtypes=(BF16,), epi=_relu2_bwd_epi,
                 extras=(hid,))
    late_g = lax.empty((N_CHIPS, 2, PACK_ROWS, HALF_W), F32)
    late_g = mm(act_bf, dh2_bf, "tn", tm=512, tn=HALF_W, name="mm_dw_ffn2",
                into=(late_g, lambda i, j: (i // 2, j, PIECE_OFF["w_ffn2"] // 512 + i % 2)))
    df = mm(dhid_bf, w["w_ffn1"], "nt", tm=1024, tn=512, name="mm_df")
    late_g = mm(f_bf, dhid_bf, "tn", tm=512, tn=HALF_W, name="mm_dw_ffn1",
                into=(late_g, lambda i, j: (j // 2, j % 2, PIECE_OFF["w_ffn1"] // 512 + i)))
    (dh1, dh1_bf), (dg_ffn,) = ew_vjp_call(f_ffn_in, [(h1, d, 0)], [w["g_ffn"]], [(df, d, 0), (dh2, d, 0)],
                                           [(F32, BF16)], [True], tm=256, name="ffn_norm_bwd")
    dmixed = mm(dh1_bf, w["w_out"], "nt", tm=512, tn=1024, name="mm_dmixed")
    late_g = mm(mixed_bf, dh1_bf, "tn", tm=256, tn=HALF_W, name="mm_dw_out",
                into=(late_g, lambda i, j: (i, j, PIECE_OFF["w_out"] // 256)))
    (dya_bf, dyb_bf, dga_bf, dgb_bf), _ = ew_vjp_call(f_mix, mix_t, [], [(dmixed, d, 0)], [(BF16,)] * 4, [], tm=256,
                                                      name="mix_bwd")
    dob = mm(dyb_bf, w["w_proj_b"], "nt", tm=512, tn=1024, name="mm_dob")
    late_g = mm(ob_bf, dyb_bf, "tn", tm=256, tn=HALF_W, name="mm_dw_proj_b",
                into=(late_g, lambda i, j: (i, j, PIECE_OFF["w_proj_b"] // 256)))
    late_state, late_token = pair_start(late_g, "late")
    post_c_after = [w["ln_x_w"] + late_token[:1, :1]] + post_c[1:]
    (do, dr_p, dkp_p, dv_p, dg), (dlnx_w, dlnx_b, dr_k) = ew_vjp_call(
        f_post, post_t, post_c_after, [(dob, d, 0)], [(F32,)] * 5, [True] * 3, tm=256, name="rwkv_post_bwd")
    late_part, late_part16 = pair_finish(late_state, do, "late")
    *scan_g, late_slots = scan_bwd(scan_ops, s0s, do, late_part16)
    pre_g = [(z, d, 0) for z in scan_g] + [(dg, d, 0), (dr_p, d, 0), (dkp_p, d, 0), (dv_p, d, 0)]
    (dq_r, dq_k, dq_v, dq_l), (dwl, dw0, dal, da0, dgl, dk_k, dk_a) = ew_vjp_call(
        f_pre, pre_t, pre_c, pre_g, [(F32,)] * 4, [True] * 7, tm=128, name="rwkv_pre_bwd")
    dp_r, dsb_r = shiftmix_bwd(dq_r, 0, p_all, sbp, tm=256, name="shiftmix_bwd_r")
    dp_k, dsb_k = shiftmix_bwd(dq_k, d, p_all, sbp, tm=256, name="shiftmix_bwd_k")
    dp_v, dsb_v = shiftmix_bwd(dq_v, 2 * d, p_all, sbp, tm=256, name="shiftmix_bwd_v")
    dp_l, dsb_l = shiftmix_bwd(dq_l, 3 * d, p_all, sbp, tm=256, name="shiftmix_bwd_l")
    ds = mm(dya_bf, w["w_proj_a"], "nt", tm=512, tn=1024, name="mm_ds")
    d_proj_a = mm(s_bf, dya_bf, "tn", tm=512, tn=1024, name="mm_dw_proj_a")
    (dp_sgu,), (dln_w, dln_b, dsw, dsbt) = ew_vjp_call(f_sgu, sgu_t, sgu_c, [(ds, d, 0)], [(BF16,)], [True] * 4,
                                                       tm=256, name="sgu_bwd")
    dp_all = jnp.concatenate([dp_sgu, dga_bf, dgb_bf, dp_r, dp_k, dp_v, dp_l], axis=1)
    d_in_pt = mm(dp_all, a_bf, "tn", tm=1280, tn=1024, name="mm_dw_in")
    early_state, early_token = pair_start(pack_early({
        "w_in": _unpad_win_rows(d_in_pt), "w_proj_a": d_proj_a, "w_lora_w": dwl[:LORA_W], "a_lora_w": dal[:LORA_A],
        "g_lora_w": dgl[:LORA_G]}), "early")
    da = mm(dp_all, win_pt, "nn", tm=1024, tn=256, name="mm_da")
    g_mix_after = w["g_mix"] + early_token[:1, :1]
    (grad_x,), (dg_mix,) = ew_vjp_call(f_norm_in, [(x, d, 0)], [g_mix_after], [(da, d, 0), (dh1, d, 0)], [(F32,)],
                                       [True], tm=256, name="norm_in_bwd")

    grads = {
        "g_mix": dg_mix, "sgu_ln_w": dln_w, "sgu_ln_b": dln_b, "sgu_w": dsw, "sgu_b": dsbt.T,
        "shift_b": _unpad_rwkv_cols(jnp.concatenate([dsb_r, dsb_k, dsb_v, dsb_l], axis=1)),
        "w0": dw0, "a0": da0, "k_k": dk_k, "k_a": dk_a, "r_k": dr_k, "ln_x_w": dlnx_w, "ln_x_b": dlnx_b,
        "g_ffn": dg_ffn, "g_final": dg_final,
    }
    return loss[0, 0], grad_x, grads, (late_part, late_slots), early_state


MESH = pl.DeviceIdType.MESH
N_CHIPS = 4
N_DEV = 8
PACK_ROWS = 2560
PACK_TILE = 512
SMALL_ROWS = 152
_ANY = pl.BlockSpec(memory_space=pl.ANY)


def _coords():
    return lax.axis_index("x"), lax.axis_index("y"), lax.axis_index("c")


def _other_chips(x, y):
    return [(1 - x, y), (x, 1 - y), (1 - x, 1 - y)]


def _remote(src, dst, send_sems, recv_sems, k, to):
    return pltpu.make_async_remote_copy(src_ref=src, dst_ref=dst, send_sem=send_sems.at[k], recv_sem=recv_sems.at[k],
                                        device_id=to, device_id_type=MESH)


def gather_shards(pack):
    def body(src_ref, out_ref, send_sems, recv_sems):
        x, y, c = _coords()
        me = 2 * x + y
        sib = (x, y, 1 - c)
        chips = _other_chips(x, y)
        first = [_remote(src_ref.at[c], out_ref.at[me, c], send_sems, recv_sems, k, (cx, cy, c))
                 for k, (cx, cy) in enumerate(chips)]
        for cp in first:
            cp.start()
        passed = []
        for k, (cx, cy) in enumerate(chips):
            j = 2 * cx + cy
            _remote(src_ref.at[c], out_ref.at[j, c], send_sems, recv_sems, k, (cx, cy, c)).wait_recv()
            fwd = _remote(out_ref.at[j, c], out_ref.at[j, c], send_sems, recv_sems, 3 + k, sib)
            fwd.start()
            passed.append(fwd)
        for k, (cx, cy) in enumerate(chips):
            j = 2 * cx + cy
            _remote(out_ref.at[j, 1 - c], out_ref.at[j, 1 - c], send_sems, recv_sems, 3 + k, sib).wait_recv()
        for cp in first + passed:
            cp.wait_send()

    return pl.pallas_call(
        body,
        name="gather_shards",
        in_specs=[_ANY],
        out_specs=_ANY,
        out_shape=jax.ShapeDtypeStruct((N_CHIPS,) + pack.shape, pack.dtype),
        scratch_shapes=[pltpu.SemaphoreType.DMA((6,)), pltpu.SemaphoreType.DMA((6,))],
    )(pack)


def reduce_pair(g, tag):
    def body(g_ref, got_ref, send_sems, recv_sems):
        x, y, c = _coords()
        sib = (x, y, 1 - c)
        sends = [_remote(g_ref.at[j, 1 - c], got_ref.at[j], send_sems, recv_sems, j, sib) for j in range(N_CHIPS)]
        for cp in sends:
            cp.start()
        for cp in sends:
            cp.wait_recv()
        for cp in sends:
            cp.wait_send()

    return pl.pallas_call(
        body,
        name="reduce_pair_" + tag,
        in_specs=[_ANY],
        out_specs=_ANY,
        out_shape=jax.ShapeDtypeStruct((N_CHIPS,) + g.shape[2:], g.dtype),
        scratch_shapes=[pltpu.SemaphoreType.DMA((N_CHIPS,)), pltpu.SemaphoreType.DMA((N_CHIPS,))],
    )(g)


def pair_sum(g, got, tag, *, tm):
    n, _, rows, width = g.shape

    def body(g0_ref, g1_ref, got_ref, out_ref, out16_ref):
        own = jnp.where(lax.axis_index("c") == 0, g0_ref[0, 0], g1_ref[0, 0])
        total = own + got_ref[0]
        out_ref[0] = total
        out16_ref[0] = total.astype(BF16)

    blk = pl.BlockSpec((1, tm, width), lambda j, i: (j, i, 0))
    return pl.pallas_call(
        body,
        name="pair_sum_" + tag,
        grid=(n, rows // tm),
        in_specs=[pl.BlockSpec((1, 1, tm, width), lambda j, i: (j, 0, i, 0)),
                  pl.BlockSpec((1, 1, tm, width), lambda j, i: (j, 1, i, 0)), blk],
        out_specs=[blk, blk],
        out_shape=[jax.ShapeDtypeStruct(got.shape, F32), jax.ShapeDtypeStruct(got.shape, BF16)],
        compiler_params=_cparams(2),
    )(g, g, got)


def reduce_chips(p):
    def body(p_ref, out_ref, send_sems, recv_sems):
        x, y, c = _coords()
        me = 2 * x + y
        chips = _other_chips(x, y)
        sends = [_remote(p_ref.at[2 * cx + cy], out_ref.at[me], send_sems, recv_sems, k, (cx, cy, c))
                 for k, (cx, cy) in enumerate(chips)]
        for cp in sends:
            cp.start()
        for k, (cx, cy) in enumerate(chips):
            _remote(p_ref.at[me], out_ref.at[2 * cx + cy], send_sems, recv_sems, k, (cx, cy, c)).wait_recv()
        for cp in sends:
            cp.wait_send()

    return pl.pallas_call(
        body,
        name="reduce_chips",
        in_specs=[_ANY],
        out_specs=_ANY,
        out_shape=jax.ShapeDtypeStruct(p.shape, p.dtype),
        scratch_shapes=[pltpu.SemaphoreType.DMA((3,)), pltpu.SemaphoreType.DMA((3,))],
    )(p)


def _pair_copies(g_ref, got_ref, send_sems, recv_sems):
    x, y, c = _coords()
    return [_remote(g_ref.at[j, 1 - c], got_ref.at[j], send_sems, recv_sems, j, (x, y, 1 - c)) for j in range(N_CHIPS)]


def reduce_pair_start(g, tag):
    hbm = pl.BlockSpec(memory_space=pltpu.HBM)
    sem = pl.BlockSpec(memory_space=pltpu.SEMAPHORE)
    got_shape = (N_CHIPS,) + g.shape[2:]

    def body(g_ref, got_ref, send_sems, recv_sems, g_thru, got_thru, token):
        for cp in _pair_copies(g_ref, got_ref, send_sems, recv_sems):
            cp.start()
        token[...] = jnp.zeros_like(token)

    return pl.pallas_call(
        body,
        name="reduce_pair_start_" + tag,
        out_shape=(pltpu.SemaphoreType.DMA((N_CHIPS,)), pltpu.SemaphoreType.DMA((N_CHIPS,)),
                   pltpu.HBM(g.shape, g.dtype), pltpu.HBM(got_shape, g.dtype), jax.ShapeDtypeStruct((8, 128), F32)),
        in_specs=(hbm, hbm),
        out_specs=(sem, sem, hbm, hbm, pl.BlockSpec(memory_space=pltpu.VMEM)),
        input_output_aliases={0: 2, 1: 3},
        compiler_params=pltpu.CompilerParams(has_side_effects=pltpu.SideEffectType.DATAFLOW_SIDE_EFFECTING),
    )(pltpu.with_memory_space_constraint(g, pltpu.HBM),
      pltpu.with_memory_space_constraint(lax.empty(got_shape, g.dtype), pltpu.HBM))


def reduce_pair_wait(send_sems, recv_sems, g_thru, got_thru, after, tag):
    hbm = pl.BlockSpec(memory_space=pltpu.HBM)
    sem = pl.BlockSpec(memory_space=pltpu.SEMAPHORE)

    def body(g_ref, got_ref, send_sems, recv_sems, after_ref, g_out, got_out):
        for cp in _pair_copies(g_ref, got_ref, send_sems, recv_sems):
            cp.wait_send()
            cp.wait_recv()

    return pl.pallas_call(
        body,
        name="reduce_pair_wait_" + tag,
        out_shape=(pltpu.HBM(g_thru.shape, g_thru.dtype), pltpu.HBM(got_thru.shape, got_thru.dtype)),
        in_specs=(hbm, hbm, sem, sem, pl.BlockSpec(memory_space=pl.ANY)),
        out_specs=(hbm, hbm),
        input_output_aliases={0: 0, 1: 1},
        compiler_params=pltpu.CompilerParams(has_side_effects=pltpu.SideEffectType.DATAFLOW_SIDE_EFFECTING),
    )(g_thru, got_thru, send_sems, recv_sems, after)


def _chip_copies(p_ref, slots_ref, send_sems, recv_sems):
    x, y, c = _coords()
    me = 2 * x + y
    return [(_remote(p_ref.at[2 * cx + cy], slots_ref.at[me], send_sems, recv_sems, k, (cx, cy, c)),
             _remote(p_ref.at[me], slots_ref.at[2 * cx + cy], send_sems, recv_sems, k, (cx, cy, c)))
            for k, (cx, cy) in enumerate(_other_chips(x, y))]


def reduce_chips_start(p):
    hbm = pl.BlockSpec(memory_space=pltpu.HBM)
    sem = pl.BlockSpec(memory_space=pltpu.SEMAPHORE)

    def body(p_ref, slots_ref, send_sems, recv_sems, p_thru, slots_thru, token):
        for send, _ in _chip_copies(p_ref, slots_ref, send_sems, recv_sems):
            send.start()
        token[...] = jnp.zeros_like(token)

    return pl.pallas_call(
        body,
        name="reduce_chips_start",
        out_shape=(pltpu.SemaphoreType.DMA((3,)), pltpu.SemaphoreType.DMA((3,)), pltpu.HBM(p.shape, p.dtype),
                   pltpu.HBM(p.shape, p.dtype), jax.ShapeDtypeStruct((8, 128), F32)),
        in_specs=(hbm, hbm),
        out_specs=(sem, sem, hbm, hbm, pl.BlockSpec(memory_space=pltpu.VMEM)),
        input_output_aliases={0: 2, 1: 3},
        compiler_params=pltpu.CompilerParams(has_side_effects=pltpu.SideEffectType.DATAFLOW_SIDE_EFFECTING),
    )(pltpu.with_memory_space_constraint(p, pltpu.HBM),
      pltpu.with_memory_space_constraint(lax.empty(p.shape, p.dtype), pltpu.HBM))


def reduce_chips_wait(send_sems, recv_sems, p_thru, slots_thru, after):
    hbm = pl.BlockSpec(memory_space=pltpu.HBM)
    sem = pl.BlockSpec(memory_space=pltpu.SEMAPHORE)

    def body(p_ref, slots_ref, send_sems, recv_sems, after_ref, p_dead, slots_out):
        for send, arrival in _chip_copies(p_ref, slots_ref, send_sems, recv_sems):
            send.wait_send()
            arrival.wait_recv()

    return pl.pallas_call(
        body,
        name="reduce_chips_wait",
        out_shape=(pltpu.HBM(p_thru.shape, p_thru.dtype), pltpu.HBM(slots_thru.shape, slots_thru.dtype)),
        in_specs=(hbm, hbm, sem, sem, pl.BlockSpec(memory_space=pl.ANY)),
        out_specs=(hbm, hbm),
        input_output_aliases={0: 0, 1: 1},
        compiler_params=pltpu.CompilerParams(has_side_effects=pltpu.SideEffectType.DATAFLOW_SIDE_EFFECTING),
    )(p_thru, slots_thru, send_sems, recv_sems, after)[1]


def sum_with_own(own, slots, index_fn, after, *, tm, name):
    n, rows, width = slots.shape

    def body(*refs):
        mine = index_fn()
        acc = None
        for s in range(n):
            term = jnp.where(mine == s, refs[s][0], refs[n + s][0].astype(F32))
            acc = term if acc is None else acc + term
        refs[-1][...] = acc

    slot_specs = [pl.BlockSpec((1, tm, width), lambda i, s=s: (s, i, 0)) for s in range(n)]
    return pl.pallas_call(
        body,
        name=name,
        grid=(rows // tm,),
        in_specs=slot_specs + slot_specs + [pl.BlockSpec(after.shape, lambda i: (0,) * after.ndim)],
        out_specs=pl.BlockSpec((tm, width), lambda i: (i, 0)),
        out_shape=jax.ShapeDtypeStruct((rows, width), F32),
        compiler_params=_cparams(1),
    )(*([own] * n), *([slots] * n), after)


def exchange_halves(s, tag):
    rq = PACK_TILE
    nq = s.shape[0] // rq

    def body(s_ref, out_ref, sbuf, rbuf, send_sems, recv_sems, in_sems, out_sems):
        x, y, c = _coords()
        sib = (x, y, 1 - c)
        rows = lambda q: pl.ds(q * rq, rq)
        loads = [pltpu.make_async_copy(s_ref.at[rows(q)], sbuf.at[rows(q)], in_sems.at[q]) for q in range(nq)]
        for cp in loads:
            cp.start()
        sends = []
        for q in range(nq):
            loads[q].wait()
            sends.append(_remote(sbuf.at[rows(q)], rbuf.at[rows(q)], send_sems, recv_sems, q, sib))
            sends[q].start()
        stores = []
        for q in range(nq):
            sends[q].wait_recv()
            stores.append(pltpu.make_async_copy(rbuf.at[rows(q)], out_ref.at[rows(q)], out_sems.at[q]))
            stores[q].start()
        for cp in sends:
            cp.wait_send()
        for cp in stores:
            cp.wait()

    return pl.pallas_call(
        body,
        name="exchange_halves_" + tag,
        in_specs=[_ANY],
        out_specs=_ANY,
        out_shape=jax.ShapeDtypeStruct(s.shape, s.dtype),
        scratch_shapes=[pltpu.VMEM(s.shape, s.dtype), pltpu.VMEM(s.shape, s.dtype)]
        + [pltpu.SemaphoreType.DMA((nq,))] * 4,
        compiler_params=pltpu.CompilerParams(vmem_limit_bytes=VMEM_LIMIT),
    )(s)


def sum_all(s, after):
    def body(s_ref, after_ref, out_ref, slots, mine, theirs, send_sems, recv_sems):
        x, y, c = _coords()
        me = 2 * x + y
        chips = _other_chips(x, y)
        sends = [_remote(s_ref, slots.at[me], send_sems, recv_sems, k, (cx, cy, c)) for k, (cx, cy) in enumerate(chips)]
        for cp in sends:
            cp.start()
        for k, (cx, cy) in enumerate(chips):
            _remote(s_ref, slots.at[2 * cx + cy], send_sems, recv_sems, k, (cx, cy, c)).wait_recv()
        slots[me] = s_ref[...]
        acc = ((slots[0] + slots[1]) + slots[2]) + slots[3]
        mine[...] = acc
        swap = _remote(mine, theirs, send_sems, recv_sems, 3, (x, y, 1 - c))
        swap.start()
        swap.wait_recv()
        out_ref[...] = acc + theirs[...]
        swap.wait_send()
        for cp in sends:
            cp.wait_send()

    vmem = pl.BlockSpec(memory_space=pltpu.VMEM)
    return pl.pallas_call(
        body,
        name="sum_all",
        in_specs=[vmem, vmem],
        out_specs=vmem,
        out_shape=jax.ShapeDtypeStruct(s.shape, s.dtype),
        scratch_shapes=[pltpu.VMEM((N_CHIPS,) + s.shape, s.dtype), pltpu.VMEM(s.shape, s.dtype),
                        pltpu.VMEM(s.shape, s.dtype), pltpu.SemaphoreType.DMA((4,)), pltpu.SemaphoreType.DMA((4,))],
        compiler_params=pltpu.CompilerParams(vmem_limit_bytes=VMEM_LIMIT),
    )(s, after)


ADAM_LR = 0.001
ADAM_B1 = 0.9
ADAM_B2 = 0.999
ADAM_EPS = 1e-08
ADAM_WD = 0.01
ADAM_STEP = 10


def f_adamw(g, w, m, v):
    m = ADAM_B1 * m + (1.0 - ADAM_B1) * g
    v = ADAM_B2 * v + (1.0 - ADAM_B2) * jnp.square(g)
    m_hat = m / (1.0 - ADAM_B1 ** ADAM_STEP)
    v_hat = v / (1.0 - ADAM_B2 ** ADAM_STEP)
    delta = -ADAM_LR * (m_hat / (jnp.sqrt(v_hat) + ADAM_EPS) + ADAM_WD * w)
    return delta, m, v


def adamw_call(g, w, m, v, *, tm, name):
    width = g.shape[1]
    return ew_call(f_adamw, [(g, width, 0), (w, width, 0), (m, width, 0), (v, width, 0)], [], [(width, F32)] * 3,
                   tm=tm, name=name)


def adamw_halves(g_own, g_other, w, m, v, *, tm):
    _, rows, width = w.shape

    def body(go_ref, gx_ref, w_ref, m_ref, v_ref, g_ref, d_ref, nm_ref, nv_ref):
        g = jnp.where(pl.program_id(0) == lax.axis_index("c"), go_ref[...], gx_ref[...])
        delta, nm, nv = f_adamw(g, w_ref[0], m_ref[0], v_ref[0])
        g_ref[0] = g
        d_ref[0] = delta
        nm_ref[0] = nm
        nv_ref[0] = nv

    half = pl.BlockSpec((tm, width), lambda h, i: (i, 0))
    full = pl.BlockSpec((1, tm, width), lambda h, i: (h, i, 0))
    return pl.pallas_call(
        body,
        name="adamw_sharded",
        grid=(2, rows // tm),
        in_specs=[half, half, full, full, full],
        out_specs=[full] * 4,
        out_shape=[jax.ShapeDtypeStruct(w.shape, F32)] * 4,
        compiler_params=_cparams(2),
    )(g_own, g_other, w, m, v)


EARLY = ["w_in", "w_proj_a", "w_lora_w", "a_lora_w", "g_lora_w"]
LATE = ["w_ffn1", "w_ffn2", "w_proj_b", "w_out"]
SHARDED = EARLY + LATE
LORAS = ["w_lora_w", "a_lora_w", "g_lora_w"]
HALF_W = 512
PIECE_ROWS = {"w_in": 1864, "w_ffn1": 1024, "w_ffn2": 1024, "w_proj_a": 256, "w_proj_b": 256, "w_out": 256,
              "w_lora_w": 32, "a_lora_w": 32, "g_lora_w": 80}
PIECE_OFF = {"w_in": 0, "w_proj_a": 1920, "w_lora_w": 2176, "a_lora_w": 2208, "g_lora_w": 2240,
             "w_ffn1": 0, "w_ffn2": 1024, "w_proj_b": 2048, "w_out": 2304}
LO_OFF = 2320
SHARD_AXIS = {"w_in": 1, "w_proj_a": 0, "w_lora_w": 1, "a_lora_w": 1, "g_lora_w": 1, "w_proj_b": 0, "w_out": 0,
              "w_ffn1": 1, "w_ffn2": 0}
SHARD_SHAPE = {"w_in": (1024, 1864), "w_proj_a": (256, 1024), "w_lora_w": (64, 256), "a_lora_w": (64, 256),
               "g_lora_w": (160, 256), "w_proj_b": (256, 1024), "w_out": (256, 1024), "w_ffn1": (1024, 1024),
               "w_ffn2": (1024, 1024)}
SHIFT_SHARD = (2, 840)
VECTORS = ["g_mix", "sgu_ln_w", "sgu_ln_b", "w0", "a0", "k_k", "k_a", "r_k", "ln_x_w", "ln_x_b", "g_ffn", "g_final"]
SMALL = VECTORS + ["sgu_w", "sgu_b"]
SMALL_SHAPE = {**{n: (1, 1024) for n in VECTORS}, "sgu_w": (8, 128, 128), "sgu_b": (8, 128)}
WEIGHTS = ["g_mix", "w_in", "sgu_ln_w", "sgu_ln_b", "sgu_w", "sgu_b", "w_proj_a", "shift_b", "w_lora_w", "w0",
           "a_lora_w", "a0", "g_lora_w", "k_k", "k_a", "r_k", "ln_x_w", "ln_x_b", "w_proj_b", "w_out", "g_ffn",
           "w_ffn1", "w_ffn2", "g_final"]


def _size(shape):
    n = 1
    for s in shape:
        n *= s
    return n


def _pack_rows(parts, rows, dtype):
    flat = jnp.concatenate([p.reshape(-1).astype(dtype) for p in parts])
    return jnp.concatenate([flat, jnp.zeros((rows * 1024 - flat.shape[0],), dtype)]).reshape(rows, 1024)


def _unpack_rows(packed, shapes):
    flat = packed.reshape(-1)
    out, off = [], 0
    for shp in shapes:
        out.append(flat[off:off + _size(shp)].reshape(shp))
        off += _size(shp)
    return out


def _shard_of(name, full, j):
    ax = SHARD_AXIS[name]
    n = SHARD_SHAPE[name][ax]
    return lax.slice_in_dim(full, j * n, (j + 1) * n, axis=ax)


def _pad_cols(z, n):
    return jnp.concatenate([z, jnp.zeros((z.shape[0], n - z.shape[1]), z.dtype)], axis=1)


def _row_form(name, s):
    return s.T if name == "w_in" else s


def _half_piece(name, rf, h):
    if name in LORAS:
        r = PIECE_ROWS[name]
        return _pad_cols(rf[h * r:(h + 1) * r], HALF_W)
    return rf[:, HALF_W * h:HALF_W * (h + 1)]


def _pack_half(group, rf_fn, h, dtype, tail=()):
    parts, pos, rows = [], 0, PACK_ROWS
    for n in group:
        if PIECE_OFF[n] > pos:
            parts.append(jnp.zeros((PIECE_OFF[n] - pos, HALF_W), dtype))
        parts.append(_half_piece(n, rf_fn(n), h).astype(dtype))
        pos = PIECE_OFF[n] + PIECE_ROWS[n]
    for t in tail:
        parts.append(t)
        pos += t.shape[0]
    parts.append(jnp.zeros((rows - pos, HALF_W), dtype))
    return jnp.concatenate(parts, axis=0)


def _piece(pack, name):
    return pack[PIECE_OFF[name]:PIECE_OFF[name] + PIECE_ROWS[name]]


def _join_halves(name, p0, p1):
    if name in LORAS:
        return jnp.concatenate([p0[:, :SHARD_SHAPE[name][1]], p1[:, :SHARD_SHAPE[name][1]]], axis=0)
    return jnp.concatenate([p0, p1], axis=1)


def _grad_row_form(name, full, j):
    if name == "w_in":
        return full[SHARD_SHAPE[name][1] * j:SHARD_SHAPE[name][1] * (j + 1)]
    return _shard_of(name, full, j)


def adamw_weight(name, g_own, g_other, w, m, v):
    rows, width = w.shape
    if name in LORAS:
        tm = PIECE_ROWS[name]
        grid = (2, 1)
        native = pl.BlockSpec((tm, width), lambda h, i: (h, 0))
    elif name == "w_in":
        tm, lanes = rows, 128
        grid = (2, HALF_W // lanes)
        native = pl.BlockSpec((tm, lanes), lambda h, i: (0, h * (HALF_W // lanes) + i))
    else:
        tm = 128
        grid = (2, rows // tm)
        native = pl.BlockSpec((tm, HALF_W), lambda h, i: (i, h))
    off = PIECE_OFF[name] // tm
    if name == "w_in":
        packed = pl.BlockSpec((tm, 128), lambda h, i: (0, i))
    else:
        packed = pl.BlockSpec((tm, HALF_W), lambda h, i: (off + i, 0))

    def body(go_ref, gx_ref, w_ref, m_ref, v_ref, g_ref, d_ref, nm_ref, nv_ref):
        g = jnp.where(pl.program_id(0) == lax.axis_index("c"), go_ref[...], gx_ref[...])[:, :w_ref.shape[1]]
        delta, nm, nv = f_adamw(g, w_ref[...], m_ref[...], v_ref[...])
        g_ref[...] = g
        d_ref[...] = delta
        nm_ref[...] = nm
        nv_ref[...] = nv

    return pl.pallas_call(
        body,
        name="adamw_" + name,
        grid=grid,
        in_specs=[packed, packed, native, native, native],
        out_specs=[native] * 4,
        out_shape=[jax.ShapeDtypeStruct(w.shape, F32)] * 4,
        compiler_params=_cparams(2),
    )(g_own, g_other, w, m, v)


def kernel(x, g_mix, w_in, sgu_ln_w, sgu_ln_b, sgu_w, sgu_b, w_proj_a, shift_b, w_lora_w, w0, a_lora_w, a0, g_lora_w, k_k, k_a, r_k, ln_x_w, ln_x_b, w_proj_b, w_out, g_ffn, w_ffn1, w_ffn2, g_final, loss_target, m_g_mix, m_w_in, m_sgu_ln_w, m_sgu_ln_b, m_sgu_w, m_sgu_b, m_w_proj_a, m_shift_b, m_w_lora_w, m_w0, m_a_lora_w, m_a0, m_g_lora_w, m_k_k, m_k_a, m_r_k, m_ln_x_w, m_ln_x_b, m_w_proj_b, m_w_out, m_g_ffn, m_w_ffn1, m_w_ffn2, m_g_final, v_g_mix, v_w_in, v_sgu_ln_w, v_sgu_ln_b, v_sgu_w, v_sgu_b, v_w_proj_a, v_shift_b, v_w_lora_w, v_w0, v_a_lora_w, v_a0, v_g_lora_w, v_k_k, v_k_a, v_r_k, v_ln_x_w, v_ln_x_b, v_w_proj_b, v_w_out, v_g_ffn, v_w_ffn1, v_w_ffn2, v_g_final):
    given = dict(zip(WEIGHTS, (g_mix, w_in, sgu_ln_w, sgu_ln_b, sgu_w, sgu_b, w_proj_a, shift_b, w_lora_w, w0, a_lora_w, a0, g_lora_w, k_k, k_a, r_k, ln_x_w, ln_x_b, w_proj_b, w_out, g_ffn, w_ffn1, w_ffn2, g_final)))
    mom_m = dict(zip(WEIGHTS, (m_g_mix, m_w_in, m_sgu_ln_w, m_sgu_ln_b, m_sgu_w, m_sgu_b, m_w_proj_a, m_shift_b, m_w_lora_w, m_w0, m_a_lora_w, m_a0, m_g_lora_w, m_k_k, m_k_a, m_r_k, m_ln_x_w, m_ln_x_b, m_w_proj_b, m_w_out, m_g_ffn, m_w_ffn1, m_w_ffn2, m_g_final)))
    mom_v = dict(zip(WEIGHTS, (v_g_mix, v_w_in, v_sgu_ln_w, v_sgu_ln_b, v_sgu_w, v_sgu_b, v_w_proj_a, v_shift_b, v_w_lora_w, v_w0, v_a_lora_w, v_a0, v_g_lora_w, v_k_k, v_k_a, v_r_k, v_ln_x_w, v_ln_x_b, v_w_proj_b, v_w_out, v_g_ffn, v_w_ffn1, v_w_ffn2, v_g_final)))
    chip = 2 * lax.axis_index("x") + lax.axis_index("y")

    def local_block(tree, n):
        return tree[n] if n == "g_final" else tree[n][0]

    sb = local_block(given, "shift_b")
    lo_part = lambda z: (z - z.astype(BF16).astype(F32)).astype(BF16)
    row_form = lambda tree: (lambda n: _row_form(n, local_block(tree, n)))
    tile16 = lambda z: jnp.pad(z, ((0, 16 - z.shape[0]), (0, HALF_W - z.shape[1])))
    sb_tiles = [tile16(f(sb[:, lanes])) for f in (lambda z: z.astype(BF16), lo_part)
                for lanes in (slice(0, HALF_W), slice(HALF_W, None))]
    tails = [[_half_piece(n, lo_part(local_block(given, n)), h) for n in LORAS] + sb_tiles for h in range(2)]
    pack_w = jnp.stack([_pack_half(EARLY, row_form(given), h, BF16, tails[h]) for h in range(2)])
    gathered = gather_shards(pack_w)
    gathered = lax.dynamic_update_index_in_dim(gathered, pack_w, chip, 0)
    pack_late = jnp.stack([_pack_half(LATE, row_form(given), h, BF16) for h in range(2)])

    def whole(group, got, own):
        half = lambda n, j, h: jnp.where(chip == j, _piece(own[h], n), _piece(got[j, h], n))
        shard = lambda n, j: _join_halves(n, half(n, j, 0), half(n, j, 1))
        return {n: jnp.concatenate([shard(n, j).astype(F32 if n == "w_in" else BF16) for j in range(N_CHIPS)],
                                   axis=0 if n == "w_in" else SHARD_AXIS[n]) for n in group}

    w = whole(EARLY, gathered, pack_w)
    late_weights = lambda got: whole(LATE, got, pack_late)
    off = LO_OFF
    for n in LORAS:
        r, cols = PIECE_ROWS[n], SHARD_SHAPE[n][1]
        lo = jnp.concatenate([jnp.concatenate([gathered[j, 0, off:off + r, :cols], gathered[j, 1, off:off + r, :cols]],
                                              axis=0) for j in range(N_CHIPS)], axis=1)
        w[n] = w[n].astype(F32) + lo.astype(F32)
        off += r
    sb_tile = lambda j, t, lanes: gathered[j, 0, off + 16 * t:off + 16 * t + 2, :lanes].astype(F32)
    rest = SHIFT_SHARD[1] - HALF_W
    w["shift_b"] = jnp.concatenate(
        [jnp.concatenate([sb_tile(j, 0, HALF_W) + sb_tile(j, 2, HALF_W), sb_tile(j, 1, rest) + sb_tile(j, 3, rest)],
                         axis=1) for j in range(N_CHIPS)], axis=1)
    for n in SMALL:
        w[n] = local_block(given, n).reshape(SMALL_SHAPE[n])

    def pair_start(g_pack, tag):
        *state, token = reduce_pair_start(g_pack, tag)
        return state, token

    def pair_finish(state, after, tag):
        return pair_sum(*reduce_pair_wait(*state, after, tag), tag, tm=PACK_TILE)

    pack_early = lambda g: jnp.stack([jnp.stack([_pack_half(EARLY, lambda n: _grad_row_form(n, g[n], j), h, F32)
                                                 for h in range(2)]) for j in range(N_CHIPS)])
    loss, grad_x, grads, (late_part, late_slots), early_state = local_step(
        x[0], loss_target[0], w, pack_late, late_weights, pair_start, pair_finish, pack_early)
    loss = lax.psum(loss, ("x", "y", "c"))

    early_part, early_part16 = pair_finish(early_state, grad_x, "early")
    s_pack = _pack_rows([grads[n] for n in SMALL] + [grads["shift_b"]], SMALL_ROWS, F32)
    sends, recvs, part_thru, slots_thru, token = reduce_chips_start(early_part16)
    my_chip = lambda: 2 * lax.axis_index("x") + lax.axis_index("y")
    out_g, out_d, out_m, out_v = {}, {}, {}, {}

    def finish(group, tag, part, slots):
        half_sum = sum_with_own(part, slots, my_chip, token, tm=PACK_TILE, name="chip_sum_" + tag)
        other_half = exchange_halves(half_sum, tag)
        for n in group:
            res = adamw_weight(n, half_sum, other_half,
                               *[_row_form(n, local_block(t, n)) for t in (given, mom_m, mom_v)])
            for tree, z in zip((out_g, out_d, out_m, out_v), res):
                tree[n] = _row_form(n, z)

    finish(LATE, "late", late_part, late_slots)

    small_shapes = [SMALL_SHAPE[n] for n in SMALL]
    g_small = sum_all(s_pack, token)
    w_small = _pack_rows([local_block(given, n) for n in SMALL], SMALL_ROWS, F32)
    m_small = _pack_rows([local_block(mom_m, n) for n in SMALL], SMALL_ROWS, F32)
    v_small = _pack_rows([local_block(mom_v, n) for n in SMALL], SMALL_ROWS, F32)
    d_small, nm_small, nv_small = adamw_call(g_small, w_small, m_small, v_small, tm=SMALL_ROWS, name="adamw_small")
    g_parts = _unpack_rows(g_small, small_shapes + [(2, N_RWKV)])
    out_g.update(zip(SMALL, g_parts[:-1]))
    out_d.update(zip(SMALL, _unpack_rows(d_small, small_shapes)))
    out_m.update(zip(SMALL, _unpack_rows(nm_small, small_shapes)))
    out_v.update(zip(SMALL, _unpack_rows(nv_small, small_shapes)))
    g_sb = lax.dynamic_slice_in_dim(g_parts[-1], chip * SHIFT_SHARD[1], SHIFT_SHARD[1], axis=1)
    sb_args = [_pack_rows([z], 8, F32) for z in (g_sb, sb, local_block(mom_m, "shift_b"), local_block(mom_v, "shift_b"))]
    sb_res = adamw_call(*sb_args, tm=8, name="adamw_shift_b")
    out_g["shift_b"] = g_sb
    for tree, res in zip((out_d, out_m, out_v), sb_res):
        tree["shift_b"] = _unpack_rows(res, [SHIFT_SHARD])[0]

    after = (out_v["w_out"], nv_small, sb_res[2])
    early_slots = reduce_chips_wait(sends, recvs, part_thru, slots_thru, jnp.concatenate([z.reshape(-1)[:8] for z in after]))
    finish(EARLY, "early", early_part, early_slots)

    def block_of(tree, n):
        return tree[n].reshape(given[n].shape)

    return (loss, grad_x[None], *[block_of(out_g, n) for n in WEIGHTS], *[block_of(out_d, n) for n in WEIGHTS],
            *[block_of(out_m, n) for n in WEIGHTS], *[block_of(out_v, n) for n in WEIGHTS])
```

```python
import functools

import jax
import jax.numpy as jnp
from jax import lax
from jax.experimental import pallas as pl
from jax.experimental.pallas import tpu as pltpu

F32 = jnp.float32
BF16 = jnp.bfloat16

D_MODEL = 1024
N_HEADS = 16
HEAD = 64
SCAN_CHUNK = 64

VMEM_LIMIT = 56 * 1024 * 1024


_BDIMS = {
    "nn": (((2,), (1,)), ((0,), (0,))),
    "nt": (((2,), (2,)), ((0,), (0,))),
    "tn": (((1,), (1,)), ((0,), (0,))),
}


def _raw_bdot(x, y, mode, fine):
    if fine:
        return lax.dot_general(x, y, _BDIMS[mode], precision=lax.Precision.HIGH, preferred_element_type=F32)
    return lax.dot_general(x.astype(BF16), y.astype(BF16), _BDIMS[mode], preferred_element_type=F32)


@functools.partial(jax.custom_vjp, nondiff_argnums=(2, 3))
def bdot(x, y, mode, fine=True):
    return _raw_bdot(x, y, mode, fine)


def _bdot_fwd(x, y, mode, fine):
    return _raw_bdot(x, y, mode, fine), (x, y)


def _bdot_bwd(mode, fine, res, g):
    x, y = res
    if mode == "nn":
        return bdot(g, y, "nt", fine), bdot(x, g, "tn", fine)
    if mode == "nt":
        return bdot(g, y, "nn", fine), bdot(g, x, "tn", fine)
    return bdot(y, g, "nt", fine), bdot(x, g, "nn", fine)


bdot.defvjp(_bdot_fwd, _bdot_bwd)


def _scan_chunk(S0, r, lw, k, v, a, b):
    nh, lc, _ = r.shape
    ti = lax.broadcasted_iota(jnp.int32, (lc, lc), 0)
    si = lax.broadcasted_iota(jnp.int32, (lc, lc), 1)
    incl = (si <= ti).astype(F32)
    strict = (si < ti).astype(F32)
    eye = (si == ti).astype(F32)
    cl = bdot(jnp.broadcast_to(incl, (nh, lc, lc)), lw, "nn")
    cl_last = cl[:, lc - 1:lc, :]
    g_last = jnp.exp(cl_last - cl)
    at = a * jnp.exp(cl - lw)
    bt = b * jnp.exp(-cl)
    kt = k * jnp.exp(-cl)
    rt = r * jnp.exp(cl)
    ar = jnp.concatenate([at, rt], axis=1)
    ar_b = bdot(ar, bt, "nt", False)
    ar_k = bdot(ar, kt, "nt", False)
    m_ab, m_rb = ar_b[:, :lc] * strict, ar_b[:, lc:] * incl
    m_ak, m_rk = ar_k[:, :lc] * strict, ar_k[:, lc:] * incl
    x = eye + m_ab
    p = bdot(m_ab, m_ab, "nn", False)
    n = 2
    while n * 2 < lc:
        px = bdot(jnp.concatenate([p, x], axis=1), p, "nn", False)
        p = px[:, :lc]
        x = x + px[:, lc:]
        n *= 2
    x = x + bdot(x, p, "nn", False)
    ar_s = bdot(ar, S0, "nt", False)
    akrk_v = bdot(jnp.concatenate([m_ak, m_rk], axis=1), v, "nn", False)
    u = bdot(x, ar_s[:, :lc] + akrk_v[:, :lc], "nn", False)
    o = ar_s[:, lc:] + bdot(m_rb, u, "nn", False) + akrk_v[:, lc:]
    s_last = S0 * jnp.exp(cl_last) + bdot(jnp.concatenate([u, v], axis=1),
                                          jnp.concatenate([b * g_last, k * g_last], axis=1), "tn", False)
    return o, s_last


def _split_heads(z):
    return jnp.stack([z[:, HEAD * h:HEAD * (h + 1)] for h in range(N_HEADS)], axis=0)


def _merge_heads(z):
    return jnp.concatenate([z[h] for h in range(N_HEADS)], axis=1)


def _scan_specs(t, ops, rev):
    nc = t // SCAN_CHUNK
    row = (lambda c: nc - 1 - c) if rev else (lambda c: c)
    specs = [pl.BlockSpec((SCAN_CHUNK, D_MODEL), lambda c, cb=cb: (row(c), cb)) for _, cb in ops]
    state = pl.BlockSpec((1, N_HEADS, HEAD, HEAD), lambda c: (row(c), 0, 0, 0))
    return nc, specs, state


def scan_fwd(ops, pack):
    t = ops[0][0].shape[0]
    nc, specs, state = _scan_specs(t, ops, False)

    def body(r_ref, lw_ref, k_ref, v_ref, a_ref, b_ref, pack_ref, o_ref, s0_ref, all_ref, s_scr, send_sems, recv_sems):
        step = pl.program_id(0)
        x, y, c = _coords()
        me = 2 * x + y
        sib = (x, y, 1 - c)
        chips = _other_chips(x, y)
        first = [_remote(pack_ref.at[c], all_ref.at[me, c], send_sems, recv_sems, k, (cx, cy, c))
                 for k, (cx, cy) in enumerate(chips)]
        passed = [_remote(all_ref.at[2 * cx + cy, c], all_ref.at[2 * cx + cy, c], send_sems, recv_sems, 3 + k, sib)
                  for k, (cx, cy) in enumerate(chips)]

        @pl.when(step == 0)
        def _():
            s_scr[...] = jnp.zeros_like(s_scr)
            for cp in first:
                cp.start()

        s0 = s_scr[...]
        s0_ref[0] = s0
        o, s_last = _scan_chunk(s0, *[_split_heads(z[...]) for z in (r_ref, lw_ref, k_ref, v_ref, a_ref, b_ref)])
        o_ref[...] = _merge_heads(o)
        s_scr[...] = s_last

        @pl.when(step == nc - 1)
        def _():
            for k, (cx, cy) in enumerate(chips):
                j = 2 * cx + cy
                _remote(pack_ref.at[c], all_ref.at[j, c], send_sems, recv_sems, k, (cx, cy, c)).wait_recv()
                passed[k].start()
            for k, (cx, cy) in enumerate(chips):
                j = 2 * cx + cy
                _remote(all_ref.at[j, 1 - c], all_ref.at[j, 1 - c], send_sems, recv_sems, 3 + k, sib).wait_recv()
            for cp in first + passed:
                cp.wait_send()

    return pl.pallas_call(
        body,
        name="scan_fwd",
        grid=(nc,),
        in_specs=specs + [_ANY],
        out_specs=[pl.BlockSpec((SCAN_CHUNK, D_MODEL), lambda c: (c, 0)), state, _ANY],
        out_shape=[jax.ShapeDtypeStruct((t, D_MODEL), F32), jax.ShapeDtypeStruct((nc, N_HEADS, HEAD, HEAD), F32),
                   jax.ShapeDtypeStruct((N_CHIPS,) + pack.shape, pack.dtype)],
        scratch_shapes=[pltpu.VMEM((N_HEADS, HEAD, HEAD), F32), pltpu.SemaphoreType.DMA((6,)),
                        pltpu.SemaphoreType.DMA((6,))],
        compiler_params=_cparams(1),
    )(*[a for a, _ in ops], pack)


def scan_bwd(ops, s0s, do, part):
    t = ops[0][0].shape[0]
    nc, specs, state = _scan_specs(t, ops + [(do, 0)], True)

    def body(r_ref, lw_ref, k_ref, v_ref, a_ref, b_ref, do_ref, s0_ref, part_ref, *rest):
        out_refs, slots_ref, ds_scr, send_sems, recv_sems = rest[:6], rest[6], rest[7], rest[8], rest[9]
        step = pl.program_id(0)
        x, y, c = _coords()
        me = 2 * x + y
        chips = _other_chips(x, y)
        sends = [_remote(part_ref.at[2 * cx + cy], slots_ref.at[me], send_sems, recv_sems, k, (cx, cy, c))
                 for k, (cx, cy) in enumerate(chips)]

        @pl.when(step == 0)
        def _():
            ds_scr[...] = jnp.zeros_like(ds_scr)
            for cp in sends:
                cp.start()

        _, vjp = jax.vjp(_scan_chunk, s0_ref[0],
                         *[_split_heads(z[...]) for z in (r_ref, lw_ref, k_ref, v_ref, a_ref, b_ref)])
        grads = vjp((_split_heads(do_ref[...]), ds_scr[...]))
        for o_ref, g in zip(out_refs, grads[1:]):
            o_ref[...] = _merge_heads(g)
        ds_scr[...] = grads[0]

        @pl.when(step == nc - 1)
        def _():
            for k, (cx, cy) in enumerate(chips):
                _remote(part_ref.at[me], slots_ref.at[2 * cx + cy], send_sems, recv_sems, k, (cx, cy, c)).wait_recv()
            for cp in sends:
                cp.wait_send()

    return pl.pallas_call(
        body,
        name="scan_bwd",
        grid=(nc,),
        in_specs=specs + [state, _ANY],
        out_specs=[pl.BlockSpec((SCAN_CHUNK, D_MODEL), lambda c: (nc - 1 - c, 0))] * 6 + [_ANY],
        out_shape=[jax.ShapeDtypeStruct((t, D_MODEL), F32)] * 6 + [jax.ShapeDtypeStruct(part.shape, part.dtype)],
        scratch_shapes=[pltpu.VMEM((N_HEADS, HEAD, HEAD), F32), pltpu.SemaphoreType.DMA((3,)),
                        pltpu.SemaphoreType.DMA((3,))],
        compiler_params=_cparams(1),
    )(*[a for a, _ in ops], do, s0s, part)


_MDIMS = {
    "nn": (((1,), (0,)), ((), ())),
    "nt": (((1,), (1,)), ((), ())),
    "tn": (((0,), (0,)), ((), ())),
}


def _raw_mdot(x, y, mode, exact):
    if exact:
        return lax.dot_general(x, y, _MDIMS[mode], precision=lax.Precision.HIGH, preferred_element_type=F32)
    return lax.dot_general(x.astype(BF16), y.astype(BF16), _MDIMS[mode], preferred_element_type=F32)


@functools.partial(jax.custom_vjp, nondiff_argnums=(2, 3))
def mdot(x, y, mode, exact):
    return _raw_mdot(x, y, mode, exact)


def _mdot_fwd(x, y, mode, exact):
    return _raw_mdot(x, y, mode, exact), (x, y)


def _mdot_bwd(mode, exact, res, g):
    x, y = res
    if mode == "nn":
        return mdot(g, y, "nt", exact), mdot(x, g, "tn", exact)
    if mode == "nt":
        return mdot(g, y, "nn", exact), mdot(g, x, "tn", exact)
    return mdot(y, g, "nt", exact), mdot(x, g, "nn", exact)


mdot.defvjp(_mdot_fwd, _mdot_bwd)


def _seg_ones():
    i = lax.broadcasted_iota(jnp.int32, (256, 256), 0) // HEAD
    j = lax.broadcasted_iota(jnp.int32, (256, 256), 1) // HEAD
    return (i == j).astype(BF16)


@jax.custom_vjp
def segsum(x):
    bd = _seg_ones()
    hi = x.astype(BF16)
    lo = (x - hi.astype(F32)).astype(BF16)
    cols = []
    for j in range(x.shape[1] // 256):
        sl = slice(256 * j, 256 * (j + 1))
        cols.append(jnp.dot(hi[:, sl], bd, preferred_element_type=F32)
                    + jnp.dot(lo[:, sl], bd, preferred_element_type=F32))
    return jnp.concatenate(cols, axis=1)


segsum.defvjp(lambda x: (segsum(x), None), lambda _, g: (segsum(g),))


NORM_EPS = 1e-6
LN_EPS = 1e-5
GN_EPS = 64e-5
SGU_CHUNK = 128
SGU_GROUPS = 8


def _rms(x, g):
    return x * lax.rsqrt(jnp.mean(x * x, axis=-1, keepdims=True) + NORM_EPS) * g


def f_norm_in(x, g):
    return _rms(x, g), x


def f_sgu(p, ln_w, ln_b, sw, sbt):
    tm = p.shape[0]
    z = 0.5 * p * (1.0 + lax.erf(p * 0.7071067811865476))
    u, v = z[:, :D_MODEL], z[:, D_MODEL:]
    mu = jnp.mean(v, axis=-1, keepdims=True)
    d = v - mu
    vn = d * lax.rsqrt(jnp.mean(d * d, axis=-1, keepdims=True) + LN_EPS) * ln_w + ln_b
    ii = lax.broadcasted_iota(jnp.int32, (SGU_CHUNK, SGU_CHUNK), 0)
    jj = lax.broadcasted_iota(jnp.int32, (SGU_CHUNK, SGU_CHUNK), 1)
    mask = (jj <= ii).astype(F32)
    gi = lax.broadcasted_iota(jnp.int32, (SGU_GROUPS, D_MODEL), 0)
    ci = lax.broadcasted_iota(jnp.int32, (SGU_GROUPS, D_MODEL), 1) // SGU_CHUNK
    bias = mdot(sbt, (gi == ci).astype(F32), "nn", True)
    rows = []
    for c in range(tm // SGU_CHUNK):
        cols = []
        for g in range(SGU_GROUPS):
            blk = vn[c * SGU_CHUNK:(c + 1) * SGU_CHUNK, g * SGU_CHUNK:(g + 1) * SGU_CHUNK]
            cols.append(mdot(sw[g] * mask, blk, "nn", False))
        rows.append(jnp.concatenate(cols, axis=1) + bias)
    return (u * jnp.concatenate(rows, axis=0),)


def _softplus(x):
    return jnp.maximum(x, 0.0) + jnp.log1p(jnp.exp(-jnp.abs(x)))


def f_pre(q, wl, w0, al, a0, gl, k_k, k_a):
    qr, qk, qv, ql = q[:, :1024], q[:, 1024:2048], q[:, 2048:3072], q[:, 3072:]
    return _f_pre(qr, qk, qv, ql, wl, w0, al, a0, gl, k_k, k_a)


def _f_pre(qr, qk, qv, ql, wl, w0, al, a0, gl, k_k, k_a):
    xw, xa, xg = ql[:, :128], ql[:, 128:256], ql[:, 256:512]
    wr = -_softplus(-(w0 + mdot(jnp.tanh(xw), wl, "nn", True))) - 0.5
    lw = -jnp.exp(wr)
    aa = jax.nn.sigmoid(a0 + mdot(xa, al, "nn", True))
    g = mdot(jax.nn.sigmoid(xg), gl, "nn", True)
    kkr = qk * k_k
    kk = kkr / jnp.maximum(jnp.sqrt(segsum(kkr * kkr)), 1e-12)
    kp = qk * (1.0 + (aa - 1.0) * k_a)
    return qr, lw, kp, qv, -kk, kk * aa, g, qr, kp, qv


def f_post(o, r, kp, v, g, lnw, lnb, rk):
    mu = segsum(o) * (1.0 / HEAD)
    d = o - mu
    gn = d * lax.rsqrt(segsum(d * d) * (1.0 / HEAD) + GN_EPS)
    return ((gn * lnw + lnb + segsum(r * kp * rk) * v) * g,)


def f_mix(ya, yb, ga, gb):
    return (jax.nn.sigmoid(ga) * ya + jax.nn.sigmoid(gb) * yb,)


def f_ffn_in(h1, g):
    return _rms(h1, g), h1


def f_final(h1, m3, tgt, g):
    y = _rms(h1 + m3, g)
    err = jnp.square(y - tgt)
    return 0.5 * jnp.sum(jnp.mean(err, axis=-1))


def _cparams(n_grid):
    return pltpu.CompilerParams(dimension_semantics=("arbitrary",) * n_grid, vmem_limit_bytes=VMEM_LIMIT)


def _tile_spec(tm, w, cb):
    return pl.BlockSpec((tm, w), lambda i: (i, cb))


def _const_spec(c):
    nd = c.ndim
    return pl.BlockSpec(c.shape, lambda i: (0,) * nd)


def ew_call(fn, tiled, consts, outs, *, tm, name):
    t = tiled[0][0].shape[0]
    n_t, n_c = len(tiled), len(consts)

    def body(*refs):
        tv = [r[...].astype(F32) for r in refs[:n_t]]
        cv = [r[...] for r in refs[n_t:n_t + n_c]]
        res = fn(*tv, *cv)
        for o_ref, val in zip(refs[n_t + n_c:], res):
            o_ref[...] = val.astype(o_ref.dtype)

    return pl.pallas_call(
        body,
        name=name,
        grid=(t // tm,),
        in_specs=[_tile_spec(tm, w, cb) for _, w, cb in tiled] + [_const_spec(c) for c in consts],
        out_specs=[_tile_spec(tm, w, 0) for w, _ in outs],
        out_shape=[jax.ShapeDtypeStruct((t, w), dt) for w, dt in outs],
        compiler_params=_cparams(1),
    )(*[a for a, _, _ in tiled], *consts)


def ew_vjp_call(fn, tiled, consts, cots, d_tiled, d_consts, *, tm, name):
    t = tiled[0][0].shape[0]
    n_t, n_c, n_g = len(tiled), len(consts), len(cots)
    dt_list = [(i, dt) for i, dts in enumerate(d_tiled) for dt in dts]
    dc_list = [i for i, want in enumerate(d_consts) if want]

    def body(*refs):
        tv = [r[...].astype(F32) for r in refs[:n_t]]
        cv = [r[...] for r in refs[n_t:n_t + n_c]]
        gv = tuple(r[...].astype(F32) for r in refs[n_t + n_c:n_t + n_c + n_g])
        out_refs = refs[n_t + n_c + n_g:]
        _, vjp = jax.vjp(fn, *tv, *cv)
        grads = vjp(gv)
        for o_ref, (i, _) in zip(out_refs, dt_list):
            o_ref[...] = grads[i].astype(o_ref.dtype)
        acc_refs = out_refs[len(dt_list):]

        @pl.when(pl.program_id(0) == 0)
        def _():
            for a_ref in acc_refs:
                a_ref[...] = jnp.zeros_like(a_ref)

        for a_ref, i in zip(acc_refs, dc_list):
            a_ref[...] += grads[n_t + i]

    res = pl.pallas_call(
        body,
        name=name,
        grid=(t // tm,),
        in_specs=[_tile_spec(tm, w, cb) for _, w, cb in tiled] + [_const_spec(c) for c in consts]
        + [_tile_spec(tm, w, cb) for _, w, cb in cots],
        out_specs=[_tile_spec(tm, tiled[i][1], 0) for i, _ in dt_list] + [_const_spec(consts[i]) for i in dc_list],
        out_shape=[jax.ShapeDtypeStruct((t, tiled[i][1]), dt) for i, dt in dt_list]
        + [jax.ShapeDtypeStruct(consts[i].shape, F32) for i in dc_list],
        compiler_params=_cparams(1),
    )(*[a for a, _, _ in tiled], *consts, *[a for a, _, _ in cots])
    return res[:len(dt_list)], res[len(dt_list):]


def mm(a, b, mode, *, tm, tn, name, out_dtypes=(F32,), epi=None, extras=(), into=None):
    m = a.shape[1] if mode == "tn" else a.shape[0]
    kd = a.shape[0] if mode == "tn" else a.shape[1]
    n = b.shape[0] if mode == "nt" else b.shape[1]
    tm, tn = min(tm, m), min(tn, n)
    if mode == "nn":
        a_spec = pl.BlockSpec((tm, kd), lambda i, j: (i, 0))
        b_spec = pl.BlockSpec((kd, tn), lambda i, j: (0, j))
    elif mode == "nt":
        a_spec = pl.BlockSpec((tm, kd), lambda i, j: (i, 0))
        b_spec = pl.BlockSpec((tn, kd), lambda i, j: (j, 0))
    else:
        a_spec = pl.BlockSpec((kd, tm), lambda i, j: (0, i))
        b_spec = pl.BlockSpec((kd, tn), lambda i, j: (0, j))
    n_e = len(extras)
    o_spec = pl.BlockSpec((tm, tn), lambda i, j: (i, j))

    if into is not None:
        buf, place = into

        def body_into(a_ref, b_ref, buf_ref, o_ref):
            o_ref[0, 0] = lax.dot_general(a_ref[...].astype(BF16), b_ref[...].astype(BF16), _MDIMS[mode],
                                          preferred_element_type=F32)

        return pl.pallas_call(
            body_into,
            name=name,
            grid=(m // tm, n // tn),
            in_specs=[a_spec, b_spec, pl.BlockSpec(memory_space=pl.ANY)],
            out_specs=pl.BlockSpec((1, 1, tm, tn), lambda i, j: (*place(i, j), 0)),
            out_shape=jax.ShapeDtypeStruct(buf.shape, F32),
            input_output_aliases={2: 0},
            compiler_params=_cparams(2),
        )(a, b, buf)

    def body(a_ref, b_ref, *refs):
        c = lax.dot_general(a_ref[...].astype(BF16), b_ref[...].astype(BF16), _MDIMS[mode],
                            preferred_element_type=F32)
        res = epi(c, *[r[...] for r in refs[:n_e]]) if epi is not None else (c,)
        for o_ref, val in zip(refs[n_e:], res):
            o_ref[...] = val.astype(o_ref.dtype)

    res = pl.pallas_call(
        body,
        name=name,
        grid=(m // tm, n // tn),
        in_specs=[a_spec, b_spec] + [o_spec] * n_e,
        out_specs=[o_spec] * len(out_dtypes),
        out_shape=[jax.ShapeDtypeStruct((m, n), dt) for dt in out_dtypes],
        compiler_params=_cparams(2),
    )(a, b, *extras)
    return res if len(out_dtypes) > 1 else res[0]


P_WIDTH = 7680
RWKV_COL0 = 4096
RWKV_WIDTH = 3584
SHIFT_BLK = 512


def _shift_down(p, prev_row):
    rows = lax.broadcasted_iota(jnp.int32, p.shape, 0)
    return jnp.where(rows == 0, prev_row, pltpu.roll(p, 1, 0))


def shiftmix_fwd(p_all, sbp, *, tm):
    t = p_all.shape[0]
    tm = min(tm, t)
    c0 = RWKV_COL0 // SHIFT_BLK
    hb = tm // 8

    def body(p_ref, halo_ref, sb_ref, q_ref):
        p = p_ref[...]
        prev = jnp.where(pl.program_id(0) == 0, 0.0, halo_ref[7:8, :])
        q_ref[...] = p * sb_ref[0:1, :] + _shift_down(p, prev) * sb_ref[1:2, :]

    return pl.pallas_call(
        body,
        name="shiftmix_fwd",
        grid=(t // tm, RWKV_WIDTH // SHIFT_BLK),
        in_specs=[
            pl.BlockSpec((tm, SHIFT_BLK), lambda i, j: (i, c0 + j)),
            pl.BlockSpec((8, SHIFT_BLK), lambda i, j: (jnp.maximum(i * hb - 1, 0), c0 + j)),
            pl.BlockSpec((2, SHIFT_BLK), lambda i, j: (0, j)),
        ],
        out_specs=pl.BlockSpec((tm, SHIFT_BLK), lambda i, j: (i, j)),
        out_shape=jax.ShapeDtypeStruct((t, RWKV_WIDTH), F32),
        compiler_params=_cparams(2),
    )(p_all, p_all, sbp)


def shiftmix_bwd(dq, col0, p_all, sbp, *, tm, name):
    t, w = dq.shape
    n_i = t // tm
    hb = tm // 8
    cq = col0 // SHIFT_BLK
    cp = (RWKV_COL0 + col0) // SHIFT_BLK

    def body(dq_ref, dqn_ref, p_ref, ph_ref, sb_ref, dp_ref, dsb_ref):
        i = pl.program_id(1)
        dq_t = dq_ref[...]
        rows = lax.broadcasted_iota(jnp.int32, dq_t.shape, 0)
        nxt = jnp.where(i == n_i - 1, 0.0, dqn_ref[0:1, :])
        up = jnp.where(rows == tm - 1, nxt, pltpu.roll(dq_t, tm - 1, 0))
        dp_ref[...] = (dq_t * sb_ref[0:1, :] + up * sb_ref[1:2, :]).astype(dp_ref.dtype)
        p = p_ref[...]
        prev = jnp.where(i == 0, 0.0, ph_ref[7:8, :])
        s0 = jnp.sum(dq_t * p, axis=0, keepdims=True)
        s1 = jnp.sum(dq_t * _shift_down(p, prev), axis=0, keepdims=True)
        two = lax.broadcasted_iota(jnp.int32, (2, SHIFT_BLK), 0)

        @pl.when(i == 0)
        def _():
            dsb_ref[...] = jnp.zeros_like(dsb_ref)

        dsb_ref[...] += jnp.where(two == 0, s0, s1)

    return pl.pallas_call(
        body,
        name=name,
        grid=(w // SHIFT_BLK, n_i),
        in_specs=[
            pl.BlockSpec((tm, SHIFT_BLK), lambda j, i: (i, j)),
            pl.BlockSpec((8, SHIFT_BLK), lambda j, i: (jnp.minimum((i + 1) * hb, t // 8 - 1), j)),
            pl.BlockSpec((tm, SHIFT_BLK), lambda j, i: (i, cp + j)),
            pl.BlockSpec((8, SHIFT_BLK), lambda j, i: (jnp.maximum(i * hb - 1, 0), cp + j)),
            pl.BlockSpec((2, SHIFT_BLK), lambda j, i: (0, cq + j)),
        ],
        out_specs=[
            pl.BlockSpec((tm, SHIFT_BLK), lambda j, i: (i, j)),
            pl.BlockSpec((2, SHIFT_BLK), lambda j, i: (0, j)),
        ],
        out_shape=[jax.ShapeDtypeStruct((t, w), BF16), jax.ShapeDtypeStruct((2, w), F32)],
        compiler_params=_cparams(2),
    )(dq, dq, p_all, p_all, sbp)


def final_call(h1, m3, tgt, g_final, *, tm):
    t = h1.shape[0]

    def body(h1_ref, m3_ref, tgt_ref, g_ref, dh_ref, dhb_ref, dg_ref, loss_ref):
        loss, vjp = jax.vjp(f_final, h1_ref[...], m3_ref[...], tgt_ref[...], g_ref[...])
        dh, _, _, dg = vjp(jnp.ones((), F32))
        dh_ref[...] = dh
        dhb_ref[...] = dh.astype(BF16)

        @pl.when(pl.program_id(0) == 0)
        def _():
            dg_ref[...] = jnp.zeros_like(dg_ref)
            loss_ref[...] = jnp.zeros_like(loss_ref)

        dg_ref[...] += dg
        loss_ref[...] += jnp.full(loss_ref.shape, loss, F32)

    tile = _tile_spec(tm, D_MODEL, 0)
    return pl.pallas_call(
        body,
        name="final_loss",
        grid=(t // tm,),
        in_specs=[tile, tile, tile, _const_spec(g_final)],
        out_specs=[tile, tile, _const_spec(g_final), pl.BlockSpec((8, 128), lambda i: (0, 0))],
        out_shape=[jax.ShapeDtypeStruct((t, D_MODEL), F32), jax.ShapeDtypeStruct((t, D_MODEL), BF16),
                   jax.ShapeDtypeStruct(g_final.shape, F32), jax.ShapeDtypeStruct((8, 128), F32)],
        compiler_params=_cparams(1),
    )(h1, m3, tgt, g_final)


N_SGU = 2048
N_RWKV = 3360
LORA_W, LORA_A, LORA_G = 64, 64, 160


def _pad_rwkv_cols(z):
    zero = lambda n: jnp.zeros(z.shape[:-1] + (n,), z.dtype)
    return jnp.concatenate([z[..., :3072], z[..., 3072:3136], zero(64), z[..., 3136:3200], zero(64),
                            z[..., 3200:3360], zero(96)], axis=-1)


def _unpad_rwkv_cols(z):
    return jnp.concatenate([z[..., :3072], z[..., 3072:3136], z[..., 3200:3264], z[..., 3328:3488]], axis=-1)


def _pad_win_rows(wt):
    z = wt[N_SGU:N_SGU + N_RWKV]
    zero = lambda n: jnp.zeros((n, wt.shape[1]), wt.dtype)
    return jnp.concatenate([wt[:N_SGU], wt[N_SGU + N_RWKV:], z[:3072], z[3072:3136], zero(64), z[3136:3200], zero(64),
                            z[3200:3360], zero(96)], axis=0)


def _unpad_win_rows(wt):
    z = wt[RWKV_COL0:]
    return jnp.concatenate([wt[:N_SGU], z[:3072], z[3072:3136], z[3200:3264], z[3328:3488], wt[N_SGU:RWKV_COL0]],
                           axis=0)


def _pad_rows(w, n):
    return jnp.concatenate([w, jnp.zeros((n - w.shape[0],) + w.shape[1:], w.dtype)], axis=0)


def _relu2_epi(c):
    return c, jnp.square(jnp.maximum(c, 0.0))


def _relu2_bwd_epi(c, hid):
    return (c * (2.0 * jnp.maximum(hid.astype(F32), 0.0)),)


def _add_epi(c, x):
    return (c + x,)


def _pre_fwd(*args):
    res = f_pre(*args)
    return res[1], res[2], res[4], res[5], res[6]


def local_step(x, tgt, w, late_pack, late_weights, pair_start, pair_finish, pack_early):
    d = D_MODEL
    win_pt = _pad_win_rows(w["w_in"])
    sbp = _pad_rwkv_cols(w["shift_b"])
    wl = _pad_rows(w["w_lora_w"], 128)
    al = _pad_rows(w["a_lora_w"], 128)
    gl = _pad_rows(w["g_lora_w"], 256)
    sbt = w["sgu_b"].T

    (a_bf,) = ew_call(lambda x_, g_: (f_norm_in(x_, g_)[0],), [(x, d, 0)], [w["g_mix"]], [(d, BF16)], tm=256,
                      name="norm_in")
    p_all = mm(a_bf, win_pt, "nt", tm=2048, tn=640, name="mm_in")
    sgu_t = [(p_all, 2 * d, 0)]
    sgu_c = [w["sgu_ln_w"], w["sgu_ln_b"], w["sgu_w"], sbt]
    (s_bf,) = ew_call(f_sgu, sgu_t, sgu_c, [(d, BF16)], tm=256, name="sgu_fwd")
    ya = mm(s_bf, w["w_proj_a"], "nn", tm=512, tn=1024, name="mm_proj_a")
    q = shiftmix_fwd(p_all, sbp, tm=1024)
    pre_t = [(q, RWKV_WIDTH, 0)]
    pre_c = [wl, w["w0"], al, w["a0"], gl, w["k_k"], w["k_a"]]
    lw, kp, na, nb, g = ew_call(_pre_fwd, pre_t, pre_c, [(d, F32)] * 5, tm=256, name="rwkv_pre_fwd")
    scan_ops = [(q, 0), (lw, 0), (kp, 0), (q, 2), (na, 0), (nb, 0)]
    o, s0s, late_all = scan_fwd(scan_ops, late_pack)
    w = {**w, **late_weights(late_all)}
    post_t = [(o, d, 0), (q, d, 0), (kp, d, 0), (q, d, 2), (g, d, 0)]
    post_c = [w["ln_x_w"], w["ln_x_b"], w["r_k"]]
    (ob_bf,) = ew_call(f_post, post_t, post_c, [(d, BF16)], tm=256, name="rwkv_post_fwd")
    yb = mm(ob_bf, w["w_proj_b"], "nn", tm=512, tn=1024, name="mm_proj_b")
    mix_t = [(ya, d, 0), (yb, d, 0), (p_all, d, 2), (p_all, d, 3)]
    (mixed_bf,) = ew_call(f_mix, mix_t, [], [(d, BF16)], tm=256, name="mix_fwd")
    h1 = mm(mixed_bf, w["w_out"], "nn", tm=512, tn=1024, name="mm_out", epi=_add_epi, extras=(x,))
    (f_bf,) = ew_call(lambda h_, g_: (f_ffn_in(h_, g_)[0],), [(h1, d, 0)], [w["g_ffn"]], [(d, BF16)], tm=256,
                      name="ffn_norm")
    hid, act_bf = mm(f_bf, w["w_ffn1"], "nn", tm=2048, tn=1024, name="mm_ffn1", out_dtypes=(BF16, BF16), epi=_relu2_epi)
    m3 = mm(act_bf, w["w_ffn2"], "nn", tm=1024, tn=512, name="mm_ffn2")
    dh2, dh2_bf, dg_final, loss = final_call(h1, m3, tgt, w["g_final"], tm=256)

    dhid_bf = mm(dh2_bf, w["w_ffn2"], "nt", tm=2048, tn=1024, name="mm_dact", out_dtypes=(BF16,), epi=_relu2_bwd_epi,
                 extras=(hid,))
    late_g = lax.empty((N_CHIPS, 2, PACK_ROWS, HALF_W), F32)
    late_g = mm(act_bf, dh2_bf, "tn", tm=512, tn=HALF_W, name="mm_dw_ffn2",
                into=(late_g, lambda i, j: (i // 2, j, PIECE_OFF["w_ffn2"] // 512 + i % 2)))
    df = mm(dhid_bf, w["w_ffn1"], "nt", tm=1024, tn=512, name="mm_df")
    late_g = mm(f_bf, dhid_bf, "tn", tm=512, tn=HALF_W, name="mm_dw_ffn1",
                into=(late_g, lambda i, j: (j // 2, j % 2, PIECE_OFF["w_ffn1"] // 512 + i)))
    (dh1, dh1_bf), (dg_ffn,) = ew_vjp_call(f_ffn_in, [(h1, d, 0)], [w["g_ffn"]], [(df, d, 0), (dh2, d, 0)],
                                           [(F32, BF16)], [True], tm=256, name="ffn_norm_bwd")
    dmixed = mm(dh1_bf, w["w_out"], "nt", tm=512, tn=1024, name="mm_dmixed")
    late_g = mm(mixed_bf, dh1_bf, "tn", tm=256, tn=HALF_W, name="mm_dw_out",
                into=(late_g, lambda i, j: (i, j, PIECE_OFF["w_out"] // 256)))
    (dya_bf, dyb_bf, dga_bf, dgb_bf), _ = ew_vjp_call(f_mix, mix_t, [], [(dmixed, d, 0)], [(BF16,)] * 4, [], tm=256,
                                                      name="mix_bwd")
    dob = mm(dyb_bf, w["w_proj_b"], "nt", tm=512, tn=1024, name="mm_dob")
    late_g = mm(ob_bf, dyb_bf, "tn", tm=256, tn=HALF_W, name="mm_dw_proj_b",
                into=(late_g, lambda i, j: (i, j, PIECE_OFF["w_proj_b"] // 256)))
    late_state, late_token = pair_start(late_g, "late")
    post_c_after = [w["ln_x_w"] + late_token[:1, :1]] + post_c[1:]
    (do, dr_p, dkp_p, dv_p, dg), (dlnx_w, dlnx_b, dr_k) = ew_vjp_call(
        f_post, post_t, post_c_after, [(dob, d, 0)], [(F32,)] * 5, [True] * 3, tm=256, name="rwkv_post_bwd")
    late_part, late_part16 = pair_finish(late_state, do, "late")
    *scan_g, late_slots = scan_bwd(scan_ops, s0s, do, late_part16)
    pre_g = [(z, d, 0) for z in scan_g] + [(dg, d, 0), (dr_p, d, 0), (dkp_p, d, 0), (dv_p, d, 0)]
    (dq,), (dwl, dw0, dal, da0, dgl, dk_k, dk_a) = ew_vjp_call(
        f_pre, pre_t, pre_c, pre_g, [(F32,)], [True] * 7, tm=128, name="rwkv_pre_bwd")
    dp_rwkv, dsb = shiftmix_bwd(dq, 0, p_all, sbp, tm=512, name="shiftmix_bwd")
    ds = mm(dya_bf, w["w_proj_a"], "nt", tm=512, tn=1024, name="mm_ds")
    d_proj_a = mm(s_bf, dya_bf, "tn", tm=512, tn=1024, name="mm_dw_proj_a")
    (dp_sgu,), (dln_w, dln_b, dsw, dsbt) = ew_vjp_call(f_sgu, sgu_t, sgu_c, [(ds, d, 0)], [(BF16,)], [True] * 4,
                                                       tm=256, name="sgu_bwd")
    dp_all = jnp.concatenate([dp_sgu, dga_bf, dgb_bf, dp_rwkv], axis=1)
    d_in_pt = mm(dp_all, a_bf, "tn", tm=1280, tn=1024, name="mm_dw_in")
    early_state, early_token = pair_start(pack_early({
        "w_in": _unpad_win_rows(d_in_pt), "w_proj_a": d_proj_a, "w_lora_w": dwl[:LORA_W], "a_lora_w": dal[:LORA_A],
        "g_lora_w": dgl[:LORA_G]}), "early")
    da = mm(dp_all, win_pt, "nn", tm=1024, tn=256, name="mm_da")
    g_mix_after = w["g_mix"] + early_token[:1, :1]
    (grad_x,), (dg_mix,) = ew_vjp_call(f_norm_in, [(x, d, 0)], [g_mix_after], [(da, d, 0), (dh1, d, 0)], [(F32,)],
                                       [True], tm=256, name="norm_in_bwd")

    grads = {
        "g_mix": dg_mix, "sgu_ln_w": dln_w, "sgu_ln_b": dln_b, "sgu_w": dsw, "sgu_b": dsbt.T,
        "shift_b": _unpad_rwkv_cols(dsb),
        "w0": dw0, "a0": da0, "k_k": dk_k, "k_a": dk_a, "r_k": dr_k, "ln_x_w": dlnx_w, "ln_x_b": dlnx_b,
        "g_ffn": dg_ffn, "g_final": dg_final,
    }
    return loss[0, 0], grad_x, grads, (late_part, late_slots), early_state


MESH = pl.DeviceIdType.MESH
N_CHIPS = 4
N_DEV = 8
PACK_ROWS = 2560
PACK_TILE = 512
SMALL_ROWS = 152
_ANY = pl.BlockSpec(memory_space=pl.ANY)


def _coords():
    return lax.axis_index("x"), lax.axis_index("y"), lax.axis_index("c")


def _other_chips(x, y):
    return [(1 - x, y), (x, 1 - y), (1 - x, 1 - y)]


def _remote(src, dst, send_sems, recv_sems, k, to):
    return pltpu.make_async_remote_copy(src_ref=src, dst_ref=dst, send_sem=send_sems.at[k], recv_sem=recv_sems.at[k],
                                        device_id=to, device_id_type=MESH)


def gather_shards(pack):
    def body(src_ref, out_ref, send_sems, recv_sems):
        x, y, c = _coords()
        me = 2 * x + y
        sib = (x, y, 1 - c)
        chips = _other_chips(x, y)
        first = [_remote(src_ref.at[c], out_ref.at[me, c], send_sems, recv_sems, k, (cx, cy, c))
                 for k, (cx, cy) in enumerate(chips)]
        for cp in first:
            cp.start()
        passed = []
        for k, (cx, cy) in enumerate(chips):
            j = 2 * cx + cy
            _remote(src_ref.at[c], out_ref.at[j, c], send_sems, recv_sems, k, (cx, cy, c)).wait_recv()
            fwd = _remote(out_ref.at[j, c], out_ref.at[j, c], send_sems, recv_sems, 3 + k, sib)
            fwd.start()
            passed.append(fwd)
        for k, (cx, cy) in enumerate(chips):
            j = 2 * cx + cy
            _remote(out_ref.at[j, 1 - c], out_ref.at[j, 1 - c], send_sems, recv_sems, 3 + k, sib).wait_recv()
        for cp in first + passed:
            cp.wait_send()

    return pl.pallas_call(
        body,
        name="gather_shards",
        in_specs=[_ANY],
        out_specs=_ANY,
        out_shape=jax.ShapeDtypeStruct((N_CHIPS,) + pack.shape, pack.dtype),
        scratch_shapes=[pltpu.SemaphoreType.DMA((6,)), pltpu.SemaphoreType.DMA((6,))],
    )(pack)


def reduce_pair(g, tag):
    def body(g_ref, got_ref, send_sems, recv_sems):
        x, y, c = _coords()
        sib = (x, y, 1 - c)
        sends = [_remote(g_ref.at[j, 1 - c], got_ref.at[j], send_sems, recv_sems, j, sib) for j in range(N_CHIPS)]
        for cp in sends:
            cp.start()
        for cp in sends:
            cp.wait_recv()
        for cp in sends:
            cp.wait_send()

    return pl.pallas_call(
        body,
        name="reduce_pair_" + tag,
        in_specs=[_ANY],
        out_specs=_ANY,
        out_shape=jax.ShapeDtypeStruct((N_CHIPS,) + g.shape[2:], g.dtype),
        scratch_shapes=[pltpu.SemaphoreType.DMA((N_CHIPS,)), pltpu.SemaphoreType.DMA((N_CHIPS,))],
    )(g)


def pair_sum(g, got, tag, *, tm):
    n, _, rows, width = g.shape

    def body(g0_ref, g1_ref, got_ref, out_ref, out16_ref):
        own = jnp.where(lax.axis_index("c") == 0, g0_ref[0, 0], g1_ref[0, 0])
        total = own + got_ref[0]
        out_ref[0] = total
        out16_ref[0] = total.astype(BF16)

    blk = pl.BlockSpec((1, tm, width), lambda j, i: (j, i, 0))
    return pl.pallas_call(
        body,
        name="pair_sum_" + tag,
        grid=(n, rows // tm),
        in_specs=[pl.BlockSpec((1, 1, tm, width), lambda j, i: (j, 0, i, 0)),
                  pl.BlockSpec((1, 1, tm, width), lambda j, i: (j, 1, i, 0)), blk],
        out_specs=[blk, blk],
        out_shape=[jax.ShapeDtypeStruct(got.shape, F32), jax.ShapeDtypeStruct(got.shape, BF16)],
        compiler_params=_cparams(2),
    )(g, g, got)


def reduce_chips(p):
    def body(p_ref, out_ref, send_sems, recv_sems):
        x, y, c = _coords()
        me = 2 * x + y
        chips = _other_chips(x, y)
        sends = [_remote(p_ref.at[2 * cx + cy], out_ref.at[me], send_sems, recv_sems, k, (cx, cy, c))
                 for k, (cx, cy) in enumerate(chips)]
        for cp in sends:
            cp.start()
        for k, (cx, cy) in enumerate(chips):
            _remote(p_ref.at[me], out_ref.at[2 * cx + cy], send_sems, recv_sems, k, (cx, cy, c)).wait_recv()
        for cp in sends:
            cp.wait_send()

    return pl.pallas_call(
        body,
        name="reduce_chips",
        in_specs=[_ANY],
        out_specs=_ANY,
        out_shape=jax.ShapeDtypeStruct(p.shape, p.dtype),
        scratch_shapes=[pltpu.SemaphoreType.DMA((3,)), pltpu.SemaphoreType.DMA((3,))],
    )(p)


def _pair_copies(g_ref, got_ref, send_sems, recv_sems):
    x, y, c = _coords()
    return [_remote(g_ref.at[j, 1 - c], got_ref.at[j], send_sems, recv_sems, j, (x, y, 1 - c)) for j in range(N_CHIPS)]


def reduce_pair_start(g, tag):
    hbm = pl.BlockSpec(memory_space=pltpu.HBM)
    sem = pl.BlockSpec(memory_space=pltpu.SEMAPHORE)
    got_shape = (N_CHIPS,) + g.shape[2:]

    def body(g_ref, got_ref, send_sems, recv_sems, g_thru, got_thru, token):
        for cp in _pair_copies(g_ref, got_ref, send_sems, recv_sems):
            cp.start()
        token[...] = jnp.zeros_like(token)

    return pl.pallas_call(
        body,
        name="reduce_pair_start_" + tag,
        out_shape=(pltpu.SemaphoreType.DMA((N_CHIPS,)), pltpu.SemaphoreType.DMA((N_CHIPS,)),
                   pltpu.HBM(g.shape, g.dtype), pltpu.HBM(got_shape, g.dtype), jax.ShapeDtypeStruct((8, 128), F32)),
        in_specs=(hbm, hbm),
        out_specs=(sem, sem, hbm, hbm, pl.BlockSpec(memory_space=pltpu.VMEM)),
        input_output_aliases={0: 2, 1: 3},
        compiler_params=pltpu.CompilerParams(has_side_effects=pltpu.SideEffectType.DATAFLOW_SIDE_EFFECTING),
    )(pltpu.with_memory_space_constraint(g, pltpu.HBM),
      pltpu.with_memory_space_constraint(lax.empty(got_shape, g.dtype), pltpu.HBM))


def reduce_pair_wait(send_sems, recv_sems, g_thru, got_thru, after, tag):
    hbm = pl.BlockSpec(memory_space=pltpu.HBM)
    sem = pl.BlockSpec(memory_space=pltpu.SEMAPHORE)

    def body(g_ref, got_ref, send_sems, recv_sems, after_ref, g_out, got_out):
        for cp in _pair_copies(g_ref, got_ref, send_sems, recv_sems):
            cp.wait_send()
            cp.wait_recv()

    return pl.pallas_call(
        body,
        name="reduce_pair_wait_" + tag,
        out_shape=(pltpu.HBM(g_thru.shape, g_thru.dtype), pltpu.HBM(got_thru.shape, got_thru.dtype)),
        in_specs=(hbm, hbm, sem, sem, pl.BlockSpec(memory_space=pl.ANY)),
        out_specs=(hbm, hbm),
        input_output_aliases={0: 0, 1: 1},
        compiler_params=pltpu.CompilerParams(has_side_effects=pltpu.SideEffectType.DATAFLOW_SIDE_EFFECTING),
    )(g_thru, got_thru, send_sems, recv_sems, after)


def _chip_copies(p_ref, slots_ref, send_sems, recv_sems):
    x, y, c = _coords()
    me = 2 * x + y
    return [(_remote(p_ref.at[2 * cx + cy], slots_ref.at[me], send_sems, recv_sems, k, (cx, cy, c)),
             _remote(p_ref.at[me], slots_ref.at[2 * cx + cy], send_sems, recv_sems, k, (cx, cy, c)))
            for k, (cx, cy) in enumerate(_other_chips(x, y))]


def reduce_chips_start(p):
    hbm = pl.BlockSpec(memory_space=pltpu.HBM)
    sem = pl.BlockSpec(memory_space=pltpu.SEMAPHORE)

    def body(p_ref, slots_ref, send_sems, recv_sems, p_thru, slots_thru, token):
        for send, _ in _chip_copies(p_ref, slots_ref, send_sems, recv_sems):
            send.start()
        token[...] = jnp.zeros_like(token)

    return pl.pallas_call(
        body,
        name="reduce_chips_start",
        out_shape=(pltpu.SemaphoreType.DMA((3,)), pltpu.SemaphoreType.DMA((3,)), pltpu.HBM(p.shape, p.dtype),
                   pltpu.HBM(p.shape, p.dtype), jax.ShapeDtypeStruct((8, 128), F32)),
        in_specs=(hbm, hbm),
        out_specs=(sem, sem, hbm, hbm, pl.BlockSpec(memory_space=pltpu.VMEM)),
        input_output_aliases={0: 2, 1: 3},
        compiler_params=pltpu.CompilerParams(has_side_effects=pltpu.SideEffectType.DATAFLOW_SIDE_EFFECTING),
    )(pltpu.with_memory_space_constraint(p, pltpu.HBM),
      pltpu.with_memory_space_constraint(lax.empty(p.shape, p.dtype), pltpu.HBM))


def reduce_chips_wait(send_sems, recv_sems, p_thru, slots_thru, after):
    hbm = pl.BlockSpec(memory_space=pltpu.HBM)
    sem = pl.BlockSpec(memory_space=pltpu.SEMAPHORE)

    def body(p_ref, slots_ref, send_sems, recv_sems, after_ref, p_dead, slots_out):
        for send, arrival in _chip_copies(p_ref, slots_ref, send_sems, recv_sems):
            send.wait_send()
            arrival.wait_recv()

    return pl.pallas_call(
        body,
        name="reduce_chips_wait",
        out_shape=(pltpu.HBM(p_thru.shape, p_thru.dtype), pltpu.HBM(slots_thru.shape, slots_thru.dtype)),
        in_specs=(hbm, hbm, sem, sem, pl.BlockSpec(memory_space=pl.ANY)),
        out_specs=(hbm, hbm),
        input_output_aliases={0: 0, 1: 1},
        compiler_params=pltpu.CompilerParams(has_side_effects=pltpu.SideEffectType.DATAFLOW_SIDE_EFFECTING),
    )(p_thru, slots_thru, send_sems, recv_sems, after)[1]


def sum_with_own(own, slots, index_fn, after, *, tm, name):
    n, rows, width = slots.shape

    def body(*refs):
        mine = index_fn()
        acc = None
        for s in range(n):
            term = jnp.where(mine == s, refs[s][0], refs[n + s][0].astype(F32))
            acc = term if acc is None else acc + term
        refs[-1][...] = acc

    slot_specs = [pl.BlockSpec((1, tm, width), lambda i, s=s: (s, i, 0)) for s in range(n)]
    return pl.pallas_call(
        body,
        name=name,
        grid=(rows // tm,),
        in_specs=slot_specs + slot_specs + [pl.BlockSpec(after.shape, lambda i: (0,) * after.ndim)],
        out_specs=pl.BlockSpec((tm, width), lambda i: (i, 0)),
        out_shape=jax.ShapeDtypeStruct((rows, width), F32),
        compiler_params=_cparams(1),
    )(*([own] * n), *([slots] * n), after)


def exchange_halves(s, tag):
    rq = PACK_TILE
    nq = s.shape[0] // rq

    def body(s_ref, out_ref, sbuf, rbuf, send_sems, recv_sems, in_sems, out_sems):
        x, y, c = _coords()
        sib = (x, y, 1 - c)
        rows = lambda q: pl.ds(q * rq, rq)
        loads = [pltpu.make_async_copy(s_ref.at[rows(q)], sbuf.at[rows(q)], in_sems.at[q]) for q in range(nq)]
        for cp in loads:
            cp.start()
        sends = []
        for q in range(nq):
            loads[q].wait()
            sends.append(_remote(sbuf.at[rows(q)], rbuf.at[rows(q)], send_sems, recv_sems, q, sib))
            sends[q].start()
        stores = []
        for q in range(nq):
            sends[q].wait_recv()
            stores.append(pltpu.make_async_copy(rbuf.at[rows(q)], out_ref.at[rows(q)], out_sems.at[q]))
            stores[q].start()
        for cp in sends:
            cp.wait_send()
        for cp in stores:
            cp.wait()

    return pl.pallas_call(
        body,
        name="exchange_halves_" + tag,
        in_specs=[_ANY],
        out_specs=_ANY,
        out_shape=jax.ShapeDtypeStruct(s.shape, s.dtype),
        scratch_shapes=[pltpu.VMEM(s.shape, s.dtype), pltpu.VMEM(s.shape, s.dtype)]
        + [pltpu.SemaphoreType.DMA((nq,))] * 4,
        compiler_params=pltpu.CompilerParams(vmem_limit_bytes=VMEM_LIMIT),
    )(s)


def sum_all(s, after):
    def body(s_ref, after_ref, out_ref, slots, mine, theirs, send_sems, recv_sems):
        x, y, c = _coords()
        me = 2 * x + y
        chips = _other_chips(x, y)
        sends = [_remote(s_ref, slots.at[me], send_sems, recv_sems, k, (cx, cy, c)) for k, (cx, cy) in enumerate(chips)]
        for cp in sends:
            cp.start()
        for k, (cx, cy) in enumerate(chips):
            _remote(s_ref, slots.at[2 * cx + cy], send_sems, recv_sems, k, (cx, cy, c)).wait_recv()
        slots[me] = s_ref[...]
        acc = ((slots[0] + slots[1]) + slots[2]) + slots[3]
        mine[...] = acc
        swap = _remote(mine, theirs, send_sems, recv_sems, 3, (x, y, 1 - c))
        swap.start()
        swap.wait_recv()
        out_ref[...] = acc + theirs[...]
        swap.wait_send()
        for cp in sends:
            cp.wait_send()

    vmem = pl.BlockSpec(memory_space=pltpu.VMEM)
    return pl.pallas_call(
        body,
        name="sum_all",
        in_specs=[vmem, vmem],
        out_specs=vmem,
        out_shape=jax.ShapeDtypeStruct(s.shape, s.dtype),
        scratch_shapes=[pltpu.VMEM((N_CHIPS,) + s.shape, s.dtype), pltpu.VMEM(s.shape, s.dtype),
                        pltpu.VMEM(s.shape, s.dtype), pltpu.SemaphoreType.DMA((4,)), pltpu.SemaphoreType.DMA((4,))],
        compiler_params=pltpu.CompilerParams(vmem_limit_bytes=VMEM_LIMIT),
    )(s, after)


ADAM_LR = 0.001
ADAM_B1 = 0.9
ADAM_B2 = 0.999
ADAM_EPS = 1e-08
ADAM_WD = 0.01
ADAM_STEP = 10


def f_adamw(g, w, m, v):
    m = ADAM_B1 * m + (1.0 - ADAM_B1) * g
    v = ADAM_B2 * v + (1.0 - ADAM_B2) * jnp.square(g)
    m_hat = m / (1.0 - ADAM_B1 ** ADAM_STEP)
    v_hat = v / (1.0 - ADAM_B2 ** ADAM_STEP)
    delta = -ADAM_LR * (m_hat / (jnp.sqrt(v_hat) + ADAM_EPS) + ADAM_WD * w)
    return delta, m, v


def adamw_call(g, w, m, v, *, tm, name):
    width = g.shape[1]
    return ew_call(f_adamw, [(g, width, 0), (w, width, 0), (m, width, 0), (v, width, 0)], [], [(width, F32)] * 3,
                   tm=tm, name=name)


def adamw_halves(g_own, g_other, w, m, v, *, tm):
    _, rows, width = w.shape

    def body(go_ref, gx_ref, w_ref, m_ref, v_ref, g_ref, d_ref, nm_ref, nv_ref):
        g = jnp.where(pl.program_id(0) == lax.axis_index("c"), go_ref[...], gx_ref[...])
        delta, nm, nv = f_adamw(g, w_ref[0], m_ref[0], v_ref[0])
        g_ref[0] = g
        d_ref[0] = delta
        nm_ref[0] = nm
        nv_ref[0] = nv

    half = pl.BlockSpec((tm, width), lambda h, i: (i, 0))
    full = pl.BlockSpec((1, tm, width), lambda h, i: (h, i, 0))
    return pl.pallas_call(
        body,
        name="adamw_sharded",
        grid=(2, rows // tm),
        in_specs=[half, half, full, full, full],
        out_specs=[full] * 4,
        out_shape=[jax.ShapeDtypeStruct(w.shape, F32)] * 4,
        compiler_params=_cparams(2),
    )(g_own, g_other, w, m, v)


EARLY = ["w_in", "w_proj_a", "w_lora_w", "a_lora_w", "g_lora_w"]
LATE = ["w_ffn1", "w_ffn2", "w_proj_b", "w_out"]
SHARDED = EARLY + LATE
LORAS = ["w_lora_w", "a_lora_w", "g_lora_w"]
HALF_W = 512
PIECE_ROWS = {"w_in": 1864, "w_ffn1": 1024, "w_ffn2": 1024, "w_proj_a": 256, "w_proj_b": 256, "w_out": 256,
              "w_lora_w": 32, "a_lora_w": 32, "g_lora_w": 80}
PIECE_OFF = {"w_in": 0, "w_proj_a": 1920, "w_lora_w": 2176, "a_lora_w": 2208, "g_lora_w": 2240,
             "w_ffn1": 0, "w_ffn2": 1024, "w_proj_b": 2048, "w_out": 2304}
LO_OFF = 2320
SHARD_AXIS = {"w_in": 1, "w_proj_a": 0, "w_lora_w": 1, "a_lora_w": 1, "g_lora_w": 1, "w_proj_b": 0, "w_out": 0,
              "w_ffn1": 1, "w_ffn2": 0}
SHARD_SHAPE = {"w_in": (1024, 1864), "w_proj_a": (256, 1024), "w_lora_w": (64, 256), "a_lora_w": (64, 256),
               "g_lora_w": (160, 256), "w_proj_b": (256, 1024), "w_out": (256, 1024), "w_ffn1": (1024, 1024),
               "w_ffn2": (1024, 1024)}
SHIFT_SHARD = (2, 840)
VECTORS = ["g_mix", "sgu_ln_w", "sgu_ln_b", "w0", "a0", "k_k", "k_a", "r_k", "ln_x_w", "ln_x_b", "g_ffn", "g_final"]
SMALL = VECTORS + ["sgu_w", "sgu_b"]
SMALL_SHAPE = {**{n: (1, 1024) for n in VECTORS}, "sgu_w": (8, 128, 128), "sgu_b": (8, 128)}
WEIGHTS = ["g_mix", "w_in", "sgu_ln_w", "sgu_ln_b", "sgu_w", "sgu_b", "w_proj_a", "shift_b", "w_lora_w", "w0",
           "a_lora_w", "a0", "g_lora_w", "k_k", "k_a", "r_k", "ln_x_w", "ln_x_b", "w_proj_b", "w_out", "g_ffn",
           "w_ffn1", "w_ffn2", "g_final"]


def _size(shape):
    n = 1
    for s in shape:
        n *= s
    return n


def _pack_rows(parts, rows, dtype):
    flat = jnp.concatenate([p.reshape(-1).astype(dtype) for p in parts])
    return jnp.concatenate([flat, jnp.zeros((rows * 1024 - flat.shape[0],), dtype)]).reshape(rows, 1024)


def _unpack_rows(packed, shapes):
    flat = packed.reshape(-1)
    out, off = [], 0
    for shp in shapes:
        out.append(flat[off:off + _size(shp)].reshape(shp))
        off += _size(shp)
    return out


def _shard_of(name, full, j):
    ax = SHARD_AXIS[name]
    n = SHARD_SHAPE[name][ax]
    return lax.slice_in_dim(full, j * n, (j + 1) * n, axis=ax)


def _pad_cols(z, n):
    return jnp.concatenate([z, jnp.zeros((z.shape[0], n - z.shape[1]), z.dtype)], axis=1)


def _row_form(name, s):
    return s.T if name == "w_in" else s


def _half_piece(name, rf, h):
    if name in LORAS:
        r = PIECE_ROWS[name]
        return _pad_cols(rf[h * r:(h + 1) * r], HALF_W)
    return rf[:, HALF_W * h:HALF_W * (h + 1)]


def _pack_half(group, rf_fn, h, dtype, tail=()):
    parts, pos, rows = [], 0, PACK_ROWS
    for n in group:
        if PIECE_OFF[n] > pos:
            parts.append(jnp.zeros((PIECE_OFF[n] - pos, HALF_W), dtype))
        parts.append(_half_piece(n, rf_fn(n), h).astype(dtype))
        pos = PIECE_OFF[n] + PIECE_ROWS[n]
    for t in tail:
        parts.append(t)
        pos += t.shape[0]
    parts.append(jnp.zeros((rows - pos, HALF_W), dtype))
    return jnp.concatenate(parts, axis=0)


def _piece(pack, name):
    return pack[PIECE_OFF[name]:PIECE_OFF[name] + PIECE_ROWS[name]]


def _join_halves(name, p0, p1):
    if name in LORAS:
        return jnp.concatenate([p0[:, :SHARD_SHAPE[name][1]], p1[:, :SHARD_SHAPE[name][1]]], axis=0)
    return jnp.concatenate([p0, p1], axis=1)


def _grad_row_form(name, full, j):
    if name == "w_in":
        return full[SHARD_SHAPE[name][1] * j:SHARD_SHAPE[name][1] * (j + 1)]
    return _shard_of(name, full, j)


def adamw_weight(name, g_own, g_other, w, m, v):
    rows, width = w.shape
    if name in LORAS:
        tm = PIECE_ROWS[name]
        grid = (2, 1)
        native = pl.BlockSpec((tm, width), lambda h, i: (h, 0))
    elif name == "w_in":
        tm, lanes = rows, 128
        grid = (2, HALF_W // lanes)
        native = pl.BlockSpec((tm, lanes), lambda h, i: (0, h * (HALF_W // lanes) + i))
    else:
        tm = 128
        grid = (2, rows // tm)
        native = pl.BlockSpec((tm, HALF_W), lambda h, i: (i, h))
    off = PIECE_OFF[name] // tm
    if name == "w_in":
        packed = pl.BlockSpec((tm, 128), lambda h, i: (0, i))
    else:
        packed = pl.BlockSpec((tm, HALF_W), lambda h, i: (off + i, 0))

    def body(go_ref, gx_ref, w_ref, m_ref, v_ref, g_ref, d_ref, nm_ref, nv_ref):
        g = jnp.where(pl.program_id(0) == lax.axis_index("c"), go_ref[...], gx_ref[...])[:, :w_ref.shape[1]]
        delta, nm, nv = f_adamw(g, w_ref[...], m_ref[...], v_ref[...])
        g_ref[...] = g
        d_ref[...] = delta
        nm_ref[...] = nm
        nv_ref[...] = nv

    return pl.pallas_call(
        body,
        name="adamw_" + name,
        grid=grid,
        in_specs=[packed, packed, native, native, native],
        out_specs=[native] * 4,
        out_shape=[jax.ShapeDtypeStruct(w.shape, F32)] * 4,
        compiler_params=_cparams(2),
    )(g_own, g_other, w, m, v)


def kernel(x, g_mix, w_in, sgu_ln_w, sgu_ln_b, sgu_w, sgu_b, w_proj_a, shift_b, w_lora_w, w0, a_lora_w, a0, g_lora_w, k_k, k_a, r_k, ln_x_w, ln_x_b, w_proj_b, w_out, g_ffn, w_ffn1, w_ffn2, g_final, loss_target, m_g_mix, m_w_in, m_sgu_ln_w, m_sgu_ln_b, m_sgu_w, m_sgu_b, m_w_proj_a, m_shift_b, m_w_lora_w, m_w0, m_a_lora_w, m_a0, m_g_lora_w, m_k_k, m_k_a, m_r_k, m_ln_x_w, m_ln_x_b, m_w_proj_b, m_w_out, m_g_ffn, m_w_ffn1, m_w_ffn2, m_g_final, v_g_mix, v_w_in, v_sgu_ln_w, v_sgu_ln_b, v_sgu_w, v_sgu_b, v_w_proj_a, v_shift_b, v_w_lora_w, v_w0, v_a_lora_w, v_a0, v_g_lora_w, v_k_k, v_k_a, v_r_k, v_ln_x_w, v_ln_x_b, v_w_proj_b, v_w_out, v_g_ffn, v_w_ffn1, v_w_ffn2, v_g_final):
    given = dict(zip(WEIGHTS, (g_mix, w_in, sgu_ln_w, sgu_ln_b, sgu_w, sgu_b, w_proj_a, shift_b, w_lora_w, w0, a_lora_w, a0, g_lora_w, k_k, k_a, r_k, ln_x_w, ln_x_b, w_proj_b, w_out, g_ffn, w_ffn1, w_ffn2, g_final)))
    mom_m = dict(zip(WEIGHTS, (m_g_mix, m_w_in, m_sgu_ln_w, m_sgu_ln_b, m_sgu_w, m_sgu_b, m_w_proj_a, m_shift_b, m_w_lora_w, m_w0, m_a_lora_w, m_a0, m_g_lora_w, m_k_k, m_k_a, m_r_k, m_ln_x_w, m_ln_x_b, m_w_proj_b, m_w_out, m_g_ffn, m_w_ffn1, m_w_ffn2, m_g_final)))
    mom_v = dict(zip(WEIGHTS, (v_g_mix, v_w_in, v_sgu_ln_w, v_sgu_ln_b, v_sgu_w, v_sgu_b, v_w_proj_a, v_shift_b, v_w_lora_w, v_w0, v_a_lora_w, v_a0, v_g_lora_w, v_k_k, v_k_a, v_r_k, v_ln_x_w, v_ln_x_b, v_w_proj_b, v_w_out, v_g_ffn, v_w_ffn1, v_w_ffn2, v_g_final)))
    chip = 2 * lax.axis_index("x") + lax.axis_index("y")

    def local_block(tree, n):
        return tree[n] if n == "g_final" else tree[n][0]

    sb = local_block(given, "shift_b")
    lo_part = lambda z: (z - z.astype(BF16).astype(F32)).astype(BF16)
    row_form = lambda tree: (lambda n: _row_form(n, local_block(tree, n)))
    tile16 = lambda z: jnp.pad(z, ((0, 16 - z.shape[0]), (0, HALF_W - z.shape[1])))
    sb_tiles = [tile16(f(sb[:, lanes])) for f in (lambda z: z.astype(BF16), lo_part)
                for lanes in (slice(0, HALF_W), slice(HALF_W, None))]
    tails = [[_half_piece(n, lo_part(local_block(given, n)), h) for n in LORAS] + sb_tiles for h in range(2)]
    pack_w = jnp.stack([_pack_half(EARLY, row_form(given), h, BF16, tails[h]) for h in range(2)])
    gathered = gather_shards(pack_w)
    gathered = lax.dynamic_update_index_in_dim(gathered, pack_w, chip, 0)
    pack_late = jnp.stack([_pack_half(LATE, row_form(given), h, BF16) for h in range(2)])

    def whole(group, got, own):
        half = lambda n, j, h: jnp.where(chip == j, _piece(own[h], n), _piece(got[j, h], n))
        shard = lambda n, j: _join_halves(n, half(n, j, 0), half(n, j, 1))
        return {n: jnp.concatenate([shard(n, j).astype(F32 if n == "w_in" else BF16) for j in range(N_CHIPS)],
                                   axis=0 if n == "w_in" else SHARD_AXIS[n]) for n in group}

    w = whole(EARLY, gathered, pack_w)
    late_weights = lambda got: whole(LATE, got, pack_late)
    off = LO_OFF
    for n in LORAS:
        r, cols = PIECE_ROWS[n], SHARD_SHAPE[n][1]
        lo = jnp.concatenate([jnp.concatenate([gathered[j, 0, off:off + r, :cols], gathered[j, 1, off:off + r, :cols]],
                                              axis=0) for j in range(N_CHIPS)], axis=1)
        w[n] = w[n].astype(F32) + lo.astype(F32)
        off += r
    sb_tile = lambda j, t, lanes: gathered[j, 0, off + 16 * t:off + 16 * t + 2, :lanes].astype(F32)
    rest = SHIFT_SHARD[1] - HALF_W
    w["shift_b"] = jnp.concatenate(
        [jnp.concatenate([sb_tile(j, 0, HALF_W) + sb_tile(j, 2, HALF_W), sb_tile(j, 1, rest) + sb_tile(j, 3, rest)],
                         axis=1) for j in range(N_CHIPS)], axis=1)
    for n in SMALL:
        w[n] = local_block(given, n).reshape(SMALL_SHAPE[n])

    def pair_start(g_pack, tag):
        *state, token = reduce_pair_start(g_pack, tag)
        return state, token

    def pair_finish(state, after, tag):
        return pair_sum(*reduce_pair_wait(*state, after, tag), tag, tm=PACK_TILE)

    pack_early = lambda g: jnp.stack([jnp.stack([_pack_half(EARLY, lambda n: _grad_row_form(n, g[n], j), h, F32)
                                                 for h in range(2)]) for j in range(N_CHIPS)])
    loss, grad_x, grads, (late_part, late_slots), early_state = local_step(
        x[0], loss_target[0], w, pack_late, late_weights, pair_start, pair_finish, pack_early)

    early_part, early_part16 = pair_finish(early_state, grad_x, "early")
    s_pack = _pack_rows([grads[n] for n in SMALL] + [grads["shift_b"], loss.reshape(1, 1)], SMALL_ROWS, F32)
    sends, recvs, part_thru, slots_thru, token = reduce_chips_start(early_part16)
    my_chip = lambda: 2 * lax.axis_index("x") + lax.axis_index("y")
    out_g, out_d, out_m, out_v = {}, {}, {}, {}

    def finish(group, tag, part, slots):
        half_sum = sum_with_own(part, slots, my_chip, token, tm=PACK_TILE, name="chip_sum_" + tag)
        other_half = exchange_halves(half_sum, tag)
        for n in group:
            res = adamw_weight(n, half_sum, other_half,
                               *[_row_form(n, local_block(t, n)) for t in (given, mom_m, mom_v)])
            for tree, z in zip((out_g, out_d, out_m, out_v), res):
                tree[n] = _row_form(n, z)

    finish(LATE, "late", late_part, late_slots)

    small_shapes = [SMALL_SHAPE[n] for n in SMALL]
    g_small = sum_all(s_pack, token)
    w_small = _pack_rows([local_block(given, n) for n in SMALL], SMALL_ROWS, F32)
    m_small = _pack_rows([local_block(mom_m, n) for n in SMALL], SMALL_ROWS, F32)
    v_small = _pack_rows([local_block(mom_v, n) for n in SMALL], SMALL_ROWS, F32)
    d_small, nm_small, nv_small = adamw_call(g_small, w_small, m_small, v_small, tm=SMALL_ROWS, name="adamw_small")
    *g_parts, loss = _unpack_rows(g_small, small_shapes + [(2, N_RWKV), ()])
    out_g.update(zip(SMALL, g_parts[:-1]))
    out_d.update(zip(SMALL, _unpack_rows(d_small, small_shapes)))
    out_m.update(zip(SMALL, _unpack_rows(nm_small, small_shapes)))
    out_v.update(zip(SMALL, _unpack_rows(nv_small, small_shapes)))
    g_sb = lax.dynamic_slice_in_dim(g_parts[-1], chip * SHIFT_SHARD[1], SHIFT_SHARD[1], axis=1)
    sb_args = [_pack_rows([z], 8, F32) for z in (g_sb, sb, local_block(mom_m, "shift_b"), local_block(mom_v, "shift_b"))]
    sb_res = adamw_call(*sb_args, tm=8, name="adamw_shift_b")
    out_g["shift_b"] = g_sb
    for tree, res in zip((out_d, out_m, out_v), sb_res):
        tree["shift_b"] = _unpack_rows(res, [SHIFT_SHARD])[0]

    after = (out_v["w_out"], nv_small, sb_res[2])
    early_slots = reduce_chips_wait(sends, recvs, part_thru, slots_thru, jnp.concatenate([z.reshape(-1)[:8] for z in after]))
    finish(EARLY, "early", early_part, early_slots)

    def block_of(tree, n):
        return tree[n].reshape(given[n].shape)

    return (loss, grad_x[None], *[block_of(out_g, n) for n in WEIGHTS], *[block_of(out_d, n) for n in WEIGHTS],
            *[block_of(out_m, n) for n in WEIGHTS], *[block_of(out_v, n) for n in WEIGHTS])
```

```python
import functools

import jax
import jax.numpy as jnp
from jax import lax
from jax.experimental import pallas as pl
from jax.experimental.pallas import tpu as pltpu

F32 = jnp.float32
BF16 = jnp.bfloat16

D_MODEL = 1024
N_HEADS = 16
HEAD = 64
SCAN_CHUNK = 64

VMEM_LIMIT = 56 * 1024 * 1024


_BDIMS = {
    "nn": (((2,), (1,)), ((0,), (0,))),
    "nt": (((2,), (2,)), ((0,), (0,))),
    "tn": (((1,), (1,)), ((0,), (0,))),
}


def _raw_bdot(x, y, mode, fine):
    if fine:
        return lax.dot_general(x, y, _BDIMS[mode], precision=lax.Precision.HIGH, preferred_element_type=F32)
    return lax.dot_general(x.astype(BF16), y.astype(BF16), _BDIMS[mode], preferred_element_type=F32)


@functools.partial(jax.custom_vjp, nondiff_argnums=(2, 3))
def bdot(x, y, mode, fine=True):
    return _raw_bdot(x, y, mode, fine)


def _bdot_fwd(x, y, mode, fine):
    return _raw_bdot(x, y, mode, fine), (x, y)


def _bdot_bwd(mode, fine, res, g):
    x, y = res
    if mode == "nn":
        return bdot(g, y, "nt", fine), bdot(x, g, "tn", fine)
    if mode == "nt":
        return bdot(g, y, "nn", fine), bdot(g, x, "tn", fine)
    return bdot(y, g, "nt", fine), bdot(x, g, "nn", fine)


bdot.defvjp(_bdot_fwd, _bdot_bwd)


def _scan_chunk(S0, r, lw, k, v, a, b):
    nh, lc, _ = r.shape
    ti = lax.broadcasted_iota(jnp.int32, (lc, lc), 0)
    si = lax.broadcasted_iota(jnp.int32, (lc, lc), 1)
    incl = (si <= ti).astype(F32)
    strict = (si < ti).astype(F32)
    eye = (si == ti).astype(F32)
    cl = bdot(jnp.broadcast_to(incl, (nh, lc, lc)), lw, "nn")
    cl_last = cl[:, lc - 1:lc, :]
    g_last = jnp.exp(cl_last - cl)
    at = a * jnp.exp(cl - lw)
    bt = b * jnp.exp(-cl)
    kt = k * jnp.exp(-cl)
    rt = r * jnp.exp(cl)
    ar = jnp.concatenate([at, rt], axis=1)
    ar_b = bdot(ar, bt, "nt", False)
    ar_k = bdot(ar, kt, "nt", False)
    m_ab, m_rb = ar_b[:, :lc] * strict, ar_b[:, lc:] * incl
    m_ak, m_rk = ar_k[:, :lc] * strict, ar_k[:, lc:] * incl
    x = eye + m_ab
    p = bdot(m_ab, m_ab, "nn", False)
    n = 2
    while n * 2 < lc:
        px = bdot(jnp.concatenate([p, x], axis=1), p, "nn", False)
        p = px[:, :lc]
        x = x + px[:, lc:]
        n *= 2
    x = x + bdot(x, p, "nn", False)
    ar_s = bdot(ar, S0, "nt", False)
    akrk_v = bdot(jnp.concatenate([m_ak, m_rk], axis=1), v, "nn", False)
    u = bdot(x, ar_s[:, :lc] + akrk_v[:, :lc], "nn", False)
    o = ar_s[:, lc:] + bdot(m_rb, u, "nn", False) + akrk_v[:, lc:]
    s_last = S0 * jnp.exp(cl_last) + bdot(jnp.concatenate([u, v], axis=1),
                                          jnp.concatenate([b * g_last, k * g_last], axis=1), "tn", False)
    return o, s_last


def _split_heads(z):
    return jnp.stack([z[:, HEAD * h:HEAD * (h + 1)] for h in range(N_HEADS)], axis=0)


def _merge_heads(z):
    return jnp.concatenate([z[h] for h in range(N_HEADS)], axis=1)


def _scan_specs(t, ops, rev):
    nc = t // SCAN_CHUNK
    row = (lambda c: nc - 1 - c) if rev else (lambda c: c)
    specs = [pl.BlockSpec((SCAN_CHUNK, D_MODEL), lambda c, cb=cb: (row(c), cb)) for _, cb in ops]
    state = pl.BlockSpec((1, N_HEADS, HEAD, HEAD), lambda c: (row(c), 0, 0, 0))
    return nc, specs, state


def scan_fwd(ops):
    t = ops[0][0].shape[0]
    nc, specs, state = _scan_specs(t, ops, False)

    def body(r_ref, lw_ref, k_ref, v_ref, a_ref, b_ref, o_ref, s0_ref, s_scr):
        @pl.when(pl.program_id(0) == 0)
        def _():
            s_scr[...] = jnp.zeros_like(s_scr)

        s0 = s_scr[...]
        s0_ref[0] = s0
        o, s_last = _scan_chunk(s0, *[_split_heads(z[...]) for z in (r_ref, lw_ref, k_ref, v_ref, a_ref, b_ref)])
        o_ref[...] = _merge_heads(o)
        s_scr[...] = s_last

    return pl.pallas_call(
        body,
        name="scan_fwd",
        grid=(nc,),
        in_specs=specs,
        out_specs=[pl.BlockSpec((SCAN_CHUNK, D_MODEL), lambda c: (c, 0)), state],
        out_shape=[jax.ShapeDtypeStruct((t, D_MODEL), F32), jax.ShapeDtypeStruct((nc, N_HEADS, HEAD, HEAD), F32)],
        scratch_shapes=[pltpu.VMEM((N_HEADS, HEAD, HEAD), F32)],
        compiler_params=_cparams(1),
    )(*[a for a, _ in ops])


def scan_bwd(ops, s0s, do, part):
    t = ops[0][0].shape[0]
    nc, specs, state = _scan_specs(t, ops + [(do, 0)], True)

    def body(r_ref, lw_ref, k_ref, v_ref, a_ref, b_ref, do_ref, s0_ref, part_ref, *rest):
        out_refs, slots_ref, ds_scr, send_sems, recv_sems = rest[:6], rest[6], rest[7], rest[8], rest[9]
        step = pl.program_id(0)
        x, y, c = _coords()
        me = 2 * x + y
        chips = _other_chips(x, y)
        sends = [_remote(part_ref.at[2 * cx + cy], slots_ref.at[me], send_sems, recv_sems, k, (cx, cy, c))
                 for k, (cx, cy) in enumerate(chips)]

        @pl.when(step == 0)
        def _():
            ds_scr[...] = jnp.zeros_like(ds_scr)
            for cp in sends:
                cp.start()

        _, vjp = jax.vjp(_scan_chunk, s0_ref[0],
                         *[_split_heads(z[...]) for z in (r_ref, lw_ref, k_ref, v_ref, a_ref, b_ref)])
        grads = vjp((_split_heads(do_ref[...]), ds_scr[...]))
        for o_ref, g in zip(out_refs, grads[1:]):
            o_ref[...] = _merge_heads(g)
        ds_scr[...] = grads[0]

        @pl.when(step == nc - 1)
        def _():
            for k, (cx, cy) in enumerate(chips):
                _remote(part_ref.at[me], slots_ref.at[2 * cx + cy], send_sems, recv_sems, k, (cx, cy, c)).wait_recv()
            for cp in sends:
                cp.wait_send()

    return pl.pallas_call(
        body,
        name="scan_bwd",
        grid=(nc,),
        in_specs=specs + [state, _ANY],
        out_specs=[pl.BlockSpec((SCAN_CHUNK, D_MODEL), lambda c: (nc - 1 - c, 0))] * 6 + [_ANY],
        out_shape=[jax.ShapeDtypeStruct((t, D_MODEL), F32)] * 6 + [jax.ShapeDtypeStruct(part.shape, part.dtype)],
        scratch_shapes=[pltpu.VMEM((N_HEADS, HEAD, HEAD), F32), pltpu.SemaphoreType.DMA((3,)),
                        pltpu.SemaphoreType.DMA((3,))],
        compiler_params=_cparams(1),
    )(*[a for a, _ in ops], do, s0s, part)


_MDIMS = {
    "nn": (((1,), (0,)), ((), ())),
    "nt": (((1,), (1,)), ((), ())),
    "tn": (((0,), (0,)), ((), ())),
}


def _raw_mdot(x, y, mode, exact):
    if exact:
        return lax.dot_general(x, y, _MDIMS[mode], precision=lax.Precision.HIGH, preferred_element_type=F32)
    return lax.dot_general(x.astype(BF16), y.astype(BF16), _MDIMS[mode], preferred_element_type=F32)


@functools.partial(jax.custom_vjp, nondiff_argnums=(2, 3))
def mdot(x, y, mode, exact):
    return _raw_mdot(x, y, mode, exact)


def _mdot_fwd(x, y, mode, exact):
    return _raw_mdot(x, y, mode, exact), (x, y)


def _mdot_bwd(mode, exact, res, g):
    x, y = res
    if mode == "nn":
        return mdot(g, y, "nt", exact), mdot(x, g, "tn", exact)
    if mode == "nt":
        return mdot(g, y, "nn", exact), mdot(g, x, "tn", exact)
    return mdot(y, g, "nt", exact), mdot(x, g, "nn", exact)


mdot.defvjp(_mdot_fwd, _mdot_bwd)


def _seg_ones():
    i = lax.broadcasted_iota(jnp.int32, (256, 256), 0) // HEAD
    j = lax.broadcasted_iota(jnp.int32, (256, 256), 1) // HEAD
    return (i == j).astype(BF16)


@jax.custom_vjp
def segsum(x):
    bd = _seg_ones()
    hi = x.astype(BF16)
    lo = (x - hi.astype(F32)).astype(BF16)
    cols = []
    for j in range(x.shape[1] // 256):
        sl = slice(256 * j, 256 * (j + 1))
        cols.append(jnp.dot(hi[:, sl], bd, preferred_element_type=F32)
                    + jnp.dot(lo[:, sl], bd, preferred_element_type=F32))
    return jnp.concatenate(cols, axis=1)


segsum.defvjp(lambda x: (segsum(x), None), lambda _, g: (segsum(g),))


NORM_EPS = 1e-6
LN_EPS = 1e-5
GN_EPS = 64e-5
SGU_CHUNK = 128
SGU_GROUPS = 8


def _rms(x, g):
    return x * lax.rsqrt(jnp.mean(x * x, axis=-1, keepdims=True) + NORM_EPS) * g


def f_norm_in(x, g):
    return _rms(x, g), x


def f_sgu(p, ln_w, ln_b, sw, sbt):
    tm = p.shape[0]
    z = 0.5 * p * (1.0 + lax.erf(p * 0.7071067811865476))
    u, v = z[:, :D_MODEL], z[:, D_MODEL:]
    mu = jnp.mean(v, axis=-1, keepdims=True)
    d = v - mu
    vn = d * lax.rsqrt(jnp.mean(d * d, axis=-1, keepdims=True) + LN_EPS) * ln_w + ln_b
    ii = lax.broadcasted_iota(jnp.int32, (SGU_CHUNK, SGU_CHUNK), 0)
    jj = lax.broadcasted_iota(jnp.int32, (SGU_CHUNK, SGU_CHUNK), 1)
    mask = (jj <= ii).astype(F32)
    gi = lax.broadcasted_iota(jnp.int32, (SGU_GROUPS, D_MODEL), 0)
    ci = lax.broadcasted_iota(jnp.int32, (SGU_GROUPS, D_MODEL), 1) // SGU_CHUNK
    bias = mdot(sbt, (gi == ci).astype(F32), "nn", True)
    rows = []
    for c in range(tm // SGU_CHUNK):
        cols = []
        for g in range(SGU_GROUPS):
            blk = vn[c * SGU_CHUNK:(c + 1) * SGU_CHUNK, g * SGU_CHUNK:(g + 1) * SGU_CHUNK]
            cols.append(mdot(sw[g] * mask, blk, "nn", False))
        rows.append(jnp.concatenate(cols, axis=1) + bias)
    return (u * jnp.concatenate(rows, axis=0),)


def _softplus(x):
    return jnp.maximum(x, 0.0) + jnp.log1p(jnp.exp(-jnp.abs(x)))


def f_pre(q, wl, w0, al, a0, gl, k_k, k_a):
    qr, qk, qv, ql = q[:, :1024], q[:, 1024:2048], q[:, 2048:3072], q[:, 3072:]
    return _f_pre(qr, qk, qv, ql, wl, w0, al, a0, gl, k_k, k_a)


def _f_pre(qr, qk, qv, ql, wl, w0, al, a0, gl, k_k, k_a):
    xw, xa, xg = ql[:, :128], ql[:, 128:256], ql[:, 256:512]
    wr = -_softplus(-(w0 + mdot(jnp.tanh(xw), wl, "nn", True))) - 0.5
    lw = -jnp.exp(wr)
    aa = jax.nn.sigmoid(a0 + mdot(xa, al, "nn", True))
    g = mdot(jax.nn.sigmoid(xg), gl, "nn", True)
    kkr = qk * k_k
    kk = kkr / jnp.maximum(jnp.sqrt(segsum(kkr * kkr)), 1e-12)
    kp = qk * (1.0 + (aa - 1.0) * k_a)
    return qr, lw, kp, qv, -kk, kk * aa, g, qr, kp, qv


def f_post(o, r, kp, v, g, lnw, lnb, rk):
    mu = segsum(o) * (1.0 / HEAD)
    d = o - mu
    gn = d * lax.rsqrt(segsum(d * d) * (1.0 / HEAD) + GN_EPS)
    return ((gn * lnw + lnb + segsum(r * kp * rk) * v) * g,)


def f_mix(ya, yb, ga, gb):
    return (jax.nn.sigmoid(ga) * ya + jax.nn.sigmoid(gb) * yb,)


def f_ffn_in(h1, g):
    return _rms(h1, g), h1


def f_final(h1, m3, tgt, g):
    y = _rms(h1 + m3, g)
    err = jnp.square(y - tgt)
    return 0.5 * jnp.sum(jnp.mean(err, axis=-1))


def _cparams(n_grid):
    return pltpu.CompilerParams(dimension_semantics=("arbitrary",) * n_grid, vmem_limit_bytes=VMEM_LIMIT)


def _tile_spec(tm, w, cb):
    return pl.BlockSpec((tm, w), lambda i: (i, cb))


def _const_spec(c):
    nd = c.ndim
    return pl.BlockSpec(c.shape, lambda i: (0,) * nd)


def ew_call(fn, tiled, consts, outs, *, tm, name):
    t = tiled[0][0].shape[0]
    n_t, n_c = len(tiled), len(consts)

    def body(*refs):
        tv = [r[...].astype(F32) for r in refs[:n_t]]
        cv = [r[...] for r in refs[n_t:n_t + n_c]]
        res = fn(*tv, *cv)
        for o_ref, val in zip(refs[n_t + n_c:], res):
            o_ref[...] = val.astype(o_ref.dtype)

    return pl.pallas_call(
        body,
        name=name,
        grid=(t // tm,),
        in_specs=[_tile_spec(tm, w, cb) for _, w, cb in tiled] + [_const_spec(c) for c in consts],
        out_specs=[_tile_spec(tm, w, 0) for w, _ in outs],
        out_shape=[jax.ShapeDtypeStruct((t, w), dt) for w, dt in outs],
        compiler_params=_cparams(1),
    )(*[a for a, _, _ in tiled], *consts)


def ew_vjp_call(fn, tiled, consts, cots, d_tiled, d_consts, *, tm, name):
    t = tiled[0][0].shape[0]
    n_t, n_c, n_g = len(tiled), len(consts), len(cots)
    dt_list = [(i, dt) for i, dts in enumerate(d_tiled) for dt in dts]
    dc_list = [i for i, want in enumerate(d_consts) if want]

    def body(*refs):
        tv = [r[...].astype(F32) for r in refs[:n_t]]
        cv = [r[...] for r in refs[n_t:n_t + n_c]]
        gv = tuple(r[...].astype(F32) for r in refs[n_t + n_c:n_t + n_c + n_g])
        out_refs = refs[n_t + n_c + n_g:]
        _, vjp = jax.vjp(fn, *tv, *cv)
        grads = vjp(gv)
        for o_ref, (i, _) in zip(out_refs, dt_list):
            o_ref[...] = grads[i].astype(o_ref.dtype)
        acc_refs = out_refs[len(dt_list):]

        @pl.when(pl.program_id(0) == 0)
        def _():
            for a_ref in acc_refs:
                a_ref[...] = jnp.zeros_like(a_ref)

        for a_ref, i in zip(acc_refs, dc_list):
            a_ref[...] += grads[n_t + i]

    res = pl.pallas_call(
        body,
        name=name,
        grid=(t // tm,),
        in_specs=[_tile_spec(tm, w, cb) for _, w, cb in tiled] + [_const_spec(c) for c in consts]
        + [_tile_spec(tm, w, cb) for _, w, cb in cots],
        out_specs=[_tile_spec(tm, tiled[i][1], 0) for i, _ in dt_list] + [_const_spec(consts[i]) for i in dc_list],
        out_shape=[jax.ShapeDtypeStruct((t, tiled[i][1]), dt) for i, dt in dt_list]
        + [jax.ShapeDtypeStruct(consts[i].shape, F32) for i in dc_list],
        compiler_params=_cparams(1),
    )(*[a for a, _, _ in tiled], *consts, *[a for a, _, _ in cots])
    return res[:len(dt_list)], res[len(dt_list):]


def mm(a, b, mode, *, tm, tn, name, out_dtypes=(F32,), epi=None, extras=(), into=None):
    m = a.shape[1] if mode == "tn" else a.shape[0]
    kd = a.shape[0] if mode == "tn" else a.shape[1]
    n = b.shape[0] if mode == "nt" else b.shape[1]
    tm, tn = min(tm, m), min(tn, n)
    if mode == "nn":
        a_spec = pl.BlockSpec((tm, kd), lambda i, j: (i, 0))
        b_spec = pl.BlockSpec((kd, tn), lambda i, j: (0, j))
    elif mode == "nt":
        a_spec = pl.BlockSpec((tm, kd), lambda i, j: (i, 0))
        b_spec = pl.BlockSpec((tn, kd), lambda i, j: (j, 0))
    else:
        a_spec = pl.BlockSpec((kd, tm), lambda i, j: (0, i))
        b_spec = pl.BlockSpec((kd, tn), lambda i, j: (0, j))
    n_e = len(extras)
    o_spec = pl.BlockSpec((tm, tn), lambda i, j: (i, j))

    if into is not None:
        buf, place = into

        def body_into(a_ref, b_ref, buf_ref, o_ref):
            o_ref[0, 0] = lax.dot_general(a_ref[...].astype(BF16), b_ref[...].astype(BF16), _MDIMS[mode],
                                          preferred_element_type=F32)

        return pl.pallas_call(
            body_into,
            name=name,
            grid=(m // tm, n // tn),
            in_specs=[a_spec, b_spec, pl.BlockSpec(memory_space=pl.ANY)],
            out_specs=pl.BlockSpec((1, 1, tm, tn), lambda i, j: (*place(i, j), 0)),
            out_shape=jax.ShapeDtypeStruct(buf.shape, F32),
            input_output_aliases={2: 0},
            compiler_params=_cparams(2),
        )(a, b, buf)

    def body(a_ref, b_ref, *refs):
        c = lax.dot_general(a_ref[...].astype(BF16), b_ref[...].astype(BF16), _MDIMS[mode],
                            preferred_element_type=F32)
        res = epi(c, *[r[...] for r in refs[:n_e]]) if epi is not None else (c,)
        for o_ref, val in zip(refs[n_e:], res):
            o_ref[...] = val.astype(o_ref.dtype)

    res = pl.pallas_call(
        body,
        name=name,
        grid=(m // tm, n // tn),
        in_specs=[a_spec, b_spec] + [o_spec] * n_e,
        out_specs=[o_spec] * len(out_dtypes),
        out_shape=[jax.ShapeDtypeStruct((m, n), dt) for dt in out_dtypes],
        compiler_params=_cparams(2),
    )(a, b, *extras)
    return res if len(out_dtypes) > 1 else res[0]


P_WIDTH = 7680
RWKV_COL0 = 4096
RWKV_WIDTH = 3584
SHIFT_BLK = 512


def _shift_down(p, prev_row):
    rows = lax.broadcasted_iota(jnp.int32, p.shape, 0)
    return jnp.where(rows == 0, prev_row, pltpu.roll(p, 1, 0))


def shiftmix_fwd(p_all, sbp, *, tm):
    t = p_all.shape[0]
    tm = min(tm, t)
    c0 = RWKV_COL0 // SHIFT_BLK
    hb = tm // 8

    def body(p_ref, halo_ref, sb_ref, q_ref):
        p = p_ref[...]
        prev = jnp.where(pl.program_id(0) == 0, 0.0, halo_ref[7:8, :])
        q_ref[...] = p * sb_ref[0:1, :] + _shift_down(p, prev) * sb_ref[1:2, :]

    return pl.pallas_call(
        body,
        name="shiftmix_fwd",
        grid=(t // tm, RWKV_WIDTH // SHIFT_BLK),
        in_specs=[
            pl.BlockSpec((tm, SHIFT_BLK), lambda i, j: (i, c0 + j)),
            pl.BlockSpec((8, SHIFT_BLK), lambda i, j: (jnp.maximum(i * hb - 1, 0), c0 + j)),
            pl.BlockSpec((2, SHIFT_BLK), lambda i, j: (0, j)),
        ],
        out_specs=pl.BlockSpec((tm, SHIFT_BLK), lambda i, j: (i, j)),
        out_shape=jax.ShapeDtypeStruct((t, RWKV_WIDTH), F32),
        compiler_params=_cparams(2),
    )(p_all, p_all, sbp)


def shiftmix_bwd(dq, col0, p_all, sbp, *, tm, name):
    t, w = dq.shape
    n_i = t // tm
    hb = tm // 8
    cq = col0 // SHIFT_BLK
    cp = (RWKV_COL0 + col0) // SHIFT_BLK

    def body(dq_ref, dqn_ref, p_ref, ph_ref, sb_ref, dp_ref, dsb_ref):
        i = pl.program_id(1)
        dq_t = dq_ref[...]
        rows = lax.broadcasted_iota(jnp.int32, dq_t.shape, 0)
        nxt = jnp.where(i == n_i - 1, 0.0, dqn_ref[0:1, :])
        up = jnp.where(rows == tm - 1, nxt, pltpu.roll(dq_t, tm - 1, 0))
        dp_ref[...] = (dq_t * sb_ref[0:1, :] + up * sb_ref[1:2, :]).astype(dp_ref.dtype)
        p = p_ref[...]
        prev = jnp.where(i == 0, 0.0, ph_ref[7:8, :])
        s0 = jnp.sum(dq_t * p, axis=0, keepdims=True)
        s1 = jnp.sum(dq_t * _shift_down(p, prev), axis=0, keepdims=True)
        two = lax.broadcasted_iota(jnp.int32, (2, SHIFT_BLK), 0)

        @pl.when(i == 0)
        def _():
            dsb_ref[...] = jnp.zeros_like(dsb_ref)

        dsb_ref[...] += jnp.where(two == 0, s0, s1)

    return pl.pallas_call(
        body,
        name=name,
        grid=(w // SHIFT_BLK, n_i),
        in_specs=[
            pl.BlockSpec((tm, SHIFT_BLK), lambda j, i: (i, j)),
            pl.BlockSpec((8, SHIFT_BLK), lambda j, i: (jnp.minimum((i + 1) * hb, t // 8 - 1), j)),
            pl.BlockSpec((tm, SHIFT_BLK), lambda j, i: (i, cp + j)),
            pl.BlockSpec((8, SHIFT_BLK), lambda j, i: (jnp.maximum(i * hb - 1, 0), cp + j)),
            pl.BlockSpec((2, SHIFT_BLK), lambda j, i: (0, cq + j)),
        ],
        out_specs=[
            pl.BlockSpec((tm, SHIFT_BLK), lambda j, i: (i, j)),
            pl.BlockSpec((2, SHIFT_BLK), lambda j, i: (0, j)),
        ],
        out_shape=[jax.ShapeDtypeStruct((t, w), BF16), jax.ShapeDtypeStruct((2, w), F32)],
        compiler_params=_cparams(2),
    )(dq, dq, p_all, p_all, sbp)


def final_call(h1, m3, tgt, g_final, *, tm):
    t = h1.shape[0]

    def body(h1_ref, m3_ref, tgt_ref, g_ref, dh_ref, dhb_ref, dg_ref, loss_ref):
        loss, vjp = jax.vjp(f_final, h1_ref[...], m3_ref[...], tgt_ref[...], g_ref[...])
        dh, _, _, dg = vjp(jnp.ones((), F32))
        dh_ref[...] = dh
        dhb_ref[...] = dh.astype(BF16)

        @pl.when(pl.program_id(0) == 0)
        def _():
            dg_ref[...] = jnp.zeros_like(dg_ref)
            loss_ref[...] = jnp.zeros_like(loss_ref)

        dg_ref[...] += dg
        loss_ref[...] += jnp.full(loss_ref.shape, loss, F32)

    tile = _tile_spec(tm, D_MODEL, 0)
    return pl.pallas_call(
        body,
        name="final_loss",
        grid=(t // tm,),
        in_specs=[tile, tile, tile, _const_spec(g_final)],
        out_specs=[tile, tile, _const_spec(g_final), pl.BlockSpec((8, 128), lambda i: (0, 0))],
        out_shape=[jax.ShapeDtypeStruct((t, D_MODEL), F32), jax.ShapeDtypeStruct((t, D_MODEL), BF16),
                   jax.ShapeDtypeStruct(g_final.shape, F32), jax.ShapeDtypeStruct((8, 128), F32)],
        compiler_params=_cparams(1),
    )(h1, m3, tgt, g_final)


N_SGU = 2048
N_RWKV = 3360
LORA_W, LORA_A, LORA_G = 64, 64, 160


def _pad_rwkv_cols(z):
    zero = lambda n: jnp.zeros(z.shape[:-1] + (n,), z.dtype)
    return jnp.concatenate([z[..., :3072], z[..., 3072:3136], zero(64), z[..., 3136:3200], zero(64),
                            z[..., 3200:3360], zero(96)], axis=-1)


def _unpad_rwkv_cols(z):
    return jnp.concatenate([z[..., :3072], z[..., 3072:3136], z[..., 3200:3264], z[..., 3328:3488]], axis=-1)


def _pad_win_rows(wt):
    z = wt[N_SGU:N_SGU + N_RWKV]
    zero = lambda n: jnp.zeros((n, wt.shape[1]), wt.dtype)
    return jnp.concatenate([wt[:N_SGU], wt[N_SGU + N_RWKV:], z[:3072], z[3072:3136], zero(64), z[3136:3200], zero(64),
                            z[3200:3360], zero(96)], axis=0)


def _unpad_win_rows(wt):
    z = wt[RWKV_COL0:]
    return jnp.concatenate([wt[:N_SGU], z[:3072], z[3072:3136], z[3200:3264], z[3328:3488], wt[N_SGU:RWKV_COL0]],
                           axis=0)


def _pad_rows(w, n):
    return jnp.concatenate([w, jnp.zeros((n - w.shape[0],) + w.shape[1:], w.dtype)], axis=0)


def _relu2_epi(c):
    return c, jnp.square(jnp.maximum(c, 0.0))


def _relu2_bwd_epi(c, hid):
    return (c * (2.0 * jnp.maximum(hid.astype(F32), 0.0)),)


def _add_epi(c, x):
    return (c + x,)


def _pre_fwd(*args):
    res = f_pre(*args)
    return res[1], res[2], res[4], res[5], res[6]


def local_step(x, tgt, w, late_token, late_weights, pair_start, pair_finish, pack_early):
    d = D_MODEL
    win_pt = _pad_win_rows(w["w_in"])
    sbp = _pad_rwkv_cols(w["shift_b"])
    wl = _pad_rows(w["w_lora_w"], 128)
    al = _pad_rows(w["a_lora_w"], 128)
    gl = _pad_rows(w["g_lora_w"], 256)
    sbt = w["sgu_b"].T

    (a_bf,) = ew_call(lambda x_, g_: (f_norm_in(x_, g_)[0],), [(x, d, 0)], [w["g_mix"] + late_token[:1, :1]],
                      [(d, BF16)], tm=256, name="norm_in")
    p_all = mm(a_bf, win_pt, "nt", tm=2048, tn=640, name="mm_in")
    sgu_t = [(p_all, 2 * d, 0)]
    sgu_c = [w["sgu_ln_w"], w["sgu_ln_b"], w["sgu_w"], sbt]
    (s_bf,) = ew_call(f_sgu, sgu_t, sgu_c, [(d, BF16)], tm=256, name="sgu_fwd")
    ya = mm(s_bf, w["w_proj_a"], "nn", tm=512, tn=1024, name="mm_proj_a")
    q = shiftmix_fwd(p_all, sbp, tm=1024)
    pre_t = [(q, RWKV_WIDTH, 0)]
    pre_c = [wl, w["w0"], al, w["a0"], gl, w["k_k"], w["k_a"]]
    lw, kp, na, nb, g = ew_call(_pre_fwd, pre_t, pre_c, [(d, F32)] * 5, tm=256, name="rwkv_pre_fwd")
    scan_ops = [(q, 0), (lw, 0), (kp, 0), (q, 2), (na, 0), (nb, 0)]
    o, s0s = scan_fwd(scan_ops)
    w = {**w, **late_weights(o)}
    post_t = [(o, d, 0), (q, d, 0), (kp, d, 0), (q, d, 2), (g, d, 0)]
    post_c = [w["ln_x_w"], w["ln_x_b"], w["r_k"]]
    (ob_bf,) = ew_call(f_post, post_t, post_c, [(d, BF16)], tm=256, name="rwkv_post_fwd")
    yb = mm(ob_bf, w["w_proj_b"], "nn", tm=512, tn=1024, name="mm_proj_b")
    mix_t = [(ya, d, 0), (yb, d, 0), (p_all, d, 2), (p_all, d, 3)]
    (mixed_bf,) = ew_call(f_mix, mix_t, [], [(d, BF16)], tm=256, name="mix_fwd")
    h1 = mm(mixed_bf, w["w_out"], "nn", tm=512, tn=1024, name="mm_out", epi=_add_epi, extras=(x,))
    (f_bf,) = ew_call(lambda h_, g_: (f_ffn_in(h_, g_)[0],), [(h1, d, 0)], [w["g_ffn"]], [(d, BF16)], tm=256,
                      name="ffn_norm")
    hid, act_bf = mm(f_bf, w["w_ffn1"], "nn", tm=2048, tn=1024, name="mm_ffn1", out_dtypes=(BF16, BF16), epi=_relu2_epi)
    m3 = mm(act_bf, w["w_ffn2"], "nn", tm=1024, tn=512, name="mm_ffn2")
    dh2, dh2_bf, dg_final, loss = final_call(h1, m3, tgt, w["g_final"], tm=256)

    dhid_bf = mm(dh2_bf, w["w_ffn2"], "nt", tm=2048, tn=1024, name="mm_dact", out_dtypes=(BF16,), epi=_relu2_bwd_epi,
                 extras=(hid,))
    late_g = lax.empty((N_CHIPS, 2, PACK_ROWS, HALF_W), F32)
    late_g = mm(act_bf, dh2_bf, "tn", tm=512, tn=HALF_W, name="mm_dw_ffn2",
                into=(late_g, lambda i, j: (i // 2, j, PIECE_OFF["w_ffn2"] // 512 + i % 2)))
    df = mm(dhid_bf, w["w_ffn1"], "nt", tm=1024, tn=512, name="mm_df")
    late_g = mm(f_bf, dhid_bf, "tn", tm=512, tn=HALF_W, name="mm_dw_ffn1",
                into=(late_g, lambda i, j: (j // 2, j % 2, PIECE_OFF["w_ffn1"] // 512 + i)))
    (dh1, dh1_bf), (dg_ffn,) = ew_vjp_call(f_ffn_in, [(h1, d, 0)], [w["g_ffn"]], [(df, d, 0), (dh2, d, 0)],
                                           [(F32, BF16)], [True], tm=256, name="ffn_norm_bwd")
    dmixed = mm(dh1_bf, w["w_out"], "nt", tm=512, tn=1024, name="mm_dmixed")
    late_g = mm(mixed_bf, dh1_bf, "tn", tm=256, tn=HALF_W, name="mm_dw_out",
                into=(late_g, lambda i, j: (i, j, PIECE_OFF["w_out"] // 256)))
    (dya_bf, dyb_bf, dga_bf, dgb_bf), _ = ew_vjp_call(f_mix, mix_t, [], [(dmixed, d, 0)], [(BF16,)] * 4, [], tm=256,
                                                      name="mix_bwd")
    dob = mm(dyb_bf, w["w_proj_b"], "nt", tm=512, tn=1024, name="mm_dob")
    late_g = mm(ob_bf, dyb_bf, "tn", tm=256, tn=HALF_W, name="mm_dw_proj_b",
                into=(late_g, lambda i, j: (i, j, PIECE_OFF["w_proj_b"] // 256)))
    late_state, late_token = pair_start(late_g, "late")
    post_c_after = [w["ln_x_w"] + late_token[:1, :1]] + post_c[1:]
    (do, dr_p, dkp_p, dv_p, dg), (dlnx_w, dlnx_b, dr_k) = ew_vjp_call(
        f_post, post_t, post_c_after, [(dob, d, 0)], [(F32,)] * 5, [True] * 3, tm=256, name="rwkv_post_bwd")
    late_part, late_part16 = pair_finish(late_state, do, "late")
    *scan_g, late_slots = scan_bwd(scan_ops, s0s, do, late_part16)
    pre_g = [(z, d, 0) for z in scan_g] + [(dg, d, 0), (dr_p, d, 0), (dkp_p, d, 0), (dv_p, d, 0)]
    (dq,), (dwl, dw0, dal, da0, dgl, dk_k, dk_a) = ew_vjp_call(
        f_pre, pre_t, pre_c, pre_g, [(F32,)], [True] * 7, tm=128, name="rwkv_pre_bwd")
    dp_rwkv, dsb = shiftmix_bwd(dq, 0, p_all, sbp, tm=512, name="shiftmix_bwd")
    ds = mm(dya_bf, w["w_proj_a"], "nt", tm=512, tn=1024, name="mm_ds")
    d_proj_a = mm(s_bf, dya_bf, "tn", tm=512, tn=1024, name="mm_dw_proj_a")
    (dp_sgu,), (dln_w, dln_b, dsw, dsbt) = ew_vjp_call(f_sgu, sgu_t, sgu_c, [(ds, d, 0)], [(BF16,)], [True] * 4,
                                                       tm=256, name="sgu_bwd")
    dp_all = jnp.concatenate([dp_sgu, dga_bf, dgb_bf, dp_rwkv], axis=1)
    d_in_pt = mm(dp_all, a_bf, "tn", tm=1280, tn=1024, name="mm_dw_in")
    early_state, early_token = pair_start(pack_early({
        "w_in": _unpad_win_rows(d_in_pt), "w_proj_a": d_proj_a, "w_lora_w": dwl[:LORA_W], "a_lora_w": dal[:LORA_A],
        "g_lora_w": dgl[:LORA_G]}), "early")
    da = mm(dp_all, win_pt, "nn", tm=1024, tn=256, name="mm_da")
    g_mix_after = w["g_mix"] + early_token[:1, :1]
    (grad_x,), (dg_mix,) = ew_vjp_call(f_norm_in, [(x, d, 0)], [g_mix_after], [(da, d, 0), (dh1, d, 0)], [(F32,)],
                                       [True], tm=256, name="norm_in_bwd")

    grads = {
        "g_mix": dg_mix, "sgu_ln_w": dln_w, "sgu_ln_b": dln_b, "sgu_w": dsw, "sgu_b": dsbt.T,
        "shift_b": _unpad_rwkv_cols(dsb),
        "w0": dw0, "a0": da0, "k_k": dk_k, "k_a": dk_a, "r_k": dr_k, "ln_x_w": dlnx_w, "ln_x_b": dlnx_b,
        "g_ffn": dg_ffn, "g_final": dg_final,
    }
    return loss[0, 0], grad_x, grads, (late_part, late_slots), early_state


MESH = pl.DeviceIdType.MESH
N_CHIPS = 4
N_DEV = 8
PACK_ROWS = 2560
PACK_TILE = 512
SMALL_ROWS = 152
_ANY = pl.BlockSpec(memory_space=pl.ANY)


def _coords():
    return lax.axis_index("x"), lax.axis_index("y"), lax.axis_index("c")


def _other_chips(x, y):
    return [(1 - x, y), (x, 1 - y), (1 - x, 1 - y)]


def _remote(src, dst, send_sems, recv_sems, k, to):
    return pltpu.make_async_remote_copy(src_ref=src, dst_ref=dst, send_sem=send_sems.at[k], recv_sem=recv_sems.at[k],
                                        device_id=to, device_id_type=MESH)


def gather_shards(pack):
    def body(src_ref, out_ref, token, send_sems, recv_sems):
        x, y, c = _coords()
        me = 2 * x + y
        sib = (x, y, 1 - c)
        chips = _other_chips(x, y)
        first = [_remote(src_ref.at[c], out_ref.at[me, c], send_sems, recv_sems, k, (cx, cy, c))
                 for k, (cx, cy) in enumerate(chips)]
        for cp in first:
            cp.start()
        passed = []
        for k, (cx, cy) in enumerate(chips):
            j = 2 * cx + cy
            _remote(src_ref.at[c], out_ref.at[j, c], send_sems, recv_sems, k, (cx, cy, c)).wait_recv()
            fwd = _remote(out_ref.at[j, c], out_ref.at[j, c], send_sems, recv_sems, 3 + k, sib)
            fwd.start()
            passed.append(fwd)
        for k, (cx, cy) in enumerate(chips):
            j = 2 * cx + cy
            _remote(out_ref.at[j, 1 - c], out_ref.at[j, 1 - c], send_sems, recv_sems, 3 + k, sib).wait_recv()
        for cp in first + passed:
            cp.wait_send()
        token[...] = jnp.zeros_like(token)

    return pl.pallas_call(
        body,
        name="gather_shards",
        in_specs=[_ANY],
        out_specs=[_ANY, pl.BlockSpec(memory_space=pltpu.VMEM)],
        out_shape=[jax.ShapeDtypeStruct((N_CHIPS,) + pack.shape, pack.dtype), jax.ShapeDtypeStruct((8, 128), F32)],
        scratch_shapes=[pltpu.SemaphoreType.DMA((6,)), pltpu.SemaphoreType.DMA((6,))],
    )(pack)


def _gather_copies(pack_ref, all_ref, send_sems, recv_sems):
    x, y, c = _coords()
    me = 2 * x + y
    return [(_remote(pack_ref.at[c], all_ref.at[me, c], send_sems, recv_sems, k, (cx, cy, c)),
             _remote(pack_ref.at[c], all_ref.at[2 * cx + cy, c], send_sems, recv_sems, k, (cx, cy, c)))
            for k, (cx, cy) in enumerate(_other_chips(x, y))]


def gather_start(pack, after):
    hbm = pl.BlockSpec(memory_space=pltpu.HBM)
    sem = pl.BlockSpec(memory_space=pltpu.SEMAPHORE)
    all_shape = (N_CHIPS,) + pack.shape

    def body(pack_ref, all_ref, after_ref, send_sems, recv_sems, pack_thru, all_thru, token):
        for send, _ in _gather_copies(pack_ref, all_ref, send_sems, recv_sems):
            send.start()
        token[...] = jnp.zeros_like(token)

    return pl.pallas_call(
        body,
        name="gather_start",
        out_shape=(pltpu.SemaphoreType.DMA((3,)), pltpu.SemaphoreType.DMA((3,)), pltpu.HBM(pack.shape, pack.dtype),
                   pltpu.HBM(all_shape, pack.dtype), jax.ShapeDtypeStruct((8, 128), F32)),
        in_specs=(hbm, hbm, pl.BlockSpec(memory_space=pl.ANY)),
        out_specs=(sem, sem, hbm, hbm, pl.BlockSpec(memory_space=pltpu.VMEM)),
        input_output_aliases={0: 2, 1: 3},
        compiler_params=pltpu.CompilerParams(has_side_effects=pltpu.SideEffectType.DATAFLOW_SIDE_EFFECTING),
    )(pltpu.with_memory_space_constraint(pack, pltpu.HBM),
      pltpu.with_memory_space_constraint(lax.empty(all_shape, pack.dtype), pltpu.HBM), after)


def gather_wait(send_sems, recv_sems, pack_thru, all_thru, after):
    hbm = pl.BlockSpec(memory_space=pltpu.HBM)
    sem = pl.BlockSpec(memory_space=pltpu.SEMAPHORE)

    def body(pack_ref, all_ref, send_sems, recv_sems, after_ref, pack_out, all_out):
        for send, arrival in _gather_copies(pack_ref, all_ref, send_sems, recv_sems):
            send.wait_send()
            arrival.wait_recv()

    return pl.pallas_call(
        body,
        name="gather_wait",
        out_shape=(pltpu.HBM(pack_thru.shape, pack_thru.dtype), pltpu.HBM(all_thru.shape, all_thru.dtype)),
        in_specs=(hbm, hbm, sem, sem, pl.BlockSpec(memory_space=pl.ANY)),
        out_specs=(hbm, hbm),
        input_output_aliases={0: 0, 1: 1},
        compiler_params=pltpu.CompilerParams(has_side_effects=pltpu.SideEffectType.DATAFLOW_SIDE_EFFECTING),
    )(pack_thru, all_thru, send_sems, recv_sems, after)[1]


def gather_forward(got):
    def body(got_ref, out_ref, send_sems, recv_sems):
        x, y, c = _coords()
        sib = (x, y, 1 - c)
        slots = [2 * cx + cy for cx, cy in _other_chips(x, y)]
        sends = [_remote(got_ref.at[j, c], out_ref.at[j, c], send_sems, recv_sems, k, sib) for k, j in enumerate(slots)]
        for cp in sends:
            cp.start()
        for k, j in enumerate(slots):
            _remote(got_ref.at[j, 1 - c], out_ref.at[j, 1 - c], send_sems, recv_sems, k, sib).wait_recv()
        for cp in sends:
            cp.wait_send()

    return pl.pallas_call(
        body,
        name="gather_forward",
        in_specs=[_ANY],
        out_specs=_ANY,
        out_shape=jax.ShapeDtypeStruct(got.shape, got.dtype),
        input_output_aliases={0: 0},
        scratch_shapes=[pltpu.SemaphoreType.DMA((3,)), pltpu.SemaphoreType.DMA((3,))],
    )(got)


def reduce_pair(g, tag):
    def body(g_ref, got_ref, send_sems, recv_sems):
        x, y, c = _coords()
        sib = (x, y, 1 - c)
        sends = [_remote(g_ref.at[j, 1 - c], got_ref.at[j], send_sems, recv_sems, j, sib) for j in range(N_CHIPS)]
        for cp in sends:
            cp.start()
        for cp in sends:
            cp.wait_recv()
        for cp in sends:
            cp.wait_send()

    return pl.pallas_call(
        body,
        name="reduce_pair_" + tag,
        in_specs=[_ANY],
        out_specs=_ANY,
        out_shape=jax.ShapeDtypeStruct((N_CHIPS,) + g.shape[2:], g.dtype),
        scratch_shapes=[pltpu.SemaphoreType.DMA((N_CHIPS,)), pltpu.SemaphoreType.DMA((N_CHIPS,))],
    )(g)


def pair_sum(g, got, tag, *, tm):
    n, _, rows, width = g.shape

    def body(g0_ref, g1_ref, got_ref, out_ref, out16_ref):
        own = jnp.where(lax.axis_index("c") == 0, g0_ref[0, 0], g1_ref[0, 0])
        total = own + got_ref[0]
        out_ref[0] = total
        out16_ref[0] = total.astype(BF16)

    blk = pl.BlockSpec((1, tm, width), lambda j, i: (j, i, 0))
    return pl.pallas_call(
        body,
        name="pair_sum_" + tag,
        grid=(n, rows // tm),
        in_specs=[pl.BlockSpec((1, 1, tm, width), lambda j, i: (j, 0, i, 0)),
                  pl.BlockSpec((1, 1, tm, width), lambda j, i: (j, 1, i, 0)), blk],
        out_specs=[blk, blk],
        out_shape=[jax.ShapeDtypeStruct(got.shape, F32), jax.ShapeDtypeStruct(got.shape, BF16)],
        compiler_params=_cparams(2),
    )(g, g, got)


def reduce_chips(p):
    def body(p_ref, out_ref, send_sems, recv_sems):
        x, y, c = _coords()
        me = 2 * x + y
        chips = _other_chips(x, y)
        sends = [_remote(p_ref.at[2 * cx + cy], out_ref.at[me], send_sems, recv_sems, k, (cx, cy, c))
                 for k, (cx, cy) in enumerate(chips)]
        for cp in sends:
            cp.start()
        for k, (cx, cy) in enumerate(chips):
            _remote(p_ref.at[me], out_ref.at[2 * cx + cy], send_sems, recv_sems, k, (cx, cy, c)).wait_recv()
        for cp in sends:
            cp.wait_send()

    return pl.pallas_call(
        body,
        name="reduce_chips",
        in_specs=[_ANY],
        out_specs=_ANY,
        out_shape=jax.ShapeDtypeStruct(p.shape, p.dtype),
        scratch_shapes=[pltpu.SemaphoreType.DMA((3,)), pltpu.SemaphoreType.DMA((3,))],
    )(p)


def _pair_copies(g_ref, got_ref, send_sems, recv_sems):
    x, y, c = _coords()
    return [_remote(g_ref.at[j, 1 - c], got_ref.at[j], send_sems, recv_sems, j, (x, y, 1 - c)) for j in range(N_CHIPS)]


def reduce_pair_start(g, tag):
    hbm = pl.BlockSpec(memory_space=pltpu.HBM)
    sem = pl.BlockSpec(memory_space=pltpu.SEMAPHORE)
    got_shape = (N_CHIPS,) + g.shape[2:]

    def body(g_ref, got_ref, send_sems, recv_sems, g_thru, got_thru, token):
        for cp in _pair_copies(g_ref, got_ref, send_sems, recv_sems):
            cp.start()
        token[...] = jnp.zeros_like(token)

    return pl.pallas_call(
        body,
        name="reduce_pair_start_" + tag,
        out_shape=(pltpu.SemaphoreType.DMA((N_CHIPS,)), pltpu.SemaphoreType.DMA((N_CHIPS,)),
                   pltpu.HBM(g.shape, g.dtype), pltpu.HBM(got_shape, g.dtype), jax.ShapeDtypeStruct((8, 128), F32)),
        in_specs=(hbm, hbm),
        out_specs=(sem, sem, hbm, hbm, pl.BlockSpec(memory_space=pltpu.VMEM)),
        input_output_aliases={0: 2, 1: 3},
        compiler_params=pltpu.CompilerParams(has_side_effects=pltpu.SideEffectType.DATAFLOW_SIDE_EFFECTING),
    )(pltpu.with_memory_space_constraint(g, pltpu.HBM),
      pltpu.with_memory_space_constraint(lax.empty(got_shape, g.dtype), pltpu.HBM))


def reduce_pair_wait(send_sems, recv_sems, g_thru, got_thru, after, tag):
    hbm = pl.BlockSpec(memory_space=pltpu.HBM)
    sem = pl.BlockSpec(memory_space=pltpu.SEMAPHORE)

    def body(g_ref, got_ref, send_sems, recv_sems, after_ref, g_out, got_out):
        for cp in _pair_copies(g_ref, got_ref, send_sems, recv_sems):
            cp.wait_send()
            cp.wait_recv()

    return pl.pallas_call(
        body,
        name="reduce_pair_wait_" + tag,
        out_shape=(pltpu.HBM(g_thru.shape, g_thru.dtype), pltpu.HBM(got_thru.shape, got_thru.dtype)),
        in_specs=(hbm, hbm, sem, sem, pl.BlockSpec(memory_space=pl.ANY)),
        out_specs=(hbm, hbm),
        input_output_aliases={0: 0, 1: 1},
        compiler_params=pltpu.CompilerParams(has_side_effects=pltpu.SideEffectType.DATAFLOW_SIDE_EFFECTING),
    )(g_thru, got_thru, send_sems, recv_sems, after)


def _chip_copies(p_ref, slots_ref, send_sems, recv_sems):
    x, y, c = _coords()
    me = 2 * x + y
    return [(_remote(p_ref.at[2 * cx + cy], slots_ref.at[me], send_sems, recv_sems, k, (cx, cy, c)),
             _remote(p_ref.at[me], slots_ref.at[2 * cx + cy], send_sems, recv_sems, k, (cx, cy, c)))
            for k, (cx, cy) in enumerate(_other_chips(x, y))]


def reduce_chips_start(p):
    hbm = pl.BlockSpec(memory_space=pltpu.HBM)
    sem = pl.BlockSpec(memory_space=pltpu.SEMAPHORE)

    def body(p_ref, slots_ref, send_sems, recv_sems, p_thru, slots_thru, token):
        for send, _ in _chip_copies(p_ref, slots_ref, send_sems, recv_sems):
            send.start()
        token[...] = jnp.zeros_like(token)

    return pl.pallas_call(
        body,
        name="reduce_chips_start",
        out_shape=(pltpu.SemaphoreType.DMA((3,)), pltpu.SemaphoreType.DMA((3,)), pltpu.HBM(p.shape, p.dtype),
                   pltpu.HBM(p.shape, p.dtype), jax.ShapeDtypeStruct((8, 128), F32)),
        in_specs=(hbm, hbm),
        out_specs=(sem, sem, hbm, hbm, pl.BlockSpec(memory_space=pltpu.VMEM)),
        input_output_aliases={0: 2, 1: 3},
        compiler_params=pltpu.CompilerParams(has_side_effects=pltpu.SideEffectType.DATAFLOW_SIDE_EFFECTING),
    )(pltpu.with_memory_space_constraint(p, pltpu.HBM),
      pltpu.with_memory_space_constraint(lax.empty(p.shape, p.dtype), pltpu.HBM))


def reduce_chips_wait(send_sems, recv_sems, p_thru, slots_thru, after):
    hbm = pl.BlockSpec(memory_space=pltpu.HBM)
    sem = pl.BlockSpec(memory_space=pltpu.SEMAPHORE)

    def body(p_ref, slots_ref, send_sems, recv_sems, after_ref, p_dead, slots_out):
        for send, arrival in _chip_copies(p_ref, slots_ref, send_sems, recv_sems):
            send.wait_send()
            arrival.wait_recv()

    return pl.pallas_call(
        body,
        name="reduce_chips_wait",
        out_shape=(pltpu.HBM(p_thru.shape, p_thru.dtype), pltpu.HBM(slots_thru.shape, slots_thru.dtype)),
        in_specs=(hbm, hbm, sem, sem, pl.BlockSpec(memory_space=pl.ANY)),
        out_specs=(hbm, hbm),
        input_output_aliases={0: 0, 1: 1},
        compiler_params=pltpu.CompilerParams(has_side_effects=pltpu.SideEffectType.DATAFLOW_SIDE_EFFECTING),
    )(p_thru, slots_thru, send_sems, recv_sems, after)[1]


def sum_with_own(own, slots, index_fn, after, *, tm, name):
    n, rows, width = slots.shape

    def body(*refs):
        mine = index_fn()
        acc = None
        for s in range(n):
            term = jnp.where(mine == s, refs[s][0], refs[n + s][0].astype(F32))
            acc = term if acc is None else acc + term
        refs[-1][...] = acc

    slot_specs = [pl.BlockSpec((1, tm, width), lambda i, s=s: (s, i, 0)) for s in range(n)]
    return pl.pallas_call(
        body,
        name=name,
        grid=(rows // tm,),
        in_specs=slot_specs + slot_specs + [pl.BlockSpec(after.shape, lambda i: (0,) * after.ndim)],
        out_specs=pl.BlockSpec((tm, width), lambda i: (i, 0)),
        out_shape=jax.ShapeDtypeStruct((rows, width), F32),
        compiler_params=_cparams(1),
    )(*([own] * n), *([slots] * n), after)


def exchange_halves(s, tag):
    rq = PACK_TILE
    nq = s.shape[0] // rq

    def body(s_ref, out_ref, sbuf, rbuf, send_sems, recv_sems, in_sems, out_sems):
        x, y, c = _coords()
        sib = (x, y, 1 - c)
        rows = lambda q: pl.ds(q * rq, rq)
        loads = [pltpu.make_async_copy(s_ref.at[rows(q)], sbuf.at[rows(q)], in_sems.at[q]) for q in range(nq)]
        for cp in loads:
            cp.start()
        sends = []
        for q in range(nq):
            loads[q].wait()
            sends.append(_remote(sbuf.at[rows(q)], rbuf.at[rows(q)], send_sems, recv_sems, q, sib))
            sends[q].start()
        stores = []
        for q in range(nq):
            sends[q].wait_recv()
            stores.append(pltpu.make_async_copy(rbuf.at[rows(q)], out_ref.at[rows(q)], out_sems.at[q]))
            stores[q].start()
        for cp in sends:
            cp.wait_send()
        for cp in stores:
            cp.wait()

    return pl.pallas_call(
        body,
        name="exchange_halves_" + tag,
        in_specs=[_ANY],
        out_specs=_ANY,
        out_shape=jax.ShapeDtypeStruct(s.shape, s.dtype),
        scratch_shapes=[pltpu.VMEM(s.shape, s.dtype), pltpu.VMEM(s.shape, s.dtype)]
        + [pltpu.SemaphoreType.DMA((nq,))] * 4,
        compiler_params=pltpu.CompilerParams(vmem_limit_bytes=VMEM_LIMIT),
    )(s)


def sum_all(s, after):
    def body(s_ref, after_ref, out_ref, slots, mine, theirs, send_sems, recv_sems):
        x, y, c = _coords()
        me = 2 * x + y
        chips = _other_chips(x, y)
        sends = [_remote(s_ref, slots.at[me], send_sems, recv_sems, k, (cx, cy, c)) for k, (cx, cy) in enumerate(chips)]
        for cp in sends:
            cp.start()
        for k, (cx, cy) in enumerate(chips):
            _remote(s_ref, slots.at[2 * cx + cy], send_sems, recv_sems, k, (cx, cy, c)).wait_recv()
        slots[me] = s_ref[...]
        acc = ((slots[0] + slots[1]) + slots[2]) + slots[3]
        mine[...] = acc
        swap = _remote(mine, theirs, send_sems, recv_sems, 3, (x, y, 1 - c))
        swap.start()
        swap.wait_recv()
        out_ref[...] = acc + theirs[...]
        swap.wait_send()
        for cp in sends:
            cp.wait_send()

    vmem = pl.BlockSpec(memory_space=pltpu.VMEM)
    return pl.pallas_call(
        body,
        name="sum_all",
        in_specs=[vmem, vmem],
        out_specs=vmem,
        out_shape=jax.ShapeDtypeStruct(s.shape, s.dtype),
        scratch_shapes=[pltpu.VMEM((N_CHIPS,) + s.shape, s.dtype), pltpu.VMEM(s.shape, s.dtype),
                        pltpu.VMEM(s.shape, s.dtype), pltpu.SemaphoreType.DMA((4,)), pltpu.SemaphoreType.DMA((4,))],
        compiler_params=pltpu.CompilerParams(vmem_limit_bytes=VMEM_LIMIT),
    )(s, after)


ADAM_LR = 0.001
ADAM_B1 = 0.9
ADAM_B2 = 0.999
ADAM_EPS = 1e-08
ADAM_WD = 0.01
ADAM_STEP = 10


def f_adamw(g, w, m, v):
    m = ADAM_B1 * m + (1.0 - ADAM_B1) * g
    v = ADAM_B2 * v + (1.0 - ADAM_B2) * jnp.square(g)
    m_hat = m / (1.0 - ADAM_B1 ** ADAM_STEP)
    v_hat = v / (1.0 - ADAM_B2 ** ADAM_STEP)
    delta = -ADAM_LR * (m_hat / (jnp.sqrt(v_hat) + ADAM_EPS) + ADAM_WD * w)
    return delta, m, v


def adamw_call(g, w, m, v, *, tm, name):
    width = g.shape[1]
    return ew_call(f_adamw, [(g, width, 0), (w, width, 0), (m, width, 0), (v, width, 0)], [], [(width, F32)] * 3,
                   tm=tm, name=name)


def adamw_halves(g_own, g_other, w, m, v, *, tm):
    _, rows, width = w.shape

    def body(go_ref, gx_ref, w_ref, m_ref, v_ref, g_ref, d_ref, nm_ref, nv_ref):
        g = jnp.where(pl.program_id(0) == lax.axis_index("c"), go_ref[...], gx_ref[...])
        delta, nm, nv = f_adamw(g, w_ref[0], m_ref[0], v_ref[0])
        g_ref[0] = g
        d_ref[0] = delta
        nm_ref[0] = nm
        nv_ref[0] = nv

    half = pl.BlockSpec((tm, width), lambda h, i: (i, 0))
    full = pl.BlockSpec((1, tm, width), lambda h, i: (h, i, 0))
    return pl.pallas_call(
        body,
        name="adamw_sharded",
        grid=(2, rows // tm),
        in_specs=[half, half, full, full, full],
        out_specs=[full] * 4,
        out_shape=[jax.ShapeDtypeStruct(w.shape, F32)] * 4,
        compiler_params=_cparams(2),
    )(g_own, g_other, w, m, v)


EARLY = ["w_in", "w_proj_a", "w_lora_w", "a_lora_w", "g_lora_w"]
LATE = ["w_ffn1", "w_ffn2", "w_proj_b", "w_out"]
SHARDED = EARLY + LATE
LORAS = ["w_lora_w", "a_lora_w", "g_lora_w"]
HALF_W = 512
PIECE_ROWS = {"w_in": 1864, "w_ffn1": 1024, "w_ffn2": 1024, "w_proj_a": 256, "w_proj_b": 256, "w_out": 256,
              "w_lora_w": 32, "a_lora_w": 32, "g_lora_w": 80}
PIECE_OFF = {"w_in": 0, "w_proj_a": 1920, "w_lora_w": 2176, "a_lora_w": 2208, "g_lora_w": 2240,
             "w_ffn1": 0, "w_ffn2": 1024, "w_proj_b": 2048, "w_out": 2304}
LO_OFF = 2320
SHARD_AXIS = {"w_in": 1, "w_proj_a": 0, "w_lora_w": 1, "a_lora_w": 1, "g_lora_w": 1, "w_proj_b": 0, "w_out": 0,
              "w_ffn1": 1, "w_ffn2": 0}
SHARD_SHAPE = {"w_in": (1024, 1864), "w_proj_a": (256, 1024), "w_lora_w": (64, 256), "a_lora_w": (64, 256),
               "g_lora_w": (160, 256), "w_proj_b": (256, 1024), "w_out": (256, 1024), "w_ffn1": (1024, 1024),
               "w_ffn2": (1024, 1024)}
SHIFT_SHARD = (2, 840)
VECTORS = ["g_mix", "sgu_ln_w", "sgu_ln_b", "w0", "a0", "k_k", "k_a", "r_k", "ln_x_w", "ln_x_b", "g_ffn", "g_final"]
SMALL = VECTORS + ["sgu_w", "sgu_b"]
SMALL_SHAPE = {**{n: (1, 1024) for n in VECTORS}, "sgu_w": (8, 128, 128), "sgu_b": (8, 128)}
WEIGHTS = ["g_mix", "w_in", "sgu_ln_w", "sgu_ln_b", "sgu_w", "sgu_b", "w_proj_a", "shift_b", "w_lora_w", "w0",
           "a_lora_w", "a0", "g_lora_w", "k_k", "k_a", "r_k", "ln_x_w", "ln_x_b", "w_proj_b", "w_out", "g_ffn",
           "w_ffn1", "w_ffn2", "g_final"]


def _size(shape):
    n = 1
    for s in shape:
        n *= s
    return n


def _pack_rows(parts, rows, dtype):
    flat = jnp.concatenate([p.reshape(-1).astype(dtype) for p in parts])
    return jnp.concatenate([flat, jnp.zeros((rows * 1024 - flat.shape[0],), dtype)]).reshape(rows, 1024)


def _unpack_rows(packed, shapes):
    flat = packed.reshape(-1)
    out, off = [], 0
    for shp in shapes:
        out.append(flat[off:off + _size(shp)].reshape(shp))
        off += _size(shp)
    return out


def _shard_of(name, full, j):
    ax = SHARD_AXIS[name]
    n = SHARD_SHAPE[name][ax]
    return lax.slice_in_dim(full, j * n, (j + 1) * n, axis=ax)


def _pad_cols(z, n):
    return jnp.concatenate([z, jnp.zeros((z.shape[0], n - z.shape[1]), z.dtype)], axis=1)


def _row_form(name, s):
    return s.T if name == "w_in" else s


def _half_piece(name, rf, h):
    if name in LORAS:
        r = PIECE_ROWS[name]
        return _pad_cols(rf[h * r:(h + 1) * r], HALF_W)
    return rf[:, HALF_W * h:HALF_W * (h + 1)]


def _pack_half(group, rf_fn, h, dtype, tail=()):
    parts, pos, rows = [], 0, PACK_ROWS
    for n in group:
        if PIECE_OFF[n] > pos:
            parts.append(jnp.zeros((PIECE_OFF[n] - pos, HALF_W), dtype))
        parts.append(_half_piece(n, rf_fn(n), h).astype(dtype))
        pos = PIECE_OFF[n] + PIECE_ROWS[n]
    for t in tail:
        parts.append(t)
        pos += t.shape[0]
    parts.append(jnp.zeros((rows - pos, HALF_W), dtype))
    return jnp.concatenate(parts, axis=0)


def _piece(pack, name):
    return pack[PIECE_OFF[name]:PIECE_OFF[name] + PIECE_ROWS[name]]


def _join_halves(name, p0, p1):
    if name in LORAS:
        return jnp.concatenate([p0[:, :SHARD_SHAPE[name][1]], p1[:, :SHARD_SHAPE[name][1]]], axis=0)
    return jnp.concatenate([p0, p1], axis=1)


def _grad_row_form(name, full, j):
    if name == "w_in":
        return full[SHARD_SHAPE[name][1] * j:SHARD_SHAPE[name][1] * (j + 1)]
    return _shard_of(name, full, j)


def adamw_weight(name, g_own, g_other, w, m, v):
    rows, width = w.shape
    if name in LORAS:
        tm = PIECE_ROWS[name]
        grid = (2, 1)
        native = pl.BlockSpec((tm, width), lambda h, i: (h, 0))
    elif name == "w_in":
        tm, lanes = rows, 128
        grid = (2, HALF_W // lanes)
        native = pl.BlockSpec((tm, lanes), lambda h, i: (0, h * (HALF_W // lanes) + i))
    else:
        tm = 128
        grid = (2, rows // tm)
        native = pl.BlockSpec((tm, HALF_W), lambda h, i: (i, h))
    off = PIECE_OFF[name] // tm
    if name == "w_in":
        packed = pl.BlockSpec((tm, 128), lambda h, i: (0, i))
    else:
        packed = pl.BlockSpec((tm, HALF_W), lambda h, i: (off + i, 0))

    def body(go_ref, gx_ref, w_ref, m_ref, v_ref, g_ref, d_ref, nm_ref, nv_ref):
        g = jnp.where(pl.program_id(0) == lax.axis_index("c"), go_ref[...], gx_ref[...])[:, :w_ref.shape[1]]
        delta, nm, nv = f_adamw(g, w_ref[...], m_ref[...], v_ref[...])
        g_ref[...] = g
        d_ref[...] = delta
        nm_ref[...] = nm
        nv_ref[...] = nv

    return pl.pallas_call(
        body,
        name="adamw_" + name,
        grid=grid,
        in_specs=[packed, packed, native, native, native],
        out_specs=[native] * 4,
        out_shape=[jax.ShapeDtypeStruct(w.shape, F32)] * 4,
        compiler_params=_cparams(2),
    )(g_own, g_other, w, m, v)


def kernel(x, g_mix, w_in, sgu_ln_w, sgu_ln_b, sgu_w, sgu_b, w_proj_a, shift_b, w_lora_w, w0, a_lora_w, a0, g_lora_w, k_k, k_a, r_k, ln_x_w, ln_x_b, w_proj_b, w_out, g_ffn, w_ffn1, w_ffn2, g_final, loss_target, m_g_mix, m_w_in, m_sgu_ln_w, m_sgu_ln_b, m_sgu_w, m_sgu_b, m_w_proj_a, m_shift_b, m_w_lora_w, m_w0, m_a_lora_w, m_a0, m_g_lora_w, m_k_k, m_k_a, m_r_k, m_ln_x_w, m_ln_x_b, m_w_proj_b, m_w_out, m_g_ffn, m_w_ffn1, m_w_ffn2, m_g_final, v_g_mix, v_w_in, v_sgu_ln_w, v_sgu_ln_b, v_sgu_w, v_sgu_b, v_w_proj_a, v_shift_b, v_w_lora_w, v_w0, v_a_lora_w, v_a0, v_g_lora_w, v_k_k, v_k_a, v_r_k, v_ln_x_w, v_ln_x_b, v_w_proj_b, v_w_out, v_g_ffn, v_w_ffn1, v_w_ffn2, v_g_final):
    given = dict(zip(WEIGHTS, (g_mix, w_in, sgu_ln_w, sgu_ln_b, sgu_w, sgu_b, w_proj_a, shift_b, w_lora_w, w0, a_lora_w, a0, g_lora_w, k_k, k_a, r_k, ln_x_w, ln_x_b, w_proj_b, w_out, g_ffn, w_ffn1, w_ffn2, g_final)))
    mom_m = dict(zip(WEIGHTS, (m_g_mix, m_w_in, m_sgu_ln_w, m_sgu_ln_b, m_sgu_w, m_sgu_b, m_w_proj_a, m_shift_b, m_w_lora_w, m_w0, m_a_lora_w, m_a0, m_g_lora_w, m_k_k, m_k_a, m_r_k, m_ln_x_w, m_ln_x_b, m_w_proj_b, m_w_out, m_g_ffn, m_w_ffn1, m_w_ffn2, m_g_final)))
    mom_v = dict(zip(WEIGHTS, (v_g_mix, v_w_in, v_sgu_ln_w, v_sgu_ln_b, v_sgu_w, v_sgu_b, v_w_proj_a, v_shift_b, v_w_lora_w, v_w0, v_a_lora_w, v_a0, v_g_lora_w, v_k_k, v_k_a, v_r_k, v_ln_x_w, v_ln_x_b, v_w_proj_b, v_w_out, v_g_ffn, v_w_ffn1, v_w_ffn2, v_g_final)))
    chip = 2 * lax.axis_index("x") + lax.axis_index("y")

    def local_block(tree, n):
        return tree[n] if n == "g_final" else tree[n][0]

    sb = local_block(given, "shift_b")
    lo_part = lambda z: (z - z.astype(BF16).astype(F32)).astype(BF16)
    row_form = lambda tree: (lambda n: _row_form(n, local_block(tree, n)))
    tile16 = lambda z: jnp.pad(z, ((0, 16 - z.shape[0]), (0, HALF_W - z.shape[1])))
    sb_tiles = [tile16(f(sb[:, lanes])) for f in (lambda z: z.astype(BF16), lo_part)
                for lanes in (slice(0, HALF_W), slice(HALF_W, None))]
    tails = [[_half_piece(n, lo_part(local_block(given, n)), h) for n in LORAS] + sb_tiles for h in range(2)]
    pack_w = jnp.stack([_pack_half(EARLY, row_form(given), h, BF16, tails[h]) for h in range(2)])
    gathered, gathered_token = gather_shards(pack_w)
    gathered = lax.dynamic_update_index_in_dim(gathered, pack_w, chip, 0)
    pack_late = jnp.stack([_pack_half(LATE, row_form(given), h, BF16) for h in range(2)])
    *late_state, late_token = gather_start(pack_late, gathered_token)

    def whole(group, got, own):
        half = lambda n, j, h: jnp.where(chip == j, _piece(own[h], n), _piece(got[j, h], n))
        shard = lambda n, j: _join_halves(n, half(n, j, 0), half(n, j, 1))
        return {n: jnp.concatenate([shard(n, j).astype(F32 if n == "w_in" else BF16) for j in range(N_CHIPS)],
                                   axis=0 if n == "w_in" else SHARD_AXIS[n]) for n in group}

    w = whole(EARLY, gathered, pack_w)
    late_weights = lambda after: whole(LATE, gather_forward(gather_wait(*late_state, after)), pack_late)
    off = LO_OFF
    for n in LORAS:
        r, cols = PIECE_ROWS[n], SHARD_SHAPE[n][1]
        lo = jnp.concatenate([jnp.concatenate([gathered[j, 0, off:off + r, :cols], gathered[j, 1, off:off + r, :cols]],
                                              axis=0) for j in range(N_CHIPS)], axis=1)
        w[n] = w[n].astype(F32) + lo.astype(F32)
        off += r
    sb_tile = lambda j, t, lanes: gathered[j, 0, off + 16 * t:off + 16 * t + 2, :lanes].astype(F32)
    rest = SHIFT_SHARD[1] - HALF_W
    w["shift_b"] = jnp.concatenate(
        [jnp.concatenate([sb_tile(j, 0, HALF_W) + sb_tile(j, 2, HALF_W), sb_tile(j, 1, rest) + sb_tile(j, 3, rest)],
                         axis=1) for j in range(N_CHIPS)], axis=1)
    for n in SMALL:
        w[n] = local_block(given, n).reshape(SMALL_SHAPE[n])

    def pair_start(g_pack, tag):
        *state, token = reduce_pair_start(g_pack, tag)
        return state, token

    def pair_finish(state, after, tag):
        return pair_sum(*reduce_pair_wait(*state, after, tag), tag, tm=PACK_TILE)

    pack_early = lambda g: jnp.stack([jnp.stack([_pack_half(EARLY, lambda n: _grad_row_form(n, g[n], j), h, F32)
                                                 for h in range(2)]) for j in range(N_CHIPS)])
    loss, grad_x, grads, (late_part, late_slots), early_state = local_step(
        x[0], loss_target[0], w, late_token, late_weights, pair_start, pair_finish, pack_early)

    early_part, early_part16 = pair_finish(early_state, grad_x, "early")
    s_pack = _pack_rows([grads[n] for n in SMALL] + [grads["shift_b"], loss.reshape(1, 1)], SMALL_ROWS, F32)
    sends, recvs, part_thru, slots_thru, token = reduce_chips_start(early_part16)
    my_chip = lambda: 2 * lax.axis_index("x") + lax.axis_index("y")
    out_g, out_d, out_m, out_v = {}, {}, {}, {}

    def finish(group, tag, part, slots):
        half_sum = sum_with_own(part, slots, my_chip, token, tm=PACK_TILE, name="chip_sum_" + tag)
        other_half = exchange_halves(half_sum, tag)
        for n in group:
            res = adamw_weight(n, half_sum, other_half,
                               *[_row_form(n, local_block(t, n)) for t in (given, mom_m, mom_v)])
            for tree, z in zip((out_g, out_d, out_m, out_v), res):
                tree[n] = _row_form(n, z)

    finish(LATE, "late", late_part, late_slots)

    small_shapes = [SMALL_SHAPE[n] for n in SMALL]
    g_small = sum_all(s_pack, token)
    w_small = _pack_rows([local_block(given, n) for n in SMALL], SMALL_ROWS, F32)
    m_small = _pack_rows([local_block(mom_m, n) for n in SMALL], SMALL_ROWS, F32)
    v_small = _pack_rows([local_block(mom_v, n) for n in SMALL], SMALL_ROWS, F32)
    d_small, nm_small, nv_small = adamw_call(g_small, w_small, m_small, v_small, tm=SMALL_ROWS, name="adamw_small")
    *g_parts, loss = _unpack_rows(g_small, small_shapes + [(2, N_RWKV), ()])
    out_g.update(zip(SMALL, g_parts[:-1]))
    out_d.update(zip(SMALL, _unpack_rows(d_small, small_shapes)))
    out_m.update(zip(SMALL, _unpack_rows(nm_small, small_shapes)))
    out_v.update(zip(SMALL, _unpack_rows(nv_small, small_shapes)))
    g_sb = lax.dynamic_slice_in_dim(g_parts[-1], chip * SHIFT_SHARD[1], SHIFT_SHARD[1], axis=1)
    sb_args = [_pack_rows([z], 8, F32) for z in (g_sb, sb, local_block(mom_m, "shift_b"), local_block(mom_v, "shift_b"))]
    sb_res = adamw_call(*sb_args, tm=8, name="adamw_shift_b")
    out_g["shift_b"] = g_sb
    for tree, res in zip((out_d, out_m, out_v), sb_res):
        tree["shift_b"] = _unpack_rows(res, [SHIFT_SHARD])[0]

    after = (out_v["w_out"], nv_small, sb_res[2])
    early_slots = reduce_chips_wait(sends, recvs, part_thru, slots_thru, jnp.concatenate([z.reshape(-1)[:8] for z in after]))
    finish(EARLY, "early", early_part, early_slots)

    def block_of(tree, n):
        return tree[n].reshape(given[n].shape)

    return (loss, grad_x[None], *[block_of(out_g, n) for n in WEIGHTS], *[block_of(out_d, n) for n in WEIGHTS],
            *[block_of(out_m, n) for n in WEIGHTS], *[block_of(out_v, n) for n in WEIGHTS])
```

```python
import functools

import jax
import jax.numpy as jnp
from jax import lax
from jax.experimental import pallas as pl
from jax.experimental.pallas import tpu as pltpu

F32 = jnp.float32
BF16 = jnp.bfloat16

D_MODEL = 1024
N_HEADS = 16
HEAD = 64
SCAN_CHUNK = 64

VMEM_LIMIT = 56 * 1024 * 1024


_BDIMS = {
    "nn": (((2,), (1,)), ((0,), (0,))),
    "nt": (((2,), (2,)), ((0,), (0,))),
    "tn": (((1,), (1,)), ((0,), (0,))),
}


def _raw_bdot(x, y, mode, fine):
    if fine:
        return lax.dot_general(x, y, _BDIMS[mode], precision=lax.Precision.HIGH, preferred_element_type=F32)
    return lax.dot_general(x.astype(BF16), y.astype(BF16), _BDIMS[mode], preferred_element_type=F32)


@functools.partial(jax.custom_vjp, nondiff_argnums=(2, 3))
def bdot(x, y, mode, fine=True):
    return _raw_bdot(x, y, mode, fine)


def _bdot_fwd(x, y, mode, fine):
    return _raw_bdot(x, y, mode, fine), (x, y)


def _bdot_bwd(mode, fine, res, g):
    x, y = res
    if mode == "nn":
        return bdot(g, y, "nt", fine), bdot(x, g, "tn", fine)
    if mode == "nt":
        return bdot(g, y, "nn", fine), bdot(g, x, "tn", fine)
    return bdot(y, g, "nt", fine), bdot(x, g, "nn", fine)


bdot.defvjp(_bdot_fwd, _bdot_bwd)


def _scan_chunk(S0, r, lw, k, v, a, b):
    nh, lc, _ = r.shape
    ti = lax.broadcasted_iota(jnp.int32, (lc, lc), 0)
    si = lax.broadcasted_iota(jnp.int32, (lc, lc), 1)
    incl = (si <= ti).astype(F32)
    strict = (si < ti).astype(F32)
    eye = (si == ti).astype(F32)
    cl = bdot(jnp.broadcast_to(incl, (nh, lc, lc)), lw, "nn")
    cl_last = cl[:, lc - 1:lc, :]
    g_last = jnp.exp(cl_last - cl)
    at = a * jnp.exp(cl - lw)
    bt = b * jnp.exp(-cl)
    kt = k * jnp.exp(-cl)
    rt = r * jnp.exp(cl)
    ar = jnp.concatenate([at, rt], axis=1)
    ar_b = bdot(ar, bt, "nt", False)
    ar_k = bdot(ar, kt, "nt", False)
    m_ab, m_rb = ar_b[:, :lc] * strict, ar_b[:, lc:] * incl
    m_ak, m_rk = ar_k[:, :lc] * strict, ar_k[:, lc:] * incl
    x = eye + m_ab
    p = bdot(m_ab, m_ab, "nn", False)
    n = 2
    while n * 2 < lc:
        px = bdot(jnp.concatenate([p, x], axis=1), p, "nn", False)
        p = px[:, :lc]
        x = x + px[:, lc:]
        n *= 2
    x = x + bdot(x, p, "nn", False)
    ar_s = bdot(ar, S0, "nt", False)
    akrk_v = bdot(jnp.concatenate([m_ak, m_rk], axis=1), v, "nn", False)
    u = bdot(x, ar_s[:, :lc] + akrk_v[:, :lc], "nn", False)
    o = ar_s[:, lc:] + bdot(m_rb, u, "nn", False) + akrk_v[:, lc:]
    s_last = S0 * jnp.exp(cl_last) + bdot(jnp.concatenate([u, v], axis=1),
                                          jnp.concatenate([b * g_last, k * g_last], axis=1), "tn", False)
    return o, s_last


def _split_heads(z):
    return jnp.stack([z[:, HEAD * h:HEAD * (h + 1)] for h in range(N_HEADS)], axis=0)


def _merge_heads(z):
    return jnp.concatenate([z[h] for h in range(N_HEADS)], axis=1)


def _scan_specs(t, ops, rev):
    nc = t // SCAN_CHUNK
    row = (lambda c: nc - 1 - c) if rev else (lambda c: c)
    specs = [pl.BlockSpec((SCAN_CHUNK, D_MODEL), lambda c, cb=cb: (row(c), cb)) for _, cb in ops]
    state = pl.BlockSpec((1, N_HEADS, HEAD, HEAD), lambda c: (row(c), 0, 0, 0))
    return nc, specs, state


def scan_fwd(ops):
    t = ops[0][0].shape[0]
    nc, specs, state = _scan_specs(t, ops, False)

    def body(r_ref, lw_ref, k_ref, v_ref, a_ref, b_ref, o_ref, s0_ref, s_scr):
        @pl.when(pl.program_id(0) == 0)
        def _():
            s_scr[...] = jnp.zeros_like(s_scr)

        s0 = s_scr[...]
        s0_ref[0] = s0
        o, s_last = _scan_chunk(s0, *[_split_heads(z[...]) for z in (r_ref, lw_ref, k_ref, v_ref, a_ref, b_ref)])
        o_ref[...] = _merge_heads(o)
        s_scr[...] = s_last

    return pl.pallas_call(
        body,
        name="scan_fwd",
        grid=(nc,),
        in_specs=specs,
        out_specs=[pl.BlockSpec((SCAN_CHUNK, D_MODEL), lambda c: (c, 0)), state],
        out_shape=[jax.ShapeDtypeStruct((t, D_MODEL), F32), jax.ShapeDtypeStruct((nc, N_HEADS, HEAD, HEAD), F32)],
        scratch_shapes=[pltpu.VMEM((N_HEADS, HEAD, HEAD), F32)],
        compiler_params=_cparams(1),
    )(*[a for a, _ in ops])


def scan_bwd(ops, s0s, do, part):
    t = ops[0][0].shape[0]
    nc, specs, state = _scan_specs(t, ops + [(do, 0)], True)

    def body(r_ref, lw_ref, k_ref, v_ref, a_ref, b_ref, do_ref, s0_ref, part_ref, *rest):
        out_refs, slots_ref, ds_scr, send_sems, recv_sems = rest[:6], rest[6], rest[7], rest[8], rest[9]
        step = pl.program_id(0)
        x, y, c = _coords()
        me = 2 * x + y
        chips = _other_chips(x, y)
        sends = [_remote(part_ref.at[2 * cx + cy], slots_ref.at[me], send_sems, recv_sems, k, (cx, cy, c))
                 for k, (cx, cy) in enumerate(chips)]

        @pl.when(step == 0)
        def _():
            ds_scr[...] = jnp.zeros_like(ds_scr)
            for cp in sends:
                cp.start()

        _, vjp = jax.vjp(_scan_chunk, s0_ref[0],
                         *[_split_heads(z[...]) for z in (r_ref, lw_ref, k_ref, v_ref, a_ref, b_ref)])
        grads = vjp((_split_heads(do_ref[...]), ds_scr[...]))
        for o_ref, g in zip(out_refs, grads[1:]):
            o_ref[...] = _merge_heads(g)
        ds_scr[...] = grads[0]

        @pl.when(step == nc - 1)
        def _():
            for k, (cx, cy) in enumerate(chips):
                _remote(part_ref.at[me], slots_ref.at[2 * cx + cy], send_sems, recv_sems, k, (cx, cy, c)).wait_recv()
            for cp in sends:
                cp.wait_send()

    return pl.pallas_call(
        body,
        name="scan_bwd",
        grid=(nc,),
        in_specs=specs + [state, _ANY],
        out_specs=[pl.BlockSpec((SCAN_CHUNK, D_MODEL), lambda c: (nc - 1 - c, 0))] * 6 + [_ANY],
        out_shape=[jax.ShapeDtypeStruct((t, D_MODEL), F32)] * 6 + [jax.ShapeDtypeStruct(part.shape, part.dtype)],
        scratch_shapes=[pltpu.VMEM((N_HEADS, HEAD, HEAD), F32), pltpu.SemaphoreType.DMA((3,)),
                        pltpu.SemaphoreType.DMA((3,))],
        compiler_params=_cparams(1),
    )(*[a for a, _ in ops], do, s0s, part)


_MDIMS = {
    "nn": (((1,), (0,)), ((), ())),
    "nt": (((1,), (1,)), ((), ())),
    "tn": (((0,), (0,)), ((), ())),
}


def _raw_mdot(x, y, mode, exact):
    if exact:
        return lax.dot_general(x, y, _MDIMS[mode], precision=lax.Precision.HIGH, preferred_element_type=F32)
    return lax.dot_general(x.astype(BF16), y.astype(BF16), _MDIMS[mode], preferred_element_type=F32)


@functools.partial(jax.custom_vjp, nondiff_argnums=(2, 3))
def mdot(x, y, mode, exact):
    return _raw_mdot(x, y, mode, exact)


def _mdot_fwd(x, y, mode, exact):
    return _raw_mdot(x, y, mode, exact), (x, y)


def _mdot_bwd(mode, exact, res, g):
    x, y = res
    if mode == "nn":
        return mdot(g, y, "nt", exact), mdot(x, g, "tn", exact)
    if mode == "nt":
        return mdot(g, y, "nn", exact), mdot(g, x, "tn", exact)
    return mdot(y, g, "nt", exact), mdot(x, g, "nn", exact)


mdot.defvjp(_mdot_fwd, _mdot_bwd)


def _seg_ones():
    i = lax.broadcasted_iota(jnp.int32, (256, 256), 0) // HEAD
    j = lax.broadcasted_iota(jnp.int32, (256, 256), 1) // HEAD
    return (i == j).astype(BF16)


@jax.custom_vjp
def segsum(x):
    bd = _seg_ones()
    hi = x.astype(BF16)
    lo = (x - hi.astype(F32)).astype(BF16)
    cols = []
    for j in range(x.shape[1] // 256):
        sl = slice(256 * j, 256 * (j + 1))
        cols.append(jnp.dot(hi[:, sl], bd, preferred_element_type=F32)
                    + jnp.dot(lo[:, sl], bd, preferred_element_type=F32))
    return jnp.concatenate(cols, axis=1)


segsum.defvjp(lambda x: (segsum(x), None), lambda _, g: (segsum(g),))


NORM_EPS = 1e-6
LN_EPS = 1e-5
GN_EPS = 64e-5
SGU_CHUNK = 128
SGU_GROUPS = 8


def _rms(x, g):
    return x * lax.rsqrt(jnp.mean(x * x, axis=-1, keepdims=True) + NORM_EPS) * g


def f_norm_in(x, g):
    return _rms(x, g), x


def f_sgu(p, ln_w, ln_b, sw, sbt):
    tm = p.shape[0]
    z = 0.5 * p * (1.0 + lax.erf(p * 0.7071067811865476))
    u, v = z[:, :D_MODEL], z[:, D_MODEL:]
    mu = jnp.mean(v, axis=-1, keepdims=True)
    d = v - mu
    vn = d * lax.rsqrt(jnp.mean(d * d, axis=-1, keepdims=True) + LN_EPS) * ln_w + ln_b
    ii = lax.broadcasted_iota(jnp.int32, (SGU_CHUNK, SGU_CHUNK), 0)
    jj = lax.broadcasted_iota(jnp.int32, (SGU_CHUNK, SGU_CHUNK), 1)
    mask = (jj <= ii).astype(F32)
    gi = lax.broadcasted_iota(jnp.int32, (SGU_GROUPS, D_MODEL), 0)
    ci = lax.broadcasted_iota(jnp.int32, (SGU_GROUPS, D_MODEL), 1) // SGU_CHUNK
    bias = mdot(sbt, (gi == ci).astype(F32), "nn", True)
    rows = []
    for c in range(tm // SGU_CHUNK):
        cols = []
        for g in range(SGU_GROUPS):
            blk = vn[c * SGU_CHUNK:(c + 1) * SGU_CHUNK, g * SGU_CHUNK:(g + 1) * SGU_CHUNK]
            cols.append(mdot(sw[g] * mask, blk, "nn", False))
        rows.append(jnp.concatenate(cols, axis=1) + bias)
    return (u * jnp.concatenate(rows, axis=0),)


def _softplus(x):
    return jnp.maximum(x, 0.0) + jnp.log1p(jnp.exp(-jnp.abs(x)))


def f_pre(q, wl, w0, al, a0, gl, k_k, k_a):
    qr, qk, qv, ql = q[:, :1024], q[:, 1024:2048], q[:, 2048:3072], q[:, 3072:]
    return _f_pre(qr, qk, qv, ql, wl, w0, al, a0, gl, k_k, k_a)


def _f_pre(qr, qk, qv, ql, wl, w0, al, a0, gl, k_k, k_a):
    xw, xa, xg = ql[:, :128], ql[:, 128:256], ql[:, 256:512]
    wr = -_softplus(-(w0 + mdot(jnp.tanh(xw), wl, "nn", False))) - 0.5
    lw = -jnp.exp(wr)
    aa = jax.nn.sigmoid(a0 + mdot(xa, al, "nn", False))
    g = mdot(jax.nn.sigmoid(xg), gl, "nn", False)
    kkr = qk * k_k
    kk = kkr / jnp.maximum(jnp.sqrt(segsum(kkr * kkr)), 1e-12)
    kp = qk * (1.0 + (aa - 1.0) * k_a)
    return qr, lw, kp, qv, -kk, kk * aa, g, qr, kp, qv


def f_post(o, r, kp, v, g, lnw, lnb, rk):
    mu = segsum(o) * (1.0 / HEAD)
    d = o - mu
    gn = d * lax.rsqrt(segsum(d * d) * (1.0 / HEAD) + GN_EPS)
    return ((gn * lnw + lnb + segsum(r * kp * rk) * v) * g,)


def f_mix(ya, yb, ga, gb):
    return (jax.nn.sigmoid(ga) * ya + jax.nn.sigmoid(gb) * yb,)


def f_ffn_in(h1, g):
    return _rms(h1, g), h1


def f_final(h1, m3, tgt, g):
    y = _rms(h1 + m3, g)
    err = jnp.square(y - tgt)
    return 0.5 * jnp.sum(jnp.mean(err, axis=-1))


def _cparams(n_grid):
    return pltpu.CompilerParams(dimension_semantics=("arbitrary",) * n_grid, vmem_limit_bytes=VMEM_LIMIT)


def _tile_spec(tm, w, cb):
    return pl.BlockSpec((tm, w), lambda i: (i, cb))


def _const_spec(c):
    nd = c.ndim
    return pl.BlockSpec(c.shape, lambda i: (0,) * nd)


def ew_call(fn, tiled, consts, outs, *, tm, name):
    t = tiled[0][0].shape[0]
    n_t, n_c = len(tiled), len(consts)

    def body(*refs):
        tv = [r[...].astype(F32) for r in refs[:n_t]]
        cv = [r[...] for r in refs[n_t:n_t + n_c]]
        res = fn(*tv, *cv)
        for o_ref, val in zip(refs[n_t + n_c:], res):
            o_ref[...] = val.astype(o_ref.dtype)

    return pl.pallas_call(
        body,
        name=name,
        grid=(t // tm,),
        in_specs=[_tile_spec(tm, w, cb) for _, w, cb in tiled] + [_const_spec(c) for c in consts],
        out_specs=[_tile_spec(tm, w, 0) for w, _ in outs],
        out_shape=[jax.ShapeDtypeStruct((t, w), dt) for w, dt in outs],
        compiler_params=_cparams(1),
    )(*[a for a, _, _ in tiled], *consts)


def ew_vjp_call(fn, tiled, consts, cots, d_tiled, d_consts, *, tm, name):
    t = tiled[0][0].shape[0]
    n_t, n_c, n_g = len(tiled), len(consts), len(cots)
    dt_list = [(i, dt) for i, dts in enumerate(d_tiled) for dt in dts]
    dc_list = [i for i, want in enumerate(d_consts) if want]

    def body(*refs):
        tv = [r[...].astype(F32) for r in refs[:n_t]]
        cv = [r[...] for r in refs[n_t:n_t + n_c]]
        gv = tuple(r[...].astype(F32) for r in refs[n_t + n_c:n_t + n_c + n_g])
        out_refs = refs[n_t + n_c + n_g:]
        _, vjp = jax.vjp(fn, *tv, *cv)
        grads = vjp(gv)
        for o_ref, (i, _) in zip(out_refs, dt_list):
            o_ref[...] = grads[i].astype(o_ref.dtype)
        acc_refs = out_refs[len(dt_list):]

        @pl.when(pl.program_id(0) == 0)
        def _():
            for a_ref in acc_refs:
                a_ref[...] = jnp.zeros_like(a_ref)

        for a_ref, i in zip(acc_refs, dc_list):
            a_ref[...] += grads[n_t + i]

    res = pl.pallas_call(
        body,
        name=name,
        grid=(t // tm,),
        in_specs=[_tile_spec(tm, w, cb) for _, w, cb in tiled] + [_const_spec(c) for c in consts]
        + [_tile_spec(tm, w, cb) for _, w, cb in cots],
        out_specs=[_tile_spec(tm, tiled[i][1], 0) for i, _ in dt_list] + [_const_spec(consts[i]) for i in dc_list],
        out_shape=[jax.ShapeDtypeStruct((t, tiled[i][1]), dt) for i, dt in dt_list]
        + [jax.ShapeDtypeStruct(consts[i].shape, F32) for i in dc_list],
        compiler_params=_cparams(1),
    )(*[a for a, _, _ in tiled], *consts, *[a for a, _, _ in cots])
    return res[:len(dt_list)], res[len(dt_list):]


def mm(a, b, mode, *, tm, tn, name, out_dtypes=(F32,), epi=None, extras=(), into=None):
    m = a.shape[1] if mode == "tn" else a.shape[0]
    kd = a.shape[0] if mode == "tn" else a.shape[1]
    n = b.shape[0] if mode == "nt" else b.shape[1]
    tm, tn = min(tm, m), min(tn, n)
    if mode == "nn":
        a_spec = pl.BlockSpec((tm, kd), lambda i, j: (i, 0))
        b_spec = pl.BlockSpec((kd, tn), lambda i, j: (0, j))
    elif mode == "nt":
        a_spec = pl.BlockSpec((tm, kd), lambda i, j: (i, 0))
        b_spec = pl.BlockSpec((tn, kd), lambda i, j: (j, 0))
    else:
        a_spec = pl.BlockSpec((kd, tm), lambda i, j: (0, i))
        b_spec = pl.BlockSpec((kd, tn), lambda i, j: (0, j))
    n_e = len(extras)
    o_spec = pl.BlockSpec((tm, tn), lambda i, j: (i, j))

    if into is not None:
        buf, place = into

        def body_into(a_ref, b_ref, buf_ref, o_ref):
            o_ref[0, 0] = lax.dot_general(a_ref[...].astype(BF16), b_ref[...].astype(BF16), _MDIMS[mode],
                                          preferred_element_type=F32)

        return pl.pallas_call(
            body_into,
            name=name,
            grid=(m // tm, n // tn),
            in_specs=[a_spec, b_spec, pl.BlockSpec(memory_space=pl.ANY)],
            out_specs=pl.BlockSpec((1, 1, tm, tn), lambda i, j: (*place(i, j), 0)),
            out_shape=jax.ShapeDtypeStruct(buf.shape, F32),
            input_output_aliases={2: 0},
            compiler_params=_cparams(2),
        )(a, b, buf)

    def body(a_ref, b_ref, *refs):
        c = lax.dot_general(a_ref[...].astype(BF16), b_ref[...].astype(BF16), _MDIMS[mode],
                            preferred_element_type=F32)
        res = epi(c, *[r[...] for r in refs[:n_e]]) if epi is not None else (c,)
        for o_ref, val in zip(refs[n_e:], res):
            o_ref[...] = val.astype(o_ref.dtype)

    res = pl.pallas_call(
        body,
        name=name,
        grid=(m // tm, n // tn),
        in_specs=[a_spec, b_spec] + [o_spec] * n_e,
        out_specs=[o_spec] * len(out_dtypes),
        out_shape=[jax.ShapeDtypeStruct((m, n), dt) for dt in out_dtypes],
        compiler_params=_cparams(2),
    )(a, b, *extras)
    return res if len(out_dtypes) > 1 else res[0]


P_WIDTH = 7680
RWKV_COL0 = 4096
RWKV_WIDTH = 3584
SHIFT_BLK = 512


def _shift_down(p, prev_row):
    rows = lax.broadcasted_iota(jnp.int32, p.shape, 0)
    return jnp.where(rows == 0, prev_row, pltpu.roll(p, 1, 0))


def shiftmix_fwd(p_all, sbp, *, tm):
    t = p_all.shape[0]
    tm = min(tm, t)
    c0 = RWKV_COL0 // SHIFT_BLK
    hb = tm // 8

    def body(p_ref, halo_ref, sb_ref, q_ref):
        p = p_ref[...]
        prev = jnp.where(pl.program_id(0) == 0, 0.0, halo_ref[7:8, :])
        q_ref[...] = p * sb_ref[0:1, :] + _shift_down(p, prev) * sb_ref[1:2, :]

    return pl.pallas_call(
        body,
        name="shiftmix_fwd",
        grid=(t // tm, RWKV_WIDTH // SHIFT_BLK),
        in_specs=[
            pl.BlockSpec((tm, SHIFT_BLK), lambda i, j: (i, c0 + j)),
            pl.BlockSpec((8, SHIFT_BLK), lambda i, j: (jnp.maximum(i * hb - 1, 0), c0 + j)),
            pl.BlockSpec((2, SHIFT_BLK), lambda i, j: (0, j)),
        ],
        out_specs=pl.BlockSpec((tm, SHIFT_BLK), lambda i, j: (i, j)),
        out_shape=jax.ShapeDtypeStruct((t, RWKV_WIDTH), F32),
        compiler_params=_cparams(2),
    )(p_all, p_all, sbp)


def shiftmix_bwd(dq, col0, p_all, sbp, *, tm, name):
    t, w = dq.shape
    n_i = t // tm
    hb = tm // 8
    cq = col0 // SHIFT_BLK
    cp = (RWKV_COL0 + col0) // SHIFT_BLK

    def body(dq_ref, dqn_ref, p_ref, ph_ref, sb_ref, dp_ref, dsb_ref):
        i = pl.program_id(1)
        dq_t = dq_ref[...]
        rows = lax.broadcasted_iota(jnp.int32, dq_t.shape, 0)
        nxt = jnp.where(i == n_i - 1, 0.0, dqn_ref[0:1, :])
        up = jnp.where(rows == tm - 1, nxt, pltpu.roll(dq_t, tm - 1, 0))
        dp_ref[...] = (dq_t * sb_ref[0:1, :] + up * sb_ref[1:2, :]).astype(dp_ref.dtype)
        p = p_ref[...]
        prev = jnp.where(i == 0, 0.0, ph_ref[7:8, :])
        s0 = jnp.sum(dq_t * p, axis=0, keepdims=True)
        s1 = jnp.sum(dq_t * _shift_down(p, prev), axis=0, keepdims=True)
        two = lax.broadcasted_iota(jnp.int32, (2, SHIFT_BLK), 0)

        @pl.when(i == 0)
        def _():
            dsb_ref[...] = jnp.zeros_like(dsb_ref)

        dsb_ref[...] += jnp.where(two == 0, s0, s1)

    return pl.pallas_call(
        body,
        name=name,
        grid=(w // SHIFT_BLK, n_i),
        in_specs=[
            pl.BlockSpec((tm, SHIFT_BLK), lambda j, i: (i, j)),
            pl.BlockSpec((8, SHIFT_BLK), lambda j, i: (jnp.minimum((i + 1) * hb, t // 8 - 1), j)),
            pl.BlockSpec((tm, SHIFT_BLK), lambda j, i: (i, cp + j)),
            pl.BlockSpec((8, SHIFT_BLK), lambda j, i: (jnp.maximum(i * hb - 1, 0), cp + j)),
            pl.BlockSpec((2, SHIFT_BLK), lambda j, i: (0, cq + j)),
        ],
        out_specs=[
            pl.BlockSpec((tm, SHIFT_BLK), lambda j, i: (i, j)),
            pl.BlockSpec((2, SHIFT_BLK), lambda j, i: (0, j)),
        ],
        out_shape=[jax.ShapeDtypeStruct((t, w), BF16), jax.ShapeDtypeStruct((2, w), F32)],
        compiler_params=_cparams(2),
    )(dq, dq, p_all, p_all, sbp)


def final_call(h1, m3, tgt, g_final, *, tm):
    t = h1.shape[0]

    def body(h1_ref, m3_ref, tgt_ref, g_ref, dh_ref, dhb_ref, dg_ref, loss_ref):
        loss, vjp = jax.vjp(f_final, h1_ref[...], m3_ref[...], tgt_ref[...], g_ref[...])
        dh, _, _, dg = vjp(jnp.ones((), F32))
        dh_ref[...] = dh
        dhb_ref[...] = dh.astype(BF16)

        @pl.when(pl.program_id(0) == 0)
        def _():
            dg_ref[...] = jnp.zeros_like(dg_ref)
            loss_ref[...] = jnp.zeros_like(loss_ref)

        dg_ref[...] += dg
        loss_ref[...] += jnp.full(loss_ref.shape, loss, F32)

    tile = _tile_spec(tm, D_MODEL, 0)
    return pl.pallas_call(
        body,
        name="final_loss",
        grid=(t // tm,),
        in_specs=[tile, tile, tile, _const_spec(g_final)],
        out_specs=[tile, tile, _const_spec(g_final), pl.BlockSpec((8, 128), lambda i: (0, 0))],
        out_shape=[jax.ShapeDtypeStruct((t, D_MODEL), F32), jax.ShapeDtypeStruct((t, D_MODEL), BF16),
                   jax.ShapeDtypeStruct(g_final.shape, F32), jax.ShapeDtypeStruct((8, 128), F32)],
        compiler_params=_cparams(1),
    )(h1, m3, tgt, g_final)


N_SGU = 2048
N_RWKV = 3360
LORA_W, LORA_A, LORA_G = 64, 64, 160


def _pad_rwkv_cols(z):
    zero = lambda n: jnp.zeros(z.shape[:-1] + (n,), z.dtype)
    return jnp.concatenate([z[..., :3072], z[..., 3072:3136], zero(64), z[..., 3136:3200], zero(64),
                            z[..., 3200:3360], zero(96)], axis=-1)


def _unpad_rwkv_cols(z):
    return jnp.concatenate([z[..., :3072], z[..., 3072:3136], z[..., 3200:3264], z[..., 3328:3488]], axis=-1)


def _pad_win_rows(wt):
    z = wt[N_SGU:N_SGU + N_RWKV]
    zero = lambda n: jnp.zeros((n, wt.shape[1]), wt.dtype)
    return jnp.concatenate([wt[:N_SGU], wt[N_SGU + N_RWKV:], z[:3072], z[3072:3136], zero(64), z[3136:3200], zero(64),
                            z[3200:3360], zero(96)], axis=0)


def _unpad_win_rows(wt):
    z = wt[RWKV_COL0:]
    return jnp.concatenate([wt[:N_SGU], z[:3072], z[3072:3136], z[3200:3264], z[3328:3488], wt[N_SGU:RWKV_COL0]],
                           axis=0)


def _pad_rows(w, n):
    return jnp.concatenate([w, jnp.zeros((n - w.shape[0],) + w.shape[1:], w.dtype)], axis=0)


def _relu2_epi(c):
    return c, jnp.square(jnp.maximum(c, 0.0))


def _relu2_bwd_epi(c, hid):
    return (c * (2.0 * jnp.maximum(hid.astype(F32), 0.0)),)


def _add_epi(c, x):
    return (c + x,)


def _pre_fwd(*args):
    res = f_pre(*args)
    return res[1], res[2], res[4], res[5], res[6]


def local_step(x, tgt, w, late_token, late_weights, pair_start, pair_finish, pack_early):
    d = D_MODEL
    win_pt = _pad_win_rows(w["w_in"])
    sbp = _pad_rwkv_cols(w["shift_b"])
    wl = _pad_rows(w["w_lora_w"], 128)
    al = _pad_rows(w["a_lora_w"], 128)
    gl = _pad_rows(w["g_lora_w"], 256)
    sbt = w["sgu_b"].T

    (a_bf,) = ew_call(lambda x_, g_: (f_norm_in(x_, g_)[0],), [(x, d, 0)], [w["g_mix"] + late_token[:1, :1]],
                      [(d, BF16)], tm=256, name="norm_in")
    p_all = mm(a_bf, win_pt, "nt", tm=2048, tn=640, name="mm_in")
    sgu_t = [(p_all, 2 * d, 0)]
    sgu_c = [w["sgu_ln_w"], w["sgu_ln_b"], w["sgu_w"], sbt]
    (s_bf,) = ew_call(f_sgu, sgu_t, sgu_c, [(d, BF16)], tm=256, name="sgu_fwd")
    ya = mm(s_bf, w["w_proj_a"], "nn", tm=512, tn=1024, name="mm_proj_a")
    q = shiftmix_fwd(p_all, sbp, tm=1024)
    pre_t = [(q, RWKV_WIDTH, 0)]
    pre_c = [wl, w["w0"], al, w["a0"], gl, w["k_k"], w["k_a"]]
    lw, kp, na, nb, g = ew_call(_pre_fwd, pre_t, pre_c, [(d, F32)] * 5, tm=256, name="rwkv_pre_fwd")
    scan_ops = [(q, 0), (lw, 0), (kp, 0), (q, 2), (na, 0), (nb, 0)]
    o, s0s = scan_fwd(scan_ops)
    w = {**w, **late_weights(o)}
    post_t = [(o, d, 0), (q, d, 0), (kp, d, 0), (q, d, 2), (g, d, 0)]
    post_c = [w["ln_x_w"], w["ln_x_b"], w["r_k"]]
    (ob_bf,) = ew_call(f_post, post_t, post_c, [(d, BF16)], tm=256, name="rwkv_post_fwd")
    yb = mm(ob_bf, w["w_proj_b"], "nn", tm=512, tn=1024, name="mm_proj_b")
    mix_t = [(ya, d, 0), (yb, d, 0), (p_all, d, 2), (p_all, d, 3)]
    (mixed_bf,) = ew_call(f_mix, mix_t, [], [(d, BF16)], tm=256, name="mix_fwd")
    h1 = mm(mixed_bf, w["w_out"], "nn", tm=512, tn=1024, name="mm_out", epi=_add_epi, extras=(x,))
    (f_bf,) = ew_call(lambda h_, g_: (f_ffn_in(h_, g_)[0],), [(h1, d, 0)], [w["g_ffn"]], [(d, BF16)], tm=256,
                      name="ffn_norm")
    hid, act_bf = mm(f_bf, w["w_ffn1"], "nn", tm=2048, tn=1024, name="mm_ffn1", out_dtypes=(BF16, BF16), epi=_relu2_epi)
    m3 = mm(act_bf, w["w_ffn2"], "nn", tm=1024, tn=512, name="mm_ffn2")
    dh2, dh2_bf, dg_final, loss = final_call(h1, m3, tgt, w["g_final"], tm=256)

    dhid_bf = mm(dh2_bf, w["w_ffn2"], "nt", tm=2048, tn=1024, name="mm_dact", out_dtypes=(BF16,), epi=_relu2_bwd_epi,
                 extras=(hid,))
    late_g = lax.empty((N_CHIPS, 2, PACK_ROWS, HALF_W), F32)
    late_g = mm(act_bf, dh2_bf, "tn", tm=512, tn=HALF_W, name="mm_dw_ffn2",
                into=(late_g, lambda i, j: (i // 2, j, PIECE_OFF["w_ffn2"] // 512 + i % 2)))
    df = mm(dhid_bf, w["w_ffn1"], "nt", tm=1024, tn=512, name="mm_df")
    late_g = mm(f_bf, dhid_bf, "tn", tm=512, tn=HALF_W, name="mm_dw_ffn1",
                into=(late_g, lambda i, j: (j // 2, j % 2, PIECE_OFF["w_ffn1"] // 512 + i)))
    (dh1, dh1_bf), (dg_ffn,) = ew_vjp_call(f_ffn_in, [(h1, d, 0)], [w["g_ffn"]], [(df, d, 0), (dh2, d, 0)],
                                           [(F32, BF16)], [True], tm=256, name="ffn_norm_bwd")
    dmixed = mm(dh1_bf, w["w_out"], "nt", tm=512, tn=1024, name="mm_dmixed")
    late_g = mm(mixed_bf, dh1_bf, "tn", tm=256, tn=HALF_W, name="mm_dw_out",
                into=(late_g, lambda i, j: (i, j, PIECE_OFF["w_out"] // 256)))
    (dya_bf, dyb_bf, dga_bf, dgb_bf), _ = ew_vjp_call(f_mix, mix_t, [], [(dmixed, d, 0)], [(BF16,)] * 4, [], tm=256,
                                                      name="mix_bwd")
    dob = mm(dyb_bf, w["w_proj_b"], "nt", tm=512, tn=1024, name="mm_dob")
    late_g = mm(ob_bf, dyb_bf, "tn", tm=256, tn=HALF_W, name="mm_dw_proj_b",
                into=(late_g, lambda i, j: (i, j, PIECE_OFF["w_proj_b"] // 256)))
    late_state, late_token = pair_start(late_g, "late")
    post_c_after = [w["ln_x_w"] + late_token[:1, :1]] + post_c[1:]
    (do, dr_p, dkp_p, dv_p, dg), (dlnx_w, dlnx_b, dr_k) = ew_vjp_call(
        f_post, post_t, post_c_after, [(dob, d, 0)], [(F32,)] * 5, [True] * 3, tm=256, name="rwkv_post_bwd")
    late_part, late_part16 = pair_finish(late_state, do, "late")
    *scan_g, late_slots = scan_bwd(scan_ops, s0s, do, late_part16)
    pre_g = [(z, d, 0) for z in scan_g] + [(dg, d, 0), (dr_p, d, 0), (dkp_p, d, 0), (dv_p, d, 0)]
    (dq,), (dwl, dw0, dal, da0, dgl, dk_k, dk_a) = ew_vjp_call(
        f_pre, pre_t, pre_c, pre_g, [(F32,)], [True] * 7, tm=128, name="rwkv_pre_bwd")
    dp_rwkv, dsb = shiftmix_bwd(dq, 0, p_all, sbp, tm=512, name="shiftmix_bwd")
    ds = mm(dya_bf, w["w_proj_a"], "nt", tm=512, tn=1024, name="mm_ds")
    d_proj_a = mm(s_bf, dya_bf, "tn", tm=512, tn=1024, name="mm_dw_proj_a")
    (dp_sgu,), (dln_w, dln_b, dsw, dsbt) = ew_vjp_call(f_sgu, sgu_t, sgu_c, [(ds, d, 0)], [(BF16,)], [True] * 4,
                                                       tm=256, name="sgu_bwd")
    dp_all = jnp.concatenate([dp_sgu, dga_bf, dgb_bf, dp_rwkv], axis=1)
    d_in_pt = mm(dp_all, a_bf, "tn", tm=1280, tn=1024, name="mm_dw_in")
    early_state, early_token = pair_start(pack_early({
        "w_in": _unpad_win_rows(d_in_pt), "w_proj_a": d_proj_a, "w_lora_w": dwl[:LORA_W], "a_lora_w": dal[:LORA_A],
        "g_lora_w": dgl[:LORA_G]}), "early")
    da = mm(dp_all, win_pt, "nn", tm=1024, tn=256, name="mm_da")
    g_mix_after = w["g_mix"] + early_token[:1, :1]
    (grad_x,), (dg_mix,) = ew_vjp_call(f_norm_in, [(x, d, 0)], [g_mix_after], [(da, d, 0), (dh1, d, 0)], [(F32,)],
                                       [True], tm=256, name="norm_in_bwd")

    grads = {
        "g_mix": dg_mix, "sgu_ln_w": dln_w, "sgu_ln_b": dln_b, "sgu_w": dsw, "sgu_b": dsbt.T,
        "shift_b": _unpad_rwkv_cols(dsb),
        "w0": dw0, "a0": da0, "k_k": dk_k, "k_a": dk_a, "r_k": dr_k, "ln_x_w": dlnx_w, "ln_x_b": dlnx_b,
        "g_ffn": dg_ffn, "g_final": dg_final,
    }
    return loss[0, 0], grad_x, grads, (late_part, late_slots), early_state


MESH = pl.DeviceIdType.MESH
N_CHIPS = 4
N_DEV = 8
PACK_ROWS = 2560
PACK_TILE = 512
SMALL_ROWS = 152
_ANY = pl.BlockSpec(memory_space=pl.ANY)


def _coords():
    return lax.axis_index("x"), lax.axis_index("y"), lax.axis_index("c")


def _other_chips(x, y):
    return [(1 - x, y), (x, 1 - y), (1 - x, 1 - y)]


def _remote(src, dst, send_sems, recv_sems, k, to):
    return pltpu.make_async_remote_copy(src_ref=src, dst_ref=dst, send_sem=send_sems.at[k], recv_sem=recv_sems.at[k],
                                        device_id=to, device_id_type=MESH)


def gather_shards(pack):
    def body(src_ref, out_ref, token, send_sems, recv_sems):
        x, y, c = _coords()
        me = 2 * x + y
        sib = (x, y, 1 - c)
        chips = _other_chips(x, y)
        first = [_remote(src_ref.at[c], out_ref.at[me, c], send_sems, recv_sems, k, (cx, cy, c))
                 for k, (cx, cy) in enumerate(chips)]
        for cp in first:
            cp.start()
        passed = []
        for k, (cx, cy) in enumerate(chips):
            j = 2 * cx + cy
            _remote(src_ref.at[c], out_ref.at[j, c], send_sems, recv_sems, k, (cx, cy, c)).wait_recv()
            fwd = _remote(out_ref.at[j, c], out_ref.at[j, c], send_sems, recv_sems, 3 + k, sib)
            fwd.start()
            passed.append(fwd)
        for k, (cx, cy) in enumerate(chips):
            j = 2 * cx + cy
            _remote(out_ref.at[j, 1 - c], out_ref.at[j, 1 - c], send_sems, recv_sems, 3 + k, sib).wait_recv()
        for cp in first + passed:
            cp.wait_send()
        token[...] = jnp.zeros_like(token)

    return pl.pallas_call(
        body,
        name="gather_shards",
        in_specs=[_ANY],
        out_specs=[_ANY, pl.BlockSpec(memory_space=pltpu.VMEM)],
        out_shape=[jax.ShapeDtypeStruct((N_CHIPS,) + pack.shape, pack.dtype), jax.ShapeDtypeStruct((8, 128), F32)],
        scratch_shapes=[pltpu.SemaphoreType.DMA((6,)), pltpu.SemaphoreType.DMA((6,))],
    )(pack)


def _gather_copies(pack_ref, all_ref, send_sems, recv_sems):
    x, y, c = _coords()
    me = 2 * x + y
    return [(_remote(pack_ref.at[c], all_ref.at[me, c], send_sems, recv_sems, k, (cx, cy, c)),
             _remote(pack_ref.at[c], all_ref.at[2 * cx + cy, c], send_sems, recv_sems, k, (cx, cy, c)))
            for k, (cx, cy) in enumerate(_other_chips(x, y))]


def gather_start(pack, after):
    hbm = pl.BlockSpec(memory_space=pltpu.HBM)
    sem = pl.BlockSpec(memory_space=pltpu.SEMAPHORE)
    all_shape = (N_CHIPS,) + pack.shape

    def body(pack_ref, all_ref, after_ref, send_sems, recv_sems, pack_thru, all_thru, token):
        for send, _ in _gather_copies(pack_ref, all_ref, send_sems, recv_sems):
            send.start()
        token[...] = jnp.zeros_like(token)

    return pl.pallas_call(
        body,
        name="gather_start",
        out_shape=(pltpu.SemaphoreType.DMA((3,)), pltpu.SemaphoreType.DMA((3,)), pltpu.HBM(pack.shape, pack.dtype),
                   pltpu.HBM(all_shape, pack.dtype), jax.ShapeDtypeStruct((8, 128), F32)),
        in_specs=(hbm, hbm, pl.BlockSpec(memory_space=pl.ANY)),
        out_specs=(sem, sem, hbm, hbm, pl.BlockSpec(memory_space=pltpu.VMEM)),
        input_output_aliases={0: 2, 1: 3},
        compiler_params=pltpu.CompilerParams(has_side_effects=pltpu.SideEffectType.DATAFLOW_SIDE_EFFECTING),
    )(pltpu.with_memory_space_constraint(pack, pltpu.HBM),
      pltpu.with_memory_space_constraint(lax.empty(all_shape, pack.dtype), pltpu.HBM), after)


def gather_wait(send_sems, recv_sems, pack_thru, all_thru, after):
    hbm = pl.BlockSpec(memory_space=pltpu.HBM)
    sem = pl.BlockSpec(memory_space=pltpu.SEMAPHORE)

    def body(pack_ref, all_ref, send_sems, recv_sems, after_ref, pack_out, all_out):
        for send, arrival in _gather_copies(pack_ref, all_ref, send_sems, recv_sems):
            send.wait_send()
            arrival.wait_recv()

    return pl.pallas_call(
        body,
        name="gather_wait",
        out_shape=(pltpu.HBM(pack_thru.shape, pack_thru.dtype), pltpu.HBM(all_thru.shape, all_thru.dtype)),
        in_specs=(hbm, hbm, sem, sem, pl.BlockSpec(memory_space=pl.ANY)),
        out_specs=(hbm, hbm),
        input_output_aliases={0: 0, 1: 1},
        compiler_params=pltpu.CompilerParams(has_side_effects=pltpu.SideEffectType.DATAFLOW_SIDE_EFFECTING),
    )(pack_thru, all_thru, send_sems, recv_sems, after)[1]


def gather_forward(got):
    def body(got_ref, out_ref, send_sems, recv_sems):
        x, y, c = _coords()
        sib = (x, y, 1 - c)
        slots = [2 * cx + cy for cx, cy in _other_chips(x, y)]
        sends = [_remote(got_ref.at[j, c], out_ref.at[j, c], send_sems, recv_sems, k, sib) for k, j in enumerate(slots)]
        for cp in sends:
            cp.start()
        for k, j in enumerate(slots):
            _remote(got_ref.at[j, 1 - c], out_ref.at[j, 1 - c], send_sems, recv_sems, k, sib).wait_recv()
        for cp in sends:
            cp.wait_send()

    return pl.pallas_call(
        body,
        name="gather_forward",
        in_specs=[_ANY],
        out_specs=_ANY,
        out_shape=jax.ShapeDtypeStruct(got.shape, got.dtype),
        input_output_aliases={0: 0},
        scratch_shapes=[pltpu.SemaphoreType.DMA((3,)), pltpu.SemaphoreType.DMA((3,))],
    )(got)


def reduce_pair(g, tag):
    def body(g_ref, got_ref, send_sems, recv_sems):
        x, y, c = _coords()
        sib = (x, y, 1 - c)
        sends = [_remote(g_ref.at[j, 1 - c], got_ref.at[j], send_sems, recv_sems, j, sib) for j in range(N_CHIPS)]
        for cp in sends:
            cp.start()
        for cp in sends:
            cp.wait_recv()
        for cp in sends:
            cp.wait_send()

    return pl.pallas_call(
        body,
        name="reduce_pair_" + tag,
        in_specs=[_ANY],
        out_specs=_ANY,
        out_shape=jax.ShapeDtypeStruct((N_CHIPS,) + g.shape[2:], g.dtype),
        scratch_shapes=[pltpu.SemaphoreType.DMA((N_CHIPS,)), pltpu.SemaphoreType.DMA((N_CHIPS,))],
    )(g)


def pair_sum(g, got, tag, *, tm):
    n, _, rows, width = g.shape

    def body(g0_ref, g1_ref, got_ref, out_ref, out16_ref):
        own = jnp.where(lax.axis_index("c") == 0, g0_ref[0, 0], g1_ref[0, 0])
        total = own + got_ref[0]
        out_ref[0] = total
        out16_ref[0] = total.astype(BF16)

    blk = pl.BlockSpec((1, tm, width), lambda j, i: (j, i, 0))
    return pl.pallas_call(
        body,
        name="pair_sum_" + tag,
        grid=(n, rows // tm),
        in_specs=[pl.BlockSpec((1, 1, tm, width), lambda j, i: (j, 0, i, 0)),
                  pl.BlockSpec((1, 1, tm, width), lambda j, i: (j, 1, i, 0)), blk],
        out_specs=[blk, blk],
        out_shape=[jax.ShapeDtypeStruct(got.shape, F32), jax.ShapeDtypeStruct(got.shape, BF16)],
        compiler_params=_cparams(2),
    )(g, g, got)


def reduce_chips(p):
    def body(p_ref, out_ref, send_sems, recv_sems):
        x, y, c = _coords()
        me = 2 * x + y
        chips = _other_chips(x, y)
        sends = [_remote(p_ref.at[2 * cx + cy], out_ref.at[me], send_sems, recv_sems, k, (cx, cy, c))
                 for k, (cx, cy) in enumerate(chips)]
        for cp in sends:
            cp.start()
        for k, (cx, cy) in enumerate(chips):
            _remote(p_ref.at[me], out_ref.at[2 * cx + cy], send_sems, recv_sems, k, (cx, cy, c)).wait_recv()
        for cp in sends:
            cp.wait_send()

    return pl.pallas_call(
        body,
        name="reduce_chips",
        in_specs=[_ANY],
        out_specs=_ANY,
        out_shape=jax.ShapeDtypeStruct(p.shape, p.dtype),
        scratch_shapes=[pltpu.SemaphoreType.DMA((3,)), pltpu.SemaphoreType.DMA((3,))],
    )(p)


def _pair_copies(g_ref, got_ref, send_sems, recv_sems):
    x, y, c = _coords()
    return [_remote(g_ref.at[j, 1 - c], got_ref.at[j], send_sems, recv_sems, j, (x, y, 1 - c)) for j in range(N_CHIPS)]


def reduce_pair_start(g, tag):
    hbm = pl.BlockSpec(memory_space=pltpu.HBM)
    sem = pl.BlockSpec(memory_space=pltpu.SEMAPHORE)
    got_shape = (N_CHIPS,) + g.shape[2:]

    def body(g_ref, got_ref, send_sems, recv_sems, g_thru, got_thru, token):
        for cp in _pair_copies(g_ref, got_ref, send_sems, recv_sems):
            cp.start()
        token[...] = jnp.zeros_like(token)

    return pl.pallas_call(
        body,
        name="reduce_pair_start_" + tag,
        out_shape=(pltpu.SemaphoreType.DMA((N_CHIPS,)), pltpu.SemaphoreType.DMA((N_CHIPS,)),
                   pltpu.HBM(g.shape, g.dtype), pltpu.HBM(got_shape, g.dtype), jax.ShapeDtypeStruct((8, 128), F32)),
        in_specs=(hbm, hbm),
        out_specs=(sem, sem, hbm, hbm, pl.BlockSpec(memory_space=pltpu.VMEM)),
        input_output_aliases={0: 2, 1: 3},
        compiler_params=pltpu.CompilerParams(has_side_effects=pltpu.SideEffectType.DATAFLOW_SIDE_EFFECTING),
    )(pltpu.with_memory_space_constraint(g, pltpu.HBM),
      pltpu.with_memory_space_constraint(lax.empty(got_shape, g.dtype), pltpu.HBM))


def reduce_pair_wait(send_sems, recv_sems, g_thru, got_thru, after, tag):
    hbm = pl.BlockSpec(memory_space=pltpu.HBM)
    sem = pl.BlockSpec(memory_space=pltpu.SEMAPHORE)

    def body(g_ref, got_ref, send_sems, recv_sems, after_ref, g_out, got_out):
        for cp in _pair_copies(g_ref, got_ref, send_sems, recv_sems):
            cp.wait_send()
            cp.wait_recv()

    return pl.pallas_call(
        body,
        name="reduce_pair_wait_" + tag,
        out_shape=(pltpu.HBM(g_thru.shape, g_thru.dtype), pltpu.HBM(got_thru.shape, got_thru.dtype)),
        in_specs=(hbm, hbm, sem, sem, pl.BlockSpec(memory_space=pl.ANY)),
        out_specs=(hbm, hbm),
        input_output_aliases={0: 0, 1: 1},
        compiler_params=pltpu.CompilerParams(has_side_effects=pltpu.SideEffectType.DATAFLOW_SIDE_EFFECTING),
    )(g_thru, got_thru, send_sems, recv_sems, after)


def _chip_copies(p_ref, slots_ref, send_sems, recv_sems):
    x, y, c = _coords()
    me = 2 * x + y
    return [(_remote(p_ref.at[2 * cx + cy], slots_ref.at[me], send_sems, recv_sems, k, (cx, cy, c)),
             _remote(p_ref.at[me], slots_ref.at[2 * cx + cy], send_sems, recv_sems, k, (cx, cy, c)))
            for k, (cx, cy) in enumerate(_other_chips(x, y))]


def reduce_chips_start(p):
    hbm = pl.BlockSpec(memory_space=pltpu.HBM)
    sem = pl.BlockSpec(memory_space=pltpu.SEMAPHORE)

    def body(p_ref, slots_ref, send_sems, recv_sems, p_thru, slots_thru, token):
        for send, _ in _chip_copies(p_ref, slots_ref, send_sems, recv_sems):
            send.start()
        token[...] = jnp.zeros_like(token)

    return pl.pallas_call(
        body,
        name="reduce_chips_start",
        out_shape=(pltpu.SemaphoreType.DMA((3,)), pltpu.SemaphoreType.DMA((3,)), pltpu.HBM(p.shape, p.dtype),
                   pltpu.HBM(p.shape, p.dtype), jax.ShapeDtypeStruct((8, 128), F32)),
        in_specs=(hbm, hbm),
        out_specs=(sem, sem, hbm, hbm, pl.BlockSpec(memory_space=pltpu.VMEM)),
        input_output_aliases={0: 2, 1: 3},
        compiler_params=pltpu.CompilerParams(has_side_effects=pltpu.SideEffectType.DATAFLOW_SIDE_EFFECTING),
    )(pltpu.with_memory_space_constraint(p, pltpu.HBM),
      pltpu.with_memory_space_constraint(lax.empty(p.shape, p.dtype), pltpu.HBM))


def reduce_chips_wait(send_sems, recv_sems, p_thru, slots_thru, after):
    hbm = pl.BlockSpec(memory_space=pltpu.HBM)
    sem = pl.BlockSpec(memory_space=pltpu.SEMAPHORE)

    def body(p_ref, slots_ref, send_sems, recv_sems, after_ref, p_dead, slots_out):
        for send, arrival in _chip_copies(p_ref, slots_ref, send_sems, recv_sems):
            send.wait_send()
            arrival.wait_recv()

    return pl.pallas_call(
        body,
        name="reduce_chips_wait",
        out_shape=(pltpu.HBM(p_thru.shape, p_thru.dtype), pltpu.HBM(slots_thru.shape, slots_thru.dtype)),
        in_specs=(hbm, hbm, sem, sem, pl.BlockSpec(memory_space=pl.ANY)),
        out_specs=(hbm, hbm),
        input_output_aliases={0: 0, 1: 1},
        compiler_params=pltpu.CompilerParams(has_side_effects=pltpu.SideEffectType.DATAFLOW_SIDE_EFFECTING),
    )(p_thru, slots_thru, send_sems, recv_sems, after)[1]


def sum_with_own(own, slots, index_fn, after, *, tm, name):
    n, rows, width = slots.shape

    def body(*refs):
        mine = index_fn()
        acc = None
        for s in range(n):
            term = jnp.where(mine == s, refs[s][0], refs[n + s][0].astype(F32))
            acc = term if acc is None else acc + term
        refs[-1][...] = acc

    slot_specs = [pl.BlockSpec((1, tm, width), lambda i, s=s: (s, i, 0)) for s in range(n)]
    return pl.pallas_call(
        body,
        name=name,
        grid=(rows // tm,),
        in_specs=slot_specs + slot_specs + [pl.BlockSpec(after.shape, lambda i: (0,) * after.ndim)],
        out_specs=pl.BlockSpec((tm, width), lambda i: (i, 0)),
        out_shape=jax.ShapeDtypeStruct((rows, width), F32),
        compiler_params=_cparams(1),
    )(*([own] * n), *([slots] * n), after)


def exchange_halves(s, tag):
    rq = PACK_TILE
    nq = s.shape[0] // rq

    def body(s_ref, out_ref, sbuf, rbuf, send_sems, recv_sems, in_sems, out_sems):
        x, y, c = _coords()
        sib = (x, y, 1 - c)
        rows = lambda q: pl.ds(q * rq, rq)
        loads = [pltpu.make_async_copy(s_ref.at[rows(q)], sbuf.at[rows(q)], in_sems.at[q]) for q in range(nq)]
        for cp in loads:
            cp.start()
        sends = []
        for q in range(nq):
            loads[q].wait()
            sends.append(_remote(sbuf.at[rows(q)], rbuf.at[rows(q)], send_sems, recv_sems, q, sib))
            sends[q].start()
        stores = []
        for q in range(nq):
            sends[q].wait_recv()
            stores.append(pltpu.make_async_copy(rbuf.at[rows(q)], out_ref.at[rows(q)], out_sems.at[q]))
            stores[q].start()
        for cp in sends:
            cp.wait_send()
        for cp in stores:
            cp.wait()

    return pl.pallas_call(
        body,
        name="exchange_halves_" + tag,
        in_specs=[_ANY],
        out_specs=_ANY,
        out_shape=jax.ShapeDtypeStruct(s.shape, s.dtype),
        scratch_shapes=[pltpu.VMEM(s.shape, s.dtype), pltpu.VMEM(s.shape, s.dtype)]
        + [pltpu.SemaphoreType.DMA((nq,))] * 4,
        compiler_params=pltpu.CompilerParams(vmem_limit_bytes=VMEM_LIMIT),
    )(s)


def sum_all(s, after):
    def body(s_ref, after_ref, out_ref, slots, mine, theirs, send_sems, recv_sems):
        x, y, c = _coords()
        me = 2 * x + y
        chips = _other_chips(x, y)
        sends = [_remote(s_ref, slots.at[me], send_sems, recv_sems, k, (cx, cy, c)) for k, (cx, cy) in enumerate(chips)]
        for cp in sends:
            cp.start()
        for k, (cx, cy) in enumerate(chips):
            _remote(s_ref, slots.at[2 * cx + cy], send_sems, recv_sems, k, (cx, cy, c)).wait_recv()
        slots[me] = s_ref[...]
        acc = ((slots[0] + slots[1]) + slots[2]) + slots[3]
        mine[...] = acc
        swap = _remote(mine, theirs, send_sems, recv_sems, 3, (x, y, 1 - c))
        swap.start()
        swap.wait_recv()
        out_ref[...] = acc + theirs[...]
        swap.wait_send()
        for cp in sends:
            cp.wait_send()

    vmem = pl.BlockSpec(memory_space=pltpu.VMEM)
    return pl.pallas_call(
        body,
        name="sum_all",
        in_specs=[vmem, vmem],
        out_specs=vmem,
        out_shape=jax.ShapeDtypeStruct(s.shape, s.dtype),
        scratch_shapes=[pltpu.VMEM((N_CHIPS,) + s.shape, s.dtype), pltpu.VMEM(s.shape, s.dtype),
                        pltpu.VMEM(s.shape, s.dtype), pltpu.SemaphoreType.DMA((4,)), pltpu.SemaphoreType.DMA((4,))],
        compiler_params=pltpu.CompilerParams(vmem_limit_bytes=VMEM_LIMIT),
    )(s, after)


ADAM_LR = 0.001
ADAM_B1 = 0.9
ADAM_B2 = 0.999
ADAM_EPS = 1e-08
ADAM_WD = 0.01
ADAM_STEP = 10


def f_adamw(g, w, m, v):
    m = ADAM_B1 * m + (1.0 - ADAM_B1) * g
    v = ADAM_B2 * v + (1.0 - ADAM_B2) * jnp.square(g)
    m_hat = m / (1.0 - ADAM_B1 ** ADAM_STEP)
    v_hat = v / (1.0 - ADAM_B2 ** ADAM_STEP)
    delta = -ADAM_LR * (m_hat / (jnp.sqrt(v_hat) + ADAM_EPS) + ADAM_WD * w)
    return delta, m, v


def adamw_call(g, w, m, v, *, tm, name):
    width = g.shape[1]
    return ew_call(f_adamw, [(g, width, 0), (w, width, 0), (m, width, 0), (v, width, 0)], [], [(width, F32)] * 3,
                   tm=tm, name=name)


def adamw_halves(g_own, g_other, w, m, v, *, tm):
    _, rows, width = w.shape

    def body(go_ref, gx_ref, w_ref, m_ref, v_ref, g_ref, d_ref, nm_ref, nv_ref):
        g = jnp.where(pl.program_id(0) == lax.axis_index("c"), go_ref[...], gx_ref[...])
        delta, nm, nv = f_adamw(g, w_ref[0], m_ref[0], v_ref[0])
        g_ref[0] = g
        d_ref[0] = delta
        nm_ref[0] = nm
        nv_ref[0] = nv

    half = pl.BlockSpec((tm, width), lambda h, i: (i, 0))
    full = pl.BlockSpec((1, tm, width), lambda h, i: (h, i, 0))
    return pl.pallas_call(
        body,
        name="adamw_sharded",
        grid=(2, rows // tm),
        in_specs=[half, half, full, full, full],
        out_specs=[full] * 4,
        out_shape=[jax.ShapeDtypeStruct(w.shape, F32)] * 4,
        compiler_params=_cparams(2),
    )(g_own, g_other, w, m, v)


EARLY = ["w_in", "w_proj_a", "w_lora_w", "a_lora_w", "g_lora_w"]
LATE = ["w_ffn1", "w_ffn2", "w_proj_b", "w_out"]
SHARDED = EARLY + LATE
LORAS = ["w_lora_w", "a_lora_w", "g_lora_w"]
HALF_W = 512
PIECE_ROWS = {"w_in": 1864, "w_ffn1": 1024, "w_ffn2": 1024, "w_proj_a": 256, "w_proj_b": 256, "w_out": 256,
              "w_lora_w": 32, "a_lora_w": 32, "g_lora_w": 80}
PIECE_OFF = {"w_in": 0, "w_proj_a": 1920, "w_lora_w": 2176, "a_lora_w": 2208, "g_lora_w": 2240,
             "w_ffn1": 0, "w_ffn2": 1024, "w_proj_b": 2048, "w_out": 2304}
LO_OFF = 2320
SHARD_AXIS = {"w_in": 1, "w_proj_a": 0, "w_lora_w": 1, "a_lora_w": 1, "g_lora_w": 1, "w_proj_b": 0, "w_out": 0,
              "w_ffn1": 1, "w_ffn2": 0}
SHARD_SHAPE = {"w_in": (1024, 1864), "w_proj_a": (256, 1024), "w_lora_w": (64, 256), "a_lora_w": (64, 256),
               "g_lora_w": (160, 256), "w_proj_b": (256, 1024), "w_out": (256, 1024), "w_ffn1": (1024, 1024),
               "w_ffn2": (1024, 1024)}
SHIFT_SHARD = (2, 840)
VECTORS = ["g_mix", "sgu_ln_w", "sgu_ln_b", "w0", "a0", "k_k", "k_a", "r_k", "ln_x_w", "ln_x_b", "g_ffn", "g_final"]
SMALL = VECTORS + ["sgu_w", "sgu_b"]
SMALL_SHAPE = {**{n: (1, 1024) for n in VECTORS}, "sgu_w": (8, 128, 128), "sgu_b": (8, 128)}
WEIGHTS = ["g_mix", "w_in", "sgu_ln_w", "sgu_ln_b", "sgu_w", "sgu_b", "w_proj_a", "shift_b", "w_lora_w", "w0",
           "a_lora_w", "a0", "g_lora_w", "k_k", "k_a", "r_k", "ln_x_w", "ln_x_b", "w_proj_b", "w_out", "g_ffn",
           "w_ffn1", "w_ffn2", "g_final"]


def _size(shape):
    n = 1
    for s in shape:
        n *= s
    return n


def _pack_rows(parts, rows, dtype):
    flat = jnp.concatenate([p.reshape(-1).astype(dtype) for p in parts])
    return jnp.concatenate([flat, jnp.zeros((rows * 1024 - flat.shape[0],), dtype)]).reshape(rows, 1024)


def _unpack_rows(packed, shapes):
    flat = packed.reshape(-1)
    out, off = [], 0
    for shp in shapes:
        out.append(flat[off:off + _size(shp)].reshape(shp))
        off += _size(shp)
    return out


def _shard_of(name, full, j):
    ax = SHARD_AXIS[name]
    n = SHARD_SHAPE[name][ax]
    return lax.slice_in_dim(full, j * n, (j + 1) * n, axis=ax)


def _pad_cols(z, n):
    return jnp.concatenate([z, jnp.zeros((z.shape[0], n - z.shape[1]), z.dtype)], axis=1)


def _row_form(name, s):
    return s.T if name == "w_in" else s


def _half_piece(name, rf, h):
    if name in LORAS:
        r = PIECE_ROWS[name]
        return _pad_cols(rf[h * r:(h + 1) * r], HALF_W)
    return rf[:, HALF_W * h:HALF_W * (h + 1)]


def _pack_half(group, rf_fn, h, dtype, tail=()):
    parts, pos, rows = [], 0, PACK_ROWS
    for n in group:
        if PIECE_OFF[n] > pos:
            parts.append(jnp.zeros((PIECE_OFF[n] - pos, HALF_W), dtype))
        parts.append(_half_piece(n, rf_fn(n), h).astype(dtype))
        pos = PIECE_OFF[n] + PIECE_ROWS[n]
    for t in tail:
        parts.append(t)
        pos += t.shape[0]
    parts.append(jnp.zeros((rows - pos, HALF_W), dtype))
    return jnp.concatenate(parts, axis=0)


def _piece(pack, name):
    return pack[PIECE_OFF[name]:PIECE_OFF[name] + PIECE_ROWS[name]]


def _join_halves(name, p0, p1):
    if name in LORAS:
        return jnp.concatenate([p0[:, :SHARD_SHAPE[name][1]], p1[:, :SHARD_SHAPE[name][1]]], axis=0)
    return jnp.concatenate([p0, p1], axis=1)


def _grad_row_form(name, full, j):
    if name == "w_in":
        return full[SHARD_SHAPE[name][1] * j:SHARD_SHAPE[name][1] * (j + 1)]
    return _shard_of(name, full, j)


def adamw_weight(name, g_own, g_other, w, m, v):
    rows, width = w.shape
    if name in LORAS:
        tm = PIECE_ROWS[name]
        grid = (2, 1)
        native = pl.BlockSpec((tm, width), lambda h, i: (h, 0))
    elif name == "w_in":
        tm, lanes = rows, 128
        grid = (2, HALF_W // lanes)
        native = pl.BlockSpec((tm, lanes), lambda h, i: (0, h * (HALF_W // lanes) + i))
    else:
        tm = 128
        grid = (2, rows // tm)
        native = pl.BlockSpec((tm, HALF_W), lambda h, i: (i, h))
    off = PIECE_OFF[name] // tm
    if name == "w_in":
        packed = pl.BlockSpec((tm, 128), lambda h, i: (0, i))
    else:
        packed = pl.BlockSpec((tm, HALF_W), lambda h, i: (off + i, 0))

    def body(go_ref, gx_ref, w_ref, m_ref, v_ref, g_ref, d_ref, nm_ref, nv_ref):
        g = jnp.where(pl.program_id(0) == lax.axis_index("c"), go_ref[...], gx_ref[...])[:, :w_ref.shape[1]]
        delta, nm, nv = f_adamw(g, w_ref[...], m_ref[...], v_ref[...])
        g_ref[...] = g
        d_ref[...] = delta
        nm_ref[...] = nm
        nv_ref[...] = nv

    return pl.pallas_call(
        body,
        name="adamw_" + name,
        grid=grid,
        in_specs=[packed, packed, native, native, native],
        out_specs=[native] * 4,
        out_shape=[jax.ShapeDtypeStruct(w.shape, F32)] * 4,
        compiler_params=_cparams(2),
    )(g_own, g_other, w, m, v)


def kernel(x, g_mix, w_in, sgu_ln_w, sgu_ln_b, sgu_w, sgu_b, w_proj_a, shift_b, w_lora_w, w0, a_lora_w, a0, g_lora_w, k_k, k_a, r_k, ln_x_w, ln_x_b, w_proj_b, w_out, g_ffn, w_ffn1, w_ffn2, g_final, loss_target, m_g_mix, m_w_in, m_sgu_ln_w, m_sgu_ln_b, m_sgu_w, m_sgu_b, m_w_proj_a, m_shift_b, m_w_lora_w, m_w0, m_a_lora_w, m_a0, m_g_lora_w, m_k_k, m_k_a, m_r_k, m_ln_x_w, m_ln_x_b, m_w_proj_b, m_w_out, m_g_ffn, m_w_ffn1, m_w_ffn2, m_g_final, v_g_mix, v_w_in, v_sgu_ln_w, v_sgu_ln_b, v_sgu_w, v_sgu_b, v_w_proj_a, v_shift_b, v_w_lora_w, v_w0, v_a_lora_w, v_a0, v_g_lora_w, v_k_k, v_k_a, v_r_k, v_ln_x_w, v_ln_x_b, v_w_proj_b, v_w_out, v_g_ffn, v_w_ffn1, v_w_ffn2, v_g_final):
    given = dict(zip(WEIGHTS, (g_mix, w_in, sgu_ln_w, sgu_ln_b, sgu_w, sgu_b, w_proj_a, shift_b, w_lora_w, w0, a_lora_w, a0, g_lora_w, k_k, k_a, r_k, ln_x_w, ln_x_b, w_proj_b, w_out, g_ffn, w_ffn1, w_ffn2, g_final)))
    mom_m = dict(zip(WEIGHTS, (m_g_mix, m_w_in, m_sgu_ln_w, m_sgu_ln_b, m_sgu_w, m_sgu_b, m_w_proj_a, m_shift_b, m_w_lora_w, m_w0, m_a_lora_w, m_a0, m_g_lora_w, m_k_k, m_k_a, m_r_k, m_ln_x_w, m_ln_x_b, m_w_proj_b, m_w_out, m_g_ffn, m_w_ffn1, m_w_ffn2, m_g_final)))
    mom_v = dict(zip(WEIGHTS, (v_g_mix, v_w_in, v_sgu_ln_w, v_sgu_ln_b, v_sgu_w, v_sgu_b, v_w_proj_a, v_shift_b, v_w_lora_w, v_w0, v_a_lora_w, v_a0, v_g_lora_w, v_k_k, v_k_a, v_r_k, v_ln_x_w, v_ln_x_b, v_w_proj_b, v_w_out, v_g_ffn, v_w_ffn1, v_w_ffn2, v_g_final)))
    chip = 2 * lax.axis_index("x") + lax.axis_index("y")

    def local_block(tree, n):
        return tree[n] if n == "g_final" else tree[n][0]

    sb = local_block(given, "shift_b")
    lo_part = lambda z: (z - z.astype(BF16).astype(F32)).astype(BF16)
    row_form = lambda tree: (lambda n: _row_form(n, local_block(tree, n)))
    tile16 = lambda z: jnp.pad(z, ((0, 16 - z.shape[0]), (0, HALF_W - z.shape[1])))
    sb_tiles = [tile16(f(sb[:, lanes])) for f in (lambda z: z.astype(BF16), lo_part)
                for lanes in (slice(0, HALF_W), slice(HALF_W, None))]
    tails = [[_half_piece(n, lo_part(local_block(given, n)), h) for n in LORAS] + sb_tiles for h in range(2)]
    pack_w = jnp.stack([_pack_half(EARLY, row_form(given), h, BF16, tails[h]) for h in range(2)])
    gathered, gathered_token = gather_shards(pack_w)
    gathered = lax.dynamic_update_index_in_dim(gathered, pack_w, chip, 0)
    pack_late = jnp.stack([_pack_half(LATE, row_form(given), h, BF16) for h in range(2)])
    *late_state, late_token = gather_start(pack_late, gathered_token)

    def whole(group, got, own):
        half = lambda n, j, h: jnp.where(chip == j, _piece(own[h], n), _piece(got[j, h], n))
        shard = lambda n, j: _join_halves(n, half(n, j, 0), half(n, j, 1))
        return {n: jnp.concatenate([shard(n, j) for j in range(N_CHIPS)],
                                   axis=0 if n == "w_in" else SHARD_AXIS[n]) for n in group}

    w = whole(EARLY, gathered, pack_w)
    late_weights = lambda after: whole(LATE, gather_forward(gather_wait(*late_state, after)), pack_late)
    off = LO_OFF
    for n in LORAS:
        r, cols = PIECE_ROWS[n], SHARD_SHAPE[n][1]
        lo = jnp.concatenate([jnp.concatenate([gathered[j, 0, off:off + r, :cols], gathered[j, 1, off:off + r, :cols]],
                                              axis=0) for j in range(N_CHIPS)], axis=1)
        w[n] = w[n].astype(F32) + lo.astype(F32)
        off += r
    sb_tile = lambda j, t, lanes: gathered[j, 0, off + 16 * t:off + 16 * t + 2, :lanes].astype(F32)
    rest = SHIFT_SHARD[1] - HALF_W
    w["shift_b"] = jnp.concatenate(
        [jnp.concatenate([sb_tile(j, 0, HALF_W) + sb_tile(j, 2, HALF_W), sb_tile(j, 1, rest) + sb_tile(j, 3, rest)],
                         axis=1) for j in range(N_CHIPS)], axis=1)
    for n in SMALL:
        w[n] = local_block(given, n).reshape(SMALL_SHAPE[n])

    def pair_start(g_pack, tag):
        *state, token = reduce_pair_start(g_pack, tag)
        return state, token

    def pair_finish(state, after, tag):
        return pair_sum(*reduce_pair_wait(*state, after, tag), tag, tm=PACK_TILE)

    pack_early = lambda g: jnp.stack([jnp.stack([_pack_half(EARLY, lambda n: _grad_row_form(n, g[n], j), h, F32)
                                                 for h in range(2)]) for j in range(N_CHIPS)])
    loss, grad_x, grads, (late_part, late_slots), early_state = local_step(
        x[0], loss_target[0], w, late_token, late_weights, pair_start, pair_finish, pack_early)

    early_part, early_part16 = pair_finish(early_state, grad_x, "early")
    s_pack = _pack_rows([grads[n] for n in SMALL] + [grads["shift_b"], loss.reshape(1, 1)], SMALL_ROWS, F32)
    sends, recvs, part_thru, slots_thru, token = reduce_chips_start(early_part16)
    my_chip = lambda: 2 * lax.axis_index("x") + lax.axis_index("y")
    out_g, out_d, out_m, out_v = {}, {}, {}, {}

    def finish(group, tag, part, slots):
        half_sum = sum_with_own(part, slots, my_chip, token, tm=PACK_TILE, name="chip_sum_" + tag)
        other_half = exchange_halves(half_sum, tag)
        for n in group:
            res = adamw_weight(n, half_sum, other_half,
                               *[_row_form(n, local_block(t, n)) for t in (given, mom_m, mom_v)])
            for tree, z in zip((out_g, out_d, out_m, out_v), res):
                tree[n] = _row_form(n, z)

    finish(LATE, "late", late_part, late_slots)

    small_shapes = [SMALL_SHAPE[n] for n in SMALL]
    g_small = sum_all(s_pack, token)
    w_small = _pack_rows([local_block(given, n) for n in SMALL], SMALL_ROWS, F32)
    m_small = _pack_rows([local_block(mom_m, n) for n in SMALL], SMALL_ROWS, F32)
    v_small = _pack_rows([local_block(mom_v, n) for n in SMALL], SMALL_ROWS, F32)
    d_small, nm_small, nv_small = adamw_call(g_small, w_small, m_small, v_small, tm=SMALL_ROWS, name="adamw_small")
    *g_parts, loss = _unpack_rows(g_small, small_shapes + [(2, N_RWKV), ()])
    out_g.update(zip(SMALL, g_parts[:-1]))
    out_d.update(zip(SMALL, _unpack_rows(d_small, small_shapes)))
    out_m.update(zip(SMALL, _unpack_rows(nm_small, small_shapes)))
    out_v.update(zip(SMALL, _unpack_rows(nv_small, small_shapes)))
    g_sb = lax.dynamic_slice_in_dim(g_parts[-1], chip * SHIFT_SHARD[1], SHIFT_SHARD[1], axis=1)
    sb_args = [_pack_rows([z], 8, F32) for z in (g_sb, sb, local_block(mom_m, "shift_b"), local_block(mom_v, "shift_b"))]
    sb_res = adamw_call(*sb_args, tm=8, name="adamw_shift_b")
    out_g["shift_b"] = g_sb
    for tree, res in zip((out_d, out_m, out_v), sb_res):
        tree["shift_b"] = _unpack_rows(res, [SHIFT_SHARD])[0]

    after = (out_v["w_out"], nv_small, sb_res[2])
    early_slots = reduce_chips_wait(sends, recvs, part_thru, slots_thru, jnp.concatenate([z.reshape(-1)[:8] for z in after]))
    finish(EARLY, "early", early_part, early_slots)

    def block_of(tree, n):
        return tree[n].reshape(given[n].shape)

    return (loss, grad_x[None], *[block_of(out_g, n) for n in WEIGHTS], *[block_of(out_d, n) for n in WEIGHTS],
            *[block_of(out_m, n) for n in WEIGHTS], *[block_of(out_v, n) for n in WEIGHTS])
```

```python
import functools

import jax
import jax.numpy as jnp
from jax import lax
from jax.experimental import pallas as pl
from jax.experimental.pallas import tpu as pltpu

F32 = jnp.float32
BF16 = jnp.bfloat16

D_MODEL = 1024
N_HEADS = 16
HEAD = 64
SCAN_CHUNK = 64

VMEM_LIMIT = 56 * 1024 * 1024


_BDIMS = {
    "nn": (((2,), (1,)), ((0,), (0,))),
    "nt": (((2,), (2,)), ((0,), (0,))),
    "tn": (((1,), (1,)), ((0,), (0,))),
}


def _raw_bdot(x, y, mode, fine):
    if fine:
        return lax.dot_general(x, y, _BDIMS[mode], precision=lax.Precision.HIGH, preferred_element_type=F32)
    return lax.dot_general(x.astype(BF16), y.astype(BF16), _BDIMS[mode], preferred_element_type=F32)


@functools.partial(jax.custom_vjp, nondiff_argnums=(2, 3))
def bdot(x, y, mode, fine=True):
    return _raw_bdot(x, y, mode, fine)


def _bdot_fwd(x, y, mode, fine):
    return _raw_bdot(x, y, mode, fine), (x, y)


def _bdot_bwd(mode, fine, res, g):
    x, y = res
    if mode == "nn":
        return bdot(g, y, "nt", fine), bdot(x, g, "tn", fine)
    if mode == "nt":
        return bdot(g, y, "nn", fine), bdot(g, x, "tn", fine)
    return bdot(y, g, "nt", fine), bdot(x, g, "nn", fine)


bdot.defvjp(_bdot_fwd, _bdot_bwd)


def _scan_chunk(S0, r, lw, k, v, a, b):
    nh, lc, _ = r.shape
    ti = lax.broadcasted_iota(jnp.int32, (lc, lc), 0)
    si = lax.broadcasted_iota(jnp.int32, (lc, lc), 1)
    incl = (si <= ti).astype(F32)
    strict = (si < ti).astype(F32)
    eye = (si == ti).astype(F32)
    cl = bdot(jnp.broadcast_to(incl, (nh, lc, lc)), lw, "nn")
    cl_last = cl[:, lc - 1:lc, :]
    g_last = jnp.exp(cl_last - cl)
    at = a * jnp.exp(cl - lw)
    bt = b * jnp.exp(-cl)
    kt = k * jnp.exp(-cl)
    rt = r * jnp.exp(cl)
    ar = jnp.concatenate([at, rt], axis=1)
    ar_b = bdot(ar, bt, "nt", False)
    ar_k = bdot(ar, kt, "nt", False)
    m_ab, m_rb = ar_b[:, :lc] * strict, ar_b[:, lc:] * incl
    m_ak, m_rk = ar_k[:, :lc] * strict, ar_k[:, lc:] * incl
    x = eye + m_ab
    p = bdot(m_ab, m_ab, "nn", False)
    n = 2
    while n * 2 < lc:
        px = bdot(jnp.concatenate([p, x], axis=1), p, "nn", False)
        p = px[:, :lc]
        x = x + px[:, lc:]
        n *= 2
    x = x + bdot(x, p, "nn", False)
    ar_s = bdot(ar, S0, "nt", False)
    akrk_v = bdot(jnp.concatenate([m_ak, m_rk], axis=1), v, "nn", False)
    u = bdot(x, ar_s[:, :lc] + akrk_v[:, :lc], "nn", False)
    o = ar_s[:, lc:] + bdot(m_rb, u, "nn", False) + akrk_v[:, lc:]
    s_last = S0 * jnp.exp(cl_last) + bdot(jnp.concatenate([u, v], axis=1),
                                          jnp.concatenate([b * g_last, k * g_last], axis=1), "tn", False)
    return o, s_last


def _split_heads(z):
    return jnp.stack([z[:, HEAD * h:HEAD * (h + 1)] for h in range(N_HEADS)], axis=0)


def _merge_heads(z):
    return jnp.concatenate([z[h] for h in range(N_HEADS)], axis=1)


def _scan_specs(t, ops, rev):
    nc = t // SCAN_CHUNK
    row = (lambda c: nc - 1 - c) if rev else (lambda c: c)
    specs = [pl.BlockSpec((SCAN_CHUNK, D_MODEL), lambda c, cb=cb: (row(c), cb)) for _, cb in ops]
    state = pl.BlockSpec((1, N_HEADS, HEAD, HEAD), lambda c: (row(c), 0, 0, 0))
    return nc, specs, state


def scan_fwd(ops):
    t = ops[0][0].shape[0]
    nc, specs, state = _scan_specs(t, ops, False)

    def body(r_ref, lw_ref, k_ref, v_ref, a_ref, b_ref, o_ref, s0_ref, s_scr):
        @pl.when(pl.program_id(0) == 0)
        def _():
            s_scr[...] = jnp.zeros_like(s_scr)

        s0 = s_scr[...]
        s0_ref[0] = s0
        o, s_last = _scan_chunk(s0, *[_split_heads(z[...]) for z in (r_ref, lw_ref, k_ref, v_ref, a_ref, b_ref)])
        o_ref[...] = _merge_heads(o)
        s_scr[...] = s_last

    return pl.pallas_call(
        body,
        name="scan_fwd",
        grid=(nc,),
        in_specs=specs,
        out_specs=[pl.BlockSpec((SCAN_CHUNK, D_MODEL), lambda c: (c, 0)), state],
        out_shape=[jax.ShapeDtypeStruct((t, D_MODEL), F32), jax.ShapeDtypeStruct((nc, N_HEADS, HEAD, HEAD), F32)],
        scratch_shapes=[pltpu.VMEM((N_HEADS, HEAD, HEAD), F32)],
        compiler_params=_cparams(1),
    )(*[a for a, _ in ops])


def scan_bwd(ops, s0s, do, part):
    t = ops[0][0].shape[0]
    nc, specs, state = _scan_specs(t, ops + [(do, 0)], True)

    def body(r_ref, lw_ref, k_ref, v_ref, a_ref, b_ref, do_ref, s0_ref, part_ref, *rest):
        out_refs, slots_ref, ds_scr, send_sems, recv_sems = rest[:6], rest[6], rest[7], rest[8], rest[9]
        step = pl.program_id(0)
        x, y, c = _coords()
        me = 2 * x + y
        chips = _other_chips(x, y)
        sends = [_remote(part_ref.at[2 * cx + cy], slots_ref.at[me], send_sems, recv_sems, k, (cx, cy, c))
                 for k, (cx, cy) in enumerate(chips)]

        @pl.when(step == 0)
        def _():
            ds_scr[...] = jnp.zeros_like(ds_scr)
            for cp in sends:
                cp.start()

        _, vjp = jax.vjp(_scan_chunk, s0_ref[0],
                         *[_split_heads(z[...]) for z in (r_ref, lw_ref, k_ref, v_ref, a_ref, b_ref)])
        grads = vjp((_split_heads(do_ref[...]), ds_scr[...]))
        for o_ref, g in zip(out_refs, grads[1:]):
            o_ref[...] = _merge_heads(g)
        ds_scr[...] = grads[0]

        @pl.when(step == nc - 1)
        def _():
            for k, (cx, cy) in enumerate(chips):
                _remote(part_ref.at[me], slots_ref.at[2 * cx + cy], send_sems, recv_sems, k, (cx, cy, c)).wait_recv()
            for cp in sends:
                cp.wait_send()

    return pl.pallas_call(
        body,
        name="scan_bwd",
        grid=(nc,),
        in_specs=specs + [state, _ANY],
        out_specs=[pl.BlockSpec((SCAN_CHUNK, D_MODEL), lambda c: (nc - 1 - c, 0))] * 6 + [_ANY],
        out_shape=[jax.ShapeDtypeStruct((t, D_MODEL), F32)] * 6 + [jax.ShapeDtypeStruct(part.shape, part.dtype)],
        scratch_shapes=[pltpu.VMEM((N_HEADS, HEAD, HEAD), F32), pltpu.SemaphoreType.DMA((3,)),
                        pltpu.SemaphoreType.DMA((3,))],
        compiler_params=_cparams(1),
    )(*[a for a, _ in ops], do, s0s, part)


_MDIMS = {
    "nn": (((1,), (0,)), ((), ())),
    "nt": (((1,), (1,)), ((), ())),
    "tn": (((0,), (0,)), ((), ())),
}


def _raw_mdot(x, y, mode, exact):
    if exact:
        return lax.dot_general(x, y, _MDIMS[mode], precision=lax.Precision.HIGH, preferred_element_type=F32)
    return lax.dot_general(x.astype(BF16), y.astype(BF16), _MDIMS[mode], preferred_element_type=F32)


@functools.partial(jax.custom_vjp, nondiff_argnums=(2, 3))
def mdot(x, y, mode, exact):
    return _raw_mdot(x, y, mode, exact)


def _mdot_fwd(x, y, mode, exact):
    return _raw_mdot(x, y, mode, exact), (x, y)


def _mdot_bwd(mode, exact, res, g):
    x, y = res
    if mode == "nn":
        return mdot(g, y, "nt", exact), mdot(x, g, "tn", exact)
    if mode == "nt":
        return mdot(g, y, "nn", exact), mdot(g, x, "tn", exact)
    return mdot(y, g, "nt", exact), mdot(x, g, "nn", exact)


mdot.defvjp(_mdot_fwd, _mdot_bwd)


def _seg_ones():
    i = lax.broadcasted_iota(jnp.int32, (256, 256), 0) // HEAD
    j = lax.broadcasted_iota(jnp.int32, (256, 256), 1) // HEAD
    return (i == j).astype(BF16)


@jax.custom_vjp
def segsum(x):
    bd = _seg_ones()
    hi = x.astype(BF16)
    lo = (x - hi.astype(F32)).astype(BF16)
    cols = []
    for j in range(x.shape[1] // 256):
        sl = slice(256 * j, 256 * (j + 1))
        cols.append(jnp.dot(hi[:, sl], bd, preferred_element_type=F32)
                    + jnp.dot(lo[:, sl], bd, preferred_element_type=F32))
    return jnp.concatenate(cols, axis=1)


segsum.defvjp(lambda x: (segsum(x), None), lambda _, g: (segsum(g),))


NORM_EPS = 1e-6
LN_EPS = 1e-5
GN_EPS = 64e-5
SGU_CHUNK = 128
SGU_GROUPS = 8


def _rms(x, g):
    return x * lax.rsqrt(jnp.mean(x * x, axis=-1, keepdims=True) + NORM_EPS) * g


def f_norm_in(x, g):
    return _rms(x, g), x


def f_sgu(p, ln_w, ln_b, sw, sbt):
    tm = p.shape[0]
    z = 0.5 * p * (1.0 + lax.erf(p * 0.7071067811865476))
    u, v = z[:, :D_MODEL], z[:, D_MODEL:]
    mu = jnp.mean(v, axis=-1, keepdims=True)
    d = v - mu
    vn = d * lax.rsqrt(jnp.mean(d * d, axis=-1, keepdims=True) + LN_EPS) * ln_w + ln_b
    ii = lax.broadcasted_iota(jnp.int32, (SGU_CHUNK, SGU_CHUNK), 0)
    jj = lax.broadcasted_iota(jnp.int32, (SGU_CHUNK, SGU_CHUNK), 1)
    mask = (jj <= ii).astype(F32)
    gi = lax.broadcasted_iota(jnp.int32, (SGU_GROUPS, D_MODEL), 0)
    ci = lax.broadcasted_iota(jnp.int32, (SGU_GROUPS, D_MODEL), 1) // SGU_CHUNK
    bias = mdot(sbt, (gi == ci).astype(F32), "nn", True)
    rows = []
    for c in range(tm // SGU_CHUNK):
        cols = []
        for g in range(SGU_GROUPS):
            blk = vn[c * SGU_CHUNK:(c + 1) * SGU_CHUNK, g * SGU_CHUNK:(g + 1) * SGU_CHUNK]
            cols.append(mdot(sw[g] * mask, blk, "nn", False))
        rows.append(jnp.concatenate(cols, axis=1) + bias)
    return (u * jnp.concatenate(rows, axis=0),)


def _softplus(x):
    return jnp.maximum(x, 0.0) + jnp.log1p(jnp.exp(-jnp.abs(x)))


def f_pre(q, wl, w0, al, a0, gl, k_k, k_a):
    qr, qk, qv, ql = q[:, :1024], q[:, 1024:2048], q[:, 2048:3072], q[:, 3072:]
    return _f_pre(qr, qk, qv, ql, wl, w0, al, a0, gl, k_k, k_a)


def _f_pre(qr, qk, qv, ql, wl, w0, al, a0, gl, k_k, k_a):
    xw, xa, xg = ql[:, :128], ql[:, 128:256], ql[:, 256:512]
    wr = -_softplus(-(w0 + mdot(jnp.tanh(xw), wl, "nn", False))) - 0.5
    lw = -jnp.exp(wr)
    aa = jax.nn.sigmoid(a0 + mdot(xa, al, "nn", False))
    g = mdot(jax.nn.sigmoid(xg), gl, "nn", False)
    kkr = qk * k_k
    kk = kkr / jnp.maximum(jnp.sqrt(segsum(kkr * kkr)), 1e-12)
    kp = qk * (1.0 + (aa - 1.0) * k_a)
    return qr, lw, kp, qv, -kk, kk * aa, g, qr, kp, qv


def f_post(o, r, kp, v, g, lnw, lnb, rk):
    mu = segsum(o) * (1.0 / HEAD)
    d = o - mu
    gn = d * lax.rsqrt(segsum(d * d) * (1.0 / HEAD) + GN_EPS)
    return ((gn * lnw + lnb + segsum(r * kp * rk) * v) * g,)


def f_mix(ya, yb, ga, gb):
    return (jax.nn.sigmoid(ga) * ya + jax.nn.sigmoid(gb) * yb,)


def f_ffn_in(h1, g):
    return _rms(h1, g), h1


def f_final(h1, m3, tgt, g):
    y = _rms(h1 + m3, g)
    err = jnp.square(y - tgt)
    return 0.5 * jnp.sum(jnp.mean(err, axis=-1))


def _cparams(n_grid):
    return pltpu.CompilerParams(dimension_semantics=("arbitrary",) * n_grid, vmem_limit_bytes=VMEM_LIMIT)


def _tile_spec(tm, w, cb):
    return pl.BlockSpec((tm, w), lambda i: (i, cb))


def _const_spec(c):
    nd = c.ndim
    return pl.BlockSpec(c.shape, lambda i: (0,) * nd)


def ew_call(fn, tiled, consts, outs, *, tm, name):
    t = tiled[0][0].shape[0]
    n_t, n_c = len(tiled), len(consts)

    def body(*refs):
        tv = [r[...].astype(F32) for r in refs[:n_t]]
        cv = [r[...] for r in refs[n_t:n_t + n_c]]
        res = fn(*tv, *cv)
        for o_ref, val in zip(refs[n_t + n_c:], res):
            o_ref[...] = val.astype(o_ref.dtype)

    return pl.pallas_call(
        body,
        name=name,
        grid=(t // tm,),
        in_specs=[_tile_spec(tm, w, cb) for _, w, cb in tiled] + [_const_spec(c) for c in consts],
        out_specs=[_tile_spec(tm, w, 0) for w, _ in outs],
        out_shape=[jax.ShapeDtypeStruct((t, w), dt) for w, dt in outs],
        compiler_params=_cparams(1),
    )(*[a for a, _, _ in tiled], *consts)


def ew_vjp_call(fn, tiled, consts, cots, d_tiled, d_consts, *, tm, name):
    t = tiled[0][0].shape[0]
    n_t, n_c, n_g = len(tiled), len(consts), len(cots)
    dt_list = [(i, dt) for i, dts in enumerate(d_tiled) for dt in dts]
    dc_list = [i for i, want in enumerate(d_consts) if want]

    def body(*refs):
        tv = [r[...].astype(F32) for r in refs[:n_t]]
        cv = [r[...] for r in refs[n_t:n_t + n_c]]
        gv = tuple(r[...].astype(F32) for r in refs[n_t + n_c:n_t + n_c + n_g])
        out_refs = refs[n_t + n_c + n_g:]
        _, vjp = jax.vjp(fn, *tv, *cv)
        grads = vjp(gv)
        for o_ref, (i, _) in zip(out_refs, dt_list):
            o_ref[...] = grads[i].astype(o_ref.dtype)
        acc_refs = out_refs[len(dt_list):]

        @pl.when(pl.program_id(0) == 0)
        def _():
            for a_ref in acc_refs:
                a_ref[...] = jnp.zeros_like(a_ref)

        for a_ref, i in zip(acc_refs, dc_list):
            a_ref[...] += grads[n_t + i]

    res = pl.pallas_call(
        body,
        name=name,
        grid=(t // tm,),
        in_specs=[_tile_spec(tm, w, cb) for _, w, cb in tiled] + [_const_spec(c) for c in consts]
        + [_tile_spec(tm, w, cb) for _, w, cb in cots],
        out_specs=[_tile_spec(tm, tiled[i][1], 0) for i, _ in dt_list] + [_const_spec(consts[i]) for i in dc_list],
        out_shape=[jax.ShapeDtypeStruct((t, tiled[i][1]), dt) for i, dt in dt_list]
        + [jax.ShapeDtypeStruct(consts[i].shape, F32) for i in dc_list],
        compiler_params=_cparams(1),
    )(*[a for a, _, _ in tiled], *consts, *[a for a, _, _ in cots])
    return res[:len(dt_list)], res[len(dt_list):]


def mm(a, b, mode, *, tm, tn, name, out_dtypes=(F32,), epi=None, extras=(), into=None):
    m = a.shape[1] if mode == "tn" else a.shape[0]
    kd = a.shape[0] if mode == "tn" else a.shape[1]
    n = b.shape[0] if mode == "nt" else b.shape[1]
    tm, tn = min(tm, m), min(tn, n)
    if mode == "nn":
        a_spec = pl.BlockSpec((tm, kd), lambda i, j: (i, 0))
        b_spec = pl.BlockSpec((kd, tn), lambda i, j: (0, j))
    elif mode == "nt":
        a_spec = pl.BlockSpec((tm, kd), lambda i, j: (i, 0))
        b_spec = pl.BlockSpec((tn, kd), lambda i, j: (j, 0))
    else:
        a_spec = pl.BlockSpec((kd, tm), lambda i, j: (0, i))
        b_spec = pl.BlockSpec((kd, tn), lambda i, j: (0, j))
    n_e = len(extras)
    o_spec = pl.BlockSpec((tm, tn), lambda i, j: (i, j))

    if into is not None:
        buf, place = into

        def body_into(a_ref, b_ref, buf_ref, o_ref):
            o_ref[0, 0] = lax.dot_general(a_ref[...].astype(BF16), b_ref[...].astype(BF16), _MDIMS[mode],
                                          preferred_element_type=F32)

        return pl.pallas_call(
            body_into,
            name=name,
            grid=(m // tm, n // tn),
            in_specs=[a_spec, b_spec, pl.BlockSpec(memory_space=pl.ANY)],
            out_specs=pl.BlockSpec((1, 1, tm, tn), lambda i, j: (*place(i, j), 0)),
            out_shape=jax.ShapeDtypeStruct(buf.shape, F32),
            input_output_aliases={2: 0},
            compiler_params=_cparams(2),
        )(a, b, buf)

    def body(a_ref, b_ref, *refs):
        c = lax.dot_general(a_ref[...].astype(BF16), b_ref[...].astype(BF16), _MDIMS[mode],
                            preferred_element_type=F32)
        res = epi(c, *[r[...] for r in refs[:n_e]]) if epi is not None else (c,)
        for o_ref, val in zip(refs[n_e:], res):
            o_ref[...] = val.astype(o_ref.dtype)

    res = pl.pallas_call(
        body,
        name=name,
        grid=(m // tm, n // tn),
        in_specs=[a_spec, b_spec] + [o_spec] * n_e,
        out_specs=[o_spec] * len(out_dtypes),
        out_shape=[jax.ShapeDtypeStruct((m, n), dt) for dt in out_dtypes],
        compiler_params=_cparams(2),
    )(a, b, *extras)
    return res if len(out_dtypes) > 1 else res[0]


RWKV_COL0 = 4096
RWKV_WIDTH = 3584
SHIFT_BLK = 512


def _shift_down(p, prev_row):
    rows = lax.broadcasted_iota(jnp.int32, p.shape, 0)
    return jnp.where(rows == 0, prev_row, pltpu.roll(p, 1, 0))


def shiftmix_fwd(p_all, sbp, *, tm):
    t = p_all.shape[0]
    tm = min(tm, t)
    c0 = RWKV_COL0 // SHIFT_BLK
    hb = tm // 8

    def body(p_ref, halo_ref, sb_ref, q_ref):
        p = p_ref[...]
        prev = jnp.where(pl.program_id(0) == 0, 0.0, halo_ref[7:8, :])
        q_ref[...] = p * sb_ref[0:1, :] + _shift_down(p, prev) * sb_ref[1:2, :]

    return pl.pallas_call(
        body,
        name="shiftmix_fwd",
        grid=(t // tm, RWKV_WIDTH // SHIFT_BLK),
        in_specs=[
            pl.BlockSpec((tm, SHIFT_BLK), lambda i, j: (i, c0 + j)),
            pl.BlockSpec((8, SHIFT_BLK), lambda i, j: (jnp.maximum(i * hb - 1, 0), c0 + j)),
            pl.BlockSpec((2, SHIFT_BLK), lambda i, j: (0, j)),
        ],
        out_specs=pl.BlockSpec((tm, SHIFT_BLK), lambda i, j: (i, j)),
        out_shape=jax.ShapeDtypeStruct((t, RWKV_WIDTH), F32),
        compiler_params=_cparams(2),
    )(p_all, p_all, sbp)


def shiftmix_bwd(dq, col0, p_all, sbp, *, tm, name):
    t, w = dq.shape
    n_i = t // tm
    hb = tm // 8
    cq = col0 // SHIFT_BLK
    cp = (RWKV_COL0 + col0) // SHIFT_BLK

    def body(dq_ref, dqn_ref, p_ref, ph_ref, sb_ref, dp_ref, dsb_ref):
        i = pl.program_id(1)
        dq_t = dq_ref[...]
        rows = lax.broadcasted_iota(jnp.int32, dq_t.shape, 0)
        nxt = jnp.where(i == n_i - 1, 0.0, dqn_ref[0:1, :])
        up = jnp.where(rows == tm - 1, nxt, pltpu.roll(dq_t, tm - 1, 0))
        dp_ref[...] = (dq_t * sb_ref[0:1, :] + up * sb_ref[1:2, :]).astype(dp_ref.dtype)
        p = p_ref[...]
        prev = jnp.where(i == 0, 0.0, ph_ref[7:8, :])
        s0 = jnp.sum(dq_t * p, axis=0, keepdims=True)
        s1 = jnp.sum(dq_t * _shift_down(p, prev), axis=0, keepdims=True)
        two = lax.broadcasted_iota(jnp.int32, (2, SHIFT_BLK), 0)

        @pl.when(i == 0)
        def _():
            dsb_ref[...] = jnp.zeros_like(dsb_ref)

        dsb_ref[...] += jnp.where(two == 0, s0, s1)

    return pl.pallas_call(
        body,
        name=name,
        grid=(w // SHIFT_BLK, n_i),
        in_specs=[
            pl.BlockSpec((tm, SHIFT_BLK), lambda j, i: (i, j)),
            pl.BlockSpec((8, SHIFT_BLK), lambda j, i: (jnp.minimum((i + 1) * hb, t // 8 - 1), j)),
            pl.BlockSpec((tm, SHIFT_BLK), lambda j, i: (i, cp + j)),
            pl.BlockSpec((8, SHIFT_BLK), lambda j, i: (jnp.maximum(i * hb - 1, 0), cp + j)),
            pl.BlockSpec((2, SHIFT_BLK), lambda j, i: (0, cq + j)),
        ],
        out_specs=[
            pl.BlockSpec((tm, SHIFT_BLK), lambda j, i: (i, j)),
            pl.BlockSpec((2, SHIFT_BLK), lambda j, i: (0, j)),
        ],
        out_shape=[jax.ShapeDtypeStruct((t, w), BF16), jax.ShapeDtypeStruct((2, w), F32)],
        compiler_params=_cparams(2),
    )(dq, dq, p_all, p_all, sbp)


def final_call(h1, m3, tgt, g_final, *, tm):
    t = h1.shape[0]

    def body(h1_ref, m3_ref, tgt_ref, g_ref, dh_ref, dhb_ref, dg_ref, loss_ref):
        loss, vjp = jax.vjp(f_final, h1_ref[...], m3_ref[...], tgt_ref[...], g_ref[...])
        dh, _, _, dg = vjp(jnp.ones((), F32))
        dh_ref[...] = dh
        dhb_ref[...] = dh.astype(BF16)

        @pl.when(pl.program_id(0) == 0)
        def _():
            dg_ref[...] = jnp.zeros_like(dg_ref)
            loss_ref[...] = jnp.zeros_like(loss_ref)

        dg_ref[...] += dg
        loss_ref[...] += jnp.full(loss_ref.shape, loss, F32)

    tile = _tile_spec(tm, D_MODEL, 0)
    return pl.pallas_call(
        body,
        name="final_loss",
        grid=(t // tm,),
        in_specs=[tile, tile, tile, _const_spec(g_final)],
        out_specs=[tile, tile, _const_spec(g_final), pl.BlockSpec((8, 128), lambda i: (0, 0))],
        out_shape=[jax.ShapeDtypeStruct((t, D_MODEL), F32), jax.ShapeDtypeStruct((t, D_MODEL), BF16),
                   jax.ShapeDtypeStruct(g_final.shape, F32), jax.ShapeDtypeStruct((8, 128), F32)],
        compiler_params=_cparams(1),
    )(h1, m3, tgt, g_final)


N_SGU = 2048
N_RWKV = 3360
LORA_W, LORA_A, LORA_G = 64, 64, 160


def _pad_rwkv_cols(z):
    zero = lambda n: jnp.zeros(z.shape[:-1] + (n,), z.dtype)
    return jnp.concatenate([z[..., :3072], z[..., 3072:3136], zero(64), z[..., 3136:3200], zero(64),
                            z[..., 3200:3360], zero(96)], axis=-1)


def _unpad_rwkv_cols(z):
    return jnp.concatenate([z[..., :3072], z[..., 3072:3136], z[..., 3200:3264], z[..., 3328:3488]], axis=-1)


def _pad_win_rows(wt):
    z = wt[N_SGU:N_SGU + N_RWKV]
    zero = lambda n: jnp.zeros((n, wt.shape[1]), wt.dtype)
    return jnp.concatenate([wt[:N_SGU], wt[N_SGU + N_RWKV:], z[:3072], z[3072:3136], zero(64), z[3136:3200], zero(64),
                            z[3200:3360], zero(96)], axis=0)


def _unpad_win_rows(wt):
    z = wt[RWKV_COL0:]
    return jnp.concatenate([wt[:N_SGU], z[:3072], z[3072:3136], z[3200:3264], z[3328:3488], wt[N_SGU:RWKV_COL0]],
                           axis=0)


def _pad_rows(w, n):
    return jnp.concatenate([w, jnp.zeros((n - w.shape[0],) + w.shape[1:], w.dtype)], axis=0)


def _relu2_epi(c):
    return c, jnp.square(jnp.maximum(c, 0.0))


def _relu2_bwd_epi(c, hid):
    return (c * (2.0 * jnp.maximum(hid.astype(F32), 0.0)),)


def _add_epi(c, x):
    return (c + x,)


def _pre_fwd(*args):
    res = f_pre(*args)
    return res[1], res[2], res[4], res[5], res[6]


def local_step(x, tgt, w, late_token, late_weights, pair_start, pair_finish, pack_early):
    d = D_MODEL
    win_pt = _pad_win_rows(w["w_in"])
    sbp = _pad_rwkv_cols(w["shift_b"])
    wl = _pad_rows(w["w_lora_w"], 128)
    al = _pad_rows(w["a_lora_w"], 128)
    gl = _pad_rows(w["g_lora_w"], 256)
    sbt = w["sgu_b"].T

    (a_bf,) = ew_call(lambda x_, g_: (f_norm_in(x_, g_)[0],), [(x, d, 0)], [w["g_mix"] + late_token[:1, :1]],
                      [(d, BF16)], tm=256, name="norm_in")
    p_all = mm(a_bf, win_pt, "nt", tm=2048, tn=1280, name="mm_in")
    sgu_t = [(p_all, 2 * d, 0)]
    sgu_c = [w["sgu_ln_w"], w["sgu_ln_b"], w["sgu_w"], sbt]
    (s_bf,) = ew_call(f_sgu, sgu_t, sgu_c, [(d, BF16)], tm=256, name="sgu_fwd")
    ya = mm(s_bf, w["w_proj_a"], "nn", tm=512, tn=1024, name="mm_proj_a")
    q = shiftmix_fwd(p_all, sbp, tm=1024)
    pre_t = [(q, RWKV_WIDTH, 0)]
    pre_c = [wl, w["w0"], al, w["a0"], gl, w["k_k"], w["k_a"]]
    lw, kp, na, nb, g = ew_call(_pre_fwd, pre_t, pre_c, [(d, F32)] * 5, tm=256, name="rwkv_pre_fwd")
    scan_ops = [(q, 0), (lw, 0), (kp, 0), (q, 2), (na, 0), (nb, 0)]
    o, s0s = scan_fwd(scan_ops)
    w = {**w, **late_weights(o)}
    post_t = [(o, d, 0), (q, d, 0), (kp, d, 0), (q, d, 2), (g, d, 0)]
    post_c = [w["ln_x_w"], w["ln_x_b"], w["r_k"]]
    (ob_bf,) = ew_call(f_post, post_t, post_c, [(d, BF16)], tm=256, name="rwkv_post_fwd")
    yb = mm(ob_bf, w["w_proj_b"], "nn", tm=512, tn=1024, name="mm_proj_b")
    mix_t = [(ya, d, 0), (yb, d, 0), (p_all, d, 2), (p_all, d, 3)]
    (mixed_bf,) = ew_call(f_mix, mix_t, [], [(d, BF16)], tm=256, name="mix_fwd")
    h1 = mm(mixed_bf, w["w_out"], "nn", tm=512, tn=1024, name="mm_out", epi=_add_epi, extras=(x,))
    (f_bf,) = ew_call(lambda h_, g_: (f_ffn_in(h_, g_)[0],), [(h1, d, 0)], [w["g_ffn"]], [(d, BF16)], tm=256,
                      name="ffn_norm")
    hid, act_bf = mm(f_bf, w["w_ffn1"], "nn", tm=2048, tn=1024, name="mm_ffn1", out_dtypes=(BF16, BF16), epi=_relu2_epi)
    m3 = mm(act_bf, w["w_ffn2"], "nn", tm=1024, tn=512, name="mm_ffn2")
    dh2, dh2_bf, dg_final, loss = final_call(h1, m3, tgt, w["g_final"], tm=256)

    dhid_bf = mm(dh2_bf, w["w_ffn2"], "nt", tm=2048, tn=1024, name="mm_dact", out_dtypes=(BF16,), epi=_relu2_bwd_epi,
                 extras=(hid,))
    late_g = lax.empty((N_CHIPS, 2, PACK_ROWS, HALF_W), F32)
    late_g = mm(act_bf, dh2_bf, "tn", tm=512, tn=HALF_W, name="mm_dw_ffn2",
                into=(late_g, lambda i, j: (i // 2, j, PIECE_OFF["w_ffn2"] // 512 + i % 2)))
    df = mm(dhid_bf, w["w_ffn1"], "nt", tm=1024, tn=512, name="mm_df")
    late_g = mm(f_bf, dhid_bf, "tn", tm=512, tn=HALF_W, name="mm_dw_ffn1",
                into=(late_g, lambda i, j: (j // 2, j % 2, PIECE_OFF["w_ffn1"] // 512 + i)))
    (dh1, dh1_bf), (dg_ffn,) = ew_vjp_call(f_ffn_in, [(h1, d, 0)], [w["g_ffn"]], [(df, d, 0), (dh2, d, 0)],
                                           [(F32, BF16)], [True], tm=256, name="ffn_norm_bwd")
    dmixed = mm(dh1_bf, w["w_out"], "nt", tm=512, tn=1024, name="mm_dmixed")
    late_g = mm(mixed_bf, dh1_bf, "tn", tm=256, tn=HALF_W, name="mm_dw_out",
                into=(late_g, lambda i, j: (i, j, PIECE_OFF["w_out"] // 256)))
    (dya_bf, dyb_bf, dga_bf, dgb_bf), _ = ew_vjp_call(f_mix, mix_t, [], [(dmixed, d, 0)], [(BF16,)] * 4, [], tm=256,
                                                      name="mix_bwd")
    dob = mm(dyb_bf, w["w_proj_b"], "nt", tm=512, tn=1024, name="mm_dob")
    late_g = mm(ob_bf, dyb_bf, "tn", tm=256, tn=HALF_W, name="mm_dw_proj_b",
                into=(late_g, lambda i, j: (i, j, PIECE_OFF["w_proj_b"] // 256)))
    late_state, late_token = pair_start(late_g, "late")
    post_c_after = [w["ln_x_w"] + late_token[:1, :1]] + post_c[1:]
    (do, dr_p, dkp_p, dv_p, dg), (dlnx_w, dlnx_b, dr_k) = ew_vjp_call(
        f_post, post_t, post_c_after, [(dob, d, 0)], [(F32,)] * 5, [True] * 3, tm=256, name="rwkv_post_bwd")
    late_part, late_part16 = pair_finish(late_state, do, "late")
    *scan_g, late_slots = scan_bwd(scan_ops, s0s, do, late_part16)
    pre_g = [(z, d, 0) for z in scan_g] + [(dg, d, 0), (dr_p, d, 0), (dkp_p, d, 0), (dv_p, d, 0)]
    (dq,), (dwl, dw0, dal, da0, dgl, dk_k, dk_a) = ew_vjp_call(
        f_pre, pre_t, pre_c, pre_g, [(F32,)], [True] * 7, tm=256, name="rwkv_pre_bwd")
    dp_rwkv, dsb = shiftmix_bwd(dq, 0, p_all, sbp, tm=512, name="shiftmix_bwd")
    ds = mm(dya_bf, w["w_proj_a"], "nt", tm=512, tn=1024, name="mm_ds")
    d_proj_a = mm(s_bf, dya_bf, "tn", tm=512, tn=1024, name="mm_dw_proj_a")
    (dp_sgu,), (dln_w, dln_b, dsw, dsbt) = ew_vjp_call(f_sgu, sgu_t, sgu_c, [(ds, d, 0)], [(BF16,)], [True] * 4,
                                                       tm=256, name="sgu_bwd")
    dp_all = jnp.concatenate([dp_sgu, dga_bf, dgb_bf, dp_rwkv], axis=1)
    d_in_pt = mm(dp_all, a_bf, "tn", tm=1280, tn=1024, name="mm_dw_in")
    early_state, early_token = pair_start(pack_early({
        "w_in": _unpad_win_rows(d_in_pt), "w_proj_a": d_proj_a, "w_lora_w": dwl[:LORA_W], "a_lora_w": dal[:LORA_A],
        "g_lora_w": dgl[:LORA_G]}), "early")
    da = mm(dp_all, win_pt, "nn", tm=1024, tn=256, name="mm_da")
    g_mix_after = w["g_mix"] + early_token[:1, :1]
    (grad_x,), (dg_mix,) = ew_vjp_call(f_norm_in, [(x, d, 0)], [g_mix_after], [(da, d, 0), (dh1, d, 0)], [(F32,)],
                                       [True], tm=256, name="norm_in_bwd")

    grads = {
        "g_mix": dg_mix, "sgu_ln_w": dln_w, "sgu_ln_b": dln_b, "sgu_w": dsw, "sgu_b": dsbt.T,
        "shift_b": _unpad_rwkv_cols(dsb),
        "w0": dw0, "a0": da0, "k_k": dk_k, "k_a": dk_a, "r_k": dr_k, "ln_x_w": dlnx_w, "ln_x_b": dlnx_b,
        "g_ffn": dg_ffn, "g_final": dg_final,
    }
    return loss[0, 0], grad_x, grads, (late_part, late_slots), early_state


MESH = pl.DeviceIdType.MESH
N_CHIPS = 4
PACK_ROWS = 2560
PACK_TILE = 512
SMALL_ROWS = 152
_ANY = pl.BlockSpec(memory_space=pl.ANY)


def _coords():
    return lax.axis_index("x"), lax.axis_index("y"), lax.axis_index("c")


def _other_chips(x, y):
    return [(1 - x, y), (x, 1 - y), (1 - x, 1 - y)]


def _remote(src, dst, send_sems, recv_sems, k, to):
    return pltpu.make_async_remote_copy(src_ref=src, dst_ref=dst, send_sem=send_sems.at[k], recv_sem=recv_sems.at[k],
                                        device_id=to, device_id_type=MESH)


def gather_shards(pack):
    def body(src_ref, out_ref, token, send_sems, recv_sems):
        x, y, c = _coords()
        me = 2 * x + y
        sib = (x, y, 1 - c)
        chips = _other_chips(x, y)
        first = [_remote(src_ref.at[c], out_ref.at[me, c], send_sems, recv_sems, k, (cx, cy, c))
                 for k, (cx, cy) in enumerate(chips)]
        for cp in first:
            cp.start()
        passed = []
        for k, (cx, cy) in enumerate(chips):
            j = 2 * cx + cy
            _remote(src_ref.at[c], out_ref.at[j, c], send_sems, recv_sems, k, (cx, cy, c)).wait_recv()
            fwd = _remote(out_ref.at[j, c], out_ref.at[j, c], send_sems, recv_sems, 3 + k, sib)
            fwd.start()
            passed.append(fwd)
        for k, (cx, cy) in enumerate(chips):
            j = 2 * cx + cy
            _remote(out_ref.at[j, 1 - c], out_ref.at[j, 1 - c], send_sems, recv_sems, 3 + k, sib).wait_recv()
        for cp in first + passed:
            cp.wait_send()
        token[...] = jnp.zeros_like(token)

    return pl.pallas_call(
        body,
        name="gather_shards",
        in_specs=[_ANY],
        out_specs=[_ANY, pl.BlockSpec(memory_space=pltpu.VMEM)],
        out_shape=[jax.ShapeDtypeStruct((N_CHIPS,) + pack.shape, pack.dtype), jax.ShapeDtypeStruct((8, 128), F32)],
        scratch_shapes=[pltpu.SemaphoreType.DMA((6,)), pltpu.SemaphoreType.DMA((6,))],
    )(pack)


def _gather_copies(pack_ref, all_ref, send_sems, recv_sems):
    x, y, c = _coords()
    me = 2 * x + y
    return [(_remote(pack_ref.at[c], all_ref.at[me, c], send_sems, recv_sems, k, (cx, cy, c)),
             _remote(pack_ref.at[c], all_ref.at[2 * cx + cy, c], send_sems, recv_sems, k, (cx, cy, c)))
            for k, (cx, cy) in enumerate(_other_chips(x, y))]


_HBM = pl.BlockSpec(memory_space=pltpu.HBM)
_SEM = pl.BlockSpec(memory_space=pltpu.SEMAPHORE)
_SIDE_EFFECT = pltpu.SideEffectType.DATAFLOW_SIDE_EFFECTING


def split_start(name, copies, n, src, land_shape, after=None):
    def body(src_ref, land_ref, *refs):
        send_sems, recv_sems, token = refs[-5], refs[-4], refs[-1]
        for send, _ in copies(src_ref, land_ref, send_sems, recv_sems):
            send.start()
        token[...] = jnp.zeros_like(token)

    extra = () if after is None else (after,)
    *state, token = pl.pallas_call(
        body,
        name=name,
        out_shape=(pltpu.SemaphoreType.DMA((n,)), pltpu.SemaphoreType.DMA((n,)), pltpu.HBM(src.shape, src.dtype),
                   pltpu.HBM(land_shape, src.dtype), jax.ShapeDtypeStruct((8, 128), F32)),
        in_specs=(_HBM, _HBM) + (pl.BlockSpec(memory_space=pl.ANY),) * len(extra),
        out_specs=(_SEM, _SEM, _HBM, _HBM, pl.BlockSpec(memory_space=pltpu.VMEM)),
        input_output_aliases={0: 2, 1: 3},
        compiler_params=pltpu.CompilerParams(has_side_effects=_SIDE_EFFECT),
    )(pltpu.with_memory_space_constraint(src, pltpu.HBM),
      pltpu.with_memory_space_constraint(lax.empty(land_shape, src.dtype), pltpu.HBM), *extra)
    return state, token


def split_wait(name, copies, state, after):
    send_sems, recv_sems, src, land = state

    def body(src_ref, land_ref, send_sems, recv_sems, after_ref, src_out, land_out):
        for send, arrival in copies(src_ref, land_ref, send_sems, recv_sems):
            send.wait_send()
            arrival.wait_recv()

    return pl.pallas_call(
        body,
        name=name,
        out_shape=(pltpu.HBM(src.shape, src.dtype), pltpu.HBM(land.shape, land.dtype)),
        in_specs=(_HBM, _HBM, _SEM, _SEM, pl.BlockSpec(memory_space=pl.ANY)),
        out_specs=(_HBM, _HBM),
        input_output_aliases={0: 0, 1: 1},
        compiler_params=pltpu.CompilerParams(has_side_effects=_SIDE_EFFECT),
    )(src, land, send_sems, recv_sems, after)


def gather_forward(got):
    def body(got_ref, out_ref, send_sems, recv_sems):
        x, y, c = _coords()
        sib = (x, y, 1 - c)
        slots = [2 * cx + cy for cx, cy in _other_chips(x, y)]
        sends = [_remote(got_ref.at[j, c], out_ref.at[j, c], send_sems, recv_sems, k, sib) for k, j in enumerate(slots)]
        for cp in sends:
            cp.start()
        for k, j in enumerate(slots):
            _remote(got_ref.at[j, 1 - c], out_ref.at[j, 1 - c], send_sems, recv_sems, k, sib).wait_recv()
        for cp in sends:
            cp.wait_send()

    return pl.pallas_call(
        body,
        name="gather_forward",
        in_specs=[_ANY],
        out_specs=_ANY,
        out_shape=jax.ShapeDtypeStruct(got.shape, got.dtype),
        input_output_aliases={0: 0},
        scratch_shapes=[pltpu.SemaphoreType.DMA((3,)), pltpu.SemaphoreType.DMA((3,))],
    )(got)


def pair_sum(g, got, tag, *, tm):
    n, _, rows, width = g.shape

    def body(g0_ref, g1_ref, got_ref, out_ref, out16_ref):
        own = jnp.where(lax.axis_index("c") == 0, g0_ref[0, 0], g1_ref[0, 0])
        total = own + got_ref[0]
        out_ref[0] = total
        out16_ref[0] = total.astype(BF16)

    blk = pl.BlockSpec((1, tm, width), lambda j, i: (j, i, 0))
    return pl.pallas_call(
        body,
        name="pair_sum_" + tag,
        grid=(n, rows // tm),
        in_specs=[pl.BlockSpec((1, 1, tm, width), lambda j, i: (j, 0, i, 0)),
                  pl.BlockSpec((1, 1, tm, width), lambda j, i: (j, 1, i, 0)), blk],
        out_specs=[blk, blk],
        out_shape=[jax.ShapeDtypeStruct(got.shape, F32), jax.ShapeDtypeStruct(got.shape, BF16)],
        compiler_params=_cparams(2),
    )(g, g, got)


def _pair_copies(g_ref, got_ref, send_sems, recv_sems):
    x, y, c = _coords()
    copies = [_remote(g_ref.at[j, 1 - c], got_ref.at[j], send_sems, recv_sems, j, (x, y, 1 - c))
              for j in range(N_CHIPS)]
    return [(cp, cp) for cp in copies]


def _chip_copies(p_ref, slots_ref, send_sems, recv_sems):
    x, y, c = _coords()
    me = 2 * x + y
    return [(_remote(p_ref.at[2 * cx + cy], slots_ref.at[me], send_sems, recv_sems, k, (cx, cy, c)),
             _remote(p_ref.at[me], slots_ref.at[2 * cx + cy], send_sems, recv_sems, k, (cx, cy, c)))
            for k, (cx, cy) in enumerate(_other_chips(x, y))]


def sum_with_own(own, slots, index_fn, after, *, tm, name):
    n, rows, width = slots.shape

    def body(*refs):
        mine = index_fn()
        acc = None
        for s in range(n):
            term = jnp.where(mine == s, refs[s][0], refs[n + s][0].astype(F32))
            acc = term if acc is None else acc + term
        refs[-1][...] = acc

    slot_specs = [pl.BlockSpec((1, tm, width), lambda i, s=s: (s, i, 0)) for s in range(n)]
    return pl.pallas_call(
        body,
        name=name,
        grid=(rows // tm,),
        in_specs=slot_specs + slot_specs + [pl.BlockSpec(after.shape, lambda i: (0,) * after.ndim)],
        out_specs=pl.BlockSpec((tm, width), lambda i: (i, 0)),
        out_shape=jax.ShapeDtypeStruct((rows, width), F32),
        compiler_params=_cparams(1),
    )(*([own] * n), *([slots] * n), after)


def exchange_halves(s, tag):
    rq = PACK_TILE
    nq = s.shape[0] // rq

    def body(s_ref, out_ref, sbuf, rbuf, send_sems, recv_sems, in_sems, out_sems):
        x, y, c = _coords()
        sib = (x, y, 1 - c)
        rows = lambda q: pl.ds(q * rq, rq)
        loads = [pltpu.make_async_copy(s_ref.at[rows(q)], sbuf.at[rows(q)], in_sems.at[q]) for q in range(nq)]
        for cp in loads:
            cp.start()
        sends = []
        for q in range(nq):
            loads[q].wait()
            sends.append(_remote(sbuf.at[rows(q)], rbuf.at[rows(q)], send_sems, recv_sems, q, sib))
            sends[q].start()
        stores = []
        for q in range(nq):
            sends[q].wait_recv()
            stores.append(pltpu.make_async_copy(rbuf.at[rows(q)], out_ref.at[rows(q)], out_sems.at[q]))
            stores[q].start()
        for cp in sends:
            cp.wait_send()
        for cp in stores:
            cp.wait()

    return pl.pallas_call(
        body,
        name="exchange_halves_" + tag,
        in_specs=[_ANY],
        out_specs=_ANY,
        out_shape=jax.ShapeDtypeStruct(s.shape, s.dtype),
        scratch_shapes=[pltpu.VMEM(s.shape, s.dtype), pltpu.VMEM(s.shape, s.dtype)]
        + [pltpu.SemaphoreType.DMA((nq,))] * 4,
        compiler_params=pltpu.CompilerParams(vmem_limit_bytes=VMEM_LIMIT),
    )(s)


def sum_all(s, after):
    def body(s_ref, after_ref, out_ref, slots, mine, theirs, send_sems, recv_sems):
        x, y, c = _coords()
        me = 2 * x + y
        chips = _other_chips(x, y)
        sends = [_remote(s_ref, slots.at[me], send_sems, recv_sems, k, (cx, cy, c)) for k, (cx, cy) in enumerate(chips)]
        for cp in sends:
            cp.start()
        for k, (cx, cy) in enumerate(chips):
            _remote(s_ref, slots.at[2 * cx + cy], send_sems, recv_sems, k, (cx, cy, c)).wait_recv()
        slots[me] = s_ref[...]
        acc = ((slots[0] + slots[1]) + slots[2]) + slots[3]
        mine[...] = acc
        swap = _remote(mine, theirs, send_sems, recv_sems, 3, (x, y, 1 - c))
        swap.start()
        swap.wait_recv()
        out_ref[...] = acc + theirs[...]
        swap.wait_send()
        for cp in sends:
            cp.wait_send()

    vmem = pl.BlockSpec(memory_space=pltpu.VMEM)
    return pl.pallas_call(
        body,
        name="sum_all",
        in_specs=[vmem, vmem],
        out_specs=vmem,
        out_shape=jax.ShapeDtypeStruct(s.shape, s.dtype),
        scratch_shapes=[pltpu.VMEM((N_CHIPS,) + s.shape, s.dtype), pltpu.VMEM(s.shape, s.dtype),
                        pltpu.VMEM(s.shape, s.dtype), pltpu.SemaphoreType.DMA((4,)), pltpu.SemaphoreType.DMA((4,))],
        compiler_params=pltpu.CompilerParams(vmem_limit_bytes=VMEM_LIMIT),
    )(s, after)


ADAM_LR = 0.001
ADAM_B1 = 0.9
ADAM_B2 = 0.999
ADAM_EPS = 1e-08
ADAM_WD = 0.01
ADAM_STEP = 10


def f_adamw(g, w, m, v):
    m = ADAM_B1 * m + (1.0 - ADAM_B1) * g
    v = ADAM_B2 * v + (1.0 - ADAM_B2) * jnp.square(g)
    m_hat = m / (1.0 - ADAM_B1 ** ADAM_STEP)
    v_hat = v / (1.0 - ADAM_B2 ** ADAM_STEP)
    delta = -ADAM_LR * (m_hat / (jnp.sqrt(v_hat) + ADAM_EPS) + ADAM_WD * w)
    return delta, m, v


def adamw_call(g, w, m, v, *, tm, name):
    width = g.shape[1]
    return ew_call(f_adamw, [(g, width, 0), (w, width, 0), (m, width, 0), (v, width, 0)], [], [(width, F32)] * 3,
                   tm=tm, name=name)


EARLY = ["w_in", "w_proj_a", "w_lora_w", "a_lora_w", "g_lora_w"]
LATE = ["w_ffn1", "w_ffn2", "w_proj_b", "w_out"]
LORAS = ["w_lora_w", "a_lora_w", "g_lora_w"]
HALF_W = 512
PIECE_ROWS = {"w_in": 1864, "w_ffn1": 1024, "w_ffn2": 1024, "w_proj_a": 256, "w_proj_b": 256, "w_out": 256,
              "w_lora_w": 32, "a_lora_w": 32, "g_lora_w": 80}
PIECE_OFF = {"w_in": 0, "w_proj_a": 1920, "w_lora_w": 2176, "a_lora_w": 2208, "g_lora_w": 2240,
             "w_ffn1": 0, "w_ffn2": 1024, "w_proj_b": 2048, "w_out": 2304}
LO_OFF = 2320
SHARD_AXIS = {"w_in": 1, "w_proj_a": 0, "w_lora_w": 1, "a_lora_w": 1, "g_lora_w": 1, "w_proj_b": 0, "w_out": 0,
              "w_ffn1": 1, "w_ffn2": 0}
SHARD_SHAPE = {"w_in": (1024, 1864), "w_proj_a": (256, 1024), "w_lora_w": (64, 256), "a_lora_w": (64, 256),
               "g_lora_w": (160, 256), "w_proj_b": (256, 1024), "w_out": (256, 1024), "w_ffn1": (1024, 1024),
               "w_ffn2": (1024, 1024)}
SHIFT_SHARD = (2, 840)
VECTORS = ["g_mix", "sgu_ln_w", "sgu_ln_b", "w0", "a0", "k_k", "k_a", "r_k", "ln_x_w", "ln_x_b", "g_ffn", "g_final"]
SMALL = VECTORS + ["sgu_w", "sgu_b"]
SMALL_SHAPE = {**{n: (1, 1024) for n in VECTORS}, "sgu_w": (8, 128, 128), "sgu_b": (8, 128)}
WEIGHTS = ["g_mix", "w_in", "sgu_ln_w", "sgu_ln_b", "sgu_w", "sgu_b", "w_proj_a", "shift_b", "w_lora_w", "w0",
           "a_lora_w", "a0", "g_lora_w", "k_k", "k_a", "r_k", "ln_x_w", "ln_x_b", "w_proj_b", "w_out", "g_ffn",
           "w_ffn1", "w_ffn2", "g_final"]


def _size(shape):
    n = 1
    for s in shape:
        n *= s
    return n


def _pack_rows(parts, rows, dtype):
    flat = jnp.concatenate([p.reshape(-1).astype(dtype) for p in parts])
    return jnp.concatenate([flat, jnp.zeros((rows * 1024 - flat.shape[0],), dtype)]).reshape(rows, 1024)


def _unpack_rows(packed, shapes):
    flat = packed.reshape(-1)
    out, off = [], 0
    for shp in shapes:
        out.append(flat[off:off + _size(shp)].reshape(shp))
        off += _size(shp)
    return out


def _shard_of(name, full, j):
    ax = SHARD_AXIS[name]
    n = SHARD_SHAPE[name][ax]
    return lax.slice_in_dim(full, j * n, (j + 1) * n, axis=ax)


def _pad_cols(z, n):
    return jnp.concatenate([z, jnp.zeros((z.shape[0], n - z.shape[1]), z.dtype)], axis=1)


def _row_form(name, s):
    return s.T if name == "w_in" else s


def _half_piece(name, rf, h):
    if name in LORAS:
        r = PIECE_ROWS[name]
        return _pad_cols(rf[h * r:(h + 1) * r], HALF_W)
    return rf[:, HALF_W * h:HALF_W * (h + 1)]


def _pack_half(group, rf_fn, h, dtype, tail=()):
    parts, pos, rows = [], 0, PACK_ROWS
    for n in group:
        if PIECE_OFF[n] > pos:
            parts.append(jnp.zeros((PIECE_OFF[n] - pos, HALF_W), dtype))
        parts.append(_half_piece(n, rf_fn(n), h).astype(dtype))
        pos = PIECE_OFF[n] + PIECE_ROWS[n]
    for t in tail:
        parts.append(t)
        pos += t.shape[0]
    parts.append(jnp.zeros((rows - pos, HALF_W), dtype))
    return jnp.concatenate(parts, axis=0)


def _piece(pack, name):
    return pack[PIECE_OFF[name]:PIECE_OFF[name] + PIECE_ROWS[name]]


def _join_halves(name, p0, p1):
    if name in LORAS:
        return jnp.concatenate([p0[:, :SHARD_SHAPE[name][1]], p1[:, :SHARD_SHAPE[name][1]]], axis=0)
    return jnp.concatenate([p0, p1], axis=1)


def _grad_row_form(name, full, j):
    if name == "w_in":
        return full[SHARD_SHAPE[name][1] * j:SHARD_SHAPE[name][1] * (j + 1)]
    return _shard_of(name, full, j)


def adamw_weight(name, g_own, g_other, w, m, v):
    rows, width = w.shape
    if name in LORAS:
        tm = PIECE_ROWS[name]
        grid = (2, 1)
        native = pl.BlockSpec((tm, width), lambda h, i: (h, 0))
    elif name == "w_in":
        tm, lanes = rows, 128
        grid = (2, HALF_W // lanes)
        native = pl.BlockSpec((tm, lanes), lambda h, i: (0, h * (HALF_W // lanes) + i))
    else:
        tm = 128
        grid = (2, rows // tm)
        native = pl.BlockSpec((tm, HALF_W), lambda h, i: (i, h))
    off = PIECE_OFF[name] // tm
    if name == "w_in":
        packed = pl.BlockSpec((tm, 128), lambda h, i: (0, i))
    else:
        packed = pl.BlockSpec((tm, HALF_W), lambda h, i: (off + i, 0))

    def body(go_ref, gx_ref, w_ref, m_ref, v_ref, g_ref, d_ref, nm_ref, nv_ref):
        g = jnp.where(pl.program_id(0) == lax.axis_index("c"), go_ref[...], gx_ref[...])[:, :w_ref.shape[1]]
        delta, nm, nv = f_adamw(g, w_ref[...], m_ref[...], v_ref[...])
        g_ref[...] = g
        d_ref[...] = delta
        nm_ref[...] = nm
        nv_ref[...] = nv

    return pl.pallas_call(
        body,
        name="adamw_" + name,
        grid=grid,
        in_specs=[packed, packed, native, native, native],
        out_specs=[native] * 4,
        out_shape=[jax.ShapeDtypeStruct(w.shape, F32)] * 4,
        compiler_params=_cparams(2),
    )(g_own, g_other, w, m, v)


def kernel(x, g_mix, w_in, sgu_ln_w, sgu_ln_b, sgu_w, sgu_b, w_proj_a, shift_b, w_lora_w, w0, a_lora_w, a0, g_lora_w, k_k, k_a, r_k, ln_x_w, ln_x_b, w_proj_b, w_out, g_ffn, w_ffn1, w_ffn2, g_final, loss_target, m_g_mix, m_w_in, m_sgu_ln_w, m_sgu_ln_b, m_sgu_w, m_sgu_b, m_w_proj_a, m_shift_b, m_w_lora_w, m_w0, m_a_lora_w, m_a0, m_g_lora_w, m_k_k, m_k_a, m_r_k, m_ln_x_w, m_ln_x_b, m_w_proj_b, m_w_out, m_g_ffn, m_w_ffn1, m_w_ffn2, m_g_final, v_g_mix, v_w_in, v_sgu_ln_w, v_sgu_ln_b, v_sgu_w, v_sgu_b, v_w_proj_a, v_shift_b, v_w_lora_w, v_w0, v_a_lora_w, v_a0, v_g_lora_w, v_k_k, v_k_a, v_r_k, v_ln_x_w, v_ln_x_b, v_w_proj_b, v_w_out, v_g_ffn, v_w_ffn1, v_w_ffn2, v_g_final):
    given = dict(zip(WEIGHTS, (g_mix, w_in, sgu_ln_w, sgu_ln_b, sgu_w, sgu_b, w_proj_a, shift_b, w_lora_w, w0, a_lora_w, a0, g_lora_w, k_k, k_a, r_k, ln_x_w, ln_x_b, w_proj_b, w_out, g_ffn, w_ffn1, w_ffn2, g_final)))
    mom_m = dict(zip(WEIGHTS, (m_g_mix, m_w_in, m_sgu_ln_w, m_sgu_ln_b, m_sgu_w, m_sgu_b, m_w_proj_a, m_shift_b, m_w_lora_w, m_w0, m_a_lora_w, m_a0, m_g_lora_w, m_k_k, m_k_a, m_r_k, m_ln_x_w, m_ln_x_b, m_w_proj_b, m_w_out, m_g_ffn, m_w_ffn1, m_w_ffn2, m_g_final)))
    mom_v = dict(zip(WEIGHTS, (v_g_mix, v_w_in, v_sgu_ln_w, v_sgu_ln_b, v_sgu_w, v_sgu_b, v_w_proj_a, v_shift_b, v_w_lora_w, v_w0, v_a_lora_w, v_a0, v_g_lora_w, v_k_k, v_k_a, v_r_k, v_ln_x_w, v_ln_x_b, v_w_proj_b, v_w_out, v_g_ffn, v_w_ffn1, v_w_ffn2, v_g_final)))
    chip = 2 * lax.axis_index("x") + lax.axis_index("y")

    def local_block(tree, n):
        return tree[n] if n == "g_final" else tree[n][0]

    sb = local_block(given, "shift_b")
    lo_part = lambda z: (z - z.astype(BF16).astype(F32)).astype(BF16)
    row_form = lambda tree: (lambda n: _row_form(n, local_block(tree, n)))
    tile16 = lambda z: jnp.pad(z, ((0, 16 - z.shape[0]), (0, HALF_W - z.shape[1])))
    sb_tiles = [tile16(f(sb[:, lanes])) for f in (lambda z: z.astype(BF16), lo_part)
                for lanes in (slice(0, HALF_W), slice(HALF_W, None))]
    tails = [[_half_piece(n, lo_part(local_block(given, n)), h) for n in LORAS] + sb_tiles for h in range(2)]
    pack_w = jnp.stack([_pack_half(EARLY, row_form(given), h, BF16, tails[h]) for h in range(2)])
    gathered, gathered_token = gather_shards(pack_w)
    gathered = lax.dynamic_update_index_in_dim(gathered, pack_w, chip, 0)
    pack_late = jnp.stack([_pack_half(LATE, row_form(given), h, BF16) for h in range(2)])
    late_state, late_token = split_start("gather_start", _gather_copies, 3, pack_late, (N_CHIPS,) + pack_late.shape,
                                         gathered_token)

    def whole(group, got, own):
        half = lambda n, j, h: jnp.where(chip == j, _piece(own[h], n), _piece(got[j, h], n))
        shard = lambda n, j: _join_halves(n, half(n, j, 0), half(n, j, 1))
        return {n: jnp.concatenate([shard(n, j) for j in range(N_CHIPS)],
                                   axis=0 if n == "w_in" else SHARD_AXIS[n]) for n in group}

    w = whole(EARLY, gathered, pack_w)
    late_weights = lambda after: whole(
        LATE, gather_forward(split_wait("gather_wait", _gather_copies, late_state, after)[1]), pack_late)
    off = LO_OFF
    for n in LORAS:
        r, cols = PIECE_ROWS[n], SHARD_SHAPE[n][1]
        lo = jnp.concatenate([jnp.concatenate([gathered[j, 0, off:off + r, :cols], gathered[j, 1, off:off + r, :cols]],
                                              axis=0) for j in range(N_CHIPS)], axis=1)
        w[n] = w[n].astype(F32) + lo.astype(F32)
        off += r
    sb_tile = lambda j, t, lanes: gathered[j, 0, off + 16 * t:off + 16 * t + 2, :lanes].astype(F32)
    rest = SHIFT_SHARD[1] - HALF_W
    w["shift_b"] = jnp.concatenate(
        [jnp.concatenate([sb_tile(j, 0, HALF_W) + sb_tile(j, 2, HALF_W), sb_tile(j, 1, rest) + sb_tile(j, 3, rest)],
                         axis=1) for j in range(N_CHIPS)], axis=1)
    for n in SMALL:
        w[n] = local_block(given, n).reshape(SMALL_SHAPE[n])

    def pair_start(g_pack, tag):
        return split_start("reduce_pair_start_" + tag, _pair_copies, N_CHIPS, g_pack, (N_CHIPS,) + g_pack.shape[2:])

    def pair_finish(state, after, tag):
        return pair_sum(*split_wait("reduce_pair_wait_" + tag, _pair_copies, state, after), tag, tm=PACK_TILE)

    pack_early = lambda g: jnp.stack([jnp.stack([_pack_half(EARLY, lambda n: _grad_row_form(n, g[n], j), h, F32)
                                                 for h in range(2)]) for j in range(N_CHIPS)])
    loss, grad_x, grads, (late_part, late_slots), early_state = local_step(
        x[0], loss_target[0], w, late_token, late_weights, pair_start, pair_finish, pack_early)

    early_part, early_part16 = pair_finish(early_state, grad_x, "early")
    s_pack = _pack_rows([grads[n] for n in SMALL] + [grads["shift_b"], loss.reshape(1, 1)], SMALL_ROWS, F32)
    chips_state, token = split_start("reduce_chips_start", _chip_copies, 3, early_part16, early_part16.shape)
    my_chip = lambda: 2 * lax.axis_index("x") + lax.axis_index("y")
    out_g, out_d, out_m, out_v = {}, {}, {}, {}

    def finish(group, tag, part, slots):
        half_sum = sum_with_own(part, slots, my_chip, token, tm=PACK_TILE, name="chip_sum_" + tag)
        other_half = exchange_halves(half_sum, tag)
        for n in group:
            res = adamw_weight(n, half_sum, other_half,
                               *[_row_form(n, local_block(t, n)) for t in (given, mom_m, mom_v)])
            for tree, z in zip((out_g, out_d, out_m, out_v), res):
                tree[n] = _row_form(n, z)

    finish(LATE, "late", late_part, late_slots)

    small_shapes = [SMALL_SHAPE[n] for n in SMALL]
    g_small = sum_all(s_pack, token)
    w_small = _pack_rows([local_block(given, n) for n in SMALL], SMALL_ROWS, F32)
    m_small = _pack_rows([local_block(mom_m, n) for n in SMALL], SMALL_ROWS, F32)
    v_small = _pack_rows([local_block(mom_v, n) for n in SMALL], SMALL_ROWS, F32)
    d_small, nm_small, nv_small = adamw_call(g_small, w_small, m_small, v_small, tm=SMALL_ROWS, name="adamw_small")
    *g_parts, loss = _unpack_rows(g_small, small_shapes + [(2, N_RWKV), ()])
    out_g.update(zip(SMALL, g_parts[:-1]))
    out_d.update(zip(SMALL, _unpack_rows(d_small, small_shapes)))
    out_m.update(zip(SMALL, _unpack_rows(nm_small, small_shapes)))
    out_v.update(zip(SMALL, _unpack_rows(nv_small, small_shapes)))
    g_sb = lax.dynamic_slice_in_dim(g_parts[-1], chip * SHIFT_SHARD[1], SHIFT_SHARD[1], axis=1)
    sb_args = [_pack_rows([z], 8, F32) for z in (g_sb, sb, local_block(mom_m, "shift_b"), local_block(mom_v, "shift_b"))]
    sb_res = adamw_call(*sb_args, tm=8, name="adamw_shift_b")
    out_g["shift_b"] = g_sb
    for tree, res in zip((out_d, out_m, out_v), sb_res):
        tree["shift_b"] = _unpack_rows(res, [SHIFT_SHARD])[0]

    after = (out_v["w_out"], nv_small, sb_res[2])
    early_slots = split_wait("reduce_chips_wait", _chip_copies, chips_state,
                             jnp.concatenate([z.reshape(-1)[:8] for z in after]))[1]
    finish(EARLY, "early", early_part, early_slots)

    def block_of(tree, n):
        return tree[n].reshape(given[n].shape)

    return (loss, grad_x[None], *[block_of(out_g, n) for n in WEIGHTS], *[block_of(out_d, n) for n in WEIGHTS],
            *[block_of(out_m, n) for n in WEIGHTS], *[block_of(out_v, n) for n in WEIGHTS])
```

```python
import functools

import jax
import jax.numpy as jnp
from jax import lax
from jax.experimental import pallas as pl
from jax.experimental.pallas import tpu as pltpu

F32 = jnp.float32
BF16 = jnp.bfloat16

D_MODEL = 1024
N_HEADS = 16
HEAD = 64
SCAN_CHUNK = 64

VMEM_LIMIT = 56 * 1024 * 1024


_BDIMS = {
    "nn": (((2,), (1,)), ((0,), (0,))),
    "nt": (((2,), (2,)), ((0,), (0,))),
    "tn": (((1,), (1,)), ((0,), (0,))),
}


def _raw_bdot(x, y, mode, fine):
    if fine:
        return lax.dot_general(x, y, _BDIMS[mode], precision=lax.Precision.HIGH, preferred_element_type=F32)
    return lax.dot_general(x.astype(BF16), y.astype(BF16), _BDIMS[mode], preferred_element_type=F32)


@functools.partial(jax.custom_vjp, nondiff_argnums=(2, 3))
def bdot(x, y, mode, fine=True):
    return _raw_bdot(x, y, mode, fine)


def _bdot_fwd(x, y, mode, fine):
    return _raw_bdot(x, y, mode, fine), (x, y)


def _bdot_bwd(mode, fine, res, g):
    x, y = res
    if mode == "nn":
        return bdot(g, y, "nt", fine), bdot(x, g, "tn", fine)
    if mode == "nt":
        return bdot(g, y, "nn", fine), bdot(g, x, "tn", fine)
    return bdot(y, g, "nt", fine), bdot(x, g, "nn", fine)


bdot.defvjp(_bdot_fwd, _bdot_bwd)


def _scan_chunk(S0, r, lw, k, v, a, b):
    nh, lc, _ = r.shape
    ti = lax.broadcasted_iota(jnp.int32, (lc, lc), 0)
    si = lax.broadcasted_iota(jnp.int32, (lc, lc), 1)
    incl = (si <= ti).astype(F32)
    strict = (si < ti).astype(F32)
    eye = (si == ti).astype(F32)
    cl = bdot(jnp.broadcast_to(incl, (nh, lc, lc)), lw, "nn")
    cl_last = cl[:, lc - 1:lc, :]
    g_last = jnp.exp(cl_last - cl)
    at = a * jnp.exp(cl - lw)
    bt = b * jnp.exp(-cl)
    kt = k * jnp.exp(-cl)
    rt = r * jnp.exp(cl)
    ar = jnp.concatenate([at, rt], axis=1)
    ar_b = bdot(ar, bt, "nt", False)
    ar_k = bdot(ar, kt, "nt", False)
    m_ab, m_rb = ar_b[:, :lc] * strict, ar_b[:, lc:] * incl
    m_ak, m_rk = ar_k[:, :lc] * strict, ar_k[:, lc:] * incl
    x = eye + m_ab
    p = bdot(m_ab, m_ab, "nn", False)
    n = 2
    while n * 2 < lc:
        px = bdot(jnp.concatenate([p, x], axis=1), p, "nn", False)
        p = px[:, :lc]
        x = x + px[:, lc:]
        n *= 2
    x = x + bdot(x, p, "nn", False)
    ar_s = bdot(ar, S0, "nt", False)
    akrk_v = bdot(jnp.concatenate([m_ak, m_rk], axis=1), v, "nn", False)
    u = bdot(x, ar_s[:, :lc] + akrk_v[:, :lc], "nn", False)
    o = ar_s[:, lc:] + bdot(m_rb, u, "nn", False) + akrk_v[:, lc:]
    s_last = S0 * jnp.exp(cl_last) + bdot(jnp.concatenate([u, v], axis=1),
                                          jnp.concatenate([b * g_last, k * g_last], axis=1), "tn", False)
    return o, s_last


def _split_heads(z):
    return jnp.stack([z[:, HEAD * h:HEAD * (h + 1)] for h in range(N_HEADS)], axis=0)


def _merge_heads(z):
    return jnp.concatenate([z[h] for h in range(N_HEADS)], axis=1)


def _scan_specs(t, ops, rev):
    nc = t // SCAN_CHUNK
    row = (lambda c: nc - 1 - c) if rev else (lambda c: c)
    specs = [pl.BlockSpec((SCAN_CHUNK, D_MODEL), lambda c, cb=cb: (row(c), cb)) for _, cb in ops]
    state = pl.BlockSpec((1, N_HEADS, HEAD, HEAD), lambda c: (row(c), 0, 0, 0))
    return nc, specs, state


def scan_fwd(ops):
    t = ops[0][0].shape[0]
    nc, specs, state = _scan_specs(t, ops, False)

    def body(r_ref, lw_ref, k_ref, v_ref, a_ref, b_ref, o_ref, s0_ref, s_scr):
        @pl.when(pl.program_id(0) == 0)
        def _():
            s_scr[...] = jnp.zeros_like(s_scr)

        s0 = s_scr[...]
        s0_ref[0] = s0
        o, s_last = _scan_chunk(s0, *[_split_heads(z[...]) for z in (r_ref, lw_ref, k_ref, v_ref, a_ref, b_ref)])
        o_ref[...] = _merge_heads(o)
        s_scr[...] = s_last

    return pl.pallas_call(
        body,
        name="scan_fwd",
        grid=(nc,),
        in_specs=specs,
        out_specs=[pl.BlockSpec((SCAN_CHUNK, D_MODEL), lambda c: (c, 0)), state],
        out_shape=[jax.ShapeDtypeStruct((t, D_MODEL), F32), jax.ShapeDtypeStruct((nc, N_HEADS, HEAD, HEAD), F32)],
        scratch_shapes=[pltpu.VMEM((N_HEADS, HEAD, HEAD), F32)],
        compiler_params=_cparams(1),
    )(*[a for a, _ in ops])


def scan_bwd(ops, s0s, do, part):
    t = ops[0][0].shape[0]
    nc, specs, state = _scan_specs(t, ops + [(do, 0)], True)

    def body(r_ref, lw_ref, k_ref, v_ref, a_ref, b_ref, do_ref, s0_ref, part_ref, *rest):
        out_refs, slots_ref, ds_scr, send_sems, recv_sems = rest[:6], rest[6], rest[7], rest[8], rest[9]
        step = pl.program_id(0)
        x, y, c = _coords()
        me = 2 * x + y
        chips = _other_chips(x, y)
        sends = [_remote(part_ref.at[2 * cx + cy], slots_ref.at[me], send_sems, recv_sems, k, (cx, cy, c))
                 for k, (cx, cy) in enumerate(chips)]

        @pl.when(step == 0)
        def _():
            ds_scr[...] = jnp.zeros_like(ds_scr)
            for cp in sends:
                cp.start()

        _, vjp = jax.vjp(_scan_chunk, s0_ref[0],
                         *[_split_heads(z[...]) for z in (r_ref, lw_ref, k_ref, v_ref, a_ref, b_ref)])
        grads = vjp((_split_heads(do_ref[...]), ds_scr[...]))
        for o_ref, g in zip(out_refs, grads[1:]):
            o_ref[...] = _merge_heads(g)
        ds_scr[...] = grads[0]

        @pl.when(step == nc - 1)
        def _():
            for k, (cx, cy) in enumerate(chips):
                _remote(part_ref.at[me], slots_ref.at[2 * cx + cy], send_sems, recv_sems, k, (cx, cy, c)).wait_recv()
            for cp in sends:
                cp.wait_send()

    return pl.pallas_call(
        body,
        name="scan_bwd",
        grid=(nc,),
        in_specs=specs + [state, _ANY],
        out_specs=[pl.BlockSpec((SCAN_CHUNK, D_MODEL), lambda c: (nc - 1 - c, 0))] * 6 + [_ANY],
        out_shape=[jax.ShapeDtypeStruct((t, D_MODEL), F32)] * 6 + [jax.ShapeDtypeStruct(part.shape, part.dtype)],
        scratch_shapes=[pltpu.VMEM((N_HEADS, HEAD, HEAD), F32), pltpu.SemaphoreType.DMA((3,)),
                        pltpu.SemaphoreType.DMA((3,))],
        compiler_params=_cparams(1),
    )(*[a for a, _ in ops], do, s0s, part)


_MDIMS = {
    "nn": (((1,), (0,)), ((), ())),
    "nt": (((1,), (1,)), ((), ())),
    "tn": (((0,), (0,)), ((), ())),
}


def _raw_mdot(x, y, mode, exact):
    if exact:
        return lax.dot_general(x, y, _MDIMS[mode], precision=lax.Precision.HIGH, preferred_element_type=F32)
    return lax.dot_general(x.astype(BF16), y.astype(BF16), _MDIMS[mode], preferred_element_type=F32)


@functools.partial(jax.custom_vjp, nondiff_argnums=(2, 3))
def mdot(x, y, mode, exact):
    return _raw_mdot(x, y, mode, exact)


def _mdot_fwd(x, y, mode, exact):
    return _raw_mdot(x, y, mode, exact), (x, y)


def _mdot_bwd(mode, exact, res, g):
    x, y = res
    if mode == "nn":
        return mdot(g, y, "nt", exact), mdot(x, g, "tn", exact)
    if mode == "nt":
        return mdot(g, y, "nn", exact), mdot(g, x, "tn", exact)
    return mdot(y, g, "nt", exact), mdot(x, g, "nn", exact)


mdot.defvjp(_mdot_fwd, _mdot_bwd)


def _seg_ones():
    i = lax.broadcasted_iota(jnp.int32, (256, 256), 0) // HEAD
    j = lax.broadcasted_iota(jnp.int32, (256, 256), 1) // HEAD
    return (i == j).astype(BF16)


@jax.custom_vjp
def segsum(x):
    bd = _seg_ones()
    hi = x.astype(BF16)
    lo = (x - hi.astype(F32)).astype(BF16)
    cols = []
    for j in range(x.shape[1] // 256):
        sl = slice(256 * j, 256 * (j + 1))
        cols.append(jnp.dot(hi[:, sl], bd, preferred_element_type=F32)
                    + jnp.dot(lo[:, sl], bd, preferred_element_type=F32))
    return jnp.concatenate(cols, axis=1)


segsum.defvjp(lambda x: (segsum(x), None), lambda _, g: (segsum(g),))


NORM_EPS = 1e-6
LN_EPS = 1e-5
GN_EPS = 64e-5
SGU_CHUNK = 128
SGU_GROUPS = 8


def _rms(x, g):
    return x * lax.rsqrt(jnp.mean(x * x, axis=-1, keepdims=True) + NORM_EPS) * g


def f_norm_in(x, g):
    return _rms(x, g), x


def f_sgu(p, ln_w, ln_b, sw, sbt):
    tm = p.shape[0]
    z = 0.5 * p * (1.0 + lax.erf(p * 0.7071067811865476))
    u, v = z[:, :D_MODEL], z[:, D_MODEL:]
    mu = jnp.mean(v, axis=-1, keepdims=True)
    d = v - mu
    vn = d * lax.rsqrt(jnp.mean(d * d, axis=-1, keepdims=True) + LN_EPS) * ln_w + ln_b
    ii = lax.broadcasted_iota(jnp.int32, (SGU_CHUNK, SGU_CHUNK), 0)
    jj = lax.broadcasted_iota(jnp.int32, (SGU_CHUNK, SGU_CHUNK), 1)
    mask = (jj <= ii).astype(F32)
    gi = lax.broadcasted_iota(jnp.int32, (SGU_GROUPS, D_MODEL), 0)
    ci = lax.broadcasted_iota(jnp.int32, (SGU_GROUPS, D_MODEL), 1) // SGU_CHUNK
    bias = mdot(sbt, (gi == ci).astype(F32), "nn", True)
    rows = []
    for c in range(tm // SGU_CHUNK):
        cols = []
        for g in range(SGU_GROUPS):
            blk = vn[c * SGU_CHUNK:(c + 1) * SGU_CHUNK, g * SGU_CHUNK:(g + 1) * SGU_CHUNK]
            cols.append(mdot(sw[g] * mask, blk, "nn", False))
        rows.append(jnp.concatenate(cols, axis=1) + bias)
    return (u * jnp.concatenate(rows, axis=0),)


def _softplus(x):
    return jnp.maximum(x, 0.0) + jnp.log1p(jnp.exp(-jnp.abs(x)))


def f_pre(q, wl, w0, al, a0, gl, k_k, k_a):
    qr, qk, qv, ql = q[:, :1024], q[:, 1024:2048], q[:, 2048:3072], q[:, 3072:]
    return _f_pre(qr, qk, qv, ql, wl, w0, al, a0, gl, k_k, k_a)


def _f_pre(qr, qk, qv, ql, wl, w0, al, a0, gl, k_k, k_a):
    xw, xa, xg = ql[:, :128], ql[:, 128:256], ql[:, 256:512]
    wr = -_softplus(-(w0 + mdot(jnp.tanh(xw), wl, "nn", False))) - 0.5
    lw = -jnp.exp(wr)
    aa = jax.nn.sigmoid(a0 + mdot(xa, al, "nn", False))
    g = mdot(jax.nn.sigmoid(xg), gl, "nn", False)
    kkr = qk * k_k
    kk = kkr / jnp.maximum(jnp.sqrt(segsum(kkr * kkr)), 1e-12)
    kp = qk * (1.0 + (aa - 1.0) * k_a)
    return qr, lw, kp, qv, -kk, kk * aa, g, qr, kp, qv


def f_post(o, r, kp, v, g, lnw, lnb, rk):
    mu = segsum(o) * (1.0 / HEAD)
    d = o - mu
    gn = d * lax.rsqrt(segsum(d * d) * (1.0 / HEAD) + GN_EPS)
    return ((gn * lnw + lnb + segsum(r * kp * rk) * v) * g,)


def f_mix(ya, yb, ga, gb):
    return (jax.nn.sigmoid(ga) * ya + jax.nn.sigmoid(gb) * yb,)


def f_ffn_in(h1, g):
    return _rms(h1, g), h1


def f_final(h1, m3, tgt, g):
    y = _rms(h1 + m3, g)
    err = jnp.square(y - tgt)
    return 0.5 * jnp.sum(jnp.mean(err, axis=-1))


def _cparams(n_grid):
    return pltpu.CompilerParams(dimension_semantics=("arbitrary",) * n_grid, vmem_limit_bytes=VMEM_LIMIT)


def _tile_spec(tm, w, cb):
    return pl.BlockSpec((tm, w), lambda i: (i, cb))


def _const_spec(c):
    nd = c.ndim
    return pl.BlockSpec(c.shape, lambda i: (0,) * nd)


def ew_call(fn, tiled, consts, outs, *, tm, name):
    t = tiled[0][0].shape[0]
    n_t, n_c = len(tiled), len(consts)

    def body(*refs):
        tv = [r[...].astype(F32) for r in refs[:n_t]]
        cv = [r[...] for r in refs[n_t:n_t + n_c]]
        res = fn(*tv, *cv)
        for o_ref, val in zip(refs[n_t + n_c:], res):
            o_ref[...] = val.astype(o_ref.dtype)

    return pl.pallas_call(
        body,
        name=name,
        grid=(t // tm,),
        in_specs=[_tile_spec(tm, w, cb) for _, w, cb in tiled] + [_const_spec(c) for c in consts],
        out_specs=[_tile_spec(tm, w, 0) for w, _ in outs],
        out_shape=[jax.ShapeDtypeStruct((t, w), dt) for w, dt in outs],
        compiler_params=_cparams(1),
    )(*[a for a, _, _ in tiled], *consts)


def ew_vjp_call(fn, tiled, consts, cots, d_tiled, d_consts, *, tm, name):
    t = tiled[0][0].shape[0]
    n_t, n_c, n_g = len(tiled), len(consts), len(cots)
    dt_list = [(i, dt) for i, dts in enumerate(d_tiled) for dt in dts]
    dc_list = [i for i, want in enumerate(d_consts) if want]

    def body(*refs):
        tv = [r[...].astype(F32) for r in refs[:n_t]]
        cv = [r[...] for r in refs[n_t:n_t + n_c]]
        gv = tuple(r[...].astype(F32) for r in refs[n_t + n_c:n_t + n_c + n_g])
        out_refs = refs[n_t + n_c + n_g:]
        _, vjp = jax.vjp(fn, *tv, *cv)
        grads = vjp(gv)
        for o_ref, (i, _) in zip(out_refs, dt_list):
            o_ref[...] = grads[i].astype(o_ref.dtype)
        acc_refs = out_refs[len(dt_list):]

        @pl.when(pl.program_id(0) == 0)
        def _():
            for a_ref in acc_refs:
                a_ref[...] = jnp.zeros_like(a_ref)

        for a_ref, i in zip(acc_refs, dc_list):
            a_ref[...] += grads[n_t + i]

    res = pl.pallas_call(
        body,
        name=name,
        grid=(t // tm,),
        in_specs=[_tile_spec(tm, w, cb) for _, w, cb in tiled] + [_const_spec(c) for c in consts]
        + [_tile_spec(tm, w, cb) for _, w, cb in cots],
        out_specs=[_tile_spec(tm, tiled[i][1], 0) for i, _ in dt_list] + [_const_spec(consts[i]) for i in dc_list],
        out_shape=[jax.ShapeDtypeStruct((t, tiled[i][1]), dt) for i, dt in dt_list]
        + [jax.ShapeDtypeStruct(consts[i].shape, F32) for i in dc_list],
        compiler_params=_cparams(1),
    )(*[a for a, _, _ in tiled], *consts, *[a for a, _, _ in cots])
    return res[:len(dt_list)], res[len(dt_list):]


def mm(a, b, mode, *, tm, tn, name, out_dtypes=(F32,), epi=None, extras=(), into=None):
    m = a.shape[1] if mode == "tn" else a.shape[0]
    kd = a.shape[0] if mode == "tn" else a.shape[1]
    n = b.shape[0] if mode == "nt" else b.shape[1]
    tm, tn = min(tm, m), min(tn, n)
    if mode == "nn":
        a_spec = pl.BlockSpec((tm, kd), lambda i, j: (i, 0))
        b_spec = pl.BlockSpec((kd, tn), lambda i, j: (0, j))
    elif mode == "nt":
        a_spec = pl.BlockSpec((tm, kd), lambda i, j: (i, 0))
        b_spec = pl.BlockSpec((tn, kd), lambda i, j: (j, 0))
    else:
        a_spec = pl.BlockSpec((kd, tm), lambda i, j: (0, i))
        b_spec = pl.BlockSpec((kd, tn), lambda i, j: (0, j))
    n_e = len(extras)
    o_spec = pl.BlockSpec((tm, tn), lambda i, j: (i, j))

    if into is not None:
        buf, place = into

        def body_into(a_ref, b_ref, buf_ref, o_ref):
            o_ref[0, 0] = lax.dot_general(a_ref[...].astype(BF16), b_ref[...].astype(BF16), _MDIMS[mode],
                                          preferred_element_type=F32)

        return pl.pallas_call(
            body_into,
            name=name,
            grid=(m // tm, n // tn),
            in_specs=[a_spec, b_spec, pl.BlockSpec(memory_space=pl.ANY)],
            out_specs=pl.BlockSpec((1, 1, tm, tn), lambda i, j: (*place(i, j), 0)),
            out_shape=jax.ShapeDtypeStruct(buf.shape, F32),
            input_output_aliases={2: 0},
            compiler_params=_cparams(2),
        )(a, b, buf)

    def body(a_ref, b_ref, *refs):
        c = lax.dot_general(a_ref[...].astype(BF16), b_ref[...].astype(BF16), _MDIMS[mode],
                            preferred_element_type=F32)
        res = epi(c, *[r[...] for r in refs[:n_e]]) if epi is not None else (c,)
        for o_ref, val in zip(refs[n_e:], res):
            o_ref[...] = val.astype(o_ref.dtype)

    res = pl.pallas_call(
        body,
        name=name,
        grid=(m // tm, n // tn),
        in_specs=[a_spec, b_spec] + [o_spec] * n_e,
        out_specs=[o_spec] * len(out_dtypes),
        out_shape=[jax.ShapeDtypeStruct((m, n), dt) for dt in out_dtypes],
        compiler_params=_cparams(2),
    )(a, b, *extras)
    return res if len(out_dtypes) > 1 else res[0]


RWKV_COL0 = 4096
RWKV_WIDTH = 3584
SHIFT_BLK = 512


def _shift_down(p, prev_row):
    rows = lax.broadcasted_iota(jnp.int32, p.shape, 0)
    return jnp.where(rows == 0, prev_row, pltpu.roll(p, 1, 0))


def shiftmix_fwd(p_all, sbp, *, tm):
    t = p_all.shape[0]
    tm = min(tm, t)
    c0 = RWKV_COL0 // SHIFT_BLK
    hb = tm // 8

    def body(p_ref, halo_ref, sb_ref, q_ref):
        p = p_ref[...]
        prev = jnp.where(pl.program_id(0) == 0, 0.0, halo_ref[7:8, :])
        q_ref[...] = p * sb_ref[0:1, :] + _shift_down(p, prev) * sb_ref[1:2, :]

    return pl.pallas_call(
        body,
        name="shiftmix_fwd",
        grid=(t // tm, RWKV_WIDTH // SHIFT_BLK),
        in_specs=[
            pl.BlockSpec((tm, SHIFT_BLK), lambda i, j: (i, c0 + j)),
            pl.BlockSpec((8, SHIFT_BLK), lambda i, j: (jnp.maximum(i * hb - 1, 0), c0 + j)),
            pl.BlockSpec((2, SHIFT_BLK), lambda i, j: (0, j)),
        ],
        out_specs=pl.BlockSpec((tm, SHIFT_BLK), lambda i, j: (i, j)),
        out_shape=jax.ShapeDtypeStruct((t, RWKV_WIDTH), F32),
        compiler_params=_cparams(2),
    )(p_all, p_all, sbp)


def shiftmix_bwd(dq, col0, p_all, sbp, *, tm, name):
    t, w = dq.shape
    n_i = t // tm
    hb = tm // 8
    cq = col0 // SHIFT_BLK
    cp = (RWKV_COL0 + col0) // SHIFT_BLK

    def body(dq_ref, dqn_ref, p_ref, ph_ref, sb_ref, dp_ref, dsb_ref):
        i = pl.program_id(1)
        dq_t = dq_ref[...]
        rows = lax.broadcasted_iota(jnp.int32, dq_t.shape, 0)
        nxt = jnp.where(i == n_i - 1, 0.0, dqn_ref[0:1, :])
        up = jnp.where(rows == tm - 1, nxt, pltpu.roll(dq_t, tm - 1, 0))
        dp_ref[...] = (dq_t * sb_ref[0:1, :] + up * sb_ref[1:2, :]).astype(dp_ref.dtype)
        p = p_ref[...]
        prev = jnp.where(i == 0, 0.0, ph_ref[7:8, :])
        s0 = jnp.sum(dq_t * p, axis=0, keepdims=True)
        s1 = jnp.sum(dq_t * _shift_down(p, prev), axis=0, keepdims=True)
        two = lax.broadcasted_iota(jnp.int32, (2, SHIFT_BLK), 0)

        @pl.when(i == 0)
        def _():
            dsb_ref[...] = jnp.zeros_like(dsb_ref)

        dsb_ref[...] += jnp.where(two == 0, s0, s1)

    return pl.pallas_call(
        body,
        name=name,
        grid=(w // SHIFT_BLK, n_i),
        in_specs=[
            pl.BlockSpec((tm, SHIFT_BLK), lambda j, i: (i, j)),
            pl.BlockSpec((8, SHIFT_BLK), lambda j, i: (jnp.minimum((i + 1) * hb, t // 8 - 1), j)),
            pl.BlockSpec((tm, SHIFT_BLK), lambda j, i: (i, cp + j)),
            pl.BlockSpec((8, SHIFT_BLK), lambda j, i: (jnp.maximum(i * hb - 1, 0), cp + j)),
            pl.BlockSpec((2, SHIFT_BLK), lambda j, i: (0, cq + j)),
        ],
        out_specs=[
            pl.BlockSpec((tm, SHIFT_BLK), lambda j, i: (i, j)),
            pl.BlockSpec((2, SHIFT_BLK), lambda j, i: (0, j)),
        ],
        out_shape=[jax.ShapeDtypeStruct((t, w), BF16), jax.ShapeDtypeStruct((2, w), F32)],
        compiler_params=_cparams(2),
    )(dq, dq, p_all, p_all, sbp)


def final_call(h1, m3, tgt, g_final, *, tm):
    t = h1.shape[0]

    def body(h1_ref, m3_ref, tgt_ref, g_ref, dh_ref, dhb_ref, dg_ref, loss_ref):
        loss, vjp = jax.vjp(f_final, h1_ref[...], m3_ref[...], tgt_ref[...], g_ref[...])
        dh, _, _, dg = vjp(jnp.ones((), F32))
        dh_ref[...] = dh
        dhb_ref[...] = dh.astype(BF16)

        @pl.when(pl.program_id(0) == 0)
        def _():
            dg_ref[...] = jnp.zeros_like(dg_ref)
            loss_ref[...] = jnp.zeros_like(loss_ref)

        dg_ref[...] += dg
        loss_ref[...] += jnp.full(loss_ref.shape, loss, F32)

    tile = _tile_spec(tm, D_MODEL, 0)
    return pl.pallas_call(
        body,
        name="final_loss",
        grid=(t // tm,),
        in_specs=[tile, tile, tile, _const_spec(g_final)],
        out_specs=[tile, tile, _const_spec(g_final), pl.BlockSpec((8, 128), lambda i: (0, 0))],
        out_shape=[jax.ShapeDtypeStruct((t, D_MODEL), F32), jax.ShapeDtypeStruct((t, D_MODEL), BF16),
                   jax.ShapeDtypeStruct(g_final.shape, F32), jax.ShapeDtypeStruct((8, 128), F32)],
        compiler_params=_cparams(1),
    )(h1, m3, tgt, g_final)


N_SGU = 2048
N_RWKV = 3360
LORA_W, LORA_A, LORA_G = 64, 64, 160


def _pad_rwkv_cols(z):
    zero = lambda n: jnp.zeros(z.shape[:-1] + (n,), z.dtype)
    return jnp.concatenate([z[..., :3072], z[..., 3072:3136], zero(64), z[..., 3136:3200], zero(64),
                            z[..., 3200:3360], zero(96)], axis=-1)


def _unpad_rwkv_cols(z):
    return jnp.concatenate([z[..., :3072], z[..., 3072:3136], z[..., 3200:3264], z[..., 3328:3488]], axis=-1)


def _pad_win_rows(wt):
    z = wt[N_SGU:N_SGU + N_RWKV]
    zero = lambda n: jnp.zeros((n, wt.shape[1]), wt.dtype)
    return jnp.concatenate([wt[:N_SGU], wt[N_SGU + N_RWKV:], z[:3072], z[3072:3136], zero(64), z[3136:3200], zero(64),
                            z[3200:3360], zero(96)], axis=0)


def _unpad_win_rows(wt):
    z = wt[RWKV_COL0:]
    return jnp.concatenate([wt[:N_SGU], z[:3072], z[3072:3136], z[3200:3264], z[3328:3488], wt[N_SGU:RWKV_COL0]],
                           axis=0)


def _pad_rows(w, n):
    return jnp.concatenate([w, jnp.zeros((n - w.shape[0],) + w.shape[1:], w.dtype)], axis=0)


def _relu2_epi(c):
    return c, jnp.square(jnp.maximum(c, 0.0))


def _relu2_bwd_epi(c, hid):
    return (c * (2.0 * jnp.maximum(hid.astype(F32), 0.0)),)


def _add_epi(c, x):
    return (c + x,)


def _pre_fwd(*args):
    res = f_pre(*args)
    return res[1], res[2], res[4], res[5], res[6]


def local_step(x, tgt, w, late_token, late_weights, pair_start, pair_finish, pack_early):
    d = D_MODEL
    win_pt = _pad_win_rows(w["w_in"])
    sbp = _pad_rwkv_cols(w["shift_b"])
    wl = _pad_rows(w["w_lora_w"], 128)
    al = _pad_rows(w["a_lora_w"], 128)
    gl = _pad_rows(w["g_lora_w"], 256)
    sbt = w["sgu_b"].T

    (a_bf,) = ew_call(lambda x_, g_: (f_norm_in(x_, g_)[0],), [(x, d, 0)], [w["g_mix"] + late_token[:1, :1]],
                      [(d, BF16)], tm=256, name="norm_in")
    p_all = mm(a_bf, win_pt, "nt", tm=2048, tn=1280, name="mm_in")
    sgu_t = [(p_all, 2 * d, 0)]
    sgu_c = [w["sgu_ln_w"], w["sgu_ln_b"], w["sgu_w"], sbt]
    (s_bf,) = ew_call(f_sgu, sgu_t, sgu_c, [(d, BF16)], tm=256, name="sgu_fwd")
    ya = mm(s_bf, w["w_proj_a"], "nn", tm=512, tn=1024, name="mm_proj_a")
    q = shiftmix_fwd(p_all, sbp, tm=1024)
    pre_t = [(q, RWKV_WIDTH, 0)]
    pre_c = [wl, w["w0"], al, w["a0"], gl, w["k_k"], w["k_a"]]
    lw, kp, na, nb, g = ew_call(_pre_fwd, pre_t, pre_c, [(d, F32)] * 5, tm=256, name="rwkv_pre_fwd")
    scan_ops = [(q, 0), (lw, 0), (kp, 0), (q, 2), (na, 0), (nb, 0)]
    o, s0s = scan_fwd(scan_ops)
    w = {**w, **late_weights(o)}
    post_t = [(o, d, 0), (q, d, 0), (kp, d, 0), (q, d, 2), (g, d, 0)]
    post_c = [w["ln_x_w"], w["ln_x_b"], w["r_k"]]
    (ob_bf,) = ew_call(f_post, post_t, post_c, [(d, BF16)], tm=256, name="rwkv_post_fwd")
    yb = mm(ob_bf, w["w_proj_b"], "nn", tm=512, tn=1024, name="mm_proj_b")
    mix_t = [(ya, d, 0), (yb, d, 0), (p_all, d, 2), (p_all, d, 3)]
    (mixed_bf,) = ew_call(f_mix, mix_t, [], [(d, BF16)], tm=256, name="mix_fwd")
    h1 = mm(mixed_bf, w["w_out"], "nn", tm=512, tn=1024, name="mm_out", epi=_add_epi, extras=(x,))
    (f_bf,) = ew_call(lambda h_, g_: (f_ffn_in(h_, g_)[0],), [(h1, d, 0)], [w["g_ffn"]], [(d, BF16)], tm=256,
                      name="ffn_norm")
    hid, act_bf = mm(f_bf, w["w_ffn1"], "nn", tm=2048, tn=1024, name="mm_ffn1", out_dtypes=(BF16, BF16), epi=_relu2_epi)
    m3 = mm(act_bf, w["w_ffn2"], "nn", tm=1024, tn=512, name="mm_ffn2")
    dh2, dh2_bf, dg_final, loss = final_call(h1, m3, tgt, w["g_final"], tm=256)

    dhid_bf = mm(dh2_bf, w["w_ffn2"], "nt", tm=2048, tn=1024, name="mm_dact", out_dtypes=(BF16,), epi=_relu2_bwd_epi,
                 extras=(hid,))
    late_g = lax.empty((N_CHIPS, 2, PACK_ROWS, HALF_W), F32)
    late_g = mm(act_bf, dh2_bf, "tn", tm=1024, tn=HALF_W, name="mm_dw_ffn2",
                into=(late_g, lambda i, j: (i, j, PIECE_OFF["w_ffn2"] // 1024)))
    df = mm(dhid_bf, w["w_ffn1"], "nt", tm=1024, tn=512, name="mm_df")
    late_g = mm(f_bf, dhid_bf, "tn", tm=1024, tn=HALF_W, name="mm_dw_ffn1",
                into=(late_g, lambda i, j: (j // 2, j % 2, PIECE_OFF["w_ffn1"] // 1024)))
    (dh1, dh1_bf), (dg_ffn,) = ew_vjp_call(f_ffn_in, [(h1, d, 0)], [w["g_ffn"]], [(df, d, 0), (dh2, d, 0)],
                                           [(F32, BF16)], [True], tm=256, name="ffn_norm_bwd")
    dmixed = mm(dh1_bf, w["w_out"], "nt", tm=512, tn=1024, name="mm_dmixed")
    late_g = mm(mixed_bf, dh1_bf, "tn", tm=256, tn=HALF_W, name="mm_dw_out",
                into=(late_g, lambda i, j: (i, j, PIECE_OFF["w_out"] // 256)))
    (dya_bf, dyb_bf, dga_bf, dgb_bf), _ = ew_vjp_call(f_mix, mix_t, [], [(dmixed, d, 0)], [(BF16,)] * 4, [], tm=256,
                                                      name="mix_bwd")
    dob = mm(dyb_bf, w["w_proj_b"], "nt", tm=512, tn=1024, name="mm_dob")
    late_g = mm(ob_bf, dyb_bf, "tn", tm=256, tn=HALF_W, name="mm_dw_proj_b",
                into=(late_g, lambda i, j: (i, j, PIECE_OFF["w_proj_b"] // 256)))
    late_state, late_token = pair_start(late_g, "late")
    post_c_after = [w["ln_x_w"] + late_token[:1, :1]] + post_c[1:]
    (do, dr_p, dkp_p, dv_p, dg), (dlnx_w, dlnx_b, dr_k) = ew_vjp_call(
        f_post, post_t, post_c_after, [(dob, d, 0)], [(F32,)] * 5, [True] * 3, tm=256, name="rwkv_post_bwd")
    late_part, late_part16 = pair_finish(late_state, do, "late")
    *scan_g, late_slots = scan_bwd(scan_ops, s0s, do, late_part16)
    pre_g = [(z, d, 0) for z in scan_g] + [(dg, d, 0), (dr_p, d, 0), (dkp_p, d, 0), (dv_p, d, 0)]
    (dq,), (dwl, dw0, dal, da0, dgl, dk_k, dk_a) = ew_vjp_call(
        f_pre, pre_t, pre_c, pre_g, [(F32,)], [True] * 7, tm=256, name="rwkv_pre_bwd")
    dp_rwkv, dsb = shiftmix_bwd(dq, 0, p_all, sbp, tm=512, name="shiftmix_bwd")
    ds = mm(dya_bf, w["w_proj_a"], "nt", tm=512, tn=1024, name="mm_ds")
    d_proj_a = mm(s_bf, dya_bf, "tn", tm=512, tn=1024, name="mm_dw_proj_a")
    (dp_sgu,), (dln_w, dln_b, dsw, dsbt) = ew_vjp_call(f_sgu, sgu_t, sgu_c, [(ds, d, 0)], [(BF16,)], [True] * 4,
                                                       tm=256, name="sgu_bwd")
    dp_all = jnp.concatenate([dp_sgu, dga_bf, dgb_bf, dp_rwkv], axis=1)
    d_in_pt = mm(dp_all, a_bf, "tn", tm=1280, tn=1024, name="mm_dw_in")
    early_state, early_token = pair_start(pack_early({
        "w_in": _unpad_win_rows(d_in_pt), "w_proj_a": d_proj_a, "w_lora_w": dwl[:LORA_W], "a_lora_w": dal[:LORA_A],
        "g_lora_w": dgl[:LORA_G]}), "early")
    da = mm(dp_all, win_pt, "nn", tm=1024, tn=256, name="mm_da")
    g_mix_after = w["g_mix"] + early_token[:1, :1]
    (grad_x,), (dg_mix,) = ew_vjp_call(f_norm_in, [(x, d, 0)], [g_mix_after], [(da, d, 0), (dh1, d, 0)], [(F32,)],
                                       [True], tm=256, name="norm_in_bwd")

    grads = {
        "g_mix": dg_mix, "sgu_ln_w": dln_w, "sgu_ln_b": dln_b, "sgu_w": dsw, "sgu_b": dsbt.T,
        "shift_b": _unpad_rwkv_cols(dsb),
        "w0": dw0, "a0": da0, "k_k": dk_k, "k_a": dk_a, "r_k": dr_k, "ln_x_w": dlnx_w, "ln_x_b": dlnx_b,
        "g_ffn": dg_ffn, "g_final": dg_final,
    }
    return loss[0, 0], grad_x, grads, (late_part, late_slots), early_state


MESH = pl.DeviceIdType.MESH
N_CHIPS = 4
PACK_ROWS = 2560
PACK_TILE = 512
SMALL_ROWS = 160
_ANY = pl.BlockSpec(memory_space=pl.ANY)


def _coords():
    return lax.axis_index("x"), lax.axis_index("y"), lax.axis_index("c")


def _other_chips(x, y):
    return [(1 - x, y), (x, 1 - y), (1 - x, 1 - y)]


def _remote(src, dst, send_sems, recv_sems, k, to):
    return pltpu.make_async_remote_copy(src_ref=src, dst_ref=dst, send_sem=send_sems.at[k], recv_sem=recv_sems.at[k],
                                        device_id=to, device_id_type=MESH)


def gather_shards(pack):
    def body(src_ref, out_ref, token, send_sems, recv_sems):
        x, y, c = _coords()
        me = 2 * x + y
        sib = (x, y, 1 - c)
        chips = _other_chips(x, y)
        first = [_remote(src_ref.at[c], out_ref.at[me, c], send_sems, recv_sems, k, (cx, cy, c))
                 for k, (cx, cy) in enumerate(chips)]
        for cp in first:
            cp.start()
        passed = []
        for k, (cx, cy) in enumerate(chips):
            j = 2 * cx + cy
            _remote(src_ref.at[c], out_ref.at[j, c], send_sems, recv_sems, k, (cx, cy, c)).wait_recv()
            fwd = _remote(out_ref.at[j, c], out_ref.at[j, c], send_sems, recv_sems, 3 + k, sib)
            fwd.start()
            passed.append(fwd)
        for k, (cx, cy) in enumerate(chips):
            j = 2 * cx + cy
            _remote(out_ref.at[j, 1 - c], out_ref.at[j, 1 - c], send_sems, recv_sems, 3 + k, sib).wait_recv()
        for cp in first + passed:
            cp.wait_send()
        token[...] = jnp.zeros_like(token)

    return pl.pallas_call(
        body,
        name="gather_shards",
        in_specs=[_ANY],
        out_specs=[_ANY, pl.BlockSpec(memory_space=pltpu.VMEM)],
        out_shape=[jax.ShapeDtypeStruct((N_CHIPS,) + pack.shape, pack.dtype), jax.ShapeDtypeStruct((8, 128), F32)],
        scratch_shapes=[pltpu.SemaphoreType.DMA((6,)), pltpu.SemaphoreType.DMA((6,))],
    )(pack)


def _gather_copies(pack_ref, all_ref, send_sems, recv_sems):
    x, y, c = _coords()
    me = 2 * x + y
    return [(_remote(pack_ref.at[c], all_ref.at[me, c], send_sems, recv_sems, k, (cx, cy, c)),
             _remote(pack_ref.at[c], all_ref.at[2 * cx + cy, c], send_sems, recv_sems, k, (cx, cy, c)))
            for k, (cx, cy) in enumerate(_other_chips(x, y))]


_HBM = pl.BlockSpec(memory_space=pltpu.HBM)
_SEM = pl.BlockSpec(memory_space=pltpu.SEMAPHORE)
_SIDE_EFFECT = pltpu.SideEffectType.DATAFLOW_SIDE_EFFECTING


def split_start(name, copies, n, src, land_shape, after=None):
    def body(src_ref, land_ref, *refs):
        send_sems, recv_sems, token = refs[-5], refs[-4], refs[-1]
        for send, _ in copies(src_ref, land_ref, send_sems, recv_sems):
            send.start()
        token[...] = jnp.zeros_like(token)

    extra = () if after is None else (after,)
    *state, token = pl.pallas_call(
        body,
        name=name,
        out_shape=(pltpu.SemaphoreType.DMA((n,)), pltpu.SemaphoreType.DMA((n,)), pltpu.HBM(src.shape, src.dtype),
                   pltpu.HBM(land_shape, src.dtype), jax.ShapeDtypeStruct((8, 128), F32)),
        in_specs=(_HBM, _HBM) + (pl.BlockSpec(memory_space=pl.ANY),) * len(extra),
        out_specs=(_SEM, _SEM, _HBM, _HBM, pl.BlockSpec(memory_space=pltpu.VMEM)),
        input_output_aliases={0: 2, 1: 3},
        compiler_params=pltpu.CompilerParams(has_side_effects=_SIDE_EFFECT),
    )(pltpu.with_memory_space_constraint(src, pltpu.HBM),
      pltpu.with_memory_space_constraint(lax.empty(land_shape, src.dtype), pltpu.HBM), *extra)
    return state, token


def split_wait(name, copies, state, after):
    send_sems, recv_sems, src, land = state

    def body(src_ref, land_ref, send_sems, recv_sems, after_ref, src_out, land_out):
        for send, arrival in copies(src_ref, land_ref, send_sems, recv_sems):
            send.wait_send()
            arrival.wait_recv()

    return pl.pallas_call(
        body,
        name=name,
        out_shape=(pltpu.HBM(src.shape, src.dtype), pltpu.HBM(land.shape, land.dtype)),
        in_specs=(_HBM, _HBM, _SEM, _SEM, pl.BlockSpec(memory_space=pl.ANY)),
        out_specs=(_HBM, _HBM),
        input_output_aliases={0: 0, 1: 1},
        compiler_params=pltpu.CompilerParams(has_side_effects=_SIDE_EFFECT),
    )(src, land, send_sems, recv_sems, after)


def gather_forward(got):
    def body(got_ref, out_ref, send_sems, recv_sems):
        x, y, c = _coords()
        sib = (x, y, 1 - c)
        slots = [2 * cx + cy for cx, cy in _other_chips(x, y)]
        sends = [_remote(got_ref.at[j, c], out_ref.at[j, c], send_sems, recv_sems, k, sib) for k, j in enumerate(slots)]
        for cp in sends:
            cp.start()
        for k, j in enumerate(slots):
            _remote(got_ref.at[j, 1 - c], out_ref.at[j, 1 - c], send_sems, recv_sems, k, sib).wait_recv()
        for cp in sends:
            cp.wait_send()

    return pl.pallas_call(
        body,
        name="gather_forward",
        in_specs=[_ANY],
        out_specs=_ANY,
        out_shape=jax.ShapeDtypeStruct(got.shape, got.dtype),
        input_output_aliases={0: 0},
        scratch_shapes=[pltpu.SemaphoreType.DMA((3,)), pltpu.SemaphoreType.DMA((3,))],
    )(got)


def pair_sum(g, got, tag, *, tm):
    n, _, rows, width = g.shape

    def body(g0_ref, g1_ref, got_ref, out_ref, out16_ref):
        own = jnp.where(lax.axis_index("c") == 0, g0_ref[0, 0], g1_ref[0, 0])
        total = own + got_ref[0]
        out_ref[0] = total
        out16_ref[0] = total.astype(BF16)

    blk = pl.BlockSpec((1, tm, width), lambda j, i: (j, i, 0))
    return pl.pallas_call(
        body,
        name="pair_sum_" + tag,
        grid=(n, rows // tm),
        in_specs=[pl.BlockSpec((1, 1, tm, width), lambda j, i: (j, 0, i, 0)),
                  pl.BlockSpec((1, 1, tm, width), lambda j, i: (j, 1, i, 0)), blk],
        out_specs=[blk, blk],
        out_shape=[jax.ShapeDtypeStruct(got.shape, F32), jax.ShapeDtypeStruct(got.shape, BF16)],
        compiler_params=_cparams(2),
    )(g, g, got)


def _pair_copies(g_ref, got_ref, send_sems, recv_sems):
    x, y, c = _coords()
    copies = [_remote(g_ref.at[j, 1 - c], got_ref.at[j], send_sems, recv_sems, j, (x, y, 1 - c))
              for j in range(N_CHIPS)]
    return [(cp, cp) for cp in copies]


def _chip_copies(p_ref, slots_ref, send_sems, recv_sems):
    x, y, c = _coords()
    me = 2 * x + y
    return [(_remote(p_ref.at[2 * cx + cy], slots_ref.at[me], send_sems, recv_sems, k, (cx, cy, c)),
             _remote(p_ref.at[me], slots_ref.at[2 * cx + cy], send_sems, recv_sems, k, (cx, cy, c)))
            for k, (cx, cy) in enumerate(_other_chips(x, y))]


def sum_with_own(own, slots, index_fn, after, *, tm, name):
    n, rows, width = slots.shape

    def body(*refs):
        mine = index_fn()
        acc = None
        for s in range(n):
            term = jnp.where(mine == s, refs[s][0], refs[n + s][0].astype(F32))
            acc = term if acc is None else acc + term
        refs[-1][...] = acc

    slot_specs = [pl.BlockSpec((1, tm, width), lambda i, s=s: (s, i, 0)) for s in range(n)]
    return pl.pallas_call(
        body,
        name=name,
        grid=(rows // tm,),
        in_specs=slot_specs + slot_specs + [pl.BlockSpec(after.shape, lambda i: (0,) * after.ndim)],
        out_specs=pl.BlockSpec((tm, width), lambda i: (i, 0)),
        out_shape=jax.ShapeDtypeStruct((rows, width), F32),
        compiler_params=_cparams(1),
    )(*([own] * n), *([slots] * n), after)


def exchange_halves(s, tag):
    rq = PACK_TILE
    nq = s.shape[0] // rq

    def body(s_ref, out_ref, sbuf, rbuf, send_sems, recv_sems, in_sems, out_sems):
        x, y, c = _coords()
        sib = (x, y, 1 - c)
        rows = lambda q: pl.ds(q * rq, rq)
        loads = [pltpu.make_async_copy(s_ref.at[rows(q)], sbuf.at[rows(q)], in_sems.at[q]) for q in range(nq)]
        for cp in loads:
            cp.start()
        sends = []
        for q in range(nq):
            loads[q].wait()
            sends.append(_remote(sbuf.at[rows(q)], rbuf.at[rows(q)], send_sems, recv_sems, q, sib))
            sends[q].start()
        stores = []
        for q in range(nq):
            sends[q].wait_recv()
            stores.append(pltpu.make_async_copy(rbuf.at[rows(q)], out_ref.at[rows(q)], out_sems.at[q]))
            stores[q].start()
        for cp in sends:
            cp.wait_send()
        for cp in stores:
            cp.wait()

    return pl.pallas_call(
        body,
        name="exchange_halves_" + tag,
        in_specs=[_ANY],
        out_specs=_ANY,
        out_shape=jax.ShapeDtypeStruct(s.shape, s.dtype),
        scratch_shapes=[pltpu.VMEM(s.shape, s.dtype), pltpu.VMEM(s.shape, s.dtype)]
        + [pltpu.SemaphoreType.DMA((nq,))] * 4,
        compiler_params=pltpu.CompilerParams(vmem_limit_bytes=VMEM_LIMIT),
    )(s)


def sum_all(s, after):
    rows = s.shape[0]
    half = rows // 2

    def body(s_ref, after_ref, out_ref, theirs, pair, slots, send_sems, recv_sems):
        x, y, c = _coords()
        me = 2 * x + y
        sib = (x, y, 1 - c)
        chips = _other_chips(x, y)
        swap = _remote(s_ref, theirs, send_sems, recv_sems, 0, sib)
        swap.start()
        swap.wait_recv()
        pair[...] = s_ref[...] + theirs[...]
        mine = pl.ds(pl.multiple_of(c * half, 8), half)
        other = pl.ds(pl.multiple_of((1 - c) * half, 8), half)
        sends = [_remote(pair.at[mine], slots.at[me], send_sems, recv_sems, 1 + k, (cx, cy, c))
                 for k, (cx, cy) in enumerate(chips)]
        for cp in sends:
            cp.start()
        for k, (cx, cy) in enumerate(chips):
            _remote(pair.at[mine], slots.at[2 * cx + cy], send_sems, recv_sems, 1 + k, (cx, cy, c)).wait_recv()
        slots[me] = pair[mine]
        out_ref[mine] = ((slots[0] + slots[1]) + slots[2]) + slots[3]
        last = _remote(out_ref.at[mine], out_ref.at[mine], send_sems, recv_sems, 4, sib)
        last.start()
        _remote(out_ref.at[other], out_ref.at[other], send_sems, recv_sems, 4, sib).wait_recv()
        for cp in [swap] + sends + [last]:
            cp.wait_send()

    vmem = pl.BlockSpec(memory_space=pltpu.VMEM)
    return pl.pallas_call(
        body,
        name="sum_all",
        in_specs=[vmem, vmem],
        out_specs=vmem,
        out_shape=jax.ShapeDtypeStruct(s.shape, s.dtype),
        scratch_shapes=[pltpu.VMEM(s.shape, s.dtype), pltpu.VMEM(s.shape, s.dtype),
                        pltpu.VMEM((N_CHIPS, half, s.shape[1]), s.dtype), pltpu.SemaphoreType.DMA((5,)),
                        pltpu.SemaphoreType.DMA((5,))],
        compiler_params=pltpu.CompilerParams(vmem_limit_bytes=VMEM_LIMIT),
    )(s, after)


ADAM_LR = 0.001
ADAM_B1 = 0.9
ADAM_B2 = 0.999
ADAM_EPS = 1e-08
ADAM_WD = 0.01
ADAM_STEP = 10


def f_adamw(g, w, m, v):
    m = ADAM_B1 * m + (1.0 - ADAM_B1) * g
    v = ADAM_B2 * v + (1.0 - ADAM_B2) * jnp.square(g)
    m_hat = m / (1.0 - ADAM_B1 ** ADAM_STEP)
    v_hat = v / (1.0 - ADAM_B2 ** ADAM_STEP)
    delta = -ADAM_LR * (m_hat / (jnp.sqrt(v_hat) + ADAM_EPS) + ADAM_WD * w)
    return delta, m, v


def adamw_call(g, w, m, v, *, tm, name):
    width = g.shape[1]
    return ew_call(f_adamw, [(g, width, 0), (w, width, 0), (m, width, 0), (v, width, 0)], [], [(width, F32)] * 3,
                   tm=tm, name=name)


EARLY = ["w_in", "w_proj_a", "w_lora_w", "a_lora_w", "g_lora_w"]
LATE = ["w_ffn1", "w_ffn2", "w_proj_b", "w_out"]
LORAS = ["w_lora_w", "a_lora_w", "g_lora_w"]
HALF_W = 512
PIECE_ROWS = {"w_in": 1864, "w_ffn1": 1024, "w_ffn2": 1024, "w_proj_a": 256, "w_proj_b": 256, "w_out": 256,
              "w_lora_w": 32, "a_lora_w": 32, "g_lora_w": 80}
PIECE_OFF = {"w_in": 0, "w_proj_a": 1920, "w_lora_w": 2176, "a_lora_w": 2208, "g_lora_w": 2240,
             "w_ffn1": 0, "w_ffn2": 1024, "w_proj_b": 2048, "w_out": 2304}
LO_OFF = 2320
SHARD_AXIS = {"w_in": 1, "w_proj_a": 0, "w_lora_w": 1, "a_lora_w": 1, "g_lora_w": 1, "w_proj_b": 0, "w_out": 0,
              "w_ffn1": 1, "w_ffn2": 0}
SHARD_SHAPE = {"w_in": (1024, 1864), "w_proj_a": (256, 1024), "w_lora_w": (64, 256), "a_lora_w": (64, 256),
               "g_lora_w": (160, 256), "w_proj_b": (256, 1024), "w_out": (256, 1024), "w_ffn1": (1024, 1024),
               "w_ffn2": (1024, 1024)}
SHIFT_SHARD = (2, 840)
VECTORS = ["g_mix", "sgu_ln_w", "sgu_ln_b", "w0", "a0", "k_k", "k_a", "r_k", "ln_x_w", "ln_x_b", "g_ffn", "g_final"]
SMALL = VECTORS + ["sgu_w", "sgu_b"]
SMALL_SHAPE = {**{n: (1, 1024) for n in VECTORS}, "sgu_w": (8, 128, 128), "sgu_b": (8, 128)}
WEIGHTS = ["g_mix", "w_in", "sgu_ln_w", "sgu_ln_b", "sgu_w", "sgu_b", "w_proj_a", "shift_b", "w_lora_w", "w0",
           "a_lora_w", "a0", "g_lora_w", "k_k", "k_a", "r_k", "ln_x_w", "ln_x_b", "w_proj_b", "w_out", "g_ffn",
           "w_ffn1", "w_ffn2", "g_final"]


def _size(shape):
    n = 1
    for s in shape:
        n *= s
    return n


def _pack_rows(parts, rows, dtype):
    flat = jnp.concatenate([p.reshape(-1).astype(dtype) for p in parts])
    return jnp.concatenate([flat, jnp.zeros((rows * 1024 - flat.shape[0],), dtype)]).reshape(rows, 1024)


def _unpack_rows(packed, shapes):
    flat = packed.reshape(-1)
    out, off = [], 0
    for shp in shapes:
        out.append(flat[off:off + _size(shp)].reshape(shp))
        off += _size(shp)
    return out


def _shard_of(name, full, j):
    ax = SHARD_AXIS[name]
    n = SHARD_SHAPE[name][ax]
    return lax.slice_in_dim(full, j * n, (j + 1) * n, axis=ax)


def _pad_cols(z, n):
    return jnp.concatenate([z, jnp.zeros((z.shape[0], n - z.shape[1]), z.dtype)], axis=1)


def _row_form(name, s):
    return s.T if name == "w_in" else s


def _half_piece(name, rf, h):
    if name in LORAS:
        r = PIECE_ROWS[name]
        return _pad_cols(rf[h * r:(h + 1) * r], HALF_W)
    return rf[:, HALF_W * h:HALF_W * (h + 1)]


def _pack_half(group, rf_fn, h, dtype, tail=()):
    parts, pos, rows = [], 0, PACK_ROWS
    for n in group:
        if PIECE_OFF[n] > pos:
            parts.append(jnp.zeros((PIECE_OFF[n] - pos, HALF_W), dtype))
        parts.append(_half_piece(n, rf_fn(n), h).astype(dtype))
        pos = PIECE_OFF[n] + PIECE_ROWS[n]
    for t in tail:
        parts.append(t)
        pos += t.shape[0]
    parts.append(jnp.zeros((rows - pos, HALF_W), dtype))
    return jnp.concatenate(parts, axis=0)


def _piece(pack, name):
    return pack[PIECE_OFF[name]:PIECE_OFF[name] + PIECE_ROWS[name]]


def _join_halves(name, p0, p1):
    if name in LORAS:
        return jnp.concatenate([p0[:, :SHARD_SHAPE[name][1]], p1[:, :SHARD_SHAPE[name][1]]], axis=0)
    return jnp.concatenate([p0, p1], axis=1)


def _grad_row_form(name, full, j):
    if name == "w_in":
        return full[SHARD_SHAPE[name][1] * j:SHARD_SHAPE[name][1] * (j + 1)]
    return _shard_of(name, full, j)


def adamw_weight(name, g_own, g_other, w, m, v):
    rows, width = w.shape
    if name in LORAS:
        tm = PIECE_ROWS[name]
        grid = (2, 1)
        native = pl.BlockSpec((tm, width), lambda h, i: (h, 0))
    elif name == "w_in":
        tm, lanes = rows, 128
        grid = (2, HALF_W // lanes)
        native = pl.BlockSpec((tm, lanes), lambda h, i: (0, h * (HALF_W // lanes) + i))
    else:
        tm = 128
        grid = (2, rows // tm)
        native = pl.BlockSpec((tm, HALF_W), lambda h, i: (i, h))
    off = PIECE_OFF[name] // tm
    if name == "w_in":
        packed = pl.BlockSpec((tm, 128), lambda h, i: (0, i))
    else:
        packed = pl.BlockSpec((tm, HALF_W), lambda h, i: (off + i, 0))

    def body(go_ref, gx_ref, w_ref, m_ref, v_ref, g_ref, d_ref, nm_ref, nv_ref):
        g = jnp.where(pl.program_id(0) == lax.axis_index("c"), go_ref[...], gx_ref[...])[:, :w_ref.shape[1]]
        delta, nm, nv = f_adamw(g, w_ref[...], m_ref[...], v_ref[...])
        g_ref[...] = g
        d_ref[...] = delta
        nm_ref[...] = nm
        nv_ref[...] = nv

    return pl.pallas_call(
        body,
        name="adamw_" + name,
        grid=grid,
        in_specs=[packed, packed, native, native, native],
        out_specs=[native] * 4,
        out_shape=[jax.ShapeDtypeStruct(w.shape, F32)] * 4,
        compiler_params=_cparams(2),
    )(g_own, g_other, w, m, v)


def kernel(x, g_mix, w_in, sgu_ln_w, sgu_ln_b, sgu_w, sgu_b, w_proj_a, shift_b, w_lora_w, w0, a_lora_w, a0, g_lora_w, k_k, k_a, r_k, ln_x_w, ln_x_b, w_proj_b, w_out, g_ffn, w_ffn1, w_ffn2, g_final, loss_target, m_g_mix, m_w_in, m_sgu_ln_w, m_sgu_ln_b, m_sgu_w, m_sgu_b, m_w_proj_a, m_shift_b, m_w_lora_w, m_w0, m_a_lora_w, m_a0, m_g_lora_w, m_k_k, m_k_a, m_r_k, m_ln_x_w, m_ln_x_b, m_w_proj_b, m_w_out, m_g_ffn, m_w_ffn1, m_w_ffn2, m_g_final, v_g_mix, v_w_in, v_sgu_ln_w, v_sgu_ln_b, v_sgu_w, v_sgu_b, v_w_proj_a, v_shift_b, v_w_lora_w, v_w0, v_a_lora_w, v_a0, v_g_lora_w, v_k_k, v_k_a, v_r_k, v_ln_x_w, v_ln_x_b, v_w_proj_b, v_w_out, v_g_ffn, v_w_ffn1, v_w_ffn2, v_g_final):
    given = dict(zip(WEIGHTS, (g_mix, w_in, sgu_ln_w, sgu_ln_b, sgu_w, sgu_b, w_proj_a, shift_b, w_lora_w, w0, a_lora_w, a0, g_lora_w, k_k, k_a, r_k, ln_x_w, ln_x_b, w_proj_b, w_out, g_ffn, w_ffn1, w_ffn2, g_final)))
    mom_m = dict(zip(WEIGHTS, (m_g_mix, m_w_in, m_sgu_ln_w, m_sgu_ln_b, m_sgu_w, m_sgu_b, m_w_proj_a, m_shift_b, m_w_lora_w, m_w0, m_a_lora_w, m_a0, m_g_lora_w, m_k_k, m_k_a, m_r_k, m_ln_x_w, m_ln_x_b, m_w_proj_b, m_w_out, m_g_ffn, m_w_ffn1, m_w_ffn2, m_g_final)))
    mom_v = dict(zip(WEIGHTS, (v_g_mix, v_w_in, v_sgu_ln_w, v_sgu_ln_b, v_sgu_w, v_sgu_b, v_w_proj_a, v_shift_b, v_w_lora_w, v_w0, v_a_lora_w, v_a0, v_g_lora_w, v_k_k, v_k_a, v_r_k, v_ln_x_w, v_ln_x_b, v_w_proj_b, v_w_out, v_g_ffn, v_w_ffn1, v_w_ffn2, v_g_final)))
    chip = 2 * lax.axis_index("x") + lax.axis_index("y")

    def local_block(tree, n):
        return tree[n] if n == "g_final" else tree[n][0]

    sb = local_block(given, "shift_b")
    lo_part = lambda z: (z - z.astype(BF16).astype(F32)).astype(BF16)
    row_form = lambda tree: (lambda n: _row_form(n, local_block(tree, n)))
    tile16 = lambda z: jnp.pad(z, ((0, 16 - z.shape[0]), (0, HALF_W - z.shape[1])))
    sb_tiles = [tile16(f(sb[:, lanes])) for f in (lambda z: z.astype(BF16), lo_part)
                for lanes in (slice(0, HALF_W), slice(HALF_W, None))]
    tails = [[_half_piece(n, lo_part(local_block(given, n)), h) for n in LORAS] + sb_tiles for h in range(2)]
    pack_w = jnp.stack([_pack_half(EARLY, row_form(given), h, BF16, tails[h]) for h in range(2)])
    gathered, gathered_token = gather_shards(pack_w)
    gathered = lax.dynamic_update_index_in_dim(gathered, pack_w, chip, 0)
    pack_late = jnp.stack([_pack_half(LATE, row_form(given), h, BF16) for h in range(2)])
    late_state, late_token = split_start("gather_start", _gather_copies, 3, pack_late, (N_CHIPS,) + pack_late.shape,
                                         gathered_token)

    def whole(group, got, own):
        half = lambda n, j, h: jnp.where(chip == j, _piece(own[h], n), _piece(got[j, h], n))
        shard = lambda n, j: _join_halves(n, half(n, j, 0), half(n, j, 1))
        return {n: jnp.concatenate([shard(n, j) for j in range(N_CHIPS)],
                                   axis=0 if n == "w_in" else SHARD_AXIS[n]) for n in group}

    w = whole(EARLY, gathered, pack_w)
    late_weights = lambda after: whole(
        LATE, gather_forward(split_wait("gather_wait", _gather_copies, late_state, after)[1]), pack_late)
    off = LO_OFF
    for n in LORAS:
        r, cols = PIECE_ROWS[n], SHARD_SHAPE[n][1]
        lo = jnp.concatenate([jnp.concatenate([gathered[j, 0, off:off + r, :cols], gathered[j, 1, off:off + r, :cols]],
                                              axis=0) for j in range(N_CHIPS)], axis=1)
        w[n] = w[n].astype(F32) + lo.astype(F32)
        off += r
    sb_tile = lambda j, t, lanes: gathered[j, 0, off + 16 * t:off + 16 * t + 2, :lanes].astype(F32)
    rest = SHIFT_SHARD[1] - HALF_W
    w["shift_b"] = jnp.concatenate(
        [jnp.concatenate([sb_tile(j, 0, HALF_W) + sb_tile(j, 2, HALF_W), sb_tile(j, 1, rest) + sb_tile(j, 3, rest)],
                         axis=1) for j in range(N_CHIPS)], axis=1)
    for n in SMALL:
        w[n] = local_block(given, n).reshape(SMALL_SHAPE[n])

    def pair_start(g_pack, tag):
        return split_start("reduce_pair_start_" + tag, _pair_copies, N_CHIPS, g_pack, (N_CHIPS,) + g_pack.shape[2:])

    def pair_finish(state, after, tag):
        return pair_sum(*split_wait("reduce_pair_wait_" + tag, _pair_copies, state, after), tag, tm=PACK_TILE)

    pack_early = lambda g: jnp.stack([jnp.stack([_pack_half(EARLY, lambda n: _grad_row_form(n, g[n], j), h, F32)
                                                 for h in range(2)]) for j in range(N_CHIPS)])
    loss, grad_x, grads, (late_part, late_slots), early_state = local_step(
        x[0], loss_target[0], w, late_token, late_weights, pair_start, pair_finish, pack_early)

    early_part, early_part16 = pair_finish(early_state, grad_x, "early")
    s_pack = _pack_rows([grads[n] for n in SMALL] + [grads["shift_b"], loss.reshape(1, 1)], SMALL_ROWS, F32)
    chips_state, token = split_start("reduce_chips_start", _chip_copies, 3, early_part16, early_part16.shape)
    my_chip = lambda: 2 * lax.axis_index("x") + lax.axis_index("y")
    out_g, out_d, out_m, out_v = {}, {}, {}, {}

    def finish(group, tag, part, slots):
        half_sum = sum_with_own(part, slots, my_chip, token, tm=PACK_TILE, name="chip_sum_" + tag)
        other_half = exchange_halves(half_sum, tag)
        for n in group:
            res = adamw_weight(n, half_sum, other_half,
                               *[_row_form(n, local_block(t, n)) for t in (given, mom_m, mom_v)])
            for tree, z in zip((out_g, out_d, out_m, out_v), res):
                tree[n] = _row_form(n, z)

    finish(LATE, "late", late_part, late_slots)

    small_shapes = [SMALL_SHAPE[n] for n in SMALL]
    g_small = sum_all(s_pack, token)
    w_small = _pack_rows([local_block(given, n) for n in SMALL], SMALL_ROWS, F32)
    m_small = _pack_rows([local_block(mom_m, n) for n in SMALL], SMALL_ROWS, F32)
    v_small = _pack_rows([local_block(mom_v, n) for n in SMALL], SMALL_ROWS, F32)
    d_small, nm_small, nv_small = adamw_call(g_small, w_small, m_small, v_small, tm=SMALL_ROWS, name="adamw_small")
    *g_parts, loss = _unpack_rows(g_small, small_shapes + [(2, N_RWKV), ()])
    out_g.update(zip(SMALL, g_parts[:-1]))
    out_d.update(zip(SMALL, _unpack_rows(d_small, small_shapes)))
    out_m.update(zip(SMALL, _unpack_rows(nm_small, small_shapes)))
    out_v.update(zip(SMALL, _unpack_rows(nv_small, small_shapes)))
    g_sb = lax.dynamic_slice_in_dim(g_parts[-1], chip * SHIFT_SHARD[1], SHIFT_SHARD[1], axis=1)
    sb_args = [_pack_rows([z], 8, F32) for z in (g_sb, sb, local_block(mom_m, "shift_b"), local_block(mom_v, "shift_b"))]
    sb_res = adamw_call(*sb_args, tm=8, name="adamw_shift_b")
    out_g["shift_b"] = g_sb
    for tree, res in zip((out_d, out_m, out_v), sb_res):
        tree["shift_b"] = _unpack_rows(res, [SHIFT_SHARD])[0]

    after = (out_v["w_out"], nv_small, sb_res[2])
    early_slots = split_wait("reduce_chips_wait", _chip_copies, chips_state,
                             jnp.concatenate([z.reshape(-1)[:8] for z in after]))[1]
    finish(EARLY, "early", early_part, early_slots)

    def block_of(tree, n):
        return tree[n].reshape(given[n].shape)

    return (loss, grad_x[None], *[block_of(out_g, n) for n in WEIGHTS], *[block_of(out_d, n) for n in WEIGHTS],
            *[block_of(out_m, n) for n in WEIGHTS], *[block_of(out_v, n) for n in WEIGHTS])
```

```python
import functools

import jax
import jax.numpy as jnp
from jax import lax
from jax.experimental import pallas as pl
from jax.experimental.pallas import tpu as pltpu

F32 = jnp.float32
BF16 = jnp.bfloat16

D_MODEL = 1024
N_HEADS = 16
HEAD = 64
SCAN_CHUNK = 64

VMEM_LIMIT = 56 * 1024 * 1024


_BDIMS = {
    "nn": (((2,), (1,)), ((0,), (0,))),
    "nt": (((2,), (2,)), ((0,), (0,))),
    "tn": (((1,), (1,)), ((0,), (0,))),
}


def _raw_bdot(x, y, mode, fine):
    if fine:
        return lax.dot_general(x, y, _BDIMS[mode], precision=lax.Precision.HIGH, preferred_element_type=F32)
    return lax.dot_general(x.astype(BF16), y.astype(BF16), _BDIMS[mode], preferred_element_type=F32)


@functools.partial(jax.custom_vjp, nondiff_argnums=(2, 3))
def bdot(x, y, mode, fine=True):
    return _raw_bdot(x, y, mode, fine)


def _bdot_fwd(x, y, mode, fine):
    return _raw_bdot(x, y, mode, fine), (x, y)


def _bdot_bwd(mode, fine, res, g):
    x, y = res
    if mode == "nn":
        return bdot(g, y, "nt", fine), bdot(x, g, "tn", fine)
    if mode == "nt":
        return bdot(g, y, "nn", fine), bdot(g, x, "tn", fine)
    return bdot(y, g, "nt", fine), bdot(x, g, "nn", fine)


bdot.defvjp(_bdot_fwd, _bdot_bwd)


def _scan_chunk(S0, r, lw, k, v, a, b):
    nh, lc, _ = r.shape
    ti = lax.broadcasted_iota(jnp.int32, (lc, lc), 0)
    si = lax.broadcasted_iota(jnp.int32, (lc, lc), 1)
    incl = (si <= ti).astype(F32)
    strict = (si < ti).astype(F32)
    eye = (si == ti).astype(F32)
    cl = bdot(jnp.broadcast_to(incl, (nh, lc, lc)), lw, "nn")
    cl_last = cl[:, lc - 1:lc, :]
    g_last = jnp.exp(cl_last - cl)
    at = a * jnp.exp(cl - lw)
    bt = b * jnp.exp(-cl)
    kt = k * jnp.exp(-cl)
    rt = r * jnp.exp(cl)
    ar = jnp.concatenate([at, rt], axis=1)
    ar_b = bdot(ar, bt, "nt", False)
    ar_k = bdot(ar, kt, "nt", False)
    m_ab, m_rb = ar_b[:, :lc] * strict, ar_b[:, lc:] * incl
    m_ak, m_rk = ar_k[:, :lc] * strict, ar_k[:, lc:] * incl
    x = eye + m_ab
    p = bdot(m_ab, m_ab, "nn", False)
    n = 2
    while n * 2 < lc:
        px = bdot(jnp.concatenate([p, x], axis=1), p, "nn", False)
        p = px[:, :lc]
        x = x + px[:, lc:]
        n *= 2
    x = x + bdot(x, p, "nn", False)
    ar_s = bdot(ar, S0, "nt", False)
    akrk_v = bdot(jnp.concatenate([m_ak, m_rk], axis=1), v, "nn", False)
    u = bdot(x, ar_s[:, :lc] + akrk_v[:, :lc], "nn", False)
    o = ar_s[:, lc:] + bdot(m_rb, u, "nn", False) + akrk_v[:, lc:]
    s_last = S0 * jnp.exp(cl_last) + bdot(jnp.concatenate([u, v], axis=1),
                                          jnp.concatenate([b * g_last, k * g_last], axis=1), "tn", False)
    return o, s_last


def _split_heads(z):
    return jnp.stack([z[:, HEAD * h:HEAD * (h + 1)] for h in range(N_HEADS)], axis=0)


def _merge_heads(z):
    return jnp.concatenate([z[h] for h in range(N_HEADS)], axis=1)


def _scan_specs(t, ops, rev):
    nc = t // SCAN_CHUNK
    row = (lambda c: nc - 1 - c) if rev else (lambda c: c)
    specs = [pl.BlockSpec((SCAN_CHUNK, D_MODEL), lambda c, cb=cb: (row(c), cb)) for _, cb in ops]
    state = pl.BlockSpec((1, N_HEADS, HEAD, HEAD), lambda c: (row(c), 0, 0, 0))
    return nc, specs, state


def scan_fwd(ops):
    t = ops[0][0].shape[0]
    nc, specs, state = _scan_specs(t, ops, False)

    def body(r_ref, lw_ref, k_ref, v_ref, a_ref, b_ref, o_ref, s0_ref, s_scr):
        @pl.when(pl.program_id(0) == 0)
        def _():
            s_scr[...] = jnp.zeros_like(s_scr)

        s0 = s_scr[...]
        s0_ref[0] = s0
        o, s_last = _scan_chunk(s0, *[_split_heads(z[...]) for z in (r_ref, lw_ref, k_ref, v_ref, a_ref, b_ref)])
        o_ref[...] = _merge_heads(o)
        s_scr[...] = s_last

    return pl.pallas_call(
        body,
        name="scan_fwd",
        grid=(nc,),
        in_specs=specs,
        out_specs=[pl.BlockSpec((SCAN_CHUNK, D_MODEL), lambda c: (c, 0)), state],
        out_shape=[jax.ShapeDtypeStruct((t, D_MODEL), F32), jax.ShapeDtypeStruct((nc, N_HEADS, HEAD, HEAD), F32)],
        scratch_shapes=[pltpu.VMEM((N_HEADS, HEAD, HEAD), F32)],
        compiler_params=_cparams(1),
    )(*[a for a, _ in ops])


def scan_bwd(ops, s0s, do, part):
    t = ops[0][0].shape[0]
    nc, specs, state = _scan_specs(t, ops + [(do, 0)], True)

    def body(r_ref, lw_ref, k_ref, v_ref, a_ref, b_ref, do_ref, s0_ref, part_ref, *rest):
        out_refs, slots_ref, ds_scr, send_sems, recv_sems = rest[:6], rest[6], rest[7], rest[8], rest[9]
        step = pl.program_id(0)
        x, y, c = _coords()
        me = 2 * x + y
        chips = _other_chips(x, y)
        sends = [_remote(part_ref.at[2 * cx + cy], slots_ref.at[me], send_sems, recv_sems, k, (cx, cy, c))
                 for k, (cx, cy) in enumerate(chips)]

        @pl.when(step == 0)
        def _():
            ds_scr[...] = jnp.zeros_like(ds_scr)
            for cp in sends:
                cp.start()

        _, vjp = jax.vjp(_scan_chunk, s0_ref[0],
                         *[_split_heads(z[...]) for z in (r_ref, lw_ref, k_ref, v_ref, a_ref, b_ref)])
        grads = vjp((_split_heads(do_ref[...]), ds_scr[...]))
        for o_ref, g in zip(out_refs, grads[1:]):
            o_ref[...] = _merge_heads(g)
        ds_scr[...] = grads[0]

        @pl.when(step == nc - 1)
        def _():
            for k, (cx, cy) in enumerate(chips):
                _remote(part_ref.at[me], slots_ref.at[2 * cx + cy], send_sems, recv_sems, k, (cx, cy, c)).wait_recv()
            for cp in sends:
                cp.wait_send()

    return pl.pallas_call(
        body,
        name="scan_bwd",
        grid=(nc,),
        in_specs=specs + [state, _ANY],
        out_specs=[pl.BlockSpec((SCAN_CHUNK, D_MODEL), lambda c: (nc - 1 - c, 0))] * 6 + [_ANY],
        out_shape=[jax.ShapeDtypeStruct((t, D_MODEL), F32)] * 6 + [jax.ShapeDtypeStruct(part.shape, part.dtype)],
        scratch_shapes=[pltpu.VMEM((N_HEADS, HEAD, HEAD), F32), pltpu.SemaphoreType.DMA((3,)),
                        pltpu.SemaphoreType.DMA((3,))],
        compiler_params=_cparams(1),
    )(*[a for a, _ in ops], do, s0s, part)


_MDIMS = {
    "nn": (((1,), (0,)), ((), ())),
    "nt": (((1,), (1,)), ((), ())),
    "tn": (((0,), (0,)), ((), ())),
}


def _raw_mdot(x, y, mode, exact):
    if exact:
        return lax.dot_general(x, y, _MDIMS[mode], precision=lax.Precision.HIGH, preferred_element_type=F32)
    return lax.dot_general(x.astype(BF16), y.astype(BF16), _MDIMS[mode], preferred_element_type=F32)


@functools.partial(jax.custom_vjp, nondiff_argnums=(2, 3))
def mdot(x, y, mode, exact):
    return _raw_mdot(x, y, mode, exact)


def _mdot_fwd(x, y, mode, exact):
    return _raw_mdot(x, y, mode, exact), (x, y)


def _mdot_bwd(mode, exact, res, g):
    x, y = res
    if mode == "nn":
        return mdot(g, y, "nt", exact), mdot(x, g, "tn", exact)
    if mode == "nt":
        return mdot(g, y, "nn", exact), mdot(g, x, "tn", exact)
    return mdot(y, g, "nt", exact), mdot(x, g, "nn", exact)


mdot.defvjp(_mdot_fwd, _mdot_bwd)


def _seg_ones():
    i = lax.broadcasted_iota(jnp.int32, (256, 256), 0) // HEAD
    j = lax.broadcasted_iota(jnp.int32, (256, 256), 1) // HEAD
    return (i == j).astype(BF16)


@jax.custom_vjp
def segsum(x):
    bd = _seg_ones()
    hi = x.astype(BF16)
    lo = (x - hi.astype(F32)).astype(BF16)
    cols = []
    for j in range(x.shape[1] // 256):
        sl = slice(256 * j, 256 * (j + 1))
        cols.append(jnp.dot(hi[:, sl], bd, preferred_element_type=F32)
                    + jnp.dot(lo[:, sl], bd, preferred_element_type=F32))
    return jnp.concatenate(cols, axis=1)


segsum.defvjp(lambda x: (segsum(x), None), lambda _, g: (segsum(g),))


NORM_EPS = 1e-6
LN_EPS = 1e-5
GN_EPS = 64e-5
SGU_CHUNK = 128
SGU_GROUPS = 8


def _rms(x, g):
    return x * lax.rsqrt(jnp.mean(x * x, axis=-1, keepdims=True) + NORM_EPS) * g


def f_norm_in(x, g):
    return _rms(x, g), x


def f_sgu(p, ln_w, ln_b, sw, sbt):
    tm = p.shape[0]
    z = 0.5 * p * (1.0 + lax.erf(p * 0.7071067811865476))
    u, v = z[:, :D_MODEL], z[:, D_MODEL:]
    mu = jnp.mean(v, axis=-1, keepdims=True)
    d = v - mu
    vn = d * lax.rsqrt(jnp.mean(d * d, axis=-1, keepdims=True) + LN_EPS) * ln_w + ln_b
    ii = lax.broadcasted_iota(jnp.int32, (SGU_CHUNK, SGU_CHUNK), 0)
    jj = lax.broadcasted_iota(jnp.int32, (SGU_CHUNK, SGU_CHUNK), 1)
    mask = (jj <= ii).astype(F32)
    gi = lax.broadcasted_iota(jnp.int32, (SGU_GROUPS, D_MODEL), 0)
    ci = lax.broadcasted_iota(jnp.int32, (SGU_GROUPS, D_MODEL), 1) // SGU_CHUNK
    bias = mdot(sbt, (gi == ci).astype(F32), "nn", True)
    rows = []
    for c in range(tm // SGU_CHUNK):
        cols = []
        for g in range(SGU_GROUPS):
            blk = vn[c * SGU_CHUNK:(c + 1) * SGU_CHUNK, g * SGU_CHUNK:(g + 1) * SGU_CHUNK]
            cols.append(mdot(sw[g] * mask, blk, "nn", False))
        rows.append(jnp.concatenate(cols, axis=1) + bias)
    return (u * jnp.concatenate(rows, axis=0),)


def _softplus(x):
    return jnp.maximum(x, 0.0) + jnp.log1p(jnp.exp(-jnp.abs(x)))


def f_pre(q, wl, w0, al, a0, gl, k_k, k_a):
    qr, qk, qv, ql = q[:, :1024], q[:, 1024:2048], q[:, 2048:3072], q[:, 3072:]
    return _f_pre(qr, qk, qv, ql, wl, w0, al, a0, gl, k_k, k_a)


def _f_pre(qr, qk, qv, ql, wl, w0, al, a0, gl, k_k, k_a):
    xw, xa, xg = ql[:, :128], ql[:, 128:256], ql[:, 256:512]
    wr = -_softplus(-(w0 + mdot(jnp.tanh(xw), wl, "nn", False))) - 0.5
    lw = -jnp.exp(wr)
    aa = jax.nn.sigmoid(a0 + mdot(xa, al, "nn", False))
    g = mdot(jax.nn.sigmoid(xg), gl, "nn", False)
    kkr = qk * k_k
    kk = kkr / jnp.maximum(jnp.sqrt(segsum(kkr * kkr)), 1e-12)
    kp = qk * (1.0 + (aa - 1.0) * k_a)
    return qr, lw, kp, qv, -kk, kk * aa, g, qr, kp, qv


def f_post(o, r, kp, v, g, lnw, lnb, rk):
    mu = segsum(o) * (1.0 / HEAD)
    d = o - mu
    gn = d * lax.rsqrt(segsum(d * d) * (1.0 / HEAD) + GN_EPS)
    return ((gn * lnw + lnb + segsum(r * kp * rk) * v) * g,)


def f_mix(ya, yb, ga, gb):
    return (jax.nn.sigmoid(ga) * ya + jax.nn.sigmoid(gb) * yb,)


def f_ffn_in(h1, g):
    return _rms(h1, g), h1


def f_final(h1, m3, tgt, g):
    y = _rms(h1 + m3, g)
    err = jnp.square(y - tgt)
    return 0.5 * jnp.sum(jnp.mean(err, axis=-1))


def _cparams(n_grid):
    return pltpu.CompilerParams(dimension_semantics=("arbitrary",) * n_grid, vmem_limit_bytes=VMEM_LIMIT)


def _tile_spec(tm, w, cb):
    return pl.BlockSpec((tm, w), lambda i: (i, cb))


def _const_spec(c):
    nd = c.ndim
    return pl.BlockSpec(c.shape, lambda i: (0,) * nd)


def ew_call(fn, tiled, consts, outs, *, tm, name):
    t = tiled[0][0].shape[0]
    n_t, n_c = len(tiled), len(consts)

    def body(*refs):
        tv = [r[...].astype(F32) for r in refs[:n_t]]
        cv = [r[...] for r in refs[n_t:n_t + n_c]]
        res = fn(*tv, *cv)
        for o_ref, val in zip(refs[n_t + n_c:], res):
            o_ref[...] = val.astype(o_ref.dtype)

    return pl.pallas_call(
        body,
        name=name,
        grid=(t // tm,),
        in_specs=[_tile_spec(tm, w, cb) for _, w, cb in tiled] + [_const_spec(c) for c in consts],
        out_specs=[_tile_spec(tm, w, 0) for w, _ in outs],
        out_shape=[jax.ShapeDtypeStruct((t, w), dt) for w, dt in outs],
        compiler_params=_cparams(1),
    )(*[a for a, _, _ in tiled], *consts)


def ew_vjp_call(fn, tiled, consts, cots, d_tiled, d_consts, *, tm, name):
    t = tiled[0][0].shape[0]
    n_t, n_c, n_g = len(tiled), len(consts), len(cots)
    dt_list = [(i, dt) for i, dts in enumerate(d_tiled) for dt in dts]
    dc_list = [i for i, want in enumerate(d_consts) if want]

    def body(*refs):
        tv = [r[...].astype(F32) for r in refs[:n_t]]
        cv = [r[...] for r in refs[n_t:n_t + n_c]]
        gv = tuple(r[...].astype(F32) for r in refs[n_t + n_c:n_t + n_c + n_g])
        out_refs = refs[n_t + n_c + n_g:]
        _, vjp = jax.vjp(fn, *tv, *cv)
        grads = vjp(gv)
        for o_ref, (i, _) in zip(out_refs, dt_list):
            o_ref[...] = grads[i].astype(o_ref.dtype)
        acc_refs = out_refs[len(dt_list):]

        @pl.when(pl.program_id(0) == 0)
        def _():
            for a_ref in acc_refs:
                a_ref[...] = jnp.zeros_like(a_ref)

        for a_ref, i in zip(acc_refs, dc_list):
            a_ref[...] += grads[n_t + i]

    res = pl.pallas_call(
        body,
        name=name,
        grid=(t // tm,),
        in_specs=[_tile_spec(tm, w, cb) for _, w, cb in tiled] + [_const_spec(c) for c in consts]
        + [_tile_spec(tm, w, cb) for _, w, cb in cots],
        out_specs=[_tile_spec(tm, tiled[i][1], 0) for i, _ in dt_list] + [_const_spec(consts[i]) for i in dc_list],
        out_shape=[jax.ShapeDtypeStruct((t, tiled[i][1]), dt) for i, dt in dt_list]
        + [jax.ShapeDtypeStruct(consts[i].shape, F32) for i in dc_list],
        compiler_params=_cparams(1),
    )(*[a for a, _, _ in tiled], *consts, *[a for a, _, _ in cots])
    return res[:len(dt_list)], res[len(dt_list):]


def mm(a, b, mode, *, tm, tn, name, out_dtypes=(F32,), epi=None, extras=(), into=None):
    m = a.shape[1] if mode == "tn" else a.shape[0]
    kd = a.shape[0] if mode == "tn" else a.shape[1]
    n = b.shape[0] if mode == "nt" else b.shape[1]
    tm, tn = min(tm, m), min(tn, n)
    if mode == "nn":
        a_spec = pl.BlockSpec((tm, kd), lambda i, j: (i, 0))
        b_spec = pl.BlockSpec((kd, tn), lambda i, j: (0, j))
    elif mode == "nt":
        a_spec = pl.BlockSpec((tm, kd), lambda i, j: (i, 0))
        b_spec = pl.BlockSpec((tn, kd), lambda i, j: (j, 0))
    else:
        a_spec = pl.BlockSpec((kd, tm), lambda i, j: (0, i))
        b_spec = pl.BlockSpec((kd, tn), lambda i, j: (0, j))
    n_e = len(extras)
    o_spec = pl.BlockSpec((tm, tn), lambda i, j: (i, j))

    if into is not None:
        buf, place = into

        def body_into(a_ref, b_ref, buf_ref, o_ref):
            o_ref[0, 0] = lax.dot_general(a_ref[...].astype(BF16), b_ref[...].astype(BF16), _MDIMS[mode],
                                          preferred_element_type=F32)

        return pl.pallas_call(
            body_into,
            name=name,
            grid=(m // tm, n // tn),
            in_specs=[a_spec, b_spec, pl.BlockSpec(memory_space=pl.ANY)],
            out_specs=pl.BlockSpec((1, 1, tm, tn), lambda i, j: (*place(i, j), 0)),
            out_shape=jax.ShapeDtypeStruct(buf.shape, F32),
            input_output_aliases={2: 0},
            compiler_params=_cparams(2),
        )(a, b, buf)

    def body(a_ref, b_ref, *refs):
        c = lax.dot_general(a_ref[...].astype(BF16), b_ref[...].astype(BF16), _MDIMS[mode],
                            preferred_element_type=F32)
        res = epi(c, *[r[...] for r in refs[:n_e]]) if epi is not None else (c,)
        for o_ref, val in zip(refs[n_e:], res):
            o_ref[...] = val.astype(o_ref.dtype)

    res = pl.pallas_call(
        body,
        name=name,
        grid=(m // tm, n // tn),
        in_specs=[a_spec, b_spec] + [o_spec] * n_e,
        out_specs=[o_spec] * len(out_dtypes),
        out_shape=[jax.ShapeDtypeStruct((m, n), dt) for dt in out_dtypes],
        compiler_params=_cparams(2),
    )(a, b, *extras)
    return res if len(out_dtypes) > 1 else res[0]


RWKV_COL0 = 4096
RWKV_WIDTH = 3584
SHIFT_BLK = 512


def _shift_down(p, prev_row):
    rows = lax.broadcasted_iota(jnp.int32, p.shape, 0)
    return jnp.where(rows == 0, prev_row, pltpu.roll(p, 1, 0))


def shiftmix_fwd(p_all, sbp, *, tm):
    t = p_all.shape[0]
    tm = min(tm, t)
    c0 = RWKV_COL0 // SHIFT_BLK
    hb = tm // 8

    def body(p_ref, halo_ref, sb_ref, q_ref):
        p = p_ref[...]
        prev = jnp.where(pl.program_id(0) == 0, 0.0, halo_ref[7:8, :])
        q_ref[...] = p * sb_ref[0:1, :] + _shift_down(p, prev) * sb_ref[1:2, :]

    return pl.pallas_call(
        body,
        name="shiftmix_fwd",
        grid=(t // tm, RWKV_WIDTH // SHIFT_BLK),
        in_specs=[
            pl.BlockSpec((tm, SHIFT_BLK), lambda i, j: (i, c0 + j)),
            pl.BlockSpec((8, SHIFT_BLK), lambda i, j: (jnp.maximum(i * hb - 1, 0), c0 + j)),
            pl.BlockSpec((2, SHIFT_BLK), lambda i, j: (0, j)),
        ],
        out_specs=pl.BlockSpec((tm, SHIFT_BLK), lambda i, j: (i, j)),
        out_shape=jax.ShapeDtypeStruct((t, RWKV_WIDTH), F32),
        compiler_params=_cparams(2),
    )(p_all, p_all, sbp)


def shiftmix_bwd(dq, col0, p_all, sbp, *, tm, name):
    t, w = dq.shape
    n_i = t // tm
    hb = tm // 8
    cq = col0 // SHIFT_BLK
    cp = (RWKV_COL0 + col0) // SHIFT_BLK

    def body(dq_ref, dqn_ref, p_ref, ph_ref, sb_ref, dp_ref, dsb_ref):
        i = pl.program_id(1)
        dq_t = dq_ref[...]
        rows = lax.broadcasted_iota(jnp.int32, dq_t.shape, 0)
        nxt = jnp.where(i == n_i - 1, 0.0, dqn_ref[0:1, :])
        up = jnp.where(rows == tm - 1, nxt, pltpu.roll(dq_t, tm - 1, 0))
        dp_ref[...] = (dq_t * sb_ref[0:1, :] + up * sb_ref[1:2, :]).astype(dp_ref.dtype)
        p = p_ref[...]
        prev = jnp.where(i == 0, 0.0, ph_ref[7:8, :])
        s0 = jnp.sum(dq_t * p, axis=0, keepdims=True)
        s1 = jnp.sum(dq_t * _shift_down(p, prev), axis=0, keepdims=True)
        two = lax.broadcasted_iota(jnp.int32, (2, SHIFT_BLK), 0)

        @pl.when(i == 0)
        def _():
            dsb_ref[...] = jnp.zeros_like(dsb_ref)

        dsb_ref[...] += jnp.where(two == 0, s0, s1)

    return pl.pallas_call(
        body,
        name=name,
        grid=(w // SHIFT_BLK, n_i),
        in_specs=[
            pl.BlockSpec((tm, SHIFT_BLK), lambda j, i: (i, j)),
            pl.BlockSpec((8, SHIFT_BLK), lambda j, i: (jnp.minimum((i + 1) * hb, t // 8 - 1), j)),
            pl.BlockSpec((tm, SHIFT_BLK), lambda j, i: (i, cp + j)),
            pl.BlockSpec((8, SHIFT_BLK), lambda j, i: (jnp.maximum(i * hb - 1, 0), cp + j)),
            pl.BlockSpec((2, SHIFT_BLK), lambda j, i: (0, cq + j)),
        ],
        out_specs=[
            pl.BlockSpec((tm, SHIFT_BLK), lambda j, i: (i, j)),
            pl.BlockSpec((2, SHIFT_BLK), lambda j, i: (0, j)),
        ],
        out_shape=[jax.ShapeDtypeStruct((t, w), BF16), jax.ShapeDtypeStruct((2, w), F32)],
        compiler_params=_cparams(2),
    )(dq, dq, p_all, p_all, sbp)


def final_call(h1, m3, tgt, g_final, *, tm):
    t = h1.shape[0]

    def body(h1_ref, m3_ref, tgt_ref, g_ref, dh_ref, dhb_ref, dg_ref, loss_ref):
        loss, vjp = jax.vjp(f_final, h1_ref[...], m3_ref[...], tgt_ref[...], g_ref[...])
        dh, _, _, dg = vjp(jnp.ones((), F32))
        dh_ref[...] = dh
        dhb_ref[...] = dh.astype(BF16)

        @pl.when(pl.program_id(0) == 0)
        def _():
            dg_ref[...] = jnp.zeros_like(dg_ref)
            loss_ref[...] = jnp.zeros_like(loss_ref)

        dg_ref[...] += dg
        loss_ref[...] += jnp.full(loss_ref.shape, loss, F32)

    tile = _tile_spec(tm, D_MODEL, 0)
    return pl.pallas_call(
        body,
        name="final_loss",
        grid=(t // tm,),
        in_specs=[tile, tile, tile, _const_spec(g_final)],
        out_specs=[tile, tile, _const_spec(g_final), pl.BlockSpec((8, 128), lambda i: (0, 0))],
        out_shape=[jax.ShapeDtypeStruct((t, D_MODEL), F32), jax.ShapeDtypeStruct((t, D_MODEL), BF16),
                   jax.ShapeDtypeStruct(g_final.shape, F32), jax.ShapeDtypeStruct((8, 128), F32)],
        compiler_params=_cparams(1),
    )(h1, m3, tgt, g_final)


N_SGU = 2048
N_RWKV = 3360
LORA_W, LORA_A, LORA_G = 64, 64, 160


def _pad_rwkv_cols(z):
    zero = lambda n: jnp.zeros(z.shape[:-1] + (n,), z.dtype)
    return jnp.concatenate([z[..., :3072], z[..., 3072:3136], zero(64), z[..., 3136:3200], zero(64),
                            z[..., 3200:3360], zero(96)], axis=-1)


def _unpad_rwkv_cols(z):
    return jnp.concatenate([z[..., :3072], z[..., 3072:3136], z[..., 3200:3264], z[..., 3328:3488]], axis=-1)


def _pad_win_rows(wt):
    z = wt[N_SGU:N_SGU + N_RWKV]
    zero = lambda n: jnp.zeros((n, wt.shape[1]), wt.dtype)
    return jnp.concatenate([wt[:N_SGU], wt[N_SGU + N_RWKV:], z[:3072], z[3072:3136], zero(64), z[3136:3200], zero(64),
                            z[3200:3360], zero(96)], axis=0)


def _unpad_win_rows(wt):
    z = wt[RWKV_COL0:]
    return jnp.concatenate([wt[:N_SGU], z[:3072], z[3072:3136], z[3200:3264], z[3328:3488], wt[N_SGU:RWKV_COL0]],
                           axis=0)


def _pad_rows(w, n):
    return jnp.concatenate([w, jnp.zeros((n - w.shape[0],) + w.shape[1:], w.dtype)], axis=0)


def _relu2_epi(c):
    return c, jnp.square(jnp.maximum(c, 0.0))


def _relu2_bwd_epi(c, hid):
    return (c * (2.0 * jnp.maximum(hid.astype(F32), 0.0)),)


def _add_epi(c, x):
    return (c + x,)


def _pre_fwd(*args):
    res = f_pre(*args)
    return res[1], res[2], res[4], res[5], res[6]


def local_step(x, tgt, w, late_token, late_weights, pair_start, pair_finish, pack_early):
    d = D_MODEL
    win_pt = _pad_win_rows(w["w_in"])
    sbp = _pad_rwkv_cols(w["shift_b"])
    wl = _pad_rows(w["w_lora_w"], 128)
    al = _pad_rows(w["a_lora_w"], 128)
    gl = _pad_rows(w["g_lora_w"], 256)
    sbt = w["sgu_b"].T

    (a_bf,) = ew_call(lambda x_, g_: (f_norm_in(x_, g_)[0],), [(x, d, 0)], [w["g_mix"] + late_token[:1, :1]],
                      [(d, BF16)], tm=256, name="norm_in")
    p_all = mm(a_bf, win_pt, "nt", tm=2048, tn=1280, name="mm_in")
    sgu_t = [(p_all, 2 * d, 0)]
    sgu_c = [w["sgu_ln_w"], w["sgu_ln_b"], w["sgu_w"], sbt]
    (s_bf,) = ew_call(f_sgu, sgu_t, sgu_c, [(d, BF16)], tm=256, name="sgu_fwd")
    ya = mm(s_bf, w["w_proj_a"], "nn", tm=512, tn=1024, name="mm_proj_a")
    q = shiftmix_fwd(p_all, sbp, tm=1024)
    pre_t = [(q, RWKV_WIDTH, 0)]
    pre_c = [wl, w["w0"], al, w["a0"], gl, w["k_k"], w["k_a"]]
    lw, kp, na, nb, g = ew_call(_pre_fwd, pre_t, pre_c, [(d, F32)] * 5, tm=256, name="rwkv_pre_fwd")
    scan_ops = [(q, 0), (lw, 0), (kp, 0), (q, 2), (na, 0), (nb, 0)]
    o, s0s = scan_fwd(scan_ops)
    w = {**w, **late_weights(o)}
    post_t = [(o, d, 0), (q, d, 0), (kp, d, 0), (q, d, 2), (g, d, 0)]
    post_c = [w["ln_x_w"], w["ln_x_b"], w["r_k"]]
    (ob_bf,) = ew_call(f_post, post_t, post_c, [(d, BF16)], tm=256, name="rwkv_post_fwd")
    yb = mm(ob_bf, w["w_proj_b"], "nn", tm=512, tn=1024, name="mm_proj_b")
    mix_t = [(ya, d, 0), (yb, d, 0), (p_all, d, 2), (p_all, d, 3)]
    (mixed_bf,) = ew_call(f_mix, mix_t, [], [(d, BF16)], tm=256, name="mix_fwd")
    h1 = mm(mixed_bf, w["w_out"], "nn", tm=512, tn=1024, name="mm_out", epi=_add_epi, extras=(x,))
    (f_bf,) = ew_call(lambda h_, g_: (f_ffn_in(h_, g_)[0],), [(h1, d, 0)], [w["g_ffn"]], [(d, BF16)], tm=256,
                      name="ffn_norm")
    hid, act_bf = mm(f_bf, w["w_ffn1"], "nn", tm=2048, tn=1024, name="mm_ffn1", out_dtypes=(BF16, BF16), epi=_relu2_epi)
    m3 = mm(act_bf, w["w_ffn2"], "nn", tm=1024, tn=512, name="mm_ffn2")
    dh2, dh2_bf, dg_final, loss = final_call(h1, m3, tgt, w["g_final"], tm=256)

    dhid_bf = mm(dh2_bf, w["w_ffn2"], "nt", tm=2048, tn=1024, name="mm_dact", out_dtypes=(BF16,), epi=_relu2_bwd_epi,
                 extras=(hid,))
    late_g = lax.empty((N_CHIPS, 2, PACK_ROWS, HALF_W), F32)
    late_g = mm(act_bf, dh2_bf, "tn", tm=1024, tn=HALF_W, name="mm_dw_ffn2",
                into=(late_g, lambda i, j: (i, j, PIECE_OFF["w_ffn2"] // 1024)))
    df = mm(dhid_bf, w["w_ffn1"], "nt", tm=1024, tn=512, name="mm_df")
    late_g = mm(f_bf, dhid_bf, "tn", tm=1024, tn=HALF_W, name="mm_dw_ffn1",
                into=(late_g, lambda i, j: (j // 2, j % 2, PIECE_OFF["w_ffn1"] // 1024)))
    (dh1, dh1_bf), (dg_ffn,) = ew_vjp_call(f_ffn_in, [(h1, d, 0)], [w["g_ffn"]], [(df, d, 0), (dh2, d, 0)],
                                           [(F32, BF16)], [True], tm=256, name="ffn_norm_bwd")
    dmixed = mm(dh1_bf, w["w_out"], "nt", tm=512, tn=1024, name="mm_dmixed")
    late_g = mm(mixed_bf, dh1_bf, "tn", tm=256, tn=HALF_W, name="mm_dw_out",
                into=(late_g, lambda i, j: (i, j, PIECE_OFF["w_out"] // 256)))
    (dya_bf, dyb_bf, dga_bf, dgb_bf), _ = ew_vjp_call(f_mix, mix_t, [], [(dmixed, d, 0)], [(BF16,)] * 4, [], tm=256,
                                                      name="mix_bwd")
    dob = mm(dyb_bf, w["w_proj_b"], "nt", tm=512, tn=1024, name="mm_dob")
    late_g = mm(ob_bf, dyb_bf, "tn", tm=256, tn=HALF_W, name="mm_dw_proj_b",
                into=(late_g, lambda i, j: (i, j, PIECE_OFF["w_proj_b"] // 256)))
    late_state, late_token = pair_start(late_g, "late")
    post_c_after = [w["ln_x_w"] + late_token[:1, :1]] + post_c[1:]
    (do, dr_p, dkp_p, dv_p, dg), (dlnx_w, dlnx_b, dr_k) = ew_vjp_call(
        f_post, post_t, post_c_after, [(dob, d, 0)], [(F32,)] * 5, [True] * 3, tm=256, name="rwkv_post_bwd")
    late_part, late_part16 = pair_finish(late_state, do, "late")
    *scan_g, late_slots = scan_bwd(scan_ops, s0s, do, late_part16)
    pre_g = [(z, d, 0) for z in scan_g] + [(dg, d, 0), (dr_p, d, 0), (dkp_p, d, 0), (dv_p, d, 0)]
    (dq,), (dwl, dw0, dal, da0, dgl, dk_k, dk_a) = ew_vjp_call(
        f_pre, pre_t, pre_c, pre_g, [(F32,)], [True] * 7, tm=256, name="rwkv_pre_bwd")
    dp_rwkv, dsb = shiftmix_bwd(dq, 0, p_all, sbp, tm=512, name="shiftmix_bwd")
    ds = mm(dya_bf, w["w_proj_a"], "nt", tm=512, tn=1024, name="mm_ds")
    d_proj_a = mm(s_bf, dya_bf, "tn", tm=512, tn=1024, name="mm_dw_proj_a")
    (dp_sgu,), (dln_w, dln_b, dsw, dsbt) = ew_vjp_call(f_sgu, sgu_t, sgu_c, [(ds, d, 0)], [(BF16,)], [True] * 4,
                                                       tm=256, name="sgu_bwd")
    dp_all = jnp.concatenate([dp_sgu, dga_bf, dgb_bf, dp_rwkv], axis=1)
    d_in_pt = mm(dp_all, a_bf, "tn", tm=1280, tn=1024, name="mm_dw_in")
    early_state, early_token = pair_start(pack_early({
        "w_in": _unpad_win_rows(d_in_pt), "w_proj_a": d_proj_a, "w_lora_w": dwl[:LORA_W], "a_lora_w": dal[:LORA_A],
        "g_lora_w": dgl[:LORA_G]}), "early")
    da = mm(dp_all, win_pt, "nn", tm=1024, tn=256, name="mm_da")
    g_mix_after = w["g_mix"] + early_token[:1, :1]
    (grad_x,), (dg_mix,) = ew_vjp_call(f_norm_in, [(x, d, 0)], [g_mix_after], [(da, d, 0), (dh1, d, 0)], [(F32,)],
                                       [True], tm=256, name="norm_in_bwd")

    grads = {
        "g_mix": dg_mix, "sgu_ln_w": dln_w, "sgu_ln_b": dln_b, "sgu_w": dsw, "sgu_b": dsbt.T,
        "shift_b": _unpad_rwkv_cols(dsb),
        "w0": dw0, "a0": da0, "k_k": dk_k, "k_a": dk_a, "r_k": dr_k, "ln_x_w": dlnx_w, "ln_x_b": dlnx_b,
        "g_ffn": dg_ffn, "g_final": dg_final,
    }
    return loss[0, 0], grad_x, grads, (late_part, late_slots), early_state


MESH = pl.DeviceIdType.MESH
N_CHIPS = 4
PACK_ROWS = 2560
PACK_TILE = 512
SMALL_ROWS = 160
_ANY = pl.BlockSpec(memory_space=pl.ANY)


def _coords():
    return lax.axis_index("x"), lax.axis_index("y"), lax.axis_index("c")


def _other_chips(x, y):
    return [(1 - x, y), (x, 1 - y), (1 - x, 1 - y)]


def _remote(src, dst, send_sems, recv_sems, k, to):
    return pltpu.make_async_remote_copy(src_ref=src, dst_ref=dst, send_sem=send_sems.at[k], recv_sem=recv_sems.at[k],
                                        device_id=to, device_id_type=MESH)


def gather_shards(pack):
    def body(src_ref, out_ref, token, send_sems, recv_sems):
        x, y, c = _coords()
        me = 2 * x + y
        sib = (x, y, 1 - c)
        chips = _other_chips(x, y)
        first = [_remote(src_ref.at[c], out_ref.at[me, c], send_sems, recv_sems, k, (cx, cy, c))
                 for k, (cx, cy) in enumerate(chips)]
        for cp in first:
            cp.start()
        passed = []
        for k, (cx, cy) in enumerate(chips):
            j = 2 * cx + cy
            _remote(src_ref.at[c], out_ref.at[j, c], send_sems, recv_sems, k, (cx, cy, c)).wait_recv()
            fwd = _remote(out_ref.at[j, c], out_ref.at[j, c], send_sems, recv_sems, 3 + k, sib)
            fwd.start()
            passed.append(fwd)
        for k, (cx, cy) in enumerate(chips):
            j = 2 * cx + cy
            _remote(out_ref.at[j, 1 - c], out_ref.at[j, 1 - c], send_sems, recv_sems, 3 + k, sib).wait_recv()
        for cp in first + passed:
            cp.wait_send()
        token[...] = jnp.zeros_like(token)

    return pl.pallas_call(
        body,
        name="gather_shards",
        in_specs=[_ANY],
        out_specs=[_ANY, pl.BlockSpec(memory_space=pltpu.VMEM)],
        out_shape=[jax.ShapeDtypeStruct((N_CHIPS,) + pack.shape, pack.dtype), jax.ShapeDtypeStruct((8, 128), F32)],
        scratch_shapes=[pltpu.SemaphoreType.DMA((6,)), pltpu.SemaphoreType.DMA((6,))],
    )(pack)


def _gather_copies(pack_ref, all_ref, send_sems, recv_sems):
    x, y, c = _coords()
    me = 2 * x + y
    return [(_remote(pack_ref.at[c], all_ref.at[me, c], send_sems, recv_sems, k, (cx, cy, c)),
             _remote(pack_ref.at[c], all_ref.at[2 * cx + cy, c], send_sems, recv_sems, k, (cx, cy, c)))
            for k, (cx, cy) in enumerate(_other_chips(x, y))]


_HBM = pl.BlockSpec(memory_space=pltpu.HBM)
_SEM = pl.BlockSpec(memory_space=pltpu.SEMAPHORE)
_SIDE_EFFECT = pltpu.SideEffectType.DATAFLOW_SIDE_EFFECTING


def split_start(name, copies, n, src, land_shape, after=None):
    def body(src_ref, land_ref, *refs):
        send_sems, recv_sems, token = refs[-5], refs[-4], refs[-1]
        for send, _ in copies(src_ref, land_ref, send_sems, recv_sems):
            send.start()
        token[...] = jnp.zeros_like(token)

    extra = () if after is None else (after,)
    *state, token = pl.pallas_call(
        body,
        name=name,
        out_shape=(pltpu.SemaphoreType.DMA((n,)), pltpu.SemaphoreType.DMA((n,)), pltpu.HBM(src.shape, src.dtype),
                   pltpu.HBM(land_shape, src.dtype), jax.ShapeDtypeStruct((8, 128), F32)),
        in_specs=(_HBM, _HBM) + (pl.BlockSpec(memory_space=pl.ANY),) * len(extra),
        out_specs=(_SEM, _SEM, _HBM, _HBM, pl.BlockSpec(memory_space=pltpu.VMEM)),
        input_output_aliases={0: 2, 1: 3},
        compiler_params=pltpu.CompilerParams(has_side_effects=_SIDE_EFFECT),
    )(pltpu.with_memory_space_constraint(src, pltpu.HBM),
      pltpu.with_memory_space_constraint(lax.empty(land_shape, src.dtype), pltpu.HBM), *extra)
    return state, token


def split_wait(name, copies, state, after):
    send_sems, recv_sems, src, land = state

    def body(src_ref, land_ref, send_sems, recv_sems, after_ref, src_out, land_out):
        for send, arrival in copies(src_ref, land_ref, send_sems, recv_sems):
            send.wait_send()
            arrival.wait_recv()

    return pl.pallas_call(
        body,
        name=name,
        out_shape=(pltpu.HBM(src.shape, src.dtype), pltpu.HBM(land.shape, land.dtype)),
        in_specs=(_HBM, _HBM, _SEM, _SEM, pl.BlockSpec(memory_space=pl.ANY)),
        out_specs=(_HBM, _HBM),
        input_output_aliases={0: 0, 1: 1},
        compiler_params=pltpu.CompilerParams(has_side_effects=_SIDE_EFFECT),
    )(src, land, send_sems, recv_sems, after)


def gather_forward(got):
    def body(got_ref, out_ref, send_sems, recv_sems):
        x, y, c = _coords()
        sib = (x, y, 1 - c)
        slots = [2 * cx + cy for cx, cy in _other_chips(x, y)]
        sends = [_remote(got_ref.at[j, c], out_ref.at[j, c], send_sems, recv_sems, k, sib) for k, j in enumerate(slots)]
        for cp in sends:
            cp.start()
        for k, j in enumerate(slots):
            _remote(got_ref.at[j, 1 - c], out_ref.at[j, 1 - c], send_sems, recv_sems, k, sib).wait_recv()
        for cp in sends:
            cp.wait_send()

    return pl.pallas_call(
        body,
        name="gather_forward",
        in_specs=[_ANY],
        out_specs=_ANY,
        out_shape=jax.ShapeDtypeStruct(got.shape, got.dtype),
        input_output_aliases={0: 0},
        scratch_shapes=[pltpu.SemaphoreType.DMA((3,)), pltpu.SemaphoreType.DMA((3,))],
    )(got)


def pair_sum(g, got, tag, *, tm):
    n, _, rows, width = g.shape

    def body(g0_ref, g1_ref, got_ref, out_ref, out16_ref):
        own = jnp.where(lax.axis_index("c") == 0, g0_ref[0, 0], g1_ref[0, 0])
        total = own + got_ref[0]
        out_ref[0] = total
        out16_ref[0] = total.astype(BF16)

    blk = pl.BlockSpec((1, tm, width), lambda j, i: (j, i, 0))
    return pl.pallas_call(
        body,
        name="pair_sum_" + tag,
        grid=(n, rows // tm),
        in_specs=[pl.BlockSpec((1, 1, tm, width), lambda j, i: (j, 0, i, 0)),
                  pl.BlockSpec((1, 1, tm, width), lambda j, i: (j, 1, i, 0)), blk],
        out_specs=[blk, blk],
        out_shape=[jax.ShapeDtypeStruct(got.shape, F32), jax.ShapeDtypeStruct(got.shape, BF16)],
        compiler_params=_cparams(2),
    )(g, g, got)


def _pair_copies(g_ref, got_ref, send_sems, recv_sems):
    x, y, c = _coords()
    copies = [_remote(g_ref.at[j, 1 - c], got_ref.at[j], send_sems, recv_sems, j, (x, y, 1 - c))
              for j in range(N_CHIPS)]
    return [(cp, cp) for cp in copies]


def _chip_copies(p_ref, slots_ref, send_sems, recv_sems):
    x, y, c = _coords()
    me = 2 * x + y
    return [(_remote(p_ref.at[2 * cx + cy], slots_ref.at[me], send_sems, recv_sems, k, (cx, cy, c)),
             _remote(p_ref.at[me], slots_ref.at[2 * cx + cy], send_sems, recv_sems, k, (cx, cy, c)))
            for k, (cx, cy) in enumerate(_other_chips(x, y))]


def sum_with_own(own, slots, index_fn, after, *, tm, name):
    n, rows, width = slots.shape

    def body(*refs):
        mine = index_fn()
        acc = None
        for s in range(n):
            term = jnp.where(mine == s, refs[s][0], refs[n + s][0].astype(F32))
            acc = term if acc is None else acc + term
        refs[-1][...] = acc

    slot_specs = [pl.BlockSpec((1, tm, width), lambda i, s=s: (s, i, 0)) for s in range(n)]
    return pl.pallas_call(
        body,
        name=name,
        grid=(rows // tm,),
        in_specs=slot_specs + slot_specs + [pl.BlockSpec(after.shape, lambda i: (0,) * after.ndim)],
        out_specs=pl.BlockSpec((tm, width), lambda i: (i, 0)),
        out_shape=jax.ShapeDtypeStruct((rows, width), F32),
        compiler_params=_cparams(1),
    )(*([own] * n), *([slots] * n), after)


def exchange_halves(s, tag):
    rq = PACK_TILE
    nq = s.shape[0] // rq

    def body(s_ref, out_ref, sbuf, rbuf, send_sems, recv_sems, in_sems, out_sems):
        x, y, c = _coords()
        sib = (x, y, 1 - c)
        rows = lambda q: pl.ds(q * rq, rq)
        loads = [pltpu.make_async_copy(s_ref.at[rows(q)], sbuf.at[rows(q)], in_sems.at[q]) for q in range(nq)]
        for cp in loads:
            cp.start()
        sends = []
        for q in range(nq):
            loads[q].wait()
            sends.append(_remote(sbuf.at[rows(q)], rbuf.at[rows(q)], send_sems, recv_sems, q, sib))
            sends[q].start()
        stores = []
        for q in range(nq):
            sends[q].wait_recv()
            stores.append(pltpu.make_async_copy(rbuf.at[rows(q)], out_ref.at[rows(q)], out_sems.at[q]))
            stores[q].start()
        for cp in sends:
            cp.wait_send()
        for cp in stores:
            cp.wait()

    return pl.pallas_call(
        body,
        name="exchange_halves_" + tag,
        in_specs=[_ANY],
        out_specs=_ANY,
        out_shape=jax.ShapeDtypeStruct(s.shape, s.dtype),
        scratch_shapes=[pltpu.VMEM(s.shape, s.dtype), pltpu.VMEM(s.shape, s.dtype)]
        + [pltpu.SemaphoreType.DMA((nq,))] * 4,
        compiler_params=pltpu.CompilerParams(vmem_limit_bytes=VMEM_LIMIT),
    )(s)


def sum_all(s, after):
    rows = s.shape[0]
    half = rows // 2

    def body(s_ref, after_ref, out_ref, theirs, pair, slots, send_sems, recv_sems):
        x, y, c = _coords()
        me = 2 * x + y
        sib = (x, y, 1 - c)
        chips = _other_chips(x, y)
        swap = _remote(s_ref, theirs, send_sems, recv_sems, 0, sib)
        swap.start()
        swap.wait_recv()
        pair[...] = s_ref[...] + theirs[...]
        mine = pl.ds(pl.multiple_of(c * half, 8), half)
        other = pl.ds(pl.multiple_of((1 - c) * half, 8), half)
        sends = [_remote(pair.at[mine], slots.at[me], send_sems, recv_sems, 1 + k, (cx, cy, c))
                 for k, (cx, cy) in enumerate(chips)]
        for cp in sends:
            cp.start()
        for k, (cx, cy) in enumerate(chips):
            _remote(pair.at[mine], slots.at[2 * cx + cy], send_sems, recv_sems, 1 + k, (cx, cy, c)).wait_recv()
        slots[me] = pair[mine]
        out_ref[mine] = ((slots[0] + slots[1]) + slots[2]) + slots[3]
        last = _remote(out_ref.at[mine], out_ref.at[mine], send_sems, recv_sems, 4, sib)
        last.start()
        _remote(out_ref.at[other], out_ref.at[other], send_sems, recv_sems, 4, sib).wait_recv()
        for cp in [swap] + sends + [last]:
            cp.wait_send()

    vmem = pl.BlockSpec(memory_space=pltpu.VMEM)
    return pl.pallas_call(
        body,
        name="sum_all",
        in_specs=[vmem, vmem],
        out_specs=vmem,
        out_shape=jax.ShapeDtypeStruct(s.shape, s.dtype),
        scratch_shapes=[pltpu.VMEM(s.shape, s.dtype), pltpu.VMEM(s.shape, s.dtype),
                        pltpu.VMEM((N_CHIPS, half, s.shape[1]), s.dtype), pltpu.SemaphoreType.DMA((5,)),
                        pltpu.SemaphoreType.DMA((5,))],
        compiler_params=pltpu.CompilerParams(vmem_limit_bytes=VMEM_LIMIT),
    )(s, after)


ADAM_LR = 0.001
ADAM_B1 = 0.9
ADAM_B2 = 0.999
ADAM_EPS = 1e-08
ADAM_WD = 0.01
ADAM_STEP = 10


def f_adamw(g, w, m, v):
    m = ADAM_B1 * m + (1.0 - ADAM_B1) * g
    v = ADAM_B2 * v + (1.0 - ADAM_B2) * jnp.square(g)
    m_hat = m / (1.0 - ADAM_B1 ** ADAM_STEP)
    v_hat = v / (1.0 - ADAM_B2 ** ADAM_STEP)
    delta = -ADAM_LR * (m_hat / (jnp.sqrt(v_hat) + ADAM_EPS) + ADAM_WD * w)
    return delta, m, v


def adamw_many(gs, ws, ms, vs):
    n = len(gs)

    def body(*refs):
        ins, outs = refs[:4 * n], refs[4 * n:]
        for i in range(n):
            delta, nm, nv = f_adamw(ins[i][...], ins[n + i][...], ins[2 * n + i][...], ins[3 * n + i][...])
            outs[i][...] = delta
            outs[n + i][...] = nm
            outs[2 * n + i][...] = nv

    vmem = pl.BlockSpec(memory_space=pltpu.VMEM)
    res = pl.pallas_call(
        body,
        name="adamw_small",
        in_specs=[vmem] * (4 * n),
        out_specs=[vmem] * (3 * n),
        out_shape=[jax.ShapeDtypeStruct(w.shape, F32) for w in ws] * 3,
    )(*gs, *ws, *ms, *vs)
    return res[:n], res[n:2 * n], res[2 * n:]


EARLY = ["w_in", "w_proj_a", "w_lora_w", "a_lora_w", "g_lora_w"]
LATE = ["w_ffn1", "w_ffn2", "w_proj_b", "w_out"]
LORAS = ["w_lora_w", "a_lora_w", "g_lora_w"]
HALF_W = 512
PIECE_ROWS = {"w_in": 1864, "w_ffn1": 1024, "w_ffn2": 1024, "w_proj_a": 256, "w_proj_b": 256, "w_out": 256,
              "w_lora_w": 32, "a_lora_w": 32, "g_lora_w": 80}
PIECE_OFF = {"w_in": 0, "w_proj_a": 1920, "w_lora_w": 2176, "a_lora_w": 2208, "g_lora_w": 2240,
             "w_ffn1": 0, "w_ffn2": 1024, "w_proj_b": 2048, "w_out": 2304}
LO_OFF = 2320
SHARD_AXIS = {"w_in": 1, "w_proj_a": 0, "w_lora_w": 1, "a_lora_w": 1, "g_lora_w": 1, "w_proj_b": 0, "w_out": 0,
              "w_ffn1": 1, "w_ffn2": 0}
SHARD_SHAPE = {"w_in": (1024, 1864), "w_proj_a": (256, 1024), "w_lora_w": (64, 256), "a_lora_w": (64, 256),
               "g_lora_w": (160, 256), "w_proj_b": (256, 1024), "w_out": (256, 1024), "w_ffn1": (1024, 1024),
               "w_ffn2": (1024, 1024)}
SHIFT_SHARD = (2, 840)
VECTORS = ["g_mix", "sgu_ln_w", "sgu_ln_b", "w0", "a0", "k_k", "k_a", "r_k", "ln_x_w", "ln_x_b", "g_ffn", "g_final"]
SMALL = VECTORS + ["sgu_w", "sgu_b"]
SMALL_SHAPE = {**{n: (1, 1024) for n in VECTORS}, "sgu_w": (8, 128, 128), "sgu_b": (8, 128)}
WEIGHTS = ["g_mix", "w_in", "sgu_ln_w", "sgu_ln_b", "sgu_w", "sgu_b", "w_proj_a", "shift_b", "w_lora_w", "w0",
           "a_lora_w", "a0", "g_lora_w", "k_k", "k_a", "r_k", "ln_x_w", "ln_x_b", "w_proj_b", "w_out", "g_ffn",
           "w_ffn1", "w_ffn2", "g_final"]


def _size(shape):
    n = 1
    for s in shape:
        n *= s
    return n


def _pack_rows(parts, rows, dtype):
    flat = jnp.concatenate([p.reshape(-1).astype(dtype) for p in parts])
    return jnp.concatenate([flat, jnp.zeros((rows * 1024 - flat.shape[0],), dtype)]).reshape(rows, 1024)


def _unpack_rows(packed, shapes):
    flat = packed.reshape(-1)
    out, off = [], 0
    for shp in shapes:
        out.append(flat[off:off + _size(shp)].reshape(shp))
        off += _size(shp)
    return out


def _shard_of(name, full, j):
    ax = SHARD_AXIS[name]
    n = SHARD_SHAPE[name][ax]
    return lax.slice_in_dim(full, j * n, (j + 1) * n, axis=ax)


def _pad_cols(z, n):
    return jnp.concatenate([z, jnp.zeros((z.shape[0], n - z.shape[1]), z.dtype)], axis=1)


def _row_form(name, s):
    return s.T if name == "w_in" else s


def _half_piece(name, rf, h):
    if name in LORAS:
        r = PIECE_ROWS[name]
        return _pad_cols(rf[h * r:(h + 1) * r], HALF_W)
    return rf[:, HALF_W * h:HALF_W * (h + 1)]


def _pack_half(group, rf_fn, h, dtype, tail=()):
    parts, pos, rows = [], 0, PACK_ROWS
    for n in group:
        if PIECE_OFF[n] > pos:
            parts.append(jnp.zeros((PIECE_OFF[n] - pos, HALF_W), dtype))
        parts.append(_half_piece(n, rf_fn(n), h).astype(dtype))
        pos = PIECE_OFF[n] + PIECE_ROWS[n]
    for t in tail:
        parts.append(t)
        pos += t.shape[0]
    parts.append(jnp.zeros((rows - pos, HALF_W), dtype))
    return jnp.concatenate(parts, axis=0)


def _piece(pack, name):
    return pack[PIECE_OFF[name]:PIECE_OFF[name] + PIECE_ROWS[name]]


def _join_halves(name, p0, p1):
    if name in LORAS:
        return jnp.concatenate([p0[:, :SHARD_SHAPE[name][1]], p1[:, :SHARD_SHAPE[name][1]]], axis=0)
    return jnp.concatenate([p0, p1], axis=1)


def _grad_row_form(name, full, j):
    if name == "w_in":
        return full[SHARD_SHAPE[name][1] * j:SHARD_SHAPE[name][1] * (j + 1)]
    return _shard_of(name, full, j)


def adamw_weight(name, g_own, g_other, w, m, v):
    rows, width = w.shape
    if name in LORAS:
        tm = PIECE_ROWS[name]
        grid = (2, 1)
        native = pl.BlockSpec((tm, width), lambda h, i: (h, 0))
    elif name == "w_in":
        tm, lanes = rows, 128
        grid = (2, HALF_W // lanes)
        native = pl.BlockSpec((tm, lanes), lambda h, i: (0, h * (HALF_W // lanes) + i))
    else:
        tm = 128
        grid = (2, rows // tm)
        native = pl.BlockSpec((tm, HALF_W), lambda h, i: (i, h))
    off = PIECE_OFF[name] // tm
    if name == "w_in":
        packed = pl.BlockSpec((tm, 128), lambda h, i: (0, i))
    else:
        packed = pl.BlockSpec((tm, HALF_W), lambda h, i: (off + i, 0))

    def body(go_ref, gx_ref, w_ref, m_ref, v_ref, g_ref, d_ref, nm_ref, nv_ref):
        g = jnp.where(pl.program_id(0) == lax.axis_index("c"), go_ref[...], gx_ref[...])[:, :w_ref.shape[1]]
        delta, nm, nv = f_adamw(g, w_ref[...], m_ref[...], v_ref[...])
        g_ref[...] = g
        d_ref[...] = delta
        nm_ref[...] = nm
        nv_ref[...] = nv

    return pl.pallas_call(
        body,
        name="adamw_" + name,
        grid=grid,
        in_specs=[packed, packed, native, native, native],
        out_specs=[native] * 4,
        out_shape=[jax.ShapeDtypeStruct(w.shape, F32)] * 4,
        compiler_params=_cparams(2),
    )(g_own, g_other, w, m, v)


def kernel(x, g_mix, w_in, sgu_ln_w, sgu_ln_b, sgu_w, sgu_b, w_proj_a, shift_b, w_lora_w, w0, a_lora_w, a0, g_lora_w, k_k, k_a, r_k, ln_x_w, ln_x_b, w_proj_b, w_out, g_ffn, w_ffn1, w_ffn2, g_final, loss_target, m_g_mix, m_w_in, m_sgu_ln_w, m_sgu_ln_b, m_sgu_w, m_sgu_b, m_w_proj_a, m_shift_b, m_w_lora_w, m_w0, m_a_lora_w, m_a0, m_g_lora_w, m_k_k, m_k_a, m_r_k, m_ln_x_w, m_ln_x_b, m_w_proj_b, m_w_out, m_g_ffn, m_w_ffn1, m_w_ffn2, m_g_final, v_g_mix, v_w_in, v_sgu_ln_w, v_sgu_ln_b, v_sgu_w, v_sgu_b, v_w_proj_a, v_shift_b, v_w_lora_w, v_w0, v_a_lora_w, v_a0, v_g_lora_w, v_k_k, v_k_a, v_r_k, v_ln_x_w, v_ln_x_b, v_w_proj_b, v_w_out, v_g_ffn, v_w_ffn1, v_w_ffn2, v_g_final):
    given = dict(zip(WEIGHTS, (g_mix, w_in, sgu_ln_w, sgu_ln_b, sgu_w, sgu_b, w_proj_a, shift_b, w_lora_w, w0, a_lora_w, a0, g_lora_w, k_k, k_a, r_k, ln_x_w, ln_x_b, w_proj_b, w_out, g_ffn, w_ffn1, w_ffn2, g_final)))
    mom_m = dict(zip(WEIGHTS, (m_g_mix, m_w_in, m_sgu_ln_w, m_sgu_ln_b, m_sgu_w, m_sgu_b, m_w_proj_a, m_shift_b, m_w_lora_w, m_w0, m_a_lora_w, m_a0, m_g_lora_w, m_k_k, m_k_a, m_r_k, m_ln_x_w, m_ln_x_b, m_w_proj_b, m_w_out, m_g_ffn, m_w_ffn1, m_w_ffn2, m_g_final)))
    mom_v = dict(zip(WEIGHTS, (v_g_mix, v_w_in, v_sgu_ln_w, v_sgu_ln_b, v_sgu_w, v_sgu_b, v_w_proj_a, v_shift_b, v_w_lora_w, v_w0, v_a_lora_w, v_a0, v_g_lora_w, v_k_k, v_k_a, v_r_k, v_ln_x_w, v_ln_x_b, v_w_proj_b, v_w_out, v_g_ffn, v_w_ffn1, v_w_ffn2, v_g_final)))
    chip = 2 * lax.axis_index("x") + lax.axis_index("y")

    def local_block(tree, n):
        return tree[n] if n == "g_final" else tree[n][0]

    sb = local_block(given, "shift_b")
    lo_part = lambda z: (z - z.astype(BF16).astype(F32)).astype(BF16)
    row_form = lambda tree: (lambda n: _row_form(n, local_block(tree, n)))
    tile16 = lambda z: jnp.pad(z, ((0, 16 - z.shape[0]), (0, HALF_W - z.shape[1])))
    sb_tiles = [tile16(f(sb[:, lanes])) for f in (lambda z: z.astype(BF16), lo_part)
                for lanes in (slice(0, HALF_W), slice(HALF_W, None))]
    tails = [[_half_piece(n, lo_part(local_block(given, n)), h) for n in LORAS] + sb_tiles for h in range(2)]
    pack_w = jnp.stack([_pack_half(EARLY, row_form(given), h, BF16, tails[h]) for h in range(2)])
    gathered, gathered_token = gather_shards(pack_w)
    gathered = lax.dynamic_update_index_in_dim(gathered, pack_w, chip, 0)
    pack_late = jnp.stack([_pack_half(LATE, row_form(given), h, BF16) for h in range(2)])
    late_state, late_token = split_start("gather_start", _gather_copies, 3, pack_late, (N_CHIPS,) + pack_late.shape,
                                         gathered_token)

    def whole(group, got, own):
        half = lambda n, j, h: jnp.where(chip == j, _piece(own[h], n), _piece(got[j, h], n))
        shard = lambda n, j: _join_halves(n, half(n, j, 0), half(n, j, 1))
        return {n: jnp.concatenate([shard(n, j) for j in range(N_CHIPS)],
                                   axis=0 if n == "w_in" else SHARD_AXIS[n]) for n in group}

    w = whole(EARLY, gathered, pack_w)
    late_weights = lambda after: whole(
        LATE, gather_forward(split_wait("gather_wait", _gather_copies, late_state, after)[1]), pack_late)
    off = LO_OFF
    for n in LORAS:
        r, cols = PIECE_ROWS[n], SHARD_SHAPE[n][1]
        lo = jnp.concatenate([jnp.concatenate([gathered[j, 0, off:off + r, :cols], gathered[j, 1, off:off + r, :cols]],
                                              axis=0) for j in range(N_CHIPS)], axis=1)
        w[n] = w[n].astype(F32) + lo.astype(F32)
        off += r
    sb_tile = lambda j, t, lanes: gathered[j, 0, off + 16 * t:off + 16 * t + 2, :lanes].astype(F32)
    rest = SHIFT_SHARD[1] - HALF_W
    w["shift_b"] = jnp.concatenate(
        [jnp.concatenate([sb_tile(j, 0, HALF_W) + sb_tile(j, 2, HALF_W), sb_tile(j, 1, rest) + sb_tile(j, 3, rest)],
                         axis=1) for j in range(N_CHIPS)], axis=1)
    for n in SMALL:
        w[n] = local_block(given, n).reshape(SMALL_SHAPE[n])

    def pair_start(g_pack, tag):
        return split_start("reduce_pair_start_" + tag, _pair_copies, N_CHIPS, g_pack, (N_CHIPS,) + g_pack.shape[2:])

    def pair_finish(state, after, tag):
        return pair_sum(*split_wait("reduce_pair_wait_" + tag, _pair_copies, state, after), tag, tm=PACK_TILE)

    pack_early = lambda g: jnp.stack([jnp.stack([_pack_half(EARLY, lambda n: _grad_row_form(n, g[n], j), h, F32)
                                                 for h in range(2)]) for j in range(N_CHIPS)])
    loss, grad_x, grads, (late_part, late_slots), early_state = local_step(
        x[0], loss_target[0], w, late_token, late_weights, pair_start, pair_finish, pack_early)

    early_part, early_part16 = pair_finish(early_state, grad_x, "early")
    s_pack = _pack_rows([grads[n] for n in SMALL] + [grads["shift_b"], loss.reshape(1, 1)], SMALL_ROWS, F32)
    chips_state, token = split_start("reduce_chips_start", _chip_copies, 3, early_part16, early_part16.shape)
    my_chip = lambda: 2 * lax.axis_index("x") + lax.axis_index("y")
    out_g, out_d, out_m, out_v = {}, {}, {}, {}

    def finish(group, tag, part, slots):
        half_sum = sum_with_own(part, slots, my_chip, token, tm=PACK_TILE, name="chip_sum_" + tag)
        other_half = exchange_halves(half_sum, tag)
        for n in group:
            res = adamw_weight(n, half_sum, other_half,
                               *[_row_form(n, local_block(t, n)) for t in (given, mom_m, mom_v)])
            for tree, z in zip((out_g, out_d, out_m, out_v), res):
                tree[n] = _row_form(n, z)

    finish(LATE, "late", late_part, late_slots)

    small_shapes = [SMALL_SHAPE[n] for n in SMALL]
    g_small = sum_all(s_pack, token)
    *g_parts, loss = _unpack_rows(g_small, small_shapes + [(2, N_RWKV), ()])
    out_g.update(zip(SMALL, g_parts[:-1]))
    g_sb = lax.dynamic_slice_in_dim(g_parts[-1], chip * SHIFT_SHARD[1], SHIFT_SHARD[1], axis=1)
    out_g["shift_b"] = g_sb
    names = SMALL + ["shift_b"]
    native = lambda tree: [local_block(tree, n).reshape(SMALL_SHAPE.get(n, SHIFT_SHARD)) for n in names]
    small_res = adamw_many(g_parts[:-1] + [g_sb], native(given), native(mom_m), native(mom_v))
    for tree, res in zip((out_d, out_m, out_v), small_res):
        tree.update(zip(names, res))

    after = (out_v["w_out"], out_v["sgu_w"])
    early_slots = split_wait("reduce_chips_wait", _chip_copies, chips_state,
                             jnp.concatenate([z.reshape(-1)[:8] for z in after]))[1]
    finish(EARLY, "early", early_part, early_slots)

    def block_of(tree, n):
        return tree[n].reshape(given[n].shape)

    return (loss, grad_x[None], *[block_of(out_g, n) for n in WEIGHTS], *[block_of(out_d, n) for n in WEIGHTS],
            *[block_of(out_m, n) for n in WEIGHTS], *[block_of(out_v, n) for n in WEIGHTS])
```

```python
import functools

import jax
import jax.numpy as jnp
from jax import lax
from jax.experimental import pallas as pl
from jax.experimental.pallas import tpu as pltpu

F32 = jnp.float32
BF16 = jnp.bfloat16

D_MODEL = 1024
N_HEADS = 16
HEAD = 64
SCAN_CHUNK = 64

VMEM_LIMIT = 56 * 1024 * 1024


_BDIMS = {
    "nn": (((2,), (1,)), ((0,), (0,))),
    "nt": (((2,), (2,)), ((0,), (0,))),
    "tn": (((1,), (1,)), ((0,), (0,))),
}


def _raw_bdot(x, y, mode, fine):
    if fine:
        return lax.dot_general(x, y, _BDIMS[mode], precision=lax.Precision.HIGH, preferred_element_type=F32)
    return lax.dot_general(x.astype(BF16), y.astype(BF16), _BDIMS[mode], preferred_element_type=F32)


@functools.partial(jax.custom_vjp, nondiff_argnums=(2, 3))
def bdot(x, y, mode, fine=True):
    return _raw_bdot(x, y, mode, fine)


def _bdot_fwd(x, y, mode, fine):
    return _raw_bdot(x, y, mode, fine), (x, y)


def _bdot_bwd(mode, fine, res, g):
    x, y = res
    if mode == "nn":
        return bdot(g, y, "nt", fine), bdot(x, g, "tn", fine)
    if mode == "nt":
        return bdot(g, y, "nn", fine), bdot(g, x, "tn", fine)
    return bdot(y, g, "nt", fine), bdot(x, g, "nn", fine)


bdot.defvjp(_bdot_fwd, _bdot_bwd)


def _scan_chunk(S0, r, lw, k, v, a, b):
    nh, lc, _ = r.shape
    ti = lax.broadcasted_iota(jnp.int32, (lc, lc), 0)
    si = lax.broadcasted_iota(jnp.int32, (lc, lc), 1)
    incl = (si <= ti).astype(F32)
    strict = (si < ti).astype(F32)
    eye = (si == ti).astype(F32)
    cl = bdot(jnp.broadcast_to(incl, (nh, lc, lc)), lw, "nn")
    cl_last = cl[:, lc - 1:lc, :]
    g_last = jnp.exp(cl_last - cl)
    at = a * jnp.exp(cl - lw)
    bt = b * jnp.exp(-cl)
    kt = k * jnp.exp(-cl)
    rt = r * jnp.exp(cl)
    ar = jnp.concatenate([at, rt], axis=1)
    ar_b = bdot(ar, bt, "nt", False)
    ar_k = bdot(ar, kt, "nt", False)
    m_ab, m_rb = ar_b[:, :lc] * strict, ar_b[:, lc:] * incl
    m_ak, m_rk = ar_k[:, :lc] * strict, ar_k[:, lc:] * incl
    x = eye + m_ab
    p = bdot(m_ab, m_ab, "nn", False)
    n = 2
    while n * 2 < lc:
        px = bdot(jnp.concatenate([p, x], axis=1), p, "nn", False)
        p = px[:, :lc]
        x = x + px[:, lc:]
        n *= 2
    x = x + bdot(x, p, "nn", False)
    ar_s = bdot(ar, S0, "nt", False)
    akrk_v = bdot(jnp.concatenate([m_ak, m_rk], axis=1), v, "nn", False)
    u = bdot(x, ar_s[:, :lc] + akrk_v[:, :lc], "nn", False)
    o = ar_s[:, lc:] + bdot(m_rb, u, "nn", False) + akrk_v[:, lc:]
    s_last = S0 * jnp.exp(cl_last) + bdot(jnp.concatenate([u, v], axis=1),
                                          jnp.concatenate([b * g_last, k * g_last], axis=1), "tn", False)
    return o, s_last


def _split_heads(z):
    return jnp.stack([z[:, HEAD * h:HEAD * (h + 1)] for h in range(N_HEADS)], axis=0)


def _merge_heads(z):
    return jnp.concatenate([z[h] for h in range(N_HEADS)], axis=1)


def _scan_specs(t, ops, rev):
    nc = t // SCAN_CHUNK
    row = (lambda c: nc - 1 - c) if rev else (lambda c: c)
    specs = [pl.BlockSpec((SCAN_CHUNK, D_MODEL), lambda c, cb=cb: (row(c), cb)) for _, cb in ops]
    state = pl.BlockSpec((1, N_HEADS, HEAD, HEAD), lambda c: (row(c), 0, 0, 0))
    return nc, specs, state


def scan_fwd(ops):
    t = ops[0][0].shape[0]
    nc, specs, state = _scan_specs(t, ops, False)

    def body(r_ref, lw_ref, k_ref, v_ref, a_ref, b_ref, o_ref, s0_ref, s_scr):
        @pl.when(pl.program_id(0) == 0)
        def _():
            s_scr[...] = jnp.zeros_like(s_scr)

        s0 = s_scr[...]
        s0_ref[0] = s0
        o, s_last = _scan_chunk(s0, *[_split_heads(z[...]) for z in (r_ref, lw_ref, k_ref, v_ref, a_ref, b_ref)])
        o_ref[...] = _merge_heads(o)
        s_scr[...] = s_last

    return pl.pallas_call(
        body,
        name="scan_fwd",
        grid=(nc,),
        in_specs=specs,
        out_specs=[pl.BlockSpec((SCAN_CHUNK, D_MODEL), lambda c: (c, 0)), state],
        out_shape=[jax.ShapeDtypeStruct((t, D_MODEL), F32), jax.ShapeDtypeStruct((nc, N_HEADS, HEAD, HEAD), F32)],
        scratch_shapes=[pltpu.VMEM((N_HEADS, HEAD, HEAD), F32)],
        compiler_params=_cparams(1),
    )(*[a for a, _ in ops])


def scan_bwd(ops, s0s, do, part):
    t = ops[0][0].shape[0]
    nc, specs, state = _scan_specs(t, ops + [(do, 0)], True)

    def body(r_ref, lw_ref, k_ref, v_ref, a_ref, b_ref, do_ref, s0_ref, part_ref, *rest):
        out_refs, slots_ref, ds_scr, send_sems, recv_sems = rest[:6], rest[6], rest[7], rest[8], rest[9]
        step = pl.program_id(0)
        x, y, c = _coords()
        me = 2 * x + y
        chips = _other_chips(x, y)
        sends = [_remote(part_ref.at[2 * cx + cy], slots_ref.at[me], send_sems, recv_sems, k, (cx, cy, c))
                 for k, (cx, cy) in enumerate(chips)]

        @pl.when(step == 0)
        def _():
            ds_scr[...] = jnp.zeros_like(ds_scr)
            for cp in sends:
                cp.start()

        _, vjp = jax.vjp(_scan_chunk, s0_ref[0],
                         *[_split_heads(z[...]) for z in (r_ref, lw_ref, k_ref, v_ref, a_ref, b_ref)])
        grads = vjp((_split_heads(do_ref[...]), ds_scr[...]))
        for o_ref, g in zip(out_refs, grads[1:]):
            o_ref[...] = _merge_heads(g)
        ds_scr[...] = grads[0]

        @pl.when(step == nc - 1)
        def _():
            for k, (cx, cy) in enumerate(chips):
                _remote(part_ref.at[me], slots_ref.at[2 * cx + cy], send_sems, recv_sems, k, (cx, cy, c)).wait_recv()
            for cp in sends:
                cp.wait_send()

    return pl.pallas_call(
        body,
        name="scan_bwd",
        grid=(nc,),
        in_specs=specs + [state, _ANY],
        out_specs=[pl.BlockSpec((SCAN_CHUNK, D_MODEL), lambda c: (nc - 1 - c, 0))] * 6 + [_ANY],
        out_shape=[jax.ShapeDtypeStruct((t, D_MODEL), F32)] * 6 + [jax.ShapeDtypeStruct(part.shape, part.dtype)],
        scratch_shapes=[pltpu.VMEM((N_HEADS, HEAD, HEAD), F32), pltpu.SemaphoreType.DMA((3,)),
                        pltpu.SemaphoreType.DMA((3,))],
        compiler_params=_cparams(1),
    )(*[a for a, _ in ops], do, s0s, part)


_MDIMS = {
    "nn": (((1,), (0,)), ((), ())),
    "nt": (((1,), (1,)), ((), ())),
    "tn": (((0,), (0,)), ((), ())),
}


def _raw_mdot(x, y, mode, exact):
    if exact:
        return lax.dot_general(x, y, _MDIMS[mode], precision=lax.Precision.HIGH, preferred_element_type=F32)
    return lax.dot_general(x.astype(BF16), y.astype(BF16), _MDIMS[mode], preferred_element_type=F32)


@functools.partial(jax.custom_vjp, nondiff_argnums=(2, 3))
def mdot(x, y, mode, exact):
    return _raw_mdot(x, y, mode, exact)


def _mdot_fwd(x, y, mode, exact):
    return _raw_mdot(x, y, mode, exact), (x, y)


def _mdot_bwd(mode, exact, res, g):
    x, y = res
    if mode == "nn":
        return mdot(g, y, "nt", exact), mdot(x, g, "tn", exact)
    if mode == "nt":
        return mdot(g, y, "nn", exact), mdot(g, x, "tn", exact)
    return mdot(y, g, "nt", exact), mdot(x, g, "nn", exact)


mdot.defvjp(_mdot_fwd, _mdot_bwd)


def _seg_ones():
    i = lax.broadcasted_iota(jnp.int32, (256, 256), 0) // HEAD
    j = lax.broadcasted_iota(jnp.int32, (256, 256), 1) // HEAD
    return (i == j).astype(BF16)


@jax.custom_vjp
def segsum(x):
    bd = _seg_ones()
    hi = x.astype(BF16)
    lo = (x - hi.astype(F32)).astype(BF16)
    cols = []
    for j in range(x.shape[1] // 256):
        sl = slice(256 * j, 256 * (j + 1))
        cols.append(jnp.dot(hi[:, sl], bd, preferred_element_type=F32)
                    + jnp.dot(lo[:, sl], bd, preferred_element_type=F32))
    return jnp.concatenate(cols, axis=1)


segsum.defvjp(lambda x: (segsum(x), None), lambda _, g: (segsum(g),))


NORM_EPS = 1e-6
LN_EPS = 1e-5
GN_EPS = 64e-5
SGU_CHUNK = 128
SGU_GROUPS = 8


def _rms(x, g):
    return x * lax.rsqrt(jnp.mean(x * x, axis=-1, keepdims=True) + NORM_EPS) * g


def f_norm_in(x, g):
    return _rms(x, g), x


def f_sgu(p, ln_w, ln_b, sw, sbt):
    tm = p.shape[0]
    z = 0.5 * p * (1.0 + lax.erf(p * 0.7071067811865476))
    u, v = z[:, :D_MODEL], z[:, D_MODEL:]
    mu = jnp.mean(v, axis=-1, keepdims=True)
    d = v - mu
    vn = d * lax.rsqrt(jnp.mean(d * d, axis=-1, keepdims=True) + LN_EPS) * ln_w + ln_b
    ii = lax.broadcasted_iota(jnp.int32, (SGU_CHUNK, SGU_CHUNK), 0)
    jj = lax.broadcasted_iota(jnp.int32, (SGU_CHUNK, SGU_CHUNK), 1)
    mask = (jj <= ii).astype(F32)
    gi = lax.broadcasted_iota(jnp.int32, (SGU_GROUPS, D_MODEL), 0)
    ci = lax.broadcasted_iota(jnp.int32, (SGU_GROUPS, D_MODEL), 1) // SGU_CHUNK
    bias = mdot(sbt, (gi == ci).astype(F32), "nn", True)
    rows = []
    for c in range(tm // SGU_CHUNK):
        cols = []
        for g in range(SGU_GROUPS):
            blk = vn[c * SGU_CHUNK:(c + 1) * SGU_CHUNK, g * SGU_CHUNK:(g + 1) * SGU_CHUNK]
            cols.append(mdot(sw[g] * mask, blk, "nn", False))
        rows.append(jnp.concatenate(cols, axis=1) + bias)
    return (u * jnp.concatenate(rows, axis=0),)


def _softplus(x):
    return jnp.maximum(x, 0.0) + jnp.log1p(jnp.exp(-jnp.abs(x)))


def f_pre(q, wl, w0, al, a0, gl, k_k, k_a):
    qr, qk, qv, ql = q[:, :1024], q[:, 1024:2048], q[:, 2048:3072], q[:, 3072:]
    return _f_pre(qr, qk, qv, ql, wl, w0, al, a0, gl, k_k, k_a)


def _f_pre(qr, qk, qv, ql, wl, w0, al, a0, gl, k_k, k_a):
    xw, xa, xg = ql[:, :128], ql[:, 128:256], ql[:, 256:512]
    wr = -_softplus(-(w0 + mdot(jnp.tanh(xw), wl, "nn", False))) - 0.5
    lw = -jnp.exp(wr)
    aa = jax.nn.sigmoid(a0 + mdot(xa, al, "nn", False))
    g = mdot(jax.nn.sigmoid(xg), gl, "nn", False)
    kkr = qk * k_k
    kk = kkr / jnp.maximum(jnp.sqrt(segsum(kkr * kkr)), 1e-12)
    kp = qk * (1.0 + (aa - 1.0) * k_a)
    return qr, lw, kp, qv, -kk, kk * aa, g, qr, kp, qv


def f_post(o, r, kp, v, g, lnw, lnb, rk):
    mu = segsum(o) * (1.0 / HEAD)
    d = o - mu
    gn = d * lax.rsqrt(segsum(d * d) * (1.0 / HEAD) + GN_EPS)
    return ((gn * lnw + lnb + segsum(r * kp * rk) * v) * g,)


def f_mix(ya, yb, ga, gb):
    return (jax.nn.sigmoid(ga) * ya + jax.nn.sigmoid(gb) * yb,)


def f_ffn_in(h1, g):
    return _rms(h1, g), h1


def f_final(h1, m3, tgt, g):
    y = _rms(h1 + m3, g)
    err = jnp.square(y - tgt)
    return 0.5 * jnp.sum(jnp.mean(err, axis=-1))


def _cparams(n_grid):
    return pltpu.CompilerParams(dimension_semantics=("arbitrary",) * n_grid, vmem_limit_bytes=VMEM_LIMIT)


def _tile_spec(tm, w, cb):
    return pl.BlockSpec((tm, w), lambda i: (i, cb))


def _const_spec(c):
    nd = c.ndim
    return pl.BlockSpec(c.shape, lambda i: (0,) * nd)


def ew_call(fn, tiled, consts, outs, *, tm, name):
    t = tiled[0][0].shape[0]
    n_t, n_c = len(tiled), len(consts)

    def body(*refs):
        tv = [r[...].astype(F32) for r in refs[:n_t]]
        cv = [r[...] for r in refs[n_t:n_t + n_c]]
        res = fn(*tv, *cv)
        for o_ref, val in zip(refs[n_t + n_c:], res):
            o_ref[...] = val.astype(o_ref.dtype)

    return pl.pallas_call(
        body,
        name=name,
        grid=(t // tm,),
        in_specs=[_tile_spec(tm, w, cb) for _, w, cb in tiled] + [_const_spec(c) for c in consts],
        out_specs=[_tile_spec(tm, w, 0) for w, _ in outs],
        out_shape=[jax.ShapeDtypeStruct((t, w), dt) for w, dt in outs],
        compiler_params=_cparams(1),
    )(*[a for a, _, _ in tiled], *consts)


def ew_vjp_call(fn, tiled, consts, cots, d_tiled, d_consts, *, tm, name):
    t = tiled[0][0].shape[0]
    n_t, n_c, n_g = len(tiled), len(consts), len(cots)
    dt_list = [(i, dt) for i, dts in enumerate(d_tiled) for dt in dts]
    dc_list = [i for i, want in enumerate(d_consts) if want]

    def body(*refs):
        tv = [r[...].astype(F32) for r in refs[:n_t]]
        cv = [r[...] for r in refs[n_t:n_t + n_c]]
        gv = tuple(r[...].astype(F32) for r in refs[n_t + n_c:n_t + n_c + n_g])
        out_refs = refs[n_t + n_c + n_g:]
        _, vjp = jax.vjp(fn, *tv, *cv)
        grads = vjp(gv)
        for o_ref, (i, _) in zip(out_refs, dt_list):
            o_ref[...] = grads[i].astype(o_ref.dtype)
        acc_refs = out_refs[len(dt_list):]

        @pl.when(pl.program_id(0) == 0)
        def _():
            for a_ref in acc_refs:
                a_ref[...] = jnp.zeros_like(a_ref)

        for a_ref, i in zip(acc_refs, dc_list):
            a_ref[...] += grads[n_t + i]

    res = pl.pallas_call(
        body,
        name=name,
        grid=(t // tm,),
        in_specs=[_tile_spec(tm, w, cb) for _, w, cb in tiled] + [_const_spec(c) for c in consts]
        + [_tile_spec(tm, w, cb) for _, w, cb in cots],
        out_specs=[_tile_spec(tm, tiled[i][1], 0) for i, _ in dt_list] + [_const_spec(consts[i]) for i in dc_list],
        out_shape=[jax.ShapeDtypeStruct((t, tiled[i][1]), dt) for i, dt in dt_list]
        + [jax.ShapeDtypeStruct(consts[i].shape, F32) for i in dc_list],
        compiler_params=_cparams(1),
    )(*[a for a, _, _ in tiled], *consts, *[a for a, _, _ in cots])
    return res[:len(dt_list)], res[len(dt_list):]


def mm(a, b, mode, *, tm, tn, name, out_dtypes=(F32,), epi=None, extras=(), into=None):
    m = a.shape[1] if mode == "tn" else a.shape[0]
    kd = a.shape[0] if mode == "tn" else a.shape[1]
    n = b.shape[0] if mode == "nt" else b.shape[1]
    tm, tn = min(tm, m), min(tn, n)
    if mode == "nn":
        a_spec = pl.BlockSpec((tm, kd), lambda i, j: (i, 0))
        b_spec = pl.BlockSpec((kd, tn), lambda i, j: (0, j))
    elif mode == "nt":
        a_spec = pl.BlockSpec((tm, kd), lambda i, j: (i, 0))
        b_spec = pl.BlockSpec((tn, kd), lambda i, j: (j, 0))
    else:
        a_spec = pl.BlockSpec((kd, tm), lambda i, j: (0, i))
        b_spec = pl.BlockSpec((kd, tn), lambda i, j: (0, j))
    n_e = len(extras)
    o_spec = pl.BlockSpec((tm, tn), lambda i, j: (i, j))

    if into is not None:
        buf, place = into

        def body_into(a_ref, b_ref, buf_ref, o_ref):
            o_ref[0, 0] = lax.dot_general(a_ref[...].astype(BF16), b_ref[...].astype(BF16), _MDIMS[mode],
                                          preferred_element_type=F32)

        return pl.pallas_call(
            body_into,
            name=name,
            grid=(m // tm, n // tn),
            in_specs=[a_spec, b_spec, pl.BlockSpec(memory_space=pl.ANY)],
            out_specs=pl.BlockSpec((1, 1, tm, tn), lambda i, j: (*place(i, j), 0)),
            out_shape=jax.ShapeDtypeStruct(buf.shape, F32),
            input_output_aliases={2: 0},
            compiler_params=_cparams(2),
        )(a, b, buf)

    def body(a_ref, b_ref, *refs):
        c = lax.dot_general(a_ref[...].astype(BF16), b_ref[...].astype(BF16), _MDIMS[mode],
                            preferred_element_type=F32)
        res = epi(c, *[r[...] for r in refs[:n_e]]) if epi is not None else (c,)
        for o_ref, val in zip(refs[n_e:], res):
            o_ref[...] = val.astype(o_ref.dtype)

    res = pl.pallas_call(
        body,
        name=name,
        grid=(m // tm, n // tn),
        in_specs=[a_spec, b_spec] + [o_spec] * n_e,
        out_specs=[o_spec] * len(out_dtypes),
        out_shape=[jax.ShapeDtypeStruct((m, n), dt) for dt in out_dtypes],
        compiler_params=_cparams(2),
    )(a, b, *extras)
    return res if len(out_dtypes) > 1 else res[0]


RWKV_COL0 = 4096
RWKV_WIDTH = 3584
SHIFT_BLK = 512


def _shift_down(p, prev_row):
    rows = lax.broadcasted_iota(jnp.int32, p.shape, 0)
    return jnp.where(rows == 0, prev_row, pltpu.roll(p, 1, 0))


def shiftmix_fwd(p_all, sbp, *, tm):
    t = p_all.shape[0]
    tm = min(tm, t)
    c0 = RWKV_COL0 // SHIFT_BLK
    hb = tm // 8

    def body(p_ref, halo_ref, sb_ref, q_ref):
        p = p_ref[...]
        prev = jnp.where(pl.program_id(0) == 0, 0.0, halo_ref[7:8, :])
        q_ref[...] = p * sb_ref[0:1, :] + _shift_down(p, prev) * sb_ref[1:2, :]

    return pl.pallas_call(
        body,
        name="shiftmix_fwd",
        grid=(t // tm, RWKV_WIDTH // SHIFT_BLK),
        in_specs=[
            pl.BlockSpec((tm, SHIFT_BLK), lambda i, j: (i, c0 + j)),
            pl.BlockSpec((8, SHIFT_BLK), lambda i, j: (jnp.maximum(i * hb - 1, 0), c0 + j)),
            pl.BlockSpec((2, SHIFT_BLK), lambda i, j: (0, j)),
        ],
        out_specs=pl.BlockSpec((tm, SHIFT_BLK), lambda i, j: (i, j)),
        out_shape=jax.ShapeDtypeStruct((t, RWKV_WIDTH), F32),
        compiler_params=_cparams(2),
    )(p_all, p_all, sbp)


def shiftmix_bwd(dq, col0, p_all, sbp, *, tm, name):
    t, w = dq.shape
    n_i = t // tm
    hb = tm // 8
    cq = col0 // SHIFT_BLK
    cp = (RWKV_COL0 + col0) // SHIFT_BLK

    def body(dq_ref, dqn_ref, p_ref, ph_ref, sb_ref, dp_ref, dsb_ref):
        i = pl.program_id(1)
        dq_t = dq_ref[...]
        rows = lax.broadcasted_iota(jnp.int32, dq_t.shape, 0)
        nxt = jnp.where(i == n_i - 1, 0.0, dqn_ref[0:1, :])
        up = jnp.where(rows == tm - 1, nxt, pltpu.roll(dq_t, tm - 1, 0))
        dp_ref[...] = (dq_t * sb_ref[0:1, :] + up * sb_ref[1:2, :]).astype(dp_ref.dtype)
        p = p_ref[...]
        prev = jnp.where(i == 0, 0.0, ph_ref[7:8, :])
        s0 = jnp.sum(dq_t * p, axis=0, keepdims=True)
        s1 = jnp.sum(dq_t * _shift_down(p, prev), axis=0, keepdims=True)
        two = lax.broadcasted_iota(jnp.int32, (2, SHIFT_BLK), 0)

        @pl.when(i == 0)
        def _():
            dsb_ref[...] = jnp.zeros_like(dsb_ref)

        dsb_ref[...] += jnp.where(two == 0, s0, s1)

    return pl.pallas_call(
        body,
        name=name,
        grid=(w // SHIFT_BLK, n_i),
        in_specs=[
            pl.BlockSpec((tm, SHIFT_BLK), lambda j, i: (i, j)),
            pl.BlockSpec((8, SHIFT_BLK), lambda j, i: (jnp.minimum((i + 1) * hb, t // 8 - 1), j)),
            pl.BlockSpec((tm, SHIFT_BLK), lambda j, i: (i, cp + j)),
            pl.BlockSpec((8, SHIFT_BLK), lambda j, i: (jnp.maximum(i * hb - 1, 0), cp + j)),
            pl.BlockSpec((2, SHIFT_BLK), lambda j, i: (0, cq + j)),
        ],
        out_specs=[
            pl.BlockSpec((tm, SHIFT_BLK), lambda j, i: (i, j)),
            pl.BlockSpec((2, SHIFT_BLK), lambda j, i: (0, j)),
        ],
        out_shape=[jax.ShapeDtypeStruct((t, w), BF16), jax.ShapeDtypeStruct((2, w), F32)],
        compiler_params=_cparams(2),
    )(dq, dq, p_all, p_all, sbp)


def final_call(h1, m3, tgt, g_final, *, tm):
    t = h1.shape[0]

    def body(h1_ref, m3_ref, tgt_ref, g_ref, dh_ref, dhb_ref, dg_ref, loss_ref):
        loss, vjp = jax.vjp(f_final, h1_ref[...], m3_ref[...], tgt_ref[...], g_ref[...])
        dh, _, _, dg = vjp(jnp.ones((), F32))
        dh_ref[...] = dh
        dhb_ref[...] = dh.astype(BF16)

        @pl.when(pl.program_id(0) == 0)
        def _():
            dg_ref[...] = jnp.zeros_like(dg_ref)
            loss_ref[...] = jnp.zeros_like(loss_ref)

        dg_ref[...] += dg
        loss_ref[...] += jnp.full(loss_ref.shape, loss, F32)

    tile = _tile_spec(tm, D_MODEL, 0)
    return pl.pallas_call(
        body,
        name="final_loss",
        grid=(t // tm,),
        in_specs=[tile, tile, tile, _const_spec(g_final)],
        out_specs=[tile, tile, _const_spec(g_final), pl.BlockSpec((8, 128), lambda i: (0, 0))],
        out_shape=[jax.ShapeDtypeStruct((t, D_MODEL), F32), jax.ShapeDtypeStruct((t, D_MODEL), BF16),
                   jax.ShapeDtypeStruct(g_final.shape, F32), jax.ShapeDtypeStruct((8, 128), F32)],
        compiler_params=_cparams(1),
    )(h1, m3, tgt, g_final)


N_SGU = 2048
N_RWKV = 3360
LORA_W, LORA_A, LORA_G = 64, 64, 160


def _pad_rwkv_cols(z):
    zero = lambda n: jnp.zeros(z.shape[:-1] + (n,), z.dtype)
    return jnp.concatenate([z[..., :3072], z[..., 3072:3136], zero(64), z[..., 3136:3200], zero(64),
                            z[..., 3200:3360], zero(96)], axis=-1)


def _unpad_rwkv_cols(z):
    return jnp.concatenate([z[..., :3072], z[..., 3072:3136], z[..., 3200:3264], z[..., 3328:3488]], axis=-1)


def _pad_win_rows(wt):
    z = wt[N_SGU:N_SGU + N_RWKV]
    zero = lambda n: jnp.zeros((n, wt.shape[1]), wt.dtype)
    return jnp.concatenate([wt[:N_SGU], wt[N_SGU + N_RWKV:], z[:3072], z[3072:3136], zero(64), z[3136:3200], zero(64),
                            z[3200:3360], zero(96)], axis=0)


def _unpad_win_rows(wt):
    z = wt[RWKV_COL0:]
    return jnp.concatenate([wt[:N_SGU], z[:3072], z[3072:3136], z[3200:3264], z[3328:3488], wt[N_SGU:RWKV_COL0]],
                           axis=0)


def _pad_rows(w, n):
    return jnp.concatenate([w, jnp.zeros((n - w.shape[0],) + w.shape[1:], w.dtype)], axis=0)


def _relu2_epi(c):
    return c, jnp.square(jnp.maximum(c, 0.0))


def _relu2_bwd_epi(c, hid):
    return (c * (2.0 * jnp.maximum(hid.astype(F32), 0.0)),)


def _add_epi(c, x):
    return (c + x,)


def _pre_fwd(*args):
    res = f_pre(*args)
    return res[1], res[2], res[4], res[5], res[6]


def local_step(x, tgt, w, late_token, late_weights, pair_start, pair_finish, pack_early):
    d = D_MODEL
    win_pt = _pad_win_rows(w["w_in"])
    sbp = _pad_rwkv_cols(w["shift_b"])
    wl = _pad_rows(w["w_lora_w"], 128)
    al = _pad_rows(w["a_lora_w"], 128)
    gl = _pad_rows(w["g_lora_w"], 256)
    sbt = w["sgu_b"].T

    (a_bf,) = ew_call(lambda x_, g_: (f_norm_in(x_, g_)[0],), [(x, d, 0)], [w["g_mix"] + late_token[:1, :1]],
                      [(d, BF16)], tm=256, name="norm_in")
    p_all = mm(a_bf, win_pt, "nt", tm=2048, tn=1280, name="mm_in")
    sgu_t = [(p_all, 2 * d, 0)]
    sgu_c = [w["sgu_ln_w"], w["sgu_ln_b"], w["sgu_w"], sbt]
    (s_bf,) = ew_call(f_sgu, sgu_t, sgu_c, [(d, BF16)], tm=256, name="sgu_fwd")
    q = shiftmix_fwd(p_all, sbp, tm=1024)
    pre_t = [(q, RWKV_WIDTH, 0)]
    pre_c = [wl, w["w0"], al, w["a0"], gl, w["k_k"], w["k_a"]]
    lw, kp, na, nb, g = ew_call(_pre_fwd, pre_t, pre_c, [(d, F32)] * 5, tm=256, name="rwkv_pre_fwd")
    scan_ops = [(q, 0), (lw, 0), (kp, 0), (q, 2), (na, 0), (nb, 0)]
    o, s0s = scan_fwd(scan_ops)
    w = {**w, **late_weights(o)}
    ya = mm(s_bf, w["w_proj_a"], "nn", tm=512, tn=1024, name="mm_proj_a")
    post_t = [(o, d, 0), (q, d, 0), (kp, d, 0), (q, d, 2), (g, d, 0)]
    post_c = [w["ln_x_w"], w["ln_x_b"], w["r_k"]]
    (ob_bf,) = ew_call(f_post, post_t, post_c, [(d, BF16)], tm=256, name="rwkv_post_fwd")
    yb = mm(ob_bf, w["w_proj_b"], "nn", tm=512, tn=1024, name="mm_proj_b")
    mix_t = [(ya, d, 0), (yb, d, 0), (p_all, d, 2), (p_all, d, 3)]
    (mixed_bf,) = ew_call(f_mix, mix_t, [], [(d, BF16)], tm=256, name="mix_fwd")
    h1 = mm(mixed_bf, w["w_out"], "nn", tm=512, tn=1024, name="mm_out", epi=_add_epi, extras=(x,))
    (f_bf,) = ew_call(lambda h_, g_: (f_ffn_in(h_, g_)[0],), [(h1, d, 0)], [w["g_ffn"]], [(d, BF16)], tm=256,
                      name="ffn_norm")
    hid, act_bf = mm(f_bf, w["w_ffn1"], "nn", tm=2048, tn=1024, name="mm_ffn1", out_dtypes=(BF16, BF16), epi=_relu2_epi)
    m3 = mm(act_bf, w["w_ffn2"], "nn", tm=1024, tn=512, name="mm_ffn2")
    dh2, dh2_bf, dg_final, loss = final_call(h1, m3, tgt, w["g_final"], tm=256)

    dhid_bf = mm(dh2_bf, w["w_ffn2"], "nt", tm=2048, tn=1024, name="mm_dact", out_dtypes=(BF16,), epi=_relu2_bwd_epi,
                 extras=(hid,))
    late_g = lax.empty((N_CHIPS, 2, pack_rows(LATE), HALF_W), F32)
    late_g = mm(act_bf, dh2_bf, "tn", tm=1024, tn=HALF_W, name="mm_dw_ffn2",
                into=(late_g, lambda i, j: (i, j, PIECE_OFF["w_ffn2"] // 1024)))
    df = mm(dhid_bf, w["w_ffn1"], "nt", tm=1024, tn=512, name="mm_df")
    late_g = mm(f_bf, dhid_bf, "tn", tm=1024, tn=HALF_W, name="mm_dw_ffn1",
                into=(late_g, lambda i, j: (j // 2, j % 2, PIECE_OFF["w_ffn1"] // 1024)))
    (dh1, dh1_bf), (dg_ffn,) = ew_vjp_call(f_ffn_in, [(h1, d, 0)], [w["g_ffn"]], [(df, d, 0), (dh2, d, 0)],
                                           [(F32, BF16)], [True], tm=256, name="ffn_norm_bwd")
    dmixed = mm(dh1_bf, w["w_out"], "nt", tm=512, tn=1024, name="mm_dmixed")
    late_g = mm(mixed_bf, dh1_bf, "tn", tm=256, tn=HALF_W, name="mm_dw_out",
                into=(late_g, lambda i, j: (i, j, PIECE_OFF["w_out"] // 256)))
    (dya_bf, dyb_bf, dga_bf, dgb_bf), _ = ew_vjp_call(f_mix, mix_t, [], [(dmixed, d, 0)], [(BF16,)] * 4, [], tm=256,
                                                      name="mix_bwd")
    dob = mm(dyb_bf, w["w_proj_b"], "nt", tm=512, tn=1024, name="mm_dob")
    late_g = mm(ob_bf, dyb_bf, "tn", tm=256, tn=HALF_W, name="mm_dw_proj_b",
                into=(late_g, lambda i, j: (i, j, PIECE_OFF["w_proj_b"] // 256)))
    late_g = mm(s_bf, dya_bf, "tn", tm=256, tn=HALF_W, name="mm_dw_proj_a",
                into=(late_g, lambda i, j: (i, j, PIECE_OFF["w_proj_a"] // 256)))
    late_state, late_token = pair_start(late_g, "late")
    post_c_after = [w["ln_x_w"] + late_token[:1, :1]] + post_c[1:]
    (do, dr_p, dkp_p, dv_p, dg), (dlnx_w, dlnx_b, dr_k) = ew_vjp_call(
        f_post, post_t, post_c_after, [(dob, d, 0)], [(F32,)] * 5, [True] * 3, tm=256, name="rwkv_post_bwd")
    late_part, late_part16 = pair_finish(late_state, do, "late")
    *scan_g, late_slots = scan_bwd(scan_ops, s0s, do, late_part16)
    pre_g = [(z, d, 0) for z in scan_g] + [(dg, d, 0), (dr_p, d, 0), (dkp_p, d, 0), (dv_p, d, 0)]
    (dq,), (dwl, dw0, dal, da0, dgl, dk_k, dk_a) = ew_vjp_call(
        f_pre, pre_t, pre_c, pre_g, [(F32,)], [True] * 7, tm=256, name="rwkv_pre_bwd")
    dp_rwkv, dsb = shiftmix_bwd(dq, 0, p_all, sbp, tm=512, name="shiftmix_bwd")
    ds = mm(dya_bf, w["w_proj_a"], "nt", tm=512, tn=1024, name="mm_ds")
    (dp_sgu,), (dln_w, dln_b, dsw, dsbt) = ew_vjp_call(f_sgu, sgu_t, sgu_c, [(ds, d, 0)], [(BF16,)], [True] * 4,
                                                       tm=256, name="sgu_bwd")
    dp_all = jnp.concatenate([dp_sgu, dga_bf, dgb_bf, dp_rwkv], axis=1)
    d_in_pt = mm(dp_all, a_bf, "tn", tm=1280, tn=1024, name="mm_dw_in")
    early_state, early_token = pair_start(pack_early({
        "w_in": _unpad_win_rows(d_in_pt), "w_lora_w": dwl[:LORA_W], "a_lora_w": dal[:LORA_A],
        "g_lora_w": dgl[:LORA_G]}), "early")
    da = mm(dp_all, win_pt, "nn", tm=1024, tn=256, name="mm_da")
    g_mix_after = w["g_mix"] + early_token[:1, :1]
    (grad_x,), (dg_mix,) = ew_vjp_call(f_norm_in, [(x, d, 0)], [g_mix_after], [(da, d, 0), (dh1, d, 0)], [(F32,)],
                                       [True], tm=256, name="norm_in_bwd")

    grads = {
        "g_mix": dg_mix, "sgu_ln_w": dln_w, "sgu_ln_b": dln_b, "sgu_w": dsw, "sgu_b": dsbt.T,
        "shift_b": _unpad_rwkv_cols(dsb),
        "w0": dw0, "a0": da0, "k_k": dk_k, "k_a": dk_a, "r_k": dr_k, "ln_x_w": dlnx_w, "ln_x_b": dlnx_b,
        "g_ffn": dg_ffn, "g_final": dg_final,
    }
    return loss[0, 0], grad_x, grads, (late_part, late_slots), early_state


MESH = pl.DeviceIdType.MESH
N_CHIPS = 4
PACK_TILE = 256
SMALL_ROWS = 160
_ANY = pl.BlockSpec(memory_space=pl.ANY)


def _coords():
    return lax.axis_index("x"), lax.axis_index("y"), lax.axis_index("c")


def _other_chips(x, y):
    return [(1 - x, y), (x, 1 - y), (1 - x, 1 - y)]


def _remote(src, dst, send_sems, recv_sems, k, to):
    return pltpu.make_async_remote_copy(src_ref=src, dst_ref=dst, send_sem=send_sems.at[k], recv_sem=recv_sems.at[k],
                                        device_id=to, device_id_type=MESH)


def gather_shards(pack):
    def body(src_ref, out_ref, token, send_sems, recv_sems):
        x, y, c = _coords()
        me = 2 * x + y
        sib = (x, y, 1 - c)
        chips = _other_chips(x, y)
        first = [_remote(src_ref.at[c], out_ref.at[me, c], send_sems, recv_sems, k, (cx, cy, c))
                 for k, (cx, cy) in enumerate(chips)]
        for cp in first:
            cp.start()
        passed = []
        for k, (cx, cy) in enumerate(chips):
            j = 2 * cx + cy
            _remote(src_ref.at[c], out_ref.at[j, c], send_sems, recv_sems, k, (cx, cy, c)).wait_recv()
            fwd = _remote(out_ref.at[j, c], out_ref.at[j, c], send_sems, recv_sems, 3 + k, sib)
            fwd.start()
            passed.append(fwd)
        for k, (cx, cy) in enumerate(chips):
            j = 2 * cx + cy
            _remote(out_ref.at[j, 1 - c], out_ref.at[j, 1 - c], send_sems, recv_sems, 3 + k, sib).wait_recv()
        for cp in first + passed:
            cp.wait_send()
        token[...] = jnp.zeros_like(token)

    return pl.pallas_call(
        body,
        name="gather_shards",
        in_specs=[_ANY],
        out_specs=[_ANY, pl.BlockSpec(memory_space=pltpu.VMEM)],
        out_shape=[jax.ShapeDtypeStruct((N_CHIPS,) + pack.shape, pack.dtype), jax.ShapeDtypeStruct((8, 128), F32)],
        scratch_shapes=[pltpu.SemaphoreType.DMA((6,)), pltpu.SemaphoreType.DMA((6,))],
    )(pack)


def _gather_copies(pack_ref, all_ref, send_sems, recv_sems):
    x, y, c = _coords()
    me = 2 * x + y
    return [(_remote(pack_ref.at[c], all_ref.at[me, c], send_sems, recv_sems, k, (cx, cy, c)),
             _remote(pack_ref.at[c], all_ref.at[2 * cx + cy, c], send_sems, recv_sems, k, (cx, cy, c)))
            for k, (cx, cy) in enumerate(_other_chips(x, y))]


_HBM = pl.BlockSpec(memory_space=pltpu.HBM)
_SEM = pl.BlockSpec(memory_space=pltpu.SEMAPHORE)
_SIDE_EFFECT = pltpu.SideEffectType.DATAFLOW_SIDE_EFFECTING


def split_start(name, copies, n, src, land_shape, after=None):
    def body(src_ref, land_ref, *refs):
        send_sems, recv_sems, token = refs[-5], refs[-4], refs[-1]
        for send, _ in copies(src_ref, land_ref, send_sems, recv_sems):
            send.start()
        token[...] = jnp.zeros_like(token)

    extra = () if after is None else (after,)
    *state, token = pl.pallas_call(
        body,
        name=name,
        out_shape=(pltpu.SemaphoreType.DMA((n,)), pltpu.SemaphoreType.DMA((n,)), pltpu.HBM(src.shape, src.dtype),
                   pltpu.HBM(land_shape, src.dtype), jax.ShapeDtypeStruct((8, 128), F32)),
        in_specs=(_HBM, _HBM) + (pl.BlockSpec(memory_space=pl.ANY),) * len(extra),
        out_specs=(_SEM, _SEM, _HBM, _HBM, pl.BlockSpec(memory_space=pltpu.VMEM)),
        input_output_aliases={0: 2, 1: 3},
        compiler_params=pltpu.CompilerParams(has_side_effects=_SIDE_EFFECT),
    )(pltpu.with_memory_space_constraint(src, pltpu.HBM),
      pltpu.with_memory_space_constraint(lax.empty(land_shape, src.dtype), pltpu.HBM), *extra)
    return state, token


def split_wait(name, copies, state, after):
    send_sems, recv_sems, src, land = state

    def body(src_ref, land_ref, send_sems, recv_sems, after_ref, src_out, land_out):
        for send, arrival in copies(src_ref, land_ref, send_sems, recv_sems):
            send.wait_send()
            arrival.wait_recv()

    return pl.pallas_call(
        body,
        name=name,
        out_shape=(pltpu.HBM(src.shape, src.dtype), pltpu.HBM(land.shape, land.dtype)),
        in_specs=(_HBM, _HBM, _SEM, _SEM, pl.BlockSpec(memory_space=pl.ANY)),
        out_specs=(_HBM, _HBM),
        input_output_aliases={0: 0, 1: 1},
        compiler_params=pltpu.CompilerParams(has_side_effects=_SIDE_EFFECT),
    )(src, land, send_sems, recv_sems, after)


def gather_forward(got):
    def body(got_ref, out_ref, send_sems, recv_sems):
        x, y, c = _coords()
        sib = (x, y, 1 - c)
        slots = [2 * cx + cy for cx, cy in _other_chips(x, y)]
        sends = [_remote(got_ref.at[j, c], out_ref.at[j, c], send_sems, recv_sems, k, sib) for k, j in enumerate(slots)]
        for cp in sends:
            cp.start()
        for k, j in enumerate(slots):
            _remote(got_ref.at[j, 1 - c], out_ref.at[j, 1 - c], send_sems, recv_sems, k, sib).wait_recv()
        for cp in sends:
            cp.wait_send()

    return pl.pallas_call(
        body,
        name="gather_forward",
        in_specs=[_ANY],
        out_specs=_ANY,
        out_shape=jax.ShapeDtypeStruct(got.shape, got.dtype),
        input_output_aliases={0: 0},
        scratch_shapes=[pltpu.SemaphoreType.DMA((3,)), pltpu.SemaphoreType.DMA((3,))],
    )(got)


def pair_sum(g, got, tag, *, tm):
    n, _, rows, width = g.shape

    def body(g0_ref, g1_ref, got_ref, out_ref, out16_ref):
        own = jnp.where(lax.axis_index("c") == 0, g0_ref[0, 0], g1_ref[0, 0])
        total = own + got_ref[0]
        out_ref[0] = total
        out16_ref[0] = total.astype(BF16)

    blk = pl.BlockSpec((1, tm, width), lambda j, i: (j, i, 0))
    return pl.pallas_call(
        body,
        name="pair_sum_" + tag,
        grid=(n, rows // tm),
        in_specs=[pl.BlockSpec((1, 1, tm, width), lambda j, i: (j, 0, i, 0)),
                  pl.BlockSpec((1, 1, tm, width), lambda j, i: (j, 1, i, 0)), blk],
        out_specs=[blk, blk],
        out_shape=[jax.ShapeDtypeStruct(got.shape, F32), jax.ShapeDtypeStruct(got.shape, BF16)],
        compiler_params=_cparams(2),
    )(g, g, got)


def _pair_copies(g_ref, got_ref, send_sems, recv_sems):
    x, y, c = _coords()
    copies = [_remote(g_ref.at[j, 1 - c], got_ref.at[j], send_sems, recv_sems, j, (x, y, 1 - c))
              for j in range(N_CHIPS)]
    return [(cp, cp) for cp in copies]


def _chip_copies(p_ref, slots_ref, send_sems, recv_sems):
    x, y, c = _coords()
    me = 2 * x + y
    return [(_remote(p_ref.at[2 * cx + cy], slots_ref.at[me], send_sems, recv_sems, k, (cx, cy, c)),
             _remote(p_ref.at[me], slots_ref.at[2 * cx + cy], send_sems, recv_sems, k, (cx, cy, c)))
            for k, (cx, cy) in enumerate(_other_chips(x, y))]


def sum_with_own(own, slots, index_fn, after, *, tm, name):
    n, rows, width = slots.shape

    def body(*refs):
        mine = index_fn()
        acc = None
        for s in range(n):
            term = jnp.where(mine == s, refs[s][0], refs[n + s][0].astype(F32))
            acc = term if acc is None else acc + term
        refs[-1][...] = acc

    slot_specs = [pl.BlockSpec((1, tm, width), lambda i, s=s: (s, i, 0)) for s in range(n)]
    return pl.pallas_call(
        body,
        name=name,
        grid=(rows // tm,),
        in_specs=slot_specs + slot_specs + [pl.BlockSpec(after.shape, lambda i: (0,) * after.ndim)],
        out_specs=pl.BlockSpec((tm, width), lambda i: (i, 0)),
        out_shape=jax.ShapeDtypeStruct((rows, width), F32),
        compiler_params=_cparams(1),
    )(*([own] * n), *([slots] * n), after)


def exchange_halves(s, tag):
    rq = PACK_TILE
    nq = s.shape[0] // rq

    def body(s_ref, out_ref, sbuf, rbuf, send_sems, recv_sems, in_sems, out_sems):
        x, y, c = _coords()
        sib = (x, y, 1 - c)
        rows = lambda q: pl.ds(q * rq, rq)
        loads = [pltpu.make_async_copy(s_ref.at[rows(q)], sbuf.at[rows(q)], in_sems.at[q]) for q in range(nq)]
        for cp in loads:
            cp.start()
        sends = []
        for q in range(nq):
            loads[q].wait()
            sends.append(_remote(sbuf.at[rows(q)], rbuf.at[rows(q)], send_sems, recv_sems, q, sib))
            sends[q].start()
        stores = []
        for q in range(nq):
            sends[q].wait_recv()
            stores.append(pltpu.make_async_copy(rbuf.at[rows(q)], out_ref.at[rows(q)], out_sems.at[q]))
            stores[q].start()
        for cp in sends:
            cp.wait_send()
        for cp in stores:
            cp.wait()

    return pl.pallas_call(
        body,
        name="exchange_halves_" + tag,
        in_specs=[_ANY],
        out_specs=_ANY,
        out_shape=jax.ShapeDtypeStruct(s.shape, s.dtype),
        scratch_shapes=[pltpu.VMEM(s.shape, s.dtype), pltpu.VMEM(s.shape, s.dtype)]
        + [pltpu.SemaphoreType.DMA((nq,))] * 4,
        compiler_params=pltpu.CompilerParams(vmem_limit_bytes=VMEM_LIMIT),
    )(s)


def sum_all(s, after):
    rows = s.shape[0]
    half = rows // 2

    def body(s_ref, after_ref, out_ref, theirs, pair, slots, send_sems, recv_sems):
        x, y, c = _coords()
        me = 2 * x + y
        sib = (x, y, 1 - c)
        chips = _other_chips(x, y)
        swap = _remote(s_ref, theirs, send_sems, recv_sems, 0, sib)
        swap.start()
        swap.wait_recv()
        pair[...] = s_ref[...] + theirs[...]
        mine = pl.ds(pl.multiple_of(c * half, 8), half)
        other = pl.ds(pl.multiple_of((1 - c) * half, 8), half)
        sends = [_remote(pair.at[mine], slots.at[me], send_sems, recv_sems, 1 + k, (cx, cy, c))
                 for k, (cx, cy) in enumerate(chips)]
        for cp in sends:
            cp.start()
        for k, (cx, cy) in enumerate(chips):
            _remote(pair.at[mine], slots.at[2 * cx + cy], send_sems, recv_sems, 1 + k, (cx, cy, c)).wait_recv()
        slots[me] = pair[mine]
        out_ref[mine] = ((slots[0] + slots[1]) + slots[2]) + slots[3]
        last = _remote(out_ref.at[mine], out_ref.at[mine], send_sems, recv_sems, 4, sib)
        last.start()
        _remote(out_ref.at[other], out_ref.at[other], send_sems, recv_sems, 4, sib).wait_recv()
        for cp in [swap] + sends + [last]:
            cp.wait_send()

    vmem = pl.BlockSpec(memory_space=pltpu.VMEM)
    return pl.pallas_call(
        body,
        name="sum_all",
        in_specs=[vmem, vmem],
        out_specs=vmem,
        out_shape=jax.ShapeDtypeStruct(s.shape, s.dtype),
        scratch_shapes=[pltpu.VMEM(s.shape, s.dtype), pltpu.VMEM(s.shape, s.dtype),
                        pltpu.VMEM((N_CHIPS, half, s.shape[1]), s.dtype), pltpu.SemaphoreType.DMA((5,)),
                        pltpu.SemaphoreType.DMA((5,))],
        compiler_params=pltpu.CompilerParams(vmem_limit_bytes=VMEM_LIMIT),
    )(s, after)


ADAM_LR = 0.001
ADAM_B1 = 0.9
ADAM_B2 = 0.999
ADAM_EPS = 1e-08
ADAM_WD = 0.01
ADAM_STEP = 10


def f_adamw(g, w, m, v):
    m = ADAM_B1 * m + (1.0 - ADAM_B1) * g
    v = ADAM_B2 * v + (1.0 - ADAM_B2) * jnp.square(g)
    m_hat = m / (1.0 - ADAM_B1 ** ADAM_STEP)
    v_hat = v / (1.0 - ADAM_B2 ** ADAM_STEP)
    delta = -ADAM_LR * (m_hat / (jnp.sqrt(v_hat) + ADAM_EPS) + ADAM_WD * w)
    return delta, m, v


def adamw_many(gs, ws, ms, vs):
    n = len(gs)

    def body(*refs):
        ins, outs = refs[:4 * n], refs[4 * n:]
        for i in range(n):
            delta, nm, nv = f_adamw(ins[i][...], ins[n + i][...], ins[2 * n + i][...], ins[3 * n + i][...])
            outs[i][...] = delta
            outs[n + i][...] = nm
            outs[2 * n + i][...] = nv

    vmem = pl.BlockSpec(memory_space=pltpu.VMEM)
    res = pl.pallas_call(
        body,
        name="adamw_small",
        in_specs=[vmem] * (4 * n),
        out_specs=[vmem] * (3 * n),
        out_shape=[jax.ShapeDtypeStruct(w.shape, F32) for w in ws] * 3,
    )(*gs, *ws, *ms, *vs)
    return res[:n], res[n:2 * n], res[2 * n:]


EARLY = ["w_in", "w_lora_w", "a_lora_w", "g_lora_w"]
LATE = ["w_ffn1", "w_ffn2", "w_proj_b", "w_out", "w_proj_a"]
LORAS = ["w_lora_w", "a_lora_w", "g_lora_w"]
HALF_W = 512
PIECE_ROWS = {"w_in": 1864, "w_ffn1": 1024, "w_ffn2": 1024, "w_proj_a": 256, "w_proj_b": 256, "w_out": 256,
              "w_lora_w": 32, "a_lora_w": 32, "g_lora_w": 80}
PIECE_OFF = {"w_in": 0, "w_lora_w": 1920, "a_lora_w": 1952, "g_lora_w": 2000,
             "w_ffn1": 0, "w_ffn2": 1024, "w_proj_b": 2048, "w_out": 2304, "w_proj_a": 2560}
LO_OFF = 2080


def pack_rows(group):
    return 2304 if group is EARLY else 2816
SHARD_AXIS = {"w_in": 1, "w_proj_a": 0, "w_lora_w": 1, "a_lora_w": 1, "g_lora_w": 1, "w_proj_b": 0, "w_out": 0,
              "w_ffn1": 1, "w_ffn2": 0}
SHARD_SHAPE = {"w_in": (1024, 1864), "w_proj_a": (256, 1024), "w_lora_w": (64, 256), "a_lora_w": (64, 256),
               "g_lora_w": (160, 256), "w_proj_b": (256, 1024), "w_out": (256, 1024), "w_ffn1": (1024, 1024),
               "w_ffn2": (1024, 1024)}
SHIFT_SHARD = (2, 840)
VECTORS = ["g_mix", "sgu_ln_w", "sgu_ln_b", "w0", "a0", "k_k", "k_a", "r_k", "ln_x_w", "ln_x_b", "g_ffn", "g_final"]
SMALL = VECTORS + ["sgu_w", "sgu_b"]
SMALL_SHAPE = {**{n: (1, 1024) for n in VECTORS}, "sgu_w": (8, 128, 128), "sgu_b": (8, 128)}
WEIGHTS = ["g_mix", "w_in", "sgu_ln_w", "sgu_ln_b", "sgu_w", "sgu_b", "w_proj_a", "shift_b", "w_lora_w", "w0",
           "a_lora_w", "a0", "g_lora_w", "k_k", "k_a", "r_k", "ln_x_w", "ln_x_b", "w_proj_b", "w_out", "g_ffn",
           "w_ffn1", "w_ffn2", "g_final"]


def _size(shape):
    n = 1
    for s in shape:
        n *= s
    return n


def _pack_rows(parts, rows, dtype):
    flat = jnp.concatenate([p.reshape(-1).astype(dtype) for p in parts])
    return jnp.concatenate([flat, jnp.zeros((rows * 1024 - flat.shape[0],), dtype)]).reshape(rows, 1024)


def _unpack_rows(packed, shapes):
    flat = packed.reshape(-1)
    out, off = [], 0
    for shp in shapes:
        out.append(flat[off:off + _size(shp)].reshape(shp))
        off += _size(shp)
    return out


def _shard_of(name, full, j):
    ax = SHARD_AXIS[name]
    n = SHARD_SHAPE[name][ax]
    return lax.slice_in_dim(full, j * n, (j + 1) * n, axis=ax)


def _pad_cols(z, n):
    return jnp.concatenate([z, jnp.zeros((z.shape[0], n - z.shape[1]), z.dtype)], axis=1)


def _row_form(name, s):
    return s.T if name == "w_in" else s


def _half_piece(name, rf, h):
    if name in LORAS:
        r = PIECE_ROWS[name]
        return _pad_cols(rf[h * r:(h + 1) * r], HALF_W)
    return rf[:, HALF_W * h:HALF_W * (h + 1)]


def _pack_half(group, rf_fn, h, dtype, tail=()):
    parts, pos, rows = [], 0, pack_rows(group)
    for n in group:
        if PIECE_OFF[n] > pos:
            parts.append(jnp.zeros((PIECE_OFF[n] - pos, HALF_W), dtype))
        parts.append(_half_piece(n, rf_fn(n), h).astype(dtype))
        pos = PIECE_OFF[n] + PIECE_ROWS[n]
    for t in tail:
        parts.append(t)
        pos += t.shape[0]
    parts.append(jnp.zeros((rows - pos, HALF_W), dtype))
    return jnp.concatenate(parts, axis=0)


def _piece(pack, name):
    return pack[PIECE_OFF[name]:PIECE_OFF[name] + PIECE_ROWS[name]]


def _join_halves(name, p0, p1):
    if name in LORAS:
        return jnp.concatenate([p0[:, :SHARD_SHAPE[name][1]], p1[:, :SHARD_SHAPE[name][1]]], axis=0)
    return jnp.concatenate([p0, p1], axis=1)


def _grad_row_form(name, full, j):
    if name == "w_in":
        return full[SHARD_SHAPE[name][1] * j:SHARD_SHAPE[name][1] * (j + 1)]
    return _shard_of(name, full, j)


def adamw_weight(name, g_own, g_other, w, m, v):
    rows, width = w.shape
    if name in LORAS:
        tm = PIECE_ROWS[name]
        grid = (2, 1)
        native = pl.BlockSpec((tm, width), lambda h, i: (h, 0))
    elif name == "w_in":
        tm, lanes = rows, 128
        grid = (2, HALF_W // lanes)
        native = pl.BlockSpec((tm, lanes), lambda h, i: (0, h * (HALF_W // lanes) + i))
    else:
        tm = 128
        grid = (2, rows // tm)
        native = pl.BlockSpec((tm, HALF_W), lambda h, i: (i, h))
    off = PIECE_OFF[name] // tm
    if name == "w_in":
        packed = pl.BlockSpec((tm, 128), lambda h, i: (0, i))
    else:
        packed = pl.BlockSpec((tm, HALF_W), lambda h, i: (off + i, 0))

    def body(go_ref, gx_ref, w_ref, m_ref, v_ref, g_ref, d_ref, nm_ref, nv_ref):
        g = jnp.where(pl.program_id(0) == lax.axis_index("c"), go_ref[...], gx_ref[...])[:, :w_ref.shape[1]]
        delta, nm, nv = f_adamw(g, w_ref[...], m_ref[...], v_ref[...])
        g_ref[...] = g
        d_ref[...] = delta
        nm_ref[...] = nm
        nv_ref[...] = nv

    return pl.pallas_call(
        body,
        name="adamw_" + name,
        grid=grid,
        in_specs=[packed, packed, native, native, native],
        out_specs=[native] * 4,
        out_shape=[jax.ShapeDtypeStruct(w.shape, F32)] * 4,
        compiler_params=_cparams(2),
    )(g_own, g_other, w, m, v)


def kernel(x, g_mix, w_in, sgu_ln_w, sgu_ln_b, sgu_w, sgu_b, w_proj_a, shift_b, w_lora_w, w0, a_lora_w, a0, g_lora_w, k_k, k_a, r_k, ln_x_w, ln_x_b, w_proj_b, w_out, g_ffn, w_ffn1, w_ffn2, g_final, loss_target, m_g_mix, m_w_in, m_sgu_ln_w, m_sgu_ln_b, m_sgu_w, m_sgu_b, m_w_proj_a, m_shift_b, m_w_lora_w, m_w0, m_a_lora_w, m_a0, m_g_lora_w, m_k_k, m_k_a, m_r_k, m_ln_x_w, m_ln_x_b, m_w_proj_b, m_w_out, m_g_ffn, m_w_ffn1, m_w_ffn2, m_g_final, v_g_mix, v_w_in, v_sgu_ln_w, v_sgu_ln_b, v_sgu_w, v_sgu_b, v_w_proj_a, v_shift_b, v_w_lora_w, v_w0, v_a_lora_w, v_a0, v_g_lora_w, v_k_k, v_k_a, v_r_k, v_ln_x_w, v_ln_x_b, v_w_proj_b, v_w_out, v_g_ffn, v_w_ffn1, v_w_ffn2, v_g_final):
    given = dict(zip(WEIGHTS, (g_mix, w_in, sgu_ln_w, sgu_ln_b, sgu_w, sgu_b, w_proj_a, shift_b, w_lora_w, w0, a_lora_w, a0, g_lora_w, k_k, k_a, r_k, ln_x_w, ln_x_b, w_proj_b, w_out, g_ffn, w_ffn1, w_ffn2, g_final)))
    mom_m = dict(zip(WEIGHTS, (m_g_mix, m_w_in, m_sgu_ln_w, m_sgu_ln_b, m_sgu_w, m_sgu_b, m_w_proj_a, m_shift_b, m_w_lora_w, m_w0, m_a_lora_w, m_a0, m_g_lora_w, m_k_k, m_k_a, m_r_k, m_ln_x_w, m_ln_x_b, m_w_proj_b, m_w_out, m_g_ffn, m_w_ffn1, m_w_ffn2, m_g_final)))
    mom_v = dict(zip(WEIGHTS, (v_g_mix, v_w_in, v_sgu_ln_w, v_sgu_ln_b, v_sgu_w, v_sgu_b, v_w_proj_a, v_shift_b, v_w_lora_w, v_w0, v_a_lora_w, v_a0, v_g_lora_w, v_k_k, v_k_a, v_r_k, v_ln_x_w, v_ln_x_b, v_w_proj_b, v_w_out, v_g_ffn, v_w_ffn1, v_w_ffn2, v_g_final)))
    chip = 2 * lax.axis_index("x") + lax.axis_index("y")

    def local_block(tree, n):
        return tree[n] if n == "g_final" else tree[n][0]

    sb = local_block(given, "shift_b")
    lo_part = lambda z: (z - z.astype(BF16).astype(F32)).astype(BF16)
    row_form = lambda tree: (lambda n: _row_form(n, local_block(tree, n)))
    tile16 = lambda z: jnp.pad(z, ((0, 16 - z.shape[0]), (0, HALF_W - z.shape[1])))
    sb_tiles = [tile16(f(sb[:, lanes])) for f in (lambda z: z.astype(BF16), lo_part)
                for lanes in (slice(0, HALF_W), slice(HALF_W, None))]
    tails = [[_half_piece(n, lo_part(local_block(given, n)), h) for n in LORAS] + sb_tiles for h in range(2)]
    pack_w = jnp.stack([_pack_half(EARLY, row_form(given), h, BF16, tails[h]) for h in range(2)])
    gathered, gathered_token = gather_shards(pack_w)
    gathered = lax.dynamic_update_index_in_dim(gathered, pack_w, chip, 0)
    pack_late = jnp.stack([_pack_half(LATE, row_form(given), h, BF16) for h in range(2)])
    late_state, late_token = split_start("gather_start", _gather_copies, 3, pack_late, (N_CHIPS,) + pack_late.shape,
                                         gathered_token)

    def whole(group, got, own):
        half = lambda n, j, h: jnp.where(chip == j, _piece(own[h], n), _piece(got[j, h], n))
        shard = lambda n, j: _join_halves(n, half(n, j, 0), half(n, j, 1))
        return {n: jnp.concatenate([shard(n, j) for j in range(N_CHIPS)],
                                   axis=0 if n == "w_in" else SHARD_AXIS[n]) for n in group}

    w = whole(EARLY, gathered, pack_w)
    late_weights = lambda after: whole(
        LATE, gather_forward(split_wait("gather_wait", _gather_copies, late_state, after)[1]), pack_late)
    off = LO_OFF
    for n in LORAS:
        r, cols = PIECE_ROWS[n], SHARD_SHAPE[n][1]
        lo = jnp.concatenate([jnp.concatenate([gathered[j, 0, off:off + r, :cols], gathered[j, 1, off:off + r, :cols]],
                                              axis=0) for j in range(N_CHIPS)], axis=1)
        w[n] = w[n].astype(F32) + lo.astype(F32)
        off += r
    sb_tile = lambda j, t, lanes: gathered[j, 0, off + 16 * t:off + 16 * t + 2, :lanes].astype(F32)
    rest = SHIFT_SHARD[1] - HALF_W
    w["shift_b"] = jnp.concatenate(
        [jnp.concatenate([sb_tile(j, 0, HALF_W) + sb_tile(j, 2, HALF_W), sb_tile(j, 1, rest) + sb_tile(j, 3, rest)],
                         axis=1) for j in range(N_CHIPS)], axis=1)
    for n in SMALL:
        w[n] = local_block(given, n).reshape(SMALL_SHAPE[n])

    def pair_start(g_pack, tag):
        return split_start("reduce_pair_start_" + tag, _pair_copies, N_CHIPS, g_pack, (N_CHIPS,) + g_pack.shape[2:])

    def pair_finish(state, after, tag):
        return pair_sum(*split_wait("reduce_pair_wait_" + tag, _pair_copies, state, after), tag, tm=PACK_TILE)

    pack_early = lambda g: jnp.stack([jnp.stack([_pack_half(EARLY, lambda n: _grad_row_form(n, g[n], j), h, F32)
                                                 for h in range(2)]) for j in range(N_CHIPS)])
    loss, grad_x, grads, (late_part, late_slots), early_state = local_step(
        x[0], loss_target[0], w, late_token, late_weights, pair_start, pair_finish, pack_early)

    early_part, early_part16 = pair_finish(early_state, grad_x, "early")
    s_pack = _pack_rows([grads[n] for n in SMALL] + [grads["shift_b"], loss.reshape(1, 1)], SMALL_ROWS, F32)
    chips_state, token = split_start("reduce_chips_start", _chip_copies, 3, early_part16, early_part16.shape)
    my_chip = lambda: 2 * lax.axis_index("x") + lax.axis_index("y")
    out_g, out_d, out_m, out_v = {}, {}, {}, {}

    def finish(group, tag, part, slots):
        half_sum = sum_with_own(part, slots, my_chip, token, tm=PACK_TILE, name="chip_sum_" + tag)
        other_half = exchange_halves(half_sum, tag)
        for n in group:
            res = adamw_weight(n, half_sum, other_half,
                               *[_row_form(n, local_block(t, n)) for t in (given, mom_m, mom_v)])
            for tree, z in zip((out_g, out_d, out_m, out_v), res):
                tree[n] = _row_form(n, z)

    finish(LATE, "late", late_part, late_slots)

    small_shapes = [SMALL_SHAPE[n] for n in SMALL]
    g_small = sum_all(s_pack, token)
    *g_parts, loss = _unpack_rows(g_small, small_shapes + [(2, N_RWKV), ()])
    out_g.update(zip(SMALL, g_parts[:-1]))
    g_sb = lax.dynamic_slice_in_dim(g_parts[-1], chip * SHIFT_SHARD[1], SHIFT_SHARD[1], axis=1)
    out_g["shift_b"] = g_sb
    names = SMALL + ["shift_b"]
    native = lambda tree: [local_block(tree, n).reshape(SMALL_SHAPE.get(n, SHIFT_SHARD)) for n in names]
    small_res = adamw_many(g_parts[:-1] + [g_sb], native(given), native(mom_m), native(mom_v))
    for tree, res in zip((out_d, out_m, out_v), small_res):
        tree.update(zip(names, res))

    after = (out_v["w_out"], out_v["sgu_w"])
    early_slots = split_wait("reduce_chips_wait", _chip_copies, chips_state,
                             jnp.concatenate([z.reshape(-1)[:8] for z in after]))[1]
    finish(EARLY, "early", early_part, early_slots)

    def block_of(tree, n):
        return tree[n].reshape(given[n].shape)

    return (loss, grad_x[None], *[block_of(out_g, n) for n in WEIGHTS], *[block_of(out_d, n) for n in WEIGHTS],
            *[block_of(out_m, n) for n in WEIGHTS], *[block_of(out_v, n) for n in WEIGHTS])
```

```python
import functools

import jax
import jax.numpy as jnp
from jax import lax
from jax.experimental import pallas as pl
from jax.experimental.pallas import tpu as pltpu

F32 = jnp.float32
BF16 = jnp.bfloat16

D_MODEL = 1024
N_HEADS = 16
HEAD = 64
SCAN_CHUNK = 64

VMEM_LIMIT = 56 * 1024 * 1024


_BDIMS = {
    "nn": (((2,), (1,)), ((0,), (0,))),
    "nt": (((2,), (2,)), ((0,), (0,))),
    "tn": (((1,), (1,)), ((0,), (0,))),
}


def _raw_bdot(x, y, mode, fine):
    if fine:
        return lax.dot_general(x, y, _BDIMS[mode], precision=lax.Precision.HIGH, preferred_element_type=F32)
    return lax.dot_general(x.astype(BF16), y.astype(BF16), _BDIMS[mode], preferred_element_type=F32)


@functools.partial(jax.custom_vjp, nondiff_argnums=(2, 3))
def bdot(x, y, mode, fine=True):
    return _raw_bdot(x, y, mode, fine)


def _bdot_fwd(x, y, mode, fine):
    return _raw_bdot(x, y, mode, fine), (x, y)


def _bdot_bwd(mode, fine, res, g):
    x, y = res
    if mode == "nn":
        return bdot(g, y, "nt", fine), bdot(x, g, "tn", fine)
    if mode == "nt":
        return bdot(g, y, "nn", fine), bdot(g, x, "tn", fine)
    return bdot(y, g, "nt", fine), bdot(x, g, "nn", fine)


bdot.defvjp(_bdot_fwd, _bdot_bwd)


def _scan_chunk(S0, r, lw, k, v, a, b):
    nh, lc, _ = r.shape
    ti = lax.broadcasted_iota(jnp.int32, (lc, lc), 0)
    si = lax.broadcasted_iota(jnp.int32, (lc, lc), 1)
    incl = (si <= ti).astype(F32)
    strict = (si < ti).astype(F32)
    eye = (si == ti).astype(F32)
    cl = bdot(jnp.broadcast_to(incl, (nh, lc, lc)), lw, "nn")
    cl_last = cl[:, lc - 1:lc, :]
    g_last = jnp.exp(cl_last - cl)
    at = a * jnp.exp(cl - lw)
    bt = b * jnp.exp(-cl)
    kt = k * jnp.exp(-cl)
    rt = r * jnp.exp(cl)
    ar = jnp.concatenate([at, rt], axis=1)
    ar_b = bdot(ar, bt, "nt", False)
    ar_k = bdot(ar, kt, "nt", False)
    m_ab, m_rb = ar_b[:, :lc] * strict, ar_b[:, lc:] * incl
    m_ak, m_rk = ar_k[:, :lc] * strict, ar_k[:, lc:] * incl
    x = eye + m_ab
    p = bdot(m_ab, m_ab, "nn", False)
    n = 2
    while n * 2 < lc:
        px = bdot(jnp.concatenate([p, x], axis=1), p, "nn", False)
        p = px[:, :lc]
        x = x + px[:, lc:]
        n *= 2
    x = x + bdot(x, p, "nn", False)
    ar_s = bdot(ar, S0, "nt", False)
    akrk_v = bdot(jnp.concatenate([m_ak, m_rk], axis=1), v, "nn", False)
    u = bdot(x, ar_s[:, :lc] + akrk_v[:, :lc], "nn", False)
    o = ar_s[:, lc:] + bdot(m_rb, u, "nn", False) + akrk_v[:, lc:]
    s_last = S0 * jnp.exp(cl_last) + bdot(jnp.concatenate([u, v], axis=1),
                                          jnp.concatenate([b * g_last, k * g_last], axis=1), "tn", False)
    return o, s_last


def _split_heads(z):
    return jnp.stack([z[:, HEAD * h:HEAD * (h + 1)] for h in range(N_HEADS)], axis=0)


def _merge_heads(z):
    return jnp.concatenate([z[h] for h in range(N_HEADS)], axis=1)


def _scan_specs(t, ops, rev):
    nc = t // SCAN_CHUNK
    row = (lambda c: nc - 1 - c) if rev else (lambda c: c)
    specs = [pl.BlockSpec((SCAN_CHUNK, D_MODEL), lambda c, cb=cb: (row(c), cb)) for _, cb in ops]
    state = pl.BlockSpec((1, N_HEADS, HEAD, HEAD), lambda c: (row(c), 0, 0, 0))
    return nc, specs, state


def scan_fwd(ops):
    t = ops[0][0].shape[0]
    nc, specs, state = _scan_specs(t, ops, False)

    def body(r_ref, lw_ref, k_ref, v_ref, a_ref, b_ref, o_ref, s0_ref, s_scr):
        @pl.when(pl.program_id(0) == 0)
        def _():
            s_scr[...] = jnp.zeros_like(s_scr)

        s0 = s_scr[...]
        s0_ref[0] = s0
        o, s_last = _scan_chunk(s0, *[_split_heads(z[...]) for z in (r_ref, lw_ref, k_ref, v_ref, a_ref, b_ref)])
        o_ref[...] = _merge_heads(o)
        s_scr[...] = s_last

    return pl.pallas_call(
        body,
        name="scan_fwd",
        grid=(nc,),
        in_specs=specs,
        out_specs=[pl.BlockSpec((SCAN_CHUNK, D_MODEL), lambda c: (c, 0)), state],
        out_shape=[jax.ShapeDtypeStruct((t, D_MODEL), F32), jax.ShapeDtypeStruct((nc, N_HEADS, HEAD, HEAD), F32)],
        scratch_shapes=[pltpu.VMEM((N_HEADS, HEAD, HEAD), F32)],
        compiler_params=_cparams(1),
    )(*[a for a, _ in ops])


def scan_bwd(ops, s0s, do, part):
    t = ops[0][0].shape[0]
    nc, specs, state = _scan_specs(t, ops + [(do, 0)], True)

    def body(r_ref, lw_ref, k_ref, v_ref, a_ref, b_ref, do_ref, s0_ref, part_ref, *rest):
        out_refs, slots_ref, ds_scr, send_sems, recv_sems = rest[:6], rest[6], rest[7], rest[8], rest[9]
        step = pl.program_id(0)
        x, y, c = _coords()
        me = 2 * x + y
        chips = _other_chips(x, y)
        sends = [_remote(part_ref.at[2 * cx + cy], slots_ref.at[me], send_sems, recv_sems, k, (cx, cy, c))
                 for k, (cx, cy) in enumerate(chips)]

        @pl.when(step == 0)
        def _():
            ds_scr[...] = jnp.zeros_like(ds_scr)
            for cp in sends:
                cp.start()

        _, vjp = jax.vjp(_scan_chunk, s0_ref[0],
                         *[_split_heads(z[...]) for z in (r_ref, lw_ref, k_ref, v_ref, a_ref, b_ref)])
        grads = vjp((_split_heads(do_ref[...]), ds_scr[...]))
        for o_ref, g in zip(out_refs, grads[1:]):
            o_ref[...] = _merge_heads(g)
        ds_scr[...] = grads[0]

        @pl.when(step == nc - 1)
        def _():
            for k, (cx, cy) in enumerate(chips):
                _remote(part_ref.at[me], slots_ref.at[2 * cx + cy], send_sems, recv_sems, k, (cx, cy, c)).wait_recv()
            for cp in sends:
                cp.wait_send()

    return pl.pallas_call(
        body,
        name="scan_bwd",
        grid=(nc,),
        in_specs=specs + [state, _ANY],
        out_specs=[pl.BlockSpec((SCAN_CHUNK, D_MODEL), lambda c: (nc - 1 - c, 0))] * 6 + [_ANY],
        out_shape=[jax.ShapeDtypeStruct((t, D_MODEL), F32)] * 6 + [jax.ShapeDtypeStruct(part.shape, part.dtype)],
        scratch_shapes=[pltpu.VMEM((N_HEADS, HEAD, HEAD), F32), pltpu.SemaphoreType.DMA((3,)),
                        pltpu.SemaphoreType.DMA((3,))],
        compiler_params=_cparams(1),
    )(*[a for a, _ in ops], do, s0s, part)


_MDIMS = {
    "nn": (((1,), (0,)), ((), ())),
    "nt": (((1,), (1,)), ((), ())),
    "tn": (((0,), (0,)), ((), ())),
}


def _raw_mdot(x, y, mode, exact):
    if exact:
        return lax.dot_general(x, y, _MDIMS[mode], precision=lax.Precision.HIGH, preferred_element_type=F32)
    return lax.dot_general(x.astype(BF16), y.astype(BF16), _MDIMS[mode], preferred_element_type=F32)


@functools.partial(jax.custom_vjp, nondiff_argnums=(2, 3))
def mdot(x, y, mode, exact):
    return _raw_mdot(x, y, mode, exact)


def _mdot_fwd(x, y, mode, exact):
    return _raw_mdot(x, y, mode, exact), (x, y)


def _mdot_bwd(mode, exact, res, g):
    x, y = res
    if mode == "nn":
        return mdot(g, y, "nt", exact), mdot(x, g, "tn", exact)
    if mode == "nt":
        return mdot(g, y, "nn", exact), mdot(g, x, "tn", exact)
    return mdot(y, g, "nt", exact), mdot(x, g, "nn", exact)


mdot.defvjp(_mdot_fwd, _mdot_bwd)


def _seg_ones():
    i = lax.broadcasted_iota(jnp.int32, (256, 256), 0) // HEAD
    j = lax.broadcasted_iota(jnp.int32, (256, 256), 1) // HEAD
    return (i == j).astype(BF16)


@jax.custom_vjp
def segsum(x):
    bd = _seg_ones()
    hi = x.astype(BF16)
    lo = (x - hi.astype(F32)).astype(BF16)
    cols = []
    for j in range(x.shape[1] // 256):
        sl = slice(256 * j, 256 * (j + 1))
        cols.append(jnp.dot(hi[:, sl], bd, preferred_element_type=F32)
                    + jnp.dot(lo[:, sl], bd, preferred_element_type=F32))
    return jnp.concatenate(cols, axis=1)


segsum.defvjp(lambda x: (segsum(x), None), lambda _, g: (segsum(g),))


NORM_EPS = 1e-6
LN_EPS = 1e-5
GN_EPS = 64e-5
SGU_CHUNK = 128
SGU_GROUPS = 8


def _rms(x, g):
    return x * lax.rsqrt(jnp.mean(x * x, axis=-1, keepdims=True) + NORM_EPS) * g


def f_norm_in(x, g):
    return _rms(x, g), x


def f_sgu(p, ln_w, ln_b, sw, sbt):
    tm = p.shape[0]
    z = 0.5 * p * (1.0 + lax.erf(p * 0.7071067811865476))
    u, v = z[:, :D_MODEL], z[:, D_MODEL:]
    mu = jnp.mean(v, axis=-1, keepdims=True)
    d = v - mu
    vn = d * lax.rsqrt(jnp.mean(d * d, axis=-1, keepdims=True) + LN_EPS) * ln_w + ln_b
    ii = lax.broadcasted_iota(jnp.int32, (SGU_CHUNK, SGU_CHUNK), 0)
    jj = lax.broadcasted_iota(jnp.int32, (SGU_CHUNK, SGU_CHUNK), 1)
    mask = (jj <= ii).astype(F32)
    gi = lax.broadcasted_iota(jnp.int32, (SGU_GROUPS, D_MODEL), 0)
    ci = lax.broadcasted_iota(jnp.int32, (SGU_GROUPS, D_MODEL), 1) // SGU_CHUNK
    bias = mdot(sbt, (gi == ci).astype(F32), "nn", True)
    rows = []
    for c in range(tm // SGU_CHUNK):
        cols = []
        for g in range(SGU_GROUPS):
            blk = vn[c * SGU_CHUNK:(c + 1) * SGU_CHUNK, g * SGU_CHUNK:(g + 1) * SGU_CHUNK]
            cols.append(mdot(sw[g] * mask, blk, "nn", False))
        rows.append(jnp.concatenate(cols, axis=1) + bias)
    return (u * jnp.concatenate(rows, axis=0),)


def _softplus(x):
    return jnp.maximum(x, 0.0) + jnp.log1p(jnp.exp(-jnp.abs(x)))


def f_pre(q, wl, w0, al, a0, gl, k_k, k_a):
    qr, qk, qv, ql = q[:, :1024], q[:, 1024:2048], q[:, 2048:3072], q[:, 3072:]
    return _f_pre(qr, qk, qv, ql, wl, w0, al, a0, gl, k_k, k_a)


def _f_pre(qr, qk, qv, ql, wl, w0, al, a0, gl, k_k, k_a):
    xw, xa, xg = ql[:, :128], ql[:, 128:256], ql[:, 256:512]
    wr = -_softplus(-(w0 + mdot(jnp.tanh(xw), wl, "nn", False))) - 0.5
    lw = -jnp.exp(wr)
    aa = jax.nn.sigmoid(a0 + mdot(xa, al, "nn", False))
    g = mdot(jax.nn.sigmoid(xg), gl, "nn", False)
    kkr = qk * k_k
    kk = kkr / jnp.maximum(jnp.sqrt(segsum(kkr * kkr)), 1e-12)
    kp = qk * (1.0 + (aa - 1.0) * k_a)
    return qr, lw, kp, qv, -kk, kk * aa, g, qr, kp, qv


def f_post(o, r, kp, v, g, lnw, lnb, rk):
    mu = segsum(o) * (1.0 / HEAD)
    d = o - mu
    gn = d * lax.rsqrt(segsum(d * d) * (1.0 / HEAD) + GN_EPS)
    return ((gn * lnw + lnb + segsum(r * kp * rk) * v) * g,)


def f_mix(ya, yb, ga, gb):
    return (jax.nn.sigmoid(ga) * ya + jax.nn.sigmoid(gb) * yb,)


def f_ffn_in(h1, g):
    return _rms(h1, g), h1


def f_final(h1, m3, tgt, g):
    y = _rms(h1 + m3, g)
    err = jnp.square(y - tgt)
    return 0.5 * jnp.sum(jnp.mean(err, axis=-1))


def _cparams(n_grid):
    return pltpu.CompilerParams(dimension_semantics=("arbitrary",) * n_grid, vmem_limit_bytes=VMEM_LIMIT)


def _tile_spec(tm, w, cb):
    return pl.BlockSpec((tm, w), lambda i: (i, cb))


def _const_spec(c):
    nd = c.ndim
    return pl.BlockSpec(c.shape, lambda i: (0,) * nd)


def ew_call(fn, tiled, consts, outs, *, tm, name):
    t = tiled[0][0].shape[0]
    n_t, n_c = len(tiled), len(consts)

    def body(*refs):
        tv = [r[...].astype(F32) for r in refs[:n_t]]
        cv = [r[...] for r in refs[n_t:n_t + n_c]]
        res = fn(*tv, *cv)
        for o_ref, val in zip(refs[n_t + n_c:], res):
            o_ref[...] = val.astype(o_ref.dtype)

    return pl.pallas_call(
        body,
        name=name,
        grid=(t // tm,),
        in_specs=[_tile_spec(tm, w, cb) for _, w, cb in tiled] + [_const_spec(c) for c in consts],
        out_specs=[_tile_spec(tm, w, 0) for w, _ in outs],
        out_shape=[jax.ShapeDtypeStruct((t, w), dt) for w, dt in outs],
        compiler_params=_cparams(1),
    )(*[a for a, _, _ in tiled], *consts)


def ew_vjp_call(fn, tiled, consts, cots, d_tiled, d_consts, *, tm, name):
    t = tiled[0][0].shape[0]
    n_t, n_c, n_g = len(tiled), len(consts), len(cots)
    dt_list = [(i, dt) for i, dts in enumerate(d_tiled) for dt in dts]
    dc_list = [i for i, want in enumerate(d_consts) if want]

    def body(*refs):
        tv = [r[...].astype(F32) for r in refs[:n_t]]
        cv = [r[...] for r in refs[n_t:n_t + n_c]]
        gv = tuple(r[...].astype(F32) for r in refs[n_t + n_c:n_t + n_c + n_g])
        out_refs = refs[n_t + n_c + n_g:]
        _, vjp = jax.vjp(fn, *tv, *cv)
        grads = vjp(gv)
        for o_ref, (i, _) in zip(out_refs, dt_list):
            o_ref[...] = grads[i].astype(o_ref.dtype)
        acc_refs = out_refs[len(dt_list):]

        @pl.when(pl.program_id(0) == 0)
        def _():
            for a_ref in acc_refs:
                a_ref[...] = jnp.zeros_like(a_ref)

        for a_ref, i in zip(acc_refs, dc_list):
            a_ref[...] += grads[n_t + i]

    res = pl.pallas_call(
        body,
        name=name,
        grid=(t // tm,),
        in_specs=[_tile_spec(tm, w, cb) for _, w, cb in tiled] + [_const_spec(c) for c in consts]
        + [_tile_spec(tm, w, cb) for _, w, cb in cots],
        out_specs=[_tile_spec(tm, tiled[i][1], 0) for i, _ in dt_list] + [_const_spec(consts[i]) for i in dc_list],
        out_shape=[jax.ShapeDtypeStruct((t, tiled[i][1]), dt) for i, dt in dt_list]
        + [jax.ShapeDtypeStruct(consts[i].shape, F32) for i in dc_list],
        compiler_params=_cparams(1),
    )(*[a for a, _, _ in tiled], *consts, *[a for a, _, _ in cots])
    return res[:len(dt_list)], res[len(dt_list):]


def mm(a, b, mode, *, tm, tn, name, out_dtypes=(F32,), epi=None, extras=(), into=None):
    m = a.shape[1] if mode == "tn" else a.shape[0]
    kd = a.shape[0] if mode == "tn" else a.shape[1]
    n = b.shape[0] if mode == "nt" else b.shape[1]
    tm, tn = min(tm, m), min(tn, n)
    if mode == "nn":
        a_spec = pl.BlockSpec((tm, kd), lambda i, j: (i, 0))
        b_spec = pl.BlockSpec((kd, tn), lambda i, j: (0, j))
    elif mode == "nt":
        a_spec = pl.BlockSpec((tm, kd), lambda i, j: (i, 0))
        b_spec = pl.BlockSpec((tn, kd), lambda i, j: (j, 0))
    else:
        a_spec = pl.BlockSpec((kd, tm), lambda i, j: (0, i))
        b_spec = pl.BlockSpec((kd, tn), lambda i, j: (0, j))
    n_e = len(extras)
    o_spec = pl.BlockSpec((tm, tn), lambda i, j: (i, j))

    if into is not None:
        buf, place = into

        def body_into(a_ref, b_ref, buf_ref, o_ref):
            o_ref[0, 0] = lax.dot_general(a_ref[...].astype(BF16), b_ref[...].astype(BF16), _MDIMS[mode],
                                          preferred_element_type=F32)

        return pl.pallas_call(
            body_into,
            name=name,
            grid=(m // tm, n // tn),
            in_specs=[a_spec, b_spec, pl.BlockSpec(memory_space=pl.ANY)],
            out_specs=pl.BlockSpec((1, 1, tm, tn), lambda i, j: (*place(i, j), 0)),
            out_shape=jax.ShapeDtypeStruct(buf.shape, F32),
            input_output_aliases={2: 0},
            compiler_params=_cparams(2),
        )(a, b, buf)

    def body(a_ref, b_ref, *refs):
        c = lax.dot_general(a_ref[...].astype(BF16), b_ref[...].astype(BF16), _MDIMS[mode],
                            preferred_element_type=F32)
        res = epi(c, *[r[...] for r in refs[:n_e]]) if epi is not None else (c,)
        for o_ref, val in zip(refs[n_e:], res):
            o_ref[...] = val.astype(o_ref.dtype)

    res = pl.pallas_call(
        body,
        name=name,
        grid=(m // tm, n // tn),
        in_specs=[a_spec, b_spec] + [o_spec] * n_e,
        out_specs=[o_spec] * len(out_dtypes),
        out_shape=[jax.ShapeDtypeStruct((m, n), dt) for dt in out_dtypes],
        compiler_params=_cparams(2),
    )(a, b, *extras)
    return res if len(out_dtypes) > 1 else res[0]


RWKV_COL0 = 4096
RWKV_WIDTH = 3584
SHIFT_BLK = 512


def _shift_down(p, prev_row):
    rows = lax.broadcasted_iota(jnp.int32, p.shape, 0)
    return jnp.where(rows == 0, prev_row, pltpu.roll(p, 1, 0))


def shiftmix_fwd(p_all, sbp, *, tm):
    t = p_all.shape[0]
    tm = min(tm, t)
    c0 = RWKV_COL0 // SHIFT_BLK
    hb = tm // 8

    def body(p_ref, halo_ref, sb_ref, q_ref):
        p = p_ref[...]
        prev = jnp.where(pl.program_id(0) == 0, 0.0, halo_ref[7:8, :])
        q_ref[...] = p * sb_ref[0:1, :] + _shift_down(p, prev) * sb_ref[1:2, :]

    return pl.pallas_call(
        body,
        name="shiftmix_fwd",
        grid=(t // tm, RWKV_WIDTH // SHIFT_BLK),
        in_specs=[
            pl.BlockSpec((tm, SHIFT_BLK), lambda i, j: (i, c0 + j)),
            pl.BlockSpec((8, SHIFT_BLK), lambda i, j: (jnp.maximum(i * hb - 1, 0), c0 + j)),
            pl.BlockSpec((2, SHIFT_BLK), lambda i, j: (0, j)),
        ],
        out_specs=pl.BlockSpec((tm, SHIFT_BLK), lambda i, j: (i, j)),
        out_shape=jax.ShapeDtypeStruct((t, RWKV_WIDTH), F32),
        compiler_params=_cparams(2),
    )(p_all, p_all, sbp)


def shiftmix_bwd(dq, col0, p_all, sbp, *, tm, name):
    t, w = dq.shape
    n_i = t // tm
    hb = tm // 8
    cq = col0 // SHIFT_BLK
    cp = (RWKV_COL0 + col0) // SHIFT_BLK

    def body(dq_ref, dqn_ref, p_ref, ph_ref, sb_ref, dp_ref, dsb_ref):
        i = pl.program_id(1)
        dq_t = dq_ref[...]
        rows = lax.broadcasted_iota(jnp.int32, dq_t.shape, 0)
        nxt = jnp.where(i == n_i - 1, 0.0, dqn_ref[0:1, :])
        up = jnp.where(rows == tm - 1, nxt, pltpu.roll(dq_t, tm - 1, 0))
        dp_ref[...] = (dq_t * sb_ref[0:1, :] + up * sb_ref[1:2, :]).astype(dp_ref.dtype)
        p = p_ref[...]
        prev = jnp.where(i == 0, 0.0, ph_ref[7:8, :])
        s0 = jnp.sum(dq_t * p, axis=0, keepdims=True)
        s1 = jnp.sum(dq_t * _shift_down(p, prev), axis=0, keepdims=True)
        two = lax.broadcasted_iota(jnp.int32, (2, SHIFT_BLK), 0)

        @pl.when(i == 0)
        def _():
            dsb_ref[...] = jnp.zeros_like(dsb_ref)

        dsb_ref[...] += jnp.where(two == 0, s0, s1)

    return pl.pallas_call(
        body,
        name=name,
        grid=(w // SHIFT_BLK, n_i),
        in_specs=[
            pl.BlockSpec((tm, SHIFT_BLK), lambda j, i: (i, j)),
            pl.BlockSpec((8, SHIFT_BLK), lambda j, i: (jnp.minimum((i + 1) * hb, t // 8 - 1), j)),
            pl.BlockSpec((tm, SHIFT_BLK), lambda j, i: (i, cp + j)),
            pl.BlockSpec((8, SHIFT_BLK), lambda j, i: (jnp.maximum(i * hb - 1, 0), cp + j)),
            pl.BlockSpec((2, SHIFT_BLK), lambda j, i: (0, cq + j)),
        ],
        out_specs=[
            pl.BlockSpec((tm, SHIFT_BLK), lambda j, i: (i, j)),
            pl.BlockSpec((2, SHIFT_BLK), lambda j, i: (0, j)),
        ],
        out_shape=[jax.ShapeDtypeStruct((t, w), BF16), jax.ShapeDtypeStruct((2, w), F32)],
        compiler_params=_cparams(2),
    )(dq, dq, p_all, p_all, sbp)


def final_call(h1, m3, tgt, g_final, *, tm):
    t = h1.shape[0]

    def body(h1_ref, m3_ref, tgt_ref, g_ref, dh_ref, dhb_ref, dg_ref, loss_ref):
        loss, vjp = jax.vjp(f_final, h1_ref[...], m3_ref[...], tgt_ref[...], g_ref[...])
        dh, _, _, dg = vjp(jnp.ones((), F32))
        dh_ref[...] = dh
        dhb_ref[...] = dh.astype(BF16)

        @pl.when(pl.program_id(0) == 0)
        def _():
            dg_ref[...] = jnp.zeros_like(dg_ref)
            loss_ref[...] = jnp.zeros_like(loss_ref)

        dg_ref[...] += dg
        loss_ref[...] += jnp.full(loss_ref.shape, loss, F32)

    tile = _tile_spec(tm, D_MODEL, 0)
    return pl.pallas_call(
        body,
        name="final_loss",
        grid=(t // tm,),
        in_specs=[tile, tile, tile, _const_spec(g_final)],
        out_specs=[tile, tile, _const_spec(g_final), pl.BlockSpec((8, 128), lambda i: (0, 0))],
        out_shape=[jax.ShapeDtypeStruct((t, D_MODEL), F32), jax.ShapeDtypeStruct((t, D_MODEL), BF16),
                   jax.ShapeDtypeStruct(g_final.shape, F32), jax.ShapeDtypeStruct((8, 128), F32)],
        compiler_params=_cparams(1),
    )(h1, m3, tgt, g_final)


N_SGU = 2048
N_RWKV = 3360
LORA_W, LORA_A, LORA_G = 64, 64, 160


def _pad_rwkv_cols(z):
    zero = lambda n: jnp.zeros(z.shape[:-1] + (n,), z.dtype)
    return jnp.concatenate([z[..., :3072], z[..., 3072:3136], zero(64), z[..., 3136:3200], zero(64),
                            z[..., 3200:3360], zero(96)], axis=-1)


def _unpad_rwkv_cols(z):
    return jnp.concatenate([z[..., :3072], z[..., 3072:3136], z[..., 3200:3264], z[..., 3328:3488]], axis=-1)


def _pad_win_rows(wt):
    z = wt[N_SGU:N_SGU + N_RWKV]
    zero = lambda n: jnp.zeros((n, wt.shape[1]), wt.dtype)
    return jnp.concatenate([wt[:N_SGU], wt[N_SGU + N_RWKV:], z[:3072], z[3072:3136], zero(64), z[3136:3200], zero(64),
                            z[3200:3360], zero(96)], axis=0)


def _unpad_win_rows(wt):
    z = wt[RWKV_COL0:]
    return jnp.concatenate([wt[:N_SGU], z[:3072], z[3072:3136], z[3200:3264], z[3328:3488], wt[N_SGU:RWKV_COL0]],
                           axis=0)


def _pad_rows(w, n):
    return jnp.concatenate([w, jnp.zeros((n - w.shape[0],) + w.shape[1:], w.dtype)], axis=0)


def _relu2_epi(c):
    return c, jnp.square(jnp.maximum(c, 0.0))


def _relu2_bwd_epi(c, hid):
    return (c * (2.0 * jnp.maximum(hid.astype(F32), 0.0)),)


def _add_epi(c, x):
    return (c + x,)


def _pre_fwd(*args):
    res = f_pre(*args)
    return res[1], res[2], res[4], res[5], res[6]


def local_step(x, tgt, w, late_token, late_weights, pair_start, pair_finish, pack_early):
    d = D_MODEL
    win_pt = _pad_win_rows(w["w_in"])
    sbp = _pad_rwkv_cols(w["shift_b"])
    wl = _pad_rows(w["w_lora_w"], 128)
    al = _pad_rows(w["a_lora_w"], 128)
    gl = _pad_rows(w["g_lora_w"], 256)
    sbt = w["sgu_b"].T

    (a_bf,) = ew_call(lambda x_, g_: (f_norm_in(x_, g_)[0],), [(x, d, 0)], [w["g_mix"] + late_token[:1, :1]],
                      [(d, BF16)], tm=256, name="norm_in")
    p_all = mm(a_bf, win_pt, "nt", tm=2048, tn=1280, name="mm_in")
    sgu_t = [(p_all, 2 * d, 0)]
    sgu_c = [w["sgu_ln_w"], w["sgu_ln_b"], w["sgu_w"], sbt]
    (s_bf,) = ew_call(f_sgu, sgu_t, sgu_c, [(d, BF16)], tm=256, name="sgu_fwd")
    q = shiftmix_fwd(p_all, sbp, tm=1024)
    pre_t = [(q, RWKV_WIDTH, 0)]
    pre_c = [wl, w["w0"], al, w["a0"], gl, w["k_k"], w["k_a"]]
    lw, kp, na, nb, g = ew_call(_pre_fwd, pre_t, pre_c, [(d, F32)] * 5, tm=256, name="rwkv_pre_fwd")
    scan_ops = [(q, 0), (lw, 0), (kp, 0), (q, 2), (na, 0), (nb, 0)]
    o, s0s = scan_fwd(scan_ops)
    w = {**w, **late_weights(o)}
    ya = mm(s_bf, w["w_proj_a"], "nn", tm=512, tn=1024, name="mm_proj_a")
    post_t = [(o, d, 0), (q, d, 0), (kp, d, 0), (q, d, 2), (g, d, 0)]
    post_c = [w["ln_x_w"], w["ln_x_b"], w["r_k"]]
    (ob_bf,) = ew_call(f_post, post_t, post_c, [(d, BF16)], tm=256, name="rwkv_post_fwd")
    yb = mm(ob_bf, w["w_proj_b"], "nn", tm=512, tn=1024, name="mm_proj_b")
    mix_t = [(ya, d, 0), (yb, d, 0), (p_all, d, 2), (p_all, d, 3)]
    (mixed_bf,) = ew_call(f_mix, mix_t, [], [(d, BF16)], tm=256, name="mix_fwd")
    h1 = mm(mixed_bf, w["w_out"], "nn", tm=512, tn=1024, name="mm_out", epi=_add_epi, extras=(x,))
    (f_bf,) = ew_call(lambda h_, g_: (f_ffn_in(h_, g_)[0],), [(h1, d, 0)], [w["g_ffn"]], [(d, BF16)], tm=256,
                      name="ffn_norm")
    hid, act_bf = mm(f_bf, w["w_ffn1"], "nn", tm=2048, tn=1024, name="mm_ffn1", out_dtypes=(BF16, BF16), epi=_relu2_epi)
    m3 = mm(act_bf, w["w_ffn2"], "nn", tm=1024, tn=512, name="mm_ffn2")
    dh2, dh2_bf, dg_final, loss = final_call(h1, m3, tgt, w["g_final"], tm=256)

    dhid_bf = mm(dh2_bf, w["w_ffn2"], "nt", tm=2048, tn=1024, name="mm_dact", out_dtypes=(BF16,), epi=_relu2_bwd_epi,
                 extras=(hid,))
    late_g = lax.empty((N_CHIPS, 2, pack_rows(LATE), HALF_W), F32)
    late_g = mm(act_bf, dh2_bf, "tn", tm=1024, tn=HALF_W, name="mm_dw_ffn2",
                into=(late_g, lambda i, j: (i, j, PIECE_OFF["w_ffn2"] // 1024)))
    df = mm(dhid_bf, w["w_ffn1"], "nt", tm=1024, tn=512, name="mm_df")
    late_g = mm(f_bf, dhid_bf, "tn", tm=1024, tn=HALF_W, name="mm_dw_ffn1",
                into=(late_g, lambda i, j: (j // 2, j % 2, PIECE_OFF["w_ffn1"] // 1024)))
    (dh1, dh1_bf), (dg_ffn,) = ew_vjp_call(f_ffn_in, [(h1, d, 0)], [w["g_ffn"]], [(df, d, 0), (dh2, d, 0)],
                                           [(F32, BF16)], [True], tm=256, name="ffn_norm_bwd")
    dmixed = mm(dh1_bf, w["w_out"], "nt", tm=512, tn=1024, name="mm_dmixed")
    late_g = mm(mixed_bf, dh1_bf, "tn", tm=256, tn=HALF_W, name="mm_dw_out",
                into=(late_g, lambda i, j: (i, j, PIECE_OFF["w_out"] // 256)))
    (dya_bf, dyb_bf, dga_bf, dgb_bf), _ = ew_vjp_call(f_mix, mix_t, [], [(dmixed, d, 0)], [(BF16,)] * 4, [], tm=256,
                                                      name="mix_bwd")
    dob = mm(dyb_bf, w["w_proj_b"], "nt", tm=512, tn=1024, name="mm_dob")
    late_g = mm(ob_bf, dyb_bf, "tn", tm=256, tn=HALF_W, name="mm_dw_proj_b",
                into=(late_g, lambda i, j: (i, j, PIECE_OFF["w_proj_b"] // 256)))
    late_g = mm(s_bf, dya_bf, "tn", tm=256, tn=HALF_W, name="mm_dw_proj_a",
                into=(late_g, lambda i, j: (i, j, PIECE_OFF["w_proj_a"] // 256)))
    late_state, late_token = pair_start(late_g, "late")
    post_c_after = [w["ln_x_w"] + late_token[:1, :1]] + post_c[1:]
    (do, dr_p, dkp_p, dv_p, dg), (dlnx_w, dlnx_b, dr_k) = ew_vjp_call(
        f_post, post_t, post_c_after, [(dob, d, 0)], [(F32,)] * 5, [True] * 3, tm=256, name="rwkv_post_bwd")
    late_part, late_part16 = pair_finish(late_state, do, "late")
    *scan_g, late_slots = scan_bwd(scan_ops, s0s, do, late_part16)
    pre_g = [(z, d, 0) for z in scan_g] + [(dg, d, 0), (dr_p, d, 0), (dkp_p, d, 0), (dv_p, d, 0)]
    (dq,), (dwl, dw0, dal, da0, dgl, dk_k, dk_a) = ew_vjp_call(
        f_pre, pre_t, pre_c, pre_g, [(F32,)], [True] * 7, tm=256, name="rwkv_pre_bwd")
    dp_rwkv, dsb = shiftmix_bwd(dq, 0, p_all, sbp, tm=512, name="shiftmix_bwd")
    ds = mm(dya_bf, w["w_proj_a"], "nt", tm=512, tn=1024, name="mm_ds")
    (dp_sgu,), (dln_w, dln_b, dsw, dsbt) = ew_vjp_call(f_sgu, sgu_t, sgu_c, [(ds, d, 0)], [(BF16,)], [True] * 4,
                                                       tm=256, name="sgu_bwd")
    dp_all = jnp.concatenate([dp_sgu, dga_bf, dgb_bf, dp_rwkv], axis=1)
    d_in_pt = mm(dp_all, a_bf, "tn", tm=1280, tn=1024, name="mm_dw_in")
    early_state, early_token = pair_start(pack_early({
        "w_in": _unpad_win_rows(d_in_pt), "w_lora_w": dwl[:LORA_W], "a_lora_w": dal[:LORA_A],
        "g_lora_w": dgl[:LORA_G]}), "early")
    da = mm(dp_all, win_pt, "nn", tm=1024, tn=256, name="mm_da")
    g_mix_after = w["g_mix"] + early_token[:1, :1]
    (grad_x,), (dg_mix,) = ew_vjp_call(f_norm_in, [(x, d, 0)], [g_mix_after], [(da, d, 0), (dh1, d, 0)], [(F32,)],
                                       [True], tm=256, name="norm_in_bwd")

    grads = {
        "g_mix": dg_mix, "sgu_ln_w": dln_w, "sgu_ln_b": dln_b, "sgu_w": dsw, "sgu_b": dsbt.T,
        "shift_b": _unpad_rwkv_cols(dsb),
        "w0": dw0, "a0": da0, "k_k": dk_k, "k_a": dk_a, "r_k": dr_k, "ln_x_w": dlnx_w, "ln_x_b": dlnx_b,
        "g_ffn": dg_ffn, "g_final": dg_final,
    }
    return loss[0, 0], grad_x, grads, (late_part, late_slots), early_state


MESH = pl.DeviceIdType.MESH
N_CHIPS = 4
PACK_TILE = 256
SMALL_ROWS = 160
_ANY = pl.BlockSpec(memory_space=pl.ANY)


def _coords():
    return lax.axis_index("x"), lax.axis_index("y"), lax.axis_index("c")


def _other_chips(x, y):
    return [(1 - x, y), (x, 1 - y), (1 - x, 1 - y)]


def _remote(src, dst, send_sems, recv_sems, k, to):
    return pltpu.make_async_remote_copy(src_ref=src, dst_ref=dst, send_sem=send_sems.at[k], recv_sem=recv_sems.at[k],
                                        device_id=to, device_id_type=MESH)


def gather_shards(pack):
    def body(src_ref, out_ref, token, send_sems, recv_sems):
        x, y, c = _coords()
        me = 2 * x + y
        sib = (x, y, 1 - c)
        chips = _other_chips(x, y)
        first = [_remote(src_ref.at[c], out_ref.at[me, c], send_sems, recv_sems, k, (cx, cy, c))
                 for k, (cx, cy) in enumerate(chips)]
        for cp in first:
            cp.start()
        passed = []
        for k, (cx, cy) in enumerate(chips):
            j = 2 * cx + cy
            _remote(src_ref.at[c], out_ref.at[j, c], send_sems, recv_sems, k, (cx, cy, c)).wait_recv()
            fwd = _remote(out_ref.at[j, c], out_ref.at[j, c], send_sems, recv_sems, 3 + k, sib)
            fwd.start()
            passed.append(fwd)
        for k, (cx, cy) in enumerate(chips):
            j = 2 * cx + cy
            _remote(out_ref.at[j, 1 - c], out_ref.at[j, 1 - c], send_sems, recv_sems, 3 + k, sib).wait_recv()
        for cp in first + passed:
            cp.wait_send()
        token[...] = jnp.zeros_like(token)

    return pl.pallas_call(
        body,
        name="gather_shards",
        in_specs=[_ANY],
        out_specs=[_ANY, pl.BlockSpec(memory_space=pltpu.VMEM)],
        out_shape=[jax.ShapeDtypeStruct((N_CHIPS,) + pack.shape, pack.dtype), jax.ShapeDtypeStruct((8, 128), F32)],
        scratch_shapes=[pltpu.SemaphoreType.DMA((6,)), pltpu.SemaphoreType.DMA((6,))],
    )(pack)


def _gather_copies(pack_ref, all_ref, send_sems, recv_sems):
    x, y, c = _coords()
    me = 2 * x + y
    return [(_remote(pack_ref.at[c], all_ref.at[me, c], send_sems, recv_sems, k, (cx, cy, c)),
             _remote(pack_ref.at[c], all_ref.at[2 * cx + cy, c], send_sems, recv_sems, k, (cx, cy, c)))
            for k, (cx, cy) in enumerate(_other_chips(x, y))]


_HBM = pl.BlockSpec(memory_space=pltpu.HBM)
_SEM = pl.BlockSpec(memory_space=pltpu.SEMAPHORE)
_SIDE_EFFECT = pltpu.SideEffectType.DATAFLOW_SIDE_EFFECTING


def split_start(name, copies, n, src, land_shape, after=None):
    def body(src_ref, land_ref, *refs):
        send_sems, recv_sems, token = refs[-5], refs[-4], refs[-1]
        for send, _ in copies(src_ref, land_ref, send_sems, recv_sems):
            send.start()
        token[...] = jnp.zeros_like(token)

    extra = () if after is None else (after,)
    *state, token = pl.pallas_call(
        body,
        name=name,
        out_shape=(pltpu.SemaphoreType.DMA((n,)), pltpu.SemaphoreType.DMA((n,)), pltpu.HBM(src.shape, src.dtype),
                   pltpu.HBM(land_shape, src.dtype), jax.ShapeDtypeStruct((8, 128), F32)),
        in_specs=(_HBM, _HBM) + (pl.BlockSpec(memory_space=pl.ANY),) * len(extra),
        out_specs=(_SEM, _SEM, _HBM, _HBM, pl.BlockSpec(memory_space=pltpu.VMEM)),
        input_output_aliases={0: 2, 1: 3},
        compiler_params=pltpu.CompilerParams(has_side_effects=_SIDE_EFFECT),
    )(pltpu.with_memory_space_constraint(src, pltpu.HBM),
      pltpu.with_memory_space_constraint(lax.empty(land_shape, src.dtype), pltpu.HBM), *extra)
    return state, token


def split_wait(name, copies, state, after):
    send_sems, recv_sems, src, land = state

    def body(src_ref, land_ref, send_sems, recv_sems, after_ref, src_out, land_out):
        for send, arrival in copies(src_ref, land_ref, send_sems, recv_sems):
            send.wait_send()
            arrival.wait_recv()

    return pl.pallas_call(
        body,
        name=name,
        out_shape=(pltpu.HBM(src.shape, src.dtype), pltpu.HBM(land.shape, land.dtype)),
        in_specs=(_HBM, _HBM, _SEM, _SEM, pl.BlockSpec(memory_space=pl.ANY)),
        out_specs=(_HBM, _HBM),
        input_output_aliases={0: 0, 1: 1},
        compiler_params=pltpu.CompilerParams(has_side_effects=_SIDE_EFFECT),
    )(src, land, send_sems, recv_sems, after)


def gather_forward(got):
    def body(got_ref, out_ref, send_sems, recv_sems):
        x, y, c = _coords()
        sib = (x, y, 1 - c)
        slots = [2 * cx + cy for cx, cy in _other_chips(x, y)]
        sends = [_remote(got_ref.at[j, c], out_ref.at[j, c], send_sems, recv_sems, k, sib) for k, j in enumerate(slots)]
        for cp in sends:
            cp.start()
        for k, j in enumerate(slots):
            _remote(got_ref.at[j, 1 - c], out_ref.at[j, 1 - c], send_sems, recv_sems, k, sib).wait_recv()
        for cp in sends:
            cp.wait_send()

    return pl.pallas_call(
        body,
        name="gather_forward",
        in_specs=[_ANY],
        out_specs=_ANY,
        out_shape=jax.ShapeDtypeStruct(got.shape, got.dtype),
        input_output_aliases={0: 0},
        scratch_shapes=[pltpu.SemaphoreType.DMA((3,)), pltpu.SemaphoreType.DMA((3,))],
    )(got)


def pair_sum(g, got, tag, *, tm):
    n, _, rows, width = g.shape

    def body(c_ref, own_ref, got_ref, out_ref, out16_ref):
        total = own_ref[0, 0] + got_ref[0]
        out_ref[0] = total
        out16_ref[0] = total.astype(BF16)

    blk = pl.BlockSpec((1, tm, width), lambda j, i, c_ref: (j, i, 0))
    return pl.pallas_call(
        body,
        name="pair_sum_" + tag,
        grid_spec=pltpu.PrefetchScalarGridSpec(
            num_scalar_prefetch=1,
            grid=(n, rows // tm),
            in_specs=[pl.BlockSpec((1, 1, tm, width), lambda j, i, c_ref: (j, c_ref[0], i, 0)), blk],
            out_specs=[blk, blk],
        ),
        out_shape=[jax.ShapeDtypeStruct(got.shape, F32), jax.ShapeDtypeStruct(got.shape, BF16)],
        compiler_params=_cparams(2),
    )(lax.axis_index("c").reshape(1).astype(jnp.int32), g, got)


def _pair_copies(g_ref, got_ref, send_sems, recv_sems):
    x, y, c = _coords()
    copies = [_remote(g_ref.at[j, 1 - c], got_ref.at[j], send_sems, recv_sems, j, (x, y, 1 - c))
              for j in range(N_CHIPS)]
    return [(cp, cp) for cp in copies]


def _chip_copies(p_ref, slots_ref, send_sems, recv_sems):
    x, y, c = _coords()
    me = 2 * x + y
    return [(_remote(p_ref.at[2 * cx + cy], slots_ref.at[me], send_sems, recv_sems, k, (cx, cy, c)),
             _remote(p_ref.at[me], slots_ref.at[2 * cx + cy], send_sems, recv_sems, k, (cx, cy, c)))
            for k, (cx, cy) in enumerate(_other_chips(x, y))]


def sum_with_own(own, slots, mine, after, *, tm, name):
    n, rows, width = slots.shape

    def body(mine_ref, own_ref, *refs):
        acc = None
        for s in range(n):
            term = jnp.where(mine_ref[0] == s, own_ref[0], refs[s][0].astype(F32))
            acc = term if acc is None else acc + term
        refs[-1][...] = acc

    return pl.pallas_call(
        body,
        name=name,
        grid_spec=pltpu.PrefetchScalarGridSpec(
            num_scalar_prefetch=1,
            grid=(rows // tm,),
            in_specs=[pl.BlockSpec((1, tm, width), lambda i, mine_ref: (mine_ref[0], i, 0))]
            + [pl.BlockSpec((1, tm, width), lambda i, mine_ref, s=s: (s, i, 0)) for s in range(n)]
            + [pl.BlockSpec(after.shape, lambda i, mine_ref: (0,) * after.ndim)],
            out_specs=pl.BlockSpec((tm, width), lambda i, mine_ref: (i, 0)),
        ),
        out_shape=jax.ShapeDtypeStruct((rows, width), F32),
        compiler_params=_cparams(1),
    )(mine.reshape(1).astype(jnp.int32), own, *([slots] * n), after)


def exchange_halves(s, tag):
    rq = PACK_TILE
    nq = s.shape[0] // rq

    def body(s_ref, out_ref, sbuf, rbuf, send_sems, recv_sems, in_sems, out_sems):
        x, y, c = _coords()
        sib = (x, y, 1 - c)
        rows = lambda q: pl.ds(q * rq, rq)
        loads = [pltpu.make_async_copy(s_ref.at[rows(q)], sbuf.at[rows(q)], in_sems.at[q]) for q in range(nq)]
        for cp in loads:
            cp.start()
        sends = []
        for q in range(nq):
            loads[q].wait()
            sends.append(_remote(sbuf.at[rows(q)], rbuf.at[rows(q)], send_sems, recv_sems, q, sib))
            sends[q].start()
        stores = []
        for q in range(nq):
            sends[q].wait_recv()
            stores.append(pltpu.make_async_copy(rbuf.at[rows(q)], out_ref.at[rows(q)], out_sems.at[q]))
            stores[q].start()
        for cp in sends:
            cp.wait_send()
        for cp in stores:
            cp.wait()

    return pl.pallas_call(
        body,
        name="exchange_halves_" + tag,
        in_specs=[_ANY],
        out_specs=_ANY,
        out_shape=jax.ShapeDtypeStruct(s.shape, s.dtype),
        scratch_shapes=[pltpu.VMEM(s.shape, s.dtype), pltpu.VMEM(s.shape, s.dtype)]
        + [pltpu.SemaphoreType.DMA((nq,))] * 4,
        compiler_params=pltpu.CompilerParams(vmem_limit_bytes=VMEM_LIMIT),
    )(s)


def sum_all(s, after):
    rows = s.shape[0]
    half = rows // 2

    def body(s_ref, after_ref, out_ref, theirs, pair, slots, send_sems, recv_sems):
        x, y, c = _coords()
        me = 2 * x + y
        sib = (x, y, 1 - c)
        chips = _other_chips(x, y)
        swap = _remote(s_ref, theirs, send_sems, recv_sems, 0, sib)
        swap.start()
        swap.wait_recv()
        pair[...] = s_ref[...] + theirs[...]
        mine = pl.ds(pl.multiple_of(c * half, 8), half)
        other = pl.ds(pl.multiple_of((1 - c) * half, 8), half)
        sends = [_remote(pair.at[mine], slots.at[me], send_sems, recv_sems, 1 + k, (cx, cy, c))
                 for k, (cx, cy) in enumerate(chips)]
        for cp in sends:
            cp.start()
        for k, (cx, cy) in enumerate(chips):
            _remote(pair.at[mine], slots.at[2 * cx + cy], send_sems, recv_sems, 1 + k, (cx, cy, c)).wait_recv()
        slots[me] = pair[mine]
        out_ref[mine] = ((slots[0] + slots[1]) + slots[2]) + slots[3]
        last = _remote(out_ref.at[mine], out_ref.at[mine], send_sems, recv_sems, 4, sib)
        last.start()
        _remote(out_ref.at[other], out_ref.at[other], send_sems, recv_sems, 4, sib).wait_recv()
        for cp in [swap] + sends + [last]:
            cp.wait_send()

    vmem = pl.BlockSpec(memory_space=pltpu.VMEM)
    return pl.pallas_call(
        body,
        name="sum_all",
        in_specs=[vmem, vmem],
        out_specs=vmem,
        out_shape=jax.ShapeDtypeStruct(s.shape, s.dtype),
        scratch_shapes=[pltpu.VMEM(s.shape, s.dtype), pltpu.VMEM(s.shape, s.dtype),
                        pltpu.VMEM((N_CHIPS, half, s.shape[1]), s.dtype), pltpu.SemaphoreType.DMA((5,)),
                        pltpu.SemaphoreType.DMA((5,))],
        compiler_params=pltpu.CompilerParams(vmem_limit_bytes=VMEM_LIMIT),
    )(s, after)


ADAM_LR = 0.001
ADAM_B1 = 0.9
ADAM_B2 = 0.999
ADAM_EPS = 1e-08
ADAM_WD = 0.01
ADAM_STEP = 10


def f_adamw(g, w, m, v):
    m = ADAM_B1 * m + (1.0 - ADAM_B1) * g
    v = ADAM_B2 * v + (1.0 - ADAM_B2) * jnp.square(g)
    m_hat = m / (1.0 - ADAM_B1 ** ADAM_STEP)
    v_hat = v / (1.0 - ADAM_B2 ** ADAM_STEP)
    delta = -ADAM_LR * (m_hat / (jnp.sqrt(v_hat) + ADAM_EPS) + ADAM_WD * w)
    return delta, m, v


def adamw_many(gs, ws, ms, vs):
    n = len(gs)

    def body(*refs):
        ins, outs = refs[:4 * n], refs[4 * n:]
        for i in range(n):
            delta, nm, nv = f_adamw(ins[i][...], ins[n + i][...], ins[2 * n + i][...], ins[3 * n + i][...])
            outs[i][...] = delta
            outs[n + i][...] = nm
            outs[2 * n + i][...] = nv

    vmem = pl.BlockSpec(memory_space=pltpu.VMEM)
    res = pl.pallas_call(
        body,
        name="adamw_small",
        in_specs=[vmem] * (4 * n),
        out_specs=[vmem] * (3 * n),
        out_shape=[jax.ShapeDtypeStruct(w.shape, F32) for w in ws] * 3,
    )(*gs, *ws, *ms, *vs)
    return res[:n], res[n:2 * n], res[2 * n:]


EARLY = ["w_in", "w_lora_w", "a_lora_w", "g_lora_w"]
LATE = ["w_ffn1", "w_ffn2", "w_proj_b", "w_out", "w_proj_a"]
LORAS = ["w_lora_w", "a_lora_w", "g_lora_w"]
HALF_W = 512
PIECE_ROWS = {"w_in": 1864, "w_ffn1": 1024, "w_ffn2": 1024, "w_proj_a": 256, "w_proj_b": 256, "w_out": 256,
              "w_lora_w": 32, "a_lora_w": 32, "g_lora_w": 80}
PIECE_OFF = {"w_in": 0, "w_lora_w": 1920, "a_lora_w": 1952, "g_lora_w": 2000,
             "w_ffn1": 0, "w_ffn2": 1024, "w_proj_b": 2048, "w_out": 2304, "w_proj_a": 2560}
LO_OFF = 2080


def pack_rows(group):
    return 2304 if group is EARLY else 2816
SHARD_AXIS = {"w_in": 1, "w_proj_a": 0, "w_lora_w": 1, "a_lora_w": 1, "g_lora_w": 1, "w_proj_b": 0, "w_out": 0,
              "w_ffn1": 1, "w_ffn2": 0}
SHARD_SHAPE = {"w_in": (1024, 1864), "w_proj_a": (256, 1024), "w_lora_w": (64, 256), "a_lora_w": (64, 256),
               "g_lora_w": (160, 256), "w_proj_b": (256, 1024), "w_out": (256, 1024), "w_ffn1": (1024, 1024),
               "w_ffn2": (1024, 1024)}
SHIFT_SHARD = (2, 840)
VECTORS = ["g_mix", "sgu_ln_w", "sgu_ln_b", "w0", "a0", "k_k", "k_a", "r_k", "ln_x_w", "ln_x_b", "g_ffn", "g_final"]
SMALL = VECTORS + ["sgu_w", "sgu_b"]
SMALL_SHAPE = {**{n: (1, 1024) for n in VECTORS}, "sgu_w": (8, 128, 128), "sgu_b": (8, 128)}
WEIGHTS = ["g_mix", "w_in", "sgu_ln_w", "sgu_ln_b", "sgu_w", "sgu_b", "w_proj_a", "shift_b", "w_lora_w", "w0",
           "a_lora_w", "a0", "g_lora_w", "k_k", "k_a", "r_k", "ln_x_w", "ln_x_b", "w_proj_b", "w_out", "g_ffn",
           "w_ffn1", "w_ffn2", "g_final"]


def _size(shape):
    n = 1
    for s in shape:
        n *= s
    return n


def _pack_rows(parts, rows, dtype):
    flat = jnp.concatenate([p.reshape(-1).astype(dtype) for p in parts])
    return jnp.concatenate([flat, jnp.zeros((rows * 1024 - flat.shape[0],), dtype)]).reshape(rows, 1024)


def _unpack_rows(packed, shapes):
    flat = packed.reshape(-1)
    out, off = [], 0
    for shp in shapes:
        out.append(flat[off:off + _size(shp)].reshape(shp))
        off += _size(shp)
    return out


def _shard_of(name, full, j):
    ax = SHARD_AXIS[name]
    n = SHARD_SHAPE[name][ax]
    return lax.slice_in_dim(full, j * n, (j + 1) * n, axis=ax)


def _pad_cols(z, n):
    return jnp.concatenate([z, jnp.zeros((z.shape[0], n - z.shape[1]), z.dtype)], axis=1)


def _row_form(name, s):
    return s.T if name == "w_in" else s


def _half_piece(name, rf, h):
    if name in LORAS:
        r = PIECE_ROWS[name]
        return _pad_cols(rf[h * r:(h + 1) * r], HALF_W)
    return rf[:, HALF_W * h:HALF_W * (h + 1)]


def _pack_half(group, rf_fn, h, dtype, tail=()):
    parts, pos, rows = [], 0, pack_rows(group)
    for n in group:
        if PIECE_OFF[n] > pos:
            parts.append(jnp.zeros((PIECE_OFF[n] - pos, HALF_W), dtype))
        parts.append(_half_piece(n, rf_fn(n), h).astype(dtype))
        pos = PIECE_OFF[n] + PIECE_ROWS[n]
    for t in tail:
        parts.append(t)
        pos += t.shape[0]
    parts.append(jnp.zeros((rows - pos, HALF_W), dtype))
    return jnp.concatenate(parts, axis=0)


def _piece(pack, name):
    return pack[PIECE_OFF[name]:PIECE_OFF[name] + PIECE_ROWS[name]]


def _join_halves(name, p0, p1):
    if name in LORAS:
        return jnp.concatenate([p0[:, :SHARD_SHAPE[name][1]], p1[:, :SHARD_SHAPE[name][1]]], axis=0)
    return jnp.concatenate([p0, p1], axis=1)


def _grad_row_form(name, full, j):
    if name == "w_in":
        return full[SHARD_SHAPE[name][1] * j:SHARD_SHAPE[name][1] * (j + 1)]
    return _shard_of(name, full, j)


def adamw_weight(name, g_own, g_other, w, m, v):
    rows, width = w.shape
    if name in LORAS:
        tm = PIECE_ROWS[name]
        grid = (2, 1)
        native = pl.BlockSpec((tm, width), lambda h, i: (h, 0))
    elif name == "w_in":
        tm, lanes = rows, 128
        grid = (2, HALF_W // lanes)
        native = pl.BlockSpec((tm, lanes), lambda h, i: (0, h * (HALF_W // lanes) + i))
    else:
        tm = 128
        grid = (2, rows // tm)
        native = pl.BlockSpec((tm, HALF_W), lambda h, i: (i, h))
    off = PIECE_OFF[name] // tm
    if name == "w_in":
        packed = pl.BlockSpec((tm, 128), lambda h, i: (0, i))
    else:
        packed = pl.BlockSpec((tm, HALF_W), lambda h, i: (off + i, 0))

    def body(go_ref, gx_ref, w_ref, m_ref, v_ref, g_ref, d_ref, nm_ref, nv_ref):
        g = jnp.where(pl.program_id(0) == lax.axis_index("c"), go_ref[...], gx_ref[...])[:, :w_ref.shape[1]]
        delta, nm, nv = f_adamw(g, w_ref[...], m_ref[...], v_ref[...])
        g_ref[...] = g
        d_ref[...] = delta
        nm_ref[...] = nm
        nv_ref[...] = nv

    return pl.pallas_call(
        body,
        name="adamw_" + name,
        grid=grid,
        in_specs=[packed, packed, native, native, native],
        out_specs=[native] * 4,
        out_shape=[jax.ShapeDtypeStruct(w.shape, F32)] * 4,
        compiler_params=_cparams(2),
    )(g_own, g_other, w, m, v)


def kernel(x, g_mix, w_in, sgu_ln_w, sgu_ln_b, sgu_w, sgu_b, w_proj_a, shift_b, w_lora_w, w0, a_lora_w, a0, g_lora_w, k_k, k_a, r_k, ln_x_w, ln_x_b, w_proj_b, w_out, g_ffn, w_ffn1, w_ffn2, g_final, loss_target, m_g_mix, m_w_in, m_sgu_ln_w, m_sgu_ln_b, m_sgu_w, m_sgu_b, m_w_proj_a, m_shift_b, m_w_lora_w, m_w0, m_a_lora_w, m_a0, m_g_lora_w, m_k_k, m_k_a, m_r_k, m_ln_x_w, m_ln_x_b, m_w_proj_b, m_w_out, m_g_ffn, m_w_ffn1, m_w_ffn2, m_g_final, v_g_mix, v_w_in, v_sgu_ln_w, v_sgu_ln_b, v_sgu_w, v_sgu_b, v_w_proj_a, v_shift_b, v_w_lora_w, v_w0, v_a_lora_w, v_a0, v_g_lora_w, v_k_k, v_k_a, v_r_k, v_ln_x_w, v_ln_x_b, v_w_proj_b, v_w_out, v_g_ffn, v_w_ffn1, v_w_ffn2, v_g_final):
    given = dict(zip(WEIGHTS, (g_mix, w_in, sgu_ln_w, sgu_ln_b, sgu_w, sgu_b, w_proj_a, shift_b, w_lora_w, w0, a_lora_w, a0, g_lora_w, k_k, k_a, r_k, ln_x_w, ln_x_b, w_proj_b, w_out, g_ffn, w_ffn1, w_ffn2, g_final)))
    mom_m = dict(zip(WEIGHTS, (m_g_mix, m_w_in, m_sgu_ln_w, m_sgu_ln_b, m_sgu_w, m_sgu_b, m_w_proj_a, m_shift_b, m_w_lora_w, m_w0, m_a_lora_w, m_a0, m_g_lora_w, m_k_k, m_k_a, m_r_k, m_ln_x_w, m_ln_x_b, m_w_proj_b, m_w_out, m_g_ffn, m_w_ffn1, m_w_ffn2, m_g_final)))
    mom_v = dict(zip(WEIGHTS, (v_g_mix, v_w_in, v_sgu_ln_w, v_sgu_ln_b, v_sgu_w, v_sgu_b, v_w_proj_a, v_shift_b, v_w_lora_w, v_w0, v_a_lora_w, v_a0, v_g_lora_w, v_k_k, v_k_a, v_r_k, v_ln_x_w, v_ln_x_b, v_w_proj_b, v_w_out, v_g_ffn, v_w_ffn1, v_w_ffn2, v_g_final)))
    chip = 2 * lax.axis_index("x") + lax.axis_index("y")

    def local_block(tree, n):
        return tree[n] if n == "g_final" else tree[n][0]

    sb = local_block(given, "shift_b")
    lo_part = lambda z: (z - z.astype(BF16).astype(F32)).astype(BF16)
    row_form = lambda tree: (lambda n: _row_form(n, local_block(tree, n)))
    tile16 = lambda z: jnp.pad(z, ((0, 16 - z.shape[0]), (0, HALF_W - z.shape[1])))
    sb_tiles = [tile16(f(sb[:, lanes])) for f in (lambda z: z.astype(BF16), lo_part)
                for lanes in (slice(0, HALF_W), slice(HALF_W, None))]
    tails = [[_half_piece(n, lo_part(local_block(given, n)), h) for n in LORAS] + sb_tiles for h in range(2)]
    pack_w = jnp.stack([_pack_half(EARLY, row_form(given), h, BF16, tails[h]) for h in range(2)])
    gathered, gathered_token = gather_shards(pack_w)
    gathered = lax.dynamic_update_index_in_dim(gathered, pack_w, chip, 0)
    pack_late = jnp.stack([_pack_half(LATE, row_form(given), h, BF16) for h in range(2)])
    late_state, late_token = split_start("gather_start", _gather_copies, 3, pack_late, (N_CHIPS,) + pack_late.shape,
                                         gathered_token)

    def whole(group, got, own):
        half = lambda n, j, h: jnp.where(chip == j, _piece(own[h], n), _piece(got[j, h], n))
        shard = lambda n, j: _join_halves(n, half(n, j, 0), half(n, j, 1))
        return {n: jnp.concatenate([shard(n, j) for j in range(N_CHIPS)],
                                   axis=0 if n == "w_in" else SHARD_AXIS[n]) for n in group}

    w = whole(EARLY, gathered, pack_w)
    late_weights = lambda after: whole(
        LATE, gather_forward(split_wait("gather_wait", _gather_copies, late_state, after)[1]), pack_late)
    off = LO_OFF
    for n in LORAS:
        r, cols = PIECE_ROWS[n], SHARD_SHAPE[n][1]
        lo = jnp.concatenate([jnp.concatenate([gathered[j, 0, off:off + r, :cols], gathered[j, 1, off:off + r, :cols]],
                                              axis=0) for j in range(N_CHIPS)], axis=1)
        w[n] = w[n].astype(F32) + lo.astype(F32)
        off += r
    sb_tile = lambda j, t, lanes: gathered[j, 0, off + 16 * t:off + 16 * t + 2, :lanes].astype(F32)
    rest = SHIFT_SHARD[1] - HALF_W
    w["shift_b"] = jnp.concatenate(
        [jnp.concatenate([sb_tile(j, 0, HALF_W) + sb_tile(j, 2, HALF_W), sb_tile(j, 1, rest) + sb_tile(j, 3, rest)],
                         axis=1) for j in range(N_CHIPS)], axis=1)
    for n in SMALL:
        w[n] = local_block(given, n).reshape(SMALL_SHAPE[n])

    def pair_start(g_pack, tag):
        return split_start("reduce_pair_start_" + tag, _pair_copies, N_CHIPS, g_pack, (N_CHIPS,) + g_pack.shape[2:])

    def pair_finish(state, after, tag):
        return pair_sum(*split_wait("reduce_pair_wait_" + tag, _pair_copies, state, after), tag, tm=PACK_TILE)

    pack_early = lambda g: jnp.stack([jnp.stack([_pack_half(EARLY, lambda n: _grad_row_form(n, g[n], j), h, F32)
                                                 for h in range(2)]) for j in range(N_CHIPS)])
    loss, grad_x, grads, (late_part, late_slots), early_state = local_step(
        x[0], loss_target[0], w, late_token, late_weights, pair_start, pair_finish, pack_early)

    early_part, early_part16 = pair_finish(early_state, grad_x, "early")
    s_pack = _pack_rows([grads[n] for n in SMALL] + [grads["shift_b"], loss.reshape(1, 1)], SMALL_ROWS, F32)
    chips_state, token = split_start("reduce_chips_start", _chip_copies, 3, early_part16, early_part16.shape)
    out_g, out_d, out_m, out_v = {}, {}, {}, {}

    def finish(group, tag, part, slots):
        half_sum = sum_with_own(part, slots, chip, token, tm=PACK_TILE, name="chip_sum_" + tag)
        other_half = exchange_halves(half_sum, tag)
        for n in group:
            res = adamw_weight(n, half_sum, other_half,
                               *[_row_form(n, local_block(t, n)) for t in (given, mom_m, mom_v)])
            for tree, z in zip((out_g, out_d, out_m, out_v), res):
                tree[n] = _row_form(n, z)

    finish(LATE, "late", late_part, late_slots)

    small_shapes = [SMALL_SHAPE[n] for n in SMALL]
    g_small = sum_all(s_pack, token)
    *g_parts, loss = _unpack_rows(g_small, small_shapes + [(2, N_RWKV), ()])
    out_g.update(zip(SMALL, g_parts[:-1]))
    g_sb = lax.dynamic_slice_in_dim(g_parts[-1], chip * SHIFT_SHARD[1], SHIFT_SHARD[1], axis=1)
    out_g["shift_b"] = g_sb
    names = SMALL + ["shift_b"]
    native = lambda tree: [local_block(tree, n).reshape(SMALL_SHAPE.get(n, SHIFT_SHARD)) for n in names]
    small_res = adamw_many(g_parts[:-1] + [g_sb], native(given), native(mom_m), native(mom_v))
    for tree, res in zip((out_d, out_m, out_v), small_res):
        tree.update(zip(names, res))

    after = (out_v["w_out"], out_v["sgu_w"])
    early_slots = split_wait("reduce_chips_wait", _chip_copies, chips_state,
                             jnp.concatenate([z.reshape(-1)[:8] for z in after]))[1]
    finish(EARLY, "early", early_part, early_slots)

    def block_of(tree, n):
        return tree[n].reshape(given[n].shape)

    return (loss, grad_x[None], *[block_of(out_g, n) for n in WEIGHTS], *[block_of(out_d, n) for n in WEIGHTS],
            *[block_of(out_m, n) for n in WEIGHTS], *[block_of(out_v, n) for n in WEIGHTS])
```

```python
import functools

import jax
import jax.numpy as jnp
from jax import lax
from jax.experimental import pallas as pl
from jax.experimental.pallas import tpu as pltpu

F32 = jnp.float32
BF16 = jnp.bfloat16

D_MODEL = 1024
N_HEADS = 16
HEAD = 64
SCAN_CHUNK = 64

VMEM_LIMIT = 56 * 1024 * 1024


_BDIMS = {
    "nn": (((2,), (1,)), ((0,), (0,))),
    "nt": (((2,), (2,)), ((0,), (0,))),
    "tn": (((1,), (1,)), ((0,), (0,))),
}


def _raw_bdot(x, y, mode, fine):
    if fine:
        return lax.dot_general(x, y, _BDIMS[mode], precision=lax.Precision.HIGH, preferred_element_type=F32)
    return lax.dot_general(x.astype(BF16), y.astype(BF16), _BDIMS[mode], preferred_element_type=F32)


@functools.partial(jax.custom_vjp, nondiff_argnums=(2, 3))
def bdot(x, y, mode, fine=True):
    return _raw_bdot(x, y, mode, fine)


def _bdot_fwd(x, y, mode, fine):
    return _raw_bdot(x, y, mode, fine), (x, y)


def _bdot_bwd(mode, fine, res, g):
    x, y = res
    if mode == "nn":
        return bdot(g, y, "nt", fine), bdot(x, g, "tn", fine)
    if mode == "nt":
        return bdot(g, y, "nn", fine), bdot(g, x, "tn", fine)
    return bdot(y, g, "nt", fine), bdot(x, g, "nn", fine)


bdot.defvjp(_bdot_fwd, _bdot_bwd)


def _scan_chunk(S0, r, lw, k, v, a, b):
    nh, lc, _ = r.shape
    ti = lax.broadcasted_iota(jnp.int32, (lc, lc), 0)
    si = lax.broadcasted_iota(jnp.int32, (lc, lc), 1)
    incl = (si <= ti).astype(F32)
    strict = (si < ti).astype(F32)
    eye = (si == ti).astype(F32)
    cl = bdot(jnp.broadcast_to(incl, (nh, lc, lc)), lw, "nn")
    cl_last = cl[:, lc - 1:lc, :]
    g_last = jnp.exp(cl_last - cl)
    at = a * jnp.exp(cl - lw)
    bt = b * jnp.exp(-cl)
    kt = k * jnp.exp(-cl)
    rt = r * jnp.exp(cl)
    ar = jnp.concatenate([at, rt], axis=1)
    ar_b = bdot(ar, bt, "nt", False)
    ar_k = bdot(ar, kt, "nt", False)
    m_ab, m_rb = ar_b[:, :lc] * strict, ar_b[:, lc:] * incl
    m_ak, m_rk = ar_k[:, :lc] * strict, ar_k[:, lc:] * incl
    x = eye + m_ab
    p = bdot(m_ab, m_ab, "nn", False)
    n = 2
    while n * 2 < lc:
        px = bdot(jnp.concatenate([p, x], axis=1), p, "nn", False)
        p = px[:, :lc]
        x = x + px[:, lc:]
        n *= 2
    x = x + bdot(x, p, "nn", False)
    ar_s = bdot(ar, S0, "nt", False)
    akrk_v = bdot(jnp.concatenate([m_ak, m_rk], axis=1), v, "nn", False)
    u = bdot(x, ar_s[:, :lc] + akrk_v[:, :lc], "nn", False)
    o = ar_s[:, lc:] + bdot(m_rb, u, "nn", False) + akrk_v[:, lc:]
    s_last = S0 * jnp.exp(cl_last) + bdot(jnp.concatenate([u, v], axis=1),
                                          jnp.concatenate([b * g_last, k * g_last], axis=1), "tn", False)
    return o, s_last


def _split_heads(z):
    return jnp.stack([z[:, HEAD * h:HEAD * (h + 1)] for h in range(N_HEADS)], axis=0)


def _merge_heads(z):
    return jnp.concatenate([z[h] for h in range(N_HEADS)], axis=1)


def _scan_specs(t, ops, rev):
    nc = t // SCAN_CHUNK
    row = (lambda c: nc - 1 - c) if rev else (lambda c: c)
    specs = [pl.BlockSpec((SCAN_CHUNK, D_MODEL), lambda c, cb=cb: (row(c), cb)) for _, cb in ops]
    state = pl.BlockSpec((1, N_HEADS, HEAD, HEAD), lambda c: (row(c), 0, 0, 0))
    return nc, specs, state


def scan_fwd(ops):
    t = ops[0][0].shape[0]
    nc, specs, state = _scan_specs(t, ops, False)

    def body(r_ref, lw_ref, k_ref, v_ref, a_ref, b_ref, o_ref, s0_ref, s_scr):
        @pl.when(pl.program_id(0) == 0)
        def _():
            s_scr[...] = jnp.zeros_like(s_scr)

        s0 = s_scr[...]
        s0_ref[0] = s0
        o, s_last = _scan_chunk(s0, *[_split_heads(z[...]) for z in (r_ref, lw_ref, k_ref, v_ref, a_ref, b_ref)])
        o_ref[...] = _merge_heads(o)
        s_scr[...] = s_last

    return pl.pallas_call(
        body,
        name="scan_fwd",
        grid=(nc,),
        in_specs=specs,
        out_specs=[pl.BlockSpec((SCAN_CHUNK, D_MODEL), lambda c: (c, 0)), state],
        out_shape=[jax.ShapeDtypeStruct((t, D_MODEL), F32), jax.ShapeDtypeStruct((nc, N_HEADS, HEAD, HEAD), F32)],
        scratch_shapes=[pltpu.VMEM((N_HEADS, HEAD, HEAD), F32)],
        compiler_params=_cparams(1),
    )(*[a for a, _ in ops])


def scan_bwd(ops, s0s, do, part):
    t = ops[0][0].shape[0]
    nc, specs, state = _scan_specs(t, ops + [(do, 0)], True)

    def body(r_ref, lw_ref, k_ref, v_ref, a_ref, b_ref, do_ref, s0_ref, part_ref, *rest):
        out_refs, slots_ref, ds_scr, send_sems, recv_sems = rest[:6], rest[6], rest[7], rest[8], rest[9]
        step = pl.program_id(0)
        x, y, c = _coords()
        me = 2 * x + y
        chips = _other_chips(x, y)
        sends = [_remote(part_ref.at[2 * cx + cy], slots_ref.at[me], send_sems, recv_sems, k, (cx, cy, c))
                 for k, (cx, cy) in enumerate(chips)]

        @pl.when(step == 0)
        def _():
            ds_scr[...] = jnp.zeros_like(ds_scr)
            for cp in sends:
                cp.start()

        _, vjp = jax.vjp(_scan_chunk, s0_ref[0],
                         *[_split_heads(z[...]) for z in (r_ref, lw_ref, k_ref, v_ref, a_ref, b_ref)])
        grads = vjp((_split_heads(do_ref[...]), ds_scr[...]))
        for o_ref, g in zip(out_refs, grads[1:]):
            o_ref[...] = _merge_heads(g)
        ds_scr[...] = grads[0]

        @pl.when(step == nc - 1)
        def _():
            for k, (cx, cy) in enumerate(chips):
                _remote(part_ref.at[me], slots_ref.at[2 * cx + cy], send_sems, recv_sems, k, (cx, cy, c)).wait_recv()
            for cp in sends:
                cp.wait_send()

    return pl.pallas_call(
        body,
        name="scan_bwd",
        grid=(nc,),
        in_specs=specs + [state, _ANY],
        out_specs=[pl.BlockSpec((SCAN_CHUNK, D_MODEL), lambda c: (nc - 1 - c, 0))] * 6 + [_ANY],
        out_shape=[jax.ShapeDtypeStruct((t, D_MODEL), F32)] * 6 + [jax.ShapeDtypeStruct(part.shape, part.dtype)],
        scratch_shapes=[pltpu.VMEM((N_HEADS, HEAD, HEAD), F32), pltpu.SemaphoreType.DMA((3,)),
                        pltpu.SemaphoreType.DMA((3,))],
        compiler_params=_cparams(1),
    )(*[a for a, _ in ops], do, s0s, part)


_MDIMS = {
    "nn": (((1,), (0,)), ((), ())),
    "nt": (((1,), (1,)), ((), ())),
    "tn": (((0,), (0,)), ((), ())),
}


def _raw_mdot(x, y, mode, exact):
    if exact:
        return lax.dot_general(x, y, _MDIMS[mode], precision=lax.Precision.HIGH, preferred_element_type=F32)
    return lax.dot_general(x.astype(BF16), y.astype(BF16), _MDIMS[mode], preferred_element_type=F32)


@functools.partial(jax.custom_vjp, nondiff_argnums=(2, 3))
def mdot(x, y, mode, exact):
    return _raw_mdot(x, y, mode, exact)


def _mdot_fwd(x, y, mode, exact):
    return _raw_mdot(x, y, mode, exact), (x, y)


def _mdot_bwd(mode, exact, res, g):
    x, y = res
    if mode == "nn":
        return mdot(g, y, "nt", exact), mdot(x, g, "tn", exact)
    if mode == "nt":
        return mdot(g, y, "nn", exact), mdot(g, x, "tn", exact)
    return mdot(y, g, "nt", exact), mdot(x, g, "nn", exact)


mdot.defvjp(_mdot_fwd, _mdot_bwd)


def _seg_ones():
    i = lax.broadcasted_iota(jnp.int32, (256, 256), 0) // HEAD
    j = lax.broadcasted_iota(jnp.int32, (256, 256), 1) // HEAD
    return (i == j).astype(BF16)


@jax.custom_vjp
def segsum(x):
    bd = _seg_ones()
    hi = x.astype(BF16)
    lo = (x - hi.astype(F32)).astype(BF16)
    cols = []
    for j in range(x.shape[1] // 256):
        sl = slice(256 * j, 256 * (j + 1))
        cols.append(jnp.dot(hi[:, sl], bd, preferred_element_type=F32)
                    + jnp.dot(lo[:, sl], bd, preferred_element_type=F32))
    return jnp.concatenate(cols, axis=1)


segsum.defvjp(lambda x: (segsum(x), None), lambda _, g: (segsum(g),))


NORM_EPS = 1e-6
LN_EPS = 1e-5
GN_EPS = 64e-5
SGU_CHUNK = 128
SGU_GROUPS = 8


def _rms(x, g):
    return x * lax.rsqrt(jnp.mean(x * x, axis=-1, keepdims=True) + NORM_EPS) * g


def f_norm_in(x, g):
    return _rms(x, g), x


def f_sgu(p, ln_w, ln_b, sw, sbt):
    tm = p.shape[0]
    z = 0.5 * p * (1.0 + lax.erf(p * 0.7071067811865476))
    u, v = z[:, :D_MODEL], z[:, D_MODEL:]
    mu = jnp.mean(v, axis=-1, keepdims=True)
    d = v - mu
    vn = d * lax.rsqrt(jnp.mean(d * d, axis=-1, keepdims=True) + LN_EPS) * ln_w + ln_b
    ii = lax.broadcasted_iota(jnp.int32, (SGU_CHUNK, SGU_CHUNK), 0)
    jj = lax.broadcasted_iota(jnp.int32, (SGU_CHUNK, SGU_CHUNK), 1)
    mask = (jj <= ii).astype(F32)
    gi = lax.broadcasted_iota(jnp.int32, (SGU_GROUPS, D_MODEL), 0)
    ci = lax.broadcasted_iota(jnp.int32, (SGU_GROUPS, D_MODEL), 1) // SGU_CHUNK
    bias = mdot(sbt, (gi == ci).astype(F32), "nn", True)
    rows = []
    for c in range(tm // SGU_CHUNK):
        cols = []
        for g in range(SGU_GROUPS):
            blk = vn[c * SGU_CHUNK:(c + 1) * SGU_CHUNK, g * SGU_CHUNK:(g + 1) * SGU_CHUNK]
            cols.append(mdot(sw[g] * mask, blk, "nn", False))
        rows.append(jnp.concatenate(cols, axis=1) + bias)
    return (u * jnp.concatenate(rows, axis=0),)


def _softplus(x):
    return jnp.maximum(x, 0.0) + jnp.log1p(jnp.exp(-jnp.abs(x)))


def f_pre(q, wl, w0, al, a0, gl, k_k, k_a):
    qr, qk, qv, ql = q[:, :1024], q[:, 1024:2048], q[:, 2048:3072], q[:, 3072:]
    return _f_pre(qr, qk, qv, ql, wl, w0, al, a0, gl, k_k, k_a)


def _f_pre(qr, qk, qv, ql, wl, w0, al, a0, gl, k_k, k_a):
    xw, xa, xg = ql[:, :128], ql[:, 128:256], ql[:, 256:512]
    wr = -_softplus(-(w0 + mdot(jnp.tanh(xw), wl, "nn", False))) - 0.5
    lw = -jnp.exp(wr)
    aa = jax.nn.sigmoid(a0 + mdot(xa, al, "nn", False))
    g = mdot(jax.nn.sigmoid(xg), gl, "nn", False)
    kkr = qk * k_k
    kk = kkr / jnp.maximum(jnp.sqrt(segsum(kkr * kkr)), 1e-12)
    kp = qk * (1.0 + (aa - 1.0) * k_a)
    return qr, lw, kp, qv, -kk, kk * aa, g, qr, kp, qv


def f_post(o, r, kp, v, g, lnw, lnb, rk):
    mu = segsum(o) * (1.0 / HEAD)
    d = o - mu
    gn = d * lax.rsqrt(segsum(d * d) * (1.0 / HEAD) + GN_EPS)
    return ((gn * lnw + lnb + segsum(r * kp * rk) * v) * g,)


def f_mix(ya, yb, ga, gb):
    return (jax.nn.sigmoid(ga) * ya + jax.nn.sigmoid(gb) * yb,)


def f_ffn_in(h1, g):
    return _rms(h1, g), h1


def f_final(h1, m3, tgt, g):
    y = _rms(h1 + m3, g)
    err = jnp.square(y - tgt)
    return 0.5 * jnp.sum(jnp.mean(err, axis=-1))


def _cparams(n_grid):
    return pltpu.CompilerParams(dimension_semantics=("arbitrary",) * n_grid, vmem_limit_bytes=VMEM_LIMIT)


def _tile_spec(tm, w, cb):
    return pl.BlockSpec((tm, w), lambda i: (i, cb))


def _const_spec(c):
    nd = c.ndim
    return pl.BlockSpec(c.shape, lambda i: (0,) * nd)


def ew_call(fn, tiled, consts, outs, *, tm, name):
    t = tiled[0][0].shape[0]
    n_t, n_c = len(tiled), len(consts)

    def body(*refs):
        tv = [r[...].astype(F32) for r in refs[:n_t]]
        cv = [r[...] for r in refs[n_t:n_t + n_c]]
        res = fn(*tv, *cv)
        for o_ref, val in zip(refs[n_t + n_c:], res):
            o_ref[...] = val.astype(o_ref.dtype)

    return pl.pallas_call(
        body,
        name=name,
        grid=(t // tm,),
        in_specs=[_tile_spec(tm, w, cb) for _, w, cb in tiled] + [_const_spec(c) for c in consts],
        out_specs=[_tile_spec(tm, w, 0) for w, _ in outs],
        out_shape=[jax.ShapeDtypeStruct((t, w), dt) for w, dt in outs],
        compiler_params=_cparams(1),
    )(*[a for a, _, _ in tiled], *consts)


def ew_vjp_call(fn, tiled, consts, cots, d_tiled, d_consts, *, tm, name):
    t = tiled[0][0].shape[0]
    n_t, n_c, n_g = len(tiled), len(consts), len(cots)
    dt_list = [(i, dt) for i, dts in enumerate(d_tiled) for dt in dts]
    dc_list = [i for i, want in enumerate(d_consts) if want]

    def body(*refs):
        tv = [r[...].astype(F32) for r in refs[:n_t]]
        cv = [r[...] for r in refs[n_t:n_t + n_c]]
        gv = tuple(r[...].astype(F32) for r in refs[n_t + n_c:n_t + n_c + n_g])
        out_refs = refs[n_t + n_c + n_g:]
        _, vjp = jax.vjp(fn, *tv, *cv)
        grads = vjp(gv)
        for o_ref, (i, _) in zip(out_refs, dt_list):
            o_ref[...] = grads[i].astype(o_ref.dtype)
        acc_refs = out_refs[len(dt_list):]

        @pl.when(pl.program_id(0) == 0)
        def _():
            for a_ref in acc_refs:
                a_ref[...] = jnp.zeros_like(a_ref)

        for a_ref, i in zip(acc_refs, dc_list):
            a_ref[...] += grads[n_t + i]

    res = pl.pallas_call(
        body,
        name=name,
        grid=(t // tm,),
        in_specs=[_tile_spec(tm, w, cb) for _, w, cb in tiled] + [_const_spec(c) for c in consts]
        + [_tile_spec(tm, w, cb) for _, w, cb in cots],
        out_specs=[_tile_spec(tm, tiled[i][1], 0) for i, _ in dt_list] + [_const_spec(consts[i]) for i in dc_list],
        out_shape=[jax.ShapeDtypeStruct((t, tiled[i][1]), dt) for i, dt in dt_list]
        + [jax.ShapeDtypeStruct(consts[i].shape, F32) for i in dc_list],
        compiler_params=_cparams(1),
    )(*[a for a, _, _ in tiled], *consts, *[a for a, _, _ in cots])
    return res[:len(dt_list)], res[len(dt_list):]


def mm(a, b, mode, *, tm, tn, name, out_dtypes=(F32,), epi=None, extras=(), into=None):
    m = a.shape[1] if mode == "tn" else a.shape[0]
    kd = a.shape[0] if mode == "tn" else a.shape[1]
    n = b.shape[0] if mode == "nt" else b.shape[1]
    tm, tn = min(tm, m), min(tn, n)
    if mode == "nn":
        a_spec = pl.BlockSpec((tm, kd), lambda i, j: (i, 0))
        b_spec = pl.BlockSpec((kd, tn), lambda i, j: (0, j))
    elif mode == "nt":
        a_spec = pl.BlockSpec((tm, kd), lambda i, j: (i, 0))
        b_spec = pl.BlockSpec((tn, kd), lambda i, j: (j, 0))
    else:
        a_spec = pl.BlockSpec((kd, tm), lambda i, j: (0, i))
        b_spec = pl.BlockSpec((kd, tn), lambda i, j: (0, j))
    n_e = len(extras)
    o_spec = pl.BlockSpec((tm, tn), lambda i, j: (i, j))

    if into is not None:
        buf, place = into

        def body_into(a_ref, b_ref, buf_ref, o_ref):
            o_ref[0, 0] = lax.dot_general(a_ref[...].astype(BF16), b_ref[...].astype(BF16), _MDIMS[mode],
                                          preferred_element_type=F32)

        return pl.pallas_call(
            body_into,
            name=name,
            grid=(m // tm, n // tn),
            in_specs=[a_spec, b_spec, pl.BlockSpec(memory_space=pl.ANY)],
            out_specs=pl.BlockSpec((1, 1, tm, tn), lambda i, j: (*place(i, j), 0)),
            out_shape=jax.ShapeDtypeStruct(buf.shape, F32),
            input_output_aliases={2: 0},
            compiler_params=_cparams(2),
        )(a, b, buf)

    def body(a_ref, b_ref, *refs):
        c = lax.dot_general(a_ref[...].astype(BF16), b_ref[...].astype(BF16), _MDIMS[mode],
                            preferred_element_type=F32)
        res = epi(c, *[r[...] for r in refs[:n_e]]) if epi is not None else (c,)
        for o_ref, val in zip(refs[n_e:], res):
            o_ref[...] = val.astype(o_ref.dtype)

    res = pl.pallas_call(
        body,
        name=name,
        grid=(m // tm, n // tn),
        in_specs=[a_spec, b_spec] + [o_spec] * n_e,
        out_specs=[o_spec] * len(out_dtypes),
        out_shape=[jax.ShapeDtypeStruct((m, n), dt) for dt in out_dtypes],
        compiler_params=_cparams(2),
    )(a, b, *extras)
    return res if len(out_dtypes) > 1 else res[0]


RWKV_COL0 = 4096
RWKV_WIDTH = 3584
SHIFT_BLK = 512


def _shift_down(p, prev_row):
    rows = lax.broadcasted_iota(jnp.int32, p.shape, 0)
    return jnp.where(rows == 0, prev_row, pltpu.roll(p, 1, 0))


def shiftmix_fwd(p_all, sbp, *, tm):
    t = p_all.shape[0]
    tm = min(tm, t)
    c0 = RWKV_COL0 // SHIFT_BLK
    hb = tm // 8

    def body(p_ref, halo_ref, sb_ref, q_ref):
        p = p_ref[...]
        prev = jnp.where(pl.program_id(0) == 0, 0.0, halo_ref[7:8, :])
        q_ref[...] = p * sb_ref[0:1, :] + _shift_down(p, prev) * sb_ref[1:2, :]

    return pl.pallas_call(
        body,
        name="shiftmix_fwd",
        grid=(t // tm, RWKV_WIDTH // SHIFT_BLK),
        in_specs=[
            pl.BlockSpec((tm, SHIFT_BLK), lambda i, j: (i, c0 + j)),
            pl.BlockSpec((8, SHIFT_BLK), lambda i, j: (jnp.maximum(i * hb - 1, 0), c0 + j)),
            pl.BlockSpec((2, SHIFT_BLK), lambda i, j: (0, j)),
        ],
        out_specs=pl.BlockSpec((tm, SHIFT_BLK), lambda i, j: (i, j)),
        out_shape=jax.ShapeDtypeStruct((t, RWKV_WIDTH), F32),
        compiler_params=_cparams(2),
    )(p_all, p_all, sbp)


def shiftmix_bwd(dq, col0, p_all, sbp, *, tm, name):
    t, w = dq.shape
    n_i = t // tm
    hb = tm // 8
    cq = col0 // SHIFT_BLK
    cp = (RWKV_COL0 + col0) // SHIFT_BLK

    def body(dq_ref, dqn_ref, p_ref, ph_ref, sb_ref, dp_ref, dsb_ref):
        i = pl.program_id(1)
        dq_t = dq_ref[...]
        rows = lax.broadcasted_iota(jnp.int32, dq_t.shape, 0)
        nxt = jnp.where(i == n_i - 1, 0.0, dqn_ref[0:1, :])
        up = jnp.where(rows == tm - 1, nxt, pltpu.roll(dq_t, tm - 1, 0))
        dp_ref[...] = (dq_t * sb_ref[0:1, :] + up * sb_ref[1:2, :]).astype(dp_ref.dtype)
        p = p_ref[...]
        prev = jnp.where(i == 0, 0.0, ph_ref[7:8, :])
        s0 = jnp.sum(dq_t * p, axis=0, keepdims=True)
        s1 = jnp.sum(dq_t * _shift_down(p, prev), axis=0, keepdims=True)
        two = lax.broadcasted_iota(jnp.int32, (2, SHIFT_BLK), 0)

        @pl.when(i == 0)
        def _():
            dsb_ref[...] = jnp.zeros_like(dsb_ref)

        dsb_ref[...] += jnp.where(two == 0, s0, s1)

    return pl.pallas_call(
        body,
        name=name,
        grid=(w // SHIFT_BLK, n_i),
        in_specs=[
            pl.BlockSpec((tm, SHIFT_BLK), lambda j, i: (i, j)),
            pl.BlockSpec((8, SHIFT_BLK), lambda j, i: (jnp.minimum((i + 1) * hb, t // 8 - 1), j)),
            pl.BlockSpec((tm, SHIFT_BLK), lambda j, i: (i, cp + j)),
            pl.BlockSpec((8, SHIFT_BLK), lambda j, i: (jnp.maximum(i * hb - 1, 0), cp + j)),
            pl.BlockSpec((2, SHIFT_BLK), lambda j, i: (0, cq + j)),
        ],
        out_specs=[
            pl.BlockSpec((tm, SHIFT_BLK), lambda j, i: (i, j)),
            pl.BlockSpec((2, SHIFT_BLK), lambda j, i: (0, j)),
        ],
        out_shape=[jax.ShapeDtypeStruct((t, w), BF16), jax.ShapeDtypeStruct((2, w), F32)],
        compiler_params=_cparams(2),
    )(dq, dq, p_all, p_all, sbp)


def final_call(h1, m3, tgt, g_final, *, tm):
    t = h1.shape[0]

    def body(h1_ref, m3_ref, tgt_ref, g_ref, dh_ref, dhb_ref, dg_ref, loss_ref):
        loss, vjp = jax.vjp(f_final, h1_ref[...], m3_ref[...], tgt_ref[...], g_ref[...])
        dh, _, _, dg = vjp(jnp.ones((), F32))
        dh_ref[...] = dh
        dhb_ref[...] = dh.astype(BF16)

        @pl.when(pl.program_id(0) == 0)
        def _():
            dg_ref[...] = jnp.zeros_like(dg_ref)
            loss_ref[...] = jnp.zeros_like(loss_ref)

        dg_ref[...] += dg
        loss_ref[...] += jnp.full(loss_ref.shape, loss, F32)

    tile = _tile_spec(tm, D_MODEL, 0)
    return pl.pallas_call(
        body,
        name="final_loss",
        grid=(t // tm,),
        in_specs=[tile, tile, tile, _const_spec(g_final)],
        out_specs=[tile, tile, _const_spec(g_final), pl.BlockSpec((8, 128), lambda i: (0, 0))],
        out_shape=[jax.ShapeDtypeStruct((t, D_MODEL), F32), jax.ShapeDtypeStruct((t, D_MODEL), BF16),
                   jax.ShapeDtypeStruct(g_final.shape, F32), jax.ShapeDtypeStruct((8, 128), F32)],
        compiler_params=_cparams(1),
    )(h1, m3, tgt, g_final)


N_SGU = 2048
N_RWKV = 3360
LORA_W, LORA_A, LORA_G = 64, 64, 160


def _pad_rwkv_cols(z):
    zero = lambda n: jnp.zeros(z.shape[:-1] + (n,), z.dtype)
    return jnp.concatenate([z[..., :3072], z[..., 3072:3136], zero(64), z[..., 3136:3200], zero(64),
                            z[..., 3200:3360], zero(96)], axis=-1)


def _unpad_rwkv_cols(z):
    return jnp.concatenate([z[..., :3072], z[..., 3072:3136], z[..., 3200:3264], z[..., 3328:3488]], axis=-1)


def _pad_win_rows(wt):
    z = wt[N_SGU:N_SGU + N_RWKV]
    zero = lambda n: jnp.zeros((n, wt.shape[1]), wt.dtype)
    return jnp.concatenate([wt[:N_SGU], wt[N_SGU + N_RWKV:], z[:3072], z[3072:3136], zero(64), z[3136:3200], zero(64),
                            z[3200:3360], zero(96)], axis=0)


def _unpad_win_rows(wt):
    z = wt[RWKV_COL0:]
    return jnp.concatenate([wt[:N_SGU], z[:3072], z[3072:3136], z[3200:3264], z[3328:3488], wt[N_SGU:RWKV_COL0]],
                           axis=0)


def _pad_rows(w, n):
    return jnp.concatenate([w, jnp.zeros((n - w.shape[0],) + w.shape[1:], w.dtype)], axis=0)


def _relu2_epi(c):
    return c, jnp.square(jnp.maximum(c, 0.0))


def _relu2_bwd_epi(c, hid):
    return (c * (2.0 * jnp.maximum(hid.astype(F32), 0.0)),)


def _add_epi(c, x):
    return (c + x,)


def _pre_fwd(*args):
    res = f_pre(*args)
    return res[1], res[2], res[4], res[5], res[6]


def local_step(x, tgt, w, late_token, late_weights, pair_start, pair_finish, pack_early):
    d = D_MODEL
    win_pt = _pad_win_rows(w["w_in"])
    sbp = _pad_rwkv_cols(w["shift_b"])
    wl = _pad_rows(w["w_lora_w"], 128)
    al = _pad_rows(w["a_lora_w"], 128)
    gl = _pad_rows(w["g_lora_w"], 256)
    sbt = w["sgu_b"].T

    (a_bf,) = ew_call(lambda x_, g_: (f_norm_in(x_, g_)[0],), [(x, d, 0)], [w["g_mix"] + late_token[:1, :1]],
                      [(d, BF16)], tm=256, name="norm_in")
    p_all = mm(a_bf, win_pt, "nt", tm=2048, tn=1280, name="mm_in")
    sgu_t = [(p_all, 2 * d, 0)]
    sgu_c = [w["sgu_ln_w"], w["sgu_ln_b"], w["sgu_w"], sbt]
    (s_bf,) = ew_call(f_sgu, sgu_t, sgu_c, [(d, BF16)], tm=256, name="sgu_fwd")
    q = shiftmix_fwd(p_all, sbp, tm=1024)
    pre_t = [(q, RWKV_WIDTH, 0)]
    pre_c = [wl, w["w0"], al, w["a0"], gl, w["k_k"], w["k_a"]]
    lw, kp, na, nb, g = ew_call(_pre_fwd, pre_t, pre_c, [(d, F32)] * 5, tm=256, name="rwkv_pre_fwd")
    scan_ops = [(q, 0), (lw, 0), (kp, 0), (q, 2), (na, 0), (nb, 0)]
    o, s0s = scan_fwd(scan_ops)
    w = {**w, **late_weights(o)}
    ya = mm(s_bf, w["w_proj_a"], "nn", tm=512, tn=1024, name="mm_proj_a")
    post_t = [(o, d, 0), (q, d, 0), (kp, d, 0), (q, d, 2), (g, d, 0)]
    post_c = [w["ln_x_w"], w["ln_x_b"], w["r_k"]]
    (ob_bf,) = ew_call(f_post, post_t, post_c, [(d, BF16)], tm=256, name="rwkv_post_fwd")
    yb = mm(ob_bf, w["w_proj_b"], "nn", tm=512, tn=1024, name="mm_proj_b")
    mix_t = [(ya, d, 0), (yb, d, 0), (p_all, d, 2), (p_all, d, 3)]
    (mixed_bf,) = ew_call(f_mix, mix_t, [], [(d, BF16)], tm=256, name="mix_fwd")
    h1 = mm(mixed_bf, w["w_out"], "nn", tm=512, tn=1024, name="mm_out", epi=_add_epi, extras=(x,))
    (f_bf,) = ew_call(lambda h_, g_: (f_ffn_in(h_, g_)[0],), [(h1, d, 0)], [w["g_ffn"]], [(d, BF16)], tm=256,
                      name="ffn_norm")
    hid, act_bf = mm(f_bf, w["w_ffn1"], "nn", tm=2048, tn=1024, name="mm_ffn1", out_dtypes=(BF16, BF16), epi=_relu2_epi)
    m3 = mm(act_bf, w["w_ffn2"], "nn", tm=1024, tn=512, name="mm_ffn2")
    dh2, dh2_bf, dg_final, loss = final_call(h1, m3, tgt, w["g_final"], tm=256)

    dhid_bf = mm(dh2_bf, w["w_ffn2"], "nt", tm=2048, tn=1024, name="mm_dact", out_dtypes=(BF16,), epi=_relu2_bwd_epi,
                 extras=(hid,))
    late_g = lax.empty((N_CHIPS, 2, pack_rows(LATE), HALF_W), F32)
    late_g = mm(act_bf, dh2_bf, "tn", tm=1024, tn=HALF_W, name="mm_dw_ffn2",
                into=(late_g, lambda i, j: (i, j, PIECE_OFF["w_ffn2"] // 1024)))
    df = mm(dhid_bf, w["w_ffn1"], "nt", tm=1024, tn=512, name="mm_df")
    late_g = mm(f_bf, dhid_bf, "tn", tm=1024, tn=HALF_W, name="mm_dw_ffn1",
                into=(late_g, lambda i, j: (j // 2, j % 2, PIECE_OFF["w_ffn1"] // 1024)))
    (dh1, dh1_bf), (dg_ffn,) = ew_vjp_call(f_ffn_in, [(h1, d, 0)], [w["g_ffn"]], [(df, d, 0), (dh2, d, 0)],
                                           [(F32, BF16)], [True], tm=256, name="ffn_norm_bwd")
    dmixed = mm(dh1_bf, w["w_out"], "nt", tm=512, tn=1024, name="mm_dmixed")
    late_g = mm(mixed_bf, dh1_bf, "tn", tm=256, tn=HALF_W, name="mm_dw_out",
                into=(late_g, lambda i, j: (i, j, PIECE_OFF["w_out"] // 256)))
    (dya_bf, dyb_bf, dga_bf, dgb_bf), _ = ew_vjp_call(f_mix, mix_t, [], [(dmixed, d, 0)], [(BF16,)] * 4, [], tm=256,
                                                      name="mix_bwd")
    dob = mm(dyb_bf, w["w_proj_b"], "nt", tm=512, tn=1024, name="mm_dob")
    late_g = mm(ob_bf, dyb_bf, "tn", tm=256, tn=HALF_W, name="mm_dw_proj_b",
                into=(late_g, lambda i, j: (i, j, PIECE_OFF["w_proj_b"] // 256)))
    late_g = mm(s_bf, dya_bf, "tn", tm=256, tn=HALF_W, name="mm_dw_proj_a",
                into=(late_g, lambda i, j: (i, j, PIECE_OFF["w_proj_a"] // 256)))
    late_state, late_token = pair_start(late_g, "late")
    post_c_after = [w["ln_x_w"] + late_token[:1, :1]] + post_c[1:]
    (do, dr_p, dkp_p, dv_p, dg), (dlnx_w, dlnx_b, dr_k) = ew_vjp_call(
        f_post, post_t, post_c_after, [(dob, d, 0)], [(F32,)] * 5, [True] * 3, tm=256, name="rwkv_post_bwd")
    late_part, late_part16 = pair_finish(late_state, do, "late")
    *scan_g, late_slots = scan_bwd(scan_ops, s0s, do, late_part16)
    pre_g = [(z, d, 0) for z in scan_g] + [(dg, d, 0), (dr_p, d, 0), (dkp_p, d, 0), (dv_p, d, 0)]
    (dq,), (dwl, dw0, dal, da0, dgl, dk_k, dk_a) = ew_vjp_call(
        f_pre, pre_t, pre_c, pre_g, [(F32,)], [True] * 7, tm=256, name="rwkv_pre_bwd")
    dp_rwkv, dsb = shiftmix_bwd(dq, 0, p_all, sbp, tm=512, name="shiftmix_bwd")
    ds = mm(dya_bf, w["w_proj_a"], "nt", tm=512, tn=1024, name="mm_ds")
    (dp_sgu,), (dln_w, dln_b, dsw, dsbt) = ew_vjp_call(f_sgu, sgu_t, sgu_c, [(ds, d, 0)], [(BF16,)], [True] * 4,
                                                       tm=256, name="sgu_bwd")
    dp_all = jnp.concatenate([dp_sgu, dga_bf, dgb_bf, dp_rwkv], axis=1)
    d_in_pt = mm(dp_all, a_bf, "tn", tm=1280, tn=1024, name="mm_dw_in")
    early_state, early_token = pair_start(pack_early({
        "w_in": _unpad_win_rows(d_in_pt), "w_lora_w": dwl[:LORA_W], "a_lora_w": dal[:LORA_A],
        "g_lora_w": dgl[:LORA_G]}), "early")
    da = mm(dp_all, win_pt, "nn", tm=1024, tn=256, name="mm_da")
    g_mix_after = w["g_mix"] + early_token[:1, :1]
    (grad_x,), (dg_mix,) = ew_vjp_call(f_norm_in, [(x, d, 0)], [g_mix_after], [(da, d, 0), (dh1, d, 0)], [(F32,)],
                                       [True], tm=256, name="norm_in_bwd")

    grads = {
        "g_mix": dg_mix, "sgu_ln_w": dln_w, "sgu_ln_b": dln_b, "sgu_w": dsw, "sgu_b": dsbt.T,
        "shift_b": _unpad_rwkv_cols(dsb),
        "w0": dw0, "a0": da0, "k_k": dk_k, "k_a": dk_a, "r_k": dr_k, "ln_x_w": dlnx_w, "ln_x_b": dlnx_b,
        "g_ffn": dg_ffn, "g_final": dg_final,
    }
    return loss[0, 0], grad_x, grads, (late_part, late_slots), early_state


MESH = pl.DeviceIdType.MESH
N_CHIPS = 4
PACK_TILE = 256
SMALL_ROWS = 160
_ANY = pl.BlockSpec(memory_space=pl.ANY)


def _coords():
    return lax.axis_index("x"), lax.axis_index("y"), lax.axis_index("c")


def _other_chips(x, y):
    return [(1 - x, y), (x, 1 - y), (1 - x, 1 - y)]


def _remote(src, dst, send_sems, recv_sems, k, to):
    return pltpu.make_async_remote_copy(src_ref=src, dst_ref=dst, send_sem=send_sems.at[k], recv_sem=recv_sems.at[k],
                                        device_id=to, device_id_type=MESH)


def gather_shards(pack):
    def body(src_ref, out_ref, token, send_sems, recv_sems):
        x, y, c = _coords()
        me = 2 * x + y
        sib = (x, y, 1 - c)
        chips = _other_chips(x, y)
        first = [_remote(src_ref.at[c], out_ref.at[me, c], send_sems, recv_sems, k, (cx, cy, c))
                 for k, (cx, cy) in enumerate(chips)]
        for cp in first:
            cp.start()
        passed = []
        for k, (cx, cy) in enumerate(chips):
            j = 2 * cx + cy
            _remote(src_ref.at[c], out_ref.at[j, c], send_sems, recv_sems, k, (cx, cy, c)).wait_recv()
            fwd = _remote(out_ref.at[j, c], out_ref.at[j, c], send_sems, recv_sems, 3 + k, sib)
            fwd.start()
            passed.append(fwd)
        for k, (cx, cy) in enumerate(chips):
            j = 2 * cx + cy
            _remote(out_ref.at[j, 1 - c], out_ref.at[j, 1 - c], send_sems, recv_sems, 3 + k, sib).wait_recv()
        for cp in first + passed:
            cp.wait_send()
        token[...] = jnp.zeros_like(token)

    return pl.pallas_call(
        body,
        name="gather_shards",
        in_specs=[_ANY],
        out_specs=[_ANY, pl.BlockSpec(memory_space=pltpu.VMEM)],
        out_shape=[jax.ShapeDtypeStruct((N_CHIPS,) + pack.shape, pack.dtype), jax.ShapeDtypeStruct((8, 128), F32)],
        scratch_shapes=[pltpu.SemaphoreType.DMA((6,)), pltpu.SemaphoreType.DMA((6,))],
    )(pack)


def _gather_copies(pack_ref, all_ref, send_sems, recv_sems):
    x, y, c = _coords()
    me = 2 * x + y
    return [(_remote(pack_ref.at[c], all_ref.at[me, c], send_sems, recv_sems, k, (cx, cy, c)),
             _remote(pack_ref.at[c], all_ref.at[2 * cx + cy, c], send_sems, recv_sems, k, (cx, cy, c)))
            for k, (cx, cy) in enumerate(_other_chips(x, y))]


_HBM = pl.BlockSpec(memory_space=pltpu.HBM)
_SEM = pl.BlockSpec(memory_space=pltpu.SEMAPHORE)
_SIDE_EFFECT = pltpu.SideEffectType.DATAFLOW_SIDE_EFFECTING


def split_start(name, copies, n, src, land_shape, after=None):
    def body(src_ref, land_ref, *refs):
        send_sems, recv_sems, token = refs[-5], refs[-4], refs[-1]
        for send, _ in copies(src_ref, land_ref, send_sems, recv_sems):
            send.start()
        token[...] = jnp.zeros_like(token)

    extra = () if after is None else (after,)
    *state, token = pl.pallas_call(
        body,
        name=name,
        out_shape=(pltpu.SemaphoreType.DMA((n,)), pltpu.SemaphoreType.DMA((n,)), pltpu.HBM(src.shape, src.dtype),
                   pltpu.HBM(land_shape, src.dtype), jax.ShapeDtypeStruct((8, 128), F32)),
        in_specs=(_HBM, _HBM) + (pl.BlockSpec(memory_space=pl.ANY),) * len(extra),
        out_specs=(_SEM, _SEM, _HBM, _HBM, pl.BlockSpec(memory_space=pltpu.VMEM)),
        input_output_aliases={0: 2, 1: 3},
        compiler_params=pltpu.CompilerParams(has_side_effects=_SIDE_EFFECT),
    )(pltpu.with_memory_space_constraint(src, pltpu.HBM),
      pltpu.with_memory_space_constraint(lax.empty(land_shape, src.dtype), pltpu.HBM), *extra)
    return state, token


def split_wait(name, copies, state, after):
    send_sems, recv_sems, src, land = state

    def body(src_ref, land_ref, send_sems, recv_sems, after_ref, src_out, land_out):
        for send, arrival in copies(src_ref, land_ref, send_sems, recv_sems):
            send.wait_send()
            arrival.wait_recv()

    return pl.pallas_call(
        body,
        name=name,
        out_shape=(pltpu.HBM(src.shape, src.dtype), pltpu.HBM(land.shape, land.dtype)),
        in_specs=(_HBM, _HBM, _SEM, _SEM, pl.BlockSpec(memory_space=pl.ANY)),
        out_specs=(_HBM, _HBM),
        input_output_aliases={0: 0, 1: 1},
        compiler_params=pltpu.CompilerParams(has_side_effects=_SIDE_EFFECT),
    )(src, land, send_sems, recv_sems, after)


def gather_forward(got):
    def body(got_ref, out_ref, send_sems, recv_sems):
        x, y, c = _coords()
        sib = (x, y, 1 - c)
        slots = [2 * cx + cy for cx, cy in _other_chips(x, y)]
        sends = [_remote(got_ref.at[j, c], out_ref.at[j, c], send_sems, recv_sems, k, sib) for k, j in enumerate(slots)]
        for cp in sends:
            cp.start()
        for k, j in enumerate(slots):
            _remote(got_ref.at[j, 1 - c], out_ref.at[j, 1 - c], send_sems, recv_sems, k, sib).wait_recv()
        for cp in sends:
            cp.wait_send()

    return pl.pallas_call(
        body,
        name="gather_forward",
        in_specs=[_ANY],
        out_specs=_ANY,
        out_shape=jax.ShapeDtypeStruct(got.shape, got.dtype),
        input_output_aliases={0: 0},
        scratch_shapes=[pltpu.SemaphoreType.DMA((3,)), pltpu.SemaphoreType.DMA((3,))],
    )(got)


def pair_sum(g, got, tag, *, tm):
    n, _, rows, width = g.shape

    def body(c_ref, own_ref, got_ref, out_ref, out16_ref):
        total = own_ref[0, 0] + got_ref[0]
        out_ref[0] = total
        out16_ref[0] = total.astype(BF16)

    blk = pl.BlockSpec((1, tm, width), lambda j, i, c_ref: (j, i, 0))
    return pl.pallas_call(
        body,
        name="pair_sum_" + tag,
        grid_spec=pltpu.PrefetchScalarGridSpec(
            num_scalar_prefetch=1,
            grid=(n, rows // tm),
            in_specs=[pl.BlockSpec((1, 1, tm, width), lambda j, i, c_ref: (j, c_ref[0], i, 0)), blk],
            out_specs=[blk, blk],
        ),
        out_shape=[jax.ShapeDtypeStruct(got.shape, F32), jax.ShapeDtypeStruct(got.shape, BF16)],
        compiler_params=_cparams(2),
    )(lax.axis_index("c").reshape(1).astype(jnp.int32), g, got)


def _pair_copies(g_ref, got_ref, send_sems, recv_sems):
    x, y, c = _coords()
    copies = [_remote(g_ref.at[j, 1 - c], got_ref.at[j], send_sems, recv_sems, j, (x, y, 1 - c))
              for j in range(N_CHIPS)]
    return [(cp, cp) for cp in copies]


def _chip_copies(p_ref, slots_ref, send_sems, recv_sems):
    x, y, c = _coords()
    me = 2 * x + y
    return [(_remote(p_ref.at[2 * cx + cy], slots_ref.at[me], send_sems, recv_sems, k, (cx, cy, c)),
             _remote(p_ref.at[me], slots_ref.at[2 * cx + cy], send_sems, recv_sems, k, (cx, cy, c)))
            for k, (cx, cy) in enumerate(_other_chips(x, y))]


def sum_with_own(own, slots, mine, after, *, tm, name):
    n, rows, width = slots.shape

    def body(mine_ref, own_ref, *refs):
        acc = None
        for s in range(n):
            term = jnp.where(mine_ref[0] == s, own_ref[0], refs[s][0].astype(F32))
            acc = term if acc is None else acc + term
        refs[-1][...] = acc

    return pl.pallas_call(
        body,
        name=name,
        grid_spec=pltpu.PrefetchScalarGridSpec(
            num_scalar_prefetch=1,
            grid=(rows // tm,),
            in_specs=[pl.BlockSpec((1, tm, width), lambda i, mine_ref: (mine_ref[0], i, 0))]
            + [pl.BlockSpec((1, tm, width), lambda i, mine_ref, s=s: (s, i, 0)) for s in range(n)]
            + [pl.BlockSpec(after.shape, lambda i, mine_ref: (0,) * after.ndim)],
            out_specs=pl.BlockSpec((tm, width), lambda i, mine_ref: (i, 0)),
        ),
        out_shape=jax.ShapeDtypeStruct((rows, width), F32),
        compiler_params=_cparams(1),
    )(mine.reshape(1).astype(jnp.int32), own, *([slots] * n), after)


def exchange_halves(s, tag):
    rq = PACK_TILE
    nq = s.shape[0] // rq

    def body(s_ref, out_ref, sbuf, rbuf, send_sems, recv_sems, in_sems, out_sems):
        x, y, c = _coords()
        sib = (x, y, 1 - c)
        rows = lambda q: pl.ds(q * rq, rq)
        loads = [pltpu.make_async_copy(s_ref.at[rows(q)], sbuf.at[rows(q)], in_sems.at[q]) for q in range(nq)]
        for cp in loads:
            cp.start()
        sends = []
        for q in range(nq):
            loads[q].wait()
            sends.append(_remote(sbuf.at[rows(q)], rbuf.at[rows(q)], send_sems, recv_sems, q, sib))
            sends[q].start()
        stores = []
        for q in range(nq):
            sends[q].wait_recv()
            stores.append(pltpu.make_async_copy(rbuf.at[rows(q)], out_ref.at[rows(q)], out_sems.at[q]))
            stores[q].start()
        for cp in sends:
            cp.wait_send()
        for cp in stores:
            cp.wait()

    return pl.pallas_call(
        body,
        name="exchange_halves_" + tag,
        in_specs=[_ANY],
        out_specs=_ANY,
        out_shape=jax.ShapeDtypeStruct(s.shape, s.dtype),
        scratch_shapes=[pltpu.VMEM(s.shape, s.dtype), pltpu.VMEM(s.shape, s.dtype)]
        + [pltpu.SemaphoreType.DMA((nq,))] * 4,
        compiler_params=pltpu.CompilerParams(vmem_limit_bytes=VMEM_LIMIT),
    )(s)


def sum_all(s, after):
    rows = s.shape[0]
    half = rows // 2

    def body(s_ref, after_ref, out_ref, theirs, pair, slots, send_sems, recv_sems):
        x, y, c = _coords()
        me = 2 * x + y
        sib = (x, y, 1 - c)
        chips = _other_chips(x, y)
        swap = _remote(s_ref, theirs, send_sems, recv_sems, 0, sib)
        swap.start()
        swap.wait_recv()
        pair[...] = s_ref[...] + theirs[...]
        mine = pl.ds(pl.multiple_of(c * half, 8), half)
        other = pl.ds(pl.multiple_of((1 - c) * half, 8), half)
        sends = [_remote(pair.at[mine], slots.at[me], send_sems, recv_sems, 1 + k, (cx, cy, c))
                 for k, (cx, cy) in enumerate(chips)]
        for cp in sends:
            cp.start()
        for k, (cx, cy) in enumerate(chips):
            _remote(pair.at[mine], slots.at[2 * cx + cy], send_sems, recv_sems, 1 + k, (cx, cy, c)).wait_recv()
        slots[me] = pair[mine]
        out_ref[mine] = ((slots[0] + slots[1]) + slots[2]) + slots[3]
        last = _remote(out_ref.at[mine], out_ref.at[mine], send_sems, recv_sems, 4, sib)
        last.start()
        _remote(out_ref.at[other], out_ref.at[other], send_sems, recv_sems, 4, sib).wait_recv()
        for cp in [swap] + sends + [last]:
            cp.wait_send()

    vmem = pl.BlockSpec(memory_space=pltpu.VMEM)
    return pl.pallas_call(
        body,
        name="sum_all",
        in_specs=[vmem, vmem],
        out_specs=vmem,
        out_shape=jax.ShapeDtypeStruct(s.shape, s.dtype),
        scratch_shapes=[pltpu.VMEM(s.shape, s.dtype), pltpu.VMEM(s.shape, s.dtype),
                        pltpu.VMEM((N_CHIPS, half, s.shape[1]), s.dtype), pltpu.SemaphoreType.DMA((5,)),
                        pltpu.SemaphoreType.DMA((5,))],
        compiler_params=pltpu.CompilerParams(vmem_limit_bytes=VMEM_LIMIT),
    )(s, after)


ADAM_LR = 0.001
ADAM_B1 = 0.9
ADAM_B2 = 0.999
ADAM_EPS = 1e-08
ADAM_WD = 0.01
ADAM_STEP = 10


def f_adamw(g, w, m, v):
    m = ADAM_B1 * m + (1.0 - ADAM_B1) * g
    v = ADAM_B2 * v + (1.0 - ADAM_B2) * jnp.square(g)
    m_hat = m / (1.0 - ADAM_B1 ** ADAM_STEP)
    v_hat = v / (1.0 - ADAM_B2 ** ADAM_STEP)
    delta = -ADAM_LR * (m_hat / (jnp.sqrt(v_hat) + ADAM_EPS) + ADAM_WD * w)
    return delta, m, v


def adamw_many(gs, ws, ms, vs):
    n = len(gs)

    def body(*refs):
        ins, outs = refs[:4 * n], refs[4 * n:]
        for i in range(n):
            delta, nm, nv = f_adamw(ins[i][...], ins[n + i][...], ins[2 * n + i][...], ins[3 * n + i][...])
            outs[i][...] = delta
            outs[n + i][...] = nm
            outs[2 * n + i][...] = nv

    vmem = pl.BlockSpec(memory_space=pltpu.VMEM)
    res = pl.pallas_call(
        body,
        name="adamw_small",
        in_specs=[vmem] * (4 * n),
        out_specs=[vmem] * (3 * n),
        out_shape=[jax.ShapeDtypeStruct(w.shape, F32) for w in ws] * 3,
    )(*gs, *ws, *ms, *vs)
    return res[:n], res[n:2 * n], res[2 * n:]


EARLY = ["w_in", "w_lora_w", "a_lora_w", "g_lora_w"]
LATE = ["w_ffn1", "w_ffn2", "w_proj_b", "w_out", "w_proj_a"]
LORAS = ["w_lora_w", "a_lora_w", "g_lora_w"]
HALF_W = 512
PIECE_ROWS = {"w_in": 1864, "w_ffn1": 1024, "w_ffn2": 1024, "w_proj_a": 256, "w_proj_b": 256, "w_out": 256,
              "w_lora_w": 32, "a_lora_w": 32, "g_lora_w": 80}
PIECE_OFF = {"w_in": 0, "w_lora_w": 1920, "a_lora_w": 1952, "g_lora_w": 2000,
             "w_ffn1": 0, "w_ffn2": 1024, "w_proj_b": 2048, "w_out": 2304, "w_proj_a": 2560}
LO_OFF = 2080


def pack_rows(group):
    return 2304 if group is EARLY else 2816
SHARD_AXIS = {"w_in": 1, "w_proj_a": 0, "w_lora_w": 1, "a_lora_w": 1, "g_lora_w": 1, "w_proj_b": 0, "w_out": 0,
              "w_ffn1": 1, "w_ffn2": 0}
SHARD_SHAPE = {"w_in": (1024, 1864), "w_proj_a": (256, 1024), "w_lora_w": (64, 256), "a_lora_w": (64, 256),
               "g_lora_w": (160, 256), "w_proj_b": (256, 1024), "w_out": (256, 1024), "w_ffn1": (1024, 1024),
               "w_ffn2": (1024, 1024)}
SHIFT_SHARD = (2, 840)
VECTORS = ["g_mix", "sgu_ln_w", "sgu_ln_b", "w0", "a0", "k_k", "k_a", "r_k", "ln_x_w", "ln_x_b", "g_ffn", "g_final"]
SMALL = VECTORS + ["sgu_w", "sgu_b"]
SMALL_SHAPE = {**{n: (1, 1024) for n in VECTORS}, "sgu_w": (8, 128, 128), "sgu_b": (8, 128)}
WEIGHTS = ["g_mix", "w_in", "sgu_ln_w", "sgu_ln_b", "sgu_w", "sgu_b", "w_proj_a", "shift_b", "w_lora_w", "w0",
           "a_lora_w", "a0", "g_lora_w", "k_k", "k_a", "r_k", "ln_x_w", "ln_x_b", "w_proj_b", "w_out", "g_ffn",
           "w_ffn1", "w_ffn2", "g_final"]


def _size(shape):
    n = 1
    for s in shape:
        n *= s
    return n


def _pack_rows(parts, rows, dtype):
    flat = jnp.concatenate([p.reshape(-1).astype(dtype) for p in parts])
    return jnp.concatenate([flat, jnp.zeros((rows * 1024 - flat.shape[0],), dtype)]).reshape(rows, 1024)


def _unpack_rows(packed, shapes):
    flat = packed.reshape(-1)
    out, off = [], 0
    for shp in shapes:
        out.append(flat[off:off + _size(shp)].reshape(shp))
        off += _size(shp)
    return out


def _shard_of(name, full, j):
    ax = SHARD_AXIS[name]
    n = SHARD_SHAPE[name][ax]
    return lax.slice_in_dim(full, j * n, (j + 1) * n, axis=ax)


def _pad_cols(z, n):
    return jnp.concatenate([z, jnp.zeros((z.shape[0], n - z.shape[1]), z.dtype)], axis=1)


def _row_form(name, s):
    return s.T if name == "w_in" else s


def _half_piece(name, rf, h):
    if name in LORAS:
        r = PIECE_ROWS[name]
        return _pad_cols(rf[h * r:(h + 1) * r], HALF_W)
    return rf[:, HALF_W * h:HALF_W * (h + 1)]


def _pack_half(group, rf_fn, h, dtype, tail=()):
    parts, pos, rows = [], 0, pack_rows(group)
    for n in group:
        if PIECE_OFF[n] > pos:
            parts.append(jnp.zeros((PIECE_OFF[n] - pos, HALF_W), dtype))
        parts.append(_half_piece(n, rf_fn(n), h).astype(dtype))
        pos = PIECE_OFF[n] + PIECE_ROWS[n]
    for t in tail:
        parts.append(t)
        pos += t.shape[0]
    parts.append(jnp.zeros((rows - pos, HALF_W), dtype))
    return jnp.concatenate(parts, axis=0)


def _piece(pack, name):
    return pack[PIECE_OFF[name]:PIECE_OFF[name] + PIECE_ROWS[name]]


def _join_halves(name, p0, p1):
    if name in LORAS:
        return jnp.concatenate([p0[:, :SHARD_SHAPE[name][1]], p1[:, :SHARD_SHAPE[name][1]]], axis=0)
    return jnp.concatenate([p0, p1], axis=1)


def _grad_row_form(name, full, j):
    if name == "w_in":
        return full[SHARD_SHAPE[name][1] * j:SHARD_SHAPE[name][1] * (j + 1)]
    return _shard_of(name, full, j)


def adamw_weight(name, g_own, g_other, w, m, v):
    rows, width = w.shape
    if name in LORAS:
        tm = PIECE_ROWS[name]
        grid = (2, 1)
        native = pl.BlockSpec((tm, width), lambda h, i: (h, 0))
    elif name == "w_in":
        tm, lanes = rows, 128
        grid = (2, HALF_W // lanes)
        native = pl.BlockSpec((tm, lanes), lambda h, i: (0, h * (HALF_W // lanes) + i))
    else:
        tm = 256
        grid = (2, rows // tm)
        native = pl.BlockSpec((tm, HALF_W), lambda h, i: (i, h))
    off = PIECE_OFF[name] // tm
    if name == "w_in":
        packed = pl.BlockSpec((tm, 128), lambda h, i: (0, i))
    else:
        packed = pl.BlockSpec((tm, HALF_W), lambda h, i: (off + i, 0))

    def body(go_ref, gx_ref, w_ref, m_ref, v_ref, g_ref, d_ref, nm_ref, nv_ref):
        g = jnp.where(pl.program_id(0) == lax.axis_index("c"), go_ref[...], gx_ref[...])[:, :w_ref.shape[1]]
        delta, nm, nv = f_adamw(g, w_ref[...], m_ref[...], v_ref[...])
        g_ref[...] = g
        d_ref[...] = delta
        nm_ref[...] = nm
        nv_ref[...] = nv

    return pl.pallas_call(
        body,
        name="adamw_" + name,
        grid=grid,
        in_specs=[packed, packed, native, native, native],
        out_specs=[native] * 4,
        out_shape=[jax.ShapeDtypeStruct(w.shape, F32)] * 4,
        compiler_params=_cparams(2),
    )(g_own, g_other, w, m, v)


def kernel(x, g_mix, w_in, sgu_ln_w, sgu_ln_b, sgu_w, sgu_b, w_proj_a, shift_b, w_lora_w, w0, a_lora_w, a0, g_lora_w, k_k, k_a, r_k, ln_x_w, ln_x_b, w_proj_b, w_out, g_ffn, w_ffn1, w_ffn2, g_final, loss_target, m_g_mix, m_w_in, m_sgu_ln_w, m_sgu_ln_b, m_sgu_w, m_sgu_b, m_w_proj_a, m_shift_b, m_w_lora_w, m_w0, m_a_lora_w, m_a0, m_g_lora_w, m_k_k, m_k_a, m_r_k, m_ln_x_w, m_ln_x_b, m_w_proj_b, m_w_out, m_g_ffn, m_w_ffn1, m_w_ffn2, m_g_final, v_g_mix, v_w_in, v_sgu_ln_w, v_sgu_ln_b, v_sgu_w, v_sgu_b, v_w_proj_a, v_shift_b, v_w_lora_w, v_w0, v_a_lora_w, v_a0, v_g_lora_w, v_k_k, v_k_a, v_r_k, v_ln_x_w, v_ln_x_b, v_w_proj_b, v_w_out, v_g_ffn, v_w_ffn1, v_w_ffn2, v_g_final):
    given = dict(zip(WEIGHTS, (g_mix, w_in, sgu_ln_w, sgu_ln_b, sgu_w, sgu_b, w_proj_a, shift_b, w_lora_w, w0, a_lora_w, a0, g_lora_w, k_k, k_a, r_k, ln_x_w, ln_x_b, w_proj_b, w_out, g_ffn, w_ffn1, w_ffn2, g_final)))
    mom_m = dict(zip(WEIGHTS, (m_g_mix, m_w_in, m_sgu_ln_w, m_sgu_ln_b, m_sgu_w, m_sgu_b, m_w_proj_a, m_shift_b, m_w_lora_w, m_w0, m_a_lora_w, m_a0, m_g_lora_w, m_k_k, m_k_a, m_r_k, m_ln_x_w, m_ln_x_b, m_w_proj_b, m_w_out, m_g_ffn, m_w_ffn1, m_w_ffn2, m_g_final)))
    mom_v = dict(zip(WEIGHTS, (v_g_mix, v_w_in, v_sgu_ln_w, v_sgu_ln_b, v_sgu_w, v_sgu_b, v_w_proj_a, v_shift_b, v_w_lora_w, v_w0, v_a_lora_w, v_a0, v_g_lora_w, v_k_k, v_k_a, v_r_k, v_ln_x_w, v_ln_x_b, v_w_proj_b, v_w_out, v_g_ffn, v_w_ffn1, v_w_ffn2, v_g_final)))
    chip = 2 * lax.axis_index("x") + lax.axis_index("y")

    def local_block(tree, n):
        return tree[n] if n == "g_final" else tree[n][0]

    sb = local_block(given, "shift_b")
    lo_part = lambda z: (z - z.astype(BF16).astype(F32)).astype(BF16)
    row_form = lambda tree: (lambda n: _row_form(n, local_block(tree, n)))
    tile16 = lambda z: jnp.pad(z, ((0, 16 - z.shape[0]), (0, HALF_W - z.shape[1])))
    sb_tiles = [tile16(f(sb[:, lanes])) for f in (lambda z: z.astype(BF16), lo_part)
                for lanes in (slice(0, HALF_W), slice(HALF_W, None))]
    tails = [[_half_piece(n, lo_part(local_block(given, n)), h) for n in LORAS] + sb_tiles for h in range(2)]
    pack_w = jnp.stack([_pack_half(EARLY, row_form(given), h, BF16, tails[h]) for h in range(2)])
    gathered, gathered_token = gather_shards(pack_w)
    gathered = lax.dynamic_update_index_in_dim(gathered, pack_w, chip, 0)
    pack_late = jnp.stack([_pack_half(LATE, row_form(given), h, BF16) for h in range(2)])
    late_state, late_token = split_start("gather_start", _gather_copies, 3, pack_late, (N_CHIPS,) + pack_late.shape,
                                         gathered_token)

    def whole(group, got, own):
        half = lambda n, j, h: jnp.where(chip == j, _piece(own[h], n), _piece(got[j, h], n))
        shard = lambda n, j: _join_halves(n, half(n, j, 0), half(n, j, 1))
        return {n: jnp.concatenate([shard(n, j) for j in range(N_CHIPS)],
                                   axis=0 if n == "w_in" else SHARD_AXIS[n]) for n in group}

    w = whole(EARLY, gathered, pack_w)
    late_weights = lambda after: whole(
        LATE, gather_forward(split_wait("gather_wait", _gather_copies, late_state, after)[1]), pack_late)
    off = LO_OFF
    for n in LORAS:
        r, cols = PIECE_ROWS[n], SHARD_SHAPE[n][1]
        lo = jnp.concatenate([jnp.concatenate([gathered[j, 0, off:off + r, :cols], gathered[j, 1, off:off + r, :cols]],
                                              axis=0) for j in range(N_CHIPS)], axis=1)
        w[n] = w[n].astype(F32) + lo.astype(F32)
        off += r
    sb_tile = lambda j, t, lanes: gathered[j, 0, off + 16 * t:off + 16 * t + 2, :lanes].astype(F32)
    rest = SHIFT_SHARD[1] - HALF_W
    w["shift_b"] = jnp.concatenate(
        [jnp.concatenate([sb_tile(j, 0, HALF_W) + sb_tile(j, 2, HALF_W), sb_tile(j, 1, rest) + sb_tile(j, 3, rest)],
                         axis=1) for j in range(N_CHIPS)], axis=1)
    for n in SMALL:
        w[n] = local_block(given, n).reshape(SMALL_SHAPE[n])

    def pair_start(g_pack, tag):
        return split_start("reduce_pair_start_" + tag, _pair_copies, N_CHIPS, g_pack, (N_CHIPS,) + g_pack.shape[2:])

    def pair_finish(state, after, tag):
        g_pack, got = split_wait("reduce_pair_wait_" + tag, _pair_copies, state, after)
        return pair_sum(g_pack, got, tag, tm=got.shape[1] // 2)

    pack_early = lambda g: jnp.stack([jnp.stack([_pack_half(EARLY, lambda n: _grad_row_form(n, g[n], j), h, F32)
                                                 for h in range(2)]) for j in range(N_CHIPS)])
    loss, grad_x, grads, (late_part, late_slots), early_state = local_step(
        x[0], loss_target[0], w, late_token, late_weights, pair_start, pair_finish, pack_early)

    early_part, early_part16 = pair_finish(early_state, grad_x, "early")
    s_pack = _pack_rows([grads[n] for n in SMALL] + [grads["shift_b"], loss.reshape(1, 1)], SMALL_ROWS, F32)
    chips_state, token = split_start("reduce_chips_start", _chip_copies, 3, early_part16, early_part16.shape)
    out_g, out_d, out_m, out_v = {}, {}, {}, {}

    def finish(group, tag, part, slots):
        half_sum = sum_with_own(part, slots, chip, token, tm=part.shape[1] // 2, name="chip_sum_" + tag)
        other_half = exchange_halves(half_sum, tag)
        for n in group:
            res = adamw_weight(n, half_sum, other_half,
                               *[_row_form(n, local_block(t, n)) for t in (given, mom_m, mom_v)])
            for tree, z in zip((out_g, out_d, out_m, out_v), res):
                tree[n] = _row_form(n, z)

    finish(LATE, "late", late_part, late_slots)

    small_shapes = [SMALL_SHAPE[n] for n in SMALL]
    g_small = sum_all(s_pack, token)
    *g_parts, loss = _unpack_rows(g_small, small_shapes + [(2, N_RWKV), ()])
    out_g.update(zip(SMALL, g_parts[:-1]))
    g_sb = lax.dynamic_slice_in_dim(g_parts[-1], chip * SHIFT_SHARD[1], SHIFT_SHARD[1], axis=1)
    out_g["shift_b"] = g_sb
    names = SMALL + ["shift_b"]
    native = lambda tree: [local_block(tree, n).reshape(SMALL_SHAPE.get(n, SHIFT_SHARD)) for n in names]
    small_res = adamw_many(g_parts[:-1] + [g_sb], native(given), native(mom_m), native(mom_v))
    for tree, res in zip((out_d, out_m, out_v), small_res):
        tree.update(zip(names, res))

    after = (out_v["w_out"], out_v["sgu_w"])
    early_slots = split_wait("reduce_chips_wait", _chip_copies, chips_state,
                             jnp.concatenate([z.reshape(-1)[:8] for z in after]))[1]
    finish(EARLY, "early", early_part, early_slots)

    def block_of(tree, n):
        return tree[n].reshape(given[n].shape)

    return (loss, grad_x[None], *[block_of(out_g, n) for n in WEIGHTS], *[block_of(out_d, n) for n in WEIGHTS],
            *[block_of(out_m, n) for n in WEIGHTS], *[block_of(out_v, n) for n in WEIGHTS])
```

```python
import functools

import jax
import jax.numpy as jnp
from jax import lax
from jax.experimental import pallas as pl
from jax.experimental.pallas import tpu as pltpu

F32 = jnp.float32
BF16 = jnp.bfloat16

D_MODEL = 1024
N_HEADS = 16
HEAD = 64
SCAN_CHUNK = 64

VMEM_LIMIT = 56 * 1024 * 1024


_BDIMS = {
    "nn": (((2,), (1,)), ((0,), (0,))),
    "nt": (((2,), (2,)), ((0,), (0,))),
    "tn": (((1,), (1,)), ((0,), (0,))),
}


def _raw_bdot(x, y, mode, fine):
    if fine:
        return lax.dot_general(x, y, _BDIMS[mode], precision=lax.Precision.HIGH, preferred_element_type=F32)
    return lax.dot_general(x.astype(BF16), y.astype(BF16), _BDIMS[mode], preferred_element_type=F32)


@functools.partial(jax.custom_vjp, nondiff_argnums=(2, 3))
def bdot(x, y, mode, fine=True):
    return _raw_bdot(x, y, mode, fine)


def _bdot_fwd(x, y, mode, fine):
    return _raw_bdot(x, y, mode, fine), (x, y)


def _bdot_bwd(mode, fine, res, g):
    x, y = res
    if mode == "nn":
        return bdot(g, y, "nt", fine), bdot(x, g, "tn", fine)
    if mode == "nt":
        return bdot(g, y, "nn", fine), bdot(g, x, "tn", fine)
    return bdot(y, g, "nt", fine), bdot(x, g, "nn", fine)


bdot.defvjp(_bdot_fwd, _bdot_bwd)


def _scan_chunk(S0, r, lw, k, v, a, b):
    nh, lc, _ = r.shape
    ti = lax.broadcasted_iota(jnp.int32, (lc, lc), 0)
    si = lax.broadcasted_iota(jnp.int32, (lc, lc), 1)
    incl = (si <= ti).astype(F32)
    strict = (si < ti).astype(F32)
    eye = (si == ti).astype(F32)
    cl = bdot(jnp.broadcast_to(incl, (nh, lc, lc)), lw, "nn")
    cl_last = cl[:, lc - 1:lc, :]
    g_last = jnp.exp(cl_last - cl)
    at = a * jnp.exp(cl - lw)
    bt = b * jnp.exp(-cl)
    kt = k * jnp.exp(-cl)
    rt = r * jnp.exp(cl)
    ar = jnp.concatenate([at, rt], axis=1)
    ar_b = bdot(ar, bt, "nt", False)
    ar_k = bdot(ar, kt, "nt", False)
    m_ab, m_rb = ar_b[:, :lc] * strict, ar_b[:, lc:] * incl
    m_ak, m_rk = ar_k[:, :lc] * strict, ar_k[:, lc:] * incl
    x = eye + m_ab
    p = bdot(m_ab, m_ab, "nn", False)
    n = 2
    while n * 2 < lc:
        px = bdot(jnp.concatenate([p, x], axis=1), p, "nn", False)
        p = px[:, :lc]
        x = x + px[:, lc:]
        n *= 2
    x = x + bdot(x, p, "nn", False)
    ar_s = bdot(ar, S0, "nt", False)
    akrk_v = bdot(jnp.concatenate([m_ak, m_rk], axis=1), v, "nn", False)
    u = bdot(x, ar_s[:, :lc] + akrk_v[:, :lc], "nn", False)
    o = ar_s[:, lc:] + bdot(m_rb, u, "nn", False) + akrk_v[:, lc:]
    s_last = S0 * jnp.exp(cl_last) + bdot(jnp.concatenate([u, v], axis=1),
                                          jnp.concatenate([b * g_last, k * g_last], axis=1), "tn", False)
    return o, s_last


def _split_heads(z):
    return jnp.stack([z[:, HEAD * h:HEAD * (h + 1)] for h in range(N_HEADS)], axis=0)


def _merge_heads(z):
    return jnp.concatenate([z[h] for h in range(N_HEADS)], axis=1)


def _scan_specs(t, ops, rev):
    nc = t // SCAN_CHUNK
    row = (lambda c: nc - 1 - c) if rev else (lambda c: c)
    specs = [pl.BlockSpec((SCAN_CHUNK, D_MODEL), lambda c, cb=cb: (row(c), cb)) for _, cb in ops]
    state = pl.BlockSpec((1, N_HEADS, HEAD, HEAD), lambda c: (row(c), 0, 0, 0))
    return nc, specs, state


def scan_fwd(ops):
    t = ops[0][0].shape[0]
    nc, specs, state = _scan_specs(t, ops, False)

    def body(r_ref, lw_ref, k_ref, v_ref, a_ref, b_ref, o_ref, s0_ref, s_scr):
        @pl.when(pl.program_id(0) == 0)
        def _():
            s_scr[...] = jnp.zeros_like(s_scr)

        s0 = s_scr[...]
        s0_ref[0] = s0
        o, s_last = _scan_chunk(s0, *[_split_heads(z[...]) for z in (r_ref, lw_ref, k_ref, v_ref, a_ref, b_ref)])
        o_ref[...] = _merge_heads(o)
        s_scr[...] = s_last

    return pl.pallas_call(
        body,
        name="scan_fwd",
        grid=(nc,),
        in_specs=specs,
        out_specs=[pl.BlockSpec((SCAN_CHUNK, D_MODEL), lambda c: (c, 0)), state],
        out_shape=[jax.ShapeDtypeStruct((t, D_MODEL), F32), jax.ShapeDtypeStruct((nc, N_HEADS, HEAD, HEAD), F32)],
        scratch_shapes=[pltpu.VMEM((N_HEADS, HEAD, HEAD), F32)],
        compiler_params=_cparams(1),
    )(*[a for a, _ in ops])


def scan_bwd(ops, s0s, do, part):
    t = ops[0][0].shape[0]
    nc, specs, state = _scan_specs(t, ops + [(do, 0)], True)

    def body(r_ref, lw_ref, k_ref, v_ref, a_ref, b_ref, do_ref, s0_ref, part_ref, *rest):
        out_refs, slots_ref, ds_scr, send_sems, recv_sems = rest[:6], rest[6], rest[7], rest[8], rest[9]
        step = pl.program_id(0)
        x, y, c = _coords()
        me = 2 * x + y
        chips = _other_chips(x, y)
        sends = [_remote(part_ref.at[2 * cx + cy], slots_ref.at[me], send_sems, recv_sems, k, (cx, cy, c))
                 for k, (cx, cy) in enumerate(chips)]

        @pl.when(step == 0)
        def _():
            ds_scr[...] = jnp.zeros_like(ds_scr)
            for cp in sends:
                cp.start()

        _, vjp = jax.vjp(_scan_chunk, s0_ref[0],
                         *[_split_heads(z[...]) for z in (r_ref, lw_ref, k_ref, v_ref, a_ref, b_ref)])
        grads = vjp((_split_heads(do_ref[...]), ds_scr[...]))
        for o_ref, g in zip(out_refs, grads[1:]):
            o_ref[...] = _merge_heads(g)
        ds_scr[...] = grads[0]

        @pl.when(step == nc - 1)
        def _():
            for k, (cx, cy) in enumerate(chips):
                _remote(part_ref.at[me], slots_ref.at[2 * cx + cy], send_sems, recv_sems, k, (cx, cy, c)).wait_recv()
            for cp in sends:
                cp.wait_send()

    return pl.pallas_call(
        body,
        name="scan_bwd",
        grid=(nc,),
        in_specs=specs + [state, _ANY],
        out_specs=[pl.BlockSpec((SCAN_CHUNK, D_MODEL), lambda c: (nc - 1 - c, 0))] * 6 + [_ANY],
        out_shape=[jax.ShapeDtypeStruct((t, D_MODEL), F32)] * 6 + [jax.ShapeDtypeStruct(part.shape, part.dtype)],
        scratch_shapes=[pltpu.VMEM((N_HEADS, HEAD, HEAD), F32), pltpu.SemaphoreType.DMA((3,)),
                        pltpu.SemaphoreType.DMA((3,))],
        compiler_params=_cparams(1),
    )(*[a for a, _ in ops], do, s0s, part)


_MDIMS = {
    "nn": (((1,), (0,)), ((), ())),
    "nt": (((1,), (1,)), ((), ())),
    "tn": (((0,), (0,)), ((), ())),
}


def _raw_mdot(x, y, mode, exact):
    if exact:
        return lax.dot_general(x, y, _MDIMS[mode], precision=lax.Precision.HIGH, preferred_element_type=F32)
    return lax.dot_general(x.astype(BF16), y.astype(BF16), _MDIMS[mode], preferred_element_type=F32)


@functools.partial(jax.custom_vjp, nondiff_argnums=(2, 3))
def mdot(x, y, mode, exact):
    return _raw_mdot(x, y, mode, exact)


def _mdot_fwd(x, y, mode, exact):
    return _raw_mdot(x, y, mode, exact), (x, y)


def _mdot_bwd(mode, exact, res, g):
    x, y = res
    if mode == "nn":
        return mdot(g, y, "nt", exact), mdot(x, g, "tn", exact)
    if mode == "nt":
        return mdot(g, y, "nn", exact), mdot(g, x, "tn", exact)
    return mdot(y, g, "nt", exact), mdot(x, g, "nn", exact)


mdot.defvjp(_mdot_fwd, _mdot_bwd)


def _seg_ones():
    i = lax.broadcasted_iota(jnp.int32, (256, 256), 0) // HEAD
    j = lax.broadcasted_iota(jnp.int32, (256, 256), 1) // HEAD
    return (i == j).astype(BF16)


@jax.custom_vjp
def segsum(x):
    bd = _seg_ones()
    hi = x.astype(BF16)
    lo = (x - hi.astype(F32)).astype(BF16)
    cols = []
    for j in range(x.shape[1] // 256):
        sl = slice(256 * j, 256 * (j + 1))
        cols.append(jnp.dot(hi[:, sl], bd, preferred_element_type=F32)
                    + jnp.dot(lo[:, sl], bd, preferred_element_type=F32))
    return jnp.concatenate(cols, axis=1)


segsum.defvjp(lambda x: (segsum(x), None), lambda _, g: (segsum(g),))


NORM_EPS = 1e-6
LN_EPS = 1e-5
GN_EPS = 64e-5
SGU_CHUNK = 128
SGU_GROUPS = 8


def _rms(x, g):
    return x * lax.rsqrt(jnp.mean(x * x, axis=-1, keepdims=True) + NORM_EPS) * g


def f_norm_in(x, g):
    return _rms(x, g), x


def f_sgu(p, ln_w, ln_b, sw, sbt):
    tm = p.shape[0]
    z = 0.5 * p * (1.0 + lax.erf(p * 0.7071067811865476))
    u, v = z[:, :D_MODEL], z[:, D_MODEL:]
    mu = jnp.mean(v, axis=-1, keepdims=True)
    d = v - mu
    vn = d * lax.rsqrt(jnp.mean(d * d, axis=-1, keepdims=True) + LN_EPS) * ln_w + ln_b
    ii = lax.broadcasted_iota(jnp.int32, (SGU_CHUNK, SGU_CHUNK), 0)
    jj = lax.broadcasted_iota(jnp.int32, (SGU_CHUNK, SGU_CHUNK), 1)
    mask = (jj <= ii).astype(F32)
    gi = lax.broadcasted_iota(jnp.int32, (SGU_GROUPS, D_MODEL), 0)
    ci = lax.broadcasted_iota(jnp.int32, (SGU_GROUPS, D_MODEL), 1) // SGU_CHUNK
    bias = mdot(sbt, (gi == ci).astype(F32), "nn", True)
    rows = []
    for c in range(tm // SGU_CHUNK):
        cols = []
        for g in range(SGU_GROUPS):
            blk = vn[c * SGU_CHUNK:(c + 1) * SGU_CHUNK, g * SGU_CHUNK:(g + 1) * SGU_CHUNK]
            cols.append(mdot(sw[g] * mask, blk, "nn", False))
        rows.append(jnp.concatenate(cols, axis=1) + bias)
    return (u * jnp.concatenate(rows, axis=0),)


def _softplus(x):
    return jnp.maximum(x, 0.0) + jnp.log1p(jnp.exp(-jnp.abs(x)))


def f_pre(q, wl, w0, al, a0, gl, k_k, k_a):
    qr, qk, qv, ql = q[:, :1024], q[:, 1024:2048], q[:, 2048:3072], q[:, 3072:]
    return _f_pre(qr, qk, qv, ql, wl, w0, al, a0, gl, k_k, k_a)


def _f_pre(qr, qk, qv, ql, wl, w0, al, a0, gl, k_k, k_a):
    xw, xa, xg = ql[:, :128], ql[:, 128:256], ql[:, 256:512]
    wr = -_softplus(-(w0 + mdot(jnp.tanh(xw), wl, "nn", False))) - 0.5
    lw = -jnp.exp(wr)
    aa = jax.nn.sigmoid(a0 + mdot(xa, al, "nn", False))
    g = mdot(jax.nn.sigmoid(xg), gl, "nn", False)
    kkr = qk * k_k
    kk = kkr / jnp.maximum(jnp.sqrt(segsum(kkr * kkr)), 1e-12)
    kp = qk * (1.0 + (aa - 1.0) * k_a)
    return qr, lw, kp, qv, -kk, kk * aa, g, qr, kp, qv


def f_post(o, r, kp, v, g, lnw, lnb, rk):
    mu = segsum(o) * (1.0 / HEAD)
    d = o - mu
    gn = d * lax.rsqrt(segsum(d * d) * (1.0 / HEAD) + GN_EPS)
    return ((gn * lnw + lnb + segsum(r * kp * rk) * v) * g,)


def f_mix(ya, yb, ga, gb):
    return (jax.nn.sigmoid(ga) * ya + jax.nn.sigmoid(gb) * yb,)


def f_ffn_in(h1, g):
    return _rms(h1, g), h1


def f_final(h1, m3, tgt, g):
    y = _rms(h1 + m3, g)
    err = jnp.square(y - tgt)
    return 0.5 * jnp.sum(jnp.mean(err, axis=-1))


def _cparams(n_grid):
    return pltpu.CompilerParams(dimension_semantics=("arbitrary",) * n_grid, vmem_limit_bytes=VMEM_LIMIT)


def _tile_spec(tm, w, cb):
    return pl.BlockSpec((tm, w), lambda i: (i, cb))


def _const_spec(c):
    nd = c.ndim
    return pl.BlockSpec(c.shape, lambda i: (0,) * nd)


def ew_call(fn, tiled, consts, outs, *, tm, name):
    t = tiled[0][0].shape[0]
    n_t, n_c = len(tiled), len(consts)

    def body(*refs):
        tv = [r[...].astype(F32) for r in refs[:n_t]]
        cv = [r[...] for r in refs[n_t:n_t + n_c]]
        res = fn(*tv, *cv)
        for o_ref, val in zip(refs[n_t + n_c:], res):
            o_ref[...] = val.astype(o_ref.dtype)

    return pl.pallas_call(
        body,
        name=name,
        grid=(t // tm,),
        in_specs=[_tile_spec(tm, w, cb) for _, w, cb in tiled] + [_const_spec(c) for c in consts],
        out_specs=[_tile_spec(tm, w, 0) for w, _ in outs],
        out_shape=[jax.ShapeDtypeStruct((t, w), dt) for w, dt in outs],
        compiler_params=_cparams(1),
    )(*[a for a, _, _ in tiled], *consts)


def ew_vjp_call(fn, tiled, consts, cots, d_tiled, d_consts, *, tm, name):
    t = tiled[0][0].shape[0]
    n_t, n_c, n_g = len(tiled), len(consts), len(cots)
    dt_list = [(i, dt) for i, dts in enumerate(d_tiled) for dt in dts]
    dc_list = [i for i, want in enumerate(d_consts) if want]

    def body(*refs):
        tv = [r[...].astype(F32) for r in refs[:n_t]]
        cv = [r[...] for r in refs[n_t:n_t + n_c]]
        gv = tuple(r[...].astype(F32) for r in refs[n_t + n_c:n_t + n_c + n_g])
        out_refs = refs[n_t + n_c + n_g:]
        _, vjp = jax.vjp(fn, *tv, *cv)
        grads = vjp(gv)
        for o_ref, (i, _) in zip(out_refs, dt_list):
            o_ref[...] = grads[i].astype(o_ref.dtype)
        acc_refs = out_refs[len(dt_list):]

        @pl.when(pl.program_id(0) == 0)
        def _():
            for a_ref in acc_refs:
                a_ref[...] = jnp.zeros_like(a_ref)

        for a_ref, i in zip(acc_refs, dc_list):
            a_ref[...] += grads[n_t + i]

    res = pl.pallas_call(
        body,
        name=name,
        grid=(t // tm,),
        in_specs=[_tile_spec(tm, w, cb) for _, w, cb in tiled] + [_const_spec(c) for c in consts]
        + [_tile_spec(tm, w, cb) for _, w, cb in cots],
        out_specs=[_tile_spec(tm, tiled[i][1], 0) for i, _ in dt_list] + [_const_spec(consts[i]) for i in dc_list],
        out_shape=[jax.ShapeDtypeStruct((t, tiled[i][1]), dt) for i, dt in dt_list]
        + [jax.ShapeDtypeStruct(consts[i].shape, F32) for i in dc_list],
        compiler_params=_cparams(1),
    )(*[a for a, _, _ in tiled], *consts, *[a for a, _, _ in cots])
    return res[:len(dt_list)], res[len(dt_list):]


def mm(a, b, mode, *, tm, tn, name, out_dtypes=(F32,), epi=None, extras=(), into=None):
    m = a.shape[1] if mode == "tn" else a.shape[0]
    kd = a.shape[0] if mode == "tn" else a.shape[1]
    n = b.shape[0] if mode == "nt" else b.shape[1]
    tm, tn = min(tm, m), min(tn, n)
    if mode == "nn":
        a_spec = pl.BlockSpec((tm, kd), lambda i, j: (i, 0))
        b_spec = pl.BlockSpec((kd, tn), lambda i, j: (0, j))
    elif mode == "nt":
        a_spec = pl.BlockSpec((tm, kd), lambda i, j: (i, 0))
        b_spec = pl.BlockSpec((tn, kd), lambda i, j: (j, 0))
    else:
        a_spec = pl.BlockSpec((kd, tm), lambda i, j: (0, i))
        b_spec = pl.BlockSpec((kd, tn), lambda i, j: (0, j))
    n_e = len(extras)
    o_spec = pl.BlockSpec((tm, tn), lambda i, j: (i, j))

    if into is not None:
        buf, place = into

        def body_into(a_ref, b_ref, buf_ref, o_ref):
            o_ref[0, 0] = lax.dot_general(a_ref[...].astype(BF16), b_ref[...].astype(BF16), _MDIMS[mode],
                                          preferred_element_type=F32)

        return pl.pallas_call(
            body_into,
            name=name,
            grid=(m // tm, n // tn),
            in_specs=[a_spec, b_spec, pl.BlockSpec(memory_space=pl.ANY)],
            out_specs=pl.BlockSpec((1, 1, tm, tn), lambda i, j: (*place(i, j), 0)),
            out_shape=jax.ShapeDtypeStruct(buf.shape, F32),
            input_output_aliases={2: 0},
            compiler_params=_cparams(2),
        )(a, b, buf)

    def body(a_ref, b_ref, *refs):
        c = lax.dot_general(a_ref[...].astype(BF16), b_ref[...].astype(BF16), _MDIMS[mode],
                            preferred_element_type=F32)
        res = epi(c, *[r[...] for r in refs[:n_e]]) if epi is not None else (c,)
        for o_ref, val in zip(refs[n_e:], res):
            o_ref[...] = val.astype(o_ref.dtype)

    res = pl.pallas_call(
        body,
        name=name,
        grid=(m // tm, n // tn),
        in_specs=[a_spec, b_spec] + [o_spec] * n_e,
        out_specs=[o_spec] * len(out_dtypes),
        out_shape=[jax.ShapeDtypeStruct((m, n), dt) for dt in out_dtypes],
        compiler_params=_cparams(2),
    )(a, b, *extras)
    return res if len(out_dtypes) > 1 else res[0]


RWKV_COL0 = 4096
RWKV_WIDTH = 3584
SHIFT_BLK = 512


def _shift_down(p, prev_row):
    rows = lax.broadcasted_iota(jnp.int32, p.shape, 0)
    return jnp.where(rows == 0, prev_row, pltpu.roll(p, 1, 0))


def shiftmix_fwd(p_all, sbp, *, tm):
    t = p_all.shape[0]
    tm = min(tm, t)
    c0 = RWKV_COL0 // SHIFT_BLK
    hb = tm // 8

    def body(p_ref, halo_ref, sb_ref, q_ref):
        p = p_ref[...]
        prev = jnp.where(pl.program_id(0) == 0, 0.0, halo_ref[7:8, :])
        q_ref[...] = p * sb_ref[0:1, :] + _shift_down(p, prev) * sb_ref[1:2, :]

    return pl.pallas_call(
        body,
        name="shiftmix_fwd",
        grid=(t // tm, RWKV_WIDTH // SHIFT_BLK),
        in_specs=[
            pl.BlockSpec((tm, SHIFT_BLK), lambda i, j: (i, c0 + j)),
            pl.BlockSpec((8, SHIFT_BLK), lambda i, j: (jnp.maximum(i * hb - 1, 0), c0 + j)),
            pl.BlockSpec((2, SHIFT_BLK), lambda i, j: (0, j)),
        ],
        out_specs=pl.BlockSpec((tm, SHIFT_BLK), lambda i, j: (i, j)),
        out_shape=jax.ShapeDtypeStruct((t, RWKV_WIDTH), F32),
        compiler_params=_cparams(2),
    )(p_all, p_all, sbp)


def shiftmix_bwd(dq, col0, p_all, sbp, *, tm, name):
    t, w = dq.shape
    n_i = t // tm
    hb = tm // 8
    cq = col0 // SHIFT_BLK
    cp = (RWKV_COL0 + col0) // SHIFT_BLK

    def body(dq_ref, dqn_ref, p_ref, ph_ref, sb_ref, dp_ref, dsb_ref):
        i = pl.program_id(1)
        dq_t = dq_ref[...]
        rows = lax.broadcasted_iota(jnp.int32, dq_t.shape, 0)
        nxt = jnp.where(i == n_i - 1, 0.0, dqn_ref[0:1, :])
        up = jnp.where(rows == tm - 1, nxt, pltpu.roll(dq_t, tm - 1, 0))
        dp_ref[...] = (dq_t * sb_ref[0:1, :] + up * sb_ref[1:2, :]).astype(dp_ref.dtype)
        p = p_ref[...]
        prev = jnp.where(i == 0, 0.0, ph_ref[7:8, :])
        s0 = jnp.sum(dq_t * p, axis=0, keepdims=True)
        s1 = jnp.sum(dq_t * _shift_down(p, prev), axis=0, keepdims=True)
        two = lax.broadcasted_iota(jnp.int32, (2, SHIFT_BLK), 0)

        @pl.when(i == 0)
        def _():
            dsb_ref[...] = jnp.zeros_like(dsb_ref)

        dsb_ref[...] += jnp.where(two == 0, s0, s1)

    return pl.pallas_call(
        body,
        name=name,
        grid=(w // SHIFT_BLK, n_i),
        in_specs=[
            pl.BlockSpec((tm, SHIFT_BLK), lambda j, i: (i, j)),
            pl.BlockSpec((8, SHIFT_BLK), lambda j, i: (jnp.minimum((i + 1) * hb, t // 8 - 1), j)),
            pl.BlockSpec((tm, SHIFT_BLK), lambda j, i: (i, cp + j)),
            pl.BlockSpec((8, SHIFT_BLK), lambda j, i: (jnp.maximum(i * hb - 1, 0), cp + j)),
            pl.BlockSpec((2, SHIFT_BLK), lambda j, i: (0, cq + j)),
        ],
        out_specs=[
            pl.BlockSpec((tm, SHIFT_BLK), lambda j, i: (i, j)),
            pl.BlockSpec((2, SHIFT_BLK), lambda j, i: (0, j)),
        ],
        out_shape=[jax.ShapeDtypeStruct((t, w), BF16), jax.ShapeDtypeStruct((2, w), F32)],
        compiler_params=_cparams(2),
    )(dq, dq, p_all, p_all, sbp)


def final_call(h1, m3, tgt, g_final, *, tm):
    t = h1.shape[0]

    def body(h1_ref, m3_ref, tgt_ref, g_ref, dh_ref, dhb_ref, dg_ref, loss_ref):
        loss, vjp = jax.vjp(f_final, h1_ref[...], m3_ref[...], tgt_ref[...], g_ref[...])
        dh, _, _, dg = vjp(jnp.ones((), F32))
        dh_ref[...] = dh
        dhb_ref[...] = dh.astype(BF16)

        @pl.when(pl.program_id(0) == 0)
        def _():
            dg_ref[...] = jnp.zeros_like(dg_ref)
            loss_ref[...] = jnp.zeros_like(loss_ref)

        dg_ref[...] += dg
        loss_ref[...] += jnp.full(loss_ref.shape, loss, F32)

    tile = _tile_spec(tm, D_MODEL, 0)
    return pl.pallas_call(
        body,
        name="final_loss",
        grid=(t // tm,),
        in_specs=[tile, tile, tile, _const_spec(g_final)],
        out_specs=[tile, tile, _const_spec(g_final), pl.BlockSpec((8, 128), lambda i: (0, 0))],
        out_shape=[jax.ShapeDtypeStruct((t, D_MODEL), F32), jax.ShapeDtypeStruct((t, D_MODEL), BF16),
                   jax.ShapeDtypeStruct(g_final.shape, F32), jax.ShapeDtypeStruct((8, 128), F32)],
        compiler_params=_cparams(1),
    )(h1, m3, tgt, g_final)


N_SGU = 2048
N_RWKV = 3360
LORA_W, LORA_A, LORA_G = 64, 64, 160


def _pad_rwkv_cols(z):
    zero = lambda n: jnp.zeros(z.shape[:-1] + (n,), z.dtype)
    return jnp.concatenate([z[..., :3072], z[..., 3072:3136], zero(64), z[..., 3136:3200], zero(64),
                            z[..., 3200:3360], zero(96)], axis=-1)


def _unpad_rwkv_cols(z):
    return jnp.concatenate([z[..., :3072], z[..., 3072:3136], z[..., 3200:3264], z[..., 3328:3488]], axis=-1)


def _pad_win_rows(wt):
    z = wt[N_SGU:N_SGU + N_RWKV]
    zero = lambda n: jnp.zeros((n, wt.shape[1]), wt.dtype)
    return jnp.concatenate([wt[:N_SGU], wt[N_SGU + N_RWKV:], z[:3072], z[3072:3136], zero(64), z[3136:3200], zero(64),
                            z[3200:3360], zero(96)], axis=0)


def _unpad_win_rows(wt):
    z = wt[RWKV_COL0:]
    return jnp.concatenate([wt[:N_SGU], z[:3072], z[3072:3136], z[3200:3264], z[3328:3488], wt[N_SGU:RWKV_COL0]],
                           axis=0)


def _pad_rows(w, n):
    return jnp.concatenate([w, jnp.zeros((n - w.shape[0],) + w.shape[1:], w.dtype)], axis=0)


def _relu2_epi(c):
    return c, jnp.square(jnp.maximum(c, 0.0))


def _relu2_bwd_epi(c, hid):
    return (c * (2.0 * jnp.maximum(hid.astype(F32), 0.0)),)


def _add_epi(c, x):
    return (c + x,)


def _pre_fwd(*args):
    res = f_pre(*args)
    return res[1], res[2], res[4], res[5], res[6]


def local_step(x, tgt, w, late_token, late_weights, pair_start, pair_finish, pack_early):
    d = D_MODEL
    win_pt = _pad_win_rows(w["w_in"])
    sbp = _pad_rwkv_cols(w["shift_b"])
    wl = _pad_rows(w["w_lora_w"], 128)
    al = _pad_rows(w["a_lora_w"], 128)
    gl = _pad_rows(w["g_lora_w"], 256)
    sbt = w["sgu_b"].T

    (a_bf,) = ew_call(lambda x_, g_: (f_norm_in(x_, g_)[0],), [(x, d, 0)], [w["g_mix"] + late_token[:1, :1]],
                      [(d, BF16)], tm=256, name="norm_in")
    p_all = mm(a_bf, win_pt, "nt", tm=2048, tn=1280, name="mm_in")
    sgu_t = [(p_all, 2 * d, 0)]
    sgu_c = [w["sgu_ln_w"], w["sgu_ln_b"], w["sgu_w"], sbt]
    (s_bf,) = ew_call(f_sgu, sgu_t, sgu_c, [(d, BF16)], tm=256, name="sgu_fwd")
    q = shiftmix_fwd(p_all, sbp, tm=1024)
    pre_t = [(q, RWKV_WIDTH, 0)]
    pre_c = [wl, w["w0"], al, w["a0"], gl, w["k_k"], w["k_a"]]
    lw, kp, na, nb, g = ew_call(_pre_fwd, pre_t, pre_c, [(d, F32)] * 5, tm=256, name="rwkv_pre_fwd")
    scan_ops = [(q, 0), (lw, 0), (kp, 0), (q, 2), (na, 0), (nb, 0)]
    o, s0s = scan_fwd(scan_ops)
    w = {**w, **late_weights(o)}
    ya = mm(s_bf, w["w_proj_a"], "nn", tm=512, tn=1024, name="mm_proj_a")
    post_t = [(o, d, 0), (q, d, 0), (kp, d, 0), (q, d, 2), (g, d, 0)]
    post_c = [w["ln_x_w"], w["ln_x_b"], w["r_k"]]
    (ob_bf,) = ew_call(f_post, post_t, post_c, [(d, BF16)], tm=256, name="rwkv_post_fwd")
    yb = mm(ob_bf, w["w_proj_b"], "nn", tm=512, tn=1024, name="mm_proj_b")
    mix_t = [(ya, d, 0), (yb, d, 0), (p_all, d, 2), (p_all, d, 3)]
    (mixed_bf,) = ew_call(f_mix, mix_t, [], [(d, BF16)], tm=256, name="mix_fwd")
    h1 = mm(mixed_bf, w["w_out"], "nn", tm=512, tn=1024, name="mm_out", epi=_add_epi, extras=(x,))
    (f_bf,) = ew_call(lambda h_, g_: (f_ffn_in(h_, g_)[0],), [(h1, d, 0)], [w["g_ffn"]], [(d, BF16)], tm=256,
                      name="ffn_norm")
    hid, act_bf = mm(f_bf, w["w_ffn1"], "nn", tm=2048, tn=1024, name="mm_ffn1", out_dtypes=(BF16, BF16), epi=_relu2_epi)
    m3 = mm(act_bf, w["w_ffn2"], "nn", tm=1024, tn=512, name="mm_ffn2")
    dh2, dh2_bf, dg_final, loss = final_call(h1, m3, tgt, w["g_final"], tm=256)

    dhid_bf = mm(dh2_bf, w["w_ffn2"], "nt", tm=2048, tn=1024, name="mm_dact", out_dtypes=(BF16,), epi=_relu2_bwd_epi,
                 extras=(hid,))
    late_g = lax.empty((N_CHIPS, 2, pack_rows(LATE), HALF_W), F32)
    late_g = mm(act_bf, dh2_bf, "tn", tm=1024, tn=HALF_W, name="mm_dw_ffn2",
                into=(late_g, lambda i, j: (i, j, PIECE_OFF["w_ffn2"] // 1024)))
    df = mm(dhid_bf, w["w_ffn1"], "nt", tm=1024, tn=512, name="mm_df")
    late_g = mm(f_bf, dhid_bf, "tn", tm=1024, tn=HALF_W, name="mm_dw_ffn1",
                into=(late_g, lambda i, j: (j // 2, j % 2, PIECE_OFF["w_ffn1"] // 1024)))
    (dh1, dh1_bf), (dg_ffn,) = ew_vjp_call(f_ffn_in, [(h1, d, 0)], [w["g_ffn"]], [(df, d, 0), (dh2, d, 0)],
                                           [(F32, BF16)], [True], tm=256, name="ffn_norm_bwd")
    dmixed = mm(dh1_bf, w["w_out"], "nt", tm=512, tn=1024, name="mm_dmixed")
    late_g = mm(mixed_bf, dh1_bf, "tn", tm=256, tn=HALF_W, name="mm_dw_out",
                into=(late_g, lambda i, j: (i, j, PIECE_OFF["w_out"] // 256)))
    (dya_bf, dyb_bf, dga_bf, dgb_bf), _ = ew_vjp_call(f_mix, mix_t, [], [(dmixed, d, 0)], [(BF16,)] * 4, [], tm=256,
                                                      name="mix_bwd")
    dob = mm(dyb_bf, w["w_proj_b"], "nt", tm=512, tn=1024, name="mm_dob")
    late_g = mm(ob_bf, dyb_bf, "tn", tm=256, tn=HALF_W, name="mm_dw_proj_b",
                into=(late_g, lambda i, j: (i, j, PIECE_OFF["w_proj_b"] // 256)))
    late_g = mm(s_bf, dya_bf, "tn", tm=256, tn=HALF_W, name="mm_dw_proj_a",
                into=(late_g, lambda i, j: (i, j, PIECE_OFF["w_proj_a"] // 256)))
    late_state, late_token = pair_start(late_g, "late")
    post_c_after = [w["ln_x_w"] + late_token[:1, :1]] + post_c[1:]
    (do, dr_p, dkp_p, dv_p, dg), (dlnx_w, dlnx_b, dr_k) = ew_vjp_call(
        f_post, post_t, post_c_after, [(dob, d, 0)], [(F32,)] * 5, [True] * 3, tm=256, name="rwkv_post_bwd")
    late_part, late_part16 = pair_finish(late_state, do, "late")
    *scan_g, late_slots = scan_bwd(scan_ops, s0s, do, late_part16)
    pre_g = [(z, d, 0) for z in scan_g] + [(dg, d, 0), (dr_p, d, 0), (dkp_p, d, 0), (dv_p, d, 0)]
    (dq,), (dwl, dw0, dal, da0, dgl, dk_k, dk_a) = ew_vjp_call(
        f_pre, pre_t, pre_c, pre_g, [(F32,)], [True] * 7, tm=256, name="rwkv_pre_bwd")
    dp_rwkv, dsb = shiftmix_bwd(dq, 0, p_all, sbp, tm=512, name="shiftmix_bwd")
    ds = mm(dya_bf, w["w_proj_a"], "nt", tm=512, tn=1024, name="mm_ds")
    (dp_sgu,), (dln_w, dln_b, dsw, dsbt) = ew_vjp_call(f_sgu, sgu_t, sgu_c, [(ds, d, 0)], [(BF16,)], [True] * 4,
                                                       tm=256, name="sgu_bwd")
    dp_all = jnp.concatenate([dp_sgu, dga_bf, dgb_bf, dp_rwkv], axis=1)
    d_in_pt = mm(dp_all, a_bf, "tn", tm=1280, tn=1024, name="mm_dw_in")
    early_state, early_token = pair_start(pack_early({
        "w_in": _unpad_win_rows(d_in_pt), "w_lora_w": dwl[:LORA_W], "a_lora_w": dal[:LORA_A],
        "g_lora_w": dgl[:LORA_G]}), "early")
    da = mm(dp_all, win_pt, "nn", tm=1024, tn=256, name="mm_da")
    g_mix_after = w["g_mix"] + early_token[:1, :1]
    (grad_x,), (dg_mix,) = ew_vjp_call(f_norm_in, [(x, d, 0)], [g_mix_after], [(da, d, 0), (dh1, d, 0)], [(F32,)],
                                       [True], tm=256, name="norm_in_bwd")

    grads = {
        "g_mix": dg_mix, "sgu_ln_w": dln_w, "sgu_ln_b": dln_b, "sgu_w": dsw, "sgu_b": dsbt.T,
        "shift_b": _unpad_rwkv_cols(dsb),
        "w0": dw0, "a0": da0, "k_k": dk_k, "k_a": dk_a, "r_k": dr_k, "ln_x_w": dlnx_w, "ln_x_b": dlnx_b,
        "g_ffn": dg_ffn, "g_final": dg_final,
    }
    return loss[0, 0], grad_x, grads, (late_part, late_slots), early_state


MESH = pl.DeviceIdType.MESH
N_CHIPS = 4
SMALL_ROWS = 160
_ANY = pl.BlockSpec(memory_space=pl.ANY)


def _coords():
    return lax.axis_index("x"), lax.axis_index("y"), lax.axis_index("c")


def _other_chips(x, y):
    return [(1 - x, y), (x, 1 - y), (1 - x, 1 - y)]


def _remote(src, dst, send_sems, recv_sems, k, to):
    return pltpu.make_async_remote_copy(src_ref=src, dst_ref=dst, send_sem=send_sems.at[k], recv_sem=recv_sems.at[k],
                                        device_id=to, device_id_type=MESH)


def gather_shards(pack):
    def body(src_ref, out_ref, token, send_sems, recv_sems):
        x, y, c = _coords()
        me = 2 * x + y
        sib = (x, y, 1 - c)
        chips = _other_chips(x, y)
        first = [_remote(src_ref.at[c], out_ref.at[me, c], send_sems, recv_sems, k, (cx, cy, c))
                 for k, (cx, cy) in enumerate(chips)]
        for cp in first:
            cp.start()
        passed = []
        for k, (cx, cy) in enumerate(chips):
            j = 2 * cx + cy
            _remote(src_ref.at[c], out_ref.at[j, c], send_sems, recv_sems, k, (cx, cy, c)).wait_recv()
            fwd = _remote(out_ref.at[j, c], out_ref.at[j, c], send_sems, recv_sems, 3 + k, sib)
            fwd.start()
            passed.append(fwd)
        for k, (cx, cy) in enumerate(chips):
            j = 2 * cx + cy
            _remote(out_ref.at[j, 1 - c], out_ref.at[j, 1 - c], send_sems, recv_sems, 3 + k, sib).wait_recv()
        for cp in first + passed:
            cp.wait_send()
        token[...] = jnp.zeros_like(token)

    return pl.pallas_call(
        body,
        name="gather_shards",
        in_specs=[_ANY],
        out_specs=[_ANY, pl.BlockSpec(memory_space=pltpu.VMEM)],
        out_shape=[jax.ShapeDtypeStruct((N_CHIPS,) + pack.shape, pack.dtype), jax.ShapeDtypeStruct((8, 128), F32)],
        scratch_shapes=[pltpu.SemaphoreType.DMA((6,)), pltpu.SemaphoreType.DMA((6,))],
    )(pack)


def _gather_copies(pack_ref, all_ref, send_sems, recv_sems):
    x, y, c = _coords()
    me = 2 * x + y
    return [(_remote(pack_ref.at[c], all_ref.at[me, c], send_sems, recv_sems, k, (cx, cy, c)),
             _remote(pack_ref.at[c], all_ref.at[2 * cx + cy, c], send_sems, recv_sems, k, (cx, cy, c)))
            for k, (cx, cy) in enumerate(_other_chips(x, y))]


_HBM = pl.BlockSpec(memory_space=pltpu.HBM)
_SEM = pl.BlockSpec(memory_space=pltpu.SEMAPHORE)
_SIDE_EFFECT = pltpu.SideEffectType.DATAFLOW_SIDE_EFFECTING


def split_start(name, copies, n, src, land_shape, after=None):
    def body(src_ref, land_ref, *refs):
        send_sems, recv_sems, token = refs[-5], refs[-4], refs[-1]
        for send, _ in copies(src_ref, land_ref, send_sems, recv_sems):
            send.start()
        token[...] = jnp.zeros_like(token)

    extra = () if after is None else (after,)
    *state, token = pl.pallas_call(
        body,
        name=name,
        out_shape=(pltpu.SemaphoreType.DMA((n,)), pltpu.SemaphoreType.DMA((n,)), pltpu.HBM(src.shape, src.dtype),
                   pltpu.HBM(land_shape, src.dtype), jax.ShapeDtypeStruct((8, 128), F32)),
        in_specs=(_HBM, _HBM) + (pl.BlockSpec(memory_space=pl.ANY),) * len(extra),
        out_specs=(_SEM, _SEM, _HBM, _HBM, pl.BlockSpec(memory_space=pltpu.VMEM)),
        input_output_aliases={0: 2, 1: 3},
        compiler_params=pltpu.CompilerParams(has_side_effects=_SIDE_EFFECT),
    )(pltpu.with_memory_space_constraint(src, pltpu.HBM),
      pltpu.with_memory_space_constraint(lax.empty(land_shape, src.dtype), pltpu.HBM), *extra)
    return state, token


def split_wait(name, copies, state, after):
    send_sems, recv_sems, src, land = state

    def body(src_ref, land_ref, send_sems, recv_sems, after_ref, src_out, land_out):
        for send, arrival in copies(src_ref, land_ref, send_sems, recv_sems):
            send.wait_send()
            arrival.wait_recv()

    return pl.pallas_call(
        body,
        name=name,
        out_shape=(pltpu.HBM(src.shape, src.dtype), pltpu.HBM(land.shape, land.dtype)),
        in_specs=(_HBM, _HBM, _SEM, _SEM, pl.BlockSpec(memory_space=pl.ANY)),
        out_specs=(_HBM, _HBM),
        input_output_aliases={0: 0, 1: 1},
        compiler_params=pltpu.CompilerParams(has_side_effects=_SIDE_EFFECT),
    )(src, land, send_sems, recv_sems, after)


def gather_forward(got):
    def body(got_ref, out_ref, send_sems, recv_sems):
        x, y, c = _coords()
        sib = (x, y, 1 - c)
        slots = [2 * cx + cy for cx, cy in _other_chips(x, y)]
        sends = [_remote(got_ref.at[j, c], out_ref.at[j, c], send_sems, recv_sems, k, sib) for k, j in enumerate(slots)]
        for cp in sends:
            cp.start()
        for k, j in enumerate(slots):
            _remote(got_ref.at[j, 1 - c], out_ref.at[j, 1 - c], send_sems, recv_sems, k, sib).wait_recv()
        for cp in sends:
            cp.wait_send()

    return pl.pallas_call(
        body,
        name="gather_forward",
        in_specs=[_ANY],
        out_specs=_ANY,
        out_shape=jax.ShapeDtypeStruct(got.shape, got.dtype),
        input_output_aliases={0: 0},
        scratch_shapes=[pltpu.SemaphoreType.DMA((3,)), pltpu.SemaphoreType.DMA((3,))],
    )(got)


def pair_sum(g, got, tag, *, tm):
    n, _, rows, width = g.shape

    def body(c_ref, own_ref, got_ref, out_ref, out16_ref):
        total = own_ref[0, 0] + got_ref[0]
        out_ref[0] = total
        out16_ref[0] = total.astype(BF16)

    blk = pl.BlockSpec((1, tm, width), lambda j, i, c_ref: (j, i, 0))
    return pl.pallas_call(
        body,
        name="pair_sum_" + tag,
        grid_spec=pltpu.PrefetchScalarGridSpec(
            num_scalar_prefetch=1,
            grid=(n, rows // tm),
            in_specs=[pl.BlockSpec((1, 1, tm, width), lambda j, i, c_ref: (j, c_ref[0], i, 0)), blk],
            out_specs=[blk, blk],
        ),
        out_shape=[jax.ShapeDtypeStruct(got.shape, F32), jax.ShapeDtypeStruct(got.shape, BF16)],
        compiler_params=_cparams(2),
    )(lax.axis_index("c").reshape(1).astype(jnp.int32), g, got)


def _pair_copies(g_ref, got_ref, send_sems, recv_sems):
    x, y, c = _coords()
    copies = [_remote(g_ref.at[j, 1 - c], got_ref.at[j], send_sems, recv_sems, j, (x, y, 1 - c))
              for j in range(N_CHIPS)]
    return [(cp, cp) for cp in copies]


def _chip_copies(p_ref, slots_ref, send_sems, recv_sems):
    x, y, c = _coords()
    me = 2 * x + y
    return [(_remote(p_ref.at[2 * cx + cy], slots_ref.at[me], send_sems, recv_sems, k, (cx, cy, c)),
             _remote(p_ref.at[me], slots_ref.at[2 * cx + cy], send_sems, recv_sems, k, (cx, cy, c)))
            for k, (cx, cy) in enumerate(_other_chips(x, y))]


def sum_with_own(own, slots, mine, after, *, tm, name):
    n, rows, width = slots.shape

    def body(mine_ref, own_ref, *refs):
        acc = None
        for s in range(n):
            term = jnp.where(mine_ref[0] == s, own_ref[0], refs[s][0].astype(F32))
            acc = term if acc is None else acc + term
        refs[-1][...] = acc

    return pl.pallas_call(
        body,
        name=name,
        grid_spec=pltpu.PrefetchScalarGridSpec(
            num_scalar_prefetch=1,
            grid=(rows // tm,),
            in_specs=[pl.BlockSpec((1, tm, width), lambda i, mine_ref: (mine_ref[0], i, 0))]
            + [pl.BlockSpec((1, tm, width), lambda i, mine_ref, s=s: (s, i, 0)) for s in range(n)]
            + [pl.BlockSpec(after.shape, lambda i, mine_ref: (0,) * after.ndim)],
            out_specs=pl.BlockSpec((tm, width), lambda i, mine_ref: (i, 0)),
        ),
        out_shape=jax.ShapeDtypeStruct((rows, width), F32),
        compiler_params=_cparams(1),
    )(mine.reshape(1).astype(jnp.int32), own, *([slots] * n), after)


def exchange_halves(s, tag):
    nq = 4
    rq = s.shape[0] // nq
    assert rq * nq == s.shape[0] and rq % 8 == 0

    def body(s_ref, out_ref, sbuf, rbuf, send_sems, recv_sems, in_sems, out_sems):
        x, y, c = _coords()
        sib = (x, y, 1 - c)
        rows = lambda q: pl.ds(q * rq, rq)
        loads = [pltpu.make_async_copy(s_ref.at[rows(q)], sbuf.at[rows(q)], in_sems.at[q]) for q in range(nq)]
        for cp in loads:
            cp.start()
        sends = []
        for q in range(nq):
            loads[q].wait()
            sends.append(_remote(sbuf.at[rows(q)], rbuf.at[rows(q)], send_sems, recv_sems, q, sib))
            sends[q].start()
        stores = []
        for q in range(nq):
            sends[q].wait_recv()
            stores.append(pltpu.make_async_copy(rbuf.at[rows(q)], out_ref.at[rows(q)], out_sems.at[q]))
            stores[q].start()
        for cp in sends:
            cp.wait_send()
        for cp in stores:
            cp.wait()

    return pl.pallas_call(
        body,
        name="exchange_halves_" + tag,
        in_specs=[_ANY],
        out_specs=_ANY,
        out_shape=jax.ShapeDtypeStruct(s.shape, s.dtype),
        scratch_shapes=[pltpu.VMEM(s.shape, s.dtype), pltpu.VMEM(s.shape, s.dtype)]
        + [pltpu.SemaphoreType.DMA((nq,))] * 4,
        compiler_params=pltpu.CompilerParams(vmem_limit_bytes=VMEM_LIMIT),
    )(s)


def sum_all(s, after):
    rows = s.shape[0]
    half = rows // 2

    def body(s_ref, after_ref, out_ref, theirs, pair, slots, send_sems, recv_sems):
        x, y, c = _coords()
        me = 2 * x + y
        sib = (x, y, 1 - c)
        chips = _other_chips(x, y)
        swap = _remote(s_ref, theirs, send_sems, recv_sems, 0, sib)
        swap.start()
        swap.wait_recv()
        pair[...] = s_ref[...] + theirs[...]
        mine = pl.ds(pl.multiple_of(c * half, 8), half)
        other = pl.ds(pl.multiple_of((1 - c) * half, 8), half)
        sends = [_remote(pair.at[mine], slots.at[me], send_sems, recv_sems, 1 + k, (cx, cy, c))
                 for k, (cx, cy) in enumerate(chips)]
        for cp in sends:
            cp.start()
        for k, (cx, cy) in enumerate(chips):
            _remote(pair.at[mine], slots.at[2 * cx + cy], send_sems, recv_sems, 1 + k, (cx, cy, c)).wait_recv()
        slots[me] = pair[mine]
        out_ref[mine] = ((slots[0] + slots[1]) + slots[2]) + slots[3]
        last = _remote(out_ref.at[mine], out_ref.at[mine], send_sems, recv_sems, 4, sib)
        last.start()
        _remote(out_ref.at[other], out_ref.at[other], send_sems, recv_sems, 4, sib).wait_recv()
        for cp in [swap] + sends + [last]:
            cp.wait_send()

    vmem = pl.BlockSpec(memory_space=pltpu.VMEM)
    return pl.pallas_call(
        body,
        name="sum_all",
        in_specs=[vmem, vmem],
        out_specs=vmem,
        out_shape=jax.ShapeDtypeStruct(s.shape, s.dtype),
        scratch_shapes=[pltpu.VMEM(s.shape, s.dtype), pltpu.VMEM(s.shape, s.dtype),
                        pltpu.VMEM((N_CHIPS, half, s.shape[1]), s.dtype), pltpu.SemaphoreType.DMA((5,)),
                        pltpu.SemaphoreType.DMA((5,))],
        compiler_params=pltpu.CompilerParams(vmem_limit_bytes=VMEM_LIMIT),
    )(s, after)


ADAM_LR = 0.001
ADAM_B1 = 0.9
ADAM_B2 = 0.999
ADAM_EPS = 1e-08
ADAM_WD = 0.01
ADAM_STEP = 10


def f_adamw(g, w, m, v):
    m = ADAM_B1 * m + (1.0 - ADAM_B1) * g
    v = ADAM_B2 * v + (1.0 - ADAM_B2) * jnp.square(g)
    m_hat = m / (1.0 - ADAM_B1 ** ADAM_STEP)
    v_hat = v / (1.0 - ADAM_B2 ** ADAM_STEP)
    delta = -ADAM_LR * (m_hat / (jnp.sqrt(v_hat) + ADAM_EPS) + ADAM_WD * w)
    return delta, m, v


def adamw_many(gs, ws, ms, vs):
    n = len(gs)

    def body(*refs):
        ins, outs = refs[:4 * n], refs[4 * n:]
        for i in range(n):
            delta, nm, nv = f_adamw(ins[i][...], ins[n + i][...], ins[2 * n + i][...], ins[3 * n + i][...])
            outs[i][...] = delta
            outs[n + i][...] = nm
            outs[2 * n + i][...] = nv

    vmem = pl.BlockSpec(memory_space=pltpu.VMEM)
    res = pl.pallas_call(
        body,
        name="adamw_small",
        in_specs=[vmem] * (4 * n),
        out_specs=[vmem] * (3 * n),
        out_shape=[jax.ShapeDtypeStruct(w.shape, F32) for w in ws] * 3,
    )(*gs, *ws, *ms, *vs)
    return res[:n], res[n:2 * n], res[2 * n:]


EARLY = ["w_in", "w_lora_w", "a_lora_w", "g_lora_w"]
LATE = ["w_ffn1", "w_ffn2", "w_proj_b", "w_out", "w_proj_a"]
LORAS = ["w_lora_w", "a_lora_w", "g_lora_w"]
HALF_W = 512
PIECE_ROWS = {"w_in": 1864, "w_ffn1": 1024, "w_ffn2": 1024, "w_proj_a": 256, "w_proj_b": 256, "w_out": 256,
              "w_lora_w": 32, "a_lora_w": 32, "g_lora_w": 80}
PIECE_OFF = {"w_in": 0, "w_lora_w": 1920, "a_lora_w": 1952, "g_lora_w": 2000,
             "w_ffn1": 0, "w_ffn2": 1024, "w_proj_b": 2048, "w_out": 2304, "w_proj_a": 2560}
LO_OFF = 2080


def pack_rows(group):
    return 2304 if group is EARLY else 2816
SHARD_AXIS = {"w_in": 1, "w_proj_a": 0, "w_lora_w": 1, "a_lora_w": 1, "g_lora_w": 1, "w_proj_b": 0, "w_out": 0,
              "w_ffn1": 1, "w_ffn2": 0}
SHARD_SHAPE = {"w_in": (1024, 1864), "w_proj_a": (256, 1024), "w_lora_w": (64, 256), "a_lora_w": (64, 256),
               "g_lora_w": (160, 256), "w_proj_b": (256, 1024), "w_out": (256, 1024), "w_ffn1": (1024, 1024),
               "w_ffn2": (1024, 1024)}
SHIFT_SHARD = (2, 840)
VECTORS = ["g_mix", "sgu_ln_w", "sgu_ln_b", "w0", "a0", "k_k", "k_a", "r_k", "ln_x_w", "ln_x_b", "g_ffn", "g_final"]
SMALL = VECTORS + ["sgu_w", "sgu_b"]
SMALL_SHAPE = {**{n: (1, 1024) for n in VECTORS}, "sgu_w": (8, 128, 128), "sgu_b": (8, 128)}
WEIGHTS = ["g_mix", "w_in", "sgu_ln_w", "sgu_ln_b", "sgu_w", "sgu_b", "w_proj_a", "shift_b", "w_lora_w", "w0",
           "a_lora_w", "a0", "g_lora_w", "k_k", "k_a", "r_k", "ln_x_w", "ln_x_b", "w_proj_b", "w_out", "g_ffn",
           "w_ffn1", "w_ffn2", "g_final"]


def _size(shape):
    n = 1
    for s in shape:
        n *= s
    return n


def _pack_rows(parts, rows, dtype):
    flat = jnp.concatenate([p.reshape(-1).astype(dtype) for p in parts])
    return jnp.concatenate([flat, jnp.zeros((rows * 1024 - flat.shape[0],), dtype)]).reshape(rows, 1024)


def _unpack_rows(packed, shapes):
    flat = packed.reshape(-1)
    out, off = [], 0
    for shp in shapes:
        out.append(flat[off:off + _size(shp)].reshape(shp))
        off += _size(shp)
    return out


def _shard_of(name, full, j):
    ax = SHARD_AXIS[name]
    n = SHARD_SHAPE[name][ax]
    return lax.slice_in_dim(full, j * n, (j + 1) * n, axis=ax)


def _pad_cols(z, n):
    return jnp.concatenate([z, jnp.zeros((z.shape[0], n - z.shape[1]), z.dtype)], axis=1)


def _row_form(name, s):
    return s.T if name == "w_in" else s


def _half_piece(name, rf, h):
    if name in LORAS:
        r = PIECE_ROWS[name]
        return _pad_cols(rf[h * r:(h + 1) * r], HALF_W)
    return rf[:, HALF_W * h:HALF_W * (h + 1)]


def _pack_half(group, rf_fn, h, dtype, tail=()):
    parts, pos, rows = [], 0, pack_rows(group)
    for n in group:
        if PIECE_OFF[n] > pos:
            parts.append(jnp.zeros((PIECE_OFF[n] - pos, HALF_W), dtype))
        parts.append(_half_piece(n, rf_fn(n), h).astype(dtype))
        pos = PIECE_OFF[n] + PIECE_ROWS[n]
    for t in tail:
        parts.append(t)
        pos += t.shape[0]
    parts.append(jnp.zeros((rows - pos, HALF_W), dtype))
    return jnp.concatenate(parts, axis=0)


def _piece(pack, name):
    return pack[PIECE_OFF[name]:PIECE_OFF[name] + PIECE_ROWS[name]]


def _join_halves(name, p0, p1):
    if name in LORAS:
        return jnp.concatenate([p0[:, :SHARD_SHAPE[name][1]], p1[:, :SHARD_SHAPE[name][1]]], axis=0)
    return jnp.concatenate([p0, p1], axis=1)


def _grad_row_form(name, full, j):
    if name == "w_in":
        return full[SHARD_SHAPE[name][1] * j:SHARD_SHAPE[name][1] * (j + 1)]
    return _shard_of(name, full, j)


def adamw_weight(name, g_own, g_other, w, m, v):
    rows, width = w.shape
    if name in LORAS:
        tm = PIECE_ROWS[name]
        grid = (2, 1)
        native = pl.BlockSpec((tm, width), lambda h, i: (h, 0))
    elif name == "w_in":
        tm, lanes = rows, 256
        grid = (2, HALF_W // lanes)
        native = pl.BlockSpec((tm, lanes), lambda h, i: (0, h * (HALF_W // lanes) + i))
    else:
        tm = rows
        grid = (2, 1)
        native = pl.BlockSpec((tm, HALF_W), lambda h, i: (i, h))
    assert PIECE_OFF[name] % tm == 0
    off = PIECE_OFF[name] // tm
    if name == "w_in":
        packed = pl.BlockSpec((tm, lanes), lambda h, i: (0, i))
    else:
        packed = pl.BlockSpec((tm, HALF_W), lambda h, i: (off + i, 0))

    def body(go_ref, gx_ref, w_ref, m_ref, v_ref, g_ref, d_ref, nm_ref, nv_ref):
        g = jnp.where(pl.program_id(0) == lax.axis_index("c"), go_ref[...], gx_ref[...])[:, :w_ref.shape[1]]
        delta, nm, nv = f_adamw(g, w_ref[...], m_ref[...], v_ref[...])
        g_ref[...] = g
        d_ref[...] = delta
        nm_ref[...] = nm
        nv_ref[...] = nv

    return pl.pallas_call(
        body,
        name="adamw_" + name,
        grid=grid,
        in_specs=[packed, packed, native, native, native],
        out_specs=[native] * 4,
        out_shape=[jax.ShapeDtypeStruct(w.shape, F32)] * 4,
        compiler_params=_cparams(2),
    )(g_own, g_other, w, m, v)


def kernel(x, g_mix, w_in, sgu_ln_w, sgu_ln_b, sgu_w, sgu_b, w_proj_a, shift_b, w_lora_w, w0, a_lora_w, a0, g_lora_w, k_k, k_a, r_k, ln_x_w, ln_x_b, w_proj_b, w_out, g_ffn, w_ffn1, w_ffn2, g_final, loss_target, m_g_mix, m_w_in, m_sgu_ln_w, m_sgu_ln_b, m_sgu_w, m_sgu_b, m_w_proj_a, m_shift_b, m_w_lora_w, m_w0, m_a_lora_w, m_a0, m_g_lora_w, m_k_k, m_k_a, m_r_k, m_ln_x_w, m_ln_x_b, m_w_proj_b, m_w_out, m_g_ffn, m_w_ffn1, m_w_ffn2, m_g_final, v_g_mix, v_w_in, v_sgu_ln_w, v_sgu_ln_b, v_sgu_w, v_sgu_b, v_w_proj_a, v_shift_b, v_w_lora_w, v_w0, v_a_lora_w, v_a0, v_g_lora_w, v_k_k, v_k_a, v_r_k, v_ln_x_w, v_ln_x_b, v_w_proj_b, v_w_out, v_g_ffn, v_w_ffn1, v_w_ffn2, v_g_final):
    given = dict(zip(WEIGHTS, (g_mix, w_in, sgu_ln_w, sgu_ln_b, sgu_w, sgu_b, w_proj_a, shift_b, w_lora_w, w0, a_lora_w, a0, g_lora_w, k_k, k_a, r_k, ln_x_w, ln_x_b, w_proj_b, w_out, g_ffn, w_ffn1, w_ffn2, g_final)))
    mom_m = dict(zip(WEIGHTS, (m_g_mix, m_w_in, m_sgu_ln_w, m_sgu_ln_b, m_sgu_w, m_sgu_b, m_w_proj_a, m_shift_b, m_w_lora_w, m_w0, m_a_lora_w, m_a0, m_g_lora_w, m_k_k, m_k_a, m_r_k, m_ln_x_w, m_ln_x_b, m_w_proj_b, m_w_out, m_g_ffn, m_w_ffn1, m_w_ffn2, m_g_final)))
    mom_v = dict(zip(WEIGHTS, (v_g_mix, v_w_in, v_sgu_ln_w, v_sgu_ln_b, v_sgu_w, v_sgu_b, v_w_proj_a, v_shift_b, v_w_lora_w, v_w0, v_a_lora_w, v_a0, v_g_lora_w, v_k_k, v_k_a, v_r_k, v_ln_x_w, v_ln_x_b, v_w_proj_b, v_w_out, v_g_ffn, v_w_ffn1, v_w_ffn2, v_g_final)))
    chip = 2 * lax.axis_index("x") + lax.axis_index("y")

    def local_block(tree, n):
        return tree[n] if n == "g_final" else tree[n][0]

    sb = local_block(given, "shift_b")
    lo_part = lambda z: (z - z.astype(BF16).astype(F32)).astype(BF16)
    row_form = lambda tree: (lambda n: _row_form(n, local_block(tree, n)))
    tile16 = lambda z: jnp.pad(z, ((0, 16 - z.shape[0]), (0, HALF_W - z.shape[1])))
    sb_tiles = [tile16(f(sb[:, lanes])) for f in (lambda z: z.astype(BF16), lo_part)
                for lanes in (slice(0, HALF_W), slice(HALF_W, None))]
    tails = [[_half_piece(n, lo_part(local_block(given, n)), h) for n in LORAS] + sb_tiles for h in range(2)]
    pack_w = jnp.stack([_pack_half(EARLY, row_form(given), h, BF16, tails[h]) for h in range(2)])
    gathered, gathered_token = gather_shards(pack_w)
    gathered = lax.dynamic_update_index_in_dim(gathered, pack_w, chip, 0)
    pack_late = jnp.stack([_pack_half(LATE, row_form(given), h, BF16) for h in range(2)])
    late_state, late_token = split_start("gather_start", _gather_copies, 3, pack_late, (N_CHIPS,) + pack_late.shape,
                                         gathered_token)

    def whole(group, got, own):
        half = lambda n, j, h: jnp.where(chip == j, _piece(own[h], n), _piece(got[j, h], n))
        shard = lambda n, j: _join_halves(n, half(n, j, 0), half(n, j, 1))
        return {n: jnp.concatenate([shard(n, j) for j in range(N_CHIPS)],
                                   axis=0 if n == "w_in" else SHARD_AXIS[n]) for n in group}

    w = whole(EARLY, gathered, pack_w)
    late_weights = lambda after: whole(
        LATE, gather_forward(split_wait("gather_wait", _gather_copies, late_state, after)[1]), pack_late)
    off = LO_OFF
    for n in LORAS:
        r, cols = PIECE_ROWS[n], SHARD_SHAPE[n][1]
        lo = jnp.concatenate([jnp.concatenate([gathered[j, 0, off:off + r, :cols], gathered[j, 1, off:off + r, :cols]],
                                              axis=0) for j in range(N_CHIPS)], axis=1)
        w[n] = w[n].astype(F32) + lo.astype(F32)
        off += r
    sb_tile = lambda j, t, lanes: gathered[j, 0, off + 16 * t:off + 16 * t + 2, :lanes].astype(F32)
    rest = SHIFT_SHARD[1] - HALF_W
    w["shift_b"] = jnp.concatenate(
        [jnp.concatenate([sb_tile(j, 0, HALF_W) + sb_tile(j, 2, HALF_W), sb_tile(j, 1, rest) + sb_tile(j, 3, rest)],
                         axis=1) for j in range(N_CHIPS)], axis=1)
    for n in SMALL:
        w[n] = local_block(given, n).reshape(SMALL_SHAPE[n])

    def pair_start(g_pack, tag):
        return split_start("reduce_pair_start_" + tag, _pair_copies, N_CHIPS, g_pack, (N_CHIPS,) + g_pack.shape[2:])

    def pair_finish(state, after, tag):
        g_pack, got = split_wait("reduce_pair_wait_" + tag, _pair_copies, state, after)
        return pair_sum(g_pack, got, tag, tm=got.shape[1] // 2)

    pack_early = lambda g: jnp.stack([jnp.stack([_pack_half(EARLY, lambda n: _grad_row_form(n, g[n], j), h, F32)
                                                 for h in range(2)]) for j in range(N_CHIPS)])
    loss, grad_x, grads, (late_part, late_slots), early_state = local_step(
        x[0], loss_target[0], w, late_token, late_weights, pair_start, pair_finish, pack_early)

    early_part, early_part16 = pair_finish(early_state, grad_x, "early")
    s_pack = _pack_rows([grads[n] for n in SMALL] + [grads["shift_b"], loss.reshape(1, 1)], SMALL_ROWS, F32)
    chips_state, token = split_start("reduce_chips_start", _chip_copies, 3, early_part16, early_part16.shape)
    out_g, out_d, out_m, out_v = {}, {}, {}, {}

    def finish(group, tag, part, slots):
        half_sum = sum_with_own(part, slots, chip, token, tm=part.shape[1] // 2, name="chip_sum_" + tag)
        other_half = exchange_halves(half_sum, tag)
        for n in group:
            res = adamw_weight(n, half_sum, other_half,
                               *[_row_form(n, local_block(t, n)) for t in (given, mom_m, mom_v)])
            for tree, z in zip((out_g, out_d, out_m, out_v), res):
                tree[n] = _row_form(n, z)

    finish(LATE, "late", late_part, late_slots)

    small_shapes = [SMALL_SHAPE[n] for n in SMALL]
    g_small = sum_all(s_pack, token)
    *g_parts, loss = _unpack_rows(g_small, small_shapes + [(2, N_RWKV), ()])
    out_g.update(zip(SMALL, g_parts[:-1]))
    g_sb = lax.dynamic_slice_in_dim(g_parts[-1], chip * SHIFT_SHARD[1], SHIFT_SHARD[1], axis=1)
    out_g["shift_b"] = g_sb
    names = SMALL + ["shift_b"]
    native = lambda tree: [local_block(tree, n).reshape(SMALL_SHAPE.get(n, SHIFT_SHARD)) for n in names]
    small_res = adamw_many(g_parts[:-1] + [g_sb], native(given), native(mom_m), native(mom_v))
    for tree, res in zip((out_d, out_m, out_v), small_res):
        tree.update(zip(names, res))

    after = (out_v["w_out"], out_v["sgu_w"])
    early_slots = split_wait("reduce_chips_wait", _chip_copies, chips_state,
                             jnp.concatenate([z.reshape(-1)[:8] for z in after]))[1]
    finish(EARLY, "early", early_part, early_slots)

    def block_of(tree, n):
        return tree[n].reshape(given[n].shape)

    return (loss, grad_x[None], *[block_of(out_g, n) for n in WEIGHTS], *[block_of(out_d, n) for n in WEIGHTS],
            *[block_of(out_m, n) for n in WEIGHTS], *[block_of(out_v, n) for n in WEIGHTS])
```

```python
import functools

import jax
import jax.numpy as jnp
from jax import lax
from jax.experimental import pallas as pl
from jax.experimental.pallas import tpu as pltpu

F32 = jnp.float32
BF16 = jnp.bfloat16

D_MODEL = 1024
N_HEADS = 16
HEAD = 64
SCAN_CHUNK = 64

VMEM_LIMIT = 56 * 1024 * 1024


_BDIMS = {
    "nn": (((2,), (1,)), ((0,), (0,))),
    "nt": (((2,), (2,)), ((0,), (0,))),
    "tn": (((1,), (1,)), ((0,), (0,))),
}


def _raw_bdot(x, y, mode, fine):
    if fine:
        return lax.dot_general(x, y, _BDIMS[mode], precision=lax.Precision.HIGH, preferred_element_type=F32)
    return lax.dot_general(x.astype(BF16), y.astype(BF16), _BDIMS[mode], preferred_element_type=F32)


@functools.partial(jax.custom_vjp, nondiff_argnums=(2, 3))
def bdot(x, y, mode, fine=True):
    return _raw_bdot(x, y, mode, fine)


def _bdot_fwd(x, y, mode, fine):
    return _raw_bdot(x, y, mode, fine), (x, y)


def _bdot_bwd(mode, fine, res, g):
    x, y = res
    if mode == "nn":
        return bdot(g, y, "nt", fine), bdot(x, g, "tn", fine)
    if mode == "nt":
        return bdot(g, y, "nn", fine), bdot(g, x, "tn", fine)
    return bdot(y, g, "nt", fine), bdot(x, g, "nn", fine)


bdot.defvjp(_bdot_fwd, _bdot_bwd)


def _scan_chunk(S0, r, lw, k, v, a, b):
    nh, lc, _ = r.shape
    ti = lax.broadcasted_iota(jnp.int32, (lc, lc), 0)
    si = lax.broadcasted_iota(jnp.int32, (lc, lc), 1)
    incl = (si <= ti).astype(F32)
    strict = (si < ti).astype(F32)
    eye = (si == ti).astype(F32)
    cl = bdot(jnp.broadcast_to(incl, (nh, lc, lc)), lw, "nn")
    cl_last = cl[:, lc - 1:lc, :]
    g_last = jnp.exp(cl_last - cl)
    at = a * jnp.exp(cl - lw)
    bt = b * jnp.exp(-cl)
    kt = k * jnp.exp(-cl)
    rt = r * jnp.exp(cl)
    ar = jnp.concatenate([at, rt], axis=1)
    ar_b = bdot(ar, bt, "nt", False)
    ar_k = bdot(ar, kt, "nt", False)
    m_ab, m_rb = ar_b[:, :lc] * strict, ar_b[:, lc:] * incl
    m_ak, m_rk = ar_k[:, :lc] * strict, ar_k[:, lc:] * incl
    x = eye + m_ab
    p = bdot(m_ab, m_ab, "nn", False)
    n = 2
    while n * 2 < lc:
        px = bdot(jnp.concatenate([p, x], axis=1), p, "nn", False)
        p = px[:, :lc]
        x = x + px[:, lc:]
        n *= 2
    x = x + bdot(x, p, "nn", False)
    ar_s = bdot(ar, S0, "nt", False)
    akrk_v = bdot(jnp.concatenate([m_ak, m_rk], axis=1), v, "nn", False)
    u = bdot(x, ar_s[:, :lc] + akrk_v[:, :lc], "nn", False)
    o = ar_s[:, lc:] + bdot(m_rb, u, "nn", False) + akrk_v[:, lc:]
    s_last = S0 * jnp.exp(cl_last) + bdot(jnp.concatenate([u, v], axis=1),
                                          jnp.concatenate([b * g_last, k * g_last], axis=1), "tn", False)
    return o, s_last


def _split_heads(z):
    return jnp.stack([z[:, HEAD * h:HEAD * (h + 1)] for h in range(N_HEADS)], axis=0)


def _merge_heads(z):
    return jnp.concatenate([z[h] for h in range(N_HEADS)], axis=1)


def _scan_specs(t, ops, rev):
    nc = t // SCAN_CHUNK
    row = (lambda c: nc - 1 - c) if rev else (lambda c: c)
    specs = [pl.BlockSpec((SCAN_CHUNK, D_MODEL), lambda c, cb=cb: (row(c), cb)) for _, cb in ops]
    state = pl.BlockSpec((1, N_HEADS, HEAD, HEAD), lambda c: (row(c), 0, 0, 0))
    return nc, specs, state


def scan_fwd(ops):
    t = ops[0][0].shape[0]
    nc, specs, state = _scan_specs(t, ops, False)

    def body(r_ref, lw_ref, k_ref, v_ref, a_ref, b_ref, o_ref, s0_ref, s_scr):
        @pl.when(pl.program_id(0) == 0)
        def _():
            s_scr[...] = jnp.zeros_like(s_scr)

        s0 = s_scr[...]
        s0_ref[0] = s0
        o, s_last = _scan_chunk(s0, *[_split_heads(z[...]) for z in (r_ref, lw_ref, k_ref, v_ref, a_ref, b_ref)])
        o_ref[...] = _merge_heads(o)
        s_scr[...] = s_last

    return pl.pallas_call(
        body,
        name="scan_fwd",
        grid=(nc,),
        in_specs=specs,
        out_specs=[pl.BlockSpec((SCAN_CHUNK, D_MODEL), lambda c: (c, 0)), state],
        out_shape=[jax.ShapeDtypeStruct((t, D_MODEL), F32), jax.ShapeDtypeStruct((nc, N_HEADS, HEAD, HEAD), F32)],
        scratch_shapes=[pltpu.VMEM((N_HEADS, HEAD, HEAD), F32)],
        compiler_params=_cparams(1),
    )(*[a for a, _ in ops])


def scan_bwd(ops, s0s, do, part):
    t = ops[0][0].shape[0]
    nc, specs, state = _scan_specs(t, ops + [(do, 0)], True)

    def body(r_ref, lw_ref, k_ref, v_ref, a_ref, b_ref, do_ref, s0_ref, part_ref, *rest):
        out_refs, slots_ref, ds_scr, send_sems, recv_sems = rest[:6], rest[6], rest[7], rest[8], rest[9]
        step = pl.program_id(0)
        x, y, c = _coords()
        me = 2 * x + y
        chips = _other_chips(x, y)
        sends = [_remote(part_ref.at[2 * cx + cy], slots_ref.at[me], send_sems, recv_sems, k, (cx, cy, c))
                 for k, (cx, cy) in enumerate(chips)]

        @pl.when(step == 0)
        def _():
            ds_scr[...] = jnp.zeros_like(ds_scr)
            for cp in sends:
                cp.start()

        _, vjp = jax.vjp(_scan_chunk, s0_ref[0],
                         *[_split_heads(z[...]) for z in (r_ref, lw_ref, k_ref, v_ref, a_ref, b_ref)])
        grads = vjp((_split_heads(do_ref[...]), ds_scr[...]))
        for o_ref, g in zip(out_refs, grads[1:]):
            o_ref[...] = _merge_heads(g)
        ds_scr[...] = grads[0]

        @pl.when(step == nc - 1)
        def _():
            for k, (cx, cy) in enumerate(chips):
                _remote(part_ref.at[me], slots_ref.at[2 * cx + cy], send_sems, recv_sems, k, (cx, cy, c)).wait_recv()
            for cp in sends:
                cp.wait_send()

    return pl.pallas_call(
        body,
        name="scan_bwd",
        grid=(nc,),
        in_specs=specs + [state, _ANY],
        out_specs=[pl.BlockSpec((SCAN_CHUNK, D_MODEL), lambda c: (nc - 1 - c, 0))] * 6 + [_ANY],
        out_shape=[jax.ShapeDtypeStruct((t, D_MODEL), F32)] * 6 + [jax.ShapeDtypeStruct(part.shape, part.dtype)],
        scratch_shapes=[pltpu.VMEM((N_HEADS, HEAD, HEAD), F32), pltpu.SemaphoreType.DMA((3,)),
                        pltpu.SemaphoreType.DMA((3,))],
        compiler_params=_cparams(1),
    )(*[a for a, _ in ops], do, s0s, part)


_MDIMS = {
    "nn": (((1,), (0,)), ((), ())),
    "nt": (((1,), (1,)), ((), ())),
    "tn": (((0,), (0,)), ((), ())),
}


def _raw_mdot(x, y, mode, exact):
    if exact:
        return lax.dot_general(x, y, _MDIMS[mode], precision=lax.Precision.HIGH, preferred_element_type=F32)
    return lax.dot_general(x.astype(BF16), y.astype(BF16), _MDIMS[mode], preferred_element_type=F32)


@functools.partial(jax.custom_vjp, nondiff_argnums=(2, 3))
def mdot(x, y, mode, exact):
    return _raw_mdot(x, y, mode, exact)


def _mdot_fwd(x, y, mode, exact):
    return _raw_mdot(x, y, mode, exact), (x, y)


def _mdot_bwd(mode, exact, res, g):
    x, y = res
    if mode == "nn":
        return mdot(g, y, "nt", exact), mdot(x, g, "tn", exact)
    if mode == "nt":
        return mdot(g, y, "nn", exact), mdot(g, x, "tn", exact)
    return mdot(y, g, "nt", exact), mdot(x, g, "nn", exact)


mdot.defvjp(_mdot_fwd, _mdot_bwd)


def _seg_ones():
    i = lax.broadcasted_iota(jnp.int32, (256, 256), 0) // HEAD
    j = lax.broadcasted_iota(jnp.int32, (256, 256), 1) // HEAD
    return (i == j).astype(BF16)


@jax.custom_vjp
def segsum(x):
    bd = _seg_ones()
    hi = x.astype(BF16)
    lo = (x - hi.astype(F32)).astype(BF16)
    cols = []
    for j in range(x.shape[1] // 256):
        sl = slice(256 * j, 256 * (j + 1))
        cols.append(jnp.dot(hi[:, sl], bd, preferred_element_type=F32)
                    + jnp.dot(lo[:, sl], bd, preferred_element_type=F32))
    return jnp.concatenate(cols, axis=1)


segsum.defvjp(lambda x: (segsum(x), None), lambda _, g: (segsum(g),))


NORM_EPS = 1e-6
LN_EPS = 1e-5
GN_EPS = 64e-5
SGU_CHUNK = 128
SGU_GROUPS = 8


def _rms(x, g):
    return x * lax.rsqrt(jnp.mean(x * x, axis=-1, keepdims=True) + NORM_EPS) * g


def f_norm_in(x, g):
    return _rms(x, g), x


def f_sgu(p, ln_w, ln_b, sw, sbt):
    tm = p.shape[0]
    z = 0.5 * p * (1.0 + lax.erf(p * 0.7071067811865476))
    u, v = z[:, :D_MODEL], z[:, D_MODEL:]
    mu = jnp.mean(v, axis=-1, keepdims=True)
    d = v - mu
    vn = d * lax.rsqrt(jnp.mean(d * d, axis=-1, keepdims=True) + LN_EPS) * ln_w + ln_b
    ii = lax.broadcasted_iota(jnp.int32, (SGU_CHUNK, SGU_CHUNK), 0)
    jj = lax.broadcasted_iota(jnp.int32, (SGU_CHUNK, SGU_CHUNK), 1)
    mask = (jj <= ii).astype(F32)
    gi = lax.broadcasted_iota(jnp.int32, (SGU_GROUPS, D_MODEL), 0)
    ci = lax.broadcasted_iota(jnp.int32, (SGU_GROUPS, D_MODEL), 1) // SGU_CHUNK
    bias = mdot(sbt, (gi == ci).astype(F32), "nn", True)
    rows = []
    for c in range(tm // SGU_CHUNK):
        cols = []
        for g in range(SGU_GROUPS):
            blk = vn[c * SGU_CHUNK:(c + 1) * SGU_CHUNK, g * SGU_CHUNK:(g + 1) * SGU_CHUNK]
            cols.append(mdot(sw[g] * mask, blk, "nn", False))
        rows.append(jnp.concatenate(cols, axis=1) + bias)
    return (u * jnp.concatenate(rows, axis=0),)


def _softplus(x):
    return jnp.maximum(x, 0.0) + jnp.log1p(jnp.exp(-jnp.abs(x)))


def f_pre(q, wl, w0, al, a0, gl, k_k, k_a):
    qr, qk, qv, ql = q[:, :1024], q[:, 1024:2048], q[:, 2048:3072], q[:, 3072:]
    return _f_pre(qr, qk, qv, ql, wl, w0, al, a0, gl, k_k, k_a)


def _f_pre(qr, qk, qv, ql, wl, w0, al, a0, gl, k_k, k_a):
    xw, xa, xg = ql[:, :128], ql[:, 128:256], ql[:, 256:512]
    wr = -_softplus(-(w0 + mdot(jnp.tanh(xw), wl, "nn", False))) - 0.5
    lw = -jnp.exp(wr)
    aa = jax.nn.sigmoid(a0 + mdot(xa, al, "nn", False))
    g = mdot(jax.nn.sigmoid(xg), gl, "nn", False)
    kkr = qk * k_k
    kk = kkr / jnp.maximum(jnp.sqrt(segsum(kkr * kkr)), 1e-12)
    kp = qk * (1.0 + (aa - 1.0) * k_a)
    return qr, lw, kp, qv, -kk, kk * aa, g, qr, kp, qv


def f_post(o, r, kp, v, g, lnw, lnb, rk):
    mu = segsum(o) * (1.0 / HEAD)
    d = o - mu
    gn = d * lax.rsqrt(segsum(d * d) * (1.0 / HEAD) + GN_EPS)
    return ((gn * lnw + lnb + segsum(r * kp * rk) * v) * g,)


def f_mix(ya, yb, ga, gb):
    return (jax.nn.sigmoid(ga) * ya + jax.nn.sigmoid(gb) * yb,)


def f_ffn_in(h1, g):
    return _rms(h1, g), h1


def f_final(h1, m3, tgt, g):
    y = _rms(h1 + m3, g)
    err = jnp.square(y - tgt)
    return 0.5 * jnp.sum(jnp.mean(err, axis=-1))


def _cparams(n_grid):
    return pltpu.CompilerParams(dimension_semantics=("arbitrary",) * n_grid, vmem_limit_bytes=VMEM_LIMIT)


def _tile_spec(tm, w, cb):
    return pl.BlockSpec((tm, w), lambda i: (i, cb))


def _const_spec(c):
    nd = c.ndim
    return pl.BlockSpec(c.shape, lambda i: (0,) * nd)


def ew_call(fn, tiled, consts, outs, *, tm, name):
    t = tiled[0][0].shape[0]
    n_t, n_c = len(tiled), len(consts)

    def body(*refs):
        tv = [r[...].astype(F32) for r in refs[:n_t]]
        cv = [r[...] for r in refs[n_t:n_t + n_c]]
        res = fn(*tv, *cv)
        for o_ref, val in zip(refs[n_t + n_c:], res):
            o_ref[...] = val.astype(o_ref.dtype)

    return pl.pallas_call(
        body,
        name=name,
        grid=(t // tm,),
        in_specs=[_tile_spec(tm, w, cb) for _, w, cb in tiled] + [_const_spec(c) for c in consts],
        out_specs=[_tile_spec(tm, w, 0) for w, _ in outs],
        out_shape=[jax.ShapeDtypeStruct((t, w), dt) for w, dt in outs],
        compiler_params=_cparams(1),
    )(*[a for a, _, _ in tiled], *consts)


def ew_vjp_call(fn, tiled, consts, cots, d_tiled, d_consts, *, tm, name):
    t = tiled[0][0].shape[0]
    n_t, n_c, n_g = len(tiled), len(consts), len(cots)
    dt_list = [(i, dt) for i, dts in enumerate(d_tiled) for dt in dts]
    dc_list = [i for i, want in enumerate(d_consts) if want]

    def body(*refs):
        tv = [r[...].astype(F32) for r in refs[:n_t]]
        cv = [r[...] for r in refs[n_t:n_t + n_c]]
        gv = tuple(r[...].astype(F32) for r in refs[n_t + n_c:n_t + n_c + n_g])
        out_refs = refs[n_t + n_c + n_g:]
        _, vjp = jax.vjp(fn, *tv, *cv)
        grads = vjp(gv)
        for o_ref, (i, _) in zip(out_refs, dt_list):
            o_ref[...] = grads[i].astype(o_ref.dtype)
        acc_refs = out_refs[len(dt_list):]

        @pl.when(pl.program_id(0) == 0)
        def _():
            for a_ref in acc_refs:
                a_ref[...] = jnp.zeros_like(a_ref)

        for a_ref, i in zip(acc_refs, dc_list):
            a_ref[...] += grads[n_t + i]

    res = pl.pallas_call(
        body,
        name=name,
        grid=(t // tm,),
        in_specs=[_tile_spec(tm, w, cb) for _, w, cb in tiled] + [_const_spec(c) for c in consts]
        + [_tile_spec(tm, w, cb) for _, w, cb in cots],
        out_specs=[_tile_spec(tm, tiled[i][1], 0) for i, _ in dt_list] + [_const_spec(consts[i]) for i in dc_list],
        out_shape=[jax.ShapeDtypeStruct((t, tiled[i][1]), dt) for i, dt in dt_list]
        + [jax.ShapeDtypeStruct(consts[i].shape, F32) for i in dc_list],
        compiler_params=_cparams(1),
    )(*[a for a, _, _ in tiled], *consts, *[a for a, _, _ in cots])
    return res[:len(dt_list)], res[len(dt_list):]


def mm(a, b, mode, *, tm, tn, name, out_dtypes=(F32,), epi=None, extras=(), into=None):
    m = a.shape[1] if mode == "tn" else a.shape[0]
    kd = a.shape[0] if mode == "tn" else a.shape[1]
    n = b.shape[0] if mode == "nt" else b.shape[1]
    tm, tn = min(tm, m), min(tn, n)
    if mode == "nn":
        a_spec = pl.BlockSpec((tm, kd), lambda i, j: (i, 0))
        b_spec = pl.BlockSpec((kd, tn), lambda i, j: (0, j))
    elif mode == "nt":
        a_spec = pl.BlockSpec((tm, kd), lambda i, j: (i, 0))
        b_spec = pl.BlockSpec((tn, kd), lambda i, j: (j, 0))
    else:
        a_spec = pl.BlockSpec((kd, tm), lambda i, j: (0, i))
        b_spec = pl.BlockSpec((kd, tn), lambda i, j: (0, j))
    n_e = len(extras)
    o_spec = pl.BlockSpec((tm, tn), lambda i, j: (i, j))

    if into is not None:
        buf, place = into

        def body_into(a_ref, b_ref, buf_ref, o_ref):
            o_ref[0, 0] = lax.dot_general(a_ref[...].astype(BF16), b_ref[...].astype(BF16), _MDIMS[mode],
                                          preferred_element_type=F32)

        return pl.pallas_call(
            body_into,
            name=name,
            grid=(m // tm, n // tn),
            in_specs=[a_spec, b_spec, pl.BlockSpec(memory_space=pl.ANY)],
            out_specs=pl.BlockSpec((1, 1, tm, tn), lambda i, j: (*place(i, j), 0)),
            out_shape=jax.ShapeDtypeStruct(buf.shape, F32),
            input_output_aliases={2: 0},
            compiler_params=_cparams(2),
        )(a, b, buf)

    def body(a_ref, b_ref, *refs):
        c = lax.dot_general(a_ref[...].astype(BF16), b_ref[...].astype(BF16), _MDIMS[mode],
                            preferred_element_type=F32)
        res = epi(c, *[r[...] for r in refs[:n_e]]) if epi is not None else (c,)
        for o_ref, val in zip(refs[n_e:], res):
            o_ref[...] = val.astype(o_ref.dtype)

    res = pl.pallas_call(
        body,
        name=name,
        grid=(m // tm, n // tn),
        in_specs=[a_spec, b_spec] + [o_spec] * n_e,
        out_specs=[o_spec] * len(out_dtypes),
        out_shape=[jax.ShapeDtypeStruct((m, n), dt) for dt in out_dtypes],
        compiler_params=_cparams(2),
    )(a, b, *extras)
    return res if len(out_dtypes) > 1 else res[0]


RWKV_COL0 = 4096
RWKV_WIDTH = 3584
SHIFT_BLK = 512


def _shift_down(p, prev_row):
    rows = lax.broadcasted_iota(jnp.int32, p.shape, 0)
    return jnp.where(rows == 0, prev_row, pltpu.roll(p, 1, 0))


def shiftmix_fwd(p_all, sbp, *, tm):
    t = p_all.shape[0]
    tm = min(tm, t)
    c0 = RWKV_COL0 // SHIFT_BLK
    hb = tm // 8

    def body(p_ref, halo_ref, sb_ref, q_ref):
        p = p_ref[...]
        prev = jnp.where(pl.program_id(0) == 0, 0.0, halo_ref[7:8, :])
        q_ref[...] = p * sb_ref[0:1, :] + _shift_down(p, prev) * sb_ref[1:2, :]

    return pl.pallas_call(
        body,
        name="shiftmix_fwd",
        grid=(t // tm, RWKV_WIDTH // SHIFT_BLK),
        in_specs=[
            pl.BlockSpec((tm, SHIFT_BLK), lambda i, j: (i, c0 + j)),
            pl.BlockSpec((8, SHIFT_BLK), lambda i, j: (jnp.maximum(i * hb - 1, 0), c0 + j)),
            pl.BlockSpec((2, SHIFT_BLK), lambda i, j: (0, j)),
        ],
        out_specs=pl.BlockSpec((tm, SHIFT_BLK), lambda i, j: (i, j)),
        out_shape=jax.ShapeDtypeStruct((t, RWKV_WIDTH), F32),
        compiler_params=_cparams(2),
    )(p_all, p_all, sbp)


def shiftmix_bwd(dq, col0, p_all, sbp, *, tm, name):
    t, w = dq.shape
    n_i = t // tm
    hb = tm // 8
    cq = col0 // SHIFT_BLK
    cp = (RWKV_COL0 + col0) // SHIFT_BLK

    def body(dq_ref, dqn_ref, p_ref, ph_ref, sb_ref, dp_ref, dsb_ref):
        i = pl.program_id(1)
        dq_t = dq_ref[...]
        rows = lax.broadcasted_iota(jnp.int32, dq_t.shape, 0)
        nxt = jnp.where(i == n_i - 1, 0.0, dqn_ref[0:1, :])
        up = jnp.where(rows == tm - 1, nxt, pltpu.roll(dq_t, tm - 1, 0))
        dp_ref[...] = (dq_t * sb_ref[0:1, :] + up * sb_ref[1:2, :]).astype(dp_ref.dtype)
        p = p_ref[...]
        prev = jnp.where(i == 0, 0.0, ph_ref[7:8, :])
        s0 = jnp.sum(dq_t * p, axis=0, keepdims=True)
        s1 = jnp.sum(dq_t * _shift_down(p, prev), axis=0, keepdims=True)
        two = lax.broadcasted_iota(jnp.int32, (2, SHIFT_BLK), 0)

        @pl.when(i == 0)
        def _():
            dsb_ref[...] = jnp.zeros_like(dsb_ref)

        dsb_ref[...] += jnp.where(two == 0, s0, s1)

    return pl.pallas_call(
        body,
        name=name,
        grid=(w // SHIFT_BLK, n_i),
        in_specs=[
            pl.BlockSpec((tm, SHIFT_BLK), lambda j, i: (i, j)),
            pl.BlockSpec((8, SHIFT_BLK), lambda j, i: (jnp.minimum((i + 1) * hb, t // 8 - 1), j)),
            pl.BlockSpec((tm, SHIFT_BLK), lambda j, i: (i, cp + j)),
            pl.BlockSpec((8, SHIFT_BLK), lambda j, i: (jnp.maximum(i * hb - 1, 0), cp + j)),
            pl.BlockSpec((2, SHIFT_BLK), lambda j, i: (0, cq + j)),
        ],
        out_specs=[
            pl.BlockSpec((tm, SHIFT_BLK), lambda j, i: (i, j)),
            pl.BlockSpec((2, SHIFT_BLK), lambda j, i: (0, j)),
        ],
        out_shape=[jax.ShapeDtypeStruct((t, w), BF16), jax.ShapeDtypeStruct((2, w), F32)],
        compiler_params=_cparams(2),
    )(dq, dq, p_all, p_all, sbp)


def final_call(h1, m3, tgt, g_final, *, tm):
    t = h1.shape[0]

    def body(h1_ref, m3_ref, tgt_ref, g_ref, dh_ref, dhb_ref, dg_ref, loss_ref):
        loss, vjp = jax.vjp(f_final, h1_ref[...], m3_ref[...], tgt_ref[...], g_ref[...])
        dh, _, _, dg = vjp(jnp.ones((), F32))
        dh_ref[...] = dh
        dhb_ref[...] = dh.astype(BF16)

        @pl.when(pl.program_id(0) == 0)
        def _():
            dg_ref[...] = jnp.zeros_like(dg_ref)
            loss_ref[...] = jnp.zeros_like(loss_ref)

        dg_ref[...] += dg
        loss_ref[...] += jnp.full(loss_ref.shape, loss, F32)

    tile = _tile_spec(tm, D_MODEL, 0)
    return pl.pallas_call(
        body,
        name="final_loss",
        grid=(t // tm,),
        in_specs=[tile, tile, tile, _const_spec(g_final)],
        out_specs=[tile, tile, _const_spec(g_final), pl.BlockSpec((8, 128), lambda i: (0, 0))],
        out_shape=[jax.ShapeDtypeStruct((t, D_MODEL), F32), jax.ShapeDtypeStruct((t, D_MODEL), BF16),
                   jax.ShapeDtypeStruct(g_final.shape, F32), jax.ShapeDtypeStruct((8, 128), F32)],
        compiler_params=_cparams(1),
    )(h1, m3, tgt, g_final)


N_SGU = 2048
N_RWKV = 3360
LORA_W, LORA_A, LORA_G = 64, 64, 160


def _pad_rwkv_cols(z):
    zero = lambda n: jnp.zeros(z.shape[:-1] + (n,), z.dtype)
    return jnp.concatenate([z[..., :3072], z[..., 3072:3136], zero(64), z[..., 3136:3200], zero(64),
                            z[..., 3200:3360], zero(96)], axis=-1)


def _unpad_rwkv_cols(z):
    return jnp.concatenate([z[..., :3072], z[..., 3072:3136], z[..., 3200:3264], z[..., 3328:3488]], axis=-1)


def _pad_win_rows(wt):
    z = wt[N_SGU:N_SGU + N_RWKV]
    zero = lambda n: jnp.zeros((n, wt.shape[1]), wt.dtype)
    return jnp.concatenate([wt[:N_SGU], wt[N_SGU + N_RWKV:], z[:3072], z[3072:3136], zero(64), z[3136:3200], zero(64),
                            z[3200:3360], zero(96)], axis=0)


def _unpad_win_rows(wt):
    z = wt[RWKV_COL0:]
    return jnp.concatenate([wt[:N_SGU], z[:3072], z[3072:3136], z[3200:3264], z[3328:3488], wt[N_SGU:RWKV_COL0]],
                           axis=0)


def _pad_rows(w, n):
    return jnp.concatenate([w, jnp.zeros((n - w.shape[0],) + w.shape[1:], w.dtype)], axis=0)


def _relu2_epi(c):
    return c, jnp.square(jnp.maximum(c, 0.0))


def _relu2_bwd_epi(c, hid):
    return (c * (2.0 * jnp.maximum(hid.astype(F32), 0.0)),)


def _add_epi(c, x):
    return (c + x,)


def _pre_fwd(*args):
    res = f_pre(*args)
    return res[1], res[2], res[4], res[5], res[6]


def local_step(x, tgt, w, late_token, late_weights, pair_start, pair_finish, pack_early):
    d = D_MODEL
    win_pt = _pad_win_rows(w["w_in"])
    sbp = _pad_rwkv_cols(w["shift_b"])
    wl = _pad_rows(w["w_lora_w"], 128)
    al = _pad_rows(w["a_lora_w"], 128)
    gl = _pad_rows(w["g_lora_w"], 256)
    sbt = w["sgu_b"].T

    (a_bf,) = ew_call(lambda x_, g_: (f_norm_in(x_, g_)[0],), [(x, d, 0)], [w["g_mix"] + late_token[:1, :1]],
                      [(d, BF16)], tm=512, name="norm_in")
    p_all = mm(a_bf, win_pt, "nt", tm=2048, tn=1280, name="mm_in")
    sgu_t = [(p_all, 2 * d, 0)]
    sgu_c = [w["sgu_ln_w"], w["sgu_ln_b"], w["sgu_w"], sbt]
    (s_bf,) = ew_call(f_sgu, sgu_t, sgu_c, [(d, BF16)], tm=256, name="sgu_fwd")
    q = shiftmix_fwd(p_all, sbp, tm=2048)
    pre_t = [(q, RWKV_WIDTH, 0)]
    pre_c = [wl, w["w0"], al, w["a0"], gl, w["k_k"], w["k_a"]]
    lw, kp, na, nb, g = ew_call(_pre_fwd, pre_t, pre_c, [(d, F32)] * 5, tm=256, name="rwkv_pre_fwd")
    scan_ops = [(q, 0), (lw, 0), (kp, 0), (q, 2), (na, 0), (nb, 0)]
    o, s0s = scan_fwd(scan_ops)
    w = {**w, **late_weights(o)}
    ya = mm(s_bf, w["w_proj_a"], "nn", tm=512, tn=1024, name="mm_proj_a")
    post_t = [(o, d, 0), (q, d, 0), (kp, d, 0), (q, d, 2), (g, d, 0)]
    post_c = [w["ln_x_w"], w["ln_x_b"], w["r_k"]]
    (ob_bf,) = ew_call(f_post, post_t, post_c, [(d, BF16)], tm=256, name="rwkv_post_fwd")
    yb = mm(ob_bf, w["w_proj_b"], "nn", tm=512, tn=1024, name="mm_proj_b")
    mix_t = [(ya, d, 0), (yb, d, 0), (p_all, d, 2), (p_all, d, 3)]
    (mixed_bf,) = ew_call(f_mix, mix_t, [], [(d, BF16)], tm=512, name="mix_fwd")
    h1 = mm(mixed_bf, w["w_out"], "nn", tm=512, tn=1024, name="mm_out", epi=_add_epi, extras=(x,))
    (f_bf,) = ew_call(lambda h_, g_: (f_ffn_in(h_, g_)[0],), [(h1, d, 0)], [w["g_ffn"]], [(d, BF16)], tm=512,
                      name="ffn_norm")
    hid, act_bf = mm(f_bf, w["w_ffn1"], "nn", tm=2048, tn=1024, name="mm_ffn1", out_dtypes=(BF16, BF16), epi=_relu2_epi)
    m3 = mm(act_bf, w["w_ffn2"], "nn", tm=1024, tn=512, name="mm_ffn2")
    dh2, dh2_bf, dg_final, loss = final_call(h1, m3, tgt, w["g_final"], tm=256)

    dhid_bf = mm(dh2_bf, w["w_ffn2"], "nt", tm=2048, tn=1024, name="mm_dact", out_dtypes=(BF16,), epi=_relu2_bwd_epi,
                 extras=(hid,))
    late_g = lax.empty((N_CHIPS, 2, pack_rows(LATE), HALF_W), F32)
    late_g = mm(act_bf, dh2_bf, "tn", tm=1024, tn=HALF_W, name="mm_dw_ffn2",
                into=(late_g, lambda i, j: (i, j, PIECE_OFF["w_ffn2"] // 1024)))
    df = mm(dhid_bf, w["w_ffn1"], "nt", tm=1024, tn=512, name="mm_df")
    late_g = mm(f_bf, dhid_bf, "tn", tm=1024, tn=HALF_W, name="mm_dw_ffn1",
                into=(late_g, lambda i, j: (j // 2, j % 2, PIECE_OFF["w_ffn1"] // 1024)))
    (dh1, dh1_bf), (dg_ffn,) = ew_vjp_call(f_ffn_in, [(h1, d, 0)], [w["g_ffn"]], [(df, d, 0), (dh2, d, 0)],
                                           [(F32, BF16)], [True], tm=256, name="ffn_norm_bwd")
    dmixed = mm(dh1_bf, w["w_out"], "nt", tm=512, tn=1024, name="mm_dmixed")
    late_g = mm(mixed_bf, dh1_bf, "tn", tm=256, tn=HALF_W, name="mm_dw_out",
                into=(late_g, lambda i, j: (i, j, PIECE_OFF["w_out"] // 256)))
    (dya_bf, dyb_bf, dga_bf, dgb_bf), _ = ew_vjp_call(f_mix, mix_t, [], [(dmixed, d, 0)], [(BF16,)] * 4, [], tm=256,
                                                      name="mix_bwd")
    dob = mm(dyb_bf, w["w_proj_b"], "nt", tm=512, tn=1024, name="mm_dob")
    late_g = mm(ob_bf, dyb_bf, "tn", tm=256, tn=HALF_W, name="mm_dw_proj_b",
                into=(late_g, lambda i, j: (i, j, PIECE_OFF["w_proj_b"] // 256)))
    late_g = mm(s_bf, dya_bf, "tn", tm=256, tn=HALF_W, name="mm_dw_proj_a",
                into=(late_g, lambda i, j: (i, j, PIECE_OFF["w_proj_a"] // 256)))
    late_state, late_token = pair_start(late_g, "late")
    post_c_after = [w["ln_x_w"] + late_token[:1, :1]] + post_c[1:]
    (do, dr_p, dkp_p, dv_p, dg), (dlnx_w, dlnx_b, dr_k) = ew_vjp_call(
        f_post, post_t, post_c_after, [(dob, d, 0)], [(F32,)] * 5, [True] * 3, tm=256, name="rwkv_post_bwd")
    late_part, late_part16 = pair_finish(late_state, do, "late")
    *scan_g, late_slots = scan_bwd(scan_ops, s0s, do, late_part16)
    pre_g = [(z, d, 0) for z in scan_g] + [(dg, d, 0), (dr_p, d, 0), (dkp_p, d, 0), (dv_p, d, 0)]
    (dq,), (dwl, dw0, dal, da0, dgl, dk_k, dk_a) = ew_vjp_call(
        f_pre, pre_t, pre_c, pre_g, [(F32,)], [True] * 7, tm=256, name="rwkv_pre_bwd")
    dp_rwkv, dsb = shiftmix_bwd(dq, 0, p_all, sbp, tm=1024, name="shiftmix_bwd")
    ds = mm(dya_bf, w["w_proj_a"], "nt", tm=512, tn=1024, name="mm_ds")
    (dp_sgu,), (dln_w, dln_b, dsw, dsbt) = ew_vjp_call(f_sgu, sgu_t, sgu_c, [(ds, d, 0)], [(BF16,)], [True] * 4,
                                                       tm=256, name="sgu_bwd")
    dp_all = jnp.concatenate([dp_sgu, dga_bf, dgb_bf, dp_rwkv], axis=1)
    d_in_pt = mm(dp_all, a_bf, "tn", tm=1280, tn=1024, name="mm_dw_in")
    early_state, early_token = pair_start(pack_early({
        "w_in": _unpad_win_rows(d_in_pt), "w_lora_w": dwl[:LORA_W], "a_lora_w": dal[:LORA_A],
        "g_lora_w": dgl[:LORA_G]}), "early")
    da = mm(dp_all, win_pt, "nn", tm=1024, tn=256, name="mm_da")
    g_mix_after = w["g_mix"] + early_token[:1, :1]
    (grad_x,), (dg_mix,) = ew_vjp_call(f_norm_in, [(x, d, 0)], [g_mix_after], [(da, d, 0), (dh1, d, 0)], [(F32,)],
                                       [True], tm=256, name="norm_in_bwd")

    grads = {
        "g_mix": dg_mix, "sgu_ln_w": dln_w, "sgu_ln_b": dln_b, "sgu_w": dsw, "sgu_b": dsbt.T,
        "shift_b": _unpad_rwkv_cols(dsb),
        "w0": dw0, "a0": da0, "k_k": dk_k, "k_a": dk_a, "r_k": dr_k, "ln_x_w": dlnx_w, "ln_x_b": dlnx_b,
        "g_ffn": dg_ffn, "g_final": dg_final,
    }
    return loss[0, 0], grad_x, grads, (late_part, late_slots), early_state


MESH = pl.DeviceIdType.MESH
N_CHIPS = 4
SMALL_ROWS = 160
_ANY = pl.BlockSpec(memory_space=pl.ANY)


def _coords():
    return lax.axis_index("x"), lax.axis_index("y"), lax.axis_index("c")


def _other_chips(x, y):
    return [(1 - x, y), (x, 1 - y), (1 - x, 1 - y)]


def _remote(src, dst, send_sems, recv_sems, k, to):
    return pltpu.make_async_remote_copy(src_ref=src, dst_ref=dst, send_sem=send_sems.at[k], recv_sem=recv_sems.at[k],
                                        device_id=to, device_id_type=MESH)


def gather_shards(pack):
    def body(src_ref, out_ref, token, send_sems, recv_sems):
        x, y, c = _coords()
        me = 2 * x + y
        sib = (x, y, 1 - c)
        chips = _other_chips(x, y)
        first = [_remote(src_ref.at[c], out_ref.at[me, c], send_sems, recv_sems, k, (cx, cy, c))
                 for k, (cx, cy) in enumerate(chips)]
        for cp in first:
            cp.start()
        passed = []
        for k, (cx, cy) in enumerate(chips):
            j = 2 * cx + cy
            _remote(src_ref.at[c], out_ref.at[j, c], send_sems, recv_sems, k, (cx, cy, c)).wait_recv()
            fwd = _remote(out_ref.at[j, c], out_ref.at[j, c], send_sems, recv_sems, 3 + k, sib)
            fwd.start()
            passed.append(fwd)
        for k, (cx, cy) in enumerate(chips):
            j = 2 * cx + cy
            _remote(out_ref.at[j, 1 - c], out_ref.at[j, 1 - c], send_sems, recv_sems, 3 + k, sib).wait_recv()
        for cp in first + passed:
            cp.wait_send()
        token[...] = jnp.zeros_like(token)

    return pl.pallas_call(
        body,
        name="gather_shards",
        in_specs=[_ANY],
        out_specs=[_ANY, pl.BlockSpec(memory_space=pltpu.VMEM)],
        out_shape=[jax.ShapeDtypeStruct((N_CHIPS,) + pack.shape, pack.dtype), jax.ShapeDtypeStruct((8, 128), F32)],
        scratch_shapes=[pltpu.SemaphoreType.DMA((6,)), pltpu.SemaphoreType.DMA((6,))],
    )(pack)


def _gather_copies(pack_ref, all_ref, send_sems, recv_sems):
    x, y, c = _coords()
    me = 2 * x + y
    return [(_remote(pack_ref.at[c], all_ref.at[me, c], send_sems, recv_sems, k, (cx, cy, c)),
             _remote(pack_ref.at[c], all_ref.at[2 * cx + cy, c], send_sems, recv_sems, k, (cx, cy, c)))
            for k, (cx, cy) in enumerate(_other_chips(x, y))]


_HBM = pl.BlockSpec(memory_space=pltpu.HBM)
_SEM = pl.BlockSpec(memory_space=pltpu.SEMAPHORE)
_SIDE_EFFECT = pltpu.SideEffectType.DATAFLOW_SIDE_EFFECTING


def split_start(name, copies, n, src, land_shape, after=None):
    def body(src_ref, land_ref, *refs):
        send_sems, recv_sems, token = refs[-5], refs[-4], refs[-1]
        for send, _ in copies(src_ref, land_ref, send_sems, recv_sems):
            send.start()
        token[...] = jnp.zeros_like(token)

    extra = () if after is None else (after,)
    *state, token = pl.pallas_call(
        body,
        name=name,
        out_shape=(pltpu.SemaphoreType.DMA((n,)), pltpu.SemaphoreType.DMA((n,)), pltpu.HBM(src.shape, src.dtype),
                   pltpu.HBM(land_shape, src.dtype), jax.ShapeDtypeStruct((8, 128), F32)),
        in_specs=(_HBM, _HBM) + (pl.BlockSpec(memory_space=pl.ANY),) * len(extra),
        out_specs=(_SEM, _SEM, _HBM, _HBM, pl.BlockSpec(memory_space=pltpu.VMEM)),
        input_output_aliases={0: 2, 1: 3},
        compiler_params=pltpu.CompilerParams(has_side_effects=_SIDE_EFFECT),
    )(pltpu.with_memory_space_constraint(src, pltpu.HBM),
      pltpu.with_memory_space_constraint(lax.empty(land_shape, src.dtype), pltpu.HBM), *extra)
    return state, token


def split_wait(name, copies, state, after):
    send_sems, recv_sems, src, land = state

    def body(src_ref, land_ref, send_sems, recv_sems, after_ref, src_out, land_out):
        for send, arrival in copies(src_ref, land_ref, send_sems, recv_sems):
            send.wait_send()
            arrival.wait_recv()

    return pl.pallas_call(
        body,
        name=name,
        out_shape=(pltpu.HBM(src.shape, src.dtype), pltpu.HBM(land.shape, land.dtype)),
        in_specs=(_HBM, _HBM, _SEM, _SEM, pl.BlockSpec(memory_space=pl.ANY)),
        out_specs=(_HBM, _HBM),
        input_output_aliases={0: 0, 1: 1},
        compiler_params=pltpu.CompilerParams(has_side_effects=_SIDE_EFFECT),
    )(src, land, send_sems, recv_sems, after)


def gather_forward(got):
    def body(got_ref, out_ref, send_sems, recv_sems):
        x, y, c = _coords()
        sib = (x, y, 1 - c)
        slots = [2 * cx + cy for cx, cy in _other_chips(x, y)]
        sends = [_remote(got_ref.at[j, c], out_ref.at[j, c], send_sems, recv_sems, k, sib) for k, j in enumerate(slots)]
        for cp in sends:
            cp.start()
        for k, j in enumerate(slots):
            _remote(got_ref.at[j, 1 - c], out_ref.at[j, 1 - c], send_sems, recv_sems, k, sib).wait_recv()
        for cp in sends:
            cp.wait_send()

    return pl.pallas_call(
        body,
        name="gather_forward",
        in_specs=[_ANY],
        out_specs=_ANY,
        out_shape=jax.ShapeDtypeStruct(got.shape, got.dtype),
        input_output_aliases={0: 0},
        scratch_shapes=[pltpu.SemaphoreType.DMA((3,)), pltpu.SemaphoreType.DMA((3,))],
    )(got)


def pair_sum(g, got, tag, *, tm):
    n, _, rows, width = g.shape

    def body(c_ref, own_ref, got_ref, out_ref, out16_ref):
        total = own_ref[0, 0] + got_ref[0]
        out_ref[0] = total
        out16_ref[0] = total.astype(BF16)

    blk = pl.BlockSpec((1, tm, width), lambda j, i, c_ref: (j, i, 0))
    return pl.pallas_call(
        body,
        name="pair_sum_" + tag,
        grid_spec=pltpu.PrefetchScalarGridSpec(
            num_scalar_prefetch=1,
            grid=(n, rows // tm),
            in_specs=[pl.BlockSpec((1, 1, tm, width), lambda j, i, c_ref: (j, c_ref[0], i, 0)), blk],
            out_specs=[blk, blk],
        ),
        out_shape=[jax.ShapeDtypeStruct(got.shape, F32), jax.ShapeDtypeStruct(got.shape, BF16)],
        compiler_params=_cparams(2),
    )(lax.axis_index("c").reshape(1).astype(jnp.int32), g, got)


def _pair_copies(g_ref, got_ref, send_sems, recv_sems):
    x, y, c = _coords()
    copies = [_remote(g_ref.at[j, 1 - c], got_ref.at[j], send_sems, recv_sems, j, (x, y, 1 - c))
              for j in range(N_CHIPS)]
    return [(cp, cp) for cp in copies]


def _chip_copies(p_ref, slots_ref, send_sems, recv_sems):
    x, y, c = _coords()
    me = 2 * x + y
    return [(_remote(p_ref.at[2 * cx + cy], slots_ref.at[me], send_sems, recv_sems, k, (cx, cy, c)),
             _remote(p_ref.at[me], slots_ref.at[2 * cx + cy], send_sems, recv_sems, k, (cx, cy, c)))
            for k, (cx, cy) in enumerate(_other_chips(x, y))]


def sum_with_own(own, slots, mine, after, *, tm, name):
    n, rows, width = slots.shape

    def body(mine_ref, own_ref, *refs):
        acc = None
        for s in range(n):
            term = jnp.where(mine_ref[0] == s, own_ref[0], refs[s][0].astype(F32))
            acc = term if acc is None else acc + term
        refs[-1][...] = acc

    return pl.pallas_call(
        body,
        name=name,
        grid_spec=pltpu.PrefetchScalarGridSpec(
            num_scalar_prefetch=1,
            grid=(rows // tm,),
            in_specs=[pl.BlockSpec((1, tm, width), lambda i, mine_ref: (mine_ref[0], i, 0))]
            + [pl.BlockSpec((1, tm, width), lambda i, mine_ref, s=s: (s, i, 0)) for s in range(n)]
            + [pl.BlockSpec(after.shape, lambda i, mine_ref: (0,) * after.ndim)],
            out_specs=pl.BlockSpec((tm, width), lambda i, mine_ref: (i, 0)),
        ),
        out_shape=jax.ShapeDtypeStruct((rows, width), F32),
        compiler_params=_cparams(1),
    )(mine.reshape(1).astype(jnp.int32), own, *([slots] * n), after)


def exchange_halves(s, tag):
    nq = 4
    rq = s.shape[0] // nq
    assert rq * nq == s.shape[0] and rq % 8 == 0

    def body(s_ref, out_ref, sbuf, rbuf, send_sems, recv_sems, in_sems, out_sems):
        x, y, c = _coords()
        sib = (x, y, 1 - c)
        rows = lambda q: pl.ds(q * rq, rq)
        loads = [pltpu.make_async_copy(s_ref.at[rows(q)], sbuf.at[rows(q)], in_sems.at[q]) for q in range(nq)]
        for cp in loads:
            cp.start()
        sends = []
        for q in range(nq):
            loads[q].wait()
            sends.append(_remote(sbuf.at[rows(q)], rbuf.at[rows(q)], send_sems, recv_sems, q, sib))
            sends[q].start()
        stores = []
        for q in range(nq):
            sends[q].wait_recv()
            stores.append(pltpu.make_async_copy(rbuf.at[rows(q)], out_ref.at[rows(q)], out_sems.at[q]))
            stores[q].start()
        for cp in sends:
            cp.wait_send()
        for cp in stores:
            cp.wait()

    return pl.pallas_call(
        body,
        name="exchange_halves_" + tag,
        in_specs=[_ANY],
        out_specs=_ANY,
        out_shape=jax.ShapeDtypeStruct(s.shape, s.dtype),
        scratch_shapes=[pltpu.VMEM(s.shape, s.dtype), pltpu.VMEM(s.shape, s.dtype)]
        + [pltpu.SemaphoreType.DMA((nq,))] * 4,
        compiler_params=pltpu.CompilerParams(vmem_limit_bytes=VMEM_LIMIT),
    )(s)


def sum_all(s, after):
    rows = s.shape[0]
    half = rows // 2

    def body(s_ref, after_ref, out_ref, theirs, pair, slots, send_sems, recv_sems):
        x, y, c = _coords()
        me = 2 * x + y
        sib = (x, y, 1 - c)
        chips = _other_chips(x, y)
        swap = _remote(s_ref, theirs, send_sems, recv_sems, 0, sib)
        swap.start()
        swap.wait_recv()
        pair[...] = s_ref[...] + theirs[...]
        mine = pl.ds(pl.multiple_of(c * half, 8), half)
        other = pl.ds(pl.multiple_of((1 - c) * half, 8), half)
        sends = [_remote(pair.at[mine], slots.at[me], send_sems, recv_sems, 1 + k, (cx, cy, c))
                 for k, (cx, cy) in enumerate(chips)]
        for cp in sends:
            cp.start()
        for k, (cx, cy) in enumerate(chips):
            _remote(pair.at[mine], slots.at[2 * cx + cy], send_sems, recv_sems, 1 + k, (cx, cy, c)).wait_recv()
        slots[me] = pair[mine]
        out_ref[mine] = ((slots[0] + slots[1]) + slots[2]) + slots[3]
        last = _remote(out_ref.at[mine], out_ref.at[mine], send_sems, recv_sems, 4, sib)
        last.start()
        _remote(out_ref.at[other], out_ref.at[other], send_sems, recv_sems, 4, sib).wait_recv()
        for cp in [swap] + sends + [last]:
            cp.wait_send()

    vmem = pl.BlockSpec(memory_space=pltpu.VMEM)
    return pl.pallas_call(
        body,
        name="sum_all",
        in_specs=[vmem, vmem],
        out_specs=vmem,
        out_shape=jax.ShapeDtypeStruct(s.shape, s.dtype),
        scratch_shapes=[pltpu.VMEM(s.shape, s.dtype), pltpu.VMEM(s.shape, s.dtype),
                        pltpu.VMEM((N_CHIPS, half, s.shape[1]), s.dtype), pltpu.SemaphoreType.DMA((5,)),
                        pltpu.SemaphoreType.DMA((5,))],
        compiler_params=pltpu.CompilerParams(vmem_limit_bytes=VMEM_LIMIT),
    )(s, after)


ADAM_LR = 0.001
ADAM_B1 = 0.9
ADAM_B2 = 0.999
ADAM_EPS = 1e-08
ADAM_WD = 0.01
ADAM_STEP = 10


def f_adamw(g, w, m, v):
    m = ADAM_B1 * m + (1.0 - ADAM_B1) * g
    v = ADAM_B2 * v + (1.0 - ADAM_B2) * jnp.square(g)
    m_hat = m / (1.0 - ADAM_B1 ** ADAM_STEP)
    v_hat = v / (1.0 - ADAM_B2 ** ADAM_STEP)
    delta = -ADAM_LR * (m_hat / (jnp.sqrt(v_hat) + ADAM_EPS) + ADAM_WD * w)
    return delta, m, v


def adamw_many(gs, ws, ms, vs):
    n = len(gs)

    def body(*refs):
        ins, outs = refs[:4 * n], refs[4 * n:]
        for i in range(n):
            delta, nm, nv = f_adamw(ins[i][...], ins[n + i][...], ins[2 * n + i][...], ins[3 * n + i][...])
            outs[i][...] = delta
            outs[n + i][...] = nm
            outs[2 * n + i][...] = nv

    vmem = pl.BlockSpec(memory_space=pltpu.VMEM)
    res = pl.pallas_call(
        body,
        name="adamw_small",
        in_specs=[vmem] * (4 * n),
        out_specs=[vmem] * (3 * n),
        out_shape=[jax.ShapeDtypeStruct(w.shape, F32) for w in ws] * 3,
    )(*gs, *ws, *ms, *vs)
    return res[:n], res[n:2 * n], res[2 * n:]


EARLY = ["w_in", "w_lora_w", "a_lora_w", "g_lora_w"]
LATE = ["w_ffn1", "w_ffn2", "w_proj_b", "w_out", "w_proj_a"]
LORAS = ["w_lora_w", "a_lora_w", "g_lora_w"]
HALF_W = 512
PIECE_ROWS = {"w_in": 1864, "w_ffn1": 1024, "w_ffn2": 1024, "w_proj_a": 256, "w_proj_b": 256, "w_out": 256,
              "w_lora_w": 32, "a_lora_w": 32, "g_lora_w": 80}
PIECE_OFF = {"w_in": 0, "w_lora_w": 1920, "a_lora_w": 1952, "g_lora_w": 2000,
             "w_ffn1": 0, "w_ffn2": 1024, "w_proj_b": 2048, "w_out": 2304, "w_proj_a": 2560}
LO_OFF = 2080


def pack_rows(group):
    return 2304 if group is EARLY else 2816
SHARD_AXIS = {"w_in": 1, "w_proj_a": 0, "w_lora_w": 1, "a_lora_w": 1, "g_lora_w": 1, "w_proj_b": 0, "w_out": 0,
              "w_ffn1": 1, "w_ffn2": 0}
SHARD_SHAPE = {"w_in": (1024, 1864), "w_proj_a": (256, 1024), "w_lora_w": (64, 256), "a_lora_w": (64, 256),
               "g_lora_w": (160, 256), "w_proj_b": (256, 1024), "w_out": (256, 1024), "w_ffn1": (1024, 1024),
               "w_ffn2": (1024, 1024)}
SHIFT_SHARD = (2, 840)
VECTORS = ["g_mix", "sgu_ln_w", "sgu_ln_b", "w0", "a0", "k_k", "k_a", "r_k", "ln_x_w", "ln_x_b", "g_ffn", "g_final"]
SMALL = VECTORS + ["sgu_w", "sgu_b"]
SMALL_SHAPE = {**{n: (1, 1024) for n in VECTORS}, "sgu_w": (8, 128, 128), "sgu_b": (8, 128)}
WEIGHTS = ["g_mix", "w_in", "sgu_ln_w", "sgu_ln_b", "sgu_w", "sgu_b", "w_proj_a", "shift_b", "w_lora_w", "w0",
           "a_lora_w", "a0", "g_lora_w", "k_k", "k_a", "r_k", "ln_x_w", "ln_x_b", "w_proj_b", "w_out", "g_ffn",
           "w_ffn1", "w_ffn2", "g_final"]


def _size(shape):
    n = 1
    for s in shape:
        n *= s
    return n


def _pack_rows(parts, rows, dtype):
    flat = jnp.concatenate([p.reshape(-1).astype(dtype) for p in parts])
    return jnp.concatenate([flat, jnp.zeros((rows * 1024 - flat.shape[0],), dtype)]).reshape(rows, 1024)


def _unpack_rows(packed, shapes):
    flat = packed.reshape(-1)
    out, off = [], 0
    for shp in shapes:
        out.append(flat[off:off + _size(shp)].reshape(shp))
        off += _size(shp)
    return out


def _shard_of(name, full, j):
    ax = SHARD_AXIS[name]
    n = SHARD_SHAPE[name][ax]
    return lax.slice_in_dim(full, j * n, (j + 1) * n, axis=ax)


def _pad_cols(z, n):
    return jnp.concatenate([z, jnp.zeros((z.shape[0], n - z.shape[1]), z.dtype)], axis=1)


def _row_form(name, s):
    return s.T if name == "w_in" else s


def _half_piece(name, rf, h):
    if name in LORAS:
        r = PIECE_ROWS[name]
        return _pad_cols(rf[h * r:(h + 1) * r], HALF_W)
    return rf[:, HALF_W * h:HALF_W * (h + 1)]


def _pack_half(group, rf_fn, h, dtype, tail=()):
    parts, pos, rows = [], 0, pack_rows(group)
    for n in group:
        if PIECE_OFF[n] > pos:
            parts.append(jnp.zeros((PIECE_OFF[n] - pos, HALF_W), dtype))
        parts.append(_half_piece(n, rf_fn(n), h).astype(dtype))
        pos = PIECE_OFF[n] + PIECE_ROWS[n]
    for t in tail:
        parts.append(t)
        pos += t.shape[0]
    parts.append(jnp.zeros((rows - pos, HALF_W), dtype))
    return jnp.concatenate(parts, axis=0)


def _piece(pack, name):
    return pack[PIECE_OFF[name]:PIECE_OFF[name] + PIECE_ROWS[name]]


def _join_halves(name, p0, p1):
    if name in LORAS:
        return jnp.concatenate([p0[:, :SHARD_SHAPE[name][1]], p1[:, :SHARD_SHAPE[name][1]]], axis=0)
    return jnp.concatenate([p0, p1], axis=1)


def _grad_row_form(name, full, j):
    if name == "w_in":
        return full[SHARD_SHAPE[name][1] * j:SHARD_SHAPE[name][1] * (j + 1)]
    return _shard_of(name, full, j)


def adamw_weight(name, g_own, g_other, w, m, v):
    rows, width = w.shape
    if name in LORAS:
        tm = PIECE_ROWS[name]
        grid = (2, 1)
        native = pl.BlockSpec((tm, width), lambda h, i: (h, 0))
    elif name == "w_in":
        tm, lanes = rows, 256
        grid = (2, HALF_W // lanes)
        native = pl.BlockSpec((tm, lanes), lambda h, i: (0, h * (HALF_W // lanes) + i))
    else:
        tm = rows
        grid = (2, 1)
        native = pl.BlockSpec((tm, HALF_W), lambda h, i: (i, h))
    assert PIECE_OFF[name] % tm == 0
    off = PIECE_OFF[name] // tm
    if name == "w_in":
        packed = pl.BlockSpec((tm, lanes), lambda h, i: (0, i))
    else:
        packed = pl.BlockSpec((tm, HALF_W), lambda h, i: (off + i, 0))

    def body(go_ref, gx_ref, w_ref, m_ref, v_ref, g_ref, d_ref, nm_ref, nv_ref):
        g = jnp.where(pl.program_id(0) == lax.axis_index("c"), go_ref[...], gx_ref[...])[:, :w_ref.shape[1]]
        delta, nm, nv = f_adamw(g, w_ref[...], m_ref[...], v_ref[...])
        g_ref[...] = g
        d_ref[...] = delta
        nm_ref[...] = nm
        nv_ref[...] = nv

    return pl.pallas_call(
        body,
        name="adamw_" + name,
        grid=grid,
        in_specs=[packed, packed, native, native, native],
        out_specs=[native] * 4,
        out_shape=[jax.ShapeDtypeStruct(w.shape, F32)] * 4,
        compiler_params=_cparams(2),
    )(g_own, g_other, w, m, v)


def kernel(x, g_mix, w_in, sgu_ln_w, sgu_ln_b, sgu_w, sgu_b, w_proj_a, shift_b, w_lora_w, w0, a_lora_w, a0, g_lora_w, k_k, k_a, r_k, ln_x_w, ln_x_b, w_proj_b, w_out, g_ffn, w_ffn1, w_ffn2, g_final, loss_target, m_g_mix, m_w_in, m_sgu_ln_w, m_sgu_ln_b, m_sgu_w, m_sgu_b, m_w_proj_a, m_shift_b, m_w_lora_w, m_w0, m_a_lora_w, m_a0, m_g_lora_w, m_k_k, m_k_a, m_r_k, m_ln_x_w, m_ln_x_b, m_w_proj_b, m_w_out, m_g_ffn, m_w_ffn1, m_w_ffn2, m_g_final, v_g_mix, v_w_in, v_sgu_ln_w, v_sgu_ln_b, v_sgu_w, v_sgu_b, v_w_proj_a, v_shift_b, v_w_lora_w, v_w0, v_a_lora_w, v_a0, v_g_lora_w, v_k_k, v_k_a, v_r_k, v_ln_x_w, v_ln_x_b, v_w_proj_b, v_w_out, v_g_ffn, v_w_ffn1, v_w_ffn2, v_g_final):
    given = dict(zip(WEIGHTS, (g_mix, w_in, sgu_ln_w, sgu_ln_b, sgu_w, sgu_b, w_proj_a, shift_b, w_lora_w, w0, a_lora_w, a0, g_lora_w, k_k, k_a, r_k, ln_x_w, ln_x_b, w_proj_b, w_out, g_ffn, w_ffn1, w_ffn2, g_final)))
    mom_m = dict(zip(WEIGHTS, (m_g_mix, m_w_in, m_sgu_ln_w, m_sgu_ln_b, m_sgu_w, m_sgu_b, m_w_proj_a, m_shift_b, m_w_lora_w, m_w0, m_a_lora_w, m_a0, m_g_lora_w, m_k_k, m_k_a, m_r_k, m_ln_x_w, m_ln_x_b, m_w_proj_b, m_w_out, m_g_ffn, m_w_ffn1, m_w_ffn2, m_g_final)))
    mom_v = dict(zip(WEIGHTS, (v_g_mix, v_w_in, v_sgu_ln_w, v_sgu_ln_b, v_sgu_w, v_sgu_b, v_w_proj_a, v_shift_b, v_w_lora_w, v_w0, v_a_lora_w, v_a0, v_g_lora_w, v_k_k, v_k_a, v_r_k, v_ln_x_w, v_ln_x_b, v_w_proj_b, v_w_out, v_g_ffn, v_w_ffn1, v_w_ffn2, v_g_final)))
    chip = 2 * lax.axis_index("x") + lax.axis_index("y")

    def local_block(tree, n):
        return tree[n] if n == "g_final" else tree[n][0]

    sb = local_block(given, "shift_b")
    lo_part = lambda z: (z - z.astype(BF16).astype(F32)).astype(BF16)
    row_form = lambda tree: (lambda n: _row_form(n, local_block(tree, n)))
    tile16 = lambda z: jnp.pad(z, ((0, 16 - z.shape[0]), (0, HALF_W - z.shape[1])))
    sb_tiles = [tile16(f(sb[:, lanes])) for f in (lambda z: z.astype(BF16), lo_part)
                for lanes in (slice(0, HALF_W), slice(HALF_W, None))]
    tails = [[_half_piece(n, lo_part(local_block(given, n)), h) for n in LORAS] + sb_tiles for h in range(2)]
    pack_w = jnp.stack([_pack_half(EARLY, row_form(given), h, BF16, tails[h]) for h in range(2)])
    gathered, gathered_token = gather_shards(pack_w)
    gathered = lax.dynamic_update_index_in_dim(gathered, pack_w, chip, 0)
    pack_late = jnp.stack([_pack_half(LATE, row_form(given), h, BF16) for h in range(2)])
    late_state, late_token = split_start("gather_start", _gather_copies, 3, pack_late, (N_CHIPS,) + pack_late.shape,
                                         gathered_token)

    def whole(group, got, own):
        half = lambda n, j, h: jnp.where(chip == j, _piece(own[h], n), _piece(got[j, h], n))
        shard = lambda n, j: _join_halves(n, half(n, j, 0), half(n, j, 1))
        return {n: jnp.concatenate([shard(n, j) for j in range(N_CHIPS)],
                                   axis=0 if n == "w_in" else SHARD_AXIS[n]) for n in group}

    w = whole(EARLY, gathered, pack_w)
    late_weights = lambda after: whole(
        LATE, gather_forward(split_wait("gather_wait", _gather_copies, late_state, after)[1]), pack_late)
    off = LO_OFF
    for n in LORAS:
        r, cols = PIECE_ROWS[n], SHARD_SHAPE[n][1]
        lo = jnp.concatenate([jnp.concatenate([gathered[j, 0, off:off + r, :cols], gathered[j, 1, off:off + r, :cols]],
                                              axis=0) for j in range(N_CHIPS)], axis=1)
        w[n] = w[n].astype(F32) + lo.astype(F32)
        off += r
    sb_tile = lambda j, t, lanes: gathered[j, 0, off + 16 * t:off + 16 * t + 2, :lanes].astype(F32)
    rest = SHIFT_SHARD[1] - HALF_W
    w["shift_b"] = jnp.concatenate(
        [jnp.concatenate([sb_tile(j, 0, HALF_W) + sb_tile(j, 2, HALF_W), sb_tile(j, 1, rest) + sb_tile(j, 3, rest)],
                         axis=1) for j in range(N_CHIPS)], axis=1)
    for n in SMALL:
        w[n] = local_block(given, n).reshape(SMALL_SHAPE[n])

    def pair_start(g_pack, tag):
        return split_start("reduce_pair_start_" + tag, _pair_copies, N_CHIPS, g_pack, (N_CHIPS,) + g_pack.shape[2:])

    def pair_finish(state, after, tag):
        g_pack, got = split_wait("reduce_pair_wait_" + tag, _pair_copies, state, after)
        return pair_sum(g_pack, got, tag, tm=got.shape[1] // 2)

    pack_early = lambda g: jnp.stack([jnp.stack([_pack_half(EARLY, lambda n: _grad_row_form(n, g[n], j), h, F32)
                                                 for h in range(2)]) for j in range(N_CHIPS)])
    loss, grad_x, grads, (late_part, late_slots), early_state = local_step(
        x[0], loss_target[0], w, late_token, late_weights, pair_start, pair_finish, pack_early)

    early_part, early_part16 = pair_finish(early_state, grad_x, "early")
    s_pack = _pack_rows([grads[n] for n in SMALL] + [grads["shift_b"], loss.reshape(1, 1)], SMALL_ROWS, F32)
    chips_state, token = split_start("reduce_chips_start", _chip_copies, 3, early_part16, early_part16.shape)
    out_g, out_d, out_m, out_v = {}, {}, {}, {}

    def finish(group, tag, part, slots):
        half_sum = sum_with_own(part, slots, chip, token, tm=part.shape[1] // 2, name="chip_sum_" + tag)
        other_half = exchange_halves(half_sum, tag)
        for n in group:
            res = adamw_weight(n, half_sum, other_half,
                               *[_row_form(n, local_block(t, n)) for t in (given, mom_m, mom_v)])
            for tree, z in zip((out_g, out_d, out_m, out_v), res):
                tree[n] = _row_form(n, z)

    finish(LATE, "late", late_part, late_slots)

    small_shapes = [SMALL_SHAPE[n] for n in SMALL]
    g_small = sum_all(s_pack, token)
    *g_parts, loss = _unpack_rows(g_small, small_shapes + [(2, N_RWKV), ()])
    out_g.update(zip(SMALL, g_parts[:-1]))
    g_sb = lax.dynamic_slice_in_dim(g_parts[-1], chip * SHIFT_SHARD[1], SHIFT_SHARD[1], axis=1)
    out_g["shift_b"] = g_sb
    names = SMALL + ["shift_b"]
    native = lambda tree: [local_block(tree, n).reshape(SMALL_SHAPE.get(n, SHIFT_SHARD)) for n in names]
    small_res = adamw_many(g_parts[:-1] + [g_sb], native(given), native(mom_m), native(mom_v))
    for tree, res in zip((out_d, out_m, out_v), small_res):
        tree.update(zip(names, res))

    after = (out_v["w_out"], out_v["sgu_w"])
    early_slots = split_wait("reduce_chips_wait", _chip_copies, chips_state,
                             jnp.concatenate([z.reshape(-1)[:8] for z in after]))[1]
    finish(EARLY, "early", early_part, early_slots)

    def block_of(tree, n):
        return tree[n].reshape(given[n].shape)

    return (loss, grad_x[None], *[block_of(out_g, n) for n in WEIGHTS], *[block_of(out_d, n) for n in WEIGHTS],
            *[block_of(out_m, n) for n in WEIGHTS], *[block_of(out_v, n) for n in WEIGHTS])
```

```python
import functools

import jax
import jax.numpy as jnp
from jax import lax
from jax.experimental import pallas as pl
from jax.experimental.pallas import tpu as pltpu

F32 = jnp.float32
BF16 = jnp.bfloat16

D_MODEL = 1024
N_HEADS = 16
HEAD = 64
SCAN_CHUNK = 64

VMEM_LIMIT = 56 * 1024 * 1024


_BDIMS = {
    "nn": (((2,), (1,)), ((0,), (0,))),
    "nt": (((2,), (2,)), ((0,), (0,))),
    "tn": (((1,), (1,)), ((0,), (0,))),
}


def _raw_bdot(x, y, mode, fine):
    if fine:
        return lax.dot_general(x, y, _BDIMS[mode], precision=lax.Precision.HIGH, preferred_element_type=F32)
    return lax.dot_general(x.astype(BF16), y.astype(BF16), _BDIMS[mode], preferred_element_type=F32)


@functools.partial(jax.custom_vjp, nondiff_argnums=(2, 3))
def bdot(x, y, mode, fine=True):
    return _raw_bdot(x, y, mode, fine)


def _bdot_fwd(x, y, mode, fine):
    return _raw_bdot(x, y, mode, fine), (x, y)


def _bdot_bwd(mode, fine, res, g):
    x, y = res
    if mode == "nn":
        return bdot(g, y, "nt", fine), bdot(x, g, "tn", fine)
    if mode == "nt":
        return bdot(g, y, "nn", fine), bdot(g, x, "tn", fine)
    return bdot(y, g, "nt", fine), bdot(x, g, "nn", fine)


bdot.defvjp(_bdot_fwd, _bdot_bwd)


def _scan_chunk(S0, r, lw, k, v, a, b):
    nh, lc, _ = r.shape
    ti = lax.broadcasted_iota(jnp.int32, (lc, lc), 0)
    si = lax.broadcasted_iota(jnp.int32, (lc, lc), 1)
    incl = (si <= ti).astype(F32)
    strict = (si < ti).astype(F32)
    eye = (si == ti).astype(F32)
    cl = bdot(jnp.broadcast_to(incl, (nh, lc, lc)), lw, "nn")
    cl_last = cl[:, lc - 1:lc, :]
    g_last = jnp.exp(cl_last - cl)
    at = a * jnp.exp(cl - lw)
    bt = b * jnp.exp(-cl)
    kt = k * jnp.exp(-cl)
    rt = r * jnp.exp(cl)
    ar = jnp.concatenate([at, rt], axis=1)
    ar_b = bdot(ar, bt, "nt", False)
    ar_k = bdot(ar, kt, "nt", False)
    m_ab, m_rb = ar_b[:, :lc] * strict, ar_b[:, lc:] * incl
    m_ak, m_rk = ar_k[:, :lc] * strict, ar_k[:, lc:] * incl
    x = eye + m_ab
    p = bdot(m_ab, m_ab, "nn", False)
    n = 2
    while n * 2 < lc:
        px = bdot(jnp.concatenate([p, x], axis=1), p, "nn", False)
        p = px[:, :lc]
        x = x + px[:, lc:]
        n *= 2
    x = x + bdot(x, p, "nn", False)
    ar_s = bdot(ar, S0, "nt", False)
    akrk_v = bdot(jnp.concatenate([m_ak, m_rk], axis=1), v, "nn", False)
    u = bdot(x, ar_s[:, :lc] + akrk_v[:, :lc], "nn", False)
    o = ar_s[:, lc:] + bdot(m_rb, u, "nn", False) + akrk_v[:, lc:]
    s_last = S0 * jnp.exp(cl_last) + bdot(jnp.concatenate([u, v], axis=1),
                                          jnp.concatenate([b * g_last, k * g_last], axis=1), "tn", False)
    return o, s_last


def _split_heads(z):
    return jnp.stack([z[:, HEAD * h:HEAD * (h + 1)] for h in range(N_HEADS)], axis=0)


def _merge_heads(z):
    return jnp.concatenate([z[h] for h in range(N_HEADS)], axis=1)


def _scan_specs(t, ops, rev):
    nc = t // SCAN_CHUNK
    row = (lambda c: nc - 1 - c) if rev else (lambda c: c)
    specs = [pl.BlockSpec((SCAN_CHUNK, D_MODEL), lambda c, cb=cb: (row(c), cb)) for _, cb in ops]
    state = pl.BlockSpec((1, N_HEADS, HEAD, HEAD), lambda c: (row(c), 0, 0, 0))
    return nc, specs, state


def scan_fwd(ops):
    t = ops[0][0].shape[0]
    nc, specs, state = _scan_specs(t, ops, False)

    def body(r_ref, lw_ref, k_ref, v_ref, a_ref, b_ref, o_ref, s0_ref, s_scr):
        @pl.when(pl.program_id(0) == 0)
        def _():
            s_scr[...] = jnp.zeros_like(s_scr)

        s0 = s_scr[...]
        s0_ref[0] = s0
        o, s_last = _scan_chunk(s0, *[_split_heads(z[...]) for z in (r_ref, lw_ref, k_ref, v_ref, a_ref, b_ref)])
        o_ref[...] = _merge_heads(o)
        s_scr[...] = s_last

    return pl.pallas_call(
        body,
        name="scan_fwd",
        grid=(nc,),
        in_specs=specs,
        out_specs=[pl.BlockSpec((SCAN_CHUNK, D_MODEL), lambda c: (c, 0)), state],
        out_shape=[jax.ShapeDtypeStruct((t, D_MODEL), F32), jax.ShapeDtypeStruct((nc, N_HEADS, HEAD, HEAD), F32)],
        scratch_shapes=[pltpu.VMEM((N_HEADS, HEAD, HEAD), F32)],
        compiler_params=_cparams(1),
    )(*[a for a, _ in ops])


def scan_bwd(ops, s0s, do, part):
    t = ops[0][0].shape[0]
    nc, specs, state = _scan_specs(t, ops + [(do, 0)], True)

    def body(r_ref, lw_ref, k_ref, v_ref, a_ref, b_ref, do_ref, s0_ref, part_ref, *rest):
        out_refs, slots_ref, ds_scr, send_sems, recv_sems = rest[:6], rest[6], rest[7], rest[8], rest[9]
        step = pl.program_id(0)
        x, y, c = _coords()
        me = 2 * x + y
        chips = _other_chips(x, y)
        sends = [_remote(part_ref.at[2 * cx + cy], slots_ref.at[me], send_sems, recv_sems, k, (cx, cy, c))
                 for k, (cx, cy) in enumerate(chips)]

        @pl.when(step == 0)
        def _():
            ds_scr[...] = jnp.zeros_like(ds_scr)
            for cp in sends:
                cp.start()

        _, vjp = jax.vjp(_scan_chunk, s0_ref[0],
                         *[_split_heads(z[...]) for z in (r_ref, lw_ref, k_ref, v_ref, a_ref, b_ref)])
        grads = vjp((_split_heads(do_ref[...]), ds_scr[...]))
        for o_ref, g in zip(out_refs, grads[1:]):
            o_ref[...] = _merge_heads(g)
        ds_scr[...] = grads[0]

        @pl.when(step == nc - 1)
        def _():
            for k, (cx, cy) in enumerate(chips):
                _remote(part_ref.at[me], slots_ref.at[2 * cx + cy], send_sems, recv_sems, k, (cx, cy, c)).wait_recv()
            for cp in sends:
                cp.wait_send()

    return pl.pallas_call(
        body,
        name="scan_bwd",
        grid=(nc,),
        in_specs=specs + [state, _ANY],
        out_specs=[pl.BlockSpec((SCAN_CHUNK, D_MODEL), lambda c: (nc - 1 - c, 0))] * 6 + [_ANY],
        out_shape=[jax.ShapeDtypeStruct((t, D_MODEL), F32)] * 6 + [jax.ShapeDtypeStruct(part.shape, part.dtype)],
        scratch_shapes=[pltpu.VMEM((N_HEADS, HEAD, HEAD), F32), pltpu.SemaphoreType.DMA((3,)),
                        pltpu.SemaphoreType.DMA((3,))],
        compiler_params=_cparams(1),
    )(*[a for a, _ in ops], do, s0s, part)


_MDIMS = {
    "nn": (((1,), (0,)), ((), ())),
    "nt": (((1,), (1,)), ((), ())),
    "tn": (((0,), (0,)), ((), ())),
}


def _raw_mdot(x, y, mode, exact):
    if exact:
        return lax.dot_general(x, y, _MDIMS[mode], precision=lax.Precision.HIGH, preferred_element_type=F32)
    return lax.dot_general(x.astype(BF16), y.astype(BF16), _MDIMS[mode], preferred_element_type=F32)


@functools.partial(jax.custom_vjp, nondiff_argnums=(2, 3))
def mdot(x, y, mode, exact):
    return _raw_mdot(x, y, mode, exact)


def _mdot_fwd(x, y, mode, exact):
    return _raw_mdot(x, y, mode, exact), (x, y)


def _mdot_bwd(mode, exact, res, g):
    x, y = res
    if mode == "nn":
        return mdot(g, y, "nt", exact), mdot(x, g, "tn", exact)
    if mode == "nt":
        return mdot(g, y, "nn", exact), mdot(g, x, "tn", exact)
    return mdot(y, g, "nt", exact), mdot(x, g, "nn", exact)


mdot.defvjp(_mdot_fwd, _mdot_bwd)


def _seg_ones():
    i = lax.broadcasted_iota(jnp.int32, (256, 256), 0) // HEAD
    j = lax.broadcasted_iota(jnp.int32, (256, 256), 1) // HEAD
    return (i == j).astype(BF16)


@jax.custom_vjp
def segsum(x):
    bd = _seg_ones()
    hi = x.astype(BF16)
    lo = (x - hi.astype(F32)).astype(BF16)
    cols = []
    for j in range(x.shape[1] // 256):
        sl = slice(256 * j, 256 * (j + 1))
        cols.append(jnp.dot(hi[:, sl], bd, preferred_element_type=F32)
                    + jnp.dot(lo[:, sl], bd, preferred_element_type=F32))
    return jnp.concatenate(cols, axis=1)


segsum.defvjp(lambda x: (segsum(x), None), lambda _, g: (segsum(g),))


NORM_EPS = 1e-6
LN_EPS = 1e-5
GN_EPS = 64e-5
SGU_CHUNK = 128
SGU_GROUPS = 8


def _rms(x, g):
    return x * lax.rsqrt(jnp.mean(x * x, axis=-1, keepdims=True) + NORM_EPS) * g


def f_norm_in(x, g):
    return _rms(x, g), x


def f_sgu(p, ln_w, ln_b, sw, sbt):
    tm = p.shape[0]
    z = 0.5 * p * (1.0 + lax.erf(p * 0.7071067811865476))
    u, v = z[:, :D_MODEL], z[:, D_MODEL:]
    mu = jnp.mean(v, axis=-1, keepdims=True)
    d = v - mu
    vn = d * lax.rsqrt(jnp.mean(d * d, axis=-1, keepdims=True) + LN_EPS) * ln_w + ln_b
    ii = lax.broadcasted_iota(jnp.int32, (SGU_CHUNK, SGU_CHUNK), 0)
    jj = lax.broadcasted_iota(jnp.int32, (SGU_CHUNK, SGU_CHUNK), 1)
    mask = (jj <= ii).astype(F32)
    gi = lax.broadcasted_iota(jnp.int32, (SGU_GROUPS, D_MODEL), 0)
    ci = lax.broadcasted_iota(jnp.int32, (SGU_GROUPS, D_MODEL), 1) // SGU_CHUNK
    bias = mdot(sbt, (gi == ci).astype(F32), "nn", True)
    rows = []
    for c in range(tm // SGU_CHUNK):
        cols = []
        for g in range(SGU_GROUPS):
            blk = vn[c * SGU_CHUNK:(c + 1) * SGU_CHUNK, g * SGU_CHUNK:(g + 1) * SGU_CHUNK]
            cols.append(mdot(sw[g] * mask, blk, "nn", False))
        rows.append(jnp.concatenate(cols, axis=1) + bias)
    return (u * jnp.concatenate(rows, axis=0),)


def _softplus(x):
    return jnp.maximum(x, 0.0) + jnp.log1p(jnp.exp(-jnp.abs(x)))


def f_pre(q, wl, w0, al, a0, gl, k_k, k_a):
    qr, qk, qv, ql = q[:, :1024], q[:, 1024:2048], q[:, 2048:3072], q[:, 3072:]
    return _f_pre(qr, qk, qv, ql, wl, w0, al, a0, gl, k_k, k_a)


def _f_pre(qr, qk, qv, ql, wl, w0, al, a0, gl, k_k, k_a):
    xw, xa, xg = ql[:, :128], ql[:, 128:256], ql[:, 256:512]
    wr = -_softplus(-(w0 + mdot(jnp.tanh(xw), wl, "nn", False))) - 0.5
    lw = -jnp.exp(wr)
    aa = jax.nn.sigmoid(a0 + mdot(xa, al, "nn", False))
    g = mdot(jax.nn.sigmoid(xg), gl, "nn", False)
    kkr = qk * k_k
    kk = kkr / jnp.maximum(jnp.sqrt(segsum(kkr * kkr)), 1e-12)
    kp = qk * (1.0 + (aa - 1.0) * k_a)
    return qr, lw, kp, qv, -kk, kk * aa, g, qr, kp, qv


def f_post(o, r, kp, v, g, lnw, lnb, rk):
    mu = segsum(o) * (1.0 / HEAD)
    d = o - mu
    gn = d * lax.rsqrt(segsum(d * d) * (1.0 / HEAD) + GN_EPS)
    return ((gn * lnw + lnb + segsum(r * kp * rk) * v) * g,)


def f_mix(ya, yb, ga, gb):
    return (jax.nn.sigmoid(ga) * ya + jax.nn.sigmoid(gb) * yb,)


def f_ffn_in(h1, g):
    return _rms(h1, g), h1


def f_final(h1, m3, tgt, g):
    y = _rms(h1 + m3, g)
    err = jnp.square(y - tgt)
    return 0.5 * jnp.sum(jnp.mean(err, axis=-1))


def _cparams(n_grid):
    return pltpu.CompilerParams(dimension_semantics=("arbitrary",) * n_grid, vmem_limit_bytes=VMEM_LIMIT)


def _tile_spec(tm, w, cb):
    return pl.BlockSpec((tm, w), lambda i: (i, cb))


def _const_spec(c):
    nd = c.ndim
    return pl.BlockSpec(c.shape, lambda i: (0,) * nd)


def ew_call(fn, tiled, consts, outs, *, tm, name):
    t = tiled[0][0].shape[0]
    n_t, n_c = len(tiled), len(consts)

    def body(*refs):
        tv = [r[...].astype(F32) for r in refs[:n_t]]
        cv = [r[...] for r in refs[n_t:n_t + n_c]]
        res = fn(*tv, *cv)
        for o_ref, val in zip(refs[n_t + n_c:], res):
            o_ref[...] = val.astype(o_ref.dtype)

    return pl.pallas_call(
        body,
        name=name,
        grid=(t // tm,),
        in_specs=[_tile_spec(tm, w, cb) for _, w, cb in tiled] + [_const_spec(c) for c in consts],
        out_specs=[_tile_spec(tm, w, 0) for w, _ in outs],
        out_shape=[jax.ShapeDtypeStruct((t, w), dt) for w, dt in outs],
        compiler_params=_cparams(1),
    )(*[a for a, _, _ in tiled], *consts)


def ew_vjp_call(fn, tiled, consts, cots, d_tiled, d_consts, *, tm, name):
    t = tiled[0][0].shape[0]
    n_t, n_c, n_g = len(tiled), len(consts), len(cots)
    dt_list = [(i, dt) for i, dts in enumerate(d_tiled) for dt in dts]
    dc_list = [i for i, want in enumerate(d_consts) if want]

    def body(*refs):
        tv = [r[...].astype(F32) for r in refs[:n_t]]
        cv = [r[...] for r in refs[n_t:n_t + n_c]]
        gv = tuple(r[...].astype(F32) for r in refs[n_t + n_c:n_t + n_c + n_g])
        out_refs = refs[n_t + n_c + n_g:]
        _, vjp = jax.vjp(fn, *tv, *cv)
        grads = vjp(gv)
        for o_ref, (i, _) in zip(out_refs, dt_list):
            o_ref[...] = grads[i].astype(o_ref.dtype)
        acc_refs = out_refs[len(dt_list):]

        @pl.when(pl.program_id(0) == 0)
        def _():
            for a_ref in acc_refs:
                a_ref[...] = jnp.zeros_like(a_ref)

        for a_ref, i in zip(acc_refs, dc_list):
            a_ref[...] += grads[n_t + i]

    res = pl.pallas_call(
        body,
        name=name,
        grid=(t // tm,),
        in_specs=[_tile_spec(tm, w, cb) for _, w, cb in tiled] + [_const_spec(c) for c in consts]
        + [_tile_spec(tm, w, cb) for _, w, cb in cots],
        out_specs=[_tile_spec(tm, tiled[i][1], 0) for i, _ in dt_list] + [_const_spec(consts[i]) for i in dc_list],
        out_shape=[jax.ShapeDtypeStruct((t, tiled[i][1]), dt) for i, dt in dt_list]
        + [jax.ShapeDtypeStruct(consts[i].shape, F32) for i in dc_list],
        compiler_params=_cparams(1),
    )(*[a for a, _, _ in tiled], *consts, *[a for a, _, _ in cots])
    return res[:len(dt_list)], res[len(dt_list):]


def mm(a, b, mode, *, tm, tn, name, out_dtypes=(F32,), epi=None, extras=(), into=None):
    m = a.shape[1] if mode == "tn" else a.shape[0]
    kd = a.shape[0] if mode == "tn" else a.shape[1]
    n = b.shape[0] if mode == "nt" else b.shape[1]
    tm, tn = min(tm, m), min(tn, n)
    if mode == "nn":
        a_spec = pl.BlockSpec((tm, kd), lambda i, j: (i, 0))
        b_spec = pl.BlockSpec((kd, tn), lambda i, j: (0, j))
    elif mode == "nt":
        a_spec = pl.BlockSpec((tm, kd), lambda i, j: (i, 0))
        b_spec = pl.BlockSpec((tn, kd), lambda i, j: (j, 0))
    else:
        a_spec = pl.BlockSpec((kd, tm), lambda i, j: (0, i))
        b_spec = pl.BlockSpec((kd, tn), lambda i, j: (0, j))
    n_e = len(extras)
    o_spec = pl.BlockSpec((tm, tn), lambda i, j: (i, j))

    if into is not None:
        buf, place = into

        def body_into(a_ref, b_ref, buf_ref, o_ref):
            o_ref[0, 0] = lax.dot_general(a_ref[...].astype(BF16), b_ref[...].astype(BF16), _MDIMS[mode],
                                          preferred_element_type=F32)

        return pl.pallas_call(
            body_into,
            name=name,
            grid=(m // tm, n // tn),
            in_specs=[a_spec, b_spec, pl.BlockSpec(memory_space=pl.ANY)],
            out_specs=pl.BlockSpec((1, 1, tm, tn), lambda i, j: (*place(i, j), 0)),
            out_shape=jax.ShapeDtypeStruct(buf.shape, F32),
            input_output_aliases={2: 0},
            compiler_params=_cparams(2),
        )(a, b, buf)

    def body(a_ref, b_ref, *refs):
        c = lax.dot_general(a_ref[...].astype(BF16), b_ref[...].astype(BF16), _MDIMS[mode],
                            preferred_element_type=F32)
        res = epi(c, *[r[...] for r in refs[:n_e]]) if epi is not None else (c,)
        for o_ref, val in zip(refs[n_e:], res):
            o_ref[...] = val.astype(o_ref.dtype)

    res = pl.pallas_call(
        body,
        name=name,
        grid=(m // tm, n // tn),
        in_specs=[a_spec, b_spec] + [o_spec] * n_e,
        out_specs=[o_spec] * len(out_dtypes),
        out_shape=[jax.ShapeDtypeStruct((m, n), dt) for dt in out_dtypes],
        compiler_params=_cparams(2),
    )(a, b, *extras)
    return res if len(out_dtypes) > 1 else res[0]


RWKV_COL0 = 4096
RWKV_WIDTH = 3584
SHIFT_BLK = 512


def _shift_down(p, prev_row):
    rows = lax.broadcasted_iota(jnp.int32, p.shape, 0)
    return jnp.where(rows == 0, prev_row, pltpu.roll(p, 1, 0))


def shiftmix_fwd(p_all, sbp, *, tm):
    t = p_all.shape[0]
    tm = min(tm, t)
    c0 = RWKV_COL0 // SHIFT_BLK
    hb = tm // 8

    def body(p_ref, halo_ref, sb_ref, q_ref):
        p = p_ref[...]
        prev = jnp.where(pl.program_id(0) == 0, 0.0, halo_ref[7:8, :])
        q_ref[...] = p * sb_ref[0:1, :] + _shift_down(p, prev) * sb_ref[1:2, :]

    return pl.pallas_call(
        body,
        name="shiftmix_fwd",
        grid=(t // tm, RWKV_WIDTH // SHIFT_BLK),
        in_specs=[
            pl.BlockSpec((tm, SHIFT_BLK), lambda i, j: (i, c0 + j)),
            pl.BlockSpec((8, SHIFT_BLK), lambda i, j: (jnp.maximum(i * hb - 1, 0), c0 + j)),
            pl.BlockSpec((2, SHIFT_BLK), lambda i, j: (0, j)),
        ],
        out_specs=pl.BlockSpec((tm, SHIFT_BLK), lambda i, j: (i, j)),
        out_shape=jax.ShapeDtypeStruct((t, RWKV_WIDTH), F32),
        compiler_params=_cparams(2),
    )(p_all, p_all, sbp)


def shiftmix_bwd(dq, col0, p_all, sbp, *, tm, name):
    t, w = dq.shape
    n_i = t // tm
    hb = tm // 8
    cq = col0 // SHIFT_BLK
    cp = (RWKV_COL0 + col0) // SHIFT_BLK

    def body(dq_ref, dqn_ref, p_ref, ph_ref, sb_ref, dp_ref, dsb_ref):
        i = pl.program_id(1)
        dq_t = dq_ref[...]
        rows = lax.broadcasted_iota(jnp.int32, dq_t.shape, 0)
        nxt = jnp.where(i == n_i - 1, 0.0, dqn_ref[0:1, :])
        up = jnp.where(rows == tm - 1, nxt, pltpu.roll(dq_t, tm - 1, 0))
        dp_ref[...] = (dq_t * sb_ref[0:1, :] + up * sb_ref[1:2, :]).astype(dp_ref.dtype)
        p = p_ref[...]
        prev = jnp.where(i == 0, 0.0, ph_ref[7:8, :])
        s0 = jnp.sum(dq_t * p, axis=0, keepdims=True)
        s1 = jnp.sum(dq_t * _shift_down(p, prev), axis=0, keepdims=True)
        two = lax.broadcasted_iota(jnp.int32, (2, SHIFT_BLK), 0)

        @pl.when(i == 0)
        def _():
            dsb_ref[...] = jnp.zeros_like(dsb_ref)

        dsb_ref[...] += jnp.where(two == 0, s0, s1)

    return pl.pallas_call(
        body,
        name=name,
        grid=(w // SHIFT_BLK, n_i),
        in_specs=[
            pl.BlockSpec((tm, SHIFT_BLK), lambda j, i: (i, j)),
            pl.BlockSpec((8, SHIFT_BLK), lambda j, i: (jnp.minimum((i + 1) * hb, t // 8 - 1), j)),
            pl.BlockSpec((tm, SHIFT_BLK), lambda j, i: (i, cp + j)),
            pl.BlockSpec((8, SHIFT_BLK), lambda j, i: (jnp.maximum(i * hb - 1, 0), cp + j)),
            pl.BlockSpec((2, SHIFT_BLK), lambda j, i: (0, cq + j)),
        ],
        out_specs=[
            pl.BlockSpec((tm, SHIFT_BLK), lambda j, i: (i, j)),
            pl.BlockSpec((2, SHIFT_BLK), lambda j, i: (0, j)),
        ],
        out_shape=[jax.ShapeDtypeStruct((t, w), BF16), jax.ShapeDtypeStruct((2, w), F32)],
        compiler_params=_cparams(2),
    )(dq, dq, p_all, p_all, sbp)


def final_call(h1, m3, tgt, g_final, *, tm):
    t = h1.shape[0]

    def body(h1_ref, m3_ref, tgt_ref, g_ref, dh_ref, dhb_ref, dg_ref, loss_ref):
        loss, vjp = jax.vjp(f_final, h1_ref[...], m3_ref[...], tgt_ref[...], g_ref[...])
        dh, _, _, dg = vjp(jnp.ones((), F32))
        dh_ref[...] = dh
        dhb_ref[...] = dh.astype(BF16)

        @pl.when(pl.program_id(0) == 0)
        def _():
            dg_ref[...] = jnp.zeros_like(dg_ref)
            loss_ref[...] = jnp.zeros_like(loss_ref)

        dg_ref[...] += dg
        loss_ref[...] += jnp.full(loss_ref.shape, loss, F32)

    tile = _tile_spec(tm, D_MODEL, 0)
    return pl.pallas_call(
        body,
        name="final_loss",
        grid=(t // tm,),
        in_specs=[tile, tile, tile, _const_spec(g_final)],
        out_specs=[tile, tile, _const_spec(g_final), pl.BlockSpec((8, 128), lambda i: (0, 0))],
        out_shape=[jax.ShapeDtypeStruct((t, D_MODEL), F32), jax.ShapeDtypeStruct((t, D_MODEL), BF16),
                   jax.ShapeDtypeStruct(g_final.shape, F32), jax.ShapeDtypeStruct((8, 128), F32)],
        compiler_params=_cparams(1),
    )(h1, m3, tgt, g_final)


N_SGU = 2048
N_RWKV = 3360
LORA_W, LORA_A, LORA_G = 64, 64, 160


def _pad_rwkv_cols(z):
    zero = lambda n: jnp.zeros(z.shape[:-1] + (n,), z.dtype)
    return jnp.concatenate([z[..., :3072], z[..., 3072:3136], zero(64), z[..., 3136:3200], zero(64),
                            z[..., 3200:3360], zero(96)], axis=-1)


def _unpad_rwkv_cols(z):
    return jnp.concatenate([z[..., :3072], z[..., 3072:3136], z[..., 3200:3264], z[..., 3328:3488]], axis=-1)


def _pad_win_rows(wt):
    z = wt[N_SGU:N_SGU + N_RWKV]
    zero = lambda n: jnp.zeros((n, wt.shape[1]), wt.dtype)
    return jnp.concatenate([wt[:N_SGU], wt[N_SGU + N_RWKV:], z[:3072], z[3072:3136], zero(64), z[3136:3200], zero(64),
                            z[3200:3360], zero(96)], axis=0)


def _unpad_win_rows(wt):
    z = wt[RWKV_COL0:]
    return jnp.concatenate([wt[:N_SGU], z[:3072], z[3072:3136], z[3200:3264], z[3328:3488], wt[N_SGU:RWKV_COL0]],
                           axis=0)


def _pad_rows(w, n):
    return jnp.concatenate([w, jnp.zeros((n - w.shape[0],) + w.shape[1:], w.dtype)], axis=0)


def _relu2_epi(c):
    return c, jnp.square(jnp.maximum(c, 0.0))


def _relu2_bwd_epi(c, hid):
    return (c * (2.0 * jnp.maximum(hid.astype(F32), 0.0)),)


def _add_epi(c, x):
    return (c + x,)


def _pre_fwd(*args):
    res = f_pre(*args)
    return res[1], res[2], res[4], res[5], res[6]


def local_step(x, tgt, w, late_token, late_weights, pair_start, pair_finish, pack_early):
    d = D_MODEL
    win_pt = _pad_win_rows(w["w_in"])
    sbp = _pad_rwkv_cols(w["shift_b"])
    wl = _pad_rows(w["w_lora_w"], 128)
    al = _pad_rows(w["a_lora_w"], 128)
    gl = _pad_rows(w["g_lora_w"], 256)
    sbt = w["sgu_b"].T

    (a_bf,) = ew_call(lambda x_, g_: (f_norm_in(x_, g_)[0],), [(x, d, 0)], [w["g_mix"] + late_token[:1, :1]],
                      [(d, BF16)], tm=512, name="norm_in")
    p_all = mm(a_bf, win_pt, "nt", tm=2048, tn=1280, name="mm_in")
    sgu_t = [(p_all, 2 * d, 0)]
    sgu_c = [w["sgu_ln_w"], w["sgu_ln_b"], w["sgu_w"], sbt]
    (s_bf,) = ew_call(f_sgu, sgu_t, sgu_c, [(d, BF16)], tm=512, name="sgu_fwd")
    q = shiftmix_fwd(p_all, sbp, tm=2048)
    pre_t = [(q, RWKV_WIDTH, 0)]
    pre_c = [wl, w["w0"], al, w["a0"], gl, w["k_k"], w["k_a"]]
    lw, kp, na, nb, g = ew_call(_pre_fwd, pre_t, pre_c, [(d, F32)] * 5, tm=256, name="rwkv_pre_fwd")
    scan_ops = [(q, 0), (lw, 0), (kp, 0), (q, 2), (na, 0), (nb, 0)]
    o, s0s = scan_fwd(scan_ops)
    w = {**w, **late_weights(o)}
    ya = mm(s_bf, w["w_proj_a"], "nn", tm=1024, tn=1024, name="mm_proj_a")
    post_t = [(o, d, 0), (q, d, 0), (kp, d, 0), (q, d, 2), (g, d, 0)]
    post_c = [w["ln_x_w"], w["ln_x_b"], w["r_k"]]
    (ob_bf,) = ew_call(f_post, post_t, post_c, [(d, BF16)], tm=512, name="rwkv_post_fwd")
    yb = mm(ob_bf, w["w_proj_b"], "nn", tm=1024, tn=1024, name="mm_proj_b")
    mix_t = [(ya, d, 0), (yb, d, 0), (p_all, d, 2), (p_all, d, 3)]
    (mixed_bf,) = ew_call(f_mix, mix_t, [], [(d, BF16)], tm=512, name="mix_fwd")
    h1 = mm(mixed_bf, w["w_out"], "nn", tm=1024, tn=1024, name="mm_out", epi=_add_epi, extras=(x,))
    (f_bf,) = ew_call(lambda h_, g_: (f_ffn_in(h_, g_)[0],), [(h1, d, 0)], [w["g_ffn"]], [(d, BF16)], tm=512,
                      name="ffn_norm")
    hid, act_bf = mm(f_bf, w["w_ffn1"], "nn", tm=2048, tn=1024, name="mm_ffn1", out_dtypes=(BF16, BF16), epi=_relu2_epi)
    m3 = mm(act_bf, w["w_ffn2"], "nn", tm=1024, tn=512, name="mm_ffn2")
    dh2, dh2_bf, dg_final, loss = final_call(h1, m3, tgt, w["g_final"], tm=512)

    dhid_bf = mm(dh2_bf, w["w_ffn2"], "nt", tm=2048, tn=1024, name="mm_dact", out_dtypes=(BF16,), epi=_relu2_bwd_epi,
                 extras=(hid,))
    late_g = lax.empty((N_CHIPS, 2, pack_rows(LATE), HALF_W), F32)
    late_g = mm(act_bf, dh2_bf, "tn", tm=1024, tn=HALF_W, name="mm_dw_ffn2",
                into=(late_g, lambda i, j: (i, j, PIECE_OFF["w_ffn2"] // 1024)))
    df = mm(dhid_bf, w["w_ffn1"], "nt", tm=1024, tn=512, name="mm_df")
    late_g = mm(f_bf, dhid_bf, "tn", tm=1024, tn=HALF_W, name="mm_dw_ffn1",
                into=(late_g, lambda i, j: (j // 2, j % 2, PIECE_OFF["w_ffn1"] // 1024)))
    (dh1, dh1_bf), (dg_ffn,) = ew_vjp_call(f_ffn_in, [(h1, d, 0)], [w["g_ffn"]], [(df, d, 0), (dh2, d, 0)],
                                           [(F32, BF16)], [True], tm=512, name="ffn_norm_bwd")
    dmixed = mm(dh1_bf, w["w_out"], "nt", tm=1024, tn=1024, name="mm_dmixed")
    late_g = mm(mixed_bf, dh1_bf, "tn", tm=256, tn=HALF_W, name="mm_dw_out",
                into=(late_g, lambda i, j: (i, j, PIECE_OFF["w_out"] // 256)))
    (dya_bf, dyb_bf, dga_bf, dgb_bf), _ = ew_vjp_call(f_mix, mix_t, [], [(dmixed, d, 0)], [(BF16,)] * 4, [], tm=256,
                                                      name="mix_bwd")
    dob = mm(dyb_bf, w["w_proj_b"], "nt", tm=1024, tn=1024, name="mm_dob")
    late_g = mm(ob_bf, dyb_bf, "tn", tm=256, tn=HALF_W, name="mm_dw_proj_b",
                into=(late_g, lambda i, j: (i, j, PIECE_OFF["w_proj_b"] // 256)))
    late_g = mm(s_bf, dya_bf, "tn", tm=256, tn=HALF_W, name="mm_dw_proj_a",
                into=(late_g, lambda i, j: (i, j, PIECE_OFF["w_proj_a"] // 256)))
    late_state, late_token = pair_start(late_g, "late")
    post_c_after = [w["ln_x_w"] + late_token[:1, :1]] + post_c[1:]
    (do, dr_p, dkp_p, dv_p, dg), (dlnx_w, dlnx_b, dr_k) = ew_vjp_call(
        f_post, post_t, post_c_after, [(dob, d, 0)], [(F32,)] * 5, [True] * 3, tm=256, name="rwkv_post_bwd")
    late_part, late_part16 = pair_finish(late_state, do, "late")
    *scan_g, late_slots = scan_bwd(scan_ops, s0s, do, late_part16)
    pre_g = [(z, d, 0) for z in scan_g] + [(dg, d, 0), (dr_p, d, 0), (dkp_p, d, 0), (dv_p, d, 0)]
    (dq,), (dwl, dw0, dal, da0, dgl, dk_k, dk_a) = ew_vjp_call(
        f_pre, pre_t, pre_c, pre_g, [(F32,)], [True] * 7, tm=256, name="rwkv_pre_bwd")
    dp_rwkv, dsb = shiftmix_bwd(dq, 0, p_all, sbp, tm=1024, name="shiftmix_bwd")
    ds = mm(dya_bf, w["w_proj_a"], "nt", tm=1024, tn=1024, name="mm_ds")
    (dp_sgu,), (dln_w, dln_b, dsw, dsbt) = ew_vjp_call(f_sgu, sgu_t, sgu_c, [(ds, d, 0)], [(BF16,)], [True] * 4,
                                                       tm=256, name="sgu_bwd")
    dp_all = jnp.concatenate([dp_sgu, dga_bf, dgb_bf, dp_rwkv], axis=1)
    d_in_pt = mm(dp_all, a_bf, "tn", tm=1280, tn=1024, name="mm_dw_in")
    early_state, early_token = pair_start(pack_early({
        "w_in": _unpad_win_rows(d_in_pt), "w_lora_w": dwl[:LORA_W], "a_lora_w": dal[:LORA_A],
        "g_lora_w": dgl[:LORA_G]}), "early")
    da = mm(dp_all, win_pt, "nn", tm=1024, tn=256, name="mm_da")
    g_mix_after = w["g_mix"] + early_token[:1, :1]
    (grad_x,), (dg_mix,) = ew_vjp_call(f_norm_in, [(x, d, 0)], [g_mix_after], [(da, d, 0), (dh1, d, 0)], [(F32,)],
                                       [True], tm=512, name="norm_in_bwd")

    grads = {
        "g_mix": dg_mix, "sgu_ln_w": dln_w, "sgu_ln_b": dln_b, "sgu_w": dsw, "sgu_b": dsbt.T,
        "shift_b": _unpad_rwkv_cols(dsb),
        "w0": dw0, "a0": da0, "k_k": dk_k, "k_a": dk_a, "r_k": dr_k, "ln_x_w": dlnx_w, "ln_x_b": dlnx_b,
        "g_ffn": dg_ffn, "g_final": dg_final,
    }
    return loss[0, 0], grad_x, grads, (late_part, late_slots), early_state


MESH = pl.DeviceIdType.MESH
N_CHIPS = 4
SMALL_ROWS = 160
_ANY = pl.BlockSpec(memory_space=pl.ANY)


def _coords():
    return lax.axis_index("x"), lax.axis_index("y"), lax.axis_index("c")


def _other_chips(x, y):
    return [(1 - x, y), (x, 1 - y), (1 - x, 1 - y)]


def _remote(src, dst, send_sems, recv_sems, k, to):
    return pltpu.make_async_remote_copy(src_ref=src, dst_ref=dst, send_sem=send_sems.at[k], recv_sem=recv_sems.at[k],
                                        device_id=to, device_id_type=MESH)


def gather_shards(pack):
    def body(src_ref, out_ref, token, send_sems, recv_sems):
        x, y, c = _coords()
        me = 2 * x + y
        sib = (x, y, 1 - c)
        chips = _other_chips(x, y)
        first = [_remote(src_ref.at[c], out_ref.at[me, c], send_sems, recv_sems, k, (cx, cy, c))
                 for k, (cx, cy) in enumerate(chips)]
        for cp in first:
            cp.start()
        passed = []
        for k, (cx, cy) in enumerate(chips):
            j = 2 * cx + cy
            _remote(src_ref.at[c], out_ref.at[j, c], send_sems, recv_sems, k, (cx, cy, c)).wait_recv()
            fwd = _remote(out_ref.at[j, c], out_ref.at[j, c], send_sems, recv_sems, 3 + k, sib)
            fwd.start()
            passed.append(fwd)
        for k, (cx, cy) in enumerate(chips):
            j = 2 * cx + cy
            _remote(out_ref.at[j, 1 - c], out_ref.at[j, 1 - c], send_sems, recv_sems, 3 + k, sib).wait_recv()
        for cp in first + passed:
            cp.wait_send()
        token[...] = jnp.zeros_like(token)

    return pl.pallas_call(
        body,
        name="gather_shards",
        in_specs=[_ANY],
        out_specs=[_ANY, pl.BlockSpec(memory_space=pltpu.VMEM)],
        out_shape=[jax.ShapeDtypeStruct((N_CHIPS,) + pack.shape, pack.dtype), jax.ShapeDtypeStruct((8, 128), F32)],
        scratch_shapes=[pltpu.SemaphoreType.DMA((6,)), pltpu.SemaphoreType.DMA((6,))],
    )(pack)


def _gather_copies(pack_ref, all_ref, send_sems, recv_sems):
    x, y, c = _coords()
    me = 2 * x + y
    return [(_remote(pack_ref.at[c], all_ref.at[me, c], send_sems, recv_sems, k, (cx, cy, c)),
             _remote(pack_ref.at[c], all_ref.at[2 * cx + cy, c], send_sems, recv_sems, k, (cx, cy, c)))
            for k, (cx, cy) in enumerate(_other_chips(x, y))]


_HBM = pl.BlockSpec(memory_space=pltpu.HBM)
_SEM = pl.BlockSpec(memory_space=pltpu.SEMAPHORE)
_SIDE_EFFECT = pltpu.SideEffectType.DATAFLOW_SIDE_EFFECTING


def split_start(name, copies, n, src, land_shape, after=None):
    def body(src_ref, land_ref, *refs):
        send_sems, recv_sems, token = refs[-5], refs[-4], refs[-1]
        for send, _ in copies(src_ref, land_ref, send_sems, recv_sems):
            send.start()
        token[...] = jnp.zeros_like(token)

    extra = () if after is None else (after,)
    *state, token = pl.pallas_call(
        body,
        name=name,
        out_shape=(pltpu.SemaphoreType.DMA((n,)), pltpu.SemaphoreType.DMA((n,)), pltpu.HBM(src.shape, src.dtype),
                   pltpu.HBM(land_shape, src.dtype), jax.ShapeDtypeStruct((8, 128), F32)),
        in_specs=(_HBM, _HBM) + (pl.BlockSpec(memory_space=pl.ANY),) * len(extra),
        out_specs=(_SEM, _SEM, _HBM, _HBM, pl.BlockSpec(memory_space=pltpu.VMEM)),
        input_output_aliases={0: 2, 1: 3},
        compiler_params=pltpu.CompilerParams(has_side_effects=_SIDE_EFFECT),
    )(pltpu.with_memory_space_constraint(src, pltpu.HBM),
      pltpu.with_memory_space_constraint(lax.empty(land_shape, src.dtype), pltpu.HBM), *extra)
    return state, token


def split_wait(name, copies, state, after):
    send_sems, recv_sems, src, land = state

    def body(src_ref, land_ref, send_sems, recv_sems, after_ref, src_out, land_out):
        for send, arrival in copies(src_ref, land_ref, send_sems, recv_sems):
            send.wait_send()
            arrival.wait_recv()

    return pl.pallas_call(
        body,
        name=name,
        out_shape=(pltpu.HBM(src.shape, src.dtype), pltpu.HBM(land.shape, land.dtype)),
        in_specs=(_HBM, _HBM, _SEM, _SEM, pl.BlockSpec(memory_space=pl.ANY)),
        out_specs=(_HBM, _HBM),
        input_output_aliases={0: 0, 1: 1},
        compiler_params=pltpu.CompilerParams(has_side_effects=_SIDE_EFFECT),
    )(src, land, send_sems, recv_sems, after)


def gather_forward(got):
    def body(got_ref, out_ref, send_sems, recv_sems):
        x, y, c = _coords()
        sib = (x, y, 1 - c)
        slots = [2 * cx + cy for cx, cy in _other_chips(x, y)]
        sends = [_remote(got_ref.at[j, c], out_ref.at[j, c], send_sems, recv_sems, k, sib) for k, j in enumerate(slots)]
        for cp in sends:
            cp.start()
        for k, j in enumerate(slots):
            _remote(got_ref.at[j, 1 - c], out_ref.at[j, 1 - c], send_sems, recv_sems, k, sib).wait_recv()
        for cp in sends:
            cp.wait_send()

    return pl.pallas_call(
        body,
        name="gather_forward",
        in_specs=[_ANY],
        out_specs=_ANY,
        out_shape=jax.ShapeDtypeStruct(got.shape, got.dtype),
        input_output_aliases={0: 0},
        scratch_shapes=[pltpu.SemaphoreType.DMA((3,)), pltpu.SemaphoreType.DMA((3,))],
    )(got)


def pair_sum(g, got, tag, *, tm):
    n, _, rows, width = g.shape

    def body(c_ref, own_ref, got_ref, out_ref, out16_ref):
        total = own_ref[0, 0] + got_ref[0]
        out_ref[0] = total
        out16_ref[0] = total.astype(BF16)

    blk = pl.BlockSpec((1, tm, width), lambda j, i, c_ref: (j, i, 0))
    return pl.pallas_call(
        body,
        name="pair_sum_" + tag,
        grid_spec=pltpu.PrefetchScalarGridSpec(
            num_scalar_prefetch=1,
            grid=(n, rows // tm),
            in_specs=[pl.BlockSpec((1, 1, tm, width), lambda j, i, c_ref: (j, c_ref[0], i, 0)), blk],
            out_specs=[blk, blk],
        ),
        out_shape=[jax.ShapeDtypeStruct(got.shape, F32), jax.ShapeDtypeStruct(got.shape, BF16)],
        compiler_params=_cparams(2),
    )(lax.axis_index("c").reshape(1).astype(jnp.int32), g, got)


def _pair_copies(g_ref, got_ref, send_sems, recv_sems):
    x, y, c = _coords()
    copies = [_remote(g_ref.at[j, 1 - c], got_ref.at[j], send_sems, recv_sems, j, (x, y, 1 - c))
              for j in range(N_CHIPS)]
    return [(cp, cp) for cp in copies]


def _chip_copies(p_ref, slots_ref, send_sems, recv_sems):
    x, y, c = _coords()
    me = 2 * x + y
    return [(_remote(p_ref.at[2 * cx + cy], slots_ref.at[me], send_sems, recv_sems, k, (cx, cy, c)),
             _remote(p_ref.at[me], slots_ref.at[2 * cx + cy], send_sems, recv_sems, k, (cx, cy, c)))
            for k, (cx, cy) in enumerate(_other_chips(x, y))]


def sum_with_own(own, slots, mine, after, *, tm, name):
    n, rows, width = slots.shape

    def body(mine_ref, own_ref, *refs):
        acc = None
        for s in range(n):
            term = jnp.where(mine_ref[0] == s, own_ref[0], refs[s][0].astype(F32))
            acc = term if acc is None else acc + term
        refs[-1][...] = acc

    return pl.pallas_call(
        body,
        name=name,
        grid_spec=pltpu.PrefetchScalarGridSpec(
            num_scalar_prefetch=1,
            grid=(rows // tm,),
            in_specs=[pl.BlockSpec((1, tm, width), lambda i, mine_ref: (mine_ref[0], i, 0))]
            + [pl.BlockSpec((1, tm, width), lambda i, mine_ref, s=s: (s, i, 0)) for s in range(n)]
            + [pl.BlockSpec(after.shape, lambda i, mine_ref: (0,) * after.ndim)],
            out_specs=pl.BlockSpec((tm, width), lambda i, mine_ref: (i, 0)),
        ),
        out_shape=jax.ShapeDtypeStruct((rows, width), F32),
        compiler_params=_cparams(1),
    )(mine.reshape(1).astype(jnp.int32), own, *([slots] * n), after)


def exchange_halves(s, tag):
    nq = 4
    rq = s.shape[0] // nq
    assert rq * nq == s.shape[0] and rq % 8 == 0

    def body(s_ref, out_ref, sbuf, rbuf, send_sems, recv_sems, in_sems, out_sems):
        x, y, c = _coords()
        sib = (x, y, 1 - c)
        rows = lambda q: pl.ds(q * rq, rq)
        loads = [pltpu.make_async_copy(s_ref.at[rows(q)], sbuf.at[rows(q)], in_sems.at[q]) for q in range(nq)]
        for cp in loads:
            cp.start()
        sends = []
        for q in range(nq):
            loads[q].wait()
            sends.append(_remote(sbuf.at[rows(q)], rbuf.at[rows(q)], send_sems, recv_sems, q, sib))
            sends[q].start()
        stores = []
        for q in range(nq):
            sends[q].wait_recv()
            stores.append(pltpu.make_async_copy(rbuf.at[rows(q)], out_ref.at[rows(q)], out_sems.at[q]))
            stores[q].start()
        for cp in sends:
            cp.wait_send()
        for cp in stores:
            cp.wait()

    return pl.pallas_call(
        body,
        name="exchange_halves_" + tag,
        in_specs=[_ANY],
        out_specs=_ANY,
        out_shape=jax.ShapeDtypeStruct(s.shape, s.dtype),
        scratch_shapes=[pltpu.VMEM(s.shape, s.dtype), pltpu.VMEM(s.shape, s.dtype)]
        + [pltpu.SemaphoreType.DMA((nq,))] * 4,
        compiler_params=pltpu.CompilerParams(vmem_limit_bytes=VMEM_LIMIT),
    )(s)


def sum_all(s, after):
    rows = s.shape[0]
    half = rows // 2

    def body(s_ref, after_ref, out_ref, theirs, pair, slots, send_sems, recv_sems):
        x, y, c = _coords()
        me = 2 * x + y
        sib = (x, y, 1 - c)
        chips = _other_chips(x, y)
        swap = _remote(s_ref, theirs, send_sems, recv_sems, 0, sib)
        swap.start()
        swap.wait_recv()
        pair[...] = s_ref[...] + theirs[...]
        mine = pl.ds(pl.multiple_of(c * half, 8), half)
        other = pl.ds(pl.multiple_of((1 - c) * half, 8), half)
        sends = [_remote(pair.at[mine], slots.at[me], send_sems, recv_sems, 1 + k, (cx, cy, c))
                 for k, (cx, cy) in enumerate(chips)]
        for cp in sends:
            cp.start()
        for k, (cx, cy) in enumerate(chips):
            _remote(pair.at[mine], slots.at[2 * cx + cy], send_sems, recv_sems, 1 + k, (cx, cy, c)).wait_recv()
        slots[me] = pair[mine]
        out_ref[mine] = ((slots[0] + slots[1]) + slots[2]) + slots[3]
        last = _remote(out_ref.at[mine], out_ref.at[mine], send_sems, recv_sems, 4, sib)
        last.start()
        _remote(out_ref.at[other], out_ref.at[other], send_sems, recv_sems, 4, sib).wait_recv()
        for cp in [swap] + sends + [last]:
            cp.wait_send()

    vmem = pl.BlockSpec(memory_space=pltpu.VMEM)
    return pl.pallas_call(
        body,
        name="sum_all",
        in_specs=[vmem, vmem],
        out_specs=vmem,
        out_shape=jax.ShapeDtypeStruct(s.shape, s.dtype),
        scratch_shapes=[pltpu.VMEM(s.shape, s.dtype), pltpu.VMEM(s.shape, s.dtype),
                        pltpu.VMEM((N_CHIPS, half, s.shape[1]), s.dtype), pltpu.SemaphoreType.DMA((5,)),
                        pltpu.SemaphoreType.DMA((5,))],
        compiler_params=pltpu.CompilerParams(vmem_limit_bytes=VMEM_LIMIT),
    )(s, after)


ADAM_LR = 0.001
ADAM_B1 = 0.9
ADAM_B2 = 0.999
ADAM_EPS = 1e-08
ADAM_WD = 0.01
ADAM_STEP = 10


def f_adamw(g, w, m, v):
    m = ADAM_B1 * m + (1.0 - ADAM_B1) * g
    v = ADAM_B2 * v + (1.0 - ADAM_B2) * jnp.square(g)
    m_hat = m / (1.0 - ADAM_B1 ** ADAM_STEP)
    v_hat = v / (1.0 - ADAM_B2 ** ADAM_STEP)
    delta = -ADAM_LR * (m_hat / (jnp.sqrt(v_hat) + ADAM_EPS) + ADAM_WD * w)
    return delta, m, v


def adamw_many(gs, ws, ms, vs):
    n = len(gs)

    def body(*refs):
        ins, outs = refs[:4 * n], refs[4 * n:]
        for i in range(n):
            delta, nm, nv = f_adamw(ins[i][...], ins[n + i][...], ins[2 * n + i][...], ins[3 * n + i][...])
            outs[i][...] = delta
            outs[n + i][...] = nm
            outs[2 * n + i][...] = nv

    vmem = pl.BlockSpec(memory_space=pltpu.VMEM)
    res = pl.pallas_call(
        body,
        name="adamw_small",
        in_specs=[vmem] * (4 * n),
        out_specs=[vmem] * (3 * n),
        out_shape=[jax.ShapeDtypeStruct(w.shape, F32) for w in ws] * 3,
    )(*gs, *ws, *ms, *vs)
    return res[:n], res[n:2 * n], res[2 * n:]


EARLY = ["w_in", "w_lora_w", "a_lora_w", "g_lora_w"]
LATE = ["w_ffn1", "w_ffn2", "w_proj_b", "w_out", "w_proj_a"]
LORAS = ["w_lora_w", "a_lora_w", "g_lora_w"]
HALF_W = 512
PIECE_ROWS = {"w_in": 1864, "w_ffn1": 1024, "w_ffn2": 1024, "w_proj_a": 256, "w_proj_b": 256, "w_out": 256,
              "w_lora_w": 32, "a_lora_w": 32, "g_lora_w": 80}
PIECE_OFF = {"w_in": 0, "w_lora_w": 1920, "a_lora_w": 1952, "g_lora_w": 2000,
             "w_ffn1": 0, "w_ffn2": 1024, "w_proj_b": 2048, "w_out": 2304, "w_proj_a": 2560}
LO_OFF = 2080


def pack_rows(group):
    return 2304 if group is EARLY else 2816
SHARD_AXIS = {"w_in": 1, "w_proj_a": 0, "w_lora_w": 1, "a_lora_w": 1, "g_lora_w": 1, "w_proj_b": 0, "w_out": 0,
              "w_ffn1": 1, "w_ffn2": 0}
SHARD_SHAPE = {"w_in": (1024, 1864), "w_proj_a": (256, 1024), "w_lora_w": (64, 256), "a_lora_w": (64, 256),
               "g_lora_w": (160, 256), "w_proj_b": (256, 1024), "w_out": (256, 1024), "w_ffn1": (1024, 1024),
               "w_ffn2": (1024, 1024)}
SHIFT_SHARD = (2, 840)
VECTORS = ["g_mix", "sgu_ln_w", "sgu_ln_b", "w0", "a0", "k_k", "k_a", "r_k", "ln_x_w", "ln_x_b", "g_ffn", "g_final"]
SMALL = VECTORS + ["sgu_w", "sgu_b"]
SMALL_SHAPE = {**{n: (1, 1024) for n in VECTORS}, "sgu_w": (8, 128, 128), "sgu_b": (8, 128)}
WEIGHTS = ["g_mix", "w_in", "sgu_ln_w", "sgu_ln_b", "sgu_w", "sgu_b", "w_proj_a", "shift_b", "w_lora_w", "w0",
           "a_lora_w", "a0", "g_lora_w", "k_k", "k_a", "r_k", "ln_x_w", "ln_x_b", "w_proj_b", "w_out", "g_ffn",
           "w_ffn1", "w_ffn2", "g_final"]


def _size(shape):
    n = 1
    for s in shape:
        n *= s
    return n


def _pack_rows(parts, rows, dtype):
    flat = jnp.concatenate([p.reshape(-1).astype(dtype) for p in parts])
    return jnp.concatenate([flat, jnp.zeros((rows * 1024 - flat.shape[0],), dtype)]).reshape(rows, 1024)


def _unpack_rows(packed, shapes):
    flat = packed.reshape(-1)
    out, off = [], 0
    for shp in shapes:
        out.append(flat[off:off + _size(shp)].reshape(shp))
        off += _size(shp)
    return out


def _shard_of(name, full, j):
    ax = SHARD_AXIS[name]
    n = SHARD_SHAPE[name][ax]
    return lax.slice_in_dim(full, j * n, (j + 1) * n, axis=ax)


def _pad_cols(z, n):
    return jnp.concatenate([z, jnp.zeros((z.shape[0], n - z.shape[1]), z.dtype)], axis=1)


def _row_form(name, s):
    return s.T if name == "w_in" else s


def _half_piece(name, rf, h):
    if name in LORAS:
        r = PIECE_ROWS[name]
        return _pad_cols(rf[h * r:(h + 1) * r], HALF_W)
    return rf[:, HALF_W * h:HALF_W * (h + 1)]


def _pack_half(group, rf_fn, h, dtype, tail=()):
    parts, pos, rows = [], 0, pack_rows(group)
    for n in group:
        if PIECE_OFF[n] > pos:
            parts.append(jnp.zeros((PIECE_OFF[n] - pos, HALF_W), dtype))
        parts.append(_half_piece(n, rf_fn(n), h).astype(dtype))
        pos = PIECE_OFF[n] + PIECE_ROWS[n]
    for t in tail:
        parts.append(t)
        pos += t.shape[0]
    parts.append(jnp.zeros((rows - pos, HALF_W), dtype))
    return jnp.concatenate(parts, axis=0)


def _piece(pack, name):
    return pack[PIECE_OFF[name]:PIECE_OFF[name] + PIECE_ROWS[name]]


def _join_halves(name, p0, p1):
    if name in LORAS:
        return jnp.concatenate([p0[:, :SHARD_SHAPE[name][1]], p1[:, :SHARD_SHAPE[name][1]]], axis=0)
    return jnp.concatenate([p0, p1], axis=1)


def _grad_row_form(name, full, j):
    if name == "w_in":
        return full[SHARD_SHAPE[name][1] * j:SHARD_SHAPE[name][1] * (j + 1)]
    return _shard_of(name, full, j)


def adamw_weight(name, g_own, g_other, w, m, v):
    rows, width = w.shape
    if name in LORAS:
        tm = PIECE_ROWS[name]
        grid = (2, 1)
        native = pl.BlockSpec((tm, width), lambda h, i: (h, 0))
    elif name == "w_in":
        tm, lanes = rows, 256
        grid = (2, HALF_W // lanes)
        native = pl.BlockSpec((tm, lanes), lambda h, i: (0, h * (HALF_W // lanes) + i))
    else:
        tm = rows
        grid = (2, 1)
        native = pl.BlockSpec((tm, HALF_W), lambda h, i: (i, h))
    assert PIECE_OFF[name] % tm == 0
    off = PIECE_OFF[name] // tm
    if name == "w_in":
        packed = pl.BlockSpec((tm, lanes), lambda h, i: (0, i))
    else:
        packed = pl.BlockSpec((tm, HALF_W), lambda h, i: (off + i, 0))

    def body(go_ref, gx_ref, w_ref, m_ref, v_ref, g_ref, d_ref, nm_ref, nv_ref):
        g = jnp.where(pl.program_id(0) == lax.axis_index("c"), go_ref[...], gx_ref[...])[:, :w_ref.shape[1]]
        delta, nm, nv = f_adamw(g, w_ref[...], m_ref[...], v_ref[...])
        g_ref[...] = g
        d_ref[...] = delta
        nm_ref[...] = nm
        nv_ref[...] = nv

    return pl.pallas_call(
        body,
        name="adamw_" + name,
        grid=grid,
        in_specs=[packed, packed, native, native, native],
        out_specs=[native] * 4,
        out_shape=[jax.ShapeDtypeStruct(w.shape, F32)] * 4,
        compiler_params=_cparams(2),
    )(g_own, g_other, w, m, v)


def kernel(x, g_mix, w_in, sgu_ln_w, sgu_ln_b, sgu_w, sgu_b, w_proj_a, shift_b, w_lora_w, w0, a_lora_w, a0, g_lora_w, k_k, k_a, r_k, ln_x_w, ln_x_b, w_proj_b, w_out, g_ffn, w_ffn1, w_ffn2, g_final, loss_target, m_g_mix, m_w_in, m_sgu_ln_w, m_sgu_ln_b, m_sgu_w, m_sgu_b, m_w_proj_a, m_shift_b, m_w_lora_w, m_w0, m_a_lora_w, m_a0, m_g_lora_w, m_k_k, m_k_a, m_r_k, m_ln_x_w, m_ln_x_b, m_w_proj_b, m_w_out, m_g_ffn, m_w_ffn1, m_w_ffn2, m_g_final, v_g_mix, v_w_in, v_sgu_ln_w, v_sgu_ln_b, v_sgu_w, v_sgu_b, v_w_proj_a, v_shift_b, v_w_lora_w, v_w0, v_a_lora_w, v_a0, v_g_lora_w, v_k_k, v_k_a, v_r_k, v_ln_x_w, v_ln_x_b, v_w_proj_b, v_w_out, v_g_ffn, v_w_ffn1, v_w_ffn2, v_g_final):
    given = dict(zip(WEIGHTS, (g_mix, w_in, sgu_ln_w, sgu_ln_b, sgu_w, sgu_b, w_proj_a, shift_b, w_lora_w, w0, a_lora_w, a0, g_lora_w, k_k, k_a, r_k, ln_x_w, ln_x_b, w_proj_b, w_out, g_ffn, w_ffn1, w_ffn2, g_final)))
    mom_m = dict(zip(WEIGHTS, (m_g_mix, m_w_in, m_sgu_ln_w, m_sgu_ln_b, m_sgu_w, m_sgu_b, m_w_proj_a, m_shift_b, m_w_lora_w, m_w0, m_a_lora_w, m_a0, m_g_lora_w, m_k_k, m_k_a, m_r_k, m_ln_x_w, m_ln_x_b, m_w_proj_b, m_w_out, m_g_ffn, m_w_ffn1, m_w_ffn2, m_g_final)))
    mom_v = dict(zip(WEIGHTS, (v_g_mix, v_w_in, v_sgu_ln_w, v_sgu_ln_b, v_sgu_w, v_sgu_b, v_w_proj_a, v_shift_b, v_w_lora_w, v_w0, v_a_lora_w, v_a0, v_g_lora_w, v_k_k, v_k_a, v_r_k, v_ln_x_w, v_ln_x_b, v_w_proj_b, v_w_out, v_g_ffn, v_w_ffn1, v_w_ffn2, v_g_final)))
    chip = 2 * lax.axis_index("x") + lax.axis_index("y")

    def local_block(tree, n):
        return tree[n] if n == "g_final" else tree[n][0]

    sb = local_block(given, "shift_b")
    lo_part = lambda z: (z - z.astype(BF16).astype(F32)).astype(BF16)
    row_form = lambda tree: (lambda n: _row_form(n, local_block(tree, n)))
    tile16 = lambda z: jnp.pad(z, ((0, 16 - z.shape[0]), (0, HALF_W - z.shape[1])))
    sb_tiles = [tile16(f(sb[:, lanes])) for f in (lambda z: z.astype(BF16), lo_part)
                for lanes in (slice(0, HALF_W), slice(HALF_W, None))]
    tails = [[_half_piece(n, lo_part(local_block(given, n)), h) for n in LORAS] + sb_tiles for h in range(2)]
    pack_w = jnp.stack([_pack_half(EARLY, row_form(given), h, BF16, tails[h]) for h in range(2)])
    gathered, gathered_token = gather_shards(pack_w)
    gathered = lax.dynamic_update_index_in_dim(gathered, pack_w, chip, 0)
    pack_late = jnp.stack([_pack_half(LATE, row_form(given), h, BF16) for h in range(2)])
    late_state, late_token = split_start("gather_start", _gather_copies, 3, pack_late, (N_CHIPS,) + pack_late.shape,
                                         gathered_token)

    def whole(group, got, own):
        half = lambda n, j, h: jnp.where(chip == j, _piece(own[h], n), _piece(got[j, h], n))
        shard = lambda n, j: _join_halves(n, half(n, j, 0), half(n, j, 1))
        return {n: jnp.concatenate([shard(n, j) for j in range(N_CHIPS)],
                                   axis=0 if n == "w_in" else SHARD_AXIS[n]) for n in group}

    w = whole(EARLY, gathered, pack_w)
    late_weights = lambda after: whole(
        LATE, gather_forward(split_wait("gather_wait", _gather_copies, late_state, after)[1]), pack_late)
    off = LO_OFF
    for n in LORAS:
        r, cols = PIECE_ROWS[n], SHARD_SHAPE[n][1]
        lo = jnp.concatenate([jnp.concatenate([gathered[j, 0, off:off + r, :cols], gathered[j, 1, off:off + r, :cols]],
                                              axis=0) for j in range(N_CHIPS)], axis=1)
        w[n] = w[n].astype(F32) + lo.astype(F32)
        off += r
    sb_tile = lambda j, t, lanes: gathered[j, 0, off + 16 * t:off + 16 * t + 2, :lanes].astype(F32)
    rest = SHIFT_SHARD[1] - HALF_W
    w["shift_b"] = jnp.concatenate(
        [jnp.concatenate([sb_tile(j, 0, HALF_W) + sb_tile(j, 2, HALF_W), sb_tile(j, 1, rest) + sb_tile(j, 3, rest)],
                         axis=1) for j in range(N_CHIPS)], axis=1)
    for n in SMALL:
        w[n] = local_block(given, n).reshape(SMALL_SHAPE[n])

    def pair_start(g_pack, tag):
        return split_start("reduce_pair_start_" + tag, _pair_copies, N_CHIPS, g_pack, (N_CHIPS,) + g_pack.shape[2:])

    def pair_finish(state, after, tag):
        g_pack, got = split_wait("reduce_pair_wait_" + tag, _pair_copies, state, after)
        return pair_sum(g_pack, got, tag, tm=got.shape[1] // 2)

    pack_early = lambda g: jnp.stack([jnp.stack([_pack_half(EARLY, lambda n: _grad_row_form(n, g[n], j), h, F32)
                                                 for h in range(2)]) for j in range(N_CHIPS)])
    loss, grad_x, grads, (late_part, late_slots), early_state = local_step(
        x[0], loss_target[0], w, late_token, late_weights, pair_start, pair_finish, pack_early)

    early_part, early_part16 = pair_finish(early_state, grad_x, "early")
    s_pack = _pack_rows([grads[n] for n in SMALL] + [grads["shift_b"], loss.reshape(1, 1)], SMALL_ROWS, F32)
    chips_state, token = split_start("reduce_chips_start", _chip_copies, 3, early_part16, early_part16.shape)
    out_g, out_d, out_m, out_v = {}, {}, {}, {}

    def finish(group, tag, part, slots):
        half_sum = sum_with_own(part, slots, chip, token, tm=part.shape[1] // 2, name="chip_sum_" + tag)
        other_half = exchange_halves(half_sum, tag)
        for n in group:
            res = adamw_weight(n, half_sum, other_half,
                               *[_row_form(n, local_block(t, n)) for t in (given, mom_m, mom_v)])
            for tree, z in zip((out_g, out_d, out_m, out_v), res):
                tree[n] = _row_form(n, z)

    finish(LATE, "late", late_part, late_slots)

    small_shapes = [SMALL_SHAPE[n] for n in SMALL]
    g_small = sum_all(s_pack, token)
    *g_parts, loss = _unpack_rows(g_small, small_shapes + [(2, N_RWKV), ()])
    out_g.update(zip(SMALL, g_parts[:-1]))
    g_sb = lax.dynamic_slice_in_dim(g_parts[-1], chip * SHIFT_SHARD[1], SHIFT_SHARD[1], axis=1)
    out_g["shift_b"] = g_sb
    names = SMALL + ["shift_b"]
    native = lambda tree: [local_block(tree, n).reshape(SMALL_SHAPE.get(n, SHIFT_SHARD)) for n in names]
    small_res = adamw_many(g_parts[:-1] + [g_sb], native(given), native(mom_m), native(mom_v))
    for tree, res in zip((out_d, out_m, out_v), small_res):
        tree.update(zip(names, res))

    after = (out_v["w_out"], out_v["sgu_w"])
    early_slots = split_wait("reduce_chips_wait", _chip_copies, chips_state,
                             jnp.concatenate([z.reshape(-1)[:8] for z in after]))[1]
    finish(EARLY, "early", early_part, early_slots)

    def block_of(tree, n):
        return tree[n].reshape(given[n].shape)

    return (loss, grad_x[None], *[block_of(out_g, n) for n in WEIGHTS], *[block_of(out_d, n) for n in WEIGHTS],
            *[block_of(out_m, n) for n in WEIGHTS], *[block_of(out_v, n) for n in WEIGHTS])
```

```python
import functools

import jax
import jax.numpy as jnp
from jax import lax
from jax.experimental import pallas as pl
from jax.experimental.pallas import tpu as pltpu

F32 = jnp.float32
BF16 = jnp.bfloat16

D_MODEL = 1024
N_HEADS = 16
HEAD = 64
SCAN_CHUNK = 64

VMEM_LIMIT = 56 * 1024 * 1024


_BDIMS = {
    "nn": (((2,), (1,)), ((0,), (0,))),
    "nt": (((2,), (2,)), ((0,), (0,))),
    "tn": (((1,), (1,)), ((0,), (0,))),
}


def _raw_bdot(x, y, mode, fine):
    if fine:
        return lax.dot_general(x, y, _BDIMS[mode], precision=lax.Precision.HIGH, preferred_element_type=F32)
    return lax.dot_general(x.astype(BF16), y.astype(BF16), _BDIMS[mode], preferred_element_type=F32)


@functools.partial(jax.custom_vjp, nondiff_argnums=(2, 3))
def bdot(x, y, mode, fine=True):
    return _raw_bdot(x, y, mode, fine)


def _bdot_fwd(x, y, mode, fine):
    return _raw_bdot(x, y, mode, fine), (x, y)


def _bdot_bwd(mode, fine, res, g):
    x, y = res
    if mode == "nn":
        return bdot(g, y, "nt", fine), bdot(x, g, "tn", fine)
    if mode == "nt":
        return bdot(g, y, "nn", fine), bdot(g, x, "tn", fine)
    return bdot(y, g, "nt", fine), bdot(x, g, "nn", fine)


bdot.defvjp(_bdot_fwd, _bdot_bwd)


def _scan_chunk(S0, r, lw, k, v, a, b):
    nh, lc, _ = r.shape
    ti = lax.broadcasted_iota(jnp.int32, (lc, lc), 0)
    si = lax.broadcasted_iota(jnp.int32, (lc, lc), 1)
    incl = (si <= ti).astype(F32)
    strict = (si < ti).astype(F32)
    eye = (si == ti).astype(F32)
    cl = bdot(jnp.broadcast_to(incl, (nh, lc, lc)), lw, "nn")
    cl_last = cl[:, lc - 1:lc, :]
    g_last = jnp.exp(cl_last - cl)
    at = a * jnp.exp(cl - lw)
    bt = b * jnp.exp(-cl)
    kt = k * jnp.exp(-cl)
    rt = r * jnp.exp(cl)
    ar = jnp.concatenate([at, rt], axis=1)
    ar_b = bdot(ar, bt, "nt", False)
    ar_k = bdot(ar, kt, "nt", False)
    m_ab, m_rb = ar_b[:, :lc] * strict, ar_b[:, lc:] * incl
    m_ak, m_rk = ar_k[:, :lc] * strict, ar_k[:, lc:] * incl
    x = eye + m_ab
    p = bdot(m_ab, m_ab, "nn", False)
    n = 2
    while n * 2 < lc:
        px = bdot(jnp.concatenate([p, x], axis=1), p, "nn", False)
        p = px[:, :lc]
        x = x + px[:, lc:]
        n *= 2
    x = x + bdot(x, p, "nn", False)
    ar_s = bdot(ar, S0, "nt", False)
    akrk_v = bdot(jnp.concatenate([m_ak, m_rk], axis=1), v, "nn", False)
    u = bdot(x, ar_s[:, :lc] + akrk_v[:, :lc], "nn", False)
    o = ar_s[:, lc:] + bdot(m_rb, u, "nn", False) + akrk_v[:, lc:]
    s_last = S0 * jnp.exp(cl_last) + bdot(jnp.concatenate([u, v], axis=1),
                                          jnp.concatenate([b * g_last, k * g_last], axis=1), "tn", False)
    return o, s_last


def _split_heads(z):
    return jnp.stack([z[:, HEAD * h:HEAD * (h + 1)] for h in range(N_HEADS)], axis=0)


def _merge_heads(z):
    return jnp.concatenate([z[h] for h in range(N_HEADS)], axis=1)


def _scan_specs(t, ops, rev):
    nc = t // SCAN_CHUNK
    row = (lambda c: nc - 1 - c) if rev else (lambda c: c)
    specs = [pl.BlockSpec((SCAN_CHUNK, D_MODEL), lambda c, cb=cb: (row(c), cb)) for _, cb in ops]
    state = pl.BlockSpec((1, N_HEADS, HEAD, HEAD), lambda c: (row(c), 0, 0, 0))
    return nc, specs, state


def scan_fwd(ops):
    t = ops[0][0].shape[0]
    nc, specs, state = _scan_specs(t, ops, False)

    def body(r_ref, lw_ref, k_ref, v_ref, a_ref, b_ref, o_ref, s0_ref, s_scr):
        @pl.when(pl.program_id(0) == 0)
        def _():
            s_scr[...] = jnp.zeros_like(s_scr)

        s0 = s_scr[...]
        s0_ref[0] = s0
        o, s_last = _scan_chunk(s0, *[_split_heads(z[...]) for z in (r_ref, lw_ref, k_ref, v_ref, a_ref, b_ref)])
        o_ref[...] = _merge_heads(o)
        s_scr[...] = s_last

    return pl.pallas_call(
        body,
        name="scan_fwd",
        grid=(nc,),
        in_specs=specs,
        out_specs=[pl.BlockSpec((SCAN_CHUNK, D_MODEL), lambda c: (c, 0)), state],
        out_shape=[jax.ShapeDtypeStruct((t, D_MODEL), F32), jax.ShapeDtypeStruct((nc, N_HEADS, HEAD, HEAD), F32)],
        scratch_shapes=[pltpu.VMEM((N_HEADS, HEAD, HEAD), F32)],
        compiler_params=_cparams(1),
    )(*[a for a, _ in ops])


def scan_bwd(ops, s0s, do, part):
    t = ops[0][0].shape[0]
    nc, specs, state = _scan_specs(t, ops + [(do, 0)], True)

    def body(r_ref, lw_ref, k_ref, v_ref, a_ref, b_ref, do_ref, s0_ref, part_ref, *rest):
        out_refs, slots_ref, ds_scr, send_sems, recv_sems = rest[:6], rest[6], rest[7], rest[8], rest[9]
        step = pl.program_id(0)
        x, y, c = _coords()
        me = 2 * x + y
        chips = _other_chips(x, y)
        sends = [_remote(part_ref.at[2 * cx + cy], slots_ref.at[me], send_sems, recv_sems, k, (cx, cy, c))
                 for k, (cx, cy) in enumerate(chips)]

        @pl.when(step == 0)
        def _():
            ds_scr[...] = jnp.zeros_like(ds_scr)
            for cp in sends:
                cp.start()

        _, vjp = jax.vjp(_scan_chunk, s0_ref[0],
                         *[_split_heads(z[...]) for z in (r_ref, lw_ref, k_ref, v_ref, a_ref, b_ref)])
        grads = vjp((_split_heads(do_ref[...]), ds_scr[...]))
        for o_ref, g in zip(out_refs, grads[1:]):
            o_ref[...] = _merge_heads(g)
        ds_scr[...] = grads[0]

        @pl.when(step == nc - 1)
        def _():
            for k, (cx, cy) in enumerate(chips):
                _remote(part_ref.at[me], slots_ref.at[2 * cx + cy], send_sems, recv_sems, k, (cx, cy, c)).wait_recv()
            for cp in sends:
                cp.wait_send()

    return pl.pallas_call(
        body,
        name="scan_bwd",
        grid=(nc,),
        in_specs=specs + [state, _ANY],
        out_specs=[pl.BlockSpec((SCAN_CHUNK, D_MODEL), lambda c: (nc - 1 - c, 0))] * 6 + [_ANY],
        out_shape=[jax.ShapeDtypeStruct((t, D_MODEL), F32)] * 6 + [jax.ShapeDtypeStruct(part.shape, part.dtype)],
        scratch_shapes=[pltpu.VMEM((N_HEADS, HEAD, HEAD), F32), pltpu.SemaphoreType.DMA((3,)),
                        pltpu.SemaphoreType.DMA((3,))],
        compiler_params=_cparams(1),
    )(*[a for a, _ in ops], do, s0s, part)


_MDIMS = {
    "nn": (((1,), (0,)), ((), ())),
    "nt": (((1,), (1,)), ((), ())),
    "tn": (((0,), (0,)), ((), ())),
}


def _raw_mdot(x, y, mode, exact):
    if exact:
        return lax.dot_general(x, y, _MDIMS[mode], precision=lax.Precision.HIGH, preferred_element_type=F32)
    return lax.dot_general(x.astype(BF16), y.astype(BF16), _MDIMS[mode], preferred_element_type=F32)


@functools.partial(jax.custom_vjp, nondiff_argnums=(2, 3))
def mdot(x, y, mode, exact):
    return _raw_mdot(x, y, mode, exact)


def _mdot_fwd(x, y, mode, exact):
    return _raw_mdot(x, y, mode, exact), (x, y)


def _mdot_bwd(mode, exact, res, g):
    x, y = res
    if mode == "nn":
        return mdot(g, y, "nt", exact), mdot(x, g, "tn", exact)
    if mode == "nt":
        return mdot(g, y, "nn", exact), mdot(g, x, "tn", exact)
    return mdot(y, g, "nt", exact), mdot(x, g, "nn", exact)


mdot.defvjp(_mdot_fwd, _mdot_bwd)


def _seg_ones():
    i = lax.broadcasted_iota(jnp.int32, (256, 256), 0) // HEAD
    j = lax.broadcasted_iota(jnp.int32, (256, 256), 1) // HEAD
    return (i == j).astype(BF16)


@jax.custom_vjp
def segsum(x):
    bd = _seg_ones()
    hi = x.astype(BF16)
    lo = (x - hi.astype(F32)).astype(BF16)
    cols = []
    for j in range(x.shape[1] // 256):
        sl = slice(256 * j, 256 * (j + 1))
        cols.append(jnp.dot(hi[:, sl], bd, preferred_element_type=F32)
                    + jnp.dot(lo[:, sl], bd, preferred_element_type=F32))
    return jnp.concatenate(cols, axis=1)


segsum.defvjp(lambda x: (segsum(x), None), lambda _, g: (segsum(g),))


NORM_EPS = 1e-6
LN_EPS = 1e-5
GN_EPS = 64e-5
SGU_CHUNK = 128
SGU_GROUPS = 8


def _rms(x, g):
    return x * lax.rsqrt(jnp.mean(x * x, axis=-1, keepdims=True) + NORM_EPS) * g


def f_norm_in(x, g):
    return _rms(x, g), x


def f_sgu(p, ln_w, ln_b, sw, sbt):
    tm = p.shape[0]
    z = 0.5 * p * (1.0 + lax.erf(p * 0.7071067811865476))
    u, v = z[:, :D_MODEL], z[:, D_MODEL:]
    mu = jnp.mean(v, axis=-1, keepdims=True)
    d = v - mu
    vn = d * lax.rsqrt(jnp.mean(d * d, axis=-1, keepdims=True) + LN_EPS) * ln_w + ln_b
    ii = lax.broadcasted_iota(jnp.int32, (SGU_CHUNK, SGU_CHUNK), 0)
    jj = lax.broadcasted_iota(jnp.int32, (SGU_CHUNK, SGU_CHUNK), 1)
    mask = (jj <= ii).astype(F32)
    gi = lax.broadcasted_iota(jnp.int32, (SGU_GROUPS, D_MODEL), 0)
    ci = lax.broadcasted_iota(jnp.int32, (SGU_GROUPS, D_MODEL), 1) // SGU_CHUNK
    bias = mdot(sbt, (gi == ci).astype(F32), "nn", True)
    rows = []
    for c in range(tm // SGU_CHUNK):
        cols = []
        for g in range(SGU_GROUPS):
            blk = vn[c * SGU_CHUNK:(c + 1) * SGU_CHUNK, g * SGU_CHUNK:(g + 1) * SGU_CHUNK]
            cols.append(mdot(sw[g] * mask, blk, "nn", False))
        rows.append(jnp.concatenate(cols, axis=1) + bias)
    return (u * jnp.concatenate(rows, axis=0),)


def _softplus(x):
    return jnp.maximum(x, 0.0) + jnp.log1p(jnp.exp(-jnp.abs(x)))


def f_pre(q, wl, w0, al, a0, gl, k_k, k_a):
    qr, qk, qv, ql = q[:, :1024], q[:, 1024:2048], q[:, 2048:3072], q[:, 3072:]
    return _f_pre(qr, qk, qv, ql, wl, w0, al, a0, gl, k_k, k_a)


def _f_pre(qr, qk, qv, ql, wl, w0, al, a0, gl, k_k, k_a):
    xw, xa, xg = ql[:, :128], ql[:, 128:256], ql[:, 256:512]
    wr = -_softplus(-(w0 + mdot(jnp.tanh(xw), wl, "nn", False))) - 0.5
    lw = -jnp.exp(wr)
    aa = jax.nn.sigmoid(a0 + mdot(xa, al, "nn", False))
    g = mdot(jax.nn.sigmoid(xg), gl, "nn", False)
    kkr = qk * k_k
    kk = kkr / jnp.maximum(jnp.sqrt(segsum(kkr * kkr)), 1e-12)
    kp = qk * (1.0 + (aa - 1.0) * k_a)
    return qr, lw, kp, qv, -kk, kk * aa, g, qr, kp, qv


def f_post(o, r, kp, v, g, lnw, lnb, rk):
    mu = segsum(o) * (1.0 / HEAD)
    d = o - mu
    gn = d * lax.rsqrt(segsum(d * d) * (1.0 / HEAD) + GN_EPS)
    return ((gn * lnw + lnb + segsum(r * kp * rk) * v) * g,)


def f_mix(ya, yb, ga, gb):
    return (jax.nn.sigmoid(ga) * ya + jax.nn.sigmoid(gb) * yb,)


def f_ffn_in(h1, g):
    return _rms(h1, g), h1


def f_final(h1, m3, tgt, g):
    y = _rms(h1 + m3, g)
    err = jnp.square(y - tgt)
    return 0.5 * jnp.sum(jnp.mean(err, axis=-1))


def _cparams(n_grid):
    return pltpu.CompilerParams(dimension_semantics=("arbitrary",) * n_grid, vmem_limit_bytes=VMEM_LIMIT)


def _tile_spec(tm, w, cb):
    return pl.BlockSpec((tm, w), lambda i: (i, cb))


def _const_spec(c):
    nd = c.ndim
    return pl.BlockSpec(c.shape, lambda i: (0,) * nd)


def ew_call(fn, tiled, consts, outs, *, tm, name):
    t = tiled[0][0].shape[0]
    n_t, n_c = len(tiled), len(consts)

    def body(*refs):
        tv = [r[...].astype(F32) for r in refs[:n_t]]
        cv = [r[...] for r in refs[n_t:n_t + n_c]]
        res = fn(*tv, *cv)
        for o_ref, val in zip(refs[n_t + n_c:], res):
            o_ref[...] = val.astype(o_ref.dtype)

    return pl.pallas_call(
        body,
        name=name,
        grid=(t // tm,),
        in_specs=[_tile_spec(tm, w, cb) for _, w, cb in tiled] + [_const_spec(c) for c in consts],
        out_specs=[_tile_spec(tm, w, 0) for w, _ in outs],
        out_shape=[jax.ShapeDtypeStruct((t, w), dt) for w, dt in outs],
        compiler_params=_cparams(1),
    )(*[a for a, _, _ in tiled], *consts)


def ew_vjp_call(fn, tiled, consts, cots, d_tiled, d_consts, *, tm, name):
    t = tiled[0][0].shape[0]
    n_t, n_c, n_g = len(tiled), len(consts), len(cots)
    dt_list = [(i, dt) for i, dts in enumerate(d_tiled) for dt in dts]
    dc_list = [i for i, want in enumerate(d_consts) if want]

    def body(*refs):
        tv = [r[...].astype(F32) for r in refs[:n_t]]
        cv = [r[...] for r in refs[n_t:n_t + n_c]]
        gv = tuple(r[...].astype(F32) for r in refs[n_t + n_c:n_t + n_c + n_g])
        out_refs = refs[n_t + n_c + n_g:]
        _, vjp = jax.vjp(fn, *tv, *cv)
        grads = vjp(gv)
        for o_ref, (i, _) in zip(out_refs, dt_list):
            o_ref[...] = grads[i].astype(o_ref.dtype)
        acc_refs = out_refs[len(dt_list):]

        @pl.when(pl.program_id(0) == 0)
        def _():
            for a_ref in acc_refs:
                a_ref[...] = jnp.zeros_like(a_ref)

        for a_ref, i in zip(acc_refs, dc_list):
            a_ref[...] += grads[n_t + i]

    res = pl.pallas_call(
        body,
        name=name,
        grid=(t // tm,),
        in_specs=[_tile_spec(tm, w, cb) for _, w, cb in tiled] + [_const_spec(c) for c in consts]
        + [_tile_spec(tm, w, cb) for _, w, cb in cots],
        out_specs=[_tile_spec(tm, tiled[i][1], 0) for i, _ in dt_list] + [_const_spec(consts[i]) for i in dc_list],
        out_shape=[jax.ShapeDtypeStruct((t, tiled[i][1]), dt) for i, dt in dt_list]
        + [jax.ShapeDtypeStruct(consts[i].shape, F32) for i in dc_list],
        compiler_params=_cparams(1),
    )(*[a for a, _, _ in tiled], *consts, *[a for a, _, _ in cots])
    return res[:len(dt_list)], res[len(dt_list):]


def mm(a, b, mode, *, tm, tn, name, out_dtypes=(F32,), epi=None, extras=(), into=None):
    m = a.shape[1] if mode == "tn" else a.shape[0]
    kd = a.shape[0] if mode == "tn" else a.shape[1]
    n = b.shape[0] if mode == "nt" else b.shape[1]
    tm, tn = min(tm, m), min(tn, n)
    if mode == "nn":
        a_spec = pl.BlockSpec((tm, kd), lambda i, j: (i, 0))
        b_spec = pl.BlockSpec((kd, tn), lambda i, j: (0, j))
    elif mode == "nt":
        a_spec = pl.BlockSpec((tm, kd), lambda i, j: (i, 0))
        b_spec = pl.BlockSpec((tn, kd), lambda i, j: (j, 0))
    else:
        a_spec = pl.BlockSpec((kd, tm), lambda i, j: (0, i))
        b_spec = pl.BlockSpec((kd, tn), lambda i, j: (0, j))
    n_e = len(extras)
    o_spec = pl.BlockSpec((tm, tn), lambda i, j: (i, j))

    if into is not None:
        buf, place = into

        def body_into(a_ref, b_ref, buf_ref, o_ref):
            o_ref[0, 0] = lax.dot_general(a_ref[...].astype(BF16), b_ref[...].astype(BF16), _MDIMS[mode],
                                          preferred_element_type=F32)

        return pl.pallas_call(
            body_into,
            name=name,
            grid=(m // tm, n // tn),
            in_specs=[a_spec, b_spec, pl.BlockSpec(memory_space=pl.ANY)],
            out_specs=pl.BlockSpec((1, 1, tm, tn), lambda i, j: (*place(i, j), 0)),
            out_shape=jax.ShapeDtypeStruct(buf.shape, F32),
            input_output_aliases={2: 0},
            compiler_params=_cparams(2),
        )(a, b, buf)

    def body(a_ref, b_ref, *refs):
        c = lax.dot_general(a_ref[...].astype(BF16), b_ref[...].astype(BF16), _MDIMS[mode],
                            preferred_element_type=F32)
        res = epi(c, *[r[...] for r in refs[:n_e]]) if epi is not None else (c,)
        for o_ref, val in zip(refs[n_e:], res):
            o_ref[...] = val.astype(o_ref.dtype)

    res = pl.pallas_call(
        body,
        name=name,
        grid=(m // tm, n // tn),
        in_specs=[a_spec, b_spec] + [o_spec] * n_e,
        out_specs=[o_spec] * len(out_dtypes),
        out_shape=[jax.ShapeDtypeStruct((m, n), dt) for dt in out_dtypes],
        compiler_params=_cparams(2),
    )(a, b, *extras)
    return res if len(out_dtypes) > 1 else res[0]


RWKV_COL0 = 4096
RWKV_WIDTH = 3584
SHIFT_BLK = 512


def _shift_down(p, prev_row):
    rows = lax.broadcasted_iota(jnp.int32, p.shape, 0)
    return jnp.where(rows == 0, prev_row, pltpu.roll(p, 1, 0))


def shiftmix_fwd(p_all, sbp, *, tm):
    t = p_all.shape[0]
    tm = min(tm, t)
    c0 = RWKV_COL0 // SHIFT_BLK
    hb = tm // 8

    def body(p_ref, halo_ref, sb_ref, q_ref):
        p = p_ref[...]
        prev = jnp.where(pl.program_id(0) == 0, 0.0, halo_ref[7:8, :])
        q_ref[...] = p * sb_ref[0:1, :] + _shift_down(p, prev) * sb_ref[1:2, :]

    return pl.pallas_call(
        body,
        name="shiftmix_fwd",
        grid=(t // tm, RWKV_WIDTH // SHIFT_BLK),
        in_specs=[
            pl.BlockSpec((tm, SHIFT_BLK), lambda i, j: (i, c0 + j)),
            pl.BlockSpec((8, SHIFT_BLK), lambda i, j: (jnp.maximum(i * hb - 1, 0), c0 + j)),
            pl.BlockSpec((2, SHIFT_BLK), lambda i, j: (0, j)),
        ],
        out_specs=pl.BlockSpec((tm, SHIFT_BLK), lambda i, j: (i, j)),
        out_shape=jax.ShapeDtypeStruct((t, RWKV_WIDTH), F32),
        compiler_params=_cparams(2),
    )(p_all, p_all, sbp)


def shiftmix_bwd(dq, col0, p_all, sbp, *, tm, name):
    t, w = dq.shape
    n_i = t // tm
    hb = tm // 8
    cq = col0 // SHIFT_BLK
    cp = (RWKV_COL0 + col0) // SHIFT_BLK

    def body(dq_ref, dqn_ref, p_ref, ph_ref, sb_ref, dp_ref, dsb_ref):
        i = pl.program_id(1)
        dq_t = dq_ref[...]
        rows = lax.broadcasted_iota(jnp.int32, dq_t.shape, 0)
        nxt = jnp.where(i == n_i - 1, 0.0, dqn_ref[0:1, :])
        up = jnp.where(rows == tm - 1, nxt, pltpu.roll(dq_t, tm - 1, 0))
        dp_ref[...] = (dq_t * sb_ref[0:1, :] + up * sb_ref[1:2, :]).astype(dp_ref.dtype)
        p = p_ref[...]
        prev = jnp.where(i == 0, 0.0, ph_ref[7:8, :])
        s0 = jnp.sum(dq_t * p, axis=0, keepdims=True)
        s1 = jnp.sum(dq_t * _shift_down(p, prev), axis=0, keepdims=True)
        two = lax.broadcasted_iota(jnp.int32, (2, SHIFT_BLK), 0)

        @pl.when(i == 0)
        def _():
            dsb_ref[...] = jnp.zeros_like(dsb_ref)

        dsb_ref[...] += jnp.where(two == 0, s0, s1)

    return pl.pallas_call(
        body,
        name=name,
        grid=(w // SHIFT_BLK, n_i),
        in_specs=[
            pl.BlockSpec((tm, SHIFT_BLK), lambda j, i: (i, j)),
            pl.BlockSpec((8, SHIFT_BLK), lambda j, i: (jnp.minimum((i + 1) * hb, t // 8 - 1), j)),
            pl.BlockSpec((tm, SHIFT_BLK), lambda j, i: (i, cp + j)),
            pl.BlockSpec((8, SHIFT_BLK), lambda j, i: (jnp.maximum(i * hb - 1, 0), cp + j)),
            pl.BlockSpec((2, SHIFT_BLK), lambda j, i: (0, cq + j)),
        ],
        out_specs=[
            pl.BlockSpec((tm, SHIFT_BLK), lambda j, i: (i, j)),
            pl.BlockSpec((2, SHIFT_BLK), lambda j, i: (0, j)),
        ],
        out_shape=[jax.ShapeDtypeStruct((t, w), BF16), jax.ShapeDtypeStruct((2, w), F32)],
        compiler_params=_cparams(2),
    )(dq, dq, p_all, p_all, sbp)


def final_call(h1, m3, tgt, g_final, *, tm):
    t = h1.shape[0]

    def body(h1_ref, m3_ref, tgt_ref, g_ref, dh_ref, dhb_ref, dg_ref, loss_ref):
        loss, vjp = jax.vjp(f_final, h1_ref[...], m3_ref[...], tgt_ref[...], g_ref[...])
        dh, _, _, dg = vjp(jnp.ones((), F32))
        dh_ref[...] = dh
        dhb_ref[...] = dh.astype(BF16)

        @pl.when(pl.program_id(0) == 0)
        def _():
            dg_ref[...] = jnp.zeros_like(dg_ref)
            loss_ref[...] = jnp.zeros_like(loss_ref)

        dg_ref[...] += dg
        loss_ref[...] += jnp.full(loss_ref.shape, loss, F32)

    tile = _tile_spec(tm, D_MODEL, 0)
    return pl.pallas_call(
        body,
        name="final_loss",
        grid=(t // tm,),
        in_specs=[tile, tile, tile, _const_spec(g_final)],
        out_specs=[tile, tile, _const_spec(g_final), pl.BlockSpec((8, 128), lambda i: (0, 0))],
        out_shape=[jax.ShapeDtypeStruct((t, D_MODEL), F32), jax.ShapeDtypeStruct((t, D_MODEL), BF16),
                   jax.ShapeDtypeStruct(g_final.shape, F32), jax.ShapeDtypeStruct((8, 128), F32)],
        compiler_params=_cparams(1),
    )(h1, m3, tgt, g_final)


N_SGU = 2048
N_RWKV = 3360
LORA_W, LORA_A, LORA_G = 64, 64, 160


def _pad_rwkv_cols(z):
    zero = lambda n: jnp.zeros(z.shape[:-1] + (n,), z.dtype)
    return jnp.concatenate([z[..., :3072], z[..., 3072:3136], zero(64), z[..., 3136:3200], zero(64),
                            z[..., 3200:3360], zero(96)], axis=-1)


def _unpad_rwkv_cols(z):
    return jnp.concatenate([z[..., :3072], z[..., 3072:3136], z[..., 3200:3264], z[..., 3328:3488]], axis=-1)


def _pad_win_rows(wt):
    z = wt[N_SGU:N_SGU + N_RWKV]
    zero = lambda n: jnp.zeros((n, wt.shape[1]), wt.dtype)
    return jnp.concatenate([wt[:N_SGU], wt[N_SGU + N_RWKV:], z[:3072], z[3072:3136], zero(64), z[3136:3200], zero(64),
                            z[3200:3360], zero(96)], axis=0)


def _unpad_win_rows(wt):
    z = wt[RWKV_COL0:]
    return jnp.concatenate([wt[:N_SGU], z[:3072], z[3072:3136], z[3200:3264], z[3328:3488], wt[N_SGU:RWKV_COL0]],
                           axis=0)


def _pad_rows(w, n):
    return jnp.concatenate([w, jnp.zeros((n - w.shape[0],) + w.shape[1:], w.dtype)], axis=0)


def _relu2_epi(c):
    return c, jnp.square(jnp.maximum(c, 0.0))


def _relu2_bwd_epi(c, hid):
    return (c * (2.0 * jnp.maximum(hid.astype(F32), 0.0)),)


def _add_epi(c, x):
    return (c + x,)


def _pre_fwd(*args):
    res = f_pre(*args)
    return res[1], res[2], res[4], res[5], res[6]


def local_step(x, tgt, w, late_token, late_weights, pair_start, pair_finish, pack_early):
    d = D_MODEL
    win_pt = _pad_win_rows(w["w_in"])
    sbp = _pad_rwkv_cols(w["shift_b"])
    wl = _pad_rows(w["w_lora_w"], 128)
    al = _pad_rows(w["a_lora_w"], 128)
    gl = _pad_rows(w["g_lora_w"], 256)
    sbt = w["sgu_b"].T

    (a_bf,) = ew_call(lambda x_, g_: (f_norm_in(x_, g_)[0],), [(x, d, 0)], [w["g_mix"] + late_token[:1, :1]],
                      [(d, BF16)], tm=512, name="norm_in")
    p_all = mm(a_bf, win_pt, "nt", tm=2048, tn=1280, name="mm_in")
    sgu_t = [(p_all, 2 * d, 0)]
    sgu_c = [w["sgu_ln_w"], w["sgu_ln_b"], w["sgu_w"], sbt]
    (s_bf,) = ew_call(f_sgu, sgu_t, sgu_c, [(d, BF16)], tm=512, name="sgu_fwd")
    q = shiftmix_fwd(p_all, sbp, tm=2048)
    pre_t = [(q, RWKV_WIDTH, 0)]
    pre_c = [wl, w["w0"], al, w["a0"], gl, w["k_k"], w["k_a"]]
    lw, kp, na, nb, g = ew_call(_pre_fwd, pre_t, pre_c, [(d, F32)] * 5, tm=256, name="rwkv_pre_fwd")
    scan_ops = [(q, 0), (lw, 0), (kp, 0), (q, 2), (na, 0), (nb, 0)]
    o, s0s = scan_fwd(scan_ops)
    w = {**w, **late_weights(o)}
    ya = mm(s_bf, w["w_proj_a"], "nn", tm=1024, tn=1024, name="mm_proj_a")
    post_t = [(o, d, 0), (q, d, 0), (kp, d, 0), (q, d, 2), (g, d, 0)]
    post_c = [w["ln_x_w"], w["ln_x_b"], w["r_k"]]
    (ob_bf,) = ew_call(f_post, post_t, post_c, [(d, BF16)], tm=512, name="rwkv_post_fwd")
    yb = mm(ob_bf, w["w_proj_b"], "nn", tm=1024, tn=1024, name="mm_proj_b")
    mix_t = [(ya, d, 0), (yb, d, 0), (p_all, d, 2), (p_all, d, 3)]
    (mixed_bf,) = ew_call(f_mix, mix_t, [], [(d, BF16)], tm=512, name="mix_fwd")
    h1 = mm(mixed_bf, w["w_out"], "nn", tm=1024, tn=1024, name="mm_out", epi=_add_epi, extras=(x,))
    (f_bf,) = ew_call(lambda h_, g_: (f_ffn_in(h_, g_)[0],), [(h1, d, 0)], [w["g_ffn"]], [(d, BF16)], tm=512,
                      name="ffn_norm")
    hid, act_bf = mm(f_bf, w["w_ffn1"], "nn", tm=2048, tn=1024, name="mm_ffn1", out_dtypes=(BF16, BF16), epi=_relu2_epi)
    m3 = mm(act_bf, w["w_ffn2"], "nn", tm=1024, tn=512, name="mm_ffn2")
    dh2, dh2_bf, dg_final, loss = final_call(h1, m3, tgt, w["g_final"], tm=512)

    dhid_bf = mm(dh2_bf, w["w_ffn2"], "nt", tm=2048, tn=1024, name="mm_dact", out_dtypes=(BF16,), epi=_relu2_bwd_epi,
                 extras=(hid,))
    late_g = lax.empty((N_CHIPS, 2, pack_rows(LATE), HALF_W), F32)
    late_g = mm(act_bf, dh2_bf, "tn", tm=1024, tn=HALF_W, name="mm_dw_ffn2",
                into=(late_g, lambda i, j: (i, j, PIECE_OFF["w_ffn2"] // 1024)))
    df = mm(dhid_bf, w["w_ffn1"], "nt", tm=1024, tn=512, name="mm_df")
    late_g = mm(f_bf, dhid_bf, "tn", tm=1024, tn=HALF_W, name="mm_dw_ffn1",
                into=(late_g, lambda i, j: (j // 2, j % 2, PIECE_OFF["w_ffn1"] // 1024)))
    (dh1, dh1_bf), (dg_ffn,) = ew_vjp_call(f_ffn_in, [(h1, d, 0)], [w["g_ffn"]], [(df, d, 0), (dh2, d, 0)],
                                           [(F32, BF16)], [True], tm=512, name="ffn_norm_bwd")
    dmixed = mm(dh1_bf, w["w_out"], "nt", tm=1024, tn=1024, name="mm_dmixed")
    late_g = mm(mixed_bf, dh1_bf, "tn", tm=256, tn=HALF_W, name="mm_dw_out",
                into=(late_g, lambda i, j: (i, j, PIECE_OFF["w_out"] // 256)))
    (dya_bf, dyb_bf, dga_bf, dgb_bf), _ = ew_vjp_call(f_mix, mix_t, [], [(dmixed, d, 0)], [(BF16,)] * 4, [], tm=256,
                                                      name="mix_bwd")
    dob = mm(dyb_bf, w["w_proj_b"], "nt", tm=1024, tn=1024, name="mm_dob")
    late_g = mm(ob_bf, dyb_bf, "tn", tm=256, tn=HALF_W, name="mm_dw_proj_b",
                into=(late_g, lambda i, j: (i, j, PIECE_OFF["w_proj_b"] // 256)))
    late_g = mm(s_bf, dya_bf, "tn", tm=256, tn=HALF_W, name="mm_dw_proj_a",
                into=(late_g, lambda i, j: (i, j, PIECE_OFF["w_proj_a"] // 256)))
    late_state, late_token = pair_start(late_g, "late")
    post_c_after = [w["ln_x_w"] + late_token[:1, :1]] + post_c[1:]
    (do, dr_p, dkp_p, dv_p, dg), (dlnx_w, dlnx_b, dr_k) = ew_vjp_call(
        f_post, post_t, post_c_after, [(dob, d, 0)], [(F32,)] * 5, [True] * 3, tm=256, name="rwkv_post_bwd")
    late_part, late_part16 = pair_finish(late_state, do, "late")
    *scan_g, late_slots = scan_bwd(scan_ops, s0s, do, late_part16)
    pre_g = [(z, d, 0) for z in scan_g] + [(dg, d, 0), (dr_p, d, 0), (dkp_p, d, 0), (dv_p, d, 0)]
    (dq,), (dwl, dw0, dal, da0, dgl, dk_k, dk_a) = ew_vjp_call(
        f_pre, pre_t, pre_c, pre_g, [(F32,)], [True] * 7, tm=256, name="rwkv_pre_bwd")
    dp_rwkv, dsb = shiftmix_bwd(dq, 0, p_all, sbp, tm=1024, name="shiftmix_bwd")
    ds = mm(dya_bf, w["w_proj_a"], "nt", tm=1024, tn=1024, name="mm_ds")
    (dp_sgu,), (dln_w, dln_b, dsw, dsbt) = ew_vjp_call(f_sgu, sgu_t, sgu_c, [(ds, d, 0)], [(BF16,)], [True] * 4,
                                                       tm=256, name="sgu_bwd")
    dp_all = jnp.concatenate([dp_sgu, dga_bf, dgb_bf, dp_rwkv], axis=1)
    d_in_pt = mm(dp_all, a_bf, "tn", tm=1280, tn=1024, name="mm_dw_in")
    early_state, early_token = pair_start(pack_early({
        "w_in": _unpad_win_rows(d_in_pt), "w_lora_w": dwl[:LORA_W], "a_lora_w": dal[:LORA_A],
        "g_lora_w": dgl[:LORA_G]}), "early")
    da = mm(dp_all, win_pt, "nn", tm=512, tn=512, name="mm_da")
    g_mix_after = w["g_mix"] + early_token[:1, :1]
    (grad_x,), (dg_mix,) = ew_vjp_call(f_norm_in, [(x, d, 0)], [g_mix_after], [(da, d, 0), (dh1, d, 0)], [(F32,)],
                                       [True], tm=512, name="norm_in_bwd")

    grads = {
        "g_mix": dg_mix, "sgu_ln_w": dln_w, "sgu_ln_b": dln_b, "sgu_w": dsw, "sgu_b": dsbt.T,
        "shift_b": _unpad_rwkv_cols(dsb),
        "w0": dw0, "a0": da0, "k_k": dk_k, "k_a": dk_a, "r_k": dr_k, "ln_x_w": dlnx_w, "ln_x_b": dlnx_b,
        "g_ffn": dg_ffn, "g_final": dg_final,
    }
    return loss[0, 0], grad_x, grads, (late_part, late_slots), early_state


MESH = pl.DeviceIdType.MESH
N_CHIPS = 4
SMALL_ROWS = 160
_ANY = pl.BlockSpec(memory_space=pl.ANY)


def _coords():
    return lax.axis_index("x"), lax.axis_index("y"), lax.axis_index("c")


def _other_chips(x, y):
    return [(1 - x, y), (x, 1 - y), (1 - x, 1 - y)]


def _remote(src, dst, send_sems, recv_sems, k, to):
    return pltpu.make_async_remote_copy(src_ref=src, dst_ref=dst, send_sem=send_sems.at[k], recv_sem=recv_sems.at[k],
                                        device_id=to, device_id_type=MESH)


def gather_shards(pack):
    def body(src_ref, out_ref, token, send_sems, recv_sems):
        x, y, c = _coords()
        me = 2 * x + y
        sib = (x, y, 1 - c)
        chips = _other_chips(x, y)
        first = [_remote(src_ref.at[c], out_ref.at[me, c], send_sems, recv_sems, k, (cx, cy, c))
                 for k, (cx, cy) in enumerate(chips)]
        for cp in first:
            cp.start()
        passed = []
        for k, (cx, cy) in enumerate(chips):
            j = 2 * cx + cy
            _remote(src_ref.at[c], out_ref.at[j, c], send_sems, recv_sems, k, (cx, cy, c)).wait_recv()
            fwd = _remote(out_ref.at[j, c], out_ref.at[j, c], send_sems, recv_sems, 3 + k, sib)
            fwd.start()
            passed.append(fwd)
        for k, (cx, cy) in enumerate(chips):
            j = 2 * cx + cy
            _remote(out_ref.at[j, 1 - c], out_ref.at[j, 1 - c], send_sems, recv_sems, 3 + k, sib).wait_recv()
        for cp in first + passed:
            cp.wait_send()
        token[...] = jnp.zeros_like(token)

    return pl.pallas_call(
        body,
        name="gather_shards",
        in_specs=[_ANY],
        out_specs=[_ANY, pl.BlockSpec(memory_space=pltpu.VMEM)],
        out_shape=[jax.ShapeDtypeStruct((N_CHIPS,) + pack.shape, pack.dtype), jax.ShapeDtypeStruct((8, 128), F32)],
        scratch_shapes=[pltpu.SemaphoreType.DMA((6,)), pltpu.SemaphoreType.DMA((6,))],
    )(pack)


def _gather_copies(pack_ref, all_ref, send_sems, recv_sems):
    x, y, c = _coords()
    me = 2 * x + y
    return [(_remote(pack_ref.at[c], all_ref.at[me, c], send_sems, recv_sems, k, (cx, cy, c)),
             _remote(pack_ref.at[c], all_ref.at[2 * cx + cy, c], send_sems, recv_sems, k, (cx, cy, c)))
            for k, (cx, cy) in enumerate(_other_chips(x, y))]


_HBM = pl.BlockSpec(memory_space=pltpu.HBM)
_SEM = pl.BlockSpec(memory_space=pltpu.SEMAPHORE)
_SIDE_EFFECT = pltpu.SideEffectType.DATAFLOW_SIDE_EFFECTING


def split_start(name, copies, n, src, land_shape, after=None):
    def body(src_ref, land_ref, *refs):
        send_sems, recv_sems, token = refs[-5], refs[-4], refs[-1]
        for send, _ in copies(src_ref, land_ref, send_sems, recv_sems):
            send.start()
        token[...] = jnp.zeros_like(token)

    extra = () if after is None else (after,)
    *state, token = pl.pallas_call(
        body,
        name=name,
        out_shape=(pltpu.SemaphoreType.DMA((n,)), pltpu.SemaphoreType.DMA((n,)), pltpu.HBM(src.shape, src.dtype),
                   pltpu.HBM(land_shape, src.dtype), jax.ShapeDtypeStruct((8, 128), F32)),
        in_specs=(_HBM, _HBM) + (pl.BlockSpec(memory_space=pl.ANY),) * len(extra),
        out_specs=(_SEM, _SEM, _HBM, _HBM, pl.BlockSpec(memory_space=pltpu.VMEM)),
        input_output_aliases={0: 2, 1: 3},
        compiler_params=pltpu.CompilerParams(has_side_effects=_SIDE_EFFECT),
    )(pltpu.with_memory_space_constraint(src, pltpu.HBM),
      pltpu.with_memory_space_constraint(lax.empty(land_shape, src.dtype), pltpu.HBM), *extra)
    return state, token


def split_wait(name, copies, state, after):
    send_sems, recv_sems, src, land = state

    def body(src_ref, land_ref, send_sems, recv_sems, after_ref, src_out, land_out):
        for send, arrival in copies(src_ref, land_ref, send_sems, recv_sems):
            send.wait_send()
            arrival.wait_recv()

    return pl.pallas_call(
        body,
        name=name,
        out_shape=(pltpu.HBM(src.shape, src.dtype), pltpu.HBM(land.shape, land.dtype)),
        in_specs=(_HBM, _HBM, _SEM, _SEM, pl.BlockSpec(memory_space=pl.ANY)),
        out_specs=(_HBM, _HBM),
        input_output_aliases={0: 0, 1: 1},
        compiler_params=pltpu.CompilerParams(has_side_effects=_SIDE_EFFECT),
    )(src, land, send_sems, recv_sems, after)


def gather_forward(got):
    def body(got_ref, out_ref, send_sems, recv_sems):
        x, y, c = _coords()
        sib = (x, y, 1 - c)
        slots = [2 * cx + cy for cx, cy in _other_chips(x, y)]
        sends = [_remote(got_ref.at[j, c], out_ref.at[j, c], send_sems, recv_sems, k, sib) for k, j in enumerate(slots)]
        for cp in sends:
            cp.start()
        for k, j in enumerate(slots):
            _remote(got_ref.at[j, 1 - c], out_ref.at[j, 1 - c], send_sems, recv_sems, k, sib).wait_recv()
        for cp in sends:
            cp.wait_send()

    return pl.pallas_call(
        body,
        name="gather_forward",
        in_specs=[_ANY],
        out_specs=_ANY,
        out_shape=jax.ShapeDtypeStruct(got.shape, got.dtype),
        input_output_aliases={0: 0},
        scratch_shapes=[pltpu.SemaphoreType.DMA((3,)), pltpu.SemaphoreType.DMA((3,))],
    )(got)


def pair_sum(g, got, tag, *, tm):
    n, _, rows, width = g.shape

    def body(c_ref, own_ref, got_ref, out_ref, out16_ref):
        total = own_ref[0, 0] + got_ref[0]
        out_ref[0] = total
        out16_ref[0] = total.astype(BF16)

    blk = pl.BlockSpec((1, tm, width), lambda j, i, c_ref: (j, i, 0))
    return pl.pallas_call(
        body,
        name="pair_sum_" + tag,
        grid_spec=pltpu.PrefetchScalarGridSpec(
            num_scalar_prefetch=1,
            grid=(n, rows // tm),
            in_specs=[pl.BlockSpec((1, 1, tm, width), lambda j, i, c_ref: (j, c_ref[0], i, 0)), blk],
            out_specs=[blk, blk],
        ),
        out_shape=[jax.ShapeDtypeStruct(got.shape, F32), jax.ShapeDtypeStruct(got.shape, BF16)],
        compiler_params=_cparams(2),
    )(lax.axis_index("c").reshape(1).astype(jnp.int32), g, got)


def _pair_copies(g_ref, got_ref, send_sems, recv_sems):
    x, y, c = _coords()
    copies = [_remote(g_ref.at[j, 1 - c], got_ref.at[j], send_sems, recv_sems, j, (x, y, 1 - c))
              for j in range(N_CHIPS)]
    return [(cp, cp) for cp in copies]


def _chip_copies(p_ref, slots_ref, send_sems, recv_sems):
    x, y, c = _coords()
    me = 2 * x + y
    return [(_remote(p_ref.at[2 * cx + cy], slots_ref.at[me], send_sems, recv_sems, k, (cx, cy, c)),
             _remote(p_ref.at[me], slots_ref.at[2 * cx + cy], send_sems, recv_sems, k, (cx, cy, c)))
            for k, (cx, cy) in enumerate(_other_chips(x, y))]


def sum_with_own(own, slots, mine, after, *, tm, name):
    n, rows, width = slots.shape

    def body(mine_ref, own_ref, *refs):
        acc = None
        for s in range(n):
            term = jnp.where(mine_ref[0] == s, own_ref[0], refs[s][0].astype(F32))
            acc = term if acc is None else acc + term
        refs[-1][...] = acc

    return pl.pallas_call(
        body,
        name=name,
        grid_spec=pltpu.PrefetchScalarGridSpec(
            num_scalar_prefetch=1,
            grid=(rows // tm,),
            in_specs=[pl.BlockSpec((1, tm, width), lambda i, mine_ref: (mine_ref[0], i, 0))]
            + [pl.BlockSpec((1, tm, width), lambda i, mine_ref, s=s: (s, i, 0)) for s in range(n)]
            + [pl.BlockSpec(after.shape, lambda i, mine_ref: (0,) * after.ndim)],
            out_specs=pl.BlockSpec((tm, width), lambda i, mine_ref: (i, 0)),
        ),
        out_shape=jax.ShapeDtypeStruct((rows, width), F32),
        compiler_params=_cparams(1),
    )(mine.reshape(1).astype(jnp.int32), own, *([slots] * n), after)


def exchange_halves(s, tag):
    nq = 4
    rq = s.shape[0] // nq
    assert rq * nq == s.shape[0] and rq % 8 == 0

    def body(s_ref, out_ref, sbuf, rbuf, send_sems, recv_sems, in_sems, out_sems):
        x, y, c = _coords()
        sib = (x, y, 1 - c)
        rows = lambda q: pl.ds(q * rq, rq)
        loads = [pltpu.make_async_copy(s_ref.at[rows(q)], sbuf.at[rows(q)], in_sems.at[q]) for q in range(nq)]
        for cp in loads:
            cp.start()
        sends = []
        for q in range(nq):
            loads[q].wait()
            sends.append(_remote(sbuf.at[rows(q)], rbuf.at[rows(q)], send_sems, recv_sems, q, sib))
            sends[q].start()
        stores = []
        for q in range(nq):
            sends[q].wait_recv()
            stores.append(pltpu.make_async_copy(rbuf.at[rows(q)], out_ref.at[rows(q)], out_sems.at[q]))
            stores[q].start()
        for cp in sends:
            cp.wait_send()
        for cp in stores:
            cp.wait()

    return pl.pallas_call(
        body,
        name="exchange_halves_" + tag,
        in_specs=[_ANY],
        out_specs=_ANY,
        out_shape=jax.ShapeDtypeStruct(s.shape, s.dtype),
        scratch_shapes=[pltpu.VMEM(s.shape, s.dtype), pltpu.VMEM(s.shape, s.dtype)]
        + [pltpu.SemaphoreType.DMA((nq,))] * 4,
        compiler_params=pltpu.CompilerParams(vmem_limit_bytes=VMEM_LIMIT),
    )(s)


def sum_all(s, after):
    rows = s.shape[0]
    half = rows // 2

    def body(s_ref, after_ref, out_ref, theirs, pair, slots, send_sems, recv_sems):
        x, y, c = _coords()
        me = 2 * x + y
        sib = (x, y, 1 - c)
        chips = _other_chips(x, y)
        swap = _remote(s_ref, theirs, send_sems, recv_sems, 0, sib)
        swap.start()
        swap.wait_recv()
        pair[...] = s_ref[...] + theirs[...]
        mine = pl.ds(pl.multiple_of(c * half, 8), half)
        other = pl.ds(pl.multiple_of((1 - c) * half, 8), half)
        sends = [_remote(pair.at[mine], slots.at[me], send_sems, recv_sems, 1 + k, (cx, cy, c))
                 for k, (cx, cy) in enumerate(chips)]
        for cp in sends:
            cp.start()
        for k, (cx, cy) in enumerate(chips):
            _remote(pair.at[mine], slots.at[2 * cx + cy], send_sems, recv_sems, 1 + k, (cx, cy, c)).wait_recv()
        slots[me] = pair[mine]
        out_ref[mine] = ((slots[0] + slots[1]) + slots[2]) + slots[3]
        last = _remote(out_ref.at[mine], out_ref.at[mine], send_sems, recv_sems, 4, sib)
        last.start()
        _remote(out_ref.at[other], out_ref.at[other], send_sems, recv_sems, 4, sib).wait_recv()
        for cp in [swap] + sends + [last]:
            cp.wait_send()

    vmem = pl.BlockSpec(memory_space=pltpu.VMEM)
    return pl.pallas_call(
        body,
        name="sum_all",
        in_specs=[vmem, vmem],
        out_specs=vmem,
        out_shape=jax.ShapeDtypeStruct(s.shape, s.dtype),
        scratch_shapes=[pltpu.VMEM(s.shape, s.dtype), pltpu.VMEM(s.shape, s.dtype),
                        pltpu.VMEM((N_CHIPS, half, s.shape[1]), s.dtype), pltpu.SemaphoreType.DMA((5,)),
                        pltpu.SemaphoreType.DMA((5,))],
        compiler_params=pltpu.CompilerParams(vmem_limit_bytes=VMEM_LIMIT),
    )(s, after)


ADAM_LR = 0.001
ADAM_B1 = 0.9
ADAM_B2 = 0.999
ADAM_EPS = 1e-08
ADAM_WD = 0.01
ADAM_STEP = 10


def f_adamw(g, w, m, v):
    m = ADAM_B1 * m + (1.0 - ADAM_B1) * g
    v = ADAM_B2 * v + (1.0 - ADAM_B2) * jnp.square(g)
    m_hat = m / (1.0 - ADAM_B1 ** ADAM_STEP)
    v_hat = v / (1.0 - ADAM_B2 ** ADAM_STEP)
    delta = -ADAM_LR * (m_hat / (jnp.sqrt(v_hat) + ADAM_EPS) + ADAM_WD * w)
    return delta, m, v


def adamw_many(gs, ws, ms, vs):
    n = len(gs)

    def body(*refs):
        ins, outs = refs[:4 * n], refs[4 * n:]
        for i in range(n):
            delta, nm, nv = f_adamw(ins[i][...], ins[n + i][...], ins[2 * n + i][...], ins[3 * n + i][...])
            outs[i][...] = delta
            outs[n + i][...] = nm
            outs[2 * n + i][...] = nv

    vmem = pl.BlockSpec(memory_space=pltpu.VMEM)
    res = pl.pallas_call(
        body,
        name="adamw_small",
        in_specs=[vmem] * (4 * n),
        out_specs=[vmem] * (3 * n),
        out_shape=[jax.ShapeDtypeStruct(w.shape, F32) for w in ws] * 3,
    )(*gs, *ws, *ms, *vs)
    return res[:n], res[n:2 * n], res[2 * n:]


EARLY = ["w_in", "w_lora_w", "a_lora_w", "g_lora_w"]
LATE = ["w_ffn1", "w_ffn2", "w_proj_b", "w_out", "w_proj_a"]
LORAS = ["w_lora_w", "a_lora_w", "g_lora_w"]
HALF_W = 512
PIECE_ROWS = {"w_in": 1864, "w_ffn1": 1024, "w_ffn2": 1024, "w_proj_a": 256, "w_proj_b": 256, "w_out": 256,
              "w_lora_w": 32, "a_lora_w": 32, "g_lora_w": 80}
PIECE_OFF = {"w_in": 0, "w_lora_w": 1920, "a_lora_w": 1952, "g_lora_w": 2000,
             "w_ffn1": 0, "w_ffn2": 1024, "w_proj_b": 2048, "w_out": 2304, "w_proj_a": 2560}
LO_OFF = 2080


def pack_rows(group):
    return 2304 if group is EARLY else 2816
SHARD_AXIS = {"w_in": 1, "w_proj_a": 0, "w_lora_w": 1, "a_lora_w": 1, "g_lora_w": 1, "w_proj_b": 0, "w_out": 0,
              "w_ffn1": 1, "w_ffn2": 0}
SHARD_SHAPE = {"w_in": (1024, 1864), "w_proj_a": (256, 1024), "w_lora_w": (64, 256), "a_lora_w": (64, 256),
               "g_lora_w": (160, 256), "w_proj_b": (256, 1024), "w_out": (256, 1024), "w_ffn1": (1024, 1024),
               "w_ffn2": (1024, 1024)}
SHIFT_SHARD = (2, 840)
VECTORS = ["g_mix", "sgu_ln_w", "sgu_ln_b", "w0", "a0", "k_k", "k_a", "r_k", "ln_x_w", "ln_x_b", "g_ffn", "g_final"]
SMALL = VECTORS + ["sgu_w", "sgu_b"]
SMALL_SHAPE = {**{n: (1, 1024) for n in VECTORS}, "sgu_w": (8, 128, 128), "sgu_b": (8, 128)}
WEIGHTS = ["g_mix", "w_in", "sgu_ln_w", "sgu_ln_b", "sgu_w", "sgu_b", "w_proj_a", "shift_b", "w_lora_w", "w0",
           "a_lora_w", "a0", "g_lora_w", "k_k", "k_a", "r_k", "ln_x_w", "ln_x_b", "w_proj_b", "w_out", "g_ffn",
           "w_ffn1", "w_ffn2", "g_final"]


def _size(shape):
    n = 1
    for s in shape:
        n *= s
    return n


def _pack_rows(parts, rows, dtype):
    flat = jnp.concatenate([p.reshape(-1).astype(dtype) for p in parts])
    return jnp.concatenate([flat, jnp.zeros((rows * 1024 - flat.shape[0],), dtype)]).reshape(rows, 1024)


def _unpack_rows(packed, shapes):
    flat = packed.reshape(-1)
    out, off = [], 0
    for shp in shapes:
        out.append(flat[off:off + _size(shp)].reshape(shp))
        off += _size(shp)
    return out


def _shard_of(name, full, j):
    ax = SHARD_AXIS[name]
    n = SHARD_SHAPE[name][ax]
    return lax.slice_in_dim(full, j * n, (j + 1) * n, axis=ax)


def _pad_cols(z, n):
    return jnp.concatenate([z, jnp.zeros((z.shape[0], n - z.shape[1]), z.dtype)], axis=1)


def _row_form(name, s):
    return s.T if name == "w_in" else s


def _half_piece(name, rf, h):
    if name in LORAS:
        r = PIECE_ROWS[name]
        return _pad_cols(rf[h * r:(h + 1) * r], HALF_W)
    return rf[:, HALF_W * h:HALF_W * (h + 1)]


def _pack_half(group, rf_fn, h, dtype, tail=()):
    parts, pos, rows = [], 0, pack_rows(group)
    for n in group:
        if PIECE_OFF[n] > pos:
            parts.append(jnp.zeros((PIECE_OFF[n] - pos, HALF_W), dtype))
        parts.append(_half_piece(n, rf_fn(n), h).astype(dtype))
        pos = PIECE_OFF[n] + PIECE_ROWS[n]
    for t in tail:
        parts.append(t)
        pos += t.shape[0]
    parts.append(jnp.zeros((rows - pos, HALF_W), dtype))
    return jnp.concatenate(parts, axis=0)


def _piece(pack, name):
    return pack[PIECE_OFF[name]:PIECE_OFF[name] + PIECE_ROWS[name]]


def _join_halves(name, p0, p1):
    if name in LORAS:
        return jnp.concatenate([p0[:, :SHARD_SHAPE[name][1]], p1[:, :SHARD_SHAPE[name][1]]], axis=0)
    return jnp.concatenate([p0, p1], axis=1)


def _grad_row_form(name, full, j):
    if name == "w_in":
        return full[SHARD_SHAPE[name][1] * j:SHARD_SHAPE[name][1] * (j + 1)]
    return _shard_of(name, full, j)


def adamw_weight(name, g_own, g_other, w, m, v):
    rows, width = w.shape
    if name in LORAS:
        tm = PIECE_ROWS[name]
        grid = (2, 1)
        native = pl.BlockSpec((tm, width), lambda h, i: (h, 0))
    elif name == "w_in":
        tm, lanes = rows, 256
        grid = (2, HALF_W // lanes)
        native = pl.BlockSpec((tm, lanes), lambda h, i: (0, h * (HALF_W // lanes) + i))
    else:
        tm = rows
        grid = (2, 1)
        native = pl.BlockSpec((tm, HALF_W), lambda h, i: (i, h))
    assert PIECE_OFF[name] % tm == 0
    off = PIECE_OFF[name] // tm
    if name == "w_in":
        packed = pl.BlockSpec((tm, lanes), lambda h, i: (0, i))
    else:
        packed = pl.BlockSpec((tm, HALF_W), lambda h, i: (off + i, 0))

    def body(go_ref, gx_ref, w_ref, m_ref, v_ref, g_ref, d_ref, nm_ref, nv_ref):
        g = jnp.where(pl.program_id(0) == lax.axis_index("c"), go_ref[...], gx_ref[...])[:, :w_ref.shape[1]]
        delta, nm, nv = f_adamw(g, w_ref[...], m_ref[...], v_ref[...])
        g_ref[...] = g
        d_ref[...] = delta
        nm_ref[...] = nm
        nv_ref[...] = nv

    return pl.pallas_call(
        body,
        name="adamw_" + name,
        grid=grid,
        in_specs=[packed, packed, native, native, native],
        out_specs=[native] * 4,
        out_shape=[jax.ShapeDtypeStruct(w.shape, F32)] * 4,
        compiler_params=_cparams(2),
    )(g_own, g_other, w, m, v)


def kernel(x, g_mix, w_in, sgu_ln_w, sgu_ln_b, sgu_w, sgu_b, w_proj_a, shift_b, w_lora_w, w0, a_lora_w, a0, g_lora_w, k_k, k_a, r_k, ln_x_w, ln_x_b, w_proj_b, w_out, g_ffn, w_ffn1, w_ffn2, g_final, loss_target, m_g_mix, m_w_in, m_sgu_ln_w, m_sgu_ln_b, m_sgu_w, m_sgu_b, m_w_proj_a, m_shift_b, m_w_lora_w, m_w0, m_a_lora_w, m_a0, m_g_lora_w, m_k_k, m_k_a, m_r_k, m_ln_x_w, m_ln_x_b, m_w_proj_b, m_w_out, m_g_ffn, m_w_ffn1, m_w_ffn2, m_g_final, v_g_mix, v_w_in, v_sgu_ln_w, v_sgu_ln_b, v_sgu_w, v_sgu_b, v_w_proj_a, v_shift_b, v_w_lora_w, v_w0, v_a_lora_w, v_a0, v_g_lora_w, v_k_k, v_k_a, v_r_k, v_ln_x_w, v_ln_x_b, v_w_proj_b, v_w_out, v_g_ffn, v_w_ffn1, v_w_ffn2, v_g_final):
    given = dict(zip(WEIGHTS, (g_mix, w_in, sgu_ln_w, sgu_ln_b, sgu_w, sgu_b, w_proj_a, shift_b, w_lora_w, w0, a_lora_w, a0, g_lora_w, k_k, k_a, r_k, ln_x_w, ln_x_b, w_proj_b, w_out, g_ffn, w_ffn1, w_ffn2, g_final)))
    mom_m = dict(zip(WEIGHTS, (m_g_mix, m_w_in, m_sgu_ln_w, m_sgu_ln_b, m_sgu_w, m_sgu_b, m_w_proj_a, m_shift_b, m_w_lora_w, m_w0, m_a_lora_w, m_a0, m_g_lora_w, m_k_k, m_k_a, m_r_k, m_ln_x_w, m_ln_x_b, m_w_proj_b, m_w_out, m_g_ffn, m_w_ffn1, m_w_ffn2, m_g_final)))
    mom_v = dict(zip(WEIGHTS, (v_g_mix, v_w_in, v_sgu_ln_w, v_sgu_ln_b, v_sgu_w, v_sgu_b, v_w_proj_a, v_shift_b, v_w_lora_w, v_w0, v_a_lora_w, v_a0, v_g_lora_w, v_k_k, v_k_a, v_r_k, v_ln_x_w, v_ln_x_b, v_w_proj_b, v_w_out, v_g_ffn, v_w_ffn1, v_w_ffn2, v_g_final)))
    chip = 2 * lax.axis_index("x") + lax.axis_index("y")

    def local_block(tree, n):
        return tree[n] if n == "g_final" else tree[n][0]

    sb = local_block(given, "shift_b")
    lo_part = lambda z: (z - z.astype(BF16).astype(F32)).astype(BF16)
    row_form = lambda tree: (lambda n: _row_form(n, local_block(tree, n)))
    tile16 = lambda z: jnp.pad(z, ((0, 16 - z.shape[0]), (0, HALF_W - z.shape[1])))
    sb_tiles = [tile16(f(sb[:, lanes])) for f in (lambda z: z.astype(BF16), lo_part)
                for lanes in (slice(0, HALF_W), slice(HALF_W, None))]
    tails = [[_half_piece(n, lo_part(local_block(given, n)), h) for n in LORAS] + sb_tiles for h in range(2)]
    pack_w = jnp.stack([_pack_half(EARLY, row_form(given), h, BF16, tails[h]) for h in range(2)])
    gathered, gathered_token = gather_shards(pack_w)
    gathered = lax.dynamic_update_index_in_dim(gathered, pack_w, chip, 0)
    pack_late = jnp.stack([_pack_half(LATE, row_form(given), h, BF16) for h in range(2)])
    late_state, late_token = split_start("gather_start", _gather_copies, 3, pack_late, (N_CHIPS,) + pack_late.shape,
                                         gathered_token)

    def whole(group, got, own):
        half = lambda n, j, h: jnp.where(chip == j, _piece(own[h], n), _piece(got[j, h], n))
        shard = lambda n, j: _join_halves(n, half(n, j, 0), half(n, j, 1))
        return {n: jnp.concatenate([shard(n, j) for j in range(N_CHIPS)],
                                   axis=0 if n == "w_in" else SHARD_AXIS[n]) for n in group}

    w = whole(EARLY, gathered, pack_w)
    late_weights = lambda after: whole(
        LATE, gather_forward(split_wait("gather_wait", _gather_copies, late_state, after)[1]), pack_late)
    off = LO_OFF
    for n in LORAS:
        r, cols = PIECE_ROWS[n], SHARD_SHAPE[n][1]
        lo = jnp.concatenate([jnp.concatenate([gathered[j, 0, off:off + r, :cols], gathered[j, 1, off:off + r, :cols]],
                                              axis=0) for j in range(N_CHIPS)], axis=1)
        w[n] = w[n].astype(F32) + lo.astype(F32)
        off += r
    sb_tile = lambda j, t, lanes: gathered[j, 0, off + 16 * t:off + 16 * t + 2, :lanes].astype(F32)
    rest = SHIFT_SHARD[1] - HALF_W
    w["shift_b"] = jnp.concatenate(
        [jnp.concatenate([sb_tile(j, 0, HALF_W) + sb_tile(j, 2, HALF_W), sb_tile(j, 1, rest) + sb_tile(j, 3, rest)],
                         axis=1) for j in range(N_CHIPS)], axis=1)
    for n in SMALL:
        w[n] = local_block(given, n).reshape(SMALL_SHAPE[n])

    def pair_start(g_pack, tag):
        return split_start("reduce_pair_start_" + tag, _pair_copies, N_CHIPS, g_pack, (N_CHIPS,) + g_pack.shape[2:])

    def pair_finish(state, after, tag):
        g_pack, got = split_wait("reduce_pair_wait_" + tag, _pair_copies, state, after)
        return pair_sum(g_pack, got, tag, tm=got.shape[1] // 2)

    pack_early = lambda g: jnp.stack([jnp.stack([_pack_half(EARLY, lambda n: _grad_row_form(n, g[n], j), h, F32)
                                                 for h in range(2)]) for j in range(N_CHIPS)])
    loss, grad_x, grads, (late_part, late_slots), early_state = local_step(
        x[0], loss_target[0], w, late_token, late_weights, pair_start, pair_finish, pack_early)

    early_part, early_part16 = pair_finish(early_state, grad_x, "early")
    s_pack = _pack_rows([grads[n] for n in SMALL] + [grads["shift_b"], loss.reshape(1, 1)], SMALL_ROWS, F32)
    chips_state, token = split_start("reduce_chips_start", _chip_copies, 3, early_part16, early_part16.shape)
    out_g, out_d, out_m, out_v = {}, {}, {}, {}

    def finish(group, tag, part, slots):
        half_sum = sum_with_own(part, slots, chip, token, tm=part.shape[1] // 2, name="chip_sum_" + tag)
        other_half = exchange_halves(half_sum, tag)
        for n in group:
            res = adamw_weight(n, half_sum, other_half,
                               *[_row_form(n, local_block(t, n)) for t in (given, mom_m, mom_v)])
            for tree, z in zip((out_g, out_d, out_m, out_v), res):
                tree[n] = _row_form(n, z)

    finish(LATE, "late", late_part, late_slots)

    small_shapes = [SMALL_SHAPE[n] for n in SMALL]
    g_small = sum_all(s_pack, token)
    *g_parts, loss = _unpack_rows(g_small, small_shapes + [(2, N_RWKV), ()])
    out_g.update(zip(SMALL, g_parts[:-1]))
    g_sb = lax.dynamic_slice_in_dim(g_parts[-1], chip * SHIFT_SHARD[1], SHIFT_SHARD[1], axis=1)
    out_g["shift_b"] = g_sb
    names = SMALL + ["shift_b"]
    native = lambda tree: [local_block(tree, n).reshape(SMALL_SHAPE.get(n, SHIFT_SHARD)) for n in names]
    small_res = adamw_many(g_parts[:-1] + [g_sb], native(given), native(mom_m), native(mom_v))
    for tree, res in zip((out_d, out_m, out_v), small_res):
        tree.update(zip(names, res))

    after = (out_v["w_out"], out_v["sgu_w"])
    early_slots = split_wait("reduce_chips_wait", _chip_copies, chips_state,
                             jnp.concatenate([z.reshape(-1)[:8] for z in after]))[1]
    finish(EARLY, "early", early_part, early_slots)

    def block_of(tree, n):
        return tree[n].reshape(given[n].shape)

    return (loss, grad_x[None], *[block_of(out_g, n) for n in WEIGHTS], *[block_of(out_d, n) for n in WEIGHTS],
            *[block_of(out_m, n) for n in WEIGHTS], *[block_of(out_v, n) for n in WEIGHTS])
```

```python
import functools

import jax
import jax.numpy as jnp
from jax import lax
from jax.experimental import pallas as pl
from jax.experimental.pallas import tpu as pltpu

F32 = jnp.float32
BF16 = jnp.bfloat16

D_MODEL = 1024
N_HEADS = 16
HEAD = 64
SCAN_CHUNK = 64

VMEM_LIMIT = 56 * 1024 * 1024


_BDIMS = {
    "nn": (((2,), (1,)), ((0,), (0,))),
    "nt": (((2,), (2,)), ((0,), (0,))),
    "tn": (((1,), (1,)), ((0,), (0,))),
}


def _raw_bdot(x, y, mode, fine):
    if fine:
        return lax.dot_general(x, y, _BDIMS[mode], precision=lax.Precision.HIGH, preferred_element_type=F32)
    return lax.dot_general(x.astype(BF16), y.astype(BF16), _BDIMS[mode], preferred_element_type=F32)


@functools.partial(jax.custom_vjp, nondiff_argnums=(2, 3))
def bdot(x, y, mode, fine=True):
    return _raw_bdot(x, y, mode, fine)


def _bdot_fwd(x, y, mode, fine):
    return _raw_bdot(x, y, mode, fine), (x, y)


def _bdot_bwd(mode, fine, res, g):
    x, y = res
    if mode == "nn":
        return bdot(g, y, "nt", fine), bdot(x, g, "tn", fine)
    if mode == "nt":
        return bdot(g, y, "nn", fine), bdot(g, x, "tn", fine)
    return bdot(y, g, "nt", fine), bdot(x, g, "nn", fine)


bdot.defvjp(_bdot_fwd, _bdot_bwd)


def _scan_chunk(S0, r, lw, k, v, a, b):
    nh, lc, _ = r.shape
    ti = lax.broadcasted_iota(jnp.int32, (lc, lc), 0)
    si = lax.broadcasted_iota(jnp.int32, (lc, lc), 1)
    incl = (si <= ti).astype(F32)
    strict = (si < ti).astype(F32)
    eye = (si == ti).astype(F32)
    cl = bdot(jnp.broadcast_to(incl, (nh, lc, lc)), lw, "nn")
    cl_last = cl[:, lc - 1:lc, :]
    g_last = jnp.exp(cl_last - cl)
    at = a * jnp.exp(cl - lw)
    bt = b * jnp.exp(-cl)
    kt = k * jnp.exp(-cl)
    rt = r * jnp.exp(cl)
    ar = jnp.concatenate([at, rt], axis=1)
    ar_b = bdot(ar, bt, "nt", False)
    ar_k = bdot(ar, kt, "nt", False)
    m_ab, m_rb = ar_b[:, :lc] * strict, ar_b[:, lc:] * incl
    m_ak, m_rk = ar_k[:, :lc] * strict, ar_k[:, lc:] * incl
    x = eye + m_ab
    p = bdot(m_ab, m_ab, "nn", False)
    n = 2
    while n * 2 < lc:
        px = bdot(jnp.concatenate([p, x], axis=1), p, "nn", False)
        p = px[:, :lc]
        x = x + px[:, lc:]
        n *= 2
    x = x + bdot(x, p, "nn", False)
    ar_s = bdot(ar, S0, "nt", False)
    akrk_v = bdot(jnp.concatenate([m_ak, m_rk], axis=1), v, "nn", False)
    u = bdot(x, ar_s[:, :lc] + akrk_v[:, :lc], "nn", False)
    o = ar_s[:, lc:] + bdot(m_rb, u, "nn", False) + akrk_v[:, lc:]
    s_last = S0 * jnp.exp(cl_last) + bdot(jnp.concatenate([u, v], axis=1),
                                          jnp.concatenate([b * g_last, k * g_last], axis=1), "tn", False)
    return o, s_last


def _split_heads(z):
    return jnp.stack([z[:, HEAD * h:HEAD * (h + 1)] for h in range(N_HEADS)], axis=0)


def _merge_heads(z):
    return jnp.concatenate([z[h] for h in range(N_HEADS)], axis=1)


def _scan_specs(t, ops, rev):
    nc = t // SCAN_CHUNK
    row = (lambda c: nc - 1 - c) if rev else (lambda c: c)
    specs = [pl.BlockSpec((SCAN_CHUNK, D_MODEL), lambda c, cb=cb: (row(c), cb)) for _, cb in ops]
    state = pl.BlockSpec((1, N_HEADS, HEAD, HEAD), lambda c: (row(c), 0, 0, 0))
    return nc, specs, state


def scan_fwd(ops):
    t = ops[0][0].shape[0]
    nc, specs, state = _scan_specs(t, ops, False)

    def body(r_ref, lw_ref, k_ref, v_ref, a_ref, b_ref, o_ref, s0_ref, s_scr):
        @pl.when(pl.program_id(0) == 0)
        def _():
            s_scr[...] = jnp.zeros_like(s_scr)

        s0 = s_scr[...]
        s0_ref[0] = s0
        o, s_last = _scan_chunk(s0, *[_split_heads(z[...]) for z in (r_ref, lw_ref, k_ref, v_ref, a_ref, b_ref)])
        o_ref[...] = _merge_heads(o)
        s_scr[...] = s_last

    return pl.pallas_call(
        body,
        name="scan_fwd",
        grid=(nc,),
        in_specs=specs,
        out_specs=[pl.BlockSpec((SCAN_CHUNK, D_MODEL), lambda c: (c, 0)), state],
        out_shape=[jax.ShapeDtypeStruct((t, D_MODEL), F32), jax.ShapeDtypeStruct((nc, N_HEADS, HEAD, HEAD), F32)],
        scratch_shapes=[pltpu.VMEM((N_HEADS, HEAD, HEAD), F32)],
        compiler_params=_cparams(1),
    )(*[a for a, _ in ops])


def scan_bwd(ops, s0s, do, part):
    t = ops[0][0].shape[0]
    nc, specs, state = _scan_specs(t, ops + [(do, 0)], True)

    def body(r_ref, lw_ref, k_ref, v_ref, a_ref, b_ref, do_ref, s0_ref, part_ref, *rest):
        out_refs, slots_ref, ds_scr, send_sems, recv_sems = rest[:6], rest[6], rest[7], rest[8], rest[9]
        step = pl.program_id(0)
        x, y, c = _coords()
        me = 2 * x + y
        chips = _other_chips(x, y)
        sends = [_remote(part_ref.at[2 * cx + cy], slots_ref.at[me], send_sems, recv_sems, k, (cx, cy, c))
                 for k, (cx, cy) in enumerate(chips)]

        @pl.when(step == 0)
        def _():
            ds_scr[...] = jnp.zeros_like(ds_scr)
            for cp in sends:
                cp.start()

        _, vjp = jax.vjp(_scan_chunk, s0_ref[0],
                         *[_split_heads(z[...]) for z in (r_ref, lw_ref, k_ref, v_ref, a_ref, b_ref)])
        grads = vjp((_split_heads(do_ref[...]), ds_scr[...]))
        for o_ref, g in zip(out_refs, grads[1:]):
            o_ref[...] = _merge_heads(g)
        ds_scr[...] = grads[0]

        @pl.when(step == nc - 1)
        def _():
            for k, (cx, cy) in enumerate(chips):
                _remote(part_ref.at[me], slots_ref.at[2 * cx + cy], send_sems, recv_sems, k, (cx, cy, c)).wait_recv()
            for cp in sends:
                cp.wait_send()

    return pl.pallas_call(
        body,
        name="scan_bwd",
        grid=(nc,),
        in_specs=specs + [state, _ANY],
        out_specs=[pl.BlockSpec((SCAN_CHUNK, D_MODEL), lambda c: (nc - 1 - c, 0))] * 6 + [_ANY],
        out_shape=[jax.ShapeDtypeStruct((t, D_MODEL), F32)] * 6 + [jax.ShapeDtypeStruct(part.shape, part.dtype)],
        scratch_shapes=[pltpu.VMEM((N_HEADS, HEAD, HEAD), F32), pltpu.SemaphoreType.DMA((3,)),
                        pltpu.SemaphoreType.DMA((3,))],
        compiler_params=_cparams(1),
    )(*[a for a, _ in ops], do, s0s, part)


_MDIMS = {
    "nn": (((1,), (0,)), ((), ())),
    "nt": (((1,), (1,)), ((), ())),
    "tn": (((0,), (0,)), ((), ())),
}


def _raw_mdot(x, y, mode, exact):
    if exact:
        return lax.dot_general(x, y, _MDIMS[mode], precision=lax.Precision.HIGH, preferred_element_type=F32)
    return lax.dot_general(x.astype(BF16), y.astype(BF16), _MDIMS[mode], preferred_element_type=F32)


@functools.partial(jax.custom_vjp, nondiff_argnums=(2, 3))
def mdot(x, y, mode, exact):
    return _raw_mdot(x, y, mode, exact)


def _mdot_fwd(x, y, mode, exact):
    return _raw_mdot(x, y, mode, exact), (x, y)


def _mdot_bwd(mode, exact, res, g):
    x, y = res
    if mode == "nn":
        return mdot(g, y, "nt", exact), mdot(x, g, "tn", exact)
    if mode == "nt":
        return mdot(g, y, "nn", exact), mdot(g, x, "tn", exact)
    return mdot(y, g, "nt", exact), mdot(x, g, "nn", exact)


mdot.defvjp(_mdot_fwd, _mdot_bwd)


def _seg_ones():
    i = lax.broadcasted_iota(jnp.int32, (256, 256), 0) // HEAD
    j = lax.broadcasted_iota(jnp.int32, (256, 256), 1) // HEAD
    return (i == j).astype(BF16)


@jax.custom_vjp
def segsum(x):
    bd = _seg_ones()
    hi = x.astype(BF16)
    lo = (x - hi.astype(F32)).astype(BF16)
    cols = []
    for j in range(x.shape[1] // 256):
        sl = slice(256 * j, 256 * (j + 1))
        cols.append(jnp.dot(hi[:, sl], bd, preferred_element_type=F32)
                    + jnp.dot(lo[:, sl], bd, preferred_element_type=F32))
    return jnp.concatenate(cols, axis=1)


segsum.defvjp(lambda x: (segsum(x), None), lambda _, g: (segsum(g),))


NORM_EPS = 1e-6
LN_EPS = 1e-5
GN_EPS = 64e-5
SGU_CHUNK = 128
SGU_GROUPS = 8


def _rms(x, g):
    return x * lax.rsqrt(jnp.mean(x * x, axis=-1, keepdims=True) + NORM_EPS) * g


def f_norm_in(x, g):
    return _rms(x, g), x


def f_sgu(p, ln_w, ln_b, sw, sbt):
    tm = p.shape[0]
    z = 0.5 * p * (1.0 + lax.erf(p * 0.7071067811865476))
    u, v = z[:, :D_MODEL], z[:, D_MODEL:]
    mu = jnp.mean(v, axis=-1, keepdims=True)
    d = v - mu
    vn = d * lax.rsqrt(jnp.mean(d * d, axis=-1, keepdims=True) + LN_EPS) * ln_w + ln_b
    ii = lax.broadcasted_iota(jnp.int32, (SGU_CHUNK, SGU_CHUNK), 0)
    jj = lax.broadcasted_iota(jnp.int32, (SGU_CHUNK, SGU_CHUNK), 1)
    mask = (jj <= ii).astype(F32)
    gi = lax.broadcasted_iota(jnp.int32, (SGU_GROUPS, D_MODEL), 0)
    ci = lax.broadcasted_iota(jnp.int32, (SGU_GROUPS, D_MODEL), 1) // SGU_CHUNK
    bias = mdot(sbt, (gi == ci).astype(F32), "nn", True)
    rows = []
    for c in range(tm // SGU_CHUNK):
        cols = []
        for g in range(SGU_GROUPS):
            blk = vn[c * SGU_CHUNK:(c + 1) * SGU_CHUNK, g * SGU_CHUNK:(g + 1) * SGU_CHUNK]
            cols.append(mdot(sw[g] * mask, blk, "nn", False))
        rows.append(jnp.concatenate(cols, axis=1) + bias)
    return (u * jnp.concatenate(rows, axis=0),)


def _softplus(x):
    return jnp.maximum(x, 0.0) + jnp.log1p(jnp.exp(-jnp.abs(x)))


def f_pre(q, wl, w0, al, a0, gl, k_k, k_a):
    qr, qk, qv, ql = q[:, :1024], q[:, 1024:2048], q[:, 2048:3072], q[:, 3072:]
    return _f_pre(qr, qk, qv, ql, wl, w0, al, a0, gl, k_k, k_a)


def _f_pre(qr, qk, qv, ql, wl, w0, al, a0, gl, k_k, k_a):
    xw, xa, xg = ql[:, :128], ql[:, 128:256], ql[:, 256:512]
    wr = -_softplus(-(w0 + mdot(jnp.tanh(xw), wl, "nn", False))) - 0.5
    lw = -jnp.exp(wr)
    aa = jax.nn.sigmoid(a0 + mdot(xa, al, "nn", False))
    g = mdot(jax.nn.sigmoid(xg), gl, "nn", False)
    kkr = qk * k_k
    kk = kkr / jnp.maximum(jnp.sqrt(segsum(kkr * kkr)), 1e-12)
    kp = qk * (1.0 + (aa - 1.0) * k_a)
    return qr, lw, kp, qv, -kk, kk * aa, g, qr, kp, qv


def f_post(o, r, kp, v, g, lnw, lnb, rk):
    mu = segsum(o) * (1.0 / HEAD)
    d = o - mu
    gn = d * lax.rsqrt(segsum(d * d) * (1.0 / HEAD) + GN_EPS)
    return ((gn * lnw + lnb + segsum(r * kp * rk) * v) * g,)


def f_mix(ya, yb, ga, gb):
    return (jax.nn.sigmoid(ga) * ya + jax.nn.sigmoid(gb) * yb,)


def f_ffn_in(h1, g):
    return _rms(h1, g), h1


def f_final(h1, m3, tgt, g):
    y = _rms(h1 + m3, g)
    err = jnp.square(y - tgt)
    return 0.5 * jnp.sum(jnp.mean(err, axis=-1))


def _cparams(n_grid):
    return pltpu.CompilerParams(dimension_semantics=("arbitrary",) * n_grid, vmem_limit_bytes=VMEM_LIMIT)


def _tile_spec(tm, w, cb):
    return pl.BlockSpec((tm, w), lambda i: (i, cb))


def _const_spec(c):
    nd = c.ndim
    return pl.BlockSpec(c.shape, lambda i: (0,) * nd)


def ew_call(fn, tiled, consts, outs, *, tm, name):
    t = tiled[0][0].shape[0]
    n_t, n_c = len(tiled), len(consts)

    def body(*refs):
        tv = [r[...].astype(F32) for r in refs[:n_t]]
        cv = [r[...] for r in refs[n_t:n_t + n_c]]
        res = fn(*tv, *cv)
        for o_ref, val in zip(refs[n_t + n_c:], res):
            o_ref[...] = val.astype(o_ref.dtype)

    return pl.pallas_call(
        body,
        name=name,
        grid=(t // tm,),
        in_specs=[_tile_spec(tm, w, cb) for _, w, cb in tiled] + [_const_spec(c) for c in consts],
        out_specs=[_tile_spec(tm, w, 0) for w, _ in outs],
        out_shape=[jax.ShapeDtypeStruct((t, w), dt) for w, dt in outs],
        compiler_params=_cparams(1),
    )(*[a for a, _, _ in tiled], *consts)


def ew_vjp_call(fn, tiled, consts, cots, d_tiled, d_consts, *, tm, name):
    t = tiled[0][0].shape[0]
    n_t, n_c, n_g = len(tiled), len(consts), len(cots)
    dt_list = [(i, dt) for i, dts in enumerate(d_tiled) for dt in dts]
    dc_list = [i for i, want in enumerate(d_consts) if want]

    def body(*refs):
        tv = [r[...].astype(F32) for r in refs[:n_t]]
        cv = [r[...] for r in refs[n_t:n_t + n_c]]
        gv = tuple(r[...].astype(F32) for r in refs[n_t + n_c:n_t + n_c + n_g])
        out_refs = refs[n_t + n_c + n_g:]
        _, vjp = jax.vjp(fn, *tv, *cv)
        grads = vjp(gv)
        for o_ref, (i, _) in zip(out_refs, dt_list):
            o_ref[...] = grads[i].astype(o_ref.dtype)
        acc_refs = out_refs[len(dt_list):]

        @pl.when(pl.program_id(0) == 0)
        def _():
            for a_ref in acc_refs:
                a_ref[...] = jnp.zeros_like(a_ref)

        for a_ref, i in zip(acc_refs, dc_list):
            a_ref[...] += grads[n_t + i]

    res = pl.pallas_call(
        body,
        name=name,
        grid=(t // tm,),
        in_specs=[_tile_spec(tm, w, cb) for _, w, cb in tiled] + [_const_spec(c) for c in consts]
        + [_tile_spec(tm, w, cb) for _, w, cb in cots],
        out_specs=[_tile_spec(tm, tiled[i][1], 0) for i, _ in dt_list] + [_const_spec(consts[i]) for i in dc_list],
        out_shape=[jax.ShapeDtypeStruct((t, tiled[i][1]), dt) for i, dt in dt_list]
        + [jax.ShapeDtypeStruct(consts[i].shape, F32) for i in dc_list],
        compiler_params=_cparams(1),
    )(*[a for a, _, _ in tiled], *consts, *[a for a, _, _ in cots])
    return res[:len(dt_list)], res[len(dt_list):]


def mm(a, b, mode, *, tm, tn, name, out_dtypes=(F32,), epi=None, extras=(), into=None):
    m = a.shape[1] if mode == "tn" else a.shape[0]
    kd = a.shape[0] if mode == "tn" else a.shape[1]
    n = b.shape[0] if mode == "nt" else b.shape[1]
    tm, tn = min(tm, m), min(tn, n)
    if mode == "nn":
        a_spec = pl.BlockSpec((tm, kd), lambda i, j: (i, 0))
        b_spec = pl.BlockSpec((kd, tn), lambda i, j: (0, j))
    elif mode == "nt":
        a_spec = pl.BlockSpec((tm, kd), lambda i, j: (i, 0))
        b_spec = pl.BlockSpec((tn, kd), lambda i, j: (j, 0))
    else:
        a_spec = pl.BlockSpec((kd, tm), lambda i, j: (0, i))
        b_spec = pl.BlockSpec((kd, tn), lambda i, j: (0, j))
    n_e = len(extras)
    o_spec = pl.BlockSpec((tm, tn), lambda i, j: (i, j))

    if into is not None:
        buf, place = into

        def body_into(a_ref, b_ref, buf_ref, o_ref):
            o_ref[0, 0] = lax.dot_general(a_ref[...].astype(BF16), b_ref[...].astype(BF16), _MDIMS[mode],
                                          preferred_element_type=F32)

        return pl.pallas_call(
            body_into,
            name=name,
            grid=(m // tm, n // tn),
            in_specs=[a_spec, b_spec, pl.BlockSpec(memory_space=pl.ANY)],
            out_specs=pl.BlockSpec((1, 1, tm, tn), lambda i, j: (*place(i, j), 0)),
            out_shape=jax.ShapeDtypeStruct(buf.shape, F32),
            input_output_aliases={2: 0},
            compiler_params=_cparams(2),
        )(a, b, buf)

    def body(a_ref, b_ref, *refs):
        c = lax.dot_general(a_ref[...].astype(BF16), b_ref[...].astype(BF16), _MDIMS[mode],
                            preferred_element_type=F32)
        res = epi(c, *[r[...] for r in refs[:n_e]]) if epi is not None else (c,)
        for o_ref, val in zip(refs[n_e:], res):
            o_ref[...] = val.astype(o_ref.dtype)

    res = pl.pallas_call(
        body,
        name=name,
        grid=(m // tm, n // tn),
        in_specs=[a_spec, b_spec] + [o_spec] * n_e,
        out_specs=[o_spec] * len(out_dtypes),
        out_shape=[jax.ShapeDtypeStruct((m, n), dt) for dt in out_dtypes],
        compiler_params=_cparams(2),
    )(a, b, *extras)
    return res if len(out_dtypes) > 1 else res[0]


FFN_BLOCK = 1024


def mm_packed(a, packs, wname, mode, *, tm, name, out_dtypes=(F32,), epi=None, extras=()):
    got, own = packs
    m, kd = a.shape
    n_chips = got.shape[0]
    r0 = PIECE_OFF[wname] // FFN_BLOCK
    assert PIECE_ROWS[wname] == FFN_BLOCK and PIECE_OFF[wname] % FFN_BLOCK == 0
    by_cols = SHARD_AXIS[wname] == 1
    pos = lambda j, h: (0, FFN_BLOCK * j + HALF_W * h) if by_cols else (FFN_BLOCK * j, HALF_W * h)
    w_rows, w_cols = (FFN_BLOCK, n_chips * FFN_BLOCK) if by_cols else (n_chips * FFN_BLOCK, 2 * HALF_W)
    n = w_cols if mode == "nn" else w_rows
    assert kd == (w_rows if mode == "nn" else w_cols)
    pieces = [(j, h) for j in range(n_chips) for h in range(2)]
    out_col = lambda j, h: pos(j, h)[1] if mode == "nn" else pos(j, h)[0]
    n_e = len(extras)

    def body(a_ref, *refs):
        got_refs, own_refs = refs[:len(pieces)], refs[len(pieces):len(pieces) + 2]
        ex_refs, o_refs = refs[len(pieces) + 2:][:n_e], refs[len(pieces) + 2 + n_e:]
        chip = 2 * lax.axis_index("x") + lax.axis_index("y")
        for col in sorted({out_col(j, h) for j, h in pieces}):
            c = None
            for idx, (j, h) in enumerate(pieces):
                if out_col(j, h) != col:
                    continue
                blk = jnp.where(chip == j, own_refs[h][0], got_refs[idx][0, 0])
                r_lo, c_lo = pos(j, h)
                if mode == "nn":
                    part = lax.dot_general(a_ref[:, r_lo:r_lo + FFN_BLOCK], blk, _MDIMS["nn"],
                                           preferred_element_type=F32)
                else:
                    part = lax.dot_general(a_ref[:, c_lo:c_lo + HALF_W], blk, _MDIMS["nt"],
                                           preferred_element_type=F32)
                c = part if c is None else c + part
            width = c.shape[1]
            res = epi(c, *[r[:, col:col + width] for r in ex_refs]) if epi is not None else (c,)
            for o_ref, val in zip(o_refs, res):
                o_ref[:, col:col + width] = val.astype(o_ref.dtype)

    row_spec = lambda width: pl.BlockSpec((tm, width), lambda i: (i, 0))
    got_specs = [pl.BlockSpec((1, 1, FFN_BLOCK, HALF_W), lambda i, j=j, h=h: (j, h, r0, 0)) for j, h in pieces]
    own_specs = [pl.BlockSpec((1, FFN_BLOCK, HALF_W), lambda i, h=h: (h, r0, 0)) for h in range(2)]
    res = pl.pallas_call(
        body,
        name=name,
        grid=(m // tm,),
        in_specs=[row_spec(kd)] + got_specs + own_specs + [row_spec(n)] * n_e,
        out_specs=[row_spec(n)] * len(out_dtypes),
        out_shape=[jax.ShapeDtypeStruct((m, n), dt) for dt in out_dtypes],
        compiler_params=_cparams(1),
    )(a, *([got] * len(pieces)), own, own, *extras)
    return res if len(out_dtypes) > 1 else res[0]


RWKV_COL0 = 4096
RWKV_WIDTH = 3584
SHIFT_BLK = 512


def _shift_down(p, prev_row):
    rows = lax.broadcasted_iota(jnp.int32, p.shape, 0)
    return jnp.where(rows == 0, prev_row, pltpu.roll(p, 1, 0))


def shiftmix_fwd(p_all, sbp, *, tm):
    t = p_all.shape[0]
    tm = min(tm, t)
    c0 = RWKV_COL0 // SHIFT_BLK
    hb = tm // 8

    def body(p_ref, halo_ref, sb_ref, q_ref):
        p = p_ref[...]
        prev = jnp.where(pl.program_id(0) == 0, 0.0, halo_ref[7:8, :])
        q_ref[...] = p * sb_ref[0:1, :] + _shift_down(p, prev) * sb_ref[1:2, :]

    return pl.pallas_call(
        body,
        name="shiftmix_fwd",
        grid=(t // tm, RWKV_WIDTH // SHIFT_BLK),
        in_specs=[
            pl.BlockSpec((tm, SHIFT_BLK), lambda i, j: (i, c0 + j)),
            pl.BlockSpec((8, SHIFT_BLK), lambda i, j: (jnp.maximum(i * hb - 1, 0), c0 + j)),
            pl.BlockSpec((2, SHIFT_BLK), lambda i, j: (0, j)),
        ],
        out_specs=pl.BlockSpec((tm, SHIFT_BLK), lambda i, j: (i, j)),
        out_shape=jax.ShapeDtypeStruct((t, RWKV_WIDTH), F32),
        compiler_params=_cparams(2),
    )(p_all, p_all, sbp)


def shiftmix_bwd(dq, col0, p_all, sbp, *, tm, name):
    t, w = dq.shape
    n_i = t // tm
    hb = tm // 8
    cq = col0 // SHIFT_BLK
    cp = (RWKV_COL0 + col0) // SHIFT_BLK

    def body(dq_ref, dqn_ref, p_ref, ph_ref, sb_ref, dp_ref, dsb_ref):
        i = pl.program_id(1)
        dq_t = dq_ref[...]
        rows = lax.broadcasted_iota(jnp.int32, dq_t.shape, 0)
        nxt = jnp.where(i == n_i - 1, 0.0, dqn_ref[0:1, :])
        up = jnp.where(rows == tm - 1, nxt, pltpu.roll(dq_t, tm - 1, 0))
        dp_ref[...] = (dq_t * sb_ref[0:1, :] + up * sb_ref[1:2, :]).astype(dp_ref.dtype)
        p = p_ref[...]
        prev = jnp.where(i == 0, 0.0, ph_ref[7:8, :])
        s0 = jnp.sum(dq_t * p, axis=0, keepdims=True)
        s1 = jnp.sum(dq_t * _shift_down(p, prev), axis=0, keepdims=True)
        two = lax.broadcasted_iota(jnp.int32, (2, SHIFT_BLK), 0)

        @pl.when(i == 0)
        def _():
            dsb_ref[...] = jnp.zeros_like(dsb_ref)

        dsb_ref[...] += jnp.where(two == 0, s0, s1)

    return pl.pallas_call(
        body,
        name=name,
        grid=(w // SHIFT_BLK, n_i),
        in_specs=[
            pl.BlockSpec((tm, SHIFT_BLK), lambda j, i: (i, j)),
            pl.BlockSpec((8, SHIFT_BLK), lambda j, i: (jnp.minimum((i + 1) * hb, t // 8 - 1), j)),
            pl.BlockSpec((tm, SHIFT_BLK), lambda j, i: (i, cp + j)),
            pl.BlockSpec((8, SHIFT_BLK), lambda j, i: (jnp.maximum(i * hb - 1, 0), cp + j)),
            pl.BlockSpec((2, SHIFT_BLK), lambda j, i: (0, cq + j)),
        ],
        out_specs=[
            pl.BlockSpec((tm, SHIFT_BLK), lambda j, i: (i, j)),
            pl.BlockSpec((2, SHIFT_BLK), lambda j, i: (0, j)),
        ],
        out_shape=[jax.ShapeDtypeStruct((t, w), BF16), jax.ShapeDtypeStruct((2, w), F32)],
        compiler_params=_cparams(2),
    )(dq, dq, p_all, p_all, sbp)


def final_call(h1, m3, tgt, g_final, *, tm):
    t = h1.shape[0]

    def body(h1_ref, m3_ref, tgt_ref, g_ref, dh_ref, dhb_ref, dg_ref, loss_ref):
        loss, vjp = jax.vjp(f_final, h1_ref[...], m3_ref[...], tgt_ref[...], g_ref[...])
        dh, _, _, dg = vjp(jnp.ones((), F32))
        dh_ref[...] = dh
        dhb_ref[...] = dh.astype(BF16)

        @pl.when(pl.program_id(0) == 0)
        def _():
            dg_ref[...] = jnp.zeros_like(dg_ref)
            loss_ref[...] = jnp.zeros_like(loss_ref)

        dg_ref[...] += dg
        loss_ref[...] += jnp.full(loss_ref.shape, loss, F32)

    tile = _tile_spec(tm, D_MODEL, 0)
    return pl.pallas_call(
        body,
        name="final_loss",
        grid=(t // tm,),
        in_specs=[tile, tile, tile, _const_spec(g_final)],
        out_specs=[tile, tile, _const_spec(g_final), pl.BlockSpec((8, 128), lambda i: (0, 0))],
        out_shape=[jax.ShapeDtypeStruct((t, D_MODEL), F32), jax.ShapeDtypeStruct((t, D_MODEL), BF16),
                   jax.ShapeDtypeStruct(g_final.shape, F32), jax.ShapeDtypeStruct((8, 128), F32)],
        compiler_params=_cparams(1),
    )(h1, m3, tgt, g_final)


N_SGU = 2048
N_RWKV = 3360
LORA_W, LORA_A, LORA_G = 64, 64, 160


def _pad_rwkv_cols(z):
    zero = lambda n: jnp.zeros(z.shape[:-1] + (n,), z.dtype)
    return jnp.concatenate([z[..., :3072], z[..., 3072:3136], zero(64), z[..., 3136:3200], zero(64),
                            z[..., 3200:3360], zero(96)], axis=-1)


def _unpad_rwkv_cols(z):
    return jnp.concatenate([z[..., :3072], z[..., 3072:3136], z[..., 3200:3264], z[..., 3328:3488]], axis=-1)


def _pad_win_rows(wt):
    z = wt[N_SGU:N_SGU + N_RWKV]
    zero = lambda n: jnp.zeros((n, wt.shape[1]), wt.dtype)
    return jnp.concatenate([wt[:N_SGU], wt[N_SGU + N_RWKV:], z[:3072], z[3072:3136], zero(64), z[3136:3200], zero(64),
                            z[3200:3360], zero(96)], axis=0)


def _unpad_win_rows(wt):
    z = wt[RWKV_COL0:]
    return jnp.concatenate([wt[:N_SGU], z[:3072], z[3072:3136], z[3200:3264], z[3328:3488], wt[N_SGU:RWKV_COL0]],
                           axis=0)


def _pad_rows(w, n):
    return jnp.concatenate([w, jnp.zeros((n - w.shape[0],) + w.shape[1:], w.dtype)], axis=0)


def _relu2_epi(c):
    return c, jnp.square(jnp.maximum(c, 0.0))


def _relu2_bwd_epi(c, hid):
    return (c * (2.0 * jnp.maximum(hid.astype(F32), 0.0)),)


def _add_epi(c, x):
    return (c + x,)


def _pre_fwd(*args):
    res = f_pre(*args)
    return res[1], res[2], res[4], res[5], res[6]


def local_step(x, tgt, w, late_token, late_weights, pair_start, pair_finish, pack_early):
    d = D_MODEL
    win_pt = _pad_win_rows(w["w_in"])
    sbp = _pad_rwkv_cols(w["shift_b"])
    wl = _pad_rows(w["w_lora_w"], 128)
    al = _pad_rows(w["a_lora_w"], 128)
    gl = _pad_rows(w["g_lora_w"], 256)
    sbt = w["sgu_b"].T

    (a_bf,) = ew_call(lambda x_, g_: (f_norm_in(x_, g_)[0],), [(x, d, 0)], [w["g_mix"] + late_token[:1, :1]],
                      [(d, BF16)], tm=512, name="norm_in")
    p_all = mm(a_bf, win_pt, "nt", tm=2048, tn=1280, name="mm_in")
    sgu_t = [(p_all, 2 * d, 0)]
    sgu_c = [w["sgu_ln_w"], w["sgu_ln_b"], w["sgu_w"], sbt]
    (s_bf,) = ew_call(f_sgu, sgu_t, sgu_c, [(d, BF16)], tm=512, name="sgu_fwd")
    q = shiftmix_fwd(p_all, sbp, tm=2048)
    pre_t = [(q, RWKV_WIDTH, 0)]
    pre_c = [wl, w["w0"], al, w["a0"], gl, w["k_k"], w["k_a"]]
    lw, kp, na, nb, g = ew_call(_pre_fwd, pre_t, pre_c, [(d, F32)] * 5, tm=256, name="rwkv_pre_fwd")
    scan_ops = [(q, 0), (lw, 0), (kp, 0), (q, 2), (na, 0), (nb, 0)]
    o, s0s = scan_fwd(scan_ops)
    w = {**w, **late_weights(o)}
    ya = mm(s_bf, w["w_proj_a"], "nn", tm=1024, tn=1024, name="mm_proj_a")
    post_t = [(o, d, 0), (q, d, 0), (kp, d, 0), (q, d, 2), (g, d, 0)]
    post_c = [w["ln_x_w"], w["ln_x_b"], w["r_k"]]
    (ob_bf,) = ew_call(f_post, post_t, post_c, [(d, BF16)], tm=512, name="rwkv_post_fwd")
    yb = mm(ob_bf, w["w_proj_b"], "nn", tm=1024, tn=1024, name="mm_proj_b")
    mix_t = [(ya, d, 0), (yb, d, 0), (p_all, d, 2), (p_all, d, 3)]
    (mixed_bf,) = ew_call(f_mix, mix_t, [], [(d, BF16)], tm=512, name="mix_fwd")
    h1 = mm(mixed_bf, w["w_out"], "nn", tm=1024, tn=1024, name="mm_out", epi=_add_epi, extras=(x,))
    (f_bf,) = ew_call(lambda h_, g_: (f_ffn_in(h_, g_)[0],), [(h1, d, 0)], [w["g_ffn"]], [(d, BF16)], tm=512,
                      name="ffn_norm")
    hid, act_bf = mm_packed(f_bf, w["late_packs"], "w_ffn1", "nn", tm=512, name="mm_ffn1", out_dtypes=(BF16, BF16),
                            epi=_relu2_epi)
    m3 = mm_packed(act_bf, w["late_packs"], "w_ffn2", "nn", tm=512, name="mm_ffn2")
    dh2, dh2_bf, dg_final, loss = final_call(h1, m3, tgt, w["g_final"], tm=512)

    dhid_bf = mm_packed(dh2_bf, w["late_packs"], "w_ffn2", "nt", tm=512, name="mm_dact", out_dtypes=(BF16,),
                        epi=_relu2_bwd_epi, extras=(hid,))
    late_g = lax.empty((N_CHIPS, 2, pack_rows(LATE), HALF_W), F32)
    late_g = mm(act_bf, dh2_bf, "tn", tm=1024, tn=HALF_W, name="mm_dw_ffn2",
                into=(late_g, lambda i, j: (i, j, PIECE_OFF["w_ffn2"] // 1024)))
    df = mm_packed(dhid_bf, w["late_packs"], "w_ffn1", "nt", tm=512, name="mm_df")
    late_g = mm(f_bf, dhid_bf, "tn", tm=1024, tn=HALF_W, name="mm_dw_ffn1",
                into=(late_g, lambda i, j: (j // 2, j % 2, PIECE_OFF["w_ffn1"] // 1024)))
    (dh1, dh1_bf), (dg_ffn,) = ew_vjp_call(f_ffn_in, [(h1, d, 0)], [w["g_ffn"]], [(df, d, 0), (dh2, d, 0)],
                                           [(F32, BF16)], [True], tm=512, name="ffn_norm_bwd")
    dmixed = mm(dh1_bf, w["w_out"], "nt", tm=1024, tn=1024, name="mm_dmixed")
    late_g = mm(mixed_bf, dh1_bf, "tn", tm=256, tn=HALF_W, name="mm_dw_out",
                into=(late_g, lambda i, j: (i, j, PIECE_OFF["w_out"] // 256)))
    (dya_bf, dyb_bf, dga_bf, dgb_bf), _ = ew_vjp_call(f_mix, mix_t, [], [(dmixed, d, 0)], [(BF16,)] * 4, [], tm=256,
                                                      name="mix_bwd")
    dob = mm(dyb_bf, w["w_proj_b"], "nt", tm=1024, tn=1024, name="mm_dob")
    late_g = mm(ob_bf, dyb_bf, "tn", tm=256, tn=HALF_W, name="mm_dw_proj_b",
                into=(late_g, lambda i, j: (i, j, PIECE_OFF["w_proj_b"] // 256)))
    late_g = mm(s_bf, dya_bf, "tn", tm=256, tn=HALF_W, name="mm_dw_proj_a",
                into=(late_g, lambda i, j: (i, j, PIECE_OFF["w_proj_a"] // 256)))
    late_state, late_token = pair_start(late_g, "late")
    post_c_after = [w["ln_x_w"] + late_token[:1, :1]] + post_c[1:]
    (do, dr_p, dkp_p, dv_p, dg), (dlnx_w, dlnx_b, dr_k) = ew_vjp_call(
        f_post, post_t, post_c_after, [(dob, d, 0)], [(F32,)] * 5, [True] * 3, tm=256, name="rwkv_post_bwd")
    late_part, late_part16 = pair_finish(late_state, do, "late")
    *scan_g, late_slots = scan_bwd(scan_ops, s0s, do, late_part16)
    pre_g = [(z, d, 0) for z in scan_g] + [(dg, d, 0), (dr_p, d, 0), (dkp_p, d, 0), (dv_p, d, 0)]
    (dq,), (dwl, dw0, dal, da0, dgl, dk_k, dk_a) = ew_vjp_call(
        f_pre, pre_t, pre_c, pre_g, [(F32,)], [True] * 7, tm=256, name="rwkv_pre_bwd")
    dp_rwkv, dsb = shiftmix_bwd(dq, 0, p_all, sbp, tm=1024, name="shiftmix_bwd")
    ds = mm(dya_bf, w["w_proj_a"], "nt", tm=1024, tn=1024, name="mm_ds")
    (dp_sgu,), (dln_w, dln_b, dsw, dsbt) = ew_vjp_call(f_sgu, sgu_t, sgu_c, [(ds, d, 0)], [(BF16,)], [True] * 4,
                                                       tm=256, name="sgu_bwd")
    dp_all = jnp.concatenate([dp_sgu, dga_bf, dgb_bf, dp_rwkv], axis=1)
    d_in_pt = mm(dp_all, a_bf, "tn", tm=1280, tn=1024, name="mm_dw_in")
    early_state, early_token = pair_start(pack_early({
        "w_in": _unpad_win_rows(d_in_pt), "w_lora_w": dwl[:LORA_W], "a_lora_w": dal[:LORA_A],
        "g_lora_w": dgl[:LORA_G]}), "early")
    da = mm(dp_all, win_pt, "nn", tm=1024, tn=256, name="mm_da")
    g_mix_after = w["g_mix"] + early_token[:1, :1]
    (grad_x,), (dg_mix,) = ew_vjp_call(f_norm_in, [(x, d, 0)], [g_mix_after], [(da, d, 0), (dh1, d, 0)], [(F32,)],
                                       [True], tm=512, name="norm_in_bwd")

    grads = {
        "g_mix": dg_mix, "sgu_ln_w": dln_w, "sgu_ln_b": dln_b, "sgu_w": dsw, "sgu_b": dsbt.T,
        "shift_b": _unpad_rwkv_cols(dsb),
        "w0": dw0, "a0": da0, "k_k": dk_k, "k_a": dk_a, "r_k": dr_k, "ln_x_w": dlnx_w, "ln_x_b": dlnx_b,
        "g_ffn": dg_ffn, "g_final": dg_final,
    }
    return loss[0, 0], grad_x, grads, (late_part, late_slots), early_state


MESH = pl.DeviceIdType.MESH
N_CHIPS = 4
SMALL_ROWS = 160
_ANY = pl.BlockSpec(memory_space=pl.ANY)


def _coords():
    return lax.axis_index("x"), lax.axis_index("y"), lax.axis_index("c")


def _other_chips(x, y):
    return [(1 - x, y), (x, 1 - y), (1 - x, 1 - y)]


def _remote(src, dst, send_sems, recv_sems, k, to):
    return pltpu.make_async_remote_copy(src_ref=src, dst_ref=dst, send_sem=send_sems.at[k], recv_sem=recv_sems.at[k],
                                        device_id=to, device_id_type=MESH)


def gather_shards(pack):
    def body(src_ref, out_ref, token, send_sems, recv_sems):
        x, y, c = _coords()
        me = 2 * x + y
        sib = (x, y, 1 - c)
        chips = _other_chips(x, y)
        first = [_remote(src_ref.at[c], out_ref.at[me, c], send_sems, recv_sems, k, (cx, cy, c))
                 for k, (cx, cy) in enumerate(chips)]
        for cp in first:
            cp.start()
        passed = []
        for k, (cx, cy) in enumerate(chips):
            j = 2 * cx + cy
            _remote(src_ref.at[c], out_ref.at[j, c], send_sems, recv_sems, k, (cx, cy, c)).wait_recv()
            fwd = _remote(out_ref.at[j, c], out_ref.at[j, c], send_sems, recv_sems, 3 + k, sib)
            fwd.start()
            passed.append(fwd)
        for k, (cx, cy) in enumerate(chips):
            j = 2 * cx + cy
            _remote(out_ref.at[j, 1 - c], out_ref.at[j, 1 - c], send_sems, recv_sems, 3 + k, sib).wait_recv()
        for cp in first + passed:
            cp.wait_send()
        token[...] = jnp.zeros_like(token)

    return pl.pallas_call(
        body,
        name="gather_shards",
        in_specs=[_ANY],
        out_specs=[_ANY, pl.BlockSpec(memory_space=pltpu.VMEM)],
        out_shape=[jax.ShapeDtypeStruct((N_CHIPS,) + pack.shape, pack.dtype), jax.ShapeDtypeStruct((8, 128), F32)],
        scratch_shapes=[pltpu.SemaphoreType.DMA((6,)), pltpu.SemaphoreType.DMA((6,))],
    )(pack)


def _gather_copies(pack_ref, all_ref, send_sems, recv_sems):
    x, y, c = _coords()
    me = 2 * x + y
    return [(_remote(pack_ref.at[c], all_ref.at[me, c], send_sems, recv_sems, k, (cx, cy, c)),
             _remote(pack_ref.at[c], all_ref.at[2 * cx + cy, c], send_sems, recv_sems, k, (cx, cy, c)))
            for k, (cx, cy) in enumerate(_other_chips(x, y))]


_HBM = pl.BlockSpec(memory_space=pltpu.HBM)
_SEM = pl.BlockSpec(memory_space=pltpu.SEMAPHORE)
_SIDE_EFFECT = pltpu.SideEffectType.DATAFLOW_SIDE_EFFECTING


def split_start(name, copies, n, src, land_shape, after=None):
    def body(src_ref, land_ref, *refs):
        send_sems, recv_sems, token = refs[-5], refs[-4], refs[-1]
        for send, _ in copies(src_ref, land_ref, send_sems, recv_sems):
            send.start()
        token[...] = jnp.zeros_like(token)

    extra = () if after is None else (after,)
    *state, token = pl.pallas_call(
        body,
        name=name,
        out_shape=(pltpu.SemaphoreType.DMA((n,)), pltpu.SemaphoreType.DMA((n,)), pltpu.HBM(src.shape, src.dtype),
                   pltpu.HBM(land_shape, src.dtype), jax.ShapeDtypeStruct((8, 128), F32)),
        in_specs=(_HBM, _HBM) + (pl.BlockSpec(memory_space=pl.ANY),) * len(extra),
        out_specs=(_SEM, _SEM, _HBM, _HBM, pl.BlockSpec(memory_space=pltpu.VMEM)),
        input_output_aliases={0: 2, 1: 3},
        compiler_params=pltpu.CompilerParams(has_side_effects=_SIDE_EFFECT),
    )(pltpu.with_memory_space_constraint(src, pltpu.HBM),
      pltpu.with_memory_space_constraint(lax.empty(land_shape, src.dtype), pltpu.HBM), *extra)
    return state, token


def split_wait(name, copies, state, after):
    send_sems, recv_sems, src, land = state

    def body(src_ref, land_ref, send_sems, recv_sems, after_ref, src_out, land_out):
        for send, arrival in copies(src_ref, land_ref, send_sems, recv_sems):
            send.wait_send()
            arrival.wait_recv()

    return pl.pallas_call(
        body,
        name=name,
        out_shape=(pltpu.HBM(src.shape, src.dtype), pltpu.HBM(land.shape, land.dtype)),
        in_specs=(_HBM, _HBM, _SEM, _SEM, pl.BlockSpec(memory_space=pl.ANY)),
        out_specs=(_HBM, _HBM),
        input_output_aliases={0: 0, 1: 1},
        compiler_params=pltpu.CompilerParams(has_side_effects=_SIDE_EFFECT),
    )(src, land, send_sems, recv_sems, after)


def gather_forward(got):
    def body(got_ref, out_ref, send_sems, recv_sems):
        x, y, c = _coords()
        sib = (x, y, 1 - c)
        slots = [2 * cx + cy for cx, cy in _other_chips(x, y)]
        sends = [_remote(got_ref.at[j, c], out_ref.at[j, c], send_sems, recv_sems, k, sib) for k, j in enumerate(slots)]
        for cp in sends:
            cp.start()
        for k, j in enumerate(slots):
            _remote(got_ref.at[j, 1 - c], out_ref.at[j, 1 - c], send_sems, recv_sems, k, sib).wait_recv()
        for cp in sends:
            cp.wait_send()

    return pl.pallas_call(
        body,
        name="gather_forward",
        in_specs=[_ANY],
        out_specs=_ANY,
        out_shape=jax.ShapeDtypeStruct(got.shape, got.dtype),
        input_output_aliases={0: 0},
        scratch_shapes=[pltpu.SemaphoreType.DMA((3,)), pltpu.SemaphoreType.DMA((3,))],
    )(got)


def pair_sum(g, got, tag, *, tm):
    n, _, rows, width = g.shape

    def body(c_ref, own_ref, got_ref, out_ref, out16_ref):
        total = own_ref[0, 0] + got_ref[0]
        out_ref[0] = total
        out16_ref[0] = total.astype(BF16)

    blk = pl.BlockSpec((1, tm, width), lambda j, i, c_ref: (j, i, 0))
    return pl.pallas_call(
        body,
        name="pair_sum_" + tag,
        grid_spec=pltpu.PrefetchScalarGridSpec(
            num_scalar_prefetch=1,
            grid=(n, rows // tm),
            in_specs=[pl.BlockSpec((1, 1, tm, width), lambda j, i, c_ref: (j, c_ref[0], i, 0)), blk],
            out_specs=[blk, blk],
        ),
        out_shape=[jax.ShapeDtypeStruct(got.shape, F32), jax.ShapeDtypeStruct(got.shape, BF16)],
        compiler_params=_cparams(2),
    )(lax.axis_index("c").reshape(1).astype(jnp.int32), g, got)


def _pair_copies(g_ref, got_ref, send_sems, recv_sems):
    x, y, c = _coords()
    copies = [_remote(g_ref.at[j, 1 - c], got_ref.at[j], send_sems, recv_sems, j, (x, y, 1 - c))
              for j in range(N_CHIPS)]
    return [(cp, cp) for cp in copies]


def _chip_copies(p_ref, slots_ref, send_sems, recv_sems):
    x, y, c = _coords()
    me = 2 * x + y
    return [(_remote(p_ref.at[2 * cx + cy], slots_ref.at[me], send_sems, recv_sems, k, (cx, cy, c)),
             _remote(p_ref.at[me], slots_ref.at[2 * cx + cy], send_sems, recv_sems, k, (cx, cy, c)))
            for k, (cx, cy) in enumerate(_other_chips(x, y))]


def sum_with_own(own, slots, mine, after, *, tm, name):
    n, rows, width = slots.shape

    def body(mine_ref, own_ref, *refs):
        acc = None
        for s in range(n):
            term = jnp.where(mine_ref[0] == s, own_ref[0], refs[s][0].astype(F32))
            acc = term if acc is None else acc + term
        refs[-1][...] = acc

    return pl.pallas_call(
        body,
        name=name,
        grid_spec=pltpu.PrefetchScalarGridSpec(
            num_scalar_prefetch=1,
            grid=(rows // tm,),
            in_specs=[pl.BlockSpec((1, tm, width), lambda i, mine_ref: (mine_ref[0], i, 0))]
            + [pl.BlockSpec((1, tm, width), lambda i, mine_ref, s=s: (s, i, 0)) for s in range(n)]
            + [pl.BlockSpec(after.shape, lambda i, mine_ref: (0,) * after.ndim)],
            out_specs=pl.BlockSpec((tm, width), lambda i, mine_ref: (i, 0)),
        ),
        out_shape=jax.ShapeDtypeStruct((rows, width), F32),
        compiler_params=_cparams(1),
    )(mine.reshape(1).astype(jnp.int32), own, *([slots] * n), after)


def exchange_halves(s, tag):
    nq = 4
    rq = s.shape[0] // nq
    assert rq * nq == s.shape[0] and rq % 8 == 0

    def body(s_ref, out_ref, sbuf, rbuf, send_sems, recv_sems, in_sems, out_sems):
        x, y, c = _coords()
        sib = (x, y, 1 - c)
        rows = lambda q: pl.ds(q * rq, rq)
        loads = [pltpu.make_async_copy(s_ref.at[rows(q)], sbuf.at[rows(q)], in_sems.at[q]) for q in range(nq)]
        for cp in loads:
            cp.start()
        sends = []
        for q in range(nq):
            loads[q].wait()
            sends.append(_remote(sbuf.at[rows(q)], rbuf.at[rows(q)], send_sems, recv_sems, q, sib))
            sends[q].start()
        stores = []
        for q in range(nq):
            sends[q].wait_recv()
            stores.append(pltpu.make_async_copy(rbuf.at[rows(q)], out_ref.at[rows(q)], out_sems.at[q]))
            stores[q].start()
        for cp in sends:
            cp.wait_send()
        for cp in stores:
            cp.wait()

    return pl.pallas_call(
        body,
        name="exchange_halves_" + tag,
        in_specs=[_ANY],
        out_specs=_ANY,
        out_shape=jax.ShapeDtypeStruct(s.shape, s.dtype),
        scratch_shapes=[pltpu.VMEM(s.shape, s.dtype), pltpu.VMEM(s.shape, s.dtype)]
        + [pltpu.SemaphoreType.DMA((nq,))] * 4,
        compiler_params=pltpu.CompilerParams(vmem_limit_bytes=VMEM_LIMIT),
    )(s)


def sum_all(s, after):
    rows = s.shape[0]
    half = rows // 2

    def body(s_ref, after_ref, out_ref, theirs, pair, slots, send_sems, recv_sems):
        x, y, c = _coords()
        me = 2 * x + y
        sib = (x, y, 1 - c)
        chips = _other_chips(x, y)
        swap = _remote(s_ref, theirs, send_sems, recv_sems, 0, sib)
        swap.start()
        swap.wait_recv()
        pair[...] = s_ref[...] + theirs[...]
        mine = pl.ds(pl.multiple_of(c * half, 8), half)
        other = pl.ds(pl.multiple_of((1 - c) * half, 8), half)
        sends = [_remote(pair.at[mine], slots.at[me], send_sems, recv_sems, 1 + k, (cx, cy, c))
                 for k, (cx, cy) in enumerate(chips)]
        for cp in sends:
            cp.start()
        for k, (cx, cy) in enumerate(chips):
            _remote(pair.at[mine], slots.at[2 * cx + cy], send_sems, recv_sems, 1 + k, (cx, cy, c)).wait_recv()
        slots[me] = pair[mine]
        out_ref[mine] = ((slots[0] + slots[1]) + slots[2]) + slots[3]
        last = _remote(out_ref.at[mine], out_ref.at[mine], send_sems, recv_sems, 4, sib)
        last.start()
        _remote(out_ref.at[other], out_ref.at[other], send_sems, recv_sems, 4, sib).wait_recv()
        for cp in [swap] + sends + [last]:
            cp.wait_send()

    vmem = pl.BlockSpec(memory_space=pltpu.VMEM)
    return pl.pallas_call(
        body,
        name="sum_all",
        in_specs=[vmem, vmem],
        out_specs=vmem,
        out_shape=jax.ShapeDtypeStruct(s.shape, s.dtype),
        scratch_shapes=[pltpu.VMEM(s.shape, s.dtype), pltpu.VMEM(s.shape, s.dtype),
                        pltpu.VMEM((N_CHIPS, half, s.shape[1]), s.dtype), pltpu.SemaphoreType.DMA((5,)),
                        pltpu.SemaphoreType.DMA((5,))],
        compiler_params=pltpu.CompilerParams(vmem_limit_bytes=VMEM_LIMIT),
    )(s, after)


ADAM_LR = 0.001
ADAM_B1 = 0.9
ADAM_B2 = 0.999
ADAM_EPS = 1e-08
ADAM_WD = 0.01
ADAM_STEP = 10


def f_adamw(g, w, m, v):
    m = ADAM_B1 * m + (1.0 - ADAM_B1) * g
    v = ADAM_B2 * v + (1.0 - ADAM_B2) * jnp.square(g)
    m_hat = m / (1.0 - ADAM_B1 ** ADAM_STEP)
    v_hat = v / (1.0 - ADAM_B2 ** ADAM_STEP)
    delta = -ADAM_LR * (m_hat / (jnp.sqrt(v_hat) + ADAM_EPS) + ADAM_WD * w)
    return delta, m, v


def adamw_many(gs, ws, ms, vs):
    n = len(gs)

    def body(*refs):
        ins, outs = refs[:4 * n], refs[4 * n:]
        for i in range(n):
            delta, nm, nv = f_adamw(ins[i][...], ins[n + i][...], ins[2 * n + i][...], ins[3 * n + i][...])
            outs[i][...] = delta
            outs[n + i][...] = nm
            outs[2 * n + i][...] = nv

    vmem = pl.BlockSpec(memory_space=pltpu.VMEM)
    res = pl.pallas_call(
        body,
        name="adamw_small",
        in_specs=[vmem] * (4 * n),
        out_specs=[vmem] * (3 * n),
        out_shape=[jax.ShapeDtypeStruct(w.shape, F32) for w in ws] * 3,
    )(*gs, *ws, *ms, *vs)
    return res[:n], res[n:2 * n], res[2 * n:]


EARLY = ["w_in", "w_lora_w", "a_lora_w", "g_lora_w"]
LATE = ["w_ffn1", "w_ffn2", "w_proj_b", "w_out", "w_proj_a"]
LORAS = ["w_lora_w", "a_lora_w", "g_lora_w"]
HALF_W = 512
PIECE_ROWS = {"w_in": 1864, "w_ffn1": 1024, "w_ffn2": 1024, "w_proj_a": 256, "w_proj_b": 256, "w_out": 256,
              "w_lora_w": 32, "a_lora_w": 32, "g_lora_w": 80}
PIECE_OFF = {"w_in": 0, "w_lora_w": 1920, "a_lora_w": 1952, "g_lora_w": 2000,
             "w_ffn1": 0, "w_ffn2": 1024, "w_proj_b": 2048, "w_out": 2304, "w_proj_a": 2560}
LO_OFF = 2080


def pack_rows(group):
    return 2304 if group is EARLY else 2816
SHARD_AXIS = {"w_in": 1, "w_proj_a": 0, "w_lora_w": 1, "a_lora_w": 1, "g_lora_w": 1, "w_proj_b": 0, "w_out": 0,
              "w_ffn1": 1, "w_ffn2": 0}
SHARD_SHAPE = {"w_in": (1024, 1864), "w_proj_a": (256, 1024), "w_lora_w": (64, 256), "a_lora_w": (64, 256),
               "g_lora_w": (160, 256), "w_proj_b": (256, 1024), "w_out": (256, 1024), "w_ffn1": (1024, 1024),
               "w_ffn2": (1024, 1024)}
SHIFT_SHARD = (2, 840)
VECTORS = ["g_mix", "sgu_ln_w", "sgu_ln_b", "w0", "a0", "k_k", "k_a", "r_k", "ln_x_w", "ln_x_b", "g_ffn", "g_final"]
SMALL = VECTORS + ["sgu_w", "sgu_b"]
SMALL_SHAPE = {**{n: (1, 1024) for n in VECTORS}, "sgu_w": (8, 128, 128), "sgu_b": (8, 128)}
WEIGHTS = ["g_mix", "w_in", "sgu_ln_w", "sgu_ln_b", "sgu_w", "sgu_b", "w_proj_a", "shift_b", "w_lora_w", "w0",
           "a_lora_w", "a0", "g_lora_w", "k_k", "k_a", "r_k", "ln_x_w", "ln_x_b", "w_proj_b", "w_out", "g_ffn",
           "w_ffn1", "w_ffn2", "g_final"]


def _size(shape):
    n = 1
    for s in shape:
        n *= s
    return n


def _pack_rows(parts, rows, dtype):
    flat = jnp.concatenate([p.reshape(-1).astype(dtype) for p in parts])
    return jnp.concatenate([flat, jnp.zeros((rows * 1024 - flat.shape[0],), dtype)]).reshape(rows, 1024)


def _unpack_rows(packed, shapes):
    flat = packed.reshape(-1)
    out, off = [], 0
    for shp in shapes:
        out.append(flat[off:off + _size(shp)].reshape(shp))
        off += _size(shp)
    return out


def _shard_of(name, full, j):
    ax = SHARD_AXIS[name]
    n = SHARD_SHAPE[name][ax]
    return lax.slice_in_dim(full, j * n, (j + 1) * n, axis=ax)


def _pad_cols(z, n):
    return jnp.concatenate([z, jnp.zeros((z.shape[0], n - z.shape[1]), z.dtype)], axis=1)


def _row_form(name, s):
    return s.T if name == "w_in" else s


def _half_piece(name, rf, h):
    if name in LORAS:
        r = PIECE_ROWS[name]
        return _pad_cols(rf[h * r:(h + 1) * r], HALF_W)
    return rf[:, HALF_W * h:HALF_W * (h + 1)]


def _pack_half(group, rf_fn, h, dtype, tail=()):
    parts, pos, rows = [], 0, pack_rows(group)
    for n in group:
        if PIECE_OFF[n] > pos:
            parts.append(jnp.zeros((PIECE_OFF[n] - pos, HALF_W), dtype))
        parts.append(_half_piece(n, rf_fn(n), h).astype(dtype))
        pos = PIECE_OFF[n] + PIECE_ROWS[n]
    for t in tail:
        parts.append(t)
        pos += t.shape[0]
    parts.append(jnp.zeros((rows - pos, HALF_W), dtype))
    return jnp.concatenate(parts, axis=0)


def _piece(pack, name):
    return pack[PIECE_OFF[name]:PIECE_OFF[name] + PIECE_ROWS[name]]


def _join_halves(name, p0, p1):
    if name in LORAS:
        return jnp.concatenate([p0[:, :SHARD_SHAPE[name][1]], p1[:, :SHARD_SHAPE[name][1]]], axis=0)
    return jnp.concatenate([p0, p1], axis=1)


def _grad_row_form(name, full, j):
    if name == "w_in":
        return full[SHARD_SHAPE[name][1] * j:SHARD_SHAPE[name][1] * (j + 1)]
    return _shard_of(name, full, j)


def adamw_weight(name, g_own, g_other, w, m, v):
    rows, width = w.shape
    if name in LORAS:
        tm = PIECE_ROWS[name]
        grid = (2, 1)
        native = pl.BlockSpec((tm, width), lambda h, i: (h, 0))
    elif name == "w_in":
        tm, lanes = rows, 256
        grid = (2, HALF_W // lanes)
        native = pl.BlockSpec((tm, lanes), lambda h, i: (0, h * (HALF_W // lanes) + i))
    else:
        tm = rows
        grid = (2, 1)
        native = pl.BlockSpec((tm, HALF_W), lambda h, i: (i, h))
    assert PIECE_OFF[name] % tm == 0
    off = PIECE_OFF[name] // tm
    if name == "w_in":
        packed = pl.BlockSpec((tm, lanes), lambda h, i: (0, i))
    else:
        packed = pl.BlockSpec((tm, HALF_W), lambda h, i: (off + i, 0))

    def body(go_ref, gx_ref, w_ref, m_ref, v_ref, g_ref, d_ref, nm_ref, nv_ref):
        g = jnp.where(pl.program_id(0) == lax.axis_index("c"), go_ref[...], gx_ref[...])[:, :w_ref.shape[1]]
        delta, nm, nv = f_adamw(g, w_ref[...], m_ref[...], v_ref[...])
        g_ref[...] = g
        d_ref[...] = delta
        nm_ref[...] = nm
        nv_ref[...] = nv

    return pl.pallas_call(
        body,
        name="adamw_" + name,
        grid=grid,
        in_specs=[packed, packed, native, native, native],
        out_specs=[native] * 4,
        out_shape=[jax.ShapeDtypeStruct(w.shape, F32)] * 4,
        compiler_params=_cparams(2),
    )(g_own, g_other, w, m, v)


def kernel(x, g_mix, w_in, sgu_ln_w, sgu_ln_b, sgu_w, sgu_b, w_proj_a, shift_b, w_lora_w, w0, a_lora_w, a0, g_lora_w, k_k, k_a, r_k, ln_x_w, ln_x_b, w_proj_b, w_out, g_ffn, w_ffn1, w_ffn2, g_final, loss_target, m_g_mix, m_w_in, m_sgu_ln_w, m_sgu_ln_b, m_sgu_w, m_sgu_b, m_w_proj_a, m_shift_b, m_w_lora_w, m_w0, m_a_lora_w, m_a0, m_g_lora_w, m_k_k, m_k_a, m_r_k, m_ln_x_w, m_ln_x_b, m_w_proj_b, m_w_out, m_g_ffn, m_w_ffn1, m_w_ffn2, m_g_final, v_g_mix, v_w_in, v_sgu_ln_w, v_sgu_ln_b, v_sgu_w, v_sgu_b, v_w_proj_a, v_shift_b, v_w_lora_w, v_w0, v_a_lora_w, v_a0, v_g_lora_w, v_k_k, v_k_a, v_r_k, v_ln_x_w, v_ln_x_b, v_w_proj_b, v_w_out, v_g_ffn, v_w_ffn1, v_w_ffn2, v_g_final):
    given = dict(zip(WEIGHTS, (g_mix, w_in, sgu_ln_w, sgu_ln_b, sgu_w, sgu_b, w_proj_a, shift_b, w_lora_w, w0, a_lora_w, a0, g_lora_w, k_k, k_a, r_k, ln_x_w, ln_x_b, w_proj_b, w_out, g_ffn, w_ffn1, w_ffn2, g_final)))
    mom_m = dict(zip(WEIGHTS, (m_g_mix, m_w_in, m_sgu_ln_w, m_sgu_ln_b, m_sgu_w, m_sgu_b, m_w_proj_a, m_shift_b, m_w_lora_w, m_w0, m_a_lora_w, m_a0, m_g_lora_w, m_k_k, m_k_a, m_r_k, m_ln_x_w, m_ln_x_b, m_w_proj_b, m_w_out, m_g_ffn, m_w_ffn1, m_w_ffn2, m_g_final)))
    mom_v = dict(zip(WEIGHTS, (v_g_mix, v_w_in, v_sgu_ln_w, v_sgu_ln_b, v_sgu_w, v_sgu_b, v_w_proj_a, v_shift_b, v_w_lora_w, v_w0, v_a_lora_w, v_a0, v_g_lora_w, v_k_k, v_k_a, v_r_k, v_ln_x_w, v_ln_x_b, v_w_proj_b, v_w_out, v_g_ffn, v_w_ffn1, v_w_ffn2, v_g_final)))
    chip = 2 * lax.axis_index("x") + lax.axis_index("y")

    def local_block(tree, n):
        return tree[n] if n == "g_final" else tree[n][0]

    sb = local_block(given, "shift_b")
    lo_part = lambda z: (z - z.astype(BF16).astype(F32)).astype(BF16)
    row_form = lambda tree: (lambda n: _row_form(n, local_block(tree, n)))
    tile16 = lambda z: jnp.pad(z, ((0, 16 - z.shape[0]), (0, HALF_W - z.shape[1])))
    sb_tiles = [tile16(f(sb[:, lanes])) for f in (lambda z: z.astype(BF16), lo_part)
                for lanes in (slice(0, HALF_W), slice(HALF_W, None))]
    tails = [[_half_piece(n, lo_part(local_block(given, n)), h) for n in LORAS] + sb_tiles for h in range(2)]
    pack_w = jnp.stack([_pack_half(EARLY, row_form(given), h, BF16, tails[h]) for h in range(2)])
    gathered, gathered_token = gather_shards(pack_w)
    gathered = lax.dynamic_update_index_in_dim(gathered, pack_w, chip, 0)
    pack_late = jnp.stack([_pack_half(LATE, row_form(given), h, BF16) for h in range(2)])
    late_state, late_token = split_start("gather_start", _gather_copies, 3, pack_late, (N_CHIPS,) + pack_late.shape,
                                         gathered_token)

    def whole(group, got, own):
        half = lambda n, j, h: jnp.where(chip == j, _piece(own[h], n), _piece(got[j, h], n))
        shard = lambda n, j: _join_halves(n, half(n, j, 0), half(n, j, 1))
        return {n: jnp.concatenate([shard(n, j) for j in range(N_CHIPS)],
                                   axis=0 if n == "w_in" else SHARD_AXIS[n]) for n in group}

    w = whole(EARLY, gathered, pack_w)
    def late_weights(after):
        got = gather_forward(split_wait("gather_wait", _gather_copies, late_state, after)[1])
        return {**whole([n for n in LATE if PIECE_ROWS[n] != FFN_BLOCK], got, pack_late), "late_packs": (got, pack_late)}
    off = LO_OFF
    for n in LORAS:
        r, cols = PIECE_ROWS[n], SHARD_SHAPE[n][1]
        lo = jnp.concatenate([jnp.concatenate([gathered[j, 0, off:off + r, :cols], gathered[j, 1, off:off + r, :cols]],
                                              axis=0) for j in range(N_CHIPS)], axis=1)
        w[n] = w[n].astype(F32) + lo.astype(F32)
        off += r
    sb_tile = lambda j, t, lanes: gathered[j, 0, off + 16 * t:off + 16 * t + 2, :lanes].astype(F32)
    rest = SHIFT_SHARD[1] - HALF_W
    w["shift_b"] = jnp.concatenate(
        [jnp.concatenate([sb_tile(j, 0, HALF_W) + sb_tile(j, 2, HALF_W), sb_tile(j, 1, rest) + sb_tile(j, 3, rest)],
                         axis=1) for j in range(N_CHIPS)], axis=1)
    for n in SMALL:
        w[n] = local_block(given, n).reshape(SMALL_SHAPE[n])

    def pair_start(g_pack, tag):
        return split_start("reduce_pair_start_" + tag, _pair_copies, N_CHIPS, g_pack, (N_CHIPS,) + g_pack.shape[2:])

    def pair_finish(state, after, tag):
        g_pack, got = split_wait("reduce_pair_wait_" + tag, _pair_copies, state, after)
        return pair_sum(g_pack, got, tag, tm=got.shape[1] // 2)

    pack_early = lambda g: jnp.stack([jnp.stack([_pack_half(EARLY, lambda n: _grad_row_form(n, g[n], j), h, F32)
                                                 for h in range(2)]) for j in range(N_CHIPS)])
    loss, grad_x, grads, (late_part, late_slots), early_state = local_step(
        x[0], loss_target[0], w, late_token, late_weights, pair_start, pair_finish, pack_early)

    early_part, early_part16 = pair_finish(early_state, grad_x, "early")
    s_pack = _pack_rows([grads[n] for n in SMALL] + [grads["shift_b"], loss.reshape(1, 1)], SMALL_ROWS, F32)
    chips_state, token = split_start("reduce_chips_start", _chip_copies, 3, early_part16, early_part16.shape)
    out_g, out_d, out_m, out_v = {}, {}, {}, {}

    def finish(group, tag, part, slots):
        half_sum = sum_with_own(part, slots, chip, token, tm=part.shape[1] // 2, name="chip_sum_" + tag)
        other_half = exchange_halves(half_sum, tag)
        for n in group:
            res = adamw_weight(n, half_sum, other_half,
                               *[_row_form(n, local_block(t, n)) for t in (given, mom_m, mom_v)])
            for tree, z in zip((out_g, out_d, out_m, out_v), res):
                tree[n] = _row_form(n, z)

    finish(LATE, "late", late_part, late_slots)

    small_shapes = [SMALL_SHAPE[n] for n in SMALL]
    g_small = sum_all(s_pack, token)
    *g_parts, loss = _unpack_rows(g_small, small_shapes + [(2, N_RWKV), ()])
    out_g.update(zip(SMALL, g_parts[:-1]))
    g_sb = lax.dynamic_slice_in_dim(g_parts[-1], chip * SHIFT_SHARD[1], SHIFT_SHARD[1], axis=1)
    out_g["shift_b"] = g_sb
    names = SMALL + ["shift_b"]
    native = lambda tree: [local_block(tree, n).reshape(SMALL_SHAPE.get(n, SHIFT_SHARD)) for n in names]
    small_res = adamw_many(g_parts[:-1] + [g_sb], native(given), native(mom_m), native(mom_v))
    for tree, res in zip((out_d, out_m, out_v), small_res):
        tree.update(zip(names, res))

    after = (out_v["w_out"], out_v["sgu_w"])
    early_slots = split_wait("reduce_chips_wait", _chip_copies, chips_state,
                             jnp.concatenate([z.reshape(-1)[:8] for z in after]))[1]
    finish(EARLY, "early", early_part, early_slots)

    def block_of(tree, n):
        return tree[n].reshape(given[n].shape)

    return (loss, grad_x[None], *[block_of(out_g, n) for n in WEIGHTS], *[block_of(out_d, n) for n in WEIGHTS],
            *[block_of(out_m, n) for n in WEIGHTS], *[block_of(out_v, n) for n in WEIGHTS])
```

```python
import functools

import jax
import jax.numpy as jnp
from jax import lax
from jax.experimental import pallas as pl
from jax.experimental.pallas import tpu as pltpu

F32 = jnp.float32
BF16 = jnp.bfloat16

D_MODEL = 1024
N_HEADS = 16
HEAD = 64
SCAN_CHUNK = 64

VMEM_LIMIT = 56 * 1024 * 1024


_BDIMS = {
    "nn": (((2,), (1,)), ((0,), (0,))),
    "nt": (((2,), (2,)), ((0,), (0,))),
    "tn": (((1,), (1,)), ((0,), (0,))),
}


def _raw_bdot(x, y, mode, fine):
    if fine:
        return lax.dot_general(x, y, _BDIMS[mode], precision=lax.Precision.HIGH, preferred_element_type=F32)
    return lax.dot_general(x.astype(BF16), y.astype(BF16), _BDIMS[mode], preferred_element_type=F32)


@functools.partial(jax.custom_vjp, nondiff_argnums=(2, 3))
def bdot(x, y, mode, fine=True):
    return _raw_bdot(x, y, mode, fine)


def _bdot_fwd(x, y, mode, fine):
    return _raw_bdot(x, y, mode, fine), (x, y)


def _bdot_bwd(mode, fine, res, g):
    x, y = res
    if mode == "nn":
        return bdot(g, y, "nt", fine), bdot(x, g, "tn", fine)
    if mode == "nt":
        return bdot(g, y, "nn", fine), bdot(g, x, "tn", fine)
    return bdot(y, g, "nt", fine), bdot(x, g, "nn", fine)


bdot.defvjp(_bdot_fwd, _bdot_bwd)


def _scan_chunk(S0, r, lw, k, v, a, b):
    nh, lc, _ = r.shape
    ti = lax.broadcasted_iota(jnp.int32, (lc, lc), 0)
    si = lax.broadcasted_iota(jnp.int32, (lc, lc), 1)
    incl = (si <= ti).astype(F32)
    strict = (si < ti).astype(F32)
    eye = (si == ti).astype(F32)
    cl = bdot(jnp.broadcast_to(incl, (nh, lc, lc)), lw, "nn")
    cl_last = cl[:, lc - 1:lc, :]
    g_last = jnp.exp(cl_last - cl)
    at = a * jnp.exp(cl - lw)
    bt = b * jnp.exp(-cl)
    kt = k * jnp.exp(-cl)
    rt = r * jnp.exp(cl)
    ar = jnp.concatenate([at, rt], axis=1)
    ar_b = bdot(ar, bt, "nt", False)
    ar_k = bdot(ar, kt, "nt", False)
    m_ab, m_rb = ar_b[:, :lc] * strict, ar_b[:, lc:] * incl
    m_ak, m_rk = ar_k[:, :lc] * strict, ar_k[:, lc:] * incl
    x = eye + m_ab
    p = bdot(m_ab, m_ab, "nn", False)
    n = 2
    while n * 2 < lc:
        px = bdot(jnp.concatenate([p, x], axis=1), p, "nn", False)
        p = px[:, :lc]
        x = x + px[:, lc:]
        n *= 2
    x = x + bdot(x, p, "nn", False)
    ar_s = bdot(ar, S0, "nt", False)
    akrk_v = bdot(jnp.concatenate([m_ak, m_rk], axis=1), v, "nn", False)
    u = bdot(x, ar_s[:, :lc] + akrk_v[:, :lc], "nn", False)
    o = ar_s[:, lc:] + bdot(m_rb, u, "nn", False) + akrk_v[:, lc:]
    s_last = S0 * jnp.exp(cl_last) + bdot(jnp.concatenate([u, v], axis=1),
                                          jnp.concatenate([b * g_last, k * g_last], axis=1), "tn", False)
    return o, s_last


def _split_heads(z):
    return jnp.stack([z[:, HEAD * h:HEAD * (h + 1)] for h in range(N_HEADS)], axis=0)


def _merge_heads(z):
    return jnp.concatenate([z[h] for h in range(N_HEADS)], axis=1)


def _scan_specs(t, ops, rev):
    nc = t // SCAN_CHUNK
    row = (lambda c: nc - 1 - c) if rev else (lambda c: c)
    specs = [pl.BlockSpec((SCAN_CHUNK, D_MODEL), lambda c, cb=cb: (row(c), cb)) for _, cb in ops]
    state = pl.BlockSpec((1, N_HEADS, HEAD, HEAD), lambda c: (row(c), 0, 0, 0))
    return nc, specs, state


def scan_fwd(ops):
    t = ops[0][0].shape[0]
    nc, specs, state = _scan_specs(t, ops, False)

    def body(r_ref, lw_ref, k_ref, v_ref, a_ref, b_ref, o_ref, s0_ref, s_scr):
        @pl.when(pl.program_id(0) == 0)
        def _():
            s_scr[...] = jnp.zeros_like(s_scr)

        s0 = s_scr[...]
        s0_ref[0] = s0
        o, s_last = _scan_chunk(s0, *[_split_heads(z[...]) for z in (r_ref, lw_ref, k_ref, v_ref, a_ref, b_ref)])
        o_ref[...] = _merge_heads(o)
        s_scr[...] = s_last

    return pl.pallas_call(
        body,
        name="scan_fwd",
        grid=(nc,),
        in_specs=specs,
        out_specs=[pl.BlockSpec((SCAN_CHUNK, D_MODEL), lambda c: (c, 0)), state],
        out_shape=[jax.ShapeDtypeStruct((t, D_MODEL), F32), jax.ShapeDtypeStruct((nc, N_HEADS, HEAD, HEAD), F32)],
        scratch_shapes=[pltpu.VMEM((N_HEADS, HEAD, HEAD), F32)],
        compiler_params=_cparams(1),
    )(*[a for a, _ in ops])


def scan_bwd(ops, s0s, do, part):
    t = ops[0][0].shape[0]
    nc, specs, state = _scan_specs(t, ops + [(do, 0)], True)

    def body(r_ref, lw_ref, k_ref, v_ref, a_ref, b_ref, do_ref, s0_ref, part_ref, *rest):
        out_refs, slots_ref, ds_scr, send_sems, recv_sems = rest[:6], rest[6], rest[7], rest[8], rest[9]
        step = pl.program_id(0)
        x, y, c = _coords()
        me = 2 * x + y
        chips = _other_chips(x, y)
        sends = [_remote(part_ref.at[2 * cx + cy], slots_ref.at[me], send_sems, recv_sems, k, (cx, cy, c))
                 for k, (cx, cy) in enumerate(chips)]

        @pl.when(step == 0)
        def _():
            ds_scr[...] = jnp.zeros_like(ds_scr)
            for cp in sends:
                cp.start()

        _, vjp = jax.vjp(_scan_chunk, s0_ref[0],
                         *[_split_heads(z[...]) for z in (r_ref, lw_ref, k_ref, v_ref, a_ref, b_ref)])
        grads = vjp((_split_heads(do_ref[...]), ds_scr[...]))
        for o_ref, g in zip(out_refs, grads[1:]):
            o_ref[...] = _merge_heads(g)
        ds_scr[...] = grads[0]

        @pl.when(step == nc - 1)
        def _():
            for k, (cx, cy) in enumerate(chips):
                _remote(part_ref.at[me], slots_ref.at[2 * cx + cy], send_sems, recv_sems, k, (cx, cy, c)).wait_recv()
            for cp in sends:
                cp.wait_send()

    return pl.pallas_call(
        body,
        name="scan_bwd",
        grid=(nc,),
        in_specs=specs + [state, _ANY],
        out_specs=[pl.BlockSpec((SCAN_CHUNK, D_MODEL), lambda c: (nc - 1 - c, 0))] * 6 + [_ANY],
        out_shape=[jax.ShapeDtypeStruct((t, D_MODEL), F32)] * 6 + [jax.ShapeDtypeStruct(part.shape, part.dtype)],
        scratch_shapes=[pltpu.VMEM((N_HEADS, HEAD, HEAD), F32), pltpu.SemaphoreType.DMA((3,)),
                        pltpu.SemaphoreType.DMA((3,))],
        compiler_params=_cparams(1),
    )(*[a for a, _ in ops], do, s0s, part)


_MDIMS = {
    "nn": (((1,), (0,)), ((), ())),
    "nt": (((1,), (1,)), ((), ())),
    "tn": (((0,), (0,)), ((), ())),
}


def _raw_mdot(x, y, mode, exact):
    if exact:
        return lax.dot_general(x, y, _MDIMS[mode], precision=lax.Precision.HIGH, preferred_element_type=F32)
    return lax.dot_general(x.astype(BF16), y.astype(BF16), _MDIMS[mode], preferred_element_type=F32)


@functools.partial(jax.custom_vjp, nondiff_argnums=(2, 3))
def mdot(x, y, mode, exact):
    return _raw_mdot(x, y, mode, exact)


def _mdot_fwd(x, y, mode, exact):
    return _raw_mdot(x, y, mode, exact), (x, y)


def _mdot_bwd(mode, exact, res, g):
    x, y = res
    if mode == "nn":
        return mdot(g, y, "nt", exact), mdot(x, g, "tn", exact)
    if mode == "nt":
        return mdot(g, y, "nn", exact), mdot(g, x, "tn", exact)
    return mdot(y, g, "nt", exact), mdot(x, g, "nn", exact)


mdot.defvjp(_mdot_fwd, _mdot_bwd)


def _seg_ones():
    i = lax.broadcasted_iota(jnp.int32, (256, 256), 0) // HEAD
    j = lax.broadcasted_iota(jnp.int32, (256, 256), 1) // HEAD
    return (i == j).astype(BF16)


@jax.custom_vjp
def segsum(x):
    bd = _seg_ones()
    hi = x.astype(BF16)
    lo = (x - hi.astype(F32)).astype(BF16)
    cols = []
    for j in range(x.shape[1] // 256):
        sl = slice(256 * j, 256 * (j + 1))
        cols.append(jnp.dot(hi[:, sl], bd, preferred_element_type=F32)
                    + jnp.dot(lo[:, sl], bd, preferred_element_type=F32))
    return jnp.concatenate(cols, axis=1)


segsum.defvjp(lambda x: (segsum(x), None), lambda _, g: (segsum(g),))


NORM_EPS = 1e-6
LN_EPS = 1e-5
GN_EPS = 64e-5
SGU_CHUNK = 128
SGU_GROUPS = 8


def _rms(x, g):
    return x * lax.rsqrt(jnp.mean(x * x, axis=-1, keepdims=True) + NORM_EPS) * g


def f_norm_in(x, g):
    return _rms(x, g), x


def f_sgu(p, ln_w, ln_b, sw, sbt):
    tm = p.shape[0]
    z = 0.5 * p * (1.0 + lax.erf(p * 0.7071067811865476))
    u, v = z[:, :D_MODEL], z[:, D_MODEL:]
    mu = jnp.mean(v, axis=-1, keepdims=True)
    d = v - mu
    vn = d * lax.rsqrt(jnp.mean(d * d, axis=-1, keepdims=True) + LN_EPS) * ln_w + ln_b
    ii = lax.broadcasted_iota(jnp.int32, (SGU_CHUNK, SGU_CHUNK), 0)
    jj = lax.broadcasted_iota(jnp.int32, (SGU_CHUNK, SGU_CHUNK), 1)
    mask = (jj <= ii).astype(F32)
    gi = lax.broadcasted_iota(jnp.int32, (SGU_GROUPS, D_MODEL), 0)
    ci = lax.broadcasted_iota(jnp.int32, (SGU_GROUPS, D_MODEL), 1) // SGU_CHUNK
    bias = mdot(sbt, (gi == ci).astype(F32), "nn", True)
    rows = []
    for c in range(tm // SGU_CHUNK):
        cols = []
        for g in range(SGU_GROUPS):
            blk = vn[c * SGU_CHUNK:(c + 1) * SGU_CHUNK, g * SGU_CHUNK:(g + 1) * SGU_CHUNK]
            cols.append(mdot(sw[g] * mask, blk, "nn", False))
        rows.append(jnp.concatenate(cols, axis=1) + bias)
    return (u * jnp.concatenate(rows, axis=0),)


def _softplus(x):
    return jnp.maximum(x, 0.0) + jnp.log1p(jnp.exp(-jnp.abs(x)))


def f_pre(q, wl, w0, al, a0, gl, k_k, k_a):
    qr, qk, qv, ql = q[:, :1024], q[:, 1024:2048], q[:, 2048:3072], q[:, 3072:]
    return _f_pre(qr, qk, qv, ql, wl, w0, al, a0, gl, k_k, k_a)


def _f_pre(qr, qk, qv, ql, wl, w0, al, a0, gl, k_k, k_a):
    xw, xa, xg = ql[:, :128], ql[:, 128:256], ql[:, 256:512]
    wr = -_softplus(-(w0 + mdot(jnp.tanh(xw), wl, "nn", False))) - 0.5
    lw = -jnp.exp(wr)
    aa = jax.nn.sigmoid(a0 + mdot(xa, al, "nn", False))
    g = mdot(jax.nn.sigmoid(xg), gl, "nn", False)
    kkr = qk * k_k
    kk = kkr / jnp.maximum(jnp.sqrt(segsum(kkr * kkr)), 1e-12)
    kp = qk * (1.0 + (aa - 1.0) * k_a)
    return qr, lw, kp, qv, -kk, kk * aa, g, qr, kp, qv


def f_post(o, r, kp, v, g, lnw, lnb, rk):
    mu = segsum(o) * (1.0 / HEAD)
    d = o - mu
    gn = d * lax.rsqrt(segsum(d * d) * (1.0 / HEAD) + GN_EPS)
    return ((gn * lnw + lnb + segsum(r * kp * rk) * v) * g,)


def f_mix(ya, yb, ga, gb):
    return (jax.nn.sigmoid(ga) * ya + jax.nn.sigmoid(gb) * yb,)


def f_ffn_in(h1, g):
    return _rms(h1, g), h1


def f_final(h1, m3, tgt, g):
    y = _rms(h1 + m3, g)
    err = jnp.square(y - tgt)
    return 0.5 * jnp.sum(jnp.mean(err, axis=-1))


def _cparams(n_grid):
    return pltpu.CompilerParams(dimension_semantics=("arbitrary",) * n_grid, vmem_limit_bytes=VMEM_LIMIT)


def _tile_spec(tm, w, cb):
    return pl.BlockSpec((tm, w), lambda i: (i, cb))


def _const_spec(c):
    nd = c.ndim
    return pl.BlockSpec(c.shape, lambda i: (0,) * nd)


def ew_call(fn, tiled, consts, outs, *, tm, name):
    t = tiled[0][0].shape[0]
    n_t, n_c = len(tiled), len(consts)

    def body(*refs):
        tv = [r[...].astype(F32) for r in refs[:n_t]]
        cv = [r[...] for r in refs[n_t:n_t + n_c]]
        res = fn(*tv, *cv)
        for o_ref, val in zip(refs[n_t + n_c:], res):
            o_ref[...] = val.astype(o_ref.dtype)

    return pl.pallas_call(
        body,
        name=name,
        grid=(t // tm,),
        in_specs=[_tile_spec(tm, w, cb) for _, w, cb in tiled] + [_const_spec(c) for c in consts],
        out_specs=[_tile_spec(tm, w, 0) for w, _ in outs],
        out_shape=[jax.ShapeDtypeStruct((t, w), dt) for w, dt in outs],
        compiler_params=_cparams(1),
    )(*[a for a, _, _ in tiled], *consts)


def ew_vjp_call(fn, tiled, consts, cots, d_tiled, d_consts, *, tm, name):
    t = tiled[0][0].shape[0]
    n_t, n_c, n_g = len(tiled), len(consts), len(cots)
    dt_list = [(i, dt) for i, dts in enumerate(d_tiled) for dt in dts]
    dc_list = [i for i, want in enumerate(d_consts) if want]

    def body(*refs):
        tv = [r[...].astype(F32) for r in refs[:n_t]]
        cv = [r[...] for r in refs[n_t:n_t + n_c]]
        gv = tuple(r[...].astype(F32) for r in refs[n_t + n_c:n_t + n_c + n_g])
        out_refs = refs[n_t + n_c + n_g:]
        _, vjp = jax.vjp(fn, *tv, *cv)
        grads = vjp(gv)
        for o_ref, (i, _) in zip(out_refs, dt_list):
            o_ref[...] = grads[i].astype(o_ref.dtype)
        acc_refs = out_refs[len(dt_list):]

        @pl.when(pl.program_id(0) == 0)
        def _():
            for a_ref in acc_refs:
                a_ref[...] = jnp.zeros_like(a_ref)

        for a_ref, i in zip(acc_refs, dc_list):
            a_ref[...] += grads[n_t + i]

    res = pl.pallas_call(
        body,
        name=name,
        grid=(t // tm,),
        in_specs=[_tile_spec(tm, w, cb) for _, w, cb in tiled] + [_const_spec(c) for c in consts]
        + [_tile_spec(tm, w, cb) for _, w, cb in cots],
        out_specs=[_tile_spec(tm, tiled[i][1], 0) for i, _ in dt_list] + [_const_spec(consts[i]) for i in dc_list],
        out_shape=[jax.ShapeDtypeStruct((t, tiled[i][1]), dt) for i, dt in dt_list]
        + [jax.ShapeDtypeStruct(consts[i].shape, F32) for i in dc_list],
        compiler_params=_cparams(1),
    )(*[a for a, _, _ in tiled], *consts, *[a for a, _, _ in cots])
    return res[:len(dt_list)], res[len(dt_list):]


def mm(a, b, mode, *, tm, tn, name, out_dtypes=(F32,), epi=None, extras=(), into=None):
    m = a.shape[1] if mode == "tn" else a.shape[0]
    kd = a.shape[0] if mode == "tn" else a.shape[1]
    n = b.shape[0] if mode == "nt" else b.shape[1]
    tm, tn = min(tm, m), min(tn, n)
    if mode == "nn":
        a_spec = pl.BlockSpec((tm, kd), lambda i, j: (i, 0))
        b_spec = pl.BlockSpec((kd, tn), lambda i, j: (0, j))
    elif mode == "nt":
        a_spec = pl.BlockSpec((tm, kd), lambda i, j: (i, 0))
        b_spec = pl.BlockSpec((tn, kd), lambda i, j: (j, 0))
    else:
        a_spec = pl.BlockSpec((kd, tm), lambda i, j: (0, i))
        b_spec = pl.BlockSpec((kd, tn), lambda i, j: (0, j))
    n_e = len(extras)
    o_spec = pl.BlockSpec((tm, tn), lambda i, j: (i, j))

    if into is not None:
        buf, place = into

        def body_into(a_ref, b_ref, buf_ref, o_ref):
            o_ref[0, 0] = lax.dot_general(a_ref[...].astype(BF16), b_ref[...].astype(BF16), _MDIMS[mode],
                                          preferred_element_type=F32)

        return pl.pallas_call(
            body_into,
            name=name,
            grid=(m // tm, n // tn),
            in_specs=[a_spec, b_spec, pl.BlockSpec(memory_space=pl.ANY)],
            out_specs=pl.BlockSpec((1, 1, tm, tn), lambda i, j: (*place(i, j), 0)),
            out_shape=jax.ShapeDtypeStruct(buf.shape, F32),
            input_output_aliases={2: 0},
            compiler_params=_cparams(2),
        )(a, b, buf)

    def body(a_ref, b_ref, *refs):
        c = lax.dot_general(a_ref[...].astype(BF16), b_ref[...].astype(BF16), _MDIMS[mode],
                            preferred_element_type=F32)
        res = epi(c, *[r[...] for r in refs[:n_e]]) if epi is not None else (c,)
        for o_ref, val in zip(refs[n_e:], res):
            o_ref[...] = val.astype(o_ref.dtype)

    res = pl.pallas_call(
        body,
        name=name,
        grid=(m // tm, n // tn),
        in_specs=[a_spec, b_spec] + [o_spec] * n_e,
        out_specs=[o_spec] * len(out_dtypes),
        out_shape=[jax.ShapeDtypeStruct((m, n), dt) for dt in out_dtypes],
        compiler_params=_cparams(2),
    )(a, b, *extras)
    return res if len(out_dtypes) > 1 else res[0]


def mm_packed(a, packs, wname, mode, *, tm, name, out_dtypes=(F32,), epi=None, extras=()):
    got, own = packs
    m, kd = a.shape
    n_chips = got.shape[0]
    piece_r = PIECE_ROWS[wname]
    assert PIECE_OFF[wname] % piece_r == 0
    r0 = PIECE_OFF[wname] // piece_r
    by_cols = SHARD_AXIS[wname] == 1
    pos = lambda j, h: (0, 2 * HALF_W * j + HALF_W * h) if by_cols else (piece_r * j, HALF_W * h)
    w_rows, w_cols = (piece_r, n_chips * 2 * HALF_W) if by_cols else (n_chips * piece_r, 2 * HALF_W)
    n = w_cols if mode == "nn" else w_rows
    assert kd == (w_rows if mode == "nn" else w_cols)
    pieces = [(j, h) for j in range(n_chips) for h in range(2)]
    out_col = lambda j, h: pos(j, h)[1] if mode == "nn" else pos(j, h)[0]
    n_e = len(extras)

    def body(a_ref, *refs):
        got_refs, own_refs = refs[:len(pieces)], refs[len(pieces):len(pieces) + 2]
        ex_refs, o_refs = refs[len(pieces) + 2:][:n_e], refs[len(pieces) + 2 + n_e:]
        chip = 2 * lax.axis_index("x") + lax.axis_index("y")
        for col in sorted({out_col(j, h) for j, h in pieces}):
            c = None
            for idx, (j, h) in enumerate(pieces):
                if out_col(j, h) != col:
                    continue
                blk = jnp.where(chip == j, own_refs[h][0], got_refs[idx][0, 0])
                r_lo, c_lo = pos(j, h)
                if mode == "nn":
                    part = lax.dot_general(a_ref[:, r_lo:r_lo + piece_r], blk, _MDIMS["nn"],
                                           preferred_element_type=F32)
                else:
                    part = lax.dot_general(a_ref[:, c_lo:c_lo + HALF_W], blk, _MDIMS["nt"],
                                           preferred_element_type=F32)
                c = part if c is None else c + part
            width = c.shape[1]
            res = epi(c, *[r[:, col:col + width] for r in ex_refs]) if epi is not None else (c,)
            for o_ref, val in zip(o_refs, res):
                o_ref[:, col:col + width] = val.astype(o_ref.dtype)

    row_spec = lambda width: pl.BlockSpec((tm, width), lambda i: (i, 0))
    got_specs = [pl.BlockSpec((1, 1, piece_r, HALF_W), lambda i, j=j, h=h: (j, h, r0, 0)) for j, h in pieces]
    own_specs = [pl.BlockSpec((1, piece_r, HALF_W), lambda i, h=h: (h, r0, 0)) for h in range(2)]
    res = pl.pallas_call(
        body,
        name=name,
        grid=(m // tm,),
        in_specs=[row_spec(kd)] + got_specs + own_specs + [row_spec(n)] * n_e,
        out_specs=[row_spec(n)] * len(out_dtypes),
        out_shape=[jax.ShapeDtypeStruct((m, n), dt) for dt in out_dtypes],
        compiler_params=_cparams(1),
    )(a, *([got] * len(pieces)), own, own, *extras)
    return res if len(out_dtypes) > 1 else res[0]


RWKV_COL0 = 4096
RWKV_WIDTH = 3584
SHIFT_BLK = 512


def _shift_down(p, prev_row):
    rows = lax.broadcasted_iota(jnp.int32, p.shape, 0)
    return jnp.where(rows == 0, prev_row, pltpu.roll(p, 1, 0))


def shiftmix_fwd(p_all, sbp, *, tm):
    t = p_all.shape[0]
    tm = min(tm, t)
    c0 = RWKV_COL0 // SHIFT_BLK
    hb = tm // 8

    def body(p_ref, halo_ref, sb_ref, q_ref):
        p = p_ref[...]
        prev = jnp.where(pl.program_id(0) == 0, 0.0, halo_ref[7:8, :])
        q_ref[...] = p * sb_ref[0:1, :] + _shift_down(p, prev) * sb_ref[1:2, :]

    return pl.pallas_call(
        body,
        name="shiftmix_fwd",
        grid=(t // tm, RWKV_WIDTH // SHIFT_BLK),
        in_specs=[
            pl.BlockSpec((tm, SHIFT_BLK), lambda i, j: (i, c0 + j)),
            pl.BlockSpec((8, SHIFT_BLK), lambda i, j: (jnp.maximum(i * hb - 1, 0), c0 + j)),
            pl.BlockSpec((2, SHIFT_BLK), lambda i, j: (0, j)),
        ],
        out_specs=pl.BlockSpec((tm, SHIFT_BLK), lambda i, j: (i, j)),
        out_shape=jax.ShapeDtypeStruct((t, RWKV_WIDTH), F32),
        compiler_params=_cparams(2),
    )(p_all, p_all, sbp)


def shiftmix_bwd(dq, col0, p_all, sbp, *, tm, name):
    t, w = dq.shape
    n_i = t // tm
    hb = tm // 8
    cq = col0 // SHIFT_BLK
    cp = (RWKV_COL0 + col0) // SHIFT_BLK

    def body(dq_ref, dqn_ref, p_ref, ph_ref, sb_ref, dp_ref, dsb_ref):
        i = pl.program_id(1)
        dq_t = dq_ref[...]
        rows = lax.broadcasted_iota(jnp.int32, dq_t.shape, 0)
        nxt = jnp.where(i == n_i - 1, 0.0, dqn_ref[0:1, :])
        up = jnp.where(rows == tm - 1, nxt, pltpu.roll(dq_t, tm - 1, 0))
        dp_ref[...] = (dq_t * sb_ref[0:1, :] + up * sb_ref[1:2, :]).astype(dp_ref.dtype)
        p = p_ref[...]
        prev = jnp.where(i == 0, 0.0, ph_ref[7:8, :])
        s0 = jnp.sum(dq_t * p, axis=0, keepdims=True)
        s1 = jnp.sum(dq_t * _shift_down(p, prev), axis=0, keepdims=True)
        two = lax.broadcasted_iota(jnp.int32, (2, SHIFT_BLK), 0)

        @pl.when(i == 0)
        def _():
            dsb_ref[...] = jnp.zeros_like(dsb_ref)

        dsb_ref[...] += jnp.where(two == 0, s0, s1)

    return pl.pallas_call(
        body,
        name=name,
        grid=(w // SHIFT_BLK, n_i),
        in_specs=[
            pl.BlockSpec((tm, SHIFT_BLK), lambda j, i: (i, j)),
            pl.BlockSpec((8, SHIFT_BLK), lambda j, i: (jnp.minimum((i + 1) * hb, t // 8 - 1), j)),
            pl.BlockSpec((tm, SHIFT_BLK), lambda j, i: (i, cp + j)),
            pl.BlockSpec((8, SHIFT_BLK), lambda j, i: (jnp.maximum(i * hb - 1, 0), cp + j)),
            pl.BlockSpec((2, SHIFT_BLK), lambda j, i: (0, cq + j)),
        ],
        out_specs=[
            pl.BlockSpec((tm, SHIFT_BLK), lambda j, i: (i, j)),
            pl.BlockSpec((2, SHIFT_BLK), lambda j, i: (0, j)),
        ],
        out_shape=[jax.ShapeDtypeStruct((t, w), BF16), jax.ShapeDtypeStruct((2, w), F32)],
        compiler_params=_cparams(2),
    )(dq, dq, p_all, p_all, sbp)


def final_call(h1, m3, tgt, g_final, *, tm):
    t = h1.shape[0]

    def body(h1_ref, m3_ref, tgt_ref, g_ref, dh_ref, dhb_ref, dg_ref, loss_ref):
        loss, vjp = jax.vjp(f_final, h1_ref[...], m3_ref[...], tgt_ref[...], g_ref[...])
        dh, _, _, dg = vjp(jnp.ones((), F32))
        dh_ref[...] = dh
        dhb_ref[...] = dh.astype(BF16)

        @pl.when(pl.program_id(0) == 0)
        def _():
            dg_ref[...] = jnp.zeros_like(dg_ref)
            loss_ref[...] = jnp.zeros_like(loss_ref)

        dg_ref[...] += dg
        loss_ref[...] += jnp.full(loss_ref.shape, loss, F32)

    tile = _tile_spec(tm, D_MODEL, 0)
    return pl.pallas_call(
        body,
        name="final_loss",
        grid=(t // tm,),
        in_specs=[tile, tile, tile, _const_spec(g_final)],
        out_specs=[tile, tile, _const_spec(g_final), pl.BlockSpec((8, 128), lambda i: (0, 0))],
        out_shape=[jax.ShapeDtypeStruct((t, D_MODEL), F32), jax.ShapeDtypeStruct((t, D_MODEL), BF16),
                   jax.ShapeDtypeStruct(g_final.shape, F32), jax.ShapeDtypeStruct((8, 128), F32)],
        compiler_params=_cparams(1),
    )(h1, m3, tgt, g_final)


N_SGU = 2048
N_RWKV = 3360
LORA_W, LORA_A, LORA_G = 64, 64, 160


def _pad_rwkv_cols(z):
    zero = lambda n: jnp.zeros(z.shape[:-1] + (n,), z.dtype)
    return jnp.concatenate([z[..., :3072], z[..., 3072:3136], zero(64), z[..., 3136:3200], zero(64),
                            z[..., 3200:3360], zero(96)], axis=-1)


def _unpad_rwkv_cols(z):
    return jnp.concatenate([z[..., :3072], z[..., 3072:3136], z[..., 3200:3264], z[..., 3328:3488]], axis=-1)


def _pad_win_rows(wt):
    z = wt[N_SGU:N_SGU + N_RWKV]
    zero = lambda n: jnp.zeros((n, wt.shape[1]), wt.dtype)
    return jnp.concatenate([wt[:N_SGU], wt[N_SGU + N_RWKV:], z[:3072], z[3072:3136], zero(64), z[3136:3200], zero(64),
                            z[3200:3360], zero(96)], axis=0)


def _unpad_win_rows(wt):
    z = wt[RWKV_COL0:]
    return jnp.concatenate([wt[:N_SGU], z[:3072], z[3072:3136], z[3200:3264], z[3328:3488], wt[N_SGU:RWKV_COL0]],
                           axis=0)


def _pad_rows(w, n):
    return jnp.concatenate([w, jnp.zeros((n - w.shape[0],) + w.shape[1:], w.dtype)], axis=0)


def _relu2_epi(c):
    return c, jnp.square(jnp.maximum(c, 0.0))


def _relu2_bwd_epi(c, hid):
    return (c * (2.0 * jnp.maximum(hid.astype(F32), 0.0)),)


def _add_epi(c, x):
    return (c + x,)


def _pre_fwd(*args):
    res = f_pre(*args)
    return res[1], res[2], res[4], res[5], res[6]


def local_step(x, tgt, w, late_token, late_weights, pair_start, pair_finish, pack_early):
    d = D_MODEL
    win_pt = _pad_win_rows(w["w_in"])
    sbp = _pad_rwkv_cols(w["shift_b"])
    wl = _pad_rows(w["w_lora_w"], 128)
    al = _pad_rows(w["a_lora_w"], 128)
    gl = _pad_rows(w["g_lora_w"], 256)
    sbt = w["sgu_b"].T

    (a_bf,) = ew_call(lambda x_, g_: (f_norm_in(x_, g_)[0],), [(x, d, 0)], [w["g_mix"] + late_token[:1, :1]],
                      [(d, BF16)], tm=512, name="norm_in")
    p_all = mm(a_bf, win_pt, "nt", tm=2048, tn=1280, name="mm_in")
    sgu_t = [(p_all, 2 * d, 0)]
    sgu_c = [w["sgu_ln_w"], w["sgu_ln_b"], w["sgu_w"], sbt]
    (s_bf,) = ew_call(f_sgu, sgu_t, sgu_c, [(d, BF16)], tm=512, name="sgu_fwd")
    q = shiftmix_fwd(p_all, sbp, tm=2048)
    pre_t = [(q, RWKV_WIDTH, 0)]
    pre_c = [wl, w["w0"], al, w["a0"], gl, w["k_k"], w["k_a"]]
    lw, kp, na, nb, g = ew_call(_pre_fwd, pre_t, pre_c, [(d, F32)] * 5, tm=256, name="rwkv_pre_fwd")
    scan_ops = [(q, 0), (lw, 0), (kp, 0), (q, 2), (na, 0), (nb, 0)]
    o, s0s = scan_fwd(scan_ops)
    w = {**w, **late_weights(o)}
    ya = mm_packed(s_bf, w["late_packs"], "w_proj_a", "nn", tm=1024, name="mm_proj_a")
    post_t = [(o, d, 0), (q, d, 0), (kp, d, 0), (q, d, 2), (g, d, 0)]
    post_c = [w["ln_x_w"], w["ln_x_b"], w["r_k"]]
    (ob_bf,) = ew_call(f_post, post_t, post_c, [(d, BF16)], tm=512, name="rwkv_post_fwd")
    yb = mm_packed(ob_bf, w["late_packs"], "w_proj_b", "nn", tm=1024, name="mm_proj_b")
    mix_t = [(ya, d, 0), (yb, d, 0), (p_all, d, 2), (p_all, d, 3)]
    (mixed_bf,) = ew_call(f_mix, mix_t, [], [(d, BF16)], tm=512, name="mix_fwd")
    h1 = mm_packed(mixed_bf, w["late_packs"], "w_out", "nn", tm=1024, name="mm_out", epi=_add_epi, extras=(x,))
    (f_bf,) = ew_call(lambda h_, g_: (f_ffn_in(h_, g_)[0],), [(h1, d, 0)], [w["g_ffn"]], [(d, BF16)], tm=512,
                      name="ffn_norm")
    hid, act_bf = mm_packed(f_bf, w["late_packs"], "w_ffn1", "nn", tm=512, name="mm_ffn1", out_dtypes=(BF16, BF16),
                            epi=_relu2_epi)
    m3 = mm_packed(act_bf, w["late_packs"], "w_ffn2", "nn", tm=512, name="mm_ffn2")
    dh2, dh2_bf, dg_final, loss = final_call(h1, m3, tgt, w["g_final"], tm=512)

    dhid_bf = mm_packed(dh2_bf, w["late_packs"], "w_ffn2", "nt", tm=512, name="mm_dact", out_dtypes=(BF16,),
                        epi=_relu2_bwd_epi, extras=(hid,))
    late_g = lax.empty((N_CHIPS, 2, pack_rows(LATE), HALF_W), F32)
    late_g = mm(act_bf, dh2_bf, "tn", tm=1024, tn=HALF_W, name="mm_dw_ffn2",
                into=(late_g, lambda i, j: (i, j, PIECE_OFF["w_ffn2"] // 1024)))
    df = mm_packed(dhid_bf, w["late_packs"], "w_ffn1", "nt", tm=512, name="mm_df")
    late_g = mm(f_bf, dhid_bf, "tn", tm=1024, tn=HALF_W, name="mm_dw_ffn1",
                into=(late_g, lambda i, j: (j // 2, j % 2, PIECE_OFF["w_ffn1"] // 1024)))
    (dh1, dh1_bf), (dg_ffn,) = ew_vjp_call(f_ffn_in, [(h1, d, 0)], [w["g_ffn"]], [(df, d, 0), (dh2, d, 0)],
                                           [(F32, BF16)], [True], tm=512, name="ffn_norm_bwd")
    dmixed = mm_packed(dh1_bf, w["late_packs"], "w_out", "nt", tm=1024, name="mm_dmixed")
    late_g = mm(mixed_bf, dh1_bf, "tn", tm=256, tn=HALF_W, name="mm_dw_out",
                into=(late_g, lambda i, j: (i, j, PIECE_OFF["w_out"] // 256)))
    (dya_bf, dyb_bf, dga_bf, dgb_bf), _ = ew_vjp_call(f_mix, mix_t, [], [(dmixed, d, 0)], [(BF16,)] * 4, [], tm=256,
                                                      name="mix_bwd")
    dob = mm_packed(dyb_bf, w["late_packs"], "w_proj_b", "nt", tm=1024, name="mm_dob")
    late_g = mm(ob_bf, dyb_bf, "tn", tm=256, tn=HALF_W, name="mm_dw_proj_b",
                into=(late_g, lambda i, j: (i, j, PIECE_OFF["w_proj_b"] // 256)))
    late_g = mm(s_bf, dya_bf, "tn", tm=256, tn=HALF_W, name="mm_dw_proj_a",
                into=(late_g, lambda i, j: (i, j, PIECE_OFF["w_proj_a"] // 256)))
    late_state, late_token = pair_start(late_g, "late")
    post_c_after = [w["ln_x_w"] + late_token[:1, :1]] + post_c[1:]
    (do, dr_p, dkp_p, dv_p, dg), (dlnx_w, dlnx_b, dr_k) = ew_vjp_call(
        f_post, post_t, post_c_after, [(dob, d, 0)], [(F32,)] * 5, [True] * 3, tm=256, name="rwkv_post_bwd")
    late_part, late_part16 = pair_finish(late_state, do, "late")
    *scan_g, late_slots = scan_bwd(scan_ops, s0s, do, late_part16)
    pre_g = [(z, d, 0) for z in scan_g] + [(dg, d, 0), (dr_p, d, 0), (dkp_p, d, 0), (dv_p, d, 0)]
    (dq,), (dwl, dw0, dal, da0, dgl, dk_k, dk_a) = ew_vjp_call(
        f_pre, pre_t, pre_c, pre_g, [(F32,)], [True] * 7, tm=256, name="rwkv_pre_bwd")
    dp_rwkv, dsb = shiftmix_bwd(dq, 0, p_all, sbp, tm=1024, name="shiftmix_bwd")
    ds = mm_packed(dya_bf, w["late_packs"], "w_proj_a", "nt", tm=1024, name="mm_ds")
    (dp_sgu,), (dln_w, dln_b, dsw, dsbt) = ew_vjp_call(f_sgu, sgu_t, sgu_c, [(ds, d, 0)], [(BF16,)], [True] * 4,
                                                       tm=256, name="sgu_bwd")
    dp_all = jnp.concatenate([dp_sgu, dga_bf, dgb_bf, dp_rwkv], axis=1)
    d_in_pt = mm(dp_all, a_bf, "tn", tm=1280, tn=1024, name="mm_dw_in")
    early_state, early_token = pair_start(pack_early({
        "w_in": _unpad_win_rows(d_in_pt), "w_lora_w": dwl[:LORA_W], "a_lora_w": dal[:LORA_A],
        "g_lora_w": dgl[:LORA_G]}), "early")
    da = mm(dp_all, win_pt, "nn", tm=1024, tn=256, name="mm_da")
    g_mix_after = w["g_mix"] + early_token[:1, :1]
    (grad_x,), (dg_mix,) = ew_vjp_call(f_norm_in, [(x, d, 0)], [g_mix_after], [(da, d, 0), (dh1, d, 0)], [(F32,)],
                                       [True], tm=512, name="norm_in_bwd")

    grads = {
        "g_mix": dg_mix, "sgu_ln_w": dln_w, "sgu_ln_b": dln_b, "sgu_w": dsw, "sgu_b": dsbt.T,
        "shift_b": _unpad_rwkv_cols(dsb),
        "w0": dw0, "a0": da0, "k_k": dk_k, "k_a": dk_a, "r_k": dr_k, "ln_x_w": dlnx_w, "ln_x_b": dlnx_b,
        "g_ffn": dg_ffn, "g_final": dg_final,
    }
    return loss[0, 0], grad_x, grads, (late_part, late_slots), early_state


MESH = pl.DeviceIdType.MESH
N_CHIPS = 4
SMALL_ROWS = 160
_ANY = pl.BlockSpec(memory_space=pl.ANY)


def _coords():
    return lax.axis_index("x"), lax.axis_index("y"), lax.axis_index("c")


def _other_chips(x, y):
    return [(1 - x, y), (x, 1 - y), (1 - x, 1 - y)]


def _remote(src, dst, send_sems, recv_sems, k, to):
    return pltpu.make_async_remote_copy(src_ref=src, dst_ref=dst, send_sem=send_sems.at[k], recv_sem=recv_sems.at[k],
                                        device_id=to, device_id_type=MESH)


def gather_shards(pack):
    def body(src_ref, out_ref, token, send_sems, recv_sems):
        x, y, c = _coords()
        me = 2 * x + y
        sib = (x, y, 1 - c)
        chips = _other_chips(x, y)
        first = [_remote(src_ref.at[c], out_ref.at[me, c], send_sems, recv_sems, k, (cx, cy, c))
                 for k, (cx, cy) in enumerate(chips)]
        for cp in first:
            cp.start()
        passed = []
        for k, (cx, cy) in enumerate(chips):
            j = 2 * cx + cy
            _remote(src_ref.at[c], out_ref.at[j, c], send_sems, recv_sems, k, (cx, cy, c)).wait_recv()
            fwd = _remote(out_ref.at[j, c], out_ref.at[j, c], send_sems, recv_sems, 3 + k, sib)
            fwd.start()
            passed.append(fwd)
        for k, (cx, cy) in enumerate(chips):
            j = 2 * cx + cy
            _remote(out_ref.at[j, 1 - c], out_ref.at[j, 1 - c], send_sems, recv_sems, 3 + k, sib).wait_recv()
        for cp in first + passed:
            cp.wait_send()
        token[...] = jnp.zeros_like(token)

    return pl.pallas_call(
        body,
        name="gather_shards",
        in_specs=[_ANY],
        out_specs=[_ANY, pl.BlockSpec(memory_space=pltpu.VMEM)],
        out_shape=[jax.ShapeDtypeStruct((N_CHIPS,) + pack.shape, pack.dtype), jax.ShapeDtypeStruct((8, 128), F32)],
        scratch_shapes=[pltpu.SemaphoreType.DMA((6,)), pltpu.SemaphoreType.DMA((6,))],
    )(pack)


def _gather_copies(pack_ref, all_ref, send_sems, recv_sems):
    x, y, c = _coords()
    me = 2 * x + y
    return [(_remote(pack_ref.at[c], all_ref.at[me, c], send_sems, recv_sems, k, (cx, cy, c)),
             _remote(pack_ref.at[c], all_ref.at[2 * cx + cy, c], send_sems, recv_sems, k, (cx, cy, c)))
            for k, (cx, cy) in enumerate(_other_chips(x, y))]


_HBM = pl.BlockSpec(memory_space=pltpu.HBM)
_SEM = pl.BlockSpec(memory_space=pltpu.SEMAPHORE)
_SIDE_EFFECT = pltpu.SideEffectType.DATAFLOW_SIDE_EFFECTING


def split_start(name, copies, n, src, land_shape, after=None):
    def body(src_ref, land_ref, *refs):
        send_sems, recv_sems, token = refs[-5], refs[-4], refs[-1]
        for send, _ in copies(src_ref, land_ref, send_sems, recv_sems):
            send.start()
        token[...] = jnp.zeros_like(token)

    extra = () if after is None else (after,)
    *state, token = pl.pallas_call(
        body,
        name=name,
        out_shape=(pltpu.SemaphoreType.DMA((n,)), pltpu.SemaphoreType.DMA((n,)), pltpu.HBM(src.shape, src.dtype),
                   pltpu.HBM(land_shape, src.dtype), jax.ShapeDtypeStruct((8, 128), F32)),
        in_specs=(_HBM, _HBM) + (pl.BlockSpec(memory_space=pl.ANY),) * len(extra),
        out_specs=(_SEM, _SEM, _HBM, _HBM, pl.BlockSpec(memory_space=pltpu.VMEM)),
        input_output_aliases={0: 2, 1: 3},
        compiler_params=pltpu.CompilerParams(has_side_effects=_SIDE_EFFECT),
    )(pltpu.with_memory_space_constraint(src, pltpu.HBM),
      pltpu.with_memory_space_constraint(lax.empty(land_shape, src.dtype), pltpu.HBM), *extra)
    return state, token


def split_wait(name, copies, state, after):
    send_sems, recv_sems, src, land = state

    def body(src_ref, land_ref, send_sems, recv_sems, after_ref, src_out, land_out):
        for send, arrival in copies(src_ref, land_ref, send_sems, recv_sems):
            send.wait_send()
            arrival.wait_recv()

    return pl.pallas_call(
        body,
        name=name,
        out_shape=(pltpu.HBM(src.shape, src.dtype), pltpu.HBM(land.shape, land.dtype)),
        in_specs=(_HBM, _HBM, _SEM, _SEM, pl.BlockSpec(memory_space=pl.ANY)),
        out_specs=(_HBM, _HBM),
        input_output_aliases={0: 0, 1: 1},
        compiler_params=pltpu.CompilerParams(has_side_effects=_SIDE_EFFECT),
    )(src, land, send_sems, recv_sems, after)


def gather_forward(got):
    def body(got_ref, out_ref, send_sems, recv_sems):
        x, y, c = _coords()
        sib = (x, y, 1 - c)
        slots = [2 * cx + cy for cx, cy in _other_chips(x, y)]
        sends = [_remote(got_ref.at[j, c], out_ref.at[j, c], send_sems, recv_sems, k, sib) for k, j in enumerate(slots)]
        for cp in sends:
            cp.start()
        for k, j in enumerate(slots):
            _remote(got_ref.at[j, 1 - c], out_ref.at[j, 1 - c], send_sems, recv_sems, k, sib).wait_recv()
        for cp in sends:
            cp.wait_send()

    return pl.pallas_call(
        body,
        name="gather_forward",
        in_specs=[_ANY],
        out_specs=_ANY,
        out_shape=jax.ShapeDtypeStruct(got.shape, got.dtype),
        input_output_aliases={0: 0},
        scratch_shapes=[pltpu.SemaphoreType.DMA((3,)), pltpu.SemaphoreType.DMA((3,))],
    )(got)


def pair_sum(g, got, tag, *, tm):
    n, _, rows, width = g.shape

    def body(c_ref, own_ref, got_ref, out_ref, out16_ref):
        total = own_ref[0, 0] + got_ref[0]
        out_ref[0] = total
        out16_ref[0] = total.astype(BF16)

    blk = pl.BlockSpec((1, tm, width), lambda j, i, c_ref: (j, i, 0))
    return pl.pallas_call(
        body,
        name="pair_sum_" + tag,
        grid_spec=pltpu.PrefetchScalarGridSpec(
            num_scalar_prefetch=1,
            grid=(n, rows // tm),
            in_specs=[pl.BlockSpec((1, 1, tm, width), lambda j, i, c_ref: (j, c_ref[0], i, 0)), blk],
            out_specs=[blk, blk],
        ),
        out_shape=[jax.ShapeDtypeStruct(got.shape, F32), jax.ShapeDtypeStruct(got.shape, BF16)],
        compiler_params=_cparams(2),
    )(lax.axis_index("c").reshape(1).astype(jnp.int32), g, got)


def _pair_copies(g_ref, got_ref, send_sems, recv_sems):
    x, y, c = _coords()
    copies = [_remote(g_ref.at[j, 1 - c], got_ref.at[j], send_sems, recv_sems, j, (x, y, 1 - c))
              for j in range(N_CHIPS)]
    return [(cp, cp) for cp in copies]


def _chip_copies(p_ref, slots_ref, send_sems, recv_sems):
    x, y, c = _coords()
    me = 2 * x + y
    return [(_remote(p_ref.at[2 * cx + cy], slots_ref.at[me], send_sems, recv_sems, k, (cx, cy, c)),
             _remote(p_ref.at[me], slots_ref.at[2 * cx + cy], send_sems, recv_sems, k, (cx, cy, c)))
            for k, (cx, cy) in enumerate(_other_chips(x, y))]


def sum_with_own(own, slots, mine, after, *, tm, name):
    n, rows, width = slots.shape

    def body(mine_ref, own_ref, *refs):
        acc = None
        for s in range(n):
            term = jnp.where(mine_ref[0] == s, own_ref[0], refs[s][0].astype(F32))
            acc = term if acc is None else acc + term
        refs[-1][...] = acc

    return pl.pallas_call(
        body,
        name=name,
        grid_spec=pltpu.PrefetchScalarGridSpec(
            num_scalar_prefetch=1,
            grid=(rows // tm,),
            in_specs=[pl.BlockSpec((1, tm, width), lambda i, mine_ref: (mine_ref[0], i, 0))]
            + [pl.BlockSpec((1, tm, width), lambda i, mine_ref, s=s: (s, i, 0)) for s in range(n)]
            + [pl.BlockSpec(after.shape, lambda i, mine_ref: (0,) * after.ndim)],
            out_specs=pl.BlockSpec((tm, width), lambda i, mine_ref: (i, 0)),
        ),
        out_shape=jax.ShapeDtypeStruct((rows, width), F32),
        compiler_params=_cparams(1),
    )(mine.reshape(1).astype(jnp.int32), own, *([slots] * n), after)


def exchange_halves(s, tag):
    nq = 4
    rq = s.shape[0] // nq
    assert rq * nq == s.shape[0] and rq % 8 == 0

    def body(s_ref, out_ref, sbuf, rbuf, send_sems, recv_sems, in_sems, out_sems):
        x, y, c = _coords()
        sib = (x, y, 1 - c)
        rows = lambda q: pl.ds(q * rq, rq)
        loads = [pltpu.make_async_copy(s_ref.at[rows(q)], sbuf.at[rows(q)], in_sems.at[q]) for q in range(nq)]
        for cp in loads:
            cp.start()
        sends = []
        for q in range(nq):
            loads[q].wait()
            sends.append(_remote(sbuf.at[rows(q)], rbuf.at[rows(q)], send_sems, recv_sems, q, sib))
            sends[q].start()
        stores = []
        for q in range(nq):
            sends[q].wait_recv()
            stores.append(pltpu.make_async_copy(rbuf.at[rows(q)], out_ref.at[rows(q)], out_sems.at[q]))
            stores[q].start()
        for cp in sends:
            cp.wait_send()
        for cp in stores:
            cp.wait()

    return pl.pallas_call(
        body,
        name="exchange_halves_" + tag,
        in_specs=[_ANY],
        out_specs=_ANY,
        out_shape=jax.ShapeDtypeStruct(s.shape, s.dtype),
        scratch_shapes=[pltpu.VMEM(s.shape, s.dtype), pltpu.VMEM(s.shape, s.dtype)]
        + [pltpu.SemaphoreType.DMA((nq,))] * 4,
        compiler_params=pltpu.CompilerParams(vmem_limit_bytes=VMEM_LIMIT),
    )(s)


def sum_all(s, after):
    rows = s.shape[0]
    half = rows // 2

    def body(s_ref, after_ref, out_ref, theirs, pair, slots, send_sems, recv_sems):
        x, y, c = _coords()
        me = 2 * x + y
        sib = (x, y, 1 - c)
        chips = _other_chips(x, y)
        swap = _remote(s_ref, theirs, send_sems, recv_sems, 0, sib)
        swap.start()
        swap.wait_recv()
        pair[...] = s_ref[...] + theirs[...]
        mine = pl.ds(pl.multiple_of(c * half, 8), half)
        other = pl.ds(pl.multiple_of((1 - c) * half, 8), half)
        sends = [_remote(pair.at[mine], slots.at[me], send_sems, recv_sems, 1 + k, (cx, cy, c))
                 for k, (cx, cy) in enumerate(chips)]
        for cp in sends:
            cp.start()
        for k, (cx, cy) in enumerate(chips):
            _remote(pair.at[mine], slots.at[2 * cx + cy], send_sems, recv_sems, 1 + k, (cx, cy, c)).wait_recv()
        slots[me] = pair[mine]
        out_ref[mine] = ((slots[0] + slots[1]) + slots[2]) + slots[3]
        last = _remote(out_ref.at[mine], out_ref.at[mine], send_sems, recv_sems, 4, sib)
        last.start()
        _remote(out_ref.at[other], out_ref.at[other], send_sems, recv_sems, 4, sib).wait_recv()
        for cp in [swap] + sends + [last]:
            cp.wait_send()

    vmem = pl.BlockSpec(memory_space=pltpu.VMEM)
    return pl.pallas_call(
        body,
        name="sum_all",
        in_specs=[vmem, vmem],
        out_specs=vmem,
        out_shape=jax.ShapeDtypeStruct(s.shape, s.dtype),
        scratch_shapes=[pltpu.VMEM(s.shape, s.dtype), pltpu.VMEM(s.shape, s.dtype),
                        pltpu.VMEM((N_CHIPS, half, s.shape[1]), s.dtype), pltpu.SemaphoreType.DMA((5,)),
                        pltpu.SemaphoreType.DMA((5,))],
        compiler_params=pltpu.CompilerParams(vmem_limit_bytes=VMEM_LIMIT),
    )(s, after)


ADAM_LR = 0.001
ADAM_B1 = 0.9
ADAM_B2 = 0.999
ADAM_EPS = 1e-08
ADAM_WD = 0.01
ADAM_STEP = 10


def f_adamw(g, w, m, v):
    m = ADAM_B1 * m + (1.0 - ADAM_B1) * g
    v = ADAM_B2 * v + (1.0 - ADAM_B2) * jnp.square(g)
    m_hat = m / (1.0 - ADAM_B1 ** ADAM_STEP)
    v_hat = v / (1.0 - ADAM_B2 ** ADAM_STEP)
    delta = -ADAM_LR * (m_hat / (jnp.sqrt(v_hat) + ADAM_EPS) + ADAM_WD * w)
    return delta, m, v


def adamw_many(gs, ws, ms, vs):
    n = len(gs)

    def body(*refs):
        ins, outs = refs[:4 * n], refs[4 * n:]
        for i in range(n):
            delta, nm, nv = f_adamw(ins[i][...], ins[n + i][...], ins[2 * n + i][...], ins[3 * n + i][...])
            outs[i][...] = delta
            outs[n + i][...] = nm
            outs[2 * n + i][...] = nv

    vmem = pl.BlockSpec(memory_space=pltpu.VMEM)
    res = pl.pallas_call(
        body,
        name="adamw_small",
        in_specs=[vmem] * (4 * n),
        out_specs=[vmem] * (3 * n),
        out_shape=[jax.ShapeDtypeStruct(w.shape, F32) for w in ws] * 3,
    )(*gs, *ws, *ms, *vs)
    return res[:n], res[n:2 * n], res[2 * n:]


EARLY = ["w_in", "w_lora_w", "a_lora_w", "g_lora_w"]
LATE = ["w_ffn1", "w_ffn2", "w_proj_b", "w_out", "w_proj_a"]
LORAS = ["w_lora_w", "a_lora_w", "g_lora_w"]
HALF_W = 512
PIECE_ROWS = {"w_in": 1864, "w_ffn1": 1024, "w_ffn2": 1024, "w_proj_a": 256, "w_proj_b": 256, "w_out": 256,
              "w_lora_w": 32, "a_lora_w": 32, "g_lora_w": 80}
PIECE_OFF = {"w_in": 0, "w_lora_w": 1920, "a_lora_w": 1952, "g_lora_w": 2000,
             "w_ffn1": 0, "w_ffn2": 1024, "w_proj_b": 2048, "w_out": 2304, "w_proj_a": 2560}
LO_OFF = 2080


def pack_rows(group):
    return 2304 if group is EARLY else 2816
SHARD_AXIS = {"w_in": 1, "w_proj_a": 0, "w_lora_w": 1, "a_lora_w": 1, "g_lora_w": 1, "w_proj_b": 0, "w_out": 0,
              "w_ffn1": 1, "w_ffn2": 0}
SHARD_SHAPE = {"w_in": (1024, 1864), "w_proj_a": (256, 1024), "w_lora_w": (64, 256), "a_lora_w": (64, 256),
               "g_lora_w": (160, 256), "w_proj_b": (256, 1024), "w_out": (256, 1024), "w_ffn1": (1024, 1024),
               "w_ffn2": (1024, 1024)}
SHIFT_SHARD = (2, 840)
VECTORS = ["g_mix", "sgu_ln_w", "sgu_ln_b", "w0", "a0", "k_k", "k_a", "r_k", "ln_x_w", "ln_x_b", "g_ffn", "g_final"]
SMALL = VECTORS + ["sgu_w", "sgu_b"]
SMALL_SHAPE = {**{n: (1, 1024) for n in VECTORS}, "sgu_w": (8, 128, 128), "sgu_b": (8, 128)}
WEIGHTS = ["g_mix", "w_in", "sgu_ln_w", "sgu_ln_b", "sgu_w", "sgu_b", "w_proj_a", "shift_b", "w_lora_w", "w0",
           "a_lora_w", "a0", "g_lora_w", "k_k", "k_a", "r_k", "ln_x_w", "ln_x_b", "w_proj_b", "w_out", "g_ffn",
           "w_ffn1", "w_ffn2", "g_final"]


def _size(shape):
    n = 1
    for s in shape:
        n *= s
    return n


def _pack_rows(parts, rows, dtype):
    flat = jnp.concatenate([p.reshape(-1).astype(dtype) for p in parts])
    return jnp.concatenate([flat, jnp.zeros((rows * 1024 - flat.shape[0],), dtype)]).reshape(rows, 1024)


def _unpack_rows(packed, shapes):
    flat = packed.reshape(-1)
    out, off = [], 0
    for shp in shapes:
        out.append(flat[off:off + _size(shp)].reshape(shp))
        off += _size(shp)
    return out


def _shard_of(name, full, j):
    ax = SHARD_AXIS[name]
    n = SHARD_SHAPE[name][ax]
    return lax.slice_in_dim(full, j * n, (j + 1) * n, axis=ax)


def _pad_cols(z, n):
    return jnp.concatenate([z, jnp.zeros((z.shape[0], n - z.shape[1]), z.dtype)], axis=1)


def _row_form(name, s):
    return s.T if name == "w_in" else s


def _half_piece(name, rf, h):
    if name in LORAS:
        r = PIECE_ROWS[name]
        return _pad_cols(rf[h * r:(h + 1) * r], HALF_W)
    return rf[:, HALF_W * h:HALF_W * (h + 1)]


def _pack_half(group, rf_fn, h, dtype, tail=()):
    parts, pos, rows = [], 0, pack_rows(group)
    for n in group:
        if PIECE_OFF[n] > pos:
            parts.append(jnp.zeros((PIECE_OFF[n] - pos, HALF_W), dtype))
        parts.append(_half_piece(n, rf_fn(n), h).astype(dtype))
        pos = PIECE_OFF[n] + PIECE_ROWS[n]
    for t in tail:
        parts.append(t)
        pos += t.shape[0]
    parts.append(jnp.zeros((rows - pos, HALF_W), dtype))
    return jnp.concatenate(parts, axis=0)


def _piece(pack, name):
    return pack[PIECE_OFF[name]:PIECE_OFF[name] + PIECE_ROWS[name]]


def _join_halves(name, p0, p1):
    if name in LORAS:
        return jnp.concatenate([p0[:, :SHARD_SHAPE[name][1]], p1[:, :SHARD_SHAPE[name][1]]], axis=0)
    return jnp.concatenate([p0, p1], axis=1)


def _grad_row_form(name, full, j):
    if name == "w_in":
        return full[SHARD_SHAPE[name][1] * j:SHARD_SHAPE[name][1] * (j + 1)]
    return _shard_of(name, full, j)


def adamw_weight(name, g_own, g_other, w, m, v):
    rows, width = w.shape
    if name in LORAS:
        tm = PIECE_ROWS[name]
        grid = (2, 1)
        native = pl.BlockSpec((tm, width), lambda h, i: (h, 0))
    elif name == "w_in":
        tm, lanes = rows, 256
        grid = (2, HALF_W // lanes)
        native = pl.BlockSpec((tm, lanes), lambda h, i: (0, h * (HALF_W // lanes) + i))
    else:
        tm = rows
        grid = (2, 1)
        native = pl.BlockSpec((tm, HALF_W), lambda h, i: (i, h))
    assert PIECE_OFF[name] % tm == 0
    off = PIECE_OFF[name] // tm
    if name == "w_in":
        packed = pl.BlockSpec((tm, lanes), lambda h, i: (0, i))
    else:
        packed = pl.BlockSpec((tm, HALF_W), lambda h, i: (off + i, 0))

    def body(go_ref, gx_ref, w_ref, m_ref, v_ref, g_ref, d_ref, nm_ref, nv_ref):
        g = jnp.where(pl.program_id(0) == lax.axis_index("c"), go_ref[...], gx_ref[...])[:, :w_ref.shape[1]]
        delta, nm, nv = f_adamw(g, w_ref[...], m_ref[...], v_ref[...])
        g_ref[...] = g
        d_ref[...] = delta
        nm_ref[...] = nm
        nv_ref[...] = nv

    return pl.pallas_call(
        body,
        name="adamw_" + name,
        grid=grid,
        in_specs=[packed, packed, native, native, native],
        out_specs=[native] * 4,
        out_shape=[jax.ShapeDtypeStruct(w.shape, F32)] * 4,
        compiler_params=_cparams(2),
    )(g_own, g_other, w, m, v)


def kernel(x, g_mix, w_in, sgu_ln_w, sgu_ln_b, sgu_w, sgu_b, w_proj_a, shift_b, w_lora_w, w0, a_lora_w, a0, g_lora_w, k_k, k_a, r_k, ln_x_w, ln_x_b, w_proj_b, w_out, g_ffn, w_ffn1, w_ffn2, g_final, loss_target, m_g_mix, m_w_in, m_sgu_ln_w, m_sgu_ln_b, m_sgu_w, m_sgu_b, m_w_proj_a, m_shift_b, m_w_lora_w, m_w0, m_a_lora_w, m_a0, m_g_lora_w, m_k_k, m_k_a, m_r_k, m_ln_x_w, m_ln_x_b, m_w_proj_b, m_w_out, m_g_ffn, m_w_ffn1, m_w_ffn2, m_g_final, v_g_mix, v_w_in, v_sgu_ln_w, v_sgu_ln_b, v_sgu_w, v_sgu_b, v_w_proj_a, v_shift_b, v_w_lora_w, v_w0, v_a_lora_w, v_a0, v_g_lora_w, v_k_k, v_k_a, v_r_k, v_ln_x_w, v_ln_x_b, v_w_proj_b, v_w_out, v_g_ffn, v_w_ffn1, v_w_ffn2, v_g_final):
    given = dict(zip(WEIGHTS, (g_mix, w_in, sgu_ln_w, sgu_ln_b, sgu_w, sgu_b, w_proj_a, shift_b, w_lora_w, w0, a_lora_w, a0, g_lora_w, k_k, k_a, r_k, ln_x_w, ln_x_b, w_proj_b, w_out, g_ffn, w_ffn1, w_ffn2, g_final)))
    mom_m = dict(zip(WEIGHTS, (m_g_mix, m_w_in, m_sgu_ln_w, m_sgu_ln_b, m_sgu_w, m_sgu_b, m_w_proj_a, m_shift_b, m_w_lora_w, m_w0, m_a_lora_w, m_a0, m_g_lora_w, m_k_k, m_k_a, m_r_k, m_ln_x_w, m_ln_x_b, m_w_proj_b, m_w_out, m_g_ffn, m_w_ffn1, m_w_ffn2, m_g_final)))
    mom_v = dict(zip(WEIGHTS, (v_g_mix, v_w_in, v_sgu_ln_w, v_sgu_ln_b, v_sgu_w, v_sgu_b, v_w_proj_a, v_shift_b, v_w_lora_w, v_w0, v_a_lora_w, v_a0, v_g_lora_w, v_k_k, v_k_a, v_r_k, v_ln_x_w, v_ln_x_b, v_w_proj_b, v_w_out, v_g_ffn, v_w_ffn1, v_w_ffn2, v_g_final)))
    chip = 2 * lax.axis_index("x") + lax.axis_index("y")

    def local_block(tree, n):
        return tree[n] if n == "g_final" else tree[n][0]

    sb = local_block(given, "shift_b")
    lo_part = lambda z: (z - z.astype(BF16).astype(F32)).astype(BF16)
    row_form = lambda tree: (lambda n: _row_form(n, local_block(tree, n)))
    tile16 = lambda z: jnp.pad(z, ((0, 16 - z.shape[0]), (0, HALF_W - z.shape[1])))
    sb_tiles = [tile16(f(sb[:, lanes])) for f in (lambda z: z.astype(BF16), lo_part)
                for lanes in (slice(0, HALF_W), slice(HALF_W, None))]
    tails = [[_half_piece(n, lo_part(local_block(given, n)), h) for n in LORAS] + sb_tiles for h in range(2)]
    pack_w = jnp.stack([_pack_half(EARLY, row_form(given), h, BF16, tails[h]) for h in range(2)])
    gathered, gathered_token = gather_shards(pack_w)
    gathered = lax.dynamic_update_index_in_dim(gathered, pack_w, chip, 0)
    pack_late = jnp.stack([_pack_half(LATE, row_form(given), h, BF16) for h in range(2)])
    late_state, late_token = split_start("gather_start", _gather_copies, 3, pack_late, (N_CHIPS,) + pack_late.shape,
                                         gathered_token)

    def whole(group, got, own):
        half = lambda n, j, h: jnp.where(chip == j, _piece(own[h], n), _piece(got[j, h], n))
        shard = lambda n, j: _join_halves(n, half(n, j, 0), half(n, j, 1))
        return {n: jnp.concatenate([shard(n, j) for j in range(N_CHIPS)],
                                   axis=0 if n == "w_in" else SHARD_AXIS[n]) for n in group}

    w = whole(EARLY, gathered, pack_w)
    def late_weights(after):
        got = gather_forward(split_wait("gather_wait", _gather_copies, late_state, after)[1])
        return {"late_packs": (got, pack_late)}
    off = LO_OFF
    for n in LORAS:
        r, cols = PIECE_ROWS[n], SHARD_SHAPE[n][1]
        lo = jnp.concatenate([jnp.concatenate([gathered[j, 0, off:off + r, :cols], gathered[j, 1, off:off + r, :cols]],
                                              axis=0) for j in range(N_CHIPS)], axis=1)
        w[n] = w[n].astype(F32) + lo.astype(F32)
        off += r
    sb_tile = lambda j, t, lanes: gathered[j, 0, off + 16 * t:off + 16 * t + 2, :lanes].astype(F32)
    rest = SHIFT_SHARD[1] - HALF_W
    w["shift_b"] = jnp.concatenate(
        [jnp.concatenate([sb_tile(j, 0, HALF_W) + sb_tile(j, 2, HALF_W), sb_tile(j, 1, rest) + sb_tile(j, 3, rest)],
                         axis=1) for j in range(N_CHIPS)], axis=1)
    for n in SMALL:
        w[n] = local_block(given, n).reshape(SMALL_SHAPE[n])

    def pair_start(g_pack, tag):
        return split_start("reduce_pair_start_" + tag, _pair_copies, N_CHIPS, g_pack, (N_CHIPS,) + g_pack.shape[2:])

    def pair_finish(state, after, tag):
        g_pack, got = split_wait("reduce_pair_wait_" + tag, _pair_copies, state, after)
        return pair_sum(g_pack, got, tag, tm=got.shape[1] // 2)

    pack_early = lambda g: jnp.stack([jnp.stack([_pack_half(EARLY, lambda n: _grad_row_form(n, g[n], j), h, F32)
                                                 for h in range(2)]) for j in range(N_CHIPS)])
    loss, grad_x, grads, (late_part, late_slots), early_state = local_step(
        x[0], loss_target[0], w, late_token, late_weights, pair_start, pair_finish, pack_early)

    early_part, early_part16 = pair_finish(early_state, grad_x, "early")
    s_pack = _pack_rows([grads[n] for n in SMALL] + [grads["shift_b"], loss.reshape(1, 1)], SMALL_ROWS, F32)
    chips_state, token = split_start("reduce_chips_start", _chip_copies, 3, early_part16, early_part16.shape)
    out_g, out_d, out_m, out_v = {}, {}, {}, {}

    def finish(group, tag, part, slots):
        half_sum = sum_with_own(part, slots, chip, token, tm=part.shape[1] // 2, name="chip_sum_" + tag)
        other_half = exchange_halves(half_sum, tag)
        for n in group:
            res = adamw_weight(n, half_sum, other_half,
                               *[_row_form(n, local_block(t, n)) for t in (given, mom_m, mom_v)])
            for tree, z in zip((out_g, out_d, out_m, out_v), res):
                tree[n] = _row_form(n, z)

    finish(LATE, "late", late_part, late_slots)

    small_shapes = [SMALL_SHAPE[n] for n in SMALL]
    g_small = sum_all(s_pack, token)
    *g_parts, loss = _unpack_rows(g_small, small_shapes + [(2, N_RWKV), ()])
    out_g.update(zip(SMALL, g_parts[:-1]))
    g_sb = lax.dynamic_slice_in_dim(g_parts[-1], chip * SHIFT_SHARD[1], SHIFT_SHARD[1], axis=1)
    out_g["shift_b"] = g_sb
    names = SMALL + ["shift_b"]
    native = lambda tree: [local_block(tree, n).reshape(SMALL_SHAPE.get(n, SHIFT_SHARD)) for n in names]
    small_res = adamw_many(g_parts[:-1] + [g_sb], native(given), native(mom_m), native(mom_v))
    for tree, res in zip((out_d, out_m, out_v), small_res):
        tree.update(zip(names, res))

    after = (out_v["w_out"], out_v["sgu_w"])
    early_slots = split_wait("reduce_chips_wait", _chip_copies, chips_state,
                             jnp.concatenate([z.reshape(-1)[:8] for z in after]))[1]
    finish(EARLY, "early", early_part, early_slots)

    def block_of(tree, n):
        return tree[n].reshape(given[n].shape)

    return (loss, grad_x[None], *[block_of(out_g, n) for n in WEIGHTS], *[block_of(out_d, n) for n in WEIGHTS],
            *[block_of(out_m, n) for n in WEIGHTS], *[block_of(out_v, n) for n in WEIGHTS])
```

```python
import functools

import jax
import jax.numpy as jnp
from jax import lax
from jax.experimental import pallas as pl
from jax.experimental.pallas import tpu as pltpu

F32 = jnp.float32
BF16 = jnp.bfloat16

D_MODEL = 1024
N_HEADS = 16
HEAD = 64
SCAN_CHUNK = 64

VMEM_LIMIT = 56 * 1024 * 1024


_BDIMS = {
    "nn": (((2,), (1,)), ((0,), (0,))),
    "nt": (((2,), (2,)), ((0,), (0,))),
    "tn": (((1,), (1,)), ((0,), (0,))),
}


def _raw_bdot(x, y, mode, fine):
    if fine:
        return lax.dot_general(x, y, _BDIMS[mode], precision=lax.Precision.HIGH, preferred_element_type=F32)
    return lax.dot_general(x.astype(BF16), y.astype(BF16), _BDIMS[mode], preferred_element_type=F32)


@functools.partial(jax.custom_vjp, nondiff_argnums=(2, 3))
def bdot(x, y, mode, fine=True):
    return _raw_bdot(x, y, mode, fine)


def _bdot_fwd(x, y, mode, fine):
    return _raw_bdot(x, y, mode, fine), (x, y)


def _bdot_bwd(mode, fine, res, g):
    x, y = res
    if mode == "nn":
        return bdot(g, y, "nt", fine), bdot(x, g, "tn", fine)
    if mode == "nt":
        return bdot(g, y, "nn", fine), bdot(g, x, "tn", fine)
    return bdot(y, g, "nt", fine), bdot(x, g, "nn", fine)


bdot.defvjp(_bdot_fwd, _bdot_bwd)


def _scan_chunk(S0, r, lw, k, v, a, b):
    nh, lc, _ = r.shape
    ti = lax.broadcasted_iota(jnp.int32, (lc, lc), 0)
    si = lax.broadcasted_iota(jnp.int32, (lc, lc), 1)
    incl = (si <= ti).astype(F32)
    strict = (si < ti).astype(F32)
    eye = (si == ti).astype(F32)
    cl = bdot(jnp.broadcast_to(incl, (nh, lc, lc)), lw, "nn")
    cl_last = cl[:, lc - 1:lc, :]
    g_last = jnp.exp(cl_last - cl)
    at = a * jnp.exp(cl - lw)
    bt = b * jnp.exp(-cl)
    kt = k * jnp.exp(-cl)
    rt = r * jnp.exp(cl)
    ar = jnp.concatenate([at, rt], axis=1)
    ar_b = bdot(ar, bt, "nt", False)
    ar_k = bdot(ar, kt, "nt", False)
    m_ab, m_rb = ar_b[:, :lc] * strict, ar_b[:, lc:] * incl
    m_ak, m_rk = ar_k[:, :lc] * strict, ar_k[:, lc:] * incl
    x = eye + m_ab
    p = bdot(m_ab, m_ab, "nn", False)
    n = 2
    while n * 2 < lc:
        px = bdot(jnp.concatenate([p, x], axis=1), p, "nn", False)
        p = px[:, :lc]
        x = x + px[:, lc:]
        n *= 2
    x = x + bdot(x, p, "nn", False)
    ar_s = bdot(ar, S0, "nt", False)
    akrk_v = bdot(jnp.concatenate([m_ak, m_rk], axis=1), v, "nn", False)
    u = bdot(x, ar_s[:, :lc] + akrk_v[:, :lc], "nn", False)
    o = ar_s[:, lc:] + bdot(m_rb, u, "nn", False) + akrk_v[:, lc:]
    s_last = S0 * jnp.exp(cl_last) + bdot(jnp.concatenate([u, v], axis=1),
                                          jnp.concatenate([b * g_last, k * g_last], axis=1), "tn", False)
    return o, s_last


def _split_heads(z):
    return jnp.stack([z[:, HEAD * h:HEAD * (h + 1)] for h in range(N_HEADS)], axis=0)


def _merge_heads(z):
    return jnp.concatenate([z[h] for h in range(N_HEADS)], axis=1)


def _scan_specs(t, ops, rev):
    nc = t // SCAN_CHUNK
    row = (lambda c: nc - 1 - c) if rev else (lambda c: c)
    specs = [pl.BlockSpec((SCAN_CHUNK, D_MODEL), lambda c, cb=cb: (row(c), cb)) for _, cb in ops]
    state = pl.BlockSpec((1, N_HEADS, HEAD, HEAD), lambda c: (row(c), 0, 0, 0))
    return nc, specs, state


def scan_fwd(ops):
    t = ops[0][0].shape[0]
    nc, specs, state = _scan_specs(t, ops, False)

    def body(r_ref, lw_ref, k_ref, v_ref, a_ref, b_ref, o_ref, s0_ref, s_scr):
        @pl.when(pl.program_id(0) == 0)
        def _():
            s_scr[...] = jnp.zeros_like(s_scr)

        s0 = s_scr[...]
        s0_ref[0] = s0
        o, s_last = _scan_chunk(s0, *[_split_heads(z[...]) for z in (r_ref, lw_ref, k_ref, v_ref, a_ref, b_ref)])
        o_ref[...] = _merge_heads(o)
        s_scr[...] = s_last

    return pl.pallas_call(
        body,
        name="scan_fwd",
        grid=(nc,),
        in_specs=specs,
        out_specs=[pl.BlockSpec((SCAN_CHUNK, D_MODEL), lambda c: (c, 0)), state],
        out_shape=[jax.ShapeDtypeStruct((t, D_MODEL), F32), jax.ShapeDtypeStruct((nc, N_HEADS, HEAD, HEAD), F32)],
        scratch_shapes=[pltpu.VMEM((N_HEADS, HEAD, HEAD), F32)],
        compiler_params=_cparams(1),
    )(*[a for a, _ in ops])


def scan_bwd(ops, s0s, do, part):
    t = ops[0][0].shape[0]
    nc, specs, state = _scan_specs(t, ops + [(do, 0)], True)

    def body(r_ref, lw_ref, k_ref, v_ref, a_ref, b_ref, do_ref, s0_ref, part_ref, *rest):
        out_refs, slots_ref, ds_scr, send_sems, recv_sems = rest[:6], rest[6], rest[7], rest[8], rest[9]
        step = pl.program_id(0)
        x, y, c = _coords()
        me = 2 * x + y
        chips = _other_chips(x, y)
        sends = [_remote(part_ref.at[2 * cx + cy], slots_ref.at[me], send_sems, recv_sems, k, (cx, cy, c))
                 for k, (cx, cy) in enumerate(chips)]

        @pl.when(step == 0)
        def _():
            ds_scr[...] = jnp.zeros_like(ds_scr)
            for cp in sends:
                cp.start()

        _, vjp = jax.vjp(_scan_chunk, s0_ref[0],
                         *[_split_heads(z[...]) for z in (r_ref, lw_ref, k_ref, v_ref, a_ref, b_ref)])
        grads = vjp((_split_heads(do_ref[...]), ds_scr[...]))
        for o_ref, g in zip(out_refs, grads[1:]):
            o_ref[...] = _merge_heads(g)
        ds_scr[...] = grads[0]

        @pl.when(step == nc - 1)
        def _():
            for k, (cx, cy) in enumerate(chips):
                _remote(part_ref.at[me], slots_ref.at[2 * cx + cy], send_sems, recv_sems, k, (cx, cy, c)).wait_recv()
            for cp in sends:
                cp.wait_send()

    return pl.pallas_call(
        body,
        name="scan_bwd",
        grid=(nc,),
        in_specs=specs + [state, _ANY],
        out_specs=[pl.BlockSpec((SCAN_CHUNK, D_MODEL), lambda c: (nc - 1 - c, 0))] * 6 + [_ANY],
        out_shape=[jax.ShapeDtypeStruct((t, D_MODEL), F32)] * 6 + [jax.ShapeDtypeStruct(part.shape, part.dtype)],
        scratch_shapes=[pltpu.VMEM((N_HEADS, HEAD, HEAD), F32), pltpu.SemaphoreType.DMA((3,)),
                        pltpu.SemaphoreType.DMA((3,))],
        compiler_params=_cparams(1),
    )(*[a for a, _ in ops], do, s0s, part)


_MDIMS = {
    "nn": (((1,), (0,)), ((), ())),
    "nt": (((1,), (1,)), ((), ())),
    "tn": (((0,), (0,)), ((), ())),
}


def _raw_mdot(x, y, mode, exact):
    if exact:
        return lax.dot_general(x, y, _MDIMS[mode], precision=lax.Precision.HIGH, preferred_element_type=F32)
    return lax.dot_general(x.astype(BF16), y.astype(BF16), _MDIMS[mode], preferred_element_type=F32)


@functools.partial(jax.custom_vjp, nondiff_argnums=(2, 3))
def mdot(x, y, mode, exact):
    return _raw_mdot(x, y, mode, exact)


def _mdot_fwd(x, y, mode, exact):
    return _raw_mdot(x, y, mode, exact), (x, y)


def _mdot_bwd(mode, exact, res, g):
    x, y = res
    if mode == "nn":
        return mdot(g, y, "nt", exact), mdot(x, g, "tn", exact)
    if mode == "nt":
        return mdot(g, y, "nn", exact), mdot(g, x, "tn", exact)
    return mdot(y, g, "nt", exact), mdot(x, g, "nn", exact)


mdot.defvjp(_mdot_fwd, _mdot_bwd)


def _seg_ones():
    i = lax.broadcasted_iota(jnp.int32, (256, 256), 0) // HEAD
    j = lax.broadcasted_iota(jnp.int32, (256, 256), 1) // HEAD
    return (i == j).astype(BF16)


@jax.custom_vjp
def segsum(x):
    bd = _seg_ones()
    hi = x.astype(BF16)
    lo = (x - hi.astype(F32)).astype(BF16)
    cols = []
    for j in range(x.shape[1] // 256):
        sl = slice(256 * j, 256 * (j + 1))
        cols.append(jnp.dot(hi[:, sl], bd, preferred_element_type=F32)
                    + jnp.dot(lo[:, sl], bd, preferred_element_type=F32))
    return jnp.concatenate(cols, axis=1)


segsum.defvjp(lambda x: (segsum(x), None), lambda _, g: (segsum(g),))


NORM_EPS = 1e-6
LN_EPS = 1e-5
GN_EPS = 64e-5
SGU_CHUNK = 128
SGU_GROUPS = 8


def _rms(x, g):
    return x * lax.rsqrt(jnp.mean(x * x, axis=-1, keepdims=True) + NORM_EPS) * g


def f_norm_in(x, g):
    return _rms(x, g), x


def f_sgu(p, ln_w, ln_b, sw, sbt):
    tm = p.shape[0]
    z = 0.5 * p * (1.0 + lax.erf(p * 0.7071067811865476))
    u, v = z[:, :D_MODEL], z[:, D_MODEL:]
    mu = jnp.mean(v, axis=-1, keepdims=True)
    d = v - mu
    vn = d * lax.rsqrt(jnp.mean(d * d, axis=-1, keepdims=True) + LN_EPS) * ln_w + ln_b
    ii = lax.broadcasted_iota(jnp.int32, (SGU_CHUNK, SGU_CHUNK), 0)
    jj = lax.broadcasted_iota(jnp.int32, (SGU_CHUNK, SGU_CHUNK), 1)
    mask = (jj <= ii).astype(F32)
    gi = lax.broadcasted_iota(jnp.int32, (SGU_GROUPS, D_MODEL), 0)
    ci = lax.broadcasted_iota(jnp.int32, (SGU_GROUPS, D_MODEL), 1) // SGU_CHUNK
    bias = mdot(sbt, (gi == ci).astype(F32), "nn", True)
    rows = []
    for c in range(tm // SGU_CHUNK):
        cols = []
        for g in range(SGU_GROUPS):
            blk = vn[c * SGU_CHUNK:(c + 1) * SGU_CHUNK, g * SGU_CHUNK:(g + 1) * SGU_CHUNK]
            cols.append(mdot(sw[g] * mask, blk, "nn", False))
        rows.append(jnp.concatenate(cols, axis=1) + bias)
    return (u * jnp.concatenate(rows, axis=0),)


def _softplus(x):
    return jnp.maximum(x, 0.0) + jnp.log1p(jnp.exp(-jnp.abs(x)))


def f_pre(q, wl, w0, al, a0, gl, k_k, k_a):
    qr, qk, qv, ql = q[:, :1024], q[:, 1024:2048], q[:, 2048:3072], q[:, 3072:]
    return _f_pre(qr, qk, qv, ql, wl, w0, al, a0, gl, k_k, k_a)


def _f_pre(qr, qk, qv, ql, wl, w0, al, a0, gl, k_k, k_a):
    xw, xa, xg = ql[:, :128], ql[:, 128:256], ql[:, 256:512]
    wr = -_softplus(-(w0 + mdot(jnp.tanh(xw), wl, "nn", False))) - 0.5
    lw = -jnp.exp(wr)
    aa = jax.nn.sigmoid(a0 + mdot(xa, al, "nn", False))
    g = mdot(jax.nn.sigmoid(xg), gl, "nn", False)
    kkr = qk * k_k
    kk = kkr / jnp.maximum(jnp.sqrt(segsum(kkr * kkr)), 1e-12)
    kp = qk * (1.0 + (aa - 1.0) * k_a)
    return qr, lw, kp, qv, -kk, kk * aa, g, qr, kp, qv


def f_post(o, r, kp, v, g, lnw, lnb, rk):
    mu = segsum(o) * (1.0 / HEAD)
    d = o - mu
    gn = d * lax.rsqrt(segsum(d * d) * (1.0 / HEAD) + GN_EPS)
    return ((gn * lnw + lnb + segsum(r * kp * rk) * v) * g,)


def f_mix(ya, yb, ga, gb):
    return (jax.nn.sigmoid(ga) * ya + jax.nn.sigmoid(gb) * yb,)


def f_ffn_in(h1, g):
    return _rms(h1, g), h1


def f_final(h1, m3, tgt, g):
    y = _rms(h1 + m3, g)
    err = jnp.square(y - tgt)
    return 0.5 * jnp.sum(jnp.mean(err, axis=-1))


def _cparams(n_grid):
    return pltpu.CompilerParams(dimension_semantics=("arbitrary",) * n_grid, vmem_limit_bytes=VMEM_LIMIT)


def _tile_spec(tm, w, cb):
    return pl.BlockSpec((tm, w), lambda i: (i, cb))


def _const_spec(c):
    nd = c.ndim
    return pl.BlockSpec(c.shape, lambda i: (0,) * nd)


def ew_call(fn, tiled, consts, outs, *, tm, name):
    t = tiled[0][0].shape[0]
    n_t, n_c = len(tiled), len(consts)

    def body(*refs):
        tv = [r[...].astype(F32) for r in refs[:n_t]]
        cv = [r[...] for r in refs[n_t:n_t + n_c]]
        res = fn(*tv, *cv)
        for o_ref, val in zip(refs[n_t + n_c:], res):
            o_ref[...] = val.astype(o_ref.dtype)

    return pl.pallas_call(
        body,
        name=name,
        grid=(t // tm,),
        in_specs=[_tile_spec(tm, w, cb) for _, w, cb in tiled] + [_const_spec(c) for c in consts],
        out_specs=[_tile_spec(tm, w, 0) for w, _ in outs],
        out_shape=[jax.ShapeDtypeStruct((t, w), dt) for w, dt in outs],
        compiler_params=_cparams(1),
    )(*[a for a, _, _ in tiled], *consts)


def ew_vjp_call(fn, tiled, consts, cots, d_tiled, d_consts, *, tm, name):
    t = tiled[0][0].shape[0]
    n_t, n_c, n_g = len(tiled), len(consts), len(cots)
    dt_list = [(i, dt) for i, dts in enumerate(d_tiled) for dt in dts]
    dc_list = [i for i, want in enumerate(d_consts) if want]

    def body(*refs):
        tv = [r[...].astype(F32) for r in refs[:n_t]]
        cv = [r[...] for r in refs[n_t:n_t + n_c]]
        gv = tuple(r[...].astype(F32) for r in refs[n_t + n_c:n_t + n_c + n_g])
        out_refs = refs[n_t + n_c + n_g:]
        _, vjp = jax.vjp(fn, *tv, *cv)
        grads = vjp(gv)
        for o_ref, (i, _) in zip(out_refs, dt_list):
            o_ref[...] = grads[i].astype(o_ref.dtype)
        acc_refs = out_refs[len(dt_list):]

        @pl.when(pl.program_id(0) == 0)
        def _():
            for a_ref in acc_refs:
                a_ref[...] = jnp.zeros_like(a_ref)

        for a_ref, i in zip(acc_refs, dc_list):
            a_ref[...] += grads[n_t + i]

    res = pl.pallas_call(
        body,
        name=name,
        grid=(t // tm,),
        in_specs=[_tile_spec(tm, w, cb) for _, w, cb in tiled] + [_const_spec(c) for c in consts]
        + [_tile_spec(tm, w, cb) for _, w, cb in cots],
        out_specs=[_tile_spec(tm, tiled[i][1], 0) for i, _ in dt_list] + [_const_spec(consts[i]) for i in dc_list],
        out_shape=[jax.ShapeDtypeStruct((t, tiled[i][1]), dt) for i, dt in dt_list]
        + [jax.ShapeDtypeStruct(consts[i].shape, F32) for i in dc_list],
        compiler_params=_cparams(1),
    )(*[a for a, _, _ in tiled], *consts, *[a for a, _, _ in cots])
    return res[:len(dt_list)], res[len(dt_list):]


def mm(a, b, mode, *, tm, tn, name, out_dtypes=(F32,), epi=None, extras=(), into=None):
    m = a.shape[1] if mode == "tn" else a.shape[0]
    kd = a.shape[0] if mode == "tn" else a.shape[1]
    n = b.shape[0] if mode == "nt" else b.shape[1]
    tm, tn = min(tm, m), min(tn, n)
    if mode == "nn":
        a_spec = pl.BlockSpec((tm, kd), lambda i, j: (i, 0))
        b_spec = pl.BlockSpec((kd, tn), lambda i, j: (0, j))
    elif mode == "nt":
        a_spec = pl.BlockSpec((tm, kd), lambda i, j: (i, 0))
        b_spec = pl.BlockSpec((tn, kd), lambda i, j: (j, 0))
    else:
        a_spec = pl.BlockSpec((kd, tm), lambda i, j: (0, i))
        b_spec = pl.BlockSpec((kd, tn), lambda i, j: (0, j))
    n_e = len(extras)
    o_spec = pl.BlockSpec((tm, tn), lambda i, j: (i, j))

    if into is not None:
        buf, place = into

        def body_into(a_ref, b_ref, buf_ref, o_ref):
            o_ref[0, 0] = lax.dot_general(a_ref[...].astype(BF16), b_ref[...].astype(BF16), _MDIMS[mode],
                                          preferred_element_type=F32)

        return pl.pallas_call(
            body_into,
            name=name,
            grid=(m // tm, n // tn),
            in_specs=[a_spec, b_spec, pl.BlockSpec(memory_space=pl.ANY)],
            out_specs=pl.BlockSpec((1, 1, tm, tn), lambda i, j: (*place(i, j), 0)),
            out_shape=jax.ShapeDtypeStruct(buf.shape, F32),
            input_output_aliases={2: 0},
            compiler_params=_cparams(2),
        )(a, b, buf)

    def body(a_ref, b_ref, *refs):
        c = lax.dot_general(a_ref[...].astype(BF16), b_ref[...].astype(BF16), _MDIMS[mode],
                            preferred_element_type=F32)
        res = epi(c, *[r[...] for r in refs[:n_e]]) if epi is not None else (c,)
        for o_ref, val in zip(refs[n_e:], res):
            o_ref[...] = val.astype(o_ref.dtype)

    res = pl.pallas_call(
        body,
        name=name,
        grid=(m // tm, n // tn),
        in_specs=[a_spec, b_spec] + [o_spec] * n_e,
        out_specs=[o_spec] * len(out_dtypes),
        out_shape=[jax.ShapeDtypeStruct((m, n), dt) for dt in out_dtypes],
        compiler_params=_cparams(2),
    )(a, b, *extras)
    return res if len(out_dtypes) > 1 else res[0]


def mm_packed(a, packs, wname, mode, *, tm, name, out_dtypes=(F32,), epi=None, extras=()):
    got, own = packs
    m, kd = a.shape
    n_chips = got.shape[0]
    piece_r = PIECE_ROWS[wname]
    assert PIECE_OFF[wname] % piece_r == 0
    r0 = PIECE_OFF[wname] // piece_r
    by_cols = SHARD_AXIS[wname] == 1
    pos = lambda j, h: (0, 2 * HALF_W * j + HALF_W * h) if by_cols else (piece_r * j, HALF_W * h)
    w_rows, w_cols = (piece_r, n_chips * 2 * HALF_W) if by_cols else (n_chips * piece_r, 2 * HALF_W)
    n = w_cols if mode == "nn" else w_rows
    assert kd == (w_rows if mode == "nn" else w_cols)
    pieces = [(j, h) for j in range(n_chips) for h in range(2)]
    out_col = lambda j, h: pos(j, h)[1] if mode == "nn" else pos(j, h)[0]
    n_e = len(extras)

    def body(a_ref, *refs):
        got_refs, own_refs = refs[:len(pieces)], refs[len(pieces):len(pieces) + 2]
        ex_refs, o_refs = refs[len(pieces) + 2:][:n_e], refs[len(pieces) + 2 + n_e:]
        chip = 2 * lax.axis_index("x") + lax.axis_index("y")
        for col in sorted({out_col(j, h) for j, h in pieces}):
            c = None
            for idx, (j, h) in enumerate(pieces):
                if out_col(j, h) != col:
                    continue
                blk = jnp.where(chip == j, own_refs[h][0], got_refs[idx][0, 0])
                r_lo, c_lo = pos(j, h)
                if mode == "nn":
                    part = lax.dot_general(a_ref[:, r_lo:r_lo + piece_r], blk, _MDIMS["nn"],
                                           preferred_element_type=F32)
                else:
                    part = lax.dot_general(a_ref[:, c_lo:c_lo + HALF_W], blk, _MDIMS["nt"],
                                           preferred_element_type=F32)
                c = part if c is None else c + part
            width = c.shape[1]
            res = epi(c, *[r[:, col:col + width] for r in ex_refs]) if epi is not None else (c,)
            for o_ref, val in zip(o_refs, res):
                o_ref[:, col:col + width] = val.astype(o_ref.dtype)

    row_spec = lambda width: pl.BlockSpec((tm, width), lambda i: (i, 0))
    got_specs = [pl.BlockSpec((1, 1, piece_r, HALF_W), lambda i, j=j, h=h: (j, h, r0, 0)) for j, h in pieces]
    own_specs = [pl.BlockSpec((1, piece_r, HALF_W), lambda i, h=h: (h, r0, 0)) for h in range(2)]
    res = pl.pallas_call(
        body,
        name=name,
        grid=(m // tm,),
        in_specs=[row_spec(kd)] + got_specs + own_specs + [row_spec(n)] * n_e,
        out_specs=[row_spec(n)] * len(out_dtypes),
        out_shape=[jax.ShapeDtypeStruct((m, n), dt) for dt in out_dtypes],
        compiler_params=_cparams(1),
    )(a, *([got] * len(pieces)), own, own, *extras)
    return res if len(out_dtypes) > 1 else res[0]


RWKV_COL0 = 4096
RWKV_WIDTH = 3584
SHIFT_BLK = 512


def _shift_down(p, prev_row):
    rows = lax.broadcasted_iota(jnp.int32, p.shape, 0)
    return jnp.where(rows == 0, prev_row, pltpu.roll(p, 1, 0))


def shiftmix_fwd(p_all, sbp, *, tm):
    t = p_all.shape[0]
    tm = min(tm, t)
    c0 = RWKV_COL0 // SHIFT_BLK
    hb = tm // 8

    def body(p_ref, halo_ref, sb_ref, q_ref):
        p = p_ref[...]
        prev = jnp.where(pl.program_id(0) == 0, 0.0, halo_ref[7:8, :])
        q_ref[...] = p * sb_ref[0:1, :] + _shift_down(p, prev) * sb_ref[1:2, :]

    return pl.pallas_call(
        body,
        name="shiftmix_fwd",
        grid=(t // tm, RWKV_WIDTH // SHIFT_BLK),
        in_specs=[
            pl.BlockSpec((tm, SHIFT_BLK), lambda i, j: (i, c0 + j)),
            pl.BlockSpec((8, SHIFT_BLK), lambda i, j: (jnp.maximum(i * hb - 1, 0), c0 + j)),
            pl.BlockSpec((2, SHIFT_BLK), lambda i, j: (0, j)),
        ],
        out_specs=pl.BlockSpec((tm, SHIFT_BLK), lambda i, j: (i, j)),
        out_shape=jax.ShapeDtypeStruct((t, RWKV_WIDTH), F32),
        compiler_params=_cparams(2),
    )(p_all, p_all, sbp)


def shiftmix_bwd(dq, col0, p_all, sbp, *, tm, name):
    t, w = dq.shape
    n_i = t // tm
    hb = tm // 8
    cq = col0 // SHIFT_BLK
    cp = (RWKV_COL0 + col0) // SHIFT_BLK

    def body(dq_ref, dqn_ref, p_ref, ph_ref, sb_ref, dp_ref, dsb_ref):
        i = pl.program_id(1)
        dq_t = dq_ref[...]
        rows = lax.broadcasted_iota(jnp.int32, dq_t.shape, 0)
        nxt = jnp.where(i == n_i - 1, 0.0, dqn_ref[0:1, :])
        up = jnp.where(rows == tm - 1, nxt, pltpu.roll(dq_t, tm - 1, 0))
        dp_ref[...] = (dq_t * sb_ref[0:1, :] + up * sb_ref[1:2, :]).astype(dp_ref.dtype)
        p = p_ref[...]
        prev = jnp.where(i == 0, 0.0, ph_ref[7:8, :])
        s0 = jnp.sum(dq_t * p, axis=0, keepdims=True)
        s1 = jnp.sum(dq_t * _shift_down(p, prev), axis=0, keepdims=True)
        two = lax.broadcasted_iota(jnp.int32, (2, SHIFT_BLK), 0)

        @pl.when(i == 0)
        def _():
            dsb_ref[...] = jnp.zeros_like(dsb_ref)

        dsb_ref[...] += jnp.where(two == 0, s0, s1)

    return pl.pallas_call(
        body,
        name=name,
        grid=(w // SHIFT_BLK, n_i),
        in_specs=[
            pl.BlockSpec((tm, SHIFT_BLK), lambda j, i: (i, j)),
            pl.BlockSpec((8, SHIFT_BLK), lambda j, i: (jnp.minimum((i + 1) * hb, t // 8 - 1), j)),
            pl.BlockSpec((tm, SHIFT_BLK), lambda j, i: (i, cp + j)),
            pl.BlockSpec((8, SHIFT_BLK), lambda j, i: (jnp.maximum(i * hb - 1, 0), cp + j)),
            pl.BlockSpec((2, SHIFT_BLK), lambda j, i: (0, cq + j)),
        ],
        out_specs=[
            pl.BlockSpec((tm, SHIFT_BLK), lambda j, i: (i, j)),
            pl.BlockSpec((2, SHIFT_BLK), lambda j, i: (0, j)),
        ],
        out_shape=[jax.ShapeDtypeStruct((t, w), BF16), jax.ShapeDtypeStruct((2, w), F32)],
        compiler_params=_cparams(2),
    )(dq, dq, p_all, p_all, sbp)


def final_call(h1, m3, tgt, g_final, *, tm):
    t = h1.shape[0]

    def body(h1_ref, m3_ref, tgt_ref, g_ref, dh_ref, dhb_ref, dg_ref, loss_ref):
        loss, vjp = jax.vjp(f_final, h1_ref[...], m3_ref[...], tgt_ref[...], g_ref[...])
        dh, _, _, dg = vjp(jnp.ones((), F32))
        dh_ref[...] = dh
        dhb_ref[...] = dh.astype(BF16)

        @pl.when(pl.program_id(0) == 0)
        def _():
            dg_ref[...] = jnp.zeros_like(dg_ref)
            loss_ref[...] = jnp.zeros_like(loss_ref)

        dg_ref[...] += dg
        loss_ref[...] += jnp.full(loss_ref.shape, loss, F32)

    tile = _tile_spec(tm, D_MODEL, 0)
    return pl.pallas_call(
        body,
        name="final_loss",
        grid=(t // tm,),
        in_specs=[tile, tile, tile, _const_spec(g_final)],
        out_specs=[tile, tile, _const_spec(g_final), pl.BlockSpec((8, 128), lambda i: (0, 0))],
        out_shape=[jax.ShapeDtypeStruct((t, D_MODEL), F32), jax.ShapeDtypeStruct((t, D_MODEL), BF16),
                   jax.ShapeDtypeStruct(g_final.shape, F32), jax.ShapeDtypeStruct((8, 128), F32)],
        compiler_params=_cparams(1),
    )(h1, m3, tgt, g_final)


N_SGU = 2048
N_RWKV = 3360
LORA_W, LORA_A, LORA_G = 64, 64, 160


def _pad_rwkv_cols(z):
    zero = lambda n: jnp.zeros(z.shape[:-1] + (n,), z.dtype)
    return jnp.concatenate([z[..., :3072], z[..., 3072:3136], zero(64), z[..., 3136:3200], zero(64),
                            z[..., 3200:3360], zero(96)], axis=-1)


def _unpad_rwkv_cols(z):
    return jnp.concatenate([z[..., :3072], z[..., 3072:3136], z[..., 3200:3264], z[..., 3328:3488]], axis=-1)


def _pad_win_rows(wt):
    z = wt[N_SGU:N_SGU + N_RWKV]
    zero = lambda n: jnp.zeros((n, wt.shape[1]), wt.dtype)
    return jnp.concatenate([wt[:N_SGU], wt[N_SGU + N_RWKV:], z[:3072], z[3072:3136], zero(64), z[3136:3200], zero(64),
                            z[3200:3360], zero(96)], axis=0)


def _unpad_win_rows(wt):
    z = wt[RWKV_COL0:]
    return jnp.concatenate([wt[:N_SGU], z[:3072], z[3072:3136], z[3200:3264], z[3328:3488], wt[N_SGU:RWKV_COL0]],
                           axis=0)


def _pad_rows(w, n):
    return jnp.concatenate([w, jnp.zeros((n - w.shape[0],) + w.shape[1:], w.dtype)], axis=0)


def _relu2_epi(c):
    return c, jnp.square(jnp.maximum(c, 0.0))


def _relu2_bwd_epi(c, hid):
    return (c * (2.0 * jnp.maximum(hid.astype(F32), 0.0)),)


def _add_epi(c, x):
    return (c + x,)


def _pre_fwd(*args):
    res = f_pre(*args)
    return res[1], res[2], res[4], res[5], res[6]


def local_step(x, tgt, w, late_token, late_weights, pair_start, pair_finish, pack_early):
    d = D_MODEL
    win_pt = _pad_win_rows(w["w_in"])
    sbp = _pad_rwkv_cols(w["shift_b"])
    wl = _pad_rows(w["w_lora_w"], 128)
    al = _pad_rows(w["a_lora_w"], 128)
    gl = _pad_rows(w["g_lora_w"], 256)
    sbt = w["sgu_b"].T

    (a_bf,) = ew_call(lambda x_, g_: (f_norm_in(x_, g_)[0],), [(x, d, 0)], [w["g_mix"] + late_token[:1, :1]],
                      [(d, BF16)], tm=512, name="norm_in")
    p_all = mm(a_bf, win_pt, "nt", tm=2048, tn=1280, name="mm_in")
    sgu_t = [(p_all, 2 * d, 0)]
    sgu_c = [w["sgu_ln_w"], w["sgu_ln_b"], w["sgu_w"], sbt]
    (s_bf,) = ew_call(f_sgu, sgu_t, sgu_c, [(d, BF16)], tm=512, name="sgu_fwd")
    q = shiftmix_fwd(p_all, sbp, tm=2048)
    pre_t = [(q, RWKV_WIDTH, 0)]
    pre_c = [wl, w["w0"], al, w["a0"], gl, w["k_k"], w["k_a"]]
    lw, kp, na, nb, g = ew_call(_pre_fwd, pre_t, pre_c, [(d, F32)] * 5, tm=256, name="rwkv_pre_fwd")
    scan_ops = [(q, 0), (lw, 0), (kp, 0), (q, 2), (na, 0), (nb, 0)]
    o, s0s = scan_fwd(scan_ops)
    w = {**w, **late_weights(o)}
    ya = mm_packed(s_bf, w["late_packs"], "w_proj_a", "nn", tm=1024, name="mm_proj_a")
    post_t = [(o, d, 0), (q, d, 0), (kp, d, 0), (q, d, 2), (g, d, 0)]
    post_c = [w["ln_x_w"], w["ln_x_b"], w["r_k"]]
    (ob_bf,) = ew_call(f_post, post_t, post_c, [(d, BF16)], tm=512, name="rwkv_post_fwd")
    yb = mm_packed(ob_bf, w["late_packs"], "w_proj_b", "nn", tm=1024, name="mm_proj_b")
    mix_t = [(ya, d, 0), (yb, d, 0), (p_all, d, 2), (p_all, d, 3)]
    (mixed_bf,) = ew_call(f_mix, mix_t, [], [(d, BF16)], tm=512, name="mix_fwd")
    h1 = mm_packed(mixed_bf, w["late_packs"], "w_out", "nn", tm=1024, name="mm_out", epi=_add_epi, extras=(x,))
    (f_bf,) = ew_call(lambda h_, g_: (f_ffn_in(h_, g_)[0],), [(h1, d, 0)], [w["g_ffn"]], [(d, BF16)], tm=512,
                      name="ffn_norm")
    hid, act_bf = mm_packed(f_bf, w["late_packs"], "w_ffn1", "nn", tm=512, name="mm_ffn1", out_dtypes=(BF16, BF16),
                            epi=_relu2_epi)
    m3 = mm_packed(act_bf, w["late_packs"], "w_ffn2", "nn", tm=512, name="mm_ffn2")
    dh2, dh2_bf, dg_final, loss = final_call(h1, m3, tgt, w["g_final"], tm=512)

    dhid_bf = mm_packed(dh2_bf, w["late_packs"], "w_ffn2", "nt", tm=512, name="mm_dact", out_dtypes=(BF16,),
                        epi=_relu2_bwd_epi, extras=(hid,))
    late_g = lax.empty((N_CHIPS, 2, pack_rows(LATE), HALF_W), F32)
    late_g = mm(act_bf, dh2_bf, "tn", tm=1024, tn=HALF_W, name="mm_dw_ffn2",
                into=(late_g, lambda i, j: (i, j, PIECE_OFF["w_ffn2"] // 1024)))
    df = mm_packed(dhid_bf, w["late_packs"], "w_ffn1", "nt", tm=512, name="mm_df")
    late_g = mm(f_bf, dhid_bf, "tn", tm=1024, tn=HALF_W, name="mm_dw_ffn1",
                into=(late_g, lambda i, j: (j // 2, j % 2, PIECE_OFF["w_ffn1"] // 1024)))
    (dh1, dh1_bf), (dg_ffn,) = ew_vjp_call(f_ffn_in, [(h1, d, 0)], [w["g_ffn"]], [(df, d, 0), (dh2, d, 0)],
                                           [(F32, BF16)], [True], tm=512, name="ffn_norm_bwd")
    dmixed = mm_packed(dh1_bf, w["late_packs"], "w_out", "nt", tm=1024, name="mm_dmixed")
    late_g = mm(mixed_bf, dh1_bf, "tn", tm=256, tn=HALF_W, name="mm_dw_out",
                into=(late_g, lambda i, j: (i, j, PIECE_OFF["w_out"] // 256)))
    (dya_bf, dyb_bf, dga_bf, dgb_bf), _ = ew_vjp_call(f_mix, mix_t, [], [(dmixed, d, 0)], [(BF16,)] * 4, [], tm=256,
                                                      name="mix_bwd")
    dob = mm_packed(dyb_bf, w["late_packs"], "w_proj_b", "nt", tm=1024, name="mm_dob")
    late_g = mm(ob_bf, dyb_bf, "tn", tm=256, tn=HALF_W, name="mm_dw_proj_b",
                into=(late_g, lambda i, j: (i, j, PIECE_OFF["w_proj_b"] // 256)))
    late_g = mm(s_bf, dya_bf, "tn", tm=256, tn=HALF_W, name="mm_dw_proj_a",
                into=(late_g, lambda i, j: (i, j, PIECE_OFF["w_proj_a"] // 256)))
    late_state, late_token = pair_start(late_g, "late")
    post_c_after = [w["ln_x_w"] + late_token[:1, :1]] + post_c[1:]
    (do, dr_p, dkp_p, dv_p, dg), (dlnx_w, dlnx_b, dr_k) = ew_vjp_call(
        f_post, post_t, post_c_after, [(dob, d, 0)], [(F32,)] * 5, [True] * 3, tm=256, name="rwkv_post_bwd")
    late_part, late_part16 = pair_finish(late_state, do, "late")
    *scan_g, late_slots = scan_bwd(scan_ops, s0s, do, late_part16)
    pre_g = [(z, d, 0) for z in scan_g] + [(dg, d, 0), (dr_p, d, 0), (dkp_p, d, 0), (dv_p, d, 0)]
    (dq,), (dwl, dw0, dal, da0, dgl, dk_k, dk_a) = ew_vjp_call(
        f_pre, pre_t, pre_c, pre_g, [(F32,)], [True] * 7, tm=256, name="rwkv_pre_bwd")
    dp_rwkv, dsb = shiftmix_bwd(dq, 0, p_all, sbp, tm=1024, name="shiftmix_bwd")
    ds = mm_packed(dya_bf, w["late_packs"], "w_proj_a", "nt", tm=1024, name="mm_ds")
    (dp_sgu,), (dln_w, dln_b, dsw, dsbt) = ew_vjp_call(f_sgu, sgu_t, sgu_c, [(ds, d, 0)], [(BF16,)], [True] * 4,
                                                       tm=256, name="sgu_bwd")
    dp_all = jnp.concatenate([dp_sgu, dga_bf, dgb_bf, dp_rwkv], axis=1)
    d_in_pt = mm(dp_all, a_bf, "tn", tm=1280, tn=1024, name="mm_dw_in")
    early_state, early_token = pair_start(pack_early({
        "w_in": _unpad_win_rows(d_in_pt), "w_lora_w": dwl[:LORA_W], "a_lora_w": dal[:LORA_A],
        "g_lora_w": dgl[:LORA_G]}), "early")
    da = mm(dp_all, win_pt, "nn", tm=1024, tn=256, name="mm_da")
    g_mix_after = w["g_mix"] + early_token[:1, :1]
    (grad_x,), (dg_mix,) = ew_vjp_call(f_norm_in, [(x, d, 0)], [g_mix_after], [(da, d, 0), (dh1, d, 0)], [(F32,)],
                                       [True], tm=512, name="norm_in_bwd")

    grads = {
        "g_mix": dg_mix, "sgu_ln_w": dln_w, "sgu_ln_b": dln_b, "sgu_w": dsw, "sgu_b": dsbt.T,
        "shift_b": _unpad_rwkv_cols(dsb),
        "w0": dw0, "a0": da0, "k_k": dk_k, "k_a": dk_a, "r_k": dr_k, "ln_x_w": dlnx_w, "ln_x_b": dlnx_b,
        "g_ffn": dg_ffn, "g_final": dg_final,
    }
    return loss[0, 0], grad_x, grads, (late_part, late_slots), early_state


MESH = pl.DeviceIdType.MESH
N_CHIPS = 4
SMALL_ROWS = 160
_ANY = pl.BlockSpec(memory_space=pl.ANY)


def _coords():
    return lax.axis_index("x"), lax.axis_index("y"), lax.axis_index("c")


def _other_chips(x, y):
    return [(1 - x, y), (x, 1 - y), (1 - x, 1 - y)]


def _remote(src, dst, send_sems, recv_sems, k, to):
    return pltpu.make_async_remote_copy(src_ref=src, dst_ref=dst, send_sem=send_sems.at[k], recv_sem=recv_sems.at[k],
                                        device_id=to, device_id_type=MESH)


def gather_shards(pack):
    def body(src_ref, out_ref, token, send_sems, recv_sems):
        x, y, c = _coords()
        me = 2 * x + y
        sib = (x, y, 1 - c)
        chips = _other_chips(x, y)
        first = [_remote(src_ref.at[c], out_ref.at[me, c], send_sems, recv_sems, k, (cx, cy, c))
                 for k, (cx, cy) in enumerate(chips)]
        for cp in first:
            cp.start()
        passed = []
        for k, (cx, cy) in enumerate(chips):
            j = 2 * cx + cy
            _remote(src_ref.at[c], out_ref.at[j, c], send_sems, recv_sems, k, (cx, cy, c)).wait_recv()
            fwd = _remote(out_ref.at[j, c], out_ref.at[j, c], send_sems, recv_sems, 3 + k, sib)
            fwd.start()
            passed.append(fwd)
        for k, (cx, cy) in enumerate(chips):
            j = 2 * cx + cy
            _remote(out_ref.at[j, 1 - c], out_ref.at[j, 1 - c], send_sems, recv_sems, 3 + k, sib).wait_recv()
        for cp in first + passed:
            cp.wait_send()
        token[...] = jnp.zeros_like(token)

    return pl.pallas_call(
        body,
        name="gather_shards",
        in_specs=[_ANY],
        out_specs=[_ANY, pl.BlockSpec(memory_space=pltpu.VMEM)],
        out_shape=[jax.ShapeDtypeStruct((N_CHIPS,) + pack.shape, pack.dtype), jax.ShapeDtypeStruct((8, 128), F32)],
        scratch_shapes=[pltpu.SemaphoreType.DMA((6,)), pltpu.SemaphoreType.DMA((6,))],
    )(pack)


def _gather_copies(pack_ref, all_ref, send_sems, recv_sems):
    x, y, c = _coords()
    me = 2 * x + y
    return [(_remote(pack_ref.at[c], all_ref.at[me, c], send_sems, recv_sems, k, (cx, cy, c)),
             _remote(pack_ref.at[c], all_ref.at[2 * cx + cy, c], send_sems, recv_sems, k, (cx, cy, c)))
            for k, (cx, cy) in enumerate(_other_chips(x, y))]


_HBM = pl.BlockSpec(memory_space=pltpu.HBM)
_SEM = pl.BlockSpec(memory_space=pltpu.SEMAPHORE)
_SIDE_EFFECT = pltpu.SideEffectType.DATAFLOW_SIDE_EFFECTING


def split_start(name, copies, n, src, land_shape, after=None):
    def body(src_ref, land_ref, *refs):
        send_sems, recv_sems, token = refs[-5], refs[-4], refs[-1]
        for send, _ in copies(src_ref, land_ref, send_sems, recv_sems):
            send.start()
        token[...] = jnp.zeros_like(token)

    extra = () if after is None else (after,)
    *state, token = pl.pallas_call(
        body,
        name=name,
        out_shape=(pltpu.SemaphoreType.DMA((n,)), pltpu.SemaphoreType.DMA((n,)), pltpu.HBM(src.shape, src.dtype),
                   pltpu.HBM(land_shape, src.dtype), jax.ShapeDtypeStruct((8, 128), F32)),
        in_specs=(_HBM, _HBM) + (pl.BlockSpec(memory_space=pl.ANY),) * len(extra),
        out_specs=(_SEM, _SEM, _HBM, _HBM, pl.BlockSpec(memory_space=pltpu.VMEM)),
        input_output_aliases={0: 2, 1: 3},
        compiler_params=pltpu.CompilerParams(has_side_effects=_SIDE_EFFECT),
    )(pltpu.with_memory_space_constraint(src, pltpu.HBM),
      pltpu.with_memory_space_constraint(lax.empty(land_shape, src.dtype), pltpu.HBM), *extra)
    return state, token


def split_wait(name, copies, state, after):
    send_sems, recv_sems, src, land = state

    def body(src_ref, land_ref, send_sems, recv_sems, after_ref, src_out, land_out):
        for send, arrival in copies(src_ref, land_ref, send_sems, recv_sems):
            send.wait_send()
            arrival.wait_recv()

    return pl.pallas_call(
        body,
        name=name,
        out_shape=(pltpu.HBM(src.shape, src.dtype), pltpu.HBM(land.shape, land.dtype)),
        in_specs=(_HBM, _HBM, _SEM, _SEM, pl.BlockSpec(memory_space=pl.ANY)),
        out_specs=(_HBM, _HBM),
        input_output_aliases={0: 0, 1: 1},
        compiler_params=pltpu.CompilerParams(has_side_effects=_SIDE_EFFECT),
    )(src, land, send_sems, recv_sems, after)


def gather_forward(got):
    def body(got_ref, out_ref, send_sems, recv_sems):
        x, y, c = _coords()
        sib = (x, y, 1 - c)
        slots = [2 * cx + cy for cx, cy in _other_chips(x, y)]
        sends = [_remote(got_ref.at[j, c], out_ref.at[j, c], send_sems, recv_sems, k, sib) for k, j in enumerate(slots)]
        for cp in sends:
            cp.start()
        for k, j in enumerate(slots):
            _remote(got_ref.at[j, 1 - c], out_ref.at[j, 1 - c], send_sems, recv_sems, k, sib).wait_recv()
        for cp in sends:
            cp.wait_send()

    return pl.pallas_call(
        body,
        name="gather_forward",
        in_specs=[_ANY],
        out_specs=_ANY,
        out_shape=jax.ShapeDtypeStruct(got.shape, got.dtype),
        input_output_aliases={0: 0},
        scratch_shapes=[pltpu.SemaphoreType.DMA((3,)), pltpu.SemaphoreType.DMA((3,))],
    )(got)


def pair_sum(g, got, tag, *, tm):
    n, _, rows, width = g.shape

    def body(c_ref, own_ref, got_ref, out_ref, out16_ref):
        total = own_ref[0, 0] + got_ref[0]
        out_ref[0] = total
        out16_ref[0] = total.astype(BF16)

    blk = pl.BlockSpec((1, tm, width), lambda j, i, c_ref: (j, i, 0))
    return pl.pallas_call(
        body,
        name="pair_sum_" + tag,
        grid_spec=pltpu.PrefetchScalarGridSpec(
            num_scalar_prefetch=1,
            grid=(n, rows // tm),
            in_specs=[pl.BlockSpec((1, 1, tm, width), lambda j, i, c_ref: (j, c_ref[0], i, 0)), blk],
            out_specs=[blk, blk],
        ),
        out_shape=[jax.ShapeDtypeStruct(got.shape, F32), jax.ShapeDtypeStruct(got.shape, BF16)],
        compiler_params=_cparams(2),
    )(lax.axis_index("c").reshape(1).astype(jnp.int32), g, got)


def _pair_copies(g_ref, got_ref, send_sems, recv_sems):
    x, y, c = _coords()
    copies = [_remote(g_ref.at[j, 1 - c], got_ref.at[j], send_sems, recv_sems, j, (x, y, 1 - c))
              for j in range(N_CHIPS)]
    return [(cp, cp) for cp in copies]


def _chip_copies(p_ref, slots_ref, send_sems, recv_sems):
    x, y, c = _coords()
    me = 2 * x + y
    return [(_remote(p_ref.at[2 * cx + cy], slots_ref.at[me], send_sems, recv_sems, k, (cx, cy, c)),
             _remote(p_ref.at[me], slots_ref.at[2 * cx + cy], send_sems, recv_sems, k, (cx, cy, c)))
            for k, (cx, cy) in enumerate(_other_chips(x, y))]


def sum_with_own(own, slots, mine, after, *, tm, name):
    n, rows, width = slots.shape

    def body(mine_ref, own_ref, *refs):
        acc = None
        for s in range(n):
            term = jnp.where(mine_ref[0] == s, own_ref[0], refs[s][0].astype(F32))
            acc = term if acc is None else acc + term
        refs[-1][...] = acc

    return pl.pallas_call(
        body,
        name=name,
        grid_spec=pltpu.PrefetchScalarGridSpec(
            num_scalar_prefetch=1,
            grid=(rows // tm,),
            in_specs=[pl.BlockSpec((1, tm, width), lambda i, mine_ref: (mine_ref[0], i, 0))]
            + [pl.BlockSpec((1, tm, width), lambda i, mine_ref, s=s: (s, i, 0)) for s in range(n)]
            + [pl.BlockSpec(after.shape, lambda i, mine_ref: (0,) * after.ndim)],
            out_specs=pl.BlockSpec((tm, width), lambda i, mine_ref: (i, 0)),
        ),
        out_shape=jax.ShapeDtypeStruct((rows, width), F32),
        compiler_params=_cparams(1),
    )(mine.reshape(1).astype(jnp.int32), own, *([slots] * n), after)


def exchange_halves(s, tag):
    nq = 4
    rq = s.shape[0] // nq
    assert rq * nq == s.shape[0] and rq % 8 == 0

    def body(s_ref, out_ref, sbuf, rbuf, send_sems, recv_sems, in_sems, out_sems):
        x, y, c = _coords()
        sib = (x, y, 1 - c)
        rows = lambda q: pl.ds(q * rq, rq)
        loads = [pltpu.make_async_copy(s_ref.at[rows(q)], sbuf.at[rows(q)], in_sems.at[q]) for q in range(nq)]
        for cp in loads:
            cp.start()
        sends = []
        for q in range(nq):
            loads[q].wait()
            sends.append(_remote(sbuf.at[rows(q)], rbuf.at[rows(q)], send_sems, recv_sems, q, sib))
            sends[q].start()
        stores = []
        for q in range(nq):
            sends[q].wait_recv()
            stores.append(pltpu.make_async_copy(rbuf.at[rows(q)], out_ref.at[rows(q)], out_sems.at[q]))
            stores[q].start()
        for cp in sends:
            cp.wait_send()
        for cp in stores:
            cp.wait()

    return pl.pallas_call(
        body,
        name="exchange_halves_" + tag,
        in_specs=[_ANY],
        out_specs=_ANY,
        out_shape=jax.ShapeDtypeStruct(s.shape, s.dtype),
        scratch_shapes=[pltpu.VMEM(s.shape, s.dtype), pltpu.VMEM(s.shape, s.dtype)]
        + [pltpu.SemaphoreType.DMA((nq,))] * 4,
        compiler_params=pltpu.CompilerParams(vmem_limit_bytes=VMEM_LIMIT),
    )(s)


def sum_all(s, after):
    rows = s.shape[0]
    half = rows // 2

    def body(s_ref, after_ref, out_ref, theirs, pair, slots, send_sems, recv_sems):
        x, y, c = _coords()
        me = 2 * x + y
        sib = (x, y, 1 - c)
        chips = _other_chips(x, y)
        swap = _remote(s_ref, theirs, send_sems, recv_sems, 0, sib)
        swap.start()
        swap.wait_recv()
        pair[...] = s_ref[...] + theirs[...]
        mine = pl.ds(pl.multiple_of(c * half, 8), half)
        other = pl.ds(pl.multiple_of((1 - c) * half, 8), half)
        sends = [_remote(pair.at[mine], slots.at[me], send_sems, recv_sems, 1 + k, (cx, cy, c))
                 for k, (cx, cy) in enumerate(chips)]
        for cp in sends:
            cp.start()
        for k, (cx, cy) in enumerate(chips):
            _remote(pair.at[mine], slots.at[2 * cx + cy], send_sems, recv_sems, 1 + k, (cx, cy, c)).wait_recv()
        slots[me] = pair[mine]
        out_ref[mine] = ((slots[0] + slots[1]) + slots[2]) + slots[3]
        last = _remote(out_ref.at[mine], out_ref.at[mine], send_sems, recv_sems, 4, sib)
        last.start()
        _remote(out_ref.at[other], out_ref.at[other], send_sems, recv_sems, 4, sib).wait_recv()
        for cp in [swap] + sends + [last]:
            cp.wait_send()

    vmem = pl.BlockSpec(memory_space=pltpu.VMEM)
    return pl.pallas_call(
        body,
        name="sum_all",
        in_specs=[vmem, vmem],
        out_specs=vmem,
        out_shape=jax.ShapeDtypeStruct(s.shape, s.dtype),
        scratch_shapes=[pltpu.VMEM(s.shape, s.dtype), pltpu.VMEM(s.shape, s.dtype),
                        pltpu.VMEM((N_CHIPS, half, s.shape[1]), s.dtype), pltpu.SemaphoreType.DMA((5,)),
                        pltpu.SemaphoreType.DMA((5,))],
        compiler_params=pltpu.CompilerParams(vmem_limit_bytes=VMEM_LIMIT),
    )(s, after)


ADAM_LR = 0.001
ADAM_B1 = 0.9
ADAM_B2 = 0.999
ADAM_EPS = 1e-08
ADAM_WD = 0.01
ADAM_STEP = 10


def f_adamw(g, w, m, v):
    m = ADAM_B1 * m + (1.0 - ADAM_B1) * g
    v = ADAM_B2 * v + (1.0 - ADAM_B2) * jnp.square(g)
    m_hat = m / (1.0 - ADAM_B1 ** ADAM_STEP)
    v_hat = v / (1.0 - ADAM_B2 ** ADAM_STEP)
    delta = -ADAM_LR * (m_hat / (jnp.sqrt(v_hat) + ADAM_EPS) + ADAM_WD * w)
    return delta, m, v


def adamw_many(gs, ws, ms, vs):
    n = len(gs)

    def body(*refs):
        ins, outs = refs[:4 * n], refs[4 * n:]
        for i in range(n):
            delta, nm, nv = f_adamw(ins[i][...], ins[n + i][...], ins[2 * n + i][...], ins[3 * n + i][...])
            outs[i][...] = delta
            outs[n + i][...] = nm
            outs[2 * n + i][...] = nv

    vmem = pl.BlockSpec(memory_space=pltpu.VMEM)
    res = pl.pallas_call(
        body,
        name="adamw_small",
        in_specs=[vmem] * (4 * n),
        out_specs=[vmem] * (3 * n),
        out_shape=[jax.ShapeDtypeStruct(w.shape, F32) for w in ws] * 3,
    )(*gs, *ws, *ms, *vs)
    return res[:n], res[n:2 * n], res[2 * n:]


EARLY = ["w_in", "w_lora_w", "a_lora_w", "g_lora_w"]
LATE = ["w_ffn1", "w_ffn2", "w_proj_b", "w_out", "w_proj_a"]
LORAS = ["w_lora_w", "a_lora_w", "g_lora_w"]
HALF_W = 512
PIECE_ROWS = {"w_in": 1864, "w_ffn1": 1024, "w_ffn2": 1024, "w_proj_a": 256, "w_proj_b": 256, "w_out": 256,
              "w_lora_w": 32, "a_lora_w": 32, "g_lora_w": 80}
PIECE_OFF = {"w_in": 0, "w_lora_w": 1920, "a_lora_w": 1952, "g_lora_w": 2000,
             "w_ffn1": 0, "w_ffn2": 1024, "w_proj_b": 2048, "w_out": 2304, "w_proj_a": 2560}
LO_OFF = 2080


def pack_rows(group):
    return 2304 if group is EARLY else 2816
SHARD_AXIS = {"w_in": 1, "w_proj_a": 0, "w_lora_w": 1, "a_lora_w": 1, "g_lora_w": 1, "w_proj_b": 0, "w_out": 0,
              "w_ffn1": 1, "w_ffn2": 0}
SHARD_SHAPE = {"w_in": (1024, 1864), "w_proj_a": (256, 1024), "w_lora_w": (64, 256), "a_lora_w": (64, 256),
               "g_lora_w": (160, 256), "w_proj_b": (256, 1024), "w_out": (256, 1024), "w_ffn1": (1024, 1024),
               "w_ffn2": (1024, 1024)}
SHIFT_SHARD = (2, 840)
VECTORS = ["g_mix", "sgu_ln_w", "sgu_ln_b", "w0", "a0", "k_k", "k_a", "r_k", "ln_x_w", "ln_x_b", "g_ffn", "g_final"]
SMALL = VECTORS + ["sgu_w", "sgu_b"]
SMALL_SHAPE = {**{n: (1, 1024) for n in VECTORS}, "sgu_w": (8, 128, 128), "sgu_b": (8, 128)}
WEIGHTS = ["g_mix", "w_in", "sgu_ln_w", "sgu_ln_b", "sgu_w", "sgu_b", "w_proj_a", "shift_b", "w_lora_w", "w0",
           "a_lora_w", "a0", "g_lora_w", "k_k", "k_a", "r_k", "ln_x_w", "ln_x_b", "w_proj_b", "w_out", "g_ffn",
           "w_ffn1", "w_ffn2", "g_final"]


def _size(shape):
    n = 1
    for s in shape:
        n *= s
    return n


def _pack_rows(parts, rows, dtype):
    flat = jnp.concatenate([p.reshape(-1).astype(dtype) for p in parts])
    return jnp.concatenate([flat, jnp.zeros((rows * 1024 - flat.shape[0],), dtype)]).reshape(rows, 1024)


def _unpack_rows(packed, shapes):
    flat = packed.reshape(-1)
    out, off = [], 0
    for shp in shapes:
        out.append(flat[off:off + _size(shp)].reshape(shp))
        off += _size(shp)
    return out


def _shard_of(name, full, j):
    ax = SHARD_AXIS[name]
    n = SHARD_SHAPE[name][ax]
    return lax.slice_in_dim(full, j * n, (j + 1) * n, axis=ax)


def _pad_cols(z, n):
    return jnp.concatenate([z, jnp.zeros((z.shape[0], n - z.shape[1]), z.dtype)], axis=1)


def _row_form(name, s):
    return s.T if name == "w_in" else s


def _half_piece(name, rf, h):
    if name in LORAS:
        r = PIECE_ROWS[name]
        return _pad_cols(rf[h * r:(h + 1) * r], HALF_W)
    return rf[:, HALF_W * h:HALF_W * (h + 1)]


def _pack_half(group, rf_fn, h, dtype, tail=()):
    parts, pos, rows = [], 0, pack_rows(group)
    for n in group:
        if PIECE_OFF[n] > pos:
            parts.append(jnp.zeros((PIECE_OFF[n] - pos, HALF_W), dtype))
        parts.append(_half_piece(n, rf_fn(n), h).astype(dtype))
        pos = PIECE_OFF[n] + PIECE_ROWS[n]
    for t in tail:
        parts.append(t)
        pos += t.shape[0]
    parts.append(jnp.zeros((rows - pos, HALF_W), dtype))
    return jnp.concatenate(parts, axis=0)


def _piece(pack, name):
    return pack[PIECE_OFF[name]:PIECE_OFF[name] + PIECE_ROWS[name]]


def _join_halves(name, p0, p1):
    if name in LORAS:
        return jnp.concatenate([p0[:, :SHARD_SHAPE[name][1]], p1[:, :SHARD_SHAPE[name][1]]], axis=0)
    return jnp.concatenate([p0, p1], axis=1)


def _grad_row_form(name, full, j):
    if name == "w_in":
        return full[SHARD_SHAPE[name][1] * j:SHARD_SHAPE[name][1] * (j + 1)]
    return _shard_of(name, full, j)


def adamw_weight(name, g_own, g_other, w, m, v):
    rows, width = w.shape
    if name in LORAS:
        tm = PIECE_ROWS[name]
        grid = (2, 1)
        native = pl.BlockSpec((tm, width), lambda h, i: (h, 0))
    elif name == "w_in":
        tm, lanes = rows, 256
        grid = (2, HALF_W // lanes)
        native = pl.BlockSpec((tm, lanes), lambda h, i: (0, h * (HALF_W // lanes) + i))
    else:
        tm = rows
        grid = (2, 1)
        native = pl.BlockSpec((tm, HALF_W), lambda h, i: (i, h))
    assert PIECE_OFF[name] % tm == 0
    off = PIECE_OFF[name] // tm
    if name == "w_in":
        packed = pl.BlockSpec((tm, lanes), lambda h, i: (0, i))
    else:
        packed = pl.BlockSpec((tm, HALF_W), lambda h, i: (off + i, 0))

    def body(go_ref, gx_ref, w_ref, m_ref, v_ref, g_ref, d_ref, nm_ref, nv_ref):
        g = jnp.where(pl.program_id(0) == lax.axis_index("c"), go_ref[...], gx_ref[...])[:, :w_ref.shape[1]]
        delta, nm, nv = f_adamw(g, w_ref[...], m_ref[...], v_ref[...])
        g_ref[...] = g
        d_ref[...] = delta
        nm_ref[...] = nm
        nv_ref[...] = nv

    return pl.pallas_call(
        body,
        name="adamw_" + name,
        grid=grid,
        in_specs=[packed, packed, native, native, native],
        out_specs=[native] * 4,
        out_shape=[jax.ShapeDtypeStruct(w.shape, F32)] * 4,
        compiler_params=_cparams(2),
    )(g_own, g_other, w, m, v)


def kernel(x, g_mix, w_in, sgu_ln_w, sgu_ln_b, sgu_w, sgu_b, w_proj_a, shift_b, w_lora_w, w0, a_lora_w, a0, g_lora_w, k_k, k_a, r_k, ln_x_w, ln_x_b, w_proj_b, w_out, g_ffn, w_ffn1, w_ffn2, g_final, loss_target, m_g_mix, m_w_in, m_sgu_ln_w, m_sgu_ln_b, m_sgu_w, m_sgu_b, m_w_proj_a, m_shift_b, m_w_lora_w, m_w0, m_a_lora_w, m_a0, m_g_lora_w, m_k_k, m_k_a, m_r_k, m_ln_x_w, m_ln_x_b, m_w_proj_b, m_w_out, m_g_ffn, m_w_ffn1, m_w_ffn2, m_g_final, v_g_mix, v_w_in, v_sgu_ln_w, v_sgu_ln_b, v_sgu_w, v_sgu_b, v_w_proj_a, v_shift_b, v_w_lora_w, v_w0, v_a_lora_w, v_a0, v_g_lora_w, v_k_k, v_k_a, v_r_k, v_ln_x_w, v_ln_x_b, v_w_proj_b, v_w_out, v_g_ffn, v_w_ffn1, v_w_ffn2, v_g_final):
    given = dict(zip(WEIGHTS, (g_mix, w_in, sgu_ln_w, sgu_ln_b, sgu_w, sgu_b, w_proj_a, shift_b, w_lora_w, w0, a_lora_w, a0, g_lora_w, k_k, k_a, r_k, ln_x_w, ln_x_b, w_proj_b, w_out, g_ffn, w_ffn1, w_ffn2, g_final)))
    mom_m = dict(zip(WEIGHTS, (m_g_mix, m_w_in, m_sgu_ln_w, m_sgu_ln_b, m_sgu_w, m_sgu_b, m_w_proj_a, m_shift_b, m_w_lora_w, m_w0, m_a_lora_w, m_a0, m_g_lora_w, m_k_k, m_k_a, m_r_k, m_ln_x_w, m_ln_x_b, m_w_proj_b, m_w_out, m_g_ffn, m_w_ffn1, m_w_ffn2, m_g_final)))
    mom_v = dict(zip(WEIGHTS, (v_g_mix, v_w_in, v_sgu_ln_w, v_sgu_ln_b, v_sgu_w, v_sgu_b, v_w_proj_a, v_shift_b, v_w_lora_w, v_w0, v_a_lora_w, v_a0, v_g_lora_w, v_k_k, v_k_a, v_r_k, v_ln_x_w, v_ln_x_b, v_w_proj_b, v_w_out, v_g_ffn, v_w_ffn1, v_w_ffn2, v_g_final)))
    chip = 2 * lax.axis_index("x") + lax.axis_index("y")

    def local_block(tree, n):
        return tree[n] if n == "g_final" else tree[n][0]

    sb = local_block(given, "shift_b")
    lo_part = lambda z: (z - z.astype(BF16).astype(F32)).astype(BF16)
    row_form = lambda tree: (lambda n: _row_form(n, local_block(tree, n)))
    tile16 = lambda z: jnp.pad(z, ((0, 16 - z.shape[0]), (0, HALF_W - z.shape[1])))
    sb_tiles = [tile16(f(sb[:, lanes])) for f in (lambda z: z.astype(BF16), lo_part)
                for lanes in (slice(0, HALF_W), slice(HALF_W, None))]
    tails = [[_half_piece(n, lo_part(local_block(given, n)), h) for n in LORAS] + sb_tiles for h in range(2)]
    pack_w = jnp.stack([_pack_half(EARLY, row_form(given), h, BF16, tails[h]) for h in range(2)])
    gathered, gathered_token = gather_shards(pack_w)
    picked = lambda j, h, rows, cols: jnp.where(chip == j, pack_w[h, rows, :cols], gathered[j, h, rows, :cols])
    pack_late = jnp.stack([_pack_half(LATE, row_form(given), h, BF16) for h in range(2)])
    late_state, late_token = split_start("gather_start", _gather_copies, 3, pack_late, (N_CHIPS,) + pack_late.shape,
                                         gathered_token)

    def whole(group, got, own):
        half = lambda n, j, h: jnp.where(chip == j, _piece(own[h], n), _piece(got[j, h], n))
        shard = lambda n, j: _join_halves(n, half(n, j, 0), half(n, j, 1))
        return {n: jnp.concatenate([shard(n, j) for j in range(N_CHIPS)],
                                   axis=0 if n == "w_in" else SHARD_AXIS[n]) for n in group}

    w = whole(EARLY, gathered, pack_w)
    def late_weights(after):
        got = gather_forward(split_wait("gather_wait", _gather_copies, late_state, after)[1])
        return {"late_packs": (got, pack_late)}
    off = LO_OFF
    for n in LORAS:
        r, cols = PIECE_ROWS[n], SHARD_SHAPE[n][1]
        lo = jnp.concatenate([jnp.concatenate([picked(j, 0, slice(off, off + r), cols),
                                               picked(j, 1, slice(off, off + r), cols)], axis=0)
                              for j in range(N_CHIPS)], axis=1)
        w[n] = w[n].astype(F32) + lo.astype(F32)
        off += r
    sb_tile = lambda j, t, lanes: picked(j, 0, slice(off + 16 * t, off + 16 * t + 2), lanes).astype(F32)
    rest = SHIFT_SHARD[1] - HALF_W
    w["shift_b"] = jnp.concatenate(
        [jnp.concatenate([sb_tile(j, 0, HALF_W) + sb_tile(j, 2, HALF_W), sb_tile(j, 1, rest) + sb_tile(j, 3, rest)],
                         axis=1) for j in range(N_CHIPS)], axis=1)
    for n in SMALL:
        w[n] = local_block(given, n).reshape(SMALL_SHAPE[n])

    def pair_start(g_pack, tag):
        return split_start("reduce_pair_start_" + tag, _pair_copies, N_CHIPS, g_pack, (N_CHIPS,) + g_pack.shape[2:])

    def pair_finish(state, after, tag):
        g_pack, got = split_wait("reduce_pair_wait_" + tag, _pair_copies, state, after)
        return pair_sum(g_pack, got, tag, tm=got.shape[1] // 2)

    pack_early = lambda g: jnp.stack([jnp.stack([_pack_half(EARLY, lambda n: _grad_row_form(n, g[n], j), h, F32)
                                                 for h in range(2)]) for j in range(N_CHIPS)])
    loss, grad_x, grads, (late_part, late_slots), early_state = local_step(
        x[0], loss_target[0], w, late_token, late_weights, pair_start, pair_finish, pack_early)

    early_part, early_part16 = pair_finish(early_state, grad_x, "early")
    s_pack = _pack_rows([grads[n] for n in SMALL] + [grads["shift_b"], loss.reshape(1, 1)], SMALL_ROWS, F32)
    chips_state, token = split_start("reduce_chips_start", _chip_copies, 3, early_part16, early_part16.shape)
    out_g, out_d, out_m, out_v = {}, {}, {}, {}

    def finish(group, tag, part, slots):
        half_sum = sum_with_own(part, slots, chip, token, tm=part.shape[1] // 2, name="chip_sum_" + tag)
        other_half = exchange_halves(half_sum, tag)
        for n in group:
            res = adamw_weight(n, half_sum, other_half,
                               *[_row_form(n, local_block(t, n)) for t in (given, mom_m, mom_v)])
            for tree, z in zip((out_g, out_d, out_m, out_v), res):
                tree[n] = _row_form(n, z)

    finish(LATE, "late", late_part, late_slots)

    small_shapes = [SMALL_SHAPE[n] for n in SMALL]
    g_small = sum_all(s_pack, token)
    *g_parts, loss = _unpack_rows(g_small, small_shapes + [(2, N_RWKV), ()])
    out_g.update(zip(SMALL, g_parts[:-1]))
    g_sb = lax.dynamic_slice_in_dim(g_parts[-1], chip * SHIFT_SHARD[1], SHIFT_SHARD[1], axis=1)
    out_g["shift_b"] = g_sb
    names = SMALL + ["shift_b"]
    native = lambda tree: [local_block(tree, n).reshape(SMALL_SHAPE.get(n, SHIFT_SHARD)) for n in names]
    small_res = adamw_many(g_parts[:-1] + [g_sb], native(given), native(mom_m), native(mom_v))
    for tree, res in zip((out_d, out_m, out_v), small_res):
        tree.update(zip(names, res))

    after = (out_v["w_out"], out_v["sgu_w"])
    early_slots = split_wait("reduce_chips_wait", _chip_copies, chips_state,
                             jnp.concatenate([z.reshape(-1)[:8] for z in after]))[1]
    finish(EARLY, "early", early_part, early_slots)

    def block_of(tree, n):
        return tree[n].reshape(given[n].shape)

    return (loss, grad_x[None], *[block_of(out_g, n) for n in WEIGHTS], *[block_of(out_d, n) for n in WEIGHTS],
            *[block_of(out_m, n) for n in WEIGHTS], *[block_of(out_v, n) for n in WEIGHTS])
```
